```python
import math
import jax, jax.numpy as jnp
from jax import lax
import numpy as np

D_MODEL = 1024
BATCH = 16
SEQ = 2048
DEPTH = 1

CTX_LEN = 256
GRID_W = 64
DN_HEADS = 8
DN_HEAD_DIM = 128
DN_WIDTH = DN_HEADS * DN_HEAD_DIM
DN_CONV = 4
CHUNK = 64
LRU_WIDTH = 1024
LRU_BLOCKS = 16
LRU_BLOCK_DIM = LRU_WIDTH // LRU_BLOCKS
LRU_CONV = 4
LRU_C = 8.0
N_DIRS = 2
N_BRANCH = 2
D_FF = 4 * D_MODEL
FFN_CONV = 3
EPS = 1e-6
IN_SIZES = (DN_WIDTH, DN_WIDTH, DN_WIDTH, DN_WIDTH, N_DIRS * DN_HEADS, N_DIRS * DN_HEADS,
            LRU_WIDTH, LRU_WIDTH, N_BRANCH * D_MODEL)
D_IN = sum(IN_SIZES)

kernel_name = "hybrid_deltanet_rglru_convffn_dit"


def rmsnorm(u, gain):
    u32 = u.astype(jnp.float32)
    y = u32 * lax.rsqrt(jnp.mean(u32 * u32, axis=-1, keepdims=True) + EPS)
    return (y * gain.astype(jnp.float32)).astype(u.dtype)


def l2norm(u):
    return u * lax.rsqrt(jnp.sum(u * u, axis=-1, keepdims=True) + EPS)


def modulate(u, shift_c, scale_c, shift_x, scale_x, n_ctx):
    return jnp.concatenate([u[:, :n_ctx] * (1 + scale_c) + shift_c,
                            u[:, n_ctx:] * (1 + scale_x[:, None]) + shift_x[:, None]], axis=1)


def gate_rows(v, gate_c, gate_x, n_ctx):
    return jnp.concatenate([v[:, :n_ctx] * gate_c, v[:, n_ctx:] * gate_x[:, None]], axis=1)


def dwconv1d(u, w, pad_left):
    K, C = w.shape
    return lax.conv_general_dilated(u, w[:, None, :].astype(u.dtype), (1,), [(pad_left, K - 1 - pad_left)],
                                    dimension_numbers=('NWC', 'WIO', 'NWC'), feature_group_count=C)


def seg_dwconv1d(u, w, n_ctx, pad_left):
    return jnp.concatenate([dwconv1d(u[:, :n_ctx], w, pad_left), dwconv1d(u[:, n_ctx:], w, pad_left)], axis=1)


def ffn_dwconv(u, w, n_ctx, rows):
    B, T, C = u.shape
    lat = lax.conv_general_dilated(u[:, n_ctx:].reshape(B, rows, GRID_W, C), w[:, :, None, :].astype(u.dtype),
                                   (1, 1), 'SAME', dimension_numbers=('NHWC', 'HWIO', 'NHWC'),
                                   feature_group_count=C).reshape(B, T - n_ctx, C)
    if n_ctx == 0:
        return lat
    ctx_part = dwconv1d(u[:, :n_ctx], w[FFN_CONV // 2], FFN_CONV // 2)
    return jnp.concatenate([ctx_part, lat], axis=1)


def seg_flip(u, n_ctx):
    return jnp.concatenate([jnp.flip(u[:, :n_ctx], axis=1), jnp.flip(u[:, n_ctx:], axis=1)], axis=1)


def to_dirs(u_fwd, u_bwd, n_ctx):
    return jnp.stack([u_fwd, seg_flip(u_bwd, n_ctx)])


def from_dirs(o, n_ctx):
    return o[0] + seg_flip(o[1], n_ctx)


def delta_rule_chunked(q, k, v, g, beta):
    *lead, T, dk = q.shape
    dv = v.shape[-1]
    n = T // CHUNK
    q = q.reshape(*lead, n, CHUNK, dk)
    k = k.reshape(*lead, n, CHUNK, dk)
    v = v.reshape(*lead, n, CHUNK, dv)
    beta = beta.reshape(*lead, n, CHUNK)
    g = jnp.cumsum(g.reshape(*lead, n, CHUNK), axis=-1)
    incl = jnp.tril(jnp.ones((CHUNK, CHUNK), bool))
    strict = jnp.tril(jnp.ones((CHUNK, CHUNK), bool), -1)
    decay = jnp.exp(jnp.where(incl, g[..., :, None] - g[..., None, :], -jnp.inf))
    k_beta = k * beta[..., None]
    lower = jnp.where(strict, jnp.einsum('...ik,...jk->...ij', k_beta, k) * decay, 0.0)
    eye = jnp.eye(CHUNK, dtype=q.dtype)
    t_inv = lax.linalg.triangular_solve(eye + lower, jnp.broadcast_to(eye, lower.shape),
                                        left_side=True, lower=True, unit_diagonal=True)
    w = t_inv @ (k_beta * jnp.exp(g)[..., None])
    u = t_inv @ (v * beta[..., None])
    attn = jnp.where(incl, jnp.einsum('...ik,...jk->...ij', q, k) * decay, 0.0)

    def step(state, xs):
        qc, kc, uc, wc, ac, gc = xs
        v_new = uc - wc @ state
        out = (qc * jnp.exp(gc)[..., None]) @ state + ac @ v_new
        g_last = gc[..., -1:]
        state = state * jnp.exp(g_last)[..., None] + jnp.einsum(
            '...ck,...cv->...kv', kc * jnp.exp(g_last - gc)[..., None], v_new)
        return state, out

    xs = tuple(jnp.moveaxis(t, len(lead), 0) for t in (q, k, u, w, attn, g))
    state0 = jnp.zeros((*lead, dk, dv), q.dtype)
    _, out = lax.scan(step, state0, xs)
    return jnp.moveaxis(out, 0, len(lead)).reshape(*lead, T, dv)


def gated_deltanet(q, k, v, z, a, b, w_conv, a_log, dt_bias, onorm, n_ctx):
    B, T, _ = q.shape
    out_dtype = q.dtype
    f32 = jnp.float32
    qkv = jax.nn.silu(seg_dwconv1d(jnp.concatenate([q, k, v], axis=-1), w_conv, n_ctx, DN_CONV // 2))
    q, k, v = jnp.split(qkv.astype(f32), 3, axis=-1)
    heads = lambda t: t.reshape(B, T, DN_HEADS, DN_HEAD_DIM)
    q = l2norm(heads(q)) * DN_HEAD_DIM ** -0.5
    k = l2norm(heads(k))
    v = heads(v)
    g = -jnp.exp(a_log.astype(f32)) * jax.nn.softplus(a.astype(f32) + dt_bias.astype(f32))
    beta = jax.nn.sigmoid(b.astype(f32))
    tth = lambda t: jnp.moveaxis(t, 2, 3)
    o = delta_rule_chunked(tth(to_dirs(q, q, n_ctx)), tth(to_dirs(k, k, n_ctx)), tth(to_dirs(v, v, n_ctx)),
                           tth(to_dirs(g[:, :, 0], g[:, :, 1], n_ctx)),
                           tth(to_dirs(beta[:, :, 0], beta[:, :, 1], n_ctx)))
    o = from_dirs(jnp.moveaxis(o, 3, 2), n_ctx)
    o = rmsnorm(o, onorm) * jax.nn.silu(heads(z).astype(f32))
    return o.reshape(B, T, DN_WIDTH).astype(out_dtype)


def rg_lru(xb, yb, w_conv, b_conv, w_rg, b_rg, w_ig, b_ig, lam, n_ctx):
    B, T, W = xb.shape
    out_dtype = xb.dtype
    f32 = jnp.float32
    xc = (seg_dwconv1d(xb, w_conv, n_ctx, LRU_CONV // 2) + b_conv).astype(f32)
    xd = to_dirs(xc, xc, n_ctx)
    blocks = xd.reshape(N_DIRS, B, T, LRU_BLOCKS, LRU_BLOCK_DIM)

    def gate(w, bias):
        y = jnp.einsum('dbthi,dhij->dbthj', blocks, w.astype(f32)).reshape(N_DIRS, B, T, W)
        return jax.nn.sigmoid(y + bias.astype(f32)[:, None, None])

    r = gate(w_rg, b_rg)
    i = gate(w_ig, b_ig)
    log_a = -LRU_C * r * jax.nn.softplus(-lam.astype(f32))[:, None, None]
    a = jnp.exp(log_a)
    inp = jnp.sqrt(-jnp.expm1(2.0 * log_a)) * (i * xd)
    _, h = lax.associative_scan(lambda e, l: (e[0] * l[0], l[0] * e[1] + l[1]), (a, inp), axis=2)
    h = from_dirs(h, n_ctx)
    return (h * jax.nn.gelu(yb.astype(f32))).astype(out_dtype)


def _fwd_setup_inputs(seed: int = 0) -> dict:
    key = jax.random.key(seed)
    ks = jax.random.split(key, 32)
    f32 = jnp.float32
    L = DEPTH
    nrm = lambda k, shape, scale: jax.random.normal(k, shape, f32) * scale
    x = nrm(ks[0], (BATCH, SEQ, D_MODEL), 1.0)
    c = nrm(ks[1], (BATCH, D_MODEL), 1.0)
    ctx = nrm(ks[2], (BATCH, CTX_LEN, D_MODEL), 1.0)
    c_ctx = nrm(ks[3], (D_MODEL,), 1.0)
    w_ada = nrm(ks[4], (L, D_MODEL, 6 * D_MODEL), 0.5 * D_MODEL ** -0.5)
    b_ada = nrm(ks[5], (L, 6 * D_MODEL), 0.02)
    g_pre_mix = 1.0 + nrm(ks[6], (L, D_MODEL), 0.05)
    g_post_mix = 1.0 + nrm(ks[7], (L, D_MODEL), 0.05)
    g_pre_ffn = 1.0 + nrm(ks[8], (L, D_MODEL), 0.05)
    g_post_ffn = 1.0 + nrm(ks[9], (L, D_MODEL), 0.05)
    w_in = nrm(ks[10], (L, D_MODEL, D_IN), D_MODEL ** -0.5)
    b_merge = nrm(ks[11], (L, N_BRANCH * D_MODEL), 0.02)
    dn_conv = nrm(ks[12], (L, DN_CONV, 3 * DN_WIDTH), DN_CONV ** -0.5)
    dn_a_log = jnp.log(jax.random.uniform(ks[13], (L, N_DIRS, DN_HEADS), f32, 1.0, 16.0))
    dt = jnp.exp(jax.random.uniform(ks[14], (L, N_DIRS, DN_HEADS), f32, math.log(1e-3), math.log(1e-1)))
    dn_dt_bias = dt + jnp.log(-jnp.expm1(-dt))
    dn_onorm = 1.0 + nrm(ks[15], (L, DN_HEAD_DIM), 0.05)
    lru_conv = nrm(ks[16], (L, LRU_CONV, LRU_WIDTH), LRU_CONV ** -0.5)
    lru_conv_b = nrm(ks[17], (L, LRU_WIDTH), 0.02)
    lru_w_rg = nrm(ks[18], (L, N_DIRS, LRU_BLOCKS, LRU_BLOCK_DIM, LRU_BLOCK_DIM), LRU_BLOCK_DIM ** -0.5)
    lru_b_rg = nrm(ks[19], (L, N_DIRS, LRU_WIDTH), 0.02)
    lru_w_ig = nrm(ks[20], (L, N_DIRS, LRU_BLOCKS, LRU_BLOCK_DIM, LRU_BLOCK_DIM), LRU_BLOCK_DIM ** -0.5)
    lru_b_ig = nrm(ks[21], (L, N_DIRS, LRU_WIDTH), 0.02)
    a0 = jax.random.uniform(ks[22], (L, N_DIRS, LRU_WIDTH), f32, 0.9, 0.999)
    s = a0 ** (1.0 / LRU_C)
    lru_lambda = jnp.log(s) - jnp.log1p(-s)
    w_branch_dn = nrm(ks[23], (L, DN_WIDTH, D_MODEL), DN_WIDTH ** -0.5)
    w_branch_lru = nrm(ks[24], (L, LRU_WIDTH, D_MODEL), LRU_WIDTH ** -0.5)
    w_out = nrm(ks[25], (L, D_MODEL, D_MODEL), D_MODEL ** -0.5)
    w_up = nrm(ks[26], (L, D_MODEL, 2 * D_FF), D_MODEL ** -0.5)
    ffn_dw = nrm(ks[27], (L, FFN_CONV, FFN_CONV, D_FF), 1.0 / FFN_CONV)
    ffn_dw_b = nrm(ks[28], (L, D_FF), 0.02)
    w_down = nrm(ks[29], (L, D_FF, D_MODEL), D_FF ** -0.5)
    return {"x": x, "c": c, "ctx": ctx, "c_ctx": c_ctx, "w_ada": w_ada, "b_ada": b_ada,
            "g_pre_mix": g_pre_mix, "g_post_mix": g_post_mix, "g_pre_ffn": g_pre_ffn, "g_post_ffn": g_post_ffn,
            "w_in": w_in, "b_merge": b_merge, "dn_conv": dn_conv, "dn_a_log": dn_a_log,
            "dn_dt_bias": dn_dt_bias, "dn_onorm": dn_onorm, "lru_conv": lru_conv, "lru_conv_b": lru_conv_b,
            "lru_w_rg": lru_w_rg, "lru_b_rg": lru_b_rg, "lru_w_ig": lru_w_ig, "lru_b_ig": lru_b_ig,
            "lru_lambda": lru_lambda, "w_branch_dn": w_branch_dn, "w_branch_lru": w_branch_lru,
            "w_out": w_out, "w_up": w_up, "ffn_dw": ffn_dw, "ffn_dw_b": ffn_dw_b, "w_down": w_down}


def _fwd_reference(x, c, ctx, c_ctx, w_ada, b_ada, g_pre_mix, g_post_mix, g_pre_ffn, g_post_ffn, w_in, b_merge,
              dn_conv, dn_a_log, dn_dt_bias, dn_onorm, lru_conv, lru_conv_b, lru_w_rg, lru_b_rg, lru_w_ig,
              lru_b_ig, lru_lambda, w_branch_dn, w_branch_lru, w_out, w_up, ffn_dw, ffn_dw_b, w_down):
    n_ctx = ctx.shape[1]
    rows = x.shape[1] // GRID_W
    B = x.shape[0]
    split_at = [int(s) for s in np.cumsum(IN_SIZES)[:-1]]
    h = jnp.concatenate([ctx, x], axis=1)
    silu_c = jax.nn.silu(c)
    silu_cc = jax.nn.silu(c_ctx)
    for l in range(DEPTH):
        last = l == DEPTH - 1
        m_x = jnp.split(silu_c @ w_ada[l] + b_ada[l], 6, axis=-1)
        m_c = jnp.split(silu_cc @ w_ada[l] + b_ada[l], 6, axis=-1)
        u = modulate(rmsnorm(h, g_pre_mix[l]), m_c[0], m_c[1], m_x[0], m_x[1], n_ctx)
        T = u.shape[1]
        q, k, v, z, a, b, xl, yl, mg = jnp.split(u @ w_in[l], split_at, axis=-1)
        y_dn = gated_deltanet(q, k, v, z, a.reshape(B, T, N_DIRS, DN_HEADS), b.reshape(B, T, N_DIRS, DN_HEADS),
                              dn_conv[l], dn_a_log[l], dn_dt_bias[l], dn_onorm[l], n_ctx)
        y_lru = rg_lru(xl, yl, lru_conv[l], lru_conv_b[l], lru_w_rg[l], lru_b_rg[l], lru_w_ig[l], lru_b_ig[l],
                       lru_lambda[l], n_ctx)
        drop = n_ctx if last else 0
        n_cur = n_ctx - drop
        h, y_dn, y_lru, mg = h[:, drop:], y_dn[:, drop:], y_lru[:, drop:], mg[:, drop:]
        g_dn, g_lru = jnp.split(jax.nn.sigmoid(mg + b_merge[l]), N_BRANCH, axis=-1)
        mix = (g_dn * (y_dn @ w_branch_dn[l]) + g_lru * (y_lru @ w_branch_lru[l])) @ w_out[l]
        h = h + gate_rows(rmsnorm(mix, g_post_mix[l]), m_c[2], m_x[2], n_cur)
        u = modulate(rmsnorm(h, g_pre_ffn[l]), m_c[3], m_c[4], m_x[3], m_x[4], n_cur)
        f_gate, f_val = jnp.split(u @ w_up[l], 2, axis=-1)
        f_gate = ffn_dwconv(f_gate, ffn_dw[l], n_cur, rows) + ffn_dw_b[l]
        f = jax.nn.gelu(f_gate) * f_val
        h = h + gate_rows(rmsnorm(f @ w_down[l], g_post_ffn[l]), m_c[5], m_x[5], n_cur)
    return h


import jax as _jax
import jax.numpy as _jnp

TWIN_FORMAT = 'train_step'
FWD_PARAMS = ['x', 'c', 'ctx', 'c_ctx', 'w_ada', 'b_ada', 'g_pre_mix', 'g_post_mix', 'g_pre_ffn', 'g_post_ffn', 'w_in', 'b_merge', 'dn_conv', 'dn_a_log', 'dn_dt_bias', 'dn_onorm', 'lru_conv', 'lru_conv_b', 'lru_w_rg', 'lru_b_rg', 'lru_w_ig', 'lru_b_ig', 'lru_lambda', 'w_branch_dn', 'w_branch_lru', 'w_out', 'w_up', 'ffn_dw', 'ffn_dw_b', 'w_down']
TWIN_WEIGHTS = ['c_ctx', 'w_ada', 'b_ada', 'g_pre_mix', 'g_post_mix', 'g_pre_ffn', 'g_post_ffn', 'w_in', 'b_merge', 'dn_conv', 'dn_a_log', 'dn_dt_bias', 'dn_onorm', 'lru_conv', 'lru_conv_b', 'lru_w_rg', 'lru_b_rg', 'lru_w_ig', 'lru_b_ig', 'lru_lambda', 'w_branch_dn', 'w_branch_lru', 'w_out', 'w_up', 'ffn_dw', 'ffn_dw_b', 'w_down']
TWIN_DIFF_INPUT = 'x'
TWIN_INPUTS = ['x', 'c', 'ctx', 'c_ctx', 'w_ada', 'b_ada', 'g_pre_mix', 'g_post_mix', 'g_pre_ffn', 'g_post_ffn', 'w_in', 'b_merge', 'dn_conv', 'dn_a_log', 'dn_dt_bias', 'dn_onorm', 'lru_conv', 'lru_conv_b', 'lru_w_rg', 'lru_b_rg', 'lru_w_ig', 'lru_b_ig', 'lru_lambda', 'w_branch_dn', 'w_branch_lru', 'w_out', 'w_up', 'ffn_dw', 'ffn_dw_b', 'w_down', 'loss_target', 'm_c_ctx', 'm_w_ada', 'm_b_ada', 'm_g_pre_mix', 'm_g_post_mix', 'm_g_pre_ffn', 'm_g_post_ffn', 'm_w_in', 'm_b_merge', 'm_dn_conv', 'm_dn_a_log', 'm_dn_dt_bias', 'm_dn_onorm', 'm_lru_conv', 'm_lru_conv_b', 'm_lru_w_rg', 'm_lru_b_rg', 'm_lru_w_ig', 'm_lru_b_ig', 'm_lru_lambda', 'm_w_branch_dn', 'm_w_branch_lru', 'm_w_out', 'm_w_up', 'm_ffn_dw', 'm_ffn_dw_b', 'm_w_down', 'v_c_ctx', 'v_w_ada', 'v_b_ada', 'v_g_pre_mix', 'v_g_post_mix', 'v_g_pre_ffn', 'v_g_post_ffn', 'v_w_in', 'v_b_merge', 'v_dn_conv', 'v_dn_a_log', 'v_dn_dt_bias', 'v_dn_onorm', 'v_lru_conv', 'v_lru_conv_b', 'v_lru_w_rg', 'v_lru_b_rg', 'v_lru_w_ig', 'v_lru_b_ig', 'v_lru_lambda', 'v_w_branch_dn', 'v_w_branch_lru', 'v_w_out', 'v_w_up', 'v_ffn_dw', 'v_ffn_dw_b', 'v_w_down']
TWIN_OUTPUTS = ['loss', 'grad_x', 'grad_c_ctx', 'grad_w_ada', 'grad_b_ada', 'grad_g_pre_mix', 'grad_g_post_mix', 'grad_g_pre_ffn', 'grad_g_post_ffn', 'grad_w_in', 'grad_b_merge', 'grad_dn_conv', 'grad_dn_a_log', 'grad_dn_dt_bias', 'grad_dn_onorm', 'grad_lru_conv', 'grad_lru_conv_b', 'grad_lru_w_rg', 'grad_lru_b_rg', 'grad_lru_w_ig', 'grad_lru_b_ig', 'grad_lru_lambda', 'grad_w_branch_dn', 'grad_w_branch_lru', 'grad_w_out', 'grad_w_up', 'grad_ffn_dw', 'grad_ffn_dw_b', 'grad_w_down', 'delta_c_ctx', 'delta_w_ada', 'delta_b_ada', 'delta_g_pre_mix', 'delta_g_post_mix', 'delta_g_pre_ffn', 'delta_g_post_ffn', 'delta_w_in', 'delta_b_merge', 'delta_dn_conv', 'delta_dn_a_log', 'delta_dn_dt_bias', 'delta_dn_onorm', 'delta_lru_conv', 'delta_lru_conv_b', 'delta_lru_w_rg', 'delta_lru_b_rg', 'delta_lru_w_ig', 'delta_lru_b_ig', 'delta_lru_lambda', 'delta_w_branch_dn', 'delta_w_branch_lru', 'delta_w_out', 'delta_w_up', 'delta_ffn_dw', 'delta_ffn_dw_b', 'delta_w_down', 'new_m_c_ctx', 'new_m_w_ada', 'new_m_b_ada', 'new_m_g_pre_mix', 'new_m_g_post_mix', 'new_m_g_pre_ffn', 'new_m_g_post_ffn', 'new_m_w_in', 'new_m_b_merge', 'new_m_dn_conv', 'new_m_dn_a_log', 'new_m_dn_dt_bias', 'new_m_dn_onorm', 'new_m_lru_conv', 'new_m_lru_conv_b', 'new_m_lru_w_rg', 'new_m_lru_b_rg', 'new_m_lru_w_ig', 'new_m_lru_b_ig', 'new_m_lru_lambda', 'new_m_w_branch_dn', 'new_m_w_branch_lru', 'new_m_w_out', 'new_m_w_up', 'new_m_ffn_dw', 'new_m_ffn_dw_b', 'new_m_w_down', 'new_v_c_ctx', 'new_v_w_ada', 'new_v_b_ada', 'new_v_g_pre_mix', 'new_v_g_post_mix', 'new_v_g_pre_ffn', 'new_v_g_post_ffn', 'new_v_w_in', 'new_v_b_merge', 'new_v_dn_conv', 'new_v_dn_a_log', 'new_v_dn_dt_bias', 'new_v_dn_onorm', 'new_v_lru_conv', 'new_v_lru_conv_b', 'new_v_lru_w_rg', 'new_v_lru_b_rg', 'new_v_lru_w_ig', 'new_v_lru_b_ig', 'new_v_lru_lambda', 'new_v_w_branch_dn', 'new_v_w_branch_lru', 'new_v_w_out', 'new_v_w_up', 'new_v_ffn_dw', 'new_v_ffn_dw_b', 'new_v_w_down']
TWIN_LEAF_KINDS = {'loss': 'loss', 'grad_x': 'grad_x', 'grad_c_ctx': 'grad_w', 'grad_w_ada': 'grad_w', 'grad_b_ada': 'grad_w', 'grad_g_pre_mix': 'grad_w', 'grad_g_post_mix': 'grad_w', 'grad_g_pre_ffn': 'grad_w', 'grad_g_post_ffn': 'grad_w', 'grad_w_in': 'grad_w', 'grad_b_merge': 'grad_w', 'grad_dn_conv': 'grad_w', 'grad_dn_a_log': 'grad_w', 'grad_dn_dt_bias': 'grad_w', 'grad_dn_onorm': 'grad_w', 'grad_lru_conv': 'grad_w', 'grad_lru_conv_b': 'grad_w', 'grad_lru_w_rg': 'grad_w', 'grad_lru_b_rg': 'grad_w', 'grad_lru_w_ig': 'grad_w', 'grad_lru_b_ig': 'grad_w', 'grad_lru_lambda': 'grad_w', 'grad_w_branch_dn': 'grad_w', 'grad_w_branch_lru': 'grad_w', 'grad_w_out': 'grad_w', 'grad_w_up': 'grad_w', 'grad_ffn_dw': 'grad_w', 'grad_ffn_dw_b': 'grad_w', 'grad_w_down': 'grad_w', 'delta_c_ctx': 'delta_w', 'delta_w_ada': 'delta_w', 'delta_b_ada': 'delta_w', 'delta_g_pre_mix': 'delta_w', 'delta_g_post_mix': 'delta_w', 'delta_g_pre_ffn': 'delta_w', 'delta_g_post_ffn': 'delta_w', 'delta_w_in': 'delta_w', 'delta_b_merge': 'delta_w', 'delta_dn_conv': 'delta_w', 'delta_dn_a_log': 'delta_w', 'delta_dn_dt_bias': 'delta_w', 'delta_dn_onorm': 'delta_w', 'delta_lru_conv': 'delta_w', 'delta_lru_conv_b': 'delta_w', 'delta_lru_w_rg': 'delta_w', 'delta_lru_b_rg': 'delta_w', 'delta_lru_w_ig': 'delta_w', 'delta_lru_b_ig': 'delta_w', 'delta_lru_lambda': 'delta_w', 'delta_w_branch_dn': 'delta_w', 'delta_w_branch_lru': 'delta_w', 'delta_w_out': 'delta_w', 'delta_w_up': 'delta_w', 'delta_ffn_dw': 'delta_w', 'delta_ffn_dw_b': 'delta_w', 'delta_w_down': 'delta_w', 'new_m_c_ctx': 'new_m', 'new_m_w_ada': 'new_m', 'new_m_b_ada': 'new_m', 'new_m_g_pre_mix': 'new_m', 'new_m_g_post_mix': 'new_m', 'new_m_g_pre_ffn': 'new_m', 'new_m_g_post_ffn': 'new_m', 'new_m_w_in': 'new_m', 'new_m_b_merge': 'new_m', 'new_m_dn_conv': 'new_m', 'new_m_dn_a_log': 'new_m', 'new_m_dn_dt_bias': 'new_m', 'new_m_dn_onorm': 'new_m', 'new_m_lru_conv': 'new_m', 'new_m_lru_conv_b': 'new_m', 'new_m_lru_w_rg': 'new_m', 'new_m_lru_b_rg': 'new_m', 'new_m_lru_w_ig': 'new_m', 'new_m_lru_b_ig': 'new_m', 'new_m_lru_lambda': 'new_m', 'new_m_w_branch_dn': 'new_m', 'new_m_w_branch_lru': 'new_m', 'new_m_w_out': 'new_m', 'new_m_w_up': 'new_m', 'new_m_ffn_dw': 'new_m', 'new_m_ffn_dw_b': 'new_m', 'new_m_w_down': 'new_m', 'new_v_c_ctx': 'new_v', 'new_v_w_ada': 'new_v', 'new_v_b_ada': 'new_v', 'new_v_g_pre_mix': 'new_v', 'new_v_g_post_mix': 'new_v', 'new_v_g_pre_ffn': 'new_v', 'new_v_g_post_ffn': 'new_v', 'new_v_w_in': 'new_v', 'new_v_b_merge': 'new_v', 'new_v_dn_conv': 'new_v', 'new_v_dn_a_log': 'new_v', 'new_v_dn_dt_bias': 'new_v', 'new_v_dn_onorm': 'new_v', 'new_v_lru_conv': 'new_v', 'new_v_lru_conv_b': 'new_v', 'new_v_lru_w_rg': 'new_v', 'new_v_lru_b_rg': 'new_v', 'new_v_lru_w_ig': 'new_v', 'new_v_lru_b_ig': 'new_v', 'new_v_lru_lambda': 'new_v', 'new_v_w_branch_dn': 'new_v', 'new_v_w_branch_lru': 'new_v', 'new_v_w_out': 'new_v', 'new_v_w_up': 'new_v', 'new_v_ffn_dw': 'new_v', 'new_v_ffn_dw_b': 'new_v', 'new_v_w_down': 'new_v'}


def _forward(args):
    return _fwd_reference(*[args[k] for k in FWD_PARAMS])


def _output_shape():
    out = _jax.eval_shape(lambda: _forward(_fwd_setup_inputs(0)))
    return out.shape, out.dtype

N_MICROBATCH = 1
ADAM_LR = 0.001
ADAM_B1 = 0.9
ADAM_B2 = 0.999
ADAM_EPS = 1e-08
ADAM_WD = 0.01
ADAM_STEP = 10
PER_EXAMPLE_BATCH_AXIS = {'x': 0, 'c': 0, 'ctx': 0, 'loss_target': 0}
SHARED_INPUTS = []
_WEIGHT_DTYPES = {'c_ctx': _jnp.float32, 'w_ada': _jnp.float32, 'b_ada': _jnp.float32, 'g_pre_mix': _jnp.float32, 'g_post_mix': _jnp.float32, 'g_pre_ffn': _jnp.float32, 'g_post_ffn': _jnp.float32, 'w_in': _jnp.float32, 'b_merge': _jnp.float32, 'dn_conv': _jnp.float32, 'dn_a_log': _jnp.float32, 'dn_dt_bias': _jnp.float32, 'dn_onorm': _jnp.float32, 'lru_conv': _jnp.float32, 'lru_conv_b': _jnp.float32, 'lru_w_rg': _jnp.float32, 'lru_b_rg': _jnp.float32, 'lru_w_ig': _jnp.float32, 'lru_b_ig': _jnp.float32, 'lru_lambda': _jnp.float32, 'w_branch_dn': _jnp.float32, 'w_branch_lru': _jnp.float32, 'w_out': _jnp.float32, 'w_up': _jnp.float32, 'ffn_dw': _jnp.float32, 'ffn_dw_b': _jnp.float32, 'w_down': _jnp.float32}
MOMENT_SCALE = {'c_ctx': 4.294370e-02, 'w_ada': 1.789025e+00, 'b_ada': 3.274365e+00, 'g_pre_mix': 1.142342e-01, 'g_post_mix': 3.757789e+00, 'g_pre_ffn': 1.341274e-01, 'g_post_ffn': 3.859444e+00, 'w_in': 1.135974e-01, 'b_merge': 9.957739e-02, 'dn_conv': 2.487251e-02, 'dn_a_log': 6.930043e-02, 'dn_dt_bias': 6.668776e-02, 'dn_onorm': 1.234971e-01, 'lru_conv': 3.596159e-01, 'lru_conv_b': 1.064977e+00, 'lru_w_rg': 2.358485e-02, 'lru_b_rg': 2.458455e-02, 'lru_w_ig': 4.562430e-02, 'lru_b_ig': 6.957254e-02, 'lru_lambda': 6.241693e-02, 'w_branch_dn': 5.038189e-02, 'w_branch_lru': 3.502413e-01, 'w_out': 3.546821e-01, 'w_up': 5.409421e-02, 'ffn_dw': 5.022494e-02, 'ffn_dw_b': 7.171216e-02, 'w_down': 1.250447e-01}


def _to_microbatches(a, axis):
    t = _jnp.moveaxis(a, axis, 0)
    t = t.reshape((N_MICROBATCH, t.shape[0] // N_MICROBATCH) + t.shape[1:])
    return _jnp.moveaxis(t, 1, axis + 1)


def setup_inputs(seed: int = 0) -> dict:
    inp = _fwd_setup_inputs(seed)
    key = _jax.random.fold_in(_jax.random.key(seed), 7919)
    shape, _ = _output_shape()
    out = dict(inp)
    out["loss_target"] = _jax.random.normal(_jax.random.fold_in(key, 0), shape, _jnp.float32)
    for i, name in enumerate(TWIN_WEIGHTS):
        w = inp[name].astype(_jnp.float32)
        if MOMENT_SCALE is None:
            s = _jnp.sqrt(_jnp.mean(_jnp.square(w)) + 1e-30)
        else:
            s = MOMENT_SCALE[name]
        km, kv = _jax.random.split(_jax.random.fold_in(key, i + 1))
        out[name] = w
        out["m_" + name] = s * _jax.random.normal(km, w.shape, _jnp.float32)
        out["v_" + name] = (s * s) * _jax.random.uniform(kv, w.shape, _jnp.float32, 0.5, 1.5)
    if N_MICROBATCH > 1:
        for name, axis in PER_EXAMPLE_BATCH_AXIS.items():
            out[name] = _to_microbatches(out[name], axis)
    return {'x': out['x'], 'c': out['c'], 'ctx': out['ctx'], 'c_ctx': out['c_ctx'], 'w_ada': out['w_ada'], 'b_ada': out['b_ada'], 'g_pre_mix': out['g_pre_mix'], 'g_post_mix': out['g_post_mix'], 'g_pre_ffn': out['g_pre_ffn'], 'g_post_ffn': out['g_post_ffn'], 'w_in': out['w_in'], 'b_merge': out['b_merge'], 'dn_conv': out['dn_conv'], 'dn_a_log': out['dn_a_log'], 'dn_dt_bias': out['dn_dt_bias'], 'dn_onorm': out['dn_onorm'], 'lru_conv': out['lru_conv'], 'lru_conv_b': out['lru_conv_b'], 'lru_w_rg': out['lru_w_rg'], 'lru_b_rg': out['lru_b_rg'], 'lru_w_ig': out['lru_w_ig'], 'lru_b_ig': out['lru_b_ig'], 'lru_lambda': out['lru_lambda'], 'w_branch_dn': out['w_branch_dn'], 'w_branch_lru': out['w_branch_lru'], 'w_out': out['w_out'], 'w_up': out['w_up'], 'ffn_dw': out['ffn_dw'], 'ffn_dw_b': out['ffn_dw_b'], 'w_down': out['w_down'], 'loss_target': out['loss_target'], 'm_c_ctx': out['m_c_ctx'], 'm_w_ada': out['m_w_ada'], 'm_b_ada': out['m_b_ada'], 'm_g_pre_mix': out['m_g_pre_mix'], 'm_g_post_mix': out['m_g_post_mix'], 'm_g_pre_ffn': out['m_g_pre_ffn'], 'm_g_post_ffn': out['m_g_post_ffn'], 'm_w_in': out['m_w_in'], 'm_b_merge': out['m_b_merge'], 'm_dn_conv': out['m_dn_conv'], 'm_dn_a_log': out['m_dn_a_log'], 'm_dn_dt_bias': out['m_dn_dt_bias'], 'm_dn_onorm': out['m_dn_onorm'], 'm_lru_conv': out['m_lru_conv'], 'm_lru_conv_b': out['m_lru_conv_b'], 'm_lru_w_rg': out['m_lru_w_rg'], 'm_lru_b_rg': out['m_lru_b_rg'], 'm_lru_w_ig': out['m_lru_w_ig'], 'm_lru_b_ig': out['m_lru_b_ig'], 'm_lru_lambda': out['m_lru_lambda'], 'm_w_branch_dn': out['m_w_branch_dn'], 'm_w_branch_lru': out['m_w_branch_lru'], 'm_w_out': out['m_w_out'], 'm_w_up': out['m_w_up'], 'm_ffn_dw': out['m_ffn_dw'], 'm_ffn_dw_b': out['m_ffn_dw_b'], 'm_w_down': out['m_w_down'], 'v_c_ctx': out['v_c_ctx'], 'v_w_ada': out['v_w_ada'], 'v_b_ada': out['v_b_ada'], 'v_g_pre_mix': out['v_g_pre_mix'], 'v_g_post_mix': out['v_g_post_mix'], 'v_g_pre_ffn': out['v_g_pre_ffn'], 'v_g_post_ffn': out['v_g_post_ffn'], 'v_w_in': out['v_w_in'], 'v_b_merge': out['v_b_merge'], 'v_dn_conv': out['v_dn_conv'], 'v_dn_a_log': out['v_dn_a_log'], 'v_dn_dt_bias': out['v_dn_dt_bias'], 'v_dn_onorm': out['v_dn_onorm'], 'v_lru_conv': out['v_lru_conv'], 'v_lru_conv_b': out['v_lru_conv_b'], 'v_lru_w_rg': out['v_lru_w_rg'], 'v_lru_b_rg': out['v_lru_b_rg'], 'v_lru_w_ig': out['v_lru_w_ig'], 'v_lru_b_ig': out['v_lru_b_ig'], 'v_lru_lambda': out['v_lru_lambda'], 'v_w_branch_dn': out['v_w_branch_dn'], 'v_w_branch_lru': out['v_w_branch_lru'], 'v_w_out': out['v_w_out'], 'v_w_up': out['v_w_up'], 'v_ffn_dw': out['v_ffn_dw'], 'v_ffn_dw_b': out['v_ffn_dw_b'], 'v_w_down': out['v_w_down']}


def _loss(weights, diff, rest, loss_target):
    with _jax.named_scope("forward"):
        args = {**rest, TWIN_DIFF_INPUT: diff, **{k: w.astype(_WEIGHT_DTYPES[k]) for k, w in weights.items()}}
        y = _forward(args)
    with _jax.named_scope("loss_head"):
        err = _jnp.square(y.astype(_jnp.float32) - loss_target)
        return 0.5 * _jnp.sum(_jnp.mean(err, axis=-1)) if err.ndim else 0.5 * err


def _adamw(w, g, m, v):
    m = ADAM_B1 * m + (1.0 - ADAM_B1) * g
    v = ADAM_B2 * v + (1.0 - ADAM_B2) * _jnp.square(g)
    m_hat = m / (1.0 - ADAM_B1 ** ADAM_STEP)
    v_hat = v / (1.0 - ADAM_B2 ** ADAM_STEP)
    delta = -ADAM_LR * (m_hat / (_jnp.sqrt(v_hat) + ADAM_EPS) + ADAM_WD * w)
    return delta, m, v


def reference(x, c, ctx, c_ctx, w_ada, b_ada, g_pre_mix, g_post_mix, g_pre_ffn, g_post_ffn, w_in, b_merge, dn_conv, dn_a_log, dn_dt_bias, dn_onorm, lru_conv, lru_conv_b, lru_w_rg, lru_b_rg, lru_w_ig, lru_b_ig, lru_lambda, w_branch_dn, w_branch_lru, w_out, w_up, ffn_dw, ffn_dw_b, w_down, loss_target, m_c_ctx, m_w_ada, m_b_ada, m_g_pre_mix, m_g_post_mix, m_g_pre_ffn, m_g_post_ffn, m_w_in, m_b_merge, m_dn_conv, m_dn_a_log, m_dn_dt_bias, m_dn_onorm, m_lru_conv, m_lru_conv_b, m_lru_w_rg, m_lru_b_rg, m_lru_w_ig, m_lru_b_ig, m_lru_lambda, m_w_branch_dn, m_w_branch_lru, m_w_out, m_w_up, m_ffn_dw, m_ffn_dw_b, m_w_down, v_c_ctx, v_w_ada, v_b_ada, v_g_pre_mix, v_g_post_mix, v_g_pre_ffn, v_g_post_ffn, v_w_in, v_b_merge, v_dn_conv, v_dn_a_log, v_dn_dt_bias, v_dn_onorm, v_lru_conv, v_lru_conv_b, v_lru_w_rg, v_lru_b_rg, v_lru_w_ig, v_lru_b_ig, v_lru_lambda, v_w_branch_dn, v_w_branch_lru, v_w_out, v_w_up, v_ffn_dw, v_ffn_dw_b, v_w_down):
    given = dict(x=x, c=c, ctx=ctx, c_ctx=c_ctx, w_ada=w_ada, b_ada=b_ada, g_pre_mix=g_pre_mix, g_post_mix=g_post_mix, g_pre_ffn=g_pre_ffn, g_post_ffn=g_post_ffn, w_in=w_in, b_merge=b_merge, dn_conv=dn_conv, dn_a_log=dn_a_log, dn_dt_bias=dn_dt_bias, dn_onorm=dn_onorm, lru_conv=lru_conv, lru_conv_b=lru_conv_b, lru_w_rg=lru_w_rg, lru_b_rg=lru_b_rg, lru_w_ig=lru_w_ig, lru_b_ig=lru_b_ig, lru_lambda=lru_lambda, w_branch_dn=w_branch_dn, w_branch_lru=w_branch_lru, w_out=w_out, w_up=w_up, ffn_dw=ffn_dw, ffn_dw_b=ffn_dw_b, w_down=w_down, loss_target=loss_target, m_c_ctx=m_c_ctx, m_w_ada=m_w_ada, m_b_ada=m_b_ada, m_g_pre_mix=m_g_pre_mix, m_g_post_mix=m_g_post_mix, m_g_pre_ffn=m_g_pre_ffn, m_g_post_ffn=m_g_post_ffn, m_w_in=m_w_in, m_b_merge=m_b_merge, m_dn_conv=m_dn_conv, m_dn_a_log=m_dn_a_log, m_dn_dt_bias=m_dn_dt_bias, m_dn_onorm=m_dn_onorm, m_lru_conv=m_lru_conv, m_lru_conv_b=m_lru_conv_b, m_lru_w_rg=m_lru_w_rg, m_lru_b_rg=m_lru_b_rg, m_lru_w_ig=m_lru_w_ig, m_lru_b_ig=m_lru_b_ig, m_lru_lambda=m_lru_lambda, m_w_branch_dn=m_w_branch_dn, m_w_branch_lru=m_w_branch_lru, m_w_out=m_w_out, m_w_up=m_w_up, m_ffn_dw=m_ffn_dw, m_ffn_dw_b=m_ffn_dw_b, m_w_down=m_w_down, v_c_ctx=v_c_ctx, v_w_ada=v_w_ada, v_b_ada=v_b_ada, v_g_pre_mix=v_g_pre_mix, v_g_post_mix=v_g_post_mix, v_g_pre_ffn=v_g_pre_ffn, v_g_post_ffn=v_g_post_ffn, v_w_in=v_w_in, v_b_merge=v_b_merge, v_dn_conv=v_dn_conv, v_dn_a_log=v_dn_a_log, v_dn_dt_bias=v_dn_dt_bias, v_dn_onorm=v_dn_onorm, v_lru_conv=v_lru_conv, v_lru_conv_b=v_lru_conv_b, v_lru_w_rg=v_lru_w_rg, v_lru_b_rg=v_lru_b_rg, v_lru_w_ig=v_lru_w_ig, v_lru_b_ig=v_lru_b_ig, v_lru_lambda=v_lru_lambda, v_w_branch_dn=v_w_branch_dn, v_w_branch_lru=v_w_branch_lru, v_w_out=v_w_out, v_w_up=v_w_up, v_ffn_dw=v_ffn_dw, v_ffn_dw_b=v_ffn_dw_b, v_w_down=v_w_down)
    weights = {n: given[n] for n in TWIN_WEIGHTS}
    shared = {n: given[n] for n in SHARED_INPUTS}
    per_example = {n: given[n] for n in ['x', 'c', 'ctx']}
    grad_fn = _jax.value_and_grad(_loss, argnums=(0, 1))

    def one_microbatch(ex, loss_target):
        ex = dict(ex)
        diff = ex.pop(TWIN_DIFF_INPUT)
        return grad_fn(weights, diff, {**shared, **ex}, loss_target)

    if N_MICROBATCH == 1:
        loss, (grad_w, grad_x) = one_microbatch(per_example, given["loss_target"])
    else:
        def body(carry, xs):
            loss_sum, grad_sum = carry
            l_k, (gw_k, gx_k) = one_microbatch(xs[0], xs[1])
            with _jax.named_scope("update"):
                return (loss_sum + l_k, _jax.tree.map(_jnp.add, grad_sum, gw_k)), gx_k

        init = (_jnp.zeros((), _jnp.float32), _jax.tree.map(_jnp.zeros_like, weights))
        (loss, grad_w), grad_x = _jax.lax.scan(body, init, (per_example, given["loss_target"]))
    with _jax.named_scope("update"):
        delta_w, new_m, new_v = {}, {}, {}
        for n in TWIN_WEIGHTS:
            delta_w[n], new_m[n], new_v[n] = _adamw(weights[n], grad_w[n], given["m_" + n], given["v_" + n])
    return (loss, grad_x, *[grad_w[n] for n in TWIN_WEIGHTS], *[delta_w[n] for n in TWIN_WEIGHTS],
            *[new_m[n] for n in TWIN_WEIGHTS], *[new_v[n] for n in TWIN_WEIGHTS])
```

```python
import functools
import math

import jax
import jax.numpy as jnp
from jax import lax
from jax.experimental import pallas as pl
from jax.experimental.pallas import tpu as pltpu

F32 = jnp.float32
BF16 = jnp.bfloat16

EPS = 1e-6
GRID_W = 64
GRID_SHIFT = 6
CHUNK = 64
LRU_C = 8.0
N_DEV = 8
ADAM_LR = 0.001
ADAM_B1 = 0.9
ADAM_B2 = 0.999
ADAM_EPS = 1e-08
ADAM_WD = 0.01
ADAM_STEP = 10

LANES = 128
SUBLANES = 8
VMEM_LIMIT = 48 * 1024 * 1024
MESH_AXES = ("x", "y", "c")
GELU_C = math.sqrt(2.0 / math.pi)


def _cparams(sem=None, vmem=None):
    kw = {}
    if sem is not None:
        kw["dimension_semantics"] = sem
    if vmem is not None:
        kw["vmem_limit_bytes"] = vmem
    return pltpu.CompilerParams(**kw)


def _tile(n, pref):
    if n <= pref:
        return n
    t = (pref // LANES) * LANES
    while n % t:
        t -= LANES
    return t


def _sigmoid(x):
    return 1.0 / (1.0 + jnp.exp(-x))


def _silu(x):
    return x * _sigmoid(x)


def _dsilu(x):
    s = _sigmoid(x)
    return s * (1.0 + x * (1.0 - s))


def _softplus(x):
    return jnp.maximum(x, 0.0) + jnp.log(1.0 + jnp.exp(-jnp.abs(x)))


def _gelu(x):
    return 0.5 * x * (1.0 + jnp.tanh(GELU_C * (x + 0.044715 * x * x * x)))


def _dgelu(x):
    t = jnp.tanh(GELU_C * (x + 0.044715 * x * x * x))
    return 0.5 * (1.0 + t) + 0.5 * x * (1.0 - t * t) * GELU_C * (1.0 + 3.0 * 0.044715 * x * x)


def _dot(a, b, dims="nn"):
    dn = {"nn": (((1,), (0,)), ((), ())),
          "nt": (((1,), (1,)), ((), ())),
          "tn": (((0,), (0,)), ((), ()))}[dims]
    return lax.dot_general(a.astype(BF16), b.astype(BF16), dn, preferred_element_type=F32)


def _split2(x):
    hi = x.astype(BF16)
    lo = (x - hi.astype(F32)).astype(BF16)
    return hi, lo


def _split3(x):
    h1 = x.astype(BF16)
    r1 = x - h1.astype(F32)
    h2 = r1.astype(BF16)
    h3 = (r1 - h2.astype(F32)).astype(BF16)
    return h1, h2, h3


def _dot_hi(a, b):
    ah, al = _split2(a)
    bh, bl = _split2(b)
    return _dot(ah, bh) + (_dot(ah, bl) + _dot(al, bh))


def _dot_mask(m, x, dims="nn"):
    h1, h2, h3 = _split3(x)
    if dims == "xnt":
        return _dot(h1, m, "nt") + (_dot(h2, m, "nt") + _dot(h3, m, "nt"))
    return _dot(m, h1, dims) + (_dot(m, h2, dims) + _dot(m, h3, dims))


def _my_index():
    return 4 * lax.axis_index("x") + 2 * lax.axis_index("y") + lax.axis_index("c")


def _all_gather(x, name):
    def body(x_ref, out_ref, send_sems, recv_sems, local_sem):
        x_, y_, c_ = lax.axis_index("x"), lax.axis_index("y"), lax.axis_index("c")
        me, sibling = (x_, y_, c_), (x_, y_, 1 - c_)
        chips = [(1 - x_, y_), (x_, 1 - y_), (1 - x_, 1 - y_)]

        def slab(px, py, pc):
            return out_ref.at[4 * px + 2 * py + pc]

        def copy(k, block, to, src=None):
            return pltpu.make_async_remote_copy(
                src_ref=slab(*block) if src is None else src, dst_ref=slab(*block),
                send_sem=send_sems.at[k], recv_sem=recv_sems.at[k],
                device_id=to, device_id_type=pl.DeviceIdType.MESH)

        mine = pltpu.make_async_copy(x_ref, slab(*me), local_sem)
        mine.start()
        first = [copy(0, me, sibling, src=x_ref)]
        first += [copy(1 + j, me, (*chip, c_), src=x_ref) for j, chip in enumerate(chips)]
        for cp in first:
            cp.start()
        passed = [copy(4 + j, (*chip, c_), sibling) for j, chip in enumerate(chips)]
        for j, chip in enumerate(chips):
            copy(1 + j, (*chip, c_), me).wait_recv()
            passed[j].start()
        copy(0, sibling, me).wait_recv()
        for j, chip in enumerate(chips):
            copy(4 + j, (*chip, 1 - c_), me).wait_recv()
        for cp in first + passed:
            cp.wait_send()
        mine.wait()

    return pl.pallas_call(
        body, name=name,
        out_shape=jax.ShapeDtypeStruct((N_DEV,) + x.shape, x.dtype),
        in_specs=[pl.BlockSpec(memory_space=pl.ANY)],
        out_specs=pl.BlockSpec(memory_space=pl.ANY),
        scratch_shapes=[pltpu.SemaphoreType.DMA((7,)), pltpu.SemaphoreType.DMA((7,)),
                        pltpu.SemaphoreType.DMA],
    )(x)


def _all_to_all(x, name):
    def body(x_ref, out_ref, send_sems, recv_sems, local_sem):
        x_, y_, c_ = lax.axis_index("x"), lax.axis_index("y"), lax.axis_index("c")
        me = 4 * x_ + 2 * y_ + c_
        mine = pltpu.make_async_copy(x_ref.at[me], out_ref.at[me], local_sem)
        mine.start()
        copies = []
        for k in range(1, N_DEV):
            fx, fy, fc = (k >> 2) & 1, (k >> 1) & 1, k & 1
            px = 1 - x_ if fx else x_
            py = 1 - y_ if fy else y_
            pc = 1 - c_ if fc else c_
            peer = 4 * px + 2 * py + pc
            cp = pltpu.make_async_remote_copy(
                src_ref=x_ref.at[peer], dst_ref=out_ref.at[me],
                send_sem=send_sems.at[k - 1], recv_sem=recv_sems.at[k - 1],
                device_id=(px, py, pc), device_id_type=pl.DeviceIdType.MESH)
            cp.start()
            copies.append((cp, peer))
        for cp, peer in copies:
            cp.wait_send()
        for k, (cp, peer) in enumerate(copies):
            pltpu.make_async_remote_copy(
                src_ref=x_ref.at[peer], dst_ref=out_ref.at[peer],
                send_sem=send_sems.at[k], recv_sem=recv_sems.at[k],
                device_id=(x_, y_, c_), device_id_type=pl.DeviceIdType.MESH).wait_recv()
        mine.wait()

    return pl.pallas_call(
        body, name=name,
        out_shape=jax.ShapeDtypeStruct(x.shape, x.dtype),
        in_specs=[pl.BlockSpec(memory_space=pl.ANY)],
        out_specs=pl.BlockSpec(memory_space=pl.ANY),
        scratch_shapes=[pltpu.SemaphoreType.DMA((7,)), pltpu.SemaphoreType.DMA((7,)),
                        pltpu.SemaphoreType.DMA],
    )(x)


def _mm(a, b, dims, name, out_dtype=F32, add=None, tm=512, tn=512, tk=512):
    if dims == "nn":
        (M, K), (K2, N) = a.shape, b.shape
    elif dims == "nt":
        (M, K), (N, K2) = a.shape, b.shape
    else:
        (K, M), (K2, N) = a.shape, b.shape
    assert K == K2, (a.shape, b.shape, dims)
    tm, tn, tk = _tile(M, tm), _tile(N, tn), _tile(K, tk)
    nk = K // tk

    def body(*refs):
        if add is None:
            a_ref, b_ref, o_ref, acc = refs
        else:
            a_ref, b_ref, c_ref, o_ref, acc = refs
        k = pl.program_id(2)

        @pl.when(k == 0)
        def _():
            acc[...] = jnp.zeros_like(acc)

        acc[...] += _dot(a_ref[...], b_ref[...], dims)

        @pl.when(k == nk - 1)
        def _():
            r = acc[...]
            if add is not None:
                r = r + c_ref[...]
            o_ref[...] = r.astype(out_dtype)

    a_spec = (pl.BlockSpec((tk, tm), lambda i, j, k: (k, i)) if dims == "tn"
              else pl.BlockSpec((tm, tk), lambda i, j, k: (i, k)))
    b_spec = (pl.BlockSpec((tn, tk), lambda i, j, k: (j, k)) if dims == "nt"
              else pl.BlockSpec((tk, tn), lambda i, j, k: (k, j)))
    in_specs = [a_spec, b_spec]
    args = [a, b]
    if add is not None:
        in_specs.append(pl.BlockSpec((tm, tn), lambda i, j, k: (i, j)))
        args.append(add)
    return pl.pallas_call(
        body, name=name, grid=(M // tm, N // tn, nk),
        in_specs=in_specs, out_specs=pl.BlockSpec((tm, tn), lambda i, j, k: (i, j)),
        out_shape=jax.ShapeDtypeStruct((M, N), out_dtype),
        scratch_shapes=[pltpu.VMEM((tm, tn), F32)],
        compiler_params=_cparams(("parallel", "parallel", "arbitrary"), VMEM_LIMIT),
    )(*args)


def _ada_fwd(ccp, w, b):
    def body(c_ref, w_ref, b_ref, o_ref):
        o_ref[...] = _dot(_silu(c_ref[...]), w_ref[...]) + b_ref[...]

    return pl.pallas_call(
        body, name="ada_fwd", out_shape=jax.ShapeDtypeStruct((ccp.shape[0], w.shape[1]), F32),
        compiler_params=_cparams(None, VMEM_LIMIT))(ccp, w, b)


def _ada_bwd(ccp, w, g, cctx, n_loc):
    def body(c_ref, w_ref, g_ref, cc_ref, gw_ref, dc_ref):
        gw_ref[...] = _dot(_silu(c_ref[...]), g_ref[...], "tn")
        dcc = _dot(g_ref[...], w_ref[...], "nt")
        row = lax.broadcasted_iota(jnp.int32, dcc.shape, 0)
        part = jnp.sum(jnp.where(row % SUBLANES == n_loc, dcc, 0.0), axis=0, keepdims=True)
        dc_ref[...] = jnp.broadcast_to(part * _dsilu(cc_ref[...]), dc_ref.shape)

    D = ccp.shape[1]
    return pl.pallas_call(
        body, name="ada_bwd",
        out_shape=(jax.ShapeDtypeStruct(w.shape, F32), jax.ShapeDtypeStruct((SUBLANES, D), F32)),
        compiler_params=_cparams(None, VMEM_LIMIT))(ccp, w, g, cctx)


def _norm_mod_fwd(h, gain, mods, i_shift, i_scale, nct, tm, name):
    B, T, D = h.shape

    def body(h_ref, g_ref, m_ref, u_ref):
        x = h_ref[0]
        r = lax.rsqrt(jnp.mean(x * x, axis=-1, keepdims=True) + EPS)
        n = x * r * g_ref[...]
        u_ref[0] = (n * (1.0 + m_ref[0, i_scale:i_scale + 1, :]) + m_ref[0, i_shift:i_shift + 1, :]).astype(BF16)

    return pl.pallas_call(
        body, name=name, grid=(B, T // tm),
        in_specs=[pl.BlockSpec((1, tm, D), lambda b, t: (b, t, 0)),
                  pl.BlockSpec((1, D), lambda b, t: (0, 0)),
                  pl.BlockSpec((1, SUBLANES, D), lambda b, t: (jnp.where(t < nct, B, b), 0, 0))],
        out_specs=pl.BlockSpec((1, tm, D), lambda b, t: (b, t, 0)),
        out_shape=jax.ShapeDtypeStruct((B, T, D), BF16),
        compiler_params=_cparams(("parallel", "parallel"), VMEM_LIMIT),
    )(h, gain, mods)


def _norm_mod_bwd(h, du, gain, mods, i_scale, nct, tm, res, name):
    B, T, D = h.shape
    nt = T // tm

    def body(h_ref, du_ref, g_ref, m_ref, res_ref, dx_ref, ax_ref, ac_ref):
        t = pl.program_id(1)
        x = h_ref[0]
        r = lax.rsqrt(jnp.mean(x * x, axis=-1, keepdims=True) + EPS)
        nh = x * r
        g = g_ref[...]
        du_ = du_ref[0]
        dn = du_ * (1.0 + m_ref[0, i_scale:i_scale + 1, :])
        dnh = dn * g
        dx = r * (dnh - nh * jnp.mean(dnh * nh, axis=-1, keepdims=True))
        sums = (jnp.sum(du_, axis=0, keepdims=True), jnp.sum(du_ * nh * g, axis=0, keepdims=True),
                jnp.sum(dn * nh, axis=0, keepdims=True))

        @pl.when(t == 0)
        def _():
            ax_ref[...] = jnp.zeros_like(ax_ref)
            ac_ref[...] = jnp.zeros_like(ac_ref)

        def accumulate(ref):
            for i, s in enumerate(sums):
                ref[0, i:i + 1, :] += s

        @pl.when(t < nct)
        def _():
            accumulate(ac_ref)

        @pl.when(t >= nct)
        def _():
            accumulate(ax_ref)
            dx_ref[0] = dx + res_ref[0]

    lat = lambda b, t: (b, jnp.maximum(t - nct, 0), 0)
    acc = pl.BlockSpec((1, SUBLANES, D), lambda b, t: (b, 0, 0))
    return pl.pallas_call(
        body, name=name, grid=(B, nt),
        in_specs=[pl.BlockSpec((1, tm, D), lambda b, t: (b, t, 0)),
                  pl.BlockSpec((1, tm, D), lambda b, t: (b, t, 0)),
                  pl.BlockSpec((1, D), lambda b, t: (0, 0)),
                  pl.BlockSpec((1, SUBLANES, D), lambda b, t: (jnp.where(t < nct, B, b), 0, 0)),
                  pl.BlockSpec((1, tm, D), lat)],
        out_specs=(pl.BlockSpec((1, tm, D), lat), acc, acc),
        out_shape=(jax.ShapeDtypeStruct(res.shape, F32),
                   jax.ShapeDtypeStruct((B, SUBLANES, D), F32), jax.ShapeDtypeStruct((B, SUBLANES, D), F32)),
        compiler_params=_cparams(("parallel", "arbitrary"), VMEM_LIMIT),
    )(h, du, gain, mods, res)


def _resid_norm_fwd(x, y, gain, mods, i_gate, tm, name):
    B, N, D = x.shape

    def body(x_ref, y_ref, g_ref, m_ref, o_ref):
        y_ = y_ref[0]
        r = lax.rsqrt(jnp.mean(y_ * y_, axis=-1, keepdims=True) + EPS)
        o_ref[0] = x_ref[0] + y_ * r * g_ref[...] * m_ref[0, i_gate:i_gate + 1, :]

    row = pl.BlockSpec((1, tm, D), lambda b, t: (b, t, 0))
    return pl.pallas_call(
        body, name=name, grid=(B, N // tm),
        in_specs=[row, row, pl.BlockSpec((1, D), lambda b, t: (0, 0)),
                  pl.BlockSpec((1, SUBLANES, D), lambda b, t: (b, 0, 0))],
        out_specs=row, out_shape=jax.ShapeDtypeStruct((B, N, D), F32),
        compiler_params=_cparams(("parallel", "parallel"), VMEM_LIMIT),
    )(x, y, gain, mods)


def _resid_norm_bwd_math(y_, dh, g, gate):
    r = lax.rsqrt(jnp.mean(y_ * y_, axis=-1, keepdims=True) + EPS)
    nh = y_ * r
    dgate = jnp.sum(dh * nh * g, axis=0, keepdims=True)
    dn = dh * gate
    dgain = jnp.sum(dn * nh, axis=0, keepdims=True)
    dnh = dn * g
    dy = r * (dnh - nh * jnp.mean(dnh * nh, axis=-1, keepdims=True))
    return dy, dgate, dgain


def _resid_norm_bwd(y, dh, gain, mods, i_gate, tm, name):
    B, N, D = y.shape

    def body(y_ref, dh_ref, g_ref, m_ref, dy_ref, acc_ref):
        t = pl.program_id(1)
        dy, dgate, dgain = _resid_norm_bwd_math(y_ref[0], dh_ref[0], g_ref[...], m_ref[0, i_gate:i_gate + 1, :])
        dy_ref[0] = dy.astype(BF16)

        @pl.when(t == 0)
        def _():
            acc_ref[...] = jnp.zeros_like(acc_ref)

        acc_ref[0, 0:1, :] += dgate
        acc_ref[0, 1:2, :] += dgain

    row = pl.BlockSpec((1, tm, D), lambda b, t: (b, t, 0))
    return pl.pallas_call(
        body, name=name, grid=(B, N // tm),
        in_specs=[row, row, pl.BlockSpec((1, D), lambda b, t: (0, 0)),
                  pl.BlockSpec((1, SUBLANES, D), lambda b, t: (b, 0, 0))],
        out_specs=(row, pl.BlockSpec((1, SUBLANES, D), lambda b, t: (b, 0, 0))),
        out_shape=(jax.ShapeDtypeStruct((B, N, D), BF16), jax.ShapeDtypeStruct((B, SUBLANES, D), F32)),
        compiler_params=_cparams(("parallel", "arbitrary"), VMEM_LIMIT),
    )(y, dh, gain, mods)


def _final_fwd_bwd(h1, d, gain, mods, i_gate, tgt, tm):
    B, N, D = h1.shape

    def body(h_ref, d_ref, g_ref, m_ref, t_ref, dd_ref, dh_ref, acc_ref):
        t = pl.program_id(1)
        d_ = d_ref[0]
        g = g_ref[...]
        gate = m_ref[0, i_gate:i_gate + 1, :]
        r = lax.rsqrt(jnp.mean(d_ * d_, axis=-1, keepdims=True) + EPS)
        e = h_ref[0] + d_ * r * g * gate - t_ref[0]
        dh = e * (1.0 / D)
        dh_ref[0] = dh
        dy, dgate, dgain = _resid_norm_bwd_math(d_, dh, g, gate)
        dd_ref[0] = dy.astype(BF16)

        @pl.when(t == 0)
        def _():
            acc_ref[...] = jnp.zeros_like(acc_ref)

        acc_ref[0, 0:1, :] += dgate
        acc_ref[0, 1:2, :] += dgain
        acc_ref[0, 2:3, :] += jnp.sum(e * e, axis=0, keepdims=True)

    row = pl.BlockSpec((1, tm, D), lambda b, t: (b, t, 0))
    return pl.pallas_call(
        body, name="final_fwd_bwd", grid=(B, N // tm),
        in_specs=[row, row, pl.BlockSpec((1, D), lambda b, t: (0, 0)),
                  pl.BlockSpec((1, SUBLANES, D), lambda b, t: (b, 0, 0)), row],
        out_specs=(row, row, pl.BlockSpec((1, SUBLANES, D), lambda b, t: (b, 0, 0))),
        out_shape=(jax.ShapeDtypeStruct((B, N, D), BF16), jax.ShapeDtypeStruct((B, N, D), F32),
                   jax.ShapeDtypeStruct((B, SUBLANES, D), F32)),
        compiler_params=_cparams(("parallel", "arbitrary"), VMEM_LIMIT),
    )(h1, d, gain, mods, tgt)


def _shifted(x, prev, nxt, off, row, tm):
    if off == 0:
        return x
    if off < 0:
        y = pltpu.roll(x, -off, 0)
        for r in range(-off):
            y = jnp.where(row == r, prev[SUBLANES + r + off:SUBLANES + r + off + 1, :], y)
        return y
    y = pltpu.roll(x, tm - off, 0)
    for r in range(off):
        y = jnp.where(row == tm - off + r, nxt[r:r + 1, :], y)
    return y


def _halo_specs(T, tm, tc, cb0, order):
    r8, n8 = tm // SUBLANES, T // SUBLANES
    if order == "btj":
        prev = lambda b, t, j: (b, jnp.maximum(t * r8 - 1, 0), cb0 + j)
        nxt = lambda b, t, j: (b, jnp.minimum((t + 1) * r8, n8 - 1), cb0 + j)
    else:
        prev = lambda b, j, t: (b, jnp.maximum(t * r8 - 1, 0), cb0 + j)
        nxt = lambda b, j, t: (b, jnp.minimum((t + 1) * r8, n8 - 1), cb0 + j)
    return pl.BlockSpec((1, SUBLANES, tc), prev), pl.BlockSpec((1, SUBLANES, tc), nxt)


def _seg_halos(p_ref, n_ref, t, nct, nt):
    seg_start = jnp.logical_or(t == 0, t == nct)
    seg_end = jnp.logical_or(t == nct - 1, t == nt - 1)
    prev = jnp.where(seg_start, 0.0, p_ref[0])
    nxt = jnp.where(seg_end, 0.0, n_ref[0])
    return prev, nxt


def _dwconv(x, col0, C, w, bias, offs, nct, tm, tc, name):
    B, T, _ = x.shape
    nt = T // tm
    cb0 = col0 // tc

    def body(*refs):
        if bias is None:
            x_ref, p_ref, n_ref, w_ref, o_ref = refs
        else:
            x_ref, p_ref, n_ref, w_ref, b_ref, o_ref = refs
        t = pl.program_id(1)
        prev, nxt = _seg_halos(p_ref, n_ref, t, nct, nt)
        x_ = x_ref[0]
        row = lax.broadcasted_iota(jnp.int32, x_.shape, 0)
        acc = None
        for j, off in enumerate(offs):
            term = _shifted(x_, prev, nxt, off, row, tm) * w_ref[j:j + 1, :]
            acc = term if acc is None else acc + term
        if bias is not None:
            acc = acc + b_ref[...]
        o_ref[0] = acc

    pspec, nspec = _halo_specs(T, tm, tc, cb0, "btj")
    in_specs = [pl.BlockSpec((1, tm, tc), lambda b, t, j: (b, t, cb0 + j)), pspec, nspec,
                pl.BlockSpec((w.shape[0], tc), lambda b, t, j: (0, j))]
    args = [x, x, x, w]
    if bias is not None:
        in_specs.append(pl.BlockSpec((1, tc), lambda b, t, j: (0, j)))
        args.append(bias)
    return pl.pallas_call(
        body, name=name, grid=(B, nt, C // tc),
        in_specs=in_specs, out_specs=pl.BlockSpec((1, tm, tc), lambda b, t, j: (b, t, j)),
        out_shape=jax.ShapeDtypeStruct((B, T, C), F32),
        compiler_params=_cparams(("parallel", "parallel", "parallel"), VMEM_LIMIT),
    )(*args)


def _dwconv_wgrad(dy, x, col0, C, offs, nct, tm, tc, name):
    B, T, _ = x.shape
    nt = T // tm
    cb0 = col0 // tc
    K = len(offs)

    def body(dy_ref, x_ref, p_ref, n_ref, acc_ref):
        t = pl.program_id(2)
        prev, nxt = _seg_halos(p_ref, n_ref, t, nct, nt)
        x_ = x_ref[0]
        dy_ = dy_ref[0]
        row = lax.broadcasted_iota(jnp.int32, x_.shape, 0)

        @pl.when(t == 0)
        def _():
            acc_ref[...] = jnp.zeros_like(acc_ref)

        for j, off in enumerate(offs):
            acc_ref[0, j:j + 1, :] += jnp.sum(dy_ * _shifted(x_, prev, nxt, off, row, tm), axis=0, keepdims=True)
        acc_ref[0, K:K + 1, :] += jnp.sum(dy_, axis=0, keepdims=True)

    pspec, nspec = _halo_specs(T, tm, tc, cb0, "bjt")
    return pl.pallas_call(
        body, name=name, grid=(B, C // tc, nt),
        in_specs=[pl.BlockSpec((1, tm, tc), lambda b, j, t: (b, t, j)),
                  pl.BlockSpec((1, tm, tc), lambda b, j, t: (b, t, cb0 + j)), pspec, nspec],
        out_specs=pl.BlockSpec((1, SUBLANES, tc), lambda b, j, t: (b, 0, j)),
        out_shape=jax.ShapeDtypeStruct((B, SUBLANES, C), F32),
        compiler_params=_cparams(("parallel", "parallel", "arbitrary"), VMEM_LIMIT),
    )(dy, x, x, x)


def _ab_act(pab, alog, dtb, nd, tm):
    R = pab.shape[0]

    def body(p_ref, al_ref, dt_ref, o_ref):
        p = p_ref[...]
        lane = lax.broadcasted_iota(jnp.int32, p.shape, 1)
        g = -jnp.exp(al_ref[...]) * _softplus(p + dt_ref[...])
        o_ref[...] = jnp.where(lane < nd, g, jnp.where(lane < 2 * nd, _sigmoid(p), 0.0))

    row = pl.BlockSpec((tm, LANES), lambda i: (i, 0))
    vec = pl.BlockSpec((1, LANES), lambda i: (0, 0))
    return pl.pallas_call(
        body, name="ab_act", grid=(R // tm,), in_specs=[row, vec, vec], out_specs=row,
        out_shape=jax.ShapeDtypeStruct((R, LANES), F32),
        compiler_params=_cparams(("parallel",), VMEM_LIMIT))(pab, alog, dtb)


def _ab_act_bwd(pab, gb, dgb, alog, dtb, nd, tm):
    R = pab.shape[0]

    def body(p_ref, gb_ref, d_ref, al_ref, dt_ref, o_ref, acc_ref):
        i = pl.program_id(0)
        p = p_ref[...]
        gb_ = gb_ref[...]
        d_ = d_ref[...]
        lane = lax.broadcasted_iota(jnp.int32, p.shape, 1)
        is_g = lane < nd
        is_b = jnp.logical_and(lane >= nd, lane < 2 * nd)
        da = jnp.where(is_g, d_ * (-jnp.exp(al_ref[...])) * _sigmoid(p + dt_ref[...]), 0.0)
        db = jnp.where(is_b, d_ * gb_ * (1.0 - gb_), 0.0)
        o_ref[...] = (da + db).astype(BF16)

        @pl.when(i == 0)
        def _():
            acc_ref[...] = jnp.zeros_like(acc_ref)

        acc_ref[0:1, :] += jnp.sum(jnp.where(is_g, d_ * gb_, 0.0), axis=0, keepdims=True)
        acc_ref[1:2, :] += jnp.sum(da, axis=0, keepdims=True)

    row = pl.BlockSpec((tm, LANES), lambda i: (i, 0))
    vec = pl.BlockSpec((1, LANES), lambda i: (0, 0))
    return pl.pallas_call(
        body, name="ab_act_bwd", grid=(R // tm,), in_specs=[row, row, row, vec, vec],
        out_specs=(row, pl.BlockSpec((SUBLANES, LANES), lambda i: (0, 0))),
        out_shape=(jax.ShapeDtypeStruct((R, LANES), BF16), jax.ShapeDtypeStruct((SUBLANES, LANES), F32)),
        compiler_params=_cparams(("arbitrary",), VMEM_LIMIT))(pab, gb, dgb, alog, dtb)


def _dn_act(cv, H, hd, tm):
    B, T, C3 = cv.shape
    qscale = float(hd) ** -0.5

    def body(c_ref, o_ref):
        part = pl.program_id(0)
        s = _silu(c_ref[0])
        rs = lax.rsqrt(jnp.sum(s * s, axis=-1, keepdims=True) + EPS)
        scale = jnp.where(part == 0, rs * qscale, jnp.where(part == 1, rs, 1.0))
        o_ref[0] = s * scale

    spec = pl.BlockSpec((1, tm, hd), lambda p, b, t, h: (b, t, p * H + h))
    return pl.pallas_call(
        body, name="dn_act", grid=(3, B, T // tm, H), in_specs=[spec], out_specs=spec,
        out_shape=jax.ShapeDtypeStruct((B, T, C3), F32),
        compiler_params=_cparams(("parallel",) * 4, VMEM_LIMIT))(cv)


def _dn_act_bwd(cv, dqkv, H, hd, tm):
    B, T, C3 = cv.shape
    qscale = float(hd) ** -0.5

    def body(c_ref, d0_ref, d1_ref, o_ref):
        part = pl.program_id(0)
        c = c_ref[0]
        s = _silu(c)
        dout = d0_ref[0, 0] + d1_ref[0, 0]
        rs = lax.rsqrt(jnp.sum(s * s, axis=-1, keepdims=True) + EPS)
        sh = s * rs
        dnorm = rs * (dout - sh * jnp.sum(dout * sh, axis=-1, keepdims=True))
        ds = jnp.where(part == 0, dnorm * qscale, jnp.where(part == 1, dnorm, dout))
        o_ref[0] = ds * _dsilu(c)

    spec = pl.BlockSpec((1, tm, hd), lambda p, b, t, h: (b, t, p * H + h))
    d0 = pl.BlockSpec((1, 1, tm, hd), lambda p, b, t, h: (0, b, t, p * H + h))
    d1 = pl.BlockSpec((1, 1, tm, hd), lambda p, b, t, h: (1, b, t, p * H + h))
    return pl.pallas_call(
        body, name="dn_act_bwd", grid=(3, B, T // tm, H), in_specs=[spec, d0, d1], out_specs=spec,
        out_shape=jax.ShapeDtypeStruct((B, T, C3), F32),
        compiler_params=_cparams(("parallel",) * 4, VMEM_LIMIT))(cv, dqkv, dqkv)


def _chunk_of(d, s, ncc, nch):
    return jnp.where(d == 0, s, jnp.where(s < ncc, ncc - 1 - s, nch - 1 - (s - ncc)))


def _delta_masks(d):
    row = lax.broadcasted_iota(jnp.int32, (CHUNK, CHUNK), 0)
    col = lax.broadcasted_iota(jnp.int32, (CHUNK, CHUNK), 1)
    sign = 1 - 2 * d
    diff = (row - col) * sign
    return diff >= 0, diff > 0


def _unit_tri_inverse(a_strict):
    row = lax.broadcasted_iota(jnp.int32, a_strict.shape, 0)
    col = lax.broadcasted_iota(jnp.int32, a_strict.shape, 1)
    eye = jnp.where(row == col, 1.0, 0.0).astype(F32)
    p = -a_strict
    tm_ = eye + p
    n = 2
    while n < a_strict.shape[0]:
        p = _dot_hi(p, p)
        tm_ = tm_ + _dot_hi(tm_, p)
        n *= 2
    return tm_


def _delta_chunk_fwd(q, k, v, g_i, g_j, beta_i, gl, S, incl, strict):
    D_ = jnp.where(incl, jnp.exp(jnp.where(incl, g_i - g_j, 0.0)), 0.0)
    kb = k * beta_i
    A = jnp.where(strict, _dot(kb, k, "nt") * D_, 0.0)
    Tm = _unit_tri_inverse(A)
    gam = jnp.exp(g_i)
    w = _dot(Tm, kb * gam)
    u = _dot(Tm, v * beta_i)
    vn = u - _dot(w, S)
    P = jnp.where(incl, _dot(q, k, "nt") * D_, 0.0)
    qg = q * gam
    o = _dot(qg, S) + _dot(P, vn)
    e_i = jnp.exp(gl - g_i)
    kd = k * e_i
    Gam = jnp.exp(gl)
    S_new = S * Gam + _dot(kd, vn, "tn")
    return dict(D=D_, kb=kb, A=A, Tm=Tm, gam=gam, w=w, u=u, vn=vn, P=P, qg=qg, o=o, e=e_i, kd=kd, Gam=Gam, S_new=S_new)


def _delta_gb(gbc_ref, gbr_ref, incl, H):
    gbc = gbc_ref[0, 0, 0]
    gbr = gbr_ref[0, 0, 0]
    inclf = incl.astype(BF16)
    gc_col = _dot_mask(inclf, gbc)
    gc_row = _dot_mask(inclf, gbr, "xnt")
    gl = jnp.sum(gbc, axis=0, keepdims=True)
    return gbc, gc_col, gc_row, gl


def _delta_fwd(qkvn, gbc, gbr, H, hd, ncc):
    B, T, _ = qkvn.shape
    nch = T // CHUNK
    HD = H * hd

    def body(q_ref, k_ref, v_ref, gbc_ref, gbr_ref, o_ref, sin_ref, S_ref):
        d = pl.program_id(0)
        s = pl.program_id(2)

        @pl.when(s == 0)
        def _():
            S_ref[...] = jnp.zeros_like(S_ref)

        incl, strict = _delta_masks(d)
        gbc_, gc_col, gc_row, gl = _delta_gb(gbc_ref, gbr_ref, incl, H)
        for h in range(H):
            sl = slice(h * hd, (h + 1) * hd)
            S = S_ref[h]
            r = _delta_chunk_fwd(q_ref[0, :, sl], k_ref[0, :, sl], v_ref[0, :, sl],
                                 gc_col[:, h:h + 1], gc_row[h:h + 1, :], gbc_[:, H + h:H + h + 1],
                                 gl[:, h:h + 1], S, incl, strict)
            sin_ref[0, 0, 0, h] = S
            S_ref[h] = r["S_new"]
            o_ref[0, 0, :, sl] = r["o"]

    cidx = lambda d, b, s: _chunk_of(d, s, ncc, nch)
    qspec = lambda part: pl.BlockSpec((1, CHUNK, HD), lambda d, b, s: (b, cidx(d, b, s), part))
    return pl.pallas_call(
        body, name="delta_fwd", grid=(2, B, nch),
        in_specs=[qspec(0), qspec(1), qspec(2),
                  pl.BlockSpec((1, 1, 1, CHUNK, 2 * H), lambda d, b, s: (b, d, cidx(d, b, s), 0, 0)),
                  pl.BlockSpec((1, 1, 1, 2 * H, CHUNK), lambda d, b, s: (b, d, cidx(d, b, s), 0, 0))],
        out_specs=(pl.BlockSpec((1, 1, CHUNK, HD), lambda d, b, s: (d, b, cidx(d, b, s), 0)),
                   pl.BlockSpec((1, 1, 1, H, hd, hd), lambda d, b, s: (d, b, cidx(d, b, s), 0, 0, 0))),
        out_shape=(jax.ShapeDtypeStruct((2, B, T, HD), F32),
                   jax.ShapeDtypeStruct((2, B, nch, H, hd, hd), F32)),
        scratch_shapes=[pltpu.VMEM((H, hd, hd), F32)],
        compiler_params=_cparams(("parallel", "parallel", "arbitrary"), VMEM_LIMIT),
    )(qkvn, qkvn, qkvn, gbc, gbr)


def _delta_bwd(qkvn, gbc, gbr, sin, do, H, hd, ncc):
    B, T, _ = qkvn.shape
    nch = T // CHUNK
    HD = H * hd

    def body(q_ref, k_ref, v_ref, gbc_ref, gbr_ref, sin_ref, do_ref, dqkv_ref, dgb_ref, dS_ref):
        d = pl.program_id(0)
        s = pl.program_id(2)

        @pl.when(s == 0)
        def _():
            dS_ref[...] = jnp.zeros_like(dS_ref)

        incl, strict = _delta_masks(d)
        gbc_, gc_col, gc_row, gl = _delta_gb(gbc_ref, gbr_ref, incl, H)
        lane = lax.broadcasted_iota(jnp.int32, (1, 2 * H), 1)
        ones = jnp.ones((CHUNK, LANES), BF16)
        dgc = jnp.zeros((CHUNK, 2 * H), F32)
        dgl = jnp.zeros((1, 2 * H), F32)
        for h in range(H):
            sl = slice(h * hd, (h + 1) * hd)
            q, k, v = q_ref[0, :, sl], k_ref[0, :, sl], v_ref[0, :, sl]
            g_i, beta_i, gl_h = gc_col[:, h:h + 1], gbc_[:, H + h:H + h + 1], gl[:, h:h + 1]
            S = sin_ref[0, 0, 0, h]
            f = _delta_chunk_fwd(q, k, v, g_i, gc_row[h:h + 1, :], beta_i, gl_h, S, incl, strict)
            do_ = do_ref[0, :, sl]
            dSn = dS_ref[h]
            dvn = _dot(f["P"], do_, "tn") + _dot(f["kd"], dSn)
            dP = jnp.where(incl, _dot(do_, f["vn"], "nt"), 0.0)
            dqg = _dot(do_, S, "nt")
            dS_in = _dot(f["qg"], do_, "tn") + f["Gam"] * dSn - _dot(f["w"], dvn, "tn")
            dGam = jnp.sum(jnp.sum(S * dSn, axis=1, keepdims=True), axis=0, keepdims=True)
            dkd = _dot(f["vn"], dSn, "nt")
            de = jnp.sum(dkd * k, axis=1, keepdims=True) * f["e"]
            dw = -_dot(dvn, S, "nt")
            dXw = _dot(f["Tm"], dw, "tn")
            dXu = _dot(f["Tm"], dvn, "tn")
            dA = -jnp.where(strict, _dot(dXw, f["w"], "nt") + _dot(dXu, f["u"], "nt"), 0.0)
            dM = dA * f["D"]
            dN = dP * f["D"]
            dkb = _dot(dM, k) + dXw * f["gam"]
            dq = dqg * f["gam"] + _dot(dN, k)
            dk = dkd * f["e"] + _dot(dM, f["kb"], "tn") + _dot(dN, q, "tn") + dkb * beta_i
            dv = dXu * beta_i
            dbeta = jnp.sum(dkb * k, axis=1, keepdims=True) + jnp.sum(dXu * v, axis=1, keepdims=True)
            E = dA * f["A"] + dP * f["P"]
            e1, e2, e3 = _split3(E)
            colsum = (_dot(e1, ones, "tn") + (_dot(e2, ones, "tn") + _dot(e3, ones, "tn")))[:, 0:1]
            dg = (jnp.sum(E, axis=1, keepdims=True) - colsum
                  + jnp.sum(dqg * f["qg"], axis=1, keepdims=True)
                  + jnp.sum(dXw * f["kb"], axis=1, keepdims=True) * f["gam"] - de)
            dgl_h = jnp.sum(de, axis=0, keepdims=True) + dGam * f["Gam"]
            dgc = dgc + dg * (lane == h).astype(F32) + dbeta * (lane == H + h).astype(F32)
            dgl = dgl + dgl_h * (lane == h).astype(F32)
            dS_ref[h] = dS_in
            dqkv_ref[0, 0, :, h * hd:(h + 1) * hd] = dq
            dqkv_ref[0, 0, :, HD + h * hd:HD + (h + 1) * hd] = dk
            dqkv_ref[0, 0, :, 2 * HD + h * hd:2 * HD + (h + 1) * hd] = dv
        draw = _dot_mask(incl.astype(BF16), dgc, "tn") + dgl
        dgb_ref[0, 0, 0] = jnp.where(lane < H, draw, dgc)

    cidx = lambda d, b, s: _chunk_of(d, nch - 1 - s, ncc, nch)
    qspec = lambda part: pl.BlockSpec((1, CHUNK, HD), lambda d, b, s: (b, cidx(d, b, s), part))
    return pl.pallas_call(
        body, name="delta_bwd", grid=(2, B, nch),
        in_specs=[qspec(0), qspec(1), qspec(2),
                  pl.BlockSpec((1, 1, 1, CHUNK, 2 * H), lambda d, b, s: (b, d, cidx(d, b, s), 0, 0)),
                  pl.BlockSpec((1, 1, 1, 2 * H, CHUNK), lambda d, b, s: (b, d, cidx(d, b, s), 0, 0)),
                  pl.BlockSpec((1, 1, 1, H, hd, hd), lambda d, b, s: (d, b, cidx(d, b, s), 0, 0, 0)),
                  pl.BlockSpec((1, CHUNK, HD), lambda d, b, s: (b, cidx(d, b, s), 0))],
        out_specs=(pl.BlockSpec((1, 1, CHUNK, 3 * HD), lambda d, b, s: (d, b, cidx(d, b, s), 0)),
                   pl.BlockSpec((1, 1, 1, CHUNK, 2 * H), lambda d, b, s: (b, d, cidx(d, b, s), 0, 0))),
        out_shape=(jax.ShapeDtypeStruct((2, B, T, 3 * HD), F32),
                   jax.ShapeDtypeStruct((B, 2, nch, CHUNK, 2 * H), F32)),
        scratch_shapes=[pltpu.VMEM((H, hd, hd), F32)],
        compiler_params=_cparams(("parallel", "parallel", "arbitrary"), VMEM_LIMIT),
    )(qkvn, qkvn, qkvn, gbc, gbr, sin, do)


def _dn_out(o2, pz, zcb0, onorm, H, hd, nct, tm):
    _, B, T, HD = o2.shape
    N = T - nct * tm

    def body(o0_ref, o1_ref, z_ref, g_ref, y_ref):
        o = o0_ref[0, 0] + o1_ref[0, 0]
        r = lax.rsqrt(jnp.mean(o * o, axis=-1, keepdims=True) + EPS)
        y_ref[0] = (o * r * g_ref[...] * _silu(z_ref[0])).astype(BF16)

    return pl.pallas_call(
        body, name="dn_out", grid=(B, N // tm, H),
        in_specs=[pl.BlockSpec((1, 1, tm, hd), lambda b, t, h: (0, b, t + nct, h)),
                  pl.BlockSpec((1, 1, tm, hd), lambda b, t, h: (1, b, t + nct, h)),
                  pl.BlockSpec((1, tm, hd), lambda b, t, h: (b, t + nct, zcb0 + h)),
                  pl.BlockSpec((1, hd), lambda b, t, h: (0, 0))],
        out_specs=pl.BlockSpec((1, tm, hd), lambda b, t, h: (b, t, h)),
        out_shape=jax.ShapeDtypeStruct((B, N, HD), BF16),
        compiler_params=_cparams(("parallel",) * 3, VMEM_LIMIT))(o2, o2, pz, onorm)


def _dn_out_bwd(o2, pz, zcb0, onorm, dy, H, hd, nct, tm):
    _, B, T, HD = o2.shape
    nt = T // tm

    def body(o0_ref, o1_ref, z_ref, g_ref, dy_ref, do_ref, dz_ref, acc_ref):
        t = pl.program_id(1)
        h = pl.program_id(2)

        @pl.when(jnp.logical_and(t == 0, h == 0))
        def _():
            acc_ref[...] = jnp.zeros_like(acc_ref)

        @pl.when(t < nct)
        def _():
            do_ref[0] = jnp.zeros_like(do_ref[0])
            dz_ref[0] = jnp.zeros_like(dz_ref[0])

        @pl.when(t >= nct)
        def _():
            o = o0_ref[0, 0] + o1_ref[0, 0]
            z = z_ref[0]
            g = g_ref[...]
            dy_ = dy_ref[0]
            r = lax.rsqrt(jnp.mean(o * o, axis=-1, keepdims=True) + EPS)
            oh = o * r
            dz_ref[0] = (dy_ * oh * g * _dsilu(z)).astype(BF16)
            dn = dy_ * _silu(z)
            acc_ref[0, 0:1, :] += jnp.sum(dn * oh, axis=0, keepdims=True)
            doh = dn * g
            do_ref[0] = r * (doh - oh * jnp.mean(doh * oh, axis=-1, keepdims=True))

    lat = lambda b, t, h: (b, jnp.maximum(t - nct, 0), h)
    return pl.pallas_call(
        body, name="dn_out_bwd", grid=(B, nt, H),
        in_specs=[pl.BlockSpec((1, 1, tm, hd), lambda b, t, h: (0, b, t, h)),
                  pl.BlockSpec((1, 1, tm, hd), lambda b, t, h: (1, b, t, h)),
                  pl.BlockSpec((1, tm, hd), lambda b, t, h: (b, t, zcb0 + h)),
                  pl.BlockSpec((1, hd), lambda b, t, h: (0, 0)),
                  pl.BlockSpec((1, tm, hd), lat)],
        out_specs=(pl.BlockSpec((1, tm, hd), lambda b, t, h: (b, t, h)),
                   pl.BlockSpec((1, tm, hd), lambda b, t, h: (b, t, h)),
                   pl.BlockSpec((1, SUBLANES, hd), lambda b, t, h: (b, 0, 0))),
        out_shape=(jax.ShapeDtypeStruct((B, T, HD), F32), jax.ShapeDtypeStruct((B, T, HD), BF16),
                   jax.ShapeDtypeStruct((B, SUBLANES, hd), F32)),
        compiler_params=_cparams(("parallel", "arbitrary", "arbitrary"), VMEM_LIMIT),
    )(o2, o2, pz, onorm, dy)


def _lru_gate_math(x, wr, wi, br, bi, lam):
    r = _sigmoid(_dot(x, wr) + br)
    i = _sigmoid(_dot(x, wi) + bi)
    sp = _softplus(-lam)
    la = -LRU_C * r * sp
    a = jnp.exp(la)
    s = jnp.sqrt(-jnp.tanh(la) * (a * a + 1.0))
    return r, i, sp, a, s


def _lru_gates(xc, wbd, bias4, lam, tm, bw):
    B, T, W = xc.shape

    def body(x_ref, w_ref, b_ref, l_ref, a_ref, i_ref):
        x = x_ref[0]
        for d in range(2):
            _, i, _, a, s = _lru_gate_math(x, w_ref[2 * d, 0], w_ref[2 * d + 1, 0],
                                           b_ref[2 * d:2 * d + 1, :], b_ref[2 * d + 1:2 * d + 2, :], l_ref[d:d + 1, :])
            a_ref[d, 0] = a
            i_ref[d, 0] = s * (i * x)

    out = pl.BlockSpec((2, 1, tm, bw), lambda b, t, j: (0, b, t, j))
    return pl.pallas_call(
        body, name="lru_gates", grid=(B, T // tm, W // bw),
        in_specs=[pl.BlockSpec((1, tm, bw), lambda b, t, j: (b, t, j)),
                  pl.BlockSpec((4, 1, bw, bw), lambda b, t, j: (0, j, 0, 0)),
                  pl.BlockSpec((4, bw), lambda b, t, j: (0, j)),
                  pl.BlockSpec((2, bw), lambda b, t, j: (0, j))],
        out_specs=(out, out),
        out_shape=(jax.ShapeDtypeStruct((2, B, T, W), F32), jax.ShapeDtypeStruct((2, B, T, W), F32)),
        compiler_params=_cparams(("parallel",) * 3, VMEM_LIMIT))(xc, wbd, bias4, lam)


def _lru_gates_bwd(xc, wbd, bias4, lam, dinp, da, tm, bw):
    B, T, W = xc.shape
    nt = T // tm

    def body(x_ref, w_ref, b_ref, l_ref, di_ref, da_ref, dx_ref, dw_ref, acc_ref):
        b_ = pl.program_id(1)
        t = pl.program_id(2)

        @pl.when(jnp.logical_and(b_ == 0, t == 0))
        def _():
            dw_ref[...] = jnp.zeros_like(dw_ref)
            acc_ref[...] = jnp.zeros_like(acc_ref)

        x = x_ref[0]
        dx = jnp.zeros_like(x)
        for d in range(2):
            wr, wi = w_ref[2 * d, 0], w_ref[2 * d + 1, 0]
            lam_ = l_ref[d:d + 1, :]
            r, i, sp, a, s = _lru_gate_math(x, wr, wi, b_ref[2 * d:2 * d + 1, :], b_ref[2 * d + 1:2 * d + 2, :], lam_)
            dinp_ = di_ref[d, 0]
            dix = dinp_ * s
            ds = dinp_ * i * x
            dla = da_ref[d, 0] * a - ds * (a * a) / s
            dpr = dla * (-LRU_C * sp) * r * (1.0 - r)
            dpi = dix * x * i * (1.0 - i)
            dx = dx + dix * i + _dot(dpr, wr, "nt") + _dot(dpi, wi, "nt")
            dw_ref[2 * d, 0] += _dot(x, dpr, "tn")
            dw_ref[2 * d + 1, 0] += _dot(x, dpi, "tn")
            acc_ref[2 * d:2 * d + 1, :] += jnp.sum(dpr, axis=0, keepdims=True)
            acc_ref[2 * d + 1:2 * d + 2, :] += jnp.sum(dpi, axis=0, keepdims=True)
            acc_ref[4 + d:5 + d, :] += jnp.sum(dla * (-LRU_C * r), axis=0, keepdims=True) * (-_sigmoid(-lam_))
        dx_ref[0] = dx

    dirs = pl.BlockSpec((2, 1, tm, bw), lambda j, b, t: (0, b, t, j))
    return pl.pallas_call(
        body, name="lru_gates_bwd", grid=(W // bw, B, nt),
        in_specs=[pl.BlockSpec((1, tm, bw), lambda j, b, t: (b, t, j)),
                  pl.BlockSpec((4, 1, bw, bw), lambda j, b, t: (0, j, 0, 0)),
                  pl.BlockSpec((4, bw), lambda j, b, t: (0, j)),
                  pl.BlockSpec((2, bw), lambda j, b, t: (0, j)), dirs, dirs],
        out_specs=(pl.BlockSpec((1, tm, bw), lambda j, b, t: (b, t, j)),
                   pl.BlockSpec((4, 1, bw, bw), lambda j, b, t: (0, j, 0, 0)),
                   pl.BlockSpec((SUBLANES, bw), lambda j, b, t: (0, j))),
        out_shape=(jax.ShapeDtypeStruct((B, T, W), F32), jax.ShapeDtypeStruct(wbd.shape, F32),
                   jax.ShapeDtypeStruct((SUBLANES, W), F32)),
        compiler_params=_cparams(("parallel", "arbitrary", "arbitrary"), VMEM_LIMIT),
    )(xc, wbd, bias4, lam, dinp, da)


def _tile_scan(a, x, rev, tm):
    row = lax.broadcasted_iota(jnp.int32, a.shape, 0)
    s = 1
    while s < tm:
        if rev:
            a_sh, x_sh, ok = pltpu.roll(a, tm - s, 0), pltpu.roll(x, tm - s, 0), row < tm - s
        else:
            a_sh, x_sh, ok = pltpu.roll(a, s, 0), pltpu.roll(x, s, 0), row >= s
        x = jnp.where(ok, x + a * x_sh, x)
        a = jnp.where(ok, a * a_sh, a)
        s *= 2
    return a, x


def _scan_tile_of(s, rev, nct, nt):
    if not rev:
        return s
    return jnp.where(s < nct, nct - 1 - s, nt - 1 - (s - nct))


def _lru_scan(a2, x2, d, nct, tm, tc):
    _, B, T, W = a2.shape
    nt = T // tm
    rev = d == 1
    last = 0 if rev else tm - 1

    def body(a_ref, x_ref, h_ref, carry):
        s = pl.program_id(2)

        @pl.when(s == 0)
        def _():
            carry[...] = jnp.zeros_like(carry)

        A, X = _tile_scan(a_ref[0, 0], x_ref[0, 0], rev, tm)
        h = X + A * carry[0:1, :]
        h_ref[0] = h
        carry[...] = jnp.broadcast_to(h[last:last + 1, :], carry.shape)

    tidx = lambda s: _scan_tile_of(s, rev, nct, nt)
    spec = pl.BlockSpec((1, 1, tm, tc), lambda b, j, s: (d, b, tidx(s), j))
    return pl.pallas_call(
        body, name=f"lru_scan{d}", grid=(B, W // tc, nt),
        in_specs=[spec, spec], out_specs=pl.BlockSpec((1, tm, tc), lambda b, j, s: (b, tidx(s), j)),
        out_shape=jax.ShapeDtypeStruct((B, T, W), F32),
        scratch_shapes=[pltpu.VMEM((SUBLANES, tc), F32)],
        compiler_params=_cparams(("parallel", "parallel", "arbitrary"), VMEM_LIMIT),
    )(a2, x2)


def _lru_scan_bwd(a2, h, dh, d, nct, tm, tc):
    _, B, T, W = a2.shape
    nt = T // tm
    rev = d == 1
    first = tm - 1 if rev else 0
    r8, n8 = tm // SUBLANES, T // SUBLANES

    def body(a_ref, h_ref, hp_ref, dh_ref, lam_ref, da_ref, ca, cl):
        s = pl.program_id(2)

        @pl.when(s == 0)
        def _():
            ca[...] = jnp.zeros_like(ca)
            cl[...] = jnp.zeros_like(cl)

        a = a_ref[0, 0]
        row = lax.broadcasted_iota(jnp.int32, a.shape, 0)
        if rev:
            c = jnp.where(row == 0, ca[0:1, :], pltpu.roll(a, 1, 0))
        else:
            c = jnp.where(row == tm - 1, ca[0:1, :], pltpu.roll(a, tm - 1, 0))
        C, L = _tile_scan(c, dh_ref[0], not rev, tm)
        lam = L + C * cl[0:1, :]
        lam_ref[0] = lam
        hp = jnp.where(s == nt - 1, 0.0, hp_ref[0])
        if rev:
            hprev = jnp.where(row == tm - 1, hp[0:1, :], pltpu.roll(h_ref[0], tm - 1, 0))
        else:
            hprev = jnp.where(row == 0, hp[SUBLANES - 1:SUBLANES, :], pltpu.roll(h_ref[0], 1, 0))
        da_ref[0] = lam * hprev
        ca[...] = jnp.broadcast_to(a[first:first + 1, :], ca.shape)
        cl[...] = jnp.broadcast_to(lam[first:first + 1, :], cl.shape)

    tidx = lambda s: _scan_tile_of(nt - 1 - s, rev, nct, nt)

    def hp_idx(b, j, s):
        tt = tidx(s)
        if rev:
            blk = jnp.where(tt == nt - 1, 0, jnp.minimum((tt + 1) * r8, n8 - 1))
        else:
            blk = jnp.maximum(tt * r8 - 1, 0)
        return (b, blk, j)

    tile = pl.BlockSpec((1, tm, tc), lambda b, j, s: (b, tidx(s), j))
    return pl.pallas_call(
        body, name=f"lru_scan_bwd{d}", grid=(B, W // tc, nt),
        in_specs=[pl.BlockSpec((1, 1, tm, tc), lambda b, j, s: (d, b, tidx(s), j)), tile,
                  pl.BlockSpec((1, SUBLANES, tc), hp_idx), tile],
        out_specs=(tile, tile),
        out_shape=(jax.ShapeDtypeStruct((B, T, W), F32), jax.ShapeDtypeStruct((B, T, W), F32)),
        scratch_shapes=[pltpu.VMEM((SUBLANES, tc), F32), pltpu.VMEM((SUBLANES, tc), F32)],
        compiler_params=_cparams(("parallel", "parallel", "arbitrary"), VMEM_LIMIT),
    )(a2, h, h, dh)


def _lru_out(h0, h1, pyl, ycb0, nct, tm, tc):
    B, T, W = h0.shape
    N = T - nct * tm

    def body(h0_ref, h1_ref, y_ref, o_ref):
        o_ref[0] = ((h0_ref[0] + h1_ref[0]) * _gelu(y_ref[0])).astype(BF16)

    hs = pl.BlockSpec((1, tm, tc), lambda b, t, j: (b, t + nct, j))
    return pl.pallas_call(
        body, name="lru_out", grid=(B, N // tm, W // tc),
        in_specs=[hs, hs, pl.BlockSpec((1, tm, tc), lambda b, t, j: (b, t + nct, ycb0 + j))],
        out_specs=pl.BlockSpec((1, tm, tc), lambda b, t, j: (b, t, j)),
        out_shape=jax.ShapeDtypeStruct((B, N, W), BF16),
        compiler_params=_cparams(("parallel",) * 3, VMEM_LIMIT))(h0, h1, pyl)


def _lru_out_bwd(h0, h1, pyl, ycb0, dy, nct, tm, tc):
    B, T, W = h0.shape

    def body(h0_ref, h1_ref, y_ref, dy_ref, dh_ref, dyl_ref):
        t = pl.program_id(1)

        @pl.when(t < nct)
        def _():
            dh_ref[0] = jnp.zeros_like(dh_ref[0])
            dyl_ref[0] = jnp.zeros_like(dyl_ref[0])

        @pl.when(t >= nct)
        def _():
            y = y_ref[0]
            dy_ = dy_ref[0]
            dh_ref[0] = dy_ * _gelu(y)
            dyl_ref[0] = (dy_ * (h0_ref[0] + h1_ref[0]) * _dgelu(y)).astype(BF16)

    hs = pl.BlockSpec((1, tm, tc), lambda b, t, j: (b, t, j))
    return pl.pallas_call(
        body, name="lru_out_bwd", grid=(B, T // tm, W // tc),
        in_specs=[hs, hs, pl.BlockSpec((1, tm, tc), lambda b, t, j: (b, t, ycb0 + j)),
                  pl.BlockSpec((1, tm, tc), lambda b, t, j: (b, jnp.maximum(t - nct, 0), j))],
        out_specs=(hs, hs),
        out_shape=(jax.ShapeDtypeStruct((B, T, W), F32), jax.ShapeDtypeStruct((B, T, W), BF16)),
        compiler_params=_cparams(("parallel",) * 3, VMEM_LIMIT))(h0, h1, pyl, dy)


def _merge(bd, bl, pmg, bm, nct, tm):
    B, N, D = bd.shape

    def body(bd_ref, bl_ref, m0_ref, m1_ref, b_ref, o_ref):
        g0 = _sigmoid(m0_ref[0] + b_ref[:, 0:D])
        g1 = _sigmoid(m1_ref[0] + b_ref[:, D:2 * D])
        o_ref[0] = (g0 * bd_ref[0] + g1 * bl_ref[0]).astype(BF16)

    row = pl.BlockSpec((1, tm, D), lambda b, t: (b, t, 0))
    return pl.pallas_call(
        body, name="merge", grid=(B, N // tm),
        in_specs=[row, row, pl.BlockSpec((1, tm, D), lambda b, t: (b, t + nct, 0)),
                  pl.BlockSpec((1, tm, D), lambda b, t: (b, t + nct, 1)),
                  pl.BlockSpec((1, 2 * D), lambda b, t: (0, 0))],
        out_specs=row, out_shape=jax.ShapeDtypeStruct((B, N, D), BF16),
        compiler_params=_cparams(("parallel", "parallel"), VMEM_LIMIT))(bd, bl, pmg, pmg, bm)


def _merge_bwd(dmi, bd, bl, pmg, bm, nct, tm):
    B, N, D = bd.shape
    T = pmg.shape[1]

    def body(dm_ref, bd_ref, bl_ref, m0_ref, m1_ref, b_ref, dbd_ref, dbl_ref, dmg_ref, acc_ref):
        t = pl.program_id(1)

        @pl.when(t == 0)
        def _():
            acc_ref[...] = jnp.zeros_like(acc_ref)

        @pl.when(t < nct)
        def _():
            dmg_ref[0] = jnp.zeros_like(dmg_ref[0])

        @pl.when(t >= nct)
        def _():
            dm = dm_ref[0]
            g0 = _sigmoid(m0_ref[0] + b_ref[:, 0:D])
            g1 = _sigmoid(m1_ref[0] + b_ref[:, D:2 * D])
            dbd_ref[0] = (dm * g0).astype(BF16)
            dbl_ref[0] = (dm * g1).astype(BF16)
            d0 = dm * bd_ref[0] * g0 * (1.0 - g0)
            d1 = dm * bl_ref[0] * g1 * (1.0 - g1)
            dmg_ref[0, :, 0:D] = d0.astype(BF16)
            dmg_ref[0, :, D:2 * D] = d1.astype(BF16)
            acc_ref[0, 0:1, 0:D] += jnp.sum(d0, axis=0, keepdims=True)
            acc_ref[0, 0:1, D:2 * D] += jnp.sum(d1, axis=0, keepdims=True)

    lat = pl.BlockSpec((1, tm, D), lambda b, t: (b, jnp.maximum(t - nct, 0), 0))
    return pl.pallas_call(
        body, name="merge_bwd", grid=(B, T // tm),
        in_specs=[lat, lat, lat, pl.BlockSpec((1, tm, D), lambda b, t: (b, t, 0)),
                  pl.BlockSpec((1, tm, D), lambda b, t: (b, t, 1)),
                  pl.BlockSpec((1, 2 * D), lambda b, t: (0, 0))],
        out_specs=(lat, lat, pl.BlockSpec((1, tm, 2 * D), lambda b, t: (b, t, 0)),
                   pl.BlockSpec((1, SUBLANES, 2 * D), lambda b, t: (b, 0, 0))),
        out_shape=(jax.ShapeDtypeStruct((B, N, D), BF16), jax.ShapeDtypeStruct((B, N, D), BF16),
                   jax.ShapeDtypeStruct((B, T, 2 * D), BF16), jax.ShapeDtypeStruct((B, SUBLANES, 2 * D), F32)),
        compiler_params=_cparams(("parallel", "arbitrary"), VMEM_LIMIT))(dmi, bd, bl, pmg, pmg, bm)


def _grid_taps(n_rows):
    taps = []
    for di in (-1, 0, 1):
        for dj in (-1, 0, 1):
            taps.append(((di + 1) * 3 + (dj + 1), di, dj))
    return taps


def _grid_shift(x, di, dj, gr, gc, n):
    off = di * GRID_W + dj
    y = x if off == 0 else pltpu.roll(x, (-off) % n, 0)
    ok = jnp.logical_and(jnp.logical_and(gr + di >= 0, gr + di < n // GRID_W),
                         jnp.logical_and(gc + dj >= 0, gc + dj < GRID_W))
    return jnp.where(ok, y, 0.0)


def _ffn_act(F, w9, b, tc):
    B, N, C2 = F.shape
    Cf = C2 // 2
    ncb = Cf // tc

    def body(g_ref, v_ref, w_ref, b_ref, o_ref):
        x = g_ref[0]
        pos = lax.broadcasted_iota(jnp.int32, x.shape, 0)
        gr, gc = jnp.right_shift(pos, GRID_SHIFT), jnp.bitwise_and(pos, GRID_W - 1)
        conv = jnp.broadcast_to(b_ref[...], x.shape)
        for k, di, dj in _grid_taps(N):
            conv = conv + _grid_shift(x, di, dj, gr, gc, N) * w_ref[k:k + 1, :]
        o_ref[0] = (_gelu(conv) * v_ref[0]).astype(BF16)

    return pl.pallas_call(
        body, name="ffn_act", grid=(B, ncb),
        in_specs=[pl.BlockSpec((1, N, tc), lambda b, j: (b, 0, j)),
                  pl.BlockSpec((1, N, tc), lambda b, j: (b, 0, ncb + j)),
                  pl.BlockSpec((9, tc), lambda b, j: (0, j)),
                  pl.BlockSpec((1, tc), lambda b, j: (0, j))],
        out_specs=pl.BlockSpec((1, N, tc), lambda b, j: (b, 0, j)),
        out_shape=jax.ShapeDtypeStruct((B, N, Cf), BF16),
        compiler_params=_cparams(("parallel", "parallel"), VMEM_LIMIT))(F, F, w9, b)


def _ffn_act_bwd(F, w9, b, df, tc):
    B, N, C2 = F.shape
    Cf = C2 // 2
    ncb = Cf // tc

    def body(g_ref, v_ref, w_ref, b_ref, df_ref, dF_ref, acc_ref, dv_s):
        p = pl.program_id(2)

        @pl.when(p == 0)
        def _():
            x = g_ref[0]
            pos = lax.broadcasted_iota(jnp.int32, x.shape, 0)
            gr, gc = jnp.right_shift(pos, GRID_SHIFT), jnp.bitwise_and(pos, GRID_W - 1)
            conv = jnp.broadcast_to(b_ref[...], x.shape)
            for k, di, dj in _grid_taps(N):
                conv = conv + _grid_shift(x, di, dj, gr, gc, N) * w_ref[k:k + 1, :]
            df_ = df_ref[0]
            dv_s[...] = (df_ * _gelu(conv)).astype(BF16)
            dc = df_ * v_ref[0] * _dgelu(conv)
            dx = jnp.zeros_like(x)
            for k, di, dj in _grid_taps(N):
                dx = dx + _grid_shift(dc, -di, -dj, gr, gc, N) * w_ref[k:k + 1, :]
                acc_ref[0, k:k + 1, :] = jnp.sum(dc * _grid_shift(x, di, dj, gr, gc, N), axis=0, keepdims=True)
            acc_ref[0, 9:10, :] = jnp.sum(dc, axis=0, keepdims=True)
            acc_ref[0, 10:16, :] = jnp.zeros((6, tc), F32)
            dF_ref[0] = dx.astype(BF16)

        @pl.when(p == 1)
        def _():
            dF_ref[0] = dv_s[...]

    return pl.pallas_call(
        body, name="ffn_act_bwd", grid=(B, ncb, 2),
        in_specs=[pl.BlockSpec((1, N, tc), lambda b, j, p: (b, 0, j)),
                  pl.BlockSpec((1, N, tc), lambda b, j, p: (b, 0, ncb + j)),
                  pl.BlockSpec((9, tc), lambda b, j, p: (0, j)),
                  pl.BlockSpec((1, tc), lambda b, j, p: (0, j)),
                  pl.BlockSpec((1, N, tc), lambda b, j, p: (b, 0, j))],
        out_specs=(pl.BlockSpec((1, N, tc), lambda b, j, p: (b, 0, j + p * ncb)),
                   pl.BlockSpec((1, 16, tc), lambda b, j, p: (b, 0, j))),
        out_shape=(jax.ShapeDtypeStruct((B, N, C2), BF16), jax.ShapeDtypeStruct((B, 16, Cf), F32)),
        scratch_shapes=[pltpu.VMEM((N, tc), BF16)],
        compiler_params=_cparams(("parallel", "parallel", "arbitrary"), VMEM_LIMIT))(F, F, w9, b, df)


ADAM_ROWS = 512


def _adamw(w, m, v, gparts, name):
    rows = w.shape[0]
    P = gparts.shape[0]
    tr = min(rows, ADAM_ROWS)
    c1 = 1.0 - ADAM_B1 ** ADAM_STEP
    c2 = 1.0 - ADAM_B2 ** ADAM_STEP

    def body(w_ref, m_ref, v_ref, g_ref, go_ref, d_ref, mo_ref, vo_ref):
        g = g_ref[0]
        for p in range(1, P):
            g = g + g_ref[p]
        m_ = ADAM_B1 * m_ref[...] + (1.0 - ADAM_B1) * g
        v_ = ADAM_B2 * v_ref[...] + (1.0 - ADAM_B2) * (g * g)
        go_ref[...] = g
        mo_ref[...] = m_
        vo_ref[...] = v_
        d_ref[...] = -ADAM_LR * ((m_ / c1) / (jnp.sqrt(v_ / c2) + ADAM_EPS) + ADAM_WD * w_ref[...])

    row = pl.BlockSpec((tr, LANES), lambda i: (i, 0))
    out = jax.ShapeDtypeStruct((rows, LANES), F32)
    return pl.pallas_call(
        body, name=name, grid=(rows // tr,),
        in_specs=[row, row, row, pl.BlockSpec((P, tr, LANES), lambda i: (0, i, 0))],
        out_specs=(row, row, row, row), out_shape=(out, out, out, out),
        compiler_params=_cparams(("parallel",), VMEM_LIMIT))(w, m, v, gparts)


def _pack_rows(flat_len):
    rows = -(-flat_len // LANES)
    if rows > ADAM_ROWS:
        rows = -(-rows // ADAM_ROWS) * ADAM_ROWS
    else:
        rows = -(-rows // SUBLANES) * SUBLANES
    return rows


def _pack(arrs, lead=()):
    flat = jnp.concatenate([a.reshape(lead + (-1,)).astype(F32) for a in arrs], axis=-1)
    n = flat.shape[-1]
    rows = _pack_rows(n)
    flat = jnp.pad(flat, [(0, 0)] * len(lead) + [(0, rows * LANES - n)])
    return flat.reshape(lead + (rows, LANES))


def _unpack(buf, shapes):
    flat = buf.reshape(-1)
    out, o = [], 0
    for s in shapes:
        n = math.prod(s)
        out.append(flat[o:o + n].reshape(s))
        o += n
    return out


def _col_shards(full, n_lead):
    C = full.shape[-1]
    x = full.reshape(full.shape[:-1] + (N_DEV, C // N_DEV))
    return jnp.moveaxis(x, -2, 0)


def _from_col_shards(g):
    x = jnp.moveaxis(g, 0, -2)
    return x.reshape(x.shape[:-2] + (x.shape[-2] * x.shape[-1],))


def kernel(x, c, ctx, c_ctx, w_ada, b_ada, g_pre_mix, g_post_mix, g_pre_ffn, g_post_ffn, w_in, b_merge, dn_conv, dn_a_log, dn_dt_bias, dn_onorm, lru_conv, lru_conv_b, lru_w_rg, lru_b_rg, lru_w_ig, lru_b_ig, lru_lambda, w_branch_dn, w_branch_lru, w_out, w_up, ffn_dw, ffn_dw_b, w_down, loss_target, m_c_ctx, m_w_ada, m_b_ada, m_g_pre_mix, m_g_post_mix, m_g_pre_ffn, m_g_post_ffn, m_w_in, m_b_merge, m_dn_conv, m_dn_a_log, m_dn_dt_bias, m_dn_onorm, m_lru_conv, m_lru_conv_b, m_lru_w_rg, m_lru_b_rg, m_lru_w_ig, m_lru_b_ig, m_lru_lambda, m_w_branch_dn, m_w_branch_lru, m_w_out, m_w_up, m_ffn_dw, m_ffn_dw_b, m_w_down, v_c_ctx, v_w_ada, v_b_ada, v_g_pre_mix, v_g_post_mix, v_g_pre_ffn, v_g_post_ffn, v_w_in, v_b_merge, v_dn_conv, v_dn_a_log, v_dn_dt_bias, v_dn_onorm, v_lru_conv, v_lru_conv_b, v_lru_w_rg, v_lru_b_rg, v_lru_w_ig, v_lru_b_ig, v_lru_lambda, v_w_branch_dn, v_w_branch_lru, v_w_out, v_w_up, v_ffn_dw, v_ffn_dw_b, v_w_down):
    params = dict(c_ctx=c_ctx, w_ada=w_ada, b_ada=b_ada, g_pre_mix=g_pre_mix, g_post_mix=g_post_mix, g_pre_ffn=g_pre_ffn, g_post_ffn=g_post_ffn, w_in=w_in, b_merge=b_merge, dn_conv=dn_conv, dn_a_log=dn_a_log, dn_dt_bias=dn_dt_bias, dn_onorm=dn_onorm, lru_conv=lru_conv, lru_conv_b=lru_conv_b, lru_w_rg=lru_w_rg, lru_b_rg=lru_b_rg, lru_w_ig=lru_w_ig, lru_b_ig=lru_b_ig, lru_lambda=lru_lambda, w_branch_dn=w_branch_dn, w_branch_lru=w_branch_lru, w_out=w_out, w_up=w_up, ffn_dw=ffn_dw, ffn_dw_b=ffn_dw_b, w_down=w_down)
    mom1 = dict(c_ctx=m_c_ctx, w_ada=m_w_ada, b_ada=m_b_ada, g_pre_mix=m_g_pre_mix, g_post_mix=m_g_post_mix, g_pre_ffn=m_g_pre_ffn, g_post_ffn=m_g_post_ffn, w_in=m_w_in, b_merge=m_b_merge, dn_conv=m_dn_conv, dn_a_log=m_dn_a_log, dn_dt_bias=m_dn_dt_bias, dn_onorm=m_dn_onorm, lru_conv=m_lru_conv, lru_conv_b=m_lru_conv_b, lru_w_rg=m_lru_w_rg, lru_b_rg=m_lru_b_rg, lru_w_ig=m_lru_w_ig, lru_b_ig=m_lru_b_ig, lru_lambda=m_lru_lambda, w_branch_dn=m_w_branch_dn, w_branch_lru=m_w_branch_lru, w_out=m_w_out, w_up=m_w_up, ffn_dw=m_ffn_dw, ffn_dw_b=m_ffn_dw_b, w_down=m_w_down)
    mom2 = dict(c_ctx=v_c_ctx, w_ada=v_w_ada, b_ada=v_b_ada, g_pre_mix=v_g_pre_mix, g_post_mix=v_g_post_mix, g_pre_ffn=v_g_pre_ffn, g_post_ffn=v_g_post_ffn, w_in=v_w_in, b_merge=v_b_merge, dn_conv=v_dn_conv, dn_a_log=v_dn_a_log, dn_dt_bias=v_dn_dt_bias, dn_onorm=v_dn_onorm, lru_conv=v_lru_conv, lru_conv_b=v_lru_conv_b, lru_w_rg=v_lru_w_rg, lru_b_rg=v_lru_b_rg, lru_w_ig=v_lru_w_ig, lru_b_ig=v_lru_b_ig, lru_lambda=v_lru_lambda, w_branch_dn=v_w_branch_dn, w_branch_lru=v_w_branch_lru, w_out=v_w_out, w_up=v_w_up, ffn_dw=v_ffn_dw, ffn_dw_b=v_ffn_dw_b, w_down=v_w_down)
    names = list(params)

    B, N, D = x.shape
    NC = ctx.shape[1]
    T = NC + N
    H, hd = dn_a_log.shape[2], dn_onorm.shape[1]
    HD = H * hd
    nd = 2 * H
    W = lru_conv_b.shape[1]
    nblk, bdim = lru_w_rg.shape[2], lru_w_rg.shape[3]
    Cf = ffn_dw_b.shape[1]
    sw = w_ada.shape[2]
    tm = min(256, NC)
    nct = NC // tm
    ncc = NC // CHUNK
    nch = T // CHUNK
    R, RL = B * T, B * N
    bw = min(256, W)
    me = _my_index()
    assert NC % tm == 0 and N % tm == 0 and B + 1 <= SUBLANES and bw % bdim == 0 and D % LANES == 0

    cpad = jnp.zeros((SUBLANES, D), F32).at[:B].set(c).at[B].set(c_ctx)
    ccp = _all_gather(cpad, "ag_cond").reshape(N_DEV * SUBLANES, D)
    mods_sh = _ada_fwd(ccp, w_ada[0], lax.dynamic_slice(b_ada, (0, me * sw), (1, sw)))
    mods_g = _all_gather(mods_sh, "ag_mods")
    mine = lax.dynamic_slice(mods_g, (0, me * SUBLANES, 0), (N_DEV, SUBLANES, sw))
    mods = jnp.moveaxis(mine, 0, 1).reshape(SUBLANES, 6, D)[:B + 1]
    mods = jnp.pad(mods, ((0, 0), (0, SUBLANES - 6), (0, 0)))

    big = [w_in[0], w_up[0], w_down[0], w_branch_dn[0], w_branch_lru[0], w_out[0]]
    big_g = _all_gather(_pack([a.astype(BF16).astype(F32) for a in big]).astype(BF16), "ag_weights")
    small = [dn_conv[0], lru_conv[0], lru_b_rg[0], lru_b_ig[0], lru_lambda[0], ffn_dw[0]]
    small_g = _all_gather(_pack(small), "ag_small")
    big_s = [jnp.stack(p) for p in zip(*[_unpack(big_g[d_], [a.shape for a in big]) for d_ in range(N_DEV)])]
    small_s = [jnp.stack(p) for p in zip(*[_unpack(small_g[d_], [a.shape for a in small]) for d_ in range(N_DEV)])]
    w_in_f = _from_col_shards(big_s[0])
    w_up_f = _from_col_shards(big_s[1])
    w_down_f = big_s[2].reshape(Cf, D)
    w_bdn_f, w_blru_f, w_out_f = big_s[3].reshape(HD, D), big_s[4].reshape(W, D), big_s[5].reshape(D, D)
    dn_conv_f = _from_col_shards(small_s[0])
    lru_conv_f = _from_col_shards(small_s[1])
    lru_brg_f, lru_big_f, lru_lam_f = (_from_col_shards(small_s[i]) for i in (2, 3, 4))
    ffn_dw_f = _from_col_shards(small_s[5]).reshape(9, Cf)

    o_z, o_a, o_xl = 3 * HD, 4 * HD, 4 * HD + 2 * nd
    o_yl, o_mg = o_xl + W, o_xl + 2 * W
    w_qkv, w_z = w_in_f[:, :o_z], w_in_f[:, o_z:o_a]
    w_ab = jnp.pad(w_in_f[:, o_a:o_xl], ((0, 0), (0, LANES - 2 * nd)))
    w_xl, w_yl, w_mg = w_in_f[:, o_xl:o_yl], w_in_f[:, o_yl:o_mg], w_in_f[:, o_mg:]

    def blockdiag(wg):
        per = bw // bdim
        g5 = wg.reshape(2, W // bw, per, bdim, bdim)
        eye = jnp.eye(per, dtype=wg.dtype)
        return jnp.einsum("dtpij,pq->dtpiqj", g5, eye).reshape(2, W // bw, bw, bw)

    bd_rg, bd_ig = blockdiag(lru_w_rg[0]), blockdiag(lru_w_ig[0])
    wbd = jnp.stack([bd_rg[0], bd_ig[0], bd_rg[1], bd_ig[1]]).astype(BF16)
    bias4 = jnp.stack([lru_brg_f[0], lru_big_f[0], lru_brg_f[1], lru_big_f[1]])
    alog_v = jnp.pad(dn_a_log.reshape(1, nd), ((0, 0), (0, LANES - nd)))
    dtb_v = jnp.pad(dn_dt_bias.reshape(1, nd), ((0, 0), (0, LANES - nd)))

    h0 = jnp.concatenate([ctx, x], axis=1)
    u1 = _norm_mod_fwd(h0, g_pre_mix, mods, 0, 1, nct, tm, "norm_mod1")
    u1f = u1.reshape(R, D)
    p_qkv = _mm(u1f, w_qkv, "nn", "mm_qkv").reshape(B, T, 3 * HD)
    p_z = _mm(u1f, w_z, "nn", "mm_z").reshape(B, T, HD)
    p_ab = _mm(u1f, w_ab, "nn", "mm_ab")
    p_xl = _mm(u1f, w_xl, "nn", "mm_xl").reshape(B, T, W)
    p_yl = _mm(u1f, w_yl, "nn", "mm_yl").reshape(B, T, W)
    p_mg = _mm(u1f, w_mg, "nn", "mm_mg").reshape(B, T, 2 * D)

    conv_offs = (-2, -1, 0, 1)
    tcc = _tile(HD, 512)
    gb = _ab_act(p_ab, alog_v, dtb_v, nd, tm)
    gb5 = gb.reshape(B, nch, CHUNK, LANES)
    g_ = gb5[..., :nd].reshape(B, nch, CHUNK, 2, H)
    be_ = gb5[..., nd:2 * nd].reshape(B, nch, CHUNK, 2, H)
    gbc = jnp.concatenate([g_, be_], axis=-1).transpose(0, 3, 1, 2, 4)
    gbr = gbc.transpose(0, 1, 2, 4, 3)
    cv = _dwconv(p_qkv, 0, 3 * HD, dn_conv_f, None, conv_offs, nct, tm, tcc, "dn_conv")
    qkvn = _dn_act(cv, H, hd, tm)
    o2, s_in = _delta_fwd(qkvn, gbc, gbr, H, hd, ncc)
    y_dn = _dn_out(o2, p_z, 0, dn_onorm, H, hd, nct, tm)

    tcw = _tile(W, 512)
    xc = _dwconv(p_xl, 0, W, lru_conv_f, lru_conv_b, conv_offs, nct, tm, tcw, "lru_conv")
    a2, inp2 = _lru_gates(xc, wbd, bias4, lru_lam_f, tm, bw)
    hd0 = _lru_scan(a2, inp2, 0, nct, tm, tcw)
    hd1 = _lru_scan(a2, inp2, 1, nct, tm, tcw)
    y_lru = _lru_out(hd0, hd1, p_yl, 0, nct, tm, tcw)

    bd = _mm(y_dn.reshape(RL, HD), w_bdn_f, "nn", "mm_bdn").reshape(B, N, D)
    bl = _mm(y_lru.reshape(RL, W), w_blru_f, "nn", "mm_blru").reshape(B, N, D)
    mixin = _merge(bd, bl, p_mg, b_merge, nct, tm)
    mix = _mm(mixin.reshape(RL, D), w_out_f, "nn", "mm_out").reshape(B, N, D)
    h1 = _resid_norm_fwd(x, mix, g_post_mix, mods, 2, tm, "resid_norm1")
    u2 = _norm_mod_fwd(h1, g_pre_ffn, mods, 3, 4, 0, tm, "norm_mod2")
    Fp = _mm(u2.reshape(RL, D), w_up_f, "nn", "mm_up").reshape(B, N, 2 * Cf)
    tcf = LANES
    f = _ffn_act(Fp, ffn_dw_f, ffn_dw_b, tcf)
    dproj = _mm(f.reshape(RL, Cf), w_down_f, "nn", "mm_down").reshape(B, N, D)

    dd, dh2, acc_f = _final_fwd_bwd(h1, dproj, g_post_ffn, mods, 5, loss_target, tm)
    loss = lax.psum((0.5 / D) * jnp.sum(acc_f[:, 2, :]), MESH_AXES)
    ddf = dd.reshape(RL, D)
    df = _mm(ddf, w_down_f, "nt", "mm_down_dx").reshape(B, N, Cf)
    gw_down = _mm(f.reshape(RL, Cf), ddf, "tn", "mm_down_dw")
    dF, acc_ffn = _ffn_act_bwd(Fp, ffn_dw_f, ffn_dw_b, df, tcf)
    dFf = dF.reshape(RL, 2 * Cf)
    du2 = _mm(dFf, w_up_f, "nt", "mm_up_dx").reshape(B, N, D)
    gw_up = _mm(u2.reshape(RL, D), dFf, "tn", "mm_up_dw")
    dh1, acc2x, _ = _norm_mod_bwd(h1, du2, g_pre_ffn, mods, 4, 0, tm, dh2, "norm_mod2_bwd")
    dmix, acc_r1 = _resid_norm_bwd(mix, dh1, g_post_mix, mods, 2, tm, "resid_norm1_bwd")
    dmixf = dmix.reshape(RL, D)
    dmixin = _mm(dmixf, w_out_f, "nt", "mm_out_dx").reshape(B, N, D)
    gw_out = _mm(mixin.reshape(RL, D), dmixf, "tn", "mm_out_dw")
    dbd, dbl, dmg, acc_mg = _merge_bwd(dmixin, bd, bl, p_mg, b_merge, nct, tm)
    dy_dn = _mm(dbd.reshape(RL, D), w_bdn_f, "nt", "mm_bdn_dx").reshape(B, N, HD)
    gw_bdn = _mm(y_dn.reshape(RL, HD), dbd.reshape(RL, D), "tn", "mm_bdn_dw")
    dy_lru = _mm(dbl.reshape(RL, D), w_blru_f, "nt", "mm_blru_dx").reshape(B, N, W)
    gw_blru = _mm(y_lru.reshape(RL, W), dbl.reshape(RL, D), "tn", "mm_blru_dw")

    dh, dyl = _lru_out_bwd(hd0, hd1, p_yl, 0, dy_lru, nct, tm, tcw)
    lam0, da0 = _lru_scan_bwd(a2, hd0, dh, 0, nct, tm, tcw)
    lam1, da1 = _lru_scan_bwd(a2, hd1, dh, 1, nct, tm, tcw)
    dxc, dwbd, acc_lru = _lru_gates_bwd(xc, wbd, bias4, lru_lam_f, jnp.stack([lam0, lam1]), jnp.stack([da0, da1]), tm, bw)
    dxl = _dwconv(dxc, 0, W, lru_conv_f, None, tuple(-o for o in conv_offs), nct, tm, tcw, "lru_conv_dx")
    acc_lc = _dwconv_wgrad(dxc, p_xl, 0, W, conv_offs, nct, tm, tcw, "lru_conv_dw")

    do, dz, acc_on = _dn_out_bwd(o2, p_z, 0, dn_onorm, dy_dn, H, hd, nct, tm)
    dqkvn, dgb = _delta_bwd(qkvn, gbc, gbr, s_in, do, H, hd, ncc)
    dcv = _dn_act_bwd(cv, dqkvn, H, hd, tm)
    dqkv = _dwconv(dcv, 0, 3 * HD, dn_conv_f, None, tuple(-o for o in conv_offs), nct, tm, tcc, "dn_conv_dx")
    acc_dc = _dwconv_wgrad(dcv, p_qkv, 0, 3 * HD, conv_offs, nct, tm, tcc, "dn_conv_dw")
    dgb_t = dgb.transpose(0, 2, 3, 1, 4)
    dgb_l = jnp.concatenate([dgb_t[..., :H].reshape(B, nch, CHUNK, nd), dgb_t[..., H:].reshape(B, nch, CHUNK, nd)], axis=-1)
    dgb_l = jnp.pad(dgb_l.reshape(R, 2 * nd), ((0, 0), (0, LANES - 2 * nd)))
    dp_ab, acc_ab = _ab_act_bwd(p_ab, gb, dgb_l, alog_v, dtb_v, nd, tm)

    groups = [(dqkv.reshape(R, 3 * HD), w_qkv, "qkv"), (dz.reshape(R, HD), w_z, "z"), (dp_ab, w_ab, "ab"),
              (dxl.reshape(R, W), w_xl, "xl"), (dyl.reshape(R, W), w_yl, "yl"), (dmg.reshape(R, 2 * D), w_mg, "mg")]
    du1 = None
    gw_groups = []
    for dg_, wg_, nm in groups:
        du1 = _mm(dg_, wg_, "nt", "mm_in_dx_" + nm, add=du1)
        gw_groups.append(_mm(u1f, dg_, "tn", "mm_in_dw_" + nm))
    gw_groups[2] = gw_groups[2][:, :2 * nd]
    gw_in = jnp.concatenate(gw_groups, axis=1)
    grad_x, acc1x, acc1c = _norm_mod_bwd(h0, du1.reshape(B, T, D), g_pre_mix, mods, 1, nct, tm, dh1, "norm_mod1_bwd")

    zeros_d = jnp.zeros((B, D), F32)
    dm_rows = jnp.stack([acc1x[:, 0], acc1x[:, 1], acc_r1[:, 0], acc2x[:, 0], acc2x[:, 1], acc_f[:, 0]], axis=1)
    dm_ctx = jnp.stack([jnp.sum(acc1c[:, 0], 0), jnp.sum(acc1c[:, 1], 0)] + [zeros_d[0]] * 4, axis=0)[None]
    dm_pad = jnp.zeros((SUBLANES, 6 * D), F32).at[:B + 1].set(jnp.concatenate([dm_rows, dm_ctx], 0).reshape(B + 1, 6 * D))
    dm_g = _all_gather(dm_pad, "ag_dmods")
    g_mine = lax.dynamic_slice(dm_g, (0, 0, me * sw), (N_DEV, SUBLANES, sw)).reshape(N_DEV * SUBLANES, sw)
    gw_ada, dcc = _ada_bwd(ccp, w_ada[0], g_mine, c_ctx.reshape(1, D), B)

    per = bw // bdim

    def diag_blocks(dwf):
        g6 = dwf.reshape(W // bw, per, bdim, per, bdim)
        return jnp.einsum("tpiqj,pq->tpij", g6, jnp.eye(per, dtype=F32)).reshape(nblk, bdim, bdim)

    g_rep = dict(
        c_ctx=dcc[0],
        g_pre_mix=jnp.sum(acc1x[:, 2] + acc1c[:, 2], 0)[None], g_post_mix=jnp.sum(acc_r1[:, 1], 0)[None],
        g_pre_ffn=jnp.sum(acc2x[:, 2], 0)[None], g_post_ffn=jnp.sum(acc_f[:, 1], 0)[None],
        b_merge=jnp.sum(acc_mg[:, 0], 0)[None],
        dn_a_log=acc_ab[0, :nd].reshape(1, 2, H), dn_dt_bias=acc_ab[1, :nd].reshape(1, 2, H),
        dn_onorm=jnp.sum(acc_on[:, 0], 0)[None],
        lru_conv_b=jnp.sum(acc_lc[:, 4], 0)[None],
        lru_w_rg=jnp.stack([diag_blocks(dwbd[0]), diag_blocks(dwbd[2])])[None],
        lru_w_ig=jnp.stack([diag_blocks(dwbd[1]), diag_blocks(dwbd[3])])[None],
        ffn_dw_b=jnp.sum(acc_ffn[:, 9], 0)[None])
    rep_names = list(g_rep)
    rep_parts = _all_gather(_pack([g_rep[n] for n in rep_names]), "ag_rep_grads")
    rep_out = _adamw(_pack([params[n] for n in rep_names]), _pack([mom1[n] for n in rep_names]),
                     _pack([mom2[n] for n in rep_names]), rep_parts, "adamw_rep")
    results = {}
    for k, buf in enumerate(rep_out):
        for n, arr in zip(rep_names, _unpack(buf, [params[n].shape for n in rep_names])):
            results.setdefault(n, [None] * 4)[k] = arr

    ba_out = _adamw(_pack([b_ada]), _pack([m_b_ada]), _pack([v_b_ada]),
                    _pack([dm_g.reshape(N_DEV * SUBLANES, 6 * D)], lead=(N_DEV * SUBLANES,)), "adamw_b_ada")
    results["b_ada"] = [_unpack(buf, [b_ada.shape])[0] for buf in ba_out]

    wa_out = _adamw(_pack([w_ada]), _pack([m_w_ada]), _pack([v_w_ada]), _pack([gw_ada])[None], "adamw_w_ada")
    results["w_ada"] = [_unpack(buf, [w_ada.shape])[0] for buf in wa_out]

    g_lru_conv = jnp.sum(acc_lc[:, :4], 0)
    g_dn_conv = jnp.sum(acc_dc[:, :4], 0)
    g_ffn_dw = jnp.sum(acc_ffn[:, :9], 0).reshape(3, 3, Cf)
    sh_names = ["w_in", "w_up", "w_down", "w_branch_dn", "w_branch_lru", "w_out", "dn_conv", "lru_conv",
                "lru_b_rg", "lru_b_ig", "lru_lambda", "ffn_dw"]
    sh_parts = [_col_shards(gw_in, 1), _col_shards(gw_up, 1), gw_down.reshape(N_DEV, -1), gw_bdn.reshape(N_DEV, -1),
                gw_blru.reshape(N_DEV, -1), gw_out.reshape(N_DEV, -1), _col_shards(g_dn_conv, 1), _col_shards(g_lru_conv, 1),
                _col_shards(jnp.stack([acc_lru[0], acc_lru[2]]), 1), _col_shards(jnp.stack([acc_lru[1], acc_lru[3]]), 1),
                _col_shards(acc_lru[4:6], 1), _col_shards(g_ffn_dw, 2)]
    sh_recv = _all_to_all(_pack(sh_parts, lead=(N_DEV,)), "a2a_grads")
    sh_out = _adamw(_pack([params[n] for n in sh_names]), _pack([mom1[n] for n in sh_names]),
                    _pack([mom2[n] for n in sh_names]), sh_recv, "adamw_sharded")
    for k, buf in enumerate(sh_out):
        for n, arr in zip(sh_names, _unpack(buf, [params[n].shape for n in sh_names])):
            results.setdefault(n, [None] * 4)[k] = arr

    outs = [loss, grad_x]
    for k in range(4):
        outs += [results[n][k] for n in names]
    return tuple(outs)
```

```python
import functools
import math

import jax
import jax.numpy as jnp
from jax import lax
from jax.experimental import pallas as pl
from jax.experimental.pallas import tpu as pltpu

F32 = jnp.float32
BF16 = jnp.bfloat16

EPS = 1e-6
GRID_W = 64
GRID_SHIFT = 6
CHUNK = 64
LRU_C = 8.0
N_DEV = 8
ADAM_LR = 0.001
ADAM_B1 = 0.9
ADAM_B2 = 0.999
ADAM_EPS = 1e-08
ADAM_WD = 0.01
ADAM_STEP = 10

LANES = 128
SUBLANES = 8
VMEM_LIMIT = 48 * 1024 * 1024
MESH_AXES = ("x", "y", "c")
GELU_C = math.sqrt(2.0 / math.pi)


def _cparams(sem=None, vmem=None):
    kw = {}
    if sem is not None:
        kw["dimension_semantics"] = sem
    if vmem is not None:
        kw["vmem_limit_bytes"] = vmem
    return pltpu.CompilerParams(**kw)


def _tile(n, pref):
    if n <= pref:
        return n
    t = (pref // LANES) * LANES
    while n % t:
        t -= LANES
    return t


def _sigmoid(x):
    return 1.0 / (1.0 + jnp.exp(-x))


def _silu(x):
    return x * _sigmoid(x)


def _dsilu(x):
    s = _sigmoid(x)
    return s * (1.0 + x * (1.0 - s))


def _softplus(x):
    return jnp.maximum(x, 0.0) + jnp.log(1.0 + jnp.exp(-jnp.abs(x)))


def _gelu(x):
    return 0.5 * x * (1.0 + jnp.tanh(GELU_C * (x + 0.044715 * x * x * x)))


def _dgelu(x):
    t = jnp.tanh(GELU_C * (x + 0.044715 * x * x * x))
    return 0.5 * (1.0 + t) + 0.5 * x * (1.0 - t * t) * GELU_C * (1.0 + 3.0 * 0.044715 * x * x)


def _dot(a, b, dims="nn"):
    dn = {"nn": (((1,), (0,)), ((), ())),
          "nt": (((1,), (1,)), ((), ())),
          "tn": (((0,), (0,)), ((), ()))}[dims]
    return lax.dot_general(a.astype(BF16), b.astype(BF16), dn, preferred_element_type=F32)


def _split2(x):
    hi = x.astype(BF16)
    lo = (x - hi.astype(F32)).astype(BF16)
    return hi, lo


def _split3(x):
    h1 = x.astype(BF16)
    r1 = x - h1.astype(F32)
    h2 = r1.astype(BF16)
    h3 = (r1 - h2.astype(F32)).astype(BF16)
    return h1, h2, h3


def _dot_hi(a, b):
    ah, al = _split2(a)
    bh, bl = _split2(b)
    return _dot(ah, bh) + (_dot(ah, bl) + _dot(al, bh))


def _dot_mask(m, x, dims="nn"):
    h1, h2, h3 = _split3(x)
    if dims == "xnt":
        return _dot(h1, m, "nt") + (_dot(h2, m, "nt") + _dot(h3, m, "nt"))
    return _dot(m, h1, dims) + (_dot(m, h2, dims) + _dot(m, h3, dims))


def _my_index():
    return 4 * lax.axis_index("x") + 2 * lax.axis_index("y") + lax.axis_index("c")


def _hbm_call(body, xs, out_shapes, n_sems, name):
    any_spec = pl.BlockSpec(memory_space=pl.ANY)
    return pl.pallas_call(
        body, name=name, out_shape=tuple(out_shapes),
        in_specs=[any_spec] * len(xs), out_specs=tuple([any_spec] * len(out_shapes)),
        scratch_shapes=[pltpu.SemaphoreType.DMA((n_sems,)), pltpu.SemaphoreType.DMA((n_sems,)),
                        pltpu.SemaphoreType.DMA((len(xs),))],
    )(*xs)


def _all_gather(xs, name):
    n = len(xs)

    def body(*refs):
        x_refs, out_refs = refs[:n], refs[n:2 * n]
        send_sems, recv_sems, local_sems = refs[2 * n:]
        x_, y_, c_ = lax.axis_index("x"), lax.axis_index("y"), lax.axis_index("c")
        me, sibling = (x_, y_, c_), (x_, y_, 1 - c_)
        chips = [(1 - x_, y_), (x_, 1 - y_), (1 - x_, 1 - y_)]

        def slab(a, px, py, pc):
            return out_refs[a].at[4 * px + 2 * py + pc]

        def copy(a, k, block, to, src=None):
            return pltpu.make_async_remote_copy(
                src_ref=slab(a, *block) if src is None else src, dst_ref=slab(a, *block),
                send_sem=send_sems.at[7 * a + k], recv_sem=recv_sems.at[7 * a + k],
                device_id=to, device_id_type=pl.DeviceIdType.MESH)

        mine = [pltpu.make_async_copy(x_refs[a], slab(a, *me), local_sems.at[a]) for a in range(n)]
        for cp in mine:
            cp.start()
        sent = []
        for j, chip in enumerate(chips):
            sent += [copy(a, 1 + j, me, (*chip, c_), src=x_refs[a]) for a in range(n)]
        sent += [copy(a, 0, me, sibling, src=x_refs[a]) for a in range(n)]
        for cp in sent:
            cp.start()
        for j, chip in enumerate(chips):
            for a in range(n):
                copy(a, 1 + j, (*chip, c_), me).wait_recv()
                fwd = copy(a, 4 + j, (*chip, c_), sibling)
                fwd.start()
                sent.append(fwd)
        for a in range(n):
            copy(a, 0, sibling, me).wait_recv()
        for j, chip in enumerate(chips):
            for a in range(n):
                copy(a, 4 + j, (*chip, 1 - c_), me).wait_recv()
        for cp in sent:
            cp.wait_send()
        for cp in mine:
            cp.wait()

    outs = [jax.ShapeDtypeStruct((N_DEV,) + x.shape, x.dtype) for x in xs]
    return _hbm_call(body, xs, outs, 7 * n, name)


def _pair_exchange(xs, name):
    n = len(xs)

    def body(*refs):
        x_refs, out_refs = refs[:n], refs[n:2 * n]
        send_sems, recv_sems, _ = refs[2 * n:]
        x_, y_, c_ = lax.axis_index("x"), lax.axis_index("y"), lax.axis_index("c")
        copies = []
        for a in range(n):
            for j in range(4):
                copies.append(pltpu.make_async_remote_copy(
                    src_ref=x_refs[a].at[2 * j + (1 - c_)], dst_ref=out_refs[a].at[j],
                    send_sem=send_sems.at[4 * a + j], recv_sem=recv_sems.at[4 * a + j],
                    device_id=(x_, y_, 1 - c_), device_id_type=pl.DeviceIdType.MESH))
        for cp in copies:
            cp.start()
        for cp in copies:
            cp.wait()

    outs = [jax.ShapeDtypeStruct((4,) + x.shape[1:], x.dtype) for x in xs]
    return _hbm_call(body, xs, outs, 4 * n, name)


def _quad_exchange(xs, name):
    n = len(xs)

    def body(*refs):
        x_refs, out_refs = refs[:n], refs[n:2 * n]
        send_sems, recv_sems, local_sems = refs[2 * n:]
        x_, y_, c_ = lax.axis_index("x"), lax.axis_index("y"), lax.axis_index("c")
        jm = 2 * x_ + y_
        mine = [pltpu.make_async_copy(x_refs[a].at[jm], out_refs[a].at[jm], local_sems.at[a]) for a in range(n)]
        for cp in mine:
            cp.start()
        sends, recvs = [], []
        for f in range(1, 4):
            px = 1 - x_ if (f >> 1) & 1 else x_
            py = 1 - y_ if f & 1 else y_
            jp = 2 * px + py
            for a in range(n):
                k = 3 * a + f - 1
                sends.append(pltpu.make_async_remote_copy(
                    src_ref=x_refs[a].at[jp], dst_ref=out_refs[a].at[jm],
                    send_sem=send_sems.at[k], recv_sem=recv_sems.at[k],
                    device_id=(px, py, c_), device_id_type=pl.DeviceIdType.MESH))
                recvs.append(pltpu.make_async_remote_copy(
                    src_ref=x_refs[a].at[jp], dst_ref=out_refs[a].at[jp],
                    send_sem=send_sems.at[k], recv_sem=recv_sems.at[k],
                    device_id=(px, py, c_), device_id_type=pl.DeviceIdType.MESH))
        for cp in sends:
            cp.start()
        for cp in sends:
            cp.wait_send()
        for cp in recvs:
            cp.wait_recv()
        for cp in mine:
            cp.wait()

    outs = [jax.ShapeDtypeStruct(x.shape, x.dtype) for x in xs]
    return _hbm_call(body, xs, outs, 3 * n, name)


def _pair_sum(x8, r4, name):
    _, r, c = x8.shape
    tr = _row_tile(r, c * 12)

    def body(x_ref, r_ref, o_ref):
        o_ref[...] = (x_ref[...].astype(F32) + r_ref[...].astype(F32)).astype(o_ref.dtype)

    return pl.pallas_call(
        body, name=name, grid=(4, r // tr),
        in_specs=[pl.BlockSpec((None, tr, c), lambda j, i: (2 * j + lax.axis_index("c"), i, 0)),
                  pl.BlockSpec((None, tr, c), lambda j, i: (j, i, 0))],
        out_specs=pl.BlockSpec((None, tr, c), lambda j, i: (j, i, 0)),
        out_shape=jax.ShapeDtypeStruct((4, r, c), x8.dtype),
        compiler_params=_cparams(("parallel", "parallel"), VMEM_LIMIT))(x8, r4)


def _row_tile(r, bytes_per_row, budget=4 * 1024 * 1024):
    best = None
    for t in range(16, r + 1, 16):
        if r % t == 0 and t * bytes_per_row <= budget:
            best = t
    return best if best is not None else r


def _mm(a, b, dims, name, out_dtype=F32, add=None, out_shards=1, tm=1024, tn=1024, tk=1024):
    S = b.shape[0] if b.ndim == 3 else 1
    bshape = (b.shape[1], S * b.shape[2]) if b.ndim == 3 else b.shape
    if dims == "nn":
        (M, K), (K2, N) = a.shape, bshape
    elif dims == "nt":
        (M, K), (N, K2) = a.shape, bshape
    else:
        (K, M), (K2, N) = a.shape, bshape
    assert K == K2, (a.shape, b.shape, dims)
    cs = bshape[1] // S
    tm, tk = _tile(M, tm), _tile(K, min(tk, cs) if (S > 1 and dims == "nt") else tk)
    tn = _tile(N, min(tn, cs) if (S > 1 and dims != "nt") else min(tn, N // out_shards))
    assert cs % (tk if dims == "nt" else tn) == 0 and (N // out_shards) % tn == 0
    nk = K // tk

    def body(*refs):
        if add is None:
            a_ref, b_ref, o_ref, acc = refs
        else:
            a_ref, b_ref, c_ref, o_ref, acc = refs
        k = pl.program_id(2)

        @pl.when(k == 0)
        def _():
            acc[...] = jnp.zeros_like(acc)

        acc[...] += _dot(a_ref[...], b_ref[...], dims)

        @pl.when(k == nk - 1)
        def _():
            r = acc[...]
            if add is not None:
                r = r + c_ref[...]
            o_ref[...] = r.astype(out_dtype)

    a_spec = (pl.BlockSpec((tk, tm), lambda i, j, k: (k, i)) if dims == "tn"
              else pl.BlockSpec((tm, tk), lambda i, j, k: (i, k)))
    if S == 1:
        b_spec = (pl.BlockSpec((tn, tk), lambda i, j, k: (j, k)) if dims == "nt"
                  else pl.BlockSpec((tk, tn), lambda i, j, k: (k, j)))
    elif dims == "nt":
        per = cs // tk
        b_spec = pl.BlockSpec((None, tn, tk), lambda i, j, k: (k // per, j, k % per))
    else:
        per = cs // tn
        b_spec = pl.BlockSpec((None, tk, tn), lambda i, j, k: (j // per, k, j % per))
    in_specs = [a_spec, b_spec]
    args = [a, b]
    if add is not None:
        in_specs.append(pl.BlockSpec((tm, tn), lambda i, j, k: (i, j)))
        args.append(add)
    if out_shards == 1:
        out_spec, out_shape = pl.BlockSpec((tm, tn), lambda i, j, k: (i, j)), (M, N)
    else:
        oper = N // out_shards // tn
        out_spec = pl.BlockSpec((None, tm, tn), lambda i, j, k: (j // oper, i, j % oper))
        out_shape = (out_shards, M, N // out_shards)
    return pl.pallas_call(
        body, name=name, grid=(M // tm, N // tn, nk),
        in_specs=in_specs, out_specs=out_spec,
        out_shape=jax.ShapeDtypeStruct(out_shape, out_dtype),
        scratch_shapes=[pltpu.VMEM((tm, tn), F32)],
        compiler_params=_cparams(("parallel", "parallel", "arbitrary"), VMEM_LIMIT),
    )(*args)


def _ada_fwd(ccp, w, b):
    def body(c_ref, w_ref, b_ref, o_ref):
        o_ref[...] = _dot(_silu(c_ref[...]), w_ref[...]) + b_ref[...]

    return pl.pallas_call(
        body, name="ada_fwd", out_shape=jax.ShapeDtypeStruct((ccp.shape[0], w.shape[1]), F32),
        compiler_params=_cparams(None, VMEM_LIMIT))(ccp, w, b)


def _ada_bwd(ccp, w, g, cctx, n_loc):
    def body(c_ref, w_ref, g_ref, cc_ref, gw_ref, dc_ref):
        gw_ref[...] = _dot(_silu(c_ref[...]), g_ref[...], "tn")
        dcc = _dot(g_ref[...], w_ref[...], "nt")
        row = lax.broadcasted_iota(jnp.int32, dcc.shape, 0)
        part = jnp.sum(jnp.where(row % SUBLANES == n_loc, dcc, 0.0), axis=0, keepdims=True)
        dc_ref[...] = jnp.broadcast_to(part * _dsilu(cc_ref[...]), dc_ref.shape)

    D = ccp.shape[1]
    return pl.pallas_call(
        body, name="ada_bwd",
        out_shape=(jax.ShapeDtypeStruct(w.shape, F32), jax.ShapeDtypeStruct((SUBLANES, D), F32)),
        compiler_params=_cparams(None, VMEM_LIMIT))(ccp, w, g, cctx)


def _norm_mod_fwd(h, gain, mods, i_shift, i_scale, nct, tm, name):
    B, T, D = h.shape

    def body(h_ref, g_ref, m_ref, u_ref):
        x = h_ref[0]
        r = lax.rsqrt(jnp.mean(x * x, axis=-1, keepdims=True) + EPS)
        n = x * r * g_ref[...]
        u_ref[0] = (n * (1.0 + m_ref[0, i_scale:i_scale + 1, :]) + m_ref[0, i_shift:i_shift + 1, :]).astype(BF16)

    return pl.pallas_call(
        body, name=name, grid=(B, T // tm),
        in_specs=[pl.BlockSpec((1, tm, D), lambda b, t: (b, t, 0)),
                  pl.BlockSpec((1, D), lambda b, t: (0, 0)),
                  pl.BlockSpec((1, SUBLANES, D), lambda b, t: (jnp.where(t < nct, B, b), 0, 0))],
        out_specs=pl.BlockSpec((1, tm, D), lambda b, t: (b, t, 0)),
        out_shape=jax.ShapeDtypeStruct((B, T, D), BF16),
        compiler_params=_cparams(("parallel", "parallel"), VMEM_LIMIT),
    )(h, gain, mods)


def _norm_mod_bwd(h, du, gain, mods, i_scale, nct, tm, res, name):
    B, T, D = h.shape
    nt = T // tm

    def body(h_ref, du_ref, g_ref, m_ref, res_ref, dx_ref, ax_ref, ac_ref):
        t = pl.program_id(1)
        x = h_ref[0]
        r = lax.rsqrt(jnp.mean(x * x, axis=-1, keepdims=True) + EPS)
        nh = x * r
        g = g_ref[...]
        du_ = du_ref[0]
        dn = du_ * (1.0 + m_ref[0, i_scale:i_scale + 1, :])
        dnh = dn * g
        dx = r * (dnh - nh * jnp.mean(dnh * nh, axis=-1, keepdims=True))
        sums = (jnp.sum(du_, axis=0, keepdims=True), jnp.sum(du_ * nh * g, axis=0, keepdims=True),
                jnp.sum(dn * nh, axis=0, keepdims=True))

        @pl.when(t == 0)
        def _():
            ax_ref[...] = jnp.zeros_like(ax_ref)
            ac_ref[...] = jnp.zeros_like(ac_ref)

        def accumulate(ref):
            for i, s in enumerate(sums):
                ref[0, i:i + 1, :] += s

        @pl.when(t < nct)
        def _():
            accumulate(ac_ref)

        @pl.when(t >= nct)
        def _():
            accumulate(ax_ref)
            dx_ref[0] = dx + res_ref[0]

    lat = lambda b, t: (b, jnp.maximum(t - nct, 0), 0)
    acc = pl.BlockSpec((1, SUBLANES, D), lambda b, t: (b, 0, 0))
    return pl.pallas_call(
        body, name=name, grid=(B, nt),
        in_specs=[pl.BlockSpec((1, tm, D), lambda b, t: (b, t, 0)),
                  pl.BlockSpec((1, tm, D), lambda b, t: (b, t, 0)),
                  pl.BlockSpec((1, D), lambda b, t: (0, 0)),
                  pl.BlockSpec((1, SUBLANES, D), lambda b, t: (jnp.where(t < nct, B, b), 0, 0)),
                  pl.BlockSpec((1, tm, D), lat)],
        out_specs=(pl.BlockSpec((1, tm, D), lat), acc, acc),
        out_shape=(jax.ShapeDtypeStruct(res.shape, F32),
                   jax.ShapeDtypeStruct((B, SUBLANES, D), F32), jax.ShapeDtypeStruct((B, SUBLANES, D), F32)),
        compiler_params=_cparams(("parallel", "arbitrary"), VMEM_LIMIT),
    )(h, du, gain, mods, res)


def _resid_norm_fwd(x, y, gain, mods, i_gate, tm, name):
    B, N, D = x.shape

    def body(x_ref, y_ref, g_ref, m_ref, o_ref):
        y_ = y_ref[0]
        r = lax.rsqrt(jnp.mean(y_ * y_, axis=-1, keepdims=True) + EPS)
        o_ref[0] = x_ref[0] + y_ * r * g_ref[...] * m_ref[0, i_gate:i_gate + 1, :]

    row = pl.BlockSpec((1, tm, D), lambda b, t: (b, t, 0))
    return pl.pallas_call(
        body, name=name, grid=(B, N // tm),
        in_specs=[row, row, pl.BlockSpec((1, D), lambda b, t: (0, 0)),
                  pl.BlockSpec((1, SUBLANES, D), lambda b, t: (b, 0, 0))],
        out_specs=row, out_shape=jax.ShapeDtypeStruct((B, N, D), F32),
        compiler_params=_cparams(("parallel", "parallel"), VMEM_LIMIT),
    )(x, y, gain, mods)


def _resid_norm_bwd_math(y_, dh, g, gate):
    r = lax.rsqrt(jnp.mean(y_ * y_, axis=-1, keepdims=True) + EPS)
    nh = y_ * r
    dgate = jnp.sum(dh * nh * g, axis=0, keepdims=True)
    dn = dh * gate
    dgain = jnp.sum(dn * nh, axis=0, keepdims=True)
    dnh = dn * g
    dy = r * (dnh - nh * jnp.mean(dnh * nh, axis=-1, keepdims=True))
    return dy, dgate, dgain


def _resid_norm_bwd(y, dh, gain, mods, i_gate, tm, name):
    B, N, D = y.shape

    def body(y_ref, dh_ref, g_ref, m_ref, dy_ref, acc_ref):
        t = pl.program_id(1)
        dy, dgate, dgain = _resid_norm_bwd_math(y_ref[0], dh_ref[0], g_ref[...], m_ref[0, i_gate:i_gate + 1, :])
        dy_ref[0] = dy.astype(BF16)

        @pl.when(t == 0)
        def _():
            acc_ref[...] = jnp.zeros_like(acc_ref)

        acc_ref[0, 0:1, :] += dgate
        acc_ref[0, 1:2, :] += dgain

    row = pl.BlockSpec((1, tm, D), lambda b, t: (b, t, 0))
    return pl.pallas_call(
        body, name=name, grid=(B, N // tm),
        in_specs=[row, row, pl.BlockSpec((1, D), lambda b, t: (0, 0)),
                  pl.BlockSpec((1, SUBLANES, D), lambda b, t: (b, 0, 0))],
        out_specs=(row, pl.BlockSpec((1, SUBLANES, D), lambda b, t: (b, 0, 0))),
        out_shape=(jax.ShapeDtypeStruct((B, N, D), BF16), jax.ShapeDtypeStruct((B, SUBLANES, D), F32)),
        compiler_params=_cparams(("parallel", "arbitrary"), VMEM_LIMIT),
    )(y, dh, gain, mods)


def _final_fwd_bwd(h1, d, gain, mods, i_gate, tgt, tm):
    B, N, D = h1.shape

    def body(h_ref, d_ref, g_ref, m_ref, t_ref, dd_ref, dh_ref, acc_ref):
        t = pl.program_id(1)
        d_ = d_ref[0]
        g = g_ref[...]
        gate = m_ref[0, i_gate:i_gate + 1, :]
        r = lax.rsqrt(jnp.mean(d_ * d_, axis=-1, keepdims=True) + EPS)
        e = h_ref[0] + d_ * r * g * gate - t_ref[0]
        dh = e * (1.0 / D)
        dh_ref[0] = dh
        dy, dgate, dgain = _resid_norm_bwd_math(d_, dh, g, gate)
        dd_ref[0] = dy.astype(BF16)

        @pl.when(t == 0)
        def _():
            acc_ref[...] = jnp.zeros_like(acc_ref)

        acc_ref[0, 0:1, :] += dgate
        acc_ref[0, 1:2, :] += dgain
        acc_ref[0, 2:3, :] += jnp.sum(e * e, axis=0, keepdims=True)

    row = pl.BlockSpec((1, tm, D), lambda b, t: (b, t, 0))
    return pl.pallas_call(
        body, name="final_fwd_bwd", grid=(B, N // tm),
        in_specs=[row, row, pl.BlockSpec((1, D), lambda b, t: (0, 0)),
                  pl.BlockSpec((1, SUBLANES, D), lambda b, t: (b, 0, 0)), row],
        out_specs=(row, row, pl.BlockSpec((1, SUBLANES, D), lambda b, t: (b, 0, 0))),
        out_shape=(jax.ShapeDtypeStruct((B, N, D), BF16), jax.ShapeDtypeStruct((B, N, D), F32),
                   jax.ShapeDtypeStruct((B, SUBLANES, D), F32)),
        compiler_params=_cparams(("parallel", "arbitrary"), VMEM_LIMIT),
    )(h1, d, gain, mods, tgt)


def _shifted(x, prev, nxt, off, row, tm):
    if off == 0:
        return x
    if off < 0:
        y = pltpu.roll(x, -off, 0)
        for r in range(-off):
            y = jnp.where(row == r, prev[SUBLANES + r + off:SUBLANES + r + off + 1, :], y)
        return y
    y = pltpu.roll(x, tm - off, 0)
    for r in range(off):
        y = jnp.where(row == tm - off + r, nxt[r:r + 1, :], y)
    return y


def _halo_specs(T, tm, tc, cb0, order):
    r8, n8 = tm // SUBLANES, T // SUBLANES
    if order == "btj":
        prev = lambda b, t, j: (b, jnp.maximum(t * r8 - 1, 0), cb0 + j)
        nxt = lambda b, t, j: (b, jnp.minimum((t + 1) * r8, n8 - 1), cb0 + j)
    else:
        prev = lambda b, j, t: (b, jnp.maximum(t * r8 - 1, 0), cb0 + j)
        nxt = lambda b, j, t: (b, jnp.minimum((t + 1) * r8, n8 - 1), cb0 + j)
    return pl.BlockSpec((1, SUBLANES, tc), prev), pl.BlockSpec((1, SUBLANES, tc), nxt)


def _seg_halos(p_ref, n_ref, t, nct, nt):
    seg_start = jnp.logical_or(t == 0, t == nct)
    seg_end = jnp.logical_or(t == nct - 1, t == nt - 1)
    prev = jnp.where(seg_start, 0.0, p_ref[0])
    nxt = jnp.where(seg_end, 0.0, n_ref[0])
    return prev, nxt


def _dwconv(x, col0, C, w, bias, offs, nct, tm, tc, name, out_dtype=F32):
    B, T, _ = x.shape
    nt = T // tm
    cb0 = col0 // tc

    def body(*refs):
        if bias is None:
            x_ref, p_ref, n_ref, w_ref, o_ref = refs
        else:
            x_ref, p_ref, n_ref, w_ref, b_ref, o_ref = refs
        t = pl.program_id(1)
        prev, nxt = _seg_halos(p_ref, n_ref, t, nct, nt)
        x_ = x_ref[0]
        row = lax.broadcasted_iota(jnp.int32, x_.shape, 0)
        acc = None
        for j, off in enumerate(offs):
            term = _shifted(x_, prev, nxt, off, row, tm) * w_ref[j:j + 1, :]
            acc = term if acc is None else acc + term
        if bias is not None:
            acc = acc + b_ref[...]
        o_ref[0] = acc.astype(out_dtype)

    pspec, nspec = _halo_specs(T, tm, tc, cb0, "btj")
    in_specs = [pl.BlockSpec((1, tm, tc), lambda b, t, j: (b, t, cb0 + j)), pspec, nspec,
                pl.BlockSpec((w.shape[0], tc), lambda b, t, j: (0, j))]
    args = [x, x, x, w]
    if bias is not None:
        in_specs.append(pl.BlockSpec((1, tc), lambda b, t, j: (0, j)))
        args.append(bias)
    return pl.pallas_call(
        body, name=name, grid=(B, nt, C // tc),
        in_specs=in_specs, out_specs=pl.BlockSpec((1, tm, tc), lambda b, t, j: (b, t, j)),
        out_shape=jax.ShapeDtypeStruct((B, T, C), out_dtype),
        compiler_params=_cparams(("parallel", "parallel", "parallel"), VMEM_LIMIT),
    )(*args)


def _dwconv_wgrad(dy, x, col0, C, offs, nct, tm, tc, name):
    B, T, _ = x.shape
    nt = T // tm
    cb0 = col0 // tc
    K = len(offs)

    def body(dy_ref, x_ref, p_ref, n_ref, acc_ref):
        t = pl.program_id(2)
        prev, nxt = _seg_halos(p_ref, n_ref, t, nct, nt)
        x_ = x_ref[0]
        dy_ = dy_ref[0]
        row = lax.broadcasted_iota(jnp.int32, x_.shape, 0)

        @pl.when(t == 0)
        def _():
            acc_ref[...] = jnp.zeros_like(acc_ref)

        for j, off in enumerate(offs):
            acc_ref[0, j:j + 1, :] += jnp.sum(dy_ * _shifted(x_, prev, nxt, off, row, tm), axis=0, keepdims=True)
        acc_ref[0, K:K + 1, :] += jnp.sum(dy_, axis=0, keepdims=True)

    pspec, nspec = _halo_specs(T, tm, tc, cb0, "bjt")
    return pl.pallas_call(
        body, name=name, grid=(B, C // tc, nt),
        in_specs=[pl.BlockSpec((1, tm, tc), lambda b, j, t: (b, t, j)),
                  pl.BlockSpec((1, tm, tc), lambda b, j, t: (b, t, cb0 + j)), pspec, nspec],
        out_specs=pl.BlockSpec((1, SUBLANES, tc), lambda b, j, t: (b, 0, j)),
        out_shape=jax.ShapeDtypeStruct((B, SUBLANES, C), F32),
        compiler_params=_cparams(("parallel", "parallel", "arbitrary"), VMEM_LIMIT),
    )(dy, x, x, x)


def _ab_act(pab, alog, dtb, nd, tm):
    R = pab.shape[0]

    def body(p_ref, al_ref, dt_ref, o_ref):
        p = p_ref[...]
        lane = lax.broadcasted_iota(jnp.int32, p.shape, 1)
        g = -jnp.exp(al_ref[...]) * _softplus(p + dt_ref[...])
        o_ref[...] = jnp.where(lane < nd, g, jnp.where(lane < 2 * nd, _sigmoid(p), 0.0))

    row = pl.BlockSpec((tm, LANES), lambda i: (i, 0))
    vec = pl.BlockSpec((1, LANES), lambda i: (0, 0))
    return pl.pallas_call(
        body, name="ab_act", grid=(R // tm,), in_specs=[row, vec, vec], out_specs=row,
        out_shape=jax.ShapeDtypeStruct((R, LANES), F32),
        compiler_params=_cparams(("parallel",), VMEM_LIMIT))(pab, alog, dtb)


def _ab_act_bwd(pab, gb, dgb, alog, dtb, nd, tm):
    R = pab.shape[0]

    def body(p_ref, gb_ref, d_ref, al_ref, dt_ref, o_ref, acc_ref):
        i = pl.program_id(0)
        p = p_ref[...]
        gb_ = gb_ref[...]
        d_ = d_ref[...]
        lane = lax.broadcasted_iota(jnp.int32, p.shape, 1)
        is_g = lane < nd
        is_b = jnp.logical_and(lane >= nd, lane < 2 * nd)
        da = jnp.where(is_g, d_ * (-jnp.exp(al_ref[...])) * _sigmoid(p + dt_ref[...]), 0.0)
        db = jnp.where(is_b, d_ * gb_ * (1.0 - gb_), 0.0)
        o_ref[...] = (da + db).astype(BF16)

        @pl.when(i == 0)
        def _():
            acc_ref[...] = jnp.zeros_like(acc_ref)

        acc_ref[0:1, :] += jnp.sum(jnp.where(is_g, d_ * gb_, 0.0), axis=0, keepdims=True)
        acc_ref[1:2, :] += jnp.sum(da, axis=0, keepdims=True)

    row = pl.BlockSpec((tm, LANES), lambda i: (i, 0))
    vec = pl.BlockSpec((1, LANES), lambda i: (0, 0))
    return pl.pallas_call(
        body, name="ab_act_bwd", grid=(R // tm,), in_specs=[row, row, row, vec, vec],
        out_specs=(row, pl.BlockSpec((SUBLANES, LANES), lambda i: (0, 0))),
        out_shape=(jax.ShapeDtypeStruct((R, LANES), BF16), jax.ShapeDtypeStruct((SUBLANES, LANES), F32)),
        compiler_params=_cparams(("arbitrary",), VMEM_LIMIT))(pab, gb, dgb, alog, dtb)


def _dn_act(cv, H, hd, tm):
    B, T, C3 = cv.shape
    qscale = float(hd) ** -0.5

    def body(c_ref, o_ref):
        part = pl.program_id(0)
        s = _silu(c_ref[0])
        rs = lax.rsqrt(jnp.sum(s * s, axis=-1, keepdims=True) + EPS)
        scale = jnp.where(part == 0, rs * qscale, jnp.where(part == 1, rs, 1.0))
        o_ref[0] = s * scale

    spec = pl.BlockSpec((1, tm, hd), lambda p, b, t, h: (b, t, p * H + h))
    return pl.pallas_call(
        body, name="dn_act", grid=(3, B, T // tm, H), in_specs=[spec], out_specs=spec,
        out_shape=jax.ShapeDtypeStruct((B, T, C3), F32),
        compiler_params=_cparams(("parallel",) * 4, VMEM_LIMIT))(cv)


def _dn_act_bwd(cv, dqkv, H, hd, tm):
    B, T, C3 = cv.shape
    qscale = float(hd) ** -0.5

    def body(c_ref, d0_ref, d1_ref, o_ref):
        part = pl.program_id(0)
        c = c_ref[0]
        s = _silu(c)
        dout = d0_ref[0, 0] + d1_ref[0, 0]
        rs = lax.rsqrt(jnp.sum(s * s, axis=-1, keepdims=True) + EPS)
        sh = s * rs
        dnorm = rs * (dout - sh * jnp.sum(dout * sh, axis=-1, keepdims=True))
        ds = jnp.where(part == 0, dnorm * qscale, jnp.where(part == 1, dnorm, dout))
        o_ref[0] = ds * _dsilu(c)

    spec = pl.BlockSpec((1, tm, hd), lambda p, b, t, h: (b, t, p * H + h))
    d0 = pl.BlockSpec((1, 1, tm, hd), lambda p, b, t, h: (0, b, t, p * H + h))
    d1 = pl.BlockSpec((1, 1, tm, hd), lambda p, b, t, h: (1, b, t, p * H + h))
    return pl.pallas_call(
        body, name="dn_act_bwd", grid=(3, B, T // tm, H), in_specs=[spec, d0, d1], out_specs=spec,
        out_shape=jax.ShapeDtypeStruct((B, T, C3), F32),
        compiler_params=_cparams(("parallel",) * 4, VMEM_LIMIT))(cv, dqkv, dqkv)


def _chunk_of(d, s, ncc, nch):
    return jnp.where(d == 0, s, jnp.where(s < ncc, ncc - 1 - s, nch - 1 - (s - ncc)))


def _delta_masks(d):
    row = lax.broadcasted_iota(jnp.int32, (CHUNK, CHUNK), 0)
    col = lax.broadcasted_iota(jnp.int32, (CHUNK, CHUNK), 1)
    sign = 1 - 2 * d
    diff = (row - col) * sign
    return diff >= 0, diff > 0


def _each(f, *lists):
    return [f(*a) for a in zip(*lists)]


def _unit_tri_inverse(As):
    C = As[0].shape[0]
    row = lax.broadcasted_iota(jnp.int32, (C, C), 0)
    col = lax.broadcasted_iota(jnp.int32, (C, C), 1)
    eye = jnp.where(row == col, 1.0, 0.0).astype(F32)
    ps = [-a for a in As]
    ts = [eye + p for p in ps]
    n = 2
    while n < C:
        ps = _each(lambda p: _dot_hi(p, p), ps)
        ts = _each(lambda t, p: t + _dot_hi(t, p), ts, ps)
        n *= 2
    return ts


def _delta_chunk_fwd(q, k, v, g_i, g_j, beta_i, gl, S, incl, strict):
    D_ = _each(lambda gi, gj: jnp.where(incl, jnp.exp(jnp.where(incl, gi - gj, 0.0)), 0.0), g_i, g_j)
    kb = _each(lambda k_, b: k_ * b, k, beta_i)
    A = _each(lambda kb_, k_, d: jnp.where(strict, _dot(kb_, k_, "nt") * d, 0.0), kb, k, D_)
    P = _each(lambda q_, k_, d: jnp.where(incl, _dot(q_, k_, "nt") * d, 0.0), q, k, D_)
    Tm = _unit_tri_inverse(A)
    gam = _each(jnp.exp, g_i)
    w = _each(lambda t, kb_, g: _dot(t, kb_ * g), Tm, kb, gam)
    u = _each(lambda t, v_, b: _dot(t, v_ * b), Tm, v, beta_i)
    vn = _each(lambda u_, w_, s: u_ - _dot(w_, s), u, w, S)
    qg = _each(lambda q_, g: q_ * g, q, gam)
    o = _each(lambda qg_, s, p, vn_: _dot(qg_, s) + _dot(p, vn_), qg, S, P, vn)
    e_i = _each(lambda gl_, gi: jnp.exp(gl_ - gi), gl, g_i)
    kd = _each(lambda k_, e: k_ * e, k, e_i)
    Gam = _each(jnp.exp, gl)
    S_new = _each(lambda s, g, kd_, vn_: s * g + _dot(kd_, vn_, "tn"), S, Gam, kd, vn)
    return dict(D=D_, kb=kb, A=A, Tm=Tm, gam=gam, w=w, u=u, vn=vn, P=P, qg=qg, o=o, e=e_i, kd=kd, Gam=Gam, S_new=S_new)


def _delta_gb(gbc_ref, gbr_ref, incl, H):
    gbc = gbc_ref[0, 0, 0]
    gbr = gbr_ref[0, 0, 0]
    inclf = incl.astype(BF16)
    gc_col = _dot_mask(inclf, gbc)
    gc_row = _dot_mask(inclf, gbr, "xnt")
    gl = jnp.sum(gbc, axis=0, keepdims=True)
    return gbc, gc_col, gc_row, gl


def _delta_fwd(qkvn, gbc, gbr, H, hd, ncc):
    B, T, _ = qkvn.shape
    nch = T // CHUNK
    HD = H * hd

    def body(q_ref, k_ref, v_ref, gbc_ref, gbr_ref, o_ref, sin_ref, S_ref):
        d = pl.program_id(0)
        s = pl.program_id(2)

        @pl.when(s == 0)
        def _():
            S_ref[...] = jnp.zeros_like(S_ref)

        incl, strict = _delta_masks(d)
        gbc_, gc_col, gc_row, gl = _delta_gb(gbc_ref, gbr_ref, incl, H)
        hs = range(H)
        S = [S_ref[h] for h in hs]
        r = _delta_chunk_fwd([q_ref[0, :, h * hd:(h + 1) * hd] for h in hs],
                             [k_ref[0, :, h * hd:(h + 1) * hd] for h in hs],
                             [v_ref[0, :, h * hd:(h + 1) * hd] for h in hs],
                             [gc_col[:, h:h + 1] for h in hs], [gc_row[h:h + 1, :] for h in hs],
                             [gbc_[:, H + h:H + h + 1] for h in hs], [gl[:, h:h + 1] for h in hs],
                             S, incl, strict)
        for h in hs:
            sin_ref[0, 0, 0, h] = S[h]
            S_ref[h] = r["S_new"][h]
            o_ref[0, 0, :, h * hd:(h + 1) * hd] = r["o"][h]

    cidx = lambda d, b, s: _chunk_of(d, s, ncc, nch)
    qspec = lambda part: pl.BlockSpec((1, CHUNK, HD), lambda d, b, s: (b, cidx(d, b, s), part))
    return pl.pallas_call(
        body, name="delta_fwd", grid=(2, B, nch),
        in_specs=[qspec(0), qspec(1), qspec(2),
                  pl.BlockSpec((1, 1, 1, CHUNK, 2 * H), lambda d, b, s: (b, d, cidx(d, b, s), 0, 0)),
                  pl.BlockSpec((1, 1, 1, 2 * H, CHUNK), lambda d, b, s: (b, d, cidx(d, b, s), 0, 0))],
        out_specs=(pl.BlockSpec((1, 1, CHUNK, HD), lambda d, b, s: (d, b, cidx(d, b, s), 0)),
                   pl.BlockSpec((1, 1, 1, H, hd, hd), lambda d, b, s: (d, b, cidx(d, b, s), 0, 0, 0))),
        out_shape=(jax.ShapeDtypeStruct((2, B, T, HD), F32),
                   jax.ShapeDtypeStruct((2, B, nch, H, hd, hd), F32)),
        scratch_shapes=[pltpu.VMEM((H, hd, hd), F32)],
        compiler_params=_cparams(("parallel", "parallel", "arbitrary"), VMEM_LIMIT),
    )(qkvn, qkvn, qkvn, gbc, gbr)


def _delta_bwd(qkvn, gbc, gbr, sin, do, H, hd, ncc):
    B, T, _ = qkvn.shape
    nch = T // CHUNK
    HD = H * hd

    def body(q_ref, k_ref, v_ref, gbc_ref, gbr_ref, sin_ref, do_ref, dqkv_ref, dgb_ref, dS_ref):
        d = pl.program_id(0)
        s = pl.program_id(2)

        @pl.when(s == 0)
        def _():
            dS_ref[...] = jnp.zeros_like(dS_ref)

        incl, strict = _delta_masks(d)
        gbc_, gc_col, gc_row, gl = _delta_gb(gbc_ref, gbr_ref, incl, H)
        lane = lax.broadcasted_iota(jnp.int32, (1, 2 * H), 1)
        ones = jnp.ones((CHUNK, LANES), BF16)
        dgc = jnp.zeros((CHUNK, 2 * H), F32)
        dgl = jnp.zeros((1, 2 * H), F32)
        hs = range(H)
        q = [q_ref[0, :, h * hd:(h + 1) * hd] for h in hs]
        k = [k_ref[0, :, h * hd:(h + 1) * hd] for h in hs]
        v = [v_ref[0, :, h * hd:(h + 1) * hd] for h in hs]
        do_ = [do_ref[0, :, h * hd:(h + 1) * hd] for h in hs]
        beta_i = [gbc_[:, H + h:H + h + 1] for h in hs]
        S = [sin_ref[0, 0, 0, h] for h in hs]
        dSn = [dS_ref[h] for h in hs]
        f = _delta_chunk_fwd(q, k, v, [gc_col[:, h:h + 1] for h in hs], [gc_row[h:h + 1, :] for h in hs], beta_i,
                             [gl[:, h:h + 1] for h in hs], S, incl, strict)
        rsum = lambda x: jnp.sum(x, axis=1, keepdims=True)
        dvn = _each(lambda p, d_, kd, ds: _dot(p, d_, "tn") + _dot(kd, ds), f["P"], do_, f["kd"], dSn)
        dP = _each(lambda d_, vn: jnp.where(incl, _dot(d_, vn, "nt"), 0.0), do_, f["vn"])
        dqg = _each(lambda d_, s: _dot(d_, s, "nt"), do_, S)
        dS_in = _each(lambda qg, d_, g, ds, w, dv_: _dot(qg, d_, "tn") + g * ds - _dot(w, dv_, "tn"),
                      f["qg"], do_, f["Gam"], dSn, f["w"], dvn)
        dGam = _each(lambda s, ds: jnp.sum(rsum(s * ds), axis=0, keepdims=True), S, dSn)
        dkd = _each(lambda vn, ds: _dot(vn, ds, "nt"), f["vn"], dSn)
        de = _each(lambda dkd_, k_, e: rsum(dkd_ * k_) * e, dkd, k, f["e"])
        dw = _each(lambda dv_, s: -_dot(dv_, s, "nt"), dvn, S)
        dXw = _each(lambda t, dw_: _dot(t, dw_, "tn"), f["Tm"], dw)
        dXu = _each(lambda t, dv_: _dot(t, dv_, "tn"), f["Tm"], dvn)
        dA = _each(lambda xw, w, xu, u: -jnp.where(strict, _dot(xw, w, "nt") + _dot(xu, u, "nt"), 0.0),
                   dXw, f["w"], dXu, f["u"])
        dM = _each(lambda a, d_: a * d_, dA, f["D"])
        dN = _each(lambda p, d_: p * d_, dP, f["D"])
        dkb = _each(lambda m, k_, xw, g: _dot(m, k_) + xw * g, dM, k, dXw, f["gam"])
        dq = _each(lambda dqg_, g, n, k_: dqg_ * g + _dot(n, k_), dqg, f["gam"], dN, k)
        dk = _each(lambda dkd_, e, m, kb, n, q_, dkb_, b: dkd_ * e + _dot(m, kb, "tn") + _dot(n, q_, "tn") + dkb_ * b,
                   dkd, f["e"], dM, f["kb"], dN, q, dkb, beta_i)
        dv = _each(lambda xu, b: xu * b, dXu, beta_i)
        dbeta = _each(lambda dkb_, k_, xu, v_: rsum(dkb_ * k_) + rsum(xu * v_), dkb, k, dXu, v)
        E = _each(lambda a, A_, p, P_: a * A_ + p * P_, dA, f["A"], dP, f["P"])

        def colsum(e):
            e1, e2, e3 = _split3(e)
            return (_dot(e1, ones, "tn") + (_dot(e2, ones, "tn") + _dot(e3, ones, "tn")))[:, 0:1]

        dg = _each(lambda e, dqg_, qg, xw, kb, g, de_: rsum(e) - colsum(e) + rsum(dqg_ * qg) + rsum(xw * kb) * g - de_,
                   E, dqg, f["qg"], dXw, f["kb"], f["gam"], de)
        dgl_h = _each(lambda de_, dg_, g: jnp.sum(de_, axis=0, keepdims=True) + dg_ * g, de, dGam, f["Gam"])
        for h in hs:
            dgc = dgc + dg[h] * (lane == h).astype(F32) + dbeta[h] * (lane == H + h).astype(F32)
            dgl = dgl + dgl_h[h] * (lane == h).astype(F32)
            dS_ref[h] = dS_in[h]
            dqkv_ref[0, 0, :, h * hd:(h + 1) * hd] = dq[h]
            dqkv_ref[0, 0, :, HD + h * hd:HD + (h + 1) * hd] = dk[h]
            dqkv_ref[0, 0, :, 2 * HD + h * hd:2 * HD + (h + 1) * hd] = dv[h]
        draw = _dot_mask(incl.astype(BF16), dgc, "tn") + dgl
        dgb_ref[0, 0, 0] = jnp.where(lane < H, draw, dgc)

    cidx = lambda d, b, s: _chunk_of(d, nch - 1 - s, ncc, nch)
    qspec = lambda part: pl.BlockSpec((1, CHUNK, HD), lambda d, b, s: (b, cidx(d, b, s), part))
    return pl.pallas_call(
        body, name="delta_bwd", grid=(2, B, nch),
        in_specs=[qspec(0), qspec(1), qspec(2),
                  pl.BlockSpec((1, 1, 1, CHUNK, 2 * H), lambda d, b, s: (b, d, cidx(d, b, s), 0, 0)),
                  pl.BlockSpec((1, 1, 1, 2 * H, CHUNK), lambda d, b, s: (b, d, cidx(d, b, s), 0, 0)),
                  pl.BlockSpec((1, 1, 1, H, hd, hd), lambda d, b, s: (d, b, cidx(d, b, s), 0, 0, 0)),
                  pl.BlockSpec((1, CHUNK, HD), lambda d, b, s: (b, cidx(d, b, s), 0))],
        out_specs=(pl.BlockSpec((1, 1, CHUNK, 3 * HD), lambda d, b, s: (d, b, cidx(d, b, s), 0)),
                   pl.BlockSpec((1, 1, 1, CHUNK, 2 * H), lambda d, b, s: (b, d, cidx(d, b, s), 0, 0))),
        out_shape=(jax.ShapeDtypeStruct((2, B, T, 3 * HD), F32),
                   jax.ShapeDtypeStruct((B, 2, nch, CHUNK, 2 * H), F32)),
        scratch_shapes=[pltpu.VMEM((H, hd, hd), F32)],
        compiler_params=_cparams(("parallel", "parallel", "arbitrary"), VMEM_LIMIT),
    )(qkvn, qkvn, qkvn, gbc, gbr, sin, do)


def _dn_out(o2, pz, zcb0, onorm, H, hd, nct, tm):
    _, B, T, HD = o2.shape
    N = T - nct * tm

    def body(o0_ref, o1_ref, z_ref, g_ref, y_ref):
        o = o0_ref[0, 0] + o1_ref[0, 0]
        r = lax.rsqrt(jnp.mean(o * o, axis=-1, keepdims=True) + EPS)
        y_ref[0] = (o * r * g_ref[...] * _silu(z_ref[0])).astype(BF16)

    return pl.pallas_call(
        body, name="dn_out", grid=(B, N // tm, H),
        in_specs=[pl.BlockSpec((1, 1, tm, hd), lambda b, t, h: (0, b, t + nct, h)),
                  pl.BlockSpec((1, 1, tm, hd), lambda b, t, h: (1, b, t + nct, h)),
                  pl.BlockSpec((1, tm, hd), lambda b, t, h: (b, t + nct, zcb0 + h)),
                  pl.BlockSpec((1, hd), lambda b, t, h: (0, 0))],
        out_specs=pl.BlockSpec((1, tm, hd), lambda b, t, h: (b, t, h)),
        out_shape=jax.ShapeDtypeStruct((B, N, HD), BF16),
        compiler_params=_cparams(("parallel",) * 3, VMEM_LIMIT))(o2, o2, pz, onorm)


def _dn_out_bwd(o2, pz, zcb0, onorm, dy, H, hd, nct, tm):
    _, B, T, HD = o2.shape
    nt = T // tm

    def body(o0_ref, o1_ref, z_ref, g_ref, dy_ref, do_ref, dz_ref, acc_ref):
        t = pl.program_id(1)
        h = pl.program_id(2)

        @pl.when(jnp.logical_and(t == 0, h == 0))
        def _():
            acc_ref[...] = jnp.zeros_like(acc_ref)

        @pl.when(t < nct)
        def _():
            do_ref[0] = jnp.zeros_like(do_ref[0])
            dz_ref[0] = jnp.zeros_like(dz_ref[0])

        @pl.when(t >= nct)
        def _():
            o = o0_ref[0, 0] + o1_ref[0, 0]
            z = z_ref[0]
            g = g_ref[...]
            dy_ = dy_ref[0]
            r = lax.rsqrt(jnp.mean(o * o, axis=-1, keepdims=True) + EPS)
            oh = o * r
            dz_ref[0] = (dy_ * oh * g * _dsilu(z)).astype(BF16)
            dn = dy_ * _silu(z)
            acc_ref[0, 0:1, :] += jnp.sum(dn * oh, axis=0, keepdims=True)
            doh = dn * g
            do_ref[0] = r * (doh - oh * jnp.mean(doh * oh, axis=-1, keepdims=True))

    lat = lambda b, t, h: (b, jnp.maximum(t - nct, 0), h)
    return pl.pallas_call(
        body, name="dn_out_bwd", grid=(B, nt, H),
        in_specs=[pl.BlockSpec((1, 1, tm, hd), lambda b, t, h: (0, b, t, h)),
                  pl.BlockSpec((1, 1, tm, hd), lambda b, t, h: (1, b, t, h)),
                  pl.BlockSpec((1, tm, hd), lambda b, t, h: (b, t, zcb0 + h)),
                  pl.BlockSpec((1, hd), lambda b, t, h: (0, 0)),
                  pl.BlockSpec((1, tm, hd), lat)],
        out_specs=(pl.BlockSpec((1, tm, hd), lambda b, t, h: (b, t, h)),
                   pl.BlockSpec((1, tm, hd), lambda b, t, h: (b, t, h)),
                   pl.BlockSpec((1, SUBLANES, hd), lambda b, t, h: (b, 0, 0))),
        out_shape=(jax.ShapeDtypeStruct((B, T, HD), F32), jax.ShapeDtypeStruct((B, T, HD), BF16),
                   jax.ShapeDtypeStruct((B, SUBLANES, hd), F32)),
        compiler_params=_cparams(("parallel", "arbitrary", "arbitrary"), VMEM_LIMIT),
    )(o2, o2, pz, onorm, dy)


def _lru_gate_math(x, wr, wi, br, bi, lam):
    r = _sigmoid(_dot(x, wr) + br)
    i = _sigmoid(_dot(x, wi) + bi)
    sp = _softplus(-lam)
    la = -LRU_C * r * sp
    a = jnp.exp(la)
    s = jnp.sqrt(-jnp.tanh(la) * (a * a + 1.0))
    return r, i, sp, a, s


def _lru_gates(xc, wbd, bias4, lam, tm, bw):
    B, T, W = xc.shape

    def body(x_ref, w_ref, b_ref, l_ref, a_ref, i_ref):
        x = x_ref[0]
        for d in range(2):
            _, i, _, a, s = _lru_gate_math(x, w_ref[2 * d, 0], w_ref[2 * d + 1, 0],
                                           b_ref[2 * d:2 * d + 1, :], b_ref[2 * d + 1:2 * d + 2, :], l_ref[d:d + 1, :])
            a_ref[d, 0] = a
            i_ref[d, 0] = s * (i * x)

    out = pl.BlockSpec((2, 1, tm, bw), lambda b, t, j: (0, b, t, j))
    return pl.pallas_call(
        body, name="lru_gates", grid=(B, T // tm, W // bw),
        in_specs=[pl.BlockSpec((1, tm, bw), lambda b, t, j: (b, t, j)),
                  pl.BlockSpec((4, 1, bw, bw), lambda b, t, j: (0, j, 0, 0)),
                  pl.BlockSpec((4, bw), lambda b, t, j: (0, j)),
                  pl.BlockSpec((2, bw), lambda b, t, j: (0, j))],
        out_specs=(out, out),
        out_shape=(jax.ShapeDtypeStruct((2, B, T, W), F32), jax.ShapeDtypeStruct((2, B, T, W), F32)),
        compiler_params=_cparams(("parallel",) * 3, VMEM_LIMIT))(xc, wbd, bias4, lam)


def _lru_gates_bwd(xc, wbd, bias4, lam, dinp, da, tm, bw):
    B, T, W = xc.shape
    nt = T // tm

    def body(x_ref, w_ref, b_ref, l_ref, di_ref, da_ref, dx_ref, dw_ref, acc_ref):
        b_ = pl.program_id(1)
        t = pl.program_id(2)

        @pl.when(jnp.logical_and(b_ == 0, t == 0))
        def _():
            dw_ref[...] = jnp.zeros_like(dw_ref)
            acc_ref[...] = jnp.zeros_like(acc_ref)

        x = x_ref[0]
        dx = jnp.zeros_like(x)
        for d in range(2):
            wr, wi = w_ref[2 * d, 0], w_ref[2 * d + 1, 0]
            lam_ = l_ref[d:d + 1, :]
            r, i, sp, a, s = _lru_gate_math(x, wr, wi, b_ref[2 * d:2 * d + 1, :], b_ref[2 * d + 1:2 * d + 2, :], lam_)
            dinp_ = di_ref[d, 0]
            dix = dinp_ * s
            ds = dinp_ * i * x
            dla = da_ref[d, 0] * a - ds * (a * a) / s
            dpr = dla * (-LRU_C * sp) * r * (1.0 - r)
            dpi = dix * x * i * (1.0 - i)
            dx = dx + dix * i + _dot(dpr, wr, "nt") + _dot(dpi, wi, "nt")
            dw_ref[2 * d, 0] += _dot(x, dpr, "tn")
            dw_ref[2 * d + 1, 0] += _dot(x, dpi, "tn")
            acc_ref[2 * d:2 * d + 1, :] += jnp.sum(dpr, axis=0, keepdims=True)
            acc_ref[2 * d + 1:2 * d + 2, :] += jnp.sum(dpi, axis=0, keepdims=True)
            acc_ref[4 + d:5 + d, :] += jnp.sum(dla * (-LRU_C * r), axis=0, keepdims=True) * (-_sigmoid(-lam_))
        dx_ref[0] = dx

    dirs = pl.BlockSpec((2, 1, tm, bw), lambda j, b, t: (0, b, t, j))
    return pl.pallas_call(
        body, name="lru_gates_bwd", grid=(W // bw, B, nt),
        in_specs=[pl.BlockSpec((1, tm, bw), lambda j, b, t: (b, t, j)),
                  pl.BlockSpec((4, 1, bw, bw), lambda j, b, t: (0, j, 0, 0)),
                  pl.BlockSpec((4, bw), lambda j, b, t: (0, j)),
                  pl.BlockSpec((2, bw), lambda j, b, t: (0, j)), dirs, dirs],
        out_specs=(pl.BlockSpec((1, tm, bw), lambda j, b, t: (b, t, j)),
                   pl.BlockSpec((4, 1, bw, bw), lambda j, b, t: (0, j, 0, 0)),
                   pl.BlockSpec((SUBLANES, bw), lambda j, b, t: (0, j))),
        out_shape=(jax.ShapeDtypeStruct((B, T, W), F32), jax.ShapeDtypeStruct(wbd.shape, F32),
                   jax.ShapeDtypeStruct((SUBLANES, W), F32)),
        compiler_params=_cparams(("parallel", "arbitrary", "arbitrary"), VMEM_LIMIT),
    )(xc, wbd, bias4, lam, dinp, da)


def _tile_scan(a, x, rev, tm):
    row = lax.broadcasted_iota(jnp.int32, a.shape, 0)
    s = 1
    while s < tm:
        if rev:
            a_sh, x_sh, ok = pltpu.roll(a, tm - s, 0), pltpu.roll(x, tm - s, 0), row < tm - s
        else:
            a_sh, x_sh, ok = pltpu.roll(a, s, 0), pltpu.roll(x, s, 0), row >= s
        x = jnp.where(ok, x + a * x_sh, x)
        a = jnp.where(ok, a * a_sh, a)
        s *= 2
    return a, x


def _scan_tile_of(s, rev, nct, nt):
    if not rev:
        return s
    return jnp.where(s < nct, nct - 1 - s, nt - 1 - (s - nct))


def _lru_scan(a2, x2, d, nct, tm, tc):
    _, B, T, W = a2.shape
    nt = T // tm
    rev = d == 1
    last = 0 if rev else tm - 1

    def body(a_ref, x_ref, h_ref, carry):
        s = pl.program_id(2)

        @pl.when(s == 0)
        def _():
            carry[...] = jnp.zeros_like(carry)

        A, X = _tile_scan(a_ref[0, 0], x_ref[0, 0], rev, tm)
        h = X + A * carry[0:1, :]
        h_ref[0] = h
        carry[...] = jnp.broadcast_to(h[last:last + 1, :], carry.shape)

    tidx = lambda s: _scan_tile_of(s, rev, nct, nt)
    spec = pl.BlockSpec((1, 1, tm, tc), lambda b, j, s: (d, b, tidx(s), j))
    return pl.pallas_call(
        body, name=f"lru_scan{d}", grid=(B, W // tc, nt),
        in_specs=[spec, spec], out_specs=pl.BlockSpec((1, tm, tc), lambda b, j, s: (b, tidx(s), j)),
        out_shape=jax.ShapeDtypeStruct((B, T, W), F32),
        scratch_shapes=[pltpu.VMEM((SUBLANES, tc), F32)],
        compiler_params=_cparams(("parallel", "parallel", "arbitrary"), VMEM_LIMIT),
    )(a2, x2)


def _lru_scan_bwd(a2, h, dh, d, nct, tm, tc):
    _, B, T, W = a2.shape
    nt = T // tm
    rev = d == 1
    first = tm - 1 if rev else 0
    r8, n8 = tm // SUBLANES, T // SUBLANES

    def body(a_ref, h_ref, hp_ref, dh_ref, lam_ref, da_ref, ca, cl):
        s = pl.program_id(2)

        @pl.when(s == 0)
        def _():
            ca[...] = jnp.zeros_like(ca)
            cl[...] = jnp.zeros_like(cl)

        a = a_ref[0, 0]
        row = lax.broadcasted_iota(jnp.int32, a.shape, 0)
        if rev:
            c = jnp.where(row == 0, ca[0:1, :], pltpu.roll(a, 1, 0))
        else:
            c = jnp.where(row == tm - 1, ca[0:1, :], pltpu.roll(a, tm - 1, 0))
        C, L = _tile_scan(c, dh_ref[0], not rev, tm)
        lam = L + C * cl[0:1, :]
        lam_ref[0] = lam
        hp = jnp.where(s == nt - 1, 0.0, hp_ref[0])
        if rev:
            hprev = jnp.where(row == tm - 1, hp[0:1, :], pltpu.roll(h_ref[0], tm - 1, 0))
        else:
            hprev = jnp.where(row == 0, hp[SUBLANES - 1:SUBLANES, :], pltpu.roll(h_ref[0], 1, 0))
        da_ref[0] = lam * hprev
        ca[...] = jnp.broadcast_to(a[first:first + 1, :], ca.shape)
        cl[...] = jnp.broadcast_to(lam[first:first + 1, :], cl.shape)

    tidx = lambda s: _scan_tile_of(nt - 1 - s, rev, nct, nt)

    def hp_idx(b, j, s):
        tt = tidx(s)
        if rev:
            blk = jnp.where(tt == nt - 1, 0, jnp.minimum((tt + 1) * r8, n8 - 1))
        else:
            blk = jnp.maximum(tt * r8 - 1, 0)
        return (b, blk, j)

    tile = pl.BlockSpec((1, tm, tc), lambda b, j, s: (b, tidx(s), j))
    return pl.pallas_call(
        body, name=f"lru_scan_bwd{d}", grid=(B, W // tc, nt),
        in_specs=[pl.BlockSpec((1, 1, tm, tc), lambda b, j, s: (d, b, tidx(s), j)), tile,
                  pl.BlockSpec((1, SUBLANES, tc), hp_idx), tile],
        out_specs=(tile, tile),
        out_shape=(jax.ShapeDtypeStruct((B, T, W), F32), jax.ShapeDtypeStruct((B, T, W), F32)),
        scratch_shapes=[pltpu.VMEM((SUBLANES, tc), F32), pltpu.VMEM((SUBLANES, tc), F32)],
        compiler_params=_cparams(("parallel", "parallel", "arbitrary"), VMEM_LIMIT),
    )(a2, h, h, dh)


def _lru_out(h0, h1, pyl, ycb0, nct, tm, tc):
    B, T, W = h0.shape
    N = T - nct * tm

    def body(h0_ref, h1_ref, y_ref, o_ref):
        o_ref[0] = ((h0_ref[0] + h1_ref[0]) * _gelu(y_ref[0])).astype(BF16)

    hs = pl.BlockSpec((1, tm, tc), lambda b, t, j: (b, t + nct, j))
    return pl.pallas_call(
        body, name="lru_out", grid=(B, N // tm, W // tc),
        in_specs=[hs, hs, pl.BlockSpec((1, tm, tc), lambda b, t, j: (b, t + nct, ycb0 + j))],
        out_specs=pl.BlockSpec((1, tm, tc), lambda b, t, j: (b, t, j)),
        out_shape=jax.ShapeDtypeStruct((B, N, W), BF16),
        compiler_params=_cparams(("parallel",) * 3, VMEM_LIMIT))(h0, h1, pyl)


def _lru_out_bwd(h0, h1, pyl, ycb0, dy, nct, tm, tc):
    B, T, W = h0.shape

    def body(h0_ref, h1_ref, y_ref, dy_ref, dh_ref, dyl_ref):
        t = pl.program_id(1)

        @pl.when(t < nct)
        def _():
            dh_ref[0] = jnp.zeros_like(dh_ref[0])
            dyl_ref[0] = jnp.zeros_like(dyl_ref[0])

        @pl.when(t >= nct)
        def _():
            y = y_ref[0]
            dy_ = dy_ref[0]
            dh_ref[0] = dy_ * _gelu(y)
            dyl_ref[0] = (dy_ * (h0_ref[0] + h1_ref[0]) * _dgelu(y)).astype(BF16)

    hs = pl.BlockSpec((1, tm, tc), lambda b, t, j: (b, t, j))
    return pl.pallas_call(
        body, name="lru_out_bwd", grid=(B, T // tm, W // tc),
        in_specs=[hs, hs, pl.BlockSpec((1, tm, tc), lambda b, t, j: (b, t, ycb0 + j)),
                  pl.BlockSpec((1, tm, tc), lambda b, t, j: (b, jnp.maximum(t - nct, 0), j))],
        out_specs=(hs, hs),
        out_shape=(jax.ShapeDtypeStruct((B, T, W), F32), jax.ShapeDtypeStruct((B, T, W), BF16)),
        compiler_params=_cparams(("parallel",) * 3, VMEM_LIMIT))(h0, h1, pyl, dy)


def _merge(bd, bl, pmg, bm, nct, tm):
    B, N, D = bd.shape

    def body(bd_ref, bl_ref, m0_ref, m1_ref, b_ref, o_ref):
        g0 = _sigmoid(m0_ref[0] + b_ref[:, 0:D])
        g1 = _sigmoid(m1_ref[0] + b_ref[:, D:2 * D])
        o_ref[0] = (g0 * bd_ref[0] + g1 * bl_ref[0]).astype(BF16)

    row = pl.BlockSpec((1, tm, D), lambda b, t: (b, t, 0))
    return pl.pallas_call(
        body, name="merge", grid=(B, N // tm),
        in_specs=[row, row, pl.BlockSpec((1, tm, D), lambda b, t: (b, t + nct, 0)),
                  pl.BlockSpec((1, tm, D), lambda b, t: (b, t + nct, 1)),
                  pl.BlockSpec((1, 2 * D), lambda b, t: (0, 0))],
        out_specs=row, out_shape=jax.ShapeDtypeStruct((B, N, D), BF16),
        compiler_params=_cparams(("parallel", "parallel"), VMEM_LIMIT))(bd, bl, pmg, pmg, bm)


def _merge_bwd(dmi, bd, bl, pmg, bm, nct, tm):
    B, N, D = bd.shape
    T = pmg.shape[1]

    def body(dm_ref, bd_ref, bl_ref, m0_ref, m1_ref, b_ref, dbd_ref, dbl_ref, dmg_ref, acc_ref):
        t = pl.program_id(1)

        @pl.when(t == 0)
        def _():
            acc_ref[...] = jnp.zeros_like(acc_ref)

        @pl.when(t < nct)
        def _():
            dmg_ref[0] = jnp.zeros_like(dmg_ref[0])

        @pl.when(t >= nct)
        def _():
            dm = dm_ref[0]
            g0 = _sigmoid(m0_ref[0] + b_ref[:, 0:D])
            g1 = _sigmoid(m1_ref[0] + b_ref[:, D:2 * D])
            dbd_ref[0] = (dm * g0).astype(BF16)
            dbl_ref[0] = (dm * g1).astype(BF16)
            d0 = dm * bd_ref[0] * g0 * (1.0 - g0)
            d1 = dm * bl_ref[0] * g1 * (1.0 - g1)
            dmg_ref[0, :, 0:D] = d0.astype(BF16)
            dmg_ref[0, :, D:2 * D] = d1.astype(BF16)
            acc_ref[0, 0:1, 0:D] += jnp.sum(d0, axis=0, keepdims=True)
            acc_ref[0, 0:1, D:2 * D] += jnp.sum(d1, axis=0, keepdims=True)

    lat = pl.BlockSpec((1, tm, D), lambda b, t: (b, jnp.maximum(t - nct, 0), 0))
    return pl.pallas_call(
        body, name="merge_bwd", grid=(B, T // tm),
        in_specs=[lat, lat, lat, pl.BlockSpec((1, tm, D), lambda b, t: (b, t, 0)),
                  pl.BlockSpec((1, tm, D), lambda b, t: (b, t, 1)),
                  pl.BlockSpec((1, 2 * D), lambda b, t: (0, 0))],
        out_specs=(lat, lat, pl.BlockSpec((1, tm, 2 * D), lambda b, t: (b, t, 0)),
                   pl.BlockSpec((1, SUBLANES, 2 * D), lambda b, t: (b, 0, 0))),
        out_shape=(jax.ShapeDtypeStruct((B, N, D), BF16), jax.ShapeDtypeStruct((B, N, D), BF16),
                   jax.ShapeDtypeStruct((B, T, 2 * D), BF16), jax.ShapeDtypeStruct((B, SUBLANES, 2 * D), F32)),
        compiler_params=_cparams(("parallel", "arbitrary"), VMEM_LIMIT))(dmi, bd, bl, pmg, pmg, bm)


def _grid_taps(n_rows):
    taps = []
    for di in (-1, 0, 1):
        for dj in (-1, 0, 1):
            taps.append(((di + 1) * 3 + (dj + 1), di, dj))
    return taps


def _grid_shift(x, di, dj, gr, gc, n):
    off = di * GRID_W + dj
    y = x if off == 0 else pltpu.roll(x, (-off) % n, 0)
    ok = jnp.logical_and(jnp.logical_and(gr + di >= 0, gr + di < n // GRID_W),
                         jnp.logical_and(gc + dj >= 0, gc + dj < GRID_W))
    return jnp.where(ok, y, 0.0)


def _ffn_act(F, w9, b, tc):
    B, N, C2 = F.shape
    Cf = C2 // 2
    ncb = Cf // tc

    def body(g_ref, v_ref, w_ref, b_ref, o_ref):
        x = g_ref[0]
        pos = lax.broadcasted_iota(jnp.int32, x.shape, 0)
        gr, gc = jnp.right_shift(pos, GRID_SHIFT), jnp.bitwise_and(pos, GRID_W - 1)
        conv = jnp.broadcast_to(b_ref[...], x.shape)
        for k, di, dj in _grid_taps(N):
            conv = conv + _grid_shift(x, di, dj, gr, gc, N) * w_ref[k:k + 1, :]
        o_ref[0] = (_gelu(conv) * v_ref[0]).astype(BF16)

    return pl.pallas_call(
        body, name="ffn_act", grid=(B, ncb),
        in_specs=[pl.BlockSpec((1, N, tc), lambda b, j: (b, 0, j)),
                  pl.BlockSpec((1, N, tc), lambda b, j: (b, 0, ncb + j)),
                  pl.BlockSpec((9, tc), lambda b, j: (0, j)),
                  pl.BlockSpec((1, tc), lambda b, j: (0, j))],
        out_specs=pl.BlockSpec((1, N, tc), lambda b, j: (b, 0, j)),
        out_shape=jax.ShapeDtypeStruct((B, N, Cf), BF16),
        compiler_params=_cparams(("parallel", "parallel"), VMEM_LIMIT))(F, F, w9, b)


def _ffn_act_bwd(F, w9, b, df, tc):
    B, N, C2 = F.shape
    Cf = C2 // 2
    ncb = Cf // tc

    def body(g_ref, v_ref, w_ref, b_ref, df_ref, dF_ref, acc_ref, dv_s):
        p = pl.program_id(2)

        @pl.when(p == 0)
        def _():
            x = g_ref[0]
            pos = lax.broadcasted_iota(jnp.int32, x.shape, 0)
            gr, gc = jnp.right_shift(pos, GRID_SHIFT), jnp.bitwise_and(pos, GRID_W - 1)
            conv = jnp.broadcast_to(b_ref[...], x.shape)
            for k, di, dj in _grid_taps(N):
                conv = conv + _grid_shift(x, di, dj, gr, gc, N) * w_ref[k:k + 1, :]
            df_ = df_ref[0]
            dv_s[...] = (df_ * _gelu(conv)).astype(BF16)
            dc = df_ * v_ref[0] * _dgelu(conv)
            dx = jnp.zeros_like(x)
            for k, di, dj in _grid_taps(N):
                dx = dx + _grid_shift(dc, -di, -dj, gr, gc, N) * w_ref[k:k + 1, :]
                acc_ref[0, k:k + 1, :] = jnp.sum(dc * _grid_shift(x, di, dj, gr, gc, N), axis=0, keepdims=True)
            acc_ref[0, 9:10, :] = jnp.sum(dc, axis=0, keepdims=True)
            acc_ref[0, 10:16, :] = jnp.zeros((6, tc), F32)
            dF_ref[0] = dx.astype(BF16)

        @pl.when(p == 1)
        def _():
            dF_ref[0] = dv_s[...]

    return pl.pallas_call(
        body, name="ffn_act_bwd", grid=(B, ncb, 2),
        in_specs=[pl.BlockSpec((1, N, tc), lambda b, j, p: (b, 0, j)),
                  pl.BlockSpec((1, N, tc), lambda b, j, p: (b, 0, ncb + j)),
                  pl.BlockSpec((9, tc), lambda b, j, p: (0, j)),
                  pl.BlockSpec((1, tc), lambda b, j, p: (0, j)),
                  pl.BlockSpec((1, N, tc), lambda b, j, p: (b, 0, j))],
        out_specs=(pl.BlockSpec((1, N, tc), lambda b, j, p: (b, 0, j + p * ncb)),
                   pl.BlockSpec((1, 16, tc), lambda b, j, p: (b, 0, j))),
        out_shape=(jax.ShapeDtypeStruct((B, N, C2), BF16), jax.ShapeDtypeStruct((B, 16, Cf), F32)),
        scratch_shapes=[pltpu.VMEM((N, tc), BF16)],
        compiler_params=_cparams(("parallel", "parallel", "arbitrary"), VMEM_LIMIT))(F, F, w9, b, df)


ADAM_ROWS = 512


def _adamw(w, m, v, gparts, name):
    rows, cols = w.shape
    P = gparts.shape[0]
    tr = _row_tile(rows, (P + 14) * 4 * (-(-cols // LANES) * LANES))
    c1 = 1.0 - ADAM_B1 ** ADAM_STEP
    c2 = 1.0 - ADAM_B2 ** ADAM_STEP

    def body(w_ref, m_ref, v_ref, g_ref, go_ref, d_ref, mo_ref, vo_ref):
        g = g_ref[0].astype(F32)
        for p in range(1, P):
            g = g + g_ref[p].astype(F32)
        m_ = ADAM_B1 * m_ref[...] + (1.0 - ADAM_B1) * g
        v_ = ADAM_B2 * v_ref[...] + (1.0 - ADAM_B2) * (g * g)
        go_ref[...] = g
        mo_ref[...] = m_
        vo_ref[...] = v_
        d_ref[...] = -ADAM_LR * ((m_ / c1) / (jnp.sqrt(v_ / c2) + ADAM_EPS) + ADAM_WD * w_ref[...])

    row = pl.BlockSpec((tr, cols), lambda i: (i, 0))
    out = jax.ShapeDtypeStruct((rows, cols), F32)
    return pl.pallas_call(
        body, name=name, grid=(rows // tr,),
        in_specs=[row, row, row, pl.BlockSpec((P, tr, cols), lambda i: (0, i, 0))],
        out_specs=(row, row, row, row), out_shape=(out, out, out, out),
        compiler_params=_cparams(("parallel",), VMEM_LIMIT))(w, m, v, gparts)


def _pack_rows(flat_len):
    rows = -(-flat_len // LANES)
    if rows > ADAM_ROWS:
        rows = -(-rows // ADAM_ROWS) * ADAM_ROWS
    else:
        rows = -(-rows // SUBLANES) * SUBLANES
    return rows


def _pack(arrs, lead=()):
    flat = jnp.concatenate([a.reshape(lead + (-1,)).astype(F32) for a in arrs], axis=-1)
    n = flat.shape[-1]
    rows = _pack_rows(n)
    flat = jnp.pad(flat, [(0, 0)] * len(lead) + [(0, rows * LANES - n)])
    return flat.reshape(lead + (rows, LANES))


def _unpack(buf, shapes):
    flat = buf.reshape(-1)
    out, o = [], 0
    for s in shapes:
        n = math.prod(s)
        out.append(flat[o:o + n].reshape(s))
        o += n
    return out


def _col_shards(full, n_lead):
    C = full.shape[-1]
    x = full.reshape(full.shape[:-1] + (N_DEV, C // N_DEV))
    return jnp.moveaxis(x, -2, 0)


def _from_col_shards(g):
    x = jnp.moveaxis(g, 0, -2)
    return x.reshape(x.shape[:-2] + (x.shape[-2] * x.shape[-1],))


def kernel(x, c, ctx, c_ctx, w_ada, b_ada, g_pre_mix, g_post_mix, g_pre_ffn, g_post_ffn, w_in, b_merge, dn_conv, dn_a_log, dn_dt_bias, dn_onorm, lru_conv, lru_conv_b, lru_w_rg, lru_b_rg, lru_w_ig, lru_b_ig, lru_lambda, w_branch_dn, w_branch_lru, w_out, w_up, ffn_dw, ffn_dw_b, w_down, loss_target, m_c_ctx, m_w_ada, m_b_ada, m_g_pre_mix, m_g_post_mix, m_g_pre_ffn, m_g_post_ffn, m_w_in, m_b_merge, m_dn_conv, m_dn_a_log, m_dn_dt_bias, m_dn_onorm, m_lru_conv, m_lru_conv_b, m_lru_w_rg, m_lru_b_rg, m_lru_w_ig, m_lru_b_ig, m_lru_lambda, m_w_branch_dn, m_w_branch_lru, m_w_out, m_w_up, m_ffn_dw, m_ffn_dw_b, m_w_down, v_c_ctx, v_w_ada, v_b_ada, v_g_pre_mix, v_g_post_mix, v_g_pre_ffn, v_g_post_ffn, v_w_in, v_b_merge, v_dn_conv, v_dn_a_log, v_dn_dt_bias, v_dn_onorm, v_lru_conv, v_lru_conv_b, v_lru_w_rg, v_lru_b_rg, v_lru_w_ig, v_lru_b_ig, v_lru_lambda, v_w_branch_dn, v_w_branch_lru, v_w_out, v_w_up, v_ffn_dw, v_ffn_dw_b, v_w_down):
    params = dict(c_ctx=c_ctx, w_ada=w_ada, b_ada=b_ada, g_pre_mix=g_pre_mix, g_post_mix=g_post_mix, g_pre_ffn=g_pre_ffn, g_post_ffn=g_post_ffn, w_in=w_in, b_merge=b_merge, dn_conv=dn_conv, dn_a_log=dn_a_log, dn_dt_bias=dn_dt_bias, dn_onorm=dn_onorm, lru_conv=lru_conv, lru_conv_b=lru_conv_b, lru_w_rg=lru_w_rg, lru_b_rg=lru_b_rg, lru_w_ig=lru_w_ig, lru_b_ig=lru_b_ig, lru_lambda=lru_lambda, w_branch_dn=w_branch_dn, w_branch_lru=w_branch_lru, w_out=w_out, w_up=w_up, ffn_dw=ffn_dw, ffn_dw_b=ffn_dw_b, w_down=w_down)
    mom1 = dict(c_ctx=m_c_ctx, w_ada=m_w_ada, b_ada=m_b_ada, g_pre_mix=m_g_pre_mix, g_post_mix=m_g_post_mix, g_pre_ffn=m_g_pre_ffn, g_post_ffn=m_g_post_ffn, w_in=m_w_in, b_merge=m_b_merge, dn_conv=m_dn_conv, dn_a_log=m_dn_a_log, dn_dt_bias=m_dn_dt_bias, dn_onorm=m_dn_onorm, lru_conv=m_lru_conv, lru_conv_b=m_lru_conv_b, lru_w_rg=m_lru_w_rg, lru_b_rg=m_lru_b_rg, lru_w_ig=m_lru_w_ig, lru_b_ig=m_lru_b_ig, lru_lambda=m_lru_lambda, w_branch_dn=m_w_branch_dn, w_branch_lru=m_w_branch_lru, w_out=m_w_out, w_up=m_w_up, ffn_dw=m_ffn_dw, ffn_dw_b=m_ffn_dw_b, w_down=m_w_down)
    mom2 = dict(c_ctx=v_c_ctx, w_ada=v_w_ada, b_ada=v_b_ada, g_pre_mix=v_g_pre_mix, g_post_mix=v_g_post_mix, g_pre_ffn=v_g_pre_ffn, g_post_ffn=v_g_post_ffn, w_in=v_w_in, b_merge=v_b_merge, dn_conv=v_dn_conv, dn_a_log=v_dn_a_log, dn_dt_bias=v_dn_dt_bias, dn_onorm=v_dn_onorm, lru_conv=v_lru_conv, lru_conv_b=v_lru_conv_b, lru_w_rg=v_lru_w_rg, lru_b_rg=v_lru_b_rg, lru_w_ig=v_lru_w_ig, lru_b_ig=v_lru_b_ig, lru_lambda=v_lru_lambda, w_branch_dn=v_w_branch_dn, w_branch_lru=v_w_branch_lru, w_out=v_w_out, w_up=v_w_up, ffn_dw=v_ffn_dw, ffn_dw_b=v_ffn_dw_b, w_down=v_w_down)
    names = list(params)

    B, N, D = x.shape
    NC = ctx.shape[1]
    T = NC + N
    H, hd = dn_a_log.shape[2], dn_onorm.shape[1]
    HD = H * hd
    nd = 2 * H
    W = lru_conv_b.shape[1]
    nblk, bdim = lru_w_rg.shape[2], lru_w_rg.shape[3]
    Cf = ffn_dw_b.shape[1]
    sw = w_ada.shape[2]
    tm = min(256, NC)
    nct = NC // tm
    ncc = NC // CHUNK
    nch = T // CHUNK
    R, RL = B * T, B * N
    bw = min(256, W)
    me = _my_index()
    assert NC % tm == 0 and N % tm == 0 and B + 1 <= SUBLANES and bw % bdim == 0 and D % LANES == 0

    cpad = jnp.zeros((SUBLANES, D), F32).at[:B].set(c).at[B].set(c_ctx)
    ccp = _all_gather([cpad], "ag_cond")[0].reshape(N_DEV * SUBLANES, D)
    mods_sh = _ada_fwd(ccp, w_ada[0], lax.dynamic_slice(b_ada, (0, me * sw), (1, sw)))
    lru_vecs = jnp.concatenate([lru_b_rg[0], lru_b_ig[0], lru_lambda[0], jnp.zeros_like(lru_lambda[0])], axis=0)
    gathered = _all_gather(
        [mods_sh, w_in[0].astype(BF16), w_up[0].astype(BF16), w_down[0].astype(BF16), w_branch_dn[0].astype(BF16),
         w_branch_lru[0].astype(BF16), w_out[0].astype(BF16), dn_conv[0], lru_conv[0], lru_vecs, ffn_dw[0].reshape(9, -1)],
        "ag_weights")
    mods_g, w_in_g, w_up_g, w_down_g, w_bdn_g, w_blru_g, w_out_g, dn_conv_g, lru_conv_g, lru_vecs_g, ffn_dw_g = gathered
    mine = lax.dynamic_slice(mods_g, (0, me * SUBLANES, 0), (N_DEV, SUBLANES, sw))
    mods = jnp.moveaxis(mine, 0, 1).reshape(SUBLANES, 6, D)[:B + 1]
    mods = jnp.pad(mods, ((0, 0), (0, SUBLANES - 6), (0, 0)))
    w_down_f = w_down_g.reshape(Cf, D)
    w_bdn_f, w_blru_f, w_out_f = w_bdn_g.reshape(HD, D), w_blru_g.reshape(W, D), w_out_g.reshape(D, D)
    dn_conv_f = _from_col_shards(dn_conv_g)
    lru_conv_f = _from_col_shards(lru_conv_g)
    lru_vecs_f = _from_col_shards(lru_vecs_g)
    lru_lam_f = lru_vecs_f[4:6]
    ffn_dw_f = _from_col_shards(ffn_dw_g)

    csh = w_in_g.shape[2]

    def in_cols(a, b):
        parts = []
        for k_ in range(a // csh, (b - 1) // csh + 1):
            parts.append(w_in_g[k_, :, max(a, k_ * csh) - k_ * csh:min(b, (k_ + 1) * csh) - k_ * csh])
        return parts[0] if len(parts) == 1 else jnp.concatenate(parts, axis=1)

    o_z, o_a, o_xl = 3 * HD, 4 * HD, 4 * HD + 2 * nd
    o_yl, o_mg = o_xl + W, o_xl + 2 * W
    w_qkv, w_z = in_cols(0, o_z), in_cols(o_z, o_a)
    w_ab = jnp.pad(in_cols(o_a, o_xl), ((0, 0), (0, LANES - 2 * nd)))
    w_xl, w_yl, w_mg = in_cols(o_xl, o_yl), in_cols(o_yl, o_mg), in_cols(o_mg, o_mg + 2 * D)

    def blockdiag(wg):
        per = bw // bdim
        g5 = wg.reshape(2, W // bw, per, bdim, bdim)
        eye = jnp.eye(per, dtype=wg.dtype)
        return jnp.einsum("dtpij,pq->dtpiqj", g5, eye).reshape(2, W // bw, bw, bw)

    bd_rg, bd_ig = blockdiag(lru_w_rg[0]), blockdiag(lru_w_ig[0])
    wbd = jnp.stack([bd_rg[0], bd_ig[0], bd_rg[1], bd_ig[1]]).astype(BF16)
    bias4 = jnp.stack([lru_vecs_f[0], lru_vecs_f[2], lru_vecs_f[1], lru_vecs_f[3]])
    alog_v = jnp.pad(dn_a_log.reshape(1, nd), ((0, 0), (0, LANES - nd)))
    dtb_v = jnp.pad(dn_dt_bias.reshape(1, nd), ((0, 0), (0, LANES - nd)))

    h0 = jnp.concatenate([ctx, x], axis=1)
    u1 = _norm_mod_fwd(h0, g_pre_mix, mods, 0, 1, nct, tm, "norm_mod1")
    u1f = u1.reshape(R, D)
    p_qkv = _mm(u1f, w_qkv, "nn", "mm_qkv").reshape(B, T, 3 * HD)
    p_z = _mm(u1f, w_z, "nn", "mm_z").reshape(B, T, HD)
    p_ab = _mm(u1f, w_ab, "nn", "mm_ab")
    p_xl = _mm(u1f, w_xl, "nn", "mm_xl").reshape(B, T, W)
    p_yl = _mm(u1f, w_yl, "nn", "mm_yl").reshape(B, T, W)
    p_mg = _mm(u1f, w_mg, "nn", "mm_mg").reshape(B, T, 2 * D)

    conv_offs = (-2, -1, 0, 1)
    tcc = _tile(HD, 512)
    gb = _ab_act(p_ab, alog_v, dtb_v, nd, tm)
    gb5 = gb.reshape(B, nch, CHUNK, LANES)
    g_ = gb5[..., :nd].reshape(B, nch, CHUNK, 2, H)
    be_ = gb5[..., nd:2 * nd].reshape(B, nch, CHUNK, 2, H)
    gbc = jnp.concatenate([g_, be_], axis=-1).transpose(0, 3, 1, 2, 4)
    gbr = gbc.transpose(0, 1, 2, 4, 3)
    cv = _dwconv(p_qkv, 0, 3 * HD, dn_conv_f, None, conv_offs, nct, tm, tcc, "dn_conv")
    qkvn = _dn_act(cv, H, hd, tm)
    o2, s_in = _delta_fwd(qkvn, gbc, gbr, H, hd, ncc)
    y_dn = _dn_out(o2, p_z, 0, dn_onorm, H, hd, nct, tm)

    tcw = _tile(W, 512)
    xc = _dwconv(p_xl, 0, W, lru_conv_f, lru_conv_b, conv_offs, nct, tm, tcw, "lru_conv")
    a2, inp2 = _lru_gates(xc, wbd, bias4, lru_lam_f, tm, bw)
    hd0 = _lru_scan(a2, inp2, 0, nct, tm, tcw)
    hd1 = _lru_scan(a2, inp2, 1, nct, tm, tcw)
    y_lru = _lru_out(hd0, hd1, p_yl, 0, nct, tm, tcw)

    bd = _mm(y_dn.reshape(RL, HD), w_bdn_f, "nn", "mm_bdn").reshape(B, N, D)
    bl = _mm(y_lru.reshape(RL, W), w_blru_f, "nn", "mm_blru").reshape(B, N, D)
    mixin = _merge(bd, bl, p_mg, b_merge, nct, tm)
    mix = _mm(mixin.reshape(RL, D), w_out_f, "nn", "mm_out").reshape(B, N, D)
    h1 = _resid_norm_fwd(x, mix, g_post_mix, mods, 2, tm, "resid_norm1")
    u2 = _norm_mod_fwd(h1, g_pre_ffn, mods, 3, 4, 0, tm, "norm_mod2")
    Fp = _mm(u2.reshape(RL, D), w_up_g, "nn", "mm_up").reshape(B, N, 2 * Cf)
    tcf = LANES
    f = _ffn_act(Fp, ffn_dw_f, ffn_dw_b, tcf)
    dproj = _mm(f.reshape(RL, Cf), w_down_f, "nn", "mm_down").reshape(B, N, D)

    dd, dh2, acc_f = _final_fwd_bwd(h1, dproj, g_post_ffn, mods, 5, loss_target, tm)
    loss = lax.psum((0.5 / D) * jnp.sum(acc_f[:, 2, :]), MESH_AXES)
    ddf = dd.reshape(RL, D)
    df = _mm(ddf, w_down_f, "nt", "mm_down_dx").reshape(B, N, Cf)
    gw_down = _mm(f.reshape(RL, Cf), ddf, "tn", "mm_down_dw", out_dtype=BF16)
    dF, acc_ffn = _ffn_act_bwd(Fp, ffn_dw_f, ffn_dw_b, df, tcf)
    dFf = dF.reshape(RL, 2 * Cf)
    du2 = _mm(dFf, w_up_g, "nt", "mm_up_dx").reshape(B, N, D)
    gw_up = _mm(u2.reshape(RL, D), dFf, "tn", "mm_up_dw", out_dtype=BF16, out_shards=N_DEV)
    dh1, acc2x, _ = _norm_mod_bwd(h1, du2, g_pre_ffn, mods, 4, 0, tm, dh2, "norm_mod2_bwd")
    dmix, acc_r1 = _resid_norm_bwd(mix, dh1, g_post_mix, mods, 2, tm, "resid_norm1_bwd")
    dmixf = dmix.reshape(RL, D)
    dmixin = _mm(dmixf, w_out_f, "nt", "mm_out_dx").reshape(B, N, D)
    gw_out = _mm(mixin.reshape(RL, D), dmixf, "tn", "mm_out_dw", out_dtype=BF16)
    dbd, dbl, dmg, acc_mg = _merge_bwd(dmixin, bd, bl, p_mg, b_merge, nct, tm)
    dy_dn = _mm(dbd.reshape(RL, D), w_bdn_f, "nt", "mm_bdn_dx").reshape(B, N, HD)
    gw_bdn = _mm(y_dn.reshape(RL, HD), dbd.reshape(RL, D), "tn", "mm_bdn_dw", out_dtype=BF16)
    dy_lru = _mm(dbl.reshape(RL, D), w_blru_f, "nt", "mm_blru_dx").reshape(B, N, W)
    gw_blru = _mm(y_lru.reshape(RL, W), dbl.reshape(RL, D), "tn", "mm_blru_dw", out_dtype=BF16)

    dh, dyl = _lru_out_bwd(hd0, hd1, p_yl, 0, dy_lru, nct, tm, tcw)
    lam0, da0 = _lru_scan_bwd(a2, hd0, dh, 0, nct, tm, tcw)
    lam1, da1 = _lru_scan_bwd(a2, hd1, dh, 1, nct, tm, tcw)
    dxc, dwbd, acc_lru = _lru_gates_bwd(xc, wbd, bias4, lru_lam_f, jnp.stack([lam0, lam1]), jnp.stack([da0, da1]), tm, bw)
    dxl = _dwconv(dxc, 0, W, lru_conv_f, None, tuple(-o for o in conv_offs), nct, tm, tcw, "lru_conv_dx", BF16)
    acc_lc = _dwconv_wgrad(dxc, p_xl, 0, W, conv_offs, nct, tm, tcw, "lru_conv_dw")

    do, dz, acc_on = _dn_out_bwd(o2, p_z, 0, dn_onorm, dy_dn, H, hd, nct, tm)
    dqkvn, dgb = _delta_bwd(qkvn, gbc, gbr, s_in, do, H, hd, ncc)
    dcv = _dn_act_bwd(cv, dqkvn, H, hd, tm)
    dqkv = _dwconv(dcv, 0, 3 * HD, dn_conv_f, None, tuple(-o for o in conv_offs), nct, tm, tcc, "dn_conv_dx", BF16)
    acc_dc = _dwconv_wgrad(dcv, p_qkv, 0, 3 * HD, conv_offs, nct, tm, tcc, "dn_conv_dw")
    dgb_t = dgb.transpose(0, 2, 3, 1, 4)
    dgb_l = jnp.concatenate([dgb_t[..., :H].reshape(B, nch, CHUNK, nd), dgb_t[..., H:].reshape(B, nch, CHUNK, nd)], axis=-1)
    dgb_l = jnp.pad(dgb_l.reshape(R, 2 * nd), ((0, 0), (0, LANES - 2 * nd)))
    dp_ab, acc_ab = _ab_act_bwd(p_ab, gb, dgb_l, alog_v, dtb_v, nd, tm)

    groups = [(dqkv.reshape(R, 3 * HD), w_qkv, "qkv"), (dz.reshape(R, HD), w_z, "z"), (dp_ab, w_ab, "ab"),
              (dxl.reshape(R, W), w_xl, "xl"), (dyl.reshape(R, W), w_yl, "yl"), (dmg.reshape(R, 2 * D), w_mg, "mg")]
    du1 = None
    gw_groups = []
    for dg_, wg_, nm in groups:
        du1 = _mm(dg_, wg_, "nt", "mm_in_dx_" + nm, add=du1)
        gw_groups.append(_mm(u1f, dg_, "tn", "mm_in_dw_" + nm, out_dtype=BF16))
    gw_groups[2] = gw_groups[2][:, :2 * nd]
    gw_in = jnp.concatenate(gw_groups, axis=1)
    gw_in = jnp.moveaxis(gw_in.reshape(D, N_DEV, csh), 1, 0)
    grad_x, acc1x, acc1c = _norm_mod_bwd(h0, du1.reshape(B, T, D), g_pre_mix, mods, 1, nct, tm, dh1, "norm_mod1_bwd")

    zeros_d = jnp.zeros((B, D), F32)
    dm_rows = jnp.stack([acc1x[:, 0], acc1x[:, 1], acc_r1[:, 0], acc2x[:, 0], acc2x[:, 1], acc_f[:, 0]], axis=1)
    dm_ctx = jnp.stack([jnp.sum(acc1c[:, 0], 0), jnp.sum(acc1c[:, 1], 0)] + [zeros_d[0]] * 4, axis=0)[None]
    dm_pad = jnp.zeros((SUBLANES, 6 * D), F32).at[:B + 1].set(jnp.concatenate([dm_rows, dm_ctx], 0).reshape(B + 1, 6 * D))
    dm_g = _all_gather([dm_pad], "ag_dmods")[0]
    g_mine = lax.dynamic_slice(dm_g, (0, 0, me * sw), (N_DEV, SUBLANES, sw)).reshape(N_DEV * SUBLANES, sw)
    gw_ada, dcc = _ada_bwd(ccp, w_ada[0], g_mine, c_ctx.reshape(1, D), B)

    per = bw // bdim

    def diag_blocks(dwf):
        g6 = dwf.reshape(W // bw, per, bdim, per, bdim)
        return jnp.einsum("tpiqj,pq->tpij", g6, jnp.eye(per, dtype=F32)).reshape(nblk, bdim, bdim)

    g_rep = dict(
        c_ctx=dcc[0],
        g_pre_mix=jnp.sum(acc1x[:, 2] + acc1c[:, 2], 0)[None], g_post_mix=jnp.sum(acc_r1[:, 1], 0)[None],
        g_pre_ffn=jnp.sum(acc2x[:, 2], 0)[None], g_post_ffn=jnp.sum(acc_f[:, 1], 0)[None],
        b_merge=jnp.sum(acc_mg[:, 0], 0)[None],
        dn_a_log=acc_ab[0, :nd].reshape(1, 2, H), dn_dt_bias=acc_ab[1, :nd].reshape(1, 2, H),
        dn_onorm=jnp.sum(acc_on[:, 0], 0)[None],
        lru_conv_b=jnp.sum(acc_lc[:, 4], 0)[None],
        lru_w_rg=jnp.stack([diag_blocks(dwbd[0]), diag_blocks(dwbd[2])])[None],
        lru_w_ig=jnp.stack([diag_blocks(dwbd[1]), diag_blocks(dwbd[3])])[None],
        ffn_dw_b=jnp.sum(acc_ffn[:, 9], 0)[None])
    rep_names = list(g_rep)
    rep_parts = _all_gather([_pack([g_rep[n] for n in rep_names])], "ag_rep_grads")[0]
    rep_out = _adamw(_pack([params[n] for n in rep_names]), _pack([mom1[n] for n in rep_names]),
                     _pack([mom2[n] for n in rep_names]), rep_parts, "adamw_rep")
    results = {}
    for k, buf in enumerate(rep_out):
        for n, arr in zip(rep_names, _unpack(buf, [params[n].shape for n in rep_names])):
            results.setdefault(n, [None] * 4)[k] = arr

    ba_out = _adamw(_pack([b_ada]), _pack([m_b_ada]), _pack([v_b_ada]),
                    _pack([dm_g.reshape(N_DEV * SUBLANES, 6 * D)], lead=(N_DEV * SUBLANES,)), "adamw_b_ada")
    results["b_ada"] = [_unpack(buf, [b_ada.shape])[0] for buf in ba_out]

    wa_out = _adamw(w_ada[0], m_w_ada[0], v_w_ada[0], gw_ada[None], "adamw_w_ada")
    results["w_ada"] = [buf[None] for buf in wa_out]

    g_lru_conv = jnp.sum(acc_lc[:, :4], 0)
    g_dn_conv = jnp.sum(acc_dc[:, :4], 0)
    g_ffn_dw = jnp.sum(acc_ffn[:, :9], 0).reshape(3, 3, Cf)
    sm_names = ["dn_conv", "lru_conv", "lru_b_rg", "lru_b_ig", "lru_lambda", "ffn_dw"]
    sm_parts = [_col_shards(g_dn_conv, 1), _col_shards(g_lru_conv, 1),
                _col_shards(jnp.stack([acc_lru[0], acc_lru[2]]), 1), _col_shards(jnp.stack([acc_lru[1], acc_lru[3]]), 1),
                _col_shards(acc_lru[4:6], 1), _col_shards(g_ffn_dw, 2)]
    big_names = ["w_in", "w_up", "w_down", "w_branch_dn", "w_branch_lru", "w_out"]
    parts8 = [gw_in, gw_up, gw_down.reshape(N_DEV, Cf // N_DEV, D), gw_bdn.reshape(N_DEV, HD // N_DEV, D),
              gw_blru.reshape(N_DEV, W // N_DEV, D), gw_out.reshape(N_DEV, D // N_DEV, D),
              _pack(sm_parts, lead=(N_DEV,))]
    sib4 = _pair_exchange(parts8, "rs_pair")
    pair4 = [_pair_sum(p8, s4, "rs_pair_sum_" + n) for p8, s4, n in zip(parts8, sib4, big_names + ["small"])]
    recv4 = _quad_exchange(pair4, "rs_quad")
    for n, g4 in zip(big_names, recv4[:-1]):
        out4 = _adamw(params[n][0], mom1[n][0], mom2[n][0], g4, "adamw_" + n)
        results[n] = [buf[None] for buf in out4]
    sm_out = _adamw(_pack([params[n] for n in sm_names]), _pack([mom1[n] for n in sm_names]),
                    _pack([mom2[n] for n in sm_names]), recv4[-1], "adamw_small")
    for k, buf in enumerate(sm_out):
        for n, arr in zip(sm_names, _unpack(buf, [params[n].shape for n in sm_names])):
            results.setdefault(n, [None] * 4)[k] = arr

    outs = [loss, grad_x]
    for k in range(4):
        outs += [results[n][k] for n in names]
    return tuple(outs)
```

```python
import functools
import math

import jax
import jax.numpy as jnp
from jax import lax
from jax.experimental import pallas as pl
from jax.experimental.pallas import tpu as pltpu

F32 = jnp.float32
BF16 = jnp.bfloat16

EPS = 1e-6
GRID_W = 64
GRID_SHIFT = 6
CHUNK = 64
LRU_C = 8.0
N_DEV = 8
ADAM_LR = 0.001
ADAM_B1 = 0.9
ADAM_B2 = 0.999
ADAM_EPS = 1e-08
ADAM_WD = 0.01
ADAM_STEP = 10

LANES = 128
SUBLANES = 8
VMEM_LIMIT = 48 * 1024 * 1024
MESH_AXES = ("x", "y", "c")
GELU_C = math.sqrt(2.0 / math.pi)


def _cparams(sem=None, vmem=None):
    kw = {}
    if sem is not None:
        kw["dimension_semantics"] = sem
    if vmem is not None:
        kw["vmem_limit_bytes"] = vmem
    return pltpu.CompilerParams(**kw)


def _tile(n, pref):
    if n <= pref:
        return n
    t = (pref // LANES) * LANES
    while n % t:
        t -= LANES
    return t


def _sigmoid(x):
    return 1.0 / (1.0 + jnp.exp(-x))


def _silu(x):
    return x * _sigmoid(x)


def _dsilu(x):
    s = _sigmoid(x)
    return s * (1.0 + x * (1.0 - s))


def _softplus(x):
    return jnp.maximum(x, 0.0) + jnp.log(1.0 + jnp.exp(-jnp.abs(x)))


def _gelu(x):
    return 0.5 * x * (1.0 + jnp.tanh(GELU_C * (x + 0.044715 * x * x * x)))


def _dgelu(x):
    t = jnp.tanh(GELU_C * (x + 0.044715 * x * x * x))
    return 0.5 * (1.0 + t) + 0.5 * x * (1.0 - t * t) * GELU_C * (1.0 + 3.0 * 0.044715 * x * x)


def _dot(a, b, dims="nn"):
    dn = {"nn": (((1,), (0,)), ((), ())),
          "nt": (((1,), (1,)), ((), ())),
          "tn": (((0,), (0,)), ((), ()))}[dims]
    return lax.dot_general(a.astype(BF16), b.astype(BF16), dn, preferred_element_type=F32)


def _split2(x):
    hi = x.astype(BF16)
    lo = (x - hi.astype(F32)).astype(BF16)
    return hi, lo


def _split3(x):
    h1 = x.astype(BF16)
    r1 = x - h1.astype(F32)
    h2 = r1.astype(BF16)
    h3 = (r1 - h2.astype(F32)).astype(BF16)
    return h1, h2, h3


def _dot_hi(a, b):
    ah, al = _split2(a)
    bh, bl = _split2(b)
    return _dot(ah, bh) + (_dot(ah, bl) + _dot(al, bh))


def _dot_mask(m, x, dims="nn"):
    h1, h2, h3 = _split3(x)
    if dims == "xnt":
        return _dot(h1, m, "nt") + (_dot(h2, m, "nt") + _dot(h3, m, "nt"))
    return _dot(m, h1, dims) + (_dot(m, h2, dims) + _dot(m, h3, dims))


def _my_index():
    return 4 * lax.axis_index("x") + 2 * lax.axis_index("y") + lax.axis_index("c")


def _hbm_call(body, xs, out_shapes, n_sems, name):
    any_spec = pl.BlockSpec(memory_space=pl.ANY)
    return pl.pallas_call(
        body, name=name, out_shape=tuple(out_shapes),
        in_specs=[any_spec] * len(xs), out_specs=tuple([any_spec] * len(out_shapes)),
        scratch_shapes=[pltpu.SemaphoreType.DMA((n_sems,)), pltpu.SemaphoreType.DMA((n_sems,)),
                        pltpu.SemaphoreType.DMA((len(xs),))],
    )(*xs)


def _all_gather(xs, name):
    n = len(xs)

    def body(*refs):
        x_refs, out_refs = refs[:n], refs[n:2 * n]
        send_sems, recv_sems, local_sems = refs[2 * n:]
        x_, y_, c_ = lax.axis_index("x"), lax.axis_index("y"), lax.axis_index("c")
        me, sibling = (x_, y_, c_), (x_, y_, 1 - c_)
        chips = [(1 - x_, y_), (x_, 1 - y_), (1 - x_, 1 - y_)]

        def slab(a, px, py, pc):
            return out_refs[a].at[4 * px + 2 * py + pc]

        def copy(a, k, block, to, src=None):
            return pltpu.make_async_remote_copy(
                src_ref=slab(a, *block) if src is None else src, dst_ref=slab(a, *block),
                send_sem=send_sems.at[7 * a + k], recv_sem=recv_sems.at[7 * a + k],
                device_id=to, device_id_type=pl.DeviceIdType.MESH)

        mine = [pltpu.make_async_copy(x_refs[a], slab(a, *me), local_sems.at[a]) for a in range(n)]
        for cp in mine:
            cp.start()
        sent = []
        for j, chip in enumerate(chips):
            sent += [copy(a, 1 + j, me, (*chip, c_), src=x_refs[a]) for a in range(n)]
        sent += [copy(a, 0, me, sibling, src=x_refs[a]) for a in range(n)]
        for cp in sent:
            cp.start()
        for j, chip in enumerate(chips):
            for a in range(n):
                copy(a, 1 + j, (*chip, c_), me).wait_recv()
                fwd = copy(a, 4 + j, (*chip, c_), sibling)
                fwd.start()
                sent.append(fwd)
        for a in range(n):
            copy(a, 0, sibling, me).wait_recv()
        for j, chip in enumerate(chips):
            for a in range(n):
                copy(a, 4 + j, (*chip, 1 - c_), me).wait_recv()
        for cp in sent:
            cp.wait_send()
        for cp in mine:
            cp.wait()

    outs = [jax.ShapeDtypeStruct((N_DEV,) + x.shape, x.dtype) for x in xs]
    return _hbm_call(body, xs, outs, 7 * n, name)


def _pair_exchange(xs, name):
    n = len(xs)

    def body(*refs):
        x_refs, out_refs = refs[:n], refs[n:2 * n]
        send_sems, recv_sems, _ = refs[2 * n:]
        x_, y_, c_ = lax.axis_index("x"), lax.axis_index("y"), lax.axis_index("c")
        copies = []
        for a in range(n):
            for j in range(4):
                copies.append(pltpu.make_async_remote_copy(
                    src_ref=x_refs[a].at[2 * j + (1 - c_)], dst_ref=out_refs[a].at[j],
                    send_sem=send_sems.at[4 * a + j], recv_sem=recv_sems.at[4 * a + j],
                    device_id=(x_, y_, 1 - c_), device_id_type=pl.DeviceIdType.MESH))
        for cp in copies:
            cp.start()
        for cp in copies:
            cp.wait()

    outs = [jax.ShapeDtypeStruct((4,) + x.shape[1:], x.dtype) for x in xs]
    return _hbm_call(body, xs, outs, 4 * n, name)


def _quad_exchange(xs, name):
    n = len(xs)

    def body(*refs):
        x_refs, out_refs = refs[:n], refs[n:2 * n]
        send_sems, recv_sems, local_sems = refs[2 * n:]
        x_, y_, c_ = lax.axis_index("x"), lax.axis_index("y"), lax.axis_index("c")
        jm = 2 * x_ + y_
        mine = [pltpu.make_async_copy(x_refs[a].at[jm], out_refs[a].at[jm], local_sems.at[a]) for a in range(n)]
        for cp in mine:
            cp.start()
        sends, recvs = [], []
        for f in range(1, 4):
            px = 1 - x_ if (f >> 1) & 1 else x_
            py = 1 - y_ if f & 1 else y_
            jp = 2 * px + py
            for a in range(n):
                k = 3 * a + f - 1
                sends.append(pltpu.make_async_remote_copy(
                    src_ref=x_refs[a].at[jp], dst_ref=out_refs[a].at[jm],
                    send_sem=send_sems.at[k], recv_sem=recv_sems.at[k],
                    device_id=(px, py, c_), device_id_type=pl.DeviceIdType.MESH))
                recvs.append(pltpu.make_async_remote_copy(
                    src_ref=x_refs[a].at[jp], dst_ref=out_refs[a].at[jp],
                    send_sem=send_sems.at[k], recv_sem=recv_sems.at[k],
                    device_id=(px, py, c_), device_id_type=pl.DeviceIdType.MESH))
        for cp in sends:
            cp.start()
        for cp in sends:
            cp.wait_send()
        for cp in recvs:
            cp.wait_recv()
        for cp in mine:
            cp.wait()

    outs = [jax.ShapeDtypeStruct(x.shape, x.dtype) for x in xs]
    return _hbm_call(body, xs, outs, 3 * n, name)


def _pair_sum(x8, r4, name):
    _, r, c = x8.shape
    tr = _row_tile(r, c * 12)

    def body(core_ref, x_ref, r_ref, o_ref):
        o_ref[...] = (x_ref[...].astype(F32) + r_ref[...].astype(F32)).astype(o_ref.dtype)

    core = lax.axis_index("c").astype(jnp.int32).reshape(1)
    return pl.pallas_call(
        body, name=name,
        grid_spec=pltpu.PrefetchScalarGridSpec(
            num_scalar_prefetch=1, grid=(4, r // tr),
            in_specs=[pl.BlockSpec((None, tr, c), lambda j, i, core_ref: (2 * j + core_ref[0], i, 0)),
                      pl.BlockSpec((None, tr, c), lambda j, i, core_ref: (j, i, 0))],
            out_specs=pl.BlockSpec((None, tr, c), lambda j, i, core_ref: (j, i, 0))),
        out_shape=jax.ShapeDtypeStruct((4, r, c), x8.dtype),
        compiler_params=_cparams(("parallel", "parallel"), VMEM_LIMIT))(core, x8, r4)


def _row_tile(r, bytes_per_row, budget=4 * 1024 * 1024):
    best = None
    for t in range(16, r + 1, 16):
        if r % t == 0 and t * bytes_per_row <= budget:
            best = t
    return best if best is not None else r


def _mm(a, b, dims, name, out_dtype=F32, add=None, out_shards=1, tm=1024, tn=1024, tk=1024):
    S = b.shape[0] if b.ndim == 3 else 1
    bshape = (b.shape[1], S * b.shape[2]) if b.ndim == 3 else b.shape
    if dims == "nn":
        (M, K), (K2, N) = a.shape, bshape
    elif dims == "nt":
        (M, K), (N, K2) = a.shape, bshape
    else:
        (K, M), (K2, N) = a.shape, bshape
    assert K == K2, (a.shape, b.shape, dims)
    cs = bshape[1] // S
    tm, tk = _tile(M, tm), _tile(K, min(tk, cs) if (S > 1 and dims == "nt") else tk)
    tn = _tile(N, min(tn, cs) if (S > 1 and dims != "nt") else min(tn, N // out_shards))
    assert cs % (tk if dims == "nt" else tn) == 0 and (N // out_shards) % tn == 0
    nk = K // tk

    def body(*refs):
        if add is None:
            a_ref, b_ref, o_ref, acc = refs
        else:
            a_ref, b_ref, c_ref, o_ref, acc = refs
        k = pl.program_id(2)

        @pl.when(k == 0)
        def _():
            acc[...] = jnp.zeros_like(acc)

        acc[...] += _dot(a_ref[...], b_ref[...], dims)

        @pl.when(k == nk - 1)
        def _():
            r = acc[...]
            if add is not None:
                r = r + c_ref[...]
            o_ref[...] = r.astype(out_dtype)

    a_spec = (pl.BlockSpec((tk, tm), lambda i, j, k: (k, i)) if dims == "tn"
              else pl.BlockSpec((tm, tk), lambda i, j, k: (i, k)))
    if S == 1:
        b_spec = (pl.BlockSpec((tn, tk), lambda i, j, k: (j, k)) if dims == "nt"
                  else pl.BlockSpec((tk, tn), lambda i, j, k: (k, j)))
    elif dims == "nt":
        per = cs // tk
        b_spec = pl.BlockSpec((None, tn, tk), lambda i, j, k: (k // per, j, k % per))
    else:
        per = cs // tn
        b_spec = pl.BlockSpec((None, tk, tn), lambda i, j, k: (j // per, k, j % per))
    in_specs = [a_spec, b_spec]
    args = [a, b]
    if add is not None:
        in_specs.append(pl.BlockSpec((tm, tn), lambda i, j, k: (i, j)))
        args.append(add)
    if out_shards == 1:
        out_spec, out_shape = pl.BlockSpec((tm, tn), lambda i, j, k: (i, j)), (M, N)
    else:
        oper = N // out_shards // tn
        out_spec = pl.BlockSpec((None, tm, tn), lambda i, j, k: (j // oper, i, j % oper))
        out_shape = (out_shards, M, N // out_shards)
    return pl.pallas_call(
        body, name=name, grid=(M // tm, N // tn, nk),
        in_specs=in_specs, out_specs=out_spec,
        out_shape=jax.ShapeDtypeStruct(out_shape, out_dtype),
        scratch_shapes=[pltpu.VMEM((tm, tn), F32)],
        compiler_params=_cparams(("parallel", "parallel", "arbitrary"), VMEM_LIMIT),
    )(*args)


def _ada_fwd(ccp, w, b):
    def body(c_ref, w_ref, b_ref, o_ref):
        o_ref[...] = _dot(_silu(c_ref[...]), w_ref[...]) + b_ref[...]

    return pl.pallas_call(
        body, name="ada_fwd", out_shape=jax.ShapeDtypeStruct((ccp.shape[0], w.shape[1]), F32),
        compiler_params=_cparams(None, VMEM_LIMIT))(ccp, w, b)


def _ada_bwd(ccp, w, g, cctx, n_loc):
    def body(c_ref, w_ref, g_ref, cc_ref, gw_ref, dc_ref):
        gw_ref[...] = _dot(_silu(c_ref[...]), g_ref[...], "tn")
        dcc = _dot(g_ref[...], w_ref[...], "nt")
        row = lax.broadcasted_iota(jnp.int32, dcc.shape, 0)
        part = jnp.sum(jnp.where(row % SUBLANES == n_loc, dcc, 0.0), axis=0, keepdims=True)
        dc_ref[...] = jnp.broadcast_to(part * _dsilu(cc_ref[...]), dc_ref.shape)

    D = ccp.shape[1]
    return pl.pallas_call(
        body, name="ada_bwd",
        out_shape=(jax.ShapeDtypeStruct(w.shape, F32), jax.ShapeDtypeStruct((SUBLANES, D), F32)),
        compiler_params=_cparams(None, VMEM_LIMIT))(ccp, w, g, cctx)


def _norm_mod_fwd(h, gain, mods, i_shift, i_scale, nct, tm, name):
    B, T, D = h.shape

    def body(h_ref, g_ref, m_ref, u_ref):
        x = h_ref[0]
        r = lax.rsqrt(jnp.mean(x * x, axis=-1, keepdims=True) + EPS)
        n = x * r * g_ref[...]
        u_ref[0] = (n * (1.0 + m_ref[0, i_scale:i_scale + 1, :]) + m_ref[0, i_shift:i_shift + 1, :]).astype(BF16)

    return pl.pallas_call(
        body, name=name, grid=(B, T // tm),
        in_specs=[pl.BlockSpec((1, tm, D), lambda b, t: (b, t, 0)),
                  pl.BlockSpec((1, D), lambda b, t: (0, 0)),
                  pl.BlockSpec((1, SUBLANES, D), lambda b, t: (jnp.where(t < nct, B, b), 0, 0))],
        out_specs=pl.BlockSpec((1, tm, D), lambda b, t: (b, t, 0)),
        out_shape=jax.ShapeDtypeStruct((B, T, D), BF16),
        compiler_params=_cparams(("parallel", "parallel"), VMEM_LIMIT),
    )(h, gain, mods)


def _norm_mod_bwd(h, du, gain, mods, i_scale, nct, tm, res, name):
    B, T, D = h.shape
    nt = T // tm

    def body(h_ref, du_ref, g_ref, m_ref, res_ref, dx_ref, ax_ref, ac_ref):
        t = pl.program_id(1)
        x = h_ref[0]
        r = lax.rsqrt(jnp.mean(x * x, axis=-1, keepdims=True) + EPS)
        nh = x * r
        g = g_ref[...]
        du_ = du_ref[0]
        dn = du_ * (1.0 + m_ref[0, i_scale:i_scale + 1, :])
        dnh = dn * g
        dx = r * (dnh - nh * jnp.mean(dnh * nh, axis=-1, keepdims=True))
        sums = (jnp.sum(du_, axis=0, keepdims=True), jnp.sum(du_ * nh * g, axis=0, keepdims=True),
                jnp.sum(dn * nh, axis=0, keepdims=True))

        @pl.when(t == 0)
        def _():
            ax_ref[...] = jnp.zeros_like(ax_ref)
            ac_ref[...] = jnp.zeros_like(ac_ref)

        def accumulate(ref):
            for i, s in enumerate(sums):
                ref[0, i:i + 1, :] += s

        @pl.when(t < nct)
        def _():
            accumulate(ac_ref)

        @pl.when(t >= nct)
        def _():
            accumulate(ax_ref)
            dx_ref[0] = dx + res_ref[0]

    lat = lambda b, t: (b, jnp.maximum(t - nct, 0), 0)
    acc = pl.BlockSpec((1, SUBLANES, D), lambda b, t: (b, 0, 0))
    return pl.pallas_call(
        body, name=name, grid=(B, nt),
        in_specs=[pl.BlockSpec((1, tm, D), lambda b, t: (b, t, 0)),
                  pl.BlockSpec((1, tm, D), lambda b, t: (b, t, 0)),
                  pl.BlockSpec((1, D), lambda b, t: (0, 0)),
                  pl.BlockSpec((1, SUBLANES, D), lambda b, t: (jnp.where(t < nct, B, b), 0, 0)),
                  pl.BlockSpec((1, tm, D), lat)],
        out_specs=(pl.BlockSpec((1, tm, D), lat), acc, acc),
        out_shape=(jax.ShapeDtypeStruct(res.shape, F32),
                   jax.ShapeDtypeStruct((B, SUBLANES, D), F32), jax.ShapeDtypeStruct((B, SUBLANES, D), F32)),
        compiler_params=_cparams(("parallel", "arbitrary"), VMEM_LIMIT),
    )(h, du, gain, mods, res)


def _resid_norm_fwd(x, y, gain, mods, i_gate, tm, name):
    B, N, D = x.shape

    def body(x_ref, y_ref, g_ref, m_ref, o_ref):
        y_ = y_ref[0]
        r = lax.rsqrt(jnp.mean(y_ * y_, axis=-1, keepdims=True) + EPS)
        o_ref[0] = x_ref[0] + y_ * r * g_ref[...] * m_ref[0, i_gate:i_gate + 1, :]

    row = pl.BlockSpec((1, tm, D), lambda b, t: (b, t, 0))
    return pl.pallas_call(
        body, name=name, grid=(B, N // tm),
        in_specs=[row, row, pl.BlockSpec((1, D), lambda b, t: (0, 0)),
                  pl.BlockSpec((1, SUBLANES, D), lambda b, t: (b, 0, 0))],
        out_specs=row, out_shape=jax.ShapeDtypeStruct((B, N, D), F32),
        compiler_params=_cparams(("parallel", "parallel"), VMEM_LIMIT),
    )(x, y, gain, mods)


def _resid_norm_bwd_math(y_, dh, g, gate):
    r = lax.rsqrt(jnp.mean(y_ * y_, axis=-1, keepdims=True) + EPS)
    nh = y_ * r
    dgate = jnp.sum(dh * nh * g, axis=0, keepdims=True)
    dn = dh * gate
    dgain = jnp.sum(dn * nh, axis=0, keepdims=True)
    dnh = dn * g
    dy = r * (dnh - nh * jnp.mean(dnh * nh, axis=-1, keepdims=True))
    return dy, dgate, dgain


def _resid_norm_bwd(y, dh, gain, mods, i_gate, tm, name):
    B, N, D = y.shape

    def body(y_ref, dh_ref, g_ref, m_ref, dy_ref, acc_ref):
        t = pl.program_id(1)
        dy, dgate, dgain = _resid_norm_bwd_math(y_ref[0], dh_ref[0], g_ref[...], m_ref[0, i_gate:i_gate + 1, :])
        dy_ref[0] = dy.astype(BF16)

        @pl.when(t == 0)
        def _():
            acc_ref[...] = jnp.zeros_like(acc_ref)

        acc_ref[0, 0:1, :] += dgate
        acc_ref[0, 1:2, :] += dgain

    row = pl.BlockSpec((1, tm, D), lambda b, t: (b, t, 0))
    return pl.pallas_call(
        body, name=name, grid=(B, N // tm),
        in_specs=[row, row, pl.BlockSpec((1, D), lambda b, t: (0, 0)),
                  pl.BlockSpec((1, SUBLANES, D), lambda b, t: (b, 0, 0))],
        out_specs=(row, pl.BlockSpec((1, SUBLANES, D), lambda b, t: (b, 0, 0))),
        out_shape=(jax.ShapeDtypeStruct((B, N, D), BF16), jax.ShapeDtypeStruct((B, SUBLANES, D), F32)),
        compiler_params=_cparams(("parallel", "arbitrary"), VMEM_LIMIT),
    )(y, dh, gain, mods)


def _final_fwd_bwd(h1, d, gain, mods, i_gate, tgt, tm):
    B, N, D = h1.shape

    def body(h_ref, d_ref, g_ref, m_ref, t_ref, dd_ref, dh_ref, acc_ref):
        t = pl.program_id(1)
        d_ = d_ref[0]
        g = g_ref[...]
        gate = m_ref[0, i_gate:i_gate + 1, :]
        r = lax.rsqrt(jnp.mean(d_ * d_, axis=-1, keepdims=True) + EPS)
        e = h_ref[0] + d_ * r * g * gate - t_ref[0]
        dh = e * (1.0 / D)
        dh_ref[0] = dh
        dy, dgate, dgain = _resid_norm_bwd_math(d_, dh, g, gate)
        dd_ref[0] = dy.astype(BF16)

        @pl.when(t == 0)
        def _():
            acc_ref[...] = jnp.zeros_like(acc_ref)

        acc_ref[0, 0:1, :] += dgate
        acc_ref[0, 1:2, :] += dgain
        acc_ref[0, 2:3, :] += jnp.sum(e * e, axis=0, keepdims=True)

    row = pl.BlockSpec((1, tm, D), lambda b, t: (b, t, 0))
    return pl.pallas_call(
        body, name="final_fwd_bwd", grid=(B, N // tm),
        in_specs=[row, row, pl.BlockSpec((1, D), lambda b, t: (0, 0)),
                  pl.BlockSpec((1, SUBLANES, D), lambda b, t: (b, 0, 0)), row],
        out_specs=(row, row, pl.BlockSpec((1, SUBLANES, D), lambda b, t: (b, 0, 0))),
        out_shape=(jax.ShapeDtypeStruct((B, N, D), BF16), jax.ShapeDtypeStruct((B, N, D), F32),
                   jax.ShapeDtypeStruct((B, SUBLANES, D), F32)),
        compiler_params=_cparams(("parallel", "arbitrary"), VMEM_LIMIT),
    )(h1, d, gain, mods, tgt)


def _shifted(x, prev, nxt, off, row, tm):
    if off == 0:
        return x
    if off < 0:
        y = pltpu.roll(x, -off, 0)
        for r in range(-off):
            y = jnp.where(row == r, prev[SUBLANES + r + off:SUBLANES + r + off + 1, :], y)
        return y
    y = pltpu.roll(x, tm - off, 0)
    for r in range(off):
        y = jnp.where(row == tm - off + r, nxt[r:r + 1, :], y)
    return y


def _halo_specs(T, tm, tc, cb0, order):
    r8, n8 = tm // SUBLANES, T // SUBLANES
    if order == "btj":
        prev = lambda b, t, j: (b, jnp.maximum(t * r8 - 1, 0), cb0 + j)
        nxt = lambda b, t, j: (b, jnp.minimum((t + 1) * r8, n8 - 1), cb0 + j)
    else:
        prev = lambda b, j, t: (b, jnp.maximum(t * r8 - 1, 0), cb0 + j)
        nxt = lambda b, j, t: (b, jnp.minimum((t + 1) * r8, n8 - 1), cb0 + j)
    return pl.BlockSpec((1, SUBLANES, tc), prev), pl.BlockSpec((1, SUBLANES, tc), nxt)


def _seg_halos(p_ref, n_ref, t, nct, nt):
    seg_start = jnp.logical_or(t == 0, t == nct)
    seg_end = jnp.logical_or(t == nct - 1, t == nt - 1)
    prev = jnp.where(seg_start, 0.0, p_ref[0])
    nxt = jnp.where(seg_end, 0.0, n_ref[0])
    return prev, nxt


def _dwconv(x, col0, C, w, bias, offs, nct, tm, tc, name, out_dtype=F32):
    B, T, _ = x.shape
    nt = T // tm
    cb0 = col0 // tc

    def body(*refs):
        if bias is None:
            x_ref, p_ref, n_ref, w_ref, o_ref = refs
        else:
            x_ref, p_ref, n_ref, w_ref, b_ref, o_ref = refs
        t = pl.program_id(1)
        prev, nxt = _seg_halos(p_ref, n_ref, t, nct, nt)
        x_ = x_ref[0]
        row = lax.broadcasted_iota(jnp.int32, x_.shape, 0)
        acc = None
        for j, off in enumerate(offs):
            term = _shifted(x_, prev, nxt, off, row, tm) * w_ref[j:j + 1, :]
            acc = term if acc is None else acc + term
        if bias is not None:
            acc = acc + b_ref[...]
        o_ref[0] = acc.astype(out_dtype)

    pspec, nspec = _halo_specs(T, tm, tc, cb0, "btj")
    in_specs = [pl.BlockSpec((1, tm, tc), lambda b, t, j: (b, t, cb0 + j)), pspec, nspec,
                pl.BlockSpec((w.shape[0], tc), lambda b, t, j: (0, j))]
    args = [x, x, x, w]
    if bias is not None:
        in_specs.append(pl.BlockSpec((1, tc), lambda b, t, j: (0, j)))
        args.append(bias)
    return pl.pallas_call(
        body, name=name, grid=(B, nt, C // tc),
        in_specs=in_specs, out_specs=pl.BlockSpec((1, tm, tc), lambda b, t, j: (b, t, j)),
        out_shape=jax.ShapeDtypeStruct((B, T, C), out_dtype),
        compiler_params=_cparams(("parallel", "parallel", "parallel"), VMEM_LIMIT),
    )(*args)


def _dwconv_wgrad(dy, x, col0, C, offs, nct, tm, tc, name):
    B, T, _ = x.shape
    nt = T // tm
    cb0 = col0 // tc
    K = len(offs)

    def body(dy_ref, x_ref, p_ref, n_ref, acc_ref):
        t = pl.program_id(2)
        prev, nxt = _seg_halos(p_ref, n_ref, t, nct, nt)
        x_ = x_ref[0]
        dy_ = dy_ref[0]
        row = lax.broadcasted_iota(jnp.int32, x_.shape, 0)

        @pl.when(t == 0)
        def _():
            acc_ref[...] = jnp.zeros_like(acc_ref)

        for j, off in enumerate(offs):
            acc_ref[0, j:j + 1, :] += jnp.sum(dy_ * _shifted(x_, prev, nxt, off, row, tm), axis=0, keepdims=True)
        acc_ref[0, K:K + 1, :] += jnp.sum(dy_, axis=0, keepdims=True)

    pspec, nspec = _halo_specs(T, tm, tc, cb0, "bjt")
    return pl.pallas_call(
        body, name=name, grid=(B, C // tc, nt),
        in_specs=[pl.BlockSpec((1, tm, tc), lambda b, j, t: (b, t, j)),
                  pl.BlockSpec((1, tm, tc), lambda b, j, t: (b, t, cb0 + j)), pspec, nspec],
        out_specs=pl.BlockSpec((1, SUBLANES, tc), lambda b, j, t: (b, 0, j)),
        out_shape=jax.ShapeDtypeStruct((B, SUBLANES, C), F32),
        compiler_params=_cparams(("parallel", "parallel", "arbitrary"), VMEM_LIMIT),
    )(dy, x, x, x)


def _ab_act(pab, alog, dtb, nd, tm):
    R = pab.shape[0]

    def body(p_ref, al_ref, dt_ref, o_ref):
        p = p_ref[...]
        lane = lax.broadcasted_iota(jnp.int32, p.shape, 1)
        g = -jnp.exp(al_ref[...]) * _softplus(p + dt_ref[...])
        o_ref[...] = jnp.where(lane < nd, g, jnp.where(lane < 2 * nd, _sigmoid(p), 0.0))

    row = pl.BlockSpec((tm, LANES), lambda i: (i, 0))
    vec = pl.BlockSpec((1, LANES), lambda i: (0, 0))
    return pl.pallas_call(
        body, name="ab_act", grid=(R // tm,), in_specs=[row, vec, vec], out_specs=row,
        out_shape=jax.ShapeDtypeStruct((R, LANES), F32),
        compiler_params=_cparams(("parallel",), VMEM_LIMIT))(pab, alog, dtb)


def _ab_act_bwd(pab, gb, dgb, alog, dtb, nd, tm):
    R = pab.shape[0]

    def body(p_ref, gb_ref, d_ref, al_ref, dt_ref, o_ref, acc_ref):
        i = pl.program_id(0)
        p = p_ref[...]
        gb_ = gb_ref[...]
        d_ = d_ref[...]
        lane = lax.broadcasted_iota(jnp.int32, p.shape, 1)
        is_g = lane < nd
        is_b = jnp.logical_and(lane >= nd, lane < 2 * nd)
        da = jnp.where(is_g, d_ * (-jnp.exp(al_ref[...])) * _sigmoid(p + dt_ref[...]), 0.0)
        db = jnp.where(is_b, d_ * gb_ * (1.0 - gb_), 0.0)
        o_ref[...] = (da + db).astype(BF16)

        @pl.when(i == 0)
        def _():
            acc_ref[...] = jnp.zeros_like(acc_ref)

        acc_ref[0:1, :] += jnp.sum(jnp.where(is_g, d_ * gb_, 0.0), axis=0, keepdims=True)
        acc_ref[1:2, :] += jnp.sum(da, axis=0, keepdims=True)

    row = pl.BlockSpec((tm, LANES), lambda i: (i, 0))
    vec = pl.BlockSpec((1, LANES), lambda i: (0, 0))
    return pl.pallas_call(
        body, name="ab_act_bwd", grid=(R // tm,), in_specs=[row, row, row, vec, vec],
        out_specs=(row, pl.BlockSpec((SUBLANES, LANES), lambda i: (0, 0))),
        out_shape=(jax.ShapeDtypeStruct((R, LANES), BF16), jax.ShapeDtypeStruct((SUBLANES, LANES), F32)),
        compiler_params=_cparams(("arbitrary",), VMEM_LIMIT))(pab, gb, dgb, alog, dtb)


def _dn_act(cv, H, hd, tm):
    B, T, C3 = cv.shape
    qscale = float(hd) ** -0.5

    def body(c_ref, o_ref):
        part = pl.program_id(0)
        for h in range(H):
            sl = slice(h * hd, (h + 1) * hd)
            s = _silu(c_ref[0, :, sl])
            rs = lax.rsqrt(jnp.sum(s * s, axis=-1, keepdims=True) + EPS)
            scale = jnp.where(part == 0, rs * qscale, jnp.where(part == 1, rs, 1.0))
            o_ref[0, :, sl] = s * scale

    spec = pl.BlockSpec((1, tm, H * hd), lambda p, b, t: (b, t, p))
    return pl.pallas_call(
        body, name="dn_act", grid=(3, B, T // tm), in_specs=[spec], out_specs=spec,
        out_shape=jax.ShapeDtypeStruct((B, T, C3), F32),
        compiler_params=_cparams(("parallel",) * 3, VMEM_LIMIT))(cv)


def _dn_act_bwd(cv, dqkv, H, hd, tm):
    B, T, C3 = cv.shape
    qscale = float(hd) ** -0.5

    def body(c_ref, d0_ref, d1_ref, o_ref):
        part = pl.program_id(0)
        for h in range(H):
            sl = slice(h * hd, (h + 1) * hd)
            c = c_ref[0, :, sl]
            s = _silu(c)
            dout = d0_ref[0, 0, :, sl] + d1_ref[0, 0, :, sl]
            rs = lax.rsqrt(jnp.sum(s * s, axis=-1, keepdims=True) + EPS)
            sh = s * rs
            dnorm = rs * (dout - sh * jnp.sum(dout * sh, axis=-1, keepdims=True))
            ds = jnp.where(part == 0, dnorm * qscale, jnp.where(part == 1, dnorm, dout))
            o_ref[0, :, sl] = ds * _dsilu(c)

    spec = pl.BlockSpec((1, tm, H * hd), lambda p, b, t: (b, t, p))
    d0 = pl.BlockSpec((1, 1, tm, H * hd), lambda p, b, t: (0, b, t, p))
    d1 = pl.BlockSpec((1, 1, tm, H * hd), lambda p, b, t: (1, b, t, p))
    return pl.pallas_call(
        body, name="dn_act_bwd", grid=(3, B, T // tm), in_specs=[spec, d0, d1], out_specs=spec,
        out_shape=jax.ShapeDtypeStruct((B, T, C3), F32),
        compiler_params=_cparams(("parallel",) * 3, VMEM_LIMIT))(cv, dqkv, dqkv)


def _chunk_of(d, s, ncc, nch):
    return jnp.where(d == 0, s, jnp.where(s < ncc, ncc - 1 - s, nch - 1 - (s - ncc)))


def _delta_masks(d):
    row = lax.broadcasted_iota(jnp.int32, (CHUNK, CHUNK), 0)
    col = lax.broadcasted_iota(jnp.int32, (CHUNK, CHUNK), 1)
    sign = 1 - 2 * d
    diff = (row - col) * sign
    return diff >= 0, diff > 0


def _each(f, *lists):
    return [f(*a) for a in zip(*lists)]


def _unit_tri_inverse(As):
    C = As[0].shape[0]
    row = lax.broadcasted_iota(jnp.int32, (C, C), 0)
    col = lax.broadcasted_iota(jnp.int32, (C, C), 1)
    eye = jnp.where(row == col, 1.0, 0.0).astype(F32)
    ps = [-a for a in As]
    ts = [eye + p for p in ps]
    n = 2
    while n < C:
        ps = _each(lambda p: _dot_hi(p, p), ps)
        ts = _each(lambda t, p: t + _dot_hi(t, p), ts, ps)
        n *= 2
    return ts


def _delta_chunk_fwd(q, k, v, g_i, g_j, beta_i, gl, S, incl, strict):
    D_ = _each(lambda gi, gj: jnp.where(incl, jnp.exp(jnp.where(incl, gi - gj, 0.0)), 0.0), g_i, g_j)
    kb = _each(lambda k_, b: k_ * b, k, beta_i)
    A = _each(lambda kb_, k_, d: jnp.where(strict, _dot(kb_, k_, "nt") * d, 0.0), kb, k, D_)
    P = _each(lambda q_, k_, d: jnp.where(incl, _dot(q_, k_, "nt") * d, 0.0), q, k, D_)
    Tm = _unit_tri_inverse(A)
    gam = _each(jnp.exp, g_i)
    w = _each(lambda t, kb_, g: _dot(t, kb_ * g), Tm, kb, gam)
    u = _each(lambda t, v_, b: _dot(t, v_ * b), Tm, v, beta_i)
    vn = _each(lambda u_, w_, s: u_ - _dot(w_, s), u, w, S)
    qg = _each(lambda q_, g: q_ * g, q, gam)
    o = _each(lambda qg_, s, p, vn_: _dot(qg_, s) + _dot(p, vn_), qg, S, P, vn)
    e_i = _each(lambda gl_, gi: jnp.exp(gl_ - gi), gl, g_i)
    kd = _each(lambda k_, e: k_ * e, k, e_i)
    Gam = _each(jnp.exp, gl)
    S_new = _each(lambda s, g, kd_, vn_: s * g + _dot(kd_, vn_, "tn"), S, Gam, kd, vn)
    return dict(D=D_, kb=kb, A=A, Tm=Tm, gam=gam, w=w, u=u, vn=vn, P=P, qg=qg, o=o, e=e_i, kd=kd, Gam=Gam, S_new=S_new)


def _delta_gb(gbc_ref, gbr_ref, incl, H):
    gbc = gbc_ref[0, 0, 0]
    gbr = gbr_ref[0, 0, 0]
    inclf = incl.astype(BF16)
    gc_col = _dot_mask(inclf, gbc)
    gc_row = _dot_mask(inclf, gbr, "xnt")
    gl = jnp.sum(gbc, axis=0, keepdims=True)
    return gbc, gc_col, gc_row, gl


def _delta_fwd(qkvn, gbc, gbr, H, hd, ncc):
    B, T, _ = qkvn.shape
    nch = T // CHUNK
    HD = H * hd

    def body(q_ref, k_ref, v_ref, gbc_ref, gbr_ref, o_ref, sin_ref, S_ref):
        d = pl.program_id(0)
        s = pl.program_id(2)

        @pl.when(s == 0)
        def _():
            S_ref[...] = jnp.zeros_like(S_ref)

        incl, strict = _delta_masks(d)
        gbc_, gc_col, gc_row, gl = _delta_gb(gbc_ref, gbr_ref, incl, H)
        hs = range(H)
        S = [S_ref[h] for h in hs]
        r = _delta_chunk_fwd([q_ref[0, :, h * hd:(h + 1) * hd] for h in hs],
                             [k_ref[0, :, h * hd:(h + 1) * hd] for h in hs],
                             [v_ref[0, :, h * hd:(h + 1) * hd] for h in hs],
                             [gc_col[:, h:h + 1] for h in hs], [gc_row[h:h + 1, :] for h in hs],
                             [gbc_[:, H + h:H + h + 1] for h in hs], [gl[:, h:h + 1] for h in hs],
                             S, incl, strict)
        for h in hs:
            sin_ref[0, 0, 0, h] = S[h]
            S_ref[h] = r["S_new"][h]
            o_ref[0, 0, :, h * hd:(h + 1) * hd] = r["o"][h]

    cidx = lambda d, b, s: _chunk_of(d, s, ncc, nch)
    qspec = lambda part: pl.BlockSpec((1, CHUNK, HD), lambda d, b, s: (b, cidx(d, b, s), part))
    return pl.pallas_call(
        body, name="delta_fwd", grid=(2, B, nch),
        in_specs=[qspec(0), qspec(1), qspec(2),
                  pl.BlockSpec((1, 1, 1, CHUNK, 2 * H), lambda d, b, s: (b, d, cidx(d, b, s), 0, 0)),
                  pl.BlockSpec((1, 1, 1, 2 * H, CHUNK), lambda d, b, s: (b, d, cidx(d, b, s), 0, 0))],
        out_specs=(pl.BlockSpec((1, 1, CHUNK, HD), lambda d, b, s: (d, b, cidx(d, b, s), 0)),
                   pl.BlockSpec((1, 1, 1, H, hd, hd), lambda d, b, s: (d, b, cidx(d, b, s), 0, 0, 0))),
        out_shape=(jax.ShapeDtypeStruct((2, B, T, HD), F32),
                   jax.ShapeDtypeStruct((2, B, nch, H, hd, hd), F32)),
        scratch_shapes=[pltpu.VMEM((H, hd, hd), F32)],
        compiler_params=_cparams(("parallel", "parallel", "arbitrary"), VMEM_LIMIT),
    )(qkvn, qkvn, qkvn, gbc, gbr)


def _delta_bwd(qkvn, gbc, gbr, sin, do, H, hd, ncc):
    B, T, _ = qkvn.shape
    nch = T // CHUNK
    HD = H * hd

    def body(q_ref, k_ref, v_ref, gbc_ref, gbr_ref, sin_ref, do_ref, dqkv_ref, dgb_ref, dS_ref):
        d = pl.program_id(0)
        s = pl.program_id(2)

        @pl.when(s == 0)
        def _():
            dS_ref[...] = jnp.zeros_like(dS_ref)

        incl, strict = _delta_masks(d)
        gbc_, gc_col, gc_row, gl = _delta_gb(gbc_ref, gbr_ref, incl, H)
        lane = lax.broadcasted_iota(jnp.int32, (1, 2 * H), 1)
        ones = jnp.ones((CHUNK, LANES), BF16)
        dgc = jnp.zeros((CHUNK, 2 * H), F32)
        dgl = jnp.zeros((1, 2 * H), F32)
        hs = range(H)
        q = [q_ref[0, :, h * hd:(h + 1) * hd] for h in hs]
        k = [k_ref[0, :, h * hd:(h + 1) * hd] for h in hs]
        v = [v_ref[0, :, h * hd:(h + 1) * hd] for h in hs]
        do_ = [do_ref[0, :, h * hd:(h + 1) * hd] for h in hs]
        beta_i = [gbc_[:, H + h:H + h + 1] for h in hs]
        S = [sin_ref[0, 0, 0, h] for h in hs]
        dSn = [dS_ref[h] for h in hs]
        f = _delta_chunk_fwd(q, k, v, [gc_col[:, h:h + 1] for h in hs], [gc_row[h:h + 1, :] for h in hs], beta_i,
                             [gl[:, h:h + 1] for h in hs], S, incl, strict)
        rsum = lambda x: jnp.sum(x, axis=1, keepdims=True)
        dvn = _each(lambda p, d_, kd, ds: _dot(p, d_, "tn") + _dot(kd, ds), f["P"], do_, f["kd"], dSn)
        dP = _each(lambda d_, vn: jnp.where(incl, _dot(d_, vn, "nt"), 0.0), do_, f["vn"])
        dqg = _each(lambda d_, s: _dot(d_, s, "nt"), do_, S)
        dS_in = _each(lambda qg, d_, g, ds, w, dv_: _dot(qg, d_, "tn") + g * ds - _dot(w, dv_, "tn"),
                      f["qg"], do_, f["Gam"], dSn, f["w"], dvn)
        dGam = _each(lambda s, ds: jnp.sum(rsum(s * ds), axis=0, keepdims=True), S, dSn)
        dkd = _each(lambda vn, ds: _dot(vn, ds, "nt"), f["vn"], dSn)
        de = _each(lambda dkd_, k_, e: rsum(dkd_ * k_) * e, dkd, k, f["e"])
        dw = _each(lambda dv_, s: -_dot(dv_, s, "nt"), dvn, S)
        dXw = _each(lambda t, dw_: _dot(t, dw_, "tn"), f["Tm"], dw)
        dXu = _each(lambda t, dv_: _dot(t, dv_, "tn"), f["Tm"], dvn)
        dA = _each(lambda xw, w, xu, u: -jnp.where(strict, _dot(xw, w, "nt") + _dot(xu, u, "nt"), 0.0),
                   dXw, f["w"], dXu, f["u"])
        dM = _each(lambda a, d_: a * d_, dA, f["D"])
        dN = _each(lambda p, d_: p * d_, dP, f["D"])
        dkb = _each(lambda m, k_, xw, g: _dot(m, k_) + xw * g, dM, k, dXw, f["gam"])
        dq = _each(lambda dqg_, g, n, k_: dqg_ * g + _dot(n, k_), dqg, f["gam"], dN, k)
        dk = _each(lambda dkd_, e, m, kb, n, q_, dkb_, b: dkd_ * e + _dot(m, kb, "tn") + _dot(n, q_, "tn") + dkb_ * b,
                   dkd, f["e"], dM, f["kb"], dN, q, dkb, beta_i)
        dv = _each(lambda xu, b: xu * b, dXu, beta_i)
        dbeta = _each(lambda dkb_, k_, xu, v_: rsum(dkb_ * k_) + rsum(xu * v_), dkb, k, dXu, v)
        E = _each(lambda a, A_, p, P_: a * A_ + p * P_, dA, f["A"], dP, f["P"])

        def colsum(e):
            e1, e2, e3 = _split3(e)
            return (_dot(e1, ones, "tn") + (_dot(e2, ones, "tn") + _dot(e3, ones, "tn")))[:, 0:1]

        dg = _each(lambda e, dqg_, qg, xw, kb, g, de_: rsum(e) - colsum(e) + rsum(dqg_ * qg) + rsum(xw * kb) * g - de_,
                   E, dqg, f["qg"], dXw, f["kb"], f["gam"], de)
        dgl_h = _each(lambda de_, dg_, g: jnp.sum(de_, axis=0, keepdims=True) + dg_ * g, de, dGam, f["Gam"])
        for h in hs:
            dgc = dgc + dg[h] * (lane == h).astype(F32) + dbeta[h] * (lane == H + h).astype(F32)
            dgl = dgl + dgl_h[h] * (lane == h).astype(F32)
            dS_ref[h] = dS_in[h]
            dqkv_ref[0, 0, :, h * hd:(h + 1) * hd] = dq[h]
            dqkv_ref[0, 0, :, HD + h * hd:HD + (h + 1) * hd] = dk[h]
            dqkv_ref[0, 0, :, 2 * HD + h * hd:2 * HD + (h + 1) * hd] = dv[h]
        draw = _dot_mask(incl.astype(BF16), dgc, "tn") + dgl
        dgb_ref[0, 0, 0] = jnp.where(lane < H, draw, dgc)

    cidx = lambda d, b, s: _chunk_of(d, nch - 1 - s, ncc, nch)
    qspec = lambda part: pl.BlockSpec((1, CHUNK, HD), lambda d, b, s: (b, cidx(d, b, s), part))
    return pl.pallas_call(
        body, name="delta_bwd", grid=(2, B, nch),
        in_specs=[qspec(0), qspec(1), qspec(2),
                  pl.BlockSpec((1, 1, 1, CHUNK, 2 * H), lambda d, b, s: (b, d, cidx(d, b, s), 0, 0)),
                  pl.BlockSpec((1, 1, 1, 2 * H, CHUNK), lambda d, b, s: (b, d, cidx(d, b, s), 0, 0)),
                  pl.BlockSpec((1, 1, 1, H, hd, hd), lambda d, b, s: (d, b, cidx(d, b, s), 0, 0, 0)),
                  pl.BlockSpec((1, CHUNK, HD), lambda d, b, s: (b, cidx(d, b, s), 0))],
        out_specs=(pl.BlockSpec((1, 1, CHUNK, 3 * HD), lambda d, b, s: (d, b, cidx(d, b, s), 0)),
                   pl.BlockSpec((1, 1, 1, CHUNK, 2 * H), lambda d, b, s: (b, d, cidx(d, b, s), 0, 0))),
        out_shape=(jax.ShapeDtypeStruct((2, B, T, 3 * HD), F32),
                   jax.ShapeDtypeStruct((B, 2, nch, CHUNK, 2 * H), F32)),
        scratch_shapes=[pltpu.VMEM((H, hd, hd), F32)],
        compiler_params=_cparams(("parallel", "parallel", "arbitrary"), VMEM_LIMIT),
    )(qkvn, qkvn, qkvn, gbc, gbr, sin, do)


def _dn_out(o2, pz, zcb0, onorm, H, hd, nct, tm):
    _, B, T, HD = o2.shape
    N = T - nct * tm

    def body(o0_ref, o1_ref, z_ref, g_ref, y_ref):
        for h in range(H):
            sl = slice(h * hd, (h + 1) * hd)
            o = o0_ref[0, 0, :, sl] + o1_ref[0, 0, :, sl]
            r = lax.rsqrt(jnp.mean(o * o, axis=-1, keepdims=True) + EPS)
            y_ref[0, :, sl] = (o * r * g_ref[...] * _silu(z_ref[0, :, sl])).astype(BF16)

    return pl.pallas_call(
        body, name="dn_out", grid=(B, N // tm),
        in_specs=[pl.BlockSpec((1, 1, tm, HD), lambda b, t: (0, b, t + nct, 0)),
                  pl.BlockSpec((1, 1, tm, HD), lambda b, t: (1, b, t + nct, 0)),
                  pl.BlockSpec((1, tm, HD), lambda b, t: (b, t + nct, zcb0)),
                  pl.BlockSpec((1, hd), lambda b, t: (0, 0))],
        out_specs=pl.BlockSpec((1, tm, HD), lambda b, t: (b, t, 0)),
        out_shape=jax.ShapeDtypeStruct((B, N, HD), BF16),
        compiler_params=_cparams(("parallel",) * 2, VMEM_LIMIT))(o2, o2, pz, onorm)


def _dn_out_bwd(o2, pz, zcb0, onorm, dy, H, hd, nct, tm):
    _, B, T, HD = o2.shape
    nt = T // tm

    def body(o0_ref, o1_ref, z_ref, g_ref, dy_ref, do_ref, dz_ref, acc_ref):
        t = pl.program_id(1)

        @pl.when(t == 0)
        def _():
            acc_ref[...] = jnp.zeros_like(acc_ref)

        @pl.when(t < nct)
        def _():
            do_ref[0] = jnp.zeros_like(do_ref[0])
            dz_ref[0] = jnp.zeros_like(dz_ref[0])

        @pl.when(t >= nct)
        def _():
            g = g_ref[...]
            for h in range(H):
                sl = slice(h * hd, (h + 1) * hd)
                o = o0_ref[0, 0, :, sl] + o1_ref[0, 0, :, sl]
                z = z_ref[0, :, sl]
                dy_ = dy_ref[0, :, sl]
                r = lax.rsqrt(jnp.mean(o * o, axis=-1, keepdims=True) + EPS)
                oh = o * r
                dz_ref[0, :, sl] = (dy_ * oh * g * _dsilu(z)).astype(BF16)
                dn = dy_ * _silu(z)
                acc_ref[0, 0:1, :] += jnp.sum(dn * oh, axis=0, keepdims=True)
                doh = dn * g
                do_ref[0, :, sl] = r * (doh - oh * jnp.mean(doh * oh, axis=-1, keepdims=True))

    lat = lambda b, t: (b, jnp.maximum(t - nct, 0), 0)
    row = pl.BlockSpec((1, tm, HD), lambda b, t: (b, t, 0))
    return pl.pallas_call(
        body, name="dn_out_bwd", grid=(B, nt),
        in_specs=[pl.BlockSpec((1, 1, tm, HD), lambda b, t: (0, b, t, 0)),
                  pl.BlockSpec((1, 1, tm, HD), lambda b, t: (1, b, t, 0)),
                  pl.BlockSpec((1, tm, HD), lambda b, t: (b, t, zcb0)),
                  pl.BlockSpec((1, hd), lambda b, t: (0, 0)),
                  pl.BlockSpec((1, tm, HD), lat)],
        out_specs=(row, row, pl.BlockSpec((1, SUBLANES, hd), lambda b, t: (b, 0, 0))),
        out_shape=(jax.ShapeDtypeStruct((B, T, HD), F32), jax.ShapeDtypeStruct((B, T, HD), BF16),
                   jax.ShapeDtypeStruct((B, SUBLANES, hd), F32)),
        compiler_params=_cparams(("parallel", "arbitrary"), VMEM_LIMIT),
    )(o2, o2, pz, onorm, dy)


def _lru_gate_math(x, wr, wi, br, bi, lam):
    r = _sigmoid(_dot(x, wr) + br)
    i = _sigmoid(_dot(x, wi) + bi)
    sp = _softplus(-lam)
    la = -LRU_C * r * sp
    a = jnp.exp(la)
    s = jnp.sqrt(-jnp.tanh(la) * (a * a + 1.0))
    return r, i, sp, a, s


def _lru_gates(xc, wbd, bias4, lam, tm, bw):
    B, T, W = xc.shape

    def body(x_ref, w_ref, b_ref, l_ref, a_ref, i_ref):
        x = x_ref[0]
        for d in range(2):
            _, i, _, a, s = _lru_gate_math(x, w_ref[2 * d, 0], w_ref[2 * d + 1, 0],
                                           b_ref[2 * d:2 * d + 1, :], b_ref[2 * d + 1:2 * d + 2, :], l_ref[d:d + 1, :])
            a_ref[d, 0] = a
            i_ref[d, 0] = s * (i * x)

    out = pl.BlockSpec((2, 1, tm, bw), lambda b, t, j: (0, b, t, j))
    return pl.pallas_call(
        body, name="lru_gates", grid=(B, T // tm, W // bw),
        in_specs=[pl.BlockSpec((1, tm, bw), lambda b, t, j: (b, t, j)),
                  pl.BlockSpec((4, 1, bw, bw), lambda b, t, j: (0, j, 0, 0)),
                  pl.BlockSpec((4, bw), lambda b, t, j: (0, j)),
                  pl.BlockSpec((2, bw), lambda b, t, j: (0, j))],
        out_specs=(out, out),
        out_shape=(jax.ShapeDtypeStruct((2, B, T, W), F32), jax.ShapeDtypeStruct((2, B, T, W), F32)),
        compiler_params=_cparams(("parallel",) * 3, VMEM_LIMIT))(xc, wbd, bias4, lam)


def _lru_gates_bwd(xc, wbd, bias4, lam, dinp, da, tm, bw):
    B, T, W = xc.shape
    nt = T // tm

    def body(x_ref, w_ref, b_ref, l_ref, di_ref, da_ref, dx_ref, dw_ref, acc_ref):
        b_ = pl.program_id(1)
        t = pl.program_id(2)

        @pl.when(jnp.logical_and(b_ == 0, t == 0))
        def _():
            dw_ref[...] = jnp.zeros_like(dw_ref)
            acc_ref[...] = jnp.zeros_like(acc_ref)

        x = x_ref[0]
        dx = jnp.zeros_like(x)
        for d in range(2):
            wr, wi = w_ref[2 * d, 0], w_ref[2 * d + 1, 0]
            lam_ = l_ref[d:d + 1, :]
            r, i, sp, a, s = _lru_gate_math(x, wr, wi, b_ref[2 * d:2 * d + 1, :], b_ref[2 * d + 1:2 * d + 2, :], lam_)
            dinp_ = di_ref[d, 0]
            dix = dinp_ * s
            ds = dinp_ * i * x
            dla = da_ref[d, 0] * a - ds * (a * a) / s
            dpr = dla * (-LRU_C * sp) * r * (1.0 - r)
            dpi = dix * x * i * (1.0 - i)
            dx = dx + dix * i + _dot(dpr, wr, "nt") + _dot(dpi, wi, "nt")
            dw_ref[2 * d, 0] += _dot(x, dpr, "tn")
            dw_ref[2 * d + 1, 0] += _dot(x, dpi, "tn")
            acc_ref[2 * d:2 * d + 1, :] += jnp.sum(dpr, axis=0, keepdims=True)
            acc_ref[2 * d + 1:2 * d + 2, :] += jnp.sum(dpi, axis=0, keepdims=True)
            acc_ref[4 + d:5 + d, :] += jnp.sum(dla * (-LRU_C * r), axis=0, keepdims=True) * (-_sigmoid(-lam_))
        dx_ref[0] = dx

    dirs = pl.BlockSpec((2, 1, tm, bw), lambda j, b, t: (0, b, t, j))
    return pl.pallas_call(
        body, name="lru_gates_bwd", grid=(W // bw, B, nt),
        in_specs=[pl.BlockSpec((1, tm, bw), lambda j, b, t: (b, t, j)),
                  pl.BlockSpec((4, 1, bw, bw), lambda j, b, t: (0, j, 0, 0)),
                  pl.BlockSpec((4, bw), lambda j, b, t: (0, j)),
                  pl.BlockSpec((2, bw), lambda j, b, t: (0, j)), dirs, dirs],
        out_specs=(pl.BlockSpec((1, tm, bw), lambda j, b, t: (b, t, j)),
                   pl.BlockSpec((4, 1, bw, bw), lambda j, b, t: (0, j, 0, 0)),
                   pl.BlockSpec((SUBLANES, bw), lambda j, b, t: (0, j))),
        out_shape=(jax.ShapeDtypeStruct((B, T, W), F32), jax.ShapeDtypeStruct(wbd.shape, F32),
                   jax.ShapeDtypeStruct((SUBLANES, W), F32)),
        compiler_params=_cparams(("parallel", "arbitrary", "arbitrary"), VMEM_LIMIT),
    )(xc, wbd, bias4, lam, dinp, da)


def _tile_scan(a, x, rev, tm):
    row = lax.broadcasted_iota(jnp.int32, a.shape, 0)
    s = 1
    while s < tm:
        if rev:
            a_sh, x_sh, ok = pltpu.roll(a, tm - s, 0), pltpu.roll(x, tm - s, 0), row < tm - s
        else:
            a_sh, x_sh, ok = pltpu.roll(a, s, 0), pltpu.roll(x, s, 0), row >= s
        x = jnp.where(ok, x + a * x_sh, x)
        a = jnp.where(ok, a * a_sh, a)
        s *= 2
    return a, x


def _scan_tile_of(s, rev, nct, nt):
    if not rev:
        return s
    return jnp.where(s < nct, nct - 1 - s, nt - 1 - (s - nct))


def _lru_scan(a2, x2, d, nct, tm, tc):
    _, B, T, W = a2.shape
    nt = T // tm
    rev = d == 1
    last = 0 if rev else tm - 1

    def body(a_ref, x_ref, h_ref, carry):
        s = pl.program_id(2)

        @pl.when(s == 0)
        def _():
            carry[...] = jnp.zeros_like(carry)

        A, X = _tile_scan(a_ref[0, 0], x_ref[0, 0], rev, tm)
        h = X + A * carry[0:1, :]
        h_ref[0] = h
        carry[...] = jnp.broadcast_to(h[last:last + 1, :], carry.shape)

    tidx = lambda s: _scan_tile_of(s, rev, nct, nt)
    spec = pl.BlockSpec((1, 1, tm, tc), lambda b, j, s: (d, b, tidx(s), j))
    return pl.pallas_call(
        body, name=f"lru_scan{d}", grid=(B, W // tc, nt),
        in_specs=[spec, spec], out_specs=pl.BlockSpec((1, tm, tc), lambda b, j, s: (b, tidx(s), j)),
        out_shape=jax.ShapeDtypeStruct((B, T, W), F32),
        scratch_shapes=[pltpu.VMEM((SUBLANES, tc), F32)],
        compiler_params=_cparams(("parallel", "parallel", "arbitrary"), VMEM_LIMIT),
    )(a2, x2)


def _lru_scan_bwd(a2, h, dh, d, nct, tm, tc):
    _, B, T, W = a2.shape
    nt = T // tm
    rev = d == 1
    first = tm - 1 if rev else 0
    r8, n8 = tm // SUBLANES, T // SUBLANES

    def body(a_ref, h_ref, hp_ref, dh_ref, lam_ref, da_ref, ca, cl):
        s = pl.program_id(2)

        @pl.when(s == 0)
        def _():
            ca[...] = jnp.zeros_like(ca)
            cl[...] = jnp.zeros_like(cl)

        a = a_ref[0, 0]
        row = lax.broadcasted_iota(jnp.int32, a.shape, 0)
        if rev:
            c = jnp.where(row == 0, ca[0:1, :], pltpu.roll(a, 1, 0))
        else:
            c = jnp.where(row == tm - 1, ca[0:1, :], pltpu.roll(a, tm - 1, 0))
        C, L = _tile_scan(c, dh_ref[0], not rev, tm)
        lam = L + C * cl[0:1, :]
        lam_ref[0] = lam
        hp = jnp.where(s == nt - 1, 0.0, hp_ref[0])
        if rev:
            hprev = jnp.where(row == tm - 1, hp[0:1, :], pltpu.roll(h_ref[0], tm - 1, 0))
        else:
            hprev = jnp.where(row == 0, hp[SUBLANES - 1:SUBLANES, :], pltpu.roll(h_ref[0], 1, 0))
        da_ref[0] = lam * hprev
        ca[...] = jnp.broadcast_to(a[first:first + 1, :], ca.shape)
        cl[...] = jnp.broadcast_to(lam[first:first + 1, :], cl.shape)

    tidx = lambda s: _scan_tile_of(nt - 1 - s, rev, nct, nt)

    def hp_idx(b, j, s):
        tt = tidx(s)
        if rev:
            blk = jnp.where(tt == nt - 1, 0, jnp.minimum((tt + 1) * r8, n8 - 1))
        else:
            blk = jnp.maximum(tt * r8 - 1, 0)
        return (b, blk, j)

    tile = pl.BlockSpec((1, tm, tc), lambda b, j, s: (b, tidx(s), j))
    return pl.pallas_call(
        body, name=f"lru_scan_bwd{d}", grid=(B, W // tc, nt),
        in_specs=[pl.BlockSpec((1, 1, tm, tc), lambda b, j, s: (d, b, tidx(s), j)), tile,
                  pl.BlockSpec((1, SUBLANES, tc), hp_idx), tile],
        out_specs=(tile, tile),
        out_shape=(jax.ShapeDtypeStruct((B, T, W), F32), jax.ShapeDtypeStruct((B, T, W), F32)),
        scratch_shapes=[pltpu.VMEM((SUBLANES, tc), F32), pltpu.VMEM((SUBLANES, tc), F32)],
        compiler_params=_cparams(("parallel", "parallel", "arbitrary"), VMEM_LIMIT),
    )(a2, h, h, dh)


def _lru_out(h0, h1, pyl, ycb0, nct, tm, tc):
    B, T, W = h0.shape
    N = T - nct * tm

    def body(h0_ref, h1_ref, y_ref, o_ref):
        o_ref[0] = ((h0_ref[0] + h1_ref[0]) * _gelu(y_ref[0])).astype(BF16)

    hs = pl.BlockSpec((1, tm, tc), lambda b, t, j: (b, t + nct, j))
    return pl.pallas_call(
        body, name="lru_out", grid=(B, N // tm, W // tc),
        in_specs=[hs, hs, pl.BlockSpec((1, tm, tc), lambda b, t, j: (b, t + nct, ycb0 + j))],
        out_specs=pl.BlockSpec((1, tm, tc), lambda b, t, j: (b, t, j)),
        out_shape=jax.ShapeDtypeStruct((B, N, W), BF16),
        compiler_params=_cparams(("parallel",) * 3, VMEM_LIMIT))(h0, h1, pyl)


def _lru_out_bwd(h0, h1, pyl, ycb0, dy, nct, tm, tc):
    B, T, W = h0.shape

    def body(h0_ref, h1_ref, y_ref, dy_ref, dh_ref, dyl_ref):
        t = pl.program_id(1)

        @pl.when(t < nct)
        def _():
            dh_ref[0] = jnp.zeros_like(dh_ref[0])
            dyl_ref[0] = jnp.zeros_like(dyl_ref[0])

        @pl.when(t >= nct)
        def _():
            y = y_ref[0]
            dy_ = dy_ref[0]
            dh_ref[0] = dy_ * _gelu(y)
            dyl_ref[0] = (dy_ * (h0_ref[0] + h1_ref[0]) * _dgelu(y)).astype(BF16)

    hs = pl.BlockSpec((1, tm, tc), lambda b, t, j: (b, t, j))
    return pl.pallas_call(
        body, name="lru_out_bwd", grid=(B, T // tm, W // tc),
        in_specs=[hs, hs, pl.BlockSpec((1, tm, tc), lambda b, t, j: (b, t, ycb0 + j)),
                  pl.BlockSpec((1, tm, tc), lambda b, t, j: (b, jnp.maximum(t - nct, 0), j))],
        out_specs=(hs, hs),
        out_shape=(jax.ShapeDtypeStruct((B, T, W), F32), jax.ShapeDtypeStruct((B, T, W), BF16)),
        compiler_params=_cparams(("parallel",) * 3, VMEM_LIMIT))(h0, h1, pyl, dy)


def _merge(bd, bl, pmg, bm, nct, tm):
    B, N, D = bd.shape

    def body(bd_ref, bl_ref, m0_ref, m1_ref, b_ref, o_ref):
        g0 = _sigmoid(m0_ref[0] + b_ref[:, 0:D])
        g1 = _sigmoid(m1_ref[0] + b_ref[:, D:2 * D])
        o_ref[0] = (g0 * bd_ref[0] + g1 * bl_ref[0]).astype(BF16)

    row = pl.BlockSpec((1, tm, D), lambda b, t: (b, t, 0))
    return pl.pallas_call(
        body, name="merge", grid=(B, N // tm),
        in_specs=[row, row, pl.BlockSpec((1, tm, D), lambda b, t: (b, t + nct, 0)),
                  pl.BlockSpec((1, tm, D), lambda b, t: (b, t + nct, 1)),
                  pl.BlockSpec((1, 2 * D), lambda b, t: (0, 0))],
        out_specs=row, out_shape=jax.ShapeDtypeStruct((B, N, D), BF16),
        compiler_params=_cparams(("parallel", "parallel"), VMEM_LIMIT))(bd, bl, pmg, pmg, bm)


def _merge_bwd(dmi, bd, bl, pmg, bm, nct, tm):
    B, N, D = bd.shape
    T = pmg.shape[1]

    def body(dm_ref, bd_ref, bl_ref, m0_ref, m1_ref, b_ref, dbd_ref, dbl_ref, dmg_ref, acc_ref):
        t = pl.program_id(1)

        @pl.when(t == 0)
        def _():
            acc_ref[...] = jnp.zeros_like(acc_ref)

        @pl.when(t < nct)
        def _():
            dmg_ref[0] = jnp.zeros_like(dmg_ref[0])

        @pl.when(t >= nct)
        def _():
            dm = dm_ref[0]
            g0 = _sigmoid(m0_ref[0] + b_ref[:, 0:D])
            g1 = _sigmoid(m1_ref[0] + b_ref[:, D:2 * D])
            dbd_ref[0] = (dm * g0).astype(BF16)
            dbl_ref[0] = (dm * g1).astype(BF16)
            d0 = dm * bd_ref[0] * g0 * (1.0 - g0)
            d1 = dm * bl_ref[0] * g1 * (1.0 - g1)
            dmg_ref[0, :, 0:D] = d0.astype(BF16)
            dmg_ref[0, :, D:2 * D] = d1.astype(BF16)
            acc_ref[0, 0:1, 0:D] += jnp.sum(d0, axis=0, keepdims=True)
            acc_ref[0, 0:1, D:2 * D] += jnp.sum(d1, axis=0, keepdims=True)

    lat = pl.BlockSpec((1, tm, D), lambda b, t: (b, jnp.maximum(t - nct, 0), 0))
    return pl.pallas_call(
        body, name="merge_bwd", grid=(B, T // tm),
        in_specs=[lat, lat, lat, pl.BlockSpec((1, tm, D), lambda b, t: (b, t, 0)),
                  pl.BlockSpec((1, tm, D), lambda b, t: (b, t, 1)),
                  pl.BlockSpec((1, 2 * D), lambda b, t: (0, 0))],
        out_specs=(lat, lat, pl.BlockSpec((1, tm, 2 * D), lambda b, t: (b, t, 0)),
                   pl.BlockSpec((1, SUBLANES, 2 * D), lambda b, t: (b, 0, 0))),
        out_shape=(jax.ShapeDtypeStruct((B, N, D), BF16), jax.ShapeDtypeStruct((B, N, D), BF16),
                   jax.ShapeDtypeStruct((B, T, 2 * D), BF16), jax.ShapeDtypeStruct((B, SUBLANES, 2 * D), F32)),
        compiler_params=_cparams(("parallel", "arbitrary"), VMEM_LIMIT))(dmi, bd, bl, pmg, pmg, bm)


def _grid_taps():
    return [((di + 1) * 3 + (dj + 1), di, dj) for di in (-1, 0, 1) for dj in (-1, 0, 1)]


def _grid_views(x, n):
    pos = lax.broadcasted_iota(jnp.int32, x.shape, 0)
    gc = jnp.bitwise_and(pos, GRID_W - 1)
    cols = {0: x,
            -1: jnp.where(gc >= 1, pltpu.roll(x, 1, 0), 0.0),
            1: jnp.where(gc <= GRID_W - 2, pltpu.roll(x, n - 1, 0), 0.0)}
    views = {}
    for dj, xc in cols.items():
        views[(0, dj)] = xc
        views[(-1, dj)] = jnp.where(pos >= GRID_W, pltpu.roll(xc, GRID_W, 0), 0.0)
        views[(1, dj)] = jnp.where(pos < n - GRID_W, pltpu.roll(xc, n - GRID_W, 0), 0.0)
    return views


def _ffn_act(F, w9, b, tc):
    B, N, C2 = F.shape
    Cf = C2 // 2
    ncb = Cf // tc

    def body(g_ref, v_ref, w_ref, b_ref, o_ref):
        xv = _grid_views(g_ref[0], N)
        conv = b_ref[...]
        for k, di, dj in _grid_taps():
            conv = conv + xv[(di, dj)] * w_ref[k:k + 1, :]
        o_ref[0] = (_gelu(conv) * v_ref[0]).astype(BF16)

    return pl.pallas_call(
        body, name="ffn_act", grid=(B, ncb),
        in_specs=[pl.BlockSpec((1, N, tc), lambda b, j: (b, 0, j)),
                  pl.BlockSpec((1, N, tc), lambda b, j: (b, 0, ncb + j)),
                  pl.BlockSpec((9, tc), lambda b, j: (0, j)),
                  pl.BlockSpec((1, tc), lambda b, j: (0, j))],
        out_specs=pl.BlockSpec((1, N, tc), lambda b, j: (b, 0, j)),
        out_shape=jax.ShapeDtypeStruct((B, N, Cf), BF16),
        compiler_params=_cparams(("parallel", "parallel"), VMEM_LIMIT))(F, F, w9, b)


def _ffn_act_bwd(F, w9, b, df, tc):
    B, N, C2 = F.shape
    Cf = C2 // 2
    ncb = Cf // tc

    def body(g_ref, v_ref, w_ref, b_ref, df_ref, dF_ref, acc_ref, dv_s):
        p = pl.program_id(2)

        @pl.when(p == 0)
        def _():
            xv = _grid_views(g_ref[0], N)
            conv = b_ref[...]
            for k, di, dj in _grid_taps():
                conv = conv + xv[(di, dj)] * w_ref[k:k + 1, :]
            df_ = df_ref[0]
            dv_s[...] = (df_ * _gelu(conv)).astype(BF16)
            dc = df_ * v_ref[0] * _dgelu(conv)
            dcv = _grid_views(dc, N)
            dx = None
            for k, di, dj in _grid_taps():
                term = dcv[(-di, -dj)] * w_ref[k:k + 1, :]
                dx = term if dx is None else dx + term
                acc_ref[0, k:k + 1, :] = jnp.sum(dc * xv[(di, dj)], axis=0, keepdims=True)
            acc_ref[0, 9:10, :] = jnp.sum(dc, axis=0, keepdims=True)
            acc_ref[0, 10:16, :] = jnp.zeros((6, tc), F32)
            dF_ref[0] = dx.astype(BF16)

        @pl.when(p == 1)
        def _():
            dF_ref[0] = dv_s[...]

    return pl.pallas_call(
        body, name="ffn_act_bwd", grid=(B, ncb, 2),
        in_specs=[pl.BlockSpec((1, N, tc), lambda b, j, p: (b, 0, j)),
                  pl.BlockSpec((1, N, tc), lambda b, j, p: (b, 0, ncb + j)),
                  pl.BlockSpec((9, tc), lambda b, j, p: (0, j)),
                  pl.BlockSpec((1, tc), lambda b, j, p: (0, j)),
                  pl.BlockSpec((1, N, tc), lambda b, j, p: (b, 0, j))],
        out_specs=(pl.BlockSpec((1, N, tc), lambda b, j, p: (b, 0, j + p * ncb)),
                   pl.BlockSpec((1, 16, tc), lambda b, j, p: (b, 0, j))),
        out_shape=(jax.ShapeDtypeStruct((B, N, C2), BF16), jax.ShapeDtypeStruct((B, 16, Cf), F32)),
        scratch_shapes=[pltpu.VMEM((N, tc), BF16)],
        compiler_params=_cparams(("parallel", "parallel", "arbitrary"), VMEM_LIMIT))(F, F, w9, b, df)


ADAM_ROWS = 512


def _adamw(w, m, v, gparts, name):
    rows, cols = w.shape
    P = gparts.shape[0]
    tr = _row_tile(rows, (P + 14) * 4 * (-(-cols // LANES) * LANES))
    c1 = 1.0 - ADAM_B1 ** ADAM_STEP
    c2 = 1.0 - ADAM_B2 ** ADAM_STEP

    def body(w_ref, m_ref, v_ref, g_ref, go_ref, d_ref, mo_ref, vo_ref):
        g = g_ref[0].astype(F32)
        for p in range(1, P):
            g = g + g_ref[p].astype(F32)
        m_ = ADAM_B1 * m_ref[...] + (1.0 - ADAM_B1) * g
        v_ = ADAM_B2 * v_ref[...] + (1.0 - ADAM_B2) * (g * g)
        go_ref[...] = g
        mo_ref[...] = m_
        vo_ref[...] = v_
        d_ref[...] = -ADAM_LR * ((m_ / c1) / (jnp.sqrt(v_ / c2) + ADAM_EPS) + ADAM_WD * w_ref[...])

    row = pl.BlockSpec((tr, cols), lambda i: (i, 0))
    out = jax.ShapeDtypeStruct((rows, cols), F32)
    return pl.pallas_call(
        body, name=name, grid=(rows // tr,),
        in_specs=[row, row, row, pl.BlockSpec((P, tr, cols), lambda i: (0, i, 0))],
        out_specs=(row, row, row, row), out_shape=(out, out, out, out),
        compiler_params=_cparams(("parallel",), VMEM_LIMIT))(w, m, v, gparts)


def _pack_rows(flat_len):
    rows = -(-flat_len // LANES)
    if rows > ADAM_ROWS:
        rows = -(-rows // ADAM_ROWS) * ADAM_ROWS
    else:
        rows = -(-rows // SUBLANES) * SUBLANES
    return rows


def _pack(arrs, lead=()):
    flat = jnp.concatenate([a.reshape(lead + (-1,)).astype(F32) for a in arrs], axis=-1)
    n = flat.shape[-1]
    rows = _pack_rows(n)
    flat = jnp.pad(flat, [(0, 0)] * len(lead) + [(0, rows * LANES - n)])
    return flat.reshape(lead + (rows, LANES))


def _unpack(buf, shapes):
    flat = buf.reshape(-1)
    out, o = [], 0
    for s in shapes:
        n = math.prod(s)
        out.append(flat[o:o + n].reshape(s))
        o += n
    return out


def _col_shards(full, n_lead):
    C = full.shape[-1]
    x = full.reshape(full.shape[:-1] + (N_DEV, C // N_DEV))
    return jnp.moveaxis(x, -2, 0)


def _from_col_shards(g):
    x = jnp.moveaxis(g, 0, -2)
    return x.reshape(x.shape[:-2] + (x.shape[-2] * x.shape[-1],))


def kernel(x, c, ctx, c_ctx, w_ada, b_ada, g_pre_mix, g_post_mix, g_pre_ffn, g_post_ffn, w_in, b_merge, dn_conv, dn_a_log, dn_dt_bias, dn_onorm, lru_conv, lru_conv_b, lru_w_rg, lru_b_rg, lru_w_ig, lru_b_ig, lru_lambda, w_branch_dn, w_branch_lru, w_out, w_up, ffn_dw, ffn_dw_b, w_down, loss_target, m_c_ctx, m_w_ada, m_b_ada, m_g_pre_mix, m_g_post_mix, m_g_pre_ffn, m_g_post_ffn, m_w_in, m_b_merge, m_dn_conv, m_dn_a_log, m_dn_dt_bias, m_dn_onorm, m_lru_conv, m_lru_conv_b, m_lru_w_rg, m_lru_b_rg, m_lru_w_ig, m_lru_b_ig, m_lru_lambda, m_w_branch_dn, m_w_branch_lru, m_w_out, m_w_up, m_ffn_dw, m_ffn_dw_b, m_w_down, v_c_ctx, v_w_ada, v_b_ada, v_g_pre_mix, v_g_post_mix, v_g_pre_ffn, v_g_post_ffn, v_w_in, v_b_merge, v_dn_conv, v_dn_a_log, v_dn_dt_bias, v_dn_onorm, v_lru_conv, v_lru_conv_b, v_lru_w_rg, v_lru_b_rg, v_lru_w_ig, v_lru_b_ig, v_lru_lambda, v_w_branch_dn, v_w_branch_lru, v_w_out, v_w_up, v_ffn_dw, v_ffn_dw_b, v_w_down):
    params = dict(c_ctx=c_ctx, w_ada=w_ada, b_ada=b_ada, g_pre_mix=g_pre_mix, g_post_mix=g_post_mix, g_pre_ffn=g_pre_ffn, g_post_ffn=g_post_ffn, w_in=w_in, b_merge=b_merge, dn_conv=dn_conv, dn_a_log=dn_a_log, dn_dt_bias=dn_dt_bias, dn_onorm=dn_onorm, lru_conv=lru_conv, lru_conv_b=lru_conv_b, lru_w_rg=lru_w_rg, lru_b_rg=lru_b_rg, lru_w_ig=lru_w_ig, lru_b_ig=lru_b_ig, lru_lambda=lru_lambda, w_branch_dn=w_branch_dn, w_branch_lru=w_branch_lru, w_out=w_out, w_up=w_up, ffn_dw=ffn_dw, ffn_dw_b=ffn_dw_b, w_down=w_down)
    mom1 = dict(c_ctx=m_c_ctx, w_ada=m_w_ada, b_ada=m_b_ada, g_pre_mix=m_g_pre_mix, g_post_mix=m_g_post_mix, g_pre_ffn=m_g_pre_ffn, g_post_ffn=m_g_post_ffn, w_in=m_w_in, b_merge=m_b_merge, dn_conv=m_dn_conv, dn_a_log=m_dn_a_log, dn_dt_bias=m_dn_dt_bias, dn_onorm=m_dn_onorm, lru_conv=m_lru_conv, lru_conv_b=m_lru_conv_b, lru_w_rg=m_lru_w_rg, lru_b_rg=m_lru_b_rg, lru_w_ig=m_lru_w_ig, lru_b_ig=m_lru_b_ig, lru_lambda=m_lru_lambda, w_branch_dn=m_w_branch_dn, w_branch_lru=m_w_branch_lru, w_out=m_w_out, w_up=m_w_up, ffn_dw=m_ffn_dw, ffn_dw_b=m_ffn_dw_b, w_down=m_w_down)
    mom2 = dict(c_ctx=v_c_ctx, w_ada=v_w_ada, b_ada=v_b_ada, g_pre_mix=v_g_pre_mix, g_post_mix=v_g_post_mix, g_pre_ffn=v_g_pre_ffn, g_post_ffn=v_g_post_ffn, w_in=v_w_in, b_merge=v_b_merge, dn_conv=v_dn_conv, dn_a_log=v_dn_a_log, dn_dt_bias=v_dn_dt_bias, dn_onorm=v_dn_onorm, lru_conv=v_lru_conv, lru_conv_b=v_lru_conv_b, lru_w_rg=v_lru_w_rg, lru_b_rg=v_lru_b_rg, lru_w_ig=v_lru_w_ig, lru_b_ig=v_lru_b_ig, lru_lambda=v_lru_lambda, w_branch_dn=v_w_branch_dn, w_branch_lru=v_w_branch_lru, w_out=v_w_out, w_up=v_w_up, ffn_dw=v_ffn_dw, ffn_dw_b=v_ffn_dw_b, w_down=v_w_down)
    names = list(params)

    B, N, D = x.shape
    NC = ctx.shape[1]
    T = NC + N
    H, hd = dn_a_log.shape[2], dn_onorm.shape[1]
    HD = H * hd
    nd = 2 * H
    W = lru_conv_b.shape[1]
    nblk, bdim = lru_w_rg.shape[2], lru_w_rg.shape[3]
    Cf = ffn_dw_b.shape[1]
    sw = w_ada.shape[2]
    tm = min(256, NC)
    nct = NC // tm
    ncc = NC // CHUNK
    nch = T // CHUNK
    R, RL = B * T, B * N
    bw = min(256, W)
    me = _my_index()
    assert NC % tm == 0 and N % tm == 0 and B + 1 <= SUBLANES and bw % bdim == 0 and D % LANES == 0

    cpad = jnp.zeros((SUBLANES, D), F32).at[:B].set(c).at[B].set(c_ctx)
    ccp = _all_gather([cpad], "ag_cond")[0].reshape(N_DEV * SUBLANES, D)
    mods_sh = _ada_fwd(ccp, w_ada[0], lax.dynamic_slice(b_ada, (0, me * sw), (1, sw)))
    lru_vecs = jnp.concatenate([lru_b_rg[0], lru_b_ig[0], lru_lambda[0], jnp.zeros_like(lru_lambda[0])], axis=0)
    gathered = _all_gather(
        [mods_sh, w_in[0].astype(BF16).T, w_up[0].astype(BF16), w_down[0].astype(BF16), w_branch_dn[0].astype(BF16),
         w_branch_lru[0].astype(BF16), w_out[0].astype(BF16), dn_conv[0], lru_conv[0], lru_vecs, ffn_dw[0].reshape(9, -1)],
        "ag_weights")
    mods_g, w_in_g, w_up_g, w_down_g, w_bdn_g, w_blru_g, w_out_g, dn_conv_g, lru_conv_g, lru_vecs_g, ffn_dw_g = gathered
    mine = lax.dynamic_slice(mods_g, (0, me * SUBLANES, 0), (N_DEV, SUBLANES, sw))
    mods = jnp.moveaxis(mine, 0, 1).reshape(SUBLANES, 6, D)[:B + 1]
    mods = jnp.pad(mods, ((0, 0), (0, SUBLANES - 6), (0, 0)))
    w_down_f = w_down_g.reshape(Cf, D)
    w_bdn_f, w_blru_f, w_out_f = w_bdn_g.reshape(HD, D), w_blru_g.reshape(W, D), w_out_g.reshape(D, D)
    dn_conv_f = _from_col_shards(dn_conv_g)
    lru_conv_f = _from_col_shards(lru_conv_g)
    lru_vecs_f = _from_col_shards(lru_vecs_g)
    lru_lam_f = lru_vecs_f[4:6]
    ffn_dw_f = _from_col_shards(ffn_dw_g)

    csh = w_in_g.shape[1]
    w_in_t = w_in_g.reshape(N_DEV * csh, D)
    o_z, o_a, o_xl = 3 * HD, 4 * HD, 4 * HD + 2 * nd
    o_yl, o_mg = o_xl + W, o_xl + 2 * W
    w_qkv, w_z = w_in_t[:o_z], w_in_t[o_z:o_a]
    w_ab = jnp.pad(w_in_t[o_a:o_xl], ((0, LANES - 2 * nd), (0, 0)))
    w_xl, w_yl, w_mg = w_in_t[o_xl:o_yl], w_in_t[o_yl:o_mg], w_in_t[o_mg:]

    def blockdiag(wg):
        per = bw // bdim
        g5 = wg.reshape(2, W // bw, per, bdim, bdim)
        eye = jnp.eye(per, dtype=wg.dtype)
        return jnp.einsum("dtpij,pq->dtpiqj", g5, eye).reshape(2, W // bw, bw, bw)

    bd_rg, bd_ig = blockdiag(lru_w_rg[0]), blockdiag(lru_w_ig[0])
    wbd = jnp.stack([bd_rg[0], bd_ig[0], bd_rg[1], bd_ig[1]]).astype(BF16)
    bias4 = jnp.stack([lru_vecs_f[0], lru_vecs_f[2], lru_vecs_f[1], lru_vecs_f[3]])
    alog_v = jnp.pad(dn_a_log.reshape(1, nd), ((0, 0), (0, LANES - nd)))
    dtb_v = jnp.pad(dn_dt_bias.reshape(1, nd), ((0, 0), (0, LANES - nd)))

    h0 = jnp.concatenate([ctx, x], axis=1)
    u1 = _norm_mod_fwd(h0, g_pre_mix, mods, 0, 1, nct, tm, "norm_mod1")
    u1f = u1.reshape(R, D)
    p_qkv = _mm(u1f, w_qkv, "nt", "mm_qkv").reshape(B, T, 3 * HD)
    p_z = _mm(u1f, w_z, "nt", "mm_z").reshape(B, T, HD)
    p_ab = _mm(u1f, w_ab, "nt", "mm_ab")
    p_xl = _mm(u1f, w_xl, "nt", "mm_xl").reshape(B, T, W)
    p_yl = _mm(u1f, w_yl, "nt", "mm_yl").reshape(B, T, W)
    p_mg = _mm(u1f, w_mg, "nt", "mm_mg").reshape(B, T, 2 * D)

    conv_offs = (-2, -1, 0, 1)
    tcc = _tile(HD, 512)
    gb = _ab_act(p_ab, alog_v, dtb_v, nd, tm)
    gb5 = gb.reshape(B, nch, CHUNK, LANES)
    g_ = gb5[..., :nd].reshape(B, nch, CHUNK, 2, H)
    be_ = gb5[..., nd:2 * nd].reshape(B, nch, CHUNK, 2, H)
    gbc = jnp.concatenate([g_, be_], axis=-1).transpose(0, 3, 1, 2, 4)
    gbr = gbc.transpose(0, 1, 2, 4, 3)
    cv = _dwconv(p_qkv, 0, 3 * HD, dn_conv_f, None, conv_offs, nct, tm, tcc, "dn_conv")
    qkvn = _dn_act(cv, H, hd, tm)
    o2, s_in = _delta_fwd(qkvn, gbc, gbr, H, hd, ncc)
    y_dn = _dn_out(o2, p_z, 0, dn_onorm, H, hd, nct, tm)

    tcw = _tile(W, 512)
    xc = _dwconv(p_xl, 0, W, lru_conv_f, lru_conv_b, conv_offs, nct, tm, tcw, "lru_conv")
    a2, inp2 = _lru_gates(xc, wbd, bias4, lru_lam_f, tm, bw)
    hd0 = _lru_scan(a2, inp2, 0, nct, tm, tcw)
    hd1 = _lru_scan(a2, inp2, 1, nct, tm, tcw)
    y_lru = _lru_out(hd0, hd1, p_yl, 0, nct, tm, tcw)

    bd = _mm(y_dn.reshape(RL, HD), w_bdn_f, "nn", "mm_bdn").reshape(B, N, D)
    bl = _mm(y_lru.reshape(RL, W), w_blru_f, "nn", "mm_blru").reshape(B, N, D)
    mixin = _merge(bd, bl, p_mg, b_merge, nct, tm)
    mix = _mm(mixin.reshape(RL, D), w_out_f, "nn", "mm_out").reshape(B, N, D)
    h1 = _resid_norm_fwd(x, mix, g_post_mix, mods, 2, tm, "resid_norm1")
    u2 = _norm_mod_fwd(h1, g_pre_ffn, mods, 3, 4, 0, tm, "norm_mod2")
    Fp = _mm(u2.reshape(RL, D), w_up_g, "nn", "mm_up").reshape(B, N, 2 * Cf)
    tcf = LANES
    f = _ffn_act(Fp, ffn_dw_f, ffn_dw_b, tcf)
    dproj = _mm(f.reshape(RL, Cf), w_down_f, "nn", "mm_down").reshape(B, N, D)

    dd, dh2, acc_f = _final_fwd_bwd(h1, dproj, g_post_ffn, mods, 5, loss_target, tm)
    loss = lax.psum((0.5 / D) * jnp.sum(acc_f[:, 2, :]), MESH_AXES)
    ddf = dd.reshape(RL, D)
    df = _mm(ddf, w_down_f, "nt", "mm_down_dx").reshape(B, N, Cf)
    gw_down = _mm(f.reshape(RL, Cf), ddf, "tn", "mm_down_dw", out_dtype=BF16)
    dF, acc_ffn = _ffn_act_bwd(Fp, ffn_dw_f, ffn_dw_b, df, tcf)
    dFf = dF.reshape(RL, 2 * Cf)
    du2 = _mm(dFf, w_up_g, "nt", "mm_up_dx").reshape(B, N, D)
    gw_up = _mm(u2.reshape(RL, D), dFf, "tn", "mm_up_dw", out_dtype=BF16, out_shards=N_DEV)
    dh1, acc2x, _ = _norm_mod_bwd(h1, du2, g_pre_ffn, mods, 4, 0, tm, dh2, "norm_mod2_bwd")
    dmix, acc_r1 = _resid_norm_bwd(mix, dh1, g_post_mix, mods, 2, tm, "resid_norm1_bwd")
    dmixf = dmix.reshape(RL, D)
    dmixin = _mm(dmixf, w_out_f, "nt", "mm_out_dx").reshape(B, N, D)
    gw_out = _mm(mixin.reshape(RL, D), dmixf, "tn", "mm_out_dw", out_dtype=BF16)
    dbd, dbl, dmg, acc_mg = _merge_bwd(dmixin, bd, bl, p_mg, b_merge, nct, tm)
    dy_dn = _mm(dbd.reshape(RL, D), w_bdn_f, "nt", "mm_bdn_dx").reshape(B, N, HD)
    gw_bdn = _mm(y_dn.reshape(RL, HD), dbd.reshape(RL, D), "tn", "mm_bdn_dw", out_dtype=BF16)
    dy_lru = _mm(dbl.reshape(RL, D), w_blru_f, "nt", "mm_blru_dx").reshape(B, N, W)
    gw_blru = _mm(y_lru.reshape(RL, W), dbl.reshape(RL, D), "tn", "mm_blru_dw", out_dtype=BF16)

    dh, dyl = _lru_out_bwd(hd0, hd1, p_yl, 0, dy_lru, nct, tm, tcw)
    lam0, da0 = _lru_scan_bwd(a2, hd0, dh, 0, nct, tm, tcw)
    lam1, da1 = _lru_scan_bwd(a2, hd1, dh, 1, nct, tm, tcw)
    dxc, dwbd, acc_lru = _lru_gates_bwd(xc, wbd, bias4, lru_lam_f, jnp.stack([lam0, lam1]), jnp.stack([da0, da1]), tm, bw)
    dxl = _dwconv(dxc, 0, W, lru_conv_f, None, tuple(-o for o in conv_offs), nct, tm, tcw, "lru_conv_dx", BF16)
    acc_lc = _dwconv_wgrad(dxc, p_xl, 0, W, conv_offs, nct, tm, tcw, "lru_conv_dw")

    do, dz, acc_on = _dn_out_bwd(o2, p_z, 0, dn_onorm, dy_dn, H, hd, nct, tm)
    dqkvn, dgb = _delta_bwd(qkvn, gbc, gbr, s_in, do, H, hd, ncc)
    dcv = _dn_act_bwd(cv, dqkvn, H, hd, tm)
    dqkv = _dwconv(dcv, 0, 3 * HD, dn_conv_f, None, tuple(-o for o in conv_offs), nct, tm, tcc, "dn_conv_dx", BF16)
    acc_dc = _dwconv_wgrad(dcv, p_qkv, 0, 3 * HD, conv_offs, nct, tm, tcc, "dn_conv_dw")
    dgb_t = dgb.transpose(0, 2, 3, 1, 4)
    dgb_l = jnp.concatenate([dgb_t[..., :H].reshape(B, nch, CHUNK, nd), dgb_t[..., H:].reshape(B, nch, CHUNK, nd)], axis=-1)
    dgb_l = jnp.pad(dgb_l.reshape(R, 2 * nd), ((0, 0), (0, LANES - 2 * nd)))
    dp_ab, acc_ab = _ab_act_bwd(p_ab, gb, dgb_l, alog_v, dtb_v, nd, tm)

    groups = [(dqkv.reshape(R, 3 * HD), w_qkv, "qkv"), (dz.reshape(R, HD), w_z, "z"), (dp_ab, w_ab, "ab"),
              (dxl.reshape(R, W), w_xl, "xl"), (dyl.reshape(R, W), w_yl, "yl"), (dmg.reshape(R, 2 * D), w_mg, "mg")]
    du1 = None
    gw_groups = []
    for dg_, wg_, nm in groups:
        du1 = _mm(dg_, wg_, "nn", "mm_in_dx_" + nm, add=du1)
        gw_groups.append(_mm(dg_, u1f, "tn", "mm_in_dw_" + nm, out_dtype=BF16))
    gw_groups[2] = gw_groups[2][:2 * nd]
    gw_in = jnp.concatenate(gw_groups, axis=0).reshape(N_DEV, csh, D)
    grad_x, acc1x, acc1c = _norm_mod_bwd(h0, du1.reshape(B, T, D), g_pre_mix, mods, 1, nct, tm, dh1, "norm_mod1_bwd")

    zeros_d = jnp.zeros((B, D), F32)
    dm_rows = jnp.stack([acc1x[:, 0], acc1x[:, 1], acc_r1[:, 0], acc2x[:, 0], acc2x[:, 1], acc_f[:, 0]], axis=1)
    dm_ctx = jnp.stack([jnp.sum(acc1c[:, 0], 0), jnp.sum(acc1c[:, 1], 0)] + [zeros_d[0]] * 4, axis=0)[None]
    dm_pad = jnp.zeros((SUBLANES, 6 * D), F32).at[:B + 1].set(jnp.concatenate([dm_rows, dm_ctx], 0).reshape(B + 1, 6 * D))
    dm_g = _all_gather([dm_pad], "ag_dmods")[0]
    g_mine = lax.dynamic_slice(dm_g, (0, 0, me * sw), (N_DEV, SUBLANES, sw)).reshape(N_DEV * SUBLANES, sw)
    gw_ada, dcc = _ada_bwd(ccp, w_ada[0], g_mine, c_ctx.reshape(1, D), B)

    per = bw // bdim

    def diag_blocks(dwf):
        g6 = dwf.reshape(W // bw, per, bdim, per, bdim)
        return jnp.einsum("tpiqj,pq->tpij", g6, jnp.eye(per, dtype=F32)).reshape(nblk, bdim, bdim)

    g_rep = dict(
        c_ctx=dcc[0],
        g_pre_mix=jnp.sum(acc1x[:, 2] + acc1c[:, 2], 0)[None], g_post_mix=jnp.sum(acc_r1[:, 1], 0)[None],
        g_pre_ffn=jnp.sum(acc2x[:, 2], 0)[None], g_post_ffn=jnp.sum(acc_f[:, 1], 0)[None],
        b_merge=jnp.sum(acc_mg[:, 0], 0)[None],
        dn_a_log=acc_ab[0, :nd].reshape(1, 2, H), dn_dt_bias=acc_ab[1, :nd].reshape(1, 2, H),
        dn_onorm=jnp.sum(acc_on[:, 0], 0)[None],
        lru_conv_b=jnp.sum(acc_lc[:, 4], 0)[None],
        lru_w_rg=jnp.stack([diag_blocks(dwbd[0]), diag_blocks(dwbd[2])])[None],
        lru_w_ig=jnp.stack([diag_blocks(dwbd[1]), diag_blocks(dwbd[3])])[None],
        ffn_dw_b=jnp.sum(acc_ffn[:, 9], 0)[None])
    rep_names = list(g_rep)
    rep_parts = _all_gather([_pack([g_rep[n] for n in rep_names])], "ag_rep_grads")[0]
    rep_out = _adamw(_pack([params[n] for n in rep_names]), _pack([mom1[n] for n in rep_names]),
                     _pack([mom2[n] for n in rep_names]), rep_parts, "adamw_rep")
    results = {}
    for k, buf in enumerate(rep_out):
        for n, arr in zip(rep_names, _unpack(buf, [params[n].shape for n in rep_names])):
            results.setdefault(n, [None] * 4)[k] = arr

    ba_out = _adamw(_pack([b_ada]), _pack([m_b_ada]), _pack([v_b_ada]),
                    _pack([dm_g.reshape(N_DEV * SUBLANES, 6 * D)], lead=(N_DEV * SUBLANES,)), "adamw_b_ada")
    results["b_ada"] = [_unpack(buf, [b_ada.shape])[0] for buf in ba_out]

    wa_out = _adamw(w_ada[0], m_w_ada[0], v_w_ada[0], gw_ada[None], "adamw_w_ada")
    results["w_ada"] = [buf[None] for buf in wa_out]

    g_lru_conv = jnp.sum(acc_lc[:, :4], 0)
    g_dn_conv = jnp.sum(acc_dc[:, :4], 0)
    g_ffn_dw = jnp.sum(acc_ffn[:, :9], 0).reshape(3, 3, Cf)
    sm_names = ["dn_conv", "lru_conv", "lru_b_rg", "lru_b_ig", "lru_lambda", "ffn_dw"]
    sm_parts = [_col_shards(g_dn_conv, 1), _col_shards(g_lru_conv, 1),
                _col_shards(jnp.stack([acc_lru[0], acc_lru[2]]), 1), _col_shards(jnp.stack([acc_lru[1], acc_lru[3]]), 1),
                _col_shards(acc_lru[4:6], 1), _col_shards(g_ffn_dw, 2)]
    big_names = ["w_in", "w_up", "w_down", "w_branch_dn", "w_branch_lru", "w_out"]
    parts8 = [gw_in, gw_up, gw_down.reshape(N_DEV, Cf // N_DEV, D), gw_bdn.reshape(N_DEV, HD // N_DEV, D),
              gw_blru.reshape(N_DEV, W // N_DEV, D), gw_out.reshape(N_DEV, D // N_DEV, D),
              _pack(sm_parts, lead=(N_DEV,))]
    sib4 = _pair_exchange(parts8, "rs_pair")
    pair4 = [_pair_sum(p8, s4, "rs_pair_sum_" + n) for p8, s4, n in zip(parts8, sib4, big_names + ["small"])]
    recv4 = _quad_exchange(pair4, "rs_quad")
    for n, g4 in zip(big_names, recv4[:-1]):
        if n == "w_in":
            g4 = jnp.swapaxes(g4, 1, 2)
        out4 = _adamw(params[n][0], mom1[n][0], mom2[n][0], g4, "adamw_" + n)
        results[n] = [buf[None] for buf in out4]
    sm_out = _adamw(_pack([params[n] for n in sm_names]), _pack([mom1[n] for n in sm_names]),
                    _pack([mom2[n] for n in sm_names]), recv4[-1], "adamw_small")
    for k, buf in enumerate(sm_out):
        for n, arr in zip(sm_names, _unpack(buf, [params[n].shape for n in sm_names])):
            results.setdefault(n, [None] * 4)[k] = arr

    outs = [loss, grad_x]
    for k in range(4):
        outs += [results[n][k] for n in names]
    return tuple(outs)
```

```python
import functools
import math

import jax
import jax.numpy as jnp
from jax import lax
from jax.experimental import pallas as pl
from jax.experimental.pallas import tpu as pltpu

F32 = jnp.float32
BF16 = jnp.bfloat16

EPS = 1e-6
GRID_W = 64
GRID_SHIFT = 6
CHUNK = 64
LRU_C = 8.0
N_DEV = 8
ADAM_LR = 0.001
ADAM_B1 = 0.9
ADAM_B2 = 0.999
ADAM_EPS = 1e-08
ADAM_WD = 0.01
ADAM_STEP = 10

LANES = 128
SUBLANES = 8
VMEM_LIMIT = 48 * 1024 * 1024
MESH_AXES = ("x", "y", "c")
GELU_C = math.sqrt(2.0 / math.pi)


def _cparams(sem=None, vmem=None):
    kw = {}
    if sem is not None:
        kw["dimension_semantics"] = sem
    if vmem is not None:
        kw["vmem_limit_bytes"] = vmem
    return pltpu.CompilerParams(**kw)


def _tile(n, pref):
    if n <= pref:
        return n
    t = (pref // LANES) * LANES
    while n % t:
        t -= LANES
    return t


def _sigmoid(x):
    return 1.0 / (1.0 + jnp.exp(-x))


def _silu(x):
    return x * _sigmoid(x)


def _dsilu(x):
    s = _sigmoid(x)
    return s * (1.0 + x * (1.0 - s))


def _softplus(x):
    return jnp.maximum(x, 0.0) + jnp.log(1.0 + jnp.exp(-jnp.abs(x)))


def _gelu(x):
    return 0.5 * x * (1.0 + jnp.tanh(GELU_C * (x + 0.044715 * x * x * x)))


def _dgelu(x):
    t = jnp.tanh(GELU_C * (x + 0.044715 * x * x * x))
    return 0.5 * (1.0 + t) + 0.5 * x * (1.0 - t * t) * GELU_C * (1.0 + 3.0 * 0.044715 * x * x)


def _dot(a, b, dims="nn"):
    dn = {"nn": (((1,), (0,)), ((), ())),
          "nt": (((1,), (1,)), ((), ())),
          "tn": (((0,), (0,)), ((), ()))}[dims]
    return lax.dot_general(a.astype(BF16), b.astype(BF16), dn, preferred_element_type=F32)


def _split2(x):
    hi = x.astype(BF16)
    lo = (x - hi.astype(F32)).astype(BF16)
    return hi, lo


def _split3(x):
    h1 = x.astype(BF16)
    r1 = x - h1.astype(F32)
    h2 = r1.astype(BF16)
    h3 = (r1 - h2.astype(F32)).astype(BF16)
    return h1, h2, h3


def _dot_hi(a, b):
    ah, al = _split2(a)
    bh, bl = _split2(b)
    return _dot(ah, bh) + (_dot(ah, bl) + _dot(al, bh))


def _dot_mask(m, x, dims="nn"):
    h1, h2, h3 = _split3(x)
    if dims == "xtn":
        return _dot(h1, m, "tn") + (_dot(h2, m, "tn") + _dot(h3, m, "tn"))
    return _dot(m, h1, dims) + (_dot(m, h2, dims) + _dot(m, h3, dims))


def _my_index():
    return 4 * lax.axis_index("x") + 2 * lax.axis_index("y") + lax.axis_index("c")


def _hbm_call(body, xs, out_shapes, n_sems, name):
    any_spec = pl.BlockSpec(memory_space=pl.ANY)
    return pl.pallas_call(
        body, name=name, out_shape=tuple(out_shapes),
        in_specs=[any_spec] * len(xs), out_specs=tuple([any_spec] * len(out_shapes)),
        scratch_shapes=[pltpu.SemaphoreType.DMA((n_sems,)), pltpu.SemaphoreType.DMA((n_sems,)),
                        pltpu.SemaphoreType.DMA((len(xs),))],
    )(*xs)


def _all_gather(xs, name):
    n = len(xs)

    def body(*refs):
        x_refs, out_refs = refs[:n], refs[n:2 * n]
        send_sems, recv_sems, local_sems = refs[2 * n:]
        x_, y_, c_ = lax.axis_index("x"), lax.axis_index("y"), lax.axis_index("c")
        me, sibling = (x_, y_, c_), (x_, y_, 1 - c_)
        chips = [(1 - x_, y_), (x_, 1 - y_), (1 - x_, 1 - y_)]

        def slab(a, px, py, pc):
            return out_refs[a].at[4 * px + 2 * py + pc]

        def copy(a, k, block, to, src=None):
            return pltpu.make_async_remote_copy(
                src_ref=slab(a, *block) if src is None else src, dst_ref=slab(a, *block),
                send_sem=send_sems.at[7 * a + k], recv_sem=recv_sems.at[7 * a + k],
                device_id=to, device_id_type=pl.DeviceIdType.MESH)

        mine = [pltpu.make_async_copy(x_refs[a], slab(a, *me), local_sems.at[a]) for a in range(n)]
        for cp in mine:
            cp.start()
        sent = []
        for j, chip in enumerate(chips):
            sent += [copy(a, 1 + j, me, (*chip, c_), src=x_refs[a]) for a in range(n)]
        sent += [copy(a, 0, me, sibling, src=x_refs[a]) for a in range(n)]
        for cp in sent:
            cp.start()
        for j, chip in enumerate(chips):
            for a in range(n):
                copy(a, 1 + j, (*chip, c_), me).wait_recv()
                fwd = copy(a, 4 + j, (*chip, c_), sibling)
                fwd.start()
                sent.append(fwd)
        for a in range(n):
            copy(a, 0, sibling, me).wait_recv()
        for j, chip in enumerate(chips):
            for a in range(n):
                copy(a, 4 + j, (*chip, 1 - c_), me).wait_recv()
        for cp in sent:
            cp.wait_send()
        for cp in mine:
            cp.wait()

    outs = [jax.ShapeDtypeStruct((N_DEV,) + x.shape, x.dtype) for x in xs]
    return _hbm_call(body, xs, outs, 7 * n, name)


def _pair_exchange(xs, name):
    n = len(xs)

    def body(*refs):
        x_refs, out_refs = refs[:n], refs[n:2 * n]
        send_sems, recv_sems, _ = refs[2 * n:]
        x_, y_, c_ = lax.axis_index("x"), lax.axis_index("y"), lax.axis_index("c")
        copies = []
        for a in range(n):
            for j in range(4):
                copies.append(pltpu.make_async_remote_copy(
                    src_ref=x_refs[a].at[2 * j + (1 - c_)], dst_ref=out_refs[a].at[j],
                    send_sem=send_sems.at[4 * a + j], recv_sem=recv_sems.at[4 * a + j],
                    device_id=(x_, y_, 1 - c_), device_id_type=pl.DeviceIdType.MESH))
        for cp in copies:
            cp.start()
        for cp in copies:
            cp.wait()

    outs = [jax.ShapeDtypeStruct((4,) + x.shape[1:], x.dtype) for x in xs]
    return _hbm_call(body, xs, outs, 4 * n, name)


def _quad_exchange(xs, name):
    n = len(xs)

    def body(*refs):
        x_refs, out_refs = refs[:n], refs[n:2 * n]
        send_sems, recv_sems, local_sems = refs[2 * n:]
        x_, y_, c_ = lax.axis_index("x"), lax.axis_index("y"), lax.axis_index("c")
        jm = 2 * x_ + y_
        mine = [pltpu.make_async_copy(x_refs[a].at[jm], out_refs[a].at[jm], local_sems.at[a]) for a in range(n)]
        for cp in mine:
            cp.start()
        sends, recvs = [], []
        for f in range(1, 4):
            px = 1 - x_ if (f >> 1) & 1 else x_
            py = 1 - y_ if f & 1 else y_
            jp = 2 * px + py
            for a in range(n):
                k = 3 * a + f - 1
                sends.append(pltpu.make_async_remote_copy(
                    src_ref=x_refs[a].at[jp], dst_ref=out_refs[a].at[jm],
                    send_sem=send_sems.at[k], recv_sem=recv_sems.at[k],
                    device_id=(px, py, c_), device_id_type=pl.DeviceIdType.MESH))
                recvs.append(pltpu.make_async_remote_copy(
                    src_ref=x_refs[a].at[jp], dst_ref=out_refs[a].at[jp],
                    send_sem=send_sems.at[k], recv_sem=recv_sems.at[k],
                    device_id=(px, py, c_), device_id_type=pl.DeviceIdType.MESH))
        for cp in sends:
            cp.start()
        for cp in sends:
            cp.wait_send()
        for cp in recvs:
            cp.wait_recv()
        for cp in mine:
            cp.wait()

    outs = [jax.ShapeDtypeStruct(x.shape, x.dtype) for x in xs]
    return _hbm_call(body, xs, outs, 3 * n, name)


def _pair_sum(x8, r4, name):
    _, r, c = x8.shape
    tr = _row_tile(r, c * 12)

    def body(core_ref, x_ref, r_ref, o_ref):
        o_ref[...] = (x_ref[...].astype(F32) + r_ref[...].astype(F32)).astype(o_ref.dtype)

    core = lax.axis_index("c").astype(jnp.int32).reshape(1)
    return pl.pallas_call(
        body, name=name,
        grid_spec=pltpu.PrefetchScalarGridSpec(
            num_scalar_prefetch=1, grid=(4, r // tr),
            in_specs=[pl.BlockSpec((None, tr, c), lambda j, i, core_ref: (2 * j + core_ref[0], i, 0)),
                      pl.BlockSpec((None, tr, c), lambda j, i, core_ref: (j, i, 0))],
            out_specs=pl.BlockSpec((None, tr, c), lambda j, i, core_ref: (j, i, 0))),
        out_shape=jax.ShapeDtypeStruct((4, r, c), x8.dtype),
        compiler_params=_cparams(("parallel", "parallel"), VMEM_LIMIT))(core, x8, r4)


def _row_tile(r, bytes_per_row, budget=4 * 1024 * 1024):
    best = None
    for t in range(16, r + 1, 16):
        if r % t == 0 and t * bytes_per_row <= budget:
            best = t
    return best if best is not None else r


def _mm(a, b, dims, name, out_dtype=F32, add=None, out_shards=1, tm=1024, tn=1024, tk=1024):
    S = b.shape[0] if b.ndim == 3 else 1
    bshape = (b.shape[1], S * b.shape[2]) if b.ndim == 3 else b.shape
    if dims == "nn":
        (M, K), (K2, N) = a.shape, bshape
    elif dims == "nt":
        (M, K), (N, K2) = a.shape, bshape
    else:
        (K, M), (K2, N) = a.shape, bshape
    assert K == K2, (a.shape, b.shape, dims)
    cs = bshape[1] // S
    tm, tk = _tile(M, tm), _tile(K, min(tk, cs) if (S > 1 and dims == "nt") else tk)
    tn = _tile(N, min(tn, cs) if (S > 1 and dims != "nt") else min(tn, N // out_shards))
    assert cs % (tk if dims == "nt" else tn) == 0 and (N // out_shards) % tn == 0
    nk = K // tk

    def body(*refs):
        if add is None:
            a_ref, b_ref, o_ref, acc = refs
        else:
            a_ref, b_ref, c_ref, o_ref, acc = refs
        k = pl.program_id(2)

        @pl.when(k == 0)
        def _():
            acc[...] = jnp.zeros_like(acc)

        acc[...] += _dot(a_ref[...], b_ref[...], dims)

        @pl.when(k == nk - 1)
        def _():
            r = acc[...]
            if add is not None:
                r = r + c_ref[...]
            o_ref[...] = r.astype(out_dtype)

    a_spec = (pl.BlockSpec((tk, tm), lambda i, j, k: (k, i)) if dims == "tn"
              else pl.BlockSpec((tm, tk), lambda i, j, k: (i, k)))
    if S == 1:
        b_spec = (pl.BlockSpec((tn, tk), lambda i, j, k: (j, k)) if dims == "nt"
                  else pl.BlockSpec((tk, tn), lambda i, j, k: (k, j)))
    elif dims == "nt":
        per = cs // tk
        b_spec = pl.BlockSpec((None, tn, tk), lambda i, j, k: (k // per, j, k % per))
    else:
        per = cs // tn
        b_spec = pl.BlockSpec((None, tk, tn), lambda i, j, k: (j // per, k, j % per))
    in_specs = [a_spec, b_spec]
    args = [a, b]
    if add is not None:
        in_specs.append(pl.BlockSpec((tm, tn), lambda i, j, k: (i, j)))
        args.append(add)
    if out_shards == 1:
        out_spec, out_shape = pl.BlockSpec((tm, tn), lambda i, j, k: (i, j)), (M, N)
    else:
        oper = N // out_shards // tn
        out_spec = pl.BlockSpec((None, tm, tn), lambda i, j, k: (j // oper, i, j % oper))
        out_shape = (out_shards, M, N // out_shards)
    return pl.pallas_call(
        body, name=name, grid=(M // tm, N // tn, nk),
        in_specs=in_specs, out_specs=out_spec,
        out_shape=jax.ShapeDtypeStruct(out_shape, out_dtype),
        scratch_shapes=[pltpu.VMEM((tm, tn), F32)],
        compiler_params=_cparams(("parallel", "parallel", "arbitrary"), VMEM_LIMIT),
    )(*args)


def _ada_fwd(ccp, w, b):
    def body(c_ref, w_ref, b_ref, o_ref):
        o_ref[...] = _dot(_silu(c_ref[...]), w_ref[...]) + b_ref[...]

    return pl.pallas_call(
        body, name="ada_fwd", out_shape=jax.ShapeDtypeStruct((ccp.shape[0], w.shape[1]), F32),
        compiler_params=_cparams(None, VMEM_LIMIT))(ccp, w, b)


def _ada_bwd(ccp, w, g, cctx, n_loc):
    def body(c_ref, w_ref, g_ref, cc_ref, gw_ref, dc_ref):
        gw_ref[...] = _dot(_silu(c_ref[...]), g_ref[...], "tn")
        dcc = _dot(g_ref[...], w_ref[...], "nt")
        row = lax.broadcasted_iota(jnp.int32, dcc.shape, 0)
        part = jnp.sum(jnp.where(row % SUBLANES == n_loc, dcc, 0.0), axis=0, keepdims=True)
        dc_ref[...] = jnp.broadcast_to(part * _dsilu(cc_ref[...]), dc_ref.shape)

    D = ccp.shape[1]
    return pl.pallas_call(
        body, name="ada_bwd",
        out_shape=(jax.ShapeDtypeStruct(w.shape, F32), jax.ShapeDtypeStruct((SUBLANES, D), F32)),
        compiler_params=_cparams(None, VMEM_LIMIT))(ccp, w, g, cctx)


def _norm_mod_fwd(h, gain, mods, i_shift, i_scale, nct, tm, name):
    B, T, D = h.shape

    def body(h_ref, g_ref, m_ref, u_ref):
        x = h_ref[0]
        r = lax.rsqrt(jnp.mean(x * x, axis=-1, keepdims=True) + EPS)
        n = x * r * g_ref[...]
        u_ref[0] = (n * (1.0 + m_ref[0, i_scale:i_scale + 1, :]) + m_ref[0, i_shift:i_shift + 1, :]).astype(BF16)

    return pl.pallas_call(
        body, name=name, grid=(B, T // tm),
        in_specs=[pl.BlockSpec((1, tm, D), lambda b, t: (b, t, 0)),
                  pl.BlockSpec((1, D), lambda b, t: (0, 0)),
                  pl.BlockSpec((1, SUBLANES, D), lambda b, t: (jnp.where(t < nct, B, b), 0, 0))],
        out_specs=pl.BlockSpec((1, tm, D), lambda b, t: (b, t, 0)),
        out_shape=jax.ShapeDtypeStruct((B, T, D), BF16),
        compiler_params=_cparams(("parallel", "parallel"), VMEM_LIMIT),
    )(h, gain, mods)


def _norm_mod_bwd(h, du, gain, mods, i_scale, nct, tm, res, name):
    B, T, D = h.shape
    nt = T // tm

    def body(h_ref, du_ref, g_ref, m_ref, res_ref, dx_ref, ax_ref, ac_ref):
        t = pl.program_id(1)
        x = h_ref[0]
        r = lax.rsqrt(jnp.mean(x * x, axis=-1, keepdims=True) + EPS)
        nh = x * r
        g = g_ref[...]
        du_ = du_ref[0]
        dn = du_ * (1.0 + m_ref[0, i_scale:i_scale + 1, :])
        dnh = dn * g
        dx = r * (dnh - nh * jnp.mean(dnh * nh, axis=-1, keepdims=True))
        sums = (jnp.sum(du_, axis=0, keepdims=True), jnp.sum(du_ * nh * g, axis=0, keepdims=True),
                jnp.sum(dn * nh, axis=0, keepdims=True))

        @pl.when(t == 0)
        def _():
            ax_ref[...] = jnp.zeros_like(ax_ref)
            ac_ref[...] = jnp.zeros_like(ac_ref)

        def accumulate(ref):
            for i, s in enumerate(sums):
                ref[0, i:i + 1, :] += s

        @pl.when(t < nct)
        def _():
            accumulate(ac_ref)

        @pl.when(t >= nct)
        def _():
            accumulate(ax_ref)
            dx_ref[0] = dx + res_ref[0]

    lat = lambda b, t: (b, jnp.maximum(t - nct, 0), 0)
    acc = pl.BlockSpec((1, SUBLANES, D), lambda b, t: (b, 0, 0))
    return pl.pallas_call(
        body, name=name, grid=(B, nt),
        in_specs=[pl.BlockSpec((1, tm, D), lambda b, t: (b, t, 0)),
                  pl.BlockSpec((1, tm, D), lambda b, t: (b, t, 0)),
                  pl.BlockSpec((1, D), lambda b, t: (0, 0)),
                  pl.BlockSpec((1, SUBLANES, D), lambda b, t: (jnp.where(t < nct, B, b), 0, 0)),
                  pl.BlockSpec((1, tm, D), lat)],
        out_specs=(pl.BlockSpec((1, tm, D), lat), acc, acc),
        out_shape=(jax.ShapeDtypeStruct(res.shape, F32),
                   jax.ShapeDtypeStruct((B, SUBLANES, D), F32), jax.ShapeDtypeStruct((B, SUBLANES, D), F32)),
        compiler_params=_cparams(("parallel", "arbitrary"), VMEM_LIMIT),
    )(h, du, gain, mods, res)


def _resid_norm_fwd(x, y, gain, mods, i_gate, tm, name):
    B, N, D = x.shape

    def body(x_ref, y_ref, g_ref, m_ref, o_ref):
        y_ = y_ref[0]
        r = lax.rsqrt(jnp.mean(y_ * y_, axis=-1, keepdims=True) + EPS)
        o_ref[0] = x_ref[0] + y_ * r * g_ref[...] * m_ref[0, i_gate:i_gate + 1, :]

    row = pl.BlockSpec((1, tm, D), lambda b, t: (b, t, 0))
    return pl.pallas_call(
        body, name=name, grid=(B, N // tm),
        in_specs=[row, row, pl.BlockSpec((1, D), lambda b, t: (0, 0)),
                  pl.BlockSpec((1, SUBLANES, D), lambda b, t: (b, 0, 0))],
        out_specs=row, out_shape=jax.ShapeDtypeStruct((B, N, D), F32),
        compiler_params=_cparams(("parallel", "parallel"), VMEM_LIMIT),
    )(x, y, gain, mods)


def _resid_norm_bwd_math(y_, dh, g, gate):
    r = lax.rsqrt(jnp.mean(y_ * y_, axis=-1, keepdims=True) + EPS)
    nh = y_ * r
    dgate = jnp.sum(dh * nh * g, axis=0, keepdims=True)
    dn = dh * gate
    dgain = jnp.sum(dn * nh, axis=0, keepdims=True)
    dnh = dn * g
    dy = r * (dnh - nh * jnp.mean(dnh * nh, axis=-1, keepdims=True))
    return dy, dgate, dgain


def _resid_norm_bwd(y, dh, gain, mods, i_gate, tm, name):
    B, N, D = y.shape

    def body(y_ref, dh_ref, g_ref, m_ref, dy_ref, acc_ref):
        t = pl.program_id(1)
        dy, dgate, dgain = _resid_norm_bwd_math(y_ref[0], dh_ref[0], g_ref[...], m_ref[0, i_gate:i_gate + 1, :])
        dy_ref[0] = dy.astype(BF16)

        @pl.when(t == 0)
        def _():
            acc_ref[...] = jnp.zeros_like(acc_ref)

        acc_ref[0, 0:1, :] += dgate
        acc_ref[0, 1:2, :] += dgain

    row = pl.BlockSpec((1, tm, D), lambda b, t: (b, t, 0))
    return pl.pallas_call(
        body, name=name, grid=(B, N // tm),
        in_specs=[row, row, pl.BlockSpec((1, D), lambda b, t: (0, 0)),
                  pl.BlockSpec((1, SUBLANES, D), lambda b, t: (b, 0, 0))],
        out_specs=(row, pl.BlockSpec((1, SUBLANES, D), lambda b, t: (b, 0, 0))),
        out_shape=(jax.ShapeDtypeStruct((B, N, D), BF16), jax.ShapeDtypeStruct((B, SUBLANES, D), F32)),
        compiler_params=_cparams(("parallel", "arbitrary"), VMEM_LIMIT),
    )(y, dh, gain, mods)


def _final_fwd_bwd(h1, d, gain, mods, i_gate, tgt, tm):
    B, N, D = h1.shape

    def body(h_ref, d_ref, g_ref, m_ref, t_ref, dd_ref, dh_ref, acc_ref):
        t = pl.program_id(1)
        d_ = d_ref[0]
        g = g_ref[...]
        gate = m_ref[0, i_gate:i_gate + 1, :]
        r = lax.rsqrt(jnp.mean(d_ * d_, axis=-1, keepdims=True) + EPS)
        e = h_ref[0] + d_ * r * g * gate - t_ref[0]
        dh = e * (1.0 / D)
        dh_ref[0] = dh
        dy, dgate, dgain = _resid_norm_bwd_math(d_, dh, g, gate)
        dd_ref[0] = dy.astype(BF16)

        @pl.when(t == 0)
        def _():
            acc_ref[...] = jnp.zeros_like(acc_ref)

        acc_ref[0, 0:1, :] += dgate
        acc_ref[0, 1:2, :] += dgain
        acc_ref[0, 2:3, :] += jnp.sum(e * e, axis=0, keepdims=True)

    row = pl.BlockSpec((1, tm, D), lambda b, t: (b, t, 0))
    return pl.pallas_call(
        body, name="final_fwd_bwd", grid=(B, N // tm),
        in_specs=[row, row, pl.BlockSpec((1, D), lambda b, t: (0, 0)),
                  pl.BlockSpec((1, SUBLANES, D), lambda b, t: (b, 0, 0)), row],
        out_specs=(row, row, pl.BlockSpec((1, SUBLANES, D), lambda b, t: (b, 0, 0))),
        out_shape=(jax.ShapeDtypeStruct((B, N, D), BF16), jax.ShapeDtypeStruct((B, N, D), F32),
                   jax.ShapeDtypeStruct((B, SUBLANES, D), F32)),
        compiler_params=_cparams(("parallel", "arbitrary"), VMEM_LIMIT),
    )(h1, d, gain, mods, tgt)


def _shifted(x, prev, nxt, off, row, tm):
    if off == 0:
        return x
    if off < 0:
        y = pltpu.roll(x, -off, 0)
        for r in range(-off):
            y = jnp.where(row == r, prev[SUBLANES + r + off:SUBLANES + r + off + 1, :], y)
        return y
    y = pltpu.roll(x, tm - off, 0)
    for r in range(off):
        y = jnp.where(row == tm - off + r, nxt[r:r + 1, :], y)
    return y


def _halo_specs(T, tm, tc, cb0, order):
    r8, n8 = tm // SUBLANES, T // SUBLANES
    if order == "btj":
        prev = lambda b, t, j: (b, jnp.maximum(t * r8 - 1, 0), cb0 + j)
        nxt = lambda b, t, j: (b, jnp.minimum((t + 1) * r8, n8 - 1), cb0 + j)
    else:
        prev = lambda b, j, t: (b, jnp.maximum(t * r8 - 1, 0), cb0 + j)
        nxt = lambda b, j, t: (b, jnp.minimum((t + 1) * r8, n8 - 1), cb0 + j)
    return pl.BlockSpec((1, SUBLANES, tc), prev), pl.BlockSpec((1, SUBLANES, tc), nxt)


def _seg_halos(p_ref, n_ref, t, nct, nt):
    seg_start = jnp.logical_or(t == 0, t == nct)
    seg_end = jnp.logical_or(t == nct - 1, t == nt - 1)
    prev = jnp.where(seg_start, 0.0, p_ref[0])
    nxt = jnp.where(seg_end, 0.0, n_ref[0])
    return prev, nxt


def _dwconv(x, col0, C, w, bias, offs, nct, tm, tc, name, out_dtype=F32):
    B, T, _ = x.shape
    nt = T // tm
    cb0 = col0 // tc

    def body(*refs):
        if bias is None:
            x_ref, p_ref, n_ref, w_ref, o_ref = refs
        else:
            x_ref, p_ref, n_ref, w_ref, b_ref, o_ref = refs
        t = pl.program_id(1)
        prev, nxt = _seg_halos(p_ref, n_ref, t, nct, nt)
        x_ = x_ref[0]
        row = lax.broadcasted_iota(jnp.int32, x_.shape, 0)
        acc = None
        for j, off in enumerate(offs):
            term = _shifted(x_, prev, nxt, off, row, tm) * w_ref[j:j + 1, :]
            acc = term if acc is None else acc + term
        if bias is not None:
            acc = acc + b_ref[...]
        o_ref[0] = acc.astype(out_dtype)

    pspec, nspec = _halo_specs(T, tm, tc, cb0, "btj")
    in_specs = [pl.BlockSpec((1, tm, tc), lambda b, t, j: (b, t, cb0 + j)), pspec, nspec,
                pl.BlockSpec((w.shape[0], tc), lambda b, t, j: (0, j))]
    args = [x, x, x, w]
    if bias is not None:
        in_specs.append(pl.BlockSpec((1, tc), lambda b, t, j: (0, j)))
        args.append(bias)
    return pl.pallas_call(
        body, name=name, grid=(B, nt, C // tc),
        in_specs=in_specs, out_specs=pl.BlockSpec((1, tm, tc), lambda b, t, j: (b, t, j)),
        out_shape=jax.ShapeDtypeStruct((B, T, C), out_dtype),
        compiler_params=_cparams(("parallel", "parallel", "parallel"), VMEM_LIMIT),
    )(*args)


def _dwconv_wgrad(dy, x, col0, C, offs, nct, tm, tc, name):
    B, T, _ = x.shape
    nt = T // tm
    cb0 = col0 // tc
    K = len(offs)

    def body(dy_ref, x_ref, p_ref, n_ref, acc_ref):
        t = pl.program_id(2)
        prev, nxt = _seg_halos(p_ref, n_ref, t, nct, nt)
        x_ = x_ref[0]
        dy_ = dy_ref[0]
        row = lax.broadcasted_iota(jnp.int32, x_.shape, 0)

        @pl.when(t == 0)
        def _():
            acc_ref[...] = jnp.zeros_like(acc_ref)

        for j, off in enumerate(offs):
            acc_ref[0, j:j + 1, :] += jnp.sum(dy_ * _shifted(x_, prev, nxt, off, row, tm), axis=0, keepdims=True)
        acc_ref[0, K:K + 1, :] += jnp.sum(dy_, axis=0, keepdims=True)

    pspec, nspec = _halo_specs(T, tm, tc, cb0, "bjt")
    return pl.pallas_call(
        body, name=name, grid=(B, C // tc, nt),
        in_specs=[pl.BlockSpec((1, tm, tc), lambda b, j, t: (b, t, j)),
                  pl.BlockSpec((1, tm, tc), lambda b, j, t: (b, t, cb0 + j)), pspec, nspec],
        out_specs=pl.BlockSpec((1, SUBLANES, tc), lambda b, j, t: (b, 0, j)),
        out_shape=jax.ShapeDtypeStruct((B, SUBLANES, C), F32),
        compiler_params=_cparams(("parallel", "parallel", "arbitrary"), VMEM_LIMIT),
    )(dy, x, x, x)


def _ab_act(pab, alog, dtb, nd, tm):
    R = pab.shape[0]

    def body(p_ref, al_ref, dt_ref, o_ref):
        p = p_ref[...]
        lane = lax.broadcasted_iota(jnp.int32, p.shape, 1)
        g = -jnp.exp(al_ref[...]) * _softplus(p + dt_ref[...])
        o_ref[...] = jnp.where(lane < nd, g, jnp.where(lane < 2 * nd, _sigmoid(p), 0.0))

    row = pl.BlockSpec((tm, LANES), lambda i: (i, 0))
    vec = pl.BlockSpec((1, LANES), lambda i: (0, 0))
    return pl.pallas_call(
        body, name="ab_act", grid=(R // tm,), in_specs=[row, vec, vec], out_specs=row,
        out_shape=jax.ShapeDtypeStruct((R, LANES), F32),
        compiler_params=_cparams(("parallel",), VMEM_LIMIT))(pab, alog, dtb)


def _ab_act_bwd(pab, gb, dgb, alog, dtb, nd, tm):
    R = pab.shape[0]

    def body(p_ref, gb_ref, d0_ref, d1_ref, al_ref, dt_ref, o_ref, acc_ref):
        i = pl.program_id(0)
        p = p_ref[...]
        gb_ = gb_ref[...]
        d_ = d0_ref[...] + d1_ref[...]
        lane = lax.broadcasted_iota(jnp.int32, p.shape, 1)
        is_g = lane < nd
        is_b = jnp.logical_and(lane >= nd, lane < 2 * nd)
        da = jnp.where(is_g, d_ * (-jnp.exp(al_ref[...])) * _sigmoid(p + dt_ref[...]), 0.0)
        db = jnp.where(is_b, d_ * gb_ * (1.0 - gb_), 0.0)
        o_ref[...] = (da + db).astype(BF16)

        @pl.when(i == 0)
        def _():
            acc_ref[...] = jnp.zeros_like(acc_ref)

        acc_ref[0:1, :] += jnp.sum(jnp.where(is_g, d_ * gb_, 0.0), axis=0, keepdims=True)
        acc_ref[1:2, :] += jnp.sum(da, axis=0, keepdims=True)

    row = pl.BlockSpec((tm, LANES), lambda i: (i, 0))
    vec = pl.BlockSpec((1, LANES), lambda i: (0, 0))
    return pl.pallas_call(
        body, name="ab_act_bwd", grid=(R // tm,),
        in_specs=[row, row, row, row, vec, vec],
        out_specs=(row, pl.BlockSpec((SUBLANES, LANES), lambda i: (0, 0))),
        out_shape=(jax.ShapeDtypeStruct((R, LANES), BF16), jax.ShapeDtypeStruct((SUBLANES, LANES), F32)),
        compiler_params=_cparams(("arbitrary",), VMEM_LIMIT))(pab, gb, dgb[0], dgb[1], alog, dtb)


def _dn_act(cv, H, hd, tm):
    B, T, C3 = cv.shape
    qscale = float(hd) ** -0.5

    def body(c_ref, o_ref):
        part = pl.program_id(0)
        for h in range(H):
            sl = slice(h * hd, (h + 1) * hd)
            s = _silu(c_ref[0, :, sl])
            rs = lax.rsqrt(jnp.sum(s * s, axis=-1, keepdims=True) + EPS)
            scale = jnp.where(part == 0, rs * qscale, jnp.where(part == 1, rs, 1.0))
            o_ref[0, :, sl] = s * scale

    spec = pl.BlockSpec((1, tm, H * hd), lambda p, b, t: (b, t, p))
    return pl.pallas_call(
        body, name="dn_act", grid=(3, B, T // tm), in_specs=[spec], out_specs=spec,
        out_shape=jax.ShapeDtypeStruct((B, T, C3), F32),
        compiler_params=_cparams(("parallel",) * 3, VMEM_LIMIT))(cv)


def _dn_act_bwd(cv, dqkv, H, hd, tm):
    B, T, C3 = cv.shape
    qscale = float(hd) ** -0.5

    def body(c_ref, d0_ref, d1_ref, o_ref):
        part = pl.program_id(0)
        for h in range(H):
            sl = slice(h * hd, (h + 1) * hd)
            c = c_ref[0, :, sl]
            s = _silu(c)
            dout = d0_ref[0, :, sl] + d1_ref[0, :, sl]
            rs = lax.rsqrt(jnp.sum(s * s, axis=-1, keepdims=True) + EPS)
            sh = s * rs
            dnorm = rs * (dout - sh * jnp.sum(dout * sh, axis=-1, keepdims=True))
            ds = jnp.where(part == 0, dnorm * qscale, jnp.where(part == 1, dnorm, dout))
            o_ref[0, :, sl] = ds * _dsilu(c)

    spec = pl.BlockSpec((1, tm, H * hd), lambda p, b, t: (b, t, p))
    return pl.pallas_call(
        body, name="dn_act_bwd", grid=(3, B, T // tm), in_specs=[spec, spec, spec], out_specs=spec,
        out_shape=jax.ShapeDtypeStruct((B, T, C3), F32),
        compiler_params=_cparams(("parallel",) * 3, VMEM_LIMIT))(cv, dqkv[0], dqkv[1])


def _chunk_of(d, s, ncc, nch):
    if d == 0:
        return s
    return jnp.where(s < ncc, ncc - 1 - s, nch - 1 - (s - ncc))


def _delta_masks(d):
    row = lax.broadcasted_iota(jnp.int32, (CHUNK, CHUNK), 0)
    col = lax.broadcasted_iota(jnp.int32, (CHUNK, CHUNK), 1)
    sign = 1 - 2 * d
    diff = (row - col) * sign
    return diff >= 0, diff > 0, diff <= 0


def _each(f, *lists):
    return [f(*a) for a in zip(*lists)]


def _unit_tri_inverse(As):
    C = As[0].shape[0]
    row = lax.broadcasted_iota(jnp.int32, (C, C), 0)
    col = lax.broadcasted_iota(jnp.int32, (C, C), 1)
    eye = jnp.where(row == col, 1.0, 0.0).astype(F32)
    same = lambda shift: jnp.right_shift(row, shift) == jnp.right_shift(col, shift)
    in8 = same(3)
    xs = [-jnp.where(in8, a, 0.0) for a in As]
    ts = [eye + x for x in xs]
    ps = _each(lambda x: _dot(x, x), xs)
    ts = _each(lambda t, p: t + _dot(t, p), ts, ps)
    ps = _each(lambda p: _dot(p, p), ps)
    ts = _each(lambda t, p: t + _dot(t, p), ts, ps)
    shift = 3
    while (1 << shift) < C:
        off = jnp.logical_and(same(shift + 1), jnp.right_shift(row, shift) != jnp.right_shift(col, shift))
        los = [jnp.where(off, a, 0.0) for a in As]
        ts = _each(lambda t, lo: t - _dot(t, _dot(lo, t)), ts, los)
        shift += 1
    rs = _each(lambda a, t: eye - _dot_hi(eye + a, t), As, ts)
    return _each(lambda t, r: t + _dot(t, r), ts, rs)


def _delta_chunk_fwd(q, k, v, g_i, g_j, beta_i, gl, S, incl, strict):
    D_ = _each(lambda gi, gj, m: jnp.where(m, jnp.exp(jnp.where(m, gi - gj, 0.0)), 0.0), g_i, g_j, incl)
    kb = _each(lambda k_, b: k_ * b, k, beta_i)
    A = _each(lambda kb_, k_, d, m: jnp.where(m, _dot(kb_, k_, "nt") * d, 0.0), kb, k, D_, strict)
    P = _each(lambda q_, k_, d, m: jnp.where(m, _dot(q_, k_, "nt") * d, 0.0), q, k, D_, incl)
    Tm = _unit_tri_inverse(A)
    gam = _each(jnp.exp, g_i)
    w = _each(lambda t, kb_, g: _dot(t, kb_ * g), Tm, kb, gam)
    u = _each(lambda t, v_, b: _dot(t, v_ * b), Tm, v, beta_i)
    vn = _each(lambda u_, w_, s: u_ - _dot(w_, s), u, w, S)
    qg = _each(lambda q_, g: q_ * g, q, gam)
    o = _each(lambda qg_, s, p, vn_: _dot(qg_, s) + _dot(p, vn_), qg, S, P, vn)
    e_i = _each(lambda gl_, gi: jnp.exp(gl_ - gi), gl, g_i)
    kd = _each(lambda k_, e: k_ * e, k, e_i)
    Gam = _each(jnp.exp, gl)
    S_new = _each(lambda s, g, kd_, vn_: s * g + _dot(kd_, vn_, "tn"), S, Gam, kd, vn)
    return dict(D=D_, kb=kb, A=A, Tm=Tm, gam=gam, w=w, u=u, vn=vn, P=P, qg=qg, o=o, e=e_i, kd=kd, Gam=Gam, S_new=S_new)


def _delta_gb(gb_ref, d, incl, incl_t, H):
    gb = gb_ref[0]
    g = gb[:, d * H:(d + 1) * H]
    beta = gb[:, (2 + d) * H:(3 + d) * H]
    gc_col = _dot_mask(incl.astype(BF16), g)
    gc_row = _dot_mask(incl_t.astype(BF16), g, "xtn")
    gl = jnp.sum(g, axis=0, keepdims=True)
    return beta, gc_col, gc_row, gl


def _delta_fwd(qkvn, gb, H, hd, ncc):
    B, T, _ = qkvn.shape
    nch = T // CHUNK
    HD = H * hd
    pairs = [(d, h) for d in range(2) for h in range(H)]

    def body(q0, k0, v0, gb0, q1, k1, v1, gb1, o0, o1, sin0, sin1, S_ref):
        s = pl.program_id(1)
        q_refs, k_refs, v_refs, gb_refs = (q0, q1), (k0, k1), (v0, v1), (gb0, gb1)
        o_refs, sin_refs = (o0, o1), (sin0, sin1)

        @pl.when(s == 0)
        def _():
            S_ref[...] = jnp.zeros_like(S_ref)

        masks = [_delta_masks(d) for d in range(2)]
        gbs = [_delta_gb(gb_refs[d], d, masks[d][0], masks[d][2], H) for d in range(2)]
        head = lambda ref, h: ref[0, :, h * hd:(h + 1) * hd]
        S = [S_ref[d, h] for d, h in pairs]
        r = _delta_chunk_fwd([head(q_refs[d], h) for d, h in pairs], [head(k_refs[d], h) for d, h in pairs],
                             [head(v_refs[d], h) for d, h in pairs],
                             [gbs[d][1][:, h:h + 1] for d, h in pairs], [gbs[d][2][h:h + 1, :] for d, h in pairs],
                             [gbs[d][0][:, h:h + 1] for d, h in pairs], [gbs[d][3][:, h:h + 1] for d, h in pairs],
                             S, [masks[d][0] for d, h in pairs], [masks[d][1] for d, h in pairs])
        for i, (d, h) in enumerate(pairs):
            sin_refs[d][0, 0, h] = S[i]
            S_ref[d, h] = r["S_new"][i]
            o_refs[d][0, :, h * hd:(h + 1) * hd] = r["o"][i]

    def dir_specs(d):
        cidx = lambda b, s: _chunk_of(d, s, ncc, nch)
        ins = [pl.BlockSpec((1, CHUNK, HD), lambda b, s, p=p: (b, cidx(b, s), p)) for p in range(3)]
        ins.append(pl.BlockSpec((1, CHUNK, LANES), lambda b, s: (b, cidx(b, s), 0)))
        return (ins, pl.BlockSpec((1, CHUNK, HD), lambda b, s: (b, cidx(b, s), 0)),
                pl.BlockSpec((1, 1, H, hd, hd), lambda b, s: (b, cidx(b, s), 0, 0, 0)))

    (in0, o_s0, sin_s0), (in1, o_s1, sin_s1) = dir_specs(0), dir_specs(1)
    o_shape = jax.ShapeDtypeStruct((B, T, HD), F32)
    sin_shape = jax.ShapeDtypeStruct((B, nch, H, hd, hd), F32)
    return pl.pallas_call(
        body, name="delta_fwd", grid=(B, nch),
        in_specs=in0 + in1, out_specs=(o_s0, o_s1, sin_s0, sin_s1),
        out_shape=(o_shape, o_shape, sin_shape, sin_shape),
        scratch_shapes=[pltpu.VMEM((2, H, hd, hd), F32)],
        compiler_params=_cparams(("parallel", "arbitrary"), VMEM_LIMIT),
    )(qkvn, qkvn, qkvn, gb, qkvn, qkvn, qkvn, gb)


def _delta_bwd(qkvn, gb, sin, do, H, hd, ncc):
    B, T, _ = qkvn.shape
    nch = T // CHUNK
    HD = H * hd
    pairs = [(d, h) for d in range(2) for h in range(H)]

    def body(q0, k0, v0, gb0, sin0, do0, q1, k1, v1, gb1, sin1, do1, dqkv0, dqkv1, dgb0, dgb1, dS_ref):
        s = pl.program_id(1)
        q_refs, k_refs, v_refs, gb_refs = (q0, q1), (k0, k1), (v0, v1), (gb0, gb1)
        sin_refs, do_refs, dqkv_refs, dgb_refs = (sin0, sin1), (do0, do1), (dqkv0, dqkv1), (dgb0, dgb1)

        @pl.when(s == 0)
        def _():
            dS_ref[...] = jnp.zeros_like(dS_ref)

        masks = [_delta_masks(d) for d in range(2)]
        gbs = [_delta_gb(gb_refs[d], d, masks[d][0], masks[d][2], H) for d in range(2)]
        incl = [masks[d][0] for d, h in pairs]
        strict = [masks[d][1] for d, h in pairs]
        lane = lax.broadcasted_iota(jnp.int32, (1, LANES), 1)
        ones = jnp.ones((CHUNK, LANES), BF16)
        head = lambda ref, h: ref[0, :, h * hd:(h + 1) * hd]
        q = [head(q_refs[d], h) for d, h in pairs]
        k = [head(k_refs[d], h) for d, h in pairs]
        v = [head(v_refs[d], h) for d, h in pairs]
        do_ = [head(do_refs[d], h) for d, h in pairs]
        beta_i = [gbs[d][0][:, h:h + 1] for d, h in pairs]
        S = [sin_refs[d][0, 0, h] for d, h in pairs]
        dSn = [dS_ref[d, h] for d, h in pairs]
        f = _delta_chunk_fwd(q, k, v, [gbs[d][1][:, h:h + 1] for d, h in pairs], [gbs[d][2][h:h + 1, :] for d, h in pairs],
                             beta_i, [gbs[d][3][:, h:h + 1] for d, h in pairs], S, incl, strict)
        rsum = lambda x: jnp.sum(x, axis=1, keepdims=True)
        dvn = _each(lambda p, d_, kd, ds: _dot(p, d_, "tn") + _dot(kd, ds), f["P"], do_, f["kd"], dSn)
        dP = _each(lambda d_, vn, m: jnp.where(m, _dot(d_, vn, "nt"), 0.0), do_, f["vn"], incl)
        dqg = _each(lambda d_, s: _dot(d_, s, "nt"), do_, S)
        dS_in = _each(lambda qg, d_, g, ds, w, dv_: _dot(qg, d_, "tn") + g * ds - _dot(w, dv_, "tn"),
                      f["qg"], do_, f["Gam"], dSn, f["w"], dvn)
        dGam = _each(lambda s, ds: jnp.sum(rsum(s * ds), axis=0, keepdims=True), S, dSn)
        dkd = _each(lambda vn, ds: _dot(vn, ds, "nt"), f["vn"], dSn)
        de = _each(lambda dkd_, k_, e: rsum(dkd_ * k_) * e, dkd, k, f["e"])
        dw = _each(lambda dv_, s: -_dot(dv_, s, "nt"), dvn, S)
        dXw = _each(lambda t, dw_: _dot(t, dw_, "tn"), f["Tm"], dw)
        dXu = _each(lambda t, dv_: _dot(t, dv_, "tn"), f["Tm"], dvn)
        dA = _each(lambda xw, w, xu, u, m: -jnp.where(m, _dot(xw, w, "nt") + _dot(xu, u, "nt"), 0.0),
                   dXw, f["w"], dXu, f["u"], strict)
        dM = _each(lambda a, d_: a * d_, dA, f["D"])
        dN = _each(lambda p, d_: p * d_, dP, f["D"])
        dkb = _each(lambda m, k_, xw, g: _dot(m, k_) + xw * g, dM, k, dXw, f["gam"])
        dq = _each(lambda dqg_, g, n, k_: dqg_ * g + _dot(n, k_), dqg, f["gam"], dN, k)
        dk = _each(lambda dkd_, e, m, kb, n, q_, dkb_, b: dkd_ * e + _dot(m, kb, "tn") + _dot(n, q_, "tn") + dkb_ * b,
                   dkd, f["e"], dM, f["kb"], dN, q, dkb, beta_i)
        dv = _each(lambda xu, b: xu * b, dXu, beta_i)
        dbeta = _each(lambda dkb_, k_, xu, v_: rsum(dkb_ * k_) + rsum(xu * v_), dkb, k, dXu, v)
        E = _each(lambda a, A_, p, P_: a * A_ + p * P_, dA, f["A"], dP, f["P"])

        def colsum(e):
            e1, e2, e3 = _split3(e)
            return (_dot(e1, ones, "tn") + (_dot(e2, ones, "tn") + _dot(e3, ones, "tn")))[:, 0:1]

        dg = _each(lambda e, dqg_, qg, xw, kb, g, de_: rsum(e) - colsum(e) + rsum(dqg_ * qg) + rsum(xw * kb) * g - de_,
                   E, dqg, f["qg"], dXw, f["kb"], f["gam"], de)
        dgl_h = _each(lambda de_, dg_, g: jnp.sum(de_, axis=0, keepdims=True) + dg_ * g, de, dGam, f["Gam"])
        for d in range(2):
            dgc = jnp.zeros((CHUNK, LANES), F32)
            dgl = jnp.zeros((1, LANES), F32)
            for h in range(H):
                i = d * H + h
                g_lane, b_lane = d * H + h, (2 + d) * H + h
                dgc = dgc + dg[i] * (lane == g_lane).astype(F32) + dbeta[i] * (lane == b_lane).astype(F32)
                dgl = dgl + dgl_h[i] * (lane == g_lane).astype(F32)
                dS_ref[d, h] = dS_in[i]
                dqkv_refs[d][0, :, h * hd:(h + 1) * hd] = dq[i]
                dqkv_refs[d][0, :, HD + h * hd:HD + (h + 1) * hd] = dk[i]
                dqkv_refs[d][0, :, 2 * HD + h * hd:2 * HD + (h + 1) * hd] = dv[i]
            draw = _dot_mask(masks[d][0].astype(BF16), dgc, "tn") + dgl
            dgb_refs[d][0] = jnp.where(lane < 2 * H, draw, dgc)

    def dir_specs(d):
        cidx = lambda b, s: _chunk_of(d, nch - 1 - s, ncc, nch)
        ins = [pl.BlockSpec((1, CHUNK, HD), lambda b, s, p=p: (b, cidx(b, s), p)) for p in range(3)]
        ins += [pl.BlockSpec((1, CHUNK, LANES), lambda b, s: (b, cidx(b, s), 0)),
                pl.BlockSpec((1, 1, H, hd, hd), lambda b, s: (b, cidx(b, s), 0, 0, 0)),
                pl.BlockSpec((1, CHUNK, HD), lambda b, s: (b, cidx(b, s), 0))]
        return (ins, pl.BlockSpec((1, CHUNK, 3 * HD), lambda b, s: (b, cidx(b, s), 0)),
                pl.BlockSpec((1, CHUNK, LANES), lambda b, s: (b, cidx(b, s), 0)))

    (in0, dq_s0, dg_s0), (in1, dq_s1, dg_s1) = dir_specs(0), dir_specs(1)
    dq_shape = jax.ShapeDtypeStruct((B, T, 3 * HD), F32)
    dg_shape = jax.ShapeDtypeStruct((B, T, LANES), F32)
    return pl.pallas_call(
        body, name="delta_bwd", grid=(B, nch),
        in_specs=in0 + in1, out_specs=(dq_s0, dq_s1, dg_s0, dg_s1),
        out_shape=(dq_shape, dq_shape, dg_shape, dg_shape),
        scratch_shapes=[pltpu.VMEM((2, H, hd, hd), F32)],
        compiler_params=_cparams(("parallel", "arbitrary"), VMEM_LIMIT),
    )(qkvn, qkvn, qkvn, gb, sin[0], do, qkvn, qkvn, qkvn, gb, sin[1], do)


def _dn_out(o2, pz, zcb0, onorm, H, hd, nct, tm):
    B, T, HD = o2[0].shape
    N = T - nct * tm

    def body(o0_ref, o1_ref, z_ref, g_ref, y_ref):
        for h in range(H):
            sl = slice(h * hd, (h + 1) * hd)
            o = o0_ref[0, :, sl] + o1_ref[0, :, sl]
            r = lax.rsqrt(jnp.mean(o * o, axis=-1, keepdims=True) + EPS)
            y_ref[0, :, sl] = (o * r * g_ref[...] * _silu(z_ref[0, :, sl])).astype(BF16)

    return pl.pallas_call(
        body, name="dn_out", grid=(B, N // tm),
        in_specs=[pl.BlockSpec((1, tm, HD), lambda b, t: (b, t + nct, 0)),
                  pl.BlockSpec((1, tm, HD), lambda b, t: (b, t + nct, 0)),
                  pl.BlockSpec((1, tm, HD), lambda b, t: (b, t + nct, zcb0)),
                  pl.BlockSpec((1, hd), lambda b, t: (0, 0))],
        out_specs=pl.BlockSpec((1, tm, HD), lambda b, t: (b, t, 0)),
        out_shape=jax.ShapeDtypeStruct((B, N, HD), BF16),
        compiler_params=_cparams(("parallel",) * 2, VMEM_LIMIT))(o2[0], o2[1], pz, onorm)


def _dn_out_bwd(o2, pz, zcb0, onorm, dy, H, hd, nct, tm):
    B, T, HD = o2[0].shape
    nt = T // tm

    def body(o0_ref, o1_ref, z_ref, g_ref, dy_ref, do_ref, dz_ref, acc_ref):
        t = pl.program_id(1)

        @pl.when(t == 0)
        def _():
            acc_ref[...] = jnp.zeros_like(acc_ref)

        @pl.when(t < nct)
        def _():
            do_ref[0] = jnp.zeros_like(do_ref[0])
            dz_ref[0] = jnp.zeros_like(dz_ref[0])

        @pl.when(t >= nct)
        def _():
            g = g_ref[...]
            for h in range(H):
                sl = slice(h * hd, (h + 1) * hd)
                o = o0_ref[0, :, sl] + o1_ref[0, :, sl]
                z = z_ref[0, :, sl]
                dy_ = dy_ref[0, :, sl]
                r = lax.rsqrt(jnp.mean(o * o, axis=-1, keepdims=True) + EPS)
                oh = o * r
                dz_ref[0, :, sl] = (dy_ * oh * g * _dsilu(z)).astype(BF16)
                dn = dy_ * _silu(z)
                acc_ref[0, 0:1, :] += jnp.sum(dn * oh, axis=0, keepdims=True)
                doh = dn * g
                do_ref[0, :, sl] = r * (doh - oh * jnp.mean(doh * oh, axis=-1, keepdims=True))

    lat = lambda b, t: (b, jnp.maximum(t - nct, 0), 0)
    row = pl.BlockSpec((1, tm, HD), lambda b, t: (b, t, 0))
    return pl.pallas_call(
        body, name="dn_out_bwd", grid=(B, nt),
        in_specs=[row, row,
                  pl.BlockSpec((1, tm, HD), lambda b, t: (b, t, zcb0)),
                  pl.BlockSpec((1, hd), lambda b, t: (0, 0)),
                  pl.BlockSpec((1, tm, HD), lat)],
        out_specs=(row, row, pl.BlockSpec((1, SUBLANES, hd), lambda b, t: (b, 0, 0))),
        out_shape=(jax.ShapeDtypeStruct((B, T, HD), F32), jax.ShapeDtypeStruct((B, T, HD), BF16),
                   jax.ShapeDtypeStruct((B, SUBLANES, hd), F32)),
        compiler_params=_cparams(("parallel", "arbitrary"), VMEM_LIMIT),
    )(o2[0], o2[1], pz, onorm, dy)


def _lru_gate_math(x, wr, wi, br, bi, lam):
    r = _sigmoid(_dot(x, wr) + br)
    i = _sigmoid(_dot(x, wi) + bi)
    sp = _softplus(-lam)
    la = -LRU_C * r * sp
    a = jnp.exp(la)
    s = jnp.sqrt(-jnp.tanh(la) * (a * a + 1.0))
    return r, i, sp, a, s


def _lru_gates(xc, wbd, bias4, lam, tm, bw):
    B, T, W = xc.shape

    def body(x_ref, w_ref, b_ref, l_ref, a_ref, i_ref):
        x = x_ref[0]
        for d in range(2):
            _, i, _, a, s = _lru_gate_math(x, w_ref[2 * d, 0], w_ref[2 * d + 1, 0],
                                           b_ref[2 * d:2 * d + 1, :], b_ref[2 * d + 1:2 * d + 2, :], l_ref[d:d + 1, :])
            a_ref[d, 0] = a
            i_ref[d, 0] = s * (i * x)

    out = pl.BlockSpec((2, 1, tm, bw), lambda b, t, j: (0, b, t, j))
    return pl.pallas_call(
        body, name="lru_gates", grid=(B, T // tm, W // bw),
        in_specs=[pl.BlockSpec((1, tm, bw), lambda b, t, j: (b, t, j)),
                  pl.BlockSpec((4, 1, bw, bw), lambda b, t, j: (0, j, 0, 0)),
                  pl.BlockSpec((4, bw), lambda b, t, j: (0, j)),
                  pl.BlockSpec((2, bw), lambda b, t, j: (0, j))],
        out_specs=(out, out),
        out_shape=(jax.ShapeDtypeStruct((2, B, T, W), F32), jax.ShapeDtypeStruct((2, B, T, W), F32)),
        compiler_params=_cparams(("parallel",) * 3, VMEM_LIMIT))(xc, wbd, bias4, lam)


def _lru_gates_bwd(xc, wbd, bias4, lam, dinp, da, tm, bw):
    B, T, W = xc.shape
    nt = T // tm

    def body(x_ref, w_ref, b_ref, l_ref, di0_ref, di1_ref, da0_ref, da1_ref, dx_ref, dw_ref, acc_ref):
        di_refs, da_refs = (di0_ref, di1_ref), (da0_ref, da1_ref)
        b_ = pl.program_id(1)
        t = pl.program_id(2)

        @pl.when(jnp.logical_and(b_ == 0, t == 0))
        def _():
            dw_ref[...] = jnp.zeros_like(dw_ref)
            acc_ref[...] = jnp.zeros_like(acc_ref)

        x = x_ref[0]
        dx = jnp.zeros_like(x)
        for d in range(2):
            wr, wi = w_ref[2 * d, 0], w_ref[2 * d + 1, 0]
            lam_ = l_ref[d:d + 1, :]
            r, i, sp, a, s = _lru_gate_math(x, wr, wi, b_ref[2 * d:2 * d + 1, :], b_ref[2 * d + 1:2 * d + 2, :], lam_)
            dinp_ = di_refs[d][0]
            dix = dinp_ * s
            ds = dinp_ * i * x
            dla = da_refs[d][0] * a - ds * (a * a) / s
            dpr = dla * (-LRU_C * sp) * r * (1.0 - r)
            dpi = dix * x * i * (1.0 - i)
            dx = dx + dix * i + _dot(dpr, wr, "nt") + _dot(dpi, wi, "nt")
            dw_ref[2 * d, 0] += _dot(x, dpr, "tn")
            dw_ref[2 * d + 1, 0] += _dot(x, dpi, "tn")
            acc_ref[2 * d:2 * d + 1, :] += jnp.sum(dpr, axis=0, keepdims=True)
            acc_ref[2 * d + 1:2 * d + 2, :] += jnp.sum(dpi, axis=0, keepdims=True)
            acc_ref[4 + d:5 + d, :] += jnp.sum(dla * (-LRU_C * r), axis=0, keepdims=True) * (-_sigmoid(-lam_))
        dx_ref[0] = dx

    tile = pl.BlockSpec((1, tm, bw), lambda j, b, t: (b, t, j))
    return pl.pallas_call(
        body, name="lru_gates_bwd", grid=(W // bw, B, nt),
        in_specs=[pl.BlockSpec((1, tm, bw), lambda j, b, t: (b, t, j)),
                  pl.BlockSpec((4, 1, bw, bw), lambda j, b, t: (0, j, 0, 0)),
                  pl.BlockSpec((4, bw), lambda j, b, t: (0, j)),
                  pl.BlockSpec((2, bw), lambda j, b, t: (0, j)), tile, tile, tile, tile],
        out_specs=(pl.BlockSpec((1, tm, bw), lambda j, b, t: (b, t, j)),
                   pl.BlockSpec((4, 1, bw, bw), lambda j, b, t: (0, j, 0, 0)),
                   pl.BlockSpec((SUBLANES, bw), lambda j, b, t: (0, j))),
        out_shape=(jax.ShapeDtypeStruct((B, T, W), F32), jax.ShapeDtypeStruct(wbd.shape, F32),
                   jax.ShapeDtypeStruct((SUBLANES, W), F32)),
        compiler_params=_cparams(("parallel", "arbitrary", "arbitrary"), VMEM_LIMIT),
    )(xc, wbd, bias4, lam, dinp[0], dinp[1], da[0], da[1])


def _tile_scan(a, x, rev, tm):
    row = lax.broadcasted_iota(jnp.int32, a.shape, 0)
    s = 1
    while s < tm:
        if rev:
            a_sh, x_sh, ok = pltpu.roll(a, tm - s, 0), pltpu.roll(x, tm - s, 0), row < tm - s
        else:
            a_sh, x_sh, ok = pltpu.roll(a, s, 0), pltpu.roll(x, s, 0), row >= s
        x = jnp.where(ok, x + a * x_sh, x)
        a = jnp.where(ok, a * a_sh, a)
        s *= 2
    return a, x


def _scan_tile_of(s, rev, nct, nt):
    if not rev:
        return s
    return jnp.where(s < nct, nct - 1 - s, nt - 1 - (s - nct))


def _lru_scan(a2, x2, d, nct, tm, tc):
    _, B, T, W = a2.shape
    nt = T // tm
    rev = d == 1
    last = 0 if rev else tm - 1

    def body(a_ref, x_ref, h_ref, carry):
        s = pl.program_id(2)

        @pl.when(s == 0)
        def _():
            carry[...] = jnp.zeros_like(carry)

        A, X = _tile_scan(a_ref[0, 0], x_ref[0, 0], rev, tm)
        h = X + A * carry[0:1, :]
        h_ref[0] = h
        carry[...] = jnp.broadcast_to(h[last:last + 1, :], carry.shape)

    tidx = lambda s: _scan_tile_of(s, rev, nct, nt)
    spec = pl.BlockSpec((1, 1, tm, tc), lambda b, j, s: (d, b, tidx(s), j))
    return pl.pallas_call(
        body, name=f"lru_scan{d}", grid=(B, W // tc, nt),
        in_specs=[spec, spec], out_specs=pl.BlockSpec((1, tm, tc), lambda b, j, s: (b, tidx(s), j)),
        out_shape=jax.ShapeDtypeStruct((B, T, W), F32),
        scratch_shapes=[pltpu.VMEM((SUBLANES, tc), F32)],
        compiler_params=_cparams(("parallel", "parallel", "arbitrary"), VMEM_LIMIT),
    )(a2, x2)


def _lru_scan_bwd(a2, h, dh, d, nct, tm, tc):
    _, B, T, W = a2.shape
    nt = T // tm
    rev = d == 1
    first = tm - 1 if rev else 0
    r8, n8 = tm // SUBLANES, T // SUBLANES

    def body(a_ref, h_ref, hp_ref, dh_ref, lam_ref, da_ref, ca, cl):
        s = pl.program_id(2)

        @pl.when(s == 0)
        def _():
            ca[...] = jnp.zeros_like(ca)
            cl[...] = jnp.zeros_like(cl)

        a = a_ref[0, 0]
        row = lax.broadcasted_iota(jnp.int32, a.shape, 0)
        if rev:
            c = jnp.where(row == 0, ca[0:1, :], pltpu.roll(a, 1, 0))
        else:
            c = jnp.where(row == tm - 1, ca[0:1, :], pltpu.roll(a, tm - 1, 0))
        C, L = _tile_scan(c, dh_ref[0], not rev, tm)
        lam = L + C * cl[0:1, :]
        lam_ref[0] = lam
        hp = jnp.where(s == nt - 1, 0.0, hp_ref[0])
        if rev:
            hprev = jnp.where(row == tm - 1, hp[0:1, :], pltpu.roll(h_ref[0], tm - 1, 0))
        else:
            hprev = jnp.where(row == 0, hp[SUBLANES - 1:SUBLANES, :], pltpu.roll(h_ref[0], 1, 0))
        da_ref[0] = lam * hprev
        ca[...] = jnp.broadcast_to(a[first:first + 1, :], ca.shape)
        cl[...] = jnp.broadcast_to(lam[first:first + 1, :], cl.shape)

    tidx = lambda s: _scan_tile_of(nt - 1 - s, rev, nct, nt)

    def hp_idx(b, j, s):
        tt = tidx(s)
        if rev:
            blk = jnp.where(tt == nt - 1, 0, jnp.minimum((tt + 1) * r8, n8 - 1))
        else:
            blk = jnp.maximum(tt * r8 - 1, 0)
        return (b, blk, j)

    tile = pl.BlockSpec((1, tm, tc), lambda b, j, s: (b, tidx(s), j))
    return pl.pallas_call(
        body, name=f"lru_scan_bwd{d}", grid=(B, W // tc, nt),
        in_specs=[pl.BlockSpec((1, 1, tm, tc), lambda b, j, s: (d, b, tidx(s), j)), tile,
                  pl.BlockSpec((1, SUBLANES, tc), hp_idx), tile],
        out_specs=(tile, tile),
        out_shape=(jax.ShapeDtypeStruct((B, T, W), F32), jax.ShapeDtypeStruct((B, T, W), F32)),
        scratch_shapes=[pltpu.VMEM((SUBLANES, tc), F32), pltpu.VMEM((SUBLANES, tc), F32)],
        compiler_params=_cparams(("parallel", "parallel", "arbitrary"), VMEM_LIMIT),
    )(a2, h, h, dh)


def _lru_out(h0, h1, pyl, ycb0, nct, tm, tc):
    B, T, W = h0.shape
    N = T - nct * tm

    def body(h0_ref, h1_ref, y_ref, o_ref):
        o_ref[0] = ((h0_ref[0] + h1_ref[0]) * _gelu(y_ref[0])).astype(BF16)

    hs = pl.BlockSpec((1, tm, tc), lambda b, t, j: (b, t + nct, j))
    return pl.pallas_call(
        body, name="lru_out", grid=(B, N // tm, W // tc),
        in_specs=[hs, hs, pl.BlockSpec((1, tm, tc), lambda b, t, j: (b, t + nct, ycb0 + j))],
        out_specs=pl.BlockSpec((1, tm, tc), lambda b, t, j: (b, t, j)),
        out_shape=jax.ShapeDtypeStruct((B, N, W), BF16),
        compiler_params=_cparams(("parallel",) * 3, VMEM_LIMIT))(h0, h1, pyl)


def _lru_out_bwd(h0, h1, pyl, ycb0, dy, nct, tm, tc):
    B, T, W = h0.shape

    def body(h0_ref, h1_ref, y_ref, dy_ref, dh_ref, dyl_ref):
        t = pl.program_id(1)

        @pl.when(t < nct)
        def _():
            dh_ref[0] = jnp.zeros_like(dh_ref[0])
            dyl_ref[0] = jnp.zeros_like(dyl_ref[0])

        @pl.when(t >= nct)
        def _():
            y = y_ref[0]
            dy_ = dy_ref[0]
            dh_ref[0] = dy_ * _gelu(y)
            dyl_ref[0] = (dy_ * (h0_ref[0] + h1_ref[0]) * _dgelu(y)).astype(BF16)

    hs = pl.BlockSpec((1, tm, tc), lambda b, t, j: (b, t, j))
    return pl.pallas_call(
        body, name="lru_out_bwd", grid=(B, T // tm, W // tc),
        in_specs=[hs, hs, pl.BlockSpec((1, tm, tc), lambda b, t, j: (b, t, ycb0 + j)),
                  pl.BlockSpec((1, tm, tc), lambda b, t, j: (b, jnp.maximum(t - nct, 0), j))],
        out_specs=(hs, hs),
        out_shape=(jax.ShapeDtypeStruct((B, T, W), F32), jax.ShapeDtypeStruct((B, T, W), BF16)),
        compiler_params=_cparams(("parallel",) * 3, VMEM_LIMIT))(h0, h1, pyl, dy)


def _merge(bd, bl, pmg, bm, nct, tm):
    B, N, D = bd.shape

    def body(bd_ref, bl_ref, m0_ref, m1_ref, b_ref, o_ref):
        g0 = _sigmoid(m0_ref[0] + b_ref[:, 0:D])
        g1 = _sigmoid(m1_ref[0] + b_ref[:, D:2 * D])
        o_ref[0] = (g0 * bd_ref[0] + g1 * bl_ref[0]).astype(BF16)

    row = pl.BlockSpec((1, tm, D), lambda b, t: (b, t, 0))
    return pl.pallas_call(
        body, name="merge", grid=(B, N // tm),
        in_specs=[row, row, pl.BlockSpec((1, tm, D), lambda b, t: (b, t + nct, 0)),
                  pl.BlockSpec((1, tm, D), lambda b, t: (b, t + nct, 1)),
                  pl.BlockSpec((1, 2 * D), lambda b, t: (0, 0))],
        out_specs=row, out_shape=jax.ShapeDtypeStruct((B, N, D), BF16),
        compiler_params=_cparams(("parallel", "parallel"), VMEM_LIMIT))(bd, bl, pmg, pmg, bm)


def _merge_bwd(dmi, bd, bl, pmg, bm, nct, tm):
    B, N, D = bd.shape
    T = pmg.shape[1]

    def body(dm_ref, bd_ref, bl_ref, m0_ref, m1_ref, b_ref, dbd_ref, dbl_ref, dmg_ref, acc_ref):
        t = pl.program_id(1)

        @pl.when(t == 0)
        def _():
            acc_ref[...] = jnp.zeros_like(acc_ref)

        @pl.when(t < nct)
        def _():
            dmg_ref[0] = jnp.zeros_like(dmg_ref[0])

        @pl.when(t >= nct)
        def _():
            dm = dm_ref[0]
            g0 = _sigmoid(m0_ref[0] + b_ref[:, 0:D])
            g1 = _sigmoid(m1_ref[0] + b_ref[:, D:2 * D])
            dbd_ref[0] = (dm * g0).astype(BF16)
            dbl_ref[0] = (dm * g1).astype(BF16)
            d0 = dm * bd_ref[0] * g0 * (1.0 - g0)
            d1 = dm * bl_ref[0] * g1 * (1.0 - g1)
            dmg_ref[0, :, 0:D] = d0.astype(BF16)
            dmg_ref[0, :, D:2 * D] = d1.astype(BF16)
            acc_ref[0, 0:1, 0:D] += jnp.sum(d0, axis=0, keepdims=True)
            acc_ref[0, 0:1, D:2 * D] += jnp.sum(d1, axis=0, keepdims=True)

    lat = pl.BlockSpec((1, tm, D), lambda b, t: (b, jnp.maximum(t - nct, 0), 0))
    return pl.pallas_call(
        body, name="merge_bwd", grid=(B, T // tm),
        in_specs=[lat, lat, lat, pl.BlockSpec((1, tm, D), lambda b, t: (b, t, 0)),
                  pl.BlockSpec((1, tm, D), lambda b, t: (b, t, 1)),
                  pl.BlockSpec((1, 2 * D), lambda b, t: (0, 0))],
        out_specs=(lat, lat, pl.BlockSpec((1, tm, 2 * D), lambda b, t: (b, t, 0)),
                   pl.BlockSpec((1, SUBLANES, 2 * D), lambda b, t: (b, 0, 0))),
        out_shape=(jax.ShapeDtypeStruct((B, N, D), BF16), jax.ShapeDtypeStruct((B, N, D), BF16),
                   jax.ShapeDtypeStruct((B, T, 2 * D), BF16), jax.ShapeDtypeStruct((B, SUBLANES, 2 * D), F32)),
        compiler_params=_cparams(("parallel", "arbitrary"), VMEM_LIMIT))(dmi, bd, bl, pmg, pmg, bm)


def _grid_taps():
    return [((di + 1) * 3 + (dj + 1), di, dj) for di in (-1, 0, 1) for dj in (-1, 0, 1)]


def _grid_views(x, n):
    pos = lax.broadcasted_iota(jnp.int32, x.shape, 0)
    gc = jnp.bitwise_and(pos, GRID_W - 1)
    cols = {0: x,
            -1: jnp.where(gc >= 1, pltpu.roll(x, 1, 0), 0.0),
            1: jnp.where(gc <= GRID_W - 2, pltpu.roll(x, n - 1, 0), 0.0)}
    views = {}
    for dj, xc in cols.items():
        views[(0, dj)] = xc
        views[(-1, dj)] = jnp.where(pos >= GRID_W, pltpu.roll(xc, GRID_W, 0), 0.0)
        views[(1, dj)] = jnp.where(pos < n - GRID_W, pltpu.roll(xc, n - GRID_W, 0), 0.0)
    return views


def _ffn_act(F, w9, b, tc):
    B, N, C2 = F.shape
    Cf = C2 // 2
    ncb = Cf // tc

    def body(g_ref, v_ref, w_ref, b_ref, o_ref):
        xv = _grid_views(g_ref[0], N)
        conv = b_ref[...]
        for k, di, dj in _grid_taps():
            conv = conv + xv[(di, dj)] * w_ref[k:k + 1, :]
        o_ref[0] = (_gelu(conv) * v_ref[0]).astype(BF16)

    return pl.pallas_call(
        body, name="ffn_act", grid=(B, ncb),
        in_specs=[pl.BlockSpec((1, N, tc), lambda b, j: (b, 0, j)),
                  pl.BlockSpec((1, N, tc), lambda b, j: (b, 0, ncb + j)),
                  pl.BlockSpec((9, tc), lambda b, j: (0, j)),
                  pl.BlockSpec((1, tc), lambda b, j: (0, j))],
        out_specs=pl.BlockSpec((1, N, tc), lambda b, j: (b, 0, j)),
        out_shape=jax.ShapeDtypeStruct((B, N, Cf), BF16),
        compiler_params=_cparams(("parallel", "parallel"), VMEM_LIMIT))(F, F, w9, b)


def _ffn_act_bwd(F, w9, b, df, tc):
    B, N, C2 = F.shape
    Cf = C2 // 2
    ncb = Cf // tc

    def body(g_ref, v_ref, w_ref, b_ref, df_ref, dF_ref, acc_ref, dv_s):
        p = pl.program_id(2)

        @pl.when(p == 0)
        def _():
            xv = _grid_views(g_ref[0], N)
            conv = b_ref[...]
            for k, di, dj in _grid_taps():
                conv = conv + xv[(di, dj)] * w_ref[k:k + 1, :]
            df_ = df_ref[0]
            dv_s[...] = (df_ * _gelu(conv)).astype(BF16)
            dc = df_ * v_ref[0] * _dgelu(conv)
            dcv = _grid_views(dc, N)
            dx = None
            for k, di, dj in _grid_taps():
                term = dcv[(-di, -dj)] * w_ref[k:k + 1, :]
                dx = term if dx is None else dx + term
                acc_ref[0, k:k + 1, :] = jnp.sum(dc * xv[(di, dj)], axis=0, keepdims=True)
            acc_ref[0, 9:10, :] = jnp.sum(dc, axis=0, keepdims=True)
            acc_ref[0, 10:16, :] = jnp.zeros((6, tc), F32)
            dF_ref[0] = dx.astype(BF16)

        @pl.when(p == 1)
        def _():
            dF_ref[0] = dv_s[...]

    return pl.pallas_call(
        body, name="ffn_act_bwd", grid=(B, ncb, 2),
        in_specs=[pl.BlockSpec((1, N, tc), lambda b, j, p: (b, 0, j)),
                  pl.BlockSpec((1, N, tc), lambda b, j, p: (b, 0, ncb + j)),
                  pl.BlockSpec((9, tc), lambda b, j, p: (0, j)),
                  pl.BlockSpec((1, tc), lambda b, j, p: (0, j)),
                  pl.BlockSpec((1, N, tc), lambda b, j, p: (b, 0, j))],
        out_specs=(pl.BlockSpec((1, N, tc), lambda b, j, p: (b, 0, j + p * ncb)),
                   pl.BlockSpec((1, 16, tc), lambda b, j, p: (b, 0, j))),
        out_shape=(jax.ShapeDtypeStruct((B, N, C2), BF16), jax.ShapeDtypeStruct((B, 16, Cf), F32)),
        scratch_shapes=[pltpu.VMEM((N, tc), BF16)],
        compiler_params=_cparams(("parallel", "parallel", "arbitrary"), VMEM_LIMIT))(F, F, w9, b, df)


ADAM_ROWS = 512


def _adamw(w, m, v, gparts, name):
    rows, cols = w.shape
    P = gparts.shape[0]
    tr = _row_tile(rows, (P + 14) * 4 * (-(-cols // LANES) * LANES))
    c1 = 1.0 - ADAM_B1 ** ADAM_STEP
    c2 = 1.0 - ADAM_B2 ** ADAM_STEP

    def body(w_ref, m_ref, v_ref, g_ref, go_ref, d_ref, mo_ref, vo_ref):
        g = g_ref[0].astype(F32)
        for p in range(1, P):
            g = g + g_ref[p].astype(F32)
        m_ = ADAM_B1 * m_ref[...] + (1.0 - ADAM_B1) * g
        v_ = ADAM_B2 * v_ref[...] + (1.0 - ADAM_B2) * (g * g)
        go_ref[...] = g
        mo_ref[...] = m_
        vo_ref[...] = v_
        d_ref[...] = -ADAM_LR * ((m_ / c1) / (jnp.sqrt(v_ / c2) + ADAM_EPS) + ADAM_WD * w_ref[...])

    row = pl.BlockSpec((tr, cols), lambda i: (i, 0))
    out = jax.ShapeDtypeStruct((rows, cols), F32)
    return pl.pallas_call(
        body, name=name, grid=(rows // tr,),
        in_specs=[row, row, row, pl.BlockSpec((P, tr, cols), lambda i: (0, i, 0))],
        out_specs=(row, row, row, row), out_shape=(out, out, out, out),
        compiler_params=_cparams(("parallel",), VMEM_LIMIT))(w, m, v, gparts)


def _pack_rows(flat_len):
    rows = -(-flat_len // LANES)
    if rows > ADAM_ROWS:
        rows = -(-rows // ADAM_ROWS) * ADAM_ROWS
    else:
        rows = -(-rows // SUBLANES) * SUBLANES
    return rows


def _pack(arrs, lead=()):
    flat = jnp.concatenate([a.reshape(lead + (-1,)).astype(F32) for a in arrs], axis=-1)
    n = flat.shape[-1]
    rows = _pack_rows(n)
    flat = jnp.pad(flat, [(0, 0)] * len(lead) + [(0, rows * LANES - n)])
    return flat.reshape(lead + (rows, LANES))


def _unpack(buf, shapes):
    flat = buf.reshape(-1)
    out, o = [], 0
    for s in shapes:
        n = math.prod(s)
        out.append(flat[o:o + n].reshape(s))
        o += n
    return out


def _col_shards(full, n_lead):
    C = full.shape[-1]
    x = full.reshape(full.shape[:-1] + (N_DEV, C // N_DEV))
    return jnp.moveaxis(x, -2, 0)


def _from_col_shards(g):
    x = jnp.moveaxis(g, 0, -2)
    return x.reshape(x.shape[:-2] + (x.shape[-2] * x.shape[-1],))


def kernel(x, c, ctx, c_ctx, w_ada, b_ada, g_pre_mix, g_post_mix, g_pre_ffn, g_post_ffn, w_in, b_merge, dn_conv, dn_a_log, dn_dt_bias, dn_onorm, lru_conv, lru_conv_b, lru_w_rg, lru_b_rg, lru_w_ig, lru_b_ig, lru_lambda, w_branch_dn, w_branch_lru, w_out, w_up, ffn_dw, ffn_dw_b, w_down, loss_target, m_c_ctx, m_w_ada, m_b_ada, m_g_pre_mix, m_g_post_mix, m_g_pre_ffn, m_g_post_ffn, m_w_in, m_b_merge, m_dn_conv, m_dn_a_log, m_dn_dt_bias, m_dn_onorm, m_lru_conv, m_lru_conv_b, m_lru_w_rg, m_lru_b_rg, m_lru_w_ig, m_lru_b_ig, m_lru_lambda, m_w_branch_dn, m_w_branch_lru, m_w_out, m_w_up, m_ffn_dw, m_ffn_dw_b, m_w_down, v_c_ctx, v_w_ada, v_b_ada, v_g_pre_mix, v_g_post_mix, v_g_pre_ffn, v_g_post_ffn, v_w_in, v_b_merge, v_dn_conv, v_dn_a_log, v_dn_dt_bias, v_dn_onorm, v_lru_conv, v_lru_conv_b, v_lru_w_rg, v_lru_b_rg, v_lru_w_ig, v_lru_b_ig, v_lru_lambda, v_w_branch_dn, v_w_branch_lru, v_w_out, v_w_up, v_ffn_dw, v_ffn_dw_b, v_w_down):
    params = dict(c_ctx=c_ctx, w_ada=w_ada, b_ada=b_ada, g_pre_mix=g_pre_mix, g_post_mix=g_post_mix, g_pre_ffn=g_pre_ffn, g_post_ffn=g_post_ffn, w_in=w_in, b_merge=b_merge, dn_conv=dn_conv, dn_a_log=dn_a_log, dn_dt_bias=dn_dt_bias, dn_onorm=dn_onorm, lru_conv=lru_conv, lru_conv_b=lru_conv_b, lru_w_rg=lru_w_rg, lru_b_rg=lru_b_rg, lru_w_ig=lru_w_ig, lru_b_ig=lru_b_ig, lru_lambda=lru_lambda, w_branch_dn=w_branch_dn, w_branch_lru=w_branch_lru, w_out=w_out, w_up=w_up, ffn_dw=ffn_dw, ffn_dw_b=ffn_dw_b, w_down=w_down)
    mom1 = dict(c_ctx=m_c_ctx, w_ada=m_w_ada, b_ada=m_b_ada, g_pre_mix=m_g_pre_mix, g_post_mix=m_g_post_mix, g_pre_ffn=m_g_pre_ffn, g_post_ffn=m_g_post_ffn, w_in=m_w_in, b_merge=m_b_merge, dn_conv=m_dn_conv, dn_a_log=m_dn_a_log, dn_dt_bias=m_dn_dt_bias, dn_onorm=m_dn_onorm, lru_conv=m_lru_conv, lru_conv_b=m_lru_conv_b, lru_w_rg=m_lru_w_rg, lru_b_rg=m_lru_b_rg, lru_w_ig=m_lru_w_ig, lru_b_ig=m_lru_b_ig, lru_lambda=m_lru_lambda, w_branch_dn=m_w_branch_dn, w_branch_lru=m_w_branch_lru, w_out=m_w_out, w_up=m_w_up, ffn_dw=m_ffn_dw, ffn_dw_b=m_ffn_dw_b, w_down=m_w_down)
    mom2 = dict(c_ctx=v_c_ctx, w_ada=v_w_ada, b_ada=v_b_ada, g_pre_mix=v_g_pre_mix, g_post_mix=v_g_post_mix, g_pre_ffn=v_g_pre_ffn, g_post_ffn=v_g_post_ffn, w_in=v_w_in, b_merge=v_b_merge, dn_conv=v_dn_conv, dn_a_log=v_dn_a_log, dn_dt_bias=v_dn_dt_bias, dn_onorm=v_dn_onorm, lru_conv=v_lru_conv, lru_conv_b=v_lru_conv_b, lru_w_rg=v_lru_w_rg, lru_b_rg=v_lru_b_rg, lru_w_ig=v_lru_w_ig, lru_b_ig=v_lru_b_ig, lru_lambda=v_lru_lambda, w_branch_dn=v_w_branch_dn, w_branch_lru=v_w_branch_lru, w_out=v_w_out, w_up=v_w_up, ffn_dw=v_ffn_dw, ffn_dw_b=v_ffn_dw_b, w_down=v_w_down)
    names = list(params)

    B, N, D = x.shape
    NC = ctx.shape[1]
    T = NC + N
    H, hd = dn_a_log.shape[2], dn_onorm.shape[1]
    HD = H * hd
    nd = 2 * H
    W = lru_conv_b.shape[1]
    nblk, bdim = lru_w_rg.shape[2], lru_w_rg.shape[3]
    Cf = ffn_dw_b.shape[1]
    sw = w_ada.shape[2]
    tm = min(256, NC)
    nct = NC // tm
    ncc = NC // CHUNK
    nch = T // CHUNK
    R, RL = B * T, B * N
    bw = min(256, W)
    me = _my_index()
    assert NC % tm == 0 and N % tm == 0 and B + 1 <= SUBLANES and bw % bdim == 0 and D % LANES == 0

    cpad = jnp.zeros((SUBLANES, D), F32).at[:B].set(c).at[B].set(c_ctx)
    ccp = _all_gather([cpad], "ag_cond")[0].reshape(N_DEV * SUBLANES, D)
    mods_sh = _ada_fwd(ccp, w_ada[0], lax.dynamic_slice(b_ada, (0, me * sw), (1, sw)))
    lru_vecs = jnp.concatenate([lru_b_rg[0], lru_b_ig[0], lru_lambda[0], jnp.zeros_like(lru_lambda[0])], axis=0)
    gathered = _all_gather(
        [mods_sh, w_in[0].astype(BF16).T, w_up[0].astype(BF16), w_down[0].astype(BF16), w_branch_dn[0].astype(BF16),
         w_branch_lru[0].astype(BF16), w_out[0].astype(BF16), dn_conv[0], lru_conv[0], lru_vecs, ffn_dw[0].reshape(9, -1)],
        "ag_weights")
    mods_g, w_in_g, w_up_g, w_down_g, w_bdn_g, w_blru_g, w_out_g, dn_conv_g, lru_conv_g, lru_vecs_g, ffn_dw_g = gathered
    mine = lax.dynamic_slice(mods_g, (0, me * SUBLANES, 0), (N_DEV, SUBLANES, sw))
    mods = jnp.moveaxis(mine, 0, 1).reshape(SUBLANES, 6, D)[:B + 1]
    mods = jnp.pad(mods, ((0, 0), (0, SUBLANES - 6), (0, 0)))
    w_down_f = w_down_g.reshape(Cf, D)
    w_bdn_f, w_blru_f, w_out_f = w_bdn_g.reshape(HD, D), w_blru_g.reshape(W, D), w_out_g.reshape(D, D)
    dn_conv_f = _from_col_shards(dn_conv_g)
    lru_conv_f = _from_col_shards(lru_conv_g)
    lru_vecs_f = _from_col_shards(lru_vecs_g)
    lru_lam_f = lru_vecs_f[4:6]
    ffn_dw_f = _from_col_shards(ffn_dw_g)

    csh = w_in_g.shape[1]
    w_in_t = w_in_g.reshape(N_DEV * csh, D)
    o_z, o_a, o_xl = 3 * HD, 4 * HD, 4 * HD + 2 * nd
    o_yl, o_mg = o_xl + W, o_xl + 2 * W
    w_qkv, w_z = w_in_t[:o_z], w_in_t[o_z:o_a]
    w_ab = jnp.pad(w_in_t[o_a:o_xl], ((0, LANES - 2 * nd), (0, 0)))
    w_xl, w_yl, w_mg = w_in_t[o_xl:o_yl], w_in_t[o_yl:o_mg], w_in_t[o_mg:]

    def blockdiag(wg):
        per = bw // bdim
        g5 = wg.reshape(2, W // bw, per, bdim, bdim)
        eye = jnp.eye(per, dtype=wg.dtype)
        return jnp.einsum("dtpij,pq->dtpiqj", g5, eye).reshape(2, W // bw, bw, bw)

    bd_rg, bd_ig = blockdiag(lru_w_rg[0]), blockdiag(lru_w_ig[0])
    wbd = jnp.stack([bd_rg[0], bd_ig[0], bd_rg[1], bd_ig[1]]).astype(BF16)
    bias4 = jnp.stack([lru_vecs_f[0], lru_vecs_f[2], lru_vecs_f[1], lru_vecs_f[3]])
    alog_v = jnp.pad(dn_a_log.reshape(1, nd), ((0, 0), (0, LANES - nd)))
    dtb_v = jnp.pad(dn_dt_bias.reshape(1, nd), ((0, 0), (0, LANES - nd)))

    h0 = jnp.concatenate([ctx, x], axis=1)
    u1 = _norm_mod_fwd(h0, g_pre_mix, mods, 0, 1, nct, tm, "norm_mod1")
    u1f = u1.reshape(R, D)
    p_qkv = _mm(u1f, w_qkv, "nt", "mm_qkv").reshape(B, T, 3 * HD)
    p_z = _mm(u1f, w_z, "nt", "mm_z").reshape(B, T, HD)
    p_ab = _mm(u1f, w_ab, "nt", "mm_ab")
    p_xl = _mm(u1f, w_xl, "nt", "mm_xl").reshape(B, T, W)
    p_yl = _mm(u1f, w_yl, "nt", "mm_yl").reshape(B, T, W)
    p_mg = _mm(u1f, w_mg, "nt", "mm_mg").reshape(B, T, 2 * D)

    conv_offs = (-2, -1, 0, 1)
    tcc = _tile(HD, 512)
    gb = _ab_act(p_ab, alog_v, dtb_v, nd, tm)
    gb3 = gb.reshape(B, T, LANES)
    cv = _dwconv(p_qkv, 0, 3 * HD, dn_conv_f, None, conv_offs, nct, tm, tcc, "dn_conv")
    qkvn = _dn_act(cv, H, hd, tm)
    o_d0, o_d1, s_in0, s_in1 = _delta_fwd(qkvn, gb3, H, hd, ncc)
    o2 = (o_d0, o_d1)
    y_dn = _dn_out(o2, p_z, 0, dn_onorm, H, hd, nct, tm)

    tcw = _tile(W, 512)
    xc = _dwconv(p_xl, 0, W, lru_conv_f, lru_conv_b, conv_offs, nct, tm, tcw, "lru_conv")
    a2, inp2 = _lru_gates(xc, wbd, bias4, lru_lam_f, tm, bw)
    hd0 = _lru_scan(a2, inp2, 0, nct, tm, tcw)
    hd1 = _lru_scan(a2, inp2, 1, nct, tm, tcw)
    y_lru = _lru_out(hd0, hd1, p_yl, 0, nct, tm, tcw)

    bd = _mm(y_dn.reshape(RL, HD), w_bdn_f, "nn", "mm_bdn").reshape(B, N, D)
    bl = _mm(y_lru.reshape(RL, W), w_blru_f, "nn", "mm_blru").reshape(B, N, D)
    mixin = _merge(bd, bl, p_mg, b_merge, nct, tm)
    mix = _mm(mixin.reshape(RL, D), w_out_f, "nn", "mm_out").reshape(B, N, D)
    h1 = _resid_norm_fwd(x, mix, g_post_mix, mods, 2, tm, "resid_norm1")
    u2 = _norm_mod_fwd(h1, g_pre_ffn, mods, 3, 4, 0, tm, "norm_mod2")
    Fp = _mm(u2.reshape(RL, D), w_up_g, "nn", "mm_up").reshape(B, N, 2 * Cf)
    tcf = LANES
    f = _ffn_act(Fp, ffn_dw_f, ffn_dw_b, tcf)
    dproj = _mm(f.reshape(RL, Cf), w_down_f, "nn", "mm_down").reshape(B, N, D)

    dd, dh2, acc_f = _final_fwd_bwd(h1, dproj, g_post_ffn, mods, 5, loss_target, tm)
    loss = lax.psum((0.5 / D) * jnp.sum(acc_f[:, 2, :]), MESH_AXES)
    ddf = dd.reshape(RL, D)
    df = _mm(ddf, w_down_f, "nt", "mm_down_dx").reshape(B, N, Cf)
    gw_down = _mm(f.reshape(RL, Cf), ddf, "tn", "mm_down_dw", out_dtype=BF16)
    dF, acc_ffn = _ffn_act_bwd(Fp, ffn_dw_f, ffn_dw_b, df, tcf)
    dFf = dF.reshape(RL, 2 * Cf)
    du2 = _mm(dFf, w_up_g, "nt", "mm_up_dx").reshape(B, N, D)
    gw_up = _mm(u2.reshape(RL, D), dFf, "tn", "mm_up_dw", out_dtype=BF16, out_shards=N_DEV)
    dh1, acc2x, _ = _norm_mod_bwd(h1, du2, g_pre_ffn, mods, 4, 0, tm, dh2, "norm_mod2_bwd")
    dmix, acc_r1 = _resid_norm_bwd(mix, dh1, g_post_mix, mods, 2, tm, "resid_norm1_bwd")
    dmixf = dmix.reshape(RL, D)
    dmixin = _mm(dmixf, w_out_f, "nt", "mm_out_dx").reshape(B, N, D)
    gw_out = _mm(mixin.reshape(RL, D), dmixf, "tn", "mm_out_dw", out_dtype=BF16)
    dbd, dbl, dmg, acc_mg = _merge_bwd(dmixin, bd, bl, p_mg, b_merge, nct, tm)
    dy_dn = _mm(dbd.reshape(RL, D), w_bdn_f, "nt", "mm_bdn_dx").reshape(B, N, HD)
    gw_bdn = _mm(y_dn.reshape(RL, HD), dbd.reshape(RL, D), "tn", "mm_bdn_dw", out_dtype=BF16)
    dy_lru = _mm(dbl.reshape(RL, D), w_blru_f, "nt", "mm_blru_dx").reshape(B, N, W)
    gw_blru = _mm(y_lru.reshape(RL, W), dbl.reshape(RL, D), "tn", "mm_blru_dw", out_dtype=BF16)

    dh, dyl = _lru_out_bwd(hd0, hd1, p_yl, 0, dy_lru, nct, tm, tcw)
    lam0, da0 = _lru_scan_bwd(a2, hd0, dh, 0, nct, tm, tcw)
    lam1, da1 = _lru_scan_bwd(a2, hd1, dh, 1, nct, tm, tcw)
    dxc, dwbd, acc_lru = _lru_gates_bwd(xc, wbd, bias4, lru_lam_f, (lam0, lam1), (da0, da1), tm, bw)
    dxl = _dwconv(dxc, 0, W, lru_conv_f, None, tuple(-o for o in conv_offs), nct, tm, tcw, "lru_conv_dx", BF16)
    acc_lc = _dwconv_wgrad(dxc, p_xl, 0, W, conv_offs, nct, tm, tcw, "lru_conv_dw")

    do, dz, acc_on = _dn_out_bwd(o2, p_z, 0, dn_onorm, dy_dn, H, hd, nct, tm)
    dqkvn0, dqkvn1, dgb0, dgb1 = _delta_bwd(qkvn, gb3, (s_in0, s_in1), do, H, hd, ncc)
    dcv = _dn_act_bwd(cv, (dqkvn0, dqkvn1), H, hd, tm)
    dqkv = _dwconv(dcv, 0, 3 * HD, dn_conv_f, None, tuple(-o for o in conv_offs), nct, tm, tcc, "dn_conv_dx", BF16)
    acc_dc = _dwconv_wgrad(dcv, p_qkv, 0, 3 * HD, conv_offs, nct, tm, tcc, "dn_conv_dw")
    dp_ab, acc_ab = _ab_act_bwd(p_ab, gb, (dgb0.reshape(R, LANES), dgb1.reshape(R, LANES)), alog_v, dtb_v, nd, tm)

    groups = [(dqkv.reshape(R, 3 * HD), w_qkv, "qkv"), (dz.reshape(R, HD), w_z, "z"), (dp_ab, w_ab, "ab"),
              (dxl.reshape(R, W), w_xl, "xl"), (dyl.reshape(R, W), w_yl, "yl"), (dmg.reshape(R, 2 * D), w_mg, "mg")]
    du1 = None
    gw_groups = []
    for dg_, wg_, nm in groups:
        du1 = _mm(dg_, wg_, "nn", "mm_in_dx_" + nm, add=du1)
        gw_groups.append(_mm(dg_, u1f, "tn", "mm_in_dw_" + nm, out_dtype=BF16))
    gw_groups[2] = gw_groups[2][:2 * nd]
    gw_in = jnp.concatenate(gw_groups, axis=0).reshape(N_DEV, csh, D)
    grad_x, acc1x, acc1c = _norm_mod_bwd(h0, du1.reshape(B, T, D), g_pre_mix, mods, 1, nct, tm, dh1, "norm_mod1_bwd")

    zeros_d = jnp.zeros((B, D), F32)
    dm_rows = jnp.stack([acc1x[:, 0], acc1x[:, 1], acc_r1[:, 0], acc2x[:, 0], acc2x[:, 1], acc_f[:, 0]], axis=1)
    dm_ctx = jnp.stack([jnp.sum(acc1c[:, 0], 0), jnp.sum(acc1c[:, 1], 0)] + [zeros_d[0]] * 4, axis=0)[None]
    dm_pad = jnp.zeros((SUBLANES, 6 * D), F32).at[:B + 1].set(jnp.concatenate([dm_rows, dm_ctx], 0).reshape(B + 1, 6 * D))
    dm_g = _all_gather([dm_pad], "ag_dmods")[0]
    g_mine = lax.dynamic_slice(dm_g, (0, 0, me * sw), (N_DEV, SUBLANES, sw)).reshape(N_DEV * SUBLANES, sw)
    gw_ada, dcc = _ada_bwd(ccp, w_ada[0], g_mine, c_ctx.reshape(1, D), B)

    per = bw // bdim

    def diag_blocks(dwf):
        g6 = dwf.reshape(W // bw, per, bdim, per, bdim)
        return jnp.einsum("tpiqj,pq->tpij", g6, jnp.eye(per, dtype=F32)).reshape(nblk, bdim, bdim)

    g_rep = dict(
        c_ctx=dcc[0],
        g_pre_mix=jnp.sum(acc1x[:, 2] + acc1c[:, 2], 0)[None], g_post_mix=jnp.sum(acc_r1[:, 1], 0)[None],
        g_pre_ffn=jnp.sum(acc2x[:, 2], 0)[None], g_post_ffn=jnp.sum(acc_f[:, 1], 0)[None],
        b_merge=jnp.sum(acc_mg[:, 0], 0)[None],
        dn_a_log=acc_ab[0, :nd].reshape(1, 2, H), dn_dt_bias=acc_ab[1, :nd].reshape(1, 2, H),
        dn_onorm=jnp.sum(acc_on[:, 0], 0)[None],
        lru_conv_b=jnp.sum(acc_lc[:, 4], 0)[None],
        lru_w_rg=jnp.stack([diag_blocks(dwbd[0]), diag_blocks(dwbd[2])])[None],
        lru_w_ig=jnp.stack([diag_blocks(dwbd[1]), diag_blocks(dwbd[3])])[None],
        ffn_dw_b=jnp.sum(acc_ffn[:, 9], 0)[None])
    rep_names = list(g_rep)
    rep_parts = _all_gather([_pack([g_rep[n] for n in rep_names])], "ag_rep_grads")[0]
    rep_out = _adamw(_pack([params[n] for n in rep_names]), _pack([mom1[n] for n in rep_names]),
                     _pack([mom2[n] for n in rep_names]), rep_parts, "adamw_rep")
    results = {}
    for k, buf in enumerate(rep_out):
        for n, arr in zip(rep_names, _unpack(buf, [params[n].shape for n in rep_names])):
            results.setdefault(n, [None] * 4)[k] = arr

    ba_out = _adamw(_pack([b_ada]), _pack([m_b_ada]), _pack([v_b_ada]),
                    _pack([dm_g.reshape(N_DEV * SUBLANES, 6 * D)], lead=(N_DEV * SUBLANES,)), "adamw_b_ada")
    results["b_ada"] = [_unpack(buf, [b_ada.shape])[0] for buf in ba_out]

    wa_out = _adamw(w_ada[0], m_w_ada[0], v_w_ada[0], gw_ada[None], "adamw_w_ada")
    results["w_ada"] = [buf[None] for buf in wa_out]

    g_lru_conv = jnp.sum(acc_lc[:, :4], 0)
    g_dn_conv = jnp.sum(acc_dc[:, :4], 0)
    g_ffn_dw = jnp.sum(acc_ffn[:, :9], 0).reshape(3, 3, Cf)
    sm_names = ["dn_conv", "lru_conv", "lru_b_rg", "lru_b_ig", "lru_lambda", "ffn_dw"]
    sm_parts = [_col_shards(g_dn_conv, 1), _col_shards(g_lru_conv, 1),
                _col_shards(jnp.stack([acc_lru[0], acc_lru[2]]), 1), _col_shards(jnp.stack([acc_lru[1], acc_lru[3]]), 1),
                _col_shards(acc_lru[4:6], 1), _col_shards(g_ffn_dw, 2)]
    big_names = ["w_in", "w_up", "w_down", "w_branch_dn", "w_branch_lru", "w_out"]
    parts8 = [gw_in, gw_up, gw_down.reshape(N_DEV, Cf // N_DEV, D), gw_bdn.reshape(N_DEV, HD // N_DEV, D),
              gw_blru.reshape(N_DEV, W // N_DEV, D), gw_out.reshape(N_DEV, D // N_DEV, D),
              _pack(sm_parts, lead=(N_DEV,))]
    sib4 = _pair_exchange(parts8, "rs_pair")
    pair4 = [_pair_sum(p8, s4, "rs_pair_sum_" + n) for p8, s4, n in zip(parts8, sib4, big_names + ["small"])]
    recv4 = _quad_exchange(pair4, "rs_quad")
    for n, g4 in zip(big_names, recv4[:-1]):
        if n == "w_in":
            g4 = jnp.swapaxes(g4, 1, 2)
        out4 = _adamw(params[n][0], mom1[n][0], mom2[n][0], g4, "adamw_" + n)
        results[n] = [buf[None] for buf in out4]
    sm_out = _adamw(_pack([params[n] for n in sm_names]), _pack([mom1[n] for n in sm_names]),
                    _pack([mom2[n] for n in sm_names]), recv4[-1], "adamw_small")
    for k, buf in enumerate(sm_out):
        for n, arr in zip(sm_names, _unpack(buf, [params[n].shape for n in sm_names])):
            results.setdefault(n, [None] * 4)[k] = arr

    outs = [loss, grad_x]
    for k in range(4):
        outs += [results[n][k] for n in names]
    return tuple(outs)
```

```python
import functools
import math

import jax
import jax.numpy as jnp
from jax import lax
from jax.experimental import pallas as pl
from jax.experimental.pallas import tpu as pltpu

F32 = jnp.float32
BF16 = jnp.bfloat16

EPS = 1e-6
GRID_W = 64
GRID_SHIFT = 6
CHUNK = 64
LRU_C = 8.0
N_DEV = 8
ADAM_LR = 0.001
ADAM_B1 = 0.9
ADAM_B2 = 0.999
ADAM_EPS = 1e-08
ADAM_WD = 0.01
ADAM_STEP = 10

LANES = 128
SUBLANES = 8
VMEM_LIMIT = 48 * 1024 * 1024
MESH_AXES = ("x", "y", "c")
GELU_C = math.sqrt(2.0 / math.pi)


def _cparams(sem=None, vmem=None):
    kw = {}
    if sem is not None:
        kw["dimension_semantics"] = sem
    if vmem is not None:
        kw["vmem_limit_bytes"] = vmem
    return pltpu.CompilerParams(**kw)


def _tile(n, pref):
    if n <= pref:
        return n
    t = (pref // LANES) * LANES
    while n % t:
        t -= LANES
    return t


def _sigmoid(x):
    return 1.0 / (1.0 + jnp.exp(-x))


def _silu(x):
    return x * _sigmoid(x)


def _dsilu(x):
    s = _sigmoid(x)
    return s * (1.0 + x * (1.0 - s))


def _softplus(x):
    return jnp.maximum(x, 0.0) + jnp.log(1.0 + jnp.exp(-jnp.abs(x)))


def _gelu(x):
    return 0.5 * x * (1.0 + jnp.tanh(GELU_C * (x + 0.044715 * x * x * x)))


def _dgelu(x):
    t = jnp.tanh(GELU_C * (x + 0.044715 * x * x * x))
    return 0.5 * (1.0 + t) + 0.5 * x * (1.0 - t * t) * GELU_C * (1.0 + 3.0 * 0.044715 * x * x)


def _dot(a, b, dims="nn"):
    dn = {"nn": (((1,), (0,)), ((), ())),
          "nt": (((1,), (1,)), ((), ())),
          "tn": (((0,), (0,)), ((), ()))}[dims]
    return lax.dot_general(a.astype(BF16), b.astype(BF16), dn, preferred_element_type=F32)


def _split2(x):
    hi = x.astype(BF16)
    lo = (x - hi.astype(F32)).astype(BF16)
    return hi, lo


def _split3(x):
    h1 = x.astype(BF16)
    r1 = x - h1.astype(F32)
    h2 = r1.astype(BF16)
    h3 = (r1 - h2.astype(F32)).astype(BF16)
    return h1, h2, h3


def _dot_hi(a, b):
    ah, al = _split2(a)
    bh, bl = _split2(b)
    return _dot(ah, bh) + (_dot(ah, bl) + _dot(al, bh))


def _dot_mask(m, x, dims="nn"):
    h1, h2, h3 = _split3(x)
    if dims == "xtn":
        return _dot(h1, m, "tn") + (_dot(h2, m, "tn") + _dot(h3, m, "tn"))
    return _dot(m, h1, dims) + (_dot(m, h2, dims) + _dot(m, h3, dims))


def _my_index():
    return 4 * lax.axis_index("x") + 2 * lax.axis_index("y") + lax.axis_index("c")


def _hbm_call(body, xs, out_shapes, n_sems, name):
    any_spec = pl.BlockSpec(memory_space=pl.ANY)
    return pl.pallas_call(
        body, name=name, out_shape=tuple(out_shapes),
        in_specs=[any_spec] * len(xs), out_specs=tuple([any_spec] * len(out_shapes)),
        scratch_shapes=[pltpu.SemaphoreType.DMA((n_sems,)), pltpu.SemaphoreType.DMA((n_sems,)),
                        pltpu.SemaphoreType.DMA((len(xs),))],
    )(*xs)


def _all_gather(xs, name):
    n = len(xs)

    def body(*refs):
        x_refs, out_refs = refs[:n], refs[n:2 * n]
        send_sems, recv_sems, local_sems = refs[2 * n:]
        x_, y_, c_ = lax.axis_index("x"), lax.axis_index("y"), lax.axis_index("c")
        me, sibling = (x_, y_, c_), (x_, y_, 1 - c_)
        chips = [(1 - x_, y_), (x_, 1 - y_), (1 - x_, 1 - y_)]

        def slab(a, px, py, pc):
            return out_refs[a].at[4 * px + 2 * py + pc]

        def copy(a, k, block, to, src=None):
            return pltpu.make_async_remote_copy(
                src_ref=slab(a, *block) if src is None else src, dst_ref=slab(a, *block),
                send_sem=send_sems.at[7 * a + k], recv_sem=recv_sems.at[7 * a + k],
                device_id=to, device_id_type=pl.DeviceIdType.MESH)

        mine = [pltpu.make_async_copy(x_refs[a], slab(a, *me), local_sems.at[a]) for a in range(n)]
        for cp in mine:
            cp.start()
        sent = []
        for j, chip in enumerate(chips):
            sent += [copy(a, 1 + j, me, (*chip, c_), src=x_refs[a]) for a in range(n)]
        sent += [copy(a, 0, me, sibling, src=x_refs[a]) for a in range(n)]
        for cp in sent:
            cp.start()
        for j, chip in enumerate(chips):
            for a in range(n):
                copy(a, 1 + j, (*chip, c_), me).wait_recv()
                fwd = copy(a, 4 + j, (*chip, c_), sibling)
                fwd.start()
                sent.append(fwd)
        for a in range(n):
            copy(a, 0, sibling, me).wait_recv()
        for j, chip in enumerate(chips):
            for a in range(n):
                copy(a, 4 + j, (*chip, 1 - c_), me).wait_recv()
        for cp in sent:
            cp.wait_send()
        for cp in mine:
            cp.wait()

    outs = [jax.ShapeDtypeStruct((N_DEV,) + x.shape, x.dtype) for x in xs]
    return _hbm_call(body, xs, outs, 7 * n, name)


def _pair_exchange(xs, name):
    n = len(xs)

    def body(*refs):
        x_refs, out_refs = refs[:n], refs[n:2 * n]
        send_sems, recv_sems, _ = refs[2 * n:]
        x_, y_, c_ = lax.axis_index("x"), lax.axis_index("y"), lax.axis_index("c")
        copies = []
        for a in range(n):
            for j in range(4):
                copies.append(pltpu.make_async_remote_copy(
                    src_ref=x_refs[a].at[2 * j + (1 - c_)], dst_ref=out_refs[a].at[j],
                    send_sem=send_sems.at[4 * a + j], recv_sem=recv_sems.at[4 * a + j],
                    device_id=(x_, y_, 1 - c_), device_id_type=pl.DeviceIdType.MESH))
        for cp in copies:
            cp.start()
        for cp in copies:
            cp.wait()

    outs = [jax.ShapeDtypeStruct((4,) + x.shape[1:], x.dtype) for x in xs]
    return _hbm_call(body, xs, outs, 4 * n, name)


def _split_views(kind, x_ref, land_ref, k):
    x_, y_, c_ = lax.axis_index("x"), lax.axis_index("y"), lax.axis_index("c")
    if kind == "gather":
        px = 1 - x_ if (k >> 2) & 1 else x_
        py = 1 - y_ if (k >> 1) & 1 else y_
        pc = 1 - c_ if k & 1 else c_
        return x_ref, land_ref.at[4 * x_ + 2 * y_ + c_], land_ref.at[4 * px + 2 * py + pc], (px, py, pc)
    px = 1 - x_ if (k >> 1) & 1 else x_
    py = 1 - y_ if k & 1 else y_
    return x_ref.at[2 * px + py], land_ref.at[2 * x_ + y_], land_ref.at[2 * px + py], (px, py, c_)


def _split_start(xs, kind, name):
    n = len(xs)
    npeer = 7 if kind == "gather" else 3
    lands = [lax.empty(((N_DEV,) + x.shape) if kind == "gather" else x.shape, x.dtype) for x in xs]

    def body(*refs):
        x_refs, land_refs = refs[:n], refs[n:2 * n]
        send_sems, recv_sems, token = refs[2 * n], refs[2 * n + 1], refs[4 * n + 2]
        for k in range(1, npeer + 1):
            for a in range(n):
                src, dst, _, peer = _split_views(kind, x_refs[a], land_refs[a], k)
                pltpu.make_async_remote_copy(
                    src_ref=src, dst_ref=dst, send_sem=send_sems.at[npeer * a + k - 1],
                    recv_sem=recv_sems.at[npeer * a + k - 1],
                    device_id=peer, device_id_type=pl.DeviceIdType.MESH).start()
        token[...] = jnp.zeros_like(token)

    hbm = pl.BlockSpec(memory_space=pltpu.HBM)
    sem = pl.BlockSpec(memory_space=pltpu.SEMAPHORE)
    sems = pltpu.SemaphoreType.DMA((npeer * n,))
    out = pl.pallas_call(
        body, name=name,
        out_shape=(sems, sems, *[pltpu.HBM(a.shape, a.dtype) for a in list(xs) + lands],
                   jax.ShapeDtypeStruct((SUBLANES, LANES), F32)),
        in_specs=[hbm] * (2 * n),
        out_specs=(sem, sem, *[hbm] * (2 * n), pl.BlockSpec(memory_space=pltpu.VMEM)),
        input_output_aliases={i: 2 + i for i in range(2 * n)},
        compiler_params=pltpu.CompilerParams(has_side_effects=pltpu.SideEffectType.DATAFLOW_SIDE_EFFECTING),
    )(*[pltpu.with_memory_space_constraint(a, pltpu.HBM) for a in list(xs) + lands])
    return out[0], out[1], list(out[2:2 + n]), list(out[2 + n:2 + 2 * n]), out[-1]


def _split_wait(started, kind, after, name):
    send_sems_, recv_sems_, x_thru, land_thru, _ = started
    n = len(x_thru)
    npeer = 7 if kind == "gather" else 3

    def body(*refs):
        x_refs, land_refs = refs[:n], refs[n:2 * n]
        send_sems, recv_sems = refs[2 * n], refs[2 * n + 1]
        for k in range(1, npeer + 1):
            for a in range(n):
                src, _, landed, peer = _split_views(kind, x_refs[a], land_refs[a], k)
                cp = pltpu.make_async_remote_copy(
                    src_ref=src, dst_ref=landed, send_sem=send_sems.at[npeer * a + k - 1],
                    recv_sem=recv_sems.at[npeer * a + k - 1],
                    device_id=peer, device_id_type=pl.DeviceIdType.MESH)
                cp.wait_send()
                cp.wait_recv()

    hbm = pl.BlockSpec(memory_space=pltpu.HBM)
    sem = pl.BlockSpec(memory_space=pltpu.SEMAPHORE)
    out = pl.pallas_call(
        body, name=name,
        out_shape=tuple(pltpu.HBM(a.shape, a.dtype) for a in x_thru + land_thru),
        in_specs=[hbm] * (2 * n) + [sem, sem, pl.BlockSpec(memory_space=pl.ANY)],
        out_specs=tuple([hbm] * (2 * n)),
        input_output_aliases={i: i for i in range(2 * n)},
        compiler_params=pltpu.CompilerParams(has_side_effects=pltpu.SideEffectType.DATAFLOW_SIDE_EFFECTING),
    )(*x_thru, *land_thru, send_sems_, recv_sems_, after)
    return list(out[:n]), list(out[n:])


def _pair_sum(x8, r4, name):
    _, r, c = x8.shape
    tr = _row_tile(r, c * 12)

    def body(core_ref, x_ref, r_ref, o_ref):
        o_ref[...] = (x_ref[...].astype(F32) + r_ref[...].astype(F32)).astype(o_ref.dtype)

    core = lax.axis_index("c").astype(jnp.int32).reshape(1)
    return pl.pallas_call(
        body, name=name,
        grid_spec=pltpu.PrefetchScalarGridSpec(
            num_scalar_prefetch=1, grid=(4, r // tr),
            in_specs=[pl.BlockSpec((None, tr, c), lambda j, i, core_ref: (2 * j + core_ref[0], i, 0)),
                      pl.BlockSpec((None, tr, c), lambda j, i, core_ref: (j, i, 0))],
            out_specs=pl.BlockSpec((None, tr, c), lambda j, i, core_ref: (j, i, 0))),
        out_shape=jax.ShapeDtypeStruct((4, r, c), x8.dtype),
        compiler_params=_cparams(("parallel", "parallel"), VMEM_LIMIT))(core, x8, r4)


def _row_tile(r, bytes_per_row, budget=4 * 1024 * 1024):
    best = None
    for t in range(16, r + 1, 16):
        if r % t == 0 and t * bytes_per_row <= budget:
            best = t
    return best if best is not None else r


def _mm(a, b, dims, name, out_dtype=F32, add=None, out_shards=1, tm=1024, tn=1024, tk=1024):
    S = b.shape[0] if b.ndim == 3 else 1
    bshape = (b.shape[1], S * b.shape[2]) if b.ndim == 3 else b.shape
    if dims == "nn":
        (M, K), (K2, N) = a.shape, bshape
    elif dims == "nt":
        (M, K), (N, K2) = a.shape, bshape
    else:
        (K, M), (K2, N) = a.shape, bshape
    assert K == K2, (a.shape, b.shape, dims)
    cs = bshape[1] // S
    tm, tk = _tile(M, tm), _tile(K, min(tk, cs) if (S > 1 and dims == "nt") else tk)
    tn = _tile(N, min(tn, cs) if (S > 1 and dims != "nt") else min(tn, N // out_shards))
    assert cs % (tk if dims == "nt" else tn) == 0 and (N // out_shards) % tn == 0
    nk = K // tk

    def body(*refs):
        if add is None:
            a_ref, b_ref, o_ref, acc = refs
        else:
            a_ref, b_ref, c_ref, o_ref, acc = refs
        k = pl.program_id(2)

        @pl.when(k == 0)
        def _():
            acc[...] = jnp.zeros_like(acc)

        acc[...] += _dot(a_ref[...], b_ref[...], dims)

        @pl.when(k == nk - 1)
        def _():
            r = acc[...]
            if add is not None:
                r = r + c_ref[...]
            o_ref[...] = r.astype(out_dtype)

    a_spec = (pl.BlockSpec((tk, tm), lambda i, j, k: (k, i)) if dims == "tn"
              else pl.BlockSpec((tm, tk), lambda i, j, k: (i, k)))
    if S == 1:
        b_spec = (pl.BlockSpec((tn, tk), lambda i, j, k: (j, k)) if dims == "nt"
                  else pl.BlockSpec((tk, tn), lambda i, j, k: (k, j)))
    elif dims == "nt":
        per = cs // tk
        b_spec = pl.BlockSpec((None, tn, tk), lambda i, j, k: (k // per, j, k % per))
    else:
        per = cs // tn
        b_spec = pl.BlockSpec((None, tk, tn), lambda i, j, k: (j // per, k, j % per))
    in_specs = [a_spec, b_spec]
    args = [a, b]
    if add is not None:
        in_specs.append(pl.BlockSpec((tm, tn), lambda i, j, k: (i, j)))
        args.append(add)
    if out_shards == 1:
        out_spec, out_shape = pl.BlockSpec((tm, tn), lambda i, j, k: (i, j)), (M, N)
    else:
        oper = N // out_shards // tn
        out_spec = pl.BlockSpec((None, tm, tn), lambda i, j, k: (j // oper, i, j % oper))
        out_shape = (out_shards, M, N // out_shards)
    return pl.pallas_call(
        body, name=name, grid=(M // tm, N // tn, nk),
        in_specs=in_specs, out_specs=out_spec,
        out_shape=jax.ShapeDtypeStruct(out_shape, out_dtype),
        scratch_shapes=[pltpu.VMEM((tm, tn), F32)],
        compiler_params=_cparams(("parallel", "parallel", "arbitrary"), VMEM_LIMIT),
    )(*args)


def _ada_fwd(ccp, w, b):
    def body(c_ref, w_ref, b_ref, o_ref):
        o_ref[...] = _dot(_silu(c_ref[...]), w_ref[...]) + b_ref[...]

    return pl.pallas_call(
        body, name="ada_fwd", out_shape=jax.ShapeDtypeStruct((ccp.shape[0], w.shape[1]), F32),
        compiler_params=_cparams(None, VMEM_LIMIT))(ccp, w, b)


def _ada_bwd(ccp, w, g, cctx, n_loc):
    def body(c_ref, w_ref, g_ref, cc_ref, gw_ref, dc_ref):
        gw_ref[...] = _dot(_silu(c_ref[...]), g_ref[...], "tn")
        dcc = _dot(g_ref[...], w_ref[...], "nt")
        row = lax.broadcasted_iota(jnp.int32, dcc.shape, 0)
        part = jnp.sum(jnp.where(row % SUBLANES == n_loc, dcc, 0.0), axis=0, keepdims=True)
        dc_ref[...] = jnp.broadcast_to(part * _dsilu(cc_ref[...]), dc_ref.shape)

    D = ccp.shape[1]
    return pl.pallas_call(
        body, name="ada_bwd",
        out_shape=(jax.ShapeDtypeStruct(w.shape, F32), jax.ShapeDtypeStruct((SUBLANES, D), F32)),
        compiler_params=_cparams(None, VMEM_LIMIT))(ccp, w, g, cctx)


def _norm_mod_fwd(h, gain, mods, i_shift, i_scale, nct, tm, name):
    B, T, D = h.shape

    def body(h_ref, g_ref, m_ref, u_ref):
        x = h_ref[0]
        r = lax.rsqrt(jnp.mean(x * x, axis=-1, keepdims=True) + EPS)
        n = x * r * g_ref[...]
        u_ref[0] = (n * (1.0 + m_ref[0, i_scale:i_scale + 1, :]) + m_ref[0, i_shift:i_shift + 1, :]).astype(BF16)

    return pl.pallas_call(
        body, name=name, grid=(B, T // tm),
        in_specs=[pl.BlockSpec((1, tm, D), lambda b, t: (b, t, 0)),
                  pl.BlockSpec((1, D), lambda b, t: (0, 0)),
                  pl.BlockSpec((1, SUBLANES, D), lambda b, t: (jnp.where(t < nct, B, b), 0, 0))],
        out_specs=pl.BlockSpec((1, tm, D), lambda b, t: (b, t, 0)),
        out_shape=jax.ShapeDtypeStruct((B, T, D), BF16),
        compiler_params=_cparams(("parallel", "parallel"), VMEM_LIMIT),
    )(h, gain, mods)


def _norm_mod_bwd(h, du, gain, mods, i_scale, nct, tm, res, name):
    B, T, D = h.shape
    nt = T // tm

    def body(h_ref, du_ref, g_ref, m_ref, res_ref, dx_ref, ax_ref, ac_ref):
        t = pl.program_id(1)
        x = h_ref[0]
        r = lax.rsqrt(jnp.mean(x * x, axis=-1, keepdims=True) + EPS)
        nh = x * r
        g = g_ref[...]
        du_ = du_ref[0]
        dn = du_ * (1.0 + m_ref[0, i_scale:i_scale + 1, :])
        dnh = dn * g
        dx = r * (dnh - nh * jnp.mean(dnh * nh, axis=-1, keepdims=True))
        sums = (jnp.sum(du_, axis=0, keepdims=True), jnp.sum(du_ * nh * g, axis=0, keepdims=True),
                jnp.sum(dn * nh, axis=0, keepdims=True))

        @pl.when(t == 0)
        def _():
            ax_ref[...] = jnp.zeros_like(ax_ref)
            ac_ref[...] = jnp.zeros_like(ac_ref)

        def accumulate(ref):
            for i, s in enumerate(sums):
                ref[0, i:i + 1, :] += s

        @pl.when(t < nct)
        def _():
            accumulate(ac_ref)

        @pl.when(t >= nct)
        def _():
            accumulate(ax_ref)
            dx_ref[0] = dx + res_ref[0]

    lat = lambda b, t: (b, jnp.maximum(t - nct, 0), 0)
    acc = pl.BlockSpec((1, SUBLANES, D), lambda b, t: (b, 0, 0))
    return pl.pallas_call(
        body, name=name, grid=(B, nt),
        in_specs=[pl.BlockSpec((1, tm, D), lambda b, t: (b, t, 0)),
                  pl.BlockSpec((1, tm, D), lambda b, t: (b, t, 0)),
                  pl.BlockSpec((1, D), lambda b, t: (0, 0)),
                  pl.BlockSpec((1, SUBLANES, D), lambda b, t: (jnp.where(t < nct, B, b), 0, 0)),
                  pl.BlockSpec((1, tm, D), lat)],
        out_specs=(pl.BlockSpec((1, tm, D), lat), acc, acc),
        out_shape=(jax.ShapeDtypeStruct(res.shape, F32),
                   jax.ShapeDtypeStruct((B, SUBLANES, D), F32), jax.ShapeDtypeStruct((B, SUBLANES, D), F32)),
        compiler_params=_cparams(("parallel", "arbitrary"), VMEM_LIMIT),
    )(h, du, gain, mods, res)


def _resid_norm_fwd(x, y, gain, mods, i_gate, tm, name):
    B, N, D = x.shape

    def body(x_ref, y_ref, g_ref, m_ref, o_ref):
        y_ = y_ref[0]
        r = lax.rsqrt(jnp.mean(y_ * y_, axis=-1, keepdims=True) + EPS)
        o_ref[0] = x_ref[0] + y_ * r * g_ref[...] * m_ref[0, i_gate:i_gate + 1, :]

    row = pl.BlockSpec((1, tm, D), lambda b, t: (b, t, 0))
    return pl.pallas_call(
        body, name=name, grid=(B, N // tm),
        in_specs=[row, row, pl.BlockSpec((1, D), lambda b, t: (0, 0)),
                  pl.BlockSpec((1, SUBLANES, D), lambda b, t: (b, 0, 0))],
        out_specs=row, out_shape=jax.ShapeDtypeStruct((B, N, D), F32),
        compiler_params=_cparams(("parallel", "parallel"), VMEM_LIMIT),
    )(x, y, gain, mods)


def _resid_norm_bwd_math(y_, dh, g, gate):
    r = lax.rsqrt(jnp.mean(y_ * y_, axis=-1, keepdims=True) + EPS)
    nh = y_ * r
    dgate = jnp.sum(dh * nh * g, axis=0, keepdims=True)
    dn = dh * gate
    dgain = jnp.sum(dn * nh, axis=0, keepdims=True)
    dnh = dn * g
    dy = r * (dnh - nh * jnp.mean(dnh * nh, axis=-1, keepdims=True))
    return dy, dgate, dgain


def _resid_norm_bwd(y, dh, gain, mods, i_gate, tm, name):
    B, N, D = y.shape

    def body(y_ref, dh_ref, g_ref, m_ref, dy_ref, acc_ref):
        t = pl.program_id(1)
        dy, dgate, dgain = _resid_norm_bwd_math(y_ref[0], dh_ref[0], g_ref[...], m_ref[0, i_gate:i_gate + 1, :])
        dy_ref[0] = dy.astype(BF16)

        @pl.when(t == 0)
        def _():
            acc_ref[...] = jnp.zeros_like(acc_ref)

        acc_ref[0, 0:1, :] += dgate
        acc_ref[0, 1:2, :] += dgain

    row = pl.BlockSpec((1, tm, D), lambda b, t: (b, t, 0))
    return pl.pallas_call(
        body, name=name, grid=(B, N // tm),
        in_specs=[row, row, pl.BlockSpec((1, D), lambda b, t: (0, 0)),
                  pl.BlockSpec((1, SUBLANES, D), lambda b, t: (b, 0, 0))],
        out_specs=(row, pl.BlockSpec((1, SUBLANES, D), lambda b, t: (b, 0, 0))),
        out_shape=(jax.ShapeDtypeStruct((B, N, D), BF16), jax.ShapeDtypeStruct((B, SUBLANES, D), F32)),
        compiler_params=_cparams(("parallel", "arbitrary"), VMEM_LIMIT),
    )(y, dh, gain, mods)


def _final_fwd_bwd(h1, d, gain, mods, i_gate, tgt, tm):
    B, N, D = h1.shape

    def body(h_ref, d_ref, g_ref, m_ref, t_ref, dd_ref, dh_ref, acc_ref):
        t = pl.program_id(1)
        d_ = d_ref[0]
        g = g_ref[...]
        gate = m_ref[0, i_gate:i_gate + 1, :]
        r = lax.rsqrt(jnp.mean(d_ * d_, axis=-1, keepdims=True) + EPS)
        e = h_ref[0] + d_ * r * g * gate - t_ref[0]
        dh = e * (1.0 / D)
        dh_ref[0] = dh
        dy, dgate, dgain = _resid_norm_bwd_math(d_, dh, g, gate)
        dd_ref[0] = dy.astype(BF16)

        @pl.when(t == 0)
        def _():
            acc_ref[...] = jnp.zeros_like(acc_ref)

        acc_ref[0, 0:1, :] += dgate
        acc_ref[0, 1:2, :] += dgain
        acc_ref[0, 2:3, :] += jnp.sum(e * e, axis=0, keepdims=True)

    row = pl.BlockSpec((1, tm, D), lambda b, t: (b, t, 0))
    return pl.pallas_call(
        body, name="final_fwd_bwd", grid=(B, N // tm),
        in_specs=[row, row, pl.BlockSpec((1, D), lambda b, t: (0, 0)),
                  pl.BlockSpec((1, SUBLANES, D), lambda b, t: (b, 0, 0)), row],
        out_specs=(row, row, pl.BlockSpec((1, SUBLANES, D), lambda b, t: (b, 0, 0))),
        out_shape=(jax.ShapeDtypeStruct((B, N, D), BF16), jax.ShapeDtypeStruct((B, N, D), F32),
                   jax.ShapeDtypeStruct((B, SUBLANES, D), F32)),
        compiler_params=_cparams(("parallel", "arbitrary"), VMEM_LIMIT),
    )(h1, d, gain, mods, tgt)


def _shifted(x, prev, nxt, off, row, tm):
    if off == 0:
        return x
    if off < 0:
        y = pltpu.roll(x, -off, 0)
        for r in range(-off):
            y = jnp.where(row == r, prev[SUBLANES + r + off:SUBLANES + r + off + 1, :], y)
        return y
    y = pltpu.roll(x, tm - off, 0)
    for r in range(off):
        y = jnp.where(row == tm - off + r, nxt[r:r + 1, :], y)
    return y


def _halo_specs(T, tm, tc, cb0, order):
    r8, n8 = tm // SUBLANES, T // SUBLANES
    if order == "btj":
        prev = lambda b, t, j: (b, jnp.maximum(t * r8 - 1, 0), cb0 + j)
        nxt = lambda b, t, j: (b, jnp.minimum((t + 1) * r8, n8 - 1), cb0 + j)
    else:
        prev = lambda b, j, t: (b, jnp.maximum(t * r8 - 1, 0), cb0 + j)
        nxt = lambda b, j, t: (b, jnp.minimum((t + 1) * r8, n8 - 1), cb0 + j)
    return pl.BlockSpec((1, SUBLANES, tc), prev), pl.BlockSpec((1, SUBLANES, tc), nxt)


def _seg_halos(p_ref, n_ref, t, nct, nt):
    seg_start = jnp.logical_or(t == 0, t == nct)
    seg_end = jnp.logical_or(t == nct - 1, t == nt - 1)
    prev = jnp.where(seg_start, 0.0, p_ref[0])
    nxt = jnp.where(seg_end, 0.0, n_ref[0])
    return prev, nxt


def _dwconv(x, col0, C, w, bias, offs, nct, tm, tc, name, out_dtype=F32):
    B, T, _ = x.shape
    nt = T // tm
    cb0 = col0 // tc

    def body(*refs):
        if bias is None:
            x_ref, p_ref, n_ref, w_ref, o_ref = refs
        else:
            x_ref, p_ref, n_ref, w_ref, b_ref, o_ref = refs
        t = pl.program_id(1)
        prev, nxt = _seg_halos(p_ref, n_ref, t, nct, nt)
        x_ = x_ref[0]
        row = lax.broadcasted_iota(jnp.int32, x_.shape, 0)
        acc = None
        for j, off in enumerate(offs):
            term = _shifted(x_, prev, nxt, off, row, tm) * w_ref[j:j + 1, :]
            acc = term if acc is None else acc + term
        if bias is not None:
            acc = acc + b_ref[...]
        o_ref[0] = acc.astype(out_dtype)

    pspec, nspec = _halo_specs(T, tm, tc, cb0, "btj")
    in_specs = [pl.BlockSpec((1, tm, tc), lambda b, t, j: (b, t, cb0 + j)), pspec, nspec,
                pl.BlockSpec((w.shape[0], tc), lambda b, t, j: (0, j))]
    args = [x, x, x, w]
    if bias is not None:
        in_specs.append(pl.BlockSpec((1, tc), lambda b, t, j: (0, j)))
        args.append(bias)
    return pl.pallas_call(
        body, name=name, grid=(B, nt, C // tc),
        in_specs=in_specs, out_specs=pl.BlockSpec((1, tm, tc), lambda b, t, j: (b, t, j)),
        out_shape=jax.ShapeDtypeStruct((B, T, C), out_dtype),
        compiler_params=_cparams(("parallel", "parallel", "parallel"), VMEM_LIMIT),
    )(*args)


def _dwconv_wgrad(dy, x, col0, C, offs, nct, tm, tc, name):
    B, T, _ = x.shape
    nt = T // tm
    cb0 = col0 // tc
    K = len(offs)

    def body(dy_ref, x_ref, p_ref, n_ref, acc_ref):
        t = pl.program_id(2)
        prev, nxt = _seg_halos(p_ref, n_ref, t, nct, nt)
        x_ = x_ref[0]
        dy_ = dy_ref[0]
        row = lax.broadcasted_iota(jnp.int32, x_.shape, 0)

        @pl.when(t == 0)
        def _():
            acc_ref[...] = jnp.zeros_like(acc_ref)

        for j, off in enumerate(offs):
            acc_ref[0, j:j + 1, :] += jnp.sum(dy_ * _shifted(x_, prev, nxt, off, row, tm), axis=0, keepdims=True)
        acc_ref[0, K:K + 1, :] += jnp.sum(dy_, axis=0, keepdims=True)

    pspec, nspec = _halo_specs(T, tm, tc, cb0, "bjt")
    return pl.pallas_call(
        body, name=name, grid=(B, C // tc, nt),
        in_specs=[pl.BlockSpec((1, tm, tc), lambda b, j, t: (b, t, j)),
                  pl.BlockSpec((1, tm, tc), lambda b, j, t: (b, t, cb0 + j)), pspec, nspec],
        out_specs=pl.BlockSpec((1, SUBLANES, tc), lambda b, j, t: (b, 0, j)),
        out_shape=jax.ShapeDtypeStruct((B, SUBLANES, C), F32),
        compiler_params=_cparams(("parallel", "parallel", "arbitrary"), VMEM_LIMIT),
    )(dy, x, x, x)


def _ab_act(pab, alog, dtb, nd, tm):
    R = pab.shape[0]

    def body(p_ref, al_ref, dt_ref, o_ref):
        p = p_ref[...]
        lane = lax.broadcasted_iota(jnp.int32, p.shape, 1)
        g = -jnp.exp(al_ref[...]) * _softplus(p + dt_ref[...])
        o_ref[...] = jnp.where(lane < nd, g, jnp.where(lane < 2 * nd, _sigmoid(p), 0.0))

    row = pl.BlockSpec((tm, LANES), lambda i: (i, 0))
    vec = pl.BlockSpec((1, LANES), lambda i: (0, 0))
    return pl.pallas_call(
        body, name="ab_act", grid=(R // tm,), in_specs=[row, vec, vec], out_specs=row,
        out_shape=jax.ShapeDtypeStruct((R, LANES), F32),
        compiler_params=_cparams(("parallel",), VMEM_LIMIT))(pab, alog, dtb)


def _ab_act_bwd(pab, gb, dgb, alog, dtb, nd, tm):
    R = pab.shape[0]

    def body(p_ref, gb_ref, d0_ref, d1_ref, al_ref, dt_ref, o_ref, acc_ref):
        i = pl.program_id(0)
        p = p_ref[...]
        gb_ = gb_ref[...]
        d_ = d0_ref[...] + d1_ref[...]
        lane = lax.broadcasted_iota(jnp.int32, p.shape, 1)
        is_g = lane < nd
        is_b = jnp.logical_and(lane >= nd, lane < 2 * nd)
        da = jnp.where(is_g, d_ * (-jnp.exp(al_ref[...])) * _sigmoid(p + dt_ref[...]), 0.0)
        db = jnp.where(is_b, d_ * gb_ * (1.0 - gb_), 0.0)
        o_ref[...] = (da + db).astype(BF16)

        @pl.when(i == 0)
        def _():
            acc_ref[...] = jnp.zeros_like(acc_ref)

        acc_ref[0:1, :] += jnp.sum(jnp.where(is_g, d_ * gb_, 0.0), axis=0, keepdims=True)
        acc_ref[1:2, :] += jnp.sum(da, axis=0, keepdims=True)

    row = pl.BlockSpec((tm, LANES), lambda i: (i, 0))
    vec = pl.BlockSpec((1, LANES), lambda i: (0, 0))
    return pl.pallas_call(
        body, name="ab_act_bwd", grid=(R // tm,),
        in_specs=[row, row, row, row, vec, vec],
        out_specs=(row, pl.BlockSpec((SUBLANES, LANES), lambda i: (0, 0))),
        out_shape=(jax.ShapeDtypeStruct((R, LANES), BF16), jax.ShapeDtypeStruct((SUBLANES, LANES), F32)),
        compiler_params=_cparams(("arbitrary",), VMEM_LIMIT))(pab, gb, dgb[0], dgb[1], alog, dtb)


def _dn_act(cv, H, hd, tm):
    B, T, C3 = cv.shape
    qscale = float(hd) ** -0.5

    def body(c_ref, o_ref):
        part = pl.program_id(0)
        for h in range(H):
            sl = slice(h * hd, (h + 1) * hd)
            s = _silu(c_ref[0, :, sl])
            rs = lax.rsqrt(jnp.sum(s * s, axis=-1, keepdims=True) + EPS)
            scale = jnp.where(part == 0, rs * qscale, jnp.where(part == 1, rs, 1.0))
            o_ref[0, :, sl] = s * scale

    spec = pl.BlockSpec((1, tm, H * hd), lambda p, b, t: (b, t, p))
    return pl.pallas_call(
        body, name="dn_act", grid=(3, B, T // tm), in_specs=[spec], out_specs=spec,
        out_shape=jax.ShapeDtypeStruct((B, T, C3), F32),
        compiler_params=_cparams(("parallel",) * 3, VMEM_LIMIT))(cv)


def _dn_act_bwd(cv, dqkv, H, hd, tm):
    B, T, C3 = cv.shape
    qscale = float(hd) ** -0.5

    def body(c_ref, d0_ref, d1_ref, o_ref):
        part = pl.program_id(0)
        for h in range(H):
            sl = slice(h * hd, (h + 1) * hd)
            c = c_ref[0, :, sl]
            s = _silu(c)
            dout = d0_ref[0, :, sl] + d1_ref[0, :, sl]
            rs = lax.rsqrt(jnp.sum(s * s, axis=-1, keepdims=True) + EPS)
            sh = s * rs
            dnorm = rs * (dout - sh * jnp.sum(dout * sh, axis=-1, keepdims=True))
            ds = jnp.where(part == 0, dnorm * qscale, jnp.where(part == 1, dnorm, dout))
            o_ref[0, :, sl] = ds * _dsilu(c)

    spec = pl.BlockSpec((1, tm, H * hd), lambda p, b, t: (b, t, p))
    return pl.pallas_call(
        body, name="dn_act_bwd", grid=(3, B, T // tm), in_specs=[spec, spec, spec], out_specs=spec,
        out_shape=jax.ShapeDtypeStruct((B, T, C3), F32),
        compiler_params=_cparams(("parallel",) * 3, VMEM_LIMIT))(cv, dqkv[0], dqkv[1])


def _chunk_of(d, s, ncc, nch):
    if d == 0:
        return s
    return jnp.where(s < ncc, ncc - 1 - s, nch - 1 - (s - ncc))


def _delta_masks(d):
    row = lax.broadcasted_iota(jnp.int32, (CHUNK, CHUNK), 0)
    col = lax.broadcasted_iota(jnp.int32, (CHUNK, CHUNK), 1)
    sign = 1 - 2 * d
    diff = (row - col) * sign
    return diff >= 0, diff > 0, diff <= 0


def _each(f, *lists):
    return [f(*a) for a in zip(*lists)]


def _unit_tri_inverse(As):
    C = As[0].shape[0]
    row = lax.broadcasted_iota(jnp.int32, (C, C), 0)
    col = lax.broadcasted_iota(jnp.int32, (C, C), 1)
    eye = jnp.where(row == col, 1.0, 0.0).astype(F32)
    same = lambda shift: jnp.right_shift(row, shift) == jnp.right_shift(col, shift)
    in8 = same(3)
    xs = [-jnp.where(in8, a, 0.0) for a in As]
    ts = [eye + x for x in xs]
    ps = _each(lambda x: _dot(x, x), xs)
    ts = _each(lambda t, p: t + _dot(t, p), ts, ps)
    ps = _each(lambda p: _dot(p, p), ps)
    ts = _each(lambda t, p: t + _dot(t, p), ts, ps)
    shift = 3
    while (1 << shift) < C:
        off = jnp.logical_and(same(shift + 1), jnp.right_shift(row, shift) != jnp.right_shift(col, shift))
        los = [jnp.where(off, a, 0.0) for a in As]
        ts = _each(lambda t, lo: t - _dot(t, _dot(lo, t)), ts, los)
        shift += 1
    rs = _each(lambda a, t: eye - _dot_hi(eye + a, t), As, ts)
    return _each(lambda t, r: t + _dot(t, r), ts, rs)


def _delta_chunk_fwd(q, k, v, g_i, g_j, beta_i, gl, S, incl, strict):
    D_ = _each(lambda gi, gj, m: jnp.where(m, jnp.exp(jnp.where(m, gi - gj, 0.0)), 0.0), g_i, g_j, incl)
    kb = _each(lambda k_, b: k_ * b, k, beta_i)
    A = _each(lambda kb_, k_, d, m: jnp.where(m, _dot(kb_, k_, "nt") * d, 0.0), kb, k, D_, strict)
    P = _each(lambda q_, k_, d, m: jnp.where(m, _dot(q_, k_, "nt") * d, 0.0), q, k, D_, incl)
    Tm = _unit_tri_inverse(A)
    gam = _each(jnp.exp, g_i)
    w = _each(lambda t, kb_, g: _dot(t, kb_ * g), Tm, kb, gam)
    u = _each(lambda t, v_, b: _dot(t, v_ * b), Tm, v, beta_i)
    vn = _each(lambda u_, w_, s: u_ - _dot(w_, s), u, w, S)
    qg = _each(lambda q_, g: q_ * g, q, gam)
    o = _each(lambda qg_, s, p, vn_: _dot(qg_, s) + _dot(p, vn_), qg, S, P, vn)
    e_i = _each(lambda gl_, gi: jnp.exp(gl_ - gi), gl, g_i)
    kd = _each(lambda k_, e: k_ * e, k, e_i)
    Gam = _each(jnp.exp, gl)
    S_new = _each(lambda s, g, kd_, vn_: s * g + _dot(kd_, vn_, "tn"), S, Gam, kd, vn)
    return dict(D=D_, kb=kb, A=A, Tm=Tm, gam=gam, w=w, u=u, vn=vn, P=P, qg=qg, o=o, e=e_i, kd=kd, Gam=Gam, S_new=S_new)


def _delta_gb(gb_ref, d, incl, incl_t, H):
    gb = gb_ref[0]
    g = gb[:, d * H:(d + 1) * H]
    beta = gb[:, (2 + d) * H:(3 + d) * H]
    gc_col = _dot_mask(incl.astype(BF16), g)
    gc_row = _dot_mask(incl_t.astype(BF16), g, "xtn")
    gl = jnp.sum(g, axis=0, keepdims=True)
    return beta, gc_col, gc_row, gl


def _delta_fwd(qkvn, gb, H, hd, ncc):
    B, T, _ = qkvn.shape
    nch = T // CHUNK
    HD = H * hd
    pairs = [(d, h) for d in range(2) for h in range(H)]

    def body(q0, k0, v0, gb0, q1, k1, v1, gb1, o0, o1, sin0, sin1, S_ref):
        s = pl.program_id(1)
        q_refs, k_refs, v_refs, gb_refs = (q0, q1), (k0, k1), (v0, v1), (gb0, gb1)
        o_refs, sin_refs = (o0, o1), (sin0, sin1)

        @pl.when(s == 0)
        def _():
            S_ref[...] = jnp.zeros_like(S_ref)

        masks = [_delta_masks(d) for d in range(2)]
        gbs = [_delta_gb(gb_refs[d], d, masks[d][0], masks[d][2], H) for d in range(2)]
        head = lambda ref, h: ref[0, :, h * hd:(h + 1) * hd]
        S = [S_ref[d, h] for d, h in pairs]
        r = _delta_chunk_fwd([head(q_refs[d], h) for d, h in pairs], [head(k_refs[d], h) for d, h in pairs],
                             [head(v_refs[d], h) for d, h in pairs],
                             [gbs[d][1][:, h:h + 1] for d, h in pairs], [gbs[d][2][h:h + 1, :] for d, h in pairs],
                             [gbs[d][0][:, h:h + 1] for d, h in pairs], [gbs[d][3][:, h:h + 1] for d, h in pairs],
                             S, [masks[d][0] for d, h in pairs], [masks[d][1] for d, h in pairs])
        for i, (d, h) in enumerate(pairs):
            sin_refs[d][0, 0, h] = S[i]
            S_ref[d, h] = r["S_new"][i]
            o_refs[d][0, :, h * hd:(h + 1) * hd] = r["o"][i]

    def dir_specs(d):
        cidx = lambda b, s: _chunk_of(d, s, ncc, nch)
        ins = [pl.BlockSpec((1, CHUNK, HD), lambda b, s, p=p: (b, cidx(b, s), p)) for p in range(3)]
        ins.append(pl.BlockSpec((1, CHUNK, LANES), lambda b, s: (b, cidx(b, s), 0)))
        return (ins, pl.BlockSpec((1, CHUNK, HD), lambda b, s: (b, cidx(b, s), 0)),
                pl.BlockSpec((1, 1, H, hd, hd), lambda b, s: (b, cidx(b, s), 0, 0, 0)))

    (in0, o_s0, sin_s0), (in1, o_s1, sin_s1) = dir_specs(0), dir_specs(1)
    o_shape = jax.ShapeDtypeStruct((B, T, HD), F32)
    sin_shape = jax.ShapeDtypeStruct((B, nch, H, hd, hd), F32)
    return pl.pallas_call(
        body, name="delta_fwd", grid=(B, nch),
        in_specs=in0 + in1, out_specs=(o_s0, o_s1, sin_s0, sin_s1),
        out_shape=(o_shape, o_shape, sin_shape, sin_shape),
        scratch_shapes=[pltpu.VMEM((2, H, hd, hd), F32)],
        compiler_params=_cparams(("parallel", "arbitrary"), VMEM_LIMIT),
    )(qkvn, qkvn, qkvn, gb, qkvn, qkvn, qkvn, gb)


def _delta_bwd(qkvn, gb, sin, do, H, hd, ncc):
    B, T, _ = qkvn.shape
    nch = T // CHUNK
    HD = H * hd
    pairs = [(d, h) for d in range(2) for h in range(H)]

    def body(q0, k0, v0, gb0, sin0, do0, q1, k1, v1, gb1, sin1, do1, dqkv0, dqkv1, dgb0, dgb1, dS_ref):
        s = pl.program_id(1)
        q_refs, k_refs, v_refs, gb_refs = (q0, q1), (k0, k1), (v0, v1), (gb0, gb1)
        sin_refs, do_refs, dqkv_refs, dgb_refs = (sin0, sin1), (do0, do1), (dqkv0, dqkv1), (dgb0, dgb1)

        @pl.when(s == 0)
        def _():
            dS_ref[...] = jnp.zeros_like(dS_ref)

        masks = [_delta_masks(d) for d in range(2)]
        gbs = [_delta_gb(gb_refs[d], d, masks[d][0], masks[d][2], H) for d in range(2)]
        incl = [masks[d][0] for d, h in pairs]
        strict = [masks[d][1] for d, h in pairs]
        lane = lax.broadcasted_iota(jnp.int32, (1, LANES), 1)
        ones = jnp.ones((CHUNK, LANES), BF16)
        head = lambda ref, h: ref[0, :, h * hd:(h + 1) * hd]
        q = [head(q_refs[d], h) for d, h in pairs]
        k = [head(k_refs[d], h) for d, h in pairs]
        v = [head(v_refs[d], h) for d, h in pairs]
        do_ = [head(do_refs[d], h) for d, h in pairs]
        beta_i = [gbs[d][0][:, h:h + 1] for d, h in pairs]
        S = [sin_refs[d][0, 0, h] for d, h in pairs]
        dSn = [dS_ref[d, h] for d, h in pairs]
        f = _delta_chunk_fwd(q, k, v, [gbs[d][1][:, h:h + 1] for d, h in pairs], [gbs[d][2][h:h + 1, :] for d, h in pairs],
                             beta_i, [gbs[d][3][:, h:h + 1] for d, h in pairs], S, incl, strict)
        rsum = lambda x: jnp.sum(x, axis=1, keepdims=True)
        dvn = _each(lambda p, d_, kd, ds: _dot(p, d_, "tn") + _dot(kd, ds), f["P"], do_, f["kd"], dSn)
        dP = _each(lambda d_, vn, m: jnp.where(m, _dot(d_, vn, "nt"), 0.0), do_, f["vn"], incl)
        dqg = _each(lambda d_, s: _dot(d_, s, "nt"), do_, S)
        dS_in = _each(lambda qg, d_, g, ds, w, dv_: _dot(qg, d_, "tn") + g * ds - _dot(w, dv_, "tn"),
                      f["qg"], do_, f["Gam"], dSn, f["w"], dvn)
        dGam = _each(lambda s, ds: jnp.sum(rsum(s * ds), axis=0, keepdims=True), S, dSn)
        dkd = _each(lambda vn, ds: _dot(vn, ds, "nt"), f["vn"], dSn)
        de = _each(lambda dkd_, k_, e: rsum(dkd_ * k_) * e, dkd, k, f["e"])
        dw = _each(lambda dv_, s: -_dot(dv_, s, "nt"), dvn, S)
        dXw = _each(lambda t, dw_: _dot(t, dw_, "tn"), f["Tm"], dw)
        dXu = _each(lambda t, dv_: _dot(t, dv_, "tn"), f["Tm"], dvn)
        dA = _each(lambda xw, w, xu, u, m: -jnp.where(m, _dot(xw, w, "nt") + _dot(xu, u, "nt"), 0.0),
                   dXw, f["w"], dXu, f["u"], strict)
        dM = _each(lambda a, d_: a * d_, dA, f["D"])
        dN = _each(lambda p, d_: p * d_, dP, f["D"])
        dkb = _each(lambda m, k_, xw, g: _dot(m, k_) + xw * g, dM, k, dXw, f["gam"])
        dq = _each(lambda dqg_, g, n, k_: dqg_ * g + _dot(n, k_), dqg, f["gam"], dN, k)
        dk = _each(lambda dkd_, e, m, kb, n, q_, dkb_, b: dkd_ * e + _dot(m, kb, "tn") + _dot(n, q_, "tn") + dkb_ * b,
                   dkd, f["e"], dM, f["kb"], dN, q, dkb, beta_i)
        dv = _each(lambda xu, b: xu * b, dXu, beta_i)
        dbeta = _each(lambda dkb_, k_, xu, v_: rsum(dkb_ * k_) + rsum(xu * v_), dkb, k, dXu, v)
        E = _each(lambda a, A_, p, P_: a * A_ + p * P_, dA, f["A"], dP, f["P"])

        def colsum(e):
            e1, e2, e3 = _split3(e)
            return (_dot(e1, ones, "tn") + (_dot(e2, ones, "tn") + _dot(e3, ones, "tn")))[:, 0:1]

        dg = _each(lambda e, dqg_, qg, xw, kb, g, de_: rsum(e) - colsum(e) + rsum(dqg_ * qg) + rsum(xw * kb) * g - de_,
                   E, dqg, f["qg"], dXw, f["kb"], f["gam"], de)
        dgl_h = _each(lambda de_, dg_, g: jnp.sum(de_, axis=0, keepdims=True) + dg_ * g, de, dGam, f["Gam"])
        for d in range(2):
            dgc = jnp.zeros((CHUNK, LANES), F32)
            dgl = jnp.zeros((1, LANES), F32)
            for h in range(H):
                i = d * H + h
                g_lane, b_lane = d * H + h, (2 + d) * H + h
                dgc = dgc + dg[i] * (lane == g_lane).astype(F32) + dbeta[i] * (lane == b_lane).astype(F32)
                dgl = dgl + dgl_h[i] * (lane == g_lane).astype(F32)
                dS_ref[d, h] = dS_in[i]
                dqkv_refs[d][0, :, h * hd:(h + 1) * hd] = dq[i]
                dqkv_refs[d][0, :, HD + h * hd:HD + (h + 1) * hd] = dk[i]
                dqkv_refs[d][0, :, 2 * HD + h * hd:2 * HD + (h + 1) * hd] = dv[i]
            draw = _dot_mask(masks[d][0].astype(BF16), dgc, "tn") + dgl
            dgb_refs[d][0] = jnp.where(lane < 2 * H, draw, dgc)

    def dir_specs(d):
        cidx = lambda b, s: _chunk_of(d, nch - 1 - s, ncc, nch)
        ins = [pl.BlockSpec((1, CHUNK, HD), lambda b, s, p=p: (b, cidx(b, s), p)) for p in range(3)]
        ins += [pl.BlockSpec((1, CHUNK, LANES), lambda b, s: (b, cidx(b, s), 0)),
                pl.BlockSpec((1, 1, H, hd, hd), lambda b, s: (b, cidx(b, s), 0, 0, 0)),
                pl.BlockSpec((1, CHUNK, HD), lambda b, s: (b, cidx(b, s), 0))]
        return (ins, pl.BlockSpec((1, CHUNK, 3 * HD), lambda b, s: (b, cidx(b, s), 0)),
                pl.BlockSpec((1, CHUNK, LANES), lambda b, s: (b, cidx(b, s), 0)))

    (in0, dq_s0, dg_s0), (in1, dq_s1, dg_s1) = dir_specs(0), dir_specs(1)
    dq_shape = jax.ShapeDtypeStruct((B, T, 3 * HD), F32)
    dg_shape = jax.ShapeDtypeStruct((B, T, LANES), F32)
    return pl.pallas_call(
        body, name="delta_bwd", grid=(B, nch),
        in_specs=in0 + in1, out_specs=(dq_s0, dq_s1, dg_s0, dg_s1),
        out_shape=(dq_shape, dq_shape, dg_shape, dg_shape),
        scratch_shapes=[pltpu.VMEM((2, H, hd, hd), F32)],
        compiler_params=_cparams(("parallel", "arbitrary"), VMEM_LIMIT),
    )(qkvn, qkvn, qkvn, gb, sin[0], do, qkvn, qkvn, qkvn, gb, sin[1], do)


def _dn_out(o2, pz, zcb0, onorm, H, hd, nct, tm):
    B, T, HD = o2[0].shape
    N = T - nct * tm

    def body(o0_ref, o1_ref, z_ref, g_ref, y_ref):
        for h in range(H):
            sl = slice(h * hd, (h + 1) * hd)
            o = o0_ref[0, :, sl] + o1_ref[0, :, sl]
            r = lax.rsqrt(jnp.mean(o * o, axis=-1, keepdims=True) + EPS)
            y_ref[0, :, sl] = (o * r * g_ref[...] * _silu(z_ref[0, :, sl])).astype(BF16)

    return pl.pallas_call(
        body, name="dn_out", grid=(B, N // tm),
        in_specs=[pl.BlockSpec((1, tm, HD), lambda b, t: (b, t + nct, 0)),
                  pl.BlockSpec((1, tm, HD), lambda b, t: (b, t + nct, 0)),
                  pl.BlockSpec((1, tm, HD), lambda b, t: (b, t + nct, zcb0)),
                  pl.BlockSpec((1, hd), lambda b, t: (0, 0))],
        out_specs=pl.BlockSpec((1, tm, HD), lambda b, t: (b, t, 0)),
        out_shape=jax.ShapeDtypeStruct((B, N, HD), BF16),
        compiler_params=_cparams(("parallel",) * 2, VMEM_LIMIT))(o2[0], o2[1], pz, onorm)


def _dn_out_bwd(o2, pz, zcb0, onorm, dy, H, hd, nct, tm):
    B, T, HD = o2[0].shape
    nt = T // tm

    def body(o0_ref, o1_ref, z_ref, g_ref, dy_ref, do_ref, dz_ref, acc_ref):
        t = pl.program_id(1)

        @pl.when(t == 0)
        def _():
            acc_ref[...] = jnp.zeros_like(acc_ref)

        @pl.when(t < nct)
        def _():
            do_ref[0] = jnp.zeros_like(do_ref[0])
            dz_ref[0] = jnp.zeros_like(dz_ref[0])

        @pl.when(t >= nct)
        def _():
            g = g_ref[...]
            for h in range(H):
                sl = slice(h * hd, (h + 1) * hd)
                o = o0_ref[0, :, sl] + o1_ref[0, :, sl]
                z = z_ref[0, :, sl]
                dy_ = dy_ref[0, :, sl]
                r = lax.rsqrt(jnp.mean(o * o, axis=-1, keepdims=True) + EPS)
                oh = o * r
                dz_ref[0, :, sl] = (dy_ * oh * g * _dsilu(z)).astype(BF16)
                dn = dy_ * _silu(z)
                acc_ref[0, 0:1, :] += jnp.sum(dn * oh, axis=0, keepdims=True)
                doh = dn * g
                do_ref[0, :, sl] = r * (doh - oh * jnp.mean(doh * oh, axis=-1, keepdims=True))

    lat = lambda b, t: (b, jnp.maximum(t - nct, 0), 0)
    row = pl.BlockSpec((1, tm, HD), lambda b, t: (b, t, 0))
    return pl.pallas_call(
        body, name="dn_out_bwd", grid=(B, nt),
        in_specs=[row, row,
                  pl.BlockSpec((1, tm, HD), lambda b, t: (b, t, zcb0)),
                  pl.BlockSpec((1, hd), lambda b, t: (0, 0)),
                  pl.BlockSpec((1, tm, HD), lat)],
        out_specs=(row, row, pl.BlockSpec((1, SUBLANES, hd), lambda b, t: (b, 0, 0))),
        out_shape=(jax.ShapeDtypeStruct((B, T, HD), F32), jax.ShapeDtypeStruct((B, T, HD), BF16),
                   jax.ShapeDtypeStruct((B, SUBLANES, hd), F32)),
        compiler_params=_cparams(("parallel", "arbitrary"), VMEM_LIMIT),
    )(o2[0], o2[1], pz, onorm, dy)


def _lru_gate_math(x, wr, wi, br, bi, lam):
    r = _sigmoid(_dot(x, wr) + br)
    i = _sigmoid(_dot(x, wi) + bi)
    sp = _softplus(-lam)
    la = -LRU_C * r * sp
    a = jnp.exp(la)
    s = jnp.sqrt(-jnp.tanh(la) * (a * a + 1.0))
    return r, i, sp, a, s


def _lru_gates(xc, wbd, bias4, lam, tm, bw):
    B, T, W = xc.shape

    def body(x_ref, w_ref, b_ref, l_ref, a_ref, i_ref):
        x = x_ref[0]
        for d in range(2):
            _, i, _, a, s = _lru_gate_math(x, w_ref[2 * d, 0], w_ref[2 * d + 1, 0],
                                           b_ref[2 * d:2 * d + 1, :], b_ref[2 * d + 1:2 * d + 2, :], l_ref[d:d + 1, :])
            a_ref[d, 0] = a
            i_ref[d, 0] = s * (i * x)

    out = pl.BlockSpec((2, 1, tm, bw), lambda b, t, j: (0, b, t, j))
    return pl.pallas_call(
        body, name="lru_gates", grid=(B, T // tm, W // bw),
        in_specs=[pl.BlockSpec((1, tm, bw), lambda b, t, j: (b, t, j)),
                  pl.BlockSpec((4, 1, bw, bw), lambda b, t, j: (0, j, 0, 0)),
                  pl.BlockSpec((4, bw), lambda b, t, j: (0, j)),
                  pl.BlockSpec((2, bw), lambda b, t, j: (0, j))],
        out_specs=(out, out),
        out_shape=(jax.ShapeDtypeStruct((2, B, T, W), F32), jax.ShapeDtypeStruct((2, B, T, W), F32)),
        compiler_params=_cparams(("parallel",) * 3, VMEM_LIMIT))(xc, wbd, bias4, lam)


def _lru_gates_bwd(xc, wbd, bias4, lam, dinp, da, tm, bw):
    B, T, W = xc.shape
    nt = T // tm

    def body(x_ref, w_ref, b_ref, l_ref, di0_ref, di1_ref, da0_ref, da1_ref, dx_ref, dw_ref, acc_ref):
        di_refs, da_refs = (di0_ref, di1_ref), (da0_ref, da1_ref)
        b_ = pl.program_id(1)
        t = pl.program_id(2)

        @pl.when(jnp.logical_and(b_ == 0, t == 0))
        def _():
            dw_ref[...] = jnp.zeros_like(dw_ref)
            acc_ref[...] = jnp.zeros_like(acc_ref)

        x = x_ref[0]
        dx = jnp.zeros_like(x)
        for d in range(2):
            wr, wi = w_ref[2 * d, 0], w_ref[2 * d + 1, 0]
            lam_ = l_ref[d:d + 1, :]
            r, i, sp, a, s = _lru_gate_math(x, wr, wi, b_ref[2 * d:2 * d + 1, :], b_ref[2 * d + 1:2 * d + 2, :], lam_)
            dinp_ = di_refs[d][0]
            dix = dinp_ * s
            ds = dinp_ * i * x
            dla = da_refs[d][0] * a - ds * (a * a) / s
            dpr = dla * (-LRU_C * sp) * r * (1.0 - r)
            dpi = dix * x * i * (1.0 - i)
            dx = dx + dix * i + _dot(dpr, wr, "nt") + _dot(dpi, wi, "nt")
            dw_ref[2 * d, 0] += _dot(x, dpr, "tn")
            dw_ref[2 * d + 1, 0] += _dot(x, dpi, "tn")
            acc_ref[2 * d:2 * d + 1, :] += jnp.sum(dpr, axis=0, keepdims=True)
            acc_ref[2 * d + 1:2 * d + 2, :] += jnp.sum(dpi, axis=0, keepdims=True)
            acc_ref[4 + d:5 + d, :] += jnp.sum(dla * (-LRU_C * r), axis=0, keepdims=True) * (-_sigmoid(-lam_))
        dx_ref[0] = dx

    tile = pl.BlockSpec((1, tm, bw), lambda j, b, t: (b, t, j))
    return pl.pallas_call(
        body, name="lru_gates_bwd", grid=(W // bw, B, nt),
        in_specs=[pl.BlockSpec((1, tm, bw), lambda j, b, t: (b, t, j)),
                  pl.BlockSpec((4, 1, bw, bw), lambda j, b, t: (0, j, 0, 0)),
                  pl.BlockSpec((4, bw), lambda j, b, t: (0, j)),
                  pl.BlockSpec((2, bw), lambda j, b, t: (0, j)), tile, tile, tile, tile],
        out_specs=(pl.BlockSpec((1, tm, bw), lambda j, b, t: (b, t, j)),
                   pl.BlockSpec((4, 1, bw, bw), lambda j, b, t: (0, j, 0, 0)),
                   pl.BlockSpec((SUBLANES, bw), lambda j, b, t: (0, j))),
        out_shape=(jax.ShapeDtypeStruct((B, T, W), F32), jax.ShapeDtypeStruct(wbd.shape, F32),
                   jax.ShapeDtypeStruct((SUBLANES, W), F32)),
        compiler_params=_cparams(("parallel", "arbitrary", "arbitrary"), VMEM_LIMIT),
    )(xc, wbd, bias4, lam, dinp[0], dinp[1], da[0], da[1])


def _tile_scan(a, x, rev, tm):
    row = lax.broadcasted_iota(jnp.int32, a.shape, 0)
    s = 1
    while s < tm:
        if rev:
            a_sh, x_sh, ok = pltpu.roll(a, tm - s, 0), pltpu.roll(x, tm - s, 0), row < tm - s
        else:
            a_sh, x_sh, ok = pltpu.roll(a, s, 0), pltpu.roll(x, s, 0), row >= s
        x = jnp.where(ok, x + a * x_sh, x)
        a = jnp.where(ok, a * a_sh, a)
        s *= 2
    return a, x


def _scan_tile_of(s, rev, nct, nt):
    if not rev:
        return s
    return jnp.where(s < nct, nct - 1 - s, nt - 1 - (s - nct))


def _lru_scan(a2, x2, d, nct, tm, tc):
    _, B, T, W = a2.shape
    nt = T // tm
    rev = d == 1
    last = 0 if rev else tm - 1

    def body(a_ref, x_ref, h_ref, carry):
        s = pl.program_id(2)

        @pl.when(s == 0)
        def _():
            carry[...] = jnp.zeros_like(carry)

        A, X = _tile_scan(a_ref[0, 0], x_ref[0, 0], rev, tm)
        h = X + A * carry[0:1, :]
        h_ref[0] = h
        carry[...] = jnp.broadcast_to(h[last:last + 1, :], carry.shape)

    tidx = lambda s: _scan_tile_of(s, rev, nct, nt)
    spec = pl.BlockSpec((1, 1, tm, tc), lambda b, j, s: (d, b, tidx(s), j))
    return pl.pallas_call(
        body, name=f"lru_scan{d}", grid=(B, W // tc, nt),
        in_specs=[spec, spec], out_specs=pl.BlockSpec((1, tm, tc), lambda b, j, s: (b, tidx(s), j)),
        out_shape=jax.ShapeDtypeStruct((B, T, W), F32),
        scratch_shapes=[pltpu.VMEM((SUBLANES, tc), F32)],
        compiler_params=_cparams(("parallel", "parallel", "arbitrary"), VMEM_LIMIT),
    )(a2, x2)


def _lru_scan_bwd(a2, h, dh, d, nct, tm, tc):
    _, B, T, W = a2.shape
    nt = T // tm
    rev = d == 1
    first = tm - 1 if rev else 0
    r8, n8 = tm // SUBLANES, T // SUBLANES

    def body(a_ref, h_ref, hp_ref, dh_ref, lam_ref, da_ref, ca, cl):
        s = pl.program_id(2)

        @pl.when(s == 0)
        def _():
            ca[...] = jnp.zeros_like(ca)
            cl[...] = jnp.zeros_like(cl)

        a = a_ref[0, 0]
        row = lax.broadcasted_iota(jnp.int32, a.shape, 0)
        if rev:
            c = jnp.where(row == 0, ca[0:1, :], pltpu.roll(a, 1, 0))
        else:
            c = jnp.where(row == tm - 1, ca[0:1, :], pltpu.roll(a, tm - 1, 0))
        C, L = _tile_scan(c, dh_ref[0], not rev, tm)
        lam = L + C * cl[0:1, :]
        lam_ref[0] = lam
        hp = jnp.where(s == nt - 1, 0.0, hp_ref[0])
        if rev:
            hprev = jnp.where(row == tm - 1, hp[0:1, :], pltpu.roll(h_ref[0], tm - 1, 0))
        else:
            hprev = jnp.where(row == 0, hp[SUBLANES - 1:SUBLANES, :], pltpu.roll(h_ref[0], 1, 0))
        da_ref[0] = lam * hprev
        ca[...] = jnp.broadcast_to(a[first:first + 1, :], ca.shape)
        cl[...] = jnp.broadcast_to(lam[first:first + 1, :], cl.shape)

    tidx = lambda s: _scan_tile_of(nt - 1 - s, rev, nct, nt)

    def hp_idx(b, j, s):
        tt = tidx(s)
        if rev:
            blk = jnp.where(tt == nt - 1, 0, jnp.minimum((tt + 1) * r8, n8 - 1))
        else:
            blk = jnp.maximum(tt * r8 - 1, 0)
        return (b, blk, j)

    tile = pl.BlockSpec((1, tm, tc), lambda b, j, s: (b, tidx(s), j))
    return pl.pallas_call(
        body, name=f"lru_scan_bwd{d}", grid=(B, W // tc, nt),
        in_specs=[pl.BlockSpec((1, 1, tm, tc), lambda b, j, s: (d, b, tidx(s), j)), tile,
                  pl.BlockSpec((1, SUBLANES, tc), hp_idx), tile],
        out_specs=(tile, tile),
        out_shape=(jax.ShapeDtypeStruct((B, T, W), F32), jax.ShapeDtypeStruct((B, T, W), F32)),
        scratch_shapes=[pltpu.VMEM((SUBLANES, tc), F32), pltpu.VMEM((SUBLANES, tc), F32)],
        compiler_params=_cparams(("parallel", "parallel", "arbitrary"), VMEM_LIMIT),
    )(a2, h, h, dh)


def _lru_out(h0, h1, pyl, ycb0, nct, tm, tc):
    B, T, W = h0.shape
    N = T - nct * tm

    def body(h0_ref, h1_ref, y_ref, o_ref):
        o_ref[0] = ((h0_ref[0] + h1_ref[0]) * _gelu(y_ref[0])).astype(BF16)

    hs = pl.BlockSpec((1, tm, tc), lambda b, t, j: (b, t + nct, j))
    return pl.pallas_call(
        body, name="lru_out", grid=(B, N // tm, W // tc),
        in_specs=[hs, hs, pl.BlockSpec((1, tm, tc), lambda b, t, j: (b, t + nct, ycb0 + j))],
        out_specs=pl.BlockSpec((1, tm, tc), lambda b, t, j: (b, t, j)),
        out_shape=jax.ShapeDtypeStruct((B, N, W), BF16),
        compiler_params=_cparams(("parallel",) * 3, VMEM_LIMIT))(h0, h1, pyl)


def _lru_out_bwd(h0, h1, pyl, ycb0, dy, nct, tm, tc):
    B, T, W = h0.shape

    def body(h0_ref, h1_ref, y_ref, dy_ref, dh_ref, dyl_ref):
        t = pl.program_id(1)

        @pl.when(t < nct)
        def _():
            dh_ref[0] = jnp.zeros_like(dh_ref[0])
            dyl_ref[0] = jnp.zeros_like(dyl_ref[0])

        @pl.when(t >= nct)
        def _():
            y = y_ref[0]
            dy_ = dy_ref[0]
            dh_ref[0] = dy_ * _gelu(y)
            dyl_ref[0] = (dy_ * (h0_ref[0] + h1_ref[0]) * _dgelu(y)).astype(BF16)

    hs = pl.BlockSpec((1, tm, tc), lambda b, t, j: (b, t, j))
    return pl.pallas_call(
        body, name="lru_out_bwd", grid=(B, T // tm, W // tc),
        in_specs=[hs, hs, pl.BlockSpec((1, tm, tc), lambda b, t, j: (b, t, ycb0 + j)),
                  pl.BlockSpec((1, tm, tc), lambda b, t, j: (b, jnp.maximum(t - nct, 0), j))],
        out_specs=(hs, hs),
        out_shape=(jax.ShapeDtypeStruct((B, T, W), F32), jax.ShapeDtypeStruct((B, T, W), BF16)),
        compiler_params=_cparams(("parallel",) * 3, VMEM_LIMIT))(h0, h1, pyl, dy)


def _merge(bd, bl, pmg, bm, nct, tm):
    B, N, D = bd.shape

    def body(bd_ref, bl_ref, m0_ref, m1_ref, b_ref, o_ref):
        g0 = _sigmoid(m0_ref[0] + b_ref[:, 0:D])
        g1 = _sigmoid(m1_ref[0] + b_ref[:, D:2 * D])
        o_ref[0] = (g0 * bd_ref[0] + g1 * bl_ref[0]).astype(BF16)

    row = pl.BlockSpec((1, tm, D), lambda b, t: (b, t, 0))
    return pl.pallas_call(
        body, name="merge", grid=(B, N // tm),
        in_specs=[row, row, pl.BlockSpec((1, tm, D), lambda b, t: (b, t + nct, 0)),
                  pl.BlockSpec((1, tm, D), lambda b, t: (b, t + nct, 1)),
                  pl.BlockSpec((1, 2 * D), lambda b, t: (0, 0))],
        out_specs=row, out_shape=jax.ShapeDtypeStruct((B, N, D), BF16),
        compiler_params=_cparams(("parallel", "parallel"), VMEM_LIMIT))(bd, bl, pmg, pmg, bm)


def _merge_bwd(dmi, bd, bl, pmg, bm, nct, tm):
    B, N, D = bd.shape
    T = pmg.shape[1]

    def body(dm_ref, bd_ref, bl_ref, m0_ref, m1_ref, b_ref, dbd_ref, dbl_ref, dmg_ref, acc_ref):
        t = pl.program_id(1)

        @pl.when(t == 0)
        def _():
            acc_ref[...] = jnp.zeros_like(acc_ref)

        @pl.when(t < nct)
        def _():
            dmg_ref[0] = jnp.zeros_like(dmg_ref[0])

        @pl.when(t >= nct)
        def _():
            dm = dm_ref[0]
            g0 = _sigmoid(m0_ref[0] + b_ref[:, 0:D])
            g1 = _sigmoid(m1_ref[0] + b_ref[:, D:2 * D])
            dbd_ref[0] = (dm * g0).astype(BF16)
            dbl_ref[0] = (dm * g1).astype(BF16)
            d0 = dm * bd_ref[0] * g0 * (1.0 - g0)
            d1 = dm * bl_ref[0] * g1 * (1.0 - g1)
            dmg_ref[0, :, 0:D] = d0.astype(BF16)
            dmg_ref[0, :, D:2 * D] = d1.astype(BF16)
            acc_ref[0, 0:1, 0:D] += jnp.sum(d0, axis=0, keepdims=True)
            acc_ref[0, 0:1, D:2 * D] += jnp.sum(d1, axis=0, keepdims=True)

    lat = pl.BlockSpec((1, tm, D), lambda b, t: (b, jnp.maximum(t - nct, 0), 0))
    return pl.pallas_call(
        body, name="merge_bwd", grid=(B, T // tm),
        in_specs=[lat, lat, lat, pl.BlockSpec((1, tm, D), lambda b, t: (b, t, 0)),
                  pl.BlockSpec((1, tm, D), lambda b, t: (b, t, 1)),
                  pl.BlockSpec((1, 2 * D), lambda b, t: (0, 0))],
        out_specs=(lat, lat, pl.BlockSpec((1, tm, 2 * D), lambda b, t: (b, t, 0)),
                   pl.BlockSpec((1, SUBLANES, 2 * D), lambda b, t: (b, 0, 0))),
        out_shape=(jax.ShapeDtypeStruct((B, N, D), BF16), jax.ShapeDtypeStruct((B, N, D), BF16),
                   jax.ShapeDtypeStruct((B, T, 2 * D), BF16), jax.ShapeDtypeStruct((B, SUBLANES, 2 * D), F32)),
        compiler_params=_cparams(("parallel", "arbitrary"), VMEM_LIMIT))(dmi, bd, bl, pmg, pmg, bm)


def _grid_taps():
    return [((di + 1) * 3 + (dj + 1), di, dj) for di in (-1, 0, 1) for dj in (-1, 0, 1)]


def _grid_views(x, n):
    pos = lax.broadcasted_iota(jnp.int32, x.shape, 0)
    gc = jnp.bitwise_and(pos, GRID_W - 1)
    cols = {0: x,
            -1: jnp.where(gc >= 1, pltpu.roll(x, 1, 0), 0.0),
            1: jnp.where(gc <= GRID_W - 2, pltpu.roll(x, n - 1, 0), 0.0)}
    views = {}
    for dj, xc in cols.items():
        views[(0, dj)] = xc
        views[(-1, dj)] = jnp.where(pos >= GRID_W, pltpu.roll(xc, GRID_W, 0), 0.0)
        views[(1, dj)] = jnp.where(pos < n - GRID_W, pltpu.roll(xc, n - GRID_W, 0), 0.0)
    return views


def _ffn_act(F, w9, b, tc):
    B, N, C2 = F.shape
    Cf = C2 // 2
    ncb = Cf // tc

    def body(g_ref, v_ref, w_ref, b_ref, o_ref):
        xv = _grid_views(g_ref[0], N)
        conv = b_ref[...]
        for k, di, dj in _grid_taps():
            conv = conv + xv[(di, dj)] * w_ref[k:k + 1, :]
        o_ref[0] = (_gelu(conv) * v_ref[0]).astype(BF16)

    return pl.pallas_call(
        body, name="ffn_act", grid=(B, ncb),
        in_specs=[pl.BlockSpec((1, N, tc), lambda b, j: (b, 0, j)),
                  pl.BlockSpec((1, N, tc), lambda b, j: (b, 0, ncb + j)),
                  pl.BlockSpec((9, tc), lambda b, j: (0, j)),
                  pl.BlockSpec((1, tc), lambda b, j: (0, j))],
        out_specs=pl.BlockSpec((1, N, tc), lambda b, j: (b, 0, j)),
        out_shape=jax.ShapeDtypeStruct((B, N, Cf), BF16),
        compiler_params=_cparams(("parallel", "parallel"), VMEM_LIMIT))(F, F, w9, b)


def _ffn_act_bwd(F, w9, b, df, tc):
    B, N, C2 = F.shape
    Cf = C2 // 2
    ncb = Cf // tc

    def body(g_ref, v_ref, w_ref, b_ref, df_ref, dF_ref, acc_ref, dv_s):
        p = pl.program_id(2)

        @pl.when(p == 0)
        def _():
            xv = _grid_views(g_ref[0], N)
            conv = b_ref[...]
            for k, di, dj in _grid_taps():
                conv = conv + xv[(di, dj)] * w_ref[k:k + 1, :]
            df_ = df_ref[0]
            dv_s[...] = (df_ * _gelu(conv)).astype(BF16)
            dc = df_ * v_ref[0] * _dgelu(conv)
            dcv = _grid_views(dc, N)
            dx = None
            for k, di, dj in _grid_taps():
                term = dcv[(-di, -dj)] * w_ref[k:k + 1, :]
                dx = term if dx is None else dx + term
                acc_ref[0, k:k + 1, :] = jnp.sum(dc * xv[(di, dj)], axis=0, keepdims=True)
            acc_ref[0, 9:10, :] = jnp.sum(dc, axis=0, keepdims=True)
            acc_ref[0, 10:16, :] = jnp.zeros((6, tc), F32)
            dF_ref[0] = dx.astype(BF16)

        @pl.when(p == 1)
        def _():
            dF_ref[0] = dv_s[...]

    return pl.pallas_call(
        body, name="ffn_act_bwd", grid=(B, ncb, 2),
        in_specs=[pl.BlockSpec((1, N, tc), lambda b, j, p: (b, 0, j)),
                  pl.BlockSpec((1, N, tc), lambda b, j, p: (b, 0, ncb + j)),
                  pl.BlockSpec((9, tc), lambda b, j, p: (0, j)),
                  pl.BlockSpec((1, tc), lambda b, j, p: (0, j)),
                  pl.BlockSpec((1, N, tc), lambda b, j, p: (b, 0, j))],
        out_specs=(pl.BlockSpec((1, N, tc), lambda b, j, p: (b, 0, j + p * ncb)),
                   pl.BlockSpec((1, 16, tc), lambda b, j, p: (b, 0, j))),
        out_shape=(jax.ShapeDtypeStruct((B, N, C2), BF16), jax.ShapeDtypeStruct((B, 16, Cf), F32)),
        scratch_shapes=[pltpu.VMEM((N, tc), BF16)],
        compiler_params=_cparams(("parallel", "parallel", "arbitrary"), VMEM_LIMIT))(F, F, w9, b, df)


ADAM_ROWS = 512


def _adamw(w, m, v, gparts, name):
    rows, cols = w.shape
    P = gparts.shape[0]
    tr = _row_tile(rows, (P + 14) * 4 * (-(-cols // LANES) * LANES))
    c1 = 1.0 - ADAM_B1 ** ADAM_STEP
    c2 = 1.0 - ADAM_B2 ** ADAM_STEP

    def body(w_ref, m_ref, v_ref, g_ref, go_ref, d_ref, mo_ref, vo_ref):
        g = g_ref[0].astype(F32)
        for p in range(1, P):
            g = g + g_ref[p].astype(F32)
        m_ = ADAM_B1 * m_ref[...] + (1.0 - ADAM_B1) * g
        v_ = ADAM_B2 * v_ref[...] + (1.0 - ADAM_B2) * (g * g)
        go_ref[...] = g
        mo_ref[...] = m_
        vo_ref[...] = v_
        d_ref[...] = -ADAM_LR * ((m_ / c1) / (jnp.sqrt(v_ / c2) + ADAM_EPS) + ADAM_WD * w_ref[...])

    row = pl.BlockSpec((tr, cols), lambda i: (i, 0))
    out = jax.ShapeDtypeStruct((rows, cols), F32)
    return pl.pallas_call(
        body, name=name, grid=(rows // tr,),
        in_specs=[row, row, row, pl.BlockSpec((P, tr, cols), lambda i: (0, i, 0))],
        out_specs=(row, row, row, row), out_shape=(out, out, out, out),
        compiler_params=_cparams(("parallel",), VMEM_LIMIT))(w, m, v, gparts)


def _pack_rows(flat_len):
    rows = -(-flat_len // LANES)
    if rows > ADAM_ROWS:
        rows = -(-rows // ADAM_ROWS) * ADAM_ROWS
    else:
        rows = -(-rows // SUBLANES) * SUBLANES
    return rows


PACK_ALIGN = SUBLANES * LANES


def _pack(arrs, lead=()):
    pads = [(0, 0)] * len(lead)
    parts = []
    for a in arrs:
        f = a.reshape(lead + (-1,)).astype(F32)
        parts.append(jnp.pad(f, pads + [(0, -f.shape[-1] % PACK_ALIGN)]))
    flat = jnp.concatenate(parts, axis=-1)
    n = flat.shape[-1]
    rows = _pack_rows(n)
    flat = jnp.pad(flat, pads + [(0, rows * LANES - n)])
    return flat.reshape(lead + (rows, LANES))


def _unpack(buf, shapes):
    out, r = [], 0
    for s in shapes:
        n = math.prod(s)
        nr = -(-n // PACK_ALIGN) * SUBLANES
        out.append(buf[r:r + nr].reshape(-1)[:n].reshape(s))
        r += nr
    return out


def _col_shards(full, n_lead):
    C = full.shape[-1]
    x = full.reshape(full.shape[:-1] + (N_DEV, C // N_DEV))
    return jnp.moveaxis(x, -2, 0)


def _from_col_shards(g):
    x = jnp.moveaxis(g, 0, -2)
    return x.reshape(x.shape[:-2] + (x.shape[-2] * x.shape[-1],))


def kernel(x, c, ctx, c_ctx, w_ada, b_ada, g_pre_mix, g_post_mix, g_pre_ffn, g_post_ffn, w_in, b_merge, dn_conv, dn_a_log, dn_dt_bias, dn_onorm, lru_conv, lru_conv_b, lru_w_rg, lru_b_rg, lru_w_ig, lru_b_ig, lru_lambda, w_branch_dn, w_branch_lru, w_out, w_up, ffn_dw, ffn_dw_b, w_down, loss_target, m_c_ctx, m_w_ada, m_b_ada, m_g_pre_mix, m_g_post_mix, m_g_pre_ffn, m_g_post_ffn, m_w_in, m_b_merge, m_dn_conv, m_dn_a_log, m_dn_dt_bias, m_dn_onorm, m_lru_conv, m_lru_conv_b, m_lru_w_rg, m_lru_b_rg, m_lru_w_ig, m_lru_b_ig, m_lru_lambda, m_w_branch_dn, m_w_branch_lru, m_w_out, m_w_up, m_ffn_dw, m_ffn_dw_b, m_w_down, v_c_ctx, v_w_ada, v_b_ada, v_g_pre_mix, v_g_post_mix, v_g_pre_ffn, v_g_post_ffn, v_w_in, v_b_merge, v_dn_conv, v_dn_a_log, v_dn_dt_bias, v_dn_onorm, v_lru_conv, v_lru_conv_b, v_lru_w_rg, v_lru_b_rg, v_lru_w_ig, v_lru_b_ig, v_lru_lambda, v_w_branch_dn, v_w_branch_lru, v_w_out, v_w_up, v_ffn_dw, v_ffn_dw_b, v_w_down):
    params = dict(c_ctx=c_ctx, w_ada=w_ada, b_ada=b_ada, g_pre_mix=g_pre_mix, g_post_mix=g_post_mix, g_pre_ffn=g_pre_ffn, g_post_ffn=g_post_ffn, w_in=w_in, b_merge=b_merge, dn_conv=dn_conv, dn_a_log=dn_a_log, dn_dt_bias=dn_dt_bias, dn_onorm=dn_onorm, lru_conv=lru_conv, lru_conv_b=lru_conv_b, lru_w_rg=lru_w_rg, lru_b_rg=lru_b_rg, lru_w_ig=lru_w_ig, lru_b_ig=lru_b_ig, lru_lambda=lru_lambda, w_branch_dn=w_branch_dn, w_branch_lru=w_branch_lru, w_out=w_out, w_up=w_up, ffn_dw=ffn_dw, ffn_dw_b=ffn_dw_b, w_down=w_down)
    mom1 = dict(c_ctx=m_c_ctx, w_ada=m_w_ada, b_ada=m_b_ada, g_pre_mix=m_g_pre_mix, g_post_mix=m_g_post_mix, g_pre_ffn=m_g_pre_ffn, g_post_ffn=m_g_post_ffn, w_in=m_w_in, b_merge=m_b_merge, dn_conv=m_dn_conv, dn_a_log=m_dn_a_log, dn_dt_bias=m_dn_dt_bias, dn_onorm=m_dn_onorm, lru_conv=m_lru_conv, lru_conv_b=m_lru_conv_b, lru_w_rg=m_lru_w_rg, lru_b_rg=m_lru_b_rg, lru_w_ig=m_lru_w_ig, lru_b_ig=m_lru_b_ig, lru_lambda=m_lru_lambda, w_branch_dn=m_w_branch_dn, w_branch_lru=m_w_branch_lru, w_out=m_w_out, w_up=m_w_up, ffn_dw=m_ffn_dw, ffn_dw_b=m_ffn_dw_b, w_down=m_w_down)
    mom2 = dict(c_ctx=v_c_ctx, w_ada=v_w_ada, b_ada=v_b_ada, g_pre_mix=v_g_pre_mix, g_post_mix=v_g_post_mix, g_pre_ffn=v_g_pre_ffn, g_post_ffn=v_g_post_ffn, w_in=v_w_in, b_merge=v_b_merge, dn_conv=v_dn_conv, dn_a_log=v_dn_a_log, dn_dt_bias=v_dn_dt_bias, dn_onorm=v_dn_onorm, lru_conv=v_lru_conv, lru_conv_b=v_lru_conv_b, lru_w_rg=v_lru_w_rg, lru_b_rg=v_lru_b_rg, lru_w_ig=v_lru_w_ig, lru_b_ig=v_lru_b_ig, lru_lambda=v_lru_lambda, w_branch_dn=v_w_branch_dn, w_branch_lru=v_w_branch_lru, w_out=v_w_out, w_up=v_w_up, ffn_dw=v_ffn_dw, ffn_dw_b=v_ffn_dw_b, w_down=v_w_down)
    names = list(params)

    B, N, D = x.shape
    NC = ctx.shape[1]
    T = NC + N
    H, hd = dn_a_log.shape[2], dn_onorm.shape[1]
    HD = H * hd
    nd = 2 * H
    W = lru_conv_b.shape[1]
    nblk, bdim = lru_w_rg.shape[2], lru_w_rg.shape[3]
    Cf = ffn_dw_b.shape[1]
    sw = w_ada.shape[2]
    tm = min(256, NC)
    nct = NC // tm
    ncc = NC // CHUNK
    nch = T // CHUNK
    R, RL = B * T, B * N
    bw = min(256, W)
    me = _my_index()
    assert NC % tm == 0 and N % tm == 0 and B + 1 <= SUBLANES and bw % bdim == 0 and D % LANES == 0

    cpad = jnp.zeros((SUBLANES, D), F32).at[:B].set(c).at[B].set(c_ctx)
    ccp = _all_gather([cpad], "ag_cond")[0].reshape(N_DEV * SUBLANES, D)
    mods_sh = _ada_fwd(ccp, w_ada[0], lax.dynamic_slice(b_ada, (0, me * sw), (1, sw)))
    lru_vecs = jnp.concatenate([lru_b_rg[0], lru_b_ig[0], lru_lambda[0], jnp.zeros_like(lru_lambda[0])], axis=0)
    mods_g, w_in_g, dn_conv_g, lru_conv_g, lru_vecs_g = _all_gather(
        [mods_sh, w_in[0].astype(BF16).T, dn_conv[0], lru_conv[0], lru_vecs], "ag_first")
    ag_rest = _split_start(
        [w_up[0].astype(BF16), w_down[0].astype(BF16), w_branch_dn[0].astype(BF16), w_branch_lru[0].astype(BF16),
         w_out[0].astype(BF16), ffn_dw[0].reshape(9, -1)], "gather", "ag_rest_start")
    mine = lax.dynamic_slice(mods_g, (0, me * SUBLANES, 0), (N_DEV, SUBLANES, sw))
    mods = jnp.moveaxis(mine, 0, 1).reshape(SUBLANES, 6, D)[:B + 1]
    mods = jnp.pad(mods, ((0, 0), (0, SUBLANES - 6), (0, 0))) + ag_rest[4][0, 0]
    dn_conv_f = _from_col_shards(dn_conv_g)
    lru_conv_f = _from_col_shards(lru_conv_g)
    lru_vecs_f = _from_col_shards(lru_vecs_g)
    lru_lam_f = lru_vecs_f[4:6]

    csh = w_in_g.shape[1]
    w_in_t = w_in_g.reshape(N_DEV * csh, D)
    o_z, o_a, o_xl = 3 * HD, 4 * HD, 4 * HD + 2 * nd
    o_yl, o_mg = o_xl + W, o_xl + 2 * W
    w_qkv, w_z = w_in_t[:o_z], w_in_t[o_z:o_a]
    w_ab = jnp.pad(w_in_t[o_a:o_xl], ((0, LANES - 2 * nd), (0, 0)))
    w_xl, w_yl, w_mg = w_in_t[o_xl:o_yl], w_in_t[o_yl:o_mg], w_in_t[o_mg:]

    def blockdiag(wg):
        per = bw // bdim
        g5 = wg.reshape(2, W // bw, per, bdim, bdim)
        eye = jnp.eye(per, dtype=wg.dtype)
        return jnp.einsum("dtpij,pq->dtpiqj", g5, eye).reshape(2, W // bw, bw, bw)

    bd_rg, bd_ig = blockdiag(lru_w_rg[0]), blockdiag(lru_w_ig[0])
    wbd = jnp.stack([bd_rg[0], bd_ig[0], bd_rg[1], bd_ig[1]]).astype(BF16)
    bias4 = jnp.stack([lru_vecs_f[0], lru_vecs_f[2], lru_vecs_f[1], lru_vecs_f[3]])
    alog_v = jnp.pad(dn_a_log.reshape(1, nd), ((0, 0), (0, LANES - nd)))
    dtb_v = jnp.pad(dn_dt_bias.reshape(1, nd), ((0, 0), (0, LANES - nd)))

    h0 = jnp.concatenate([ctx, x], axis=1)
    u1 = _norm_mod_fwd(h0, g_pre_mix, mods, 0, 1, nct, tm, "norm_mod1")
    u1f = u1.reshape(R, D)
    p_qkv = _mm(u1f, w_qkv, "nt", "mm_qkv").reshape(B, T, 3 * HD)
    p_z = _mm(u1f, w_z, "nt", "mm_z").reshape(B, T, HD)
    p_ab = _mm(u1f, w_ab, "nt", "mm_ab")
    p_xl = _mm(u1f, w_xl, "nt", "mm_xl").reshape(B, T, W)
    p_yl = _mm(u1f, w_yl, "nt", "mm_yl").reshape(B, T, W)
    p_mg = _mm(u1f, w_mg, "nt", "mm_mg").reshape(B, T, 2 * D)

    conv_offs = (-2, -1, 0, 1)
    tcc = _tile(HD, 512)
    gb = _ab_act(p_ab, alog_v, dtb_v, nd, tm)
    gb3 = gb.reshape(B, T, LANES)
    cv = _dwconv(p_qkv, 0, 3 * HD, dn_conv_f, None, conv_offs, nct, tm, tcc, "dn_conv")
    qkvn = _dn_act(cv, H, hd, tm)
    o_d0, o_d1, s_in0, s_in1 = _delta_fwd(qkvn, gb3, H, hd, ncc)
    o2 = (o_d0, o_d1)
    y_dn = _dn_out(o2, p_z, 0, dn_onorm, H, hd, nct, tm)

    tcw = _tile(W, 512)
    xc = _dwconv(p_xl, 0, W, lru_conv_f, lru_conv_b, conv_offs, nct, tm, tcw, "lru_conv")
    a2, inp2 = _lru_gates(xc, wbd, bias4, lru_lam_f, tm, bw)
    hd0 = _lru_scan(a2, inp2, 0, nct, tm, tcw)
    hd1 = _lru_scan(a2, inp2, 1, nct, tm, tcw)
    y_lru = _lru_out(hd0, hd1, p_yl, 0, nct, tm, tcw)

    rest_x, rest_land = _split_wait(ag_rest, "gather", y_lru, "ag_rest_wait")
    w_up_g, w_down_g, w_bdn_g, w_blru_g, w_out_g, ffn_dw_g = [
        lax.dynamic_update_slice(land, x_[None], (me,) + (0,) * x_.ndim) for land, x_ in zip(rest_land, rest_x)]
    w_down_f = w_down_g.reshape(Cf, D)
    w_bdn_f, w_blru_f, w_out_f = w_bdn_g.reshape(HD, D), w_blru_g.reshape(W, D), w_out_g.reshape(D, D)
    ffn_dw_f = _from_col_shards(ffn_dw_g)
    bd = _mm(y_dn.reshape(RL, HD), w_bdn_f, "nn", "mm_bdn").reshape(B, N, D)
    bl = _mm(y_lru.reshape(RL, W), w_blru_f, "nn", "mm_blru").reshape(B, N, D)
    mixin = _merge(bd, bl, p_mg, b_merge, nct, tm)
    mix = _mm(mixin.reshape(RL, D), w_out_f, "nn", "mm_out").reshape(B, N, D)
    h1 = _resid_norm_fwd(x, mix, g_post_mix, mods, 2, tm, "resid_norm1")
    u2 = _norm_mod_fwd(h1, g_pre_ffn, mods, 3, 4, 0, tm, "norm_mod2")
    Fp = _mm(u2.reshape(RL, D), w_up_g, "nn", "mm_up").reshape(B, N, 2 * Cf)
    tcf = LANES
    f = _ffn_act(Fp, ffn_dw_f, ffn_dw_b, tcf)
    dproj = _mm(f.reshape(RL, Cf), w_down_f, "nn", "mm_down").reshape(B, N, D)

    dd, dh2, acc_f = _final_fwd_bwd(h1, dproj, g_post_ffn, mods, 5, loss_target, tm)
    loss = lax.psum((0.5 / D) * jnp.sum(acc_f[:, 2, :]), MESH_AXES)
    ddf = dd.reshape(RL, D)
    df = _mm(ddf, w_down_f, "nt", "mm_down_dx").reshape(B, N, Cf)
    gw_down = _mm(f.reshape(RL, Cf), ddf, "tn", "mm_down_dw", out_dtype=BF16)
    dF, acc_ffn = _ffn_act_bwd(Fp, ffn_dw_f, ffn_dw_b, df, tcf)
    dFf = dF.reshape(RL, 2 * Cf)
    du2 = _mm(dFf, w_up_g, "nt", "mm_up_dx").reshape(B, N, D)
    gw_up = _mm(u2.reshape(RL, D), dFf, "tn", "mm_up_dw", out_dtype=BF16, out_shards=N_DEV)
    jm = 2 * lax.axis_index("x") + lax.axis_index("y")

    def pair_sums(parts8, tag):
        sib4 = _pair_exchange(parts8, "rs_pair_" + tag)
        return [_pair_sum(p8, s4, f"rs_pair_sum_{tag}{i}") for i, (p8, s4) in enumerate(zip(parts8, sib4))]

    def own_slab(land, x4):
        idx = (jm,) + (0,) * (x4.ndim - 1)
        return lax.dynamic_update_slice(land, lax.dynamic_slice(x4, idx, (1,) + x4.shape[1:]), idx)

    q_early = _split_start(pair_sums([gw_up, gw_down.reshape(N_DEV, Cf // N_DEV, D)], "ffn"), "quad", "rs_quad_ffn_start")
    mods_b = mods + q_early[4][0, 0]
    dh1, acc2x, _ = _norm_mod_bwd(h1, du2, g_pre_ffn, mods_b, 4, 0, tm, dh2, "norm_mod2_bwd")
    dmix, acc_r1 = _resid_norm_bwd(mix, dh1, g_post_mix, mods, 2, tm, "resid_norm1_bwd")
    dmixf = dmix.reshape(RL, D)
    dmixin = _mm(dmixf, w_out_f, "nt", "mm_out_dx").reshape(B, N, D)
    gw_out = _mm(mixin.reshape(RL, D), dmixf, "tn", "mm_out_dw", out_dtype=BF16)
    dbd, dbl, dmg, acc_mg = _merge_bwd(dmixin, bd, bl, p_mg, b_merge, nct, tm)
    dy_dn = _mm(dbd.reshape(RL, D), w_bdn_f, "nt", "mm_bdn_dx").reshape(B, N, HD)
    gw_bdn = _mm(y_dn.reshape(RL, HD), dbd.reshape(RL, D), "tn", "mm_bdn_dw", out_dtype=BF16)
    dy_lru = _mm(dbl.reshape(RL, D), w_blru_f, "nt", "mm_blru_dx").reshape(B, N, W)
    gw_blru = _mm(y_lru.reshape(RL, W), dbl.reshape(RL, D), "tn", "mm_blru_dw", out_dtype=BF16)

    dh, dyl = _lru_out_bwd(hd0, hd1, p_yl, 0, dy_lru, nct, tm, tcw)
    lam0, da0 = _lru_scan_bwd(a2, hd0, dh, 0, nct, tm, tcw)
    lam1, da1 = _lru_scan_bwd(a2, hd1, dh, 1, nct, tm, tcw)
    dxc, dwbd, acc_lru = _lru_gates_bwd(xc, wbd, bias4, lru_lam_f, (lam0, lam1), (da0, da1), tm, bw)
    dxl = _dwconv(dxc, 0, W, lru_conv_f, None, tuple(-o for o in conv_offs), nct, tm, tcw, "lru_conv_dx", BF16)
    acc_lc = _dwconv_wgrad(dxc, p_xl, 0, W, conv_offs, nct, tm, tcw, "lru_conv_dw")

    do, dz, acc_on = _dn_out_bwd(o2, p_z, 0, dn_onorm, dy_dn, H, hd, nct, tm)
    dqkvn0, dqkvn1, dgb0, dgb1 = _delta_bwd(qkvn, gb3, (s_in0, s_in1), do, H, hd, ncc)
    dcv = _dn_act_bwd(cv, (dqkvn0, dqkvn1), H, hd, tm)
    dqkv = _dwconv(dcv, 0, 3 * HD, dn_conv_f, None, tuple(-o for o in conv_offs), nct, tm, tcc, "dn_conv_dx", BF16)
    acc_dc = _dwconv_wgrad(dcv, p_qkv, 0, 3 * HD, conv_offs, nct, tm, tcc, "dn_conv_dw")
    dp_ab, acc_ab = _ab_act_bwd(p_ab, gb, (dgb0.reshape(R, LANES), dgb1.reshape(R, LANES)), alog_v, dtb_v, nd, tm)

    groups = [(dqkv.reshape(R, 3 * HD), w_qkv, "qkv"), (dz.reshape(R, HD), w_z, "z"), (dp_ab, w_ab, "ab"),
              (dxl.reshape(R, W), w_xl, "xl"), (dyl.reshape(R, W), w_yl, "yl"), (dmg.reshape(R, 2 * D), w_mg, "mg")]
    du1 = None
    gw_groups = []
    for dg_, wg_, nm in groups:
        du1 = _mm(dg_, wg_, "nn", "mm_in_dx_" + nm, add=du1)
        gw_groups.append(_mm(dg_, u1f, "tn", "mm_in_dw_" + nm, out_dtype=BF16))
    gw_groups[2] = gw_groups[2][:2 * nd]
    gw_in = jnp.concatenate(gw_groups, axis=0).reshape(N_DEV, csh, D)
    grad_x, acc1x, acc1c = _norm_mod_bwd(h0, du1.reshape(B, T, D), g_pre_mix, mods, 1, nct, tm, dh1, "norm_mod1_bwd")

    g_lru_conv = jnp.sum(acc_lc[:, :4], 0)
    g_dn_conv = jnp.sum(acc_dc[:, :4], 0)
    g_ffn_dw = jnp.sum(acc_ffn[:, :9], 0).reshape(3, 3, Cf)
    sm_names = ["dn_conv", "lru_conv", "lru_b_rg", "lru_b_ig", "lru_lambda", "ffn_dw"]
    sm_parts = [_col_shards(g_dn_conv, 1), _col_shards(g_lru_conv, 1),
                _col_shards(jnp.stack([acc_lru[0], acc_lru[2]]), 1), _col_shards(jnp.stack([acc_lru[1], acc_lru[3]]), 1),
                _col_shards(acc_lru[4:6], 1), _col_shards(g_ffn_dw, 2)]
    late8 = [gw_in, gw_bdn.reshape(N_DEV, HD // N_DEV, D), gw_blru.reshape(N_DEV, W // N_DEV, D),
             gw_out.reshape(N_DEV, D // N_DEV, D), _pack(sm_parts, lead=(N_DEV,))]
    q_late = _split_start(pair_sums(late8, "mix"), "quad", "rs_quad_mix_start")
    ffn_x, ffn_land = _split_wait(q_early, "quad", q_late[4], "rs_quad_ffn_wait")
    results = {}
    for n, x4, land in zip(["w_up", "w_down"], ffn_x, ffn_land):
        out4 = _adamw(params[n][0], mom1[n][0], mom2[n][0], own_slab(land, x4), "adamw_" + n)
        results[n] = [buf[None] for buf in out4]

    zeros_d = jnp.zeros((B, D), F32)
    dm_rows = jnp.stack([acc1x[:, 0], acc1x[:, 1], acc_r1[:, 0], acc2x[:, 0], acc2x[:, 1], acc_f[:, 0]], axis=1)
    dm_ctx = jnp.stack([jnp.sum(acc1c[:, 0], 0), jnp.sum(acc1c[:, 1], 0)] + [zeros_d[0]] * 4, axis=0)[None]
    dm_pad = jnp.zeros((SUBLANES, 6 * D), F32).at[:B + 1].set(jnp.concatenate([dm_rows, dm_ctx], 0).reshape(B + 1, 6 * D))
    dm_g = _all_gather([dm_pad], "ag_dmods")[0]
    g_mine = lax.dynamic_slice(dm_g, (0, 0, me * sw), (N_DEV, SUBLANES, sw)).reshape(N_DEV * SUBLANES, sw)
    gw_ada, dcc = _ada_bwd(ccp, w_ada[0], g_mine, c_ctx.reshape(1, D), B)

    per = bw // bdim

    def diag_blocks(dwf):
        g6 = dwf.reshape(W // bw, per, bdim, per, bdim)
        return jnp.einsum("tpiqj,pq->tpij", g6, jnp.eye(per, dtype=F32)).reshape(nblk, bdim, bdim)

    g_rep = dict(
        c_ctx=dcc[0],
        g_pre_mix=jnp.sum(acc1x[:, 2] + acc1c[:, 2], 0)[None], g_post_mix=jnp.sum(acc_r1[:, 1], 0)[None],
        g_pre_ffn=jnp.sum(acc2x[:, 2], 0)[None], g_post_ffn=jnp.sum(acc_f[:, 1], 0)[None],
        b_merge=jnp.sum(acc_mg[:, 0], 0)[None],
        dn_a_log=acc_ab[0, :nd].reshape(1, 2, H), dn_dt_bias=acc_ab[1, :nd].reshape(1, 2, H),
        dn_onorm=jnp.sum(acc_on[:, 0], 0)[None],
        lru_conv_b=jnp.sum(acc_lc[:, 4], 0)[None],
        lru_w_rg=jnp.stack([diag_blocks(dwbd[0]), diag_blocks(dwbd[2])])[None],
        lru_w_ig=jnp.stack([diag_blocks(dwbd[1]), diag_blocks(dwbd[3])])[None],
        ffn_dw_b=jnp.sum(acc_ffn[:, 9], 0)[None])
    rep_names = list(g_rep)
    rep_parts = _all_gather([_pack([g_rep[n] for n in rep_names])], "ag_rep_grads")[0]
    rep_out = _adamw(_pack([params[n] for n in rep_names]), _pack([mom1[n] for n in rep_names]),
                     _pack([mom2[n] for n in rep_names]), rep_parts, "adamw_rep")
    for k, buf in enumerate(rep_out):
        for n, arr in zip(rep_names, _unpack(buf, [params[n].shape for n in rep_names])):
            results.setdefault(n, [None] * 4)[k] = arr

    ba_out = _adamw(_pack([b_ada]), _pack([m_b_ada]), _pack([v_b_ada]),
                    _pack([dm_g.reshape(N_DEV * SUBLANES, 6 * D)], lead=(N_DEV * SUBLANES,)), "adamw_b_ada")
    results["b_ada"] = [_unpack(buf, [b_ada.shape])[0] for buf in ba_out]

    wa_out = _adamw(w_ada[0], m_w_ada[0], v_w_ada[0], gw_ada[None], "adamw_w_ada")
    results["w_ada"] = [buf[None] for buf in wa_out]

    mix_x, mix_land = _split_wait(q_late, "quad", wa_out[0], "rs_quad_mix_wait")
    recv4 = [own_slab(land, x4) for land, x4 in zip(mix_land, mix_x)]
    for n, g4 in zip(["w_in", "w_branch_dn", "w_branch_lru", "w_out"], recv4[:-1]):
        if n == "w_in":
            g4 = jnp.swapaxes(g4, 1, 2)
        out4 = _adamw(params[n][0], mom1[n][0], mom2[n][0], g4, "adamw_" + n)
        results[n] = [buf[None] for buf in out4]
    sm_out = _adamw(_pack([params[n] for n in sm_names]), _pack([mom1[n] for n in sm_names]),
                    _pack([mom2[n] for n in sm_names]), recv4[-1], "adamw_small")
    for k, buf in enumerate(sm_out):
        for n, arr in zip(sm_names, _unpack(buf, [params[n].shape for n in sm_names])):
            results.setdefault(n, [None] * 4)[k] = arr

    outs = [loss, grad_x]
    for k in range(4):
        outs += [results[n][k] for n in names]
    return tuple(outs)
```

```python
import functools
import math

import jax
import jax.numpy as jnp
from jax import lax
from jax.experimental import pallas as pl
from jax.experimental.pallas import tpu as pltpu

F32 = jnp.float32
BF16 = jnp.bfloat16

EPS = 1e-6
GRID_W = 64
GRID_SHIFT = 6
CHUNK = 64
LRU_C = 8.0
N_DEV = 8
ADAM_LR = 0.001
ADAM_B1 = 0.9
ADAM_B2 = 0.999
ADAM_EPS = 1e-08
ADAM_WD = 0.01
ADAM_STEP = 10

LANES = 128
SUBLANES = 8
VMEM_LIMIT = 48 * 1024 * 1024
MESH_AXES = ("x", "y", "c")
GELU_C = math.sqrt(2.0 / math.pi)


def _cparams(sem=None, vmem=None):
    kw = {}
    if sem is not None:
        kw["dimension_semantics"] = sem
    if vmem is not None:
        kw["vmem_limit_bytes"] = vmem
    return pltpu.CompilerParams(**kw)


def _tile(n, pref):
    if n <= pref:
        return n
    t = (pref // LANES) * LANES
    while n % t:
        t -= LANES
    return t


def _sigmoid(x):
    return 1.0 / (1.0 + jnp.exp(-x))


def _silu(x):
    return x * _sigmoid(x)


def _dsilu(x):
    s = _sigmoid(x)
    return s * (1.0 + x * (1.0 - s))


def _softplus(x):
    return jnp.maximum(x, 0.0) + jnp.log(1.0 + jnp.exp(-jnp.abs(x)))


def _gelu(x):
    return 0.5 * x * (1.0 + jnp.tanh(GELU_C * (x + 0.044715 * x * x * x)))


def _dgelu(x):
    t = jnp.tanh(GELU_C * (x + 0.044715 * x * x * x))
    return 0.5 * (1.0 + t) + 0.5 * x * (1.0 - t * t) * GELU_C * (1.0 + 3.0 * 0.044715 * x * x)


def _dot(a, b, dims="nn"):
    dn = {"nn": (((1,), (0,)), ((), ())),
          "nt": (((1,), (1,)), ((), ())),
          "tn": (((0,), (0,)), ((), ()))}[dims]
    return lax.dot_general(a.astype(BF16), b.astype(BF16), dn, preferred_element_type=F32)


def _split2(x):
    hi = x.astype(BF16)
    lo = (x - hi.astype(F32)).astype(BF16)
    return hi, lo


def _split3(x):
    h1 = x.astype(BF16)
    r1 = x - h1.astype(F32)
    h2 = r1.astype(BF16)
    h3 = (r1 - h2.astype(F32)).astype(BF16)
    return h1, h2, h3


def _dot_hi(a, b):
    ah, al = _split2(a)
    bh, bl = _split2(b)
    return _dot(ah, bh) + (_dot(ah, bl) + _dot(al, bh))


def _dot_mask(m, x, dims="nn"):
    h1, h2, h3 = _split3(x)
    if dims == "xtn":
        return _dot(h1, m, "tn") + (_dot(h2, m, "tn") + _dot(h3, m, "tn"))
    return _dot(m, h1, dims) + (_dot(m, h2, dims) + _dot(m, h3, dims))


def _my_index():
    return 4 * lax.axis_index("x") + 2 * lax.axis_index("y") + lax.axis_index("c")


def _hbm_call(body, xs, out_shapes, n_sems, name):
    any_spec = pl.BlockSpec(memory_space=pl.ANY)
    return pl.pallas_call(
        body, name=name, out_shape=tuple(out_shapes),
        in_specs=[any_spec] * len(xs), out_specs=tuple([any_spec] * len(out_shapes)),
        scratch_shapes=[pltpu.SemaphoreType.DMA((n_sems,)), pltpu.SemaphoreType.DMA((n_sems,)),
                        pltpu.SemaphoreType.DMA((len(xs),))],
    )(*xs)


def _all_gather(xs, name):
    n = len(xs)

    def body(*refs):
        x_refs, out_refs = refs[:n], refs[n:2 * n]
        send_sems, recv_sems, local_sems = refs[2 * n:]
        x_, y_, c_ = lax.axis_index("x"), lax.axis_index("y"), lax.axis_index("c")
        me, sibling = (x_, y_, c_), (x_, y_, 1 - c_)
        chips = [(1 - x_, y_), (x_, 1 - y_), (1 - x_, 1 - y_)]

        def slab(a, px, py, pc):
            return out_refs[a].at[4 * px + 2 * py + pc]

        def copy(a, k, block, to, src=None):
            return pltpu.make_async_remote_copy(
                src_ref=slab(a, *block) if src is None else src, dst_ref=slab(a, *block),
                send_sem=send_sems.at[7 * a + k], recv_sem=recv_sems.at[7 * a + k],
                device_id=to, device_id_type=pl.DeviceIdType.MESH)

        mine = [pltpu.make_async_copy(x_refs[a], slab(a, *me), local_sems.at[a]) for a in range(n)]
        for cp in mine:
            cp.start()
        sent = []
        for j, chip in enumerate(chips):
            sent += [copy(a, 1 + j, me, (*chip, c_), src=x_refs[a]) for a in range(n)]
        sent += [copy(a, 0, me, sibling, src=x_refs[a]) for a in range(n)]
        for cp in sent:
            cp.start()
        for j, chip in enumerate(chips):
            for a in range(n):
                copy(a, 1 + j, (*chip, c_), me).wait_recv()
                fwd = copy(a, 4 + j, (*chip, c_), sibling)
                fwd.start()
                sent.append(fwd)
        for a in range(n):
            copy(a, 0, sibling, me).wait_recv()
        for j, chip in enumerate(chips):
            for a in range(n):
                copy(a, 4 + j, (*chip, 1 - c_), me).wait_recv()
        for cp in sent:
            cp.wait_send()
        for cp in mine:
            cp.wait()

    outs = [jax.ShapeDtypeStruct((N_DEV,) + x.shape, x.dtype) for x in xs]
    return _hbm_call(body, xs, outs, 7 * n, name)


def _pair_exchange(xs, name):
    n = len(xs)

    def body(*refs):
        x_refs, out_refs = refs[:n], refs[n:2 * n]
        send_sems, recv_sems, _ = refs[2 * n:]
        x_, y_, c_ = lax.axis_index("x"), lax.axis_index("y"), lax.axis_index("c")
        copies = []
        for a in range(n):
            for j in range(4):
                copies.append(pltpu.make_async_remote_copy(
                    src_ref=x_refs[a].at[2 * j + (1 - c_)], dst_ref=out_refs[a].at[j],
                    send_sem=send_sems.at[4 * a + j], recv_sem=recv_sems.at[4 * a + j],
                    device_id=(x_, y_, 1 - c_), device_id_type=pl.DeviceIdType.MESH))
        for cp in copies:
            cp.start()
        for cp in copies:
            cp.wait()

    outs = [jax.ShapeDtypeStruct((4,) + x.shape[1:], x.dtype) for x in xs]
    return _hbm_call(body, xs, outs, 4 * n, name)


def _split_views(kind, x_ref, land_ref, k):
    x_, y_, c_ = lax.axis_index("x"), lax.axis_index("y"), lax.axis_index("c")
    if kind == "gather":
        px = 1 - x_ if (k >> 2) & 1 else x_
        py = 1 - y_ if (k >> 1) & 1 else y_
        pc = 1 - c_ if k & 1 else c_
        return x_ref, land_ref.at[4 * x_ + 2 * y_ + c_], land_ref.at[4 * px + 2 * py + pc], (px, py, pc)
    px = 1 - x_ if (k >> 1) & 1 else x_
    py = 1 - y_ if k & 1 else y_
    return x_ref.at[2 * px + py], land_ref.at[2 * x_ + y_], land_ref.at[2 * px + py], (px, py, c_)


def _split_start(xs, kind, name):
    n = len(xs)
    npeer = 7 if kind == "gather" else 3
    lands = [lax.empty(((N_DEV,) + x.shape) if kind == "gather" else x.shape, x.dtype) for x in xs]

    def body(*refs):
        x_refs, land_refs = refs[:n], refs[n:2 * n]
        send_sems, recv_sems, token = refs[2 * n], refs[2 * n + 1], refs[4 * n + 2]
        for k in range(1, npeer + 1):
            for a in range(n):
                src, dst, _, peer = _split_views(kind, x_refs[a], land_refs[a], k)
                pltpu.make_async_remote_copy(
                    src_ref=src, dst_ref=dst, send_sem=send_sems.at[npeer * a + k - 1],
                    recv_sem=recv_sems.at[npeer * a + k - 1],
                    device_id=peer, device_id_type=pl.DeviceIdType.MESH).start()
        token[...] = jnp.zeros_like(token)

    hbm = pl.BlockSpec(memory_space=pltpu.HBM)
    sem = pl.BlockSpec(memory_space=pltpu.SEMAPHORE)
    sems = pltpu.SemaphoreType.DMA((npeer * n,))
    out = pl.pallas_call(
        body, name=name,
        out_shape=(sems, sems, *[pltpu.HBM(a.shape, a.dtype) for a in list(xs) + lands],
                   jax.ShapeDtypeStruct((SUBLANES, LANES), F32)),
        in_specs=[hbm] * (2 * n),
        out_specs=(sem, sem, *[hbm] * (2 * n), pl.BlockSpec(memory_space=pltpu.VMEM)),
        input_output_aliases={i: 2 + i for i in range(2 * n)},
        compiler_params=pltpu.CompilerParams(has_side_effects=pltpu.SideEffectType.DATAFLOW_SIDE_EFFECTING),
    )(*[pltpu.with_memory_space_constraint(a, pltpu.HBM) for a in list(xs) + lands])
    return out[0], out[1], list(out[2:2 + n]), list(out[2 + n:2 + 2 * n]), out[-1]


def _split_wait(started, kind, after, name):
    send_sems_, recv_sems_, x_thru, land_thru, _ = started
    n = len(x_thru)
    npeer = 7 if kind == "gather" else 3

    def body(*refs):
        x_refs, land_refs = refs[:n], refs[n:2 * n]
        send_sems, recv_sems = refs[2 * n], refs[2 * n + 1]
        for k in range(1, npeer + 1):
            for a in range(n):
                src, _, landed, peer = _split_views(kind, x_refs[a], land_refs[a], k)
                cp = pltpu.make_async_remote_copy(
                    src_ref=src, dst_ref=landed, send_sem=send_sems.at[npeer * a + k - 1],
                    recv_sem=recv_sems.at[npeer * a + k - 1],
                    device_id=peer, device_id_type=pl.DeviceIdType.MESH)
                cp.wait_send()
                cp.wait_recv()

    hbm = pl.BlockSpec(memory_space=pltpu.HBM)
    sem = pl.BlockSpec(memory_space=pltpu.SEMAPHORE)
    out = pl.pallas_call(
        body, name=name,
        out_shape=tuple(pltpu.HBM(a.shape, a.dtype) for a in x_thru + land_thru),
        in_specs=[hbm] * (2 * n) + [sem, sem, pl.BlockSpec(memory_space=pl.ANY)],
        out_specs=tuple([hbm] * (2 * n)),
        input_output_aliases={i: i for i in range(2 * n)},
        compiler_params=pltpu.CompilerParams(has_side_effects=pltpu.SideEffectType.DATAFLOW_SIDE_EFFECTING),
    )(*x_thru, *land_thru, send_sems_, recv_sems_, after)
    return list(out[:n]), list(out[n:])


def _pair_sum(x8, r4, name):
    _, r, c = x8.shape
    tr = _row_tile(r, c * 12)

    def body(core_ref, x_ref, r_ref, o_ref):
        o_ref[...] = (x_ref[...].astype(F32) + r_ref[...].astype(F32)).astype(o_ref.dtype)

    core = lax.axis_index("c").astype(jnp.int32).reshape(1)
    return pl.pallas_call(
        body, name=name,
        grid_spec=pltpu.PrefetchScalarGridSpec(
            num_scalar_prefetch=1, grid=(4, r // tr),
            in_specs=[pl.BlockSpec((None, tr, c), lambda j, i, core_ref: (2 * j + core_ref[0], i, 0)),
                      pl.BlockSpec((None, tr, c), lambda j, i, core_ref: (j, i, 0))],
            out_specs=pl.BlockSpec((None, tr, c), lambda j, i, core_ref: (j, i, 0))),
        out_shape=jax.ShapeDtypeStruct((4, r, c), x8.dtype),
        compiler_params=_cparams(("parallel", "parallel"), VMEM_LIMIT))(core, x8, r4)


def _row_tile(r, bytes_per_row, budget=4 * 1024 * 1024):
    best = None
    for t in range(16, r + 1, 16):
        if r % t == 0 and t * bytes_per_row <= budget:
            best = t
    return best if best is not None else r


def _mm(a, b, dims, name, out_dtype=F32, add=None, out_shards=1, tm=1024, tn=1024, tk=1024):
    S = b.shape[0] if b.ndim == 3 else 1
    bshape = (b.shape[1], S * b.shape[2]) if b.ndim == 3 else b.shape
    if dims == "nn":
        (M, K), (K2, N) = a.shape, bshape
    elif dims == "nt":
        (M, K), (N, K2) = a.shape, bshape
    else:
        (K, M), (K2, N) = a.shape, bshape
    assert K == K2, (a.shape, b.shape, dims)
    cs = bshape[1] // S
    tm, tk = _tile(M, tm), _tile(K, min(tk, cs) if (S > 1 and dims == "nt") else tk)
    tn = _tile(N, min(tn, cs) if (S > 1 and dims != "nt") else min(tn, N // out_shards))
    assert cs % (tk if dims == "nt" else tn) == 0 and (N // out_shards) % tn == 0
    nk = K // tk

    def body(*refs):
        if add is None:
            a_ref, b_ref, o_ref, acc = refs
        else:
            a_ref, b_ref, c_ref, o_ref, acc = refs
        k = pl.program_id(2)

        @pl.when(k == 0)
        def _():
            acc[...] = jnp.zeros_like(acc)

        acc[...] += _dot(a_ref[...], b_ref[...], dims)

        @pl.when(k == nk - 1)
        def _():
            r = acc[...]
            if add is not None:
                r = r + c_ref[...]
            o_ref[...] = r.astype(out_dtype)

    a_spec = (pl.BlockSpec((tk, tm), lambda i, j, k: (k, i)) if dims == "tn"
              else pl.BlockSpec((tm, tk), lambda i, j, k: (i, k)))
    if S == 1:
        b_spec = (pl.BlockSpec((tn, tk), lambda i, j, k: (j, k)) if dims == "nt"
                  else pl.BlockSpec((tk, tn), lambda i, j, k: (k, j)))
    elif dims == "nt":
        per = cs // tk
        b_spec = pl.BlockSpec((None, tn, tk), lambda i, j, k: (k // per, j, k % per))
    else:
        per = cs // tn
        b_spec = pl.BlockSpec((None, tk, tn), lambda i, j, k: (j // per, k, j % per))
    in_specs = [a_spec, b_spec]
    args = [a, b]
    if add is not None:
        in_specs.append(pl.BlockSpec((tm, tn), lambda i, j, k: (i, j)))
        args.append(add)
    if out_shards == 1:
        out_spec, out_shape = pl.BlockSpec((tm, tn), lambda i, j, k: (i, j)), (M, N)
    else:
        oper = N // out_shards // tn
        out_spec = pl.BlockSpec((None, tm, tn), lambda i, j, k: (j // oper, i, j % oper))
        out_shape = (out_shards, M, N // out_shards)
    return pl.pallas_call(
        body, name=name, grid=(M // tm, N // tn, nk),
        in_specs=in_specs, out_specs=out_spec,
        out_shape=jax.ShapeDtypeStruct(out_shape, out_dtype),
        scratch_shapes=[pltpu.VMEM((tm, tn), F32)],
        compiler_params=_cparams(("parallel", "parallel", "arbitrary"), VMEM_LIMIT),
    )(*args)


def _mm_sum(pairs, name, out_dtype=F32, tm=768, tn=512, tk=1024):
    M, N = pairs[0][0].shape[0], pairs[0][1].shape[1]
    tm, tn = _tile(M, tm), _tile(N, tn)
    tks = [_tile(a.shape[1], tk) for a, _ in pairs]
    nks = [a.shape[1] // t for (a, _), t in zip(pairs, tks)]
    starts = [sum(nks[:g]) for g in range(len(pairs))]
    nk = sum(nks)
    G = len(pairs)

    def body(*refs):
        o_ref, acc = refs[2 * G], refs[2 * G + 1]
        k = pl.program_id(2)

        @pl.when(k == 0)
        def _():
            acc[...] = jnp.zeros_like(acc)

        for g in range(G):
            @pl.when(jnp.logical_and(k >= starts[g], k < starts[g] + nks[g]))
            def _(g=g):
                acc[...] += _dot(refs[2 * g][...], refs[2 * g + 1][...])

        @pl.when(k == nk - 1)
        def _():
            o_ref[...] = acc[...].astype(out_dtype)

    in_specs, args = [], []
    for g, (a, b) in enumerate(pairs):
        kk = lambda k, g=g: jnp.clip(k - starts[g], 0, nks[g] - 1)
        in_specs += [pl.BlockSpec((tm, tks[g]), lambda i, j, k, kk=kk: (i, kk(k))),
                     pl.BlockSpec((tks[g], tn), lambda i, j, k, kk=kk: (kk(k), j))]
        args += [a, b]
    return pl.pallas_call(
        body, name=name, grid=(M // tm, N // tn, nk),
        in_specs=in_specs, out_specs=pl.BlockSpec((tm, tn), lambda i, j, k: (i, j)),
        out_shape=jax.ShapeDtypeStruct((M, N), out_dtype),
        scratch_shapes=[pltpu.VMEM((tm, tn), F32)],
        compiler_params=_cparams(("parallel", "parallel", "arbitrary"), VMEM_LIMIT),
    )(*args)


def _ada_fwd(ccp, w, b):
    def body(c_ref, w_ref, b_ref, o_ref):
        o_ref[...] = _dot(_silu(c_ref[...]), w_ref[...]) + b_ref[...]

    return pl.pallas_call(
        body, name="ada_fwd", out_shape=jax.ShapeDtypeStruct((ccp.shape[0], w.shape[1]), F32),
        compiler_params=_cparams(None, VMEM_LIMIT))(ccp, w, b)


def _ada_bwd(ccp, w, g, cctx, n_loc):
    def body(c_ref, w_ref, g_ref, cc_ref, gw_ref, dc_ref):
        gw_ref[...] = _dot(_silu(c_ref[...]), g_ref[...], "tn")
        dcc = _dot(g_ref[...], w_ref[...], "nt")
        row = lax.broadcasted_iota(jnp.int32, dcc.shape, 0)
        part = jnp.sum(jnp.where(row % SUBLANES == n_loc, dcc, 0.0), axis=0, keepdims=True)
        dc_ref[...] = jnp.broadcast_to(part * _dsilu(cc_ref[...]), dc_ref.shape)

    D = ccp.shape[1]
    return pl.pallas_call(
        body, name="ada_bwd",
        out_shape=(jax.ShapeDtypeStruct(w.shape, F32), jax.ShapeDtypeStruct((SUBLANES, D), F32)),
        compiler_params=_cparams(None, VMEM_LIMIT))(ccp, w, g, cctx)


def _norm_mod_fwd(h, gain, mods, i_shift, i_scale, nct, tm, name):
    B, T, D = h.shape

    def body(h_ref, g_ref, m_ref, u_ref):
        x = h_ref[0]
        r = lax.rsqrt(jnp.mean(x * x, axis=-1, keepdims=True) + EPS)
        n = x * r * g_ref[...]
        u_ref[0] = (n * (1.0 + m_ref[0, i_scale:i_scale + 1, :]) + m_ref[0, i_shift:i_shift + 1, :]).astype(BF16)

    return pl.pallas_call(
        body, name=name, grid=(B, T // tm),
        in_specs=[pl.BlockSpec((1, tm, D), lambda b, t: (b, t, 0)),
                  pl.BlockSpec((1, D), lambda b, t: (0, 0)),
                  pl.BlockSpec((1, SUBLANES, D), lambda b, t: (jnp.where(t < nct, B, b), 0, 0))],
        out_specs=pl.BlockSpec((1, tm, D), lambda b, t: (b, t, 0)),
        out_shape=jax.ShapeDtypeStruct((B, T, D), BF16),
        compiler_params=_cparams(("parallel", "parallel"), VMEM_LIMIT),
    )(h, gain, mods)


def _norm_mod_bwd(h, du, gain, mods, i_scale, nct, tm, res, name):
    B, T, D = h.shape
    nt = T // tm

    def body(h_ref, du_ref, g_ref, m_ref, res_ref, dx_ref, ax_ref, ac_ref):
        t = pl.program_id(1)
        x = h_ref[0]
        r = lax.rsqrt(jnp.mean(x * x, axis=-1, keepdims=True) + EPS)
        nh = x * r
        g = g_ref[...]
        du_ = du_ref[0]
        dn = du_ * (1.0 + m_ref[0, i_scale:i_scale + 1, :])
        dnh = dn * g
        dx = r * (dnh - nh * jnp.mean(dnh * nh, axis=-1, keepdims=True))
        sums = (jnp.sum(du_, axis=0, keepdims=True), jnp.sum(du_ * nh * g, axis=0, keepdims=True),
                jnp.sum(dn * nh, axis=0, keepdims=True))

        @pl.when(t == 0)
        def _():
            ax_ref[...] = jnp.zeros_like(ax_ref)
            ac_ref[...] = jnp.zeros_like(ac_ref)

        def accumulate(ref):
            for i, s in enumerate(sums):
                ref[0, i:i + 1, :] += s

        @pl.when(t < nct)
        def _():
            accumulate(ac_ref)

        @pl.when(t >= nct)
        def _():
            accumulate(ax_ref)
            dx_ref[0] = dx + res_ref[0]

    lat = lambda b, t: (b, jnp.maximum(t - nct, 0), 0)
    acc = pl.BlockSpec((1, SUBLANES, D), lambda b, t: (b, 0, 0))
    return pl.pallas_call(
        body, name=name, grid=(B, nt),
        in_specs=[pl.BlockSpec((1, tm, D), lambda b, t: (b, t, 0)),
                  pl.BlockSpec((1, tm, D), lambda b, t: (b, t, 0)),
                  pl.BlockSpec((1, D), lambda b, t: (0, 0)),
                  pl.BlockSpec((1, SUBLANES, D), lambda b, t: (jnp.where(t < nct, B, b), 0, 0)),
                  pl.BlockSpec((1, tm, D), lat)],
        out_specs=(pl.BlockSpec((1, tm, D), lat), acc, acc),
        out_shape=(jax.ShapeDtypeStruct(res.shape, F32),
                   jax.ShapeDtypeStruct((B, SUBLANES, D), F32), jax.ShapeDtypeStruct((B, SUBLANES, D), F32)),
        compiler_params=_cparams(("parallel", "arbitrary"), VMEM_LIMIT),
    )(h, du, gain, mods, res)


def _resid_norm_fwd(x, y, gain, mods, i_gate, tm, name):
    B, N, D = x.shape

    def body(x_ref, y_ref, g_ref, m_ref, o_ref):
        y_ = y_ref[0]
        r = lax.rsqrt(jnp.mean(y_ * y_, axis=-1, keepdims=True) + EPS)
        o_ref[0] = x_ref[0] + y_ * r * g_ref[...] * m_ref[0, i_gate:i_gate + 1, :]

    row = pl.BlockSpec((1, tm, D), lambda b, t: (b, t, 0))
    return pl.pallas_call(
        body, name=name, grid=(B, N // tm),
        in_specs=[row, row, pl.BlockSpec((1, D), lambda b, t: (0, 0)),
                  pl.BlockSpec((1, SUBLANES, D), lambda b, t: (b, 0, 0))],
        out_specs=row, out_shape=jax.ShapeDtypeStruct((B, N, D), F32),
        compiler_params=_cparams(("parallel", "parallel"), VMEM_LIMIT),
    )(x, y, gain, mods)


def _resid_norm_bwd_math(y_, dh, g, gate):
    r = lax.rsqrt(jnp.mean(y_ * y_, axis=-1, keepdims=True) + EPS)
    nh = y_ * r
    dgate = jnp.sum(dh * nh * g, axis=0, keepdims=True)
    dn = dh * gate
    dgain = jnp.sum(dn * nh, axis=0, keepdims=True)
    dnh = dn * g
    dy = r * (dnh - nh * jnp.mean(dnh * nh, axis=-1, keepdims=True))
    return dy, dgate, dgain


def _resid_norm_bwd(y, dh, gain, mods, i_gate, tm, name):
    B, N, D = y.shape

    def body(y_ref, dh_ref, g_ref, m_ref, dy_ref, acc_ref):
        t = pl.program_id(1)
        dy, dgate, dgain = _resid_norm_bwd_math(y_ref[0], dh_ref[0], g_ref[...], m_ref[0, i_gate:i_gate + 1, :])
        dy_ref[0] = dy.astype(BF16)

        @pl.when(t == 0)
        def _():
            acc_ref[...] = jnp.zeros_like(acc_ref)

        acc_ref[0, 0:1, :] += dgate
        acc_ref[0, 1:2, :] += dgain

    row = pl.BlockSpec((1, tm, D), lambda b, t: (b, t, 0))
    return pl.pallas_call(
        body, name=name, grid=(B, N // tm),
        in_specs=[row, row, pl.BlockSpec((1, D), lambda b, t: (0, 0)),
                  pl.BlockSpec((1, SUBLANES, D), lambda b, t: (b, 0, 0))],
        out_specs=(row, pl.BlockSpec((1, SUBLANES, D), lambda b, t: (b, 0, 0))),
        out_shape=(jax.ShapeDtypeStruct((B, N, D), BF16), jax.ShapeDtypeStruct((B, SUBLANES, D), F32)),
        compiler_params=_cparams(("parallel", "arbitrary"), VMEM_LIMIT),
    )(y, dh, gain, mods)


def _final_fwd_bwd(h1, d, gain, mods, i_gate, tgt, tm):
    B, N, D = h1.shape

    def body(h_ref, d_ref, g_ref, m_ref, t_ref, dd_ref, dh_ref, acc_ref):
        t = pl.program_id(1)
        d_ = d_ref[0]
        g = g_ref[...]
        gate = m_ref[0, i_gate:i_gate + 1, :]
        r = lax.rsqrt(jnp.mean(d_ * d_, axis=-1, keepdims=True) + EPS)
        e = h_ref[0] + d_ * r * g * gate - t_ref[0]
        dh = e * (1.0 / D)
        dh_ref[0] = dh
        dy, dgate, dgain = _resid_norm_bwd_math(d_, dh, g, gate)
        dd_ref[0] = dy.astype(BF16)

        @pl.when(t == 0)
        def _():
            acc_ref[...] = jnp.zeros_like(acc_ref)

        acc_ref[0, 0:1, :] += dgate
        acc_ref[0, 1:2, :] += dgain
        acc_ref[0, 2:3, :] += jnp.sum(e * e, axis=0, keepdims=True)

    row = pl.BlockSpec((1, tm, D), lambda b, t: (b, t, 0))
    return pl.pallas_call(
        body, name="final_fwd_bwd", grid=(B, N // tm),
        in_specs=[row, row, pl.BlockSpec((1, D), lambda b, t: (0, 0)),
                  pl.BlockSpec((1, SUBLANES, D), lambda b, t: (b, 0, 0)), row],
        out_specs=(row, row, pl.BlockSpec((1, SUBLANES, D), lambda b, t: (b, 0, 0))),
        out_shape=(jax.ShapeDtypeStruct((B, N, D), BF16), jax.ShapeDtypeStruct((B, N, D), F32),
                   jax.ShapeDtypeStruct((B, SUBLANES, D), F32)),
        compiler_params=_cparams(("parallel", "arbitrary"), VMEM_LIMIT),
    )(h1, d, gain, mods, tgt)


def _shifted(x, prev, nxt, off, row, tm):
    if off == 0:
        return x
    if off < 0:
        y = pltpu.roll(x, -off, 0)
        for r in range(-off):
            y = jnp.where(row == r, prev[SUBLANES + r + off:SUBLANES + r + off + 1, :], y)
        return y
    y = pltpu.roll(x, tm - off, 0)
    for r in range(off):
        y = jnp.where(row == tm - off + r, nxt[r:r + 1, :], y)
    return y


def _halo_specs(T, tm, tc, cb0, order):
    r8, n8 = tm // SUBLANES, T // SUBLANES
    if order == "btj":
        prev = lambda b, t, j: (b, jnp.maximum(t * r8 - 1, 0), cb0 + j)
        nxt = lambda b, t, j: (b, jnp.minimum((t + 1) * r8, n8 - 1), cb0 + j)
    else:
        prev = lambda b, j, t: (b, jnp.maximum(t * r8 - 1, 0), cb0 + j)
        nxt = lambda b, j, t: (b, jnp.minimum((t + 1) * r8, n8 - 1), cb0 + j)
    return pl.BlockSpec((1, SUBLANES, tc), prev), pl.BlockSpec((1, SUBLANES, tc), nxt)


def _seg_halos(p_ref, n_ref, t, nct, nt):
    seg_start = jnp.logical_or(t == 0, t == nct)
    seg_end = jnp.logical_or(t == nct - 1, t == nt - 1)
    prev = jnp.where(seg_start, 0.0, p_ref[0])
    nxt = jnp.where(seg_end, 0.0, n_ref[0])
    return prev, nxt


def _dwconv(x, col0, C, w, bias, offs, nct, tm, tc, name, out_dtype=F32):
    B, T, _ = x.shape
    nt = T // tm
    cb0 = col0 // tc

    def body(*refs):
        if bias is None:
            x_ref, p_ref, n_ref, w_ref, o_ref = refs
        else:
            x_ref, p_ref, n_ref, w_ref, b_ref, o_ref = refs
        t = pl.program_id(1)
        prev, nxt = _seg_halos(p_ref, n_ref, t, nct, nt)
        x_ = x_ref[0]
        row = lax.broadcasted_iota(jnp.int32, x_.shape, 0)
        acc = None
        for j, off in enumerate(offs):
            term = _shifted(x_, prev, nxt, off, row, tm) * w_ref[j:j + 1, :]
            acc = term if acc is None else acc + term
        if bias is not None:
            acc = acc + b_ref[...]
        o_ref[0] = acc.astype(out_dtype)

    pspec, nspec = _halo_specs(T, tm, tc, cb0, "btj")
    in_specs = [pl.BlockSpec((1, tm, tc), lambda b, t, j: (b, t, cb0 + j)), pspec, nspec,
                pl.BlockSpec((w.shape[0], tc), lambda b, t, j: (0, j))]
    args = [x, x, x, w]
    if bias is not None:
        in_specs.append(pl.BlockSpec((1, tc), lambda b, t, j: (0, j)))
        args.append(bias)
    return pl.pallas_call(
        body, name=name, grid=(B, nt, C // tc),
        in_specs=in_specs, out_specs=pl.BlockSpec((1, tm, tc), lambda b, t, j: (b, t, j)),
        out_shape=jax.ShapeDtypeStruct((B, T, C), out_dtype),
        compiler_params=_cparams(("parallel", "parallel", "parallel"), VMEM_LIMIT),
    )(*args)


def _dwconv_wgrad(dy, x, col0, C, offs, nct, tm, tc, name):
    B, T, _ = x.shape
    nt = T // tm
    cb0 = col0 // tc
    K = len(offs)

    def body(dy_ref, x_ref, p_ref, n_ref, acc_ref):
        t = pl.program_id(2)
        prev, nxt = _seg_halos(p_ref, n_ref, t, nct, nt)
        x_ = x_ref[0]
        dy_ = dy_ref[0]
        row = lax.broadcasted_iota(jnp.int32, x_.shape, 0)

        @pl.when(t == 0)
        def _():
            acc_ref[...] = jnp.zeros_like(acc_ref)

        for j, off in enumerate(offs):
            acc_ref[0, j:j + 1, :] += jnp.sum(dy_ * _shifted(x_, prev, nxt, off, row, tm), axis=0, keepdims=True)
        acc_ref[0, K:K + 1, :] += jnp.sum(dy_, axis=0, keepdims=True)

    pspec, nspec = _halo_specs(T, tm, tc, cb0, "bjt")
    return pl.pallas_call(
        body, name=name, grid=(B, C // tc, nt),
        in_specs=[pl.BlockSpec((1, tm, tc), lambda b, j, t: (b, t, j)),
                  pl.BlockSpec((1, tm, tc), lambda b, j, t: (b, t, cb0 + j)), pspec, nspec],
        out_specs=pl.BlockSpec((1, SUBLANES, tc), lambda b, j, t: (b, 0, j)),
        out_shape=jax.ShapeDtypeStruct((B, SUBLANES, C), F32),
        compiler_params=_cparams(("parallel", "parallel", "arbitrary"), VMEM_LIMIT),
    )(dy, x, x, x)


def _ab_act(pab, alog, dtb, nd, tm):
    R = pab.shape[0]

    def body(p_ref, al_ref, dt_ref, o_ref):
        p = p_ref[...]
        lane = lax.broadcasted_iota(jnp.int32, p.shape, 1)
        g = -jnp.exp(al_ref[...]) * _softplus(p + dt_ref[...])
        o_ref[...] = jnp.where(lane < nd, g, jnp.where(lane < 2 * nd, _sigmoid(p), 0.0))

    row = pl.BlockSpec((tm, LANES), lambda i: (i, 0))
    vec = pl.BlockSpec((1, LANES), lambda i: (0, 0))
    return pl.pallas_call(
        body, name="ab_act", grid=(R // tm,), in_specs=[row, vec, vec], out_specs=row,
        out_shape=jax.ShapeDtypeStruct((R, LANES), F32),
        compiler_params=_cparams(("parallel",), VMEM_LIMIT))(pab, alog, dtb)


def _ab_act_bwd(pab, gb, dgb, alog, dtb, nd, tm):
    R = pab.shape[0]

    def body(p_ref, gb_ref, d0_ref, d1_ref, al_ref, dt_ref, o_ref, acc_ref):
        i = pl.program_id(0)
        p = p_ref[...]
        gb_ = gb_ref[...]
        d_ = d0_ref[...] + d1_ref[...]
        lane = lax.broadcasted_iota(jnp.int32, p.shape, 1)
        is_g = lane < nd
        is_b = jnp.logical_and(lane >= nd, lane < 2 * nd)
        da = jnp.where(is_g, d_ * (-jnp.exp(al_ref[...])) * _sigmoid(p + dt_ref[...]), 0.0)
        db = jnp.where(is_b, d_ * gb_ * (1.0 - gb_), 0.0)
        o_ref[...] = (da + db).astype(BF16)

        @pl.when(i == 0)
        def _():
            acc_ref[...] = jnp.zeros_like(acc_ref)

        acc_ref[0:1, :] += jnp.sum(jnp.where(is_g, d_ * gb_, 0.0), axis=0, keepdims=True)
        acc_ref[1:2, :] += jnp.sum(da, axis=0, keepdims=True)

    row = pl.BlockSpec((tm, LANES), lambda i: (i, 0))
    vec = pl.BlockSpec((1, LANES), lambda i: (0, 0))
    return pl.pallas_call(
        body, name="ab_act_bwd", grid=(R // tm,),
        in_specs=[row, row, row, row, vec, vec],
        out_specs=(row, pl.BlockSpec((SUBLANES, LANES), lambda i: (0, 0))),
        out_shape=(jax.ShapeDtypeStruct((R, LANES), BF16), jax.ShapeDtypeStruct((SUBLANES, LANES), F32)),
        compiler_params=_cparams(("arbitrary",), VMEM_LIMIT))(pab, gb, dgb[0], dgb[1], alog, dtb)


def _dn_act(cv, H, hd, tm):
    B, T, C3 = cv.shape
    qscale = float(hd) ** -0.5

    def body(c_ref, o_ref):
        part = pl.program_id(0)
        for h in range(H):
            sl = slice(h * hd, (h + 1) * hd)
            s = _silu(c_ref[0, :, sl])
            rs = lax.rsqrt(jnp.sum(s * s, axis=-1, keepdims=True) + EPS)
            scale = jnp.where(part == 0, rs * qscale, jnp.where(part == 1, rs, 1.0))
            o_ref[0, :, sl] = s * scale

    spec = pl.BlockSpec((1, tm, H * hd), lambda p, b, t: (b, t, p))
    return pl.pallas_call(
        body, name="dn_act", grid=(3, B, T // tm), in_specs=[spec], out_specs=spec,
        out_shape=jax.ShapeDtypeStruct((B, T, C3), F32),
        compiler_params=_cparams(("parallel",) * 3, VMEM_LIMIT))(cv)


def _dn_act_bwd(cv, dqkv, H, hd, tm):
    B, T, C3 = cv.shape
    qscale = float(hd) ** -0.5

    def body(c_ref, d0_ref, d1_ref, o_ref):
        part = pl.program_id(0)
        for h in range(H):
            sl = slice(h * hd, (h + 1) * hd)
            c = c_ref[0, :, sl]
            s = _silu(c)
            dout = d0_ref[0, :, sl] + d1_ref[0, :, sl]
            rs = lax.rsqrt(jnp.sum(s * s, axis=-1, keepdims=True) + EPS)
            sh = s * rs
            dnorm = rs * (dout - sh * jnp.sum(dout * sh, axis=-1, keepdims=True))
            ds = jnp.where(part == 0, dnorm * qscale, jnp.where(part == 1, dnorm, dout))
            o_ref[0, :, sl] = ds * _dsilu(c)

    spec = pl.BlockSpec((1, tm, H * hd), lambda p, b, t: (b, t, p))
    return pl.pallas_call(
        body, name="dn_act_bwd", grid=(3, B, T // tm), in_specs=[spec, spec, spec], out_specs=spec,
        out_shape=jax.ShapeDtypeStruct((B, T, C3), F32),
        compiler_params=_cparams(("parallel",) * 3, VMEM_LIMIT))(cv, dqkv[0], dqkv[1])


def _chunk_of(d, s, ncc, nch):
    if d == 0:
        return s
    return jnp.where(s < ncc, ncc - 1 - s, nch - 1 - (s - ncc))


def _delta_masks(d):
    row = lax.broadcasted_iota(jnp.int32, (CHUNK, CHUNK), 0)
    col = lax.broadcasted_iota(jnp.int32, (CHUNK, CHUNK), 1)
    sign = 1 - 2 * d
    diff = (row - col) * sign
    return diff >= 0, diff > 0, diff <= 0


def _each(f, *lists):
    return [f(*a) for a in zip(*lists)]


def _unit_tri_inverse(As):
    C = As[0].shape[0]
    row = lax.broadcasted_iota(jnp.int32, (C, C), 0)
    col = lax.broadcasted_iota(jnp.int32, (C, C), 1)
    eye = jnp.where(row == col, 1.0, 0.0).astype(F32)
    same = lambda shift: jnp.right_shift(row, shift) == jnp.right_shift(col, shift)
    in8 = same(3)
    xs = [-jnp.where(in8, a, 0.0) for a in As]
    ts = [eye + x for x in xs]
    ps = _each(lambda x: _dot(x, x), xs)
    tp = _each(lambda t, p: _dot(_rows(t, p), p), ts, ps)
    ts = _each(lambda t, m: t + m[:C], ts, tp)
    ts = _each(lambda t, m: t + _dot(t, m[C:]), ts, tp)
    shift = 3
    while (1 << shift) < C:
        off = jnp.logical_and(same(shift + 1), jnp.right_shift(row, shift) != jnp.right_shift(col, shift))
        los = [jnp.where(off, a, 0.0) for a in As]
        ts = _each(lambda t, lo: t - _dot(t, _dot(lo, t)), ts, los)
        shift += 1
    def residual(a, t):
        (ah, al), (th, tl) = _split2(eye + a), _split2(t)
        m = _dot(_rows(ah, al), th)
        return eye - (m[:C] + (m[C:] + _dot(ah, tl)))

    rs = _each(residual, As, ts)
    return _each(lambda t, r: t + _dot(t, r), ts, rs)


def _rows(*xs):
    return jnp.concatenate(xs, axis=0)


def _cols(*xs):
    return jnp.concatenate(xs, axis=1)


def _delta_chunk_fwd(q, k, v, g_i, g_j, beta_i, gl, S, incl, strict):
    D_ = _each(lambda gi, gj, m: jnp.where(m, jnp.exp(jnp.where(m, gi - gj, 0.0)), 0.0), g_i, g_j, incl)
    C, dk = k[0].shape
    kb = _each(lambda k_, b: k_ * b, k, beta_i)
    kq = _each(lambda kb_, q_, k_: _dot(_rows(kb_, q_), k_, "nt"), kb, q, k)
    A = _each(lambda m_, d, m: jnp.where(m, m_[:C] * d, 0.0), kq, D_, strict)
    P = _each(lambda m_, d, m: jnp.where(m, m_[C:] * d, 0.0), kq, D_, incl)
    Tm = _unit_tri_inverse(A)
    gam = _each(jnp.exp, g_i)
    wu = _each(lambda t, kb_, g, v_, b: _dot(t, _cols(kb_ * g, v_ * b)), Tm, kb, gam, v, beta_i)
    w = [m_[:, :dk] for m_ in wu]
    u = [m_[:, dk:] for m_ in wu]
    qg = _each(lambda q_, g: q_ * g, q, gam)
    ws = _each(lambda w_, qg_, s: _dot(_rows(w_, qg_), s), w, qg, S)
    vn = _each(lambda u_, m_: u_ - m_[:C], u, ws)
    o = _each(lambda m_, p, vn_: m_[C:] + _dot(p, vn_), ws, P, vn)
    e_i = _each(lambda gl_, gi: jnp.exp(gl_ - gi), gl, g_i)
    kd = _each(lambda k_, e: k_ * e, k, e_i)
    Gam = _each(jnp.exp, gl)
    S_new = _each(lambda s, g, kd_, vn_: s * g + _dot(kd_, vn_, "tn"), S, Gam, kd, vn)
    return dict(D=D_, kb=kb, A=A, Tm=Tm, gam=gam, w=w, u=u, vn=vn, P=P, qg=qg, o=o, e=e_i, kd=kd, Gam=Gam, S_new=S_new)


def _delta_gb(gb_ref, d, incl, incl_t, H):
    gb = gb_ref[0]
    g = gb[:, d * H:(d + 1) * H]
    beta = gb[:, (2 + d) * H:(3 + d) * H]
    gc_col = _dot_mask(incl.astype(BF16), g)
    gc_row = _dot_mask(incl_t.astype(BF16), g, "xtn")
    gl = jnp.sum(g, axis=0, keepdims=True)
    return beta, gc_col, gc_row, gl


def _delta_fwd(qkvn, gb, H, hd, ncc):
    B, T, _ = qkvn.shape
    nch = T // CHUNK
    HD = H * hd
    pairs = [(d, h) for d in range(2) for h in range(H)]

    def body(q0, k0, v0, gb0, q1, k1, v1, gb1, o0, o1, sin0, sin1, S_ref):
        s = pl.program_id(1)
        q_refs, k_refs, v_refs, gb_refs = (q0, q1), (k0, k1), (v0, v1), (gb0, gb1)
        o_refs, sin_refs = (o0, o1), (sin0, sin1)

        @pl.when(s == 0)
        def _():
            S_ref[...] = jnp.zeros_like(S_ref)

        masks = [_delta_masks(d) for d in range(2)]
        gbs = [_delta_gb(gb_refs[d], d, masks[d][0], masks[d][2], H) for d in range(2)]
        head = lambda ref, h: ref[0, :, h * hd:(h + 1) * hd]
        S = [S_ref[d, h] for d, h in pairs]
        r = _delta_chunk_fwd([head(q_refs[d], h) for d, h in pairs], [head(k_refs[d], h) for d, h in pairs],
                             [head(v_refs[d], h) for d, h in pairs],
                             [gbs[d][1][:, h:h + 1] for d, h in pairs], [gbs[d][2][h:h + 1, :] for d, h in pairs],
                             [gbs[d][0][:, h:h + 1] for d, h in pairs], [gbs[d][3][:, h:h + 1] for d, h in pairs],
                             S, [masks[d][0] for d, h in pairs], [masks[d][1] for d, h in pairs])
        for i, (d, h) in enumerate(pairs):
            sin_refs[d][0, 0, h] = S[i]
            S_ref[d, h] = r["S_new"][i]
            o_refs[d][0, :, h * hd:(h + 1) * hd] = r["o"][i]

    def dir_specs(d):
        cidx = lambda b, s: _chunk_of(d, s, ncc, nch)
        ins = [pl.BlockSpec((1, CHUNK, HD), lambda b, s, p=p: (b, cidx(b, s), p)) for p in range(3)]
        ins.append(pl.BlockSpec((1, CHUNK, LANES), lambda b, s: (b, cidx(b, s), 0)))
        return (ins, pl.BlockSpec((1, CHUNK, HD), lambda b, s: (b, cidx(b, s), 0)),
                pl.BlockSpec((1, 1, H, hd, hd), lambda b, s: (b, cidx(b, s), 0, 0, 0)))

    (in0, o_s0, sin_s0), (in1, o_s1, sin_s1) = dir_specs(0), dir_specs(1)
    o_shape = jax.ShapeDtypeStruct((B, T, HD), F32)
    sin_shape = jax.ShapeDtypeStruct((B, nch, H, hd, hd), F32)
    return pl.pallas_call(
        body, name="delta_fwd", grid=(B, nch),
        in_specs=in0 + in1, out_specs=(o_s0, o_s1, sin_s0, sin_s1),
        out_shape=(o_shape, o_shape, sin_shape, sin_shape),
        scratch_shapes=[pltpu.VMEM((2, H, hd, hd), F32)],
        compiler_params=_cparams(("parallel", "arbitrary"), VMEM_LIMIT),
    )(qkvn, qkvn, qkvn, gb, qkvn, qkvn, qkvn, gb)


def _delta_bwd(qkvn, gb, sin, do, H, hd, ncc):
    B, T, _ = qkvn.shape
    nch = T // CHUNK
    HD = H * hd
    pairs = [(d, h) for d in range(2) for h in range(H)]

    def body(q0, k0, v0, gb0, sin0, do0, q1, k1, v1, gb1, sin1, do1, dqkv0, dqkv1, dgb0, dgb1, dS_ref):
        s = pl.program_id(1)
        q_refs, k_refs, v_refs, gb_refs = (q0, q1), (k0, k1), (v0, v1), (gb0, gb1)
        sin_refs, do_refs, dqkv_refs, dgb_refs = (sin0, sin1), (do0, do1), (dqkv0, dqkv1), (dgb0, dgb1)

        @pl.when(s == 0)
        def _():
            dS_ref[...] = jnp.zeros_like(dS_ref)

        masks = [_delta_masks(d) for d in range(2)]
        gbs = [_delta_gb(gb_refs[d], d, masks[d][0], masks[d][2], H) for d in range(2)]
        incl = [masks[d][0] for d, h in pairs]
        strict = [masks[d][1] for d, h in pairs]
        lane = lax.broadcasted_iota(jnp.int32, (1, LANES), 1)
        ones = jnp.ones((3 * CHUNK, LANES), BF16)
        head = lambda ref, h: ref[0, :, h * hd:(h + 1) * hd]
        q = [head(q_refs[d], h) for d, h in pairs]
        k = [head(k_refs[d], h) for d, h in pairs]
        v = [head(v_refs[d], h) for d, h in pairs]
        do_ = [head(do_refs[d], h) for d, h in pairs]
        beta_i = [gbs[d][0][:, h:h + 1] for d, h in pairs]
        S = [sin_refs[d][0, 0, h] for d, h in pairs]
        dSn = [dS_ref[d, h] for d, h in pairs]
        f = _delta_chunk_fwd(q, k, v, [gbs[d][1][:, h:h + 1] for d, h in pairs], [gbs[d][2][h:h + 1, :] for d, h in pairs],
                             beta_i, [gbs[d][3][:, h:h + 1] for d, h in pairs], S, incl, strict)
        rsum = lambda x: jnp.sum(x, axis=1, keepdims=True)
        dvn = _each(lambda p, d_, kd, ds: _dot(p, d_, "tn") + _dot(kd, ds), f["P"], do_, f["kd"], dSn)
        dP = _each(lambda d_, vn, m: jnp.where(m, _dot(d_, vn, "nt"), 0.0), do_, f["vn"], incl)
        dqg = _each(lambda d_, s: _dot(d_, s, "nt"), do_, S)
        dS_in = _each(lambda qg, d_, g, ds, w, dv_: _dot(qg, d_, "tn") + g * ds - _dot(w, dv_, "tn"),
                      f["qg"], do_, f["Gam"], dSn, f["w"], dvn)
        dGam = _each(lambda s, ds: jnp.sum(rsum(s * ds), axis=0, keepdims=True), S, dSn)
        dkd = _each(lambda vn, ds: _dot(vn, ds, "nt"), f["vn"], dSn)
        de = _each(lambda dkd_, k_, e: rsum(dkd_ * k_) * e, dkd, k, f["e"])
        dw = _each(lambda dv_, s: -_dot(dv_, s, "nt"), dvn, S)
        dXw = _each(lambda t, dw_: _dot(t, dw_, "tn"), f["Tm"], dw)
        dXu = _each(lambda t, dv_: _dot(t, dv_, "tn"), f["Tm"], dvn)
        dA = _each(lambda xw, w, xu, u, m: -jnp.where(m, _dot(xw, w, "nt") + _dot(xu, u, "nt"), 0.0),
                   dXw, f["w"], dXu, f["u"], strict)
        dM = _each(lambda a, d_: a * d_, dA, f["D"])
        dN = _each(lambda p, d_: p * d_, dP, f["D"])
        dkb = _each(lambda m, k_, xw, g: _dot(m, k_) + xw * g, dM, k, dXw, f["gam"])
        dq = _each(lambda dqg_, g, n, k_: dqg_ * g + _dot(n, k_), dqg, f["gam"], dN, k)
        dk = _each(lambda dkd_, e, m, kb, n, q_, dkb_, b: dkd_ * e + _dot(m, kb, "tn") + _dot(n, q_, "tn") + dkb_ * b,
                   dkd, f["e"], dM, f["kb"], dN, q, dkb, beta_i)
        dv = _each(lambda xu, b: xu * b, dXu, beta_i)
        dbeta = _each(lambda dkb_, k_, xu, v_: rsum(dkb_ * k_) + rsum(xu * v_), dkb, k, dXu, v)
        E = _each(lambda a, A_, p, P_: a * A_ + p * P_, dA, f["A"], dP, f["P"])

        def colsum(e):
            return _dot(_rows(*_split3(e)), ones, "tn")[:, 0:1]

        dg = _each(lambda e, dqg_, qg, xw, kb, g, de_: rsum(e) - colsum(e) + rsum(dqg_ * qg) + rsum(xw * kb) * g - de_,
                   E, dqg, f["qg"], dXw, f["kb"], f["gam"], de)
        dgl_h = _each(lambda de_, dg_, g: jnp.sum(de_, axis=0, keepdims=True) + dg_ * g, de, dGam, f["Gam"])
        for d in range(2):
            dgc = jnp.zeros((CHUNK, LANES), F32)
            dgl = jnp.zeros((1, LANES), F32)
            for h in range(H):
                i = d * H + h
                g_lane, b_lane = d * H + h, (2 + d) * H + h
                dgc = dgc + dg[i] * (lane == g_lane).astype(F32) + dbeta[i] * (lane == b_lane).astype(F32)
                dgl = dgl + dgl_h[i] * (lane == g_lane).astype(F32)
                dS_ref[d, h] = dS_in[i]
                dqkv_refs[d][0, :, h * hd:(h + 1) * hd] = dq[i]
                dqkv_refs[d][0, :, HD + h * hd:HD + (h + 1) * hd] = dk[i]
                dqkv_refs[d][0, :, 2 * HD + h * hd:2 * HD + (h + 1) * hd] = dv[i]
            draw = _dot_mask(masks[d][0].astype(BF16), dgc, "tn") + dgl
            dgb_refs[d][0] = jnp.where(lane < 2 * H, draw, dgc)

    def dir_specs(d):
        cidx = lambda b, s: _chunk_of(d, nch - 1 - s, ncc, nch)
        ins = [pl.BlockSpec((1, CHUNK, HD), lambda b, s, p=p: (b, cidx(b, s), p)) for p in range(3)]
        ins += [pl.BlockSpec((1, CHUNK, LANES), lambda b, s: (b, cidx(b, s), 0)),
                pl.BlockSpec((1, 1, H, hd, hd), lambda b, s: (b, cidx(b, s), 0, 0, 0)),
                pl.BlockSpec((1, CHUNK, HD), lambda b, s: (b, cidx(b, s), 0))]
        return (ins, pl.BlockSpec((1, CHUNK, 3 * HD), lambda b, s: (b, cidx(b, s), 0)),
                pl.BlockSpec((1, CHUNK, LANES), lambda b, s: (b, cidx(b, s), 0)))

    (in0, dq_s0, dg_s0), (in1, dq_s1, dg_s1) = dir_specs(0), dir_specs(1)
    dq_shape = jax.ShapeDtypeStruct((B, T, 3 * HD), F32)
    dg_shape = jax.ShapeDtypeStruct((B, T, LANES), F32)
    return pl.pallas_call(
        body, name="delta_bwd", grid=(B, nch),
        in_specs=in0 + in1, out_specs=(dq_s0, dq_s1, dg_s0, dg_s1),
        out_shape=(dq_shape, dq_shape, dg_shape, dg_shape),
        scratch_shapes=[pltpu.VMEM((2, H, hd, hd), F32)],
        compiler_params=_cparams(("parallel", "arbitrary"), VMEM_LIMIT),
    )(qkvn, qkvn, qkvn, gb, sin[0], do, qkvn, qkvn, qkvn, gb, sin[1], do)


def _dn_out(o2, pz, zcb0, onorm, H, hd, nct, tm):
    B, T, HD = o2[0].shape
    N = T - nct * tm

    def body(o0_ref, o1_ref, z_ref, g_ref, y_ref):
        for h in range(H):
            sl = slice(h * hd, (h + 1) * hd)
            o = o0_ref[0, :, sl] + o1_ref[0, :, sl]
            r = lax.rsqrt(jnp.mean(o * o, axis=-1, keepdims=True) + EPS)
            y_ref[0, :, sl] = (o * r * g_ref[...] * _silu(z_ref[0, :, sl])).astype(BF16)

    return pl.pallas_call(
        body, name="dn_out", grid=(B, N // tm),
        in_specs=[pl.BlockSpec((1, tm, HD), lambda b, t: (b, t + nct, 0)),
                  pl.BlockSpec((1, tm, HD), lambda b, t: (b, t + nct, 0)),
                  pl.BlockSpec((1, tm, HD), lambda b, t: (b, t + nct, zcb0)),
                  pl.BlockSpec((1, hd), lambda b, t: (0, 0))],
        out_specs=pl.BlockSpec((1, tm, HD), lambda b, t: (b, t, 0)),
        out_shape=jax.ShapeDtypeStruct((B, N, HD), BF16),
        compiler_params=_cparams(("parallel",) * 2, VMEM_LIMIT))(o2[0], o2[1], pz, onorm)


def _dn_out_bwd(o2, pz, zcb0, onorm, dy, H, hd, nct, tm):
    B, T, HD = o2[0].shape
    nt = T // tm

    def body(o0_ref, o1_ref, z_ref, g_ref, dy_ref, do_ref, dz_ref, acc_ref):
        t = pl.program_id(1)

        @pl.when(t == 0)
        def _():
            acc_ref[...] = jnp.zeros_like(acc_ref)

        @pl.when(t < nct)
        def _():
            do_ref[0] = jnp.zeros_like(do_ref[0])
            dz_ref[0] = jnp.zeros_like(dz_ref[0])

        @pl.when(t >= nct)
        def _():
            g = g_ref[...]
            for h in range(H):
                sl = slice(h * hd, (h + 1) * hd)
                o = o0_ref[0, :, sl] + o1_ref[0, :, sl]
                z = z_ref[0, :, sl]
                dy_ = dy_ref[0, :, sl]
                r = lax.rsqrt(jnp.mean(o * o, axis=-1, keepdims=True) + EPS)
                oh = o * r
                dz_ref[0, :, sl] = (dy_ * oh * g * _dsilu(z)).astype(BF16)
                dn = dy_ * _silu(z)
                acc_ref[0, 0:1, :] += jnp.sum(dn * oh, axis=0, keepdims=True)
                doh = dn * g
                do_ref[0, :, sl] = r * (doh - oh * jnp.mean(doh * oh, axis=-1, keepdims=True))

    lat = lambda b, t: (b, jnp.maximum(t - nct, 0), 0)
    row = pl.BlockSpec((1, tm, HD), lambda b, t: (b, t, 0))
    return pl.pallas_call(
        body, name="dn_out_bwd", grid=(B, nt),
        in_specs=[row, row,
                  pl.BlockSpec((1, tm, HD), lambda b, t: (b, t, zcb0)),
                  pl.BlockSpec((1, hd), lambda b, t: (0, 0)),
                  pl.BlockSpec((1, tm, HD), lat)],
        out_specs=(row, row, pl.BlockSpec((1, SUBLANES, hd), lambda b, t: (b, 0, 0))),
        out_shape=(jax.ShapeDtypeStruct((B, T, HD), F32), jax.ShapeDtypeStruct((B, T, HD), BF16),
                   jax.ShapeDtypeStruct((B, SUBLANES, hd), F32)),
        compiler_params=_cparams(("parallel", "arbitrary"), VMEM_LIMIT),
    )(o2[0], o2[1], pz, onorm, dy)


def _lru_gate_math(x, wr, wi, br, bi, lam):
    r = _sigmoid(_dot(x, wr) + br)
    i = _sigmoid(_dot(x, wi) + bi)
    sp = _softplus(-lam)
    la = -LRU_C * r * sp
    a = jnp.exp(la)
    s = jnp.sqrt(-jnp.tanh(la) * (a * a + 1.0))
    return r, i, sp, a, s


def _lru_gates(xc, wbd, bias4, lam, tm, bw):
    B, T, W = xc.shape

    def body(x_ref, w_ref, b_ref, l_ref, a_ref, i_ref):
        x = x_ref[0]
        for d in range(2):
            _, i, _, a, s = _lru_gate_math(x, w_ref[2 * d, 0], w_ref[2 * d + 1, 0],
                                           b_ref[2 * d:2 * d + 1, :], b_ref[2 * d + 1:2 * d + 2, :], l_ref[d:d + 1, :])
            a_ref[d, 0] = a
            i_ref[d, 0] = s * (i * x)

    out = pl.BlockSpec((2, 1, tm, bw), lambda b, t, j: (0, b, t, j))
    return pl.pallas_call(
        body, name="lru_gates", grid=(B, T // tm, W // bw),
        in_specs=[pl.BlockSpec((1, tm, bw), lambda b, t, j: (b, t, j)),
                  pl.BlockSpec((4, 1, bw, bw), lambda b, t, j: (0, j, 0, 0)),
                  pl.BlockSpec((4, bw), lambda b, t, j: (0, j)),
                  pl.BlockSpec((2, bw), lambda b, t, j: (0, j))],
        out_specs=(out, out),
        out_shape=(jax.ShapeDtypeStruct((2, B, T, W), F32), jax.ShapeDtypeStruct((2, B, T, W), F32)),
        compiler_params=_cparams(("parallel",) * 3, VMEM_LIMIT))(xc, wbd, bias4, lam)


def _lru_gates_bwd(xc, wbd, bias4, lam, dinp, da, tm, bw):
    B, T, W = xc.shape
    nt = T // tm

    def body(x_ref, w_ref, b_ref, l_ref, di0_ref, di1_ref, da0_ref, da1_ref, dx_ref, dw_ref, acc_ref):
        di_refs, da_refs = (di0_ref, di1_ref), (da0_ref, da1_ref)
        b_ = pl.program_id(1)
        t = pl.program_id(2)

        @pl.when(jnp.logical_and(b_ == 0, t == 0))
        def _():
            dw_ref[...] = jnp.zeros_like(dw_ref)
            acc_ref[...] = jnp.zeros_like(acc_ref)

        x = x_ref[0]
        dx = jnp.zeros_like(x)
        for d in range(2):
            wr, wi = w_ref[2 * d, 0], w_ref[2 * d + 1, 0]
            lam_ = l_ref[d:d + 1, :]
            r, i, sp, a, s = _lru_gate_math(x, wr, wi, b_ref[2 * d:2 * d + 1, :], b_ref[2 * d + 1:2 * d + 2, :], lam_)
            dinp_ = di_refs[d][0]
            dix = dinp_ * s
            ds = dinp_ * i * x
            dla = da_refs[d][0] * a - ds * (a * a) / s
            dpr = dla * (-LRU_C * sp) * r * (1.0 - r)
            dpi = dix * x * i * (1.0 - i)
            dx = dx + dix * i + _dot(dpr, wr, "nt") + _dot(dpi, wi, "nt")
            dw_ref[2 * d, 0] += _dot(x, dpr, "tn")
            dw_ref[2 * d + 1, 0] += _dot(x, dpi, "tn")
            acc_ref[2 * d:2 * d + 1, :] += jnp.sum(dpr, axis=0, keepdims=True)
            acc_ref[2 * d + 1:2 * d + 2, :] += jnp.sum(dpi, axis=0, keepdims=True)
            acc_ref[4 + d:5 + d, :] += jnp.sum(dla * (-LRU_C * r), axis=0, keepdims=True) * (-_sigmoid(-lam_))
        dx_ref[0] = dx

    tile = pl.BlockSpec((1, tm, bw), lambda j, b, t: (b, t, j))
    return pl.pallas_call(
        body, name="lru_gates_bwd", grid=(W // bw, B, nt),
        in_specs=[pl.BlockSpec((1, tm, bw), lambda j, b, t: (b, t, j)),
                  pl.BlockSpec((4, 1, bw, bw), lambda j, b, t: (0, j, 0, 0)),
                  pl.BlockSpec((4, bw), lambda j, b, t: (0, j)),
                  pl.BlockSpec((2, bw), lambda j, b, t: (0, j)), tile, tile, tile, tile],
        out_specs=(pl.BlockSpec((1, tm, bw), lambda j, b, t: (b, t, j)),
                   pl.BlockSpec((4, 1, bw, bw), lambda j, b, t: (0, j, 0, 0)),
                   pl.BlockSpec((SUBLANES, bw), lambda j, b, t: (0, j))),
        out_shape=(jax.ShapeDtypeStruct((B, T, W), F32), jax.ShapeDtypeStruct(wbd.shape, F32),
                   jax.ShapeDtypeStruct((SUBLANES, W), F32)),
        compiler_params=_cparams(("parallel", "arbitrary", "arbitrary"), VMEM_LIMIT),
    )(xc, wbd, bias4, lam, dinp[0], dinp[1], da[0], da[1])


def _tile_scan(a, x, rev, tm):
    row = lax.broadcasted_iota(jnp.int32, a.shape, 0)
    s = 1
    while s < tm:
        if rev:
            a_sh, x_sh, ok = pltpu.roll(a, tm - s, 0), pltpu.roll(x, tm - s, 0), row < tm - s
        else:
            a_sh, x_sh, ok = pltpu.roll(a, s, 0), pltpu.roll(x, s, 0), row >= s
        x = jnp.where(ok, x + a * x_sh, x)
        a = jnp.where(ok, a * a_sh, a)
        s *= 2
    return a, x


def _scan_tile_of(s, rev, nct, nt):
    if not rev:
        return s
    return jnp.where(s < nct, nct - 1 - s, nt - 1 - (s - nct))


def _lru_scan(a2, x2, d, nct, tm, tc):
    _, B, T, W = a2.shape
    nt = T // tm
    rev = d == 1
    last = 0 if rev else tm - 1

    def body(a_ref, x_ref, h_ref, carry):
        s = pl.program_id(2)

        @pl.when(s == 0)
        def _():
            carry[...] = jnp.zeros_like(carry)

        A, X = _tile_scan(a_ref[0, 0], x_ref[0, 0], rev, tm)
        h = X + A * carry[0:1, :]
        h_ref[0] = h
        carry[...] = jnp.broadcast_to(h[last:last + 1, :], carry.shape)

    tidx = lambda s: _scan_tile_of(s, rev, nct, nt)
    spec = pl.BlockSpec((1, 1, tm, tc), lambda b, j, s: (d, b, tidx(s), j))
    return pl.pallas_call(
        body, name=f"lru_scan{d}", grid=(B, W // tc, nt),
        in_specs=[spec, spec], out_specs=pl.BlockSpec((1, tm, tc), lambda b, j, s: (b, tidx(s), j)),
        out_shape=jax.ShapeDtypeStruct((B, T, W), F32),
        scratch_shapes=[pltpu.VMEM((SUBLANES, tc), F32)],
        compiler_params=_cparams(("parallel", "parallel", "arbitrary"), VMEM_LIMIT),
    )(a2, x2)


def _lru_scan_bwd(a2, h, dh, d, nct, tm, tc):
    _, B, T, W = a2.shape
    nt = T // tm
    rev = d == 1
    first = tm - 1 if rev else 0
    r8, n8 = tm // SUBLANES, T // SUBLANES

    def body(a_ref, h_ref, hp_ref, dh_ref, lam_ref, da_ref, ca, cl):
        s = pl.program_id(2)

        @pl.when(s == 0)
        def _():
            ca[...] = jnp.zeros_like(ca)
            cl[...] = jnp.zeros_like(cl)

        a = a_ref[0, 0]
        row = lax.broadcasted_iota(jnp.int32, a.shape, 0)
        if rev:
            c = jnp.where(row == 0, ca[0:1, :], pltpu.roll(a, 1, 0))
        else:
            c = jnp.where(row == tm - 1, ca[0:1, :], pltpu.roll(a, tm - 1, 0))
        C, L = _tile_scan(c, dh_ref[0], not rev, tm)
        lam = L + C * cl[0:1, :]
        lam_ref[0] = lam
        hp = jnp.where(s == nt - 1, 0.0, hp_ref[0])
        if rev:
            hprev = jnp.where(row == tm - 1, hp[0:1, :], pltpu.roll(h_ref[0], tm - 1, 0))
        else:
            hprev = jnp.where(row == 0, hp[SUBLANES - 1:SUBLANES, :], pltpu.roll(h_ref[0], 1, 0))
        da_ref[0] = lam * hprev
        ca[...] = jnp.broadcast_to(a[first:first + 1, :], ca.shape)
        cl[...] = jnp.broadcast_to(lam[first:first + 1, :], cl.shape)

    tidx = lambda s: _scan_tile_of(nt - 1 - s, rev, nct, nt)

    def hp_idx(b, j, s):
        tt = tidx(s)
        if rev:
            blk = jnp.where(tt == nt - 1, 0, jnp.minimum((tt + 1) * r8, n8 - 1))
        else:
            blk = jnp.maximum(tt * r8 - 1, 0)
        return (b, blk, j)

    tile = pl.BlockSpec((1, tm, tc), lambda b, j, s: (b, tidx(s), j))
    return pl.pallas_call(
        body, name=f"lru_scan_bwd{d}", grid=(B, W // tc, nt),
        in_specs=[pl.BlockSpec((1, 1, tm, tc), lambda b, j, s: (d, b, tidx(s), j)), tile,
                  pl.BlockSpec((1, SUBLANES, tc), hp_idx), tile],
        out_specs=(tile, tile),
        out_shape=(jax.ShapeDtypeStruct((B, T, W), F32), jax.ShapeDtypeStruct((B, T, W), F32)),
        scratch_shapes=[pltpu.VMEM((SUBLANES, tc), F32), pltpu.VMEM((SUBLANES, tc), F32)],
        compiler_params=_cparams(("parallel", "parallel", "arbitrary"), VMEM_LIMIT),
    )(a2, h, h, dh)


def _lru_out(h0, h1, pyl, ycb0, nct, tm, tc):
    B, T, W = h0.shape
    N = T - nct * tm

    def body(h0_ref, h1_ref, y_ref, o_ref):
        o_ref[0] = ((h0_ref[0] + h1_ref[0]) * _gelu(y_ref[0])).astype(BF16)

    hs = pl.BlockSpec((1, tm, tc), lambda b, t, j: (b, t + nct, j))
    return pl.pallas_call(
        body, name="lru_out", grid=(B, N // tm, W // tc),
        in_specs=[hs, hs, pl.BlockSpec((1, tm, tc), lambda b, t, j: (b, t + nct, ycb0 + j))],
        out_specs=pl.BlockSpec((1, tm, tc), lambda b, t, j: (b, t, j)),
        out_shape=jax.ShapeDtypeStruct((B, N, W), BF16),
        compiler_params=_cparams(("parallel",) * 3, VMEM_LIMIT))(h0, h1, pyl)


def _lru_out_bwd(h0, h1, pyl, ycb0, dy, nct, tm, tc):
    B, T, W = h0.shape

    def body(h0_ref, h1_ref, y_ref, dy_ref, dh_ref, dyl_ref):
        t = pl.program_id(1)

        @pl.when(t < nct)
        def _():
            dh_ref[0] = jnp.zeros_like(dh_ref[0])
            dyl_ref[0] = jnp.zeros_like(dyl_ref[0])

        @pl.when(t >= nct)
        def _():
            y = y_ref[0]
            dy_ = dy_ref[0]
            dh_ref[0] = dy_ * _gelu(y)
            dyl_ref[0] = (dy_ * (h0_ref[0] + h1_ref[0]) * _dgelu(y)).astype(BF16)

    hs = pl.BlockSpec((1, tm, tc), lambda b, t, j: (b, t, j))
    return pl.pallas_call(
        body, name="lru_out_bwd", grid=(B, T // tm, W // tc),
        in_specs=[hs, hs, pl.BlockSpec((1, tm, tc), lambda b, t, j: (b, t, ycb0 + j)),
                  pl.BlockSpec((1, tm, tc), lambda b, t, j: (b, jnp.maximum(t - nct, 0), j))],
        out_specs=(hs, hs),
        out_shape=(jax.ShapeDtypeStruct((B, T, W), F32), jax.ShapeDtypeStruct((B, T, W), BF16)),
        compiler_params=_cparams(("parallel",) * 3, VMEM_LIMIT))(h0, h1, pyl, dy)


def _merge(bd, bl, pmg, bm, nct, tm):
    B, N, D = bd.shape

    def body(bd_ref, bl_ref, m0_ref, m1_ref, b_ref, o_ref):
        g0 = _sigmoid(m0_ref[0] + b_ref[:, 0:D])
        g1 = _sigmoid(m1_ref[0] + b_ref[:, D:2 * D])
        o_ref[0] = (g0 * bd_ref[0] + g1 * bl_ref[0]).astype(BF16)

    row = pl.BlockSpec((1, tm, D), lambda b, t: (b, t, 0))
    return pl.pallas_call(
        body, name="merge", grid=(B, N // tm),
        in_specs=[row, row, pl.BlockSpec((1, tm, D), lambda b, t: (b, t + nct, 0)),
                  pl.BlockSpec((1, tm, D), lambda b, t: (b, t + nct, 1)),
                  pl.BlockSpec((1, 2 * D), lambda b, t: (0, 0))],
        out_specs=row, out_shape=jax.ShapeDtypeStruct((B, N, D), BF16),
        compiler_params=_cparams(("parallel", "parallel"), VMEM_LIMIT))(bd, bl, pmg, pmg, bm)


def _merge_bwd(dmi, bd, bl, pmg, bm, nct, tm):
    B, N, D = bd.shape
    T = pmg.shape[1]

    def body(dm_ref, bd_ref, bl_ref, m0_ref, m1_ref, b_ref, dbd_ref, dbl_ref, dmg_ref, acc_ref):
        t = pl.program_id(1)

        @pl.when(t == 0)
        def _():
            acc_ref[...] = jnp.zeros_like(acc_ref)

        @pl.when(t < nct)
        def _():
            dmg_ref[0] = jnp.zeros_like(dmg_ref[0])

        @pl.when(t >= nct)
        def _():
            dm = dm_ref[0]
            g0 = _sigmoid(m0_ref[0] + b_ref[:, 0:D])
            g1 = _sigmoid(m1_ref[0] + b_ref[:, D:2 * D])
            dbd_ref[0] = (dm * g0).astype(BF16)
            dbl_ref[0] = (dm * g1).astype(BF16)
            d0 = dm * bd_ref[0] * g0 * (1.0 - g0)
            d1 = dm * bl_ref[0] * g1 * (1.0 - g1)
            dmg_ref[0, :, 0:D] = d0.astype(BF16)
            dmg_ref[0, :, D:2 * D] = d1.astype(BF16)
            acc_ref[0, 0:1, 0:D] += jnp.sum(d0, axis=0, keepdims=True)
            acc_ref[0, 0:1, D:2 * D] += jnp.sum(d1, axis=0, keepdims=True)

    lat = pl.BlockSpec((1, tm, D), lambda b, t: (b, jnp.maximum(t - nct, 0), 0))
    return pl.pallas_call(
        body, name="merge_bwd", grid=(B, T // tm),
        in_specs=[lat, lat, lat, pl.BlockSpec((1, tm, D), lambda b, t: (b, t, 0)),
                  pl.BlockSpec((1, tm, D), lambda b, t: (b, t, 1)),
                  pl.BlockSpec((1, 2 * D), lambda b, t: (0, 0))],
        out_specs=(lat, lat, pl.BlockSpec((1, tm, 2 * D), lambda b, t: (b, t, 0)),
                   pl.BlockSpec((1, SUBLANES, 2 * D), lambda b, t: (b, 0, 0))),
        out_shape=(jax.ShapeDtypeStruct((B, N, D), BF16), jax.ShapeDtypeStruct((B, N, D), BF16),
                   jax.ShapeDtypeStruct((B, T, 2 * D), BF16), jax.ShapeDtypeStruct((B, SUBLANES, 2 * D), F32)),
        compiler_params=_cparams(("parallel", "arbitrary"), VMEM_LIMIT))(dmi, bd, bl, pmg, pmg, bm)


def _grid_taps():
    return [((di + 1) * 3 + (dj + 1), di, dj) for di in (-1, 0, 1) for dj in (-1, 0, 1)]


def _grid_views(x, n):
    pos = lax.broadcasted_iota(jnp.int32, x.shape, 0)
    gc = jnp.bitwise_and(pos, GRID_W - 1)
    cols = {0: x,
            -1: jnp.where(gc >= 1, pltpu.roll(x, 1, 0), 0.0),
            1: jnp.where(gc <= GRID_W - 2, pltpu.roll(x, n - 1, 0), 0.0)}
    views = {}
    edge = jnp.zeros((GRID_W, x.shape[1]), x.dtype)
    for dj, xc in cols.items():
        views[(0, dj)] = xc
        views[(-1, dj)] = jnp.concatenate([edge, xc[:n - GRID_W]], axis=0)
        views[(1, dj)] = jnp.concatenate([xc[GRID_W:], edge], axis=0)
    return views


def _ffn_act(F, w9, b, tc):
    B, N, C2 = F.shape
    Cf = C2 // 2
    ncb = Cf // tc

    def body(g_ref, v_ref, w_ref, b_ref, o_ref):
        xv = _grid_views(g_ref[0], N)
        conv = b_ref[...]
        for k, di, dj in _grid_taps():
            conv = conv + xv[(di, dj)] * w_ref[k:k + 1, :]
        o_ref[0] = (_gelu(conv) * v_ref[0]).astype(BF16)

    return pl.pallas_call(
        body, name="ffn_act", grid=(B, ncb),
        in_specs=[pl.BlockSpec((1, N, tc), lambda b, j: (b, 0, j)),
                  pl.BlockSpec((1, N, tc), lambda b, j: (b, 0, ncb + j)),
                  pl.BlockSpec((9, tc), lambda b, j: (0, j)),
                  pl.BlockSpec((1, tc), lambda b, j: (0, j))],
        out_specs=pl.BlockSpec((1, N, tc), lambda b, j: (b, 0, j)),
        out_shape=jax.ShapeDtypeStruct((B, N, Cf), BF16),
        compiler_params=_cparams(("parallel", "parallel"), VMEM_LIMIT))(F, F, w9, b)


def _ffn_act_bwd(F, w9, b, df, tc):
    B, N, C2 = F.shape
    Cf = C2 // 2
    ncb = Cf // tc

    def body(g_ref, v_ref, w_ref, b_ref, df_ref, dF_ref, acc_ref, dv_s):
        p = pl.program_id(2)

        @pl.when(p == 0)
        def _():
            xv = _grid_views(g_ref[0], N)
            conv = b_ref[...]
            for k, di, dj in _grid_taps():
                conv = conv + xv[(di, dj)] * w_ref[k:k + 1, :]
            df_ = df_ref[0]
            th = jnp.tanh(GELU_C * (conv + 0.044715 * conv * conv * conv))
            dv_s[...] = (df_ * (0.5 * conv * (1.0 + th))).astype(BF16)
            dgel = 0.5 * (1.0 + th) + 0.5 * conv * (1.0 - th * th) * GELU_C * (1.0 + 3.0 * 0.044715 * conv * conv)
            dc = df_ * v_ref[0] * dgel
            dcv = _grid_views(dc, N)
            dx = None
            for k, di, dj in _grid_taps():
                term = dcv[(-di, -dj)] * w_ref[k:k + 1, :]
                dx = term if dx is None else dx + term
                acc_ref[0, k:k + 1, :] = jnp.sum(dc * xv[(di, dj)], axis=0, keepdims=True)
            acc_ref[0, 9:10, :] = jnp.sum(dc, axis=0, keepdims=True)
            acc_ref[0, 10:16, :] = jnp.zeros((6, tc), F32)
            dF_ref[0] = dx.astype(BF16)

        @pl.when(p == 1)
        def _():
            dF_ref[0] = dv_s[...]

    return pl.pallas_call(
        body, name="ffn_act_bwd", grid=(B, ncb, 2),
        in_specs=[pl.BlockSpec((1, N, tc), lambda b, j, p: (b, 0, j)),
                  pl.BlockSpec((1, N, tc), lambda b, j, p: (b, 0, ncb + j)),
                  pl.BlockSpec((9, tc), lambda b, j, p: (0, j)),
                  pl.BlockSpec((1, tc), lambda b, j, p: (0, j)),
                  pl.BlockSpec((1, N, tc), lambda b, j, p: (b, 0, j))],
        out_specs=(pl.BlockSpec((1, N, tc), lambda b, j, p: (b, 0, j + p * ncb)),
                   pl.BlockSpec((1, 16, tc), lambda b, j, p: (b, 0, j))),
        out_shape=(jax.ShapeDtypeStruct((B, N, C2), BF16), jax.ShapeDtypeStruct((B, 16, Cf), F32)),
        scratch_shapes=[pltpu.VMEM((N, tc), BF16)],
        compiler_params=_cparams(("parallel", "parallel", "arbitrary"), VMEM_LIMIT))(F, F, w9, b, df)


ADAM_ROWS = 512


def _adamw(w, m, v, gparts, name):
    rows, cols = w.shape[-2:]
    P = gparts.shape[0]
    tr = _row_tile(rows, (P + 14) * 4 * (-(-cols // LANES) * LANES))
    c1 = 1.0 - ADAM_B1 ** ADAM_STEP
    c2 = 1.0 - ADAM_B2 ** ADAM_STEP

    def body(w_ref, m_ref, v_ref, g_ref, go_ref, d_ref, mo_ref, vo_ref):
        g = g_ref[0].astype(F32)
        for p in range(1, P):
            g = g + g_ref[p].astype(F32)
        m_ = ADAM_B1 * m_ref[...] + (1.0 - ADAM_B1) * g
        v_ = ADAM_B2 * v_ref[...] + (1.0 - ADAM_B2) * (g * g)
        go_ref[...] = g
        mo_ref[...] = m_
        vo_ref[...] = v_
        d_ref[...] = -ADAM_LR * ((m_ / c1) / (jnp.sqrt(v_ / c2) + ADAM_EPS) + ADAM_WD * w_ref[...])

    if w.ndim == 3:
        row = pl.BlockSpec((None, tr, cols), lambda i: (0, i, 0))
    else:
        row = pl.BlockSpec((tr, cols), lambda i: (i, 0))
    out = jax.ShapeDtypeStruct(w.shape, F32)
    return pl.pallas_call(
        body, name=name, grid=(rows // tr,),
        in_specs=[row, row, row, pl.BlockSpec((P, tr, cols), lambda i: (0, i, 0))],
        out_specs=(row, row, row, row), out_shape=(out, out, out, out),
        compiler_params=_cparams(("parallel",), VMEM_LIMIT))(w, m, v, gparts)


def _pack_rows(flat_len):
    rows = -(-flat_len // LANES)
    if rows > ADAM_ROWS:
        rows = -(-rows // ADAM_ROWS) * ADAM_ROWS
    else:
        rows = -(-rows // SUBLANES) * SUBLANES
    return rows


PACK_ALIGN = SUBLANES * LANES


def _pack(arrs, lead=()):
    pads = [(0, 0)] * len(lead)
    parts = []
    for a in arrs:
        f = a.reshape(lead + (-1,)).astype(F32)
        parts.append(jnp.pad(f, pads + [(0, -f.shape[-1] % PACK_ALIGN)]))
    flat = jnp.concatenate(parts, axis=-1)
    n = flat.shape[-1]
    rows = _pack_rows(n)
    flat = jnp.pad(flat, pads + [(0, rows * LANES - n)])
    return flat.reshape(lead + (rows, LANES))


def _unpack(buf, shapes):
    out, r = [], 0
    for s in shapes:
        n = math.prod(s)
        nr = -(-n // PACK_ALIGN) * SUBLANES
        out.append(buf[r:r + nr].reshape(-1)[:n].reshape(s))
        r += nr
    return out


def _col_shards(full, n_lead):
    C = full.shape[-1]
    x = full.reshape(full.shape[:-1] + (N_DEV, C // N_DEV))
    return jnp.moveaxis(x, -2, 0)


def _from_col_shards(g):
    x = jnp.moveaxis(g, 0, -2)
    return x.reshape(x.shape[:-2] + (x.shape[-2] * x.shape[-1],))


def kernel(x, c, ctx, c_ctx, w_ada, b_ada, g_pre_mix, g_post_mix, g_pre_ffn, g_post_ffn, w_in, b_merge, dn_conv, dn_a_log, dn_dt_bias, dn_onorm, lru_conv, lru_conv_b, lru_w_rg, lru_b_rg, lru_w_ig, lru_b_ig, lru_lambda, w_branch_dn, w_branch_lru, w_out, w_up, ffn_dw, ffn_dw_b, w_down, loss_target, m_c_ctx, m_w_ada, m_b_ada, m_g_pre_mix, m_g_post_mix, m_g_pre_ffn, m_g_post_ffn, m_w_in, m_b_merge, m_dn_conv, m_dn_a_log, m_dn_dt_bias, m_dn_onorm, m_lru_conv, m_lru_conv_b, m_lru_w_rg, m_lru_b_rg, m_lru_w_ig, m_lru_b_ig, m_lru_lambda, m_w_branch_dn, m_w_branch_lru, m_w_out, m_w_up, m_ffn_dw, m_ffn_dw_b, m_w_down, v_c_ctx, v_w_ada, v_b_ada, v_g_pre_mix, v_g_post_mix, v_g_pre_ffn, v_g_post_ffn, v_w_in, v_b_merge, v_dn_conv, v_dn_a_log, v_dn_dt_bias, v_dn_onorm, v_lru_conv, v_lru_conv_b, v_lru_w_rg, v_lru_b_rg, v_lru_w_ig, v_lru_b_ig, v_lru_lambda, v_w_branch_dn, v_w_branch_lru, v_w_out, v_w_up, v_ffn_dw, v_ffn_dw_b, v_w_down):
    params = dict(c_ctx=c_ctx, w_ada=w_ada, b_ada=b_ada, g_pre_mix=g_pre_mix, g_post_mix=g_post_mix, g_pre_ffn=g_pre_ffn, g_post_ffn=g_post_ffn, w_in=w_in, b_merge=b_merge, dn_conv=dn_conv, dn_a_log=dn_a_log, dn_dt_bias=dn_dt_bias, dn_onorm=dn_onorm, lru_conv=lru_conv, lru_conv_b=lru_conv_b, lru_w_rg=lru_w_rg, lru_b_rg=lru_b_rg, lru_w_ig=lru_w_ig, lru_b_ig=lru_b_ig, lru_lambda=lru_lambda, w_branch_dn=w_branch_dn, w_branch_lru=w_branch_lru, w_out=w_out, w_up=w_up, ffn_dw=ffn_dw, ffn_dw_b=ffn_dw_b, w_down=w_down)
    mom1 = dict(c_ctx=m_c_ctx, w_ada=m_w_ada, b_ada=m_b_ada, g_pre_mix=m_g_pre_mix, g_post_mix=m_g_post_mix, g_pre_ffn=m_g_pre_ffn, g_post_ffn=m_g_post_ffn, w_in=m_w_in, b_merge=m_b_merge, dn_conv=m_dn_conv, dn_a_log=m_dn_a_log, dn_dt_bias=m_dn_dt_bias, dn_onorm=m_dn_onorm, lru_conv=m_lru_conv, lru_conv_b=m_lru_conv_b, lru_w_rg=m_lru_w_rg, lru_b_rg=m_lru_b_rg, lru_w_ig=m_lru_w_ig, lru_b_ig=m_lru_b_ig, lru_lambda=m_lru_lambda, w_branch_dn=m_w_branch_dn, w_branch_lru=m_w_branch_lru, w_out=m_w_out, w_up=m_w_up, ffn_dw=m_ffn_dw, ffn_dw_b=m_ffn_dw_b, w_down=m_w_down)
    mom2 = dict(c_ctx=v_c_ctx, w_ada=v_w_ada, b_ada=v_b_ada, g_pre_mix=v_g_pre_mix, g_post_mix=v_g_post_mix, g_pre_ffn=v_g_pre_ffn, g_post_ffn=v_g_post_ffn, w_in=v_w_in, b_merge=v_b_merge, dn_conv=v_dn_conv, dn_a_log=v_dn_a_log, dn_dt_bias=v_dn_dt_bias, dn_onorm=v_dn_onorm, lru_conv=v_lru_conv, lru_conv_b=v_lru_conv_b, lru_w_rg=v_lru_w_rg, lru_b_rg=v_lru_b_rg, lru_w_ig=v_lru_w_ig, lru_b_ig=v_lru_b_ig, lru_lambda=v_lru_lambda, w_branch_dn=v_w_branch_dn, w_branch_lru=v_w_branch_lru, w_out=v_w_out, w_up=v_w_up, ffn_dw=v_ffn_dw, ffn_dw_b=v_ffn_dw_b, w_down=v_w_down)
    names = list(params)

    B, N, D = x.shape
    NC = ctx.shape[1]
    T = NC + N
    H, hd = dn_a_log.shape[2], dn_onorm.shape[1]
    HD = H * hd
    nd = 2 * H
    W = lru_conv_b.shape[1]
    nblk, bdim = lru_w_rg.shape[2], lru_w_rg.shape[3]
    Cf = ffn_dw_b.shape[1]
    sw = w_ada.shape[2]
    tm = min(256, NC)
    nct = NC // tm
    ncc = NC // CHUNK
    nch = T // CHUNK
    R, RL = B * T, B * N
    bw = min(256, W)
    me = _my_index()
    assert NC % tm == 0 and N % tm == 0 and B + 1 <= SUBLANES and bw % bdim == 0 and D % LANES == 0

    cpad = jnp.zeros((SUBLANES, D), F32).at[:B].set(c).at[B].set(c_ctx)
    ccp = _all_gather([cpad], "ag_cond")[0].reshape(N_DEV * SUBLANES, D)
    mods_sh = _ada_fwd(ccp, w_ada[0], lax.dynamic_slice(b_ada, (0, me * sw), (1, sw)))
    lru_vecs = jnp.concatenate([lru_b_rg[0], lru_b_ig[0], lru_lambda[0], jnp.zeros_like(lru_lambda[0])], axis=0)
    mods_g, w_in_g, dn_conv_g, lru_conv_g, lru_vecs_g = _all_gather(
        [mods_sh, w_in[0].astype(BF16).T, dn_conv[0], lru_conv[0], lru_vecs], "ag_first")
    ag_rest = _split_start(
        [w_up[0].astype(BF16), w_down[0].astype(BF16), w_branch_dn[0].astype(BF16), w_branch_lru[0].astype(BF16),
         w_out[0].astype(BF16), ffn_dw[0].reshape(9, -1)], "gather", "ag_rest_start")
    mine = lax.dynamic_slice(mods_g, (0, me * SUBLANES, 0), (N_DEV, SUBLANES, sw))
    mods = jnp.moveaxis(mine, 0, 1).reshape(SUBLANES, 6, D)[:B + 1]
    mods = jnp.pad(mods, ((0, 0), (0, SUBLANES - 6), (0, 0))) + ag_rest[4][0, 0]
    dn_conv_f = _from_col_shards(dn_conv_g)
    lru_conv_f = _from_col_shards(lru_conv_g)
    lru_vecs_f = _from_col_shards(lru_vecs_g)
    lru_lam_f = lru_vecs_f[4:6]

    csh = w_in_g.shape[1]
    w_in_t = w_in_g.reshape(N_DEV * csh, D)
    o_z, o_a, o_xl = 3 * HD, 4 * HD, 4 * HD + 2 * nd
    o_yl, o_mg = o_xl + W, o_xl + 2 * W
    w_qkv, w_z = w_in_t[:o_z], w_in_t[o_z:o_a]
    w_ab = jnp.pad(w_in_t[o_a:o_xl], ((0, LANES - 2 * nd), (0, 0)))
    w_xl, w_yl, w_mg = w_in_t[o_xl:o_yl], w_in_t[o_yl:o_mg], w_in_t[o_mg:]

    def blockdiag(wg):
        per = bw // bdim
        g5 = wg.reshape(2, W // bw, per, bdim, bdim)
        eye = jnp.eye(per, dtype=wg.dtype)
        return jnp.einsum("dtpij,pq->dtpiqj", g5, eye).reshape(2, W // bw, bw, bw)

    bd_rg, bd_ig = blockdiag(lru_w_rg[0]), blockdiag(lru_w_ig[0])
    wbd = jnp.stack([bd_rg[0], bd_ig[0], bd_rg[1], bd_ig[1]]).astype(BF16)
    bias4 = jnp.stack([lru_vecs_f[0], lru_vecs_f[2], lru_vecs_f[1], lru_vecs_f[3]])
    alog_v = jnp.pad(dn_a_log.reshape(1, nd), ((0, 0), (0, LANES - nd)))
    dtb_v = jnp.pad(dn_dt_bias.reshape(1, nd), ((0, 0), (0, LANES - nd)))

    h0 = jnp.concatenate([ctx, x], axis=1)
    u1 = _norm_mod_fwd(h0, g_pre_mix, mods, 0, 1, nct, tm, "norm_mod1")
    u1f = u1.reshape(R, D)
    p_qkv = _mm(u1f, w_qkv, "nt", "mm_qkv").reshape(B, T, 3 * HD)
    p_z = _mm(u1f, w_z, "nt", "mm_z").reshape(B, T, HD)
    p_ab = _mm(u1f, w_ab, "nt", "mm_ab")
    p_xl = _mm(u1f, w_xl, "nt", "mm_xl").reshape(B, T, W)
    p_yl = _mm(u1f, w_yl, "nt", "mm_yl").reshape(B, T, W)
    p_mg = _mm(u1f, w_mg, "nt", "mm_mg").reshape(B, T, 2 * D)

    conv_offs = (-2, -1, 0, 1)
    tcc = _tile(HD, 512)
    gb = _ab_act(p_ab, alog_v, dtb_v, nd, tm)
    gb3 = gb.reshape(B, T, LANES)
    cv = _dwconv(p_qkv, 0, 3 * HD, dn_conv_f, None, conv_offs, nct, tm, tcc, "dn_conv")
    qkvn = _dn_act(cv, H, hd, tm)
    o_d0, o_d1, s_in0, s_in1 = _delta_fwd(qkvn, gb3, H, hd, ncc)
    o2 = (o_d0, o_d1)
    y_dn = _dn_out(o2, p_z, 0, dn_onorm, H, hd, nct, tm)

    tcw = _tile(W, 512)
    xc = _dwconv(p_xl, 0, W, lru_conv_f, lru_conv_b, conv_offs, nct, tm, tcw, "lru_conv")
    a2, inp2 = _lru_gates(xc, wbd, bias4, lru_lam_f, tm, bw)
    hd0 = _lru_scan(a2, inp2, 0, nct, tm, tcw)
    hd1 = _lru_scan(a2, inp2, 1, nct, tm, tcw)
    y_lru = _lru_out(hd0, hd1, p_yl, 0, nct, tm, tcw)

    rest_x, rest_land = _split_wait(ag_rest, "gather", y_lru, "ag_rest_wait")
    w_up_g, w_down_g, w_bdn_g, w_blru_g, w_out_g, ffn_dw_g = [
        lax.dynamic_update_slice(land, x_[None], (me,) + (0,) * x_.ndim) for land, x_ in zip(rest_land, rest_x)]
    w_down_f = w_down_g.reshape(Cf, D)
    w_bdn_f, w_blru_f, w_out_f = w_bdn_g.reshape(HD, D), w_blru_g.reshape(W, D), w_out_g.reshape(D, D)
    ffn_dw_f = _from_col_shards(ffn_dw_g)
    bd = _mm(y_dn.reshape(RL, HD), w_bdn_f, "nn", "mm_bdn").reshape(B, N, D)
    bl = _mm(y_lru.reshape(RL, W), w_blru_f, "nn", "mm_blru").reshape(B, N, D)
    mixin = _merge(bd, bl, p_mg, b_merge, nct, tm)
    mix = _mm(mixin.reshape(RL, D), w_out_f, "nn", "mm_out").reshape(B, N, D)
    h1 = _resid_norm_fwd(x, mix, g_post_mix, mods, 2, tm, "resid_norm1")
    u2 = _norm_mod_fwd(h1, g_pre_ffn, mods, 3, 4, 0, tm, "norm_mod2")
    Fp = _mm(u2.reshape(RL, D), w_up_g, "nn", "mm_up").reshape(B, N, 2 * Cf)
    tcf = LANES
    f = _ffn_act(Fp, ffn_dw_f, ffn_dw_b, tcf)
    dproj = _mm(f.reshape(RL, Cf), w_down_f, "nn", "mm_down").reshape(B, N, D)

    dd, dh2, acc_f = _final_fwd_bwd(h1, dproj, g_post_ffn, mods, 5, loss_target, tm)
    loss = lax.psum((0.5 / D) * jnp.sum(acc_f[:, 2, :]), MESH_AXES)
    ddf = dd.reshape(RL, D)
    df = _mm(ddf, w_down_f, "nt", "mm_down_dx").reshape(B, N, Cf)
    gw_down = _mm(f.reshape(RL, Cf), ddf, "tn", "mm_down_dw", out_dtype=BF16)
    dF, acc_ffn = _ffn_act_bwd(Fp, ffn_dw_f, ffn_dw_b, df, tcf)
    dFf = dF.reshape(RL, 2 * Cf)
    du2 = _mm(dFf, w_up_g, "nt", "mm_up_dx").reshape(B, N, D)
    gw_up = _mm(u2.reshape(RL, D), dFf, "tn", "mm_up_dw", out_dtype=BF16, out_shards=N_DEV)
    jm = 2 * lax.axis_index("x") + lax.axis_index("y")

    def pair_sums(parts8, tag):
        sib4 = _pair_exchange(parts8, "rs_pair_" + tag)
        return [_pair_sum(p8, s4, f"rs_pair_sum_{tag}{i}") for i, (p8, s4) in enumerate(zip(parts8, sib4))]

    def own_slab(land, x4):
        idx = (jm,) + (0,) * (x4.ndim - 1)
        return lax.dynamic_update_slice(land, lax.dynamic_slice(x4, idx, (1,) + x4.shape[1:]), idx)

    q_early = _split_start(pair_sums([gw_up, gw_down.reshape(N_DEV, Cf // N_DEV, D)], "ffn"), "quad", "rs_quad_ffn_start")
    mods_b = mods + q_early[4][0, 0]
    dh1, acc2x, _ = _norm_mod_bwd(h1, du2, g_pre_ffn, mods_b, 4, 0, tm, dh2, "norm_mod2_bwd")
    dmix, acc_r1 = _resid_norm_bwd(mix, dh1, g_post_mix, mods, 2, tm, "resid_norm1_bwd")
    dmixf = dmix.reshape(RL, D)
    dmixin = _mm(dmixf, w_out_f, "nt", "mm_out_dx").reshape(B, N, D)
    gw_out = _mm(mixin.reshape(RL, D), dmixf, "tn", "mm_out_dw", out_dtype=BF16)
    dbd, dbl, dmg, acc_mg = _merge_bwd(dmixin, bd, bl, p_mg, b_merge, nct, tm)
    dy_dn = _mm(dbd.reshape(RL, D), w_bdn_f, "nt", "mm_bdn_dx").reshape(B, N, HD)
    gw_bdn = _mm(y_dn.reshape(RL, HD), dbd.reshape(RL, D), "tn", "mm_bdn_dw", out_dtype=BF16)
    dy_lru = _mm(dbl.reshape(RL, D), w_blru_f, "nt", "mm_blru_dx").reshape(B, N, W)
    gw_blru = _mm(y_lru.reshape(RL, W), dbl.reshape(RL, D), "tn", "mm_blru_dw", out_dtype=BF16)

    dh, dyl = _lru_out_bwd(hd0, hd1, p_yl, 0, dy_lru, nct, tm, tcw)
    lam0, da0 = _lru_scan_bwd(a2, hd0, dh, 0, nct, tm, tcw)
    lam1, da1 = _lru_scan_bwd(a2, hd1, dh, 1, nct, tm, tcw)
    dxc, dwbd, acc_lru = _lru_gates_bwd(xc, wbd, bias4, lru_lam_f, (lam0, lam1), (da0, da1), tm, bw)
    dxl = _dwconv(dxc, 0, W, lru_conv_f, None, tuple(-o for o in conv_offs), nct, tm, tcw, "lru_conv_dx", BF16)
    acc_lc = _dwconv_wgrad(dxc, p_xl, 0, W, conv_offs, nct, tm, tcw, "lru_conv_dw")

    do, dz, acc_on = _dn_out_bwd(o2, p_z, 0, dn_onorm, dy_dn, H, hd, nct, tm)
    dqkvn0, dqkvn1, dgb0, dgb1 = _delta_bwd(qkvn, gb3, (s_in0, s_in1), do, H, hd, ncc)
    dcv = _dn_act_bwd(cv, (dqkvn0, dqkvn1), H, hd, tm)
    dqkv = _dwconv(dcv, 0, 3 * HD, dn_conv_f, None, tuple(-o for o in conv_offs), nct, tm, tcc, "dn_conv_dx", BF16)
    acc_dc = _dwconv_wgrad(dcv, p_qkv, 0, 3 * HD, conv_offs, nct, tm, tcc, "dn_conv_dw")
    dp_ab, acc_ab = _ab_act_bwd(p_ab, gb, (dgb0.reshape(R, LANES), dgb1.reshape(R, LANES)), alog_v, dtb_v, nd, tm)

    groups = [(dqkv.reshape(R, 3 * HD), w_qkv, "qkv"), (dz.reshape(R, HD), w_z, "z"), (dp_ab, w_ab, "ab"),
              (dxl.reshape(R, W), w_xl, "xl"), (dyl.reshape(R, W), w_yl, "yl"), (dmg.reshape(R, 2 * D), w_mg, "mg")]
    du1 = _mm_sum([(dg_, wg_) for dg_, wg_, _ in groups], "mm_in_dx")
    gw_groups = [_mm(dg_, u1f, "tn", "mm_in_dw_" + nm, out_dtype=BF16) for dg_, _, nm in groups]
    gw_groups[2] = gw_groups[2][:2 * nd]
    gw_in = jnp.concatenate(gw_groups, axis=0).reshape(N_DEV, csh, D)
    grad_x, acc1x, acc1c = _norm_mod_bwd(h0, du1.reshape(B, T, D), g_pre_mix, mods, 1, nct, tm, dh1, "norm_mod1_bwd")

    g_lru_conv = jnp.sum(acc_lc[:, :4], 0)
    g_dn_conv = jnp.sum(acc_dc[:, :4], 0)
    g_ffn_dw = jnp.sum(acc_ffn[:, :9], 0).reshape(3, 3, Cf)
    sm_names = ["dn_conv", "lru_conv", "lru_b_rg", "lru_b_ig", "lru_lambda", "ffn_dw"]
    sm_parts = [_col_shards(g_dn_conv, 1), _col_shards(g_lru_conv, 1),
                _col_shards(jnp.stack([acc_lru[0], acc_lru[2]]), 1), _col_shards(jnp.stack([acc_lru[1], acc_lru[3]]), 1),
                _col_shards(acc_lru[4:6], 1), _col_shards(g_ffn_dw, 2)]
    late8 = [gw_in, gw_bdn.reshape(N_DEV, HD // N_DEV, D), gw_blru.reshape(N_DEV, W // N_DEV, D),
             gw_out.reshape(N_DEV, D // N_DEV, D), _pack(sm_parts, lead=(N_DEV,))]
    q_late = _split_start(pair_sums(late8, "mix"), "quad", "rs_quad_mix_start")
    ffn_x, ffn_land = _split_wait(q_early, "quad", q_late[4], "rs_quad_ffn_wait")
    results = {}
    for n, x4, land in zip(["w_up", "w_down"], ffn_x, ffn_land):
        results[n] = list(_adamw(params[n], mom1[n], mom2[n], own_slab(land, x4), "adamw_" + n))

    zeros_d = jnp.zeros((B, D), F32)
    dm_rows = jnp.stack([acc1x[:, 0], acc1x[:, 1], acc_r1[:, 0], acc2x[:, 0], acc2x[:, 1], acc_f[:, 0]], axis=1)
    dm_ctx = jnp.stack([jnp.sum(acc1c[:, 0], 0), jnp.sum(acc1c[:, 1], 0)] + [zeros_d[0]] * 4, axis=0)[None]
    dm_pad = jnp.zeros((SUBLANES, 6 * D), F32).at[:B + 1].set(jnp.concatenate([dm_rows, dm_ctx], 0).reshape(B + 1, 6 * D))
    dm_g = _all_gather([dm_pad], "ag_dmods")[0]
    g_mine = lax.dynamic_slice(dm_g, (0, 0, me * sw), (N_DEV, SUBLANES, sw)).reshape(N_DEV * SUBLANES, sw)
    gw_ada, dcc = _ada_bwd(ccp, w_ada[0], g_mine, c_ctx.reshape(1, D), B)

    per = bw // bdim

    def diag_blocks(dwf):
        g6 = dwf.reshape(W // bw, per, bdim, per, bdim)
        return jnp.einsum("tpiqj,pq->tpij", g6, jnp.eye(per, dtype=F32)).reshape(nblk, bdim, bdim)

    g_rep = dict(
        c_ctx=dcc[0],
        g_pre_mix=jnp.sum(acc1x[:, 2] + acc1c[:, 2], 0)[None], g_post_mix=jnp.sum(acc_r1[:, 1], 0)[None],
        g_pre_ffn=jnp.sum(acc2x[:, 2], 0)[None], g_post_ffn=jnp.sum(acc_f[:, 1], 0)[None],
        b_merge=jnp.sum(acc_mg[:, 0], 0)[None],
        dn_a_log=acc_ab[0, :nd].reshape(1, 2, H), dn_dt_bias=acc_ab[1, :nd].reshape(1, 2, H),
        dn_onorm=jnp.sum(acc_on[:, 0], 0)[None],
        lru_conv_b=jnp.sum(acc_lc[:, 4], 0)[None],
        lru_w_rg=jnp.stack([diag_blocks(dwbd[0]), diag_blocks(dwbd[2])])[None],
        lru_w_ig=jnp.stack([diag_blocks(dwbd[1]), diag_blocks(dwbd[3])])[None],
        ffn_dw_b=jnp.sum(acc_ffn[:, 9], 0)[None])
    rep_names = list(g_rep)
    rep_parts = _all_gather([_pack([g_rep[n] for n in rep_names])], "ag_rep_grads")[0]
    rep_out = _adamw(_pack([params[n] for n in rep_names]), _pack([mom1[n] for n in rep_names]),
                     _pack([mom2[n] for n in rep_names]), rep_parts, "adamw_rep")
    for k, buf in enumerate(rep_out):
        for n, arr in zip(rep_names, _unpack(buf, [params[n].shape for n in rep_names])):
            results.setdefault(n, [None] * 4)[k] = arr

    ba_out = _adamw(_pack([b_ada]), _pack([m_b_ada]), _pack([v_b_ada]),
                    _pack([dm_g.reshape(N_DEV * SUBLANES, 6 * D)], lead=(N_DEV * SUBLANES,)), "adamw_b_ada")
    results["b_ada"] = [_unpack(buf, [b_ada.shape])[0] for buf in ba_out]

    wa_out = _adamw(w_ada, m_w_ada, v_w_ada, gw_ada[None], "adamw_w_ada")
    results["w_ada"] = list(wa_out)

    mix_x, mix_land = _split_wait(q_late, "quad", wa_out[0], "rs_quad_mix_wait")
    recv4 = [own_slab(land, x4) for land, x4 in zip(mix_land, mix_x)]
    for n, g4 in zip(["w_in", "w_branch_dn", "w_branch_lru", "w_out"], recv4[:-1]):
        if n == "w_in":
            g4 = jnp.swapaxes(g4, 1, 2)
        results[n] = list(_adamw(params[n], mom1[n], mom2[n], g4, "adamw_" + n))
    sm_out = _adamw(_pack([params[n] for n in sm_names]), _pack([mom1[n] for n in sm_names]),
                    _pack([mom2[n] for n in sm_names]), recv4[-1], "adamw_small")
    for k, buf in enumerate(sm_out):
        for n, arr in zip(sm_names, _unpack(buf, [params[n].shape for n in sm_names])):
            results.setdefault(n, [None] * 4)[k] = arr

    outs = [loss, grad_x]
    for k in range(4):
        outs += [results[n][k] for n in names]
    return tuple(outs)
```

```python
import functools
import math

import jax
import jax.numpy as jnp
from jax import lax
from jax.experimental import pallas as pl
from jax.experimental.pallas import tpu as pltpu

F32 = jnp.float32
BF16 = jnp.bfloat16

EPS = 1e-6
GRID_W = 64
GRID_SHIFT = 6
CHUNK = 64
LRU_C = 8.0
N_DEV = 8
ADAM_LR = 0.001
ADAM_B1 = 0.9
ADAM_B2 = 0.999
ADAM_EPS = 1e-08
ADAM_WD = 0.01
ADAM_STEP = 10

LANES = 128
SUBLANES = 8
VMEM_LIMIT = 48 * 1024 * 1024
MESH_AXES = ("x", "y", "c")
GELU_C = math.sqrt(2.0 / math.pi)


def _cparams(sem=None, vmem=None):
    kw = {}
    if sem is not None:
        kw["dimension_semantics"] = sem
    if vmem is not None:
        kw["vmem_limit_bytes"] = vmem
    return pltpu.CompilerParams(**kw)


def _tile(n, pref):
    if n <= pref:
        return n
    t = (pref // LANES) * LANES
    while n % t:
        t -= LANES
    return t


def _sigmoid(x):
    return 1.0 / (1.0 + jnp.exp(-x))


def _silu(x):
    return x * _sigmoid(x)


def _dsilu(x):
    s = _sigmoid(x)
    return s * (1.0 + x * (1.0 - s))


def _softplus(x):
    return jnp.maximum(x, 0.0) + jnp.log(1.0 + jnp.exp(-jnp.abs(x)))


def _gelu(x):
    return 0.5 * x * (1.0 + jnp.tanh(GELU_C * (x + 0.044715 * x * x * x)))


def _dgelu(x):
    t = jnp.tanh(GELU_C * (x + 0.044715 * x * x * x))
    return 0.5 * (1.0 + t) + 0.5 * x * (1.0 - t * t) * GELU_C * (1.0 + 3.0 * 0.044715 * x * x)


def _dot(a, b, dims="nn"):
    dn = {"nn": (((1,), (0,)), ((), ())),
          "nt": (((1,), (1,)), ((), ())),
          "tn": (((0,), (0,)), ((), ()))}[dims]
    return lax.dot_general(a.astype(BF16), b.astype(BF16), dn, preferred_element_type=F32)


def _split2(x):
    hi = x.astype(BF16)
    lo = (x - hi.astype(F32)).astype(BF16)
    return hi, lo


def _split3(x):
    h1 = x.astype(BF16)
    r1 = x - h1.astype(F32)
    h2 = r1.astype(BF16)
    h3 = (r1 - h2.astype(F32)).astype(BF16)
    return h1, h2, h3


def _dot_hi(a, b):
    ah, al = _split2(a)
    bh, bl = _split2(b)
    return _dot(ah, bh) + (_dot(ah, bl) + _dot(al, bh))


def _dot_mask(m, x, dims="nn"):
    h1, h2, h3 = _split3(x)
    if dims == "xtn":
        return _dot(h1, m, "tn") + (_dot(h2, m, "tn") + _dot(h3, m, "tn"))
    return _dot(m, h1, dims) + (_dot(m, h2, dims) + _dot(m, h3, dims))


def _my_index():
    return 4 * lax.axis_index("x") + 2 * lax.axis_index("y") + lax.axis_index("c")


def _hbm_call(body, xs, out_shapes, n_sems, name):
    any_spec = pl.BlockSpec(memory_space=pl.ANY)
    return pl.pallas_call(
        body, name=name, out_shape=tuple(out_shapes),
        in_specs=[any_spec] * len(xs), out_specs=tuple([any_spec] * len(out_shapes)),
        scratch_shapes=[pltpu.SemaphoreType.DMA((n_sems,)), pltpu.SemaphoreType.DMA((n_sems,)),
                        pltpu.SemaphoreType.DMA((len(xs),))],
    )(*xs)


def _all_gather(xs, name):
    n = len(xs)

    def body(*refs):
        x_refs, out_refs = refs[:n], refs[n:2 * n]
        send_sems, recv_sems, local_sems = refs[2 * n:]
        x_, y_, c_ = lax.axis_index("x"), lax.axis_index("y"), lax.axis_index("c")
        me, sibling = (x_, y_, c_), (x_, y_, 1 - c_)
        chips = [(1 - x_, y_), (x_, 1 - y_), (1 - x_, 1 - y_)]

        def slab(a, px, py, pc):
            return out_refs[a].at[4 * px + 2 * py + pc]

        def copy(a, k, block, to, src=None):
            return pltpu.make_async_remote_copy(
                src_ref=slab(a, *block) if src is None else src, dst_ref=slab(a, *block),
                send_sem=send_sems.at[7 * a + k], recv_sem=recv_sems.at[7 * a + k],
                device_id=to, device_id_type=pl.DeviceIdType.MESH)

        mine = [pltpu.make_async_copy(x_refs[a], slab(a, *me), local_sems.at[a]) for a in range(n)]
        for cp in mine:
            cp.start()
        sent = []
        for j, chip in enumerate(chips):
            sent += [copy(a, 1 + j, me, (*chip, c_), src=x_refs[a]) for a in range(n)]
        sent += [copy(a, 0, me, sibling, src=x_refs[a]) for a in range(n)]
        for cp in sent:
            cp.start()
        for j, chip in enumerate(chips):
            for a in range(n):
                copy(a, 1 + j, (*chip, c_), me).wait_recv()
                fwd = copy(a, 4 + j, (*chip, c_), sibling)
                fwd.start()
                sent.append(fwd)
        for a in range(n):
            copy(a, 0, sibling, me).wait_recv()
        for j, chip in enumerate(chips):
            for a in range(n):
                copy(a, 4 + j, (*chip, 1 - c_), me).wait_recv()
        for cp in sent:
            cp.wait_send()
        for cp in mine:
            cp.wait()

    outs = [jax.ShapeDtypeStruct((N_DEV,) + x.shape, x.dtype) for x in xs]
    return _hbm_call(body, xs, outs, 7 * n, name)


def _pair_exchange(xs, name):
    n = len(xs)

    def body(*refs):
        x_refs, out_refs = refs[:n], refs[n:2 * n]
        send_sems, recv_sems, _ = refs[2 * n:]
        x_, y_, c_ = lax.axis_index("x"), lax.axis_index("y"), lax.axis_index("c")
        copies = []
        for a in range(n):
            for j in range(4):
                copies.append(pltpu.make_async_remote_copy(
                    src_ref=x_refs[a].at[2 * j + (1 - c_)], dst_ref=out_refs[a].at[j],
                    send_sem=send_sems.at[4 * a + j], recv_sem=recv_sems.at[4 * a + j],
                    device_id=(x_, y_, 1 - c_), device_id_type=pl.DeviceIdType.MESH))
        for cp in copies:
            cp.start()
        for cp in copies:
            cp.wait()

    outs = [jax.ShapeDtypeStruct((4,) + x.shape[1:], x.dtype) for x in xs]
    return _hbm_call(body, xs, outs, 4 * n, name)


def _split_views(kind, x_ref, land_ref, k):
    x_, y_, c_ = lax.axis_index("x"), lax.axis_index("y"), lax.axis_index("c")
    if kind == "gather":
        px = 1 - x_ if (k >> 2) & 1 else x_
        py = 1 - y_ if (k >> 1) & 1 else y_
        pc = 1 - c_ if k & 1 else c_
        return x_ref, land_ref.at[4 * x_ + 2 * y_ + c_], land_ref.at[4 * px + 2 * py + pc], (px, py, pc)
    px = 1 - x_ if (k >> 1) & 1 else x_
    py = 1 - y_ if k & 1 else y_
    return x_ref.at[2 * px + py], land_ref.at[2 * x_ + y_], land_ref.at[2 * px + py], (px, py, c_)


def _split_start(xs, kind, name):
    n = len(xs)
    npeer = 7 if kind == "gather" else 3
    lands = [lax.empty(((N_DEV,) + x.shape) if kind == "gather" else x.shape, x.dtype) for x in xs]

    def body(*refs):
        x_refs, land_refs = refs[:n], refs[n:2 * n]
        send_sems, recv_sems, token = refs[2 * n], refs[2 * n + 1], refs[4 * n + 2]
        for k in range(1, npeer + 1):
            for a in range(n):
                src, dst, _, peer = _split_views(kind, x_refs[a], land_refs[a], k)
                pltpu.make_async_remote_copy(
                    src_ref=src, dst_ref=dst, send_sem=send_sems.at[npeer * a + k - 1],
                    recv_sem=recv_sems.at[npeer * a + k - 1],
                    device_id=peer, device_id_type=pl.DeviceIdType.MESH).start()
        token[...] = jnp.zeros_like(token)

    hbm = pl.BlockSpec(memory_space=pltpu.HBM)
    sem = pl.BlockSpec(memory_space=pltpu.SEMAPHORE)
    sems = pltpu.SemaphoreType.DMA((npeer * n,))
    out = pl.pallas_call(
        body, name=name,
        out_shape=(sems, sems, *[pltpu.HBM(a.shape, a.dtype) for a in list(xs) + lands],
                   jax.ShapeDtypeStruct((SUBLANES, LANES), F32)),
        in_specs=[hbm] * (2 * n),
        out_specs=(sem, sem, *[hbm] * (2 * n), pl.BlockSpec(memory_space=pltpu.VMEM)),
        input_output_aliases={i: 2 + i for i in range(2 * n)},
        compiler_params=pltpu.CompilerParams(has_side_effects=pltpu.SideEffectType.DATAFLOW_SIDE_EFFECTING),
    )(*[pltpu.with_memory_space_constraint(a, pltpu.HBM) for a in list(xs) + lands])
    return out[0], out[1], list(out[2:2 + n]), list(out[2 + n:2 + 2 * n]), out[-1]


def _split_wait(started, kind, after, name):
    send_sems_, recv_sems_, x_thru, land_thru, _ = started
    n = len(x_thru)
    npeer = 7 if kind == "gather" else 3

    def body(*refs):
        x_refs, land_refs = refs[:n], refs[n:2 * n]
        send_sems, recv_sems = refs[2 * n], refs[2 * n + 1]
        for k in range(1, npeer + 1):
            for a in range(n):
                src, _, landed, peer = _split_views(kind, x_refs[a], land_refs[a], k)
                cp = pltpu.make_async_remote_copy(
                    src_ref=src, dst_ref=landed, send_sem=send_sems.at[npeer * a + k - 1],
                    recv_sem=recv_sems.at[npeer * a + k - 1],
                    device_id=peer, device_id_type=pl.DeviceIdType.MESH)
                cp.wait_send()
                cp.wait_recv()

    hbm = pl.BlockSpec(memory_space=pltpu.HBM)
    sem = pl.BlockSpec(memory_space=pltpu.SEMAPHORE)
    out = pl.pallas_call(
        body, name=name,
        out_shape=tuple(pltpu.HBM(a.shape, a.dtype) for a in x_thru + land_thru),
        in_specs=[hbm] * (2 * n) + [sem, sem, pl.BlockSpec(memory_space=pl.ANY)],
        out_specs=tuple([hbm] * (2 * n)),
        input_output_aliases={i: i for i in range(2 * n)},
        compiler_params=pltpu.CompilerParams(has_side_effects=pltpu.SideEffectType.DATAFLOW_SIDE_EFFECTING),
    )(*x_thru, *land_thru, send_sems_, recv_sems_, after)
    return list(out[:n]), list(out[n:])


def _pair_sum(x8, r4, name):
    _, r, c = x8.shape
    tr = _row_tile(r, c * 12)

    def body(core_ref, x_ref, r_ref, o_ref):
        o_ref[...] = (x_ref[...].astype(F32) + r_ref[...].astype(F32)).astype(o_ref.dtype)

    core = lax.axis_index("c").astype(jnp.int32).reshape(1)
    return pl.pallas_call(
        body, name=name,
        grid_spec=pltpu.PrefetchScalarGridSpec(
            num_scalar_prefetch=1, grid=(4, r // tr),
            in_specs=[pl.BlockSpec((None, tr, c), lambda j, i, core_ref: (2 * j + core_ref[0], i, 0)),
                      pl.BlockSpec((None, tr, c), lambda j, i, core_ref: (j, i, 0))],
            out_specs=pl.BlockSpec((None, tr, c), lambda j, i, core_ref: (j, i, 0))),
        out_shape=jax.ShapeDtypeStruct((4, r, c), x8.dtype),
        compiler_params=_cparams(("parallel", "parallel"), VMEM_LIMIT))(core, x8, r4)


def _row_tile(r, bytes_per_row, budget=4 * 1024 * 1024):
    best = None
    for t in range(16, r + 1, 16):
        if r % t == 0 and t * bytes_per_row <= budget:
            best = t
    return best if best is not None else r


def _mm(a, b, dims, name, out_dtype=F32, add=None, out_shards=1, tm=1024, tn=1024, tk=1024):
    S = b.shape[0] if b.ndim == 3 else 1
    bshape = (b.shape[1], S * b.shape[2]) if b.ndim == 3 else b.shape
    if dims == "nn":
        (M, K), (K2, N) = a.shape, bshape
    elif dims == "nt":
        (M, K), (N, K2) = a.shape, bshape
    else:
        (K, M), (K2, N) = a.shape, bshape
    assert K == K2, (a.shape, b.shape, dims)
    cs = bshape[1] // S
    tm, tk = _tile(M, tm), _tile(K, min(tk, cs) if (S > 1 and dims == "nt") else tk)
    tn = _tile(N, min(tn, cs) if (S > 1 and dims != "nt") else min(tn, N // out_shards))
    assert cs % (tk if dims == "nt" else tn) == 0 and (N // out_shards) % tn == 0
    nk = K // tk

    def body(*refs):
        if add is None:
            a_ref, b_ref, o_ref, acc = refs
        else:
            a_ref, b_ref, c_ref, o_ref, acc = refs
        k = pl.program_id(2)

        @pl.when(k == 0)
        def _():
            acc[...] = jnp.zeros_like(acc)

        acc[...] += _dot(a_ref[...], b_ref[...], dims)

        @pl.when(k == nk - 1)
        def _():
            r = acc[...]
            if add is not None:
                r = r + c_ref[...]
            o_ref[...] = r.astype(out_dtype)

    a_spec = (pl.BlockSpec((tk, tm), lambda i, j, k: (k, i)) if dims == "tn"
              else pl.BlockSpec((tm, tk), lambda i, j, k: (i, k)))
    if S == 1:
        b_spec = (pl.BlockSpec((tn, tk), lambda i, j, k: (j, k)) if dims == "nt"
                  else pl.BlockSpec((tk, tn), lambda i, j, k: (k, j)))
    elif dims == "nt":
        per = cs // tk
        b_spec = pl.BlockSpec((None, tn, tk), lambda i, j, k: (k // per, j, k % per))
    else:
        per = cs // tn
        b_spec = pl.BlockSpec((None, tk, tn), lambda i, j, k: (j // per, k, j % per))
    in_specs = [a_spec, b_spec]
    args = [a, b]
    if add is not None:
        in_specs.append(pl.BlockSpec((tm, tn), lambda i, j, k: (i, j)))
        args.append(add)
    if out_shards == 1:
        out_spec, out_shape = pl.BlockSpec((tm, tn), lambda i, j, k: (i, j)), (M, N)
    else:
        oper = N // out_shards // tn
        out_spec = pl.BlockSpec((None, tm, tn), lambda i, j, k: (j // oper, i, j % oper))
        out_shape = (out_shards, M, N // out_shards)
    return pl.pallas_call(
        body, name=name, grid=(M // tm, N // tn, nk),
        in_specs=in_specs, out_specs=out_spec,
        out_shape=jax.ShapeDtypeStruct(out_shape, out_dtype),
        scratch_shapes=[pltpu.VMEM((tm, tn), F32)],
        compiler_params=_cparams(("parallel", "parallel", "arbitrary"), VMEM_LIMIT),
    )(*args)


def _mm_sum(pairs, name, out_dtype=F32, tm=768, tn=1024, tk=512):
    M, N = pairs[0][0].shape[0], pairs[0][1].shape[1]
    tm, tn = _tile(M, tm), _tile(N, tn)
    tks = [_tile(a.shape[1], tk) for a, _ in pairs]
    nks = [a.shape[1] // t for (a, _), t in zip(pairs, tks)]
    starts = [sum(nks[:g]) for g in range(len(pairs))]
    nk = sum(nks)
    G = len(pairs)

    def body(*refs):
        o_ref, acc = refs[2 * G], refs[2 * G + 1]
        k = pl.program_id(2)

        @pl.when(k == 0)
        def _():
            acc[...] = jnp.zeros_like(acc)

        for g in range(G):
            @pl.when(jnp.logical_and(k >= starts[g], k < starts[g] + nks[g]))
            def _(g=g):
                acc[...] += _dot(refs[2 * g][...], refs[2 * g + 1][...])

        @pl.when(k == nk - 1)
        def _():
            o_ref[...] = acc[...].astype(out_dtype)

    in_specs, args = [], []
    for g, (a, b) in enumerate(pairs):
        kk = lambda k, g=g: jnp.clip(k - starts[g], 0, nks[g] - 1)
        in_specs += [pl.BlockSpec((tm, tks[g]), lambda i, j, k, kk=kk: (i, kk(k))),
                     pl.BlockSpec((tks[g], tn), lambda i, j, k, kk=kk: (kk(k), j))]
        args += [a, b]
    return pl.pallas_call(
        body, name=name, grid=(M // tm, N // tn, nk),
        in_specs=in_specs, out_specs=pl.BlockSpec((tm, tn), lambda i, j, k: (i, j)),
        out_shape=jax.ShapeDtypeStruct((M, N), out_dtype),
        scratch_shapes=[pltpu.VMEM((tm, tn), F32)],
        compiler_params=_cparams(("parallel", "parallel", "arbitrary"), VMEM_LIMIT),
    )(*args)


def _ada_fwd(ccp, w, b):
    def body(c_ref, w_ref, b_ref, o_ref):
        o_ref[...] = _dot(_silu(c_ref[...]), w_ref[...]) + b_ref[...]

    return pl.pallas_call(
        body, name="ada_fwd", out_shape=jax.ShapeDtypeStruct((ccp.shape[0], w.shape[1]), F32),
        compiler_params=_cparams(None, VMEM_LIMIT))(ccp, w, b)


def _ada_bwd(ccp, w, g, cctx, n_loc):
    def body(c_ref, w_ref, g_ref, cc_ref, gw_ref, dc_ref):
        gw_ref[...] = _dot(_silu(c_ref[...]), g_ref[...], "tn")
        dcc = _dot(g_ref[...], w_ref[...], "nt")
        row = lax.broadcasted_iota(jnp.int32, dcc.shape, 0)
        part = jnp.sum(jnp.where(row % SUBLANES == n_loc, dcc, 0.0), axis=0, keepdims=True)
        dc_ref[...] = jnp.broadcast_to(part * _dsilu(cc_ref[...]), dc_ref.shape)

    D = ccp.shape[1]
    return pl.pallas_call(
        body, name="ada_bwd",
        out_shape=(jax.ShapeDtypeStruct(w.shape, F32), jax.ShapeDtypeStruct((SUBLANES, D), F32)),
        compiler_params=_cparams(None, VMEM_LIMIT))(ccp, w, g, cctx)


def _norm_mod_fwd(h, gain, mods, i_shift, i_scale, nct, tm, name):
    B, T, D = h.shape

    def body(h_ref, g_ref, m_ref, u_ref):
        x = h_ref[0]
        r = lax.rsqrt(jnp.mean(x * x, axis=-1, keepdims=True) + EPS)
        n = x * r * g_ref[...]
        u_ref[0] = (n * (1.0 + m_ref[0, i_scale:i_scale + 1, :]) + m_ref[0, i_shift:i_shift + 1, :]).astype(BF16)

    return pl.pallas_call(
        body, name=name, grid=(B, T // tm),
        in_specs=[pl.BlockSpec((1, tm, D), lambda b, t: (b, t, 0)),
                  pl.BlockSpec((1, D), lambda b, t: (0, 0)),
                  pl.BlockSpec((1, SUBLANES, D), lambda b, t: (jnp.where(t < nct, B, b), 0, 0))],
        out_specs=pl.BlockSpec((1, tm, D), lambda b, t: (b, t, 0)),
        out_shape=jax.ShapeDtypeStruct((B, T, D), BF16),
        compiler_params=_cparams(("parallel", "parallel"), VMEM_LIMIT),
    )(h, gain, mods)


def _norm_mod_bwd(h, du, gain, mods, i_scale, nct, tm, res, name):
    B, T, D = h.shape
    nt = T // tm

    def body(h_ref, du_ref, g_ref, m_ref, res_ref, dx_ref, ax_ref, ac_ref):
        t = pl.program_id(1)
        x = h_ref[0]
        r = lax.rsqrt(jnp.mean(x * x, axis=-1, keepdims=True) + EPS)
        nh = x * r
        g = g_ref[...]
        du_ = du_ref[0]
        dn = du_ * (1.0 + m_ref[0, i_scale:i_scale + 1, :])
        dnh = dn * g
        dx = r * (dnh - nh * jnp.mean(dnh * nh, axis=-1, keepdims=True))
        sums = (jnp.sum(du_, axis=0, keepdims=True), jnp.sum(du_ * nh * g, axis=0, keepdims=True),
                jnp.sum(dn * nh, axis=0, keepdims=True))

        @pl.when(t == 0)
        def _():
            ax_ref[...] = jnp.zeros_like(ax_ref)
            ac_ref[...] = jnp.zeros_like(ac_ref)

        def accumulate(ref):
            for i, s in enumerate(sums):
                ref[0, i:i + 1, :] += s

        @pl.when(t < nct)
        def _():
            accumulate(ac_ref)

        @pl.when(t >= nct)
        def _():
            accumulate(ax_ref)
            dx_ref[0] = dx + res_ref[0]

    lat = lambda b, t: (b, jnp.maximum(t - nct, 0), 0)
    acc = pl.BlockSpec((1, SUBLANES, D), lambda b, t: (b, 0, 0))
    return pl.pallas_call(
        body, name=name, grid=(B, nt),
        in_specs=[pl.BlockSpec((1, tm, D), lambda b, t: (b, t, 0)),
                  pl.BlockSpec((1, tm, D), lambda b, t: (b, t, 0)),
                  pl.BlockSpec((1, D), lambda b, t: (0, 0)),
                  pl.BlockSpec((1, SUBLANES, D), lambda b, t: (jnp.where(t < nct, B, b), 0, 0)),
                  pl.BlockSpec((1, tm, D), lat)],
        out_specs=(pl.BlockSpec((1, tm, D), lat), acc, acc),
        out_shape=(jax.ShapeDtypeStruct(res.shape, F32),
                   jax.ShapeDtypeStruct((B, SUBLANES, D), F32), jax.ShapeDtypeStruct((B, SUBLANES, D), F32)),
        compiler_params=_cparams(("parallel", "arbitrary"), VMEM_LIMIT),
    )(h, du, gain, mods, res)


def _resid_norm_fwd(x, y, gain, mods, i_gate, tm, name):
    B, N, D = x.shape

    def body(x_ref, y_ref, g_ref, m_ref, o_ref):
        y_ = y_ref[0]
        r = lax.rsqrt(jnp.mean(y_ * y_, axis=-1, keepdims=True) + EPS)
        o_ref[0] = x_ref[0] + y_ * r * g_ref[...] * m_ref[0, i_gate:i_gate + 1, :]

    row = pl.BlockSpec((1, tm, D), lambda b, t: (b, t, 0))
    return pl.pallas_call(
        body, name=name, grid=(B, N // tm),
        in_specs=[row, row, pl.BlockSpec((1, D), lambda b, t: (0, 0)),
                  pl.BlockSpec((1, SUBLANES, D), lambda b, t: (b, 0, 0))],
        out_specs=row, out_shape=jax.ShapeDtypeStruct((B, N, D), F32),
        compiler_params=_cparams(("parallel", "parallel"), VMEM_LIMIT),
    )(x, y, gain, mods)


def _resid_norm_bwd_math(y_, dh, g, gate):
    r = lax.rsqrt(jnp.mean(y_ * y_, axis=-1, keepdims=True) + EPS)
    nh = y_ * r
    dgate = jnp.sum(dh * nh * g, axis=0, keepdims=True)
    dn = dh * gate
    dgain = jnp.sum(dn * nh, axis=0, keepdims=True)
    dnh = dn * g
    dy = r * (dnh - nh * jnp.mean(dnh * nh, axis=-1, keepdims=True))
    return dy, dgate, dgain


def _resid_norm_bwd(y, dh, gain, mods, i_gate, tm, name):
    B, N, D = y.shape

    def body(y_ref, dh_ref, g_ref, m_ref, dy_ref, acc_ref):
        t = pl.program_id(1)
        dy, dgate, dgain = _resid_norm_bwd_math(y_ref[0], dh_ref[0], g_ref[...], m_ref[0, i_gate:i_gate + 1, :])
        dy_ref[0] = dy.astype(BF16)

        @pl.when(t == 0)
        def _():
            acc_ref[...] = jnp.zeros_like(acc_ref)

        acc_ref[0, 0:1, :] += dgate
        acc_ref[0, 1:2, :] += dgain

    row = pl.BlockSpec((1, tm, D), lambda b, t: (b, t, 0))
    return pl.pallas_call(
        body, name=name, grid=(B, N // tm),
        in_specs=[row, row, pl.BlockSpec((1, D), lambda b, t: (0, 0)),
                  pl.BlockSpec((1, SUBLANES, D), lambda b, t: (b, 0, 0))],
        out_specs=(row, pl.BlockSpec((1, SUBLANES, D), lambda b, t: (b, 0, 0))),
        out_shape=(jax.ShapeDtypeStruct((B, N, D), BF16), jax.ShapeDtypeStruct((B, SUBLANES, D), F32)),
        compiler_params=_cparams(("parallel", "arbitrary"), VMEM_LIMIT),
    )(y, dh, gain, mods)


def _final_fwd_bwd(h1, d, gain, mods, i_gate, tgt, tm):
    B, N, D = h1.shape

    def body(h_ref, d_ref, g_ref, m_ref, t_ref, dd_ref, dh_ref, acc_ref):
        t = pl.program_id(1)
        d_ = d_ref[0]
        g = g_ref[...]
        gate = m_ref[0, i_gate:i_gate + 1, :]
        r = lax.rsqrt(jnp.mean(d_ * d_, axis=-1, keepdims=True) + EPS)
        e = h_ref[0] + d_ * r * g * gate - t_ref[0]
        dh = e * (1.0 / D)
        dh_ref[0] = dh
        dy, dgate, dgain = _resid_norm_bwd_math(d_, dh, g, gate)
        dd_ref[0] = dy.astype(BF16)

        @pl.when(t == 0)
        def _():
            acc_ref[...] = jnp.zeros_like(acc_ref)

        acc_ref[0, 0:1, :] += dgate
        acc_ref[0, 1:2, :] += dgain
        acc_ref[0, 2:3, :] += jnp.sum(e * e, axis=0, keepdims=True)

    row = pl.BlockSpec((1, tm, D), lambda b, t: (b, t, 0))
    return pl.pallas_call(
        body, name="final_fwd_bwd", grid=(B, N // tm),
        in_specs=[row, row, pl.BlockSpec((1, D), lambda b, t: (0, 0)),
                  pl.BlockSpec((1, SUBLANES, D), lambda b, t: (b, 0, 0)), row],
        out_specs=(row, row, pl.BlockSpec((1, SUBLANES, D), lambda b, t: (b, 0, 0))),
        out_shape=(jax.ShapeDtypeStruct((B, N, D), BF16), jax.ShapeDtypeStruct((B, N, D), F32),
                   jax.ShapeDtypeStruct((B, SUBLANES, D), F32)),
        compiler_params=_cparams(("parallel", "arbitrary"), VMEM_LIMIT),
    )(h1, d, gain, mods, tgt)


def _shifted(x, prev, nxt, off, row, tm):
    if off == 0:
        return x
    if off < 0:
        y = pltpu.roll(x, -off, 0)
        for r in range(-off):
            y = jnp.where(row == r, prev[SUBLANES + r + off:SUBLANES + r + off + 1, :], y)
        return y
    y = pltpu.roll(x, tm - off, 0)
    for r in range(off):
        y = jnp.where(row == tm - off + r, nxt[r:r + 1, :], y)
    return y


def _halo_specs(T, tm, tc, cb0, order):
    r8, n8 = tm // SUBLANES, T // SUBLANES
    if order == "btj":
        prev = lambda b, t, j: (b, jnp.maximum(t * r8 - 1, 0), cb0 + j)
        nxt = lambda b, t, j: (b, jnp.minimum((t + 1) * r8, n8 - 1), cb0 + j)
    else:
        prev = lambda b, j, t: (b, jnp.maximum(t * r8 - 1, 0), cb0 + j)
        nxt = lambda b, j, t: (b, jnp.minimum((t + 1) * r8, n8 - 1), cb0 + j)
    return pl.BlockSpec((1, SUBLANES, tc), prev), pl.BlockSpec((1, SUBLANES, tc), nxt)


def _seg_halos(p_ref, n_ref, t, nct, nt):
    seg_start = jnp.logical_or(t == 0, t == nct)
    seg_end = jnp.logical_or(t == nct - 1, t == nt - 1)
    prev = jnp.where(seg_start, 0.0, p_ref[0])
    nxt = jnp.where(seg_end, 0.0, n_ref[0])
    return prev, nxt


def _dwconv(x, col0, C, w, bias, offs, nct, tm, tc, name, out_dtype=F32, qkv_heads=None):
    B, T, _ = x.shape
    nt = T // tm
    cb0 = col0 // tc
    if qkv_heads is not None:
        assert tc == qkv_heads[0] * qkv_heads[1] and C == 3 * tc

    def body(*refs):
        refs = list(refs)
        a_ref = refs.pop() if qkv_heads is not None else None
        if bias is None:
            x_ref, p_ref, n_ref, w_ref, o_ref = refs
        else:
            x_ref, p_ref, n_ref, w_ref, b_ref, o_ref = refs
        t = pl.program_id(1)
        prev, nxt = _seg_halos(p_ref, n_ref, t, nct, nt)
        x_ = x_ref[0]
        row = lax.broadcasted_iota(jnp.int32, x_.shape, 0)
        acc = None
        for j, off in enumerate(offs):
            term = _shifted(x_, prev, nxt, off, row, tm) * w_ref[j:j + 1, :]
            acc = term if acc is None else acc + term
        if bias is not None:
            acc = acc + b_ref[...]
        o_ref[0] = acc.astype(out_dtype)
        if qkv_heads is not None:
            H, hd = qkv_heads
            part = pl.program_id(2)
            for h in range(H):
                sl = slice(h * hd, (h + 1) * hd)
                s = _silu(acc[:, sl])
                rs = lax.rsqrt(jnp.sum(s * s, axis=-1, keepdims=True) + EPS)
                scale = jnp.where(part == 0, rs * (float(hd) ** -0.5), jnp.where(part == 1, rs, 1.0))
                a_ref[0, :, sl] = s * scale

    pspec, nspec = _halo_specs(T, tm, tc, cb0, "btj")
    in_specs = [pl.BlockSpec((1, tm, tc), lambda b, t, j: (b, t, cb0 + j)), pspec, nspec,
                pl.BlockSpec((w.shape[0], tc), lambda b, t, j: (0, j))]
    args = [x, x, x, w]
    if bias is not None:
        in_specs.append(pl.BlockSpec((1, tc), lambda b, t, j: (0, j)))
        args.append(bias)
    tile = pl.BlockSpec((1, tm, tc), lambda b, t, j: (b, t, j))
    out_specs, out_shape = tile, jax.ShapeDtypeStruct((B, T, C), out_dtype)
    if qkv_heads is not None:
        out_specs, out_shape = (tile, tile), (out_shape, jax.ShapeDtypeStruct((B, T, C), F32))
    return pl.pallas_call(
        body, name=name, grid=(B, nt, C // tc),
        in_specs=in_specs, out_specs=out_specs, out_shape=out_shape,
        compiler_params=_cparams(("parallel", "parallel", "parallel"), VMEM_LIMIT),
    )(*args)


def _dwconv_wgrad(dy, x, col0, C, offs, nct, tm, tc, name):
    B, T, _ = x.shape
    nt = T // tm
    cb0 = col0 // tc
    K = len(offs)

    def body(dy_ref, x_ref, p_ref, n_ref, acc_ref):
        t = pl.program_id(2)
        prev, nxt = _seg_halos(p_ref, n_ref, t, nct, nt)
        x_ = x_ref[0]
        dy_ = dy_ref[0]
        row = lax.broadcasted_iota(jnp.int32, x_.shape, 0)

        @pl.when(t == 0)
        def _():
            acc_ref[...] = jnp.zeros_like(acc_ref)

        for j, off in enumerate(offs):
            acc_ref[0, j:j + 1, :] += jnp.sum(dy_ * _shifted(x_, prev, nxt, off, row, tm), axis=0, keepdims=True)
        acc_ref[0, K:K + 1, :] += jnp.sum(dy_, axis=0, keepdims=True)

    pspec, nspec = _halo_specs(T, tm, tc, cb0, "bjt")
    return pl.pallas_call(
        body, name=name, grid=(B, C // tc, nt),
        in_specs=[pl.BlockSpec((1, tm, tc), lambda b, j, t: (b, t, j)),
                  pl.BlockSpec((1, tm, tc), lambda b, j, t: (b, t, cb0 + j)), pspec, nspec],
        out_specs=pl.BlockSpec((1, SUBLANES, tc), lambda b, j, t: (b, 0, j)),
        out_shape=jax.ShapeDtypeStruct((B, SUBLANES, C), F32),
        compiler_params=_cparams(("parallel", "parallel", "arbitrary"), VMEM_LIMIT),
    )(dy, x, x, x)


def _ab_act(pab, alog, dtb, nd, tm):
    R = pab.shape[0]

    def body(p_ref, al_ref, dt_ref, o_ref):
        p = p_ref[...]
        lane = lax.broadcasted_iota(jnp.int32, p.shape, 1)
        g = -jnp.exp(al_ref[...]) * _softplus(p + dt_ref[...])
        o_ref[...] = jnp.where(lane < nd, g, jnp.where(lane < 2 * nd, _sigmoid(p), 0.0))

    row = pl.BlockSpec((tm, LANES), lambda i: (i, 0))
    vec = pl.BlockSpec((1, LANES), lambda i: (0, 0))
    return pl.pallas_call(
        body, name="ab_act", grid=(R // tm,), in_specs=[row, vec, vec], out_specs=row,
        out_shape=jax.ShapeDtypeStruct((R, LANES), F32),
        compiler_params=_cparams(("parallel",), VMEM_LIMIT))(pab, alog, dtb)


def _ab_act_bwd(pab, gb, dgb, alog, dtb, nd, tm):
    R = pab.shape[0]

    def body(p_ref, gb_ref, d0_ref, d1_ref, al_ref, dt_ref, o_ref, acc_ref):
        i = pl.program_id(0)
        p = p_ref[...]
        gb_ = gb_ref[...]
        d_ = d0_ref[...] + d1_ref[...]
        lane = lax.broadcasted_iota(jnp.int32, p.shape, 1)
        is_g = lane < nd
        is_b = jnp.logical_and(lane >= nd, lane < 2 * nd)
        da = jnp.where(is_g, d_ * (-jnp.exp(al_ref[...])) * _sigmoid(p + dt_ref[...]), 0.0)
        db = jnp.where(is_b, d_ * gb_ * (1.0 - gb_), 0.0)
        o_ref[...] = (da + db).astype(BF16)

        @pl.when(i == 0)
        def _():
            acc_ref[...] = jnp.zeros_like(acc_ref)

        acc_ref[0:1, :] += jnp.sum(jnp.where(is_g, d_ * gb_, 0.0), axis=0, keepdims=True)
        acc_ref[1:2, :] += jnp.sum(da, axis=0, keepdims=True)

    row = pl.BlockSpec((tm, LANES), lambda i: (i, 0))
    vec = pl.BlockSpec((1, LANES), lambda i: (0, 0))
    return pl.pallas_call(
        body, name="ab_act_bwd", grid=(R // tm,),
        in_specs=[row, row, row, row, vec, vec],
        out_specs=(row, pl.BlockSpec((SUBLANES, LANES), lambda i: (0, 0))),
        out_shape=(jax.ShapeDtypeStruct((R, LANES), BF16), jax.ShapeDtypeStruct((SUBLANES, LANES), F32)),
        compiler_params=_cparams(("arbitrary",), VMEM_LIMIT))(pab, gb, dgb[0], dgb[1], alog, dtb)


def _dn_act_bwd(cv, dqkv, H, hd, tm):
    B, T, C3 = cv.shape
    qscale = float(hd) ** -0.5

    def body(c_ref, d0_ref, d1_ref, o_ref):
        part = pl.program_id(0)
        for h in range(H):
            sl = slice(h * hd, (h + 1) * hd)
            c = c_ref[0, :, sl]
            s = _silu(c)
            dout = d0_ref[0, :, sl] + d1_ref[0, :, sl]
            rs = lax.rsqrt(jnp.sum(s * s, axis=-1, keepdims=True) + EPS)
            sh = s * rs
            dnorm = rs * (dout - sh * jnp.sum(dout * sh, axis=-1, keepdims=True))
            ds = jnp.where(part == 0, dnorm * qscale, jnp.where(part == 1, dnorm, dout))
            o_ref[0, :, sl] = ds * _dsilu(c)

    spec = pl.BlockSpec((1, tm, H * hd), lambda p, b, t: (b, t, p))
    return pl.pallas_call(
        body, name="dn_act_bwd", grid=(3, B, T // tm), in_specs=[spec, spec, spec], out_specs=spec,
        out_shape=jax.ShapeDtypeStruct((B, T, C3), F32),
        compiler_params=_cparams(("parallel",) * 3, VMEM_LIMIT))(cv, dqkv[0], dqkv[1])


def _chunk_of(d, s, ncc, nch):
    if d == 0:
        return s
    return jnp.where(s < ncc, ncc - 1 - s, nch - 1 - (s - ncc))


def _delta_masks(d):
    row = lax.broadcasted_iota(jnp.int32, (CHUNK, CHUNK), 0)
    col = lax.broadcasted_iota(jnp.int32, (CHUNK, CHUNK), 1)
    sign = 1 - 2 * d
    diff = (row - col) * sign
    return diff >= 0, diff > 0, diff <= 0


def _each(f, *lists):
    return [f(*a) for a in zip(*lists)]


def _unit_tri_inverse(As):
    C = As[0].shape[0]
    row = lax.broadcasted_iota(jnp.int32, (C, C), 0)
    col = lax.broadcasted_iota(jnp.int32, (C, C), 1)
    eye = jnp.where(row == col, 1.0, 0.0).astype(F32)
    same = lambda shift: jnp.right_shift(row, shift) == jnp.right_shift(col, shift)
    in8 = same(3)
    xs = [-jnp.where(in8, a, 0.0) for a in As]
    ts = [eye + x for x in xs]
    ps = _each(lambda x: _dot(x, x), xs)
    tp = _each(lambda t, p: _dot(_rows(t, p), p), ts, ps)
    ts = _each(lambda t, m: t + m[:C], ts, tp)
    ts = _each(lambda t, m: t + _dot(t, m[C:]), ts, tp)
    shift = 3
    while (1 << shift) < C:
        off = jnp.logical_and(same(shift + 1), jnp.right_shift(row, shift) != jnp.right_shift(col, shift))
        los = [jnp.where(off, a, 0.0) for a in As]
        ts = _each(lambda t, lo: t - _dot(t, _dot(lo, t)), ts, los)
        shift += 1
    def residual(a, t):
        (ah, al), (th, tl) = _split2(eye + a), _split2(t)
        m = _dot(_rows(ah, al), th)
        return eye - (m[:C] + (m[C:] + _dot(ah, tl)))

    rs = _each(residual, As, ts)
    return _each(lambda t, r: t + _dot(t, r), ts, rs)


def _rows(*xs):
    return jnp.concatenate(xs, axis=0)


def _cols(*xs):
    return jnp.concatenate(xs, axis=1)


def _delta_chunk_fwd(q, k, v, g_i, g_j, beta_i, gl, S, incl, strict):
    D_ = _each(lambda gi, gj, m: jnp.where(m, jnp.exp(jnp.where(m, gi - gj, 0.0)), 0.0), g_i, g_j, incl)
    C, dk = k[0].shape
    kb = _each(lambda k_, b: k_ * b, k, beta_i)
    kq = _each(lambda kb_, q_, k_: _dot(_rows(kb_, q_), k_, "nt"), kb, q, k)
    A = _each(lambda m_, d, m: jnp.where(m, m_[:C] * d, 0.0), kq, D_, strict)
    P = _each(lambda m_, d, m: jnp.where(m, m_[C:] * d, 0.0), kq, D_, incl)
    Tm = _unit_tri_inverse(A)
    gam = _each(jnp.exp, g_i)
    wu = _each(lambda t, kb_, g, v_, b: _dot(t, _cols(kb_ * g, v_ * b)), Tm, kb, gam, v, beta_i)
    w = [m_[:, :dk] for m_ in wu]
    u = [m_[:, dk:] for m_ in wu]
    qg = _each(lambda q_, g: q_ * g, q, gam)
    ws = _each(lambda w_, qg_, s: _dot(_rows(w_, qg_), s), w, qg, S)
    vn = _each(lambda u_, m_: u_ - m_[:C], u, ws)
    o = _each(lambda m_, p, vn_: m_[C:] + _dot(p, vn_), ws, P, vn)
    e_i = _each(lambda gl_, gi: jnp.exp(gl_ - gi), gl, g_i)
    kd = _each(lambda k_, e: k_ * e, k, e_i)
    Gam = _each(jnp.exp, gl)
    S_new = _each(lambda s, g, kd_, vn_: s * g + _dot(kd_, vn_, "tn"), S, Gam, kd, vn)
    return dict(D=D_, kb=kb, A=A, Tm=Tm, gam=gam, w=w, u=u, vn=vn, P=P, qg=qg, o=o, e=e_i, kd=kd, Gam=Gam, S_new=S_new)


def _delta_gb(gb_ref, d, incl, incl_t, H):
    gb = gb_ref[0]
    g = gb[:, d * H:(d + 1) * H]
    beta = gb[:, (2 + d) * H:(3 + d) * H]
    gc_col = _dot_mask(incl.astype(BF16), g)
    gc_row = _dot_mask(incl_t.astype(BF16), g, "xtn")
    gl = jnp.sum(g, axis=0, keepdims=True)
    return beta, gc_col, gc_row, gl


def _delta_fwd(qkvn, gb, H, hd, ncc):
    B, T, _ = qkvn.shape
    nch = T // CHUNK
    HD = H * hd
    pairs = [(d, h) for d in range(2) for h in range(H)]

    def body(q0, k0, v0, gb0, q1, k1, v1, gb1, o0, o1, sin0, sin1, S_ref):
        s = pl.program_id(1)
        q_refs, k_refs, v_refs, gb_refs = (q0, q1), (k0, k1), (v0, v1), (gb0, gb1)
        o_refs, sin_refs = (o0, o1), (sin0, sin1)

        @pl.when(s == 0)
        def _():
            S_ref[...] = jnp.zeros_like(S_ref)

        masks = [_delta_masks(d) for d in range(2)]
        gbs = [_delta_gb(gb_refs[d], d, masks[d][0], masks[d][2], H) for d in range(2)]
        head = lambda ref, h: ref[0, :, h * hd:(h + 1) * hd]
        S = [S_ref[d, h] for d, h in pairs]
        r = _delta_chunk_fwd([head(q_refs[d], h) for d, h in pairs], [head(k_refs[d], h) for d, h in pairs],
                             [head(v_refs[d], h) for d, h in pairs],
                             [gbs[d][1][:, h:h + 1] for d, h in pairs], [gbs[d][2][h:h + 1, :] for d, h in pairs],
                             [gbs[d][0][:, h:h + 1] for d, h in pairs], [gbs[d][3][:, h:h + 1] for d, h in pairs],
                             S, [masks[d][0] for d, h in pairs], [masks[d][1] for d, h in pairs])
        for i, (d, h) in enumerate(pairs):
            sin_refs[d][0, 0, h] = S[i]
            S_ref[d, h] = r["S_new"][i]
            o_refs[d][0, :, h * hd:(h + 1) * hd] = r["o"][i]

    def dir_specs(d):
        cidx = lambda b, s: _chunk_of(d, s, ncc, nch)
        ins = [pl.BlockSpec((1, CHUNK, HD), lambda b, s, p=p: (b, cidx(b, s), p)) for p in range(3)]
        ins.append(pl.BlockSpec((1, CHUNK, LANES), lambda b, s: (b, cidx(b, s), 0)))
        return (ins, pl.BlockSpec((1, CHUNK, HD), lambda b, s: (b, cidx(b, s), 0)),
                pl.BlockSpec((1, 1, H, hd, hd), lambda b, s: (b, cidx(b, s), 0, 0, 0)))

    (in0, o_s0, sin_s0), (in1, o_s1, sin_s1) = dir_specs(0), dir_specs(1)
    o_shape = jax.ShapeDtypeStruct((B, T, HD), F32)
    sin_shape = jax.ShapeDtypeStruct((B, nch, H, hd, hd), F32)
    return pl.pallas_call(
        body, name="delta_fwd", grid=(B, nch),
        in_specs=in0 + in1, out_specs=(o_s0, o_s1, sin_s0, sin_s1),
        out_shape=(o_shape, o_shape, sin_shape, sin_shape),
        scratch_shapes=[pltpu.VMEM((2, H, hd, hd), F32)],
        compiler_params=_cparams(("parallel", "arbitrary"), VMEM_LIMIT),
    )(qkvn, qkvn, qkvn, gb, qkvn, qkvn, qkvn, gb)


def _delta_bwd(qkvn, gb, sin, do, H, hd, ncc):
    B, T, _ = qkvn.shape
    nch = T // CHUNK
    HD = H * hd
    pairs = [(d, h) for d in range(2) for h in range(H)]

    def body(q0, k0, v0, gb0, sin0, do0, q1, k1, v1, gb1, sin1, do1, dqkv0, dqkv1, dgb0, dgb1, dS_ref):
        s = pl.program_id(1)
        q_refs, k_refs, v_refs, gb_refs = (q0, q1), (k0, k1), (v0, v1), (gb0, gb1)
        sin_refs, do_refs, dqkv_refs, dgb_refs = (sin0, sin1), (do0, do1), (dqkv0, dqkv1), (dgb0, dgb1)

        @pl.when(s == 0)
        def _():
            dS_ref[...] = jnp.zeros_like(dS_ref)

        masks = [_delta_masks(d) for d in range(2)]
        gbs = [_delta_gb(gb_refs[d], d, masks[d][0], masks[d][2], H) for d in range(2)]
        incl = [masks[d][0] for d, h in pairs]
        strict = [masks[d][1] for d, h in pairs]
        lane = lax.broadcasted_iota(jnp.int32, (1, LANES), 1)
        ones = jnp.ones((3 * CHUNK, LANES), BF16)
        head = lambda ref, h: ref[0, :, h * hd:(h + 1) * hd]
        q = [head(q_refs[d], h) for d, h in pairs]
        k = [head(k_refs[d], h) for d, h in pairs]
        v = [head(v_refs[d], h) for d, h in pairs]
        do_ = [head(do_refs[d], h) for d, h in pairs]
        beta_i = [gbs[d][0][:, h:h + 1] for d, h in pairs]
        S = [sin_refs[d][0, 0, h] for d, h in pairs]
        dSn = [dS_ref[d, h] for d, h in pairs]
        f = _delta_chunk_fwd(q, k, v, [gbs[d][1][:, h:h + 1] for d, h in pairs], [gbs[d][2][h:h + 1, :] for d, h in pairs],
                             beta_i, [gbs[d][3][:, h:h + 1] for d, h in pairs], S, incl, strict)
        rsum = lambda x: jnp.sum(x, axis=1, keepdims=True)
        dvn = _each(lambda p, d_, kd, ds: _dot(p, d_, "tn") + _dot(kd, ds), f["P"], do_, f["kd"], dSn)
        dP = _each(lambda d_, vn, m: jnp.where(m, _dot(d_, vn, "nt"), 0.0), do_, f["vn"], incl)
        dqg = _each(lambda d_, s: _dot(d_, s, "nt"), do_, S)
        dS_in = _each(lambda qg, d_, g, ds, w, dv_: _dot(qg, d_, "tn") + g * ds - _dot(w, dv_, "tn"),
                      f["qg"], do_, f["Gam"], dSn, f["w"], dvn)
        dGam = _each(lambda s, ds: jnp.sum(rsum(s * ds), axis=0, keepdims=True), S, dSn)
        dkd = _each(lambda vn, ds: _dot(vn, ds, "nt"), f["vn"], dSn)
        de = _each(lambda dkd_, k_, e: rsum(dkd_ * k_) * e, dkd, k, f["e"])
        dw = _each(lambda dv_, s: -_dot(dv_, s, "nt"), dvn, S)
        dXw = _each(lambda t, dw_: _dot(t, dw_, "tn"), f["Tm"], dw)
        dXu = _each(lambda t, dv_: _dot(t, dv_, "tn"), f["Tm"], dvn)
        dA = _each(lambda xw, w, xu, u, m: -jnp.where(m, _dot(xw, w, "nt") + _dot(xu, u, "nt"), 0.0),
                   dXw, f["w"], dXu, f["u"], strict)
        dM = _each(lambda a, d_: a * d_, dA, f["D"])
        dN = _each(lambda p, d_: p * d_, dP, f["D"])
        dkb = _each(lambda m, k_, xw, g: _dot(m, k_) + xw * g, dM, k, dXw, f["gam"])
        dq = _each(lambda dqg_, g, n, k_: dqg_ * g + _dot(n, k_), dqg, f["gam"], dN, k)
        dk = _each(lambda dkd_, e, m, kb, n, q_, dkb_, b: dkd_ * e + _dot(m, kb, "tn") + _dot(n, q_, "tn") + dkb_ * b,
                   dkd, f["e"], dM, f["kb"], dN, q, dkb, beta_i)
        dv = _each(lambda xu, b: xu * b, dXu, beta_i)
        dbeta = _each(lambda dkb_, k_, xu, v_: rsum(dkb_ * k_) + rsum(xu * v_), dkb, k, dXu, v)
        E = _each(lambda a, A_, p, P_: a * A_ + p * P_, dA, f["A"], dP, f["P"])

        def colsum(e):
            return _dot(_rows(*_split3(e)), ones, "tn")[:, 0:1]

        dg = _each(lambda e, dqg_, qg, xw, kb, g, de_: rsum(e) - colsum(e) + rsum(dqg_ * qg) + rsum(xw * kb) * g - de_,
                   E, dqg, f["qg"], dXw, f["kb"], f["gam"], de)
        dgl_h = _each(lambda de_, dg_, g: jnp.sum(de_, axis=0, keepdims=True) + dg_ * g, de, dGam, f["Gam"])
        for d in range(2):
            dgc = jnp.zeros((CHUNK, LANES), F32)
            dgl = jnp.zeros((1, LANES), F32)
            for h in range(H):
                i = d * H + h
                g_lane, b_lane = d * H + h, (2 + d) * H + h
                dgc = dgc + dg[i] * (lane == g_lane).astype(F32) + dbeta[i] * (lane == b_lane).astype(F32)
                dgl = dgl + dgl_h[i] * (lane == g_lane).astype(F32)
                dS_ref[d, h] = dS_in[i]
                dqkv_refs[d][0, :, h * hd:(h + 1) * hd] = dq[i]
                dqkv_refs[d][0, :, HD + h * hd:HD + (h + 1) * hd] = dk[i]
                dqkv_refs[d][0, :, 2 * HD + h * hd:2 * HD + (h + 1) * hd] = dv[i]
            draw = _dot_mask(masks[d][0].astype(BF16), dgc, "tn") + dgl
            dgb_refs[d][0] = jnp.where(lane < 2 * H, draw, dgc)

    def dir_specs(d):
        cidx = lambda b, s: _chunk_of(d, nch - 1 - s, ncc, nch)
        ins = [pl.BlockSpec((1, CHUNK, HD), lambda b, s, p=p: (b, cidx(b, s), p)) for p in range(3)]
        ins += [pl.BlockSpec((1, CHUNK, LANES), lambda b, s: (b, cidx(b, s), 0)),
                pl.BlockSpec((1, 1, H, hd, hd), lambda b, s: (b, cidx(b, s), 0, 0, 0)),
                pl.BlockSpec((1, CHUNK, HD), lambda b, s: (b, cidx(b, s), 0))]
        return (ins, pl.BlockSpec((1, CHUNK, 3 * HD), lambda b, s: (b, cidx(b, s), 0)),
                pl.BlockSpec((1, CHUNK, LANES), lambda b, s: (b, cidx(b, s), 0)))

    (in0, dq_s0, dg_s0), (in1, dq_s1, dg_s1) = dir_specs(0), dir_specs(1)
    dq_shape = jax.ShapeDtypeStruct((B, T, 3 * HD), F32)
    dg_shape = jax.ShapeDtypeStruct((B, T, LANES), F32)
    return pl.pallas_call(
        body, name="delta_bwd", grid=(B, nch),
        in_specs=in0 + in1, out_specs=(dq_s0, dq_s1, dg_s0, dg_s1),
        out_shape=(dq_shape, dq_shape, dg_shape, dg_shape),
        scratch_shapes=[pltpu.VMEM((2, H, hd, hd), F32)],
        compiler_params=_cparams(("parallel", "arbitrary"), VMEM_LIMIT),
    )(qkvn, qkvn, qkvn, gb, sin[0], do, qkvn, qkvn, qkvn, gb, sin[1], do)


def _dn_out(o2, pz, zcb0, onorm, H, hd, nct, tm):
    B, T, HD = o2[0].shape
    N = T - nct * tm

    def body(o0_ref, o1_ref, z_ref, g_ref, y_ref):
        for h in range(H):
            sl = slice(h * hd, (h + 1) * hd)
            o = o0_ref[0, :, sl] + o1_ref[0, :, sl]
            r = lax.rsqrt(jnp.mean(o * o, axis=-1, keepdims=True) + EPS)
            y_ref[0, :, sl] = (o * r * g_ref[...] * _silu(z_ref[0, :, sl])).astype(BF16)

    return pl.pallas_call(
        body, name="dn_out", grid=(B, N // tm),
        in_specs=[pl.BlockSpec((1, tm, HD), lambda b, t: (b, t + nct, 0)),
                  pl.BlockSpec((1, tm, HD), lambda b, t: (b, t + nct, 0)),
                  pl.BlockSpec((1, tm, HD), lambda b, t: (b, t + nct, zcb0)),
                  pl.BlockSpec((1, hd), lambda b, t: (0, 0))],
        out_specs=pl.BlockSpec((1, tm, HD), lambda b, t: (b, t, 0)),
        out_shape=jax.ShapeDtypeStruct((B, N, HD), BF16),
        compiler_params=_cparams(("parallel",) * 2, VMEM_LIMIT))(o2[0], o2[1], pz, onorm)


def _dn_out_bwd(o2, pz, zcb0, onorm, dy, H, hd, nct, tm):
    B, T, HD = o2[0].shape
    nt = T // tm

    def body(o0_ref, o1_ref, z_ref, g_ref, dy_ref, do_ref, dz_ref, acc_ref):
        t = pl.program_id(1)

        @pl.when(t == 0)
        def _():
            acc_ref[...] = jnp.zeros_like(acc_ref)

        @pl.when(t < nct)
        def _():
            do_ref[0] = jnp.zeros_like(do_ref[0])
            dz_ref[0] = jnp.zeros_like(dz_ref[0])

        @pl.when(t >= nct)
        def _():
            g = g_ref[...]
            for h in range(H):
                sl = slice(h * hd, (h + 1) * hd)
                o = o0_ref[0, :, sl] + o1_ref[0, :, sl]
                z = z_ref[0, :, sl]
                dy_ = dy_ref[0, :, sl]
                r = lax.rsqrt(jnp.mean(o * o, axis=-1, keepdims=True) + EPS)
                oh = o * r
                dz_ref[0, :, sl] = (dy_ * oh * g * _dsilu(z)).astype(BF16)
                dn = dy_ * _silu(z)
                acc_ref[0, 0:1, :] += jnp.sum(dn * oh, axis=0, keepdims=True)
                doh = dn * g
                do_ref[0, :, sl] = r * (doh - oh * jnp.mean(doh * oh, axis=-1, keepdims=True))

    lat = lambda b, t: (b, jnp.maximum(t - nct, 0), 0)
    row = pl.BlockSpec((1, tm, HD), lambda b, t: (b, t, 0))
    return pl.pallas_call(
        body, name="dn_out_bwd", grid=(B, nt),
        in_specs=[row, row,
                  pl.BlockSpec((1, tm, HD), lambda b, t: (b, t, zcb0)),
                  pl.BlockSpec((1, hd), lambda b, t: (0, 0)),
                  pl.BlockSpec((1, tm, HD), lat)],
        out_specs=(row, row, pl.BlockSpec((1, SUBLANES, hd), lambda b, t: (b, 0, 0))),
        out_shape=(jax.ShapeDtypeStruct((B, T, HD), F32), jax.ShapeDtypeStruct((B, T, HD), BF16),
                   jax.ShapeDtypeStruct((B, SUBLANES, hd), F32)),
        compiler_params=_cparams(("parallel", "arbitrary"), VMEM_LIMIT),
    )(o2[0], o2[1], pz, onorm, dy)


def _lru_gate_math(x, wr, wi, br, bi, lam):
    r = _sigmoid(_dot(x, wr) + br)
    i = _sigmoid(_dot(x, wi) + bi)
    sp = _softplus(-lam)
    la = -LRU_C * r * sp
    a = jnp.exp(la)
    s = jnp.sqrt(-jnp.tanh(la) * (a * a + 1.0))
    return r, i, sp, a, s


def _lru_gates(xc, wbd, bias4, lam, tm, bw):
    B, T, W = xc.shape

    def body(x_ref, w_ref, b_ref, l_ref, a_ref, i_ref):
        x = x_ref[0]
        for d in range(2):
            _, i, _, a, s = _lru_gate_math(x, w_ref[2 * d, 0], w_ref[2 * d + 1, 0],
                                           b_ref[2 * d:2 * d + 1, :], b_ref[2 * d + 1:2 * d + 2, :], l_ref[d:d + 1, :])
            a_ref[d, 0] = a
            i_ref[d, 0] = s * (i * x)

    out = pl.BlockSpec((2, 1, tm, bw), lambda b, t, j: (0, b, t, j))
    return pl.pallas_call(
        body, name="lru_gates", grid=(B, T // tm, W // bw),
        in_specs=[pl.BlockSpec((1, tm, bw), lambda b, t, j: (b, t, j)),
                  pl.BlockSpec((4, 1, bw, bw), lambda b, t, j: (0, j, 0, 0)),
                  pl.BlockSpec((4, bw), lambda b, t, j: (0, j)),
                  pl.BlockSpec((2, bw), lambda b, t, j: (0, j))],
        out_specs=(out, out),
        out_shape=(jax.ShapeDtypeStruct((2, B, T, W), F32), jax.ShapeDtypeStruct((2, B, T, W), F32)),
        compiler_params=_cparams(("parallel",) * 3, VMEM_LIMIT))(xc, wbd, bias4, lam)


def _lru_gates_bwd(xc, wbd, bias4, lam, dinp, da, tm, bw):
    B, T, W = xc.shape
    nt = T // tm

    def body(x_ref, w_ref, b_ref, l_ref, di0_ref, di1_ref, da0_ref, da1_ref, dx_ref, dw_ref, acc_ref):
        di_refs, da_refs = (di0_ref, di1_ref), (da0_ref, da1_ref)
        b_ = pl.program_id(1)
        t = pl.program_id(2)

        @pl.when(jnp.logical_and(b_ == 0, t == 0))
        def _():
            dw_ref[...] = jnp.zeros_like(dw_ref)
            acc_ref[...] = jnp.zeros_like(acc_ref)

        x = x_ref[0]
        dx = jnp.zeros_like(x)
        for d in range(2):
            wr, wi = w_ref[2 * d, 0], w_ref[2 * d + 1, 0]
            lam_ = l_ref[d:d + 1, :]
            r, i, sp, a, s = _lru_gate_math(x, wr, wi, b_ref[2 * d:2 * d + 1, :], b_ref[2 * d + 1:2 * d + 2, :], lam_)
            dinp_ = di_refs[d][0]
            dix = dinp_ * s
            ds = dinp_ * i * x
            dla = da_refs[d][0] * a - ds * (a * a) / s
            dpr = dla * (-LRU_C * sp) * r * (1.0 - r)
            dpi = dix * x * i * (1.0 - i)
            dx = dx + dix * i + _dot(dpr, wr, "nt") + _dot(dpi, wi, "nt")
            dw_ref[2 * d, 0] += _dot(x, dpr, "tn")
            dw_ref[2 * d + 1, 0] += _dot(x, dpi, "tn")
            acc_ref[2 * d:2 * d + 1, :] += jnp.sum(dpr, axis=0, keepdims=True)
            acc_ref[2 * d + 1:2 * d + 2, :] += jnp.sum(dpi, axis=0, keepdims=True)
            acc_ref[4 + d:5 + d, :] += jnp.sum(dla * (-LRU_C * r), axis=0, keepdims=True) * (-_sigmoid(-lam_))
        dx_ref[0] = dx

    tile = pl.BlockSpec((1, tm, bw), lambda j, b, t: (b, t, j))
    return pl.pallas_call(
        body, name="lru_gates_bwd", grid=(W // bw, B, nt),
        in_specs=[pl.BlockSpec((1, tm, bw), lambda j, b, t: (b, t, j)),
                  pl.BlockSpec((4, 1, bw, bw), lambda j, b, t: (0, j, 0, 0)),
                  pl.BlockSpec((4, bw), lambda j, b, t: (0, j)),
                  pl.BlockSpec((2, bw), lambda j, b, t: (0, j)), tile, tile, tile, tile],
        out_specs=(pl.BlockSpec((1, tm, bw), lambda j, b, t: (b, t, j)),
                   pl.BlockSpec((4, 1, bw, bw), lambda j, b, t: (0, j, 0, 0)),
                   pl.BlockSpec((SUBLANES, bw), lambda j, b, t: (0, j))),
        out_shape=(jax.ShapeDtypeStruct((B, T, W), F32), jax.ShapeDtypeStruct(wbd.shape, F32),
                   jax.ShapeDtypeStruct((SUBLANES, W), F32)),
        compiler_params=_cparams(("parallel", "arbitrary", "arbitrary"), VMEM_LIMIT),
    )(xc, wbd, bias4, lam, dinp[0], dinp[1], da[0], da[1])


def _tile_scan(a, x, rev, tm):
    row = lax.broadcasted_iota(jnp.int32, a.shape, 0)
    s = 1
    while s < tm:
        if rev:
            a_sh, x_sh, ok = pltpu.roll(a, tm - s, 0), pltpu.roll(x, tm - s, 0), row < tm - s
        else:
            a_sh, x_sh, ok = pltpu.roll(a, s, 0), pltpu.roll(x, s, 0), row >= s
        x = jnp.where(ok, x + a * x_sh, x)
        a = jnp.where(ok, a * a_sh, a)
        s *= 2
    return a, x


def _scan_tile_of(s, rev, nct, nt):
    if not rev:
        return s
    return jnp.where(s < nct, nct - 1 - s, nt - 1 - (s - nct))


def _lru_scan(a2, x2, d, nct, tm, tc):
    _, B, T, W = a2.shape
    nt = T // tm
    rev = d == 1
    last = 0 if rev else tm - 1

    def body(a_ref, x_ref, h_ref, carry):
        s = pl.program_id(2)

        @pl.when(s == 0)
        def _():
            carry[...] = jnp.zeros_like(carry)

        A, X = _tile_scan(a_ref[0, 0], x_ref[0, 0], rev, tm)
        h = X + A * carry[0:1, :]
        h_ref[0] = h
        carry[...] = jnp.broadcast_to(h[last:last + 1, :], carry.shape)

    tidx = lambda s: _scan_tile_of(s, rev, nct, nt)
    spec = pl.BlockSpec((1, 1, tm, tc), lambda b, j, s: (d, b, tidx(s), j))
    return pl.pallas_call(
        body, name=f"lru_scan{d}", grid=(B, W // tc, nt),
        in_specs=[spec, spec], out_specs=pl.BlockSpec((1, tm, tc), lambda b, j, s: (b, tidx(s), j)),
        out_shape=jax.ShapeDtypeStruct((B, T, W), F32),
        scratch_shapes=[pltpu.VMEM((SUBLANES, tc), F32)],
        compiler_params=_cparams(("parallel", "parallel", "arbitrary"), VMEM_LIMIT),
    )(a2, x2)


def _lru_scan_bwd(a2, h, dh, d, nct, tm, tc):
    _, B, T, W = a2.shape
    nt = T // tm
    rev = d == 1
    first = tm - 1 if rev else 0
    r8, n8 = tm // SUBLANES, T // SUBLANES

    def body(a_ref, h_ref, hp_ref, dh_ref, lam_ref, da_ref, ca, cl):
        s = pl.program_id(2)

        @pl.when(s == 0)
        def _():
            ca[...] = jnp.zeros_like(ca)
            cl[...] = jnp.zeros_like(cl)

        a = a_ref[0, 0]
        row = lax.broadcasted_iota(jnp.int32, a.shape, 0)
        if rev:
            c = jnp.where(row == 0, ca[0:1, :], pltpu.roll(a, 1, 0))
        else:
            c = jnp.where(row == tm - 1, ca[0:1, :], pltpu.roll(a, tm - 1, 0))
        C, L = _tile_scan(c, dh_ref[0], not rev, tm)
        lam = L + C * cl[0:1, :]
        lam_ref[0] = lam
        hp = jnp.where(s == nt - 1, 0.0, hp_ref[0])
        if rev:
            hprev = jnp.where(row == tm - 1, hp[0:1, :], pltpu.roll(h_ref[0], tm - 1, 0))
        else:
            hprev = jnp.where(row == 0, hp[SUBLANES - 1:SUBLANES, :], pltpu.roll(h_ref[0], 1, 0))
        da_ref[0] = lam * hprev
        ca[...] = jnp.broadcast_to(a[first:first + 1, :], ca.shape)
        cl[...] = jnp.broadcast_to(lam[first:first + 1, :], cl.shape)

    tidx = lambda s: _scan_tile_of(nt - 1 - s, rev, nct, nt)

    def hp_idx(b, j, s):
        tt = tidx(s)
        if rev:
            blk = jnp.where(tt == nt - 1, 0, jnp.minimum((tt + 1) * r8, n8 - 1))
        else:
            blk = jnp.maximum(tt * r8 - 1, 0)
        return (b, blk, j)

    tile = pl.BlockSpec((1, tm, tc), lambda b, j, s: (b, tidx(s), j))
    return pl.pallas_call(
        body, name=f"lru_scan_bwd{d}", grid=(B, W // tc, nt),
        in_specs=[pl.BlockSpec((1, 1, tm, tc), lambda b, j, s: (d, b, tidx(s), j)), tile,
                  pl.BlockSpec((1, SUBLANES, tc), hp_idx), tile],
        out_specs=(tile, tile),
        out_shape=(jax.ShapeDtypeStruct((B, T, W), F32), jax.ShapeDtypeStruct((B, T, W), F32)),
        scratch_shapes=[pltpu.VMEM((SUBLANES, tc), F32), pltpu.VMEM((SUBLANES, tc), F32)],
        compiler_params=_cparams(("parallel", "parallel", "arbitrary"), VMEM_LIMIT),
    )(a2, h, h, dh)


def _lru_out(h0, h1, pyl, ycb0, nct, tm, tc):
    B, T, W = h0.shape
    N = T - nct * tm

    def body(h0_ref, h1_ref, y_ref, o_ref):
        o_ref[0] = ((h0_ref[0] + h1_ref[0]) * _gelu(y_ref[0])).astype(BF16)

    hs = pl.BlockSpec((1, tm, tc), lambda b, t, j: (b, t + nct, j))
    return pl.pallas_call(
        body, name="lru_out", grid=(B, N // tm, W // tc),
        in_specs=[hs, hs, pl.BlockSpec((1, tm, tc), lambda b, t, j: (b, t + nct, ycb0 + j))],
        out_specs=pl.BlockSpec((1, tm, tc), lambda b, t, j: (b, t, j)),
        out_shape=jax.ShapeDtypeStruct((B, N, W), BF16),
        compiler_params=_cparams(("parallel",) * 3, VMEM_LIMIT))(h0, h1, pyl)


def _lru_out_bwd(h0, h1, pyl, ycb0, dy, nct, tm, tc):
    B, T, W = h0.shape

    def body(h0_ref, h1_ref, y_ref, dy_ref, dh_ref, dyl_ref):
        t = pl.program_id(1)

        @pl.when(t < nct)
        def _():
            dh_ref[0] = jnp.zeros_like(dh_ref[0])
            dyl_ref[0] = jnp.zeros_like(dyl_ref[0])

        @pl.when(t >= nct)
        def _():
            y = y_ref[0]
            dy_ = dy_ref[0]
            dh_ref[0] = dy_ * _gelu(y)
            dyl_ref[0] = (dy_ * (h0_ref[0] + h1_ref[0]) * _dgelu(y)).astype(BF16)

    hs = pl.BlockSpec((1, tm, tc), lambda b, t, j: (b, t, j))
    return pl.pallas_call(
        body, name="lru_out_bwd", grid=(B, T // tm, W // tc),
        in_specs=[hs, hs, pl.BlockSpec((1, tm, tc), lambda b, t, j: (b, t, ycb0 + j)),
                  pl.BlockSpec((1, tm, tc), lambda b, t, j: (b, jnp.maximum(t - nct, 0), j))],
        out_specs=(hs, hs),
        out_shape=(jax.ShapeDtypeStruct((B, T, W), F32), jax.ShapeDtypeStruct((B, T, W), BF16)),
        compiler_params=_cparams(("parallel",) * 3, VMEM_LIMIT))(h0, h1, pyl, dy)


def _merge(bd, bl, pmg, bm, nct, tm):
    B, N, D = bd.shape

    def body(bd_ref, bl_ref, m0_ref, m1_ref, b_ref, o_ref):
        g0 = _sigmoid(m0_ref[0] + b_ref[:, 0:D])
        g1 = _sigmoid(m1_ref[0] + b_ref[:, D:2 * D])
        o_ref[0] = (g0 * bd_ref[0] + g1 * bl_ref[0]).astype(BF16)

    row = pl.BlockSpec((1, tm, D), lambda b, t: (b, t, 0))
    return pl.pallas_call(
        body, name="merge", grid=(B, N // tm),
        in_specs=[row, row, pl.BlockSpec((1, tm, D), lambda b, t: (b, t + nct, 0)),
                  pl.BlockSpec((1, tm, D), lambda b, t: (b, t + nct, 1)),
                  pl.BlockSpec((1, 2 * D), lambda b, t: (0, 0))],
        out_specs=row, out_shape=jax.ShapeDtypeStruct((B, N, D), BF16),
        compiler_params=_cparams(("parallel", "parallel"), VMEM_LIMIT))(bd, bl, pmg, pmg, bm)


def _merge_bwd(dmi, bd, bl, pmg, bm, nct, tm):
    B, N, D = bd.shape
    T = pmg.shape[1]

    def body(dm_ref, bd_ref, bl_ref, m0_ref, m1_ref, b_ref, dbd_ref, dbl_ref, dmg_ref, acc_ref):
        t = pl.program_id(1)

        @pl.when(t == 0)
        def _():
            acc_ref[...] = jnp.zeros_like(acc_ref)

        @pl.when(t < nct)
        def _():
            dmg_ref[0] = jnp.zeros_like(dmg_ref[0])

        @pl.when(t >= nct)
        def _():
            dm = dm_ref[0]
            g0 = _sigmoid(m0_ref[0] + b_ref[:, 0:D])
            g1 = _sigmoid(m1_ref[0] + b_ref[:, D:2 * D])
            dbd_ref[0] = (dm * g0).astype(BF16)
            dbl_ref[0] = (dm * g1).astype(BF16)
            d0 = dm * bd_ref[0] * g0 * (1.0 - g0)
            d1 = dm * bl_ref[0] * g1 * (1.0 - g1)
            dmg_ref[0, :, 0:D] = d0.astype(BF16)
            dmg_ref[0, :, D:2 * D] = d1.astype(BF16)
            acc_ref[0, 0:1, 0:D] += jnp.sum(d0, axis=0, keepdims=True)
            acc_ref[0, 0:1, D:2 * D] += jnp.sum(d1, axis=0, keepdims=True)

    lat = pl.BlockSpec((1, tm, D), lambda b, t: (b, jnp.maximum(t - nct, 0), 0))
    return pl.pallas_call(
        body, name="merge_bwd", grid=(B, T // tm),
        in_specs=[lat, lat, lat, pl.BlockSpec((1, tm, D), lambda b, t: (b, t, 0)),
                  pl.BlockSpec((1, tm, D), lambda b, t: (b, t, 1)),
                  pl.BlockSpec((1, 2 * D), lambda b, t: (0, 0))],
        out_specs=(lat, lat, pl.BlockSpec((1, tm, 2 * D), lambda b, t: (b, t, 0)),
                   pl.BlockSpec((1, SUBLANES, 2 * D), lambda b, t: (b, 0, 0))),
        out_shape=(jax.ShapeDtypeStruct((B, N, D), BF16), jax.ShapeDtypeStruct((B, N, D), BF16),
                   jax.ShapeDtypeStruct((B, T, 2 * D), BF16), jax.ShapeDtypeStruct((B, SUBLANES, 2 * D), F32)),
        compiler_params=_cparams(("parallel", "arbitrary"), VMEM_LIMIT))(dmi, bd, bl, pmg, pmg, bm)


def _grid_taps():
    return [((di + 1) * 3 + (dj + 1), di, dj) for di in (-1, 0, 1) for dj in (-1, 0, 1)]


def _grid_views(x, n):
    pos = lax.broadcasted_iota(jnp.int32, x.shape, 0)
    gc = jnp.bitwise_and(pos, GRID_W - 1)
    cols = {0: x,
            -1: jnp.where(gc >= 1, pltpu.roll(x, 1, 0), 0.0),
            1: jnp.where(gc <= GRID_W - 2, pltpu.roll(x, n - 1, 0), 0.0)}
    views = {}
    edge = jnp.zeros((GRID_W, x.shape[1]), x.dtype)
    for dj, xc in cols.items():
        views[(0, dj)] = xc
        views[(-1, dj)] = jnp.concatenate([edge, xc[:n - GRID_W]], axis=0)
        views[(1, dj)] = jnp.concatenate([xc[GRID_W:], edge], axis=0)
    return views


def _ffn_act(F, w9, b, tc):
    B, N, C2 = F.shape
    Cf = C2 // 2
    ncb = Cf // tc

    def body(g_ref, v_ref, w_ref, b_ref, o_ref):
        xv = _grid_views(g_ref[0], N)
        conv = b_ref[...]
        for k, di, dj in _grid_taps():
            conv = conv + xv[(di, dj)] * w_ref[k:k + 1, :]
        o_ref[0] = (_gelu(conv) * v_ref[0]).astype(BF16)

    return pl.pallas_call(
        body, name="ffn_act", grid=(B, ncb),
        in_specs=[pl.BlockSpec((1, N, tc), lambda b, j: (b, 0, j)),
                  pl.BlockSpec((1, N, tc), lambda b, j: (b, 0, ncb + j)),
                  pl.BlockSpec((9, tc), lambda b, j: (0, j)),
                  pl.BlockSpec((1, tc), lambda b, j: (0, j))],
        out_specs=pl.BlockSpec((1, N, tc), lambda b, j: (b, 0, j)),
        out_shape=jax.ShapeDtypeStruct((B, N, Cf), BF16),
        compiler_params=_cparams(("parallel", "parallel"), VMEM_LIMIT))(F, F, w9, b)


def _ffn_act_bwd(F, w9, b, df, tc):
    B, N, C2 = F.shape
    Cf = C2 // 2
    ncb = Cf // tc

    def body(g_ref, v_ref, w_ref, b_ref, df_ref, dF_ref, acc_ref, dv_s):
        p = pl.program_id(2)

        @pl.when(p == 0)
        def _():
            xv = _grid_views(g_ref[0], N)
            conv = b_ref[...]
            for k, di, dj in _grid_taps():
                conv = conv + xv[(di, dj)] * w_ref[k:k + 1, :]
            df_ = df_ref[0]
            th = jnp.tanh(GELU_C * (conv + 0.044715 * conv * conv * conv))
            dv_s[...] = (df_ * (0.5 * conv * (1.0 + th))).astype(BF16)
            dgel = 0.5 * (1.0 + th) + 0.5 * conv * (1.0 - th * th) * GELU_C * (1.0 + 3.0 * 0.044715 * conv * conv)
            dc = df_ * v_ref[0] * dgel
            dcv = _grid_views(dc, N)
            dx = None
            for k, di, dj in _grid_taps():
                term = dcv[(-di, -dj)] * w_ref[k:k + 1, :]
                dx = term if dx is None else dx + term
                acc_ref[0, k:k + 1, :] = jnp.sum(dc * xv[(di, dj)], axis=0, keepdims=True)
            acc_ref[0, 9:10, :] = jnp.sum(dc, axis=0, keepdims=True)
            acc_ref[0, 10:16, :] = jnp.zeros((6, tc), F32)
            dF_ref[0] = dx.astype(BF16)

        @pl.when(p == 1)
        def _():
            dF_ref[0] = dv_s[...]

    return pl.pallas_call(
        body, name="ffn_act_bwd", grid=(B, ncb, 2),
        in_specs=[pl.BlockSpec((1, N, tc), lambda b, j, p: (b, 0, j)),
                  pl.BlockSpec((1, N, tc), lambda b, j, p: (b, 0, ncb + j)),
                  pl.BlockSpec((9, tc), lambda b, j, p: (0, j)),
                  pl.BlockSpec((1, tc), lambda b, j, p: (0, j)),
                  pl.BlockSpec((1, N, tc), lambda b, j, p: (b, 0, j))],
        out_specs=(pl.BlockSpec((1, N, tc), lambda b, j, p: (b, 0, j + p * ncb)),
                   pl.BlockSpec((1, 16, tc), lambda b, j, p: (b, 0, j))),
        out_shape=(jax.ShapeDtypeStruct((B, N, C2), BF16), jax.ShapeDtypeStruct((B, 16, Cf), F32)),
        scratch_shapes=[pltpu.VMEM((N, tc), BF16)],
        compiler_params=_cparams(("parallel", "parallel", "arbitrary"), VMEM_LIMIT))(F, F, w9, b, df)


ADAM_ROWS = 512


def _adamw(w, m, v, gparts, name):
    rows, cols = w.shape[-2:]
    P = gparts.shape[0]
    tr = _row_tile(rows, (P + 14) * 4 * (-(-cols // LANES) * LANES))
    c1 = 1.0 - ADAM_B1 ** ADAM_STEP
    c2 = 1.0 - ADAM_B2 ** ADAM_STEP

    def body(w_ref, m_ref, v_ref, g_ref, go_ref, d_ref, mo_ref, vo_ref):
        g = g_ref[0].astype(F32)
        for p in range(1, P):
            g = g + g_ref[p].astype(F32)
        m_ = ADAM_B1 * m_ref[...] + (1.0 - ADAM_B1) * g
        v_ = ADAM_B2 * v_ref[...] + (1.0 - ADAM_B2) * (g * g)
        go_ref[...] = g
        mo_ref[...] = m_
        vo_ref[...] = v_
        d_ref[...] = -ADAM_LR * ((m_ / c1) / (jnp.sqrt(v_ / c2) + ADAM_EPS) + ADAM_WD * w_ref[...])

    if w.ndim == 3:
        row = pl.BlockSpec((None, tr, cols), lambda i: (0, i, 0))
    else:
        row = pl.BlockSpec((tr, cols), lambda i: (i, 0))
    out = jax.ShapeDtypeStruct(w.shape, F32)
    return pl.pallas_call(
        body, name=name, grid=(rows // tr,),
        in_specs=[row, row, row, pl.BlockSpec((P, tr, cols), lambda i: (0, i, 0))],
        out_specs=(row, row, row, row), out_shape=(out, out, out, out),
        compiler_params=_cparams(("parallel",), VMEM_LIMIT))(w, m, v, gparts)


def _pack_rows(flat_len):
    rows = -(-flat_len // LANES)
    if rows > ADAM_ROWS:
        rows = -(-rows // ADAM_ROWS) * ADAM_ROWS
    else:
        rows = -(-rows // SUBLANES) * SUBLANES
    return rows


PACK_ALIGN = SUBLANES * LANES


def _pack(arrs, lead=()):
    pads = [(0, 0)] * len(lead)
    parts = []
    for a in arrs:
        f = a.reshape(lead + (-1,)).astype(F32)
        parts.append(jnp.pad(f, pads + [(0, -f.shape[-1] % PACK_ALIGN)]))
    flat = jnp.concatenate(parts, axis=-1)
    n = flat.shape[-1]
    rows = _pack_rows(n)
    flat = jnp.pad(flat, pads + [(0, rows * LANES - n)])
    return flat.reshape(lead + (rows, LANES))


def _unpack(buf, shapes):
    out, r = [], 0
    for s in shapes:
        n = math.prod(s)
        nr = -(-n // PACK_ALIGN) * SUBLANES
        out.append(buf[r:r + nr].reshape(-1)[:n].reshape(s))
        r += nr
    return out


def _col_shards(full, n_lead):
    C = full.shape[-1]
    x = full.reshape(full.shape[:-1] + (N_DEV, C // N_DEV))
    return jnp.moveaxis(x, -2, 0)


def _from_col_shards(g):
    x = jnp.moveaxis(g, 0, -2)
    return x.reshape(x.shape[:-2] + (x.shape[-2] * x.shape[-1],))


def kernel(x, c, ctx, c_ctx, w_ada, b_ada, g_pre_mix, g_post_mix, g_pre_ffn, g_post_ffn, w_in, b_merge, dn_conv, dn_a_log, dn_dt_bias, dn_onorm, lru_conv, lru_conv_b, lru_w_rg, lru_b_rg, lru_w_ig, lru_b_ig, lru_lambda, w_branch_dn, w_branch_lru, w_out, w_up, ffn_dw, ffn_dw_b, w_down, loss_target, m_c_ctx, m_w_ada, m_b_ada, m_g_pre_mix, m_g_post_mix, m_g_pre_ffn, m_g_post_ffn, m_w_in, m_b_merge, m_dn_conv, m_dn_a_log, m_dn_dt_bias, m_dn_onorm, m_lru_conv, m_lru_conv_b, m_lru_w_rg, m_lru_b_rg, m_lru_w_ig, m_lru_b_ig, m_lru_lambda, m_w_branch_dn, m_w_branch_lru, m_w_out, m_w_up, m_ffn_dw, m_ffn_dw_b, m_w_down, v_c_ctx, v_w_ada, v_b_ada, v_g_pre_mix, v_g_post_mix, v_g_pre_ffn, v_g_post_ffn, v_w_in, v_b_merge, v_dn_conv, v_dn_a_log, v_dn_dt_bias, v_dn_onorm, v_lru_conv, v_lru_conv_b, v_lru_w_rg, v_lru_b_rg, v_lru_w_ig, v_lru_b_ig, v_lru_lambda, v_w_branch_dn, v_w_branch_lru, v_w_out, v_w_up, v_ffn_dw, v_ffn_dw_b, v_w_down):
    params = dict(c_ctx=c_ctx, w_ada=w_ada, b_ada=b_ada, g_pre_mix=g_pre_mix, g_post_mix=g_post_mix, g_pre_ffn=g_pre_ffn, g_post_ffn=g_post_ffn, w_in=w_in, b_merge=b_merge, dn_conv=dn_conv, dn_a_log=dn_a_log, dn_dt_bias=dn_dt_bias, dn_onorm=dn_onorm, lru_conv=lru_conv, lru_conv_b=lru_conv_b, lru_w_rg=lru_w_rg, lru_b_rg=lru_b_rg, lru_w_ig=lru_w_ig, lru_b_ig=lru_b_ig, lru_lambda=lru_lambda, w_branch_dn=w_branch_dn, w_branch_lru=w_branch_lru, w_out=w_out, w_up=w_up, ffn_dw=ffn_dw, ffn_dw_b=ffn_dw_b, w_down=w_down)
    mom1 = dict(c_ctx=m_c_ctx, w_ada=m_w_ada, b_ada=m_b_ada, g_pre_mix=m_g_pre_mix, g_post_mix=m_g_post_mix, g_pre_ffn=m_g_pre_ffn, g_post_ffn=m_g_post_ffn, w_in=m_w_in, b_merge=m_b_merge, dn_conv=m_dn_conv, dn_a_log=m_dn_a_log, dn_dt_bias=m_dn_dt_bias, dn_onorm=m_dn_onorm, lru_conv=m_lru_conv, lru_conv_b=m_lru_conv_b, lru_w_rg=m_lru_w_rg, lru_b_rg=m_lru_b_rg, lru_w_ig=m_lru_w_ig, lru_b_ig=m_lru_b_ig, lru_lambda=m_lru_lambda, w_branch_dn=m_w_branch_dn, w_branch_lru=m_w_branch_lru, w_out=m_w_out, w_up=m_w_up, ffn_dw=m_ffn_dw, ffn_dw_b=m_ffn_dw_b, w_down=m_w_down)
    mom2 = dict(c_ctx=v_c_ctx, w_ada=v_w_ada, b_ada=v_b_ada, g_pre_mix=v_g_pre_mix, g_post_mix=v_g_post_mix, g_pre_ffn=v_g_pre_ffn, g_post_ffn=v_g_post_ffn, w_in=v_w_in, b_merge=v_b_merge, dn_conv=v_dn_conv, dn_a_log=v_dn_a_log, dn_dt_bias=v_dn_dt_bias, dn_onorm=v_dn_onorm, lru_conv=v_lru_conv, lru_conv_b=v_lru_conv_b, lru_w_rg=v_lru_w_rg, lru_b_rg=v_lru_b_rg, lru_w_ig=v_lru_w_ig, lru_b_ig=v_lru_b_ig, lru_lambda=v_lru_lambda, w_branch_dn=v_w_branch_dn, w_branch_lru=v_w_branch_lru, w_out=v_w_out, w_up=v_w_up, ffn_dw=v_ffn_dw, ffn_dw_b=v_ffn_dw_b, w_down=v_w_down)
    names = list(params)

    B, N, D = x.shape
    NC = ctx.shape[1]
    T = NC + N
    H, hd = dn_a_log.shape[2], dn_onorm.shape[1]
    HD = H * hd
    nd = 2 * H
    W = lru_conv_b.shape[1]
    nblk, bdim = lru_w_rg.shape[2], lru_w_rg.shape[3]
    Cf = ffn_dw_b.shape[1]
    sw = w_ada.shape[2]
    tm = min(256, NC)
    nct = NC // tm
    ncc = NC // CHUNK
    nch = T // CHUNK
    R, RL = B * T, B * N
    bw = min(256, W)
    me = _my_index()
    assert NC % tm == 0 and N % tm == 0 and B + 1 <= SUBLANES and bw % bdim == 0 and D % LANES == 0

    cpad = jnp.zeros((SUBLANES, D), F32).at[:B].set(c).at[B].set(c_ctx)
    ccp = _all_gather([cpad], "ag_cond")[0].reshape(N_DEV * SUBLANES, D)
    mods_sh = _ada_fwd(ccp, w_ada[0], lax.dynamic_slice(b_ada, (0, me * sw), (1, sw)))
    lru_vecs = jnp.concatenate([lru_b_rg[0], lru_b_ig[0], lru_lambda[0], jnp.zeros_like(lru_lambda[0])], axis=0)
    mods_g, w_in_g, dn_conv_g, lru_conv_g, lru_vecs_g = _all_gather(
        [mods_sh, w_in[0].astype(BF16).T, dn_conv[0], lru_conv[0], lru_vecs], "ag_first")
    ag_rest = _split_start(
        [w_up[0].astype(BF16), w_down[0].astype(BF16), w_branch_dn[0].astype(BF16), w_branch_lru[0].astype(BF16),
         w_out[0].astype(BF16), ffn_dw[0].reshape(9, -1)], "gather", "ag_rest_start")
    mine = lax.dynamic_slice(mods_g, (0, me * SUBLANES, 0), (N_DEV, SUBLANES, sw))
    mods = jnp.moveaxis(mine, 0, 1).reshape(SUBLANES, 6, D)[:B + 1]
    mods = jnp.pad(mods, ((0, 0), (0, SUBLANES - 6), (0, 0))) + ag_rest[4][0, 0]
    dn_conv_f = _from_col_shards(dn_conv_g)
    lru_conv_f = _from_col_shards(lru_conv_g)
    lru_vecs_f = _from_col_shards(lru_vecs_g)
    lru_lam_f = lru_vecs_f[4:6]

    csh = w_in_g.shape[1]
    w_in_t = w_in_g.reshape(N_DEV * csh, D)
    o_z, o_a, o_xl = 3 * HD, 4 * HD, 4 * HD + 2 * nd
    o_yl, o_mg = o_xl + W, o_xl + 2 * W
    w_qkv, w_z = w_in_t[:o_z], w_in_t[o_z:o_a]
    w_ab = jnp.pad(w_in_t[o_a:o_xl], ((0, LANES - 2 * nd), (0, 0)))
    w_xl, w_yl, w_mg = w_in_t[o_xl:o_yl], w_in_t[o_yl:o_mg], w_in_t[o_mg:]

    def blockdiag(wg):
        per = bw // bdim
        g5 = wg.reshape(2, W // bw, per, bdim, bdim)
        eye = jnp.eye(per, dtype=wg.dtype)
        return jnp.einsum("dtpij,pq->dtpiqj", g5, eye).reshape(2, W // bw, bw, bw)

    bd_rg, bd_ig = blockdiag(lru_w_rg[0]), blockdiag(lru_w_ig[0])
    wbd = jnp.stack([bd_rg[0], bd_ig[0], bd_rg[1], bd_ig[1]]).astype(BF16)
    bias4 = jnp.stack([lru_vecs_f[0], lru_vecs_f[2], lru_vecs_f[1], lru_vecs_f[3]])
    alog_v = jnp.pad(dn_a_log.reshape(1, nd), ((0, 0), (0, LANES - nd)))
    dtb_v = jnp.pad(dn_dt_bias.reshape(1, nd), ((0, 0), (0, LANES - nd)))

    h0 = jnp.concatenate([ctx, x], axis=1)
    u1 = _norm_mod_fwd(h0, g_pre_mix, mods, 0, 1, nct, tm, "norm_mod1")
    u1f = u1.reshape(R, D)
    p_qkv = _mm(u1f, w_qkv, "nt", "mm_qkv").reshape(B, T, 3 * HD)
    p_z = _mm(u1f, w_z, "nt", "mm_z").reshape(B, T, HD)
    p_ab = _mm(u1f, w_ab, "nt", "mm_ab")
    p_xl = _mm(u1f, w_xl, "nt", "mm_xl").reshape(B, T, W)
    p_yl = _mm(u1f, w_yl, "nt", "mm_yl").reshape(B, T, W)
    p_mg = _mm(u1f, w_mg, "nt", "mm_mg").reshape(B, T, 2 * D)

    conv_offs = (-2, -1, 0, 1)
    tcc = _tile(HD, 1024)
    gb = _ab_act(p_ab, alog_v, dtb_v, nd, tm)
    gb3 = gb.reshape(B, T, LANES)
    cv, qkvn = _dwconv(p_qkv, 0, 3 * HD, dn_conv_f, None, conv_offs, nct, tm, HD, "dn_conv", qkv_heads=(H, hd))
    o_d0, o_d1, s_in0, s_in1 = _delta_fwd(qkvn, gb3, H, hd, ncc)
    o2 = (o_d0, o_d1)
    y_dn = _dn_out(o2, p_z, 0, dn_onorm, H, hd, nct, tm)

    tcw = _tile(W, 1024)
    xc = _dwconv(p_xl, 0, W, lru_conv_f, lru_conv_b, conv_offs, nct, tm, tcw, "lru_conv")
    a2, inp2 = _lru_gates(xc, wbd, bias4, lru_lam_f, tm, bw)
    hd0 = _lru_scan(a2, inp2, 0, nct, tm, tcw)
    hd1 = _lru_scan(a2, inp2, 1, nct, tm, tcw)
    y_lru = _lru_out(hd0, hd1, p_yl, 0, nct, tm, tcw)

    rest_x, rest_land = _split_wait(ag_rest, "gather", y_lru, "ag_rest_wait")
    w_up_g, w_down_g, w_bdn_g, w_blru_g, w_out_g, ffn_dw_g = [
        lax.dynamic_update_slice(land, x_[None], (me,) + (0,) * x_.ndim) for land, x_ in zip(rest_land, rest_x)]
    w_down_f = w_down_g.reshape(Cf, D)
    w_bdn_f, w_blru_f, w_out_f = w_bdn_g.reshape(HD, D), w_blru_g.reshape(W, D), w_out_g.reshape(D, D)
    ffn_dw_f = _from_col_shards(ffn_dw_g)
    bd = _mm(y_dn.reshape(RL, HD), w_bdn_f, "nn", "mm_bdn").reshape(B, N, D)
    bl = _mm(y_lru.reshape(RL, W), w_blru_f, "nn", "mm_blru").reshape(B, N, D)
    mixin = _merge(bd, bl, p_mg, b_merge, nct, tm)
    mix = _mm(mixin.reshape(RL, D), w_out_f, "nn", "mm_out").reshape(B, N, D)
    h1 = _resid_norm_fwd(x, mix, g_post_mix, mods, 2, tm, "resid_norm1")
    u2 = _norm_mod_fwd(h1, g_pre_ffn, mods, 3, 4, 0, tm, "norm_mod2")
    Fp = _mm(u2.reshape(RL, D), w_up_g, "nn", "mm_up").reshape(B, N, 2 * Cf)
    tcf = LANES
    f = _ffn_act(Fp, ffn_dw_f, ffn_dw_b, tcf)
    dproj = _mm(f.reshape(RL, Cf), w_down_f, "nn", "mm_down").reshape(B, N, D)

    dd, dh2, acc_f = _final_fwd_bwd(h1, dproj, g_post_ffn, mods, 5, loss_target, tm)
    loss = lax.psum((0.5 / D) * jnp.sum(acc_f[:, 2, :]), MESH_AXES)
    ddf = dd.reshape(RL, D)
    df = _mm(ddf, w_down_f, "nt", "mm_down_dx").reshape(B, N, Cf)
    gw_down = _mm(f.reshape(RL, Cf), ddf, "tn", "mm_down_dw", out_dtype=BF16)
    dF, acc_ffn = _ffn_act_bwd(Fp, ffn_dw_f, ffn_dw_b, df, tcf)
    dFf = dF.reshape(RL, 2 * Cf)
    du2 = _mm(dFf, w_up_g, "nt", "mm_up_dx").reshape(B, N, D)
    gw_up = _mm(u2.reshape(RL, D), dFf, "tn", "mm_up_dw", out_dtype=BF16, out_shards=N_DEV)
    jm = 2 * lax.axis_index("x") + lax.axis_index("y")

    def pair_sums(parts8, tag):
        sib4 = _pair_exchange(parts8, "rs_pair_" + tag)
        return [_pair_sum(p8, s4, f"rs_pair_sum_{tag}{i}") for i, (p8, s4) in enumerate(zip(parts8, sib4))]

    def own_slab(land, x4):
        idx = (jm,) + (0,) * (x4.ndim - 1)
        return lax.dynamic_update_slice(land, lax.dynamic_slice(x4, idx, (1,) + x4.shape[1:]), idx)

    q_early = _split_start(pair_sums([gw_up, gw_down.reshape(N_DEV, Cf // N_DEV, D)], "ffn"), "quad", "rs_quad_ffn_start")
    mods_b = mods + q_early[4][0, 0]
    dh1, acc2x, _ = _norm_mod_bwd(h1, du2, g_pre_ffn, mods_b, 4, 0, tm, dh2, "norm_mod2_bwd")
    dmix, acc_r1 = _resid_norm_bwd(mix, dh1, g_post_mix, mods, 2, tm, "resid_norm1_bwd")
    dmixf = dmix.reshape(RL, D)
    dmixin = _mm(dmixf, w_out_f, "nt", "mm_out_dx").reshape(B, N, D)
    gw_out = _mm(mixin.reshape(RL, D), dmixf, "tn", "mm_out_dw", out_dtype=BF16)
    dbd, dbl, dmg, acc_mg = _merge_bwd(dmixin, bd, bl, p_mg, b_merge, nct, tm)
    dy_dn = _mm(dbd.reshape(RL, D), w_bdn_f, "nt", "mm_bdn_dx").reshape(B, N, HD)
    gw_bdn = _mm(y_dn.reshape(RL, HD), dbd.reshape(RL, D), "tn", "mm_bdn_dw", out_dtype=BF16)
    dy_lru = _mm(dbl.reshape(RL, D), w_blru_f, "nt", "mm_blru_dx").reshape(B, N, W)
    gw_blru = _mm(y_lru.reshape(RL, W), dbl.reshape(RL, D), "tn", "mm_blru_dw", out_dtype=BF16)

    dh, dyl = _lru_out_bwd(hd0, hd1, p_yl, 0, dy_lru, nct, tm, tcw)
    lam0, da0 = _lru_scan_bwd(a2, hd0, dh, 0, nct, tm, tcw)
    lam1, da1 = _lru_scan_bwd(a2, hd1, dh, 1, nct, tm, tcw)
    dxc, dwbd, acc_lru = _lru_gates_bwd(xc, wbd, bias4, lru_lam_f, (lam0, lam1), (da0, da1), tm, bw)
    dxl = _dwconv(dxc, 0, W, lru_conv_f, None, tuple(-o for o in conv_offs), nct, tm, tcw, "lru_conv_dx", BF16)
    acc_lc = _dwconv_wgrad(dxc, p_xl, 0, W, conv_offs, nct, tm, tcw, "lru_conv_dw")

    do, dz, acc_on = _dn_out_bwd(o2, p_z, 0, dn_onorm, dy_dn, H, hd, nct, tm)
    dqkvn0, dqkvn1, dgb0, dgb1 = _delta_bwd(qkvn, gb3, (s_in0, s_in1), do, H, hd, ncc)
    dcv = _dn_act_bwd(cv, (dqkvn0, dqkvn1), H, hd, tm)
    dqkv = _dwconv(dcv, 0, 3 * HD, dn_conv_f, None, tuple(-o for o in conv_offs), nct, tm, tcc, "dn_conv_dx", BF16)
    acc_dc = _dwconv_wgrad(dcv, p_qkv, 0, 3 * HD, conv_offs, nct, tm, tcc, "dn_conv_dw")
    dp_ab, acc_ab = _ab_act_bwd(p_ab, gb, (dgb0.reshape(R, LANES), dgb1.reshape(R, LANES)), alog_v, dtb_v, nd, tm)

    groups = [(dqkv.reshape(R, 3 * HD), w_qkv, "qkv"), (dz.reshape(R, HD), w_z, "z"), (dp_ab, w_ab, "ab"),
              (dxl.reshape(R, W), w_xl, "xl"), (dyl.reshape(R, W), w_yl, "yl"), (dmg.reshape(R, 2 * D), w_mg, "mg")]
    du1 = _mm_sum([(dg_, wg_) for dg_, wg_, _ in groups], "mm_in_dx")
    gw_groups = [_mm(dg_, u1f, "tn", "mm_in_dw_" + nm, out_dtype=BF16) for dg_, _, nm in groups]
    gw_groups[2] = gw_groups[2][:2 * nd]
    gw_in = jnp.concatenate(gw_groups, axis=0).reshape(N_DEV, csh, D)
    grad_x, acc1x, acc1c = _norm_mod_bwd(h0, du1.reshape(B, T, D), g_pre_mix, mods, 1, nct, tm, dh1, "norm_mod1_bwd")

    g_lru_conv = jnp.sum(acc_lc[:, :4], 0)
    g_dn_conv = jnp.sum(acc_dc[:, :4], 0)
    g_ffn_dw = jnp.sum(acc_ffn[:, :9], 0).reshape(3, 3, Cf)
    sm_names = ["dn_conv", "lru_conv", "lru_b_rg", "lru_b_ig", "lru_lambda", "ffn_dw"]
    sm_parts = [_col_shards(g_dn_conv, 1), _col_shards(g_lru_conv, 1),
                _col_shards(jnp.stack([acc_lru[0], acc_lru[2]]), 1), _col_shards(jnp.stack([acc_lru[1], acc_lru[3]]), 1),
                _col_shards(acc_lru[4:6], 1), _col_shards(g_ffn_dw, 2)]
    late8 = [gw_in, gw_bdn.reshape(N_DEV, HD // N_DEV, D), gw_blru.reshape(N_DEV, W // N_DEV, D),
             gw_out.reshape(N_DEV, D // N_DEV, D), _pack(sm_parts, lead=(N_DEV,))]
    q_late = _split_start(pair_sums(late8, "mix"), "quad", "rs_quad_mix_start")
    ffn_x, ffn_land = _split_wait(q_early, "quad", q_late[4], "rs_quad_ffn_wait")
    results = {}
    for n, x4, land in zip(["w_up", "w_down"], ffn_x, ffn_land):
        results[n] = list(_adamw(params[n], mom1[n], mom2[n], own_slab(land, x4), "adamw_" + n))

    zeros_d = jnp.zeros((B, D), F32)
    dm_rows = jnp.stack([acc1x[:, 0], acc1x[:, 1], acc_r1[:, 0], acc2x[:, 0], acc2x[:, 1], acc_f[:, 0]], axis=1)
    dm_ctx = jnp.stack([jnp.sum(acc1c[:, 0], 0), jnp.sum(acc1c[:, 1], 0)] + [zeros_d[0]] * 4, axis=0)[None]
    dm_pad = jnp.zeros((SUBLANES, 6 * D), F32).at[:B + 1].set(jnp.concatenate([dm_rows, dm_ctx], 0).reshape(B + 1, 6 * D))
    dm_g = _all_gather([dm_pad], "ag_dmods")[0]
    g_mine = lax.dynamic_slice(dm_g, (0, 0, me * sw), (N_DEV, SUBLANES, sw)).reshape(N_DEV * SUBLANES, sw)
    gw_ada, dcc = _ada_bwd(ccp, w_ada[0], g_mine, c_ctx.reshape(1, D), B)

    per = bw // bdim

    def diag_blocks(dwf):
        g6 = dwf.reshape(W // bw, per, bdim, per, bdim)
        return jnp.einsum("tpiqj,pq->tpij", g6, jnp.eye(per, dtype=F32)).reshape(nblk, bdim, bdim)

    g_rep = dict(
        c_ctx=dcc[0],
        g_pre_mix=jnp.sum(acc1x[:, 2] + acc1c[:, 2], 0)[None], g_post_mix=jnp.sum(acc_r1[:, 1], 0)[None],
        g_pre_ffn=jnp.sum(acc2x[:, 2], 0)[None], g_post_ffn=jnp.sum(acc_f[:, 1], 0)[None],
        b_merge=jnp.sum(acc_mg[:, 0], 0)[None],
        dn_a_log=acc_ab[0, :nd].reshape(1, 2, H), dn_dt_bias=acc_ab[1, :nd].reshape(1, 2, H),
        dn_onorm=jnp.sum(acc_on[:, 0], 0)[None],
        lru_conv_b=jnp.sum(acc_lc[:, 4], 0)[None],
        lru_w_rg=jnp.stack([diag_blocks(dwbd[0]), diag_blocks(dwbd[2])])[None],
        lru_w_ig=jnp.stack([diag_blocks(dwbd[1]), diag_blocks(dwbd[3])])[None],
        ffn_dw_b=jnp.sum(acc_ffn[:, 9], 0)[None])
    rep_names = list(g_rep)
    rep_parts = _all_gather([_pack([g_rep[n] for n in rep_names])], "ag_rep_grads")[0]
    rep_out = _adamw(_pack([params[n] for n in rep_names]), _pack([mom1[n] for n in rep_names]),
                     _pack([mom2[n] for n in rep_names]), rep_parts, "adamw_rep")
    for k, buf in enumerate(rep_out):
        for n, arr in zip(rep_names, _unpack(buf, [params[n].shape for n in rep_names])):
            results.setdefault(n, [None] * 4)[k] = arr

    ba_out = _adamw(_pack([b_ada]), _pack([m_b_ada]), _pack([v_b_ada]),
                    _pack([dm_g.reshape(N_DEV * SUBLANES, 6 * D)], lead=(N_DEV * SUBLANES,)), "adamw_b_ada")
    results["b_ada"] = [_unpack(buf, [b_ada.shape])[0] for buf in ba_out]

    wa_out = _adamw(w_ada, m_w_ada, v_w_ada, gw_ada[None], "adamw_w_ada")
    results["w_ada"] = list(wa_out)

    mix_x, mix_land = _split_wait(q_late, "quad", wa_out[0], "rs_quad_mix_wait")
    recv4 = [own_slab(land, x4) for land, x4 in zip(mix_land, mix_x)]
    for n, g4 in zip(["w_in", "w_branch_dn", "w_branch_lru", "w_out"], recv4[:-1]):
        if n == "w_in":
            g4 = jnp.swapaxes(g4, 1, 2)
        results[n] = list(_adamw(params[n], mom1[n], mom2[n], g4, "adamw_" + n))
    sm_out = _adamw(_pack([params[n] for n in sm_names]), _pack([mom1[n] for n in sm_names]),
                    _pack([mom2[n] for n in sm_names]), recv4[-1], "adamw_small")
    for k, buf in enumerate(sm_out):
        for n, arr in zip(sm_names, _unpack(buf, [params[n].shape for n in sm_names])):
            results.setdefault(n, [None] * 4)[k] = arr

    outs = [loss, grad_x]
    for k in range(4):
        outs += [results[n][k] for n in names]
    return tuple(outs)
```

```python
import functools
import math

import jax
import jax.numpy as jnp
from jax import lax
from jax.experimental import pallas as pl
from jax.experimental.pallas import tpu as pltpu

F32 = jnp.float32
BF16 = jnp.bfloat16

EPS = 1e-6
GRID_W = 64
GRID_SHIFT = 6
CHUNK = 64
LRU_C = 8.0
N_DEV = 8
ADAM_LR = 0.001
ADAM_B1 = 0.9
ADAM_B2 = 0.999
ADAM_EPS = 1e-08
ADAM_WD = 0.01
ADAM_STEP = 10

LANES = 128
SUBLANES = 8
VMEM_LIMIT = 48 * 1024 * 1024
MESH_AXES = ("x", "y", "c")
GELU_C = math.sqrt(2.0 / math.pi)


def _cparams(sem=None, vmem=None):
    kw = {}
    if sem is not None:
        kw["dimension_semantics"] = sem
    if vmem is not None:
        kw["vmem_limit_bytes"] = vmem
    return pltpu.CompilerParams(**kw)


def _tile(n, pref):
    if n <= pref:
        return n
    t = (pref // LANES) * LANES
    while n % t:
        t -= LANES
    return t


def _sigmoid(x):
    return 1.0 / (1.0 + jnp.exp(-x))


def _silu(x):
    return x * _sigmoid(x)


def _dsilu(x):
    s = _sigmoid(x)
    return s * (1.0 + x * (1.0 - s))


def _softplus(x):
    return jnp.maximum(x, 0.0) + jnp.log(1.0 + jnp.exp(-jnp.abs(x)))


def _gelu(x):
    return 0.5 * x * (1.0 + jnp.tanh(GELU_C * (x + 0.044715 * x * x * x)))


def _dgelu(x):
    t = jnp.tanh(GELU_C * (x + 0.044715 * x * x * x))
    return 0.5 * (1.0 + t) + 0.5 * x * (1.0 - t * t) * GELU_C * (1.0 + 3.0 * 0.044715 * x * x)


def _dot(a, b, dims="nn"):
    dn = {"nn": (((1,), (0,)), ((), ())),
          "nt": (((1,), (1,)), ((), ())),
          "tn": (((0,), (0,)), ((), ()))}[dims]
    return lax.dot_general(a.astype(BF16), b.astype(BF16), dn, preferred_element_type=F32)


def _split2(x):
    hi = x.astype(BF16)
    lo = (x - hi.astype(F32)).astype(BF16)
    return hi, lo


def _split3(x):
    h1 = x.astype(BF16)
    r1 = x - h1.astype(F32)
    h2 = r1.astype(BF16)
    h3 = (r1 - h2.astype(F32)).astype(BF16)
    return h1, h2, h3


def _dot_hi(a, b):
    ah, al = _split2(a)
    bh, bl = _split2(b)
    return _dot(ah, bh) + (_dot(ah, bl) + _dot(al, bh))


def _dot_mask(m, x, dims="nn"):
    h1, h2, h3 = _split3(x)
    if dims == "xtn":
        return _dot(h1, m, "tn") + (_dot(h2, m, "tn") + _dot(h3, m, "tn"))
    return _dot(m, h1, dims) + (_dot(m, h2, dims) + _dot(m, h3, dims))


def _my_index():
    return 4 * lax.axis_index("x") + 2 * lax.axis_index("y") + lax.axis_index("c")


def _hbm_call(body, xs, out_shapes, n_sems, name):
    any_spec = pl.BlockSpec(memory_space=pl.ANY)
    return pl.pallas_call(
        body, name=name, out_shape=tuple(out_shapes),
        in_specs=[any_spec] * len(xs), out_specs=tuple([any_spec] * len(out_shapes)),
        scratch_shapes=[pltpu.SemaphoreType.DMA((n_sems,)), pltpu.SemaphoreType.DMA((n_sems,)),
                        pltpu.SemaphoreType.DMA((len(xs),))],
    )(*xs)


def _all_gather(xs, name):
    n = len(xs)

    def body(*refs):
        x_refs, out_refs = refs[:n], refs[n:2 * n]
        send_sems, recv_sems, local_sems = refs[2 * n:]
        x_, y_, c_ = lax.axis_index("x"), lax.axis_index("y"), lax.axis_index("c")
        me, sibling = (x_, y_, c_), (x_, y_, 1 - c_)
        chips = [(1 - x_, y_), (x_, 1 - y_), (1 - x_, 1 - y_)]

        def slab(a, px, py, pc):
            return out_refs[a].at[4 * px + 2 * py + pc]

        def copy(a, k, block, to, src=None):
            return pltpu.make_async_remote_copy(
                src_ref=slab(a, *block) if src is None else src, dst_ref=slab(a, *block),
                send_sem=send_sems.at[7 * a + k], recv_sem=recv_sems.at[7 * a + k],
                device_id=to, device_id_type=pl.DeviceIdType.MESH)

        mine = [pltpu.make_async_copy(x_refs[a], slab(a, *me), local_sems.at[a]) for a in range(n)]
        for cp in mine:
            cp.start()
        sent = []
        for j, chip in enumerate(chips):
            sent += [copy(a, 1 + j, me, (*chip, c_), src=x_refs[a]) for a in range(n)]
        sent += [copy(a, 0, me, sibling, src=x_refs[a]) for a in range(n)]
        for cp in sent:
            cp.start()
        for j, chip in enumerate(chips):
            for a in range(n):
                copy(a, 1 + j, (*chip, c_), me).wait_recv()
                fwd = copy(a, 4 + j, (*chip, c_), sibling)
                fwd.start()
                sent.append(fwd)
        for a in range(n):
            copy(a, 0, sibling, me).wait_recv()
        for j, chip in enumerate(chips):
            for a in range(n):
                copy(a, 4 + j, (*chip, 1 - c_), me).wait_recv()
        for cp in sent:
            cp.wait_send()
        for cp in mine:
            cp.wait()

    outs = [jax.ShapeDtypeStruct((N_DEV,) + x.shape, x.dtype) for x in xs]
    return _hbm_call(body, xs, outs, 7 * n, name)


def _pair_exchange(xs, name):
    n = len(xs)

    def body(*refs):
        x_refs, out_refs = refs[:n], refs[n:2 * n]
        send_sems, recv_sems, _ = refs[2 * n:]
        x_, y_, c_ = lax.axis_index("x"), lax.axis_index("y"), lax.axis_index("c")
        copies = []
        for a in range(n):
            for j in range(4):
                copies.append(pltpu.make_async_remote_copy(
                    src_ref=x_refs[a].at[2 * j + (1 - c_)], dst_ref=out_refs[a].at[j],
                    send_sem=send_sems.at[4 * a + j], recv_sem=recv_sems.at[4 * a + j],
                    device_id=(x_, y_, 1 - c_), device_id_type=pl.DeviceIdType.MESH))
        for cp in copies:
            cp.start()
        for cp in copies:
            cp.wait()

    outs = [jax.ShapeDtypeStruct((4,) + x.shape[1:], x.dtype) for x in xs]
    return _hbm_call(body, xs, outs, 4 * n, name)


def _split_views(kind, x_ref, land_ref, k):
    x_, y_, c_ = lax.axis_index("x"), lax.axis_index("y"), lax.axis_index("c")
    if kind == "gather":
        px = 1 - x_ if (k >> 2) & 1 else x_
        py = 1 - y_ if (k >> 1) & 1 else y_
        pc = 1 - c_ if k & 1 else c_
        return x_ref, land_ref.at[4 * x_ + 2 * y_ + c_], land_ref.at[4 * px + 2 * py + pc], (px, py, pc)
    px = 1 - x_ if (k >> 1) & 1 else x_
    py = 1 - y_ if k & 1 else y_
    return x_ref.at[2 * px + py], land_ref.at[2 * x_ + y_], land_ref.at[2 * px + py], (px, py, c_)


def _split_start(xs, kind, name):
    n = len(xs)
    npeer = 7 if kind == "gather" else 3
    lands = [lax.empty(((N_DEV,) + x.shape) if kind == "gather" else x.shape, x.dtype) for x in xs]

    def body(*refs):
        x_refs, land_refs = refs[:n], refs[n:2 * n]
        send_sems, recv_sems, token = refs[2 * n], refs[2 * n + 1], refs[4 * n + 2]
        for k in range(1, npeer + 1):
            for a in range(n):
                src, dst, _, peer = _split_views(kind, x_refs[a], land_refs[a], k)
                pltpu.make_async_remote_copy(
                    src_ref=src, dst_ref=dst, send_sem=send_sems.at[npeer * a + k - 1],
                    recv_sem=recv_sems.at[npeer * a + k - 1],
                    device_id=peer, device_id_type=pl.DeviceIdType.MESH).start()
        token[...] = jnp.zeros_like(token)

    hbm = pl.BlockSpec(memory_space=pltpu.HBM)
    sem = pl.BlockSpec(memory_space=pltpu.SEMAPHORE)
    sems = pltpu.SemaphoreType.DMA((npeer * n,))
    out = pl.pallas_call(
        body, name=name,
        out_shape=(sems, sems, *[pltpu.HBM(a.shape, a.dtype) for a in list(xs) + lands],
                   jax.ShapeDtypeStruct((SUBLANES, LANES), F32)),
        in_specs=[hbm] * (2 * n),
        out_specs=(sem, sem, *[hbm] * (2 * n), pl.BlockSpec(memory_space=pltpu.VMEM)),
        input_output_aliases={i: 2 + i for i in range(2 * n)},
        compiler_params=pltpu.CompilerParams(has_side_effects=pltpu.SideEffectType.DATAFLOW_SIDE_EFFECTING),
    )(*[pltpu.with_memory_space_constraint(a, pltpu.HBM) for a in list(xs) + lands])
    return out[0], out[1], list(out[2:2 + n]), list(out[2 + n:2 + 2 * n]), out[-1]


def _split_wait(started, kind, after, name):
    send_sems_, recv_sems_, x_thru, land_thru, _ = started
    n = len(x_thru)
    npeer = 7 if kind == "gather" else 3

    def body(*refs):
        x_refs, land_refs = refs[:n], refs[n:2 * n]
        send_sems, recv_sems = refs[2 * n], refs[2 * n + 1]
        for k in range(1, npeer + 1):
            for a in range(n):
                src, _, landed, peer = _split_views(kind, x_refs[a], land_refs[a], k)
                cp = pltpu.make_async_remote_copy(
                    src_ref=src, dst_ref=landed, send_sem=send_sems.at[npeer * a + k - 1],
                    recv_sem=recv_sems.at[npeer * a + k - 1],
                    device_id=peer, device_id_type=pl.DeviceIdType.MESH)
                cp.wait_send()
                cp.wait_recv()

    hbm = pl.BlockSpec(memory_space=pltpu.HBM)
    sem = pl.BlockSpec(memory_space=pltpu.SEMAPHORE)
    out = pl.pallas_call(
        body, name=name,
        out_shape=tuple(pltpu.HBM(a.shape, a.dtype) for a in x_thru + land_thru),
        in_specs=[hbm] * (2 * n) + [sem, sem, pl.BlockSpec(memory_space=pl.ANY)],
        out_specs=tuple([hbm] * (2 * n)),
        input_output_aliases={i: i for i in range(2 * n)},
        compiler_params=pltpu.CompilerParams(has_side_effects=pltpu.SideEffectType.DATAFLOW_SIDE_EFFECTING),
    )(*x_thru, *land_thru, send_sems_, recv_sems_, after)
    return list(out[:n]), list(out[n:])


def _pair_sum(x8, r4, name):
    _, r, c = x8.shape
    tr = _row_tile(r, c * 12)

    def body(core_ref, x_ref, r_ref, o_ref):
        o_ref[...] = (x_ref[...].astype(F32) + r_ref[...].astype(F32)).astype(o_ref.dtype)

    core = lax.axis_index("c").astype(jnp.int32).reshape(1)
    return pl.pallas_call(
        body, name=name,
        grid_spec=pltpu.PrefetchScalarGridSpec(
            num_scalar_prefetch=1, grid=(4, r // tr),
            in_specs=[pl.BlockSpec((None, tr, c), lambda j, i, core_ref: (2 * j + core_ref[0], i, 0)),
                      pl.BlockSpec((None, tr, c), lambda j, i, core_ref: (j, i, 0))],
            out_specs=pl.BlockSpec((None, tr, c), lambda j, i, core_ref: (j, i, 0))),
        out_shape=jax.ShapeDtypeStruct((4, r, c), x8.dtype),
        compiler_params=_cparams(("parallel", "parallel"), VMEM_LIMIT))(core, x8, r4)


def _row_tile(r, bytes_per_row, budget=4 * 1024 * 1024):
    best = None
    for t in range(16, r + 1, 16):
        if r % t == 0 and t * bytes_per_row <= budget:
            best = t
    return best if best is not None else r


def _mm(a, b, dims, name, out_dtype=F32, add=None, out_shards=1, tm=1024, tn=1024, tk=1024):
    S = b.shape[0] if b.ndim == 3 else 1
    bshape = (b.shape[1], S * b.shape[2]) if b.ndim == 3 else b.shape
    if dims == "nn":
        (M, K), (K2, N) = a.shape, bshape
    elif dims == "nt":
        (M, K), (N, K2) = a.shape, bshape
    else:
        (K, M), (K2, N) = a.shape, bshape
    assert K == K2, (a.shape, b.shape, dims)
    cs = bshape[1] // S
    tm, tk = _tile(M, tm), _tile(K, min(tk, cs) if (S > 1 and dims == "nt") else tk)
    tn = _tile(N, min(tn, cs) if (S > 1 and dims != "nt") else min(tn, N // out_shards))
    assert cs % (tk if dims == "nt" else tn) == 0 and (N // out_shards) % tn == 0
    nk = K // tk

    def body(*refs):
        if add is None:
            a_ref, b_ref, o_ref, acc = refs
        else:
            a_ref, b_ref, c_ref, o_ref, acc = refs
        k = pl.program_id(2)

        @pl.when(k == 0)
        def _():
            acc[...] = jnp.zeros_like(acc)

        acc[...] += _dot(a_ref[...], b_ref[...], dims)

        @pl.when(k == nk - 1)
        def _():
            r = acc[...]
            if add is not None:
                r = r + c_ref[...]
            o_ref[...] = r.astype(out_dtype)

    a_spec = (pl.BlockSpec((tk, tm), lambda i, j, k: (k, i)) if dims == "tn"
              else pl.BlockSpec((tm, tk), lambda i, j, k: (i, k)))
    if S == 1:
        b_spec = (pl.BlockSpec((tn, tk), lambda i, j, k: (j, k)) if dims == "nt"
                  else pl.BlockSpec((tk, tn), lambda i, j, k: (k, j)))
    elif dims == "nt":
        per = cs // tk
        b_spec = pl.BlockSpec((None, tn, tk), lambda i, j, k: (k // per, j, k % per))
    else:
        per = cs // tn
        b_spec = pl.BlockSpec((None, tk, tn), lambda i, j, k: (j // per, k, j % per))
    in_specs = [a_spec, b_spec]
    args = [a, b]
    if add is not None:
        in_specs.append(pl.BlockSpec((tm, tn), lambda i, j, k: (i, j)))
        args.append(add)
    if out_shards == 1:
        out_spec, out_shape = pl.BlockSpec((tm, tn), lambda i, j, k: (i, j)), (M, N)
    else:
        oper = N // out_shards // tn
        out_spec = pl.BlockSpec((None, tm, tn), lambda i, j, k: (j // oper, i, j % oper))
        out_shape = (out_shards, M, N // out_shards)
    return pl.pallas_call(
        body, name=name, grid=(M // tm, N // tn, nk),
        in_specs=in_specs, out_specs=out_spec,
        out_shape=jax.ShapeDtypeStruct(out_shape, out_dtype),
        scratch_shapes=[pltpu.VMEM((tm, tn), F32)],
        compiler_params=_cparams(("parallel", "parallel", "arbitrary"), VMEM_LIMIT),
    )(*args)


def _mm_sum(pairs, name, out_dtype=F32, tm=768, tn=1024, tk=512):
    M, N = pairs[0][0].shape[0], pairs[0][1].shape[1]
    tm, tn = _tile(M, tm), _tile(N, tn)
    tks = [_tile(a.shape[1], tk) for a, _ in pairs]
    nks = [a.shape[1] // t for (a, _), t in zip(pairs, tks)]
    starts = [sum(nks[:g]) for g in range(len(pairs))]
    nk = sum(nks)
    G = len(pairs)

    def body(*refs):
        o_ref, acc = refs[2 * G], refs[2 * G + 1]
        k = pl.program_id(2)

        @pl.when(k == 0)
        def _():
            acc[...] = jnp.zeros_like(acc)

        for g in range(G):
            @pl.when(jnp.logical_and(k >= starts[g], k < starts[g] + nks[g]))
            def _(g=g):
                acc[...] += _dot(refs[2 * g][...], refs[2 * g + 1][...])

        @pl.when(k == nk - 1)
        def _():
            o_ref[...] = acc[...].astype(out_dtype)

    in_specs, args = [], []
    for g, (a, b) in enumerate(pairs):
        kk = lambda k, g=g: jnp.clip(k - starts[g], 0, nks[g] - 1)
        in_specs += [pl.BlockSpec((tm, tks[g]), lambda i, j, k, kk=kk: (i, kk(k))),
                     pl.BlockSpec((tks[g], tn), lambda i, j, k, kk=kk: (kk(k), j))]
        args += [a, b]
    return pl.pallas_call(
        body, name=name, grid=(M // tm, N // tn, nk),
        in_specs=in_specs, out_specs=pl.BlockSpec((tm, tn), lambda i, j, k: (i, j)),
        out_shape=jax.ShapeDtypeStruct((M, N), out_dtype),
        scratch_shapes=[pltpu.VMEM((tm, tn), F32)],
        compiler_params=_cparams(("parallel", "parallel", "arbitrary"), VMEM_LIMIT),
    )(*args)


def _ada_fwd(ccp, w, b):
    def body(c_ref, w_ref, b_ref, o_ref):
        o_ref[...] = _dot(_silu(c_ref[...]), w_ref[...]) + b_ref[...]

    return pl.pallas_call(
        body, name="ada_fwd", out_shape=jax.ShapeDtypeStruct((ccp.shape[0], w.shape[1]), F32),
        compiler_params=_cparams(None, VMEM_LIMIT))(ccp, w, b)


def _ada_bwd(ccp, w, g, cctx, n_loc):
    def body(c_ref, w_ref, g_ref, cc_ref, gw_ref, dc_ref):
        gw_ref[...] = _dot(_silu(c_ref[...]), g_ref[...], "tn")
        dcc = _dot(g_ref[...], w_ref[...], "nt")
        row = lax.broadcasted_iota(jnp.int32, dcc.shape, 0)
        part = jnp.sum(jnp.where(row % SUBLANES == n_loc, dcc, 0.0), axis=0, keepdims=True)
        dc_ref[...] = jnp.broadcast_to(part * _dsilu(cc_ref[...]), dc_ref.shape)

    D = ccp.shape[1]
    return pl.pallas_call(
        body, name="ada_bwd",
        out_shape=(jax.ShapeDtypeStruct(w.shape, F32), jax.ShapeDtypeStruct((SUBLANES, D), F32)),
        compiler_params=_cparams(None, VMEM_LIMIT))(ccp, w, g, cctx)


def _norm_mod_fwd(h, gain, mods, i_shift, i_scale, nct, tm, name):
    B, T, D = h.shape

    def body(h_ref, g_ref, m_ref, u_ref):
        x = h_ref[0]
        r = lax.rsqrt(jnp.mean(x * x, axis=-1, keepdims=True) + EPS)
        n = x * r * g_ref[...]
        u_ref[0] = (n * (1.0 + m_ref[0, i_scale:i_scale + 1, :]) + m_ref[0, i_shift:i_shift + 1, :]).astype(BF16)

    return pl.pallas_call(
        body, name=name, grid=(B, T // tm),
        in_specs=[pl.BlockSpec((1, tm, D), lambda b, t: (b, t, 0)),
                  pl.BlockSpec((1, D), lambda b, t: (0, 0)),
                  pl.BlockSpec((1, SUBLANES, D), lambda b, t: (jnp.where(t < nct, B, b), 0, 0))],
        out_specs=pl.BlockSpec((1, tm, D), lambda b, t: (b, t, 0)),
        out_shape=jax.ShapeDtypeStruct((B, T, D), BF16),
        compiler_params=_cparams(("parallel", "parallel"), VMEM_LIMIT),
    )(h, gain, mods)


def _norm_mod_bwd(h, du, gain, mods, i_scale, nct, tm, res, name):
    B, T, D = h.shape
    nt = T // tm

    def body(h_ref, du_ref, g_ref, m_ref, res_ref, dx_ref, ax_ref, ac_ref):
        t = pl.program_id(1)
        x = h_ref[0]
        r = lax.rsqrt(jnp.mean(x * x, axis=-1, keepdims=True) + EPS)
        nh = x * r
        g = g_ref[...]
        du_ = du_ref[0]
        dn = du_ * (1.0 + m_ref[0, i_scale:i_scale + 1, :])
        dnh = dn * g
        dx = r * (dnh - nh * jnp.mean(dnh * nh, axis=-1, keepdims=True))
        sums = (jnp.sum(du_, axis=0, keepdims=True), jnp.sum(du_ * nh * g, axis=0, keepdims=True),
                jnp.sum(dn * nh, axis=0, keepdims=True))

        @pl.when(t == 0)
        def _():
            ax_ref[...] = jnp.zeros_like(ax_ref)
            ac_ref[...] = jnp.zeros_like(ac_ref)

        def accumulate(ref):
            for i, s in enumerate(sums):
                ref[0, i:i + 1, :] += s

        @pl.when(t < nct)
        def _():
            accumulate(ac_ref)

        @pl.when(t >= nct)
        def _():
            accumulate(ax_ref)
            dx_ref[0] = dx + res_ref[0]

    lat = lambda b, t: (b, jnp.maximum(t - nct, 0), 0)
    acc = pl.BlockSpec((1, SUBLANES, D), lambda b, t: (b, 0, 0))
    return pl.pallas_call(
        body, name=name, grid=(B, nt),
        in_specs=[pl.BlockSpec((1, tm, D), lambda b, t: (b, t, 0)),
                  pl.BlockSpec((1, tm, D), lambda b, t: (b, t, 0)),
                  pl.BlockSpec((1, D), lambda b, t: (0, 0)),
                  pl.BlockSpec((1, SUBLANES, D), lambda b, t: (jnp.where(t < nct, B, b), 0, 0)),
                  pl.BlockSpec((1, tm, D), lat)],
        out_specs=(pl.BlockSpec((1, tm, D), lat), acc, acc),
        out_shape=(jax.ShapeDtypeStruct(res.shape, F32),
                   jax.ShapeDtypeStruct((B, SUBLANES, D), F32), jax.ShapeDtypeStruct((B, SUBLANES, D), F32)),
        compiler_params=_cparams(("parallel", "arbitrary"), VMEM_LIMIT),
    )(h, du, gain, mods, res)


def _resid_norm_fwd(x, y, gain, mods, i_gate, tm, name):
    B, N, D = x.shape

    def body(x_ref, y_ref, g_ref, m_ref, o_ref):
        y_ = y_ref[0]
        r = lax.rsqrt(jnp.mean(y_ * y_, axis=-1, keepdims=True) + EPS)
        o_ref[0] = x_ref[0] + y_ * r * g_ref[...] * m_ref[0, i_gate:i_gate + 1, :]

    row = pl.BlockSpec((1, tm, D), lambda b, t: (b, t, 0))
    return pl.pallas_call(
        body, name=name, grid=(B, N // tm),
        in_specs=[row, row, pl.BlockSpec((1, D), lambda b, t: (0, 0)),
                  pl.BlockSpec((1, SUBLANES, D), lambda b, t: (b, 0, 0))],
        out_specs=row, out_shape=jax.ShapeDtypeStruct((B, N, D), F32),
        compiler_params=_cparams(("parallel", "parallel"), VMEM_LIMIT),
    )(x, y, gain, mods)


def _resid_norm_bwd_math(y_, dh, g, gate):
    r = lax.rsqrt(jnp.mean(y_ * y_, axis=-1, keepdims=True) + EPS)
    nh = y_ * r
    dgate = jnp.sum(dh * nh * g, axis=0, keepdims=True)
    dn = dh * gate
    dgain = jnp.sum(dn * nh, axis=0, keepdims=True)
    dnh = dn * g
    dy = r * (dnh - nh * jnp.mean(dnh * nh, axis=-1, keepdims=True))
    return dy, dgate, dgain


def _resid_norm_bwd(y, dh, gain, mods, i_gate, tm, name):
    B, N, D = y.shape

    def body(y_ref, dh_ref, g_ref, m_ref, dy_ref, acc_ref):
        t = pl.program_id(1)
        dy, dgate, dgain = _resid_norm_bwd_math(y_ref[0], dh_ref[0], g_ref[...], m_ref[0, i_gate:i_gate + 1, :])
        dy_ref[0] = dy.astype(BF16)

        @pl.when(t == 0)
        def _():
            acc_ref[...] = jnp.zeros_like(acc_ref)

        acc_ref[0, 0:1, :] += dgate
        acc_ref[0, 1:2, :] += dgain

    row = pl.BlockSpec((1, tm, D), lambda b, t: (b, t, 0))
    return pl.pallas_call(
        body, name=name, grid=(B, N // tm),
        in_specs=[row, row, pl.BlockSpec((1, D), lambda b, t: (0, 0)),
                  pl.BlockSpec((1, SUBLANES, D), lambda b, t: (b, 0, 0))],
        out_specs=(row, pl.BlockSpec((1, SUBLANES, D), lambda b, t: (b, 0, 0))),
        out_shape=(jax.ShapeDtypeStruct((B, N, D), BF16), jax.ShapeDtypeStruct((B, SUBLANES, D), F32)),
        compiler_params=_cparams(("parallel", "arbitrary"), VMEM_LIMIT),
    )(y, dh, gain, mods)


def _final_fwd_bwd(h1, d, gain, mods, i_gate, tgt, tm):
    B, N, D = h1.shape

    def body(h_ref, d_ref, g_ref, m_ref, t_ref, dd_ref, dh_ref, acc_ref):
        t = pl.program_id(1)
        d_ = d_ref[0]
        g = g_ref[...]
        gate = m_ref[0, i_gate:i_gate + 1, :]
        r = lax.rsqrt(jnp.mean(d_ * d_, axis=-1, keepdims=True) + EPS)
        e = h_ref[0] + d_ * r * g * gate - t_ref[0]
        dh = e * (1.0 / D)
        dh_ref[0] = dh
        dy, dgate, dgain = _resid_norm_bwd_math(d_, dh, g, gate)
        dd_ref[0] = dy.astype(BF16)

        @pl.when(t == 0)
        def _():
            acc_ref[...] = jnp.zeros_like(acc_ref)

        acc_ref[0, 0:1, :] += dgate
        acc_ref[0, 1:2, :] += dgain
        acc_ref[0, 2:3, :] += jnp.sum(e * e, axis=0, keepdims=True)

    row = pl.BlockSpec((1, tm, D), lambda b, t: (b, t, 0))
    return pl.pallas_call(
        body, name="final_fwd_bwd", grid=(B, N // tm),
        in_specs=[row, row, pl.BlockSpec((1, D), lambda b, t: (0, 0)),
                  pl.BlockSpec((1, SUBLANES, D), lambda b, t: (b, 0, 0)), row],
        out_specs=(row, row, pl.BlockSpec((1, SUBLANES, D), lambda b, t: (b, 0, 0))),
        out_shape=(jax.ShapeDtypeStruct((B, N, D), BF16), jax.ShapeDtypeStruct((B, N, D), F32),
                   jax.ShapeDtypeStruct((B, SUBLANES, D), F32)),
        compiler_params=_cparams(("parallel", "arbitrary"), VMEM_LIMIT),
    )(h1, d, gain, mods, tgt)


def _shifted(x, prev, nxt, off, row, tm):
    if off == 0:
        return x
    if off < 0:
        y = pltpu.roll(x, -off, 0)
        for r in range(-off):
            y = jnp.where(row == r, prev[SUBLANES + r + off:SUBLANES + r + off + 1, :], y)
        return y
    y = pltpu.roll(x, tm - off, 0)
    for r in range(off):
        y = jnp.where(row == tm - off + r, nxt[r:r + 1, :], y)
    return y


def _halo_specs(T, tm, tc, cb0, order):
    r8, n8 = tm // SUBLANES, T // SUBLANES
    if order == "btj":
        prev = lambda b, t, j: (b, jnp.maximum(t * r8 - 1, 0), cb0 + j)
        nxt = lambda b, t, j: (b, jnp.minimum((t + 1) * r8, n8 - 1), cb0 + j)
    else:
        prev = lambda b, j, t: (b, jnp.maximum(t * r8 - 1, 0), cb0 + j)
        nxt = lambda b, j, t: (b, jnp.minimum((t + 1) * r8, n8 - 1), cb0 + j)
    return pl.BlockSpec((1, SUBLANES, tc), prev), pl.BlockSpec((1, SUBLANES, tc), nxt)


def _seg_halos(p_ref, n_ref, t, nct, nt):
    seg_start = jnp.logical_or(t == 0, t == nct)
    seg_end = jnp.logical_or(t == nct - 1, t == nt - 1)
    prev = jnp.where(seg_start, 0.0, p_ref[0])
    nxt = jnp.where(seg_end, 0.0, n_ref[0])
    return prev, nxt


def _dwconv(x, col0, C, w, bias, offs, nct, tm, tc, name, out_dtype=F32, qkv_heads=None):
    B, T, _ = x.shape
    nt = T // tm
    cb0 = col0 // tc
    if qkv_heads is not None:
        assert tc == qkv_heads[0] * qkv_heads[1] and C == 3 * tc

    def body(*refs):
        refs = list(refs)
        a_ref = refs.pop() if qkv_heads is not None else None
        if bias is None:
            x_ref, p_ref, n_ref, w_ref, o_ref = refs
        else:
            x_ref, p_ref, n_ref, w_ref, b_ref, o_ref = refs
        t = pl.program_id(1)
        prev, nxt = _seg_halos(p_ref, n_ref, t, nct, nt)
        x_ = x_ref[0]
        row = lax.broadcasted_iota(jnp.int32, x_.shape, 0)
        acc = None
        for j, off in enumerate(offs):
            term = _shifted(x_, prev, nxt, off, row, tm) * w_ref[j:j + 1, :]
            acc = term if acc is None else acc + term
        if bias is not None:
            acc = acc + b_ref[...]
        o_ref[0] = acc.astype(out_dtype)
        if qkv_heads is not None:
            H, hd = qkv_heads
            part = pl.program_id(2)
            for h in range(H):
                sl = slice(h * hd, (h + 1) * hd)
                s = _silu(acc[:, sl])
                rs = lax.rsqrt(jnp.sum(s * s, axis=-1, keepdims=True) + EPS)
                scale = jnp.where(part == 0, rs * (float(hd) ** -0.5), jnp.where(part == 1, rs, 1.0))
                a_ref[0, :, sl] = s * scale

    pspec, nspec = _halo_specs(T, tm, tc, cb0, "btj")
    in_specs = [pl.BlockSpec((1, tm, tc), lambda b, t, j: (b, t, cb0 + j)), pspec, nspec,
                pl.BlockSpec((w.shape[0], tc), lambda b, t, j: (0, j))]
    args = [x, x, x, w]
    if bias is not None:
        in_specs.append(pl.BlockSpec((1, tc), lambda b, t, j: (0, j)))
        args.append(bias)
    tile = pl.BlockSpec((1, tm, tc), lambda b, t, j: (b, t, j))
    out_specs, out_shape = tile, jax.ShapeDtypeStruct((B, T, C), out_dtype)
    if qkv_heads is not None:
        out_specs, out_shape = (tile, tile), (out_shape, jax.ShapeDtypeStruct((B, T, C), F32))
    return pl.pallas_call(
        body, name=name, grid=(B, nt, C // tc),
        in_specs=in_specs, out_specs=out_specs, out_shape=out_shape,
        compiler_params=_cparams(("parallel", "parallel", "parallel"), VMEM_LIMIT),
    )(*args)


def _dwconv_wgrad(dy, x, col0, C, offs, nct, tm, tc, name):
    B, T, _ = x.shape
    nt = T // tm
    cb0 = col0 // tc
    K = len(offs)

    def body(dy_ref, x_ref, p_ref, n_ref, acc_ref):
        t = pl.program_id(2)
        prev, nxt = _seg_halos(p_ref, n_ref, t, nct, nt)
        x_ = x_ref[0]
        dy_ = dy_ref[0]
        row = lax.broadcasted_iota(jnp.int32, x_.shape, 0)

        @pl.when(t == 0)
        def _():
            acc_ref[...] = jnp.zeros_like(acc_ref)

        for j, off in enumerate(offs):
            acc_ref[0, j:j + 1, :] += jnp.sum(dy_ * _shifted(x_, prev, nxt, off, row, tm), axis=0, keepdims=True)
        acc_ref[0, K:K + 1, :] += jnp.sum(dy_, axis=0, keepdims=True)

    pspec, nspec = _halo_specs(T, tm, tc, cb0, "bjt")
    return pl.pallas_call(
        body, name=name, grid=(B, C // tc, nt),
        in_specs=[pl.BlockSpec((1, tm, tc), lambda b, j, t: (b, t, j)),
                  pl.BlockSpec((1, tm, tc), lambda b, j, t: (b, t, cb0 + j)), pspec, nspec],
        out_specs=pl.BlockSpec((1, SUBLANES, tc), lambda b, j, t: (b, 0, j)),
        out_shape=jax.ShapeDtypeStruct((B, SUBLANES, C), F32),
        compiler_params=_cparams(("parallel", "parallel", "arbitrary"), VMEM_LIMIT),
    )(dy, x, x, x)


def _ab_act(pab, alog, dtb, nd, tm):
    R = pab.shape[0]

    def body(p_ref, al_ref, dt_ref, o_ref):
        p = p_ref[...]
        lane = lax.broadcasted_iota(jnp.int32, p.shape, 1)
        g = -jnp.exp(al_ref[...]) * _softplus(p + dt_ref[...])
        o_ref[...] = jnp.where(lane < nd, g, jnp.where(lane < 2 * nd, _sigmoid(p), 0.0))

    row = pl.BlockSpec((tm, LANES), lambda i: (i, 0))
    vec = pl.BlockSpec((1, LANES), lambda i: (0, 0))
    return pl.pallas_call(
        body, name="ab_act", grid=(R // tm,), in_specs=[row, vec, vec], out_specs=row,
        out_shape=jax.ShapeDtypeStruct((R, LANES), F32),
        compiler_params=_cparams(("parallel",), VMEM_LIMIT))(pab, alog, dtb)


def _ab_act_bwd(pab, gb, dgb, alog, dtb, nd, tm):
    R = pab.shape[0]

    def body(p_ref, gb_ref, d0_ref, d1_ref, al_ref, dt_ref, o_ref, acc_ref):
        i = pl.program_id(0)
        p = p_ref[...]
        gb_ = gb_ref[...]
        d_ = d0_ref[...] + d1_ref[...]
        lane = lax.broadcasted_iota(jnp.int32, p.shape, 1)
        is_g = lane < nd
        is_b = jnp.logical_and(lane >= nd, lane < 2 * nd)
        da = jnp.where(is_g, d_ * (-jnp.exp(al_ref[...])) * _sigmoid(p + dt_ref[...]), 0.0)
        db = jnp.where(is_b, d_ * gb_ * (1.0 - gb_), 0.0)
        o_ref[...] = (da + db).astype(BF16)

        @pl.when(i == 0)
        def _():
            acc_ref[...] = jnp.zeros_like(acc_ref)

        acc_ref[0:1, :] += jnp.sum(jnp.where(is_g, d_ * gb_, 0.0), axis=0, keepdims=True)
        acc_ref[1:2, :] += jnp.sum(da, axis=0, keepdims=True)

    row = pl.BlockSpec((tm, LANES), lambda i: (i, 0))
    vec = pl.BlockSpec((1, LANES), lambda i: (0, 0))
    return pl.pallas_call(
        body, name="ab_act_bwd", grid=(R // tm,),
        in_specs=[row, row, row, row, vec, vec],
        out_specs=(row, pl.BlockSpec((SUBLANES, LANES), lambda i: (0, 0))),
        out_shape=(jax.ShapeDtypeStruct((R, LANES), BF16), jax.ShapeDtypeStruct((SUBLANES, LANES), F32)),
        compiler_params=_cparams(("arbitrary",), VMEM_LIMIT))(pab, gb, dgb[0], dgb[1], alog, dtb)


def _dn_act_bwd(cv, dqkv, H, hd, tm):
    B, T, C3 = cv.shape
    qscale = float(hd) ** -0.5

    def body(c_ref, d0_ref, d1_ref, o_ref):
        part = pl.program_id(0)
        for h in range(H):
            sl = slice(h * hd, (h + 1) * hd)
            c = c_ref[0, :, sl]
            s = _silu(c)
            dout = d0_ref[0, :, sl] + d1_ref[0, :, sl]
            rs = lax.rsqrt(jnp.sum(s * s, axis=-1, keepdims=True) + EPS)
            sh = s * rs
            dnorm = rs * (dout - sh * jnp.sum(dout * sh, axis=-1, keepdims=True))
            ds = jnp.where(part == 0, dnorm * qscale, jnp.where(part == 1, dnorm, dout))
            o_ref[0, :, sl] = ds * _dsilu(c)

    spec = pl.BlockSpec((1, tm, H * hd), lambda p, b, t: (b, t, p))
    return pl.pallas_call(
        body, name="dn_act_bwd", grid=(3, B, T // tm), in_specs=[spec, spec, spec], out_specs=spec,
        out_shape=jax.ShapeDtypeStruct((B, T, C3), F32),
        compiler_params=_cparams(("parallel",) * 3, VMEM_LIMIT))(cv, dqkv[0], dqkv[1])


def _chunk_of(d, s, ncc, nch):
    if d == 0:
        return s
    return jnp.where(s < ncc, ncc - 1 - s, nch - 1 - (s - ncc))


def _delta_masks(d):
    row = lax.broadcasted_iota(jnp.int32, (CHUNK, CHUNK), 0)
    col = lax.broadcasted_iota(jnp.int32, (CHUNK, CHUNK), 1)
    sign = 1 - 2 * d
    diff = (row - col) * sign
    return diff >= 0, diff > 0, diff <= 0


def _each(f, *lists):
    return [f(*a) for a in zip(*lists)]


def _unit_tri_inverse(As):
    C = As[0].shape[0]
    row = lax.broadcasted_iota(jnp.int32, (C, C), 0)
    col = lax.broadcasted_iota(jnp.int32, (C, C), 1)
    eye = jnp.where(row == col, 1.0, 0.0).astype(F32)
    same = lambda shift: jnp.right_shift(row, shift) == jnp.right_shift(col, shift)
    in8 = same(3)
    xs = [-jnp.where(in8, a, 0.0) for a in As]
    ts = [eye + x for x in xs]
    ps = _each(lambda x: _dot(x, x), xs)
    tp = _each(lambda t, p: _dot(_rows(t, p), p), ts, ps)
    ts = _each(lambda t, m: t + m[:C], ts, tp)
    ts = _each(lambda t, m: t + _dot(t, m[C:]), ts, tp)
    shift = 3
    while (1 << shift) < C:
        off = jnp.logical_and(same(shift + 1), jnp.right_shift(row, shift) != jnp.right_shift(col, shift))
        los = [jnp.where(off, a, 0.0) for a in As]
        ts = _each(lambda t, lo: t - _dot(t, _dot(lo, t)), ts, los)
        shift += 1
    def residual(a, t):
        (ah, al), (th, tl) = _split2(eye + a), _split2(t)
        m = _dot(_rows(ah, al), th)
        return eye - (m[:C] + (m[C:] + _dot(ah, tl)))

    rs = _each(residual, As, ts)
    return _each(lambda t, r: t + _dot(t, r), ts, rs)


def _rows(*xs):
    return jnp.concatenate(xs, axis=0)


def _cols(*xs):
    return jnp.concatenate(xs, axis=1)


def _delta_chunk_fwd(q, k, v, g_i, g_j, beta_i, gl, S, incl, strict, Tm=None):
    D_ = _each(lambda gi, gj, m: jnp.where(m, jnp.exp(jnp.where(m, gi - gj, 0.0)), 0.0), g_i, g_j, incl)
    C, dk = k[0].shape
    kb = _each(lambda k_, b: k_ * b, k, beta_i)
    kq = _each(lambda kb_, q_, k_: _dot(_rows(kb_, q_), k_, "nt"), kb, q, k)
    A = _each(lambda m_, d, m: jnp.where(m, m_[:C] * d, 0.0), kq, D_, strict)
    P = _each(lambda m_, d, m: jnp.where(m, m_[C:] * d, 0.0), kq, D_, incl)
    if Tm is None:
        Tm = _unit_tri_inverse(A)
    gam = _each(jnp.exp, g_i)
    wu = _each(lambda t, kb_, g, v_, b: _dot(t, _cols(kb_ * g, v_ * b)), Tm, kb, gam, v, beta_i)
    w = [m_[:, :dk] for m_ in wu]
    u = [m_[:, dk:] for m_ in wu]
    qg = _each(lambda q_, g: q_ * g, q, gam)
    ws = _each(lambda w_, qg_, s: _dot(_rows(w_, qg_), s), w, qg, S)
    vn = _each(lambda u_, m_: u_ - m_[:C], u, ws)
    o = _each(lambda m_, p, vn_: m_[C:] + _dot(p, vn_), ws, P, vn)
    e_i = _each(lambda gl_, gi: jnp.exp(gl_ - gi), gl, g_i)
    kd = _each(lambda k_, e: k_ * e, k, e_i)
    Gam = _each(jnp.exp, gl)
    S_new = _each(lambda s, g, kd_, vn_: s * g + _dot(kd_, vn_, "tn"), S, Gam, kd, vn)
    return dict(D=D_, kb=kb, A=A, Tm=Tm, gam=gam, w=w, u=u, vn=vn, P=P, qg=qg, o=o, e=e_i, kd=kd, Gam=Gam, S_new=S_new)


def _delta_gb(gb_ref, d, incl, incl_t, H):
    gb = gb_ref[0]
    g = gb[:, d * H:(d + 1) * H]
    beta = gb[:, (2 + d) * H:(3 + d) * H]
    gc_col = _dot_mask(incl.astype(BF16), g)
    gc_row = _dot_mask(incl_t.astype(BF16), g, "xtn")
    gl = jnp.sum(g, axis=0, keepdims=True)
    return beta, gc_col, gc_row, gl


def _delta_fwd(qkvn, gb, H, hd, ncc):
    B, T, _ = qkvn.shape
    nch = T // CHUNK
    HD = H * hd
    pairs = [(d, h) for d in range(2) for h in range(H)]

    def body(q0, k0, v0, gb0, q1, k1, v1, gb1, o0, o1, sin0, sin1, tm0, tm1, S_ref):
        s = pl.program_id(1)
        q_refs, k_refs, v_refs, gb_refs = (q0, q1), (k0, k1), (v0, v1), (gb0, gb1)
        o_refs, sin_refs, tm_refs = (o0, o1), (sin0, sin1), (tm0, tm1)

        @pl.when(s == 0)
        def _():
            S_ref[...] = jnp.zeros_like(S_ref)

        masks = [_delta_masks(d) for d in range(2)]
        gbs = [_delta_gb(gb_refs[d], d, masks[d][0], masks[d][2], H) for d in range(2)]
        head = lambda ref, h: ref[0, :, h * hd:(h + 1) * hd]
        S = [S_ref[d, h] for d, h in pairs]
        r = _delta_chunk_fwd([head(q_refs[d], h) for d, h in pairs], [head(k_refs[d], h) for d, h in pairs],
                             [head(v_refs[d], h) for d, h in pairs],
                             [gbs[d][1][:, h:h + 1] for d, h in pairs], [gbs[d][2][h:h + 1, :] for d, h in pairs],
                             [gbs[d][0][:, h:h + 1] for d, h in pairs], [gbs[d][3][:, h:h + 1] for d, h in pairs],
                             S, [masks[d][0] for d, h in pairs], [masks[d][1] for d, h in pairs])
        for i, (d, h) in enumerate(pairs):
            sin_refs[d][0, 0, h] = S[i]
            tm_refs[d][0, 0, h] = r["Tm"][i]
            S_ref[d, h] = r["S_new"][i]
            o_refs[d][0, :, h * hd:(h + 1) * hd] = r["o"][i]

    def dir_specs(d):
        cidx = lambda b, s: _chunk_of(d, s, ncc, nch)
        ins = [pl.BlockSpec((1, CHUNK, HD), lambda b, s, p=p: (b, cidx(b, s), p)) for p in range(3)]
        ins.append(pl.BlockSpec((1, CHUNK, LANES), lambda b, s: (b, cidx(b, s), 0)))
        return (ins, pl.BlockSpec((1, CHUNK, HD), lambda b, s: (b, cidx(b, s), 0)),
                pl.BlockSpec((1, 1, H, hd, hd), lambda b, s: (b, cidx(b, s), 0, 0, 0)),
                pl.BlockSpec((1, 1, H, CHUNK, CHUNK), lambda b, s: (b, cidx(b, s), 0, 0, 0)))

    (in0, o_s0, sin_s0, tm_s0), (in1, o_s1, sin_s1, tm_s1) = dir_specs(0), dir_specs(1)
    o_shape = jax.ShapeDtypeStruct((B, T, HD), F32)
    sin_shape = jax.ShapeDtypeStruct((B, nch, H, hd, hd), F32)
    tm_shape = jax.ShapeDtypeStruct((B, nch, H, CHUNK, CHUNK), F32)
    return pl.pallas_call(
        body, name="delta_fwd", grid=(B, nch),
        in_specs=in0 + in1, out_specs=(o_s0, o_s1, sin_s0, sin_s1, tm_s0, tm_s1),
        out_shape=(o_shape, o_shape, sin_shape, sin_shape, tm_shape, tm_shape),
        scratch_shapes=[pltpu.VMEM((2, H, hd, hd), F32)],
        compiler_params=_cparams(("parallel", "arbitrary"), VMEM_LIMIT),
    )(qkvn, qkvn, qkvn, gb, qkvn, qkvn, qkvn, gb)


def _delta_bwd(qkvn, gb, sin, tms, do, H, hd, ncc):
    B, T, _ = qkvn.shape
    nch = T // CHUNK
    HD = H * hd
    pairs = [(d, h) for d in range(2) for h in range(H)]

    def body(q0, k0, v0, gb0, sin0, tm0, do0, q1, k1, v1, gb1, sin1, tm1, do1, dqkv0, dqkv1, dgb0, dgb1, dS_ref):
        s = pl.program_id(1)
        tm_refs = (tm0, tm1)
        q_refs, k_refs, v_refs, gb_refs = (q0, q1), (k0, k1), (v0, v1), (gb0, gb1)
        sin_refs, do_refs, dqkv_refs, dgb_refs = (sin0, sin1), (do0, do1), (dqkv0, dqkv1), (dgb0, dgb1)

        @pl.when(s == 0)
        def _():
            dS_ref[...] = jnp.zeros_like(dS_ref)

        masks = [_delta_masks(d) for d in range(2)]
        gbs = [_delta_gb(gb_refs[d], d, masks[d][0], masks[d][2], H) for d in range(2)]
        incl = [masks[d][0] for d, h in pairs]
        strict = [masks[d][1] for d, h in pairs]
        lane = lax.broadcasted_iota(jnp.int32, (1, LANES), 1)
        ones = jnp.ones((3 * CHUNK, LANES), BF16)
        head = lambda ref, h: ref[0, :, h * hd:(h + 1) * hd]
        q = [head(q_refs[d], h) for d, h in pairs]
        k = [head(k_refs[d], h) for d, h in pairs]
        v = [head(v_refs[d], h) for d, h in pairs]
        do_ = [head(do_refs[d], h) for d, h in pairs]
        beta_i = [gbs[d][0][:, h:h + 1] for d, h in pairs]
        S = [sin_refs[d][0, 0, h] for d, h in pairs]
        dSn = [dS_ref[d, h] for d, h in pairs]
        f = _delta_chunk_fwd(q, k, v, [gbs[d][1][:, h:h + 1] for d, h in pairs], [gbs[d][2][h:h + 1, :] for d, h in pairs],
                             beta_i, [gbs[d][3][:, h:h + 1] for d, h in pairs], S, incl, strict,
                             Tm=[tm_refs[d][0, 0, h] for d, h in pairs])
        rsum = lambda x: jnp.sum(x, axis=1, keepdims=True)
        dvn = _each(lambda p, d_, kd, ds: _dot(p, d_, "tn") + _dot(kd, ds), f["P"], do_, f["kd"], dSn)
        dP = _each(lambda d_, vn, m: jnp.where(m, _dot(d_, vn, "nt"), 0.0), do_, f["vn"], incl)
        dqg = _each(lambda d_, s: _dot(d_, s, "nt"), do_, S)
        dS_in = _each(lambda qg, d_, g, ds, w, dv_: _dot(qg, d_, "tn") + g * ds - _dot(w, dv_, "tn"),
                      f["qg"], do_, f["Gam"], dSn, f["w"], dvn)
        dGam = _each(lambda s, ds: jnp.sum(rsum(s * ds), axis=0, keepdims=True), S, dSn)
        dkd = _each(lambda vn, ds: _dot(vn, ds, "nt"), f["vn"], dSn)
        de = _each(lambda dkd_, k_, e: rsum(dkd_ * k_) * e, dkd, k, f["e"])
        dw = _each(lambda dv_, s: -_dot(dv_, s, "nt"), dvn, S)
        dXw = _each(lambda t, dw_: _dot(t, dw_, "tn"), f["Tm"], dw)
        dXu = _each(lambda t, dv_: _dot(t, dv_, "tn"), f["Tm"], dvn)
        dA = _each(lambda xw, w, xu, u, m: -jnp.where(m, _dot(xw, w, "nt") + _dot(xu, u, "nt"), 0.0),
                   dXw, f["w"], dXu, f["u"], strict)
        dM = _each(lambda a, d_: a * d_, dA, f["D"])
        dN = _each(lambda p, d_: p * d_, dP, f["D"])
        dkb = _each(lambda m, k_, xw, g: _dot(m, k_) + xw * g, dM, k, dXw, f["gam"])
        dq = _each(lambda dqg_, g, n, k_: dqg_ * g + _dot(n, k_), dqg, f["gam"], dN, k)
        dk = _each(lambda dkd_, e, m, kb, n, q_, dkb_, b: dkd_ * e + _dot(m, kb, "tn") + _dot(n, q_, "tn") + dkb_ * b,
                   dkd, f["e"], dM, f["kb"], dN, q, dkb, beta_i)
        dv = _each(lambda xu, b: xu * b, dXu, beta_i)
        dbeta = _each(lambda dkb_, k_, xu, v_: rsum(dkb_ * k_) + rsum(xu * v_), dkb, k, dXu, v)
        E = _each(lambda a, A_, p, P_: a * A_ + p * P_, dA, f["A"], dP, f["P"])

        def colsum(e):
            return _dot(_rows(*_split3(e)), ones, "tn")[:, 0:1]

        dg = _each(lambda e, dqg_, qg, xw, kb, g, de_: rsum(e) - colsum(e) + rsum(dqg_ * qg) + rsum(xw * kb) * g - de_,
                   E, dqg, f["qg"], dXw, f["kb"], f["gam"], de)
        dgl_h = _each(lambda de_, dg_, g: jnp.sum(de_, axis=0, keepdims=True) + dg_ * g, de, dGam, f["Gam"])
        for d in range(2):
            dgc = jnp.zeros((CHUNK, LANES), F32)
            dgl = jnp.zeros((1, LANES), F32)
            for h in range(H):
                i = d * H + h
                g_lane, b_lane = d * H + h, (2 + d) * H + h
                dgc = dgc + dg[i] * (lane == g_lane).astype(F32) + dbeta[i] * (lane == b_lane).astype(F32)
                dgl = dgl + dgl_h[i] * (lane == g_lane).astype(F32)
                dS_ref[d, h] = dS_in[i]
                dqkv_refs[d][0, :, h * hd:(h + 1) * hd] = dq[i]
                dqkv_refs[d][0, :, HD + h * hd:HD + (h + 1) * hd] = dk[i]
                dqkv_refs[d][0, :, 2 * HD + h * hd:2 * HD + (h + 1) * hd] = dv[i]
            draw = _dot_mask(masks[d][0].astype(BF16), dgc, "tn") + dgl
            dgb_refs[d][0] = jnp.where(lane < 2 * H, draw, dgc)

    def dir_specs(d):
        cidx = lambda b, s: _chunk_of(d, nch - 1 - s, ncc, nch)
        ins = [pl.BlockSpec((1, CHUNK, HD), lambda b, s, p=p: (b, cidx(b, s), p)) for p in range(3)]
        ins += [pl.BlockSpec((1, CHUNK, LANES), lambda b, s: (b, cidx(b, s), 0)),
                pl.BlockSpec((1, 1, H, hd, hd), lambda b, s: (b, cidx(b, s), 0, 0, 0)),
                pl.BlockSpec((1, 1, H, CHUNK, CHUNK), lambda b, s: (b, cidx(b, s), 0, 0, 0)),
                pl.BlockSpec((1, CHUNK, HD), lambda b, s: (b, cidx(b, s), 0))]
        return (ins, pl.BlockSpec((1, CHUNK, 3 * HD), lambda b, s: (b, cidx(b, s), 0)),
                pl.BlockSpec((1, CHUNK, LANES), lambda b, s: (b, cidx(b, s), 0)))

    (in0, dq_s0, dg_s0), (in1, dq_s1, dg_s1) = dir_specs(0), dir_specs(1)
    dq_shape = jax.ShapeDtypeStruct((B, T, 3 * HD), F32)
    dg_shape = jax.ShapeDtypeStruct((B, T, LANES), F32)
    return pl.pallas_call(
        body, name="delta_bwd", grid=(B, nch),
        in_specs=in0 + in1, out_specs=(dq_s0, dq_s1, dg_s0, dg_s1),
        out_shape=(dq_shape, dq_shape, dg_shape, dg_shape),
        scratch_shapes=[pltpu.VMEM((2, H, hd, hd), F32)],
        compiler_params=_cparams(("parallel", "arbitrary"), VMEM_LIMIT),
    )(qkvn, qkvn, qkvn, gb, sin[0], tms[0], do, qkvn, qkvn, qkvn, gb, sin[1], tms[1], do)


def _dn_out(o2, pz, zcb0, onorm, H, hd, nct, tm):
    B, T, HD = o2[0].shape
    N = T - nct * tm

    def body(o0_ref, o1_ref, z_ref, g_ref, y_ref):
        for h in range(H):
            sl = slice(h * hd, (h + 1) * hd)
            o = o0_ref[0, :, sl] + o1_ref[0, :, sl]
            r = lax.rsqrt(jnp.mean(o * o, axis=-1, keepdims=True) + EPS)
            y_ref[0, :, sl] = (o * r * g_ref[...] * _silu(z_ref[0, :, sl])).astype(BF16)

    return pl.pallas_call(
        body, name="dn_out", grid=(B, N // tm),
        in_specs=[pl.BlockSpec((1, tm, HD), lambda b, t: (b, t + nct, 0)),
                  pl.BlockSpec((1, tm, HD), lambda b, t: (b, t + nct, 0)),
                  pl.BlockSpec((1, tm, HD), lambda b, t: (b, t + nct, zcb0)),
                  pl.BlockSpec((1, hd), lambda b, t: (0, 0))],
        out_specs=pl.BlockSpec((1, tm, HD), lambda b, t: (b, t, 0)),
        out_shape=jax.ShapeDtypeStruct((B, N, HD), BF16),
        compiler_params=_cparams(("parallel",) * 2, VMEM_LIMIT))(o2[0], o2[1], pz, onorm)


def _dn_out_bwd(o2, pz, zcb0, onorm, dy, H, hd, nct, tm):
    B, T, HD = o2[0].shape
    nt = T // tm

    def body(o0_ref, o1_ref, z_ref, g_ref, dy_ref, do_ref, dz_ref, acc_ref):
        t = pl.program_id(1)

        @pl.when(t == 0)
        def _():
            acc_ref[...] = jnp.zeros_like(acc_ref)

        @pl.when(t < nct)
        def _():
            do_ref[0] = jnp.zeros_like(do_ref[0])
            dz_ref[0] = jnp.zeros_like(dz_ref[0])

        @pl.when(t >= nct)
        def _():
            g = g_ref[...]
            for h in range(H):
                sl = slice(h * hd, (h + 1) * hd)
                o = o0_ref[0, :, sl] + o1_ref[0, :, sl]
                z = z_ref[0, :, sl]
                dy_ = dy_ref[0, :, sl]
                r = lax.rsqrt(jnp.mean(o * o, axis=-1, keepdims=True) + EPS)
                oh = o * r
                dz_ref[0, :, sl] = (dy_ * oh * g * _dsilu(z)).astype(BF16)
                dn = dy_ * _silu(z)
                acc_ref[0, 0:1, :] += jnp.sum(dn * oh, axis=0, keepdims=True)
                doh = dn * g
                do_ref[0, :, sl] = r * (doh - oh * jnp.mean(doh * oh, axis=-1, keepdims=True))

    lat = lambda b, t: (b, jnp.maximum(t - nct, 0), 0)
    row = pl.BlockSpec((1, tm, HD), lambda b, t: (b, t, 0))
    return pl.pallas_call(
        body, name="dn_out_bwd", grid=(B, nt),
        in_specs=[row, row,
                  pl.BlockSpec((1, tm, HD), lambda b, t: (b, t, zcb0)),
                  pl.BlockSpec((1, hd), lambda b, t: (0, 0)),
                  pl.BlockSpec((1, tm, HD), lat)],
        out_specs=(row, row, pl.BlockSpec((1, SUBLANES, hd), lambda b, t: (b, 0, 0))),
        out_shape=(jax.ShapeDtypeStruct((B, T, HD), F32), jax.ShapeDtypeStruct((B, T, HD), BF16),
                   jax.ShapeDtypeStruct((B, SUBLANES, hd), F32)),
        compiler_params=_cparams(("parallel", "arbitrary"), VMEM_LIMIT),
    )(o2[0], o2[1], pz, onorm, dy)


def _lru_gate_math(x, wr, wi, br, bi, lam):
    r = _sigmoid(_dot(x, wr) + br)
    i = _sigmoid(_dot(x, wi) + bi)
    sp = _softplus(-lam)
    la = -LRU_C * r * sp
    a = jnp.exp(la)
    s = jnp.sqrt(-jnp.tanh(la) * (a * a + 1.0))
    return r, i, sp, a, s


def _lru_gates(xc, wbd, bias4, lam, tm, bw):
    B, T, W = xc.shape

    def body(x_ref, w_ref, b_ref, l_ref, a_ref, i_ref):
        x = x_ref[0]
        for d in range(2):
            _, i, _, a, s = _lru_gate_math(x, w_ref[2 * d, 0], w_ref[2 * d + 1, 0],
                                           b_ref[2 * d:2 * d + 1, :], b_ref[2 * d + 1:2 * d + 2, :], l_ref[d:d + 1, :])
            a_ref[d, 0] = a
            i_ref[d, 0] = s * (i * x)

    out = pl.BlockSpec((2, 1, tm, bw), lambda b, t, j: (0, b, t, j))
    return pl.pallas_call(
        body, name="lru_gates", grid=(B, T // tm, W // bw),
        in_specs=[pl.BlockSpec((1, tm, bw), lambda b, t, j: (b, t, j)),
                  pl.BlockSpec((4, 1, bw, bw), lambda b, t, j: (0, j, 0, 0)),
                  pl.BlockSpec((4, bw), lambda b, t, j: (0, j)),
                  pl.BlockSpec((2, bw), lambda b, t, j: (0, j))],
        out_specs=(out, out),
        out_shape=(jax.ShapeDtypeStruct((2, B, T, W), F32), jax.ShapeDtypeStruct((2, B, T, W), F32)),
        compiler_params=_cparams(("parallel",) * 3, VMEM_LIMIT))(xc, wbd, bias4, lam)


def _lru_gates_bwd(xc, wbd, bias4, lam, dinp, da, tm, bw):
    B, T, W = xc.shape
    nt = T // tm

    def body(x_ref, w_ref, b_ref, l_ref, di0_ref, di1_ref, da0_ref, da1_ref, dx_ref, dw_ref, acc_ref):
        di_refs, da_refs = (di0_ref, di1_ref), (da0_ref, da1_ref)
        b_ = pl.program_id(1)
        t = pl.program_id(2)

        @pl.when(jnp.logical_and(b_ == 0, t == 0))
        def _():
            dw_ref[...] = jnp.zeros_like(dw_ref)
            acc_ref[...] = jnp.zeros_like(acc_ref)

        x = x_ref[0]
        dx = jnp.zeros_like(x)
        for d in range(2):
            wr, wi = w_ref[2 * d, 0], w_ref[2 * d + 1, 0]
            lam_ = l_ref[d:d + 1, :]
            r, i, sp, a, s = _lru_gate_math(x, wr, wi, b_ref[2 * d:2 * d + 1, :], b_ref[2 * d + 1:2 * d + 2, :], lam_)
            dinp_ = di_refs[d][0]
            dix = dinp_ * s
            ds = dinp_ * i * x
            dla = da_refs[d][0] * a - ds * (a * a) / s
            dpr = dla * (-LRU_C * sp) * r * (1.0 - r)
            dpi = dix * x * i * (1.0 - i)
            dx = dx + dix * i + _dot(dpr, wr, "nt") + _dot(dpi, wi, "nt")
            dw_ref[2 * d, 0] += _dot(x, dpr, "tn")
            dw_ref[2 * d + 1, 0] += _dot(x, dpi, "tn")
            acc_ref[2 * d:2 * d + 1, :] += jnp.sum(dpr, axis=0, keepdims=True)
            acc_ref[2 * d + 1:2 * d + 2, :] += jnp.sum(dpi, axis=0, keepdims=True)
            acc_ref[4 + d:5 + d, :] += jnp.sum(dla * (-LRU_C * r), axis=0, keepdims=True) * (-_sigmoid(-lam_))
        dx_ref[0] = dx

    tile = pl.BlockSpec((1, tm, bw), lambda j, b, t: (b, t, j))
    return pl.pallas_call(
        body, name="lru_gates_bwd", grid=(W // bw, B, nt),
        in_specs=[pl.BlockSpec((1, tm, bw), lambda j, b, t: (b, t, j)),
                  pl.BlockSpec((4, 1, bw, bw), lambda j, b, t: (0, j, 0, 0)),
                  pl.BlockSpec((4, bw), lambda j, b, t: (0, j)),
                  pl.BlockSpec((2, bw), lambda j, b, t: (0, j)), tile, tile, tile, tile],
        out_specs=(pl.BlockSpec((1, tm, bw), lambda j, b, t: (b, t, j)),
                   pl.BlockSpec((4, 1, bw, bw), lambda j, b, t: (0, j, 0, 0)),
                   pl.BlockSpec((SUBLANES, bw), lambda j, b, t: (0, j))),
        out_shape=(jax.ShapeDtypeStruct((B, T, W), F32), jax.ShapeDtypeStruct(wbd.shape, F32),
                   jax.ShapeDtypeStruct((SUBLANES, W), F32)),
        compiler_params=_cparams(("parallel", "arbitrary", "arbitrary"), VMEM_LIMIT),
    )(xc, wbd, bias4, lam, dinp[0], dinp[1], da[0], da[1])


def _tile_scan(a, x, rev, tm):
    row = lax.broadcasted_iota(jnp.int32, a.shape, 0)
    s = 1
    while s < tm:
        if rev:
            a_sh, x_sh, ok = pltpu.roll(a, tm - s, 0), pltpu.roll(x, tm - s, 0), row < tm - s
        else:
            a_sh, x_sh, ok = pltpu.roll(a, s, 0), pltpu.roll(x, s, 0), row >= s
        x = jnp.where(ok, x + a * x_sh, x)
        a = jnp.where(ok, a * a_sh, a)
        s *= 2
    return a, x


def _scan_tile_of(s, rev, nct, nt):
    if not rev:
        return s
    return jnp.where(s < nct, nct - 1 - s, nt - 1 - (s - nct))


def _lru_scan(a2, x2, d, nct, tm, tc):
    _, B, T, W = a2.shape
    nt = T // tm
    rev = d == 1
    last = 0 if rev else tm - 1

    def body(a_ref, x_ref, h_ref, carry):
        s = pl.program_id(2)

        @pl.when(s == 0)
        def _():
            carry[...] = jnp.zeros_like(carry)

        A, X = _tile_scan(a_ref[0, 0], x_ref[0, 0], rev, tm)
        h = X + A * carry[0:1, :]
        h_ref[0] = h
        carry[...] = jnp.broadcast_to(h[last:last + 1, :], carry.shape)

    tidx = lambda s: _scan_tile_of(s, rev, nct, nt)
    spec = pl.BlockSpec((1, 1, tm, tc), lambda b, j, s: (d, b, tidx(s), j))
    return pl.pallas_call(
        body, name=f"lru_scan{d}", grid=(B, W // tc, nt),
        in_specs=[spec, spec], out_specs=pl.BlockSpec((1, tm, tc), lambda b, j, s: (b, tidx(s), j)),
        out_shape=jax.ShapeDtypeStruct((B, T, W), F32),
        scratch_shapes=[pltpu.VMEM((SUBLANES, tc), F32)],
        compiler_params=_cparams(("parallel", "parallel", "arbitrary"), VMEM_LIMIT),
    )(a2, x2)


def _lru_scan_bwd(a2, h, dh, d, nct, tm, tc):
    _, B, T, W = a2.shape
    nt = T // tm
    rev = d == 1
    first = tm - 1 if rev else 0
    r8, n8 = tm // SUBLANES, T // SUBLANES

    def body(a_ref, h_ref, hp_ref, dh_ref, lam_ref, da_ref, ca, cl):
        s = pl.program_id(2)

        @pl.when(s == 0)
        def _():
            ca[...] = jnp.zeros_like(ca)
            cl[...] = jnp.zeros_like(cl)

        a = a_ref[0, 0]
        row = lax.broadcasted_iota(jnp.int32, a.shape, 0)
        if rev:
            c = jnp.where(row == 0, ca[0:1, :], pltpu.roll(a, 1, 0))
        else:
            c = jnp.where(row == tm - 1, ca[0:1, :], pltpu.roll(a, tm - 1, 0))
        C, L = _tile_scan(c, dh_ref[0], not rev, tm)
        lam = L + C * cl[0:1, :]
        lam_ref[0] = lam
        hp = jnp.where(s == nt - 1, 0.0, hp_ref[0])
        if rev:
            hprev = jnp.where(row == tm - 1, hp[0:1, :], pltpu.roll(h_ref[0], tm - 1, 0))
        else:
            hprev = jnp.where(row == 0, hp[SUBLANES - 1:SUBLANES, :], pltpu.roll(h_ref[0], 1, 0))
        da_ref[0] = lam * hprev
        ca[...] = jnp.broadcast_to(a[first:first + 1, :], ca.shape)
        cl[...] = jnp.broadcast_to(lam[first:first + 1, :], cl.shape)

    tidx = lambda s: _scan_tile_of(nt - 1 - s, rev, nct, nt)

    def hp_idx(b, j, s):
        tt = tidx(s)
        if rev:
            blk = jnp.where(tt == nt - 1, 0, jnp.minimum((tt + 1) * r8, n8 - 1))
        else:
            blk = jnp.maximum(tt * r8 - 1, 0)
        return (b, blk, j)

    tile = pl.BlockSpec((1, tm, tc), lambda b, j, s: (b, tidx(s), j))
    return pl.pallas_call(
        body, name=f"lru_scan_bwd{d}", grid=(B, W // tc, nt),
        in_specs=[pl.BlockSpec((1, 1, tm, tc), lambda b, j, s: (d, b, tidx(s), j)), tile,
                  pl.BlockSpec((1, SUBLANES, tc), hp_idx), tile],
        out_specs=(tile, tile),
        out_shape=(jax.ShapeDtypeStruct((B, T, W), F32), jax.ShapeDtypeStruct((B, T, W), F32)),
        scratch_shapes=[pltpu.VMEM((SUBLANES, tc), F32), pltpu.VMEM((SUBLANES, tc), F32)],
        compiler_params=_cparams(("parallel", "parallel", "arbitrary"), VMEM_LIMIT),
    )(a2, h, h, dh)


def _lru_out(h0, h1, pyl, ycb0, nct, tm, tc):
    B, T, W = h0.shape
    N = T - nct * tm

    def body(h0_ref, h1_ref, y_ref, o_ref):
        o_ref[0] = ((h0_ref[0] + h1_ref[0]) * _gelu(y_ref[0])).astype(BF16)

    hs = pl.BlockSpec((1, tm, tc), lambda b, t, j: (b, t + nct, j))
    return pl.pallas_call(
        body, name="lru_out", grid=(B, N // tm, W // tc),
        in_specs=[hs, hs, pl.BlockSpec((1, tm, tc), lambda b, t, j: (b, t + nct, ycb0 + j))],
        out_specs=pl.BlockSpec((1, tm, tc), lambda b, t, j: (b, t, j)),
        out_shape=jax.ShapeDtypeStruct((B, N, W), BF16),
        compiler_params=_cparams(("parallel",) * 3, VMEM_LIMIT))(h0, h1, pyl)


def _lru_out_bwd(h0, h1, pyl, ycb0, dy, nct, tm, tc):
    B, T, W = h0.shape

    def body(h0_ref, h1_ref, y_ref, dy_ref, dh_ref, dyl_ref):
        t = pl.program_id(1)

        @pl.when(t < nct)
        def _():
            dh_ref[0] = jnp.zeros_like(dh_ref[0])
            dyl_ref[0] = jnp.zeros_like(dyl_ref[0])

        @pl.when(t >= nct)
        def _():
            y = y_ref[0]
            dy_ = dy_ref[0]
            dh_ref[0] = dy_ * _gelu(y)
            dyl_ref[0] = (dy_ * (h0_ref[0] + h1_ref[0]) * _dgelu(y)).astype(BF16)

    hs = pl.BlockSpec((1, tm, tc), lambda b, t, j: (b, t, j))
    return pl.pallas_call(
        body, name="lru_out_bwd", grid=(B, T // tm, W // tc),
        in_specs=[hs, hs, pl.BlockSpec((1, tm, tc), lambda b, t, j: (b, t, ycb0 + j)),
                  pl.BlockSpec((1, tm, tc), lambda b, t, j: (b, jnp.maximum(t - nct, 0), j))],
        out_specs=(hs, hs),
        out_shape=(jax.ShapeDtypeStruct((B, T, W), F32), jax.ShapeDtypeStruct((B, T, W), BF16)),
        compiler_params=_cparams(("parallel",) * 3, VMEM_LIMIT))(h0, h1, pyl, dy)


def _merge(bd, bl, pmg, bm, nct, tm):
    B, N, D = bd.shape

    def body(bd_ref, bl_ref, m0_ref, m1_ref, b_ref, o_ref):
        g0 = _sigmoid(m0_ref[0] + b_ref[:, 0:D])
        g1 = _sigmoid(m1_ref[0] + b_ref[:, D:2 * D])
        o_ref[0] = (g0 * bd_ref[0] + g1 * bl_ref[0]).astype(BF16)

    row = pl.BlockSpec((1, tm, D), lambda b, t: (b, t, 0))
    return pl.pallas_call(
        body, name="merge", grid=(B, N // tm),
        in_specs=[row, row, pl.BlockSpec((1, tm, D), lambda b, t: (b, t + nct, 0)),
                  pl.BlockSpec((1, tm, D), lambda b, t: (b, t + nct, 1)),
                  pl.BlockSpec((1, 2 * D), lambda b, t: (0, 0))],
        out_specs=row, out_shape=jax.ShapeDtypeStruct((B, N, D), BF16),
        compiler_params=_cparams(("parallel", "parallel"), VMEM_LIMIT))(bd, bl, pmg, pmg, bm)


def _merge_bwd(dmi, bd, bl, pmg, bm, nct, tm):
    B, N, D = bd.shape
    T = pmg.shape[1]

    def body(dm_ref, bd_ref, bl_ref, m0_ref, m1_ref, b_ref, dbd_ref, dbl_ref, dmg_ref, acc_ref):
        t = pl.program_id(1)

        @pl.when(t == 0)
        def _():
            acc_ref[...] = jnp.zeros_like(acc_ref)

        @pl.when(t < nct)
        def _():
            dmg_ref[0] = jnp.zeros_like(dmg_ref[0])

        @pl.when(t >= nct)
        def _():
            dm = dm_ref[0]
            g0 = _sigmoid(m0_ref[0] + b_ref[:, 0:D])
            g1 = _sigmoid(m1_ref[0] + b_ref[:, D:2 * D])
            dbd_ref[0] = (dm * g0).astype(BF16)
            dbl_ref[0] = (dm * g1).astype(BF16)
            d0 = dm * bd_ref[0] * g0 * (1.0 - g0)
            d1 = dm * bl_ref[0] * g1 * (1.0 - g1)
            dmg_ref[0, :, 0:D] = d0.astype(BF16)
            dmg_ref[0, :, D:2 * D] = d1.astype(BF16)
            acc_ref[0, 0:1, 0:D] += jnp.sum(d0, axis=0, keepdims=True)
            acc_ref[0, 0:1, D:2 * D] += jnp.sum(d1, axis=0, keepdims=True)

    lat = pl.BlockSpec((1, tm, D), lambda b, t: (b, jnp.maximum(t - nct, 0), 0))
    return pl.pallas_call(
        body, name="merge_bwd", grid=(B, T // tm),
        in_specs=[lat, lat, lat, pl.BlockSpec((1, tm, D), lambda b, t: (b, t, 0)),
                  pl.BlockSpec((1, tm, D), lambda b, t: (b, t, 1)),
                  pl.BlockSpec((1, 2 * D), lambda b, t: (0, 0))],
        out_specs=(lat, lat, pl.BlockSpec((1, tm, 2 * D), lambda b, t: (b, t, 0)),
                   pl.BlockSpec((1, SUBLANES, 2 * D), lambda b, t: (b, 0, 0))),
        out_shape=(jax.ShapeDtypeStruct((B, N, D), BF16), jax.ShapeDtypeStruct((B, N, D), BF16),
                   jax.ShapeDtypeStruct((B, T, 2 * D), BF16), jax.ShapeDtypeStruct((B, SUBLANES, 2 * D), F32)),
        compiler_params=_cparams(("parallel", "arbitrary"), VMEM_LIMIT))(dmi, bd, bl, pmg, pmg, bm)


def _grid_taps():
    return [((di + 1) * 3 + (dj + 1), di, dj) for di in (-1, 0, 1) for dj in (-1, 0, 1)]


def _grid_views(x, n):
    pos = lax.broadcasted_iota(jnp.int32, x.shape, 0)
    gc = jnp.bitwise_and(pos, GRID_W - 1)
    cols = {0: x,
            -1: jnp.where(gc >= 1, pltpu.roll(x, 1, 0), 0.0),
            1: jnp.where(gc <= GRID_W - 2, pltpu.roll(x, n - 1, 0), 0.0)}
    views = {}
    edge = jnp.zeros((GRID_W, x.shape[1]), x.dtype)
    for dj, xc in cols.items():
        views[(0, dj)] = xc
        views[(-1, dj)] = jnp.concatenate([edge, xc[:n - GRID_W]], axis=0)
        views[(1, dj)] = jnp.concatenate([xc[GRID_W:], edge], axis=0)
    return views


def _ffn_act(F, w9, b, tc):
    B, N, C2 = F.shape
    Cf = C2 // 2
    ncb = Cf // tc

    def body(g_ref, v_ref, w_ref, b_ref, o_ref):
        xv = _grid_views(g_ref[0], N)
        conv = b_ref[...]
        for k, di, dj in _grid_taps():
            conv = conv + xv[(di, dj)] * w_ref[k:k + 1, :]
        o_ref[0] = (_gelu(conv) * v_ref[0]).astype(BF16)

    return pl.pallas_call(
        body, name="ffn_act", grid=(B, ncb),
        in_specs=[pl.BlockSpec((1, N, tc), lambda b, j: (b, 0, j)),
                  pl.BlockSpec((1, N, tc), lambda b, j: (b, 0, ncb + j)),
                  pl.BlockSpec((9, tc), lambda b, j: (0, j)),
                  pl.BlockSpec((1, tc), lambda b, j: (0, j))],
        out_specs=pl.BlockSpec((1, N, tc), lambda b, j: (b, 0, j)),
        out_shape=jax.ShapeDtypeStruct((B, N, Cf), BF16),
        compiler_params=_cparams(("parallel", "parallel"), VMEM_LIMIT))(F, F, w9, b)


def _ffn_act_bwd(F, w9, b, df, tc):
    B, N, C2 = F.shape
    Cf = C2 // 2
    ncb = Cf // tc

    def body(g_ref, v_ref, w_ref, b_ref, df_ref, dF_ref, acc_ref, dv_s):
        p = pl.program_id(2)

        @pl.when(p == 0)
        def _():
            xv = _grid_views(g_ref[0], N)
            conv = b_ref[...]
            for k, di, dj in _grid_taps():
                conv = conv + xv[(di, dj)] * w_ref[k:k + 1, :]
            df_ = df_ref[0]
            th = jnp.tanh(GELU_C * (conv + 0.044715 * conv * conv * conv))
            dv_s[...] = (df_ * (0.5 * conv * (1.0 + th))).astype(BF16)
            dgel = 0.5 * (1.0 + th) + 0.5 * conv * (1.0 - th * th) * GELU_C * (1.0 + 3.0 * 0.044715 * conv * conv)
            dc = df_ * v_ref[0] * dgel
            dcv = _grid_views(dc, N)
            dx = None
            for k, di, dj in _grid_taps():
                term = dcv[(-di, -dj)] * w_ref[k:k + 1, :]
                dx = term if dx is None else dx + term
                acc_ref[0, k:k + 1, :] = jnp.sum(dc * xv[(di, dj)], axis=0, keepdims=True)
            acc_ref[0, 9:10, :] = jnp.sum(dc, axis=0, keepdims=True)
            acc_ref[0, 10:16, :] = jnp.zeros((6, tc), F32)
            dF_ref[0] = dx.astype(BF16)

        @pl.when(p == 1)
        def _():
            dF_ref[0] = dv_s[...]

    return pl.pallas_call(
        body, name="ffn_act_bwd", grid=(B, ncb, 2),
        in_specs=[pl.BlockSpec((1, N, tc), lambda b, j, p: (b, 0, j)),
                  pl.BlockSpec((1, N, tc), lambda b, j, p: (b, 0, ncb + j)),
                  pl.BlockSpec((9, tc), lambda b, j, p: (0, j)),
                  pl.BlockSpec((1, tc), lambda b, j, p: (0, j)),
                  pl.BlockSpec((1, N, tc), lambda b, j, p: (b, 0, j))],
        out_specs=(pl.BlockSpec((1, N, tc), lambda b, j, p: (b, 0, j + p * ncb)),
                   pl.BlockSpec((1, 16, tc), lambda b, j, p: (b, 0, j))),
        out_shape=(jax.ShapeDtypeStruct((B, N, C2), BF16), jax.ShapeDtypeStruct((B, 16, Cf), F32)),
        scratch_shapes=[pltpu.VMEM((N, tc), BF16)],
        compiler_params=_cparams(("parallel", "parallel", "arbitrary"), VMEM_LIMIT))(F, F, w9, b, df)


ADAM_ROWS = 512


def _adamw(w, m, v, gparts, name):
    rows, cols = w.shape[-2:]
    P = gparts.shape[0]
    tr = _row_tile(rows, (P + 14) * 4 * (-(-cols // LANES) * LANES))
    c1 = 1.0 - ADAM_B1 ** ADAM_STEP
    c2 = 1.0 - ADAM_B2 ** ADAM_STEP

    def body(w_ref, m_ref, v_ref, g_ref, go_ref, d_ref, mo_ref, vo_ref):
        g = g_ref[0].astype(F32)
        for p in range(1, P):
            g = g + g_ref[p].astype(F32)
        m_ = ADAM_B1 * m_ref[...] + (1.0 - ADAM_B1) * g
        v_ = ADAM_B2 * v_ref[...] + (1.0 - ADAM_B2) * (g * g)
        go_ref[...] = g
        mo_ref[...] = m_
        vo_ref[...] = v_
        d_ref[...] = -ADAM_LR * ((m_ / c1) / (jnp.sqrt(v_ / c2) + ADAM_EPS) + ADAM_WD * w_ref[...])

    if w.ndim == 3:
        row = pl.BlockSpec((None, tr, cols), lambda i: (0, i, 0))
    else:
        row = pl.BlockSpec((tr, cols), lambda i: (i, 0))
    out = jax.ShapeDtypeStruct(w.shape, F32)
    return pl.pallas_call(
        body, name=name, grid=(rows // tr,),
        in_specs=[row, row, row, pl.BlockSpec((P, tr, cols), lambda i: (0, i, 0))],
        out_specs=(row, row, row, row), out_shape=(out, out, out, out),
        compiler_params=_cparams(("parallel",), VMEM_LIMIT))(w, m, v, gparts)


def _pack_rows(flat_len):
    rows = -(-flat_len // LANES)
    if rows > ADAM_ROWS:
        rows = -(-rows // ADAM_ROWS) * ADAM_ROWS
    else:
        rows = -(-rows // SUBLANES) * SUBLANES
    return rows


PACK_ALIGN = SUBLANES * LANES


def _pack(arrs, lead=()):
    pads = [(0, 0)] * len(lead)
    parts = []
    for a in arrs:
        f = a.reshape(lead + (-1,)).astype(F32)
        parts.append(jnp.pad(f, pads + [(0, -f.shape[-1] % PACK_ALIGN)]))
    flat = jnp.concatenate(parts, axis=-1)
    n = flat.shape[-1]
    rows = _pack_rows(n)
    flat = jnp.pad(flat, pads + [(0, rows * LANES - n)])
    return flat.reshape(lead + (rows, LANES))


def _unpack(buf, shapes):
    out, r = [], 0
    for s in shapes:
        n = math.prod(s)
        nr = -(-n // PACK_ALIGN) * SUBLANES
        out.append(buf[r:r + nr].reshape(-1)[:n].reshape(s))
        r += nr
    return out


def _col_shards(full, n_lead):
    C = full.shape[-1]
    x = full.reshape(full.shape[:-1] + (N_DEV, C // N_DEV))
    return jnp.moveaxis(x, -2, 0)


def _from_col_shards(g):
    x = jnp.moveaxis(g, 0, -2)
    return x.reshape(x.shape[:-2] + (x.shape[-2] * x.shape[-1],))


def kernel(x, c, ctx, c_ctx, w_ada, b_ada, g_pre_mix, g_post_mix, g_pre_ffn, g_post_ffn, w_in, b_merge, dn_conv, dn_a_log, dn_dt_bias, dn_onorm, lru_conv, lru_conv_b, lru_w_rg, lru_b_rg, lru_w_ig, lru_b_ig, lru_lambda, w_branch_dn, w_branch_lru, w_out, w_up, ffn_dw, ffn_dw_b, w_down, loss_target, m_c_ctx, m_w_ada, m_b_ada, m_g_pre_mix, m_g_post_mix, m_g_pre_ffn, m_g_post_ffn, m_w_in, m_b_merge, m_dn_conv, m_dn_a_log, m_dn_dt_bias, m_dn_onorm, m_lru_conv, m_lru_conv_b, m_lru_w_rg, m_lru_b_rg, m_lru_w_ig, m_lru_b_ig, m_lru_lambda, m_w_branch_dn, m_w_branch_lru, m_w_out, m_w_up, m_ffn_dw, m_ffn_dw_b, m_w_down, v_c_ctx, v_w_ada, v_b_ada, v_g_pre_mix, v_g_post_mix, v_g_pre_ffn, v_g_post_ffn, v_w_in, v_b_merge, v_dn_conv, v_dn_a_log, v_dn_dt_bias, v_dn_onorm, v_lru_conv, v_lru_conv_b, v_lru_w_rg, v_lru_b_rg, v_lru_w_ig, v_lru_b_ig, v_lru_lambda, v_w_branch_dn, v_w_branch_lru, v_w_out, v_w_up, v_ffn_dw, v_ffn_dw_b, v_w_down):
    params = dict(c_ctx=c_ctx, w_ada=w_ada, b_ada=b_ada, g_pre_mix=g_pre_mix, g_post_mix=g_post_mix, g_pre_ffn=g_pre_ffn, g_post_ffn=g_post_ffn, w_in=w_in, b_merge=b_merge, dn_conv=dn_conv, dn_a_log=dn_a_log, dn_dt_bias=dn_dt_bias, dn_onorm=dn_onorm, lru_conv=lru_conv, lru_conv_b=lru_conv_b, lru_w_rg=lru_w_rg, lru_b_rg=lru_b_rg, lru_w_ig=lru_w_ig, lru_b_ig=lru_b_ig, lru_lambda=lru_lambda, w_branch_dn=w_branch_dn, w_branch_lru=w_branch_lru, w_out=w_out, w_up=w_up, ffn_dw=ffn_dw, ffn_dw_b=ffn_dw_b, w_down=w_down)
    mom1 = dict(c_ctx=m_c_ctx, w_ada=m_w_ada, b_ada=m_b_ada, g_pre_mix=m_g_pre_mix, g_post_mix=m_g_post_mix, g_pre_ffn=m_g_pre_ffn, g_post_ffn=m_g_post_ffn, w_in=m_w_in, b_merge=m_b_merge, dn_conv=m_dn_conv, dn_a_log=m_dn_a_log, dn_dt_bias=m_dn_dt_bias, dn_onorm=m_dn_onorm, lru_conv=m_lru_conv, lru_conv_b=m_lru_conv_b, lru_w_rg=m_lru_w_rg, lru_b_rg=m_lru_b_rg, lru_w_ig=m_lru_w_ig, lru_b_ig=m_lru_b_ig, lru_lambda=m_lru_lambda, w_branch_dn=m_w_branch_dn, w_branch_lru=m_w_branch_lru, w_out=m_w_out, w_up=m_w_up, ffn_dw=m_ffn_dw, ffn_dw_b=m_ffn_dw_b, w_down=m_w_down)
    mom2 = dict(c_ctx=v_c_ctx, w_ada=v_w_ada, b_ada=v_b_ada, g_pre_mix=v_g_pre_mix, g_post_mix=v_g_post_mix, g_pre_ffn=v_g_pre_ffn, g_post_ffn=v_g_post_ffn, w_in=v_w_in, b_merge=v_b_merge, dn_conv=v_dn_conv, dn_a_log=v_dn_a_log, dn_dt_bias=v_dn_dt_bias, dn_onorm=v_dn_onorm, lru_conv=v_lru_conv, lru_conv_b=v_lru_conv_b, lru_w_rg=v_lru_w_rg, lru_b_rg=v_lru_b_rg, lru_w_ig=v_lru_w_ig, lru_b_ig=v_lru_b_ig, lru_lambda=v_lru_lambda, w_branch_dn=v_w_branch_dn, w_branch_lru=v_w_branch_lru, w_out=v_w_out, w_up=v_w_up, ffn_dw=v_ffn_dw, ffn_dw_b=v_ffn_dw_b, w_down=v_w_down)
    names = list(params)

    B, N, D = x.shape
    NC = ctx.shape[1]
    T = NC + N
    H, hd = dn_a_log.shape[2], dn_onorm.shape[1]
    HD = H * hd
    nd = 2 * H
    W = lru_conv_b.shape[1]
    nblk, bdim = lru_w_rg.shape[2], lru_w_rg.shape[3]
    Cf = ffn_dw_b.shape[1]
    sw = w_ada.shape[2]
    tm = min(256, NC)
    nct = NC // tm
    ncc = NC // CHUNK
    nch = T // CHUNK
    R, RL = B * T, B * N
    bw = min(256, W)
    me = _my_index()
    assert NC % tm == 0 and N % tm == 0 and B + 1 <= SUBLANES and bw % bdim == 0 and D % LANES == 0

    cpad = jnp.zeros((SUBLANES, D), F32).at[:B].set(c).at[B].set(c_ctx)
    ccp = _all_gather([cpad], "ag_cond")[0].reshape(N_DEV * SUBLANES, D)
    mods_sh = _ada_fwd(ccp, w_ada[0], lax.dynamic_slice(b_ada, (0, me * sw), (1, sw)))
    lru_vecs = jnp.concatenate([lru_b_rg[0], lru_b_ig[0], lru_lambda[0], jnp.zeros_like(lru_lambda[0])], axis=0)
    mods_g, w_in_g, dn_conv_g, lru_conv_g, lru_vecs_g = _all_gather(
        [mods_sh, w_in[0].astype(BF16).T, dn_conv[0], lru_conv[0], lru_vecs], "ag_first")
    ag_rest = _split_start(
        [w_up[0].astype(BF16), w_down[0].astype(BF16), w_branch_dn[0].astype(BF16), w_branch_lru[0].astype(BF16),
         w_out[0].astype(BF16), ffn_dw[0].reshape(9, -1)], "gather", "ag_rest_start")
    mine = lax.dynamic_slice(mods_g, (0, me * SUBLANES, 0), (N_DEV, SUBLANES, sw))
    mods = jnp.moveaxis(mine, 0, 1).reshape(SUBLANES, 6, D)[:B + 1]
    mods = jnp.pad(mods, ((0, 0), (0, SUBLANES - 6), (0, 0))) + ag_rest[4][0, 0]
    dn_conv_f = _from_col_shards(dn_conv_g)
    lru_conv_f = _from_col_shards(lru_conv_g)
    lru_vecs_f = _from_col_shards(lru_vecs_g)
    lru_lam_f = lru_vecs_f[4:6]

    csh = w_in_g.shape[1]
    w_in_t = w_in_g.reshape(N_DEV * csh, D)
    o_z, o_a, o_xl = 3 * HD, 4 * HD, 4 * HD + 2 * nd
    o_yl, o_mg = o_xl + W, o_xl + 2 * W
    w_qkv, w_z = w_in_t[:o_z], w_in_t[o_z:o_a]
    w_ab = jnp.pad(w_in_t[o_a:o_xl], ((0, LANES - 2 * nd), (0, 0)))
    w_xl, w_yl, w_mg = w_in_t[o_xl:o_yl], w_in_t[o_yl:o_mg], w_in_t[o_mg:]

    def blockdiag(wg):
        per = bw // bdim
        g5 = wg.reshape(2, W // bw, per, bdim, bdim)
        eye = jnp.eye(per, dtype=wg.dtype)
        return jnp.einsum("dtpij,pq->dtpiqj", g5, eye).reshape(2, W // bw, bw, bw)

    bd_rg, bd_ig = blockdiag(lru_w_rg[0]), blockdiag(lru_w_ig[0])
    wbd = jnp.stack([bd_rg[0], bd_ig[0], bd_rg[1], bd_ig[1]]).astype(BF16)
    bias4 = jnp.stack([lru_vecs_f[0], lru_vecs_f[2], lru_vecs_f[1], lru_vecs_f[3]])
    alog_v = jnp.pad(dn_a_log.reshape(1, nd), ((0, 0), (0, LANES - nd)))
    dtb_v = jnp.pad(dn_dt_bias.reshape(1, nd), ((0, 0), (0, LANES - nd)))

    h0 = jnp.concatenate([ctx, x], axis=1)
    u1 = _norm_mod_fwd(h0, g_pre_mix, mods, 0, 1, nct, tm, "norm_mod1")
    u1f = u1.reshape(R, D)
    p_qkv = _mm(u1f, w_qkv, "nt", "mm_qkv").reshape(B, T, 3 * HD)
    p_z = _mm(u1f, w_z, "nt", "mm_z").reshape(B, T, HD)
    p_ab = _mm(u1f, w_ab, "nt", "mm_ab")
    p_xl = _mm(u1f, w_xl, "nt", "mm_xl").reshape(B, T, W)
    p_yl = _mm(u1f, w_yl, "nt", "mm_yl").reshape(B, T, W)
    p_mg = _mm(u1f, w_mg, "nt", "mm_mg").reshape(B, T, 2 * D)

    conv_offs = (-2, -1, 0, 1)
    tcc = _tile(HD, 1024)
    gb = _ab_act(p_ab, alog_v, dtb_v, nd, tm)
    gb3 = gb.reshape(B, T, LANES)
    cv, qkvn = _dwconv(p_qkv, 0, 3 * HD, dn_conv_f, None, conv_offs, nct, tm, HD, "dn_conv", qkv_heads=(H, hd))
    o_d0, o_d1, s_in0, s_in1, tm_d0, tm_d1 = _delta_fwd(qkvn, gb3, H, hd, ncc)
    o2 = (o_d0, o_d1)
    y_dn = _dn_out(o2, p_z, 0, dn_onorm, H, hd, nct, tm)

    tcw = _tile(W, 1024)
    xc = _dwconv(p_xl, 0, W, lru_conv_f, lru_conv_b, conv_offs, nct, tm, tcw, "lru_conv")
    tmg = max(t_ for t_ in range(SUBLANES, min(T, 768) + 1, SUBLANES) if T % t_ == 0)
    a2, inp2 = _lru_gates(xc, wbd, bias4, lru_lam_f, tmg, bw)
    hd0 = _lru_scan(a2, inp2, 0, nct, tm, tcw)
    hd1 = _lru_scan(a2, inp2, 1, nct, tm, tcw)
    y_lru = _lru_out(hd0, hd1, p_yl, 0, nct, tm, tcw)

    rest_x, rest_land = _split_wait(ag_rest, "gather", y_lru, "ag_rest_wait")
    w_up_g, w_down_g, w_bdn_g, w_blru_g, w_out_g, ffn_dw_g = [
        lax.dynamic_update_slice(land, x_[None], (me,) + (0,) * x_.ndim) for land, x_ in zip(rest_land, rest_x)]
    w_down_f = w_down_g.reshape(Cf, D)
    w_bdn_f, w_blru_f, w_out_f = w_bdn_g.reshape(HD, D), w_blru_g.reshape(W, D), w_out_g.reshape(D, D)
    ffn_dw_f = _from_col_shards(ffn_dw_g)
    bd = _mm(y_dn.reshape(RL, HD), w_bdn_f, "nn", "mm_bdn").reshape(B, N, D)
    bl = _mm(y_lru.reshape(RL, W), w_blru_f, "nn", "mm_blru").reshape(B, N, D)
    mixin = _merge(bd, bl, p_mg, b_merge, nct, tm)
    mix = _mm(mixin.reshape(RL, D), w_out_f, "nn", "mm_out").reshape(B, N, D)
    h1 = _resid_norm_fwd(x, mix, g_post_mix, mods, 2, tm, "resid_norm1")
    u2 = _norm_mod_fwd(h1, g_pre_ffn, mods, 3, 4, 0, tm, "norm_mod2")
    Fp = _mm(u2.reshape(RL, D), w_up_g, "nn", "mm_up").reshape(B, N, 2 * Cf)
    tcf = LANES
    f = _ffn_act(Fp, ffn_dw_f, ffn_dw_b, tcf)
    dproj = _mm(f.reshape(RL, Cf), w_down_f, "nn", "mm_down").reshape(B, N, D)

    dd, dh2, acc_f = _final_fwd_bwd(h1, dproj, g_post_ffn, mods, 5, loss_target, tm)
    loss = lax.psum((0.5 / D) * jnp.sum(acc_f[:, 2, :]), MESH_AXES)
    ddf = dd.reshape(RL, D)
    df = _mm(ddf, w_down_f, "nt", "mm_down_dx").reshape(B, N, Cf)
    gw_down = _mm(f.reshape(RL, Cf), ddf, "tn", "mm_down_dw", out_dtype=BF16)
    dF, acc_ffn = _ffn_act_bwd(Fp, ffn_dw_f, ffn_dw_b, df, tcf)
    dFf = dF.reshape(RL, 2 * Cf)
    du2 = _mm(dFf, w_up_g, "nt", "mm_up_dx").reshape(B, N, D)
    gw_up = _mm(u2.reshape(RL, D), dFf, "tn", "mm_up_dw", out_dtype=BF16, out_shards=N_DEV)
    jm = 2 * lax.axis_index("x") + lax.axis_index("y")

    def pair_sums(parts8, tag):
        sib4 = _pair_exchange(parts8, "rs_pair_" + tag)
        return [_pair_sum(p8, s4, f"rs_pair_sum_{tag}{i}") for i, (p8, s4) in enumerate(zip(parts8, sib4))]

    def own_slab(land, x4):
        idx = (jm,) + (0,) * (x4.ndim - 1)
        return lax.dynamic_update_slice(land, lax.dynamic_slice(x4, idx, (1,) + x4.shape[1:]), idx)

    q_early = _split_start(pair_sums([gw_up, gw_down.reshape(N_DEV, Cf // N_DEV, D)], "ffn"), "quad", "rs_quad_ffn_start")
    mods_b = mods + q_early[4][0, 0]
    dh1, acc2x, _ = _norm_mod_bwd(h1, du2, g_pre_ffn, mods_b, 4, 0, tm, dh2, "norm_mod2_bwd")
    dmix, acc_r1 = _resid_norm_bwd(mix, dh1, g_post_mix, mods, 2, tm, "resid_norm1_bwd")
    dmixf = dmix.reshape(RL, D)
    dmixin = _mm(dmixf, w_out_f, "nt", "mm_out_dx").reshape(B, N, D)
    gw_out = _mm(mixin.reshape(RL, D), dmixf, "tn", "mm_out_dw", out_dtype=BF16)
    dbd, dbl, dmg, acc_mg = _merge_bwd(dmixin, bd, bl, p_mg, b_merge, nct, tm)
    dy_dn = _mm(dbd.reshape(RL, D), w_bdn_f, "nt", "mm_bdn_dx").reshape(B, N, HD)
    gw_bdn = _mm(y_dn.reshape(RL, HD), dbd.reshape(RL, D), "tn", "mm_bdn_dw", out_dtype=BF16)
    dy_lru = _mm(dbl.reshape(RL, D), w_blru_f, "nt", "mm_blru_dx").reshape(B, N, W)
    gw_blru = _mm(y_lru.reshape(RL, W), dbl.reshape(RL, D), "tn", "mm_blru_dw", out_dtype=BF16)

    dh, dyl = _lru_out_bwd(hd0, hd1, p_yl, 0, dy_lru, nct, tm, tcw)
    lam0, da0 = _lru_scan_bwd(a2, hd0, dh, 0, nct, tm, tcw)
    lam1, da1 = _lru_scan_bwd(a2, hd1, dh, 1, nct, tm, tcw)
    dxc, dwbd, acc_lru = _lru_gates_bwd(xc, wbd, bias4, lru_lam_f, (lam0, lam1), (da0, da1), tmg, bw)
    dxl = _dwconv(dxc, 0, W, lru_conv_f, None, tuple(-o for o in conv_offs), nct, tm, tcw, "lru_conv_dx", BF16)
    acc_lc = _dwconv_wgrad(dxc, p_xl, 0, W, conv_offs, nct, tm, tcw, "lru_conv_dw")

    do, dz, acc_on = _dn_out_bwd(o2, p_z, 0, dn_onorm, dy_dn, H, hd, nct, tm)
    dqkvn0, dqkvn1, dgb0, dgb1 = _delta_bwd(qkvn, gb3, (s_in0, s_in1), (tm_d0, tm_d1), do, H, hd, ncc)
    dcv = _dn_act_bwd(cv, (dqkvn0, dqkvn1), H, hd, tm)
    dqkv = _dwconv(dcv, 0, 3 * HD, dn_conv_f, None, tuple(-o for o in conv_offs), nct, tm, tcc, "dn_conv_dx", BF16)
    acc_dc = _dwconv_wgrad(dcv, p_qkv, 0, 3 * HD, conv_offs, nct, tm, tcc, "dn_conv_dw")
    dp_ab, acc_ab = _ab_act_bwd(p_ab, gb, (dgb0.reshape(R, LANES), dgb1.reshape(R, LANES)), alog_v, dtb_v, nd, tm)

    groups = [(dqkv.reshape(R, 3 * HD), w_qkv, "qkv"), (dz.reshape(R, HD), w_z, "z"), (dp_ab, w_ab, "ab"),
              (dxl.reshape(R, W), w_xl, "xl"), (dyl.reshape(R, W), w_yl, "yl"), (dmg.reshape(R, 2 * D), w_mg, "mg")]
    du1 = _mm_sum([(dg_, wg_) for dg_, wg_, _ in groups], "mm_in_dx")
    gw_groups = [_mm(dg_, u1f, "tn", "mm_in_dw_" + nm, out_dtype=BF16) for dg_, _, nm in groups]
    gw_groups[2] = gw_groups[2][:2 * nd]
    gw_in = jnp.concatenate(gw_groups, axis=0).reshape(N_DEV, csh, D)
    grad_x, acc1x, acc1c = _norm_mod_bwd(h0, du1.reshape(B, T, D), g_pre_mix, mods, 1, nct, tm, dh1, "norm_mod1_bwd")

    g_lru_conv = jnp.sum(acc_lc[:, :4], 0)
    g_dn_conv = jnp.sum(acc_dc[:, :4], 0)
    g_ffn_dw = jnp.sum(acc_ffn[:, :9], 0).reshape(3, 3, Cf)
    sm_names = ["dn_conv", "lru_conv", "lru_b_rg", "lru_b_ig", "lru_lambda", "ffn_dw"]
    sm_parts = [_col_shards(g_dn_conv, 1), _col_shards(g_lru_conv, 1),
                _col_shards(jnp.stack([acc_lru[0], acc_lru[2]]), 1), _col_shards(jnp.stack([acc_lru[1], acc_lru[3]]), 1),
                _col_shards(acc_lru[4:6], 1), _col_shards(g_ffn_dw, 2)]
    late8 = [gw_in, gw_bdn.reshape(N_DEV, HD // N_DEV, D), gw_blru.reshape(N_DEV, W // N_DEV, D),
             gw_out.reshape(N_DEV, D // N_DEV, D), _pack(sm_parts, lead=(N_DEV,))]
    q_late = _split_start(pair_sums(late8, "mix"), "quad", "rs_quad_mix_start")
    ffn_x, ffn_land = _split_wait(q_early, "quad", q_late[4], "rs_quad_ffn_wait")
    results = {}
    for n, x4, land in zip(["w_up", "w_down"], ffn_x, ffn_land):
        results[n] = list(_adamw(params[n], mom1[n], mom2[n], own_slab(land, x4), "adamw_" + n))

    zeros_d = jnp.zeros((B, D), F32)
    dm_rows = jnp.stack([acc1x[:, 0], acc1x[:, 1], acc_r1[:, 0], acc2x[:, 0], acc2x[:, 1], acc_f[:, 0]], axis=1)
    dm_ctx = jnp.stack([jnp.sum(acc1c[:, 0], 0), jnp.sum(acc1c[:, 1], 0)] + [zeros_d[0]] * 4, axis=0)[None]
    dm_pad = jnp.zeros((SUBLANES, 6 * D), F32).at[:B + 1].set(jnp.concatenate([dm_rows, dm_ctx], 0).reshape(B + 1, 6 * D))
    dm_g = _all_gather([dm_pad], "ag_dmods")[0]
    g_mine = lax.dynamic_slice(dm_g, (0, 0, me * sw), (N_DEV, SUBLANES, sw)).reshape(N_DEV * SUBLANES, sw)
    gw_ada, dcc = _ada_bwd(ccp, w_ada[0], g_mine, c_ctx.reshape(1, D), B)

    per = bw // bdim

    def diag_blocks(dwf):
        g6 = dwf.reshape(W // bw, per, bdim, per, bdim)
        return jnp.einsum("tpiqj,pq->tpij", g6, jnp.eye(per, dtype=F32)).reshape(nblk, bdim, bdim)

    g_rep = dict(
        c_ctx=dcc[0],
        g_pre_mix=jnp.sum(acc1x[:, 2] + acc1c[:, 2], 0)[None], g_post_mix=jnp.sum(acc_r1[:, 1], 0)[None],
        g_pre_ffn=jnp.sum(acc2x[:, 2], 0)[None], g_post_ffn=jnp.sum(acc_f[:, 1], 0)[None],
        b_merge=jnp.sum(acc_mg[:, 0], 0)[None],
        dn_a_log=acc_ab[0, :nd].reshape(1, 2, H), dn_dt_bias=acc_ab[1, :nd].reshape(1, 2, H),
        dn_onorm=jnp.sum(acc_on[:, 0], 0)[None],
        lru_conv_b=jnp.sum(acc_lc[:, 4], 0)[None],
        lru_w_rg=jnp.stack([diag_blocks(dwbd[0]), diag_blocks(dwbd[2])])[None],
        lru_w_ig=jnp.stack([diag_blocks(dwbd[1]), diag_blocks(dwbd[3])])[None],
        ffn_dw_b=jnp.sum(acc_ffn[:, 9], 0)[None])
    mat_names = ["lru_w_rg", "lru_w_ig"]
    vec_names = [n for n in g_rep if n not in mat_names]
    vec_parts, mat_parts = _all_gather([_pack([g_rep[n] for n in vec_names]),
                                        _pack([g_rep[n] for n in mat_names]).astype(BF16)], "ag_rep_grads")
    for grp, parts, nm in ((vec_names, vec_parts, "adamw_rep_vec"), (mat_names, mat_parts, "adamw_rep_mat")):
        out4 = _adamw(_pack([params[n] for n in grp]), _pack([mom1[n] for n in grp]),
                      _pack([mom2[n] for n in grp]), parts, nm)
        for k, buf in enumerate(out4):
            for n, arr in zip(grp, _unpack(buf, [params[n].shape for n in grp])):
                results.setdefault(n, [None] * 4)[k] = arr

    ba_out = _adamw(_pack([b_ada]), _pack([m_b_ada]), _pack([v_b_ada]),
                    _pack([dm_g.reshape(N_DEV * SUBLANES, 6 * D)], lead=(N_DEV * SUBLANES,)), "adamw_b_ada")
    results["b_ada"] = [_unpack(buf, [b_ada.shape])[0] for buf in ba_out]

    wa_out = _adamw(w_ada, m_w_ada, v_w_ada, gw_ada[None], "adamw_w_ada")
    results["w_ada"] = list(wa_out)

    mix_x, mix_land = _split_wait(q_late, "quad", wa_out[0], "rs_quad_mix_wait")
    recv4 = [own_slab(land, x4) for land, x4 in zip(mix_land, mix_x)]
    for n, g4 in zip(["w_in", "w_branch_dn", "w_branch_lru", "w_out"], recv4[:-1]):
        if n == "w_in":
            g4 = jnp.swapaxes(g4, 1, 2)
        results[n] = list(_adamw(params[n], mom1[n], mom2[n], g4, "adamw_" + n))
    sm_out = _adamw(_pack([params[n] for n in sm_names]), _pack([mom1[n] for n in sm_names]),
                    _pack([mom2[n] for n in sm_names]), recv4[-1], "adamw_small")
    for k, buf in enumerate(sm_out):
        for n, arr in zip(sm_names, _unpack(buf, [params[n].shape for n in sm_names])):
            results.setdefault(n, [None] * 4)[k] = arr

    outs = [loss, grad_x]
    for k in range(4):
        outs += [results[n][k] for n in names]
    return tuple(outs)
```

```python
import functools
import math

import jax
import jax.numpy as jnp
from jax import lax
from jax.experimental import pallas as pl
from jax.experimental.pallas import tpu as pltpu

F32 = jnp.float32
BF16 = jnp.bfloat16

EPS = 1e-6
GRID_W = 64
GRID_SHIFT = 6
CHUNK = 64
LRU_C = 8.0
N_DEV = 8
ADAM_LR = 0.001
ADAM_B1 = 0.9
ADAM_B2 = 0.999
ADAM_EPS = 1e-08
ADAM_WD = 0.01
ADAM_STEP = 10

LANES = 128
SUBLANES = 8
VMEM_LIMIT = 48 * 1024 * 1024
MESH_AXES = ("x", "y", "c")
GELU_C = math.sqrt(2.0 / math.pi)


def _cparams(sem=None, vmem=None):
    kw = {}
    if sem is not None:
        kw["dimension_semantics"] = sem
    if vmem is not None:
        kw["vmem_limit_bytes"] = vmem
    return pltpu.CompilerParams(**kw)


def _tile(n, pref):
    if n <= pref:
        return n
    t = (pref // LANES) * LANES
    while n % t:
        t -= LANES
    return t


def _sigmoid(x):
    return 1.0 / (1.0 + jnp.exp(-x))


def _silu(x):
    return x * _sigmoid(x)


def _dsilu(x):
    s = _sigmoid(x)
    return s * (1.0 + x * (1.0 - s))


def _softplus(x):
    return jnp.maximum(x, 0.0) + jnp.log(1.0 + jnp.exp(-jnp.abs(x)))


def _gelu(x):
    return 0.5 * x * (1.0 + jnp.tanh(GELU_C * (x + 0.044715 * x * x * x)))


def _dgelu(x):
    t = jnp.tanh(GELU_C * (x + 0.044715 * x * x * x))
    return 0.5 * (1.0 + t) + 0.5 * x * (1.0 - t * t) * GELU_C * (1.0 + 3.0 * 0.044715 * x * x)


def _dot(a, b, dims="nn"):
    dn = {"nn": (((1,), (0,)), ((), ())),
          "nt": (((1,), (1,)), ((), ())),
          "tn": (((0,), (0,)), ((), ()))}[dims]
    return lax.dot_general(a.astype(BF16), b.astype(BF16), dn, preferred_element_type=F32)


def _split2(x):
    hi = x.astype(BF16)
    lo = (x - hi.astype(F32)).astype(BF16)
    return hi, lo


def _split3(x):
    h1 = x.astype(BF16)
    r1 = x - h1.astype(F32)
    h2 = r1.astype(BF16)
    h3 = (r1 - h2.astype(F32)).astype(BF16)
    return h1, h2, h3


def _dot_hi(a, b):
    ah, al = _split2(a)
    bh, bl = _split2(b)
    return _dot(ah, bh) + (_dot(ah, bl) + _dot(al, bh))


def _dot_mask(m, x, dims="nn"):
    h1, h2, h3 = _split3(x)
    if dims == "xtn":
        return _dot(h1, m, "tn") + (_dot(h2, m, "tn") + _dot(h3, m, "tn"))
    return _dot(m, h1, dims) + (_dot(m, h2, dims) + _dot(m, h3, dims))


def _my_index():
    return 4 * lax.axis_index("x") + 2 * lax.axis_index("y") + lax.axis_index("c")


def _hbm_call(body, xs, out_shapes, n_sems, name):
    any_spec = pl.BlockSpec(memory_space=pl.ANY)
    return pl.pallas_call(
        body, name=name, out_shape=tuple(out_shapes),
        in_specs=[any_spec] * len(xs), out_specs=tuple([any_spec] * len(out_shapes)),
        scratch_shapes=[pltpu.SemaphoreType.DMA((n_sems,)), pltpu.SemaphoreType.DMA((n_sems,)),
                        pltpu.SemaphoreType.DMA((len(xs),))],
    )(*xs)


def _all_gather(xs, name):
    n = len(xs)

    def body(*refs):
        x_refs, out_refs = refs[:n], refs[n:2 * n]
        send_sems, recv_sems, local_sems = refs[2 * n:]
        x_, y_, c_ = lax.axis_index("x"), lax.axis_index("y"), lax.axis_index("c")
        me, sibling = (x_, y_, c_), (x_, y_, 1 - c_)
        chips = [(1 - x_, y_), (x_, 1 - y_), (1 - x_, 1 - y_)]

        def slab(a, px, py, pc):
            return out_refs[a].at[4 * px + 2 * py + pc]

        def copy(a, k, block, to, src=None):
            return pltpu.make_async_remote_copy(
                src_ref=slab(a, *block) if src is None else src, dst_ref=slab(a, *block),
                send_sem=send_sems.at[7 * a + k], recv_sem=recv_sems.at[7 * a + k],
                device_id=to, device_id_type=pl.DeviceIdType.MESH)

        mine = [pltpu.make_async_copy(x_refs[a], slab(a, *me), local_sems.at[a]) for a in range(n)]
        for cp in mine:
            cp.start()
        sent = []
        for j, chip in enumerate(chips):
            sent += [copy(a, 1 + j, me, (*chip, c_), src=x_refs[a]) for a in range(n)]
        sent += [copy(a, 0, me, sibling, src=x_refs[a]) for a in range(n)]
        for cp in sent:
            cp.start()
        for j, chip in enumerate(chips):
            for a in range(n):
                copy(a, 1 + j, (*chip, c_), me).wait_recv()
                fwd = copy(a, 4 + j, (*chip, c_), sibling)
                fwd.start()
                sent.append(fwd)
        for a in range(n):
            copy(a, 0, sibling, me).wait_recv()
        for j, chip in enumerate(chips):
            for a in range(n):
                copy(a, 4 + j, (*chip, 1 - c_), me).wait_recv()
        for cp in sent:
            cp.wait_send()
        for cp in mine:
            cp.wait()

    outs = [jax.ShapeDtypeStruct((N_DEV,) + x.shape, x.dtype) for x in xs]
    return _hbm_call(body, xs, outs, 7 * n, name)


def _pair_exchange(xs, name):
    n = len(xs)

    def body(*refs):
        x_refs, out_refs = refs[:n], refs[n:2 * n]
        send_sems, recv_sems, _ = refs[2 * n:]
        x_, y_, c_ = lax.axis_index("x"), lax.axis_index("y"), lax.axis_index("c")
        copies = []
        for a in range(n):
            for j in range(4):
                copies.append(pltpu.make_async_remote_copy(
                    src_ref=x_refs[a].at[2 * j + (1 - c_)], dst_ref=out_refs[a].at[j],
                    send_sem=send_sems.at[4 * a + j], recv_sem=recv_sems.at[4 * a + j],
                    device_id=(x_, y_, 1 - c_), device_id_type=pl.DeviceIdType.MESH))
        for cp in copies:
            cp.start()
        for cp in copies:
            cp.wait()

    outs = [jax.ShapeDtypeStruct((4,) + x.shape[1:], x.dtype) for x in xs]
    return _hbm_call(body, xs, outs, 4 * n, name)


def _split_views(kind, x_ref, land_ref, k):
    x_, y_, c_ = lax.axis_index("x"), lax.axis_index("y"), lax.axis_index("c")
    if kind == "gather":
        px = 1 - x_ if (k >> 2) & 1 else x_
        py = 1 - y_ if (k >> 1) & 1 else y_
        pc = 1 - c_ if k & 1 else c_
        return x_ref, land_ref.at[4 * x_ + 2 * y_ + c_], land_ref.at[4 * px + 2 * py + pc], (px, py, pc)
    px = 1 - x_ if (k >> 1) & 1 else x_
    py = 1 - y_ if k & 1 else y_
    return x_ref.at[2 * px + py], land_ref.at[2 * x_ + y_], land_ref.at[2 * px + py], (px, py, c_)


def _split_start(xs, kind, name):
    n = len(xs)
    npeer = 7 if kind == "gather" else 3
    lands = [lax.empty(((N_DEV,) + x.shape) if kind == "gather" else x.shape, x.dtype) for x in xs]

    def body(*refs):
        x_refs, land_refs = refs[:n], refs[n:2 * n]
        send_sems, recv_sems, token = refs[2 * n], refs[2 * n + 1], refs[4 * n + 2]
        for k in range(1, npeer + 1):
            for a in range(n):
                src, dst, _, peer = _split_views(kind, x_refs[a], land_refs[a], k)
                pltpu.make_async_remote_copy(
                    src_ref=src, dst_ref=dst, send_sem=send_sems.at[npeer * a + k - 1],
                    recv_sem=recv_sems.at[npeer * a + k - 1],
                    device_id=peer, device_id_type=pl.DeviceIdType.MESH).start()
        token[...] = jnp.zeros_like(token)

    hbm = pl.BlockSpec(memory_space=pltpu.HBM)
    sem = pl.BlockSpec(memory_space=pltpu.SEMAPHORE)
    sems = pltpu.SemaphoreType.DMA((npeer * n,))
    out = pl.pallas_call(
        body, name=name,
        out_shape=(sems, sems, *[pltpu.HBM(a.shape, a.dtype) for a in list(xs) + lands],
                   jax.ShapeDtypeStruct((SUBLANES, LANES), F32)),
        in_specs=[hbm] * (2 * n),
        out_specs=(sem, sem, *[hbm] * (2 * n), pl.BlockSpec(memory_space=pltpu.VMEM)),
        input_output_aliases={i: 2 + i for i in range(2 * n)},
        compiler_params=pltpu.CompilerParams(has_side_effects=pltpu.SideEffectType.DATAFLOW_SIDE_EFFECTING),
    )(*[pltpu.with_memory_space_constraint(a, pltpu.HBM) for a in list(xs) + lands])
    return out[0], out[1], list(out[2:2 + n]), list(out[2 + n:2 + 2 * n]), out[-1]


def _split_wait(started, kind, after, name):
    send_sems_, recv_sems_, x_thru, land_thru, _ = started
    n = len(x_thru)
    npeer = 7 if kind == "gather" else 3

    def body(*refs):
        x_refs, land_refs = refs[:n], refs[n:2 * n]
        send_sems, recv_sems = refs[2 * n], refs[2 * n + 1]
        for k in range(1, npeer + 1):
            for a in range(n):
                src, _, landed, peer = _split_views(kind, x_refs[a], land_refs[a], k)
                cp = pltpu.make_async_remote_copy(
                    src_ref=src, dst_ref=landed, send_sem=send_sems.at[npeer * a + k - 1],
                    recv_sem=recv_sems.at[npeer * a + k - 1],
                    device_id=peer, device_id_type=pl.DeviceIdType.MESH)
                cp.wait_send()
                cp.wait_recv()

    hbm = pl.BlockSpec(memory_space=pltpu.HBM)
    sem = pl.BlockSpec(memory_space=pltpu.SEMAPHORE)
    out = pl.pallas_call(
        body, name=name,
        out_shape=tuple(pltpu.HBM(a.shape, a.dtype) for a in x_thru + land_thru),
        in_specs=[hbm] * (2 * n) + [sem, sem, pl.BlockSpec(memory_space=pl.ANY)],
        out_specs=tuple([hbm] * (2 * n)),
        input_output_aliases={i: i for i in range(2 * n)},
        compiler_params=pltpu.CompilerParams(has_side_effects=pltpu.SideEffectType.DATAFLOW_SIDE_EFFECTING),
    )(*x_thru, *land_thru, send_sems_, recv_sems_, after)
    return list(out[:n]), list(out[n:])


def _pair_sum(x8, r4, name):
    _, r, c = x8.shape
    tr = _row_tile(r, c * 12)

    def body(core_ref, x_ref, r_ref, o_ref):
        o_ref[...] = (x_ref[...].astype(F32) + r_ref[...].astype(F32)).astype(o_ref.dtype)

    core = lax.axis_index("c").astype(jnp.int32).reshape(1)
    return pl.pallas_call(
        body, name=name,
        grid_spec=pltpu.PrefetchScalarGridSpec(
            num_scalar_prefetch=1, grid=(4, r // tr),
            in_specs=[pl.BlockSpec((None, tr, c), lambda j, i, core_ref: (2 * j + core_ref[0], i, 0)),
                      pl.BlockSpec((None, tr, c), lambda j, i, core_ref: (j, i, 0))],
            out_specs=pl.BlockSpec((None, tr, c), lambda j, i, core_ref: (j, i, 0))),
        out_shape=jax.ShapeDtypeStruct((4, r, c), x8.dtype),
        compiler_params=_cparams(("parallel", "parallel"), VMEM_LIMIT))(core, x8, r4)


def _row_tile(r, bytes_per_row, budget=4 * 1024 * 1024):
    best = None
    for t in range(16, r + 1, 16):
        if r % t == 0 and t * bytes_per_row <= budget:
            best = t
    return best if best is not None else r


def _mm(a, b, dims, name, out_dtype=F32, add=None, out_shards=1, tm=1024, tn=1024, tk=1024):
    S = b.shape[0] if b.ndim == 3 else 1
    bshape = (b.shape[1], S * b.shape[2]) if b.ndim == 3 else b.shape
    if dims == "nn":
        (M, K), (K2, N) = a.shape, bshape
    elif dims == "nt":
        (M, K), (N, K2) = a.shape, bshape
    else:
        (K, M), (K2, N) = a.shape, bshape
    assert K == K2, (a.shape, b.shape, dims)
    cs = bshape[1] // S
    tm, tk = _tile(M, tm), _tile(K, min(tk, cs) if (S > 1 and dims == "nt") else tk)
    tn = _tile(N, min(tn, cs) if (S > 1 and dims != "nt") else min(tn, N // out_shards))
    assert cs % (tk if dims == "nt" else tn) == 0 and (N // out_shards) % tn == 0
    nk = K // tk

    def body(*refs):
        if add is None:
            a_ref, b_ref, o_ref, acc = refs
        else:
            a_ref, b_ref, c_ref, o_ref, acc = refs
        k = pl.program_id(2)

        @pl.when(k == 0)
        def _():
            acc[...] = jnp.zeros_like(acc)

        acc[...] += _dot(a_ref[...], b_ref[...], dims)

        @pl.when(k == nk - 1)
        def _():
            r = acc[...]
            if add is not None:
                r = r + c_ref[...]
            o_ref[...] = r.astype(out_dtype)

    a_spec = (pl.BlockSpec((tk, tm), lambda i, j, k: (k, i)) if dims == "tn"
              else pl.BlockSpec((tm, tk), lambda i, j, k: (i, k)))
    if S == 1:
        b_spec = (pl.BlockSpec((tn, tk), lambda i, j, k: (j, k)) if dims == "nt"
                  else pl.BlockSpec((tk, tn), lambda i, j, k: (k, j)))
    elif dims == "nt":
        per = cs // tk
        b_spec = pl.BlockSpec((None, tn, tk), lambda i, j, k: (k // per, j, k % per))
    else:
        per = cs // tn
        b_spec = pl.BlockSpec((None, tk, tn), lambda i, j, k: (j // per, k, j % per))
    in_specs = [a_spec, b_spec]
    args = [a, b]
    if add is not None:
        in_specs.append(pl.BlockSpec((tm, tn), lambda i, j, k: (i, j)))
        args.append(add)
    if out_shards == 1:
        out_spec, out_shape = pl.BlockSpec((tm, tn), lambda i, j, k: (i, j)), (M, N)
    else:
        oper = N // out_shards // tn
        out_spec = pl.BlockSpec((None, tm, tn), lambda i, j, k: (j // oper, i, j % oper))
        out_shape = (out_shards, M, N // out_shards)
    return pl.pallas_call(
        body, name=name, grid=(M // tm, N // tn, nk),
        in_specs=in_specs, out_specs=out_spec,
        out_shape=jax.ShapeDtypeStruct(out_shape, out_dtype),
        scratch_shapes=[pltpu.VMEM((tm, tn), F32)],
        compiler_params=_cparams(("parallel", "parallel", "arbitrary"), VMEM_LIMIT),
    )(*args)


VMEM_LIMIT_SUM = 56 * 1024 * 1024


def _mm_sum(pairs, name, out_dtype=F32, tm=768, tn=1024, tk=1024):
    M, N = pairs[0][0].shape[0], pairs[0][1].shape[1]
    tm, tn = _tile(M, tm), _tile(N, tn)
    tks = [_tile(a.shape[1], tk) for a, _ in pairs]
    nks = [a.shape[1] // t for (a, _), t in zip(pairs, tks)]
    starts = [sum(nks[:g]) for g in range(len(pairs))]
    nk = sum(nks)
    G = len(pairs)

    def body(*refs):
        o_ref, acc = refs[2 * G], refs[2 * G + 1]
        k = pl.program_id(2)

        @pl.when(k == 0)
        def _():
            acc[...] = jnp.zeros_like(acc)

        for g in range(G):
            @pl.when(jnp.logical_and(k >= starts[g], k < starts[g] + nks[g]))
            def _(g=g):
                acc[...] += _dot(refs[2 * g][...], refs[2 * g + 1][...])

        @pl.when(k == nk - 1)
        def _():
            o_ref[...] = acc[...].astype(out_dtype)

    in_specs, args = [], []
    for g, (a, b) in enumerate(pairs):
        kk = lambda k, g=g: jnp.clip(k - starts[g], 0, nks[g] - 1)
        in_specs += [pl.BlockSpec((tm, tks[g]), lambda i, j, k, kk=kk: (i, kk(k))),
                     pl.BlockSpec((tks[g], tn), lambda i, j, k, kk=kk: (kk(k), j))]
        args += [a, b]
    return pl.pallas_call(
        body, name=name, grid=(M // tm, N // tn, nk),
        in_specs=in_specs, out_specs=pl.BlockSpec((tm, tn), lambda i, j, k: (i, j)),
        out_shape=jax.ShapeDtypeStruct((M, N), out_dtype),
        scratch_shapes=[pltpu.VMEM((tm, tn), F32)],
        compiler_params=_cparams(("parallel", "parallel", "arbitrary"), VMEM_LIMIT_SUM),
    )(*args)


def _ada_fwd(ccp, w, b):
    def body(c_ref, w_ref, b_ref, o_ref):
        o_ref[...] = _dot(_silu(c_ref[...]), w_ref[...]) + b_ref[...]

    return pl.pallas_call(
        body, name="ada_fwd", out_shape=jax.ShapeDtypeStruct((ccp.shape[0], w.shape[1]), F32),
        compiler_params=_cparams(None, VMEM_LIMIT))(ccp, w, b)


def _ada_bwd(ccp, w, g, cctx, n_loc):
    def body(c_ref, w_ref, g_ref, cc_ref, gw_ref, dc_ref):
        gw_ref[...] = _dot(_silu(c_ref[...]), g_ref[...], "tn")
        dcc = _dot(g_ref[...], w_ref[...], "nt")
        row = lax.broadcasted_iota(jnp.int32, dcc.shape, 0)
        part = jnp.sum(jnp.where(row % SUBLANES == n_loc, dcc, 0.0), axis=0, keepdims=True)
        dc_ref[...] = jnp.broadcast_to(part * _dsilu(cc_ref[...]), dc_ref.shape)

    D = ccp.shape[1]
    return pl.pallas_call(
        body, name="ada_bwd",
        out_shape=(jax.ShapeDtypeStruct(w.shape, F32), jax.ShapeDtypeStruct((SUBLANES, D), F32)),
        compiler_params=_cparams(None, VMEM_LIMIT))(ccp, w, g, cctx)


def _norm_mod_fwd(h, gain, mods, i_shift, i_scale, nct, tm, name):
    B, T, D = h.shape

    def body(h_ref, g_ref, m_ref, u_ref):
        x = h_ref[0]
        r = lax.rsqrt(jnp.mean(x * x, axis=-1, keepdims=True) + EPS)
        n = x * r * g_ref[...]
        u_ref[0] = (n * (1.0 + m_ref[0, i_scale:i_scale + 1, :]) + m_ref[0, i_shift:i_shift + 1, :]).astype(BF16)

    return pl.pallas_call(
        body, name=name, grid=(B, T // tm),
        in_specs=[pl.BlockSpec((1, tm, D), lambda b, t: (b, t, 0)),
                  pl.BlockSpec((1, D), lambda b, t: (0, 0)),
                  pl.BlockSpec((1, SUBLANES, D), lambda b, t: (jnp.where(t < nct, B, b), 0, 0))],
        out_specs=pl.BlockSpec((1, tm, D), lambda b, t: (b, t, 0)),
        out_shape=jax.ShapeDtypeStruct((B, T, D), BF16),
        compiler_params=_cparams(("parallel", "parallel"), VMEM_LIMIT),
    )(h, gain, mods)


def _norm_mod_bwd(h, du, gain, mods, i_scale, nct, tm, res, name):
    B, T, D = h.shape
    nt = T // tm

    def body(h_ref, du_ref, g_ref, m_ref, res_ref, dx_ref, ax_ref, ac_ref):
        t = pl.program_id(1)
        x = h_ref[0]
        r = lax.rsqrt(jnp.mean(x * x, axis=-1, keepdims=True) + EPS)
        nh = x * r
        g = g_ref[...]
        du_ = du_ref[0]
        dn = du_ * (1.0 + m_ref[0, i_scale:i_scale + 1, :])
        dnh = dn * g
        dx = r * (dnh - nh * jnp.mean(dnh * nh, axis=-1, keepdims=True))
        sums = (jnp.sum(du_, axis=0, keepdims=True), jnp.sum(du_ * nh * g, axis=0, keepdims=True),
                jnp.sum(dn * nh, axis=0, keepdims=True))

        @pl.when(t == 0)
        def _():
            ax_ref[...] = jnp.zeros_like(ax_ref)
            ac_ref[...] = jnp.zeros_like(ac_ref)

        def accumulate(ref):
            for i, s in enumerate(sums):
                ref[0, i:i + 1, :] += s

        @pl.when(t < nct)
        def _():
            accumulate(ac_ref)

        @pl.when(t >= nct)
        def _():
            accumulate(ax_ref)
            dx_ref[0] = dx + res_ref[0]

    lat = lambda b, t: (b, jnp.maximum(t - nct, 0), 0)
    acc = pl.BlockSpec((1, SUBLANES, D), lambda b, t: (b, 0, 0))
    return pl.pallas_call(
        body, name=name, grid=(B, nt),
        in_specs=[pl.BlockSpec((1, tm, D), lambda b, t: (b, t, 0)),
                  pl.BlockSpec((1, tm, D), lambda b, t: (b, t, 0)),
                  pl.BlockSpec((1, D), lambda b, t: (0, 0)),
                  pl.BlockSpec((1, SUBLANES, D), lambda b, t: (jnp.where(t < nct, B, b), 0, 0)),
                  pl.BlockSpec((1, tm, D), lat)],
        out_specs=(pl.BlockSpec((1, tm, D), lat), acc, acc),
        out_shape=(jax.ShapeDtypeStruct(res.shape, F32),
                   jax.ShapeDtypeStruct((B, SUBLANES, D), F32), jax.ShapeDtypeStruct((B, SUBLANES, D), F32)),
        compiler_params=_cparams(("parallel", "arbitrary"), VMEM_LIMIT),
    )(h, du, gain, mods, res)


def _resid_norm_fwd(x, y, gain, mods, i_gate, tm, name):
    B, N, D = x.shape

    def body(x_ref, y_ref, g_ref, m_ref, o_ref):
        y_ = y_ref[0]
        r = lax.rsqrt(jnp.mean(y_ * y_, axis=-1, keepdims=True) + EPS)
        o_ref[0] = x_ref[0] + y_ * r * g_ref[...] * m_ref[0, i_gate:i_gate + 1, :]

    row = pl.BlockSpec((1, tm, D), lambda b, t: (b, t, 0))
    return pl.pallas_call(
        body, name=name, grid=(B, N // tm),
        in_specs=[row, row, pl.BlockSpec((1, D), lambda b, t: (0, 0)),
                  pl.BlockSpec((1, SUBLANES, D), lambda b, t: (b, 0, 0))],
        out_specs=row, out_shape=jax.ShapeDtypeStruct((B, N, D), F32),
        compiler_params=_cparams(("parallel", "parallel"), VMEM_LIMIT),
    )(x, y, gain, mods)


def _resid_norm_bwd_math(y_, dh, g, gate):
    r = lax.rsqrt(jnp.mean(y_ * y_, axis=-1, keepdims=True) + EPS)
    nh = y_ * r
    dgate = jnp.sum(dh * nh * g, axis=0, keepdims=True)
    dn = dh * gate
    dgain = jnp.sum(dn * nh, axis=0, keepdims=True)
    dnh = dn * g
    dy = r * (dnh - nh * jnp.mean(dnh * nh, axis=-1, keepdims=True))
    return dy, dgate, dgain


def _resid_norm_bwd(y, dh, gain, mods, i_gate, tm, name):
    B, N, D = y.shape

    def body(y_ref, dh_ref, g_ref, m_ref, dy_ref, acc_ref):
        t = pl.program_id(1)
        dy, dgate, dgain = _resid_norm_bwd_math(y_ref[0], dh_ref[0], g_ref[...], m_ref[0, i_gate:i_gate + 1, :])
        dy_ref[0] = dy.astype(BF16)

        @pl.when(t == 0)
        def _():
            acc_ref[...] = jnp.zeros_like(acc_ref)

        acc_ref[0, 0:1, :] += dgate
        acc_ref[0, 1:2, :] += dgain

    row = pl.BlockSpec((1, tm, D), lambda b, t: (b, t, 0))
    return pl.pallas_call(
        body, name=name, grid=(B, N // tm),
        in_specs=[row, row, pl.BlockSpec((1, D), lambda b, t: (0, 0)),
                  pl.BlockSpec((1, SUBLANES, D), lambda b, t: (b, 0, 0))],
        out_specs=(row, pl.BlockSpec((1, SUBLANES, D), lambda b, t: (b, 0, 0))),
        out_shape=(jax.ShapeDtypeStruct((B, N, D), BF16), jax.ShapeDtypeStruct((B, SUBLANES, D), F32)),
        compiler_params=_cparams(("parallel", "arbitrary"), VMEM_LIMIT),
    )(y, dh, gain, mods)


def _final_fwd_bwd(h1, d, gain, mods, i_gate, tgt, tm):
    B, N, D = h1.shape

    def body(h_ref, d_ref, g_ref, m_ref, t_ref, dd_ref, dh_ref, acc_ref):
        t = pl.program_id(1)
        d_ = d_ref[0]
        g = g_ref[...]
        gate = m_ref[0, i_gate:i_gate + 1, :]
        r = lax.rsqrt(jnp.mean(d_ * d_, axis=-1, keepdims=True) + EPS)
        e = h_ref[0] + d_ * r * g * gate - t_ref[0]
        dh = e * (1.0 / D)
        dh_ref[0] = dh
        dy, dgate, dgain = _resid_norm_bwd_math(d_, dh, g, gate)
        dd_ref[0] = dy.astype(BF16)

        @pl.when(t == 0)
        def _():
            acc_ref[...] = jnp.zeros_like(acc_ref)

        acc_ref[0, 0:1, :] += dgate
        acc_ref[0, 1:2, :] += dgain
        acc_ref[0, 2:3, :] += jnp.sum(e * e, axis=0, keepdims=True)

    row = pl.BlockSpec((1, tm, D), lambda b, t: (b, t, 0))
    return pl.pallas_call(
        body, name="final_fwd_bwd", grid=(B, N // tm),
        in_specs=[row, row, pl.BlockSpec((1, D), lambda b, t: (0, 0)),
                  pl.BlockSpec((1, SUBLANES, D), lambda b, t: (b, 0, 0)), row],
        out_specs=(row, row, pl.BlockSpec((1, SUBLANES, D), lambda b, t: (b, 0, 0))),
        out_shape=(jax.ShapeDtypeStruct((B, N, D), BF16), jax.ShapeDtypeStruct((B, N, D), F32),
                   jax.ShapeDtypeStruct((B, SUBLANES, D), F32)),
        compiler_params=_cparams(("parallel", "arbitrary"), VMEM_LIMIT),
    )(h1, d, gain, mods, tgt)


def _shifted(x, prev, nxt, off, row, tm):
    if off == 0:
        return x
    if off < 0:
        y = pltpu.roll(x, -off, 0)
        for r in range(-off):
            y = jnp.where(row == r, prev[SUBLANES + r + off:SUBLANES + r + off + 1, :], y)
        return y
    y = pltpu.roll(x, tm - off, 0)
    for r in range(off):
        y = jnp.where(row == tm - off + r, nxt[r:r + 1, :], y)
    return y


def _halo_specs(T, tm, tc, cb0, order):
    r8, n8 = tm // SUBLANES, T // SUBLANES
    if order == "btj":
        prev = lambda b, t, j: (b, jnp.maximum(t * r8 - 1, 0), cb0 + j)
        nxt = lambda b, t, j: (b, jnp.minimum((t + 1) * r8, n8 - 1), cb0 + j)
    else:
        prev = lambda b, j, t: (b, jnp.maximum(t * r8 - 1, 0), cb0 + j)
        nxt = lambda b, j, t: (b, jnp.minimum((t + 1) * r8, n8 - 1), cb0 + j)
    return pl.BlockSpec((1, SUBLANES, tc), prev), pl.BlockSpec((1, SUBLANES, tc), nxt)


def _seg_halos(p_ref, n_ref, t, nct, nt):
    seg_start = jnp.logical_or(t == 0, t == nct)
    seg_end = jnp.logical_or(t == nct - 1, t == nt - 1)
    prev = jnp.where(seg_start, 0.0, p_ref[0])
    nxt = jnp.where(seg_end, 0.0, n_ref[0])
    return prev, nxt


def _dwconv(x, col0, C, w, bias, offs, nct, tm, tc, name, out_dtype=F32, qkv_heads=None):
    B, T, _ = x.shape
    nt = T // tm
    cb0 = col0 // tc
    if qkv_heads is not None:
        assert tc == qkv_heads[0] * qkv_heads[1] and C == 3 * tc

    def body(*refs):
        refs = list(refs)
        a_ref = refs.pop() if qkv_heads is not None else None
        if bias is None:
            x_ref, p_ref, n_ref, w_ref, o_ref = refs
        else:
            x_ref, p_ref, n_ref, w_ref, b_ref, o_ref = refs
        t = pl.program_id(1)
        prev, nxt = _seg_halos(p_ref, n_ref, t, nct, nt)
        x_ = x_ref[0]
        row = lax.broadcasted_iota(jnp.int32, x_.shape, 0)
        acc = None
        for j, off in enumerate(offs):
            term = _shifted(x_, prev, nxt, off, row, tm) * w_ref[j:j + 1, :]
            acc = term if acc is None else acc + term
        if bias is not None:
            acc = acc + b_ref[...]
        o_ref[0] = acc.astype(out_dtype)
        if qkv_heads is not None:
            H, hd = qkv_heads
            part = pl.program_id(2)
            for h in range(H):
                sl = slice(h * hd, (h + 1) * hd)
                s = _silu(acc[:, sl])
                rs = lax.rsqrt(jnp.sum(s * s, axis=-1, keepdims=True) + EPS)
                scale = jnp.where(part == 0, rs * (float(hd) ** -0.5), jnp.where(part == 1, rs, 1.0))
                a_ref[0, :, sl] = s * scale

    pspec, nspec = _halo_specs(T, tm, tc, cb0, "btj")
    in_specs = [pl.BlockSpec((1, tm, tc), lambda b, t, j: (b, t, cb0 + j)), pspec, nspec,
                pl.BlockSpec((w.shape[0], tc), lambda b, t, j: (0, j))]
    args = [x, x, x, w]
    if bias is not None:
        in_specs.append(pl.BlockSpec((1, tc), lambda b, t, j: (0, j)))
        args.append(bias)
    tile = pl.BlockSpec((1, tm, tc), lambda b, t, j: (b, t, j))
    out_specs, out_shape = tile, jax.ShapeDtypeStruct((B, T, C), out_dtype)
    if qkv_heads is not None:
        out_specs, out_shape = (tile, tile), (out_shape, jax.ShapeDtypeStruct((B, T, C), F32))
    return pl.pallas_call(
        body, name=name, grid=(B, nt, C // tc),
        in_specs=in_specs, out_specs=out_specs, out_shape=out_shape,
        compiler_params=_cparams(("parallel", "parallel", "parallel"), VMEM_LIMIT),
    )(*args)


def _dwconv_wgrad(dy, x, col0, C, offs, nct, tm, tc, name):
    B, T, _ = x.shape
    nt = T // tm
    cb0 = col0 // tc
    K = len(offs)

    def body(dy_ref, x_ref, p_ref, n_ref, acc_ref):
        t = pl.program_id(2)
        prev, nxt = _seg_halos(p_ref, n_ref, t, nct, nt)
        x_ = x_ref[0]
        dy_ = dy_ref[0]
        row = lax.broadcasted_iota(jnp.int32, x_.shape, 0)

        @pl.when(t == 0)
        def _():
            acc_ref[...] = jnp.zeros_like(acc_ref)

        for j, off in enumerate(offs):
            acc_ref[0, j:j + 1, :] += jnp.sum(dy_ * _shifted(x_, prev, nxt, off, row, tm), axis=0, keepdims=True)
        acc_ref[0, K:K + 1, :] += jnp.sum(dy_, axis=0, keepdims=True)

    pspec, nspec = _halo_specs(T, tm, tc, cb0, "bjt")
    return pl.pallas_call(
        body, name=name, grid=(B, C // tc, nt),
        in_specs=[pl.BlockSpec((1, tm, tc), lambda b, j, t: (b, t, j)),
                  pl.BlockSpec((1, tm, tc), lambda b, j, t: (b, t, cb0 + j)), pspec, nspec],
        out_specs=pl.BlockSpec((1, SUBLANES, tc), lambda b, j, t: (b, 0, j)),
        out_shape=jax.ShapeDtypeStruct((B, SUBLANES, C), F32),
        compiler_params=_cparams(("parallel", "parallel", "arbitrary"), VMEM_LIMIT),
    )(dy, x, x, x)


def _ab_act(pab, alog, dtb, nd, tm):
    R = pab.shape[0]

    def body(p_ref, al_ref, dt_ref, o_ref):
        p = p_ref[...]
        lane = lax.broadcasted_iota(jnp.int32, p.shape, 1)
        g = -jnp.exp(al_ref[...]) * _softplus(p + dt_ref[...])
        o_ref[...] = jnp.where(lane < nd, g, jnp.where(lane < 2 * nd, _sigmoid(p), 0.0))

    row = pl.BlockSpec((tm, LANES), lambda i: (i, 0))
    vec = pl.BlockSpec((1, LANES), lambda i: (0, 0))
    return pl.pallas_call(
        body, name="ab_act", grid=(R // tm,), in_specs=[row, vec, vec], out_specs=row,
        out_shape=jax.ShapeDtypeStruct((R, LANES), F32),
        compiler_params=_cparams(("parallel",), VMEM_LIMIT))(pab, alog, dtb)


def _ab_act_bwd(pab, gb, dgb, alog, dtb, nd, tm):
    R = pab.shape[0]

    def body(p_ref, gb_ref, d0_ref, d1_ref, al_ref, dt_ref, o_ref, acc_ref):
        i = pl.program_id(0)
        p = p_ref[...]
        gb_ = gb_ref[...]
        d_ = d0_ref[...] + d1_ref[...]
        lane = lax.broadcasted_iota(jnp.int32, p.shape, 1)
        is_g = lane < nd
        is_b = jnp.logical_and(lane >= nd, lane < 2 * nd)
        da = jnp.where(is_g, d_ * (-jnp.exp(al_ref[...])) * _sigmoid(p + dt_ref[...]), 0.0)
        db = jnp.where(is_b, d_ * gb_ * (1.0 - gb_), 0.0)
        o_ref[...] = (da + db).astype(BF16)

        @pl.when(i == 0)
        def _():
            acc_ref[...] = jnp.zeros_like(acc_ref)

        acc_ref[0:1, :] += jnp.sum(jnp.where(is_g, d_ * gb_, 0.0), axis=0, keepdims=True)
        acc_ref[1:2, :] += jnp.sum(da, axis=0, keepdims=True)

    row = pl.BlockSpec((tm, LANES), lambda i: (i, 0))
    vec = pl.BlockSpec((1, LANES), lambda i: (0, 0))
    return pl.pallas_call(
        body, name="ab_act_bwd", grid=(R // tm,),
        in_specs=[row, row, row, row, vec, vec],
        out_specs=(row, pl.BlockSpec((SUBLANES, LANES), lambda i: (0, 0))),
        out_shape=(jax.ShapeDtypeStruct((R, LANES), BF16), jax.ShapeDtypeStruct((SUBLANES, LANES), F32)),
        compiler_params=_cparams(("arbitrary",), VMEM_LIMIT))(pab, gb, dgb[0], dgb[1], alog, dtb)


def _dn_act_bwd(cv, dqkv, H, hd, tm):
    B, T, C3 = cv.shape
    qscale = float(hd) ** -0.5

    def body(c_ref, d0_ref, d1_ref, o_ref):
        part = pl.program_id(0)
        for h in range(H):
            sl = slice(h * hd, (h + 1) * hd)
            c = c_ref[0, :, sl]
            s = _silu(c)
            dout = d0_ref[0, :, sl] + d1_ref[0, :, sl]
            rs = lax.rsqrt(jnp.sum(s * s, axis=-1, keepdims=True) + EPS)
            sh = s * rs
            dnorm = rs * (dout - sh * jnp.sum(dout * sh, axis=-1, keepdims=True))
            ds = jnp.where(part == 0, dnorm * qscale, jnp.where(part == 1, dnorm, dout))
            o_ref[0, :, sl] = ds * _dsilu(c)

    spec = pl.BlockSpec((1, tm, H * hd), lambda p, b, t: (b, t, p))
    return pl.pallas_call(
        body, name="dn_act_bwd", grid=(3, B, T // tm), in_specs=[spec, spec, spec], out_specs=spec,
        out_shape=jax.ShapeDtypeStruct((B, T, C3), F32),
        compiler_params=_cparams(("parallel",) * 3, VMEM_LIMIT))(cv, dqkv[0], dqkv[1])


def _chunk_of(d, s, ncc, nch):
    if d == 0:
        return s
    return jnp.where(s < ncc, ncc - 1 - s, nch - 1 - (s - ncc))


def _delta_masks(d):
    row = lax.broadcasted_iota(jnp.int32, (CHUNK, CHUNK), 0)
    col = lax.broadcasted_iota(jnp.int32, (CHUNK, CHUNK), 1)
    sign = 1 - 2 * d
    diff = (row - col) * sign
    return diff >= 0, diff > 0, diff <= 0


def _each(f, *lists):
    return [f(*a) for a in zip(*lists)]


def _unit_tri_inverse(As):
    C = As[0].shape[0]
    row = lax.broadcasted_iota(jnp.int32, (C, C), 0)
    col = lax.broadcasted_iota(jnp.int32, (C, C), 1)
    eye = jnp.where(row == col, 1.0, 0.0).astype(F32)
    same = lambda shift: jnp.right_shift(row, shift) == jnp.right_shift(col, shift)
    in8 = same(3)
    xs = [-jnp.where(in8, a, 0.0) for a in As]
    ts = [eye + x for x in xs]
    ps = _each(lambda x: _dot(x, x), xs)
    tp = _each(lambda t, p: _dot(_rows(t, p), p), ts, ps)
    ts = _each(lambda t, m: t + m[:C], ts, tp)
    ts = _each(lambda t, m: t + _dot(t, m[C:]), ts, tp)
    shift = 3
    while (1 << shift) < C:
        off = jnp.logical_and(same(shift + 1), jnp.right_shift(row, shift) != jnp.right_shift(col, shift))
        los = [jnp.where(off, a, 0.0) for a in As]
        ts = _each(lambda t, lo: t - _dot(t, _dot(lo, t)), ts, los)
        shift += 1
    def residual(a, t):
        (ah, al), (th, tl) = _split2(eye + a), _split2(t)
        m = _dot(_rows(ah, al), th)
        return eye - (m[:C] + (m[C:] + _dot(ah, tl)))

    rs = _each(residual, As, ts)
    return _each(lambda t, r: t + _dot(t, r), ts, rs)


def _rows(*xs):
    return jnp.concatenate(xs, axis=0)


def _cols(*xs):
    return jnp.concatenate(xs, axis=1)


def _delta_chunk_fwd(q, k, v, g_i, g_j, beta_i, gl, S, incl, strict, Tm=None):
    D_ = _each(lambda gi, gj, m: jnp.where(m, jnp.exp(jnp.where(m, gi - gj, 0.0)), 0.0), g_i, g_j, incl)
    C, dk = k[0].shape
    kb = _each(lambda k_, b: k_ * b, k, beta_i)
    kq = _each(lambda kb_, q_, k_: _dot(_rows(kb_, q_), k_, "nt"), kb, q, k)
    A = _each(lambda m_, d, m: jnp.where(m, m_[:C] * d, 0.0), kq, D_, strict)
    P = _each(lambda m_, d, m: jnp.where(m, m_[C:] * d, 0.0), kq, D_, incl)
    if Tm is None:
        Tm = _unit_tri_inverse(A)
    gam = _each(jnp.exp, g_i)
    wu = _each(lambda t, kb_, g, v_, b: _dot(t, _cols(kb_ * g, v_ * b)), Tm, kb, gam, v, beta_i)
    w = [m_[:, :dk] for m_ in wu]
    u = [m_[:, dk:] for m_ in wu]
    qg = _each(lambda q_, g: q_ * g, q, gam)
    ws = _each(lambda w_, qg_, s: _dot(_rows(w_, qg_), s), w, qg, S)
    vn = _each(lambda u_, m_: u_ - m_[:C], u, ws)
    o = _each(lambda m_, p, vn_: m_[C:] + _dot(p, vn_), ws, P, vn)
    e_i = _each(lambda gl_, gi: jnp.exp(gl_ - gi), gl, g_i)
    kd = _each(lambda k_, e: k_ * e, k, e_i)
    Gam = _each(jnp.exp, gl)
    S_new = _each(lambda s, g, kd_, vn_: s * g + _dot(kd_, vn_, "tn"), S, Gam, kd, vn)
    return dict(D=D_, kb=kb, A=A, Tm=Tm, gam=gam, w=w, u=u, vn=vn, P=P, qg=qg, o=o, e=e_i, kd=kd, Gam=Gam, S_new=S_new)


def _delta_gb(gb_ref, d, incl, incl_t, H):
    gb = gb_ref[0]
    g = gb[:, d * H:(d + 1) * H]
    beta = gb[:, (2 + d) * H:(3 + d) * H]
    gc_col = _dot_mask(incl.astype(BF16), g)
    gc_row = _dot_mask(incl_t.astype(BF16), g, "xtn")
    gl = jnp.sum(g, axis=0, keepdims=True)
    return beta, gc_col, gc_row, gl


def _delta_fwd(qkvn, gb, H, hd, ncc):
    B, T, _ = qkvn.shape
    nch = T // CHUNK
    HD = H * hd
    pairs = [(d, h) for d in range(2) for h in range(H)]

    def body(q0, k0, v0, gb0, q1, k1, v1, gb1, o0, o1, sin0, sin1, tm0, tm1, S_ref):
        s = pl.program_id(1)
        q_refs, k_refs, v_refs, gb_refs = (q0, q1), (k0, k1), (v0, v1), (gb0, gb1)
        o_refs, sin_refs, tm_refs = (o0, o1), (sin0, sin1), (tm0, tm1)

        @pl.when(s == 0)
        def _():
            S_ref[...] = jnp.zeros_like(S_ref)

        masks = [_delta_masks(d) for d in range(2)]
        gbs = [_delta_gb(gb_refs[d], d, masks[d][0], masks[d][2], H) for d in range(2)]
        head = lambda ref, h: ref[0, :, h * hd:(h + 1) * hd]
        S = [S_ref[d, h] for d, h in pairs]
        r = _delta_chunk_fwd([head(q_refs[d], h) for d, h in pairs], [head(k_refs[d], h) for d, h in pairs],
                             [head(v_refs[d], h) for d, h in pairs],
                             [gbs[d][1][:, h:h + 1] for d, h in pairs], [gbs[d][2][h:h + 1, :] for d, h in pairs],
                             [gbs[d][0][:, h:h + 1] for d, h in pairs], [gbs[d][3][:, h:h + 1] for d, h in pairs],
                             S, [masks[d][0] for d, h in pairs], [masks[d][1] for d, h in pairs])
        for i, (d, h) in enumerate(pairs):
            sin_refs[d][0, 0, h] = S[i]
            tm_refs[d][0, 0, h] = r["Tm"][i]
            S_ref[d, h] = r["S_new"][i]
            o_refs[d][0, :, h * hd:(h + 1) * hd] = r["o"][i]

    def dir_specs(d):
        cidx = lambda b, s: _chunk_of(d, s, ncc, nch)
        ins = [pl.BlockSpec((1, CHUNK, HD), lambda b, s, p=p: (b, cidx(b, s), p)) for p in range(3)]
        ins.append(pl.BlockSpec((1, CHUNK, LANES), lambda b, s: (b, cidx(b, s), 0)))
        return (ins, pl.BlockSpec((1, CHUNK, HD), lambda b, s: (b, cidx(b, s), 0)),
                pl.BlockSpec((1, 1, H, hd, hd), lambda b, s: (b, cidx(b, s), 0, 0, 0)),
                pl.BlockSpec((1, 1, H, CHUNK, CHUNK), lambda b, s: (b, cidx(b, s), 0, 0, 0)))

    (in0, o_s0, sin_s0, tm_s0), (in1, o_s1, sin_s1, tm_s1) = dir_specs(0), dir_specs(1)
    o_shape = jax.ShapeDtypeStruct((B, T, HD), F32)
    sin_shape = jax.ShapeDtypeStruct((B, nch, H, hd, hd), F32)
    tm_shape = jax.ShapeDtypeStruct((B, nch, H, CHUNK, CHUNK), F32)
    return pl.pallas_call(
        body, name="delta_fwd", grid=(B, nch),
        in_specs=in0 + in1, out_specs=(o_s0, o_s1, sin_s0, sin_s1, tm_s0, tm_s1),
        out_shape=(o_shape, o_shape, sin_shape, sin_shape, tm_shape, tm_shape),
        scratch_shapes=[pltpu.VMEM((2, H, hd, hd), F32)],
        compiler_params=_cparams(("parallel", "arbitrary"), VMEM_LIMIT),
    )(qkvn, qkvn, qkvn, gb, qkvn, qkvn, qkvn, gb)


def _delta_bwd(qkvn, gb, sin, tms, do, H, hd, ncc):
    B, T, _ = qkvn.shape
    nch = T // CHUNK
    HD = H * hd
    pairs = [(d, h) for d in range(2) for h in range(H)]

    def body(q0, k0, v0, gb0, sin0, tm0, do0, q1, k1, v1, gb1, sin1, tm1, do1, dqkv0, dqkv1, dgb0, dgb1, dS_ref):
        s = pl.program_id(1)
        tm_refs = (tm0, tm1)
        q_refs, k_refs, v_refs, gb_refs = (q0, q1), (k0, k1), (v0, v1), (gb0, gb1)
        sin_refs, do_refs, dqkv_refs, dgb_refs = (sin0, sin1), (do0, do1), (dqkv0, dqkv1), (dgb0, dgb1)

        @pl.when(s == 0)
        def _():
            dS_ref[...] = jnp.zeros_like(dS_ref)

        masks = [_delta_masks(d) for d in range(2)]
        gbs = [_delta_gb(gb_refs[d], d, masks[d][0], masks[d][2], H) for d in range(2)]
        incl = [masks[d][0] for d, h in pairs]
        strict = [masks[d][1] for d, h in pairs]
        lane = lax.broadcasted_iota(jnp.int32, (1, LANES), 1)
        ones = jnp.ones((3 * CHUNK, LANES), BF16)
        head = lambda ref, h: ref[0, :, h * hd:(h + 1) * hd]
        q = [head(q_refs[d], h) for d, h in pairs]
        k = [head(k_refs[d], h) for d, h in pairs]
        v = [head(v_refs[d], h) for d, h in pairs]
        do_ = [head(do_refs[d], h) for d, h in pairs]
        beta_i = [gbs[d][0][:, h:h + 1] for d, h in pairs]
        S = [sin_refs[d][0, 0, h] for d, h in pairs]
        dSn = [dS_ref[d, h] for d, h in pairs]
        f = _delta_chunk_fwd(q, k, v, [gbs[d][1][:, h:h + 1] for d, h in pairs], [gbs[d][2][h:h + 1, :] for d, h in pairs],
                             beta_i, [gbs[d][3][:, h:h + 1] for d, h in pairs], S, incl, strict,
                             Tm=[tm_refs[d][0, 0, h] for d, h in pairs])
        rsum = lambda x: jnp.sum(x, axis=1, keepdims=True)
        dvn = _each(lambda p, d_, kd, ds: _dot(p, d_, "tn") + _dot(kd, ds), f["P"], do_, f["kd"], dSn)
        dP = _each(lambda d_, vn, m: jnp.where(m, _dot(d_, vn, "nt"), 0.0), do_, f["vn"], incl)
        dqg = _each(lambda d_, s: _dot(d_, s, "nt"), do_, S)
        dS_in = _each(lambda qg, d_, g, ds, w, dv_: _dot(qg, d_, "tn") + g * ds - _dot(w, dv_, "tn"),
                      f["qg"], do_, f["Gam"], dSn, f["w"], dvn)
        dGam = _each(lambda s, ds: jnp.sum(rsum(s * ds), axis=0, keepdims=True), S, dSn)
        dkd = _each(lambda vn, ds: _dot(vn, ds, "nt"), f["vn"], dSn)
        de = _each(lambda dkd_, k_, e: rsum(dkd_ * k_) * e, dkd, k, f["e"])
        dw = _each(lambda dv_, s: -_dot(dv_, s, "nt"), dvn, S)
        dXw = _each(lambda t, dw_: _dot(t, dw_, "tn"), f["Tm"], dw)
        dXu = _each(lambda t, dv_: _dot(t, dv_, "tn"), f["Tm"], dvn)
        dA = _each(lambda xw, w, xu, u, m: -jnp.where(m, _dot(xw, w, "nt") + _dot(xu, u, "nt"), 0.0),
                   dXw, f["w"], dXu, f["u"], strict)
        dM = _each(lambda a, d_: a * d_, dA, f["D"])
        dN = _each(lambda p, d_: p * d_, dP, f["D"])
        dkb = _each(lambda m, k_, xw, g: _dot(m, k_) + xw * g, dM, k, dXw, f["gam"])
        dq = _each(lambda dqg_, g, n, k_: dqg_ * g + _dot(n, k_), dqg, f["gam"], dN, k)
        dk = _each(lambda dkd_, e, m, kb, n, q_, dkb_, b: dkd_ * e + _dot(m, kb, "tn") + _dot(n, q_, "tn") + dkb_ * b,
                   dkd, f["e"], dM, f["kb"], dN, q, dkb, beta_i)
        dv = _each(lambda xu, b: xu * b, dXu, beta_i)
        dbeta = _each(lambda dkb_, k_, xu, v_: rsum(dkb_ * k_) + rsum(xu * v_), dkb, k, dXu, v)
        E = _each(lambda a, A_, p, P_: a * A_ + p * P_, dA, f["A"], dP, f["P"])

        def colsum(e):
            return _dot(_rows(*_split3(e)), ones, "tn")[:, 0:1]

        dg = _each(lambda e, dqg_, qg, xw, kb, g, de_: rsum(e) - colsum(e) + rsum(dqg_ * qg) + rsum(xw * kb) * g - de_,
                   E, dqg, f["qg"], dXw, f["kb"], f["gam"], de)
        dgl_h = _each(lambda de_, dg_, g: jnp.sum(de_, axis=0, keepdims=True) + dg_ * g, de, dGam, f["Gam"])
        for d in range(2):
            dgc = jnp.zeros((CHUNK, LANES), F32)
            dgl = jnp.zeros((1, LANES), F32)
            for h in range(H):
                i = d * H + h
                g_lane, b_lane = d * H + h, (2 + d) * H + h
                dgc = dgc + dg[i] * (lane == g_lane).astype(F32) + dbeta[i] * (lane == b_lane).astype(F32)
                dgl = dgl + dgl_h[i] * (lane == g_lane).astype(F32)
                dS_ref[d, h] = dS_in[i]
                dqkv_refs[d][0, :, h * hd:(h + 1) * hd] = dq[i]
                dqkv_refs[d][0, :, HD + h * hd:HD + (h + 1) * hd] = dk[i]
                dqkv_refs[d][0, :, 2 * HD + h * hd:2 * HD + (h + 1) * hd] = dv[i]
            draw = _dot_mask(masks[d][0].astype(BF16), dgc, "tn") + dgl
            dgb_refs[d][0] = jnp.where(lane < 2 * H, draw, dgc)

    def dir_specs(d):
        cidx = lambda b, s: _chunk_of(d, nch - 1 - s, ncc, nch)
        ins = [pl.BlockSpec((1, CHUNK, HD), lambda b, s, p=p: (b, cidx(b, s), p)) for p in range(3)]
        ins += [pl.BlockSpec((1, CHUNK, LANES), lambda b, s: (b, cidx(b, s), 0)),
                pl.BlockSpec((1, 1, H, hd, hd), lambda b, s: (b, cidx(b, s), 0, 0, 0)),
                pl.BlockSpec((1, 1, H, CHUNK, CHUNK), lambda b, s: (b, cidx(b, s), 0, 0, 0)),
                pl.BlockSpec((1, CHUNK, HD), lambda b, s: (b, cidx(b, s), 0))]
        return (ins, pl.BlockSpec((1, CHUNK, 3 * HD), lambda b, s: (b, cidx(b, s), 0)),
                pl.BlockSpec((1, CHUNK, LANES), lambda b, s: (b, cidx(b, s), 0)))

    (in0, dq_s0, dg_s0), (in1, dq_s1, dg_s1) = dir_specs(0), dir_specs(1)
    dq_shape = jax.ShapeDtypeStruct((B, T, 3 * HD), F32)
    dg_shape = jax.ShapeDtypeStruct((B, T, LANES), F32)
    return pl.pallas_call(
        body, name="delta_bwd", grid=(B, nch),
        in_specs=in0 + in1, out_specs=(dq_s0, dq_s1, dg_s0, dg_s1),
        out_shape=(dq_shape, dq_shape, dg_shape, dg_shape),
        scratch_shapes=[pltpu.VMEM((2, H, hd, hd), F32)],
        compiler_params=_cparams(("parallel", "arbitrary"), VMEM_LIMIT),
    )(qkvn, qkvn, qkvn, gb, sin[0], tms[0], do, qkvn, qkvn, qkvn, gb, sin[1], tms[1], do)


def _dn_out(o2, pz, zcb0, onorm, H, hd, nct, tm):
    B, T, HD = o2[0].shape
    N = T - nct * tm

    def body(o0_ref, o1_ref, z_ref, g_ref, y_ref):
        for h in range(H):
            sl = slice(h * hd, (h + 1) * hd)
            o = o0_ref[0, :, sl] + o1_ref[0, :, sl]
            r = lax.rsqrt(jnp.mean(o * o, axis=-1, keepdims=True) + EPS)
            y_ref[0, :, sl] = (o * r * g_ref[...] * _silu(z_ref[0, :, sl])).astype(BF16)

    return pl.pallas_call(
        body, name="dn_out", grid=(B, N // tm),
        in_specs=[pl.BlockSpec((1, tm, HD), lambda b, t: (b, t + nct, 0)),
                  pl.BlockSpec((1, tm, HD), lambda b, t: (b, t + nct, 0)),
                  pl.BlockSpec((1, tm, HD), lambda b, t: (b, t + nct, zcb0)),
                  pl.BlockSpec((1, hd), lambda b, t: (0, 0))],
        out_specs=pl.BlockSpec((1, tm, HD), lambda b, t: (b, t, 0)),
        out_shape=jax.ShapeDtypeStruct((B, N, HD), BF16),
        compiler_params=_cparams(("parallel",) * 2, VMEM_LIMIT))(o2[0], o2[1], pz, onorm)


def _dn_out_bwd(o2, pz, zcb0, onorm, dy, H, hd, nct, tm):
    B, T, HD = o2[0].shape
    nt = T // tm

    def body(o0_ref, o1_ref, z_ref, g_ref, dy_ref, do_ref, dz_ref, acc_ref):
        t = pl.program_id(1)

        @pl.when(t == 0)
        def _():
            acc_ref[...] = jnp.zeros_like(acc_ref)

        @pl.when(t < nct)
        def _():
            do_ref[0] = jnp.zeros_like(do_ref[0])
            dz_ref[0] = jnp.zeros_like(dz_ref[0])

        @pl.when(t >= nct)
        def _():
            g = g_ref[...]
            for h in range(H):
                sl = slice(h * hd, (h + 1) * hd)
                o = o0_ref[0, :, sl] + o1_ref[0, :, sl]
                z = z_ref[0, :, sl]
                dy_ = dy_ref[0, :, sl]
                r = lax.rsqrt(jnp.mean(o * o, axis=-1, keepdims=True) + EPS)
                oh = o * r
                dz_ref[0, :, sl] = (dy_ * oh * g * _dsilu(z)).astype(BF16)
                dn = dy_ * _silu(z)
                acc_ref[0, 0:1, :] += jnp.sum(dn * oh, axis=0, keepdims=True)
                doh = dn * g
                do_ref[0, :, sl] = r * (doh - oh * jnp.mean(doh * oh, axis=-1, keepdims=True))

    lat = lambda b, t: (b, jnp.maximum(t - nct, 0), 0)
    row = pl.BlockSpec((1, tm, HD), lambda b, t: (b, t, 0))
    return pl.pallas_call(
        body, name="dn_out_bwd", grid=(B, nt),
        in_specs=[row, row,
                  pl.BlockSpec((1, tm, HD), lambda b, t: (b, t, zcb0)),
                  pl.BlockSpec((1, hd), lambda b, t: (0, 0)),
                  pl.BlockSpec((1, tm, HD), lat)],
        out_specs=(row, row, pl.BlockSpec((1, SUBLANES, hd), lambda b, t: (b, 0, 0))),
        out_shape=(jax.ShapeDtypeStruct((B, T, HD), F32), jax.ShapeDtypeStruct((B, T, HD), BF16),
                   jax.ShapeDtypeStruct((B, SUBLANES, hd), F32)),
        compiler_params=_cparams(("parallel", "arbitrary"), VMEM_LIMIT),
    )(o2[0], o2[1], pz, onorm, dy)


def _lru_gate_math(x, wr, wi, br, bi, lam):
    r = _sigmoid(_dot(x, wr) + br)
    i = _sigmoid(_dot(x, wi) + bi)
    sp = _softplus(-lam)
    la = -LRU_C * r * sp
    a = jnp.exp(la)
    s = jnp.sqrt(-jnp.tanh(la) * (a * a + 1.0))
    return r, i, sp, a, s


def _lru_gates(xc, wbd, bias4, lam, tm, bw):
    B, T, W = xc.shape

    def body(x_ref, w_ref, b_ref, l_ref, a_ref, i_ref):
        x = x_ref[0]
        for d in range(2):
            _, i, _, a, s = _lru_gate_math(x, w_ref[2 * d, 0], w_ref[2 * d + 1, 0],
                                           b_ref[2 * d:2 * d + 1, :], b_ref[2 * d + 1:2 * d + 2, :], l_ref[d:d + 1, :])
            a_ref[d, 0] = a
            i_ref[d, 0] = s * (i * x)

    out = pl.BlockSpec((2, 1, tm, bw), lambda b, t, j: (0, b, t, j))
    return pl.pallas_call(
        body, name="lru_gates", grid=(B, T // tm, W // bw),
        in_specs=[pl.BlockSpec((1, tm, bw), lambda b, t, j: (b, t, j)),
                  pl.BlockSpec((4, 1, bw, bw), lambda b, t, j: (0, j, 0, 0)),
                  pl.BlockSpec((4, bw), lambda b, t, j: (0, j)),
                  pl.BlockSpec((2, bw), lambda b, t, j: (0, j))],
        out_specs=(out, out),
        out_shape=(jax.ShapeDtypeStruct((2, B, T, W), F32), jax.ShapeDtypeStruct((2, B, T, W), F32)),
        compiler_params=_cparams(("parallel",) * 3, VMEM_LIMIT))(xc, wbd, bias4, lam)


def _lru_gates_bwd(xc, wbd, bias4, lam, dinp, da, tm, bw):
    B, T, W = xc.shape
    nt = T // tm

    def body(x_ref, w_ref, b_ref, l_ref, di0_ref, di1_ref, da0_ref, da1_ref, dx_ref, dw_ref, acc_ref):
        di_refs, da_refs = (di0_ref, di1_ref), (da0_ref, da1_ref)
        b_ = pl.program_id(1)
        t = pl.program_id(2)

        @pl.when(jnp.logical_and(b_ == 0, t == 0))
        def _():
            dw_ref[...] = jnp.zeros_like(dw_ref)
            acc_ref[...] = jnp.zeros_like(acc_ref)

        x = x_ref[0]
        dx = jnp.zeros_like(x)
        for d in range(2):
            wr, wi = w_ref[2 * d, 0], w_ref[2 * d + 1, 0]
            lam_ = l_ref[d:d + 1, :]
            r, i, sp, a, s = _lru_gate_math(x, wr, wi, b_ref[2 * d:2 * d + 1, :], b_ref[2 * d + 1:2 * d + 2, :], lam_)
            dinp_ = di_refs[d][0]
            dix = dinp_ * s
            ds = dinp_ * i * x
            dla = da_refs[d][0] * a - ds * (a * a) / s
            dpr = dla * (-LRU_C * sp) * r * (1.0 - r)
            dpi = dix * x * i * (1.0 - i)
            dx = dx + dix * i + _dot(dpr, wr, "nt") + _dot(dpi, wi, "nt")
            dw_ref[2 * d, 0] += _dot(x, dpr, "tn")
            dw_ref[2 * d + 1, 0] += _dot(x, dpi, "tn")
            acc_ref[2 * d:2 * d + 1, :] += jnp.sum(dpr, axis=0, keepdims=True)
            acc_ref[2 * d + 1:2 * d + 2, :] += jnp.sum(dpi, axis=0, keepdims=True)
            acc_ref[4 + d:5 + d, :] += jnp.sum(dla * (-LRU_C * r), axis=0, keepdims=True) * (-_sigmoid(-lam_))
        dx_ref[0] = dx

    tile = pl.BlockSpec((1, tm, bw), lambda j, b, t: (b, t, j))
    return pl.pallas_call(
        body, name="lru_gates_bwd", grid=(W // bw, B, nt),
        in_specs=[pl.BlockSpec((1, tm, bw), lambda j, b, t: (b, t, j)),
                  pl.BlockSpec((4, 1, bw, bw), lambda j, b, t: (0, j, 0, 0)),
                  pl.BlockSpec((4, bw), lambda j, b, t: (0, j)),
                  pl.BlockSpec((2, bw), lambda j, b, t: (0, j)), tile, tile, tile, tile],
        out_specs=(pl.BlockSpec((1, tm, bw), lambda j, b, t: (b, t, j)),
                   pl.BlockSpec((4, 1, bw, bw), lambda j, b, t: (0, j, 0, 0)),
                   pl.BlockSpec((SUBLANES, bw), lambda j, b, t: (0, j))),
        out_shape=(jax.ShapeDtypeStruct((B, T, W), F32), jax.ShapeDtypeStruct(wbd.shape, F32),
                   jax.ShapeDtypeStruct((SUBLANES, W), F32)),
        compiler_params=_cparams(("parallel", "arbitrary", "arbitrary"), VMEM_LIMIT),
    )(xc, wbd, bias4, lam, dinp[0], dinp[1], da[0], da[1])


def _tile_scan(a, x, rev, tm):
    row = lax.broadcasted_iota(jnp.int32, a.shape, 0)
    s = 1
    while s < tm:
        if rev:
            a_sh, x_sh, ok = pltpu.roll(a, tm - s, 0), pltpu.roll(x, tm - s, 0), row < tm - s
        else:
            a_sh, x_sh, ok = pltpu.roll(a, s, 0), pltpu.roll(x, s, 0), row >= s
        x = jnp.where(ok, x + a * x_sh, x)
        a = jnp.where(ok, a * a_sh, a)
        s *= 2
    return a, x


def _scan_tile_of(s, rev, nct, nt):
    if not rev:
        return s
    return jnp.where(s < nct, nct - 1 - s, nt - 1 - (s - nct))


def _lru_scan(a2, x2, d, nct, tm, tc):
    _, B, T, W = a2.shape
    nt = T // tm
    rev = d == 1
    last = 0 if rev else tm - 1

    def body(a_ref, x_ref, h_ref, carry):
        s = pl.program_id(2)

        @pl.when(s == 0)
        def _():
            carry[...] = jnp.zeros_like(carry)

        A, X = _tile_scan(a_ref[0, 0], x_ref[0, 0], rev, tm)
        h = X + A * carry[0:1, :]
        h_ref[0] = h
        carry[...] = jnp.broadcast_to(h[last:last + 1, :], carry.shape)

    tidx = lambda s: _scan_tile_of(s, rev, nct, nt)
    spec = pl.BlockSpec((1, 1, tm, tc), lambda b, j, s: (d, b, tidx(s), j))
    return pl.pallas_call(
        body, name=f"lru_scan{d}", grid=(B, W // tc, nt),
        in_specs=[spec, spec], out_specs=pl.BlockSpec((1, tm, tc), lambda b, j, s: (b, tidx(s), j)),
        out_shape=jax.ShapeDtypeStruct((B, T, W), F32),
        scratch_shapes=[pltpu.VMEM((SUBLANES, tc), F32)],
        compiler_params=_cparams(("parallel", "parallel", "arbitrary"), VMEM_LIMIT),
    )(a2, x2)


def _lru_scan_bwd(a2, h, dh, d, nct, tm, tc):
    _, B, T, W = a2.shape
    nt = T // tm
    rev = d == 1
    first = tm - 1 if rev else 0
    r8, n8 = tm // SUBLANES, T // SUBLANES

    def body(a_ref, h_ref, hp_ref, dh_ref, lam_ref, da_ref, ca, cl):
        s = pl.program_id(2)

        @pl.when(s == 0)
        def _():
            ca[...] = jnp.zeros_like(ca)
            cl[...] = jnp.zeros_like(cl)

        a = a_ref[0, 0]
        row = lax.broadcasted_iota(jnp.int32, a.shape, 0)
        if rev:
            c = jnp.where(row == 0, ca[0:1, :], pltpu.roll(a, 1, 0))
        else:
            c = jnp.where(row == tm - 1, ca[0:1, :], pltpu.roll(a, tm - 1, 0))
        C, L = _tile_scan(c, dh_ref[0], not rev, tm)
        lam = L + C * cl[0:1, :]
        lam_ref[0] = lam
        hp = jnp.where(s == nt - 1, 0.0, hp_ref[0])
        if rev:
            hprev = jnp.where(row == tm - 1, hp[0:1, :], pltpu.roll(h_ref[0], tm - 1, 0))
        else:
            hprev = jnp.where(row == 0, hp[SUBLANES - 1:SUBLANES, :], pltpu.roll(h_ref[0], 1, 0))
        da_ref[0] = lam * hprev
        ca[...] = jnp.broadcast_to(a[first:first + 1, :], ca.shape)
        cl[...] = jnp.broadcast_to(lam[first:first + 1, :], cl.shape)

    tidx = lambda s: _scan_tile_of(nt - 1 - s, rev, nct, nt)

    def hp_idx(b, j, s):
        tt = tidx(s)
        if rev:
            blk = jnp.where(tt == nt - 1, 0, jnp.minimum((tt + 1) * r8, n8 - 1))
        else:
            blk = jnp.maximum(tt * r8 - 1, 0)
        return (b, blk, j)

    tile = pl.BlockSpec((1, tm, tc), lambda b, j, s: (b, tidx(s), j))
    return pl.pallas_call(
        body, name=f"lru_scan_bwd{d}", grid=(B, W // tc, nt),
        in_specs=[pl.BlockSpec((1, 1, tm, tc), lambda b, j, s: (d, b, tidx(s), j)), tile,
                  pl.BlockSpec((1, SUBLANES, tc), hp_idx), tile],
        out_specs=(tile, tile),
        out_shape=(jax.ShapeDtypeStruct((B, T, W), F32), jax.ShapeDtypeStruct((B, T, W), F32)),
        scratch_shapes=[pltpu.VMEM((SUBLANES, tc), F32), pltpu.VMEM((SUBLANES, tc), F32)],
        compiler_params=_cparams(("parallel", "parallel", "arbitrary"), VMEM_LIMIT),
    )(a2, h, h, dh)


def _lru_out(h0, h1, pyl, ycb0, nct, tm, tc):
    B, T, W = h0.shape
    N = T - nct * tm

    def body(h0_ref, h1_ref, y_ref, o_ref):
        o_ref[0] = ((h0_ref[0] + h1_ref[0]) * _gelu(y_ref[0])).astype(BF16)

    hs = pl.BlockSpec((1, tm, tc), lambda b, t, j: (b, t + nct, j))
    return pl.pallas_call(
        body, name="lru_out", grid=(B, N // tm, W // tc),
        in_specs=[hs, hs, pl.BlockSpec((1, tm, tc), lambda b, t, j: (b, t + nct, ycb0 + j))],
        out_specs=pl.BlockSpec((1, tm, tc), lambda b, t, j: (b, t, j)),
        out_shape=jax.ShapeDtypeStruct((B, N, W), BF16),
        compiler_params=_cparams(("parallel",) * 3, VMEM_LIMIT))(h0, h1, pyl)


def _lru_out_bwd(h0, h1, pyl, ycb0, dy, nct, tm, tc):
    B, T, W = h0.shape

    def body(h0_ref, h1_ref, y_ref, dy_ref, dh_ref, dyl_ref):
        t = pl.program_id(1)

        @pl.when(t < nct)
        def _():
            dh_ref[0] = jnp.zeros_like(dh_ref[0])
            dyl_ref[0] = jnp.zeros_like(dyl_ref[0])

        @pl.when(t >= nct)
        def _():
            y = y_ref[0]
            dy_ = dy_ref[0]
            dh_ref[0] = dy_ * _gelu(y)
            dyl_ref[0] = (dy_ * (h0_ref[0] + h1_ref[0]) * _dgelu(y)).astype(BF16)

    hs = pl.BlockSpec((1, tm, tc), lambda b, t, j: (b, t, j))
    return pl.pallas_call(
        body, name="lru_out_bwd", grid=(B, T // tm, W // tc),
        in_specs=[hs, hs, pl.BlockSpec((1, tm, tc), lambda b, t, j: (b, t, ycb0 + j)),
                  pl.BlockSpec((1, tm, tc), lambda b, t, j: (b, jnp.maximum(t - nct, 0), j))],
        out_specs=(hs, hs),
        out_shape=(jax.ShapeDtypeStruct((B, T, W), F32), jax.ShapeDtypeStruct((B, T, W), BF16)),
        compiler_params=_cparams(("parallel",) * 3, VMEM_LIMIT))(h0, h1, pyl, dy)


def _merge(bd, bl, pmg, bm, nct, tm):
    B, N, D = bd.shape

    def body(bd_ref, bl_ref, m0_ref, m1_ref, b_ref, o_ref):
        g0 = _sigmoid(m0_ref[0] + b_ref[:, 0:D])
        g1 = _sigmoid(m1_ref[0] + b_ref[:, D:2 * D])
        o_ref[0] = (g0 * bd_ref[0] + g1 * bl_ref[0]).astype(BF16)

    row = pl.BlockSpec((1, tm, D), lambda b, t: (b, t, 0))
    return pl.pallas_call(
        body, name="merge", grid=(B, N // tm),
        in_specs=[row, row, pl.BlockSpec((1, tm, D), lambda b, t: (b, t + nct, 0)),
                  pl.BlockSpec((1, tm, D), lambda b, t: (b, t + nct, 1)),
                  pl.BlockSpec((1, 2 * D), lambda b, t: (0, 0))],
        out_specs=row, out_shape=jax.ShapeDtypeStruct((B, N, D), BF16),
        compiler_params=_cparams(("parallel", "parallel"), VMEM_LIMIT))(bd, bl, pmg, pmg, bm)


def _merge_bwd(dmi, bd, bl, pmg, bm, nct, tm):
    B, N, D = bd.shape
    T = pmg.shape[1]

    def body(dm_ref, bd_ref, bl_ref, m0_ref, m1_ref, b_ref, dbd_ref, dbl_ref, dmg_ref, acc_ref):
        t = pl.program_id(1)

        @pl.when(t == 0)
        def _():
            acc_ref[...] = jnp.zeros_like(acc_ref)

        @pl.when(t < nct)
        def _():
            dmg_ref[0] = jnp.zeros_like(dmg_ref[0])

        @pl.when(t >= nct)
        def _():
            dm = dm_ref[0]
            g0 = _sigmoid(m0_ref[0] + b_ref[:, 0:D])
            g1 = _sigmoid(m1_ref[0] + b_ref[:, D:2 * D])
            dbd_ref[0] = (dm * g0).astype(BF16)
            dbl_ref[0] = (dm * g1).astype(BF16)
            d0 = dm * bd_ref[0] * g0 * (1.0 - g0)
            d1 = dm * bl_ref[0] * g1 * (1.0 - g1)
            dmg_ref[0, :, 0:D] = d0.astype(BF16)
            dmg_ref[0, :, D:2 * D] = d1.astype(BF16)
            acc_ref[0, 0:1, 0:D] += jnp.sum(d0, axis=0, keepdims=True)
            acc_ref[0, 0:1, D:2 * D] += jnp.sum(d1, axis=0, keepdims=True)

    lat = pl.BlockSpec((1, tm, D), lambda b, t: (b, jnp.maximum(t - nct, 0), 0))
    return pl.pallas_call(
        body, name="merge_bwd", grid=(B, T // tm),
        in_specs=[lat, lat, lat, pl.BlockSpec((1, tm, D), lambda b, t: (b, t, 0)),
                  pl.BlockSpec((1, tm, D), lambda b, t: (b, t, 1)),
                  pl.BlockSpec((1, 2 * D), lambda b, t: (0, 0))],
        out_specs=(lat, lat, pl.BlockSpec((1, tm, 2 * D), lambda b, t: (b, t, 0)),
                   pl.BlockSpec((1, SUBLANES, 2 * D), lambda b, t: (b, 0, 0))),
        out_shape=(jax.ShapeDtypeStruct((B, N, D), BF16), jax.ShapeDtypeStruct((B, N, D), BF16),
                   jax.ShapeDtypeStruct((B, T, 2 * D), BF16), jax.ShapeDtypeStruct((B, SUBLANES, 2 * D), F32)),
        compiler_params=_cparams(("parallel", "arbitrary"), VMEM_LIMIT))(dmi, bd, bl, pmg, pmg, bm)


def _grid_taps():
    return [((di + 1) * 3 + (dj + 1), di, dj) for di in (-1, 0, 1) for dj in (-1, 0, 1)]


def _grid_views(x, n):
    pos = lax.broadcasted_iota(jnp.int32, x.shape, 0)
    gc = jnp.bitwise_and(pos, GRID_W - 1)
    cols = {0: x,
            -1: jnp.where(gc >= 1, pltpu.roll(x, 1, 0), 0.0),
            1: jnp.where(gc <= GRID_W - 2, pltpu.roll(x, n - 1, 0), 0.0)}
    views = {}
    edge = jnp.zeros((GRID_W, x.shape[1]), x.dtype)
    for dj, xc in cols.items():
        views[(0, dj)] = xc
        views[(-1, dj)] = jnp.concatenate([edge, xc[:n - GRID_W]], axis=0)
        views[(1, dj)] = jnp.concatenate([xc[GRID_W:], edge], axis=0)
    return views


def _ffn_act(F, w9, b, tc):
    B, N, C2 = F.shape
    Cf = C2 // 2
    ncb = Cf // tc

    def body(g_ref, v_ref, w_ref, b_ref, o_ref):
        xv = _grid_views(g_ref[0], N)
        conv = b_ref[...]
        for k, di, dj in _grid_taps():
            conv = conv + xv[(di, dj)] * w_ref[k:k + 1, :]
        o_ref[0] = (_gelu(conv) * v_ref[0]).astype(BF16)

    return pl.pallas_call(
        body, name="ffn_act", grid=(B, ncb),
        in_specs=[pl.BlockSpec((1, N, tc), lambda b, j: (b, 0, j)),
                  pl.BlockSpec((1, N, tc), lambda b, j: (b, 0, ncb + j)),
                  pl.BlockSpec((9, tc), lambda b, j: (0, j)),
                  pl.BlockSpec((1, tc), lambda b, j: (0, j))],
        out_specs=pl.BlockSpec((1, N, tc), lambda b, j: (b, 0, j)),
        out_shape=jax.ShapeDtypeStruct((B, N, Cf), BF16),
        compiler_params=_cparams(("parallel", "parallel"), VMEM_LIMIT))(F, F, w9, b)


def _ffn_act_bwd(F, w9, b, df, tc):
    B, N, C2 = F.shape
    Cf = C2 // 2
    ncb = Cf // tc

    def body(g_ref, v_ref, w_ref, b_ref, df_ref, dF_ref, acc_ref, dv_s):
        p = pl.program_id(2)

        @pl.when(p == 0)
        def _():
            xv = _grid_views(g_ref[0], N)
            conv = b_ref[...]
            for k, di, dj in _grid_taps():
                conv = conv + xv[(di, dj)] * w_ref[k:k + 1, :]
            df_ = df_ref[0]
            th = jnp.tanh(GELU_C * (conv + 0.044715 * conv * conv * conv))
            dv_s[...] = (df_ * (0.5 * conv * (1.0 + th))).astype(BF16)
            dgel = 0.5 * (1.0 + th) + 0.5 * conv * (1.0 - th * th) * GELU_C * (1.0 + 3.0 * 0.044715 * conv * conv)
            dc = df_ * v_ref[0] * dgel
            dcv = _grid_views(dc, N)
            dx = None
            for k, di, dj in _grid_taps():
                term = dcv[(-di, -dj)] * w_ref[k:k + 1, :]
                dx = term if dx is None else dx + term
                acc_ref[0, k:k + 1, :] = jnp.sum(dc * xv[(di, dj)], axis=0, keepdims=True)
            acc_ref[0, 9:10, :] = jnp.sum(dc, axis=0, keepdims=True)
            acc_ref[0, 10:16, :] = jnp.zeros((6, tc), F32)
            dF_ref[0] = dx.astype(BF16)

        @pl.when(p == 1)
        def _():
            dF_ref[0] = dv_s[...]

    return pl.pallas_call(
        body, name="ffn_act_bwd", grid=(B, ncb, 2),
        in_specs=[pl.BlockSpec((1, N, tc), lambda b, j, p: (b, 0, j)),
                  pl.BlockSpec((1, N, tc), lambda b, j, p: (b, 0, ncb + j)),
                  pl.BlockSpec((9, tc), lambda b, j, p: (0, j)),
                  pl.BlockSpec((1, tc), lambda b, j, p: (0, j)),
                  pl.BlockSpec((1, N, tc), lambda b, j, p: (b, 0, j))],
        out_specs=(pl.BlockSpec((1, N, tc), lambda b, j, p: (b, 0, j + p * ncb)),
                   pl.BlockSpec((1, 16, tc), lambda b, j, p: (b, 0, j))),
        out_shape=(jax.ShapeDtypeStruct((B, N, C2), BF16), jax.ShapeDtypeStruct((B, 16, Cf), F32)),
        scratch_shapes=[pltpu.VMEM((N, tc), BF16)],
        compiler_params=_cparams(("parallel", "parallel", "arbitrary"), VMEM_LIMIT))(F, F, w9, b, df)


ADAM_ROWS = 512


def _adamw(w, m, v, gparts, name):
    rows, cols = w.shape[-2:]
    P = gparts.shape[0]
    tr = _row_tile(rows, (P + 14) * 4 * (-(-cols // LANES) * LANES))
    c1 = 1.0 - ADAM_B1 ** ADAM_STEP
    c2 = 1.0 - ADAM_B2 ** ADAM_STEP

    def body(w_ref, m_ref, v_ref, g_ref, go_ref, d_ref, mo_ref, vo_ref):
        g = g_ref[0].astype(F32)
        for p in range(1, P):
            g = g + g_ref[p].astype(F32)
        m_ = ADAM_B1 * m_ref[...] + (1.0 - ADAM_B1) * g
        v_ = ADAM_B2 * v_ref[...] + (1.0 - ADAM_B2) * (g * g)
        go_ref[...] = g
        mo_ref[...] = m_
        vo_ref[...] = v_
        d_ref[...] = -ADAM_LR * ((m_ / c1) / (jnp.sqrt(v_ / c2) + ADAM_EPS) + ADAM_WD * w_ref[...])

    if w.ndim == 3:
        row = pl.BlockSpec((None, tr, cols), lambda i: (0, i, 0))
    else:
        row = pl.BlockSpec((tr, cols), lambda i: (i, 0))
    out = jax.ShapeDtypeStruct(w.shape, F32)
    return pl.pallas_call(
        body, name=name, grid=(rows // tr,),
        in_specs=[row, row, row, pl.BlockSpec((P, tr, cols), lambda i: (0, i, 0))],
        out_specs=(row, row, row, row), out_shape=(out, out, out, out),
        compiler_params=_cparams(("parallel",), VMEM_LIMIT))(w, m, v, gparts)


def _pack_rows(flat_len):
    rows = -(-flat_len // LANES)
    if rows > ADAM_ROWS:
        rows = -(-rows // ADAM_ROWS) * ADAM_ROWS
    else:
        rows = -(-rows // SUBLANES) * SUBLANES
    return rows


PACK_ALIGN = SUBLANES * LANES


def _pack(arrs, lead=()):
    pads = [(0, 0)] * len(lead)
    parts = []
    for a in arrs:
        f = a.reshape(lead + (-1,)).astype(F32)
        parts.append(jnp.pad(f, pads + [(0, -f.shape[-1] % PACK_ALIGN)]))
    flat = jnp.concatenate(parts, axis=-1)
    n = flat.shape[-1]
    rows = _pack_rows(n)
    flat = jnp.pad(flat, pads + [(0, rows * LANES - n)])
    return flat.reshape(lead + (rows, LANES))


def _unpack(buf, shapes):
    out, r = [], 0
    for s in shapes:
        n = math.prod(s)
        nr = -(-n // PACK_ALIGN) * SUBLANES
        out.append(buf[r:r + nr].reshape(-1)[:n].reshape(s))
        r += nr
    return out


def _col_shards(full, n_lead):
    C = full.shape[-1]
    x = full.reshape(full.shape[:-1] + (N_DEV, C // N_DEV))
    return jnp.moveaxis(x, -2, 0)


def _from_col_shards(g):
    x = jnp.moveaxis(g, 0, -2)
    return x.reshape(x.shape[:-2] + (x.shape[-2] * x.shape[-1],))


def kernel(x, c, ctx, c_ctx, w_ada, b_ada, g_pre_mix, g_post_mix, g_pre_ffn, g_post_ffn, w_in, b_merge, dn_conv, dn_a_log, dn_dt_bias, dn_onorm, lru_conv, lru_conv_b, lru_w_rg, lru_b_rg, lru_w_ig, lru_b_ig, lru_lambda, w_branch_dn, w_branch_lru, w_out, w_up, ffn_dw, ffn_dw_b, w_down, loss_target, m_c_ctx, m_w_ada, m_b_ada, m_g_pre_mix, m_g_post_mix, m_g_pre_ffn, m_g_post_ffn, m_w_in, m_b_merge, m_dn_conv, m_dn_a_log, m_dn_dt_bias, m_dn_onorm, m_lru_conv, m_lru_conv_b, m_lru_w_rg, m_lru_b_rg, m_lru_w_ig, m_lru_b_ig, m_lru_lambda, m_w_branch_dn, m_w_branch_lru, m_w_out, m_w_up, m_ffn_dw, m_ffn_dw_b, m_w_down, v_c_ctx, v_w_ada, v_b_ada, v_g_pre_mix, v_g_post_mix, v_g_pre_ffn, v_g_post_ffn, v_w_in, v_b_merge, v_dn_conv, v_dn_a_log, v_dn_dt_bias, v_dn_onorm, v_lru_conv, v_lru_conv_b, v_lru_w_rg, v_lru_b_rg, v_lru_w_ig, v_lru_b_ig, v_lru_lambda, v_w_branch_dn, v_w_branch_lru, v_w_out, v_w_up, v_ffn_dw, v_ffn_dw_b, v_w_down):
    params = dict(c_ctx=c_ctx, w_ada=w_ada, b_ada=b_ada, g_pre_mix=g_pre_mix, g_post_mix=g_post_mix, g_pre_ffn=g_pre_ffn, g_post_ffn=g_post_ffn, w_in=w_in, b_merge=b_merge, dn_conv=dn_conv, dn_a_log=dn_a_log, dn_dt_bias=dn_dt_bias, dn_onorm=dn_onorm, lru_conv=lru_conv, lru_conv_b=lru_conv_b, lru_w_rg=lru_w_rg, lru_b_rg=lru_b_rg, lru_w_ig=lru_w_ig, lru_b_ig=lru_b_ig, lru_lambda=lru_lambda, w_branch_dn=w_branch_dn, w_branch_lru=w_branch_lru, w_out=w_out, w_up=w_up, ffn_dw=ffn_dw, ffn_dw_b=ffn_dw_b, w_down=w_down)
    mom1 = dict(c_ctx=m_c_ctx, w_ada=m_w_ada, b_ada=m_b_ada, g_pre_mix=m_g_pre_mix, g_post_mix=m_g_post_mix, g_pre_ffn=m_g_pre_ffn, g_post_ffn=m_g_post_ffn, w_in=m_w_in, b_merge=m_b_merge, dn_conv=m_dn_conv, dn_a_log=m_dn_a_log, dn_dt_bias=m_dn_dt_bias, dn_onorm=m_dn_onorm, lru_conv=m_lru_conv, lru_conv_b=m_lru_conv_b, lru_w_rg=m_lru_w_rg, lru_b_rg=m_lru_b_rg, lru_w_ig=m_lru_w_ig, lru_b_ig=m_lru_b_ig, lru_lambda=m_lru_lambda, w_branch_dn=m_w_branch_dn, w_branch_lru=m_w_branch_lru, w_out=m_w_out, w_up=m_w_up, ffn_dw=m_ffn_dw, ffn_dw_b=m_ffn_dw_b, w_down=m_w_down)
    mom2 = dict(c_ctx=v_c_ctx, w_ada=v_w_ada, b_ada=v_b_ada, g_pre_mix=v_g_pre_mix, g_post_mix=v_g_post_mix, g_pre_ffn=v_g_pre_ffn, g_post_ffn=v_g_post_ffn, w_in=v_w_in, b_merge=v_b_merge, dn_conv=v_dn_conv, dn_a_log=v_dn_a_log, dn_dt_bias=v_dn_dt_bias, dn_onorm=v_dn_onorm, lru_conv=v_lru_conv, lru_conv_b=v_lru_conv_b, lru_w_rg=v_lru_w_rg, lru_b_rg=v_lru_b_rg, lru_w_ig=v_lru_w_ig, lru_b_ig=v_lru_b_ig, lru_lambda=v_lru_lambda, w_branch_dn=v_w_branch_dn, w_branch_lru=v_w_branch_lru, w_out=v_w_out, w_up=v_w_up, ffn_dw=v_ffn_dw, ffn_dw_b=v_ffn_dw_b, w_down=v_w_down)
    names = list(params)

    B, N, D = x.shape
    NC = ctx.shape[1]
    T = NC + N
    H, hd = dn_a_log.shape[2], dn_onorm.shape[1]
    HD = H * hd
    nd = 2 * H
    W = lru_conv_b.shape[1]
    nblk, bdim = lru_w_rg.shape[2], lru_w_rg.shape[3]
    Cf = ffn_dw_b.shape[1]
    sw = w_ada.shape[2]
    tm = min(256, NC)
    nct = NC // tm
    ncc = NC // CHUNK
    nch = T // CHUNK
    R, RL = B * T, B * N
    bw = min(256, W)
    me = _my_index()
    assert NC % tm == 0 and N % tm == 0 and B + 1 <= SUBLANES and bw % bdim == 0 and D % LANES == 0

    cpad = jnp.zeros((SUBLANES, D), F32).at[:B].set(c).at[B].set(c_ctx)
    ccp = _all_gather([cpad], "ag_cond")[0].reshape(N_DEV * SUBLANES, D)
    mods_sh = _ada_fwd(ccp, w_ada[0], lax.dynamic_slice(b_ada, (0, me * sw), (1, sw)))
    lru_vecs = jnp.concatenate([lru_b_rg[0], lru_b_ig[0], lru_lambda[0], jnp.zeros_like(lru_lambda[0])], axis=0)
    mods_g, w_in_g, dn_conv_g, lru_conv_g, lru_vecs_g = _all_gather(
        [mods_sh, w_in[0].astype(BF16).T, dn_conv[0], lru_conv[0], lru_vecs], "ag_first")
    ag_mix = _split_start([w_branch_dn[0].astype(BF16), w_branch_lru[0].astype(BF16), w_out[0].astype(BF16)],
                          "gather", "ag_mix_start")
    ag_ffn = _split_start([w_up[0].astype(BF16) + ag_mix[4][0, 0].astype(BF16), w_down[0].astype(BF16),
                           ffn_dw[0].reshape(9, -1)], "gather", "ag_ffn_start")
    mine = lax.dynamic_slice(mods_g, (0, me * SUBLANES, 0), (N_DEV, SUBLANES, sw))
    mods = jnp.moveaxis(mine, 0, 1).reshape(SUBLANES, 6, D)[:B + 1]
    mods = jnp.pad(mods, ((0, 0), (0, SUBLANES - 6), (0, 0))) + ag_ffn[4][0, 0]
    dn_conv_f = _from_col_shards(dn_conv_g)
    lru_conv_f = _from_col_shards(lru_conv_g)
    lru_vecs_f = _from_col_shards(lru_vecs_g)
    lru_lam_f = lru_vecs_f[4:6]

    csh = w_in_g.shape[1]
    w_in_t = w_in_g.reshape(N_DEV * csh, D)
    o_z, o_a, o_xl = 3 * HD, 4 * HD, 4 * HD + 2 * nd
    o_yl, o_mg = o_xl + W, o_xl + 2 * W
    w_qkv, w_z = w_in_t[:o_z], w_in_t[o_z:o_a]
    w_ab = jnp.pad(w_in_t[o_a:o_xl], ((0, LANES - 2 * nd), (0, 0)))
    w_xl, w_yl, w_mg = w_in_t[o_xl:o_yl], w_in_t[o_yl:o_mg], w_in_t[o_mg:]

    def blockdiag(wg):
        per = bw // bdim
        g5 = wg.reshape(2, W // bw, per, bdim, bdim)
        eye = jnp.eye(per, dtype=wg.dtype)
        return jnp.einsum("dtpij,pq->dtpiqj", g5, eye).reshape(2, W // bw, bw, bw)

    bd_rg, bd_ig = blockdiag(lru_w_rg[0]), blockdiag(lru_w_ig[0])
    wbd = jnp.stack([bd_rg[0], bd_ig[0], bd_rg[1], bd_ig[1]]).astype(BF16)
    bias4 = jnp.stack([lru_vecs_f[0], lru_vecs_f[2], lru_vecs_f[1], lru_vecs_f[3]])
    alog_v = jnp.pad(dn_a_log.reshape(1, nd), ((0, 0), (0, LANES - nd)))
    dtb_v = jnp.pad(dn_dt_bias.reshape(1, nd), ((0, 0), (0, LANES - nd)))

    h0 = jnp.concatenate([ctx, x], axis=1)
    u1 = _norm_mod_fwd(h0, g_pre_mix, mods, 0, 1, nct, tm, "norm_mod1")
    u1f = u1.reshape(R, D)
    p_qkv = _mm(u1f, w_qkv, "nt", "mm_qkv").reshape(B, T, 3 * HD)
    p_z = _mm(u1f, w_z, "nt", "mm_z").reshape(B, T, HD)
    p_ab = _mm(u1f, w_ab, "nt", "mm_ab")
    p_xl = _mm(u1f, w_xl, "nt", "mm_xl").reshape(B, T, W)
    p_yl = _mm(u1f, w_yl, "nt", "mm_yl").reshape(B, T, W)
    p_mg = _mm(u1f, w_mg, "nt", "mm_mg").reshape(B, T, 2 * D)

    conv_offs = (-2, -1, 0, 1)
    tcc = _tile(HD, 1024)
    gb = _ab_act(p_ab, alog_v, dtb_v, nd, tm)
    gb3 = gb.reshape(B, T, LANES)
    cv, qkvn = _dwconv(p_qkv, 0, 3 * HD, dn_conv_f, None, conv_offs, nct, tm, HD, "dn_conv", qkv_heads=(H, hd))
    o_d0, o_d1, s_in0, s_in1, tm_d0, tm_d1 = _delta_fwd(qkvn, gb3, H, hd, ncc)
    o2 = (o_d0, o_d1)
    y_dn = _dn_out(o2, p_z, 0, dn_onorm, H, hd, nct, tm)

    tcw = _tile(W, 1024)
    xc = _dwconv(p_xl, 0, W, lru_conv_f, lru_conv_b, conv_offs, nct, tm, tcw, "lru_conv")
    tmg = max(t_ for t_ in range(SUBLANES, min(T, 768) + 1, SUBLANES) if T % t_ == 0)
    a2, inp2 = _lru_gates(xc, wbd, bias4, lru_lam_f, tmg, bw)
    hd0 = _lru_scan(a2, inp2, 0, nct, tm, tcw)
    hd1 = _lru_scan(a2, inp2, 1, nct, tm, tcw)
    y_lru = _lru_out(hd0, hd1, p_yl, 0, nct, tm, tcw)

    def gathered(started, after, name):
        xs_, lands_ = _split_wait(started, "gather", after, name)
        return [lax.dynamic_update_slice(land, x_[None], (me,) + (0,) * x_.ndim) for land, x_ in zip(lands_, xs_)]

    w_bdn_g, w_blru_g, w_out_g = gathered(ag_mix, y_lru, "ag_mix_wait")
    w_bdn_f, w_blru_f, w_out_f = w_bdn_g.reshape(HD, D), w_blru_g.reshape(W, D), w_out_g.reshape(D, D)
    bd = _mm(y_dn.reshape(RL, HD), w_bdn_f, "nn", "mm_bdn").reshape(B, N, D)
    bl = _mm(y_lru.reshape(RL, W), w_blru_f, "nn", "mm_blru").reshape(B, N, D)
    mixin = _merge(bd, bl, p_mg, b_merge, nct, tm)
    mix = _mm(mixin.reshape(RL, D), w_out_f, "nn", "mm_out").reshape(B, N, D)
    h1 = _resid_norm_fwd(x, mix, g_post_mix, mods, 2, tm, "resid_norm1")
    u2 = _norm_mod_fwd(h1, g_pre_ffn, mods, 3, 4, 0, tm, "norm_mod2")
    w_up_g, w_down_g, ffn_dw_g = gathered(ag_ffn, u2, "ag_ffn_wait")
    w_down_f = w_down_g.reshape(Cf, D)
    ffn_dw_f = _from_col_shards(ffn_dw_g)
    Fp = _mm(u2.reshape(RL, D), w_up_g, "nn", "mm_up").reshape(B, N, 2 * Cf)
    tcf = LANES
    f = _ffn_act(Fp, ffn_dw_f, ffn_dw_b, tcf)
    dproj = _mm(f.reshape(RL, Cf), w_down_f, "nn", "mm_down").reshape(B, N, D)

    dd, dh2, acc_f = _final_fwd_bwd(h1, dproj, g_post_ffn, mods, 5, loss_target, tm)
    loss = lax.psum((0.5 / D) * jnp.sum(acc_f[:, 2, :]), MESH_AXES)
    ddf = dd.reshape(RL, D)
    df = _mm(ddf, w_down_f, "nt", "mm_down_dx").reshape(B, N, Cf)
    gw_down = _mm(f.reshape(RL, Cf), ddf, "tn", "mm_down_dw", out_dtype=BF16)
    dF, acc_ffn = _ffn_act_bwd(Fp, ffn_dw_f, ffn_dw_b, df, tcf)
    dFf = dF.reshape(RL, 2 * Cf)
    du2 = _mm(dFf, w_up_g, "nt", "mm_up_dx").reshape(B, N, D)
    gw_up = _mm(u2.reshape(RL, D), dFf, "tn", "mm_up_dw", out_dtype=BF16, out_shards=N_DEV)
    jm = 2 * lax.axis_index("x") + lax.axis_index("y")

    def pair_sums(parts8, tag):
        sib4 = _pair_exchange(parts8, "rs_pair_" + tag)
        return [_pair_sum(p8, s4, f"rs_pair_sum_{tag}{i}") for i, (p8, s4) in enumerate(zip(parts8, sib4))]

    def own_slab(land, x4):
        idx = (jm,) + (0,) * (x4.ndim - 1)
        return lax.dynamic_update_slice(land, lax.dynamic_slice(x4, idx, (1,) + x4.shape[1:]), idx)

    q_early = _split_start(pair_sums([gw_up, gw_down.reshape(N_DEV, Cf // N_DEV, D)], "ffn"), "quad", "rs_quad_ffn_start")
    mods_b = mods + q_early[4][0, 0]
    dh1, acc2x, _ = _norm_mod_bwd(h1, du2, g_pre_ffn, mods_b, 4, 0, tm, dh2, "norm_mod2_bwd")
    dmix, acc_r1 = _resid_norm_bwd(mix, dh1, g_post_mix, mods, 2, tm, "resid_norm1_bwd")
    dmixf = dmix.reshape(RL, D)
    dmixin = _mm(dmixf, w_out_f, "nt", "mm_out_dx").reshape(B, N, D)
    gw_out = _mm(mixin.reshape(RL, D), dmixf, "tn", "mm_out_dw", out_dtype=BF16)
    dbd, dbl, dmg, acc_mg = _merge_bwd(dmixin, bd, bl, p_mg, b_merge, nct, tm)
    dy_dn = _mm(dbd.reshape(RL, D), w_bdn_f, "nt", "mm_bdn_dx").reshape(B, N, HD)
    gw_bdn = _mm(y_dn.reshape(RL, HD), dbd.reshape(RL, D), "tn", "mm_bdn_dw", out_dtype=BF16)
    dy_lru = _mm(dbl.reshape(RL, D), w_blru_f, "nt", "mm_blru_dx").reshape(B, N, W)
    gw_blru = _mm(y_lru.reshape(RL, W), dbl.reshape(RL, D), "tn", "mm_blru_dw", out_dtype=BF16)

    dh, dyl = _lru_out_bwd(hd0, hd1, p_yl, 0, dy_lru, nct, tm, tcw)
    lam0, da0 = _lru_scan_bwd(a2, hd0, dh, 0, nct, tm, tcw)
    lam1, da1 = _lru_scan_bwd(a2, hd1, dh, 1, nct, tm, tcw)
    dxc, dwbd, acc_lru = _lru_gates_bwd(xc, wbd, bias4, lru_lam_f, (lam0, lam1), (da0, da1), tmg, bw)
    dxl = _dwconv(dxc, 0, W, lru_conv_f, None, tuple(-o for o in conv_offs), nct, tm, tcw, "lru_conv_dx", BF16)
    acc_lc = _dwconv_wgrad(dxc, p_xl, 0, W, conv_offs, nct, tm, tcw, "lru_conv_dw")

    do, dz, acc_on = _dn_out_bwd(o2, p_z, 0, dn_onorm, dy_dn, H, hd, nct, tm)
    dqkvn0, dqkvn1, dgb0, dgb1 = _delta_bwd(qkvn, gb3, (s_in0, s_in1), (tm_d0, tm_d1), do, H, hd, ncc)
    dcv = _dn_act_bwd(cv, (dqkvn0, dqkvn1), H, hd, tm)
    dqkv = _dwconv(dcv, 0, 3 * HD, dn_conv_f, None, tuple(-o for o in conv_offs), nct, tm, tcc, "dn_conv_dx", BF16)
    acc_dc = _dwconv_wgrad(dcv, p_qkv, 0, 3 * HD, conv_offs, nct, tm, tcc, "dn_conv_dw")
    dp_ab, acc_ab = _ab_act_bwd(p_ab, gb, (dgb0.reshape(R, LANES), dgb1.reshape(R, LANES)), alog_v, dtb_v, nd, tm)

    groups = [(dqkv.reshape(R, 3 * HD), w_qkv, "qkv"), (dz.reshape(R, HD), w_z, "z"), (dp_ab, w_ab, "ab"),
              (dxl.reshape(R, W), w_xl, "xl"), (dyl.reshape(R, W), w_yl, "yl"), (dmg.reshape(R, 2 * D), w_mg, "mg")]
    du1 = _mm_sum([(dg_, wg_) for dg_, wg_, _ in groups], "mm_in_dx")
    gw_groups = [_mm(dg_, u1f, "tn", "mm_in_dw_" + nm, out_dtype=BF16) for dg_, _, nm in groups]
    gw_groups[2] = gw_groups[2][:2 * nd]
    gw_in = jnp.concatenate(gw_groups, axis=0).reshape(N_DEV, csh, D)
    grad_x, acc1x, acc1c = _norm_mod_bwd(h0, du1.reshape(B, T, D), g_pre_mix, mods, 1, nct, tm, dh1, "norm_mod1_bwd")

    g_lru_conv = jnp.sum(acc_lc[:, :4], 0)
    g_dn_conv = jnp.sum(acc_dc[:, :4], 0)
    g_ffn_dw = jnp.sum(acc_ffn[:, :9], 0).reshape(3, 3, Cf)
    sm_names = ["dn_conv", "lru_conv", "lru_b_rg", "lru_b_ig", "lru_lambda", "ffn_dw"]
    sm_parts = [_col_shards(g_dn_conv, 1), _col_shards(g_lru_conv, 1),
                _col_shards(jnp.stack([acc_lru[0], acc_lru[2]]), 1), _col_shards(jnp.stack([acc_lru[1], acc_lru[3]]), 1),
                _col_shards(acc_lru[4:6], 1), _col_shards(g_ffn_dw, 2)]
    late8 = [gw_in, gw_bdn.reshape(N_DEV, HD // N_DEV, D), gw_blru.reshape(N_DEV, W // N_DEV, D),
             gw_out.reshape(N_DEV, D // N_DEV, D), _pack(sm_parts, lead=(N_DEV,))]
    q_late = _split_start(pair_sums(late8, "mix"), "quad", "rs_quad_mix_start")
    ffn_x, ffn_land = _split_wait(q_early, "quad", q_late[4], "rs_quad_ffn_wait")
    results = {}
    for n, x4, land in zip(["w_up", "w_down"], ffn_x, ffn_land):
        results[n] = list(_adamw(params[n], mom1[n], mom2[n], own_slab(land, x4), "adamw_" + n))

    zeros_d = jnp.zeros((B, D), F32)
    dm_rows = jnp.stack([acc1x[:, 0], acc1x[:, 1], acc_r1[:, 0], acc2x[:, 0], acc2x[:, 1], acc_f[:, 0]], axis=1)
    dm_ctx = jnp.stack([jnp.sum(acc1c[:, 0], 0), jnp.sum(acc1c[:, 1], 0)] + [zeros_d[0]] * 4, axis=0)[None]
    dm_pad = jnp.zeros((SUBLANES, 6 * D), F32).at[:B + 1].set(jnp.concatenate([dm_rows, dm_ctx], 0).reshape(B + 1, 6 * D))
    dm_g = _all_gather([dm_pad], "ag_dmods")[0]
    g_mine = lax.dynamic_slice(dm_g, (0, 0, me * sw), (N_DEV, SUBLANES, sw)).reshape(N_DEV * SUBLANES, sw)
    gw_ada, dcc = _ada_bwd(ccp, w_ada[0], g_mine, c_ctx.reshape(1, D), B)

    per = bw // bdim

    def diag_blocks(dwf):
        g6 = dwf.reshape(W // bw, per, bdim, per, bdim)
        return jnp.einsum("tpiqj,pq->tpij", g6, jnp.eye(per, dtype=F32)).reshape(nblk, bdim, bdim)

    g_rep = dict(
        c_ctx=dcc[0],
        g_pre_mix=jnp.sum(acc1x[:, 2] + acc1c[:, 2], 0)[None], g_post_mix=jnp.sum(acc_r1[:, 1], 0)[None],
        g_pre_ffn=jnp.sum(acc2x[:, 2], 0)[None], g_post_ffn=jnp.sum(acc_f[:, 1], 0)[None],
        b_merge=jnp.sum(acc_mg[:, 0], 0)[None],
        dn_a_log=acc_ab[0, :nd].reshape(1, 2, H), dn_dt_bias=acc_ab[1, :nd].reshape(1, 2, H),
        dn_onorm=jnp.sum(acc_on[:, 0], 0)[None],
        lru_conv_b=jnp.sum(acc_lc[:, 4], 0)[None],
        lru_w_rg=jnp.stack([diag_blocks(dwbd[0]), diag_blocks(dwbd[2])])[None],
        lru_w_ig=jnp.stack([diag_blocks(dwbd[1]), diag_blocks(dwbd[3])])[None],
        ffn_dw_b=jnp.sum(acc_ffn[:, 9], 0)[None])
    mat_names = ["lru_w_rg", "lru_w_ig"]
    vec_names = [n for n in g_rep if n not in mat_names]
    vec_parts, mat_parts = _all_gather([_pack([g_rep[n] for n in vec_names]),
                                        _pack([g_rep[n] for n in mat_names]).astype(BF16)], "ag_rep_grads")
    for grp, parts, nm in ((vec_names, vec_parts, "adamw_rep_vec"), (mat_names, mat_parts, "adamw_rep_mat")):
        out4 = _adamw(_pack([params[n] for n in grp]), _pack([mom1[n] for n in grp]),
                      _pack([mom2[n] for n in grp]), parts, nm)
        for k, buf in enumerate(out4):
            for n, arr in zip(grp, _unpack(buf, [params[n].shape for n in grp])):
                results.setdefault(n, [None] * 4)[k] = arr

    ba_out = _adamw(_pack([b_ada]), _pack([m_b_ada]), _pack([v_b_ada]),
                    _pack([dm_g.reshape(N_DEV * SUBLANES, 6 * D)], lead=(N_DEV * SUBLANES,)), "adamw_b_ada")
    results["b_ada"] = [_unpack(buf, [b_ada.shape])[0] for buf in ba_out]

    wa_out = _adamw(w_ada, m_w_ada, v_w_ada, gw_ada[None], "adamw_w_ada")
    results["w_ada"] = list(wa_out)

    mix_x, mix_land = _split_wait(q_late, "quad", wa_out[0], "rs_quad_mix_wait")
    recv4 = [own_slab(land, x4) for land, x4 in zip(mix_land, mix_x)]
    for n, g4 in zip(["w_in", "w_branch_dn", "w_branch_lru", "w_out"], recv4[:-1]):
        if n == "w_in":
            g4 = jnp.swapaxes(g4, 1, 2)
        results[n] = list(_adamw(params[n], mom1[n], mom2[n], g4, "adamw_" + n))
    sm_out = _adamw(_pack([params[n] for n in sm_names]), _pack([mom1[n] for n in sm_names]),
                    _pack([mom2[n] for n in sm_names]), recv4[-1], "adamw_small")
    for k, buf in enumerate(sm_out):
        for n, arr in zip(sm_names, _unpack(buf, [params[n].shape for n in sm_names])):
            results.setdefault(n, [None] * 4)[k] = arr

    outs = [loss, grad_x]
    for k in range(4):
        outs += [results[n][k] for n in names]
    return tuple(outs)
```

```python
import functools
import math

import jax
import jax.numpy as jnp
from jax import lax
from jax.experimental import pallas as pl
from jax.experimental.pallas import tpu as pltpu

F32 = jnp.float32
BF16 = jnp.bfloat16

EPS = 1e-6
GRID_W = 64
GRID_SHIFT = 6
CHUNK = 64
LRU_C = 8.0
N_DEV = 8
ADAM_LR = 0.001
ADAM_B1 = 0.9
ADAM_B2 = 0.999
ADAM_EPS = 1e-08
ADAM_WD = 0.01
ADAM_STEP = 10

LANES = 128
SUBLANES = 8
VMEM_LIMIT = 48 * 1024 * 1024
MESH_AXES = ("x", "y", "c")
GELU_C = math.sqrt(2.0 / math.pi)


def _cparams(sem=None, vmem=None):
    kw = {}
    if sem is not None:
        kw["dimension_semantics"] = sem
    if vmem is not None:
        kw["vmem_limit_bytes"] = vmem
    return pltpu.CompilerParams(**kw)


def _tile(n, pref):
    if n <= pref:
        return n
    t = (pref // LANES) * LANES
    while n % t:
        t -= LANES
    return t


def _sigmoid(x):
    return 1.0 / (1.0 + jnp.exp(-x))


def _silu(x):
    return x * _sigmoid(x)


def _dsilu(x):
    s = _sigmoid(x)
    return s * (1.0 + x * (1.0 - s))


def _softplus(x):
    return jnp.maximum(x, 0.0) + jnp.log(1.0 + jnp.exp(-jnp.abs(x)))


def _gelu(x):
    return 0.5 * x * (1.0 + jnp.tanh(GELU_C * (x + 0.044715 * x * x * x)))


def _dgelu(x):
    t = jnp.tanh(GELU_C * (x + 0.044715 * x * x * x))
    return 0.5 * (1.0 + t) + 0.5 * x * (1.0 - t * t) * GELU_C * (1.0 + 3.0 * 0.044715 * x * x)


def _dot(a, b, dims="nn"):
    dn = {"nn": (((1,), (0,)), ((), ())),
          "nt": (((1,), (1,)), ((), ())),
          "tn": (((0,), (0,)), ((), ()))}[dims]
    return lax.dot_general(a.astype(BF16), b.astype(BF16), dn, preferred_element_type=F32)


def _split2(x):
    hi = x.astype(BF16)
    lo = (x - hi.astype(F32)).astype(BF16)
    return hi, lo


def _split3(x):
    h1 = x.astype(BF16)
    r1 = x - h1.astype(F32)
    h2 = r1.astype(BF16)
    h3 = (r1 - h2.astype(F32)).astype(BF16)
    return h1, h2, h3


def _dot_hi(a, b):
    ah, al = _split2(a)
    bh, bl = _split2(b)
    return _dot(ah, bh) + (_dot(ah, bl) + _dot(al, bh))


def _dot_mask(m, x, dims="nn"):
    h1, h2, h3 = _split3(x)
    if dims == "xtn":
        return _dot(h1, m, "tn") + (_dot(h2, m, "tn") + _dot(h3, m, "tn"))
    return _dot(m, h1, dims) + (_dot(m, h2, dims) + _dot(m, h3, dims))


def _my_index():
    return 4 * lax.axis_index("x") + 2 * lax.axis_index("y") + lax.axis_index("c")


def _hbm_call(body, xs, out_shapes, n_sems, name):
    any_spec = pl.BlockSpec(memory_space=pl.ANY)
    return pl.pallas_call(
        body, name=name, out_shape=tuple(out_shapes),
        in_specs=[any_spec] * len(xs), out_specs=tuple([any_spec] * len(out_shapes)),
        scratch_shapes=[pltpu.SemaphoreType.DMA((n_sems,)), pltpu.SemaphoreType.DMA((n_sems,)),
                        pltpu.SemaphoreType.DMA((len(xs),))],
    )(*xs)


def _all_gather(xs, name):
    n = len(xs)

    def body(*refs):
        x_refs, out_refs = refs[:n], refs[n:2 * n]
        send_sems, recv_sems, local_sems = refs[2 * n:]
        x_, y_, c_ = lax.axis_index("x"), lax.axis_index("y"), lax.axis_index("c")
        me, sibling = (x_, y_, c_), (x_, y_, 1 - c_)
        chips = [(1 - x_, y_), (x_, 1 - y_), (1 - x_, 1 - y_)]

        def slab(a, px, py, pc):
            return out_refs[a].at[4 * px + 2 * py + pc]

        def copy(a, k, block, to, src=None):
            return pltpu.make_async_remote_copy(
                src_ref=slab(a, *block) if src is None else src, dst_ref=slab(a, *block),
                send_sem=send_sems.at[7 * a + k], recv_sem=recv_sems.at[7 * a + k],
                device_id=to, device_id_type=pl.DeviceIdType.MESH)

        mine = [pltpu.make_async_copy(x_refs[a], slab(a, *me), local_sems.at[a]) for a in range(n)]
        for cp in mine:
            cp.start()
        sent = []
        for j, chip in enumerate(chips):
            sent += [copy(a, 1 + j, me, (*chip, c_), src=x_refs[a]) for a in range(n)]
        sent += [copy(a, 0, me, sibling, src=x_refs[a]) for a in range(n)]
        for cp in sent:
            cp.start()
        for j, chip in enumerate(chips):
            for a in range(n):
                copy(a, 1 + j, (*chip, c_), me).wait_recv()
                fwd = copy(a, 4 + j, (*chip, c_), sibling)
                fwd.start()
                sent.append(fwd)
        for a in range(n):
            copy(a, 0, sibling, me).wait_recv()
        for j, chip in enumerate(chips):
            for a in range(n):
                copy(a, 4 + j, (*chip, 1 - c_), me).wait_recv()
        for cp in sent:
            cp.wait_send()
        for cp in mine:
            cp.wait()

    outs = [jax.ShapeDtypeStruct((N_DEV,) + x.shape, x.dtype) for x in xs]
    return _hbm_call(body, xs, outs, 7 * n, name)


def _pair_exchange(xs, name):
    n = len(xs)

    def body(*refs):
        x_refs, out_refs = refs[:n], refs[n:2 * n]
        send_sems, recv_sems, _ = refs[2 * n:]
        x_, y_, c_ = lax.axis_index("x"), lax.axis_index("y"), lax.axis_index("c")
        copies = []
        for a in range(n):
            for j in range(4):
                copies.append(pltpu.make_async_remote_copy(
                    src_ref=x_refs[a].at[2 * j + (1 - c_)], dst_ref=out_refs[a].at[j],
                    send_sem=send_sems.at[4 * a + j], recv_sem=recv_sems.at[4 * a + j],
                    device_id=(x_, y_, 1 - c_), device_id_type=pl.DeviceIdType.MESH))
        for cp in copies:
            cp.start()
        for cp in copies:
            cp.wait()

    outs = [jax.ShapeDtypeStruct((4,) + x.shape[1:], x.dtype) for x in xs]
    return _hbm_call(body, xs, outs, 4 * n, name)


def _split_views(kind, x_ref, land_ref, k):
    x_, y_, c_ = lax.axis_index("x"), lax.axis_index("y"), lax.axis_index("c")
    if kind == "gather":
        px = 1 - x_ if (k >> 2) & 1 else x_
        py = 1 - y_ if (k >> 1) & 1 else y_
        pc = 1 - c_ if k & 1 else c_
        return x_ref, land_ref.at[4 * x_ + 2 * y_ + c_], land_ref.at[4 * px + 2 * py + pc], (px, py, pc)
    px = 1 - x_ if (k >> 1) & 1 else x_
    py = 1 - y_ if k & 1 else y_
    return x_ref.at[2 * px + py], land_ref.at[2 * x_ + y_], land_ref.at[2 * px + py], (px, py, c_)


def _split_start(xs, kind, name):
    n = len(xs)
    npeer = 7 if kind == "gather" else 3
    lands = [lax.empty(((N_DEV,) + x.shape) if kind == "gather" else x.shape, x.dtype) for x in xs]

    def body(*refs):
        x_refs, land_refs = refs[:n], refs[n:2 * n]
        send_sems, recv_sems, token = refs[2 * n], refs[2 * n + 1], refs[4 * n + 2]
        for k in range(1, npeer + 1):
            for a in range(n):
                src, dst, _, peer = _split_views(kind, x_refs[a], land_refs[a], k)
                pltpu.make_async_remote_copy(
                    src_ref=src, dst_ref=dst, send_sem=send_sems.at[npeer * a + k - 1],
                    recv_sem=recv_sems.at[npeer * a + k - 1],
                    device_id=peer, device_id_type=pl.DeviceIdType.MESH).start()
        token[...] = jnp.zeros_like(token)

    hbm = pl.BlockSpec(memory_space=pltpu.HBM)
    sem = pl.BlockSpec(memory_space=pltpu.SEMAPHORE)
    sems = pltpu.SemaphoreType.DMA((npeer * n,))
    out = pl.pallas_call(
        body, name=name,
        out_shape=(sems, sems, *[pltpu.HBM(a.shape, a.dtype) for a in list(xs) + lands],
                   jax.ShapeDtypeStruct((SUBLANES, LANES), F32)),
        in_specs=[hbm] * (2 * n),
        out_specs=(sem, sem, *[hbm] * (2 * n), pl.BlockSpec(memory_space=pltpu.VMEM)),
        input_output_aliases={i: 2 + i for i in range(2 * n)},
        compiler_params=pltpu.CompilerParams(has_side_effects=pltpu.SideEffectType.DATAFLOW_SIDE_EFFECTING),
    )(*[pltpu.with_memory_space_constraint(a, pltpu.HBM) for a in list(xs) + lands])
    return out[0], out[1], list(out[2:2 + n]), list(out[2 + n:2 + 2 * n]), out[-1]


def _split_wait(started, kind, after, name):
    send_sems_, recv_sems_, x_thru, land_thru, _ = started
    n = len(x_thru)
    npeer = 7 if kind == "gather" else 3

    def body(*refs):
        x_refs, land_refs = refs[:n], refs[n:2 * n]
        send_sems, recv_sems = refs[2 * n], refs[2 * n + 1]
        for k in range(1, npeer + 1):
            for a in range(n):
                src, _, landed, peer = _split_views(kind, x_refs[a], land_refs[a], k)
                cp = pltpu.make_async_remote_copy(
                    src_ref=src, dst_ref=landed, send_sem=send_sems.at[npeer * a + k - 1],
                    recv_sem=recv_sems.at[npeer * a + k - 1],
                    device_id=peer, device_id_type=pl.DeviceIdType.MESH)
                cp.wait_send()
                cp.wait_recv()

    hbm = pl.BlockSpec(memory_space=pltpu.HBM)
    sem = pl.BlockSpec(memory_space=pltpu.SEMAPHORE)
    out = pl.pallas_call(
        body, name=name,
        out_shape=tuple(pltpu.HBM(a.shape, a.dtype) for a in x_thru + land_thru),
        in_specs=[hbm] * (2 * n) + [sem, sem, pl.BlockSpec(memory_space=pl.ANY)],
        out_specs=tuple([hbm] * (2 * n)),
        input_output_aliases={i: i for i in range(2 * n)},
        compiler_params=pltpu.CompilerParams(has_side_effects=pltpu.SideEffectType.DATAFLOW_SIDE_EFFECTING),
    )(*x_thru, *land_thru, send_sems_, recv_sems_, after)
    return list(out[:n]), list(out[n:])


def _pair_sum(x8, r4, name):
    _, r, c = x8.shape
    tr = _row_tile(r, c * 12)

    def body(core_ref, x_ref, r_ref, o_ref):
        o_ref[...] = (x_ref[...].astype(F32) + r_ref[...].astype(F32)).astype(o_ref.dtype)

    core = lax.axis_index("c").astype(jnp.int32).reshape(1)
    return pl.pallas_call(
        body, name=name,
        grid_spec=pltpu.PrefetchScalarGridSpec(
            num_scalar_prefetch=1, grid=(4, r // tr),
            in_specs=[pl.BlockSpec((None, tr, c), lambda j, i, core_ref: (2 * j + core_ref[0], i, 0)),
                      pl.BlockSpec((None, tr, c), lambda j, i, core_ref: (j, i, 0))],
            out_specs=pl.BlockSpec((None, tr, c), lambda j, i, core_ref: (j, i, 0))),
        out_shape=jax.ShapeDtypeStruct((4, r, c), x8.dtype),
        compiler_params=_cparams(("parallel", "parallel"), VMEM_LIMIT))(core, x8, r4)


def _row_tile(r, bytes_per_row, budget=4 * 1024 * 1024):
    best = None
    for t in range(16, r + 1, 16):
        if r % t == 0 and t * bytes_per_row <= budget:
            best = t
    return best if best is not None else r


def _mm(a, b, dims, name, out_dtype=F32, add=None, out_shards=1, tm=1024, tn=1024, tk=1024):
    S = b.shape[0] if b.ndim == 3 else 1
    bshape = (b.shape[1], S * b.shape[2]) if b.ndim == 3 else b.shape
    if dims == "nn":
        (M, K), (K2, N) = a.shape, bshape
    elif dims == "nt":
        (M, K), (N, K2) = a.shape, bshape
    else:
        (K, M), (K2, N) = a.shape, bshape
    assert K == K2, (a.shape, b.shape, dims)
    cs = bshape[1] // S
    tm, tk = _tile(M, tm), _tile(K, min(tk, cs) if (S > 1 and dims == "nt") else tk)
    tn = _tile(N, min(tn, cs) if (S > 1 and dims != "nt") else min(tn, N // out_shards))
    assert cs % (tk if dims == "nt" else tn) == 0 and (N // out_shards) % tn == 0
    nk = K // tk

    def body(*refs):
        if add is None:
            a_ref, b_ref, o_ref, acc = refs
        else:
            a_ref, b_ref, c_ref, o_ref, acc = refs
        k = pl.program_id(2)

        @pl.when(k == 0)
        def _():
            acc[...] = jnp.zeros_like(acc)

        acc[...] += _dot(a_ref[...], b_ref[...], dims)

        @pl.when(k == nk - 1)
        def _():
            r = acc[...]
            if add is not None:
                r = r + c_ref[...]
            o_ref[...] = r.astype(out_dtype)

    a_spec = (pl.BlockSpec((tk, tm), lambda i, j, k: (k, i)) if dims == "tn"
              else pl.BlockSpec((tm, tk), lambda i, j, k: (i, k)))
    if S == 1:
        b_spec = (pl.BlockSpec((tn, tk), lambda i, j, k: (j, k)) if dims == "nt"
                  else pl.BlockSpec((tk, tn), lambda i, j, k: (k, j)))
    elif dims == "nt":
        per = cs // tk
        b_spec = pl.BlockSpec((None, tn, tk), lambda i, j, k: (k // per, j, k % per))
    else:
        per = cs // tn
        b_spec = pl.BlockSpec((None, tk, tn), lambda i, j, k: (j // per, k, j % per))
    in_specs = [a_spec, b_spec]
    args = [a, b]
    if add is not None:
        in_specs.append(pl.BlockSpec((tm, tn), lambda i, j, k: (i, j)))
        args.append(add)
    if out_shards == 1:
        out_spec, out_shape = pl.BlockSpec((tm, tn), lambda i, j, k: (i, j)), (M, N)
    else:
        oper = N // out_shards // tn
        out_spec = pl.BlockSpec((None, tm, tn), lambda i, j, k: (j // oper, i, j % oper))
        out_shape = (out_shards, M, N // out_shards)
    return pl.pallas_call(
        body, name=name, grid=(M // tm, N // tn, nk),
        in_specs=in_specs, out_specs=out_spec,
        out_shape=jax.ShapeDtypeStruct(out_shape, out_dtype),
        scratch_shapes=[pltpu.VMEM((tm, tn), F32)],
        compiler_params=_cparams(("parallel", "parallel", "arbitrary"), VMEM_LIMIT),
    )(*args)


VMEM_LIMIT_SUM = 56 * 1024 * 1024


def _mm_sum(pairs, name, out_dtype=F32, tm=768, tn=1024, tk=1024):
    M, N = pairs[0][0].shape[0], pairs[0][1].shape[1]
    tm, tn = _tile(M, tm), _tile(N, tn)
    tks = [_tile(a.shape[1], tk) for a, _ in pairs]
    nks = [a.shape[1] // t for (a, _), t in zip(pairs, tks)]
    starts = [sum(nks[:g]) for g in range(len(pairs))]
    nk = sum(nks)
    G = len(pairs)

    def body(*refs):
        o_ref, acc = refs[2 * G], refs[2 * G + 1]
        k = pl.program_id(2)

        @pl.when(k == 0)
        def _():
            acc[...] = jnp.zeros_like(acc)

        for g in range(G):
            @pl.when(jnp.logical_and(k >= starts[g], k < starts[g] + nks[g]))
            def _(g=g):
                acc[...] += _dot(refs[2 * g][...], refs[2 * g + 1][...])

        @pl.when(k == nk - 1)
        def _():
            o_ref[...] = acc[...].astype(out_dtype)

    in_specs, args = [], []
    for g, (a, b) in enumerate(pairs):
        kk = lambda k, g=g: jnp.clip(k - starts[g], 0, nks[g] - 1)
        in_specs += [pl.BlockSpec((tm, tks[g]), lambda i, j, k, kk=kk: (i, kk(k))),
                     pl.BlockSpec((tks[g], tn), lambda i, j, k, kk=kk: (kk(k), j))]
        args += [a, b]
    return pl.pallas_call(
        body, name=name, grid=(M // tm, N // tn, nk),
        in_specs=in_specs, out_specs=pl.BlockSpec((tm, tn), lambda i, j, k: (i, j)),
        out_shape=jax.ShapeDtypeStruct((M, N), out_dtype),
        scratch_shapes=[pltpu.VMEM((tm, tn), F32)],
        compiler_params=_cparams(("parallel", "parallel", "arbitrary"), VMEM_LIMIT_SUM),
    )(*args)


def _ada_fwd(ccp, w, b):
    def body(c_ref, w_ref, b_ref, o_ref):
        o_ref[...] = _dot(_silu(c_ref[...]), w_ref[...]) + b_ref[...]

    return pl.pallas_call(
        body, name="ada_fwd", out_shape=jax.ShapeDtypeStruct((ccp.shape[0], w.shape[1]), F32),
        compiler_params=_cparams(None, VMEM_LIMIT))(ccp, w, b)


def _ada_bwd(ccp, w, g, cctx, n_loc):
    def body(c_ref, w_ref, g_ref, cc_ref, gw_ref, dc_ref):
        gw_ref[...] = _dot(_silu(c_ref[...]), g_ref[...], "tn")
        dcc = _dot(g_ref[...], w_ref[...], "nt")
        row = lax.broadcasted_iota(jnp.int32, dcc.shape, 0)
        part = jnp.sum(jnp.where(row % SUBLANES == n_loc, dcc, 0.0), axis=0, keepdims=True)
        dc_ref[...] = jnp.broadcast_to(part * _dsilu(cc_ref[...]), dc_ref.shape)

    D = ccp.shape[1]
    return pl.pallas_call(
        body, name="ada_bwd",
        out_shape=(jax.ShapeDtypeStruct(w.shape, F32), jax.ShapeDtypeStruct((SUBLANES, D), F32)),
        compiler_params=_cparams(None, VMEM_LIMIT))(ccp, w, g, cctx)


def _norm_mod_fwd(h, gain, mods, i_shift, i_scale, nct, tm, name):
    B, T, D = h.shape

    def body(h_ref, g_ref, m_ref, u_ref):
        x = h_ref[0]
        r = lax.rsqrt(jnp.mean(x * x, axis=-1, keepdims=True) + EPS)
        n = x * r * g_ref[...]
        u_ref[0] = (n * (1.0 + m_ref[0, i_scale:i_scale + 1, :]) + m_ref[0, i_shift:i_shift + 1, :]).astype(BF16)

    return pl.pallas_call(
        body, name=name, grid=(B, T // tm),
        in_specs=[pl.BlockSpec((1, tm, D), lambda b, t: (b, t, 0)),
                  pl.BlockSpec((1, D), lambda b, t: (0, 0)),
                  pl.BlockSpec((1, SUBLANES, D), lambda b, t: (jnp.where(t < nct, B, b), 0, 0))],
        out_specs=pl.BlockSpec((1, tm, D), lambda b, t: (b, t, 0)),
        out_shape=jax.ShapeDtypeStruct((B, T, D), BF16),
        compiler_params=_cparams(("parallel", "parallel"), VMEM_LIMIT),
    )(h, gain, mods)


def _norm_mod_bwd(h, du, gain, mods, i_scale, nct, tm, res, name):
    B, T, D = h.shape
    nt = T // tm

    def body(h_ref, du_ref, g_ref, m_ref, res_ref, dx_ref, ax_ref, ac_ref):
        t = pl.program_id(1)
        x = h_ref[0]
        r = lax.rsqrt(jnp.mean(x * x, axis=-1, keepdims=True) + EPS)
        nh = x * r
        g = g_ref[...]
        du_ = du_ref[0]
        dn = du_ * (1.0 + m_ref[0, i_scale:i_scale + 1, :])
        dnh = dn * g
        dx = r * (dnh - nh * jnp.mean(dnh * nh, axis=-1, keepdims=True))
        sums = (jnp.sum(du_, axis=0, keepdims=True), jnp.sum(du_ * nh * g, axis=0, keepdims=True),
                jnp.sum(dn * nh, axis=0, keepdims=True))

        @pl.when(t == 0)
        def _():
            ax_ref[...] = jnp.zeros_like(ax_ref)
            ac_ref[...] = jnp.zeros_like(ac_ref)

        def accumulate(ref):
            for i, s in enumerate(sums):
                ref[0, i:i + 1, :] += s

        @pl.when(t < nct)
        def _():
            accumulate(ac_ref)

        @pl.when(t >= nct)
        def _():
            accumulate(ax_ref)
            dx_ref[0] = dx + res_ref[0]

    lat = lambda b, t: (b, jnp.maximum(t - nct, 0), 0)
    acc = pl.BlockSpec((1, SUBLANES, D), lambda b, t: (b, 0, 0))
    return pl.pallas_call(
        body, name=name, grid=(B, nt),
        in_specs=[pl.BlockSpec((1, tm, D), lambda b, t: (b, t, 0)),
                  pl.BlockSpec((1, tm, D), lambda b, t: (b, t, 0)),
                  pl.BlockSpec((1, D), lambda b, t: (0, 0)),
                  pl.BlockSpec((1, SUBLANES, D), lambda b, t: (jnp.where(t < nct, B, b), 0, 0)),
                  pl.BlockSpec((1, tm, D), lat)],
        out_specs=(pl.BlockSpec((1, tm, D), lat), acc, acc),
        out_shape=(jax.ShapeDtypeStruct(res.shape, F32),
                   jax.ShapeDtypeStruct((B, SUBLANES, D), F32), jax.ShapeDtypeStruct((B, SUBLANES, D), F32)),
        compiler_params=_cparams(("parallel", "arbitrary"), VMEM_LIMIT),
    )(h, du, gain, mods, res)


def _resid_norm_fwd(x, y, gain, mods, i_gate, tm, name):
    B, N, D = x.shape

    def body(x_ref, y_ref, g_ref, m_ref, o_ref):
        y_ = y_ref[0]
        r = lax.rsqrt(jnp.mean(y_ * y_, axis=-1, keepdims=True) + EPS)
        o_ref[0] = x_ref[0] + y_ * r * g_ref[...] * m_ref[0, i_gate:i_gate + 1, :]

    row = pl.BlockSpec((1, tm, D), lambda b, t: (b, t, 0))
    return pl.pallas_call(
        body, name=name, grid=(B, N // tm),
        in_specs=[row, row, pl.BlockSpec((1, D), lambda b, t: (0, 0)),
                  pl.BlockSpec((1, SUBLANES, D), lambda b, t: (b, 0, 0))],
        out_specs=row, out_shape=jax.ShapeDtypeStruct((B, N, D), F32),
        compiler_params=_cparams(("parallel", "parallel"), VMEM_LIMIT),
    )(x, y, gain, mods)


def _resid_norm_bwd_math(y_, dh, g, gate):
    r = lax.rsqrt(jnp.mean(y_ * y_, axis=-1, keepdims=True) + EPS)
    nh = y_ * r
    dgate = jnp.sum(dh * nh * g, axis=0, keepdims=True)
    dn = dh * gate
    dgain = jnp.sum(dn * nh, axis=0, keepdims=True)
    dnh = dn * g
    dy = r * (dnh - nh * jnp.mean(dnh * nh, axis=-1, keepdims=True))
    return dy, dgate, dgain


def _resid_norm_bwd(y, dh, gain, mods, i_gate, tm, name):
    B, N, D = y.shape

    def body(y_ref, dh_ref, g_ref, m_ref, dy_ref, acc_ref):
        t = pl.program_id(1)
        dy, dgate, dgain = _resid_norm_bwd_math(y_ref[0], dh_ref[0], g_ref[...], m_ref[0, i_gate:i_gate + 1, :])
        dy_ref[0] = dy.astype(BF16)

        @pl.when(t == 0)
        def _():
            acc_ref[...] = jnp.zeros_like(acc_ref)

        acc_ref[0, 0:1, :] += dgate
        acc_ref[0, 1:2, :] += dgain

    row = pl.BlockSpec((1, tm, D), lambda b, t: (b, t, 0))
    return pl.pallas_call(
        body, name=name, grid=(B, N // tm),
        in_specs=[row, row, pl.BlockSpec((1, D), lambda b, t: (0, 0)),
                  pl.BlockSpec((1, SUBLANES, D), lambda b, t: (b, 0, 0))],
        out_specs=(row, pl.BlockSpec((1, SUBLANES, D), lambda b, t: (b, 0, 0))),
        out_shape=(jax.ShapeDtypeStruct((B, N, D), BF16), jax.ShapeDtypeStruct((B, SUBLANES, D), F32)),
        compiler_params=_cparams(("parallel", "arbitrary"), VMEM_LIMIT),
    )(y, dh, gain, mods)


def _final_fwd_bwd(h1, d, gain, mods, i_gate, tgt, tm):
    B, N, D = h1.shape

    def body(h_ref, d_ref, g_ref, m_ref, t_ref, dd_ref, dh_ref, acc_ref):
        t = pl.program_id(1)
        d_ = d_ref[0]
        g = g_ref[...]
        gate = m_ref[0, i_gate:i_gate + 1, :]
        r = lax.rsqrt(jnp.mean(d_ * d_, axis=-1, keepdims=True) + EPS)
        e = h_ref[0] + d_ * r * g * gate - t_ref[0]
        dh = e * (1.0 / D)
        dh_ref[0] = dh
        dy, dgate, dgain = _resid_norm_bwd_math(d_, dh, g, gate)
        dd_ref[0] = dy.astype(BF16)

        @pl.when(t == 0)
        def _():
            acc_ref[...] = jnp.zeros_like(acc_ref)

        acc_ref[0, 0:1, :] += dgate
        acc_ref[0, 1:2, :] += dgain
        acc_ref[0, 2:3, :] += jnp.sum(e * e, axis=0, keepdims=True)

    row = pl.BlockSpec((1, tm, D), lambda b, t: (b, t, 0))
    return pl.pallas_call(
        body, name="final_fwd_bwd", grid=(B, N // tm),
        in_specs=[row, row, pl.BlockSpec((1, D), lambda b, t: (0, 0)),
                  pl.BlockSpec((1, SUBLANES, D), lambda b, t: (b, 0, 0)), row],
        out_specs=(row, row, pl.BlockSpec((1, SUBLANES, D), lambda b, t: (b, 0, 0))),
        out_shape=(jax.ShapeDtypeStruct((B, N, D), BF16), jax.ShapeDtypeStruct((B, N, D), F32),
                   jax.ShapeDtypeStruct((B, SUBLANES, D), F32)),
        compiler_params=_cparams(("parallel", "arbitrary"), VMEM_LIMIT),
    )(h1, d, gain, mods, tgt)


def _shifted(x, prev, nxt, off, row, tm):
    if off == 0:
        return x
    if off < 0:
        y = pltpu.roll(x, -off, 0)
        for r in range(-off):
            y = jnp.where(row == r, prev[SUBLANES + r + off:SUBLANES + r + off + 1, :], y)
        return y
    y = pltpu.roll(x, tm - off, 0)
    for r in range(off):
        y = jnp.where(row == tm - off + r, nxt[r:r + 1, :], y)
    return y


def _halo_specs(T, tm, tc, cb0, order):
    r8, n8 = tm // SUBLANES, T // SUBLANES
    if order == "btj":
        prev = lambda b, t, j: (b, jnp.maximum(t * r8 - 1, 0), cb0 + j)
        nxt = lambda b, t, j: (b, jnp.minimum((t + 1) * r8, n8 - 1), cb0 + j)
    else:
        prev = lambda b, j, t: (b, jnp.maximum(t * r8 - 1, 0), cb0 + j)
        nxt = lambda b, j, t: (b, jnp.minimum((t + 1) * r8, n8 - 1), cb0 + j)
    return pl.BlockSpec((1, SUBLANES, tc), prev), pl.BlockSpec((1, SUBLANES, tc), nxt)


def _seg_halos(p_ref, n_ref, t, nct, nt):
    seg_start = jnp.logical_or(t == 0, t == nct)
    seg_end = jnp.logical_or(t == nct - 1, t == nt - 1)
    prev = jnp.where(seg_start, 0.0, p_ref[0])
    nxt = jnp.where(seg_end, 0.0, n_ref[0])
    return prev, nxt


def _dwconv(x, col0, C, w, bias, offs, nct, tm, tc, name, out_dtype=F32, qkv_heads=None):
    B, T, _ = x.shape
    nt = T // tm
    cb0 = col0 // tc
    if qkv_heads is not None:
        assert tc == qkv_heads[0] * qkv_heads[1] and C == 3 * tc

    def body(*refs):
        refs = list(refs)
        a_ref = refs.pop() if qkv_heads is not None else None
        if bias is None:
            x_ref, p_ref, n_ref, w_ref, o_ref = refs
        else:
            x_ref, p_ref, n_ref, w_ref, b_ref, o_ref = refs
        t = pl.program_id(1)
        prev, nxt = _seg_halos(p_ref, n_ref, t, nct, nt)
        x_ = x_ref[0]
        row = lax.broadcasted_iota(jnp.int32, x_.shape, 0)
        acc = None
        for j, off in enumerate(offs):
            term = _shifted(x_, prev, nxt, off, row, tm) * w_ref[j:j + 1, :]
            acc = term if acc is None else acc + term
        if bias is not None:
            acc = acc + b_ref[...]
        o_ref[0] = acc.astype(out_dtype)
        if qkv_heads is not None:
            H, hd = qkv_heads
            part = pl.program_id(2)
            for h in range(H):
                sl = slice(h * hd, (h + 1) * hd)
                s = _silu(acc[:, sl])
                rs = lax.rsqrt(jnp.sum(s * s, axis=-1, keepdims=True) + EPS)
                scale = jnp.where(part == 0, rs * (float(hd) ** -0.5), jnp.where(part == 1, rs, 1.0))
                a_ref[0, :, sl] = s * scale

    pspec, nspec = _halo_specs(T, tm, tc, cb0, "btj")
    in_specs = [pl.BlockSpec((1, tm, tc), lambda b, t, j: (b, t, cb0 + j)), pspec, nspec,
                pl.BlockSpec((w.shape[0], tc), lambda b, t, j: (0, j))]
    args = [x, x, x, w]
    if bias is not None:
        in_specs.append(pl.BlockSpec((1, tc), lambda b, t, j: (0, j)))
        args.append(bias)
    tile = pl.BlockSpec((1, tm, tc), lambda b, t, j: (b, t, j))
    out_specs, out_shape = tile, jax.ShapeDtypeStruct((B, T, C), out_dtype)
    if qkv_heads is not None:
        out_specs, out_shape = (tile, tile), (out_shape, jax.ShapeDtypeStruct((B, T, C), F32))
    return pl.pallas_call(
        body, name=name, grid=(B, nt, C // tc),
        in_specs=in_specs, out_specs=out_specs, out_shape=out_shape,
        compiler_params=_cparams(("parallel", "parallel", "parallel"), VMEM_LIMIT),
    )(*args)


def _dwconv_wgrad(dy, x, col0, C, offs, nct, tm, tc, name):
    B, T, _ = x.shape
    nt = T // tm
    cb0 = col0 // tc
    K = len(offs)

    def body(dy_ref, x_ref, p_ref, n_ref, acc_ref):
        t = pl.program_id(2)
        prev, nxt = _seg_halos(p_ref, n_ref, t, nct, nt)
        x_ = x_ref[0]
        dy_ = dy_ref[0]
        row = lax.broadcasted_iota(jnp.int32, x_.shape, 0)

        @pl.when(t == 0)
        def _():
            acc_ref[...] = jnp.zeros_like(acc_ref)

        for j, off in enumerate(offs):
            acc_ref[0, j:j + 1, :] += jnp.sum(dy_ * _shifted(x_, prev, nxt, off, row, tm), axis=0, keepdims=True)
        acc_ref[0, K:K + 1, :] += jnp.sum(dy_, axis=0, keepdims=True)

    pspec, nspec = _halo_specs(T, tm, tc, cb0, "bjt")
    return pl.pallas_call(
        body, name=name, grid=(B, C // tc, nt),
        in_specs=[pl.BlockSpec((1, tm, tc), lambda b, j, t: (b, t, j)),
                  pl.BlockSpec((1, tm, tc), lambda b, j, t: (b, t, cb0 + j)), pspec, nspec],
        out_specs=pl.BlockSpec((1, SUBLANES, tc), lambda b, j, t: (b, 0, j)),
        out_shape=jax.ShapeDtypeStruct((B, SUBLANES, C), F32),
        compiler_params=_cparams(("parallel", "parallel", "arbitrary"), VMEM_LIMIT),
    )(dy, x, x, x)


def _ab_act(pab, alog, dtb, nd, tm):
    R = pab.shape[0]

    def body(p_ref, al_ref, dt_ref, o_ref):
        p = p_ref[...]
        lane = lax.broadcasted_iota(jnp.int32, p.shape, 1)
        g = -jnp.exp(al_ref[...]) * _softplus(p + dt_ref[...])
        o_ref[...] = jnp.where(lane < nd, g, jnp.where(lane < 2 * nd, _sigmoid(p), 0.0))

    row = pl.BlockSpec((tm, LANES), lambda i: (i, 0))
    vec = pl.BlockSpec((1, LANES), lambda i: (0, 0))
    return pl.pallas_call(
        body, name="ab_act", grid=(R // tm,), in_specs=[row, vec, vec], out_specs=row,
        out_shape=jax.ShapeDtypeStruct((R, LANES), F32),
        compiler_params=_cparams(("parallel",), VMEM_LIMIT))(pab, alog, dtb)


def _ab_act_bwd(pab, gb, dgb, alog, dtb, nd, tm):
    R = pab.shape[0]

    def body(p_ref, gb_ref, d0_ref, d1_ref, al_ref, dt_ref, o_ref, acc_ref):
        i = pl.program_id(0)
        p = p_ref[...]
        gb_ = gb_ref[...]
        d_ = d0_ref[...] + d1_ref[...]
        lane = lax.broadcasted_iota(jnp.int32, p.shape, 1)
        is_g = lane < nd
        is_b = jnp.logical_and(lane >= nd, lane < 2 * nd)
        da = jnp.where(is_g, d_ * (-jnp.exp(al_ref[...])) * _sigmoid(p + dt_ref[...]), 0.0)
        db = jnp.where(is_b, d_ * gb_ * (1.0 - gb_), 0.0)
        o_ref[...] = (da + db).astype(BF16)

        @pl.when(i == 0)
        def _():
            acc_ref[...] = jnp.zeros_like(acc_ref)

        acc_ref[0:1, :] += jnp.sum(jnp.where(is_g, d_ * gb_, 0.0), axis=0, keepdims=True)
        acc_ref[1:2, :] += jnp.sum(da, axis=0, keepdims=True)

    row = pl.BlockSpec((tm, LANES), lambda i: (i, 0))
    vec = pl.BlockSpec((1, LANES), lambda i: (0, 0))
    return pl.pallas_call(
        body, name="ab_act_bwd", grid=(R // tm,),
        in_specs=[row, row, row, row, vec, vec],
        out_specs=(row, pl.BlockSpec((SUBLANES, LANES), lambda i: (0, 0))),
        out_shape=(jax.ShapeDtypeStruct((R, LANES), BF16), jax.ShapeDtypeStruct((SUBLANES, LANES), F32)),
        compiler_params=_cparams(("arbitrary",), VMEM_LIMIT))(pab, gb, dgb[0], dgb[1], alog, dtb)


def _dn_act_bwd(cv, dqkv, H, hd, tm):
    B, T, C3 = cv.shape
    qscale = float(hd) ** -0.5

    def body(c_ref, d0_ref, d1_ref, o_ref):
        part = pl.program_id(0)
        for h in range(H):
            sl = slice(h * hd, (h + 1) * hd)
            c = c_ref[0, :, sl]
            s = _silu(c)
            dout = d0_ref[0, :, sl] + d1_ref[0, :, sl]
            rs = lax.rsqrt(jnp.sum(s * s, axis=-1, keepdims=True) + EPS)
            sh = s * rs
            dnorm = rs * (dout - sh * jnp.sum(dout * sh, axis=-1, keepdims=True))
            ds = jnp.where(part == 0, dnorm * qscale, jnp.where(part == 1, dnorm, dout))
            o_ref[0, :, sl] = ds * _dsilu(c)

    spec = pl.BlockSpec((1, tm, H * hd), lambda p, b, t: (b, t, p))
    return pl.pallas_call(
        body, name="dn_act_bwd", grid=(3, B, T // tm), in_specs=[spec, spec, spec], out_specs=spec,
        out_shape=jax.ShapeDtypeStruct((B, T, C3), F32),
        compiler_params=_cparams(("parallel",) * 3, VMEM_LIMIT))(cv, dqkv[0], dqkv[1])


def _chunk_of(d, s, ncc, nch):
    if d == 0:
        return s
    return jnp.where(s < ncc, ncc - 1 - s, nch - 1 - (s - ncc))


def _delta_masks(d):
    row = lax.broadcasted_iota(jnp.int32, (CHUNK, CHUNK), 0)
    col = lax.broadcasted_iota(jnp.int32, (CHUNK, CHUNK), 1)
    sign = 1 - 2 * d
    diff = (row - col) * sign
    return diff >= 0, diff > 0, diff <= 0


def _each(f, *lists):
    return [f(*a) for a in zip(*lists)]


def _unit_tri_inverse(As):
    C = As[0].shape[0]
    row = lax.broadcasted_iota(jnp.int32, (C, C), 0)
    col = lax.broadcasted_iota(jnp.int32, (C, C), 1)
    eye = jnp.where(row == col, 1.0, 0.0).astype(F32)
    same = lambda shift: jnp.right_shift(row, shift) == jnp.right_shift(col, shift)
    in8 = same(3)
    xs = [-jnp.where(in8, a, 0.0) for a in As]
    ts = [eye + x for x in xs]
    ps = _each(lambda x: _dot(x, x), xs)
    tp = _each(lambda t, p: _dot(_rows(t, p), p), ts, ps)
    ts = _each(lambda t, m: t + m[:C], ts, tp)
    ts = _each(lambda t, m: t + _dot(t, m[C:]), ts, tp)
    shift = 3
    while (1 << shift) < C:
        off = jnp.logical_and(same(shift + 1), jnp.right_shift(row, shift) != jnp.right_shift(col, shift))
        los = [jnp.where(off, a, 0.0) for a in As]
        ts = _each(lambda t, lo: t - _dot(t, _dot(lo, t)), ts, los)
        shift += 1
    def residual(a, t):
        (ah, al), (th, tl) = _split2(eye + a), _split2(t)
        m = _dot(_rows(ah, al), th)
        return eye - (m[:C] + (m[C:] + _dot(ah, tl)))

    rs = _each(residual, As, ts)
    return _each(lambda t, r: t + _dot(t, r), ts, rs)


def _rows(*xs):
    return jnp.concatenate(xs, axis=0)


def _cols(*xs):
    return jnp.concatenate(xs, axis=1)


def _delta_chunk_fwd(q, k, v, g_i, g_j, beta_i, gl, S, incl, strict, Tm=None):
    D_ = _each(lambda gi, gj, m: jnp.where(m, jnp.exp(jnp.where(m, gi - gj, 0.0)), 0.0), g_i, g_j, incl)
    C, dk = k[0].shape
    kb = _each(lambda k_, b: k_ * b, k, beta_i)
    kq = _each(lambda kb_, q_, k_: _dot(_rows(kb_, q_), k_, "nt"), kb, q, k)
    A = _each(lambda m_, d, m: jnp.where(m, m_[:C] * d, 0.0), kq, D_, strict)
    P = _each(lambda m_, d, m: jnp.where(m, m_[C:] * d, 0.0), kq, D_, incl)
    if Tm is None:
        Tm = _unit_tri_inverse(A)
    gam = _each(jnp.exp, g_i)
    wu = _each(lambda t, kb_, g, v_, b: _dot(t, _cols(kb_ * g, v_ * b)), Tm, kb, gam, v, beta_i)
    w = [m_[:, :dk] for m_ in wu]
    u = [m_[:, dk:] for m_ in wu]
    qg = _each(lambda q_, g: q_ * g, q, gam)
    ws = _each(lambda w_, qg_, s: _dot(_rows(w_, qg_), s), w, qg, S)
    vn = _each(lambda u_, m_: u_ - m_[:C], u, ws)
    o = _each(lambda m_, p, vn_: m_[C:] + _dot(p, vn_), ws, P, vn)
    e_i = _each(lambda gl_, gi: jnp.exp(gl_ - gi), gl, g_i)
    kd = _each(lambda k_, e: k_ * e, k, e_i)
    Gam = _each(jnp.exp, gl)
    S_new = _each(lambda s, g, kd_, vn_: s * g + _dot(kd_, vn_, "tn"), S, Gam, kd, vn)
    return dict(D=D_, kb=kb, A=A, Tm=Tm, gam=gam, w=w, u=u, vn=vn, P=P, qg=qg, o=o, e=e_i, kd=kd, Gam=Gam, S_new=S_new)


def _delta_gb(gb_ref, d, incl, incl_t, H):
    gb = gb_ref[0]
    g = gb[:, d * H:(d + 1) * H]
    beta = gb[:, (2 + d) * H:(3 + d) * H]
    gc_col = _dot_mask(incl.astype(BF16), g)
    gc_row = _dot_mask(incl_t.astype(BF16), g, "xtn")
    gl = jnp.sum(g, axis=0, keepdims=True)
    return beta, gc_col, gc_row, gl


def _delta_fwd(qkvn, gb, H, hd, ncc):
    B, T, _ = qkvn.shape
    nch = T // CHUNK
    HD = H * hd
    pairs = [(d, h) for d in range(2) for h in range(H)]

    def body(q0, k0, v0, gb0, q1, k1, v1, gb1, o0, o1, sin0, sin1, tm0, tm1, S_ref):
        s = pl.program_id(1)
        q_refs, k_refs, v_refs, gb_refs = (q0, q1), (k0, k1), (v0, v1), (gb0, gb1)
        o_refs, sin_refs, tm_refs = (o0, o1), (sin0, sin1), (tm0, tm1)

        @pl.when(s == 0)
        def _():
            S_ref[...] = jnp.zeros_like(S_ref)

        masks = [_delta_masks(d) for d in range(2)]
        gbs = [_delta_gb(gb_refs[d], d, masks[d][0], masks[d][2], H) for d in range(2)]
        head = lambda ref, h: ref[0, :, h * hd:(h + 1) * hd]
        S = [S_ref[d, h] for d, h in pairs]
        r = _delta_chunk_fwd([head(q_refs[d], h) for d, h in pairs], [head(k_refs[d], h) for d, h in pairs],
                             [head(v_refs[d], h) for d, h in pairs],
                             [gbs[d][1][:, h:h + 1] for d, h in pairs], [gbs[d][2][h:h + 1, :] for d, h in pairs],
                             [gbs[d][0][:, h:h + 1] for d, h in pairs], [gbs[d][3][:, h:h + 1] for d, h in pairs],
                             S, [masks[d][0] for d, h in pairs], [masks[d][1] for d, h in pairs])
        for i, (d, h) in enumerate(pairs):
            sin_refs[d][0, 0, h] = S[i]
            tm_refs[d][0, 0, h] = r["Tm"][i]
            S_ref[d, h] = r["S_new"][i]
            o_refs[d][0, :, h * hd:(h + 1) * hd] = r["o"][i]

    def dir_specs(d):
        cidx = lambda b, s: _chunk_of(d, s, ncc, nch)
        ins = [pl.BlockSpec((1, CHUNK, HD), lambda b, s, p=p: (b, cidx(b, s), p)) for p in range(3)]
        ins.append(pl.BlockSpec((1, CHUNK, LANES), lambda b, s: (b, cidx(b, s), 0)))
        return (ins, pl.BlockSpec((1, CHUNK, HD), lambda b, s: (b, cidx(b, s), 0)),
                pl.BlockSpec((1, 1, H, hd, hd), lambda b, s: (b, cidx(b, s), 0, 0, 0)),
                pl.BlockSpec((1, 1, H, CHUNK, CHUNK), lambda b, s: (b, cidx(b, s), 0, 0, 0)))

    (in0, o_s0, sin_s0, tm_s0), (in1, o_s1, sin_s1, tm_s1) = dir_specs(0), dir_specs(1)
    o_shape = jax.ShapeDtypeStruct((B, T, HD), F32)
    sin_shape = jax.ShapeDtypeStruct((B, nch, H, hd, hd), F32)
    tm_shape = jax.ShapeDtypeStruct((B, nch, H, CHUNK, CHUNK), F32)
    return pl.pallas_call(
        body, name="delta_fwd", grid=(B, nch),
        in_specs=in0 + in1, out_specs=(o_s0, o_s1, sin_s0, sin_s1, tm_s0, tm_s1),
        out_shape=(o_shape, o_shape, sin_shape, sin_shape, tm_shape, tm_shape),
        scratch_shapes=[pltpu.VMEM((2, H, hd, hd), F32)],
        compiler_params=_cparams(("parallel", "arbitrary"), VMEM_LIMIT),
    )(qkvn, qkvn, qkvn, gb, qkvn, qkvn, qkvn, gb)


def _delta_bwd(qkvn, gb, sin, tms, do, H, hd, ncc):
    B, T, _ = qkvn.shape
    nch = T // CHUNK
    HD = H * hd
    pairs = [(d, h) for d in range(2) for h in range(H)]

    def body(q0, k0, v0, gb0, sin0, tm0, do0, q1, k1, v1, gb1, sin1, tm1, do1, dqkv0, dqkv1, dgb0, dgb1, dS_ref):
        s = pl.program_id(1)
        tm_refs = (tm0, tm1)
        q_refs, k_refs, v_refs, gb_refs = (q0, q1), (k0, k1), (v0, v1), (gb0, gb1)
        sin_refs, do_refs, dqkv_refs, dgb_refs = (sin0, sin1), (do0, do1), (dqkv0, dqkv1), (dgb0, dgb1)

        @pl.when(s == 0)
        def _():
            dS_ref[...] = jnp.zeros_like(dS_ref)

        masks = [_delta_masks(d) for d in range(2)]
        gbs = [_delta_gb(gb_refs[d], d, masks[d][0], masks[d][2], H) for d in range(2)]
        incl = [masks[d][0] for d, h in pairs]
        strict = [masks[d][1] for d, h in pairs]
        lane = lax.broadcasted_iota(jnp.int32, (1, LANES), 1)
        ones = jnp.ones((3 * CHUNK, LANES), BF16)
        head = lambda ref, h: ref[0, :, h * hd:(h + 1) * hd]
        q = [head(q_refs[d], h) for d, h in pairs]
        k = [head(k_refs[d], h) for d, h in pairs]
        v = [head(v_refs[d], h) for d, h in pairs]
        do_ = [head(do_refs[d], h) for d, h in pairs]
        beta_i = [gbs[d][0][:, h:h + 1] for d, h in pairs]
        S = [sin_refs[d][0, 0, h] for d, h in pairs]
        dSn = [dS_ref[d, h] for d, h in pairs]
        f = _delta_chunk_fwd(q, k, v, [gbs[d][1][:, h:h + 1] for d, h in pairs], [gbs[d][2][h:h + 1, :] for d, h in pairs],
                             beta_i, [gbs[d][3][:, h:h + 1] for d, h in pairs], S, incl, strict,
                             Tm=[tm_refs[d][0, 0, h] for d, h in pairs])
        rsum = lambda x: jnp.sum(x, axis=1, keepdims=True)
        dvn = _each(lambda p, d_, kd, ds: _dot(p, d_, "tn") + _dot(kd, ds), f["P"], do_, f["kd"], dSn)
        dP = _each(lambda d_, vn, m: jnp.where(m, _dot(d_, vn, "nt"), 0.0), do_, f["vn"], incl)
        dqg = _each(lambda d_, s: _dot(d_, s, "nt"), do_, S)
        dS_in = _each(lambda qg, d_, g, ds, w, dv_: _dot(qg, d_, "tn") + g * ds - _dot(w, dv_, "tn"),
                      f["qg"], do_, f["Gam"], dSn, f["w"], dvn)
        dGam = _each(lambda s, ds: jnp.sum(rsum(s * ds), axis=0, keepdims=True), S, dSn)
        dkd = _each(lambda vn, ds: _dot(vn, ds, "nt"), f["vn"], dSn)
        de = _each(lambda dkd_, k_, e: rsum(dkd_ * k_) * e, dkd, k, f["e"])
        dw = _each(lambda dv_, s: -_dot(dv_, s, "nt"), dvn, S)
        dXw = _each(lambda t, dw_: _dot(t, dw_, "tn"), f["Tm"], dw)
        dXu = _each(lambda t, dv_: _dot(t, dv_, "tn"), f["Tm"], dvn)
        dA = _each(lambda xw, w, xu, u, m: -jnp.where(m, _dot(xw, w, "nt") + _dot(xu, u, "nt"), 0.0),
                   dXw, f["w"], dXu, f["u"], strict)
        dM = _each(lambda a, d_: a * d_, dA, f["D"])
        dN = _each(lambda p, d_: p * d_, dP, f["D"])
        dkb = _each(lambda m, k_, xw, g: _dot(m, k_) + xw * g, dM, k, dXw, f["gam"])
        dq = _each(lambda dqg_, g, n, k_: dqg_ * g + _dot(n, k_), dqg, f["gam"], dN, k)
        dk = _each(lambda dkd_, e, m, kb, n, q_, dkb_, b: dkd_ * e + _dot(m, kb, "tn") + _dot(n, q_, "tn") + dkb_ * b,
                   dkd, f["e"], dM, f["kb"], dN, q, dkb, beta_i)
        dv = _each(lambda xu, b: xu * b, dXu, beta_i)
        dbeta = _each(lambda dkb_, k_, xu, v_: rsum(dkb_ * k_) + rsum(xu * v_), dkb, k, dXu, v)
        E = _each(lambda a, A_, p, P_: a * A_ + p * P_, dA, f["A"], dP, f["P"])

        def colsum(e):
            return _dot(_rows(*_split3(e)), ones, "tn")[:, 0:1]

        dg = _each(lambda e, dqg_, qg, xw, kb, g, de_: rsum(e) - colsum(e) + rsum(dqg_ * qg) + rsum(xw * kb) * g - de_,
                   E, dqg, f["qg"], dXw, f["kb"], f["gam"], de)
        dgl_h = _each(lambda de_, dg_, g: jnp.sum(de_, axis=0, keepdims=True) + dg_ * g, de, dGam, f["Gam"])
        for d in range(2):
            dgc = jnp.zeros((CHUNK, LANES), F32)
            dgl = jnp.zeros((1, LANES), F32)
            for h in range(H):
                i = d * H + h
                g_lane, b_lane = d * H + h, (2 + d) * H + h
                dgc = dgc + dg[i] * (lane == g_lane).astype(F32) + dbeta[i] * (lane == b_lane).astype(F32)
                dgl = dgl + dgl_h[i] * (lane == g_lane).astype(F32)
                dS_ref[d, h] = dS_in[i]
                dqkv_refs[d][0, :, h * hd:(h + 1) * hd] = dq[i]
                dqkv_refs[d][0, :, HD + h * hd:HD + (h + 1) * hd] = dk[i]
                dqkv_refs[d][0, :, 2 * HD + h * hd:2 * HD + (h + 1) * hd] = dv[i]
            draw = _dot_mask(masks[d][0].astype(BF16), dgc, "tn") + dgl
            dgb_refs[d][0] = jnp.where(lane < 2 * H, draw, dgc)

    def dir_specs(d):
        cidx = lambda b, s: _chunk_of(d, nch - 1 - s, ncc, nch)
        ins = [pl.BlockSpec((1, CHUNK, HD), lambda b, s, p=p: (b, cidx(b, s), p)) for p in range(3)]
        ins += [pl.BlockSpec((1, CHUNK, LANES), lambda b, s: (b, cidx(b, s), 0)),
                pl.BlockSpec((1, 1, H, hd, hd), lambda b, s: (b, cidx(b, s), 0, 0, 0)),
                pl.BlockSpec((1, 1, H, CHUNK, CHUNK), lambda b, s: (b, cidx(b, s), 0, 0, 0)),
                pl.BlockSpec((1, CHUNK, HD), lambda b, s: (b, cidx(b, s), 0))]
        return (ins, pl.BlockSpec((1, CHUNK, 3 * HD), lambda b, s: (b, cidx(b, s), 0)),
                pl.BlockSpec((1, CHUNK, LANES), lambda b, s: (b, cidx(b, s), 0)))

    (in0, dq_s0, dg_s0), (in1, dq_s1, dg_s1) = dir_specs(0), dir_specs(1)
    dq_shape = jax.ShapeDtypeStruct((B, T, 3 * HD), F32)
    dg_shape = jax.ShapeDtypeStruct((B, T, LANES), F32)
    return pl.pallas_call(
        body, name="delta_bwd", grid=(B, nch),
        in_specs=in0 + in1, out_specs=(dq_s0, dq_s1, dg_s0, dg_s1),
        out_shape=(dq_shape, dq_shape, dg_shape, dg_shape),
        scratch_shapes=[pltpu.VMEM((2, H, hd, hd), F32)],
        compiler_params=_cparams(("parallel", "arbitrary"), VMEM_LIMIT),
    )(qkvn, qkvn, qkvn, gb, sin[0], tms[0], do, qkvn, qkvn, qkvn, gb, sin[1], tms[1], do)


def _dn_out(o2, pz, zcb0, onorm, H, hd, nct, tm):
    B, T, HD = o2[0].shape
    N = T - nct * tm

    def body(o0_ref, o1_ref, z_ref, g_ref, y_ref):
        for h in range(H):
            sl = slice(h * hd, (h + 1) * hd)
            o = o0_ref[0, :, sl] + o1_ref[0, :, sl]
            r = lax.rsqrt(jnp.mean(o * o, axis=-1, keepdims=True) + EPS)
            y_ref[0, :, sl] = (o * r * g_ref[...] * _silu(z_ref[0, :, sl])).astype(BF16)

    return pl.pallas_call(
        body, name="dn_out", grid=(B, N // tm),
        in_specs=[pl.BlockSpec((1, tm, HD), lambda b, t: (b, t + nct, 0)),
                  pl.BlockSpec((1, tm, HD), lambda b, t: (b, t + nct, 0)),
                  pl.BlockSpec((1, tm, HD), lambda b, t: (b, t + nct, zcb0)),
                  pl.BlockSpec((1, hd), lambda b, t: (0, 0))],
        out_specs=pl.BlockSpec((1, tm, HD), lambda b, t: (b, t, 0)),
        out_shape=jax.ShapeDtypeStruct((B, N, HD), BF16),
        compiler_params=_cparams(("parallel",) * 2, VMEM_LIMIT))(o2[0], o2[1], pz, onorm)


def _dn_out_bwd(o2, pz, zcb0, onorm, dy, H, hd, nct, tm):
    B, T, HD = o2[0].shape
    nt = T // tm

    def body(o0_ref, o1_ref, z_ref, g_ref, dy_ref, do_ref, dz_ref, acc_ref):
        t = pl.program_id(1)

        @pl.when(t == 0)
        def _():
            acc_ref[...] = jnp.zeros_like(acc_ref)

        @pl.when(t < nct)
        def _():
            do_ref[0] = jnp.zeros_like(do_ref[0])
            dz_ref[0] = jnp.zeros_like(dz_ref[0])

        @pl.when(t >= nct)
        def _():
            g = g_ref[...]
            for h in range(H):
                sl = slice(h * hd, (h + 1) * hd)
                o = o0_ref[0, :, sl] + o1_ref[0, :, sl]
                z = z_ref[0, :, sl]
                dy_ = dy_ref[0, :, sl]
                r = lax.rsqrt(jnp.mean(o * o, axis=-1, keepdims=True) + EPS)
                oh = o * r
                dz_ref[0, :, sl] = (dy_ * oh * g * _dsilu(z)).astype(BF16)
                dn = dy_ * _silu(z)
                acc_ref[0, 0:1, :] += jnp.sum(dn * oh, axis=0, keepdims=True)
                doh = dn * g
                do_ref[0, :, sl] = r * (doh - oh * jnp.mean(doh * oh, axis=-1, keepdims=True))

    lat = lambda b, t: (b, jnp.maximum(t - nct, 0), 0)
    row = pl.BlockSpec((1, tm, HD), lambda b, t: (b, t, 0))
    return pl.pallas_call(
        body, name="dn_out_bwd", grid=(B, nt),
        in_specs=[row, row,
                  pl.BlockSpec((1, tm, HD), lambda b, t: (b, t, zcb0)),
                  pl.BlockSpec((1, hd), lambda b, t: (0, 0)),
                  pl.BlockSpec((1, tm, HD), lat)],
        out_specs=(row, row, pl.BlockSpec((1, SUBLANES, hd), lambda b, t: (b, 0, 0))),
        out_shape=(jax.ShapeDtypeStruct((B, T, HD), F32), jax.ShapeDtypeStruct((B, T, HD), BF16),
                   jax.ShapeDtypeStruct((B, SUBLANES, hd), F32)),
        compiler_params=_cparams(("parallel", "arbitrary"), VMEM_LIMIT),
    )(o2[0], o2[1], pz, onorm, dy)


def _lru_gate_math(x, wr, wi, br, bi, lam):
    r = _sigmoid(_dot(x, wr) + br)
    i = _sigmoid(_dot(x, wi) + bi)
    sp = _softplus(-lam)
    la = -LRU_C * r * sp
    a = jnp.exp(la)
    s = jnp.sqrt(-jnp.tanh(la) * (a * a + 1.0))
    return r, i, sp, a, s


def _lru_gates(xc, wbd, bias4, lam, tm, bw):
    B, T, W = xc.shape

    def body(x_ref, w_ref, b_ref, l_ref, a_ref, i_ref):
        x = x_ref[0]
        for d in range(2):
            _, i, _, a, s = _lru_gate_math(x, w_ref[2 * d, 0], w_ref[2 * d + 1, 0],
                                           b_ref[2 * d:2 * d + 1, :], b_ref[2 * d + 1:2 * d + 2, :], l_ref[d:d + 1, :])
            a_ref[d, 0] = a
            i_ref[d, 0] = s * (i * x)

    out = pl.BlockSpec((2, 1, tm, bw), lambda b, t, j: (0, b, t, j))
    return pl.pallas_call(
        body, name="lru_gates", grid=(B, T // tm, W // bw),
        in_specs=[pl.BlockSpec((1, tm, bw), lambda b, t, j: (b, t, j)),
                  pl.BlockSpec((4, 1, bw, bw), lambda b, t, j: (0, j, 0, 0)),
                  pl.BlockSpec((4, bw), lambda b, t, j: (0, j)),
                  pl.BlockSpec((2, bw), lambda b, t, j: (0, j))],
        out_specs=(out, out),
        out_shape=(jax.ShapeDtypeStruct((2, B, T, W), F32), jax.ShapeDtypeStruct((2, B, T, W), F32)),
        compiler_params=_cparams(("parallel",) * 3, VMEM_LIMIT))(xc, wbd, bias4, lam)


def _lru_gates_bwd(xc, wbd, bias4, lam, dinp, da, tm, bw):
    B, T, W = xc.shape
    nt = T // tm

    def body(x_ref, w_ref, b_ref, l_ref, di0_ref, di1_ref, da0_ref, da1_ref, dx_ref, dw_ref, acc_ref):
        di_refs, da_refs = (di0_ref, di1_ref), (da0_ref, da1_ref)
        b_ = pl.program_id(1)
        t = pl.program_id(2)

        @pl.when(jnp.logical_and(b_ == 0, t == 0))
        def _():
            dw_ref[...] = jnp.zeros_like(dw_ref)
            acc_ref[...] = jnp.zeros_like(acc_ref)

        x = x_ref[0]
        dx = jnp.zeros_like(x)
        for d in range(2):
            wr, wi = w_ref[2 * d, 0], w_ref[2 * d + 1, 0]
            lam_ = l_ref[d:d + 1, :]
            r, i, sp, a, s = _lru_gate_math(x, wr, wi, b_ref[2 * d:2 * d + 1, :], b_ref[2 * d + 1:2 * d + 2, :], lam_)
            dinp_ = di_refs[d][0]
            dix = dinp_ * s
            ds = dinp_ * i * x
            dla = da_refs[d][0] * a - ds * (a * a) / s
            dpr = dla * (-LRU_C * sp) * r * (1.0 - r)
            dpi = dix * x * i * (1.0 - i)
            dx = dx + dix * i + _dot(dpr, wr, "nt") + _dot(dpi, wi, "nt")
            dw_ref[2 * d, 0] += _dot(x, dpr, "tn")
            dw_ref[2 * d + 1, 0] += _dot(x, dpi, "tn")
            acc_ref[2 * d:2 * d + 1, :] += jnp.sum(dpr, axis=0, keepdims=True)
            acc_ref[2 * d + 1:2 * d + 2, :] += jnp.sum(dpi, axis=0, keepdims=True)
            acc_ref[4 + d:5 + d, :] += jnp.sum(dla * (-LRU_C * r), axis=0, keepdims=True) * (-_sigmoid(-lam_))
        dx_ref[0] = dx

    tile = pl.BlockSpec((1, tm, bw), lambda j, b, t: (b, t, j))
    return pl.pallas_call(
        body, name="lru_gates_bwd", grid=(W // bw, B, nt),
        in_specs=[pl.BlockSpec((1, tm, bw), lambda j, b, t: (b, t, j)),
                  pl.BlockSpec((4, 1, bw, bw), lambda j, b, t: (0, j, 0, 0)),
                  pl.BlockSpec((4, bw), lambda j, b, t: (0, j)),
                  pl.BlockSpec((2, bw), lambda j, b, t: (0, j)), tile, tile, tile, tile],
        out_specs=(pl.BlockSpec((1, tm, bw), lambda j, b, t: (b, t, j)),
                   pl.BlockSpec((4, 1, bw, bw), lambda j, b, t: (0, j, 0, 0)),
                   pl.BlockSpec((SUBLANES, bw), lambda j, b, t: (0, j))),
        out_shape=(jax.ShapeDtypeStruct((B, T, W), F32), jax.ShapeDtypeStruct(wbd.shape, F32),
                   jax.ShapeDtypeStruct((SUBLANES, W), F32)),
        compiler_params=_cparams(("parallel", "arbitrary", "arbitrary"), VMEM_LIMIT),
    )(xc, wbd, bias4, lam, dinp[0], dinp[1], da[0], da[1])


def _tile_scan(a, x, rev, tm):
    row = lax.broadcasted_iota(jnp.int32, a.shape, 0)
    s = 1
    while s < tm:
        if rev:
            a_sh, x_sh, ok = pltpu.roll(a, tm - s, 0), pltpu.roll(x, tm - s, 0), row < tm - s
        else:
            a_sh, x_sh, ok = pltpu.roll(a, s, 0), pltpu.roll(x, s, 0), row >= s
        x = jnp.where(ok, x + a * x_sh, x)
        a = jnp.where(ok, a * a_sh, a)
        s *= 2
    return a, x


def _scan_tile_of(s, rev, nct, nt):
    if not rev:
        return s
    return jnp.where(s < nct, nct - 1 - s, nt - 1 - (s - nct))


def _lru_scan(a2, x2, d, nct, tm, tc):
    _, B, T, W = a2.shape
    nt = T // tm
    rev = d == 1
    last = 0 if rev else tm - 1

    def body(a_ref, x_ref, h_ref, carry):
        s = pl.program_id(2)

        @pl.when(s == 0)
        def _():
            carry[...] = jnp.zeros_like(carry)

        A, X = _tile_scan(a_ref[0, 0], x_ref[0, 0], rev, tm)
        h = X + A * carry[0:1, :]
        h_ref[0] = h
        carry[...] = jnp.broadcast_to(h[last:last + 1, :], carry.shape)

    tidx = lambda s: _scan_tile_of(s, rev, nct, nt)
    spec = pl.BlockSpec((1, 1, tm, tc), lambda b, j, s: (d, b, tidx(s), j))
    return pl.pallas_call(
        body, name=f"lru_scan{d}", grid=(B, W // tc, nt),
        in_specs=[spec, spec], out_specs=pl.BlockSpec((1, tm, tc), lambda b, j, s: (b, tidx(s), j)),
        out_shape=jax.ShapeDtypeStruct((B, T, W), F32),
        scratch_shapes=[pltpu.VMEM((SUBLANES, tc), F32)],
        compiler_params=_cparams(("parallel", "parallel", "arbitrary"), VMEM_LIMIT),
    )(a2, x2)


def _lru_scan_bwd(a2, h, dh, d, nct, tm, tc):
    _, B, T, W = a2.shape
    nt = T // tm
    rev = d == 1
    first = tm - 1 if rev else 0
    r8, n8 = tm // SUBLANES, T // SUBLANES

    def body(a_ref, h_ref, hp_ref, dh_ref, lam_ref, da_ref, ca, cl):
        s = pl.program_id(2)

        @pl.when(s == 0)
        def _():
            ca[...] = jnp.zeros_like(ca)
            cl[...] = jnp.zeros_like(cl)

        a = a_ref[0, 0]
        row = lax.broadcasted_iota(jnp.int32, a.shape, 0)
        if rev:
            c = jnp.where(row == 0, ca[0:1, :], pltpu.roll(a, 1, 0))
        else:
            c = jnp.where(row == tm - 1, ca[0:1, :], pltpu.roll(a, tm - 1, 0))
        C, L = _tile_scan(c, dh_ref[0], not rev, tm)
        lam = L + C * cl[0:1, :]
        lam_ref[0] = lam
        hp = jnp.where(s == nt - 1, 0.0, hp_ref[0])
        if rev:
            hprev = jnp.where(row == tm - 1, hp[0:1, :], pltpu.roll(h_ref[0], tm - 1, 0))
        else:
            hprev = jnp.where(row == 0, hp[SUBLANES - 1:SUBLANES, :], pltpu.roll(h_ref[0], 1, 0))
        da_ref[0] = lam * hprev
        ca[...] = jnp.broadcast_to(a[first:first + 1, :], ca.shape)
        cl[...] = jnp.broadcast_to(lam[first:first + 1, :], cl.shape)

    tidx = lambda s: _scan_tile_of(nt - 1 - s, rev, nct, nt)

    def hp_idx(b, j, s):
        tt = tidx(s)
        if rev:
            blk = jnp.where(tt == nt - 1, 0, jnp.minimum((tt + 1) * r8, n8 - 1))
        else:
            blk = jnp.maximum(tt * r8 - 1, 0)
        return (b, blk, j)

    tile = pl.BlockSpec((1, tm, tc), lambda b, j, s: (b, tidx(s), j))
    return pl.pallas_call(
        body, name=f"lru_scan_bwd{d}", grid=(B, W // tc, nt),
        in_specs=[pl.BlockSpec((1, 1, tm, tc), lambda b, j, s: (d, b, tidx(s), j)), tile,
                  pl.BlockSpec((1, SUBLANES, tc), hp_idx), tile],
        out_specs=(tile, tile),
        out_shape=(jax.ShapeDtypeStruct((B, T, W), F32), jax.ShapeDtypeStruct((B, T, W), F32)),
        scratch_shapes=[pltpu.VMEM((SUBLANES, tc), F32), pltpu.VMEM((SUBLANES, tc), F32)],
        compiler_params=_cparams(("parallel", "parallel", "arbitrary"), VMEM_LIMIT),
    )(a2, h, h, dh)


def _lru_out(h0, h1, pyl, ycb0, nct, tm, tc):
    B, T, W = h0.shape
    N = T - nct * tm

    def body(h0_ref, h1_ref, y_ref, o_ref):
        o_ref[0] = ((h0_ref[0] + h1_ref[0]) * _gelu(y_ref[0])).astype(BF16)

    hs = pl.BlockSpec((1, tm, tc), lambda b, t, j: (b, t + nct, j))
    return pl.pallas_call(
        body, name="lru_out", grid=(B, N // tm, W // tc),
        in_specs=[hs, hs, pl.BlockSpec((1, tm, tc), lambda b, t, j: (b, t + nct, ycb0 + j))],
        out_specs=pl.BlockSpec((1, tm, tc), lambda b, t, j: (b, t, j)),
        out_shape=jax.ShapeDtypeStruct((B, N, W), BF16),
        compiler_params=_cparams(("parallel",) * 3, VMEM_LIMIT))(h0, h1, pyl)


def _lru_out_bwd(h0, h1, pyl, ycb0, dy, nct, tm, tc):
    B, T, W = h0.shape

    def body(h0_ref, h1_ref, y_ref, dy_ref, dh_ref, dyl_ref):
        t = pl.program_id(1)

        @pl.when(t < nct)
        def _():
            dh_ref[0] = jnp.zeros_like(dh_ref[0])
            dyl_ref[0] = jnp.zeros_like(dyl_ref[0])

        @pl.when(t >= nct)
        def _():
            y = y_ref[0]
            dy_ = dy_ref[0]
            dh_ref[0] = dy_ * _gelu(y)
            dyl_ref[0] = (dy_ * (h0_ref[0] + h1_ref[0]) * _dgelu(y)).astype(BF16)

    hs = pl.BlockSpec((1, tm, tc), lambda b, t, j: (b, t, j))
    return pl.pallas_call(
        body, name="lru_out_bwd", grid=(B, T // tm, W // tc),
        in_specs=[hs, hs, pl.BlockSpec((1, tm, tc), lambda b, t, j: (b, t, ycb0 + j)),
                  pl.BlockSpec((1, tm, tc), lambda b, t, j: (b, jnp.maximum(t - nct, 0), j))],
        out_specs=(hs, hs),
        out_shape=(jax.ShapeDtypeStruct((B, T, W), F32), jax.ShapeDtypeStruct((B, T, W), BF16)),
        compiler_params=_cparams(("parallel",) * 3, VMEM_LIMIT))(h0, h1, pyl, dy)


def _merge(bd, bl, pmg, bm, nct, tm):
    B, N, D = bd.shape

    def body(bd_ref, bl_ref, m0_ref, m1_ref, b_ref, o_ref):
        g0 = _sigmoid(m0_ref[0] + b_ref[:, 0:D])
        g1 = _sigmoid(m1_ref[0] + b_ref[:, D:2 * D])
        o_ref[0] = (g0 * bd_ref[0] + g1 * bl_ref[0]).astype(BF16)

    row = pl.BlockSpec((1, tm, D), lambda b, t: (b, t, 0))
    return pl.pallas_call(
        body, name="merge", grid=(B, N // tm),
        in_specs=[row, row, pl.BlockSpec((1, tm, D), lambda b, t: (b, t + nct, 0)),
                  pl.BlockSpec((1, tm, D), lambda b, t: (b, t + nct, 1)),
                  pl.BlockSpec((1, 2 * D), lambda b, t: (0, 0))],
        out_specs=row, out_shape=jax.ShapeDtypeStruct((B, N, D), BF16),
        compiler_params=_cparams(("parallel", "parallel"), VMEM_LIMIT))(bd, bl, pmg, pmg, bm)


def _merge_bwd(dmi, bd, bl, pmg, bm, nct, tm):
    B, N, D = bd.shape
    T = pmg.shape[1]

    def body(dm_ref, bd_ref, bl_ref, m0_ref, m1_ref, b_ref, dbd_ref, dbl_ref, dmg_ref, acc_ref):
        t = pl.program_id(1)

        @pl.when(t == 0)
        def _():
            acc_ref[...] = jnp.zeros_like(acc_ref)

        @pl.when(t < nct)
        def _():
            dmg_ref[0] = jnp.zeros_like(dmg_ref[0])

        @pl.when(t >= nct)
        def _():
            dm = dm_ref[0]
            g0 = _sigmoid(m0_ref[0] + b_ref[:, 0:D])
            g1 = _sigmoid(m1_ref[0] + b_ref[:, D:2 * D])
            dbd_ref[0] = (dm * g0).astype(BF16)
            dbl_ref[0] = (dm * g1).astype(BF16)
            d0 = dm * bd_ref[0] * g0 * (1.0 - g0)
            d1 = dm * bl_ref[0] * g1 * (1.0 - g1)
            dmg_ref[0, :, 0:D] = d0.astype(BF16)
            dmg_ref[0, :, D:2 * D] = d1.astype(BF16)
            acc_ref[0, 0:1, 0:D] += jnp.sum(d0, axis=0, keepdims=True)
            acc_ref[0, 0:1, D:2 * D] += jnp.sum(d1, axis=0, keepdims=True)

    lat = pl.BlockSpec((1, tm, D), lambda b, t: (b, jnp.maximum(t - nct, 0), 0))
    return pl.pallas_call(
        body, name="merge_bwd", grid=(B, T // tm),
        in_specs=[lat, lat, lat, pl.BlockSpec((1, tm, D), lambda b, t: (b, t, 0)),
                  pl.BlockSpec((1, tm, D), lambda b, t: (b, t, 1)),
                  pl.BlockSpec((1, 2 * D), lambda b, t: (0, 0))],
        out_specs=(lat, lat, pl.BlockSpec((1, tm, 2 * D), lambda b, t: (b, t, 0)),
                   pl.BlockSpec((1, SUBLANES, 2 * D), lambda b, t: (b, 0, 0))),
        out_shape=(jax.ShapeDtypeStruct((B, N, D), BF16), jax.ShapeDtypeStruct((B, N, D), BF16),
                   jax.ShapeDtypeStruct((B, T, 2 * D), BF16), jax.ShapeDtypeStruct((B, SUBLANES, 2 * D), F32)),
        compiler_params=_cparams(("parallel", "arbitrary"), VMEM_LIMIT))(dmi, bd, bl, pmg, pmg, bm)


def _grid_taps():
    return [((di + 1) * 3 + (dj + 1), di, dj) for di in (-1, 0, 1) for dj in (-1, 0, 1)]


def _grid_views(x, n):
    pos = lax.broadcasted_iota(jnp.int32, x.shape, 0)
    gc = jnp.bitwise_and(pos, GRID_W - 1)
    cols = {0: x,
            -1: jnp.where(gc >= 1, pltpu.roll(x, 1, 0), 0.0),
            1: jnp.where(gc <= GRID_W - 2, pltpu.roll(x, n - 1, 0), 0.0)}
    views = {}
    edge = jnp.zeros((GRID_W, x.shape[1]), x.dtype)
    for dj, xc in cols.items():
        views[(0, dj)] = xc
        views[(-1, dj)] = jnp.concatenate([edge, xc[:n - GRID_W]], axis=0)
        views[(1, dj)] = jnp.concatenate([xc[GRID_W:], edge], axis=0)
    return views


def _ffn_act(F, w9, b, tc):
    B, N, C2 = F.shape
    Cf = C2 // 2
    ncb = Cf // tc

    def body(g_ref, v_ref, w_ref, b_ref, o_ref, gel_ref, dgel_ref):
        xv = _grid_views(g_ref[0], N)
        conv = b_ref[...]
        for k, di, dj in _grid_taps():
            conv = conv + xv[(di, dj)] * w_ref[k:k + 1, :]
        th = jnp.tanh(GELU_C * (conv + 0.044715 * conv * conv * conv))
        gel = 0.5 * conv * (1.0 + th)
        o_ref[0] = (gel * v_ref[0]).astype(BF16)
        gel_ref[0] = gel.astype(BF16)
        dgel_ref[0] = (0.5 * (1.0 + th)
                       + 0.5 * conv * (1.0 - th * th) * GELU_C * (1.0 + 3.0 * 0.044715 * conv * conv)).astype(BF16)

    tile = pl.BlockSpec((1, N, tc), lambda b, j: (b, 0, j))
    out = jax.ShapeDtypeStruct((B, N, Cf), BF16)
    return pl.pallas_call(
        body, name="ffn_act", grid=(B, ncb),
        in_specs=[tile, pl.BlockSpec((1, N, tc), lambda b, j: (b, 0, ncb + j)),
                  pl.BlockSpec((9, tc), lambda b, j: (0, j)),
                  pl.BlockSpec((1, tc), lambda b, j: (0, j))],
        out_specs=(tile, tile, tile), out_shape=(out, out, out),
        compiler_params=_cparams(("parallel", "parallel"), VMEM_LIMIT))(F, F, w9, b)


def _ffn_act_bwd(F, w9, gel, dgel, df, tc):
    B, N, C2 = F.shape
    Cf = C2 // 2
    ncb = Cf // tc

    def body(g_ref, v_ref, w_ref, gel_ref, dgel_ref, df_ref, dF_ref, acc_ref, dv_s):
        p = pl.program_id(2)

        @pl.when(p == 0)
        def _():
            xv = _grid_views(g_ref[0], N)
            df_ = df_ref[0]
            dv_s[...] = (df_ * gel_ref[0].astype(F32)).astype(BF16)
            dc = df_ * v_ref[0] * dgel_ref[0].astype(F32)
            dcv = _grid_views(dc, N)
            dx = None
            for k, di, dj in _grid_taps():
                term = dcv[(-di, -dj)] * w_ref[k:k + 1, :]
                dx = term if dx is None else dx + term
                acc_ref[0, k:k + 1, :] = jnp.sum(dc * xv[(di, dj)], axis=0, keepdims=True)
            acc_ref[0, 9:10, :] = jnp.sum(dc, axis=0, keepdims=True)
            acc_ref[0, 10:16, :] = jnp.zeros((6, tc), F32)
            dF_ref[0] = dx.astype(BF16)

        @pl.when(p == 1)
        def _():
            dF_ref[0] = dv_s[...]

    tile = pl.BlockSpec((1, N, tc), lambda b, j, p: (b, 0, j))
    return pl.pallas_call(
        body, name="ffn_act_bwd", grid=(B, ncb, 2),
        in_specs=[tile, pl.BlockSpec((1, N, tc), lambda b, j, p: (b, 0, ncb + j)),
                  pl.BlockSpec((9, tc), lambda b, j, p: (0, j)), tile, tile, tile],
        out_specs=(pl.BlockSpec((1, N, tc), lambda b, j, p: (b, 0, j + p * ncb)),
                   pl.BlockSpec((1, 16, tc), lambda b, j, p: (b, 0, j))),
        out_shape=(jax.ShapeDtypeStruct((B, N, C2), BF16), jax.ShapeDtypeStruct((B, 16, Cf), F32)),
        scratch_shapes=[pltpu.VMEM((N, tc), BF16)],
        compiler_params=_cparams(("parallel", "parallel", "arbitrary"), VMEM_LIMIT))(F, F, w9, gel, dgel, df)


ADAM_ROWS = 512


def _adamw(w, m, v, gparts, name):
    rows, cols = w.shape[-2:]
    P = gparts.shape[0]
    tr = _row_tile(rows, (P + 14) * 4 * (-(-cols // LANES) * LANES))
    c1 = 1.0 - ADAM_B1 ** ADAM_STEP
    c2 = 1.0 - ADAM_B2 ** ADAM_STEP

    def body(w_ref, m_ref, v_ref, g_ref, go_ref, d_ref, mo_ref, vo_ref):
        g = g_ref[0].astype(F32)
        for p in range(1, P):
            g = g + g_ref[p].astype(F32)
        m_ = ADAM_B1 * m_ref[...] + (1.0 - ADAM_B1) * g
        v_ = ADAM_B2 * v_ref[...] + (1.0 - ADAM_B2) * (g * g)
        go_ref[...] = g
        mo_ref[...] = m_
        vo_ref[...] = v_
        d_ref[...] = -ADAM_LR * ((m_ / c1) / (jnp.sqrt(v_ / c2) + ADAM_EPS) + ADAM_WD * w_ref[...])

    if w.ndim == 3:
        row = pl.BlockSpec((None, tr, cols), lambda i: (0, i, 0))
    else:
        row = pl.BlockSpec((tr, cols), lambda i: (i, 0))
    out = jax.ShapeDtypeStruct(w.shape, F32)
    return pl.pallas_call(
        body, name=name, grid=(rows // tr,),
        in_specs=[row, row, row, pl.BlockSpec((P, tr, cols), lambda i: (0, i, 0))],
        out_specs=(row, row, row, row), out_shape=(out, out, out, out),
        compiler_params=_cparams(("parallel",), VMEM_LIMIT))(w, m, v, gparts)


def _pack_rows(flat_len):
    rows = -(-flat_len // LANES)
    if rows > ADAM_ROWS:
        rows = -(-rows // ADAM_ROWS) * ADAM_ROWS
    else:
        rows = -(-rows // SUBLANES) * SUBLANES
    return rows


PACK_ALIGN = SUBLANES * LANES


def _pack(arrs, lead=()):
    pads = [(0, 0)] * len(lead)
    parts = []
    for a in arrs:
        f = a.reshape(lead + (-1,)).astype(F32)
        parts.append(jnp.pad(f, pads + [(0, -f.shape[-1] % PACK_ALIGN)]))
    flat = jnp.concatenate(parts, axis=-1)
    n = flat.shape[-1]
    rows = _pack_rows(n)
    flat = jnp.pad(flat, pads + [(0, rows * LANES - n)])
    return flat.reshape(lead + (rows, LANES))


def _unpack(buf, shapes):
    out, r = [], 0
    for s in shapes:
        n = math.prod(s)
        nr = -(-n // PACK_ALIGN) * SUBLANES
        out.append(buf[r:r + nr].reshape(-1)[:n].reshape(s))
        r += nr
    return out


def _col_shards(full, n_lead):
    C = full.shape[-1]
    x = full.reshape(full.shape[:-1] + (N_DEV, C // N_DEV))
    return jnp.moveaxis(x, -2, 0)


def _from_col_shards(g):
    x = jnp.moveaxis(g, 0, -2)
    return x.reshape(x.shape[:-2] + (x.shape[-2] * x.shape[-1],))


def kernel(x, c, ctx, c_ctx, w_ada, b_ada, g_pre_mix, g_post_mix, g_pre_ffn, g_post_ffn, w_in, b_merge, dn_conv, dn_a_log, dn_dt_bias, dn_onorm, lru_conv, lru_conv_b, lru_w_rg, lru_b_rg, lru_w_ig, lru_b_ig, lru_lambda, w_branch_dn, w_branch_lru, w_out, w_up, ffn_dw, ffn_dw_b, w_down, loss_target, m_c_ctx, m_w_ada, m_b_ada, m_g_pre_mix, m_g_post_mix, m_g_pre_ffn, m_g_post_ffn, m_w_in, m_b_merge, m_dn_conv, m_dn_a_log, m_dn_dt_bias, m_dn_onorm, m_lru_conv, m_lru_conv_b, m_lru_w_rg, m_lru_b_rg, m_lru_w_ig, m_lru_b_ig, m_lru_lambda, m_w_branch_dn, m_w_branch_lru, m_w_out, m_w_up, m_ffn_dw, m_ffn_dw_b, m_w_down, v_c_ctx, v_w_ada, v_b_ada, v_g_pre_mix, v_g_post_mix, v_g_pre_ffn, v_g_post_ffn, v_w_in, v_b_merge, v_dn_conv, v_dn_a_log, v_dn_dt_bias, v_dn_onorm, v_lru_conv, v_lru_conv_b, v_lru_w_rg, v_lru_b_rg, v_lru_w_ig, v_lru_b_ig, v_lru_lambda, v_w_branch_dn, v_w_branch_lru, v_w_out, v_w_up, v_ffn_dw, v_ffn_dw_b, v_w_down):
    params = dict(c_ctx=c_ctx, w_ada=w_ada, b_ada=b_ada, g_pre_mix=g_pre_mix, g_post_mix=g_post_mix, g_pre_ffn=g_pre_ffn, g_post_ffn=g_post_ffn, w_in=w_in, b_merge=b_merge, dn_conv=dn_conv, dn_a_log=dn_a_log, dn_dt_bias=dn_dt_bias, dn_onorm=dn_onorm, lru_conv=lru_conv, lru_conv_b=lru_conv_b, lru_w_rg=lru_w_rg, lru_b_rg=lru_b_rg, lru_w_ig=lru_w_ig, lru_b_ig=lru_b_ig, lru_lambda=lru_lambda, w_branch_dn=w_branch_dn, w_branch_lru=w_branch_lru, w_out=w_out, w_up=w_up, ffn_dw=ffn_dw, ffn_dw_b=ffn_dw_b, w_down=w_down)
    mom1 = dict(c_ctx=m_c_ctx, w_ada=m_w_ada, b_ada=m_b_ada, g_pre_mix=m_g_pre_mix, g_post_mix=m_g_post_mix, g_pre_ffn=m_g_pre_ffn, g_post_ffn=m_g_post_ffn, w_in=m_w_in, b_merge=m_b_merge, dn_conv=m_dn_conv, dn_a_log=m_dn_a_log, dn_dt_bias=m_dn_dt_bias, dn_onorm=m_dn_onorm, lru_conv=m_lru_conv, lru_conv_b=m_lru_conv_b, lru_w_rg=m_lru_w_rg, lru_b_rg=m_lru_b_rg, lru_w_ig=m_lru_w_ig, lru_b_ig=m_lru_b_ig, lru_lambda=m_lru_lambda, w_branch_dn=m_w_branch_dn, w_branch_lru=m_w_branch_lru, w_out=m_w_out, w_up=m_w_up, ffn_dw=m_ffn_dw, ffn_dw_b=m_ffn_dw_b, w_down=m_w_down)
    mom2 = dict(c_ctx=v_c_ctx, w_ada=v_w_ada, b_ada=v_b_ada, g_pre_mix=v_g_pre_mix, g_post_mix=v_g_post_mix, g_pre_ffn=v_g_pre_ffn, g_post_ffn=v_g_post_ffn, w_in=v_w_in, b_merge=v_b_merge, dn_conv=v_dn_conv, dn_a_log=v_dn_a_log, dn_dt_bias=v_dn_dt_bias, dn_onorm=v_dn_onorm, lru_conv=v_lru_conv, lru_conv_b=v_lru_conv_b, lru_w_rg=v_lru_w_rg, lru_b_rg=v_lru_b_rg, lru_w_ig=v_lru_w_ig, lru_b_ig=v_lru_b_ig, lru_lambda=v_lru_lambda, w_branch_dn=v_w_branch_dn, w_branch_lru=v_w_branch_lru, w_out=v_w_out, w_up=v_w_up, ffn_dw=v_ffn_dw, ffn_dw_b=v_ffn_dw_b, w_down=v_w_down)
    names = list(params)

    B, N, D = x.shape
    NC = ctx.shape[1]
    T = NC + N
    H, hd = dn_a_log.shape[2], dn_onorm.shape[1]
    HD = H * hd
    nd = 2 * H
    W = lru_conv_b.shape[1]
    nblk, bdim = lru_w_rg.shape[2], lru_w_rg.shape[3]
    Cf = ffn_dw_b.shape[1]
    sw = w_ada.shape[2]
    tm = min(256, NC)
    nct = NC // tm
    ncc = NC // CHUNK
    nch = T // CHUNK
    R, RL = B * T, B * N
    bw = min(256, W)
    me = _my_index()
    assert NC % tm == 0 and N % tm == 0 and B + 1 <= SUBLANES and bw % bdim == 0 and D % LANES == 0

    cpad = jnp.zeros((SUBLANES, D), F32).at[:B].set(c).at[B].set(c_ctx)
    ccp = _all_gather([cpad], "ag_cond")[0].reshape(N_DEV * SUBLANES, D)
    mods_sh = _ada_fwd(ccp, w_ada[0], lax.dynamic_slice(b_ada, (0, me * sw), (1, sw)))
    lru_vecs = jnp.concatenate([lru_b_rg[0], lru_b_ig[0], lru_lambda[0], jnp.zeros_like(lru_lambda[0])], axis=0)
    mods_g, w_in_g, dn_conv_g, lru_conv_g, lru_vecs_g = _all_gather(
        [mods_sh, w_in[0].astype(BF16).T, dn_conv[0], lru_conv[0], lru_vecs], "ag_first")
    ag_mix = _split_start([w_branch_dn[0].astype(BF16), w_branch_lru[0].astype(BF16), w_out[0].astype(BF16)],
                          "gather", "ag_mix_start")
    ag_ffn = _split_start([w_up[0].astype(BF16) + ag_mix[4][0, 0].astype(BF16), w_down[0].astype(BF16),
                           ffn_dw[0].reshape(9, -1)], "gather", "ag_ffn_start")
    mine = lax.dynamic_slice(mods_g, (0, me * SUBLANES, 0), (N_DEV, SUBLANES, sw))
    mods = jnp.moveaxis(mine, 0, 1).reshape(SUBLANES, 6, D)[:B + 1]
    mods = jnp.pad(mods, ((0, 0), (0, SUBLANES - 6), (0, 0))) + ag_ffn[4][0, 0]
    dn_conv_f = _from_col_shards(dn_conv_g)
    lru_conv_f = _from_col_shards(lru_conv_g)
    lru_vecs_f = _from_col_shards(lru_vecs_g)
    lru_lam_f = lru_vecs_f[4:6]

    csh = w_in_g.shape[1]
    w_in_t = w_in_g.reshape(N_DEV * csh, D)
    o_z, o_a, o_xl = 3 * HD, 4 * HD, 4 * HD + 2 * nd
    o_yl, o_mg = o_xl + W, o_xl + 2 * W
    w_qkv, w_z = w_in_t[:o_z], w_in_t[o_z:o_a]
    w_ab = jnp.pad(w_in_t[o_a:o_xl], ((0, LANES - 2 * nd), (0, 0)))
    w_xl, w_yl, w_mg = w_in_t[o_xl:o_yl], w_in_t[o_yl:o_mg], w_in_t[o_mg:]

    def blockdiag(wg):
        per = bw // bdim
        g5 = wg.reshape(2, W // bw, per, bdim, bdim)
        eye = jnp.eye(per, dtype=wg.dtype)
        return jnp.einsum("dtpij,pq->dtpiqj", g5, eye).reshape(2, W // bw, bw, bw)

    bd_rg, bd_ig = blockdiag(lru_w_rg[0]), blockdiag(lru_w_ig[0])
    wbd = jnp.stack([bd_rg[0], bd_ig[0], bd_rg[1], bd_ig[1]]).astype(BF16)
    bias4 = jnp.stack([lru_vecs_f[0], lru_vecs_f[2], lru_vecs_f[1], lru_vecs_f[3]])
    alog_v = jnp.pad(dn_a_log.reshape(1, nd), ((0, 0), (0, LANES - nd)))
    dtb_v = jnp.pad(dn_dt_bias.reshape(1, nd), ((0, 0), (0, LANES - nd)))

    h0 = jnp.concatenate([ctx, x], axis=1)
    u1 = _norm_mod_fwd(h0, g_pre_mix, mods, 0, 1, nct, tm, "norm_mod1")
    u1f = u1.reshape(R, D)
    p_qkv = _mm(u1f, w_qkv, "nt", "mm_qkv").reshape(B, T, 3 * HD)
    p_z = _mm(u1f, w_z, "nt", "mm_z").reshape(B, T, HD)
    p_ab = _mm(u1f, w_ab, "nt", "mm_ab")
    p_xl = _mm(u1f, w_xl, "nt", "mm_xl").reshape(B, T, W)
    p_yl = _mm(u1f, w_yl, "nt", "mm_yl").reshape(B, T, W)
    p_mg = _mm(u1f, w_mg, "nt", "mm_mg").reshape(B, T, 2 * D)

    conv_offs = (-2, -1, 0, 1)
    tcc = _tile(HD, 1024)
    gb = _ab_act(p_ab, alog_v, dtb_v, nd, tm)
    gb3 = gb.reshape(B, T, LANES)
    cv, qkvn = _dwconv(p_qkv, 0, 3 * HD, dn_conv_f, None, conv_offs, nct, tm, HD, "dn_conv", qkv_heads=(H, hd))
    o_d0, o_d1, s_in0, s_in1, tm_d0, tm_d1 = _delta_fwd(qkvn, gb3, H, hd, ncc)
    o2 = (o_d0, o_d1)
    y_dn = _dn_out(o2, p_z, 0, dn_onorm, H, hd, nct, tm)

    tcw = _tile(W, 1024)
    xc = _dwconv(p_xl, 0, W, lru_conv_f, lru_conv_b, conv_offs, nct, tm, tcw, "lru_conv")
    tmg = max(t_ for t_ in range(SUBLANES, min(T, 768) + 1, SUBLANES) if T % t_ == 0)
    a2, inp2 = _lru_gates(xc, wbd, bias4, lru_lam_f, tmg, bw)
    hd0 = _lru_scan(a2, inp2, 0, nct, tm, tcw)
    hd1 = _lru_scan(a2, inp2, 1, nct, tm, tcw)
    y_lru = _lru_out(hd0, hd1, p_yl, 0, nct, tm, tcw)

    def gathered(started, after, name):
        xs_, lands_ = _split_wait(started, "gather", after, name)
        return [lax.dynamic_update_slice(land, x_[None], (me,) + (0,) * x_.ndim) for land, x_ in zip(lands_, xs_)]

    w_bdn_g, w_blru_g, w_out_g = gathered(ag_mix, y_lru, "ag_mix_wait")
    w_bdn_f, w_blru_f, w_out_f = w_bdn_g.reshape(HD, D), w_blru_g.reshape(W, D), w_out_g.reshape(D, D)
    bd = _mm(y_dn.reshape(RL, HD), w_bdn_f, "nn", "mm_bdn").reshape(B, N, D)
    bl = _mm(y_lru.reshape(RL, W), w_blru_f, "nn", "mm_blru").reshape(B, N, D)
    mixin = _merge(bd, bl, p_mg, b_merge, nct, tm)
    mix = _mm(mixin.reshape(RL, D), w_out_f, "nn", "mm_out").reshape(B, N, D)
    h1 = _resid_norm_fwd(x, mix, g_post_mix, mods, 2, tm, "resid_norm1")
    u2 = _norm_mod_fwd(h1, g_pre_ffn, mods, 3, 4, 0, tm, "norm_mod2")
    w_up_g, w_down_g, ffn_dw_g = gathered(ag_ffn, u2, "ag_ffn_wait")
    w_down_f = w_down_g.reshape(Cf, D)
    ffn_dw_f = _from_col_shards(ffn_dw_g)
    Fp = _mm(u2.reshape(RL, D), w_up_g, "nn", "mm_up").reshape(B, N, 2 * Cf)
    tcf = LANES
    f, f_gel, f_dgel = _ffn_act(Fp, ffn_dw_f, ffn_dw_b, tcf)
    dproj = _mm(f.reshape(RL, Cf), w_down_f, "nn", "mm_down").reshape(B, N, D)

    dd, dh2, acc_f = _final_fwd_bwd(h1, dproj, g_post_ffn, mods, 5, loss_target, tm)
    loss = lax.psum((0.5 / D) * jnp.sum(acc_f[:, 2, :]), MESH_AXES)
    ddf = dd.reshape(RL, D)
    df = _mm(ddf, w_down_f, "nt", "mm_down_dx").reshape(B, N, Cf)
    gw_down = _mm(f.reshape(RL, Cf), ddf, "tn", "mm_down_dw", out_dtype=BF16)
    dF, acc_ffn = _ffn_act_bwd(Fp, ffn_dw_f, f_gel, f_dgel, df, tcf)
    dFf = dF.reshape(RL, 2 * Cf)
    du2 = _mm(dFf, w_up_g, "nt", "mm_up_dx").reshape(B, N, D)
    gw_up = _mm(u2.reshape(RL, D), dFf, "tn", "mm_up_dw", out_dtype=BF16, out_shards=N_DEV)
    jm = 2 * lax.axis_index("x") + lax.axis_index("y")

    def pair_sums(parts8, tag):
        sib4 = _pair_exchange(parts8, "rs_pair_" + tag)
        return [_pair_sum(p8, s4, f"rs_pair_sum_{tag}{i}") for i, (p8, s4) in enumerate(zip(parts8, sib4))]

    def own_slab(land, x4):
        idx = (jm,) + (0,) * (x4.ndim - 1)
        return lax.dynamic_update_slice(land, lax.dynamic_slice(x4, idx, (1,) + x4.shape[1:]), idx)

    q_early = _split_start(pair_sums([gw_up, gw_down.reshape(N_DEV, Cf // N_DEV, D)], "ffn"), "quad", "rs_quad_ffn_start")
    mods_b = mods + q_early[4][0, 0]
    dh1, acc2x, _ = _norm_mod_bwd(h1, du2, g_pre_ffn, mods_b, 4, 0, tm, dh2, "norm_mod2_bwd")
    dmix, acc_r1 = _resid_norm_bwd(mix, dh1, g_post_mix, mods, 2, tm, "resid_norm1_bwd")
    dmixf = dmix.reshape(RL, D)
    dmixin = _mm(dmixf, w_out_f, "nt", "mm_out_dx").reshape(B, N, D)
    gw_out = _mm(mixin.reshape(RL, D), dmixf, "tn", "mm_out_dw", out_dtype=BF16)
    dbd, dbl, dmg, acc_mg = _merge_bwd(dmixin, bd, bl, p_mg, b_merge, nct, tm)
    dy_dn = _mm(dbd.reshape(RL, D), w_bdn_f, "nt", "mm_bdn_dx").reshape(B, N, HD)
    gw_bdn = _mm(y_dn.reshape(RL, HD), dbd.reshape(RL, D), "tn", "mm_bdn_dw", out_dtype=BF16)
    dy_lru = _mm(dbl.reshape(RL, D), w_blru_f, "nt", "mm_blru_dx").reshape(B, N, W)
    gw_blru = _mm(y_lru.reshape(RL, W), dbl.reshape(RL, D), "tn", "mm_blru_dw", out_dtype=BF16)

    dh, dyl = _lru_out_bwd(hd0, hd1, p_yl, 0, dy_lru, nct, tm, tcw)
    lam0, da0 = _lru_scan_bwd(a2, hd0, dh, 0, nct, tm, tcw)
    lam1, da1 = _lru_scan_bwd(a2, hd1, dh, 1, nct, tm, tcw)
    dxc, dwbd, acc_lru = _lru_gates_bwd(xc, wbd, bias4, lru_lam_f, (lam0, lam1), (da0, da1), tmg, bw)
    dxl = _dwconv(dxc, 0, W, lru_conv_f, None, tuple(-o for o in conv_offs), nct, tm, tcw, "lru_conv_dx", BF16)
    acc_lc = _dwconv_wgrad(dxc, p_xl, 0, W, conv_offs, nct, tm, tcw, "lru_conv_dw")

    do, dz, acc_on = _dn_out_bwd(o2, p_z, 0, dn_onorm, dy_dn, H, hd, nct, tm)
    dqkvn0, dqkvn1, dgb0, dgb1 = _delta_bwd(qkvn, gb3, (s_in0, s_in1), (tm_d0, tm_d1), do, H, hd, ncc)
    dcv = _dn_act_bwd(cv, (dqkvn0, dqkvn1), H, hd, tm)
    dqkv = _dwconv(dcv, 0, 3 * HD, dn_conv_f, None, tuple(-o for o in conv_offs), nct, tm, tcc, "dn_conv_dx", BF16)
    acc_dc = _dwconv_wgrad(dcv, p_qkv, 0, 3 * HD, conv_offs, nct, tm, tcc, "dn_conv_dw")
    dp_ab, acc_ab = _ab_act_bwd(p_ab, gb, (dgb0.reshape(R, LANES), dgb1.reshape(R, LANES)), alog_v, dtb_v, nd, tm)

    groups = [(dqkv.reshape(R, 3 * HD), w_qkv, "qkv"), (dz.reshape(R, HD), w_z, "z"), (dp_ab, w_ab, "ab"),
              (dxl.reshape(R, W), w_xl, "xl"), (dyl.reshape(R, W), w_yl, "yl"), (dmg.reshape(R, 2 * D), w_mg, "mg")]
    du1 = _mm_sum([(dg_, wg_) for dg_, wg_, _ in groups], "mm_in_dx")
    gw_groups = [_mm(dg_, u1f, "tn", "mm_in_dw_" + nm, out_dtype=BF16) for dg_, _, nm in groups]
    gw_groups[2] = gw_groups[2][:2 * nd]
    gw_in = jnp.concatenate(gw_groups, axis=0).reshape(N_DEV, csh, D)
    grad_x, acc1x, acc1c = _norm_mod_bwd(h0, du1.reshape(B, T, D), g_pre_mix, mods, 1, nct, tm, dh1, "norm_mod1_bwd")

    g_lru_conv = jnp.sum(acc_lc[:, :4], 0)
    g_dn_conv = jnp.sum(acc_dc[:, :4], 0)
    g_ffn_dw = jnp.sum(acc_ffn[:, :9], 0).reshape(3, 3, Cf)
    sm_names = ["dn_conv", "lru_conv", "lru_b_rg", "lru_b_ig", "lru_lambda", "ffn_dw"]
    sm_parts = [_col_shards(g_dn_conv, 1), _col_shards(g_lru_conv, 1),
                _col_shards(jnp.stack([acc_lru[0], acc_lru[2]]), 1), _col_shards(jnp.stack([acc_lru[1], acc_lru[3]]), 1),
                _col_shards(acc_lru[4:6], 1), _col_shards(g_ffn_dw, 2)]
    late8 = [gw_in, gw_bdn.reshape(N_DEV, HD // N_DEV, D), gw_blru.reshape(N_DEV, W // N_DEV, D),
             gw_out.reshape(N_DEV, D // N_DEV, D), _pack(sm_parts, lead=(N_DEV,))]
    q_late = _split_start(pair_sums(late8, "mix"), "quad", "rs_quad_mix_start")
    ffn_x, ffn_land = _split_wait(q_early, "quad", q_late[4], "rs_quad_ffn_wait")
    results = {}
    for n, x4, land in zip(["w_up", "w_down"], ffn_x, ffn_land):
        results[n] = list(_adamw(params[n], mom1[n], mom2[n], own_slab(land, x4), "adamw_" + n))

    zeros_d = jnp.zeros((B, D), F32)
    dm_rows = jnp.stack([acc1x[:, 0], acc1x[:, 1], acc_r1[:, 0], acc2x[:, 0], acc2x[:, 1], acc_f[:, 0]], axis=1)
    dm_ctx = jnp.stack([jnp.sum(acc1c[:, 0], 0), jnp.sum(acc1c[:, 1], 0)] + [zeros_d[0]] * 4, axis=0)[None]
    dm_pad = jnp.zeros((SUBLANES, 6 * D), F32).at[:B + 1].set(jnp.concatenate([dm_rows, dm_ctx], 0).reshape(B + 1, 6 * D))
    dm_g = _all_gather([dm_pad], "ag_dmods")[0]
    g_mine = lax.dynamic_slice(dm_g, (0, 0, me * sw), (N_DEV, SUBLANES, sw)).reshape(N_DEV * SUBLANES, sw)
    gw_ada, dcc = _ada_bwd(ccp, w_ada[0], g_mine, c_ctx.reshape(1, D), B)

    per = bw // bdim

    def diag_blocks(dwf):
        g6 = dwf.reshape(W // bw, per, bdim, per, bdim)
        return jnp.einsum("tpiqj,pq->tpij", g6, jnp.eye(per, dtype=F32)).reshape(nblk, bdim, bdim)

    g_rep = dict(
        c_ctx=dcc[0],
        g_pre_mix=jnp.sum(acc1x[:, 2] + acc1c[:, 2], 0)[None], g_post_mix=jnp.sum(acc_r1[:, 1], 0)[None],
        g_pre_ffn=jnp.sum(acc2x[:, 2], 0)[None], g_post_ffn=jnp.sum(acc_f[:, 1], 0)[None],
        b_merge=jnp.sum(acc_mg[:, 0], 0)[None],
        dn_a_log=acc_ab[0, :nd].reshape(1, 2, H), dn_dt_bias=acc_ab[1, :nd].reshape(1, 2, H),
        dn_onorm=jnp.sum(acc_on[:, 0], 0)[None],
        lru_conv_b=jnp.sum(acc_lc[:, 4], 0)[None],
        lru_w_rg=jnp.stack([diag_blocks(dwbd[0]), diag_blocks(dwbd[2])])[None],
        lru_w_ig=jnp.stack([diag_blocks(dwbd[1]), diag_blocks(dwbd[3])])[None],
        ffn_dw_b=jnp.sum(acc_ffn[:, 9], 0)[None])
    mat_names = ["lru_w_rg", "lru_w_ig"]
    vec_names = [n for n in g_rep if n not in mat_names]
    vec_parts, mat_parts = _all_gather([_pack([g_rep[n] for n in vec_names]),
                                        _pack([g_rep[n] for n in mat_names]).astype(BF16)], "ag_rep_grads")
    for grp, parts, nm in ((vec_names, vec_parts, "adamw_rep_vec"), (mat_names, mat_parts, "adamw_rep_mat")):
        out4 = _adamw(_pack([params[n] for n in grp]), _pack([mom1[n] for n in grp]),
                      _pack([mom2[n] for n in grp]), parts, nm)
        for k, buf in enumerate(out4):
            for n, arr in zip(grp, _unpack(buf, [params[n].shape for n in grp])):
                results.setdefault(n, [None] * 4)[k] = arr

    ba_out = _adamw(_pack([b_ada]), _pack([m_b_ada]), _pack([v_b_ada]),
                    _pack([dm_g.reshape(N_DEV * SUBLANES, 6 * D)], lead=(N_DEV * SUBLANES,)), "adamw_b_ada")
    results["b_ada"] = [_unpack(buf, [b_ada.shape])[0] for buf in ba_out]

    wa_out = _adamw(w_ada, m_w_ada, v_w_ada, gw_ada[None], "adamw_w_ada")
    results["w_ada"] = list(wa_out)

    mix_x, mix_land = _split_wait(q_late, "quad", wa_out[0], "rs_quad_mix_wait")
    recv4 = [own_slab(land, x4) for land, x4 in zip(mix_land, mix_x)]
    for n, g4 in zip(["w_in", "w_branch_dn", "w_branch_lru", "w_out"], recv4[:-1]):
        if n == "w_in":
            g4 = jnp.swapaxes(g4, 1, 2)
        results[n] = list(_adamw(params[n], mom1[n], mom2[n], g4, "adamw_" + n))
    sm_out = _adamw(_pack([params[n] for n in sm_names]), _pack([mom1[n] for n in sm_names]),
                    _pack([mom2[n] for n in sm_names]), recv4[-1], "adamw_small")
    for k, buf in enumerate(sm_out):
        for n, arr in zip(sm_names, _unpack(buf, [params[n].shape for n in sm_names])):
            results.setdefault(n, [None] * 4)[k] = arr

    outs = [loss, grad_x]
    for k in range(4):
        outs += [results[n][k] for n in names]
    return tuple(outs)
```

```python
import functools
import math

import jax
import jax.numpy as jnp
from jax import lax
from jax.experimental import pallas as pl
from jax.experimental.pallas import tpu as pltpu

F32 = jnp.float32
BF16 = jnp.bfloat16

EPS = 1e-6
GRID_W = 64
GRID_SHIFT = 6
CHUNK = 64
LRU_C = 8.0
N_DEV = 8
ADAM_LR = 0.001
ADAM_B1 = 0.9
ADAM_B2 = 0.999
ADAM_EPS = 1e-08
ADAM_WD = 0.01
ADAM_STEP = 10

LANES = 128
SUBLANES = 8
VMEM_LIMIT = 48 * 1024 * 1024
MESH_AXES = ("x", "y", "c")
GELU_C = math.sqrt(2.0 / math.pi)


def _cparams(sem=None, vmem=None):
    kw = {}
    if sem is not None:
        kw["dimension_semantics"] = sem
    if vmem is not None:
        kw["vmem_limit_bytes"] = vmem
    return pltpu.CompilerParams(**kw)


def _tile(n, pref):
    if n <= pref:
        return n
    t = (pref // LANES) * LANES
    while n % t:
        t -= LANES
    return t


def _sigmoid(x):
    return 1.0 / (1.0 + jnp.exp(-x))


def _silu(x):
    return x * _sigmoid(x)


def _dsilu(x):
    s = _sigmoid(x)
    return s * (1.0 + x * (1.0 - s))


def _softplus(x):
    return jnp.maximum(x, 0.0) + jnp.log(1.0 + jnp.exp(-jnp.abs(x)))


def _gelu(x):
    return 0.5 * x * (1.0 + jnp.tanh(GELU_C * (x + 0.044715 * x * x * x)))


def _dgelu(x):
    t = jnp.tanh(GELU_C * (x + 0.044715 * x * x * x))
    return 0.5 * (1.0 + t) + 0.5 * x * (1.0 - t * t) * GELU_C * (1.0 + 3.0 * 0.044715 * x * x)


def _dot(a, b, dims="nn"):
    dn = {"nn": (((1,), (0,)), ((), ())),
          "nt": (((1,), (1,)), ((), ())),
          "tn": (((0,), (0,)), ((), ()))}[dims]
    return lax.dot_general(a.astype(BF16), b.astype(BF16), dn, preferred_element_type=F32)


def _split2(x):
    hi = x.astype(BF16)
    lo = (x - hi.astype(F32)).astype(BF16)
    return hi, lo


def _split3(x):
    h1 = x.astype(BF16)
    r1 = x - h1.astype(F32)
    h2 = r1.astype(BF16)
    h3 = (r1 - h2.astype(F32)).astype(BF16)
    return h1, h2, h3


def _dot_hi(a, b):
    ah, al = _split2(a)
    bh, bl = _split2(b)
    return _dot(ah, bh) + (_dot(ah, bl) + _dot(al, bh))


def _dot_mask(m, x, dims="nn"):
    h1, h2, h3 = _split3(x)
    if dims == "xtn":
        return _dot(h1, m, "tn") + (_dot(h2, m, "tn") + _dot(h3, m, "tn"))
    return _dot(m, h1, dims) + (_dot(m, h2, dims) + _dot(m, h3, dims))


def _my_index():
    return 4 * lax.axis_index("x") + 2 * lax.axis_index("y") + lax.axis_index("c")


def _hbm_call(body, xs, out_shapes, n_sems, name):
    any_spec = pl.BlockSpec(memory_space=pl.ANY)
    return pl.pallas_call(
        body, name=name, out_shape=tuple(out_shapes),
        in_specs=[any_spec] * len(xs), out_specs=tuple([any_spec] * len(out_shapes)),
        scratch_shapes=[pltpu.SemaphoreType.DMA((n_sems,)), pltpu.SemaphoreType.DMA((n_sems,)),
                        pltpu.SemaphoreType.DMA((len(xs),))],
    )(*xs)


def _all_gather(xs, name):
    n = len(xs)

    def body(*refs):
        x_refs, out_refs = refs[:n], refs[n:2 * n]
        send_sems, recv_sems, local_sems = refs[2 * n:]
        x_, y_, c_ = lax.axis_index("x"), lax.axis_index("y"), lax.axis_index("c")
        me, sibling = (x_, y_, c_), (x_, y_, 1 - c_)
        chips = [(1 - x_, y_), (x_, 1 - y_), (1 - x_, 1 - y_)]

        def slab(a, px, py, pc):
            return out_refs[a].at[4 * px + 2 * py + pc]

        def copy(a, k, block, to, src=None):
            return pltpu.make_async_remote_copy(
                src_ref=slab(a, *block) if src is None else src, dst_ref=slab(a, *block),
                send_sem=send_sems.at[7 * a + k], recv_sem=recv_sems.at[7 * a + k],
                device_id=to, device_id_type=pl.DeviceIdType.MESH)

        mine = [pltpu.make_async_copy(x_refs[a], slab(a, *me), local_sems.at[a]) for a in range(n)]
        for cp in mine:
            cp.start()
        sent = []
        for j, chip in enumerate(chips):
            sent += [copy(a, 1 + j, me, (*chip, c_), src=x_refs[a]) for a in range(n)]
        sent += [copy(a, 0, me, sibling, src=x_refs[a]) for a in range(n)]
        for cp in sent:
            cp.start()
        for j, chip in enumerate(chips):
            for a in range(n):
                copy(a, 1 + j, (*chip, c_), me).wait_recv()
                fwd = copy(a, 4 + j, (*chip, c_), sibling)
                fwd.start()
                sent.append(fwd)
        for a in range(n):
            copy(a, 0, sibling, me).wait_recv()
        for j, chip in enumerate(chips):
            for a in range(n):
                copy(a, 4 + j, (*chip, 1 - c_), me).wait_recv()
        for cp in sent:
            cp.wait_send()
        for cp in mine:
            cp.wait()

    outs = [jax.ShapeDtypeStruct((N_DEV,) + x.shape, x.dtype) for x in xs]
    return _hbm_call(body, xs, outs, 7 * n, name)


def _pair_exchange(xs, name):
    n = len(xs)

    def body(*refs):
        x_refs, out_refs = refs[:n], refs[n:2 * n]
        send_sems, recv_sems, _ = refs[2 * n:]
        x_, y_, c_ = lax.axis_index("x"), lax.axis_index("y"), lax.axis_index("c")
        copies = []
        for a in range(n):
            for j in range(4):
                copies.append(pltpu.make_async_remote_copy(
                    src_ref=x_refs[a].at[2 * j + (1 - c_)], dst_ref=out_refs[a].at[j],
                    send_sem=send_sems.at[4 * a + j], recv_sem=recv_sems.at[4 * a + j],
                    device_id=(x_, y_, 1 - c_), device_id_type=pl.DeviceIdType.MESH))
        for cp in copies:
            cp.start()
        for cp in copies:
            cp.wait()

    outs = [jax.ShapeDtypeStruct((4,) + x.shape[1:], x.dtype) for x in xs]
    return _hbm_call(body, xs, outs, 4 * n, name)


def _split_views(kind, x_ref, land_ref, k):
    x_, y_, c_ = lax.axis_index("x"), lax.axis_index("y"), lax.axis_index("c")
    if kind == "gather":
        px = 1 - x_ if (k >> 2) & 1 else x_
        py = 1 - y_ if (k >> 1) & 1 else y_
        pc = 1 - c_ if k & 1 else c_
        return x_ref, land_ref.at[4 * x_ + 2 * y_ + c_], land_ref.at[4 * px + 2 * py + pc], (px, py, pc)
    px = 1 - x_ if (k >> 1) & 1 else x_
    py = 1 - y_ if k & 1 else y_
    return x_ref.at[2 * px + py], land_ref.at[2 * x_ + y_], land_ref.at[2 * px + py], (px, py, c_)


def _split_start(xs, kind, name):
    n = len(xs)
    npeer = 7 if kind == "gather" else 3
    lands = [lax.empty(((N_DEV,) + x.shape) if kind == "gather" else x.shape, x.dtype) for x in xs]

    def body(*refs):
        x_refs, land_refs = refs[:n], refs[n:2 * n]
        send_sems, recv_sems, token = refs[2 * n], refs[2 * n + 1], refs[4 * n + 2]
        for k in range(1, npeer + 1):
            for a in range(n):
                src, dst, _, peer = _split_views(kind, x_refs[a], land_refs[a], k)
                pltpu.make_async_remote_copy(
                    src_ref=src, dst_ref=dst, send_sem=send_sems.at[npeer * a + k - 1],
                    recv_sem=recv_sems.at[npeer * a + k - 1],
                    device_id=peer, device_id_type=pl.DeviceIdType.MESH).start()
        token[...] = jnp.zeros_like(token)

    hbm = pl.BlockSpec(memory_space=pltpu.HBM)
    sem = pl.BlockSpec(memory_space=pltpu.SEMAPHORE)
    sems = pltpu.SemaphoreType.DMA((npeer * n,))
    out = pl.pallas_call(
        body, name=name,
        out_shape=(sems, sems, *[pltpu.HBM(a.shape, a.dtype) for a in list(xs) + lands],
                   jax.ShapeDtypeStruct((SUBLANES, LANES), F32)),
        in_specs=[hbm] * (2 * n),
        out_specs=(sem, sem, *[hbm] * (2 * n), pl.BlockSpec(memory_space=pltpu.VMEM)),
        input_output_aliases={i: 2 + i for i in range(2 * n)},
        compiler_params=pltpu.CompilerParams(has_side_effects=pltpu.SideEffectType.DATAFLOW_SIDE_EFFECTING),
    )(*[pltpu.with_memory_space_constraint(a, pltpu.HBM) for a in list(xs) + lands])
    return out[0], out[1], list(out[2:2 + n]), list(out[2 + n:2 + 2 * n]), out[-1]


def _split_wait(started, kind, after, name):
    send_sems_, recv_sems_, x_thru, land_thru, _ = started
    n = len(x_thru)
    npeer = 7 if kind == "gather" else 3

    def body(*refs):
        x_refs, land_refs = refs[:n], refs[n:2 * n]
        send_sems, recv_sems = refs[2 * n], refs[2 * n + 1]
        for k in range(1, npeer + 1):
            for a in range(n):
                src, _, landed, peer = _split_views(kind, x_refs[a], land_refs[a], k)
                cp = pltpu.make_async_remote_copy(
                    src_ref=src, dst_ref=landed, send_sem=send_sems.at[npeer * a + k - 1],
                    recv_sem=recv_sems.at[npeer * a + k - 1],
                    device_id=peer, device_id_type=pl.DeviceIdType.MESH)
                cp.wait_send()
                cp.wait_recv()

    hbm = pl.BlockSpec(memory_space=pltpu.HBM)
    sem = pl.BlockSpec(memory_space=pltpu.SEMAPHORE)
    out = pl.pallas_call(
        body, name=name,
        out_shape=tuple(pltpu.HBM(a.shape, a.dtype) for a in x_thru + land_thru),
        in_specs=[hbm] * (2 * n) + [sem, sem, pl.BlockSpec(memory_space=pl.ANY)],
        out_specs=tuple([hbm] * (2 * n)),
        input_output_aliases={i: i for i in range(2 * n)},
        compiler_params=pltpu.CompilerParams(has_side_effects=pltpu.SideEffectType.DATAFLOW_SIDE_EFFECTING),
    )(*x_thru, *land_thru, send_sems_, recv_sems_, after)
    return list(out[:n]), list(out[n:])


def _pair_sum(x8, r4, name):
    _, r, c = x8.shape
    tr = _row_tile(r, c * 12)

    def body(core_ref, x_ref, r_ref, o_ref):
        o_ref[...] = (x_ref[...].astype(F32) + r_ref[...].astype(F32)).astype(o_ref.dtype)

    core = lax.axis_index("c").astype(jnp.int32).reshape(1)
    return pl.pallas_call(
        body, name=name,
        grid_spec=pltpu.PrefetchScalarGridSpec(
            num_scalar_prefetch=1, grid=(4, r // tr),
            in_specs=[pl.BlockSpec((None, tr, c), lambda j, i, core_ref: (2 * j + core_ref[0], i, 0)),
                      pl.BlockSpec((None, tr, c), lambda j, i, core_ref: (j, i, 0))],
            out_specs=pl.BlockSpec((None, tr, c), lambda j, i, core_ref: (j, i, 0))),
        out_shape=jax.ShapeDtypeStruct((4, r, c), x8.dtype),
        compiler_params=_cparams(("parallel", "parallel"), VMEM_LIMIT))(core, x8, r4)


def _row_tile(r, bytes_per_row, budget=4 * 1024 * 1024):
    best = None
    for t in range(16, r + 1, 16):
        if r % t == 0 and t * bytes_per_row <= budget:
            best = t
    return best if best is not None else r


def _mm(a, b, dims, name, out_dtype=F32, add=None, out_shards=1, tm=1024, tn=1024, tk=1024):
    S = b.shape[0] if b.ndim == 3 else 1
    bshape = (b.shape[1], S * b.shape[2]) if b.ndim == 3 else b.shape
    if dims == "nn":
        (M, K), (K2, N) = a.shape, bshape
    elif dims == "nt":
        (M, K), (N, K2) = a.shape, bshape
    else:
        (K, M), (K2, N) = a.shape, bshape
    assert K == K2, (a.shape, b.shape, dims)
    cs = bshape[1] // S
    tm, tk = _tile(M, tm), _tile(K, min(tk, cs) if (S > 1 and dims == "nt") else tk)
    tn = _tile(N, min(tn, cs) if (S > 1 and dims != "nt") else min(tn, N // out_shards))
    assert cs % (tk if dims == "nt" else tn) == 0 and (N // out_shards) % tn == 0
    nk = K // tk

    def body(*refs):
        if add is None:
            a_ref, b_ref, o_ref, acc = refs
        else:
            a_ref, b_ref, c_ref, o_ref, acc = refs
        k = pl.program_id(2)

        @pl.when(k == 0)
        def _():
            acc[...] = jnp.zeros_like(acc)

        acc[...] += _dot(a_ref[...], b_ref[...], dims)

        @pl.when(k == nk - 1)
        def _():
            r = acc[...]
            if add is not None:
                r = r + c_ref[...]
            o_ref[...] = r.astype(out_dtype)

    a_spec = (pl.BlockSpec((tk, tm), lambda i, j, k: (k, i)) if dims == "tn"
              else pl.BlockSpec((tm, tk), lambda i, j, k: (i, k)))
    if S == 1:
        b_spec = (pl.BlockSpec((tn, tk), lambda i, j, k: (j, k)) if dims == "nt"
                  else pl.BlockSpec((tk, tn), lambda i, j, k: (k, j)))
    elif dims == "nt":
        per = cs // tk
        b_spec = pl.BlockSpec((None, tn, tk), lambda i, j, k: (k // per, j, k % per))
    else:
        per = cs // tn
        b_spec = pl.BlockSpec((None, tk, tn), lambda i, j, k: (j // per, k, j % per))
    in_specs = [a_spec, b_spec]
    args = [a, b]
    if add is not None:
        in_specs.append(pl.BlockSpec((tm, tn), lambda i, j, k: (i, j)))
        args.append(add)
    if out_shards == 1:
        out_spec, out_shape = pl.BlockSpec((tm, tn), lambda i, j, k: (i, j)), (M, N)
    else:
        oper = N // out_shards // tn
        out_spec = pl.BlockSpec((None, tm, tn), lambda i, j, k: (j // oper, i, j % oper))
        out_shape = (out_shards, M, N // out_shards)
    return pl.pallas_call(
        body, name=name, grid=(M // tm, N // tn, nk),
        in_specs=in_specs, out_specs=out_spec,
        out_shape=jax.ShapeDtypeStruct(out_shape, out_dtype),
        scratch_shapes=[pltpu.VMEM((tm, tn), F32)],
        compiler_params=_cparams(("parallel", "parallel", "arbitrary"), VMEM_LIMIT),
    )(*args)


VMEM_LIMIT_SUM = 56 * 1024 * 1024


def _mm_sum(pairs, name, out_dtype=F32, tm=768, tn=1024, tk=1024):
    M, N = pairs[0][0].shape[0], pairs[0][1].shape[1]
    tm, tn = _tile(M, tm), _tile(N, tn)
    tks = [_tile(a.shape[1], tk) for a, _ in pairs]
    nks = [a.shape[1] // t for (a, _), t in zip(pairs, tks)]
    starts = [sum(nks[:g]) for g in range(len(pairs))]
    nk = sum(nks)
    G = len(pairs)

    def body(*refs):
        o_ref, acc = refs[2 * G], refs[2 * G + 1]
        k = pl.program_id(2)

        @pl.when(k == 0)
        def _():
            acc[...] = jnp.zeros_like(acc)

        for g in range(G):
            @pl.when(jnp.logical_and(k >= starts[g], k < starts[g] + nks[g]))
            def _(g=g):
                acc[...] += _dot(refs[2 * g][...], refs[2 * g + 1][...])

        @pl.when(k == nk - 1)
        def _():
            o_ref[...] = acc[...].astype(out_dtype)

    in_specs, args = [], []
    for g, (a, b) in enumerate(pairs):
        kk = lambda k, g=g: jnp.clip(k - starts[g], 0, nks[g] - 1)
        in_specs += [pl.BlockSpec((tm, tks[g]), lambda i, j, k, kk=kk: (i, kk(k))),
                     pl.BlockSpec((tks[g], tn), lambda i, j, k, kk=kk: (kk(k), j))]
        args += [a, b]
    return pl.pallas_call(
        body, name=name, grid=(M // tm, N // tn, nk),
        in_specs=in_specs, out_specs=pl.BlockSpec((tm, tn), lambda i, j, k: (i, j)),
        out_shape=jax.ShapeDtypeStruct((M, N), out_dtype),
        scratch_shapes=[pltpu.VMEM((tm, tn), F32)],
        compiler_params=_cparams(("parallel", "parallel", "arbitrary"), VMEM_LIMIT_SUM),
    )(*args)


def _ada_fwd(ccp, w, b):
    def body(c_ref, w_ref, b_ref, o_ref):
        o_ref[...] = _dot(_silu(c_ref[...]), w_ref[...]) + b_ref[...]

    return pl.pallas_call(
        body, name="ada_fwd", out_shape=jax.ShapeDtypeStruct((ccp.shape[0], w.shape[1]), F32),
        compiler_params=_cparams(None, VMEM_LIMIT))(ccp, w, b)


def _ada_bwd(ccp, w, g, cctx, n_loc):
    def body(c_ref, w_ref, g_ref, cc_ref, gw_ref, dc_ref):
        gw_ref[...] = _dot(_silu(c_ref[...]), g_ref[...], "tn")
        dcc = _dot(g_ref[...], w_ref[...], "nt")
        row = lax.broadcasted_iota(jnp.int32, dcc.shape, 0)
        part = jnp.sum(jnp.where(row % SUBLANES == n_loc, dcc, 0.0), axis=0, keepdims=True)
        dc_ref[...] = jnp.broadcast_to(part * _dsilu(cc_ref[...]), dc_ref.shape)

    D = ccp.shape[1]
    return pl.pallas_call(
        body, name="ada_bwd",
        out_shape=(jax.ShapeDtypeStruct(w.shape, F32), jax.ShapeDtypeStruct((SUBLANES, D), F32)),
        compiler_params=_cparams(None, VMEM_LIMIT))(ccp, w, g, cctx)


def _norm_mod_fwd(h, gain, mods, i_shift, i_scale, nct, tm, name):
    B, T, D = h.shape

    def body(h_ref, g_ref, m_ref, u_ref):
        x = h_ref[0]
        r = lax.rsqrt(jnp.mean(x * x, axis=-1, keepdims=True) + EPS)
        n = x * r * g_ref[...]
        u_ref[0] = (n * (1.0 + m_ref[0, i_scale:i_scale + 1, :]) + m_ref[0, i_shift:i_shift + 1, :]).astype(BF16)

    return pl.pallas_call(
        body, name=name, grid=(B, T // tm),
        in_specs=[pl.BlockSpec((1, tm, D), lambda b, t: (b, t, 0)),
                  pl.BlockSpec((1, D), lambda b, t: (0, 0)),
                  pl.BlockSpec((1, SUBLANES, D), lambda b, t: (jnp.where(t < nct, B, b), 0, 0))],
        out_specs=pl.BlockSpec((1, tm, D), lambda b, t: (b, t, 0)),
        out_shape=jax.ShapeDtypeStruct((B, T, D), BF16),
        compiler_params=_cparams(("parallel", "parallel"), VMEM_LIMIT),
    )(h, gain, mods)


def _norm_mod_bwd(h, du, gain, mods, i_scale, nct, tm, res, name):
    B, T, D = h.shape
    nt = T // tm

    def body(h_ref, du_ref, g_ref, m_ref, res_ref, dx_ref, ax_ref, ac_ref):
        t = pl.program_id(1)
        x = h_ref[0]
        r = lax.rsqrt(jnp.mean(x * x, axis=-1, keepdims=True) + EPS)
        nh = x * r
        g = g_ref[...]
        du_ = du_ref[0]
        dn = du_ * (1.0 + m_ref[0, i_scale:i_scale + 1, :])
        dnh = dn * g
        dx = r * (dnh - nh * jnp.mean(dnh * nh, axis=-1, keepdims=True))
        sums = (jnp.sum(du_, axis=0, keepdims=True), jnp.sum(du_ * nh * g, axis=0, keepdims=True),
                jnp.sum(dn * nh, axis=0, keepdims=True))

        @pl.when(t == 0)
        def _():
            ax_ref[...] = jnp.zeros_like(ax_ref)
            ac_ref[...] = jnp.zeros_like(ac_ref)

        def accumulate(ref):
            for i, s in enumerate(sums):
                ref[0, i:i + 1, :] += s

        @pl.when(t < nct)
        def _():
            accumulate(ac_ref)

        @pl.when(t >= nct)
        def _():
            accumulate(ax_ref)
            dx_ref[0] = dx + res_ref[0]

    lat = lambda b, t: (b, jnp.maximum(t - nct, 0), 0)
    acc = pl.BlockSpec((1, SUBLANES, D), lambda b, t: (b, 0, 0))
    return pl.pallas_call(
        body, name=name, grid=(B, nt),
        in_specs=[pl.BlockSpec((1, tm, D), lambda b, t: (b, t, 0)),
                  pl.BlockSpec((1, tm, D), lambda b, t: (b, t, 0)),
                  pl.BlockSpec((1, D), lambda b, t: (0, 0)),
                  pl.BlockSpec((1, SUBLANES, D), lambda b, t: (jnp.where(t < nct, B, b), 0, 0)),
                  pl.BlockSpec((1, tm, D), lat)],
        out_specs=(pl.BlockSpec((1, tm, D), lat), acc, acc),
        out_shape=(jax.ShapeDtypeStruct(res.shape, F32),
                   jax.ShapeDtypeStruct((B, SUBLANES, D), F32), jax.ShapeDtypeStruct((B, SUBLANES, D), F32)),
        compiler_params=_cparams(("parallel", "arbitrary"), VMEM_LIMIT),
    )(h, du, gain, mods, res)


def _resid_norm_fwd(x, y, gain, mods, i_gate, tm, name):
    B, N, D = x.shape

    def body(x_ref, y_ref, g_ref, m_ref, o_ref):
        y_ = y_ref[0]
        r = lax.rsqrt(jnp.mean(y_ * y_, axis=-1, keepdims=True) + EPS)
        o_ref[0] = x_ref[0] + y_ * r * g_ref[...] * m_ref[0, i_gate:i_gate + 1, :]

    row = pl.BlockSpec((1, tm, D), lambda b, t: (b, t, 0))
    return pl.pallas_call(
        body, name=name, grid=(B, N // tm),
        in_specs=[row, row, pl.BlockSpec((1, D), lambda b, t: (0, 0)),
                  pl.BlockSpec((1, SUBLANES, D), lambda b, t: (b, 0, 0))],
        out_specs=row, out_shape=jax.ShapeDtypeStruct((B, N, D), F32),
        compiler_params=_cparams(("parallel", "parallel"), VMEM_LIMIT),
    )(x, y, gain, mods)


def _resid_norm_bwd_math(y_, dh, g, gate):
    r = lax.rsqrt(jnp.mean(y_ * y_, axis=-1, keepdims=True) + EPS)
    nh = y_ * r
    dgate = jnp.sum(dh * nh * g, axis=0, keepdims=True)
    dn = dh * gate
    dgain = jnp.sum(dn * nh, axis=0, keepdims=True)
    dnh = dn * g
    dy = r * (dnh - nh * jnp.mean(dnh * nh, axis=-1, keepdims=True))
    return dy, dgate, dgain


def _resid_norm_bwd(y, dh, gain, mods, i_gate, tm, name):
    B, N, D = y.shape

    def body(y_ref, dh_ref, g_ref, m_ref, dy_ref, acc_ref):
        t = pl.program_id(1)
        dy, dgate, dgain = _resid_norm_bwd_math(y_ref[0], dh_ref[0], g_ref[...], m_ref[0, i_gate:i_gate + 1, :])
        dy_ref[0] = dy.astype(BF16)

        @pl.when(t == 0)
        def _():
            acc_ref[...] = jnp.zeros_like(acc_ref)

        acc_ref[0, 0:1, :] += dgate
        acc_ref[0, 1:2, :] += dgain

    row = pl.BlockSpec((1, tm, D), lambda b, t: (b, t, 0))
    return pl.pallas_call(
        body, name=name, grid=(B, N // tm),
        in_specs=[row, row, pl.BlockSpec((1, D), lambda b, t: (0, 0)),
                  pl.BlockSpec((1, SUBLANES, D), lambda b, t: (b, 0, 0))],
        out_specs=(row, pl.BlockSpec((1, SUBLANES, D), lambda b, t: (b, 0, 0))),
        out_shape=(jax.ShapeDtypeStruct((B, N, D), BF16), jax.ShapeDtypeStruct((B, SUBLANES, D), F32)),
        compiler_params=_cparams(("parallel", "arbitrary"), VMEM_LIMIT),
    )(y, dh, gain, mods)


def _final_fwd_bwd(h1, d, gain, mods, i_gate, tgt, tm):
    B, N, D = h1.shape

    def body(h_ref, d_ref, g_ref, m_ref, t_ref, dd_ref, dh_ref, acc_ref):
        t = pl.program_id(1)
        d_ = d_ref[0]
        g = g_ref[...]
        gate = m_ref[0, i_gate:i_gate + 1, :]
        r = lax.rsqrt(jnp.mean(d_ * d_, axis=-1, keepdims=True) + EPS)
        e = h_ref[0] + d_ * r * g * gate - t_ref[0]
        dh = e * (1.0 / D)
        dh_ref[0] = dh
        dy, dgate, dgain = _resid_norm_bwd_math(d_, dh, g, gate)
        dd_ref[0] = dy.astype(BF16)

        @pl.when(t == 0)
        def _():
            acc_ref[...] = jnp.zeros_like(acc_ref)

        acc_ref[0, 0:1, :] += dgate
        acc_ref[0, 1:2, :] += dgain
        acc_ref[0, 2:3, :] += jnp.sum(e * e, axis=0, keepdims=True)

    row = pl.BlockSpec((1, tm, D), lambda b, t: (b, t, 0))
    return pl.pallas_call(
        body, name="final_fwd_bwd", grid=(B, N // tm),
        in_specs=[row, row, pl.BlockSpec((1, D), lambda b, t: (0, 0)),
                  pl.BlockSpec((1, SUBLANES, D), lambda b, t: (b, 0, 0)), row],
        out_specs=(row, row, pl.BlockSpec((1, SUBLANES, D), lambda b, t: (b, 0, 0))),
        out_shape=(jax.ShapeDtypeStruct((B, N, D), BF16), jax.ShapeDtypeStruct((B, N, D), F32),
                   jax.ShapeDtypeStruct((B, SUBLANES, D), F32)),
        compiler_params=_cparams(("parallel", "arbitrary"), VMEM_LIMIT),
    )(h1, d, gain, mods, tgt)


def _shifted(x, prev, nxt, off, row, tm):
    if off == 0:
        return x
    if off < 0:
        y = pltpu.roll(x, -off, 0)
        for r in range(-off):
            y = jnp.where(row == r, prev[SUBLANES + r + off:SUBLANES + r + off + 1, :], y)
        return y
    y = pltpu.roll(x, tm - off, 0)
    for r in range(off):
        y = jnp.where(row == tm - off + r, nxt[r:r + 1, :], y)
    return y


def _halo_specs(T, tm, tc, cb0, order):
    r8, n8 = tm // SUBLANES, T // SUBLANES
    if order == "btj":
        prev = lambda b, t, j: (b, jnp.maximum(t * r8 - 1, 0), cb0 + j)
        nxt = lambda b, t, j: (b, jnp.minimum((t + 1) * r8, n8 - 1), cb0 + j)
    else:
        prev = lambda b, j, t: (b, jnp.maximum(t * r8 - 1, 0), cb0 + j)
        nxt = lambda b, j, t: (b, jnp.minimum((t + 1) * r8, n8 - 1), cb0 + j)
    return pl.BlockSpec((1, SUBLANES, tc), prev), pl.BlockSpec((1, SUBLANES, tc), nxt)


def _seg_halos(p_ref, n_ref, t, nct, nt):
    seg_start = jnp.logical_or(t == 0, t == nct)
    seg_end = jnp.logical_or(t == nct - 1, t == nt - 1)
    prev = jnp.where(seg_start, 0.0, p_ref[0])
    nxt = jnp.where(seg_end, 0.0, n_ref[0])
    return prev, nxt


def _dwconv(x, col0, C, w, bias, offs, nct, tm, tc, name, out_dtype=F32, qkv_heads=None):
    B, T, _ = x.shape
    nt = T // tm
    cb0 = col0 // tc
    if qkv_heads is not None:
        assert tc == qkv_heads[0] * qkv_heads[1] and C == 3 * tc

    def body(*refs):
        refs = list(refs)
        a_ref = refs.pop() if qkv_heads is not None else None
        if bias is None:
            x_ref, p_ref, n_ref, w_ref, o_ref = refs
        else:
            x_ref, p_ref, n_ref, w_ref, b_ref, o_ref = refs
        t = pl.program_id(1)
        prev, nxt = _seg_halos(p_ref, n_ref, t, nct, nt)
        x_ = x_ref[0]
        row = lax.broadcasted_iota(jnp.int32, x_.shape, 0)
        acc = None
        for j, off in enumerate(offs):
            term = _shifted(x_, prev, nxt, off, row, tm) * w_ref[j:j + 1, :]
            acc = term if acc is None else acc + term
        if bias is not None:
            acc = acc + b_ref[...]
        o_ref[0] = acc.astype(out_dtype)
        if qkv_heads is not None:
            H, hd = qkv_heads
            part = pl.program_id(2)
            for h in range(H):
                sl = slice(h * hd, (h + 1) * hd)
                s = _silu(acc[:, sl])
                rs = lax.rsqrt(jnp.sum(s * s, axis=-1, keepdims=True) + EPS)
                scale = jnp.where(part == 0, rs * (float(hd) ** -0.5), jnp.where(part == 1, rs, 1.0))
                a_ref[0, :, sl] = s * scale

    pspec, nspec = _halo_specs(T, tm, tc, cb0, "btj")
    in_specs = [pl.BlockSpec((1, tm, tc), lambda b, t, j: (b, t, cb0 + j)), pspec, nspec,
                pl.BlockSpec((w.shape[0], tc), lambda b, t, j: (0, j))]
    args = [x, x, x, w]
    if bias is not None:
        in_specs.append(pl.BlockSpec((1, tc), lambda b, t, j: (0, j)))
        args.append(bias)
    tile = pl.BlockSpec((1, tm, tc), lambda b, t, j: (b, t, j))
    out_specs, out_shape = tile, jax.ShapeDtypeStruct((B, T, C), out_dtype)
    if qkv_heads is not None:
        out_specs, out_shape = (tile, tile), (out_shape, jax.ShapeDtypeStruct((B, T, C), F32))
    return pl.pallas_call(
        body, name=name, grid=(B, nt, C // tc),
        in_specs=in_specs, out_specs=out_specs, out_shape=out_shape,
        compiler_params=_cparams(("parallel", "parallel", "parallel"), VMEM_LIMIT),
    )(*args)


def _dwconv_wgrad(dy, x, col0, C, offs, nct, tm, tc, name):
    B, T, _ = x.shape
    nt = T // tm
    cb0 = col0 // tc
    K = len(offs)

    def body(dy_ref, x_ref, p_ref, n_ref, acc_ref):
        t = pl.program_id(2)
        prev, nxt = _seg_halos(p_ref, n_ref, t, nct, nt)
        x_ = x_ref[0]
        dy_ = dy_ref[0]
        row = lax.broadcasted_iota(jnp.int32, x_.shape, 0)

        @pl.when(t == 0)
        def _():
            acc_ref[...] = jnp.zeros_like(acc_ref)

        for j, off in enumerate(offs):
            acc_ref[0, j:j + 1, :] += jnp.sum(dy_ * _shifted(x_, prev, nxt, off, row, tm), axis=0, keepdims=True)
        acc_ref[0, K:K + 1, :] += jnp.sum(dy_, axis=0, keepdims=True)

    pspec, nspec = _halo_specs(T, tm, tc, cb0, "bjt")
    return pl.pallas_call(
        body, name=name, grid=(B, C // tc, nt),
        in_specs=[pl.BlockSpec((1, tm, tc), lambda b, j, t: (b, t, j)),
                  pl.BlockSpec((1, tm, tc), lambda b, j, t: (b, t, cb0 + j)), pspec, nspec],
        out_specs=pl.BlockSpec((1, SUBLANES, tc), lambda b, j, t: (b, 0, j)),
        out_shape=jax.ShapeDtypeStruct((B, SUBLANES, C), F32),
        compiler_params=_cparams(("parallel", "parallel", "arbitrary"), VMEM_LIMIT),
    )(dy, x, x, x)


def _ab_act(pab, alog, dtb, nd, tm):
    R = pab.shape[0]

    def body(p_ref, al_ref, dt_ref, o_ref):
        p = p_ref[...]
        lane = lax.broadcasted_iota(jnp.int32, p.shape, 1)
        g = -jnp.exp(al_ref[...]) * _softplus(p + dt_ref[...])
        o_ref[...] = jnp.where(lane < nd, g, jnp.where(lane < 2 * nd, _sigmoid(p), 0.0))

    row = pl.BlockSpec((tm, LANES), lambda i: (i, 0))
    vec = pl.BlockSpec((1, LANES), lambda i: (0, 0))
    return pl.pallas_call(
        body, name="ab_act", grid=(R // tm,), in_specs=[row, vec, vec], out_specs=row,
        out_shape=jax.ShapeDtypeStruct((R, LANES), F32),
        compiler_params=_cparams(("parallel",), VMEM_LIMIT))(pab, alog, dtb)


def _ab_act_bwd(pab, gb, dgb, alog, dtb, nd, tm):
    R = pab.shape[0]

    def body(p_ref, gb_ref, d0_ref, d1_ref, al_ref, dt_ref, o_ref, acc_ref):
        i = pl.program_id(0)
        p = p_ref[...]
        gb_ = gb_ref[...]
        d_ = d0_ref[...] + d1_ref[...]
        lane = lax.broadcasted_iota(jnp.int32, p.shape, 1)
        is_g = lane < nd
        is_b = jnp.logical_and(lane >= nd, lane < 2 * nd)
        da = jnp.where(is_g, d_ * (-jnp.exp(al_ref[...])) * _sigmoid(p + dt_ref[...]), 0.0)
        db = jnp.where(is_b, d_ * gb_ * (1.0 - gb_), 0.0)
        o_ref[...] = (da + db).astype(BF16)

        @pl.when(i == 0)
        def _():
            acc_ref[...] = jnp.zeros_like(acc_ref)

        acc_ref[0:1, :] += jnp.sum(jnp.where(is_g, d_ * gb_, 0.0), axis=0, keepdims=True)
        acc_ref[1:2, :] += jnp.sum(da, axis=0, keepdims=True)

    row = pl.BlockSpec((tm, LANES), lambda i: (i, 0))
    vec = pl.BlockSpec((1, LANES), lambda i: (0, 0))
    return pl.pallas_call(
        body, name="ab_act_bwd", grid=(R // tm,),
        in_specs=[row, row, row, row, vec, vec],
        out_specs=(row, pl.BlockSpec((SUBLANES, LANES), lambda i: (0, 0))),
        out_shape=(jax.ShapeDtypeStruct((R, LANES), BF16), jax.ShapeDtypeStruct((SUBLANES, LANES), F32)),
        compiler_params=_cparams(("arbitrary",), VMEM_LIMIT))(pab, gb, dgb[0], dgb[1], alog, dtb)


def _dn_act_bwd(cv, dqkv, H, hd, tm):
    B, T, C3 = cv.shape
    qscale = float(hd) ** -0.5

    def body(c_ref, d0_ref, d1_ref, o_ref):
        part = pl.program_id(0)
        for h in range(H):
            sl = slice(h * hd, (h + 1) * hd)
            c = c_ref[0, :, sl]
            s = _silu(c)
            dout = d0_ref[0, :, sl] + d1_ref[0, :, sl]
            rs = lax.rsqrt(jnp.sum(s * s, axis=-1, keepdims=True) + EPS)
            sh = s * rs
            dnorm = rs * (dout - sh * jnp.sum(dout * sh, axis=-1, keepdims=True))
            ds = jnp.where(part == 0, dnorm * qscale, jnp.where(part == 1, dnorm, dout))
            o_ref[0, :, sl] = ds * _dsilu(c)

    spec = pl.BlockSpec((1, tm, H * hd), lambda p, b, t: (b, t, p))
    return pl.pallas_call(
        body, name="dn_act_bwd", grid=(3, B, T // tm), in_specs=[spec, spec, spec], out_specs=spec,
        out_shape=jax.ShapeDtypeStruct((B, T, C3), F32),
        compiler_params=_cparams(("parallel",) * 3, VMEM_LIMIT))(cv, dqkv[0], dqkv[1])


def _chunk_of(d, s, ncc, nch):
    if d == 0:
        return s
    return jnp.where(s < ncc, ncc - 1 - s, nch - 1 - (s - ncc))


def _delta_masks(d):
    row = lax.broadcasted_iota(jnp.int32, (CHUNK, CHUNK), 0)
    col = lax.broadcasted_iota(jnp.int32, (CHUNK, CHUNK), 1)
    sign = 1 - 2 * d
    diff = (row - col) * sign
    return diff >= 0, diff > 0, diff <= 0


def _each(f, *lists):
    return [f(*a) for a in zip(*lists)]


def _unit_tri_inverse(As):
    C = As[0].shape[0]
    row = lax.broadcasted_iota(jnp.int32, (C, C), 0)
    col = lax.broadcasted_iota(jnp.int32, (C, C), 1)
    eye = jnp.where(row == col, 1.0, 0.0).astype(F32)
    same = lambda shift: jnp.right_shift(row, shift) == jnp.right_shift(col, shift)
    in8 = same(3)
    xs = [-jnp.where(in8, a, 0.0) for a in As]
    ts = [eye + x for x in xs]
    ps = _each(lambda x: _dot(x, x), xs)
    tp = _each(lambda t, p: _dot(_rows(t, p), p), ts, ps)
    ts = _each(lambda t, m: t + m[:C], ts, tp)
    ts = _each(lambda t, m: t + _dot(t, m[C:]), ts, tp)
    shift = 3
    while (1 << shift) < C:
        off = jnp.logical_and(same(shift + 1), jnp.right_shift(row, shift) != jnp.right_shift(col, shift))
        los = [jnp.where(off, a, 0.0) for a in As]
        ts = _each(lambda t, lo: t - _dot(t, _dot(lo, t)), ts, los)
        shift += 1
    def residual(a, t):
        (ah, al), (th, tl) = _split2(eye + a), _split2(t)
        m = _dot(_rows(ah, al), th)
        return eye - (m[:C] + (m[C:] + _dot(ah, tl)))

    rs = _each(residual, As, ts)
    return _each(lambda t, r: t + _dot(t, r), ts, rs)


def _rows(*xs):
    return jnp.concatenate(xs, axis=0)


def _cols(*xs):
    return jnp.concatenate(xs, axis=1)


def _delta_chunk_fwd(q, k, v, g_i, g_j, beta_i, gl, S, incl, strict, Tm=None):
    D_ = _each(lambda gi, gj, m: jnp.where(m, jnp.exp(jnp.where(m, gi - gj, 0.0)), 0.0), g_i, g_j, incl)
    C, dk = k[0].shape
    kb = _each(lambda k_, b: k_ * b, k, beta_i)
    kq = _each(lambda kb_, q_, k_: _dot(_rows(kb_, q_), k_, "nt"), kb, q, k)
    A = _each(lambda m_, d, m: jnp.where(m, m_[:C] * d, 0.0), kq, D_, strict)
    P = _each(lambda m_, d, m: jnp.where(m, m_[C:] * d, 0.0), kq, D_, incl)
    if Tm is None:
        Tm = _unit_tri_inverse(A)
    gam = _each(jnp.exp, g_i)
    wu = _each(lambda t, kb_, g, v_, b: _dot(t, _cols(kb_ * g, v_ * b)), Tm, kb, gam, v, beta_i)
    w = [m_[:, :dk] for m_ in wu]
    u = [m_[:, dk:] for m_ in wu]
    qg = _each(lambda q_, g: q_ * g, q, gam)
    ws = _each(lambda w_, qg_, s: _dot(_rows(w_, qg_), s), w, qg, S)
    vn = _each(lambda u_, m_: u_ - m_[:C], u, ws)
    o = _each(lambda m_, p, vn_: m_[C:] + _dot(p, vn_), ws, P, vn)
    e_i = _each(lambda gl_, gi: jnp.exp(gl_ - gi), gl, g_i)
    kd = _each(lambda k_, e: k_ * e, k, e_i)
    Gam = _each(jnp.exp, gl)
    S_new = _each(lambda s, g, kd_, vn_: s * g + _dot(kd_, vn_, "tn"), S, Gam, kd, vn)
    return dict(D=D_, kb=kb, A=A, Tm=Tm, gam=gam, w=w, u=u, vn=vn, P=P, qg=qg, o=o, e=e_i, kd=kd, Gam=Gam, S_new=S_new)


def _delta_gb(gb_ref, d, incl, incl_t, H):
    gb = gb_ref[0]
    g = gb[:, d * H:(d + 1) * H]
    beta = gb[:, (2 + d) * H:(3 + d) * H]
    gc_col = _dot_mask(incl.astype(BF16), g)
    gc_row = _dot_mask(incl_t.astype(BF16), g, "xtn")
    gl = jnp.sum(g, axis=0, keepdims=True)
    return beta, gc_col, gc_row, gl


def _delta_fwd(qkvn, gb, H, hd, ncc):
    B, T, _ = qkvn.shape
    nch = T // CHUNK
    HD = H * hd
    pairs = [(d, h) for d in range(2) for h in range(H)]

    def body(q0, k0, v0, gb0, q1, k1, v1, gb1, o0, o1, sin0, sin1, tm0, tm1, S_ref):
        s = pl.program_id(1)
        q_refs, k_refs, v_refs, gb_refs = (q0, q1), (k0, k1), (v0, v1), (gb0, gb1)
        o_refs, sin_refs, tm_refs = (o0, o1), (sin0, sin1), (tm0, tm1)

        @pl.when(s == 0)
        def _():
            S_ref[...] = jnp.zeros_like(S_ref)

        masks = [_delta_masks(d) for d in range(2)]
        gbs = [_delta_gb(gb_refs[d], d, masks[d][0], masks[d][2], H) for d in range(2)]
        head = lambda ref, h: ref[0, :, h * hd:(h + 1) * hd]
        S = [S_ref[d, h] for d, h in pairs]
        r = _delta_chunk_fwd([head(q_refs[d], h) for d, h in pairs], [head(k_refs[d], h) for d, h in pairs],
                             [head(v_refs[d], h) for d, h in pairs],
                             [gbs[d][1][:, h:h + 1] for d, h in pairs], [gbs[d][2][h:h + 1, :] for d, h in pairs],
                             [gbs[d][0][:, h:h + 1] for d, h in pairs], [gbs[d][3][:, h:h + 1] for d, h in pairs],
                             S, [masks[d][0] for d, h in pairs], [masks[d][1] for d, h in pairs])
        for i, (d, h) in enumerate(pairs):
            sin_refs[d][0, 0, h] = S[i]
            tm_refs[d][0, 0, h] = r["Tm"][i]
            S_ref[d, h] = r["S_new"][i]
            o_refs[d][0, :, h * hd:(h + 1) * hd] = r["o"][i]

    def dir_specs(d):
        cidx = lambda b, s: _chunk_of(d, s, ncc, nch)
        ins = [pl.BlockSpec((1, CHUNK, HD), lambda b, s, p=p: (b, cidx(b, s), p)) for p in range(3)]
        ins.append(pl.BlockSpec((1, CHUNK, LANES), lambda b, s: (b, cidx(b, s), 0)))
        return (ins, pl.BlockSpec((1, CHUNK, HD), lambda b, s: (b, cidx(b, s), 0)),
                pl.BlockSpec((1, 1, H, hd, hd), lambda b, s: (b, cidx(b, s), 0, 0, 0)),
                pl.BlockSpec((1, 1, H, CHUNK, CHUNK), lambda b, s: (b, cidx(b, s), 0, 0, 0)))

    (in0, o_s0, sin_s0, tm_s0), (in1, o_s1, sin_s1, tm_s1) = dir_specs(0), dir_specs(1)
    o_shape = jax.ShapeDtypeStruct((B, T, HD), F32)
    sin_shape = jax.ShapeDtypeStruct((B, nch, H, hd, hd), F32)
    tm_shape = jax.ShapeDtypeStruct((B, nch, H, CHUNK, CHUNK), F32)
    return pl.pallas_call(
        body, name="delta_fwd", grid=(B, nch),
        in_specs=in0 + in1, out_specs=(o_s0, o_s1, sin_s0, sin_s1, tm_s0, tm_s1),
        out_shape=(o_shape, o_shape, sin_shape, sin_shape, tm_shape, tm_shape),
        scratch_shapes=[pltpu.VMEM((2, H, hd, hd), F32)],
        compiler_params=_cparams(("parallel", "arbitrary"), VMEM_LIMIT),
    )(qkvn, qkvn, qkvn, gb, qkvn, qkvn, qkvn, gb)


def _delta_bwd(qkvn, gb, sin, tms, do, H, hd, ncc):
    B, T, _ = qkvn.shape
    nch = T // CHUNK
    HD = H * hd
    pairs = [(d, h) for d in range(2) for h in range(H)]

    def body(q0, k0, v0, gb0, sin0, tm0, do0, q1, k1, v1, gb1, sin1, tm1, do1, dqkv0, dqkv1, dgb0, dgb1, dS_ref):
        s = pl.program_id(1)
        tm_refs = (tm0, tm1)
        q_refs, k_refs, v_refs, gb_refs = (q0, q1), (k0, k1), (v0, v1), (gb0, gb1)
        sin_refs, do_refs, dqkv_refs, dgb_refs = (sin0, sin1), (do0, do1), (dqkv0, dqkv1), (dgb0, dgb1)

        @pl.when(s == 0)
        def _():
            dS_ref[...] = jnp.zeros_like(dS_ref)

        masks = [_delta_masks(d) for d in range(2)]
        gbs = [_delta_gb(gb_refs[d], d, masks[d][0], masks[d][2], H) for d in range(2)]
        incl = [masks[d][0] for d, h in pairs]
        strict = [masks[d][1] for d, h in pairs]
        lane = lax.broadcasted_iota(jnp.int32, (1, LANES), 1)
        ones = jnp.ones((3 * CHUNK, LANES), BF16)
        head = lambda ref, h: ref[0, :, h * hd:(h + 1) * hd]
        q = [head(q_refs[d], h) for d, h in pairs]
        k = [head(k_refs[d], h) for d, h in pairs]
        v = [head(v_refs[d], h) for d, h in pairs]
        do_ = [head(do_refs[d], h) for d, h in pairs]
        beta_i = [gbs[d][0][:, h:h + 1] for d, h in pairs]
        S = [sin_refs[d][0, 0, h] for d, h in pairs]
        dSn = [dS_ref[d, h] for d, h in pairs]
        f = _delta_chunk_fwd(q, k, v, [gbs[d][1][:, h:h + 1] for d, h in pairs], [gbs[d][2][h:h + 1, :] for d, h in pairs],
                             beta_i, [gbs[d][3][:, h:h + 1] for d, h in pairs], S, incl, strict,
                             Tm=[tm_refs[d][0, 0, h] for d, h in pairs])
        rsum = lambda x: jnp.sum(x, axis=1, keepdims=True)
        dvn = _each(lambda p, d_, kd, ds: _dot(p, d_, "tn") + _dot(kd, ds), f["P"], do_, f["kd"], dSn)
        dP = _each(lambda d_, vn, m: jnp.where(m, _dot(d_, vn, "nt"), 0.0), do_, f["vn"], incl)
        dqg = _each(lambda d_, s: _dot(d_, s, "nt"), do_, S)
        dS_in = _each(lambda qg, d_, g, ds, w, dv_: _dot(qg, d_, "tn") + g * ds - _dot(w, dv_, "tn"),
                      f["qg"], do_, f["Gam"], dSn, f["w"], dvn)
        dGam = _each(lambda s, ds: jnp.sum(rsum(s * ds), axis=0, keepdims=True), S, dSn)
        dkd = _each(lambda vn, ds: _dot(vn, ds, "nt"), f["vn"], dSn)
        de = _each(lambda dkd_, k_, e: rsum(dkd_ * k_) * e, dkd, k, f["e"])
        dw = _each(lambda dv_, s: -_dot(dv_, s, "nt"), dvn, S)
        dXw = _each(lambda t, dw_: _dot(t, dw_, "tn"), f["Tm"], dw)
        dXu = _each(lambda t, dv_: _dot(t, dv_, "tn"), f["Tm"], dvn)
        dA = _each(lambda xw, w, xu, u, m: -jnp.where(m, _dot(xw, w, "nt") + _dot(xu, u, "nt"), 0.0),
                   dXw, f["w"], dXu, f["u"], strict)
        dM = _each(lambda a, d_: a * d_, dA, f["D"])
        dN = _each(lambda p, d_: p * d_, dP, f["D"])
        dkb = _each(lambda m, k_, xw, g: _dot(m, k_) + xw * g, dM, k, dXw, f["gam"])
        dq = _each(lambda dqg_, g, n, k_: dqg_ * g + _dot(n, k_), dqg, f["gam"], dN, k)
        dk = _each(lambda dkd_, e, m, kb, n, q_, dkb_, b: dkd_ * e + _dot(m, kb, "tn") + _dot(n, q_, "tn") + dkb_ * b,
                   dkd, f["e"], dM, f["kb"], dN, q, dkb, beta_i)
        dv = _each(lambda xu, b: xu * b, dXu, beta_i)
        dbeta = _each(lambda dkb_, k_, xu, v_: rsum(dkb_ * k_) + rsum(xu * v_), dkb, k, dXu, v)
        E = _each(lambda a, A_, p, P_: a * A_ + p * P_, dA, f["A"], dP, f["P"])

        def colsum(e):
            return _dot(_rows(*_split3(e)), ones, "tn")[:, 0:1]

        dg = _each(lambda e, dqg_, qg, xw, kb, g, de_: rsum(e) - colsum(e) + rsum(dqg_ * qg) + rsum(xw * kb) * g - de_,
                   E, dqg, f["qg"], dXw, f["kb"], f["gam"], de)
        dgl_h = _each(lambda de_, dg_, g: jnp.sum(de_, axis=0, keepdims=True) + dg_ * g, de, dGam, f["Gam"])
        for d in range(2):
            dgc = jnp.zeros((CHUNK, LANES), F32)
            dgl = jnp.zeros((1, LANES), F32)
            for h in range(H):
                i = d * H + h
                g_lane, b_lane = d * H + h, (2 + d) * H + h
                dgc = dgc + dg[i] * (lane == g_lane).astype(F32) + dbeta[i] * (lane == b_lane).astype(F32)
                dgl = dgl + dgl_h[i] * (lane == g_lane).astype(F32)
                dS_ref[d, h] = dS_in[i]
                dqkv_refs[d][0, :, h * hd:(h + 1) * hd] = dq[i]
                dqkv_refs[d][0, :, HD + h * hd:HD + (h + 1) * hd] = dk[i]
                dqkv_refs[d][0, :, 2 * HD + h * hd:2 * HD + (h + 1) * hd] = dv[i]
            draw = _dot_mask(masks[d][0].astype(BF16), dgc, "tn") + dgl
            dgb_refs[d][0] = jnp.where(lane < 2 * H, draw, dgc)

    def dir_specs(d):
        cidx = lambda b, s: _chunk_of(d, nch - 1 - s, ncc, nch)
        ins = [pl.BlockSpec((1, CHUNK, HD), lambda b, s, p=p: (b, cidx(b, s), p)) for p in range(3)]
        ins += [pl.BlockSpec((1, CHUNK, LANES), lambda b, s: (b, cidx(b, s), 0)),
                pl.BlockSpec((1, 1, H, hd, hd), lambda b, s: (b, cidx(b, s), 0, 0, 0)),
                pl.BlockSpec((1, 1, H, CHUNK, CHUNK), lambda b, s: (b, cidx(b, s), 0, 0, 0)),
                pl.BlockSpec((1, CHUNK, HD), lambda b, s: (b, cidx(b, s), 0))]
        return (ins, pl.BlockSpec((1, CHUNK, 3 * HD), lambda b, s: (b, cidx(b, s), 0)),
                pl.BlockSpec((1, CHUNK, LANES), lambda b, s: (b, cidx(b, s), 0)))

    (in0, dq_s0, dg_s0), (in1, dq_s1, dg_s1) = dir_specs(0), dir_specs(1)
    dq_shape = jax.ShapeDtypeStruct((B, T, 3 * HD), F32)
    dg_shape = jax.ShapeDtypeStruct((B, T, LANES), F32)
    return pl.pallas_call(
        body, name="delta_bwd", grid=(B, nch),
        in_specs=in0 + in1, out_specs=(dq_s0, dq_s1, dg_s0, dg_s1),
        out_shape=(dq_shape, dq_shape, dg_shape, dg_shape),
        scratch_shapes=[pltpu.VMEM((2, H, hd, hd), F32)],
        compiler_params=_cparams(("parallel", "arbitrary"), VMEM_LIMIT),
    )(qkvn, qkvn, qkvn, gb, sin[0], tms[0], do, qkvn, qkvn, qkvn, gb, sin[1], tms[1], do)


def _dn_out(o2, pz, zcb0, onorm, H, hd, nct, tm):
    B, T, HD = o2[0].shape
    N = T - nct * tm

    def body(o0_ref, o1_ref, z_ref, g_ref, y_ref):
        for h in range(H):
            sl = slice(h * hd, (h + 1) * hd)
            o = o0_ref[0, :, sl] + o1_ref[0, :, sl]
            r = lax.rsqrt(jnp.mean(o * o, axis=-1, keepdims=True) + EPS)
            y_ref[0, :, sl] = (o * r * g_ref[...] * _silu(z_ref[0, :, sl])).astype(BF16)

    return pl.pallas_call(
        body, name="dn_out", grid=(B, N // tm),
        in_specs=[pl.BlockSpec((1, tm, HD), lambda b, t: (b, t + nct, 0)),
                  pl.BlockSpec((1, tm, HD), lambda b, t: (b, t + nct, 0)),
                  pl.BlockSpec((1, tm, HD), lambda b, t: (b, t + nct, zcb0)),
                  pl.BlockSpec((1, hd), lambda b, t: (0, 0))],
        out_specs=pl.BlockSpec((1, tm, HD), lambda b, t: (b, t, 0)),
        out_shape=jax.ShapeDtypeStruct((B, N, HD), BF16),
        compiler_params=_cparams(("parallel",) * 2, VMEM_LIMIT))(o2[0], o2[1], pz, onorm)


def _dn_out_bwd(o2, pz, zcb0, onorm, dy, H, hd, nct, tm):
    B, T, HD = o2[0].shape
    nt = T // tm

    def body(o0_ref, o1_ref, z_ref, g_ref, dy_ref, do_ref, dz_ref, acc_ref):
        t = pl.program_id(1)

        @pl.when(t == 0)
        def _():
            acc_ref[...] = jnp.zeros_like(acc_ref)

        @pl.when(t < nct)
        def _():
            do_ref[0] = jnp.zeros_like(do_ref[0])
            dz_ref[0] = jnp.zeros_like(dz_ref[0])

        @pl.when(t >= nct)
        def _():
            g = g_ref[...]
            for h in range(H):
                sl = slice(h * hd, (h + 1) * hd)
                o = o0_ref[0, :, sl] + o1_ref[0, :, sl]
                z = z_ref[0, :, sl]
                dy_ = dy_ref[0, :, sl]
                r = lax.rsqrt(jnp.mean(o * o, axis=-1, keepdims=True) + EPS)
                oh = o * r
                dz_ref[0, :, sl] = (dy_ * oh * g * _dsilu(z)).astype(BF16)
                dn = dy_ * _silu(z)
                acc_ref[0, 0:1, :] += jnp.sum(dn * oh, axis=0, keepdims=True)
                doh = dn * g
                do_ref[0, :, sl] = r * (doh - oh * jnp.mean(doh * oh, axis=-1, keepdims=True))

    lat = lambda b, t: (b, jnp.maximum(t - nct, 0), 0)
    row = pl.BlockSpec((1, tm, HD), lambda b, t: (b, t, 0))
    return pl.pallas_call(
        body, name="dn_out_bwd", grid=(B, nt),
        in_specs=[row, row,
                  pl.BlockSpec((1, tm, HD), lambda b, t: (b, t, zcb0)),
                  pl.BlockSpec((1, hd), lambda b, t: (0, 0)),
                  pl.BlockSpec((1, tm, HD), lat)],
        out_specs=(row, row, pl.BlockSpec((1, SUBLANES, hd), lambda b, t: (b, 0, 0))),
        out_shape=(jax.ShapeDtypeStruct((B, T, HD), F32), jax.ShapeDtypeStruct((B, T, HD), BF16),
                   jax.ShapeDtypeStruct((B, SUBLANES, hd), F32)),
        compiler_params=_cparams(("parallel", "arbitrary"), VMEM_LIMIT),
    )(o2[0], o2[1], pz, onorm, dy)


def _lru_gate_math(x, wr, wi, br, bi, lam):
    r = _sigmoid(_dot(x, wr) + br)
    i = _sigmoid(_dot(x, wi) + bi)
    sp = _softplus(-lam)
    la = -LRU_C * r * sp
    a = jnp.exp(la)
    s = jnp.sqrt(-jnp.tanh(la) * (a * a + 1.0))
    return r, i, sp, a, s


def _lru_gates(xc, wbd, bias4, lam, tm, bw):
    B, T, W = xc.shape

    def body(x_ref, w_ref, b_ref, l_ref, a_ref, i_ref):
        x = x_ref[0]
        for d in range(2):
            _, i, _, a, s = _lru_gate_math(x, w_ref[2 * d, 0], w_ref[2 * d + 1, 0],
                                           b_ref[2 * d:2 * d + 1, :], b_ref[2 * d + 1:2 * d + 2, :], l_ref[d:d + 1, :])
            a_ref[d, 0] = a
            i_ref[d, 0] = s * (i * x)

    out = pl.BlockSpec((2, 1, tm, bw), lambda b, t, j: (0, b, t, j))
    return pl.pallas_call(
        body, name="lru_gates", grid=(B, T // tm, W // bw),
        in_specs=[pl.BlockSpec((1, tm, bw), lambda b, t, j: (b, t, j)),
                  pl.BlockSpec((4, 1, bw, bw), lambda b, t, j: (0, j, 0, 0)),
                  pl.BlockSpec((4, bw), lambda b, t, j: (0, j)),
                  pl.BlockSpec((2, bw), lambda b, t, j: (0, j))],
        out_specs=(out, out),
        out_shape=(jax.ShapeDtypeStruct((2, B, T, W), F32), jax.ShapeDtypeStruct((2, B, T, W), F32)),
        compiler_params=_cparams(("parallel",) * 3, VMEM_LIMIT))(xc, wbd, bias4, lam)


def _lru_gates_bwd(xc, wbd, bias4, lam, dinp, da, tm, bw):
    B, T, W = xc.shape
    nt = T // tm

    def body(x_ref, w_ref, b_ref, l_ref, di0_ref, di1_ref, da0_ref, da1_ref, dx_ref, dw_ref, acc_ref):
        di_refs, da_refs = (di0_ref, di1_ref), (da0_ref, da1_ref)
        b_ = pl.program_id(1)
        t = pl.program_id(2)

        @pl.when(jnp.logical_and(b_ == 0, t == 0))
        def _():
            dw_ref[...] = jnp.zeros_like(dw_ref)
            acc_ref[...] = jnp.zeros_like(acc_ref)

        x = x_ref[0]
        dx = jnp.zeros_like(x)
        for d in range(2):
            wr, wi = w_ref[2 * d, 0], w_ref[2 * d + 1, 0]
            lam_ = l_ref[d:d + 1, :]
            r, i, sp, a, s = _lru_gate_math(x, wr, wi, b_ref[2 * d:2 * d + 1, :], b_ref[2 * d + 1:2 * d + 2, :], lam_)
            dinp_ = di_refs[d][0]
            dix = dinp_ * s
            ds = dinp_ * i * x
            dla = da_refs[d][0] * a - ds * (a * a) / s
            dpr = dla * (-LRU_C * sp) * r * (1.0 - r)
            dpi = dix * x * i * (1.0 - i)
            dx = dx + dix * i + _dot(dpr, wr, "nt") + _dot(dpi, wi, "nt")
            dw_ref[2 * d, 0] += _dot(x, dpr, "tn")
            dw_ref[2 * d + 1, 0] += _dot(x, dpi, "tn")
            acc_ref[2 * d:2 * d + 1, :] += jnp.sum(dpr, axis=0, keepdims=True)
            acc_ref[2 * d + 1:2 * d + 2, :] += jnp.sum(dpi, axis=0, keepdims=True)
            acc_ref[4 + d:5 + d, :] += jnp.sum(dla * (-LRU_C * r), axis=0, keepdims=True) * (-_sigmoid(-lam_))
        dx_ref[0] = dx

    tile = pl.BlockSpec((1, tm, bw), lambda j, b, t: (b, t, j))
    return pl.pallas_call(
        body, name="lru_gates_bwd", grid=(W // bw, B, nt),
        in_specs=[pl.BlockSpec((1, tm, bw), lambda j, b, t: (b, t, j)),
                  pl.BlockSpec((4, 1, bw, bw), lambda j, b, t: (0, j, 0, 0)),
                  pl.BlockSpec((4, bw), lambda j, b, t: (0, j)),
                  pl.BlockSpec((2, bw), lambda j, b, t: (0, j)), tile, tile, tile, tile],
        out_specs=(pl.BlockSpec((1, tm, bw), lambda j, b, t: (b, t, j)),
                   pl.BlockSpec((4, 1, bw, bw), lambda j, b, t: (0, j, 0, 0)),
                   pl.BlockSpec((SUBLANES, bw), lambda j, b, t: (0, j))),
        out_shape=(jax.ShapeDtypeStruct((B, T, W), F32), jax.ShapeDtypeStruct(wbd.shape, F32),
                   jax.ShapeDtypeStruct((SUBLANES, W), F32)),
        compiler_params=_cparams(("parallel", "arbitrary", "arbitrary"), VMEM_LIMIT),
    )(xc, wbd, bias4, lam, dinp[0], dinp[1], da[0], da[1])


def _tile_scan(a, x, rev, tm):
    row = lax.broadcasted_iota(jnp.int32, a.shape, 0)
    s = 1
    while s < tm:
        if rev:
            a_sh, x_sh, ok = pltpu.roll(a, tm - s, 0), pltpu.roll(x, tm - s, 0), row < tm - s
        else:
            a_sh, x_sh, ok = pltpu.roll(a, s, 0), pltpu.roll(x, s, 0), row >= s
        x = jnp.where(ok, x + a * x_sh, x)
        a = jnp.where(ok, a * a_sh, a)
        s *= 2
    return a, x


def _scan_tile_of(s, rev, nct, nt):
    if not rev:
        return s
    return jnp.where(s < nct, nct - 1 - s, nt - 1 - (s - nct))


def _lru_scan(a2, x2, d, nct, tm, tc):
    _, B, T, W = a2.shape
    nt = T // tm
    rev = d == 1
    last = 0 if rev else tm - 1

    def body(a_ref, x_ref, h_ref, carry):
        s = pl.program_id(2)

        @pl.when(s == 0)
        def _():
            carry[...] = jnp.zeros_like(carry)

        A, X = _tile_scan(a_ref[0, 0], x_ref[0, 0], rev, tm)
        h = X + A * carry[0:1, :]
        h_ref[0] = h
        carry[...] = jnp.broadcast_to(h[last:last + 1, :], carry.shape)

    tidx = lambda s: _scan_tile_of(s, rev, nct, nt)
    spec = pl.BlockSpec((1, 1, tm, tc), lambda b, j, s: (d, b, tidx(s), j))
    return pl.pallas_call(
        body, name=f"lru_scan{d}", grid=(B, W // tc, nt),
        in_specs=[spec, spec], out_specs=pl.BlockSpec((1, tm, tc), lambda b, j, s: (b, tidx(s), j)),
        out_shape=jax.ShapeDtypeStruct((B, T, W), F32),
        scratch_shapes=[pltpu.VMEM((SUBLANES, tc), F32)],
        compiler_params=_cparams(("parallel", "parallel", "arbitrary"), VMEM_LIMIT),
    )(a2, x2)


def _lru_scan_bwd(a2, h, dh, d, nct, tm, tc):
    _, B, T, W = a2.shape
    nt = T // tm
    rev = d == 1
    first = tm - 1 if rev else 0
    r8, n8 = tm // SUBLANES, T // SUBLANES

    def body(a_ref, h_ref, hp_ref, dh_ref, lam_ref, da_ref, ca, cl):
        s = pl.program_id(2)

        @pl.when(s == 0)
        def _():
            ca[...] = jnp.zeros_like(ca)
            cl[...] = jnp.zeros_like(cl)

        a = a_ref[0, 0]
        row = lax.broadcasted_iota(jnp.int32, a.shape, 0)
        if rev:
            c = jnp.where(row == 0, ca[0:1, :], pltpu.roll(a, 1, 0))
        else:
            c = jnp.where(row == tm - 1, ca[0:1, :], pltpu.roll(a, tm - 1, 0))
        C, L = _tile_scan(c, dh_ref[0], not rev, tm)
        lam = L + C * cl[0:1, :]
        lam_ref[0] = lam
        hp = jnp.where(s == nt - 1, 0.0, hp_ref[0])
        if rev:
            hprev = jnp.where(row == tm - 1, hp[0:1, :], pltpu.roll(h_ref[0], tm - 1, 0))
        else:
            hprev = jnp.where(row == 0, hp[SUBLANES - 1:SUBLANES, :], pltpu.roll(h_ref[0], 1, 0))
        da_ref[0] = lam * hprev
        ca[...] = jnp.broadcast_to(a[first:first + 1, :], ca.shape)
        cl[...] = jnp.broadcast_to(lam[first:first + 1, :], cl.shape)

    tidx = lambda s: _scan_tile_of(nt - 1 - s, rev, nct, nt)

    def hp_idx(b, j, s):
        tt = tidx(s)
        if rev:
            blk = jnp.where(tt == nt - 1, 0, jnp.minimum((tt + 1) * r8, n8 - 1))
        else:
            blk = jnp.maximum(tt * r8 - 1, 0)
        return (b, blk, j)

    tile = pl.BlockSpec((1, tm, tc), lambda b, j, s: (b, tidx(s), j))
    return pl.pallas_call(
        body, name=f"lru_scan_bwd{d}", grid=(B, W // tc, nt),
        in_specs=[pl.BlockSpec((1, 1, tm, tc), lambda b, j, s: (d, b, tidx(s), j)), tile,
                  pl.BlockSpec((1, SUBLANES, tc), hp_idx), tile],
        out_specs=(tile, tile),
        out_shape=(jax.ShapeDtypeStruct((B, T, W), F32), jax.ShapeDtypeStruct((B, T, W), F32)),
        scratch_shapes=[pltpu.VMEM((SUBLANES, tc), F32), pltpu.VMEM((SUBLANES, tc), F32)],
        compiler_params=_cparams(("parallel", "parallel", "arbitrary"), VMEM_LIMIT),
    )(a2, h, h, dh)


def _lru_out(h0, h1, pyl, ycb0, nct, tm, tc):
    B, T, W = h0.shape
    N = T - nct * tm

    def body(h0_ref, h1_ref, y_ref, o_ref):
        o_ref[0] = ((h0_ref[0] + h1_ref[0]) * _gelu(y_ref[0])).astype(BF16)

    hs = pl.BlockSpec((1, tm, tc), lambda b, t, j: (b, t + nct, j))
    return pl.pallas_call(
        body, name="lru_out", grid=(B, N // tm, W // tc),
        in_specs=[hs, hs, pl.BlockSpec((1, tm, tc), lambda b, t, j: (b, t + nct, ycb0 + j))],
        out_specs=pl.BlockSpec((1, tm, tc), lambda b, t, j: (b, t, j)),
        out_shape=jax.ShapeDtypeStruct((B, N, W), BF16),
        compiler_params=_cparams(("parallel",) * 3, VMEM_LIMIT))(h0, h1, pyl)


def _lru_out_bwd(h0, h1, pyl, ycb0, dy, nct, tm, tc):
    B, T, W = h0.shape

    def body(h0_ref, h1_ref, y_ref, dy_ref, dh_ref, dyl_ref):
        t = pl.program_id(1)

        @pl.when(t < nct)
        def _():
            dh_ref[0] = jnp.zeros_like(dh_ref[0])
            dyl_ref[0] = jnp.zeros_like(dyl_ref[0])

        @pl.when(t >= nct)
        def _():
            y = y_ref[0]
            dy_ = dy_ref[0]
            dh_ref[0] = dy_ * _gelu(y)
            dyl_ref[0] = (dy_ * (h0_ref[0] + h1_ref[0]) * _dgelu(y)).astype(BF16)

    hs = pl.BlockSpec((1, tm, tc), lambda b, t, j: (b, t, j))
    return pl.pallas_call(
        body, name="lru_out_bwd", grid=(B, T // tm, W // tc),
        in_specs=[hs, hs, pl.BlockSpec((1, tm, tc), lambda b, t, j: (b, t, ycb0 + j)),
                  pl.BlockSpec((1, tm, tc), lambda b, t, j: (b, jnp.maximum(t - nct, 0), j))],
        out_specs=(hs, hs),
        out_shape=(jax.ShapeDtypeStruct((B, T, W), F32), jax.ShapeDtypeStruct((B, T, W), BF16)),
        compiler_params=_cparams(("parallel",) * 3, VMEM_LIMIT))(h0, h1, pyl, dy)


def _merge(bd, bl, pmg, bm, nct, tm):
    B, N, D = bd.shape

    def body(bd_ref, bl_ref, m0_ref, m1_ref, b_ref, o_ref):
        g0 = _sigmoid(m0_ref[0] + b_ref[:, 0:D])
        g1 = _sigmoid(m1_ref[0] + b_ref[:, D:2 * D])
        o_ref[0] = (g0 * bd_ref[0] + g1 * bl_ref[0]).astype(BF16)

    row = pl.BlockSpec((1, tm, D), lambda b, t: (b, t, 0))
    return pl.pallas_call(
        body, name="merge", grid=(B, N // tm),
        in_specs=[row, row, pl.BlockSpec((1, tm, D), lambda b, t: (b, t + nct, 0)),
                  pl.BlockSpec((1, tm, D), lambda b, t: (b, t + nct, 1)),
                  pl.BlockSpec((1, 2 * D), lambda b, t: (0, 0))],
        out_specs=row, out_shape=jax.ShapeDtypeStruct((B, N, D), BF16),
        compiler_params=_cparams(("parallel", "parallel"), VMEM_LIMIT))(bd, bl, pmg, pmg, bm)


def _merge_bwd(dmi, bd, bl, pmg, bm, nct, tm):
    B, N, D = bd.shape
    T = pmg.shape[1]

    def body(dm_ref, bd_ref, bl_ref, m0_ref, m1_ref, b_ref, dbd_ref, dbl_ref, dmg_ref, acc_ref):
        t = pl.program_id(1)

        @pl.when(t == 0)
        def _():
            acc_ref[...] = jnp.zeros_like(acc_ref)

        @pl.when(t < nct)
        def _():
            dmg_ref[0] = jnp.zeros_like(dmg_ref[0])

        @pl.when(t >= nct)
        def _():
            dm = dm_ref[0]
            g0 = _sigmoid(m0_ref[0] + b_ref[:, 0:D])
            g1 = _sigmoid(m1_ref[0] + b_ref[:, D:2 * D])
            dbd_ref[0] = (dm * g0).astype(BF16)
            dbl_ref[0] = (dm * g1).astype(BF16)
            d0 = dm * bd_ref[0] * g0 * (1.0 - g0)
            d1 = dm * bl_ref[0] * g1 * (1.0 - g1)
            dmg_ref[0, :, 0:D] = d0.astype(BF16)
            dmg_ref[0, :, D:2 * D] = d1.astype(BF16)
            acc_ref[0, 0:1, 0:D] += jnp.sum(d0, axis=0, keepdims=True)
            acc_ref[0, 0:1, D:2 * D] += jnp.sum(d1, axis=0, keepdims=True)

    lat = pl.BlockSpec((1, tm, D), lambda b, t: (b, jnp.maximum(t - nct, 0), 0))
    return pl.pallas_call(
        body, name="merge_bwd", grid=(B, T // tm),
        in_specs=[lat, lat, lat, pl.BlockSpec((1, tm, D), lambda b, t: (b, t, 0)),
                  pl.BlockSpec((1, tm, D), lambda b, t: (b, t, 1)),
                  pl.BlockSpec((1, 2 * D), lambda b, t: (0, 0))],
        out_specs=(lat, lat, pl.BlockSpec((1, tm, 2 * D), lambda b, t: (b, t, 0)),
                   pl.BlockSpec((1, SUBLANES, 2 * D), lambda b, t: (b, 0, 0))),
        out_shape=(jax.ShapeDtypeStruct((B, N, D), BF16), jax.ShapeDtypeStruct((B, N, D), BF16),
                   jax.ShapeDtypeStruct((B, T, 2 * D), BF16), jax.ShapeDtypeStruct((B, SUBLANES, 2 * D), F32)),
        compiler_params=_cparams(("parallel", "arbitrary"), VMEM_LIMIT))(dmi, bd, bl, pmg, pmg, bm)


def _grid_taps():
    return [((di + 1) * 3 + (dj + 1), di, dj) for di in (-1, 0, 1) for dj in (-1, 0, 1)]


def _grid_views(x, n):
    pos = lax.broadcasted_iota(jnp.int32, x.shape, 0)
    gc = jnp.bitwise_and(pos, GRID_W - 1)
    cols = {0: x,
            -1: jnp.where(gc >= 1, pltpu.roll(x, 1, 0), 0.0),
            1: jnp.where(gc <= GRID_W - 2, pltpu.roll(x, n - 1, 0), 0.0)}
    views = {}
    edge = jnp.zeros((GRID_W, x.shape[1]), x.dtype)
    for dj, xc in cols.items():
        views[(0, dj)] = xc
        views[(-1, dj)] = jnp.concatenate([edge, xc[:n - GRID_W]], axis=0)
        views[(1, dj)] = jnp.concatenate([xc[GRID_W:], edge], axis=0)
    return views


def _ffn_act(F, w9, b, tc):
    B, N, C2 = F.shape
    Cf = C2 // 2
    ncb = Cf // tc

    def body(g_ref, v_ref, w_ref, b_ref, o_ref, gel_ref, dgel_ref):
        xv = _grid_views(g_ref[0], N)
        conv = b_ref[...]
        for k, di, dj in _grid_taps():
            conv = conv + xv[(di, dj)] * w_ref[k:k + 1, :]
        th = jnp.tanh(GELU_C * (conv + 0.044715 * conv * conv * conv))
        gel = 0.5 * conv * (1.0 + th)
        o_ref[0] = (gel * v_ref[0]).astype(BF16)
        gel_ref[0] = gel.astype(BF16)
        dgel_ref[0] = (0.5 * (1.0 + th)
                       + 0.5 * conv * (1.0 - th * th) * GELU_C * (1.0 + 3.0 * 0.044715 * conv * conv)).astype(BF16)

    tile = pl.BlockSpec((1, N, tc), lambda b, j: (b, 0, j))
    out = jax.ShapeDtypeStruct((B, N, Cf), BF16)
    return pl.pallas_call(
        body, name="ffn_act", grid=(B, ncb),
        in_specs=[tile, pl.BlockSpec((1, N, tc), lambda b, j: (b, 0, ncb + j)),
                  pl.BlockSpec((9, tc), lambda b, j: (0, j)),
                  pl.BlockSpec((1, tc), lambda b, j: (0, j))],
        out_specs=(tile, tile, tile), out_shape=(out, out, out),
        compiler_params=_cparams(("parallel", "parallel"), VMEM_LIMIT))(F, F, w9, b)


def _ffn_act_bwd(F, w9, gel, dgel, df, tc):
    B, N, C2 = F.shape
    Cf = C2 // 2
    ncb = Cf // tc

    def body(g_ref, v_ref, w_ref, gel_ref, dgel_ref, df_ref, dF_ref, acc_ref):
        p = pl.program_id(2)

        @pl.when(p == 0)
        def _():
            dF_ref[0] = (df_ref[0] * gel_ref[0].astype(F32)).astype(BF16)

        @pl.when(p == 1)
        def _():
            xv = _grid_views(g_ref[0], N)
            df_ = df_ref[0]
            dc = df_ * v_ref[0] * dgel_ref[0].astype(F32)
            dcv = _grid_views(dc, N)
            dx = None
            for k, di, dj in _grid_taps():
                term = dcv[(-di, -dj)] * w_ref[k:k + 1, :]
                dx = term if dx is None else dx + term
                acc_ref[0, k:k + 1, :] = jnp.sum(dc * xv[(di, dj)], axis=0, keepdims=True)
            acc_ref[0, 9:10, :] = jnp.sum(dc, axis=0, keepdims=True)
            acc_ref[0, 10:16, :] = jnp.zeros((6, tc), F32)
            dF_ref[0] = dx.astype(BF16)

    tile = pl.BlockSpec((1, N, tc), lambda b, j, p: (b, 0, j))
    return pl.pallas_call(
        body, name="ffn_act_bwd", grid=(B, ncb, 2),
        in_specs=[tile, pl.BlockSpec((1, N, tc), lambda b, j, p: (b, 0, ncb + j)),
                  pl.BlockSpec((9, tc), lambda b, j, p: (0, j)), tile, tile, tile],
        out_specs=(pl.BlockSpec((1, N, tc), lambda b, j, p: (b, 0, j + (1 - p) * ncb)),
                   pl.BlockSpec((1, 16, tc), lambda b, j, p: (b, 0, j))),
        out_shape=(jax.ShapeDtypeStruct((B, N, C2), BF16), jax.ShapeDtypeStruct((B, 16, Cf), F32)),
        compiler_params=_cparams(("parallel", "parallel", "arbitrary"), VMEM_LIMIT))(F, F, w9, gel, dgel, df)


ADAM_ROWS = 512


def _adamw(w, m, v, gparts, name):
    rows, cols = w.shape[-2:]
    P = gparts.shape[0]
    tr = _row_tile(rows, (P + 14) * 4 * (-(-cols // LANES) * LANES))
    c1 = 1.0 - ADAM_B1 ** ADAM_STEP
    c2 = 1.0 - ADAM_B2 ** ADAM_STEP

    def body(w_ref, m_ref, v_ref, g_ref, go_ref, d_ref, mo_ref, vo_ref):
        g = g_ref[0].astype(F32)
        for p in range(1, P):
            g = g + g_ref[p].astype(F32)
        m_ = ADAM_B1 * m_ref[...] + (1.0 - ADAM_B1) * g
        v_ = ADAM_B2 * v_ref[...] + (1.0 - ADAM_B2) * (g * g)
        go_ref[...] = g
        mo_ref[...] = m_
        vo_ref[...] = v_
        d_ref[...] = -ADAM_LR * ((m_ / c1) / (jnp.sqrt(v_ / c2) + ADAM_EPS) + ADAM_WD * w_ref[...])

    if w.ndim == 3:
        row = pl.BlockSpec((None, tr, cols), lambda i: (0, i, 0))
    else:
        row = pl.BlockSpec((tr, cols), lambda i: (i, 0))
    out = jax.ShapeDtypeStruct(w.shape, F32)
    return pl.pallas_call(
        body, name=name, grid=(rows // tr,),
        in_specs=[row, row, row, pl.BlockSpec((P, tr, cols), lambda i: (0, i, 0))],
        out_specs=(row, row, row, row), out_shape=(out, out, out, out),
        compiler_params=_cparams(("parallel",), VMEM_LIMIT))(w, m, v, gparts)


def _pack_rows(flat_len):
    rows = -(-flat_len // LANES)
    if rows > ADAM_ROWS:
        rows = -(-rows // ADAM_ROWS) * ADAM_ROWS
    else:
        rows = -(-rows // SUBLANES) * SUBLANES
    return rows


PACK_ALIGN = SUBLANES * LANES


def _pack(arrs, lead=()):
    pads = [(0, 0)] * len(lead)
    parts = []
    for a in arrs:
        f = a.reshape(lead + (-1,)).astype(F32)
        parts.append(jnp.pad(f, pads + [(0, -f.shape[-1] % PACK_ALIGN)]))
    flat = jnp.concatenate(parts, axis=-1)
    n = flat.shape[-1]
    rows = _pack_rows(n)
    flat = jnp.pad(flat, pads + [(0, rows * LANES - n)])
    return flat.reshape(lead + (rows, LANES))


def _unpack(buf, shapes):
    out, r = [], 0
    for s in shapes:
        n = math.prod(s)
        nr = -(-n // PACK_ALIGN) * SUBLANES
        out.append(buf[r:r + nr].reshape(-1)[:n].reshape(s))
        r += nr
    return out


def _col_shards(full, n_lead):
    C = full.shape[-1]
    x = full.reshape(full.shape[:-1] + (N_DEV, C // N_DEV))
    return jnp.moveaxis(x, -2, 0)


def _from_col_shards(g):
    x = jnp.moveaxis(g, 0, -2)
    return x.reshape(x.shape[:-2] + (x.shape[-2] * x.shape[-1],))


def kernel(x, c, ctx, c_ctx, w_ada, b_ada, g_pre_mix, g_post_mix, g_pre_ffn, g_post_ffn, w_in, b_merge, dn_conv, dn_a_log, dn_dt_bias, dn_onorm, lru_conv, lru_conv_b, lru_w_rg, lru_b_rg, lru_w_ig, lru_b_ig, lru_lambda, w_branch_dn, w_branch_lru, w_out, w_up, ffn_dw, ffn_dw_b, w_down, loss_target, m_c_ctx, m_w_ada, m_b_ada, m_g_pre_mix, m_g_post_mix, m_g_pre_ffn, m_g_post_ffn, m_w_in, m_b_merge, m_dn_conv, m_dn_a_log, m_dn_dt_bias, m_dn_onorm, m_lru_conv, m_lru_conv_b, m_lru_w_rg, m_lru_b_rg, m_lru_w_ig, m_lru_b_ig, m_lru_lambda, m_w_branch_dn, m_w_branch_lru, m_w_out, m_w_up, m_ffn_dw, m_ffn_dw_b, m_w_down, v_c_ctx, v_w_ada, v_b_ada, v_g_pre_mix, v_g_post_mix, v_g_pre_ffn, v_g_post_ffn, v_w_in, v_b_merge, v_dn_conv, v_dn_a_log, v_dn_dt_bias, v_dn_onorm, v_lru_conv, v_lru_conv_b, v_lru_w_rg, v_lru_b_rg, v_lru_w_ig, v_lru_b_ig, v_lru_lambda, v_w_branch_dn, v_w_branch_lru, v_w_out, v_w_up, v_ffn_dw, v_ffn_dw_b, v_w_down):
    params = dict(c_ctx=c_ctx, w_ada=w_ada, b_ada=b_ada, g_pre_mix=g_pre_mix, g_post_mix=g_post_mix, g_pre_ffn=g_pre_ffn, g_post_ffn=g_post_ffn, w_in=w_in, b_merge=b_merge, dn_conv=dn_conv, dn_a_log=dn_a_log, dn_dt_bias=dn_dt_bias, dn_onorm=dn_onorm, lru_conv=lru_conv, lru_conv_b=lru_conv_b, lru_w_rg=lru_w_rg, lru_b_rg=lru_b_rg, lru_w_ig=lru_w_ig, lru_b_ig=lru_b_ig, lru_lambda=lru_lambda, w_branch_dn=w_branch_dn, w_branch_lru=w_branch_lru, w_out=w_out, w_up=w_up, ffn_dw=ffn_dw, ffn_dw_b=ffn_dw_b, w_down=w_down)
    mom1 = dict(c_ctx=m_c_ctx, w_ada=m_w_ada, b_ada=m_b_ada, g_pre_mix=m_g_pre_mix, g_post_mix=m_g_post_mix, g_pre_ffn=m_g_pre_ffn, g_post_ffn=m_g_post_ffn, w_in=m_w_in, b_merge=m_b_merge, dn_conv=m_dn_conv, dn_a_log=m_dn_a_log, dn_dt_bias=m_dn_dt_bias, dn_onorm=m_dn_onorm, lru_conv=m_lru_conv, lru_conv_b=m_lru_conv_b, lru_w_rg=m_lru_w_rg, lru_b_rg=m_lru_b_rg, lru_w_ig=m_lru_w_ig, lru_b_ig=m_lru_b_ig, lru_lambda=m_lru_lambda, w_branch_dn=m_w_branch_dn, w_branch_lru=m_w_branch_lru, w_out=m_w_out, w_up=m_w_up, ffn_dw=m_ffn_dw, ffn_dw_b=m_ffn_dw_b, w_down=m_w_down)
    mom2 = dict(c_ctx=v_c_ctx, w_ada=v_w_ada, b_ada=v_b_ada, g_pre_mix=v_g_pre_mix, g_post_mix=v_g_post_mix, g_pre_ffn=v_g_pre_ffn, g_post_ffn=v_g_post_ffn, w_in=v_w_in, b_merge=v_b_merge, dn_conv=v_dn_conv, dn_a_log=v_dn_a_log, dn_dt_bias=v_dn_dt_bias, dn_onorm=v_dn_onorm, lru_conv=v_lru_conv, lru_conv_b=v_lru_conv_b, lru_w_rg=v_lru_w_rg, lru_b_rg=v_lru_b_rg, lru_w_ig=v_lru_w_ig, lru_b_ig=v_lru_b_ig, lru_lambda=v_lru_lambda, w_branch_dn=v_w_branch_dn, w_branch_lru=v_w_branch_lru, w_out=v_w_out, w_up=v_w_up, ffn_dw=v_ffn_dw, ffn_dw_b=v_ffn_dw_b, w_down=v_w_down)
    names = list(params)

    B, N, D = x.shape
    NC = ctx.shape[1]
    T = NC + N
    H, hd = dn_a_log.shape[2], dn_onorm.shape[1]
    HD = H * hd
    nd = 2 * H
    W = lru_conv_b.shape[1]
    nblk, bdim = lru_w_rg.shape[2], lru_w_rg.shape[3]
    Cf = ffn_dw_b.shape[1]
    sw = w_ada.shape[2]
    tm = min(256, NC)
    nct = NC // tm
    ncc = NC // CHUNK
    nch = T // CHUNK
    R, RL = B * T, B * N
    bw = min(256, W)
    me = _my_index()
    assert NC % tm == 0 and N % tm == 0 and B + 1 <= SUBLANES and bw % bdim == 0 and D % LANES == 0

    cpad = jnp.zeros((SUBLANES, D), F32).at[:B].set(c).at[B].set(c_ctx)
    ccp = _all_gather([cpad], "ag_cond")[0].reshape(N_DEV * SUBLANES, D)
    mods_sh = _ada_fwd(ccp, w_ada[0], lax.dynamic_slice(b_ada, (0, me * sw), (1, sw)))
    lru_vecs = jnp.concatenate([lru_b_rg[0], lru_b_ig[0], lru_lambda[0], jnp.zeros_like(lru_lambda[0])], axis=0)
    mods_g, w_in_g, dn_conv_g, lru_conv_g, lru_vecs_g = _all_gather(
        [mods_sh, w_in[0].astype(BF16).T, dn_conv[0], lru_conv[0], lru_vecs], "ag_first")
    ag_mix = _split_start([w_branch_dn[0].astype(BF16), w_branch_lru[0].astype(BF16), w_out[0].astype(BF16)],
                          "gather", "ag_mix_start")
    ag_ffn = _split_start([w_up[0].astype(BF16) + ag_mix[4][0, 0].astype(BF16), w_down[0].astype(BF16),
                           ffn_dw[0].reshape(9, -1)], "gather", "ag_ffn_start")
    mine = lax.dynamic_slice(mods_g, (0, me * SUBLANES, 0), (N_DEV, SUBLANES, sw))
    mods = jnp.moveaxis(mine, 0, 1).reshape(SUBLANES, 6, D)[:B + 1]
    mods = jnp.pad(mods, ((0, 0), (0, SUBLANES - 6), (0, 0))) + ag_ffn[4][0, 0]
    dn_conv_f = _from_col_shards(dn_conv_g)
    lru_conv_f = _from_col_shards(lru_conv_g)
    lru_vecs_f = _from_col_shards(lru_vecs_g)
    lru_lam_f = lru_vecs_f[4:6]

    csh = w_in_g.shape[1]
    w_in_t = w_in_g.reshape(N_DEV * csh, D)
    o_z, o_a, o_xl = 3 * HD, 4 * HD, 4 * HD + 2 * nd
    o_yl, o_mg = o_xl + W, o_xl + 2 * W
    w_qkv, w_z = w_in_t[:o_z], w_in_t[o_z:o_a]
    w_ab = jnp.pad(w_in_t[o_a:o_xl], ((0, LANES - 2 * nd), (0, 0)))
    w_xl, w_yl, w_mg = w_in_t[o_xl:o_yl], w_in_t[o_yl:o_mg], w_in_t[o_mg:]

    def blockdiag(wg):
        per = bw // bdim
        g5 = wg.reshape(2, W // bw, per, bdim, bdim)
        eye = jnp.eye(per, dtype=wg.dtype)
        return jnp.einsum("dtpij,pq->dtpiqj", g5, eye).reshape(2, W // bw, bw, bw)

    bd_rg, bd_ig = blockdiag(lru_w_rg[0]), blockdiag(lru_w_ig[0])
    wbd = jnp.stack([bd_rg[0], bd_ig[0], bd_rg[1], bd_ig[1]]).astype(BF16)
    bias4 = jnp.stack([lru_vecs_f[0], lru_vecs_f[2], lru_vecs_f[1], lru_vecs_f[3]])
    alog_v = jnp.pad(dn_a_log.reshape(1, nd), ((0, 0), (0, LANES - nd)))
    dtb_v = jnp.pad(dn_dt_bias.reshape(1, nd), ((0, 0), (0, LANES - nd)))

    h0 = jnp.concatenate([ctx, x], axis=1)
    u1 = _norm_mod_fwd(h0, g_pre_mix, mods, 0, 1, nct, tm, "norm_mod1")
    u1f = u1.reshape(R, D)
    p_qkv = _mm(u1f, w_qkv, "nt", "mm_qkv").reshape(B, T, 3 * HD)
    p_z = _mm(u1f, w_z, "nt", "mm_z").reshape(B, T, HD)
    p_ab = _mm(u1f, w_ab, "nt", "mm_ab")
    p_xl = _mm(u1f, w_xl, "nt", "mm_xl").reshape(B, T, W)
    p_yl = _mm(u1f, w_yl, "nt", "mm_yl").reshape(B, T, W)
    p_mg = _mm(u1f, w_mg, "nt", "mm_mg").reshape(B, T, 2 * D)

    conv_offs = (-2, -1, 0, 1)
    tcc = _tile(HD, 1024)
    gb = _ab_act(p_ab, alog_v, dtb_v, nd, tm)
    gb3 = gb.reshape(B, T, LANES)
    cv, qkvn = _dwconv(p_qkv, 0, 3 * HD, dn_conv_f, None, conv_offs, nct, tm, HD, "dn_conv", qkv_heads=(H, hd))
    o_d0, o_d1, s_in0, s_in1, tm_d0, tm_d1 = _delta_fwd(qkvn, gb3, H, hd, ncc)
    o2 = (o_d0, o_d1)
    y_dn = _dn_out(o2, p_z, 0, dn_onorm, H, hd, nct, tm)

    tcw = _tile(W, 1024)
    xc = _dwconv(p_xl, 0, W, lru_conv_f, lru_conv_b, conv_offs, nct, tm, tcw, "lru_conv")
    tmg = max(t_ for t_ in range(SUBLANES, min(T, 768) + 1, SUBLANES) if T % t_ == 0)
    a2, inp2 = _lru_gates(xc, wbd, bias4, lru_lam_f, tmg, bw)
    hd0 = _lru_scan(a2, inp2, 0, nct, tm, tcw)
    hd1 = _lru_scan(a2, inp2, 1, nct, tm, tcw)
    y_lru = _lru_out(hd0, hd1, p_yl, 0, nct, tm, tcw)

    def gathered(started, after, name):
        xs_, lands_ = _split_wait(started, "gather", after, name)
        return [lax.dynamic_update_slice(land, x_[None], (me,) + (0,) * x_.ndim) for land, x_ in zip(lands_, xs_)]

    w_bdn_g, w_blru_g, w_out_g = gathered(ag_mix, y_lru, "ag_mix_wait")
    w_bdn_f, w_blru_f, w_out_f = w_bdn_g.reshape(HD, D), w_blru_g.reshape(W, D), w_out_g.reshape(D, D)
    bd = _mm(y_dn.reshape(RL, HD), w_bdn_f, "nn", "mm_bdn").reshape(B, N, D)
    bl = _mm(y_lru.reshape(RL, W), w_blru_f, "nn", "mm_blru").reshape(B, N, D)
    mixin = _merge(bd, bl, p_mg, b_merge, nct, tm)
    mix = _mm(mixin.reshape(RL, D), w_out_f, "nn", "mm_out").reshape(B, N, D)
    h1 = _resid_norm_fwd(x, mix, g_post_mix, mods, 2, tm, "resid_norm1")
    u2 = _norm_mod_fwd(h1, g_pre_ffn, mods, 3, 4, 0, tm, "norm_mod2")
    w_up_g, w_down_g, ffn_dw_g = gathered(ag_ffn, u2, "ag_ffn_wait")
    w_down_f = w_down_g.reshape(Cf, D)
    ffn_dw_f = _from_col_shards(ffn_dw_g)
    Fp = _mm(u2.reshape(RL, D), w_up_g, "nn", "mm_up").reshape(B, N, 2 * Cf)
    tcf = LANES
    f, f_gel, f_dgel = _ffn_act(Fp, ffn_dw_f, ffn_dw_b, tcf)
    dproj = _mm(f.reshape(RL, Cf), w_down_f, "nn", "mm_down").reshape(B, N, D)

    dd, dh2, acc_f = _final_fwd_bwd(h1, dproj, g_post_ffn, mods, 5, loss_target, tm)
    loss = lax.psum((0.5 / D) * jnp.sum(acc_f[:, 2, :]), MESH_AXES)
    ddf = dd.reshape(RL, D)
    df = _mm(ddf, w_down_f, "nt", "mm_down_dx").reshape(B, N, Cf)
    gw_down = _mm(f.reshape(RL, Cf), ddf, "tn", "mm_down_dw", out_dtype=BF16)
    dF, acc_ffn = _ffn_act_bwd(Fp, ffn_dw_f, f_gel, f_dgel, df, tcf)
    dFf = dF.reshape(RL, 2 * Cf)
    du2 = _mm(dFf, w_up_g, "nt", "mm_up_dx").reshape(B, N, D)
    gw_up = _mm(u2.reshape(RL, D), dFf, "tn", "mm_up_dw", out_dtype=BF16, out_shards=N_DEV)
    jm = 2 * lax.axis_index("x") + lax.axis_index("y")

    def pair_sums(parts8, tag):
        sib4 = _pair_exchange(parts8, "rs_pair_" + tag)
        return [_pair_sum(p8, s4, f"rs_pair_sum_{tag}{i}") for i, (p8, s4) in enumerate(zip(parts8, sib4))]

    def own_slab(land, x4):
        idx = (jm,) + (0,) * (x4.ndim - 1)
        return lax.dynamic_update_slice(land, lax.dynamic_slice(x4, idx, (1,) + x4.shape[1:]), idx)

    q_early = _split_start(pair_sums([gw_up, gw_down.reshape(N_DEV, Cf // N_DEV, D)], "ffn"), "quad", "rs_quad_ffn_start")
    mods_b = mods + q_early[4][0, 0]
    dh1, acc2x, _ = _norm_mod_bwd(h1, du2, g_pre_ffn, mods_b, 4, 0, tm, dh2, "norm_mod2_bwd")
    dmix, acc_r1 = _resid_norm_bwd(mix, dh1, g_post_mix, mods, 2, tm, "resid_norm1_bwd")
    dmixf = dmix.reshape(RL, D)
    dmixin = _mm(dmixf, w_out_f, "nt", "mm_out_dx").reshape(B, N, D)
    gw_out = _mm(mixin.reshape(RL, D), dmixf, "tn", "mm_out_dw", out_dtype=BF16)
    dbd, dbl, dmg, acc_mg = _merge_bwd(dmixin, bd, bl, p_mg, b_merge, nct, tm)
    dy_dn = _mm(dbd.reshape(RL, D), w_bdn_f, "nt", "mm_bdn_dx").reshape(B, N, HD)
    gw_bdn = _mm(y_dn.reshape(RL, HD), dbd.reshape(RL, D), "tn", "mm_bdn_dw", out_dtype=BF16)
    dy_lru = _mm(dbl.reshape(RL, D), w_blru_f, "nt", "mm_blru_dx").reshape(B, N, W)
    gw_blru = _mm(y_lru.reshape(RL, W), dbl.reshape(RL, D), "tn", "mm_blru_dw", out_dtype=BF16)

    dh, dyl = _lru_out_bwd(hd0, hd1, p_yl, 0, dy_lru, nct, tm, tcw)
    lam0, da0 = _lru_scan_bwd(a2, hd0, dh, 0, nct, tm, tcw)
    lam1, da1 = _lru_scan_bwd(a2, hd1, dh, 1, nct, tm, tcw)
    dxc, dwbd, acc_lru = _lru_gates_bwd(xc, wbd, bias4, lru_lam_f, (lam0, lam1), (da0, da1), tmg, bw)
    dxl = _dwconv(dxc, 0, W, lru_conv_f, None, tuple(-o for o in conv_offs), nct, tm, tcw, "lru_conv_dx", BF16)
    acc_lc = _dwconv_wgrad(dxc, p_xl, 0, W, conv_offs, nct, tm, tcw, "lru_conv_dw")

    do, dz, acc_on = _dn_out_bwd(o2, p_z, 0, dn_onorm, dy_dn, H, hd, nct, tm)
    dqkvn0, dqkvn1, dgb0, dgb1 = _delta_bwd(qkvn, gb3, (s_in0, s_in1), (tm_d0, tm_d1), do, H, hd, ncc)
    dcv = _dn_act_bwd(cv, (dqkvn0, dqkvn1), H, hd, tm)
    dqkv = _dwconv(dcv, 0, 3 * HD, dn_conv_f, None, tuple(-o for o in conv_offs), nct, tm, tcc, "dn_conv_dx", BF16)
    acc_dc = _dwconv_wgrad(dcv, p_qkv, 0, 3 * HD, conv_offs, nct, tm, tcc, "dn_conv_dw")
    dp_ab, acc_ab = _ab_act_bwd(p_ab, gb, (dgb0.reshape(R, LANES), dgb1.reshape(R, LANES)), alog_v, dtb_v, nd, tm)

    groups = [(dqkv.reshape(R, 3 * HD), w_qkv, "qkv"), (dz.reshape(R, HD), w_z, "z"), (dp_ab, w_ab, "ab"),
              (dxl.reshape(R, W), w_xl, "xl"), (dyl.reshape(R, W), w_yl, "yl"), (dmg.reshape(R, 2 * D), w_mg, "mg")]
    du1 = _mm_sum([(dg_, wg_) for dg_, wg_, _ in groups], "mm_in_dx")
    gw_groups = [_mm(dg_, u1f, "tn", "mm_in_dw_" + nm, out_dtype=BF16) for dg_, _, nm in groups]
    gw_groups[2] = gw_groups[2][:2 * nd]
    gw_in = jnp.concatenate(gw_groups, axis=0).reshape(N_DEV, csh, D)
    grad_x, acc1x, acc1c = _norm_mod_bwd(h0, du1.reshape(B, T, D), g_pre_mix, mods, 1, nct, tm, dh1, "norm_mod1_bwd")

    g_lru_conv = jnp.sum(acc_lc[:, :4], 0)
    g_dn_conv = jnp.sum(acc_dc[:, :4], 0)
    g_ffn_dw = jnp.sum(acc_ffn[:, :9], 0).reshape(3, 3, Cf)
    sm_names = ["dn_conv", "lru_conv", "lru_b_rg", "lru_b_ig", "lru_lambda", "ffn_dw"]
    sm_parts = [_col_shards(g_dn_conv, 1), _col_shards(g_lru_conv, 1),
                _col_shards(jnp.stack([acc_lru[0], acc_lru[2]]), 1), _col_shards(jnp.stack([acc_lru[1], acc_lru[3]]), 1),
                _col_shards(acc_lru[4:6], 1), _col_shards(g_ffn_dw, 2)]
    late8 = [gw_in, gw_bdn.reshape(N_DEV, HD // N_DEV, D), gw_blru.reshape(N_DEV, W // N_DEV, D),
             gw_out.reshape(N_DEV, D // N_DEV, D), _pack(sm_parts, lead=(N_DEV,))]
    q_late = _split_start(pair_sums(late8, "mix"), "quad", "rs_quad_mix_start")
    ffn_x, ffn_land = _split_wait(q_early, "quad", q_late[4], "rs_quad_ffn_wait")
    results = {}
    for n, x4, land in zip(["w_up", "w_down"], ffn_x, ffn_land):
        results[n] = list(_adamw(params[n], mom1[n], mom2[n], own_slab(land, x4), "adamw_" + n))

    zeros_d = jnp.zeros((B, D), F32)
    dm_rows = jnp.stack([acc1x[:, 0], acc1x[:, 1], acc_r1[:, 0], acc2x[:, 0], acc2x[:, 1], acc_f[:, 0]], axis=1)
    dm_ctx = jnp.stack([jnp.sum(acc1c[:, 0], 0), jnp.sum(acc1c[:, 1], 0)] + [zeros_d[0]] * 4, axis=0)[None]
    dm_pad = jnp.zeros((SUBLANES, 6 * D), F32).at[:B + 1].set(jnp.concatenate([dm_rows, dm_ctx], 0).reshape(B + 1, 6 * D))
    dm_g = _all_gather([dm_pad], "ag_dmods")[0]
    g_mine = lax.dynamic_slice(dm_g, (0, 0, me * sw), (N_DEV, SUBLANES, sw)).reshape(N_DEV * SUBLANES, sw)
    gw_ada, dcc = _ada_bwd(ccp, w_ada[0], g_mine, c_ctx.reshape(1, D), B)

    per = bw // bdim

    def diag_blocks(dwf):
        g6 = dwf.reshape(W // bw, per, bdim, per, bdim)
        return jnp.einsum("tpiqj,pq->tpij", g6, jnp.eye(per, dtype=F32)).reshape(nblk, bdim, bdim)

    g_rep = dict(
        c_ctx=dcc[0],
        g_pre_mix=jnp.sum(acc1x[:, 2] + acc1c[:, 2], 0)[None], g_post_mix=jnp.sum(acc_r1[:, 1], 0)[None],
        g_pre_ffn=jnp.sum(acc2x[:, 2], 0)[None], g_post_ffn=jnp.sum(acc_f[:, 1], 0)[None],
        b_merge=jnp.sum(acc_mg[:, 0], 0)[None],
        dn_a_log=acc_ab[0, :nd].reshape(1, 2, H), dn_dt_bias=acc_ab[1, :nd].reshape(1, 2, H),
        dn_onorm=jnp.sum(acc_on[:, 0], 0)[None],
        lru_conv_b=jnp.sum(acc_lc[:, 4], 0)[None],
        lru_w_rg=jnp.stack([diag_blocks(dwbd[0]), diag_blocks(dwbd[2])])[None],
        lru_w_ig=jnp.stack([diag_blocks(dwbd[1]), diag_blocks(dwbd[3])])[None],
        ffn_dw_b=jnp.sum(acc_ffn[:, 9], 0)[None])
    mat_names = ["lru_w_rg", "lru_w_ig"]
    vec_names = [n for n in g_rep if n not in mat_names]
    vec_parts, mat_parts = _all_gather([_pack([g_rep[n] for n in vec_names]),
                                        _pack([g_rep[n] for n in mat_names]).astype(BF16)], "ag_rep_grads")
    for grp, parts, nm in ((vec_names, vec_parts, "adamw_rep_vec"), (mat_names, mat_parts, "adamw_rep_mat")):
        out4 = _adamw(_pack([params[n] for n in grp]), _pack([mom1[n] for n in grp]),
                      _pack([mom2[n] for n in grp]), parts, nm)
        for k, buf in enumerate(out4):
            for n, arr in zip(grp, _unpack(buf, [params[n].shape for n in grp])):
                results.setdefault(n, [None] * 4)[k] = arr

    ba_out = _adamw(_pack([b_ada]), _pack([m_b_ada]), _pack([v_b_ada]),
                    _pack([dm_g.reshape(N_DEV * SUBLANES, 6 * D)], lead=(N_DEV * SUBLANES,)), "adamw_b_ada")
    results["b_ada"] = [_unpack(buf, [b_ada.shape])[0] for buf in ba_out]

    wa_out = _adamw(w_ada, m_w_ada, v_w_ada, gw_ada[None], "adamw_w_ada")
    results["w_ada"] = list(wa_out)

    mix_x, mix_land = _split_wait(q_late, "quad", wa_out[0], "rs_quad_mix_wait")
    recv4 = [own_slab(land, x4) for land, x4 in zip(mix_land, mix_x)]
    for n, g4 in zip(["w_in", "w_branch_dn", "w_branch_lru", "w_out"], recv4[:-1]):
        if n == "w_in":
            g4 = jnp.swapaxes(g4, 1, 2)
        results[n] = list(_adamw(params[n], mom1[n], mom2[n], g4, "adamw_" + n))
    sm_out = _adamw(_pack([params[n] for n in sm_names]), _pack([mom1[n] for n in sm_names]),
                    _pack([mom2[n] for n in sm_names]), recv4[-1], "adamw_small")
    for k, buf in enumerate(sm_out):
        for n, arr in zip(sm_names, _unpack(buf, [params[n].shape for n in sm_names])):
            results.setdefault(n, [None] * 4)[k] = arr

    outs = [loss, grad_x]
    for k in range(4):
        outs += [results[n][k] for n in names]
    return tuple(outs)
```

```python
import functools
import math

import jax
import jax.numpy as jnp
from jax import lax
from jax.experimental import pallas as pl
from jax.experimental.pallas import tpu as pltpu

F32 = jnp.float32
BF16 = jnp.bfloat16

EPS = 1e-6
GRID_W = 64
GRID_SHIFT = 6
CHUNK = 64
LRU_C = 8.0
N_DEV = 8
ADAM_LR = 0.001
ADAM_B1 = 0.9
ADAM_B2 = 0.999
ADAM_EPS = 1e-08
ADAM_WD = 0.01
ADAM_STEP = 10

LANES = 128
SUBLANES = 8
VMEM_LIMIT = 48 * 1024 * 1024
MESH_AXES = ("x", "y", "c")
GELU_C = math.sqrt(2.0 / math.pi)


def _cparams(sem=None, vmem=None):
    kw = {}
    if sem is not None:
        kw["dimension_semantics"] = sem
    if vmem is not None:
        kw["vmem_limit_bytes"] = vmem
    return pltpu.CompilerParams(**kw)


def _tile(n, pref):
    if n <= pref:
        return n
    t = (pref // LANES) * LANES
    while n % t:
        t -= LANES
    return t


def _sigmoid(x):
    return 1.0 / (1.0 + jnp.exp(-x))


def _silu(x):
    return x * _sigmoid(x)


def _dsilu(x):
    s = _sigmoid(x)
    return s * (1.0 + x * (1.0 - s))


def _softplus(x):
    return jnp.maximum(x, 0.0) + jnp.log(1.0 + jnp.exp(-jnp.abs(x)))


def _gelu(x):
    return 0.5 * x * (1.0 + jnp.tanh(GELU_C * (x + 0.044715 * x * x * x)))


def _dgelu(x):
    t = jnp.tanh(GELU_C * (x + 0.044715 * x * x * x))
    return 0.5 * (1.0 + t) + 0.5 * x * (1.0 - t * t) * GELU_C * (1.0 + 3.0 * 0.044715 * x * x)


def _dot(a, b, dims="nn"):
    dn = {"nn": (((1,), (0,)), ((), ())),
          "nt": (((1,), (1,)), ((), ())),
          "tn": (((0,), (0,)), ((), ()))}[dims]
    return lax.dot_general(a.astype(BF16), b.astype(BF16), dn, preferred_element_type=F32)


def _split2(x):
    hi = x.astype(BF16)
    lo = (x - hi.astype(F32)).astype(BF16)
    return hi, lo


def _split3(x):
    h1 = x.astype(BF16)
    r1 = x - h1.astype(F32)
    h2 = r1.astype(BF16)
    h3 = (r1 - h2.astype(F32)).astype(BF16)
    return h1, h2, h3


def _dot_hi(a, b):
    ah, al = _split2(a)
    bh, bl = _split2(b)
    return _dot(ah, bh) + (_dot(ah, bl) + _dot(al, bh))


def _dot_mask(m, x, dims="nn"):
    h1, h2, h3 = _split3(x)
    if dims == "xtn":
        return _dot(h1, m, "tn") + (_dot(h2, m, "tn") + _dot(h3, m, "tn"))
    return _dot(m, h1, dims) + (_dot(m, h2, dims) + _dot(m, h3, dims))


def _my_index():
    return 4 * lax.axis_index("x") + 2 * lax.axis_index("y") + lax.axis_index("c")


def _hbm_call(body, xs, out_shapes, n_sems, name):
    any_spec = pl.BlockSpec(memory_space=pl.ANY)
    return pl.pallas_call(
        body, name=name, out_shape=tuple(out_shapes),
        in_specs=[any_spec] * len(xs), out_specs=tuple([any_spec] * len(out_shapes)),
        scratch_shapes=[pltpu.SemaphoreType.DMA((n_sems,)), pltpu.SemaphoreType.DMA((n_sems,)),
                        pltpu.SemaphoreType.DMA((len(xs),))],
    )(*xs)


def _all_gather(xs, name):
    n = len(xs)

    def body(*refs):
        x_refs, out_refs = refs[:n], refs[n:2 * n]
        send_sems, recv_sems, local_sems = refs[2 * n:]
        x_, y_, c_ = lax.axis_index("x"), lax.axis_index("y"), lax.axis_index("c")
        me, sibling = (x_, y_, c_), (x_, y_, 1 - c_)
        chips = [(1 - x_, y_), (x_, 1 - y_), (1 - x_, 1 - y_)]

        def slab(a, px, py, pc):
            return out_refs[a].at[4 * px + 2 * py + pc]

        def copy(a, k, block, to, src=None):
            return pltpu.make_async_remote_copy(
                src_ref=slab(a, *block) if src is None else src, dst_ref=slab(a, *block),
                send_sem=send_sems.at[7 * a + k], recv_sem=recv_sems.at[7 * a + k],
                device_id=to, device_id_type=pl.DeviceIdType.MESH)

        mine = [pltpu.make_async_copy(x_refs[a], slab(a, *me), local_sems.at[a]) for a in range(n)]
        for cp in mine:
            cp.start()
        sent = []
        for j, chip in enumerate(chips):
            sent += [copy(a, 1 + j, me, (*chip, c_), src=x_refs[a]) for a in range(n)]
        sent += [copy(a, 0, me, sibling, src=x_refs[a]) for a in range(n)]
        for cp in sent:
            cp.start()
        for j, chip in enumerate(chips):
            for a in range(n):
                copy(a, 1 + j, (*chip, c_), me).wait_recv()
                fwd = copy(a, 4 + j, (*chip, c_), sibling)
                fwd.start()
                sent.append(fwd)
        for a in range(n):
            copy(a, 0, sibling, me).wait_recv()
        for j, chip in enumerate(chips):
            for a in range(n):
                copy(a, 4 + j, (*chip, 1 - c_), me).wait_recv()
        for cp in sent:
            cp.wait_send()
        for cp in mine:
            cp.wait()

    outs = [jax.ShapeDtypeStruct((N_DEV,) + x.shape, x.dtype) for x in xs]
    return _hbm_call(body, xs, outs, 7 * n, name)


def _pair_exchange(xs, name):
    n = len(xs)

    def body(*refs):
        x_refs, out_refs = refs[:n], refs[n:2 * n]
        send_sems, recv_sems, _ = refs[2 * n:]
        x_, y_, c_ = lax.axis_index("x"), lax.axis_index("y"), lax.axis_index("c")
        copies = []
        for a in range(n):
            for j in range(4):
                copies.append(pltpu.make_async_remote_copy(
                    src_ref=x_refs[a].at[2 * j + (1 - c_)], dst_ref=out_refs[a].at[j],
                    send_sem=send_sems.at[4 * a + j], recv_sem=recv_sems.at[4 * a + j],
                    device_id=(x_, y_, 1 - c_), device_id_type=pl.DeviceIdType.MESH))
        for cp in copies:
            cp.start()
        for cp in copies:
            cp.wait()

    outs = [jax.ShapeDtypeStruct((4,) + x.shape[1:], x.dtype) for x in xs]
    return _hbm_call(body, xs, outs, 4 * n, name)


def _split_views(kind, x_ref, land_ref, k):
    x_, y_, c_ = lax.axis_index("x"), lax.axis_index("y"), lax.axis_index("c")
    if kind == "gather":
        px = 1 - x_ if (k >> 2) & 1 else x_
        py = 1 - y_ if (k >> 1) & 1 else y_
        pc = 1 - c_ if k & 1 else c_
        return x_ref, land_ref.at[4 * x_ + 2 * y_ + c_], land_ref.at[4 * px + 2 * py + pc], (px, py, pc)
    px = 1 - x_ if (k >> 1) & 1 else x_
    py = 1 - y_ if k & 1 else y_
    return x_ref.at[2 * px + py], land_ref.at[2 * x_ + y_], land_ref.at[2 * px + py], (px, py, c_)


def _split_start(xs, kind, name):
    n = len(xs)
    npeer = 7 if kind == "gather" else 3
    lands = [lax.empty(((N_DEV,) + x.shape) if kind == "gather" else x.shape, x.dtype) for x in xs]

    def body(*refs):
        x_refs, land_refs = refs[:n], refs[n:2 * n]
        send_sems, recv_sems, token = refs[2 * n], refs[2 * n + 1], refs[4 * n + 2]
        for k in range(1, npeer + 1):
            for a in range(n):
                src, dst, _, peer = _split_views(kind, x_refs[a], land_refs[a], k)
                pltpu.make_async_remote_copy(
                    src_ref=src, dst_ref=dst, send_sem=send_sems.at[npeer * a + k - 1],
                    recv_sem=recv_sems.at[npeer * a + k - 1],
                    device_id=peer, device_id_type=pl.DeviceIdType.MESH).start()
        token[...] = jnp.zeros_like(token)

    hbm = pl.BlockSpec(memory_space=pltpu.HBM)
    sem = pl.BlockSpec(memory_space=pltpu.SEMAPHORE)
    sems = pltpu.SemaphoreType.DMA((npeer * n,))
    out = pl.pallas_call(
        body, name=name,
        out_shape=(sems, sems, *[pltpu.HBM(a.shape, a.dtype) for a in list(xs) + lands],
                   jax.ShapeDtypeStruct((SUBLANES, LANES), F32)),
        in_specs=[hbm] * (2 * n),
        out_specs=(sem, sem, *[hbm] * (2 * n), pl.BlockSpec(memory_space=pltpu.VMEM)),
        input_output_aliases={i: 2 + i for i in range(2 * n)},
        compiler_params=pltpu.CompilerParams(has_side_effects=pltpu.SideEffectType.DATAFLOW_SIDE_EFFECTING),
    )(*[pltpu.with_memory_space_constraint(a, pltpu.HBM) for a in list(xs) + lands])
    return out[0], out[1], list(out[2:2 + n]), list(out[2 + n:2 + 2 * n]), out[-1]


def _split_wait(started, kind, after, name):
    send_sems_, recv_sems_, x_thru, land_thru, _ = started
    n = len(x_thru)
    npeer = 7 if kind == "gather" else 3

    def body(*refs):
        x_refs, land_refs = refs[:n], refs[n:2 * n]
        send_sems, recv_sems = refs[2 * n], refs[2 * n + 1]
        for k in range(1, npeer + 1):
            for a in range(n):
                src, _, landed, peer = _split_views(kind, x_refs[a], land_refs[a], k)
                cp = pltpu.make_async_remote_copy(
                    src_ref=src, dst_ref=landed, send_sem=send_sems.at[npeer * a + k - 1],
                    recv_sem=recv_sems.at[npeer * a + k - 1],
                    device_id=peer, device_id_type=pl.DeviceIdType.MESH)
                cp.wait_send()
                cp.wait_recv()

    hbm = pl.BlockSpec(memory_space=pltpu.HBM)
    sem = pl.BlockSpec(memory_space=pltpu.SEMAPHORE)
    out = pl.pallas_call(
        body, name=name,
        out_shape=tuple(pltpu.HBM(a.shape, a.dtype) for a in x_thru + land_thru),
        in_specs=[hbm] * (2 * n) + [sem, sem, pl.BlockSpec(memory_space=pl.ANY)],
        out_specs=tuple([hbm] * (2 * n)),
        input_output_aliases={i: i for i in range(2 * n)},
        compiler_params=pltpu.CompilerParams(has_side_effects=pltpu.SideEffectType.DATAFLOW_SIDE_EFFECTING),
    )(*x_thru, *land_thru, send_sems_, recv_sems_, after)
    return list(out[:n]), list(out[n:])


def _pair_sum(x8, r4, name):
    _, r, c = x8.shape
    tr = _row_tile(r, c * 12)

    def body(core_ref, x_ref, r_ref, o_ref):
        o_ref[...] = (x_ref[...].astype(F32) + r_ref[...].astype(F32)).astype(o_ref.dtype)

    core = lax.axis_index("c").astype(jnp.int32).reshape(1)
    return pl.pallas_call(
        body, name=name,
        grid_spec=pltpu.PrefetchScalarGridSpec(
            num_scalar_prefetch=1, grid=(4, r // tr),
            in_specs=[pl.BlockSpec((None, tr, c), lambda j, i, core_ref: (2 * j + core_ref[0], i, 0)),
                      pl.BlockSpec((None, tr, c), lambda j, i, core_ref: (j, i, 0))],
            out_specs=pl.BlockSpec((None, tr, c), lambda j, i, core_ref: (j, i, 0))),
        out_shape=jax.ShapeDtypeStruct((4, r, c), x8.dtype),
        compiler_params=_cparams(("parallel", "parallel"), VMEM_LIMIT))(core, x8, r4)


def _row_tile(r, bytes_per_row, budget=4 * 1024 * 1024):
    best = None
    for t in range(16, r + 1, 16):
        if r % t == 0 and t * bytes_per_row <= budget:
            best = t
    return best if best is not None else r


def _mm(a, b, dims, name, out_dtype=F32, add=None, out_shards=1, tm=1024, tn=1024, tk=1024):
    S = b.shape[0] if b.ndim == 3 else 1
    bshape = (b.shape[1], S * b.shape[2]) if b.ndim == 3 else b.shape
    if dims == "nn":
        (M, K), (K2, N) = a.shape, bshape
    elif dims == "nt":
        (M, K), (N, K2) = a.shape, bshape
    else:
        (K, M), (K2, N) = a.shape, bshape
    assert K == K2, (a.shape, b.shape, dims)
    cs = bshape[1] // S
    tm, tk = _tile(M, tm), _tile(K, min(tk, cs) if (S > 1 and dims == "nt") else tk)
    tn = _tile(N, min(tn, cs) if (S > 1 and dims != "nt") else min(tn, N // out_shards))
    assert cs % (tk if dims == "nt" else tn) == 0 and (N // out_shards) % tn == 0
    nk = K // tk

    def body(*refs):
        if add is None:
            a_ref, b_ref, o_ref, acc = refs
        else:
            a_ref, b_ref, c_ref, o_ref, acc = refs
        k = pl.program_id(2)

        @pl.when(k == 0)
        def _():
            acc[...] = jnp.zeros_like(acc)

        acc[...] += _dot(a_ref[...], b_ref[...], dims)

        @pl.when(k == nk - 1)
        def _():
            r = acc[...]
            if add is not None:
                r = r + c_ref[...]
            o_ref[...] = r.astype(out_dtype)

    a_spec = (pl.BlockSpec((tk, tm), lambda i, j, k: (k, i)) if dims == "tn"
              else pl.BlockSpec((tm, tk), lambda i, j, k: (i, k)))
    if S == 1:
        b_spec = (pl.BlockSpec((tn, tk), lambda i, j, k: (j, k)) if dims == "nt"
                  else pl.BlockSpec((tk, tn), lambda i, j, k: (k, j)))
    elif dims == "nt":
        per = cs // tk
        b_spec = pl.BlockSpec((None, tn, tk), lambda i, j, k: (k // per, j, k % per))
    else:
        per = cs // tn
        b_spec = pl.BlockSpec((None, tk, tn), lambda i, j, k: (j // per, k, j % per))
    in_specs = [a_spec, b_spec]
    args = [a, b]
    if add is not None:
        in_specs.append(pl.BlockSpec((tm, tn), lambda i, j, k: (i, j)))
        args.append(add)
    if out_shards == 1:
        out_spec, out_shape = pl.BlockSpec((tm, tn), lambda i, j, k: (i, j)), (M, N)
    else:
        oper = N // out_shards // tn
        out_spec = pl.BlockSpec((None, tm, tn), lambda i, j, k: (j // oper, i, j % oper))
        out_shape = (out_shards, M, N // out_shards)
    return pl.pallas_call(
        body, name=name, grid=(M // tm, N // tn, nk),
        in_specs=in_specs, out_specs=out_spec,
        out_shape=jax.ShapeDtypeStruct(out_shape, out_dtype),
        scratch_shapes=[pltpu.VMEM((tm, tn), F32)],
        compiler_params=_cparams(("parallel", "parallel", "arbitrary"), VMEM_LIMIT),
    )(*args)


VMEM_LIMIT_SUM = 56 * 1024 * 1024


def _mm_sum(pairs, name, out_dtype=F32, tm=768, tn=1024, tk=1024):
    M, N = pairs[0][0].shape[0], pairs[0][1].shape[1]
    tm, tn = _tile(M, tm), _tile(N, tn)
    tks = [_tile(a.shape[1], tk) for a, _ in pairs]
    nks = [a.shape[1] // t for (a, _), t in zip(pairs, tks)]
    starts = [sum(nks[:g]) for g in range(len(pairs))]
    nk = sum(nks)
    G = len(pairs)

    def body(*refs):
        o_ref, acc = refs[2 * G], refs[2 * G + 1]
        k = pl.program_id(2)

        @pl.when(k == 0)
        def _():
            acc[...] = jnp.zeros_like(acc)

        for g in range(G):
            @pl.when(jnp.logical_and(k >= starts[g], k < starts[g] + nks[g]))
            def _(g=g):
                acc[...] += _dot(refs[2 * g][...], refs[2 * g + 1][...])

        @pl.when(k == nk - 1)
        def _():
            o_ref[...] = acc[...].astype(out_dtype)

    in_specs, args = [], []
    for g, (a, b) in enumerate(pairs):
        kk = lambda k, g=g: jnp.clip(k - starts[g], 0, nks[g] - 1)
        in_specs += [pl.BlockSpec((tm, tks[g]), lambda i, j, k, kk=kk: (i, kk(k))),
                     pl.BlockSpec((tks[g], tn), lambda i, j, k, kk=kk: (kk(k), j))]
        args += [a, b]
    return pl.pallas_call(
        body, name=name, grid=(M // tm, N // tn, nk),
        in_specs=in_specs, out_specs=pl.BlockSpec((tm, tn), lambda i, j, k: (i, j)),
        out_shape=jax.ShapeDtypeStruct((M, N), out_dtype),
        scratch_shapes=[pltpu.VMEM((tm, tn), F32)],
        compiler_params=_cparams(("parallel", "parallel", "arbitrary"), VMEM_LIMIT_SUM),
    )(*args)


def _ada_fwd(ccp, w, b):
    def body(c_ref, w_ref, b_ref, o_ref):
        o_ref[...] = _dot(_silu(c_ref[...]), w_ref[...]) + b_ref[...]

    return pl.pallas_call(
        body, name="ada_fwd", out_shape=jax.ShapeDtypeStruct((ccp.shape[0], w.shape[1]), F32),
        compiler_params=_cparams(None, VMEM_LIMIT))(ccp, w, b)


def _ada_bwd(ccp, w, g, cctx, n_loc):
    def body(c_ref, w_ref, g_ref, cc_ref, gw_ref, dc_ref):
        gw_ref[...] = _dot(_silu(c_ref[...]), g_ref[...], "tn")
        dcc = _dot(g_ref[...], w_ref[...], "nt")
        row = lax.broadcasted_iota(jnp.int32, dcc.shape, 0)
        part = jnp.sum(jnp.where(row % SUBLANES == n_loc, dcc, 0.0), axis=0, keepdims=True)
        dc_ref[...] = jnp.broadcast_to(part * _dsilu(cc_ref[...]), dc_ref.shape)

    D = ccp.shape[1]
    return pl.pallas_call(
        body, name="ada_bwd",
        out_shape=(jax.ShapeDtypeStruct(w.shape, F32), jax.ShapeDtypeStruct((SUBLANES, D), F32)),
        compiler_params=_cparams(None, VMEM_LIMIT))(ccp, w, g, cctx)


def _norm_mod_fwd(h, gain, mods, i_shift, i_scale, nct, tm, name):
    B, T, D = h.shape

    def body(h_ref, g_ref, m_ref, u_ref):
        x = h_ref[0]
        r = lax.rsqrt(jnp.mean(x * x, axis=-1, keepdims=True) + EPS)
        n = x * r * g_ref[...]
        u_ref[0] = (n * (1.0 + m_ref[0, i_scale:i_scale + 1, :]) + m_ref[0, i_shift:i_shift + 1, :]).astype(BF16)

    return pl.pallas_call(
        body, name=name, grid=(B, T // tm),
        in_specs=[pl.BlockSpec((1, tm, D), lambda b, t: (b, t, 0)),
                  pl.BlockSpec((1, D), lambda b, t: (0, 0)),
                  pl.BlockSpec((1, SUBLANES, D), lambda b, t: (jnp.where(t < nct, B, b), 0, 0))],
        out_specs=pl.BlockSpec((1, tm, D), lambda b, t: (b, t, 0)),
        out_shape=jax.ShapeDtypeStruct((B, T, D), BF16),
        compiler_params=_cparams(("parallel", "parallel"), VMEM_LIMIT),
    )(h, gain, mods)


def _norm_mod_bwd(h, du, gain, mods, i_scale, nct, tm, res, name):
    B, T, D = h.shape
    nt = T // tm

    def body(h_ref, du_ref, g_ref, m_ref, res_ref, dx_ref, ax_ref, ac_ref):
        t = pl.program_id(1)
        x = h_ref[0]
        r = lax.rsqrt(jnp.mean(x * x, axis=-1, keepdims=True) + EPS)
        nh = x * r
        g = g_ref[...]
        du_ = du_ref[0]
        dn = du_ * (1.0 + m_ref[0, i_scale:i_scale + 1, :])
        dnh = dn * g
        dx = r * (dnh - nh * jnp.mean(dnh * nh, axis=-1, keepdims=True))
        sums = (jnp.sum(du_, axis=0, keepdims=True), jnp.sum(du_ * nh * g, axis=0, keepdims=True),
                jnp.sum(dn * nh, axis=0, keepdims=True))

        @pl.when(t == 0)
        def _():
            ax_ref[...] = jnp.zeros_like(ax_ref)
            ac_ref[...] = jnp.zeros_like(ac_ref)

        def accumulate(ref):
            for i, s in enumerate(sums):
                ref[0, i:i + 1, :] += s

        @pl.when(t < nct)
        def _():
            accumulate(ac_ref)

        @pl.when(t >= nct)
        def _():
            accumulate(ax_ref)
            dx_ref[0] = dx + res_ref[0]

    lat = lambda b, t: (b, jnp.maximum(t - nct, 0), 0)
    acc = pl.BlockSpec((1, SUBLANES, D), lambda b, t: (b, 0, 0))
    return pl.pallas_call(
        body, name=name, grid=(B, nt),
        in_specs=[pl.BlockSpec((1, tm, D), lambda b, t: (b, t, 0)),
                  pl.BlockSpec((1, tm, D), lambda b, t: (b, t, 0)),
                  pl.BlockSpec((1, D), lambda b, t: (0, 0)),
                  pl.BlockSpec((1, SUBLANES, D), lambda b, t: (jnp.where(t < nct, B, b), 0, 0)),
                  pl.BlockSpec((1, tm, D), lat)],
        out_specs=(pl.BlockSpec((1, tm, D), lat), acc, acc),
        out_shape=(jax.ShapeDtypeStruct(res.shape, F32),
                   jax.ShapeDtypeStruct((B, SUBLANES, D), F32), jax.ShapeDtypeStruct((B, SUBLANES, D), F32)),
        compiler_params=_cparams(("parallel", "arbitrary"), VMEM_LIMIT),
    )(h, du, gain, mods, res)


def _resid_norm_fwd(x, y, gain, mods, i_gate, tm, name):
    B, N, D = x.shape

    def body(x_ref, y_ref, g_ref, m_ref, o_ref):
        y_ = y_ref[0]
        r = lax.rsqrt(jnp.mean(y_ * y_, axis=-1, keepdims=True) + EPS)
        o_ref[0] = x_ref[0] + y_ * r * g_ref[...] * m_ref[0, i_gate:i_gate + 1, :]

    row = pl.BlockSpec((1, tm, D), lambda b, t: (b, t, 0))
    return pl.pallas_call(
        body, name=name, grid=(B, N // tm),
        in_specs=[row, row, pl.BlockSpec((1, D), lambda b, t: (0, 0)),
                  pl.BlockSpec((1, SUBLANES, D), lambda b, t: (b, 0, 0))],
        out_specs=row, out_shape=jax.ShapeDtypeStruct((B, N, D), F32),
        compiler_params=_cparams(("parallel", "parallel"), VMEM_LIMIT),
    )(x, y, gain, mods)


def _resid_norm_bwd_math(y_, dh, g, gate):
    r = lax.rsqrt(jnp.mean(y_ * y_, axis=-1, keepdims=True) + EPS)
    nh = y_ * r
    dgate = jnp.sum(dh * nh * g, axis=0, keepdims=True)
    dn = dh * gate
    dgain = jnp.sum(dn * nh, axis=0, keepdims=True)
    dnh = dn * g
    dy = r * (dnh - nh * jnp.mean(dnh * nh, axis=-1, keepdims=True))
    return dy, dgate, dgain


def _resid_norm_bwd(y, dh, gain, mods, i_gate, tm, name):
    B, N, D = y.shape

    def body(y_ref, dh_ref, g_ref, m_ref, dy_ref, acc_ref):
        t = pl.program_id(1)
        dy, dgate, dgain = _resid_norm_bwd_math(y_ref[0], dh_ref[0], g_ref[...], m_ref[0, i_gate:i_gate + 1, :])
        dy_ref[0] = dy.astype(BF16)

        @pl.when(t == 0)
        def _():
            acc_ref[...] = jnp.zeros_like(acc_ref)

        acc_ref[0, 0:1, :] += dgate
        acc_ref[0, 1:2, :] += dgain

    row = pl.BlockSpec((1, tm, D), lambda b, t: (b, t, 0))
    return pl.pallas_call(
        body, name=name, grid=(B, N // tm),
        in_specs=[row, row, pl.BlockSpec((1, D), lambda b, t: (0, 0)),
                  pl.BlockSpec((1, SUBLANES, D), lambda b, t: (b, 0, 0))],
        out_specs=(row, pl.BlockSpec((1, SUBLANES, D), lambda b, t: (b, 0, 0))),
        out_shape=(jax.ShapeDtypeStruct((B, N, D), BF16), jax.ShapeDtypeStruct((B, SUBLANES, D), F32)),
        compiler_params=_cparams(("parallel", "arbitrary"), VMEM_LIMIT),
    )(y, dh, gain, mods)


def _final_fwd_bwd(h1, d, gain, mods, i_gate, tgt, tm):
    B, N, D = h1.shape

    def body(h_ref, d_ref, g_ref, m_ref, t_ref, dd_ref, dh_ref, acc_ref):
        t = pl.program_id(1)
        d_ = d_ref[0]
        g = g_ref[...]
        gate = m_ref[0, i_gate:i_gate + 1, :]
        r = lax.rsqrt(jnp.mean(d_ * d_, axis=-1, keepdims=True) + EPS)
        e = h_ref[0] + d_ * r * g * gate - t_ref[0]
        dh = e * (1.0 / D)
        dh_ref[0] = dh
        dy, dgate, dgain = _resid_norm_bwd_math(d_, dh, g, gate)
        dd_ref[0] = dy.astype(BF16)

        @pl.when(t == 0)
        def _():
            acc_ref[...] = jnp.zeros_like(acc_ref)

        acc_ref[0, 0:1, :] += dgate
        acc_ref[0, 1:2, :] += dgain
        acc_ref[0, 2:3, :] += jnp.sum(e * e, axis=0, keepdims=True)

    row = pl.BlockSpec((1, tm, D), lambda b, t: (b, t, 0))
    return pl.pallas_call(
        body, name="final_fwd_bwd", grid=(B, N // tm),
        in_specs=[row, row, pl.BlockSpec((1, D), lambda b, t: (0, 0)),
                  pl.BlockSpec((1, SUBLANES, D), lambda b, t: (b, 0, 0)), row],
        out_specs=(row, row, pl.BlockSpec((1, SUBLANES, D), lambda b, t: (b, 0, 0))),
        out_shape=(jax.ShapeDtypeStruct((B, N, D), BF16), jax.ShapeDtypeStruct((B, N, D), F32),
                   jax.ShapeDtypeStruct((B, SUBLANES, D), F32)),
        compiler_params=_cparams(("parallel", "arbitrary"), VMEM_LIMIT),
    )(h1, d, gain, mods, tgt)


def _shifted(x, prev, nxt, off, row, tm):
    if off == 0:
        return x
    if off < 0:
        y = pltpu.roll(x, -off, 0)
        for r in range(-off):
            y = jnp.where(row == r, prev[SUBLANES + r + off:SUBLANES + r + off + 1, :], y)
        return y
    y = pltpu.roll(x, tm - off, 0)
    for r in range(off):
        y = jnp.where(row == tm - off + r, nxt[r:r + 1, :], y)
    return y


def _halo_specs(T, tm, tc, cb0, order):
    r8, n8 = tm // SUBLANES, T // SUBLANES
    if order == "btj":
        prev = lambda b, t, j: (b, jnp.maximum(t * r8 - 1, 0), cb0 + j)
        nxt = lambda b, t, j: (b, jnp.minimum((t + 1) * r8, n8 - 1), cb0 + j)
    else:
        prev = lambda b, j, t: (b, jnp.maximum(t * r8 - 1, 0), cb0 + j)
        nxt = lambda b, j, t: (b, jnp.minimum((t + 1) * r8, n8 - 1), cb0 + j)
    return pl.BlockSpec((1, SUBLANES, tc), prev), pl.BlockSpec((1, SUBLANES, tc), nxt)


def _seg_halos(p_ref, n_ref, t, nct, nt):
    seg_start = jnp.logical_or(t == 0, t == nct)
    seg_end = jnp.logical_or(t == nct - 1, t == nt - 1)
    prev = jnp.where(seg_start, 0.0, p_ref[0])
    nxt = jnp.where(seg_end, 0.0, n_ref[0])
    return prev, nxt


def _dwconv(x, col0, C, w, bias, offs, nct, tm, tc, name, out_dtype=F32, qkv_heads=None):
    B, T, _ = x.shape
    nt = T // tm
    cb0 = col0 // tc
    if qkv_heads is not None:
        assert tc == qkv_heads[0] * qkv_heads[1] and C == 3 * tc

    def body(*refs):
        refs = list(refs)
        a_ref = refs.pop() if qkv_heads is not None else None
        if bias is None:
            x_ref, p_ref, n_ref, w_ref, o_ref = refs
        else:
            x_ref, p_ref, n_ref, w_ref, b_ref, o_ref = refs
        t = pl.program_id(1)
        prev, nxt = _seg_halos(p_ref, n_ref, t, nct, nt)
        x_ = x_ref[0]
        row = lax.broadcasted_iota(jnp.int32, x_.shape, 0)
        acc = None
        for j, off in enumerate(offs):
            term = _shifted(x_, prev, nxt, off, row, tm) * w_ref[j:j + 1, :]
            acc = term if acc is None else acc + term
        if bias is not None:
            acc = acc + b_ref[...]
        o_ref[0] = acc.astype(out_dtype)
        if qkv_heads is not None:
            H, hd = qkv_heads
            part = pl.program_id(2)
            for h in range(H):
                sl = slice(h * hd, (h + 1) * hd)
                s = _silu(acc[:, sl])
                rs = lax.rsqrt(jnp.sum(s * s, axis=-1, keepdims=True) + EPS)
                scale = jnp.where(part == 0, rs * (float(hd) ** -0.5), jnp.where(part == 1, rs, 1.0))
                a_ref[0, :, sl] = (s * scale).astype(a_ref.dtype)

    pspec, nspec = _halo_specs(T, tm, tc, cb0, "btj")
    in_specs = [pl.BlockSpec((1, tm, tc), lambda b, t, j: (b, t, cb0 + j)), pspec, nspec,
                pl.BlockSpec((w.shape[0], tc), lambda b, t, j: (0, j))]
    args = [x, x, x, w]
    if bias is not None:
        in_specs.append(pl.BlockSpec((1, tc), lambda b, t, j: (0, j)))
        args.append(bias)
    tile = pl.BlockSpec((1, tm, tc), lambda b, t, j: (b, t, j))
    out_specs, out_shape = tile, jax.ShapeDtypeStruct((B, T, C), out_dtype)
    if qkv_heads is not None:
        out_specs, out_shape = (tile, tile), (out_shape, jax.ShapeDtypeStruct((B, T, C), BF16))
    return pl.pallas_call(
        body, name=name, grid=(B, nt, C // tc),
        in_specs=in_specs, out_specs=out_specs, out_shape=out_shape,
        compiler_params=_cparams(("parallel", "parallel", "parallel"), VMEM_LIMIT),
    )(*args)


def _dwconv_wgrad(dy, x, col0, C, offs, nct, tm, tc, name):
    B, T, _ = x.shape
    nt = T // tm
    cb0 = col0 // tc
    K = len(offs)

    def body(dy_ref, x_ref, p_ref, n_ref, acc_ref):
        t = pl.program_id(2)
        prev, nxt = _seg_halos(p_ref, n_ref, t, nct, nt)
        x_ = x_ref[0]
        dy_ = dy_ref[0]
        row = lax.broadcasted_iota(jnp.int32, x_.shape, 0)

        @pl.when(t == 0)
        def _():
            acc_ref[...] = jnp.zeros_like(acc_ref)

        for j, off in enumerate(offs):
            acc_ref[0, j:j + 1, :] += jnp.sum(dy_ * _shifted(x_, prev, nxt, off, row, tm), axis=0, keepdims=True)
        acc_ref[0, K:K + 1, :] += jnp.sum(dy_, axis=0, keepdims=True)

    pspec, nspec = _halo_specs(T, tm, tc, cb0, "bjt")
    return pl.pallas_call(
        body, name=name, grid=(B, C // tc, nt),
        in_specs=[pl.BlockSpec((1, tm, tc), lambda b, j, t: (b, t, j)),
                  pl.BlockSpec((1, tm, tc), lambda b, j, t: (b, t, cb0 + j)), pspec, nspec],
        out_specs=pl.BlockSpec((1, SUBLANES, tc), lambda b, j, t: (b, 0, j)),
        out_shape=jax.ShapeDtypeStruct((B, SUBLANES, C), F32),
        compiler_params=_cparams(("parallel", "parallel", "arbitrary"), VMEM_LIMIT),
    )(dy, x, x, x)


def _ab_act(pab, alog, dtb, nd, tm):
    R = pab.shape[0]

    def body(p_ref, al_ref, dt_ref, o_ref):
        p = p_ref[...]
        lane = lax.broadcasted_iota(jnp.int32, p.shape, 1)
        g = -jnp.exp(al_ref[...]) * _softplus(p + dt_ref[...])
        o_ref[...] = jnp.where(lane < nd, g, jnp.where(lane < 2 * nd, _sigmoid(p), 0.0))

    row = pl.BlockSpec((tm, LANES), lambda i: (i, 0))
    vec = pl.BlockSpec((1, LANES), lambda i: (0, 0))
    return pl.pallas_call(
        body, name="ab_act", grid=(R // tm,), in_specs=[row, vec, vec], out_specs=row,
        out_shape=jax.ShapeDtypeStruct((R, LANES), F32),
        compiler_params=_cparams(("parallel",), VMEM_LIMIT))(pab, alog, dtb)


def _ab_act_bwd(pab, gb, dgb, alog, dtb, nd, tm):
    R = pab.shape[0]

    def body(p_ref, gb_ref, d0_ref, d1_ref, al_ref, dt_ref, o_ref, acc_ref):
        i = pl.program_id(0)
        p = p_ref[...]
        gb_ = gb_ref[...]
        d_ = d0_ref[...] + d1_ref[...]
        lane = lax.broadcasted_iota(jnp.int32, p.shape, 1)
        is_g = lane < nd
        is_b = jnp.logical_and(lane >= nd, lane < 2 * nd)
        da = jnp.where(is_g, d_ * (-jnp.exp(al_ref[...])) * _sigmoid(p + dt_ref[...]), 0.0)
        db = jnp.where(is_b, d_ * gb_ * (1.0 - gb_), 0.0)
        o_ref[...] = (da + db).astype(BF16)

        @pl.when(i == 0)
        def _():
            acc_ref[...] = jnp.zeros_like(acc_ref)

        acc_ref[0:1, :] += jnp.sum(jnp.where(is_g, d_ * gb_, 0.0), axis=0, keepdims=True)
        acc_ref[1:2, :] += jnp.sum(da, axis=0, keepdims=True)

    row = pl.BlockSpec((tm, LANES), lambda i: (i, 0))
    vec = pl.BlockSpec((1, LANES), lambda i: (0, 0))
    return pl.pallas_call(
        body, name="ab_act_bwd", grid=(R // tm,),
        in_specs=[row, row, row, row, vec, vec],
        out_specs=(row, pl.BlockSpec((SUBLANES, LANES), lambda i: (0, 0))),
        out_shape=(jax.ShapeDtypeStruct((R, LANES), BF16), jax.ShapeDtypeStruct((SUBLANES, LANES), F32)),
        compiler_params=_cparams(("arbitrary",), VMEM_LIMIT))(pab, gb, dgb[0], dgb[1], alog, dtb)


def _dn_act_bwd(cv, dqkv, H, hd, tm):
    B, T, C3 = cv.shape
    qscale = float(hd) ** -0.5

    def body(c_ref, d0_ref, d1_ref, o_ref):
        part = pl.program_id(0)
        for h in range(H):
            sl = slice(h * hd, (h + 1) * hd)
            c = c_ref[0, :, sl]
            s = _silu(c)
            dout = d0_ref[0, :, sl] + d1_ref[0, :, sl]
            rs = lax.rsqrt(jnp.sum(s * s, axis=-1, keepdims=True) + EPS)
            sh = s * rs
            dnorm = rs * (dout - sh * jnp.sum(dout * sh, axis=-1, keepdims=True))
            ds = jnp.where(part == 0, dnorm * qscale, jnp.where(part == 1, dnorm, dout))
            o_ref[0, :, sl] = ds * _dsilu(c)

    spec = pl.BlockSpec((1, tm, H * hd), lambda p, b, t: (b, t, p))
    return pl.pallas_call(
        body, name="dn_act_bwd", grid=(3, B, T // tm), in_specs=[spec, spec, spec], out_specs=spec,
        out_shape=jax.ShapeDtypeStruct((B, T, C3), F32),
        compiler_params=_cparams(("parallel",) * 3, VMEM_LIMIT))(cv, dqkv[0], dqkv[1])


def _chunk_of(d, s, ncc, nch):
    if d == 0:
        return s
    return jnp.where(s < ncc, ncc - 1 - s, nch - 1 - (s - ncc))


def _delta_masks(d):
    row = lax.broadcasted_iota(jnp.int32, (CHUNK, CHUNK), 0)
    col = lax.broadcasted_iota(jnp.int32, (CHUNK, CHUNK), 1)
    sign = 1 - 2 * d
    diff = (row - col) * sign
    return diff >= 0, diff > 0, diff <= 0


def _each(f, *lists):
    return [f(*a) for a in zip(*lists)]


def _unit_tri_inverse(As):
    C = As[0].shape[0]
    row = lax.broadcasted_iota(jnp.int32, (C, C), 0)
    col = lax.broadcasted_iota(jnp.int32, (C, C), 1)
    eye = jnp.where(row == col, 1.0, 0.0).astype(F32)
    same = lambda shift: jnp.right_shift(row, shift) == jnp.right_shift(col, shift)
    in8 = same(3)
    xs = [-jnp.where(in8, a, 0.0) for a in As]
    ts = [eye + x for x in xs]
    ps = _each(lambda x: _dot(x, x), xs)
    tp = _each(lambda t, p: _dot(_rows(t, p), p), ts, ps)
    ts = _each(lambda t, m: t + m[:C], ts, tp)
    ts = _each(lambda t, m: t + _dot(t, m[C:]), ts, tp)
    shift = 3
    while (1 << shift) < C:
        off = jnp.logical_and(same(shift + 1), jnp.right_shift(row, shift) != jnp.right_shift(col, shift))
        los = [jnp.where(off, a, 0.0) for a in As]
        ts = _each(lambda t, lo: t - _dot(t, _dot(lo, t)), ts, los)
        shift += 1
    def residual(a, t):
        (ah, al), (th, tl) = _split2(eye + a), _split2(t)
        m = _dot(_rows(ah, al), th)
        return eye - (m[:C] + (m[C:] + _dot(ah, tl)))

    rs = _each(residual, As, ts)
    return _each(lambda t, r: t + _dot(t, r), ts, rs)


def _rows(*xs):
    return jnp.concatenate(xs, axis=0)


def _cols(*xs):
    return jnp.concatenate(xs, axis=1)


def _delta_chunk_fwd(q, k, v, g_i, g_j, beta_i, gl, S, incl, strict, Tm=None):
    D_ = _each(lambda gi, gj, m: jnp.where(m, jnp.exp(jnp.where(m, gi - gj, 0.0)), 0.0), g_i, g_j, incl)
    C, dk = k[0].shape
    kb = _each(lambda k_, b: k_ * b, k, beta_i)
    kq = _each(lambda kb_, q_, k_: _dot(_rows(kb_, q_), k_, "nt"), kb, q, k)
    A = _each(lambda m_, d, m: jnp.where(m, m_[:C] * d, 0.0), kq, D_, strict)
    P = _each(lambda m_, d, m: jnp.where(m, m_[C:] * d, 0.0), kq, D_, incl)
    if Tm is None:
        Tm = _unit_tri_inverse(A)
    gam = _each(jnp.exp, g_i)
    wu = _each(lambda t, kb_, g, v_, b: _dot(t, _cols(kb_ * g, v_ * b)), Tm, kb, gam, v, beta_i)
    w = [m_[:, :dk] for m_ in wu]
    u = [m_[:, dk:] for m_ in wu]
    qg = _each(lambda q_, g: q_ * g, q, gam)
    ws = _each(lambda w_, qg_, s: _dot(_rows(w_, qg_), s), w, qg, S)
    vn = _each(lambda u_, m_: u_ - m_[:C], u, ws)
    o = _each(lambda m_, p, vn_: m_[C:] + _dot(p, vn_), ws, P, vn)
    e_i = _each(lambda gl_, gi: jnp.exp(gl_ - gi), gl, g_i)
    kd = _each(lambda k_, e: k_ * e, k, e_i)
    Gam = _each(jnp.exp, gl)
    S_new = _each(lambda s, g, kd_, vn_: s * g + _dot(kd_, vn_, "tn"), S, Gam, kd, vn)
    return dict(D=D_, kb=kb, A=A, Tm=Tm, gam=gam, w=w, u=u, vn=vn, P=P, qg=qg, o=o, e=e_i, kd=kd, Gam=Gam, S_new=S_new)


def _delta_gb(gb_ref, d, incl, incl_t, H):
    gb = gb_ref[0]
    g = gb[:, d * H:(d + 1) * H]
    beta = gb[:, (2 + d) * H:(3 + d) * H]
    gc_col = _dot_mask(incl.astype(BF16), g)
    gc_row = _dot_mask(incl_t.astype(BF16), g, "xtn")
    gl = jnp.sum(g, axis=0, keepdims=True)
    return beta, gc_col, gc_row, gl


def _delta_fwd(qkvn, gb, H, hd, ncc):
    B, T, _ = qkvn.shape
    nch = T // CHUNK
    HD = H * hd
    pairs = [(d, h) for d in range(2) for h in range(H)]

    def body(q0, k0, v0, gb0, q1, k1, v1, gb1, o0, o1, sin0, sin1, tm0, tm1, S_ref):
        s = pl.program_id(1)
        q_refs, k_refs, v_refs, gb_refs = (q0, q1), (k0, k1), (v0, v1), (gb0, gb1)
        o_refs, sin_refs, tm_refs = (o0, o1), (sin0, sin1), (tm0, tm1)

        @pl.when(s == 0)
        def _():
            S_ref[...] = jnp.zeros_like(S_ref)

        masks = [_delta_masks(d) for d in range(2)]
        gbs = [_delta_gb(gb_refs[d], d, masks[d][0], masks[d][2], H) for d in range(2)]
        head = lambda ref, h: ref[0, :, h * hd:(h + 1) * hd]
        S = [S_ref[d, h] for d, h in pairs]
        r = _delta_chunk_fwd([head(q_refs[d], h) for d, h in pairs], [head(k_refs[d], h) for d, h in pairs],
                             [head(v_refs[d], h) for d, h in pairs],
                             [gbs[d][1][:, h:h + 1] for d, h in pairs], [gbs[d][2][h:h + 1, :] for d, h in pairs],
                             [gbs[d][0][:, h:h + 1] for d, h in pairs], [gbs[d][3][:, h:h + 1] for d, h in pairs],
                             S, [masks[d][0] for d, h in pairs], [masks[d][1] for d, h in pairs])
        for i, (d, h) in enumerate(pairs):
            sin_refs[d][0, 0, h] = S[i]
            tm_refs[d][0, 0, h] = r["Tm"][i]
            S_ref[d, h] = r["S_new"][i]
            o_refs[d][0, :, h * hd:(h + 1) * hd] = r["o"][i]

    def dir_specs(d):
        cidx = lambda b, s: _chunk_of(d, s, ncc, nch)
        ins = [pl.BlockSpec((1, CHUNK, HD), lambda b, s, p=p: (b, cidx(b, s), p)) for p in range(3)]
        ins.append(pl.BlockSpec((1, CHUNK, LANES), lambda b, s: (b, cidx(b, s), 0)))
        return (ins, pl.BlockSpec((1, CHUNK, HD), lambda b, s: (b, cidx(b, s), 0)),
                pl.BlockSpec((1, 1, H, hd, hd), lambda b, s: (b, cidx(b, s), 0, 0, 0)),
                pl.BlockSpec((1, 1, H, CHUNK, CHUNK), lambda b, s: (b, cidx(b, s), 0, 0, 0)))

    (in0, o_s0, sin_s0, tm_s0), (in1, o_s1, sin_s1, tm_s1) = dir_specs(0), dir_specs(1)
    o_shape = jax.ShapeDtypeStruct((B, T, HD), F32)
    sin_shape = jax.ShapeDtypeStruct((B, nch, H, hd, hd), F32)
    tm_shape = jax.ShapeDtypeStruct((B, nch, H, CHUNK, CHUNK), F32)
    return pl.pallas_call(
        body, name="delta_fwd", grid=(B, nch),
        in_specs=in0 + in1, out_specs=(o_s0, o_s1, sin_s0, sin_s1, tm_s0, tm_s1),
        out_shape=(o_shape, o_shape, sin_shape, sin_shape, tm_shape, tm_shape),
        scratch_shapes=[pltpu.VMEM((2, H, hd, hd), F32)],
        compiler_params=_cparams(("parallel", "arbitrary"), VMEM_LIMIT),
    )(qkvn, qkvn, qkvn, gb, qkvn, qkvn, qkvn, gb)


def _delta_bwd(qkvn, gb, sin, tms, do, H, hd, ncc):
    B, T, _ = qkvn.shape
    nch = T // CHUNK
    HD = H * hd
    pairs = [(d, h) for d in range(2) for h in range(H)]

    def body(q0, k0, v0, gb0, sin0, tm0, do0, q1, k1, v1, gb1, sin1, tm1, do1, dqkv0, dqkv1, dgb0, dgb1, dS_ref):
        s = pl.program_id(1)
        tm_refs = (tm0, tm1)
        q_refs, k_refs, v_refs, gb_refs = (q0, q1), (k0, k1), (v0, v1), (gb0, gb1)
        sin_refs, do_refs, dqkv_refs, dgb_refs = (sin0, sin1), (do0, do1), (dqkv0, dqkv1), (dgb0, dgb1)

        @pl.when(s == 0)
        def _():
            dS_ref[...] = jnp.zeros_like(dS_ref)

        masks = [_delta_masks(d) for d in range(2)]
        gbs = [_delta_gb(gb_refs[d], d, masks[d][0], masks[d][2], H) for d in range(2)]
        incl = [masks[d][0] for d, h in pairs]
        strict = [masks[d][1] for d, h in pairs]
        lane = lax.broadcasted_iota(jnp.int32, (1, LANES), 1)
        ones = jnp.ones((3 * CHUNK, LANES), BF16)
        head = lambda ref, h: ref[0, :, h * hd:(h + 1) * hd]
        q = [head(q_refs[d], h) for d, h in pairs]
        k = [head(k_refs[d], h) for d, h in pairs]
        v = [head(v_refs[d], h) for d, h in pairs]
        do_ = [head(do_refs[d], h) for d, h in pairs]
        beta_i = [gbs[d][0][:, h:h + 1] for d, h in pairs]
        S = [sin_refs[d][0, 0, h] for d, h in pairs]
        dSn = [dS_ref[d, h] for d, h in pairs]
        f = _delta_chunk_fwd(q, k, v, [gbs[d][1][:, h:h + 1] for d, h in pairs], [gbs[d][2][h:h + 1, :] for d, h in pairs],
                             beta_i, [gbs[d][3][:, h:h + 1] for d, h in pairs], S, incl, strict,
                             Tm=[tm_refs[d][0, 0, h] for d, h in pairs])
        rsum = lambda x: jnp.sum(x, axis=1, keepdims=True)
        dvn = _each(lambda p, d_, kd, ds: _dot(p, d_, "tn") + _dot(kd, ds), f["P"], do_, f["kd"], dSn)
        dP = _each(lambda d_, vn, m: jnp.where(m, _dot(d_, vn, "nt"), 0.0), do_, f["vn"], incl)
        dqg = _each(lambda d_, s: _dot(d_, s, "nt"), do_, S)
        dS_in = _each(lambda qg, d_, g, ds, w, dv_: _dot(qg, d_, "tn") + g * ds - _dot(w, dv_, "tn"),
                      f["qg"], do_, f["Gam"], dSn, f["w"], dvn)
        dGam = _each(lambda s, ds: jnp.sum(rsum(s * ds), axis=0, keepdims=True), S, dSn)
        dkd = _each(lambda vn, ds: _dot(vn, ds, "nt"), f["vn"], dSn)
        de = _each(lambda dkd_, k_, e: rsum(dkd_ * k_) * e, dkd, k, f["e"])
        dw = _each(lambda dv_, s: -_dot(dv_, s, "nt"), dvn, S)
        dXw = _each(lambda t, dw_: _dot(t, dw_, "tn"), f["Tm"], dw)
        dXu = _each(lambda t, dv_: _dot(t, dv_, "tn"), f["Tm"], dvn)
        dA = _each(lambda xw, w, xu, u, m: -jnp.where(m, _dot(xw, w, "nt") + _dot(xu, u, "nt"), 0.0),
                   dXw, f["w"], dXu, f["u"], strict)
        dM = _each(lambda a, d_: a * d_, dA, f["D"])
        dN = _each(lambda p, d_: p * d_, dP, f["D"])
        dkb = _each(lambda m, k_, xw, g: _dot(m, k_) + xw * g, dM, k, dXw, f["gam"])
        dq = _each(lambda dqg_, g, n, k_: dqg_ * g + _dot(n, k_), dqg, f["gam"], dN, k)
        dk = _each(lambda dkd_, e, m, kb, n, q_, dkb_, b: dkd_ * e + _dot(m, kb, "tn") + _dot(n, q_, "tn") + dkb_ * b,
                   dkd, f["e"], dM, f["kb"], dN, q, dkb, beta_i)
        dv = _each(lambda xu, b: xu * b, dXu, beta_i)
        dbeta = _each(lambda dkb_, k_, xu, v_: rsum(dkb_ * k_) + rsum(xu * v_), dkb, k, dXu, v)
        E = _each(lambda a, A_, p, P_: a * A_ + p * P_, dA, f["A"], dP, f["P"])

        def colsum(e):
            return _dot(_rows(*_split3(e)), ones, "tn")[:, 0:1]

        dg = _each(lambda e, dqg_, qg, xw, kb, g, de_: rsum(e) - colsum(e) + rsum(dqg_ * qg) + rsum(xw * kb) * g - de_,
                   E, dqg, f["qg"], dXw, f["kb"], f["gam"], de)
        dgl_h = _each(lambda de_, dg_, g: jnp.sum(de_, axis=0, keepdims=True) + dg_ * g, de, dGam, f["Gam"])
        for d in range(2):
            dgc = jnp.zeros((CHUNK, LANES), F32)
            dgl = jnp.zeros((1, LANES), F32)
            for h in range(H):
                i = d * H + h
                g_lane, b_lane = d * H + h, (2 + d) * H + h
                dgc = dgc + dg[i] * (lane == g_lane).astype(F32) + dbeta[i] * (lane == b_lane).astype(F32)
                dgl = dgl + dgl_h[i] * (lane == g_lane).astype(F32)
                dS_ref[d, h] = dS_in[i]
                dqkv_refs[d][0, :, h * hd:(h + 1) * hd] = dq[i]
                dqkv_refs[d][0, :, HD + h * hd:HD + (h + 1) * hd] = dk[i]
                dqkv_refs[d][0, :, 2 * HD + h * hd:2 * HD + (h + 1) * hd] = dv[i]
            draw = _dot_mask(masks[d][0].astype(BF16), dgc, "tn") + dgl
            dgb_refs[d][0] = jnp.where(lane < 2 * H, draw, dgc)

    def dir_specs(d):
        cidx = lambda b, s: _chunk_of(d, nch - 1 - s, ncc, nch)
        ins = [pl.BlockSpec((1, CHUNK, HD), lambda b, s, p=p: (b, cidx(b, s), p)) for p in range(3)]
        ins += [pl.BlockSpec((1, CHUNK, LANES), lambda b, s: (b, cidx(b, s), 0)),
                pl.BlockSpec((1, 1, H, hd, hd), lambda b, s: (b, cidx(b, s), 0, 0, 0)),
                pl.BlockSpec((1, 1, H, CHUNK, CHUNK), lambda b, s: (b, cidx(b, s), 0, 0, 0)),
                pl.BlockSpec((1, CHUNK, HD), lambda b, s: (b, cidx(b, s), 0))]
        return (ins, pl.BlockSpec((1, CHUNK, 3 * HD), lambda b, s: (b, cidx(b, s), 0)),
                pl.BlockSpec((1, CHUNK, LANES), lambda b, s: (b, cidx(b, s), 0)))

    (in0, dq_s0, dg_s0), (in1, dq_s1, dg_s1) = dir_specs(0), dir_specs(1)
    dq_shape = jax.ShapeDtypeStruct((B, T, 3 * HD), F32)
    dg_shape = jax.ShapeDtypeStruct((B, T, LANES), F32)
    return pl.pallas_call(
        body, name="delta_bwd", grid=(B, nch),
        in_specs=in0 + in1, out_specs=(dq_s0, dq_s1, dg_s0, dg_s1),
        out_shape=(dq_shape, dq_shape, dg_shape, dg_shape),
        scratch_shapes=[pltpu.VMEM((2, H, hd, hd), F32)],
        compiler_params=_cparams(("parallel", "arbitrary"), VMEM_LIMIT),
    )(qkvn, qkvn, qkvn, gb, sin[0], tms[0], do, qkvn, qkvn, qkvn, gb, sin[1], tms[1], do)


def _dn_out(o2, pz, zcb0, onorm, H, hd, nct, tm):
    B, T, HD = o2[0].shape
    N = T - nct * tm

    def body(o0_ref, o1_ref, z_ref, g_ref, y_ref):
        for h in range(H):
            sl = slice(h * hd, (h + 1) * hd)
            o = o0_ref[0, :, sl] + o1_ref[0, :, sl]
            r = lax.rsqrt(jnp.mean(o * o, axis=-1, keepdims=True) + EPS)
            y_ref[0, :, sl] = (o * r * g_ref[...] * _silu(z_ref[0, :, sl].astype(F32))).astype(BF16)

    return pl.pallas_call(
        body, name="dn_out", grid=(B, N // tm),
        in_specs=[pl.BlockSpec((1, tm, HD), lambda b, t: (b, t + nct, 0)),
                  pl.BlockSpec((1, tm, HD), lambda b, t: (b, t + nct, 0)),
                  pl.BlockSpec((1, tm, HD), lambda b, t: (b, t + nct, zcb0)),
                  pl.BlockSpec((1, hd), lambda b, t: (0, 0))],
        out_specs=pl.BlockSpec((1, tm, HD), lambda b, t: (b, t, 0)),
        out_shape=jax.ShapeDtypeStruct((B, N, HD), BF16),
        compiler_params=_cparams(("parallel",) * 2, VMEM_LIMIT))(o2[0], o2[1], pz, onorm)


def _dn_out_bwd(o2, pz, zcb0, onorm, dy, H, hd, nct, tm):
    B, T, HD = o2[0].shape
    nt = T // tm

    def body(o0_ref, o1_ref, z_ref, g_ref, dy_ref, do_ref, dz_ref, acc_ref):
        t = pl.program_id(1)

        @pl.when(t == 0)
        def _():
            acc_ref[...] = jnp.zeros_like(acc_ref)

        @pl.when(t < nct)
        def _():
            do_ref[0] = jnp.zeros_like(do_ref[0])
            dz_ref[0] = jnp.zeros_like(dz_ref[0])

        @pl.when(t >= nct)
        def _():
            g = g_ref[...]
            for h in range(H):
                sl = slice(h * hd, (h + 1) * hd)
                o = o0_ref[0, :, sl] + o1_ref[0, :, sl]
                z = z_ref[0, :, sl].astype(F32)
                dy_ = dy_ref[0, :, sl]
                r = lax.rsqrt(jnp.mean(o * o, axis=-1, keepdims=True) + EPS)
                oh = o * r
                dz_ref[0, :, sl] = (dy_ * oh * g * _dsilu(z)).astype(BF16)
                dn = dy_ * _silu(z)
                acc_ref[0, 0:1, :] += jnp.sum(dn * oh, axis=0, keepdims=True)
                doh = dn * g
                do_ref[0, :, sl] = r * (doh - oh * jnp.mean(doh * oh, axis=-1, keepdims=True))

    lat = lambda b, t: (b, jnp.maximum(t - nct, 0), 0)
    row = pl.BlockSpec((1, tm, HD), lambda b, t: (b, t, 0))
    return pl.pallas_call(
        body, name="dn_out_bwd", grid=(B, nt),
        in_specs=[row, row,
                  pl.BlockSpec((1, tm, HD), lambda b, t: (b, t, zcb0)),
                  pl.BlockSpec((1, hd), lambda b, t: (0, 0)),
                  pl.BlockSpec((1, tm, HD), lat)],
        out_specs=(row, row, pl.BlockSpec((1, SUBLANES, hd), lambda b, t: (b, 0, 0))),
        out_shape=(jax.ShapeDtypeStruct((B, T, HD), F32), jax.ShapeDtypeStruct((B, T, HD), BF16),
                   jax.ShapeDtypeStruct((B, SUBLANES, hd), F32)),
        compiler_params=_cparams(("parallel", "arbitrary"), VMEM_LIMIT),
    )(o2[0], o2[1], pz, onorm, dy)


def _lru_gate_math(x, wr, wi, br, bi, lam):
    r = _sigmoid(_dot(x, wr) + br)
    i = _sigmoid(_dot(x, wi) + bi)
    sp = _softplus(-lam)
    la = -LRU_C * r * sp
    a = jnp.exp(la)
    s = jnp.sqrt(-jnp.tanh(la) * (a * a + 1.0))
    return r, i, sp, a, s


def _lru_gates(xc, wbd, bias4, lam, tm, bw):
    B, T, W = xc.shape

    def body(x_ref, w_ref, b_ref, l_ref, a_ref, i_ref):
        x = x_ref[0]
        for d in range(2):
            _, i, _, a, s = _lru_gate_math(x, w_ref[2 * d, 0], w_ref[2 * d + 1, 0],
                                           b_ref[2 * d:2 * d + 1, :], b_ref[2 * d + 1:2 * d + 2, :], l_ref[d:d + 1, :])
            a_ref[d, 0] = a
            i_ref[d, 0] = s * (i * x)

    out = pl.BlockSpec((2, 1, tm, bw), lambda b, t, j: (0, b, t, j))
    return pl.pallas_call(
        body, name="lru_gates", grid=(B, T // tm, W // bw),
        in_specs=[pl.BlockSpec((1, tm, bw), lambda b, t, j: (b, t, j)),
                  pl.BlockSpec((4, 1, bw, bw), lambda b, t, j: (0, j, 0, 0)),
                  pl.BlockSpec((4, bw), lambda b, t, j: (0, j)),
                  pl.BlockSpec((2, bw), lambda b, t, j: (0, j))],
        out_specs=(out, out),
        out_shape=(jax.ShapeDtypeStruct((2, B, T, W), F32), jax.ShapeDtypeStruct((2, B, T, W), F32)),
        compiler_params=_cparams(("parallel",) * 3, VMEM_LIMIT))(xc, wbd, bias4, lam)


def _lru_gates_bwd(xc, wbd, bias4, lam, dinp, da, tm, bw):
    B, T, W = xc.shape
    nt = T // tm

    def body(x_ref, w_ref, b_ref, l_ref, di0_ref, di1_ref, da0_ref, da1_ref, dx_ref, dw_ref, acc_ref):
        di_refs, da_refs = (di0_ref, di1_ref), (da0_ref, da1_ref)
        b_ = pl.program_id(1)
        t = pl.program_id(2)

        @pl.when(jnp.logical_and(b_ == 0, t == 0))
        def _():
            dw_ref[...] = jnp.zeros_like(dw_ref)
            acc_ref[...] = jnp.zeros_like(acc_ref)

        x = x_ref[0]
        dx = jnp.zeros_like(x)
        for d in range(2):
            wr, wi = w_ref[2 * d, 0], w_ref[2 * d + 1, 0]
            lam_ = l_ref[d:d + 1, :]
            r, i, sp, a, s = _lru_gate_math(x, wr, wi, b_ref[2 * d:2 * d + 1, :], b_ref[2 * d + 1:2 * d + 2, :], lam_)
            dinp_ = di_refs[d][0]
            dix = dinp_ * s
            ds = dinp_ * i * x
            dla = da_refs[d][0] * a - ds * (a * a) / s
            dpr = dla * (-LRU_C * sp) * r * (1.0 - r)
            dpi = dix * x * i * (1.0 - i)
            dx = dx + dix * i + _dot(dpr, wr, "nt") + _dot(dpi, wi, "nt")
            dw_ref[2 * d, 0] += _dot(x, dpr, "tn")
            dw_ref[2 * d + 1, 0] += _dot(x, dpi, "tn")
            acc_ref[2 * d:2 * d + 1, :] += jnp.sum(dpr, axis=0, keepdims=True)
            acc_ref[2 * d + 1:2 * d + 2, :] += jnp.sum(dpi, axis=0, keepdims=True)
            acc_ref[4 + d:5 + d, :] += jnp.sum(dla * (-LRU_C * r), axis=0, keepdims=True) * (-_sigmoid(-lam_))
        dx_ref[0] = dx

    tile = pl.BlockSpec((1, tm, bw), lambda j, b, t: (b, t, j))
    return pl.pallas_call(
        body, name="lru_gates_bwd", grid=(W // bw, B, nt),
        in_specs=[pl.BlockSpec((1, tm, bw), lambda j, b, t: (b, t, j)),
                  pl.BlockSpec((4, 1, bw, bw), lambda j, b, t: (0, j, 0, 0)),
                  pl.BlockSpec((4, bw), lambda j, b, t: (0, j)),
                  pl.BlockSpec((2, bw), lambda j, b, t: (0, j)), tile, tile, tile, tile],
        out_specs=(pl.BlockSpec((1, tm, bw), lambda j, b, t: (b, t, j)),
                   pl.BlockSpec((4, 1, bw, bw), lambda j, b, t: (0, j, 0, 0)),
                   pl.BlockSpec((SUBLANES, bw), lambda j, b, t: (0, j))),
        out_shape=(jax.ShapeDtypeStruct((B, T, W), F32), jax.ShapeDtypeStruct(wbd.shape, F32),
                   jax.ShapeDtypeStruct((SUBLANES, W), F32)),
        compiler_params=_cparams(("parallel", "arbitrary", "arbitrary"), VMEM_LIMIT),
    )(xc, wbd, bias4, lam, dinp[0], dinp[1], da[0], da[1])


def _tile_scan(a, x, rev, tm):
    row = lax.broadcasted_iota(jnp.int32, a.shape, 0)
    s = 1
    while s < tm:
        if rev:
            a_sh, x_sh, ok = pltpu.roll(a, tm - s, 0), pltpu.roll(x, tm - s, 0), row < tm - s
        else:
            a_sh, x_sh, ok = pltpu.roll(a, s, 0), pltpu.roll(x, s, 0), row >= s
        x = jnp.where(ok, x + a * x_sh, x)
        a = jnp.where(ok, a * a_sh, a)
        s *= 2
    return a, x


def _scan_tile_of(s, rev, nct, nt):
    if not rev:
        return s
    return jnp.where(s < nct, nct - 1 - s, nt - 1 - (s - nct))


def _lru_scan(a2, x2, d, nct, tm, tc):
    _, B, T, W = a2.shape
    nt = T // tm
    rev = d == 1
    last = 0 if rev else tm - 1

    def body(a_ref, x_ref, h_ref, carry):
        s = pl.program_id(2)

        @pl.when(s == 0)
        def _():
            carry[...] = jnp.zeros_like(carry)

        A, X = _tile_scan(a_ref[0, 0], x_ref[0, 0], rev, tm)
        h = X + A * carry[0:1, :]
        h_ref[0] = h
        carry[...] = jnp.broadcast_to(h[last:last + 1, :], carry.shape)

    tidx = lambda s: _scan_tile_of(s, rev, nct, nt)
    spec = pl.BlockSpec((1, 1, tm, tc), lambda b, j, s: (d, b, tidx(s), j))
    return pl.pallas_call(
        body, name=f"lru_scan{d}", grid=(B, W // tc, nt),
        in_specs=[spec, spec], out_specs=pl.BlockSpec((1, tm, tc), lambda b, j, s: (b, tidx(s), j)),
        out_shape=jax.ShapeDtypeStruct((B, T, W), F32),
        scratch_shapes=[pltpu.VMEM((SUBLANES, tc), F32)],
        compiler_params=_cparams(("parallel", "parallel", "arbitrary"), VMEM_LIMIT),
    )(a2, x2)


def _lru_scan_bwd(a2, h, dh, d, nct, tm, tc):
    _, B, T, W = a2.shape
    nt = T // tm
    rev = d == 1
    first = tm - 1 if rev else 0
    r8, n8 = tm // SUBLANES, T // SUBLANES

    def body(a_ref, h_ref, hp_ref, dh_ref, lam_ref, da_ref, ca, cl):
        s = pl.program_id(2)

        @pl.when(s == 0)
        def _():
            ca[...] = jnp.zeros_like(ca)
            cl[...] = jnp.zeros_like(cl)

        a = a_ref[0, 0]
        row = lax.broadcasted_iota(jnp.int32, a.shape, 0)
        if rev:
            c = jnp.where(row == 0, ca[0:1, :], pltpu.roll(a, 1, 0))
        else:
            c = jnp.where(row == tm - 1, ca[0:1, :], pltpu.roll(a, tm - 1, 0))
        C, L = _tile_scan(c, dh_ref[0], not rev, tm)
        lam = L + C * cl[0:1, :]
        lam_ref[0] = lam
        hp = jnp.where(s == nt - 1, 0.0, hp_ref[0])
        if rev:
            hprev = jnp.where(row == tm - 1, hp[0:1, :], pltpu.roll(h_ref[0], tm - 1, 0))
        else:
            hprev = jnp.where(row == 0, hp[SUBLANES - 1:SUBLANES, :], pltpu.roll(h_ref[0], 1, 0))
        da_ref[0] = lam * hprev
        ca[...] = jnp.broadcast_to(a[first:first + 1, :], ca.shape)
        cl[...] = jnp.broadcast_to(lam[first:first + 1, :], cl.shape)

    tidx = lambda s: _scan_tile_of(nt - 1 - s, rev, nct, nt)

    def hp_idx(b, j, s):
        tt = tidx(s)
        if rev:
            blk = jnp.where(tt == nt - 1, 0, jnp.minimum((tt + 1) * r8, n8 - 1))
        else:
            blk = jnp.maximum(tt * r8 - 1, 0)
        return (b, blk, j)

    tile = pl.BlockSpec((1, tm, tc), lambda b, j, s: (b, tidx(s), j))
    return pl.pallas_call(
        body, name=f"lru_scan_bwd{d}", grid=(B, W // tc, nt),
        in_specs=[pl.BlockSpec((1, 1, tm, tc), lambda b, j, s: (d, b, tidx(s), j)), tile,
                  pl.BlockSpec((1, SUBLANES, tc), hp_idx), tile],
        out_specs=(tile, tile),
        out_shape=(jax.ShapeDtypeStruct((B, T, W), F32), jax.ShapeDtypeStruct((B, T, W), F32)),
        scratch_shapes=[pltpu.VMEM((SUBLANES, tc), F32), pltpu.VMEM((SUBLANES, tc), F32)],
        compiler_params=_cparams(("parallel", "parallel", "arbitrary"), VMEM_LIMIT),
    )(a2, h, h, dh)


def _lru_out(h0, h1, pyl, ycb0, nct, tm, tc):
    B, T, W = h0.shape
    N = T - nct * tm

    def body(h0_ref, h1_ref, y_ref, o_ref):
        o_ref[0] = ((h0_ref[0] + h1_ref[0]) * _gelu(y_ref[0].astype(F32))).astype(BF16)

    hs = pl.BlockSpec((1, tm, tc), lambda b, t, j: (b, t + nct, j))
    return pl.pallas_call(
        body, name="lru_out", grid=(B, N // tm, W // tc),
        in_specs=[hs, hs, pl.BlockSpec((1, tm, tc), lambda b, t, j: (b, t + nct, ycb0 + j))],
        out_specs=pl.BlockSpec((1, tm, tc), lambda b, t, j: (b, t, j)),
        out_shape=jax.ShapeDtypeStruct((B, N, W), BF16),
        compiler_params=_cparams(("parallel",) * 3, VMEM_LIMIT))(h0, h1, pyl)


def _lru_out_bwd(h0, h1, pyl, ycb0, dy, nct, tm, tc):
    B, T, W = h0.shape

    def body(h0_ref, h1_ref, y_ref, dy_ref, dh_ref, dyl_ref):
        t = pl.program_id(1)

        @pl.when(t < nct)
        def _():
            dh_ref[0] = jnp.zeros_like(dh_ref[0])
            dyl_ref[0] = jnp.zeros_like(dyl_ref[0])

        @pl.when(t >= nct)
        def _():
            y = y_ref[0].astype(F32)
            dy_ = dy_ref[0]
            dh_ref[0] = dy_ * _gelu(y)
            dyl_ref[0] = (dy_ * (h0_ref[0] + h1_ref[0]) * _dgelu(y)).astype(BF16)

    hs = pl.BlockSpec((1, tm, tc), lambda b, t, j: (b, t, j))
    return pl.pallas_call(
        body, name="lru_out_bwd", grid=(B, T // tm, W // tc),
        in_specs=[hs, hs, pl.BlockSpec((1, tm, tc), lambda b, t, j: (b, t, ycb0 + j)),
                  pl.BlockSpec((1, tm, tc), lambda b, t, j: (b, jnp.maximum(t - nct, 0), j))],
        out_specs=(hs, hs),
        out_shape=(jax.ShapeDtypeStruct((B, T, W), F32), jax.ShapeDtypeStruct((B, T, W), BF16)),
        compiler_params=_cparams(("parallel",) * 3, VMEM_LIMIT))(h0, h1, pyl, dy)


def _merge(bd, bl, pmg, bm, nct, tm):
    B, N, D = bd.shape

    def body(bd_ref, bl_ref, m0_ref, m1_ref, b_ref, o_ref):
        g0 = _sigmoid(m0_ref[0].astype(F32) + b_ref[:, 0:D])
        g1 = _sigmoid(m1_ref[0].astype(F32) + b_ref[:, D:2 * D])
        o_ref[0] = (g0 * bd_ref[0].astype(F32) + g1 * bl_ref[0].astype(F32)).astype(BF16)

    row = pl.BlockSpec((1, tm, D), lambda b, t: (b, t, 0))
    return pl.pallas_call(
        body, name="merge", grid=(B, N // tm),
        in_specs=[row, row, pl.BlockSpec((1, tm, D), lambda b, t: (b, t + nct, 0)),
                  pl.BlockSpec((1, tm, D), lambda b, t: (b, t + nct, 1)),
                  pl.BlockSpec((1, 2 * D), lambda b, t: (0, 0))],
        out_specs=row, out_shape=jax.ShapeDtypeStruct((B, N, D), BF16),
        compiler_params=_cparams(("parallel", "parallel"), VMEM_LIMIT))(bd, bl, pmg, pmg, bm)


def _merge_bwd(dmi, bd, bl, pmg, bm, nct, tm):
    B, N, D = bd.shape
    T = pmg.shape[1]

    def body(dm_ref, bd_ref, bl_ref, m0_ref, m1_ref, b_ref, dbd_ref, dbl_ref, dmg_ref, acc_ref):
        t = pl.program_id(1)

        @pl.when(t == 0)
        def _():
            acc_ref[...] = jnp.zeros_like(acc_ref)

        @pl.when(t < nct)
        def _():
            dmg_ref[0] = jnp.zeros_like(dmg_ref[0])

        @pl.when(t >= nct)
        def _():
            dm = dm_ref[0]
            g0 = _sigmoid(m0_ref[0].astype(F32) + b_ref[:, 0:D])
            g1 = _sigmoid(m1_ref[0].astype(F32) + b_ref[:, D:2 * D])
            dbd_ref[0] = (dm * g0).astype(BF16)
            dbl_ref[0] = (dm * g1).astype(BF16)
            d0 = dm * bd_ref[0].astype(F32) * g0 * (1.0 - g0)
            d1 = dm * bl_ref[0].astype(F32) * g1 * (1.0 - g1)
            dmg_ref[0, :, 0:D] = d0.astype(BF16)
            dmg_ref[0, :, D:2 * D] = d1.astype(BF16)
            acc_ref[0, 0:1, 0:D] += jnp.sum(d0, axis=0, keepdims=True)
            acc_ref[0, 0:1, D:2 * D] += jnp.sum(d1, axis=0, keepdims=True)

    lat = pl.BlockSpec((1, tm, D), lambda b, t: (b, jnp.maximum(t - nct, 0), 0))
    return pl.pallas_call(
        body, name="merge_bwd", grid=(B, T // tm),
        in_specs=[lat, lat, lat, pl.BlockSpec((1, tm, D), lambda b, t: (b, t, 0)),
                  pl.BlockSpec((1, tm, D), lambda b, t: (b, t, 1)),
                  pl.BlockSpec((1, 2 * D), lambda b, t: (0, 0))],
        out_specs=(lat, lat, pl.BlockSpec((1, tm, 2 * D), lambda b, t: (b, t, 0)),
                   pl.BlockSpec((1, SUBLANES, 2 * D), lambda b, t: (b, 0, 0))),
        out_shape=(jax.ShapeDtypeStruct((B, N, D), BF16), jax.ShapeDtypeStruct((B, N, D), BF16),
                   jax.ShapeDtypeStruct((B, T, 2 * D), BF16), jax.ShapeDtypeStruct((B, SUBLANES, 2 * D), F32)),
        compiler_params=_cparams(("parallel", "arbitrary"), VMEM_LIMIT))(dmi, bd, bl, pmg, pmg, bm)


def _grid_taps():
    return [((di + 1) * 3 + (dj + 1), di, dj) for di in (-1, 0, 1) for dj in (-1, 0, 1)]


def _grid_views(x, n):
    pos = lax.broadcasted_iota(jnp.int32, x.shape, 0)
    gc = jnp.bitwise_and(pos, GRID_W - 1)
    cols = {0: x,
            -1: jnp.where(gc >= 1, pltpu.roll(x, 1, 0), 0.0),
            1: jnp.where(gc <= GRID_W - 2, pltpu.roll(x, n - 1, 0), 0.0)}
    views = {}
    edge = jnp.zeros((GRID_W, x.shape[1]), x.dtype)
    for dj, xc in cols.items():
        views[(0, dj)] = xc
        views[(-1, dj)] = jnp.concatenate([edge, xc[:n - GRID_W]], axis=0)
        views[(1, dj)] = jnp.concatenate([xc[GRID_W:], edge], axis=0)
    return views


def _ffn_act(F, w9, b, tc):
    B, N, C2 = F.shape
    Cf = C2 // 2
    ncb = Cf // tc

    def body(g_ref, v_ref, w_ref, b_ref, o_ref, gel_ref, dgel_ref):
        xv = _grid_views(g_ref[0], N)
        conv = b_ref[...]
        for k, di, dj in _grid_taps():
            conv = conv + xv[(di, dj)] * w_ref[k:k + 1, :]
        th = jnp.tanh(GELU_C * (conv + 0.044715 * conv * conv * conv))
        gel = 0.5 * conv * (1.0 + th)
        o_ref[0] = (gel * v_ref[0]).astype(BF16)
        gel_ref[0] = gel.astype(BF16)
        dgel_ref[0] = (0.5 * (1.0 + th)
                       + 0.5 * conv * (1.0 - th * th) * GELU_C * (1.0 + 3.0 * 0.044715 * conv * conv)).astype(BF16)

    tile = pl.BlockSpec((1, N, tc), lambda b, j: (b, 0, j))
    out = jax.ShapeDtypeStruct((B, N, Cf), BF16)
    return pl.pallas_call(
        body, name="ffn_act", grid=(B, ncb),
        in_specs=[tile, pl.BlockSpec((1, N, tc), lambda b, j: (b, 0, ncb + j)),
                  pl.BlockSpec((9, tc), lambda b, j: (0, j)),
                  pl.BlockSpec((1, tc), lambda b, j: (0, j))],
        out_specs=(tile, tile, tile), out_shape=(out, out, out),
        compiler_params=_cparams(("parallel", "parallel"), VMEM_LIMIT))(F, F, w9, b)


def _ffn_act_bwd(F, w9, gel, dgel, df, tc):
    B, N, C2 = F.shape
    Cf = C2 // 2
    ncb = Cf // tc

    def body(g_ref, v_ref, w_ref, gel_ref, dgel_ref, df_ref, dF_ref, acc_ref):
        p = pl.program_id(2)

        @pl.when(p == 0)
        def _():
            dF_ref[0] = (df_ref[0] * gel_ref[0].astype(F32)).astype(BF16)

        @pl.when(p == 1)
        def _():
            xv = _grid_views(g_ref[0], N)
            df_ = df_ref[0]
            dc = df_ * v_ref[0] * dgel_ref[0].astype(F32)
            dcv = _grid_views(dc, N)
            dx = None
            for k, di, dj in _grid_taps():
                term = dcv[(-di, -dj)] * w_ref[k:k + 1, :]
                dx = term if dx is None else dx + term
                acc_ref[0, k:k + 1, :] = jnp.sum(dc * xv[(di, dj)], axis=0, keepdims=True)
            acc_ref[0, 9:10, :] = jnp.sum(dc, axis=0, keepdims=True)
            acc_ref[0, 10:16, :] = jnp.zeros((6, tc), F32)
            dF_ref[0] = dx.astype(BF16)

    tile = pl.BlockSpec((1, N, tc), lambda b, j, p: (b, 0, j))
    return pl.pallas_call(
        body, name="ffn_act_bwd", grid=(B, ncb, 2),
        in_specs=[tile, pl.BlockSpec((1, N, tc), lambda b, j, p: (b, 0, ncb + j)),
                  pl.BlockSpec((9, tc), lambda b, j, p: (0, j)), tile, tile, tile],
        out_specs=(pl.BlockSpec((1, N, tc), lambda b, j, p: (b, 0, j + (1 - p) * ncb)),
                   pl.BlockSpec((1, 16, tc), lambda b, j, p: (b, 0, j))),
        out_shape=(jax.ShapeDtypeStruct((B, N, C2), BF16), jax.ShapeDtypeStruct((B, 16, Cf), F32)),
        compiler_params=_cparams(("parallel", "parallel", "arbitrary"), VMEM_LIMIT))(F, F, w9, gel, dgel, df)


ADAM_ROWS = 512


def _adamw(w, m, v, gparts, name):
    rows, cols = w.shape[-2:]
    P = gparts.shape[0]
    tr = _row_tile(rows, (P + 14) * 4 * (-(-cols // LANES) * LANES))
    c1 = 1.0 - ADAM_B1 ** ADAM_STEP
    c2 = 1.0 - ADAM_B2 ** ADAM_STEP

    def body(w_ref, m_ref, v_ref, g_ref, go_ref, d_ref, mo_ref, vo_ref):
        g = g_ref[0].astype(F32)
        for p in range(1, P):
            g = g + g_ref[p].astype(F32)
        m_ = ADAM_B1 * m_ref[...] + (1.0 - ADAM_B1) * g
        v_ = ADAM_B2 * v_ref[...] + (1.0 - ADAM_B2) * (g * g)
        go_ref[...] = g
        mo_ref[...] = m_
        vo_ref[...] = v_
        d_ref[...] = -ADAM_LR * ((m_ / c1) / (jnp.sqrt(v_ / c2) + ADAM_EPS) + ADAM_WD * w_ref[...])

    if w.ndim == 3:
        row = pl.BlockSpec((None, tr, cols), lambda i: (0, i, 0))
    else:
        row = pl.BlockSpec((tr, cols), lambda i: (i, 0))
    out = jax.ShapeDtypeStruct(w.shape, F32)
    return pl.pallas_call(
        body, name=name, grid=(rows // tr,),
        in_specs=[row, row, row, pl.BlockSpec((P, tr, cols), lambda i: (0, i, 0))],
        out_specs=(row, row, row, row), out_shape=(out, out, out, out),
        compiler_params=_cparams(("parallel",), VMEM_LIMIT))(w, m, v, gparts)


def _pack_rows(flat_len):
    rows = -(-flat_len // LANES)
    if rows > ADAM_ROWS:
        rows = -(-rows // ADAM_ROWS) * ADAM_ROWS
    else:
        rows = -(-rows // SUBLANES) * SUBLANES
    return rows


PACK_ALIGN = SUBLANES * LANES


def _pack(arrs, lead=()):
    pads = [(0, 0)] * len(lead)
    parts = []
    for a in arrs:
        f = a.reshape(lead + (-1,)).astype(F32)
        parts.append(jnp.pad(f, pads + [(0, -f.shape[-1] % PACK_ALIGN)]))
    flat = jnp.concatenate(parts, axis=-1)
    n = flat.shape[-1]
    rows = _pack_rows(n)
    flat = jnp.pad(flat, pads + [(0, rows * LANES - n)])
    return flat.reshape(lead + (rows, LANES))


def _unpack(buf, shapes):
    out, r = [], 0
    for s in shapes:
        n = math.prod(s)
        nr = -(-n // PACK_ALIGN) * SUBLANES
        out.append(buf[r:r + nr].reshape(-1)[:n].reshape(s))
        r += nr
    return out


def _col_shards(full, n_lead):
    C = full.shape[-1]
    x = full.reshape(full.shape[:-1] + (N_DEV, C // N_DEV))
    return jnp.moveaxis(x, -2, 0)


def _from_col_shards(g):
    x = jnp.moveaxis(g, 0, -2)
    return x.reshape(x.shape[:-2] + (x.shape[-2] * x.shape[-1],))


def kernel(x, c, ctx, c_ctx, w_ada, b_ada, g_pre_mix, g_post_mix, g_pre_ffn, g_post_ffn, w_in, b_merge, dn_conv, dn_a_log, dn_dt_bias, dn_onorm, lru_conv, lru_conv_b, lru_w_rg, lru_b_rg, lru_w_ig, lru_b_ig, lru_lambda, w_branch_dn, w_branch_lru, w_out, w_up, ffn_dw, ffn_dw_b, w_down, loss_target, m_c_ctx, m_w_ada, m_b_ada, m_g_pre_mix, m_g_post_mix, m_g_pre_ffn, m_g_post_ffn, m_w_in, m_b_merge, m_dn_conv, m_dn_a_log, m_dn_dt_bias, m_dn_onorm, m_lru_conv, m_lru_conv_b, m_lru_w_rg, m_lru_b_rg, m_lru_w_ig, m_lru_b_ig, m_lru_lambda, m_w_branch_dn, m_w_branch_lru, m_w_out, m_w_up, m_ffn_dw, m_ffn_dw_b, m_w_down, v_c_ctx, v_w_ada, v_b_ada, v_g_pre_mix, v_g_post_mix, v_g_pre_ffn, v_g_post_ffn, v_w_in, v_b_merge, v_dn_conv, v_dn_a_log, v_dn_dt_bias, v_dn_onorm, v_lru_conv, v_lru_conv_b, v_lru_w_rg, v_lru_b_rg, v_lru_w_ig, v_lru_b_ig, v_lru_lambda, v_w_branch_dn, v_w_branch_lru, v_w_out, v_w_up, v_ffn_dw, v_ffn_dw_b, v_w_down):
    params = dict(c_ctx=c_ctx, w_ada=w_ada, b_ada=b_ada, g_pre_mix=g_pre_mix, g_post_mix=g_post_mix, g_pre_ffn=g_pre_ffn, g_post_ffn=g_post_ffn, w_in=w_in, b_merge=b_merge, dn_conv=dn_conv, dn_a_log=dn_a_log, dn_dt_bias=dn_dt_bias, dn_onorm=dn_onorm, lru_conv=lru_conv, lru_conv_b=lru_conv_b, lru_w_rg=lru_w_rg, lru_b_rg=lru_b_rg, lru_w_ig=lru_w_ig, lru_b_ig=lru_b_ig, lru_lambda=lru_lambda, w_branch_dn=w_branch_dn, w_branch_lru=w_branch_lru, w_out=w_out, w_up=w_up, ffn_dw=ffn_dw, ffn_dw_b=ffn_dw_b, w_down=w_down)
    mom1 = dict(c_ctx=m_c_ctx, w_ada=m_w_ada, b_ada=m_b_ada, g_pre_mix=m_g_pre_mix, g_post_mix=m_g_post_mix, g_pre_ffn=m_g_pre_ffn, g_post_ffn=m_g_post_ffn, w_in=m_w_in, b_merge=m_b_merge, dn_conv=m_dn_conv, dn_a_log=m_dn_a_log, dn_dt_bias=m_dn_dt_bias, dn_onorm=m_dn_onorm, lru_conv=m_lru_conv, lru_conv_b=m_lru_conv_b, lru_w_rg=m_lru_w_rg, lru_b_rg=m_lru_b_rg, lru_w_ig=m_lru_w_ig, lru_b_ig=m_lru_b_ig, lru_lambda=m_lru_lambda, w_branch_dn=m_w_branch_dn, w_branch_lru=m_w_branch_lru, w_out=m_w_out, w_up=m_w_up, ffn_dw=m_ffn_dw, ffn_dw_b=m_ffn_dw_b, w_down=m_w_down)
    mom2 = dict(c_ctx=v_c_ctx, w_ada=v_w_ada, b_ada=v_b_ada, g_pre_mix=v_g_pre_mix, g_post_mix=v_g_post_mix, g_pre_ffn=v_g_pre_ffn, g_post_ffn=v_g_post_ffn, w_in=v_w_in, b_merge=v_b_merge, dn_conv=v_dn_conv, dn_a_log=v_dn_a_log, dn_dt_bias=v_dn_dt_bias, dn_onorm=v_dn_onorm, lru_conv=v_lru_conv, lru_conv_b=v_lru_conv_b, lru_w_rg=v_lru_w_rg, lru_b_rg=v_lru_b_rg, lru_w_ig=v_lru_w_ig, lru_b_ig=v_lru_b_ig, lru_lambda=v_lru_lambda, w_branch_dn=v_w_branch_dn, w_branch_lru=v_w_branch_lru, w_out=v_w_out, w_up=v_w_up, ffn_dw=v_ffn_dw, ffn_dw_b=v_ffn_dw_b, w_down=v_w_down)
    names = list(params)

    B, N, D = x.shape
    NC = ctx.shape[1]
    T = NC + N
    H, hd = dn_a_log.shape[2], dn_onorm.shape[1]
    HD = H * hd
    nd = 2 * H
    W = lru_conv_b.shape[1]
    nblk, bdim = lru_w_rg.shape[2], lru_w_rg.shape[3]
    Cf = ffn_dw_b.shape[1]
    sw = w_ada.shape[2]
    tm = min(256, NC)
    nct = NC // tm
    ncc = NC // CHUNK
    nch = T // CHUNK
    R, RL = B * T, B * N
    bw = min(256, W)
    me = _my_index()
    assert NC % tm == 0 and N % tm == 0 and B + 1 <= SUBLANES and bw % bdim == 0 and D % LANES == 0

    cpad = jnp.zeros((SUBLANES, D), F32).at[:B].set(c).at[B].set(c_ctx)
    ccp = _all_gather([cpad], "ag_cond")[0].reshape(N_DEV * SUBLANES, D)
    mods_sh = _ada_fwd(ccp, w_ada[0], lax.dynamic_slice(b_ada, (0, me * sw), (1, sw)))
    lru_vecs = jnp.concatenate([lru_b_rg[0], lru_b_ig[0], lru_lambda[0], jnp.zeros_like(lru_lambda[0])], axis=0)
    mods_g, w_in_g, dn_conv_g, lru_conv_g, lru_vecs_g = _all_gather(
        [mods_sh, w_in[0].astype(BF16).T, dn_conv[0], lru_conv[0], lru_vecs], "ag_first")
    ag_mix = _split_start([w_branch_dn[0].astype(BF16), w_branch_lru[0].astype(BF16), w_out[0].astype(BF16)],
                          "gather", "ag_mix_start")
    ag_ffn = _split_start([w_up[0].astype(BF16) + ag_mix[4][0, 0].astype(BF16), w_down[0].astype(BF16),
                           ffn_dw[0].reshape(9, -1)], "gather", "ag_ffn_start")
    mine = lax.dynamic_slice(mods_g, (0, me * SUBLANES, 0), (N_DEV, SUBLANES, sw))
    mods = jnp.moveaxis(mine, 0, 1).reshape(SUBLANES, 6, D)[:B + 1]
    mods = jnp.pad(mods, ((0, 0), (0, SUBLANES - 6), (0, 0))) + ag_ffn[4][0, 0]
    dn_conv_f = _from_col_shards(dn_conv_g)
    lru_conv_f = _from_col_shards(lru_conv_g)
    lru_vecs_f = _from_col_shards(lru_vecs_g)
    lru_lam_f = lru_vecs_f[4:6]

    csh = w_in_g.shape[1]
    w_in_t = w_in_g.reshape(N_DEV * csh, D)
    o_z, o_a, o_xl = 3 * HD, 4 * HD, 4 * HD + 2 * nd
    o_yl, o_mg = o_xl + W, o_xl + 2 * W
    w_qkv, w_z = w_in_t[:o_z], w_in_t[o_z:o_a]
    w_ab = jnp.pad(w_in_t[o_a:o_xl], ((0, LANES - 2 * nd), (0, 0)))
    w_xl, w_yl, w_mg = w_in_t[o_xl:o_yl], w_in_t[o_yl:o_mg], w_in_t[o_mg:]

    def blockdiag(wg):
        per = bw // bdim
        g5 = wg.reshape(2, W // bw, per, bdim, bdim)
        eye = jnp.eye(per, dtype=wg.dtype)
        return jnp.einsum("dtpij,pq->dtpiqj", g5, eye).reshape(2, W // bw, bw, bw)

    bd_rg, bd_ig = blockdiag(lru_w_rg[0]), blockdiag(lru_w_ig[0])
    wbd = jnp.stack([bd_rg[0], bd_ig[0], bd_rg[1], bd_ig[1]]).astype(BF16)
    bias4 = jnp.stack([lru_vecs_f[0], lru_vecs_f[2], lru_vecs_f[1], lru_vecs_f[3]])
    alog_v = jnp.pad(dn_a_log.reshape(1, nd), ((0, 0), (0, LANES - nd)))
    dtb_v = jnp.pad(dn_dt_bias.reshape(1, nd), ((0, 0), (0, LANES - nd)))

    h0 = jnp.concatenate([ctx, x], axis=1)
    u1 = _norm_mod_fwd(h0, g_pre_mix, mods, 0, 1, nct, tm, "norm_mod1")
    u1f = u1.reshape(R, D)
    p_qkv = _mm(u1f, w_qkv, "nt", "mm_qkv").reshape(B, T, 3 * HD)
    p_z = _mm(u1f, w_z, "nt", "mm_z", out_dtype=BF16).reshape(B, T, HD)
    p_ab = _mm(u1f, w_ab, "nt", "mm_ab")
    p_xl = _mm(u1f, w_xl, "nt", "mm_xl").reshape(B, T, W)
    p_yl = _mm(u1f, w_yl, "nt", "mm_yl", out_dtype=BF16).reshape(B, T, W)
    p_mg = _mm(u1f, w_mg, "nt", "mm_mg", out_dtype=BF16).reshape(B, T, 2 * D)

    conv_offs = (-2, -1, 0, 1)
    tcc = _tile(HD, 1024)
    gb = _ab_act(p_ab, alog_v, dtb_v, nd, tm)
    gb3 = gb.reshape(B, T, LANES)
    cv, qkvn = _dwconv(p_qkv, 0, 3 * HD, dn_conv_f, None, conv_offs, nct, tm, HD, "dn_conv", qkv_heads=(H, hd))
    o_d0, o_d1, s_in0, s_in1, tm_d0, tm_d1 = _delta_fwd(qkvn, gb3, H, hd, ncc)
    o2 = (o_d0, o_d1)
    y_dn = _dn_out(o2, p_z, 0, dn_onorm, H, hd, nct, tm)

    tcw = _tile(W, 1024)
    xc = _dwconv(p_xl, 0, W, lru_conv_f, lru_conv_b, conv_offs, nct, tm, tcw, "lru_conv")
    tmg = max(t_ for t_ in range(SUBLANES, min(T, 768) + 1, SUBLANES) if T % t_ == 0)
    a2, inp2 = _lru_gates(xc, wbd, bias4, lru_lam_f, tmg, bw)
    hd0 = _lru_scan(a2, inp2, 0, nct, tm, tcw)
    hd1 = _lru_scan(a2, inp2, 1, nct, tm, tcw)
    y_lru = _lru_out(hd0, hd1, p_yl, 0, nct, tm, tcw)

    def gathered(started, after, name):
        xs_, lands_ = _split_wait(started, "gather", after, name)
        return [lax.dynamic_update_slice(land, x_[None], (me,) + (0,) * x_.ndim) for land, x_ in zip(lands_, xs_)]

    w_bdn_g, w_blru_g, w_out_g = gathered(ag_mix, y_lru, "ag_mix_wait")
    w_bdn_f, w_blru_f, w_out_f = w_bdn_g.reshape(HD, D), w_blru_g.reshape(W, D), w_out_g.reshape(D, D)
    bd = _mm(y_dn.reshape(RL, HD), w_bdn_f, "nn", "mm_bdn", out_dtype=BF16).reshape(B, N, D)
    bl = _mm(y_lru.reshape(RL, W), w_blru_f, "nn", "mm_blru", out_dtype=BF16).reshape(B, N, D)
    mixin = _merge(bd, bl, p_mg, b_merge, nct, tm)
    mix = _mm(mixin.reshape(RL, D), w_out_f, "nn", "mm_out").reshape(B, N, D)
    h1 = _resid_norm_fwd(x, mix, g_post_mix, mods, 2, tm, "resid_norm1")
    u2 = _norm_mod_fwd(h1, g_pre_ffn, mods, 3, 4, 0, tm, "norm_mod2")
    w_up_g, w_down_g, ffn_dw_g = gathered(ag_ffn, u2, "ag_ffn_wait")
    w_down_f = w_down_g.reshape(Cf, D)
    ffn_dw_f = _from_col_shards(ffn_dw_g)
    Fp = _mm(u2.reshape(RL, D), w_up_g, "nn", "mm_up").reshape(B, N, 2 * Cf)
    tcf = LANES
    f, f_gel, f_dgel = _ffn_act(Fp, ffn_dw_f, ffn_dw_b, tcf)
    dproj = _mm(f.reshape(RL, Cf), w_down_f, "nn", "mm_down").reshape(B, N, D)

    dd, dh2, acc_f = _final_fwd_bwd(h1, dproj, g_post_ffn, mods, 5, loss_target, tm)
    loss = lax.psum((0.5 / D) * jnp.sum(acc_f[:, 2, :]), MESH_AXES)
    ddf = dd.reshape(RL, D)
    df = _mm(ddf, w_down_f, "nt", "mm_down_dx").reshape(B, N, Cf)
    gw_down = _mm(f.reshape(RL, Cf), ddf, "tn", "mm_down_dw", out_dtype=BF16)
    dF, acc_ffn = _ffn_act_bwd(Fp, ffn_dw_f, f_gel, f_dgel, df, tcf)
    dFf = dF.reshape(RL, 2 * Cf)
    du2 = _mm(dFf, w_up_g, "nt", "mm_up_dx").reshape(B, N, D)
    gw_up = _mm(u2.reshape(RL, D), dFf, "tn", "mm_up_dw", out_dtype=BF16, out_shards=N_DEV)
    jm = 2 * lax.axis_index("x") + lax.axis_index("y")

    def pair_sums(parts8, tag):
        sib4 = _pair_exchange(parts8, "rs_pair_" + tag)
        return [_pair_sum(p8, s4, f"rs_pair_sum_{tag}{i}") for i, (p8, s4) in enumerate(zip(parts8, sib4))]

    def own_slab(land, x4):
        idx = (jm,) + (0,) * (x4.ndim - 1)
        return lax.dynamic_update_slice(land, lax.dynamic_slice(x4, idx, (1,) + x4.shape[1:]), idx)

    q_early = _split_start(pair_sums([gw_up, gw_down.reshape(N_DEV, Cf // N_DEV, D)], "ffn"), "quad", "rs_quad_ffn_start")
    mods_b = mods + q_early[4][0, 0]
    dh1, acc2x, _ = _norm_mod_bwd(h1, du2, g_pre_ffn, mods_b, 4, 0, tm, dh2, "norm_mod2_bwd")
    dmix, acc_r1 = _resid_norm_bwd(mix, dh1, g_post_mix, mods, 2, tm, "resid_norm1_bwd")
    dmixf = dmix.reshape(RL, D)
    dmixin = _mm(dmixf, w_out_f, "nt", "mm_out_dx").reshape(B, N, D)
    gw_out = _mm(mixin.reshape(RL, D), dmixf, "tn", "mm_out_dw", out_dtype=BF16)
    dbd, dbl, dmg, acc_mg = _merge_bwd(dmixin, bd, bl, p_mg, b_merge, nct, tm)
    dy_dn = _mm(dbd.reshape(RL, D), w_bdn_f, "nt", "mm_bdn_dx").reshape(B, N, HD)
    gw_bdn = _mm(y_dn.reshape(RL, HD), dbd.reshape(RL, D), "tn", "mm_bdn_dw", out_dtype=BF16)
    dy_lru = _mm(dbl.reshape(RL, D), w_blru_f, "nt", "mm_blru_dx").reshape(B, N, W)
    gw_blru = _mm(y_lru.reshape(RL, W), dbl.reshape(RL, D), "tn", "mm_blru_dw", out_dtype=BF16)

    dh, dyl = _lru_out_bwd(hd0, hd1, p_yl, 0, dy_lru, nct, tm, tcw)
    lam0, da0 = _lru_scan_bwd(a2, hd0, dh, 0, nct, tm, tcw)
    lam1, da1 = _lru_scan_bwd(a2, hd1, dh, 1, nct, tm, tcw)
    dxc, dwbd, acc_lru = _lru_gates_bwd(xc, wbd, bias4, lru_lam_f, (lam0, lam1), (da0, da1), tmg, bw)
    dxl = _dwconv(dxc, 0, W, lru_conv_f, None, tuple(-o for o in conv_offs), nct, tm, tcw, "lru_conv_dx", BF16)
    acc_lc = _dwconv_wgrad(dxc, p_xl, 0, W, conv_offs, nct, tm, tcw, "lru_conv_dw")

    do, dz, acc_on = _dn_out_bwd(o2, p_z, 0, dn_onorm, dy_dn, H, hd, nct, tm)
    dqkvn0, dqkvn1, dgb0, dgb1 = _delta_bwd(qkvn, gb3, (s_in0, s_in1), (tm_d0, tm_d1), do, H, hd, ncc)
    dcv = _dn_act_bwd(cv, (dqkvn0, dqkvn1), H, hd, tm)
    dqkv = _dwconv(dcv, 0, 3 * HD, dn_conv_f, None, tuple(-o for o in conv_offs), nct, tm, tcc, "dn_conv_dx", BF16)
    acc_dc = _dwconv_wgrad(dcv, p_qkv, 0, 3 * HD, conv_offs, nct, tm, tcc, "dn_conv_dw")
    dp_ab, acc_ab = _ab_act_bwd(p_ab, gb, (dgb0.reshape(R, LANES), dgb1.reshape(R, LANES)), alog_v, dtb_v, nd, tm)

    groups = [(dqkv.reshape(R, 3 * HD), w_qkv, "qkv"), (dz.reshape(R, HD), w_z, "z"), (dp_ab, w_ab, "ab"),
              (dxl.reshape(R, W), w_xl, "xl"), (dyl.reshape(R, W), w_yl, "yl"), (dmg.reshape(R, 2 * D), w_mg, "mg")]
    du1 = _mm_sum([(dg_, wg_) for dg_, wg_, _ in groups], "mm_in_dx")
    gw_groups = [_mm(dg_, u1f, "tn", "mm_in_dw_" + nm, out_dtype=BF16) for dg_, _, nm in groups]
    gw_groups[2] = gw_groups[2][:2 * nd]
    gw_in = jnp.concatenate(gw_groups, axis=0).reshape(N_DEV, csh, D)
    grad_x, acc1x, acc1c = _norm_mod_bwd(h0, du1.reshape(B, T, D), g_pre_mix, mods, 1, nct, tm, dh1, "norm_mod1_bwd")

    g_lru_conv = jnp.sum(acc_lc[:, :4], 0)
    g_dn_conv = jnp.sum(acc_dc[:, :4], 0)
    g_ffn_dw = jnp.sum(acc_ffn[:, :9], 0).reshape(3, 3, Cf)
    sm_names = ["dn_conv", "lru_conv", "lru_b_rg", "lru_b_ig", "lru_lambda", "ffn_dw"]
    sm_parts = [_col_shards(g_dn_conv, 1), _col_shards(g_lru_conv, 1),
                _col_shards(jnp.stack([acc_lru[0], acc_lru[2]]), 1), _col_shards(jnp.stack([acc_lru[1], acc_lru[3]]), 1),
                _col_shards(acc_lru[4:6], 1), _col_shards(g_ffn_dw, 2)]
    late8 = [gw_in, gw_bdn.reshape(N_DEV, HD // N_DEV, D), gw_blru.reshape(N_DEV, W // N_DEV, D),
             gw_out.reshape(N_DEV, D // N_DEV, D), _pack(sm_parts, lead=(N_DEV,))]
    q_late = _split_start(pair_sums(late8, "mix"), "quad", "rs_quad_mix_start")
    ffn_x, ffn_land = _split_wait(q_early, "quad", q_late[4], "rs_quad_ffn_wait")
    results = {}
    for n, x4, land in zip(["w_up", "w_down"], ffn_x, ffn_land):
        results[n] = list(_adamw(params[n], mom1[n], mom2[n], own_slab(land, x4), "adamw_" + n))

    zeros_d = jnp.zeros((B, D), F32)
    dm_rows = jnp.stack([acc1x[:, 0], acc1x[:, 1], acc_r1[:, 0], acc2x[:, 0], acc2x[:, 1], acc_f[:, 0]], axis=1)
    dm_ctx = jnp.stack([jnp.sum(acc1c[:, 0], 0), jnp.sum(acc1c[:, 1], 0)] + [zeros_d[0]] * 4, axis=0)[None]
    dm_pad = jnp.zeros((SUBLANES, 6 * D), F32).at[:B + 1].set(jnp.concatenate([dm_rows, dm_ctx], 0).reshape(B + 1, 6 * D))
    dm_g = _all_gather([dm_pad], "ag_dmods")[0]
    g_mine = lax.dynamic_slice(dm_g, (0, 0, me * sw), (N_DEV, SUBLANES, sw)).reshape(N_DEV * SUBLANES, sw)
    gw_ada, dcc = _ada_bwd(ccp, w_ada[0], g_mine, c_ctx.reshape(1, D), B)

    per = bw // bdim

    def diag_blocks(dwf):
        g6 = dwf.reshape(W // bw, per, bdim, per, bdim)
        return jnp.einsum("tpiqj,pq->tpij", g6, jnp.eye(per, dtype=F32)).reshape(nblk, bdim, bdim)

    g_rep = dict(
        c_ctx=dcc[0],
        g_pre_mix=jnp.sum(acc1x[:, 2] + acc1c[:, 2], 0)[None], g_post_mix=jnp.sum(acc_r1[:, 1], 0)[None],
        g_pre_ffn=jnp.sum(acc2x[:, 2], 0)[None], g_post_ffn=jnp.sum(acc_f[:, 1], 0)[None],
        b_merge=jnp.sum(acc_mg[:, 0], 0)[None],
        dn_a_log=acc_ab[0, :nd].reshape(1, 2, H), dn_dt_bias=acc_ab[1, :nd].reshape(1, 2, H),
        dn_onorm=jnp.sum(acc_on[:, 0], 0)[None],
        lru_conv_b=jnp.sum(acc_lc[:, 4], 0)[None],
        lru_w_rg=jnp.stack([diag_blocks(dwbd[0]), diag_blocks(dwbd[2])])[None],
        lru_w_ig=jnp.stack([diag_blocks(dwbd[1]), diag_blocks(dwbd[3])])[None],
        ffn_dw_b=jnp.sum(acc_ffn[:, 9], 0)[None])
    mat_names = ["lru_w_rg", "lru_w_ig"]
    vec_names = [n for n in g_rep if n not in mat_names]
    vec_parts, mat_parts = _all_gather([_pack([g_rep[n] for n in vec_names]),
                                        _pack([g_rep[n] for n in mat_names]).astype(BF16)], "ag_rep_grads")
    for grp, parts, nm in ((vec_names, vec_parts, "adamw_rep_vec"), (mat_names, mat_parts, "adamw_rep_mat")):
        out4 = _adamw(_pack([params[n] for n in grp]), _pack([mom1[n] for n in grp]),
                      _pack([mom2[n] for n in grp]), parts, nm)
        for k, buf in enumerate(out4):
            for n, arr in zip(grp, _unpack(buf, [params[n].shape for n in grp])):
                results.setdefault(n, [None] * 4)[k] = arr

    ba_out = _adamw(_pack([b_ada]), _pack([m_b_ada]), _pack([v_b_ada]),
                    _pack([dm_g.reshape(N_DEV * SUBLANES, 6 * D)], lead=(N_DEV * SUBLANES,)), "adamw_b_ada")
    results["b_ada"] = [_unpack(buf, [b_ada.shape])[0] for buf in ba_out]

    wa_out = _adamw(w_ada, m_w_ada, v_w_ada, gw_ada[None], "adamw_w_ada")
    results["w_ada"] = list(wa_out)

    mix_x, mix_land = _split_wait(q_late, "quad", wa_out[0], "rs_quad_mix_wait")
    recv4 = [own_slab(land, x4) for land, x4 in zip(mix_land, mix_x)]
    for n, g4 in zip(["w_in", "w_branch_dn", "w_branch_lru", "w_out"], recv4[:-1]):
        if n == "w_in":
            g4 = jnp.swapaxes(g4, 1, 2)
        results[n] = list(_adamw(params[n], mom1[n], mom2[n], g4, "adamw_" + n))
    sm_out = _adamw(_pack([params[n] for n in sm_names]), _pack([mom1[n] for n in sm_names]),
                    _pack([mom2[n] for n in sm_names]), recv4[-1], "adamw_small")
    for k, buf in enumerate(sm_out):
        for n, arr in zip(sm_names, _unpack(buf, [params[n].shape for n in sm_names])):
            results.setdefault(n, [None] * 4)[k] = arr

    outs = [loss, grad_x]
    for k in range(4):
        outs += [results[n][k] for n in names]
    return tuple(outs)
```

```python
import math

import jax
import jax.numpy as jnp
from jax import lax
from jax.experimental import pallas as pl
from jax.experimental.pallas import tpu as pltpu

F32 = jnp.float32
BF16 = jnp.bfloat16

EPS = 1e-6
GRID_W = 64
CHUNK = 64
LRU_C = 8.0
N_DEV = 8
ADAM_LR = 0.001
ADAM_B1 = 0.9
ADAM_B2 = 0.999
ADAM_EPS = 1e-08
ADAM_WD = 0.01
ADAM_STEP = 10

LANES = 128
SUBLANES = 8
VMEM_LIMIT = 48 * 1024 * 1024
MESH_AXES = ("x", "y", "c")
GELU_C = math.sqrt(2.0 / math.pi)


def _cparams(sem=None, vmem=None):
    kw = {}
    if sem is not None:
        kw["dimension_semantics"] = sem
    if vmem is not None:
        kw["vmem_limit_bytes"] = vmem
    return pltpu.CompilerParams(**kw)


def _tile(n, pref):
    if n <= pref:
        return n
    t = (pref // LANES) * LANES
    while n % t:
        t -= LANES
    return t


def _sigmoid(x):
    return 1.0 / (1.0 + jnp.exp(-x))


def _silu(x):
    return x * _sigmoid(x)


def _dsilu(x):
    s = _sigmoid(x)
    return s * (1.0 + x * (1.0 - s))


def _softplus(x):
    return jnp.maximum(x, 0.0) + jnp.log(1.0 + jnp.exp(-jnp.abs(x)))


def _gelu(x):
    return 0.5 * x * (1.0 + jnp.tanh(GELU_C * (x + 0.044715 * x * x * x)))


def _dgelu(x):
    t = jnp.tanh(GELU_C * (x + 0.044715 * x * x * x))
    return 0.5 * (1.0 + t) + 0.5 * x * (1.0 - t * t) * GELU_C * (1.0 + 3.0 * 0.044715 * x * x)


def _dot(a, b, dims="nn"):
    dn = {"nn": (((1,), (0,)), ((), ())),
          "nt": (((1,), (1,)), ((), ())),
          "tn": (((0,), (0,)), ((), ()))}[dims]
    return lax.dot_general(a.astype(BF16), b.astype(BF16), dn, preferred_element_type=F32)


def _split2(x):
    hi = x.astype(BF16)
    lo = (x - hi.astype(F32)).astype(BF16)
    return hi, lo


def _split3(x):
    h1 = x.astype(BF16)
    r1 = x - h1.astype(F32)
    h2 = r1.astype(BF16)
    h3 = (r1 - h2.astype(F32)).astype(BF16)
    return h1, h2, h3


def _dot_hi(a, b):
    ah, al = _split2(a)
    bh, bl = _split2(b)
    return _dot(ah, bh) + (_dot(ah, bl) + _dot(al, bh))


def _dot_mask(m, x, dims="nn"):
    h1, h2, h3 = _split3(x)
    if dims == "xtn":
        return _dot(h1, m, "tn") + (_dot(h2, m, "tn") + _dot(h3, m, "tn"))
    return _dot(m, h1, dims) + (_dot(m, h2, dims) + _dot(m, h3, dims))


def _my_index():
    return 4 * lax.axis_index("x") + 2 * lax.axis_index("y") + lax.axis_index("c")


def _hbm_call(body, xs, out_shapes, n_sems, name):
    any_spec = pl.BlockSpec(memory_space=pl.ANY)
    return pl.pallas_call(
        body, name=name, out_shape=tuple(out_shapes),
        in_specs=[any_spec] * len(xs), out_specs=tuple([any_spec] * len(out_shapes)),
        scratch_shapes=[pltpu.SemaphoreType.DMA((n_sems,)), pltpu.SemaphoreType.DMA((n_sems,)),
                        pltpu.SemaphoreType.DMA((len(xs),))],
    )(*xs)


def _all_gather(xs, name):
    n = len(xs)

    def body(*refs):
        x_refs, out_refs = refs[:n], refs[n:2 * n]
        send_sems, recv_sems, local_sems = refs[2 * n:]
        x_, y_, c_ = lax.axis_index("x"), lax.axis_index("y"), lax.axis_index("c")
        me, sibling = (x_, y_, c_), (x_, y_, 1 - c_)
        chips = [(1 - x_, y_), (x_, 1 - y_), (1 - x_, 1 - y_)]

        def slab(a, px, py, pc):
            return out_refs[a].at[4 * px + 2 * py + pc]

        def copy(a, k, block, to, src=None):
            return pltpu.make_async_remote_copy(
                src_ref=slab(a, *block) if src is None else src, dst_ref=slab(a, *block),
                send_sem=send_sems.at[7 * a + k], recv_sem=recv_sems.at[7 * a + k],
                device_id=to, device_id_type=pl.DeviceIdType.MESH)

        mine = [pltpu.make_async_copy(x_refs[a], slab(a, *me), local_sems.at[a]) for a in range(n)]
        for cp in mine:
            cp.start()
        sent = []
        for j, chip in enumerate(chips):
            sent += [copy(a, 1 + j, me, (*chip, c_), src=x_refs[a]) for a in range(n)]
        sent += [copy(a, 0, me, sibling, src=x_refs[a]) for a in range(n)]
        for cp in sent:
            cp.start()
        for j, chip in enumerate(chips):
            for a in range(n):
                copy(a, 1 + j, (*chip, c_), me).wait_recv()
                fwd = copy(a, 4 + j, (*chip, c_), sibling)
                fwd.start()
                sent.append(fwd)
        for a in range(n):
            copy(a, 0, sibling, me).wait_recv()
        for j, chip in enumerate(chips):
            for a in range(n):
                copy(a, 4 + j, (*chip, 1 - c_), me).wait_recv()
        for cp in sent:
            cp.wait_send()
        for cp in mine:
            cp.wait()

    outs = [jax.ShapeDtypeStruct((N_DEV,) + x.shape, x.dtype) for x in xs]
    return _hbm_call(body, xs, outs, 7 * n, name)


SPLIT_COPIES = {"gather": 7, "quad": 3, "pair": 4}


def _split_views(kind, x_ref, land_ref, k):
    x_, y_, c_ = lax.axis_index("x"), lax.axis_index("y"), lax.axis_index("c")
    if kind == "pair":
        return x_ref.at[2 * (k - 1) + (1 - c_)], land_ref.at[k - 1], land_ref.at[k - 1], (x_, y_, 1 - c_)
    if kind == "gather":
        px = 1 - x_ if (k >> 2) & 1 else x_
        py = 1 - y_ if (k >> 1) & 1 else y_
        pc = 1 - c_ if k & 1 else c_
        return x_ref, land_ref.at[4 * x_ + 2 * y_ + c_], land_ref.at[4 * px + 2 * py + pc], (px, py, pc)
    px = 1 - x_ if (k >> 1) & 1 else x_
    py = 1 - y_ if k & 1 else y_
    return x_ref.at[2 * px + py], land_ref.at[2 * x_ + y_], land_ref.at[2 * px + py], (px, py, c_)


def _split_start(xs, kind, name):
    n = len(xs)
    npeer = SPLIT_COPIES[kind]
    land_shape = {"gather": lambda x: (N_DEV,) + x.shape, "quad": lambda x: x.shape, "pair": lambda x: (4,) + x.shape[1:]}
    lands = [lax.empty(land_shape[kind](x), x.dtype) for x in xs]

    def body(*refs):
        x_refs, land_refs = refs[:n], refs[n:2 * n]
        send_sems, recv_sems, token = refs[2 * n], refs[2 * n + 1], refs[4 * n + 2]
        for k in range(1, npeer + 1):
            for a in range(n):
                src, dst, _, peer = _split_views(kind, x_refs[a], land_refs[a], k)
                pltpu.make_async_remote_copy(
                    src_ref=src, dst_ref=dst, send_sem=send_sems.at[npeer * a + k - 1],
                    recv_sem=recv_sems.at[npeer * a + k - 1],
                    device_id=peer, device_id_type=pl.DeviceIdType.MESH).start()
        token[...] = jnp.zeros_like(token)

    hbm = pl.BlockSpec(memory_space=pltpu.HBM)
    sem = pl.BlockSpec(memory_space=pltpu.SEMAPHORE)
    sems = pltpu.SemaphoreType.DMA((npeer * n,))
    out = pl.pallas_call(
        body, name=name,
        out_shape=(sems, sems, *[pltpu.HBM(a.shape, a.dtype) for a in list(xs) + lands],
                   jax.ShapeDtypeStruct((SUBLANES, LANES), F32)),
        in_specs=[hbm] * (2 * n),
        out_specs=(sem, sem, *[hbm] * (2 * n), pl.BlockSpec(memory_space=pltpu.VMEM)),
        input_output_aliases={i: 2 + i for i in range(2 * n)},
        compiler_params=pltpu.CompilerParams(has_side_effects=pltpu.SideEffectType.DATAFLOW_SIDE_EFFECTING),
    )(*[pltpu.with_memory_space_constraint(a, pltpu.HBM) for a in list(xs) + lands])
    return out[0], out[1], list(out[2:2 + n]), list(out[2 + n:2 + 2 * n]), out[-1]


def _split_wait(started, kind, after, name):
    send_sems_, recv_sems_, x_thru, land_thru, _ = started
    n = len(x_thru)
    npeer = SPLIT_COPIES[kind]

    def body(*refs):
        x_refs, land_refs = refs[:n], refs[n:2 * n]
        send_sems, recv_sems = refs[2 * n], refs[2 * n + 1]
        for k in range(1, npeer + 1):
            for a in range(n):
                src, _, landed, peer = _split_views(kind, x_refs[a], land_refs[a], k)
                cp = pltpu.make_async_remote_copy(
                    src_ref=src, dst_ref=landed, send_sem=send_sems.at[npeer * a + k - 1],
                    recv_sem=recv_sems.at[npeer * a + k - 1],
                    device_id=peer, device_id_type=pl.DeviceIdType.MESH)
                cp.wait_send()
                cp.wait_recv()

    hbm = pl.BlockSpec(memory_space=pltpu.HBM)
    sem = pl.BlockSpec(memory_space=pltpu.SEMAPHORE)
    out = pl.pallas_call(
        body, name=name,
        out_shape=tuple(pltpu.HBM(a.shape, a.dtype) for a in x_thru + land_thru),
        in_specs=[hbm] * (2 * n) + [sem, sem, pl.BlockSpec(memory_space=pl.ANY)],
        out_specs=tuple([hbm] * (2 * n)),
        input_output_aliases={i: i for i in range(2 * n)},
        compiler_params=pltpu.CompilerParams(has_side_effects=pltpu.SideEffectType.DATAFLOW_SIDE_EFFECTING),
    )(*x_thru, *land_thru, send_sems_, recv_sems_, after)
    return list(out[:n]), list(out[n:])


def _pair_sum(x8, r4, name):
    _, r, c = x8.shape
    tr = _row_tile(r, c * 12)

    def body(core_ref, x_ref, r_ref, o_ref):
        o_ref[...] = (x_ref[...].astype(F32) + r_ref[...].astype(F32)).astype(o_ref.dtype)

    core = lax.axis_index("c").astype(jnp.int32).reshape(1)
    return pl.pallas_call(
        body, name=name,
        grid_spec=pltpu.PrefetchScalarGridSpec(
            num_scalar_prefetch=1, grid=(4, r // tr),
            in_specs=[pl.BlockSpec((None, tr, c), lambda j, i, core_ref: (2 * j + core_ref[0], i, 0)),
                      pl.BlockSpec((None, tr, c), lambda j, i, core_ref: (j, i, 0))],
            out_specs=pl.BlockSpec((None, tr, c), lambda j, i, core_ref: (j, i, 0))),
        out_shape=jax.ShapeDtypeStruct((4, r, c), x8.dtype),
        compiler_params=_cparams(("parallel", "parallel"), VMEM_LIMIT))(core, x8, r4)


def _row_tile(r, bytes_per_row, budget=4 * 1024 * 1024):
    best = None
    for t in range(16, r + 1, 16):
        if r % t == 0 and t * bytes_per_row <= budget:
            best = t
    return best if best is not None else r


def _mm(a, b, dims, name, out_dtype=F32, out_shards=1, tm=1024, tn=1024, tk=1024):
    S = b.shape[0] if b.ndim == 3 else 1
    bshape = (b.shape[1], S * b.shape[2]) if b.ndim == 3 else b.shape
    if dims == "nn":
        (M, K), (K2, N) = a.shape, bshape
    elif dims == "nt":
        (M, K), (N, K2) = a.shape, bshape
    else:
        (K, M), (K2, N) = a.shape, bshape
    assert K == K2, (a.shape, b.shape, dims)
    cs = bshape[1] // S
    tm, tk = _tile(M, tm), _tile(K, min(tk, cs) if (S > 1 and dims == "nt") else tk)
    tn = _tile(N, min(tn, cs) if (S > 1 and dims != "nt") else min(tn, N // out_shards))
    assert cs % (tk if dims == "nt" else tn) == 0 and (N // out_shards) % tn == 0
    nk = K // tk

    def body(a_ref, b_ref, o_ref, acc):
        k = pl.program_id(2)

        @pl.when(k == 0)
        def _():
            acc[...] = jnp.zeros_like(acc)

        acc[...] += _dot(a_ref[...], b_ref[...], dims)

        @pl.when(k == nk - 1)
        def _():
            o_ref[...] = acc[...].astype(out_dtype)

    a_spec =(pl.BlockSpec((tk, tm), lambda i, j, k: (k, i)) if dims == "tn"
              else pl.BlockSpec((tm, tk), lambda i, j, k: (i, k)))
    if S == 1:
        b_spec = (pl.BlockSpec((tn, tk), lambda i, j, k: (j, k)) if dims == "nt"
                  else pl.BlockSpec((tk, tn), lambda i, j, k: (k, j)))
    elif dims == "nt":
        per = cs // tk
        b_spec = pl.BlockSpec((None, tn, tk), lambda i, j, k: (k // per, j, k % per))
    else:
        per = cs // tn
        b_spec = pl.BlockSpec((None, tk, tn), lambda i, j, k: (j // per, k, j % per))
    if out_shards == 1:
        out_spec, out_shape = pl.BlockSpec((tm, tn), lambda i, j, k: (i, j)), (M, N)
    else:
        oper = N // out_shards // tn
        out_spec = pl.BlockSpec((None, tm, tn), lambda i, j, k: (j // oper, i, j % oper))
        out_shape = (out_shards, M, N // out_shards)
    return pl.pallas_call(
        body, name=name, grid=(M // tm, N // tn, nk),
        in_specs=[a_spec, b_spec], out_specs=out_spec,
        out_shape=jax.ShapeDtypeStruct(out_shape, out_dtype),
        scratch_shapes=[pltpu.VMEM((tm, tn), F32)],
        compiler_params=_cparams(("parallel", "parallel", "arbitrary"), VMEM_LIMIT),
    )(a, b)


VMEM_LIMIT_SUM = 56 * 1024 * 1024


def _mm_sum(pairs, name, out_dtype=F32, tm=768, tn=1024, tk=1024):
    M, N = pairs[0][0].shape[0], pairs[0][1].shape[1]
    tm, tn = _tile(M, tm), _tile(N, tn)
    tks = [_tile(a.shape[1], tk) for a, _ in pairs]
    nks = [a.shape[1] // t for (a, _), t in zip(pairs, tks)]
    starts = [sum(nks[:g]) for g in range(len(pairs))]
    nk = sum(nks)
    G = len(pairs)

    def body(*refs):
        o_ref, acc = refs[2 * G], refs[2 * G + 1]
        k = pl.program_id(2)

        @pl.when(k == 0)
        def _():
            acc[...] = jnp.zeros_like(acc)

        for g in range(G):
            @pl.when(jnp.logical_and(k >= starts[g], k < starts[g] + nks[g]))
            def _(g=g):
                acc[...] += _dot(refs[2 * g][...], refs[2 * g + 1][...])

        @pl.when(k == nk - 1)
        def _():
            o_ref[...] = acc[...].astype(out_dtype)

    in_specs, args = [], []
    for g, (a, b) in enumerate(pairs):
        kk = lambda k, g=g: jnp.clip(k - starts[g], 0, nks[g] - 1)
        in_specs += [pl.BlockSpec((tm, tks[g]), lambda i, j, k, kk=kk: (i, kk(k))),
                     pl.BlockSpec((tks[g], tn), lambda i, j, k, kk=kk: (kk(k), j))]
        args += [a, b]
    return pl.pallas_call(
        body, name=name, grid=(M // tm, N // tn, nk),
        in_specs=in_specs, out_specs=pl.BlockSpec((tm, tn), lambda i, j, k: (i, j)),
        out_shape=jax.ShapeDtypeStruct((M, N), out_dtype),
        scratch_shapes=[pltpu.VMEM((tm, tn), F32)],
        compiler_params=_cparams(("parallel", "parallel", "arbitrary"), VMEM_LIMIT_SUM),
    )(*args)


def _ada_fwd(ccp, w, b):
    def body(c_ref, w_ref, b_ref, o_ref):
        o_ref[...] = _dot(_silu(c_ref[...]), w_ref[...]) + b_ref[...]

    return pl.pallas_call(
        body, name="ada_fwd", out_shape=jax.ShapeDtypeStruct((ccp.shape[0], w.shape[1]), F32),
        compiler_params=_cparams(None, VMEM_LIMIT))(ccp, w, b)


def _ada_bwd(ccp, w, g, cctx, n_loc):
    def body(c_ref, w_ref, g_ref, cc_ref, gw_ref, dc_ref):
        gw_ref[...] = _dot(_silu(c_ref[...]), g_ref[...], "tn")
        dcc = _dot(g_ref[...], w_ref[...], "nt")
        row = lax.broadcasted_iota(jnp.int32, dcc.shape, 0)
        part = jnp.sum(jnp.where(row % SUBLANES == n_loc, dcc, 0.0), axis=0, keepdims=True)
        dc_ref[...] = jnp.broadcast_to(part * _dsilu(cc_ref[...]), dc_ref.shape)

    D = ccp.shape[1]
    return pl.pallas_call(
        body, name="ada_bwd",
        out_shape=(jax.ShapeDtypeStruct(w.shape, F32), jax.ShapeDtypeStruct((SUBLANES, D), F32)),
        compiler_params=_cparams(None, VMEM_LIMIT))(ccp, w, g, cctx)


def _norm_mod_fwd(h, gain, mods, i_shift, i_scale, nct, tm, name):
    B, T, D = h.shape

    def body(h_ref, g_ref, m_ref, u_ref):
        x = h_ref[0]
        r = lax.rsqrt(jnp.mean(x * x, axis=-1, keepdims=True) + EPS)
        n = x * r * g_ref[...]
        u_ref[0] = (n * (1.0 + m_ref[0, i_scale:i_scale + 1, :]) + m_ref[0, i_shift:i_shift + 1, :]).astype(BF16)

    return pl.pallas_call(
        body, name=name, grid=(B, T // tm),
        in_specs=[pl.BlockSpec((1, tm, D), lambda b, t: (b, t, 0)),
                  pl.BlockSpec((1, D), lambda b, t: (0, 0)),
                  pl.BlockSpec((1, SUBLANES, D), lambda b, t: (jnp.where(t < nct, B, b), 0, 0))],
        out_specs=pl.BlockSpec((1, tm, D), lambda b, t: (b, t, 0)),
        out_shape=jax.ShapeDtypeStruct((B, T, D), BF16),
        compiler_params=_cparams(("parallel", "parallel"), VMEM_LIMIT),
    )(h, gain, mods)


def _norm_mod_bwd(h, du, gain, mods, i_scale, nct, tm, res, name):
    B, T, D = h.shape
    nt = T // tm

    def body(h_ref, du_ref, g_ref, m_ref, res_ref, dx_ref, ax_ref, ac_ref):
        t = pl.program_id(1)
        x = h_ref[0]
        r = lax.rsqrt(jnp.mean(x * x, axis=-1, keepdims=True) + EPS)
        nh = x * r
        g = g_ref[...]
        du_ = du_ref[0]
        dn = du_ * (1.0 + m_ref[0, i_scale:i_scale + 1, :])
        dnh = dn * g
        dx = r * (dnh - nh * jnp.mean(dnh * nh, axis=-1, keepdims=True))
        sums = (jnp.sum(du_, axis=0, keepdims=True), jnp.sum(du_ * nh * g, axis=0, keepdims=True),
                jnp.sum(dn * nh, axis=0, keepdims=True))

        @pl.when(t == 0)
        def _():
            ax_ref[...] = jnp.zeros_like(ax_ref)
            ac_ref[...] = jnp.zeros_like(ac_ref)

        def accumulate(ref):
            for i, s in enumerate(sums):
                ref[0, i:i + 1, :] += s

        @pl.when(t < nct)
        def _():
            accumulate(ac_ref)

        @pl.when(t >= nct)
        def _():
            accumulate(ax_ref)
            dx_ref[0] = dx + res_ref[0]

    lat = lambda b, t: (b, jnp.maximum(t - nct, 0), 0)
    acc = pl.BlockSpec((1, SUBLANES, D), lambda b, t: (b, 0, 0))
    return pl.pallas_call(
        body, name=name, grid=(B, nt),
        in_specs=[pl.BlockSpec((1, tm, D), lambda b, t: (b, t, 0)),
                  pl.BlockSpec((1, tm, D), lambda b, t: (b, t, 0)),
                  pl.BlockSpec((1, D), lambda b, t: (0, 0)),
                  pl.BlockSpec((1, SUBLANES, D), lambda b, t: (jnp.where(t < nct, B, b), 0, 0)),
                  pl.BlockSpec((1, tm, D), lat)],
        out_specs=(pl.BlockSpec((1, tm, D), lat), acc, acc),
        out_shape=(jax.ShapeDtypeStruct(res.shape, F32),
                   jax.ShapeDtypeStruct((B, SUBLANES, D), F32), jax.ShapeDtypeStruct((B, SUBLANES, D), F32)),
        compiler_params=_cparams(("parallel", "arbitrary"), VMEM_LIMIT),
    )(h, du, gain, mods, res)


def _resid_norm_fwd(x, y, gain, mods, i_gate, tm, name):
    B, N, D = x.shape

    def body(x_ref, y_ref, g_ref, m_ref, o_ref):
        y_ = y_ref[0]
        r = lax.rsqrt(jnp.mean(y_ * y_, axis=-1, keepdims=True) + EPS)
        o_ref[0] = x_ref[0] + y_ * r * g_ref[...] * m_ref[0, i_gate:i_gate + 1, :]

    row = pl.BlockSpec((1, tm, D), lambda b, t: (b, t, 0))
    return pl.pallas_call(
        body, name=name, grid=(B, N // tm),
        in_specs=[row, row, pl.BlockSpec((1, D), lambda b, t: (0, 0)),
                  pl.BlockSpec((1, SUBLANES, D), lambda b, t: (b, 0, 0))],
        out_specs=row, out_shape=jax.ShapeDtypeStruct((B, N, D), F32),
        compiler_params=_cparams(("parallel", "parallel"), VMEM_LIMIT),
    )(x, y, gain, mods)


def _resid_norm_bwd_math(y_, dh, g, gate):
    r = lax.rsqrt(jnp.mean(y_ * y_, axis=-1, keepdims=True) + EPS)
    nh = y_ * r
    dgate = jnp.sum(dh * nh * g, axis=0, keepdims=True)
    dn = dh * gate
    dgain = jnp.sum(dn * nh, axis=0, keepdims=True)
    dnh = dn * g
    dy = r * (dnh - nh * jnp.mean(dnh * nh, axis=-1, keepdims=True))
    return dy, dgate, dgain


def _resid_norm_bwd(y, dh, gain, mods, i_gate, tm, name):
    B, N, D = y.shape

    def body(y_ref, dh_ref, g_ref, m_ref, dy_ref, acc_ref):
        t = pl.program_id(1)
        dy, dgate, dgain = _resid_norm_bwd_math(y_ref[0], dh_ref[0], g_ref[...], m_ref[0, i_gate:i_gate + 1, :])
        dy_ref[0] = dy.astype(BF16)

        @pl.when(t == 0)
        def _():
            acc_ref[...] = jnp.zeros_like(acc_ref)

        acc_ref[0, 0:1, :] += dgate
        acc_ref[0, 1:2, :] += dgain

    row = pl.BlockSpec((1, tm, D), lambda b, t: (b, t, 0))
    return pl.pallas_call(
        body, name=name, grid=(B, N // tm),
        in_specs=[row, row, pl.BlockSpec((1, D), lambda b, t: (0, 0)),
                  pl.BlockSpec((1, SUBLANES, D), lambda b, t: (b, 0, 0))],
        out_specs=(row, pl.BlockSpec((1, SUBLANES, D), lambda b, t: (b, 0, 0))),
        out_shape=(jax.ShapeDtypeStruct((B, N, D), BF16), jax.ShapeDtypeStruct((B, SUBLANES, D), F32)),
        compiler_params=_cparams(("parallel", "arbitrary"), VMEM_LIMIT),
    )(y, dh, gain, mods)


def _final_fwd_bwd(h1, d, gain, mods, i_gate, tgt, tm):
    B, N, D = h1.shape

    def body(h_ref, d_ref, g_ref, m_ref, t_ref, dd_ref, dh_ref, acc_ref):
        t = pl.program_id(1)
        d_ = d_ref[0]
        g = g_ref[...]
        gate = m_ref[0, i_gate:i_gate + 1, :]
        r = lax.rsqrt(jnp.mean(d_ * d_, axis=-1, keepdims=True) + EPS)
        e = h_ref[0] + d_ * r * g * gate - t_ref[0]
        dh = e * (1.0 / D)
        dh_ref[0] = dh
        dy, dgate, dgain = _resid_norm_bwd_math(d_, dh, g, gate)
        dd_ref[0] = dy.astype(BF16)

        @pl.when(t == 0)
        def _():
            acc_ref[...] = jnp.zeros_like(acc_ref)

        acc_ref[0, 0:1, :] += dgate
        acc_ref[0, 1:2, :] += dgain
        acc_ref[0, 2:3, :] += jnp.sum(e * e, axis=0, keepdims=True)

    row = pl.BlockSpec((1, tm, D), lambda b, t: (b, t, 0))
    return pl.pallas_call(
        body, name="final_fwd_bwd", grid=(B, N // tm),
        in_specs=[row, row, pl.BlockSpec((1, D), lambda b, t: (0, 0)),
                  pl.BlockSpec((1, SUBLANES, D), lambda b, t: (b, 0, 0)), row],
        out_specs=(row, row, pl.BlockSpec((1, SUBLANES, D), lambda b, t: (b, 0, 0))),
        out_shape=(jax.ShapeDtypeStruct((B, N, D), BF16), jax.ShapeDtypeStruct((B, N, D), F32),
                   jax.ShapeDtypeStruct((B, SUBLANES, D), F32)),
        compiler_params=_cparams(("parallel", "arbitrary"), VMEM_LIMIT),
    )(h1, d, gain, mods, tgt)


def _shifted(x, prev, nxt, off, row, tm):
    if off == 0:
        return x
    if off < 0:
        y = pltpu.roll(x, -off, 0)
        for r in range(-off):
            y = jnp.where(row == r, prev[SUBLANES + r + off:SUBLANES + r + off + 1, :], y)
        return y
    y = pltpu.roll(x, tm - off, 0)
    for r in range(off):
        y = jnp.where(row == tm - off + r, nxt[r:r + 1, :], y)
    return y


def _halo_specs(T, tm, tc, cb0, order):
    r8, n8 = tm // SUBLANES, T // SUBLANES
    if order == "btj":
        prev = lambda b, t, j: (b, jnp.maximum(t * r8 - 1, 0), cb0 + j)
        nxt = lambda b, t, j: (b, jnp.minimum((t + 1) * r8, n8 - 1), cb0 + j)
    else:
        prev = lambda b, j, t: (b, jnp.maximum(t * r8 - 1, 0), cb0 + j)
        nxt = lambda b, j, t: (b, jnp.minimum((t + 1) * r8, n8 - 1), cb0 + j)
    return pl.BlockSpec((1, SUBLANES, tc), prev), pl.BlockSpec((1, SUBLANES, tc), nxt)


def _seg_halos(p_ref, n_ref, t, nct, nt):
    seg_start = jnp.logical_or(t == 0, t == nct)
    seg_end = jnp.logical_or(t == nct - 1, t == nt - 1)
    prev = jnp.where(seg_start, 0.0, p_ref[0])
    nxt = jnp.where(seg_end, 0.0, n_ref[0])
    return prev, nxt


def _dwconv(x, col0, C, w, bias, offs, nct, tm, tc, name, out_dtype=F32, qkv_heads=None):
    B, T, _ = x.shape
    nt = T // tm
    cb0 = col0 // tc
    if qkv_heads is not None:
        assert tc == qkv_heads[0] * qkv_heads[1] and C == 3 * tc

    def body(*refs):
        refs = list(refs)
        a_ref = refs.pop() if qkv_heads is not None else None
        if bias is None:
            x_ref, p_ref, n_ref, w_ref, o_ref = refs
        else:
            x_ref, p_ref, n_ref, w_ref, b_ref, o_ref = refs
        t = pl.program_id(1)
        prev, nxt = _seg_halos(p_ref, n_ref, t, nct, nt)
        x_ = x_ref[0]
        row = lax.broadcasted_iota(jnp.int32, x_.shape, 0)
        acc = None
        for j, off in enumerate(offs):
            term = _shifted(x_, prev, nxt, off, row, tm) * w_ref[j:j + 1, :]
            acc = term if acc is None else acc + term
        if bias is not None:
            acc = acc + b_ref[...]
        o_ref[0] = acc.astype(out_dtype)
        if qkv_heads is not None:
            H, hd = qkv_heads
            part = pl.program_id(2)
            for h in range(H):
                sl = slice(h * hd, (h + 1) * hd)
                s = _silu(acc[:, sl])
                rs = lax.rsqrt(jnp.sum(s * s, axis=-1, keepdims=True) + EPS)
                scale = jnp.where(part == 0, rs * (float(hd) ** -0.5), jnp.where(part == 1, rs, 1.0))
                a_ref[0, :, sl] = (s * scale).astype(a_ref.dtype)

    pspec, nspec = _halo_specs(T, tm, tc, cb0, "btj")
    in_specs = [pl.BlockSpec((1, tm, tc), lambda b, t, j: (b, t, cb0 + j)), pspec, nspec,
                pl.BlockSpec((w.shape[0], tc), lambda b, t, j: (0, j))]
    args = [x, x, x, w]
    if bias is not None:
        in_specs.append(pl.BlockSpec((1, tc), lambda b, t, j: (0, j)))
        args.append(bias)
    tile = pl.BlockSpec((1, tm, tc), lambda b, t, j: (b, t, j))
    out_specs, out_shape = tile, jax.ShapeDtypeStruct((B, T, C), out_dtype)
    if qkv_heads is not None:
        out_specs, out_shape = (tile, tile), (out_shape, jax.ShapeDtypeStruct((B, T, C), BF16))
    return pl.pallas_call(
        body, name=name, grid=(B, nt, C // tc),
        in_specs=in_specs, out_specs=out_specs, out_shape=out_shape,
        compiler_params=_cparams(("parallel", "parallel", "parallel"), VMEM_LIMIT),
    )(*args)


def _dwconv_wgrad(dy, x, col0, C, offs, nct, tm, tc, name):
    B, T, _ = x.shape
    nt = T // tm
    cb0 = col0 // tc
    K = len(offs)

    def body(dy_ref, x_ref, p_ref, n_ref, acc_ref):
        t = pl.program_id(2)
        prev, nxt = _seg_halos(p_ref, n_ref, t, nct, nt)
        x_ = x_ref[0]
        dy_ = dy_ref[0]
        row = lax.broadcasted_iota(jnp.int32, x_.shape, 0)

        @pl.when(t == 0)
        def _():
            acc_ref[...] = jnp.zeros_like(acc_ref)

        for j, off in enumerate(offs):
            acc_ref[0, j:j + 1, :] += jnp.sum(dy_ * _shifted(x_, prev, nxt, off, row, tm), axis=0, keepdims=True)
        acc_ref[0, K:K + 1, :] += jnp.sum(dy_, axis=0, keepdims=True)

    pspec, nspec = _halo_specs(T, tm, tc, cb0, "bjt")
    return pl.pallas_call(
        body, name=name, grid=(B, C // tc, nt),
        in_specs=[pl.BlockSpec((1, tm, tc), lambda b, j, t: (b, t, j)),
                  pl.BlockSpec((1, tm, tc), lambda b, j, t: (b, t, cb0 + j)), pspec, nspec],
        out_specs=pl.BlockSpec((1, SUBLANES, tc), lambda b, j, t: (b, 0, j)),
        out_shape=jax.ShapeDtypeStruct((B, SUBLANES, C), F32),
        compiler_params=_cparams(("parallel", "parallel", "arbitrary"), VMEM_LIMIT),
    )(dy, x, x, x)


def _ab_act(pab, alog, dtb, nd, tm):
    R = pab.shape[0]

    def body(p_ref, al_ref, dt_ref, o_ref):
        p = p_ref[...]
        lane = lax.broadcasted_iota(jnp.int32, p.shape, 1)
        g = -jnp.exp(al_ref[...]) * _softplus(p + dt_ref[...])
        o_ref[...] = jnp.where(lane < nd, g, jnp.where(lane < 2 * nd, _sigmoid(p), 0.0))

    row = pl.BlockSpec((tm, LANES), lambda i: (i, 0))
    vec = pl.BlockSpec((1, LANES), lambda i: (0, 0))
    return pl.pallas_call(
        body, name="ab_act", grid=(R // tm,), in_specs=[row, vec, vec], out_specs=row,
        out_shape=jax.ShapeDtypeStruct((R, LANES), F32),
        compiler_params=_cparams(("parallel",), VMEM_LIMIT))(pab, alog, dtb)


def _ab_act_bwd(pab, gb, dgb, alog, dtb, nd, tm):
    R = pab.shape[0]

    def body(p_ref, gb_ref, d0_ref, d1_ref, al_ref, dt_ref, o_ref, acc_ref):
        i = pl.program_id(0)
        p = p_ref[...]
        gb_ = gb_ref[...]
        d_ = d0_ref[...] + d1_ref[...]
        lane = lax.broadcasted_iota(jnp.int32, p.shape, 1)
        is_g = lane < nd
        is_b = jnp.logical_and(lane >= nd, lane < 2 * nd)
        da = jnp.where(is_g, d_ * (-jnp.exp(al_ref[...])) * _sigmoid(p + dt_ref[...]), 0.0)
        db = jnp.where(is_b, d_ * gb_ * (1.0 - gb_), 0.0)
        o_ref[...] = (da + db).astype(BF16)

        @pl.when(i == 0)
        def _():
            acc_ref[...] = jnp.zeros_like(acc_ref)

        acc_ref[0:1, :] += jnp.sum(jnp.where(is_g, d_ * gb_, 0.0), axis=0, keepdims=True)
        acc_ref[1:2, :] += jnp.sum(da, axis=0, keepdims=True)

    row = pl.BlockSpec((tm, LANES), lambda i: (i, 0))
    vec = pl.BlockSpec((1, LANES), lambda i: (0, 0))
    return pl.pallas_call(
        body, name="ab_act_bwd", grid=(R // tm,),
        in_specs=[row, row, row, row, vec, vec],
        out_specs=(row, pl.BlockSpec((SUBLANES, LANES), lambda i: (0, 0))),
        out_shape=(jax.ShapeDtypeStruct((R, LANES), BF16), jax.ShapeDtypeStruct((SUBLANES, LANES), F32)),
        compiler_params=_cparams(("arbitrary",), VMEM_LIMIT))(pab, gb, dgb[0], dgb[1], alog, dtb)


def _dn_act_bwd(cv, dqkv, H, hd, tm):
    B, T, C3 = cv.shape
    qscale = float(hd) ** -0.5

    def body(c_ref, d0_ref, d1_ref, o_ref):
        part = pl.program_id(0)
        for h in range(H):
            sl = slice(h * hd, (h + 1) * hd)
            c = c_ref[0, :, sl]
            s = _silu(c)
            dout = d0_ref[0, :, sl] + d1_ref[0, :, sl]
            rs = lax.rsqrt(jnp.sum(s * s, axis=-1, keepdims=True) + EPS)
            sh = s * rs
            dnorm = rs * (dout - sh * jnp.sum(dout * sh, axis=-1, keepdims=True))
            ds = jnp.where(part == 0, dnorm * qscale, jnp.where(part == 1, dnorm, dout))
            o_ref[0, :, sl] = ds * _dsilu(c)

    spec = pl.BlockSpec((1, tm, H * hd), lambda p, b, t: (b, t, p))
    return pl.pallas_call(
        body, name="dn_act_bwd", grid=(3, B, T // tm), in_specs=[spec, spec, spec], out_specs=spec,
        out_shape=jax.ShapeDtypeStruct((B, T, C3), F32),
        compiler_params=_cparams(("parallel",) * 3, VMEM_LIMIT))(cv, dqkv[0], dqkv[1])


def _chunk_of(d, s, ncc, nch):
    if d == 0:
        return s
    return jnp.where(s < ncc, ncc - 1 - s, nch - 1 - (s - ncc))


def _delta_masks(d):
    row = lax.broadcasted_iota(jnp.int32, (CHUNK, CHUNK), 0)
    col = lax.broadcasted_iota(jnp.int32, (CHUNK, CHUNK), 1)
    sign = 1 - 2 * d
    diff = (row - col) * sign
    return diff >= 0, diff > 0, diff <= 0


def _each(f, *lists):
    return [f(*a) for a in zip(*lists)]


def _unit_tri_inverse(As):
    C = As[0].shape[0]
    row = lax.broadcasted_iota(jnp.int32, (C, C), 0)
    col = lax.broadcasted_iota(jnp.int32, (C, C), 1)
    eye = jnp.where(row == col, 1.0, 0.0).astype(F32)
    same = lambda shift: jnp.right_shift(row, shift) == jnp.right_shift(col, shift)
    in8 = same(3)
    xs = [-jnp.where(in8, a, 0.0) for a in As]
    ts = [eye + x for x in xs]
    ps = _each(lambda x: _dot(x, x), xs)
    tp = _each(lambda t, p: _dot(_rows(t, p), p), ts, ps)
    ts = _each(lambda t, m: t + m[:C], ts, tp)
    ts = _each(lambda t, m: t + _dot(t, m[C:]), ts, tp)
    shift = 3
    while (1 << shift) < C:
        off = jnp.logical_and(same(shift + 1), jnp.right_shift(row, shift) != jnp.right_shift(col, shift))
        los = [jnp.where(off, a, 0.0) for a in As]
        ts = _each(lambda t, lo: t - _dot(t, _dot(lo, t)), ts, los)
        shift += 1
    def residual(a, t):
        (ah, al), (th, tl) = _split2(eye + a), _split2(t)
        m = _dot(_rows(ah, al), th)
        return eye - (m[:C] + (m[C:] + _dot(ah, tl)))

    rs = _each(residual, As, ts)
    return _each(lambda t, r: t + _dot(t, r), ts, rs)


def _rows(*xs):
    return jnp.concatenate(xs, axis=0)


def _cols(*xs):
    return jnp.concatenate(xs, axis=1)


def _delta_chunk_fwd(q, k, v, g_i, g_j, beta_i, gl, S, incl, strict, Tm=None):
    D_ = _each(lambda gi, gj, m: jnp.where(m, jnp.exp(jnp.where(m, gi - gj, 0.0)), 0.0), g_i, g_j, incl)
    C, dk = k[0].shape
    kb = _each(lambda k_, b: k_ * b, k, beta_i)
    kq = _each(lambda kb_, q_, k_: _dot(_rows(kb_, q_), k_, "nt"), kb, q, k)
    A = _each(lambda m_, d, m: jnp.where(m, m_[:C] * d, 0.0), kq, D_, strict)
    P = _each(lambda m_, d, m: jnp.where(m, m_[C:] * d, 0.0), kq, D_, incl)
    if Tm is None:
        Tm = _unit_tri_inverse(A)
    gam = _each(jnp.exp, g_i)
    wu = _each(lambda t, kb_, g, v_, b: _dot(t, _cols(kb_ * g, v_ * b)), Tm, kb, gam, v, beta_i)
    w = [m_[:, :dk] for m_ in wu]
    u = [m_[:, dk:] for m_ in wu]
    qg = _each(lambda q_, g: q_ * g, q, gam)
    ws = _each(lambda w_, qg_, s: _dot(_rows(w_, qg_), s), w, qg, S)
    vn = _each(lambda u_, m_: u_ - m_[:C], u, ws)
    o = _each(lambda m_, p, vn_: m_[C:] + _dot(p, vn_), ws, P, vn)
    e_i = _each(lambda gl_, gi: jnp.exp(gl_ - gi), gl, g_i)
    kd = _each(lambda k_, e: k_ * e, k, e_i)
    Gam = _each(jnp.exp, gl)
    S_new = _each(lambda s, g, kd_, vn_: s * g + _dot(kd_, vn_, "tn"), S, Gam, kd, vn)
    return dict(D=D_, kb=kb, A=A, Tm=Tm, gam=gam, w=w, u=u, vn=vn, P=P, qg=qg, o=o, e=e_i, kd=kd, Gam=Gam, S_new=S_new)


def _delta_gb(gb_ref, d, incl, incl_t, H):
    gb = gb_ref[0]
    g = gb[:, d * H:(d + 1) * H]
    beta = gb[:, (2 + d) * H:(3 + d) * H]
    gc_col = _dot_mask(incl.astype(BF16), g)
    gc_row = _dot_mask(incl_t.astype(BF16), g, "xtn")
    gl = jnp.sum(g, axis=0, keepdims=True)
    return beta, gc_col, gc_row, gl


def _delta_fwd(qkvn, gb, H, hd, ncc):
    B, T, _ = qkvn.shape
    nch = T // CHUNK
    HD = H * hd
    pairs = [(d, h) for d in range(2) for h in range(H)]

    def body(q0, k0, v0, gb0, q1, k1, v1, gb1, o0, o1, sin0, sin1, tm0, tm1, S_ref):
        s = pl.program_id(1)
        q_refs, k_refs, v_refs, gb_refs = (q0, q1), (k0, k1), (v0, v1), (gb0, gb1)
        o_refs, sin_refs, tm_refs = (o0, o1), (sin0, sin1), (tm0, tm1)

        @pl.when(s == 0)
        def _():
            S_ref[...] = jnp.zeros_like(S_ref)

        masks = [_delta_masks(d) for d in range(2)]
        gbs = [_delta_gb(gb_refs[d], d, masks[d][0], masks[d][2], H) for d in range(2)]
        head = lambda ref, h: ref[0, :, h * hd:(h + 1) * hd]
        S = [S_ref[d, h] for d, h in pairs]
        r = _delta_chunk_fwd([head(q_refs[d], h) for d, h in pairs], [head(k_refs[d], h) for d, h in pairs],
                             [head(v_refs[d], h) for d, h in pairs],
                             [gbs[d][1][:, h:h + 1] for d, h in pairs], [gbs[d][2][h:h + 1, :] for d, h in pairs],
                             [gbs[d][0][:, h:h + 1] for d, h in pairs], [gbs[d][3][:, h:h + 1] for d, h in pairs],
                             S, [masks[d][0] for d, h in pairs], [masks[d][1] for d, h in pairs])
        for i, (d, h) in enumerate(pairs):
            sin_refs[d][0, 0, h] = S[i]
            tm_refs[d][0, 0, h] = r["Tm"][i]
            S_ref[d, h] = r["S_new"][i]
            o_refs[d][0, :, h * hd:(h + 1) * hd] = r["o"][i]

    def dir_specs(d):
        cidx = lambda b, s: _chunk_of(d, s, ncc, nch)
        ins = [pl.BlockSpec((1, CHUNK, HD), lambda b, s, p=p: (b, cidx(b, s), p)) for p in range(3)]
        ins.append(pl.BlockSpec((1, CHUNK, LANES), lambda b, s: (b, cidx(b, s), 0)))
        return (ins, pl.BlockSpec((1, CHUNK, HD), lambda b, s: (b, cidx(b, s), 0)),
                pl.BlockSpec((1, 1, H, hd, hd), lambda b, s: (b, cidx(b, s), 0, 0, 0)),
                pl.BlockSpec((1, 1, H, CHUNK, CHUNK), lambda b, s: (b, cidx(b, s), 0, 0, 0)))

    (in0, o_s0, sin_s0, tm_s0), (in1, o_s1, sin_s1, tm_s1) = dir_specs(0), dir_specs(1)
    o_shape = jax.ShapeDtypeStruct((B, T, HD), F32)
    sin_shape = jax.ShapeDtypeStruct((B, nch, H, hd, hd), F32)
    tm_shape = jax.ShapeDtypeStruct((B, nch, H, CHUNK, CHUNK), F32)
    return pl.pallas_call(
        body, name="delta_fwd", grid=(B, nch),
        in_specs=in0 + in1, out_specs=(o_s0, o_s1, sin_s0, sin_s1, tm_s0, tm_s1),
        out_shape=(o_shape, o_shape, sin_shape, sin_shape, tm_shape, tm_shape),
        scratch_shapes=[pltpu.VMEM((2, H, hd, hd), F32)],
        compiler_params=_cparams(("parallel", "arbitrary"), VMEM_LIMIT),
    )(qkvn, qkvn, qkvn, gb, qkvn, qkvn, qkvn, gb)


def _delta_bwd(qkvn, gb, sin, tms, do, H, hd, ncc):
    B, T, _ = qkvn.shape
    nch = T // CHUNK
    HD = H * hd
    pairs = [(d, h) for d in range(2) for h in range(H)]

    def body(q0, k0, v0, gb0, sin0, tm0, do0, q1, k1, v1, gb1, sin1, tm1, do1, dqkv0, dqkv1, dgb0, dgb1, dS_ref):
        s = pl.program_id(1)
        tm_refs = (tm0, tm1)
        q_refs, k_refs, v_refs, gb_refs = (q0, q1), (k0, k1), (v0, v1), (gb0, gb1)
        sin_refs, do_refs, dqkv_refs, dgb_refs = (sin0, sin1), (do0, do1), (dqkv0, dqkv1), (dgb0, dgb1)

        @pl.when(s == 0)
        def _():
            dS_ref[...] = jnp.zeros_like(dS_ref)

        masks = [_delta_masks(d) for d in range(2)]
        gbs = [_delta_gb(gb_refs[d], d, masks[d][0], masks[d][2], H) for d in range(2)]
        incl = [masks[d][0] for d, h in pairs]
        strict = [masks[d][1] for d, h in pairs]
        lane = lax.broadcasted_iota(jnp.int32, (1, LANES), 1)
        ones = jnp.ones((3 * CHUNK, LANES), BF16)
        head = lambda ref, h: ref[0, :, h * hd:(h + 1) * hd]
        q = [head(q_refs[d], h) for d, h in pairs]
        k = [head(k_refs[d], h) for d, h in pairs]
        v = [head(v_refs[d], h) for d, h in pairs]
        do_ = [head(do_refs[d], h) for d, h in pairs]
        beta_i = [gbs[d][0][:, h:h + 1] for d, h in pairs]
        S = [sin_refs[d][0, 0, h] for d, h in pairs]
        dSn = [dS_ref[d, h] for d, h in pairs]
        f = _delta_chunk_fwd(q, k, v, [gbs[d][1][:, h:h + 1] for d, h in pairs], [gbs[d][2][h:h + 1, :] for d, h in pairs],
                             beta_i, [gbs[d][3][:, h:h + 1] for d, h in pairs], S, incl, strict,
                             Tm=[tm_refs[d][0, 0, h] for d, h in pairs])
        rsum = lambda x: jnp.sum(x, axis=1, keepdims=True)
        dvn = _each(lambda p, d_, kd, ds: _dot(p, d_, "tn") + _dot(kd, ds), f["P"], do_, f["kd"], dSn)
        dP = _each(lambda d_, vn, m: jnp.where(m, _dot(d_, vn, "nt"), 0.0), do_, f["vn"], incl)
        dqg = _each(lambda d_, s: _dot(d_, s, "nt"), do_, S)
        dS_in = _each(lambda qg, d_, g, ds, w, dv_: _dot(qg, d_, "tn") + g * ds - _dot(w, dv_, "tn"),
                      f["qg"], do_, f["Gam"], dSn, f["w"], dvn)
        dGam = _each(lambda s, ds: jnp.sum(rsum(s * ds), axis=0, keepdims=True), S, dSn)
        dkd = _each(lambda vn, ds: _dot(vn, ds, "nt"), f["vn"], dSn)
        de = _each(lambda dkd_, k_, e: rsum(dkd_ * k_) * e, dkd, k, f["e"])
        dw = _each(lambda dv_, s: -_dot(dv_, s, "nt"), dvn, S)
        dXw = _each(lambda t, dw_: _dot(t, dw_, "tn"), f["Tm"], dw)
        dXu = _each(lambda t, dv_: _dot(t, dv_, "tn"), f["Tm"], dvn)
        dA = _each(lambda xw, w, xu, u, m: -jnp.where(m, _dot(xw, w, "nt") + _dot(xu, u, "nt"), 0.0),
                   dXw, f["w"], dXu, f["u"], strict)
        dM = _each(lambda a, d_: a * d_, dA, f["D"])
        dN = _each(lambda p, d_: p * d_, dP, f["D"])
        dkb = _each(lambda m, k_, xw, g: _dot(m, k_) + xw * g, dM, k, dXw, f["gam"])
        dq = _each(lambda dqg_, g, n, k_: dqg_ * g + _dot(n, k_), dqg, f["gam"], dN, k)
        dk = _each(lambda dkd_, e, m, kb, n, q_, dkb_, b: dkd_ * e + _dot(m, kb, "tn") + _dot(n, q_, "tn") + dkb_ * b,
                   dkd, f["e"], dM, f["kb"], dN, q, dkb, beta_i)
        dv = _each(lambda xu, b: xu * b, dXu, beta_i)
        dbeta = _each(lambda dkb_, k_, xu, v_: rsum(dkb_ * k_) + rsum(xu * v_), dkb, k, dXu, v)
        E = _each(lambda a, A_, p, P_: a * A_ + p * P_, dA, f["A"], dP, f["P"])

        def colsum(e):
            return _dot(_rows(*_split3(e)), ones, "tn")[:, 0:1]

        dg = _each(lambda e, dqg_, qg, xw, kb, g, de_: rsum(e) - colsum(e) + rsum(dqg_ * qg) + rsum(xw * kb) * g - de_,
                   E, dqg, f["qg"], dXw, f["kb"], f["gam"], de)
        dgl_h = _each(lambda de_, dg_, g: jnp.sum(de_, axis=0, keepdims=True) + dg_ * g, de, dGam, f["Gam"])
        for d in range(2):
            dgc = jnp.zeros((CHUNK, LANES), F32)
            dgl = jnp.zeros((1, LANES), F32)
            for h in range(H):
                i = d * H + h
                g_lane, b_lane = d * H + h, (2 + d) * H + h
                dgc = dgc + dg[i] * (lane == g_lane).astype(F32) + dbeta[i] * (lane == b_lane).astype(F32)
                dgl = dgl + dgl_h[i] * (lane == g_lane).astype(F32)
                dS_ref[d, h] = dS_in[i]
                dqkv_refs[d][0, :, h * hd:(h + 1) * hd] = dq[i]
                dqkv_refs[d][0, :, HD + h * hd:HD + (h + 1) * hd] = dk[i]
                dqkv_refs[d][0, :, 2 * HD + h * hd:2 * HD + (h + 1) * hd] = dv[i]
            draw = _dot_mask(masks[d][0].astype(BF16), dgc, "tn") + dgl
            dgb_refs[d][0] = jnp.where(lane < 2 * H, draw, dgc)

    def dir_specs(d):
        cidx = lambda b, s: _chunk_of(d, nch - 1 - s, ncc, nch)
        ins = [pl.BlockSpec((1, CHUNK, HD), lambda b, s, p=p: (b, cidx(b, s), p)) for p in range(3)]
        ins += [pl.BlockSpec((1, CHUNK, LANES), lambda b, s: (b, cidx(b, s), 0)),
                pl.BlockSpec((1, 1, H, hd, hd), lambda b, s: (b, cidx(b, s), 0, 0, 0)),
                pl.BlockSpec((1, 1, H, CHUNK, CHUNK), lambda b, s: (b, cidx(b, s), 0, 0, 0)),
                pl.BlockSpec((1, CHUNK, HD), lambda b, s: (b, cidx(b, s), 0))]
        return (ins, pl.BlockSpec((1, CHUNK, 3 * HD), lambda b, s: (b, cidx(b, s), 0)),
                pl.BlockSpec((1, CHUNK, LANES), lambda b, s: (b, cidx(b, s), 0)))

    (in0, dq_s0, dg_s0), (in1, dq_s1, dg_s1) = dir_specs(0), dir_specs(1)
    dq_shape = jax.ShapeDtypeStruct((B, T, 3 * HD), F32)
    dg_shape = jax.ShapeDtypeStruct((B, T, LANES), F32)
    return pl.pallas_call(
        body, name="delta_bwd", grid=(B, nch),
        in_specs=in0 + in1, out_specs=(dq_s0, dq_s1, dg_s0, dg_s1),
        out_shape=(dq_shape, dq_shape, dg_shape, dg_shape),
        scratch_shapes=[pltpu.VMEM((2, H, hd, hd), F32)],
        compiler_params=_cparams(("parallel", "arbitrary"), VMEM_LIMIT),
    )(qkvn, qkvn, qkvn, gb, sin[0], tms[0], do, qkvn, qkvn, qkvn, gb, sin[1], tms[1], do)


def _dn_out(o2, pz, zcb0, onorm, H, hd, nct, tm):
    B, T, HD = o2[0].shape
    N = T - nct * tm

    def body(o0_ref, o1_ref, z_ref, g_ref, y_ref):
        for h in range(H):
            sl = slice(h * hd, (h + 1) * hd)
            o = o0_ref[0, :, sl] + o1_ref[0, :, sl]
            r = lax.rsqrt(jnp.mean(o * o, axis=-1, keepdims=True) + EPS)
            y_ref[0, :, sl] = (o * r * g_ref[...] * _silu(z_ref[0, :, sl].astype(F32))).astype(BF16)

    return pl.pallas_call(
        body, name="dn_out", grid=(B, N // tm),
        in_specs=[pl.BlockSpec((1, tm, HD), lambda b, t: (b, t + nct, 0)),
                  pl.BlockSpec((1, tm, HD), lambda b, t: (b, t + nct, 0)),
                  pl.BlockSpec((1, tm, HD), lambda b, t: (b, t + nct, zcb0)),
                  pl.BlockSpec((1, hd), lambda b, t: (0, 0))],
        out_specs=pl.BlockSpec((1, tm, HD), lambda b, t: (b, t, 0)),
        out_shape=jax.ShapeDtypeStruct((B, N, HD), BF16),
        compiler_params=_cparams(("parallel",) * 2, VMEM_LIMIT))(o2[0], o2[1], pz, onorm)


def _dn_out_bwd(o2, pz, zcb0, onorm, dy, H, hd, nct, tm):
    B, T, HD = o2[0].shape
    nt = T // tm

    def body(o0_ref, o1_ref, z_ref, g_ref, dy_ref, do_ref, dz_ref, acc_ref):
        t = pl.program_id(1)

        @pl.when(t == 0)
        def _():
            acc_ref[...] = jnp.zeros_like(acc_ref)

        @pl.when(t < nct)
        def _():
            do_ref[0] = jnp.zeros_like(do_ref[0])
            dz_ref[0] = jnp.zeros_like(dz_ref[0])

        @pl.when(t >= nct)
        def _():
            g = g_ref[...]
            for h in range(H):
                sl = slice(h * hd, (h + 1) * hd)
                o = o0_ref[0, :, sl] + o1_ref[0, :, sl]
                z = z_ref[0, :, sl].astype(F32)
                dy_ = dy_ref[0, :, sl]
                r = lax.rsqrt(jnp.mean(o * o, axis=-1, keepdims=True) + EPS)
                oh = o * r
                dz_ref[0, :, sl] = (dy_ * oh * g * _dsilu(z)).astype(BF16)
                dn = dy_ * _silu(z)
                acc_ref[0, 0:1, :] += jnp.sum(dn * oh, axis=0, keepdims=True)
                doh = dn * g
                do_ref[0, :, sl] = r * (doh - oh * jnp.mean(doh * oh, axis=-1, keepdims=True))

    lat = lambda b, t: (b, jnp.maximum(t - nct, 0), 0)
    row = pl.BlockSpec((1, tm, HD), lambda b, t: (b, t, 0))
    return pl.pallas_call(
        body, name="dn_out_bwd", grid=(B, nt),
        in_specs=[row, row,
                  pl.BlockSpec((1, tm, HD), lambda b, t: (b, t, zcb0)),
                  pl.BlockSpec((1, hd), lambda b, t: (0, 0)),
                  pl.BlockSpec((1, tm, HD), lat)],
        out_specs=(row, row, pl.BlockSpec((1, SUBLANES, hd), lambda b, t: (b, 0, 0))),
        out_shape=(jax.ShapeDtypeStruct((B, T, HD), F32), jax.ShapeDtypeStruct((B, T, HD), BF16),
                   jax.ShapeDtypeStruct((B, SUBLANES, hd), F32)),
        compiler_params=_cparams(("parallel", "arbitrary"), VMEM_LIMIT),
    )(o2[0], o2[1], pz, onorm, dy)


def _lru_gate_math(x, wr, wi, br, bi, lam):
    r = _sigmoid(_dot(x, wr) + br)
    i = _sigmoid(_dot(x, wi) + bi)
    sp = _softplus(-lam)
    la = -LRU_C * r * sp
    a = jnp.exp(la)
    s = jnp.sqrt(-jnp.tanh(la) * (a * a + 1.0))
    return r, i, sp, a, s


def _lru_gates(xc, wbd, bias4, lam, tm, bw):
    B, T, W = xc.shape

    def body(x_ref, w_ref, b_ref, l_ref, a_ref, i_ref):
        x = x_ref[0]
        for d in range(2):
            _, i, _, a, s = _lru_gate_math(x, w_ref[2 * d, 0], w_ref[2 * d + 1, 0],
                                           b_ref[2 * d:2 * d + 1, :], b_ref[2 * d + 1:2 * d + 2, :], l_ref[d:d + 1, :])
            a_ref[d, 0] = a
            i_ref[d, 0] = s * (i * x)

    out = pl.BlockSpec((2, 1, tm, bw), lambda b, t, j: (0, b, t, j))
    return pl.pallas_call(
        body, name="lru_gates", grid=(B, T // tm, W // bw),
        in_specs=[pl.BlockSpec((1, tm, bw), lambda b, t, j: (b, t, j)),
                  pl.BlockSpec((4, 1, bw, bw), lambda b, t, j: (0, j, 0, 0)),
                  pl.BlockSpec((4, bw), lambda b, t, j: (0, j)),
                  pl.BlockSpec((2, bw), lambda b, t, j: (0, j))],
        out_specs=(out, out),
        out_shape=(jax.ShapeDtypeStruct((2, B, T, W), F32), jax.ShapeDtypeStruct((2, B, T, W), F32)),
        compiler_params=_cparams(("parallel",) * 3, VMEM_LIMIT))(xc, wbd, bias4, lam)


def _lru_gates_bwd(xc, wbd, bias4, lam, dinp, da, tm, bw):
    B, T, W = xc.shape
    nt = T // tm

    def body(x_ref, w_ref, b_ref, l_ref, di0_ref, di1_ref, da0_ref, da1_ref, dx_ref, dw_ref, acc_ref):
        di_refs, da_refs = (di0_ref, di1_ref), (da0_ref, da1_ref)
        b_ = pl.program_id(1)
        t = pl.program_id(2)

        @pl.when(jnp.logical_and(b_ == 0, t == 0))
        def _():
            dw_ref[...] = jnp.zeros_like(dw_ref)
            acc_ref[...] = jnp.zeros_like(acc_ref)

        x = x_ref[0]
        dx = jnp.zeros_like(x)
        for d in range(2):
            wr, wi = w_ref[2 * d, 0], w_ref[2 * d + 1, 0]
            lam_ = l_ref[d:d + 1, :]
            r, i, sp, a, s = _lru_gate_math(x, wr, wi, b_ref[2 * d:2 * d + 1, :], b_ref[2 * d + 1:2 * d + 2, :], lam_)
            dinp_ = di_refs[d][0]
            dix = dinp_ * s
            ds = dinp_ * i * x
            dla = da_refs[d][0] * a - ds * (a * a) / s
            dpr = dla * (-LRU_C * sp) * r * (1.0 - r)
            dpi = dix * x * i * (1.0 - i)
            dx = dx + dix * i + _dot(dpr, wr, "nt") + _dot(dpi, wi, "nt")
            dw_ref[2 * d, 0] += _dot(x, dpr, "tn")
            dw_ref[2 * d + 1, 0] += _dot(x, dpi, "tn")
            acc_ref[2 * d:2 * d + 1, :] += jnp.sum(dpr, axis=0, keepdims=True)
            acc_ref[2 * d + 1:2 * d + 2, :] += jnp.sum(dpi, axis=0, keepdims=True)
            acc_ref[4 + d:5 + d, :] += jnp.sum(dla * (-LRU_C * r), axis=0, keepdims=True) * (-_sigmoid(-lam_))
        dx_ref[0] = dx

    tile = pl.BlockSpec((1, tm, bw), lambda j, b, t: (b, t, j))
    return pl.pallas_call(
        body, name="lru_gates_bwd", grid=(W // bw, B, nt),
        in_specs=[pl.BlockSpec((1, tm, bw), lambda j, b, t: (b, t, j)),
                  pl.BlockSpec((4, 1, bw, bw), lambda j, b, t: (0, j, 0, 0)),
                  pl.BlockSpec((4, bw), lambda j, b, t: (0, j)),
                  pl.BlockSpec((2, bw), lambda j, b, t: (0, j)), tile, tile, tile, tile],
        out_specs=(pl.BlockSpec((1, tm, bw), lambda j, b, t: (b, t, j)),
                   pl.BlockSpec((4, 1, bw, bw), lambda j, b, t: (0, j, 0, 0)),
                   pl.BlockSpec((SUBLANES, bw), lambda j, b, t: (0, j))),
        out_shape=(jax.ShapeDtypeStruct((B, T, W), F32), jax.ShapeDtypeStruct(wbd.shape, F32),
                   jax.ShapeDtypeStruct((SUBLANES, W), F32)),
        compiler_params=_cparams(("parallel", "arbitrary", "arbitrary"), VMEM_LIMIT),
    )(xc, wbd, bias4, lam, dinp[0], dinp[1], da[0], da[1])


def _tile_scan(a, x, rev, tm):
    row = lax.broadcasted_iota(jnp.int32, a.shape, 0)
    s = 1
    while s < tm:
        if rev:
            a_sh, x_sh, ok = pltpu.roll(a, tm - s, 0), pltpu.roll(x, tm - s, 0), row < tm - s
        else:
            a_sh, x_sh, ok = pltpu.roll(a, s, 0), pltpu.roll(x, s, 0), row >= s
        x = jnp.where(ok, x + a * x_sh, x)
        a = jnp.where(ok, a * a_sh, a)
        s *= 2
    return a, x


def _scan_tile_of(s, rev, nct, nt):
    if not rev:
        return s
    return jnp.where(s < nct, nct - 1 - s, nt - 1 - (s - nct))


def _lru_scan(a2, x2, d, nct, tm, tc):
    _, B, T, W = a2.shape
    nt = T // tm
    rev = d == 1
    last = 0 if rev else tm - 1

    def body(a_ref, x_ref, h_ref, carry):
        s = pl.program_id(2)

        @pl.when(s == 0)
        def _():
            carry[...] = jnp.zeros_like(carry)

        A, X = _tile_scan(a_ref[0, 0], x_ref[0, 0], rev, tm)
        h = X + A * carry[0:1, :]
        h_ref[0] = h
        carry[...] = jnp.broadcast_to(h[last:last + 1, :], carry.shape)

    tidx = lambda s: _scan_tile_of(s, rev, nct, nt)
    spec = pl.BlockSpec((1, 1, tm, tc), lambda b, j, s: (d, b, tidx(s), j))
    return pl.pallas_call(
        body, name=f"lru_scan{d}", grid=(B, W // tc, nt),
        in_specs=[spec, spec], out_specs=pl.BlockSpec((1, tm, tc), lambda b, j, s: (b, tidx(s), j)),
        out_shape=jax.ShapeDtypeStruct((B, T, W), F32),
        scratch_shapes=[pltpu.VMEM((SUBLANES, tc), F32)],
        compiler_params=_cparams(("parallel", "parallel", "arbitrary"), VMEM_LIMIT),
    )(a2, x2)


def _lru_scan_bwd(a2, h, dh, d, nct, tm, tc):
    _, B, T, W = a2.shape
    nt = T // tm
    rev = d == 1
    first = tm - 1 if rev else 0
    r8, n8 = tm // SUBLANES, T // SUBLANES

    def body(a_ref, h_ref, hp_ref, dh_ref, lam_ref, da_ref, ca, cl):
        s = pl.program_id(2)

        @pl.when(s == 0)
        def _():
            ca[...] = jnp.zeros_like(ca)
            cl[...] = jnp.zeros_like(cl)

        a = a_ref[0, 0]
        row = lax.broadcasted_iota(jnp.int32, a.shape, 0)
        if rev:
            c = jnp.where(row == 0, ca[0:1, :], pltpu.roll(a, 1, 0))
        else:
            c = jnp.where(row == tm - 1, ca[0:1, :], pltpu.roll(a, tm - 1, 0))
        C, L = _tile_scan(c, dh_ref[0], not rev, tm)
        lam = L + C * cl[0:1, :]
        lam_ref[0] = lam
        hp = jnp.where(s == nt - 1, 0.0, hp_ref[0])
        if rev:
            hprev = jnp.where(row == tm - 1, hp[0:1, :], pltpu.roll(h_ref[0], tm - 1, 0))
        else:
            hprev = jnp.where(row == 0, hp[SUBLANES - 1:SUBLANES, :], pltpu.roll(h_ref[0], 1, 0))
        da_ref[0] = lam * hprev
        ca[...] = jnp.broadcast_to(a[first:first + 1, :], ca.shape)
        cl[...] = jnp.broadcast_to(lam[first:first + 1, :], cl.shape)

    tidx = lambda s: _scan_tile_of(nt - 1 - s, rev, nct, nt)

    def hp_idx(b, j, s):
        tt = tidx(s)
        if rev:
            blk = jnp.where(tt == nt - 1, 0, jnp.minimum((tt + 1) * r8, n8 - 1))
        else:
            blk = jnp.maximum(tt * r8 - 1, 0)
        return (b, blk, j)

    tile = pl.BlockSpec((1, tm, tc), lambda b, j, s: (b, tidx(s), j))
    return pl.pallas_call(
        body, name=f"lru_scan_bwd{d}", grid=(B, W // tc, nt),
        in_specs=[pl.BlockSpec((1, 1, tm, tc), lambda b, j, s: (d, b, tidx(s), j)), tile,
                  pl.BlockSpec((1, SUBLANES, tc), hp_idx), tile],
        out_specs=(tile, tile),
        out_shape=(jax.ShapeDtypeStruct((B, T, W), F32), jax.ShapeDtypeStruct((B, T, W), F32)),
        scratch_shapes=[pltpu.VMEM((SUBLANES, tc), F32), pltpu.VMEM((SUBLANES, tc), F32)],
        compiler_params=_cparams(("parallel", "parallel", "arbitrary"), VMEM_LIMIT),
    )(a2, h, h, dh)


def _lru_out(h0, h1, pyl, ycb0, nct, tm, tc):
    B, T, W = h0.shape
    N = T - nct * tm

    def body(h0_ref, h1_ref, y_ref, o_ref):
        o_ref[0] = ((h0_ref[0] + h1_ref[0]) * _gelu(y_ref[0].astype(F32))).astype(BF16)

    hs = pl.BlockSpec((1, tm, tc), lambda b, t, j: (b, t + nct, j))
    return pl.pallas_call(
        body, name="lru_out", grid=(B, N // tm, W // tc),
        in_specs=[hs, hs, pl.BlockSpec((1, tm, tc), lambda b, t, j: (b, t + nct, ycb0 + j))],
        out_specs=pl.BlockSpec((1, tm, tc), lambda b, t, j: (b, t, j)),
        out_shape=jax.ShapeDtypeStruct((B, N, W), BF16),
        compiler_params=_cparams(("parallel",) * 3, VMEM_LIMIT))(h0, h1, pyl)


def _lru_out_bwd(h0, h1, pyl, ycb0, dy, nct, tm, tc):
    B, T, W = h0.shape

    def body(h0_ref, h1_ref, y_ref, dy_ref, dh_ref, dyl_ref):
        t = pl.program_id(1)

        @pl.when(t < nct)
        def _():
            dh_ref[0] = jnp.zeros_like(dh_ref[0])
            dyl_ref[0] = jnp.zeros_like(dyl_ref[0])

        @pl.when(t >= nct)
        def _():
            y = y_ref[0].astype(F32)
            dy_ = dy_ref[0]
            dh_ref[0] = dy_ * _gelu(y)
            dyl_ref[0] = (dy_ * (h0_ref[0] + h1_ref[0]) * _dgelu(y)).astype(BF16)

    hs = pl.BlockSpec((1, tm, tc), lambda b, t, j: (b, t, j))
    return pl.pallas_call(
        body, name="lru_out_bwd", grid=(B, T // tm, W // tc),
        in_specs=[hs, hs, pl.BlockSpec((1, tm, tc), lambda b, t, j: (b, t, ycb0 + j)),
                  pl.BlockSpec((1, tm, tc), lambda b, t, j: (b, jnp.maximum(t - nct, 0), j))],
        out_specs=(hs, hs),
        out_shape=(jax.ShapeDtypeStruct((B, T, W), F32), jax.ShapeDtypeStruct((B, T, W), BF16)),
        compiler_params=_cparams(("parallel",) * 3, VMEM_LIMIT))(h0, h1, pyl, dy)


def _merge(bd, bl, pmg, bm, nct, tm):
    B, N, D = bd.shape

    def body(bd_ref, bl_ref, m0_ref, m1_ref, b_ref, o_ref):
        g0 = _sigmoid(m0_ref[0].astype(F32) + b_ref[:, 0:D])
        g1 = _sigmoid(m1_ref[0].astype(F32) + b_ref[:, D:2 * D])
        o_ref[0] = (g0 * bd_ref[0].astype(F32) + g1 * bl_ref[0].astype(F32)).astype(BF16)

    row = pl.BlockSpec((1, tm, D), lambda b, t: (b, t, 0))
    return pl.pallas_call(
        body, name="merge", grid=(B, N // tm),
        in_specs=[row, row, pl.BlockSpec((1, tm, D), lambda b, t: (b, t + nct, 0)),
                  pl.BlockSpec((1, tm, D), lambda b, t: (b, t + nct, 1)),
                  pl.BlockSpec((1, 2 * D), lambda b, t: (0, 0))],
        out_specs=row, out_shape=jax.ShapeDtypeStruct((B, N, D), BF16),
        compiler_params=_cparams(("parallel", "parallel"), VMEM_LIMIT))(bd, bl, pmg, pmg, bm)


def _merge_bwd(dmi, bd, bl, pmg, bm, nct, tm):
    B, N, D = bd.shape
    T = pmg.shape[1]

    def body(dm_ref, bd_ref, bl_ref, m0_ref, m1_ref, b_ref, dbd_ref, dbl_ref, dmg_ref, acc_ref):
        t = pl.program_id(1)

        @pl.when(t == 0)
        def _():
            acc_ref[...] = jnp.zeros_like(acc_ref)

        @pl.when(t < nct)
        def _():
            dmg_ref[0] = jnp.zeros_like(dmg_ref[0])

        @pl.when(t >= nct)
        def _():
            dm = dm_ref[0]
            g0 = _sigmoid(m0_ref[0].astype(F32) + b_ref[:, 0:D])
            g1 = _sigmoid(m1_ref[0].astype(F32) + b_ref[:, D:2 * D])
            dbd_ref[0] = (dm * g0).astype(BF16)
            dbl_ref[0] = (dm * g1).astype(BF16)
            d0 = dm * bd_ref[0].astype(F32) * g0 * (1.0 - g0)
            d1 = dm * bl_ref[0].astype(F32) * g1 * (1.0 - g1)
            dmg_ref[0, :, 0:D] = d0.astype(BF16)
            dmg_ref[0, :, D:2 * D] = d1.astype(BF16)
            acc_ref[0, 0:1, 0:D] += jnp.sum(d0, axis=0, keepdims=True)
            acc_ref[0, 0:1, D:2 * D] += jnp.sum(d1, axis=0, keepdims=True)

    lat = pl.BlockSpec((1, tm, D), lambda b, t: (b, jnp.maximum(t - nct, 0), 0))
    return pl.pallas_call(
        body, name="merge_bwd", grid=(B, T // tm),
        in_specs=[lat, lat, lat, pl.BlockSpec((1, tm, D), lambda b, t: (b, t, 0)),
                  pl.BlockSpec((1, tm, D), lambda b, t: (b, t, 1)),
                  pl.BlockSpec((1, 2 * D), lambda b, t: (0, 0))],
        out_specs=(lat, lat, pl.BlockSpec((1, tm, 2 * D), lambda b, t: (b, t, 0)),
                   pl.BlockSpec((1, SUBLANES, 2 * D), lambda b, t: (b, 0, 0))),
        out_shape=(jax.ShapeDtypeStruct((B, N, D), BF16), jax.ShapeDtypeStruct((B, N, D), BF16),
                   jax.ShapeDtypeStruct((B, T, 2 * D), BF16), jax.ShapeDtypeStruct((B, SUBLANES, 2 * D), F32)),
        compiler_params=_cparams(("parallel", "arbitrary"), VMEM_LIMIT))(dmi, bd, bl, pmg, pmg, bm)


def _grid_taps():
    return [((di + 1) * 3 + (dj + 1), di, dj) for di in (-1, 0, 1) for dj in (-1, 0, 1)]


def _grid_views(x, n):
    pos = lax.broadcasted_iota(jnp.int32, x.shape, 0)
    gc = jnp.bitwise_and(pos, GRID_W - 1)
    cols = {0: x,
            -1: jnp.where(gc >= 1, pltpu.roll(x, 1, 0), 0.0),
            1: jnp.where(gc <= GRID_W - 2, pltpu.roll(x, n - 1, 0), 0.0)}
    views = {}
    edge = jnp.zeros((GRID_W, x.shape[1]), x.dtype)
    for dj, xc in cols.items():
        views[(0, dj)] = xc
        views[(-1, dj)] = jnp.concatenate([edge, xc[:n - GRID_W]], axis=0)
        views[(1, dj)] = jnp.concatenate([xc[GRID_W:], edge], axis=0)
    return views


def _ffn_act(F, w9, b, tc):
    B, N, C2 = F.shape
    Cf = C2 // 2
    ncb = Cf // tc

    def body(g_ref, v_ref, w_ref, b_ref, o_ref, gel_ref, dgel_ref):
        xv = _grid_views(g_ref[0], N)
        conv = b_ref[...]
        for k, di, dj in _grid_taps():
            conv = conv + xv[(di, dj)] * w_ref[k:k + 1, :]
        th = jnp.tanh(GELU_C * (conv + 0.044715 * conv * conv * conv))
        gel = 0.5 * conv * (1.0 + th)
        o_ref[0] = (gel * v_ref[0]).astype(BF16)
        gel_ref[0] = gel.astype(BF16)
        dgel_ref[0] = (0.5 * (1.0 + th)
                       + 0.5 * conv * (1.0 - th * th) * GELU_C * (1.0 + 3.0 * 0.044715 * conv * conv)).astype(BF16)

    tile = pl.BlockSpec((1, N, tc), lambda b, j: (b, 0, j))
    out = jax.ShapeDtypeStruct((B, N, Cf), BF16)
    return pl.pallas_call(
        body, name="ffn_act", grid=(B, ncb),
        in_specs=[tile, pl.BlockSpec((1, N, tc), lambda b, j: (b, 0, ncb + j)),
                  pl.BlockSpec((9, tc), lambda b, j: (0, j)),
                  pl.BlockSpec((1, tc), lambda b, j: (0, j))],
        out_specs=(tile, tile, tile), out_shape=(out, out, out),
        compiler_params=_cparams(("parallel", "parallel"), VMEM_LIMIT))(F, F, w9, b)


def _ffn_act_bwd(F, w9, gel, dgel, df, tc):
    B, N, C2 = F.shape
    Cf = C2 // 2
    ncb = Cf // tc

    def body(g_ref, v_ref, w_ref, gel_ref, dgel_ref, df_ref, dF_ref, acc_ref):
        p = pl.program_id(2)

        @pl.when(p == 0)
        def _():
            dF_ref[0] = (df_ref[0] * gel_ref[0].astype(F32)).astype(BF16)

        @pl.when(p == 1)
        def _():
            xv = _grid_views(g_ref[0], N)
            df_ = df_ref[0]
            dc = df_ * v_ref[0] * dgel_ref[0].astype(F32)
            dcv = _grid_views(dc, N)
            dx = None
            for k, di, dj in _grid_taps():
                term = dcv[(-di, -dj)] * w_ref[k:k + 1, :]
                dx = term if dx is None else dx + term
                acc_ref[0, k:k + 1, :] = jnp.sum(dc * xv[(di, dj)], axis=0, keepdims=True)
            acc_ref[0, 9:10, :] = jnp.sum(dc, axis=0, keepdims=True)
            acc_ref[0, 10:16, :] = jnp.zeros((6, tc), F32)
            dF_ref[0] = dx.astype(BF16)

    tile = pl.BlockSpec((1, N, tc), lambda b, j, p: (b, 0, j))
    return pl.pallas_call(
        body, name="ffn_act_bwd", grid=(B, ncb, 2),
        in_specs=[tile, pl.BlockSpec((1, N, tc), lambda b, j, p: (b, 0, ncb + j)),
                  pl.BlockSpec((9, tc), lambda b, j, p: (0, j)), tile, tile, tile],
        out_specs=(pl.BlockSpec((1, N, tc), lambda b, j, p: (b, 0, j + (1 - p) * ncb)),
                   pl.BlockSpec((1, 16, tc), lambda b, j, p: (b, 0, j))),
        out_shape=(jax.ShapeDtypeStruct((B, N, C2), BF16), jax.ShapeDtypeStruct((B, 16, Cf), F32)),
        compiler_params=_cparams(("parallel", "parallel", "arbitrary"), VMEM_LIMIT))(F, F, w9, gel, dgel, df)


ADAM_ROWS = 512


def _adamw(w, m, v, gparts, name):
    rows, cols = w.shape[-2:]
    P = gparts.shape[0]
    tr = _row_tile(rows, (P + 14) * 4 * (-(-cols // LANES) * LANES))
    c1 = 1.0 - ADAM_B1 ** ADAM_STEP
    c2 = 1.0 - ADAM_B2 ** ADAM_STEP

    def body(w_ref, m_ref, v_ref, g_ref, go_ref, d_ref, mo_ref, vo_ref):
        g = g_ref[0].astype(F32)
        for p in range(1, P):
            g = g + g_ref[p].astype(F32)
        m_ = ADAM_B1 * m_ref[...] + (1.0 - ADAM_B1) * g
        v_ = ADAM_B2 * v_ref[...] + (1.0 - ADAM_B2) * (g * g)
        go_ref[...] = g
        mo_ref[...] = m_
        vo_ref[...] = v_
        d_ref[...] = -ADAM_LR * ((m_ / c1) / (jnp.sqrt(v_ / c2) + ADAM_EPS) + ADAM_WD * w_ref[...])

    if w.ndim == 3:
        row = pl.BlockSpec((None, tr, cols), lambda i: (0, i, 0))
    else:
        row = pl.BlockSpec((tr, cols), lambda i: (i, 0))
    out = jax.ShapeDtypeStruct(w.shape, F32)
    return pl.pallas_call(
        body, name=name, grid=(rows // tr,),
        in_specs=[row, row, row, pl.BlockSpec((P, tr, cols), lambda i: (0, i, 0))],
        out_specs=(row, row, row, row), out_shape=(out, out, out, out),
        compiler_params=_cparams(("parallel",), VMEM_LIMIT))(w, m, v, gparts)


def _pack_rows(flat_len):
    rows = -(-flat_len // LANES)
    if rows > ADAM_ROWS:
        rows = -(-rows // ADAM_ROWS) * ADAM_ROWS
    else:
        rows = -(-rows // SUBLANES) * SUBLANES
    return rows


PACK_ALIGN = SUBLANES * LANES


def _pack(arrs, lead=()):
    pads = [(0, 0)] * len(lead)
    parts = []
    for a in arrs:
        f = a.reshape(lead + (-1,)).astype(F32)
        parts.append(jnp.pad(f, pads + [(0, -f.shape[-1] % PACK_ALIGN)]))
    flat = jnp.concatenate(parts, axis=-1)
    n = flat.shape[-1]
    rows = _pack_rows(n)
    flat = jnp.pad(flat, pads + [(0, rows * LANES - n)])
    return flat.reshape(lead + (rows, LANES))


def _unpack(buf, shapes):
    out, r = [], 0
    for s in shapes:
        n = math.prod(s)
        nr = -(-n // PACK_ALIGN) * SUBLANES
        out.append(buf[r:r + nr].reshape(-1)[:n].reshape(s))
        r += nr
    return out


def _col_shards(full):
    C = full.shape[-1]
    x = full.reshape(full.shape[:-1] + (N_DEV, C // N_DEV))
    return jnp.moveaxis(x, -2, 0)


def _from_col_shards(g):
    x = jnp.moveaxis(g, 0, -2)
    return x.reshape(x.shape[:-2] + (x.shape[-2] * x.shape[-1],))


def kernel(x, c, ctx, c_ctx, w_ada, b_ada, g_pre_mix, g_post_mix, g_pre_ffn, g_post_ffn, w_in, b_merge, dn_conv, dn_a_log, dn_dt_bias, dn_onorm, lru_conv, lru_conv_b, lru_w_rg, lru_b_rg, lru_w_ig, lru_b_ig, lru_lambda, w_branch_dn, w_branch_lru, w_out, w_up, ffn_dw, ffn_dw_b, w_down, loss_target, m_c_ctx, m_w_ada, m_b_ada, m_g_pre_mix, m_g_post_mix, m_g_pre_ffn, m_g_post_ffn, m_w_in, m_b_merge, m_dn_conv, m_dn_a_log, m_dn_dt_bias, m_dn_onorm, m_lru_conv, m_lru_conv_b, m_lru_w_rg, m_lru_b_rg, m_lru_w_ig, m_lru_b_ig, m_lru_lambda, m_w_branch_dn, m_w_branch_lru, m_w_out, m_w_up, m_ffn_dw, m_ffn_dw_b, m_w_down, v_c_ctx, v_w_ada, v_b_ada, v_g_pre_mix, v_g_post_mix, v_g_pre_ffn, v_g_post_ffn, v_w_in, v_b_merge, v_dn_conv, v_dn_a_log, v_dn_dt_bias, v_dn_onorm, v_lru_conv, v_lru_conv_b, v_lru_w_rg, v_lru_b_rg, v_lru_w_ig, v_lru_b_ig, v_lru_lambda, v_w_branch_dn, v_w_branch_lru, v_w_out, v_w_up, v_ffn_dw, v_ffn_dw_b, v_w_down):
    params = dict(c_ctx=c_ctx, w_ada=w_ada, b_ada=b_ada, g_pre_mix=g_pre_mix, g_post_mix=g_post_mix, g_pre_ffn=g_pre_ffn, g_post_ffn=g_post_ffn, w_in=w_in, b_merge=b_merge, dn_conv=dn_conv, dn_a_log=dn_a_log, dn_dt_bias=dn_dt_bias, dn_onorm=dn_onorm, lru_conv=lru_conv, lru_conv_b=lru_conv_b, lru_w_rg=lru_w_rg, lru_b_rg=lru_b_rg, lru_w_ig=lru_w_ig, lru_b_ig=lru_b_ig, lru_lambda=lru_lambda, w_branch_dn=w_branch_dn, w_branch_lru=w_branch_lru, w_out=w_out, w_up=w_up, ffn_dw=ffn_dw, ffn_dw_b=ffn_dw_b, w_down=w_down)
    mom1 = dict(c_ctx=m_c_ctx, w_ada=m_w_ada, b_ada=m_b_ada, g_pre_mix=m_g_pre_mix, g_post_mix=m_g_post_mix, g_pre_ffn=m_g_pre_ffn, g_post_ffn=m_g_post_ffn, w_in=m_w_in, b_merge=m_b_merge, dn_conv=m_dn_conv, dn_a_log=m_dn_a_log, dn_dt_bias=m_dn_dt_bias, dn_onorm=m_dn_onorm, lru_conv=m_lru_conv, lru_conv_b=m_lru_conv_b, lru_w_rg=m_lru_w_rg, lru_b_rg=m_lru_b_rg, lru_w_ig=m_lru_w_ig, lru_b_ig=m_lru_b_ig, lru_lambda=m_lru_lambda, w_branch_dn=m_w_branch_dn, w_branch_lru=m_w_branch_lru, w_out=m_w_out, w_up=m_w_up, ffn_dw=m_ffn_dw, ffn_dw_b=m_ffn_dw_b, w_down=m_w_down)
    mom2 = dict(c_ctx=v_c_ctx, w_ada=v_w_ada, b_ada=v_b_ada, g_pre_mix=v_g_pre_mix, g_post_mix=v_g_post_mix, g_pre_ffn=v_g_pre_ffn, g_post_ffn=v_g_post_ffn, w_in=v_w_in, b_merge=v_b_merge, dn_conv=v_dn_conv, dn_a_log=v_dn_a_log, dn_dt_bias=v_dn_dt_bias, dn_onorm=v_dn_onorm, lru_conv=v_lru_conv, lru_conv_b=v_lru_conv_b, lru_w_rg=v_lru_w_rg, lru_b_rg=v_lru_b_rg, lru_w_ig=v_lru_w_ig, lru_b_ig=v_lru_b_ig, lru_lambda=v_lru_lambda, w_branch_dn=v_w_branch_dn, w_branch_lru=v_w_branch_lru, w_out=v_w_out, w_up=v_w_up, ffn_dw=v_ffn_dw, ffn_dw_b=v_ffn_dw_b, w_down=v_w_down)
    names = list(params)

    B, N, D = x.shape
    NC = ctx.shape[1]
    T = NC + N
    H, hd = dn_a_log.shape[2], dn_onorm.shape[1]
    HD = H * hd
    nd = 2 * H
    W = lru_conv_b.shape[1]
    nblk, bdim = lru_w_rg.shape[2], lru_w_rg.shape[3]
    Cf = ffn_dw_b.shape[1]
    sw = w_ada.shape[2]
    tm = min(256, NC)
    nct = NC // tm
    ncc = NC // CHUNK
    nch = T // CHUNK
    R, RL = B * T, B * N
    bw = min(256, W)
    me = _my_index()
    assert NC % tm == 0 and N % tm == 0 and B + 1 <= SUBLANES and bw % bdim == 0 and D % LANES == 0

    cpad = jnp.zeros((SUBLANES, D), F32).at[:B].set(c).at[B].set(c_ctx)
    ccp = _all_gather([cpad], "ag_cond")[0].reshape(N_DEV * SUBLANES, D)
    mods_sh = _ada_fwd(ccp, w_ada[0], lax.dynamic_slice(b_ada, (0, me * sw), (1, sw)))
    lru_vecs = jnp.concatenate([lru_b_rg[0], lru_b_ig[0], lru_lambda[0], jnp.zeros_like(lru_lambda[0])], axis=0)
    mods_g, w_in_g, dn_conv_g, lru_conv_g, lru_vecs_g = _all_gather(
        [mods_sh, w_in[0].astype(BF16).T, dn_conv[0], lru_conv[0], lru_vecs], "ag_first")
    ag_mix = _split_start([w_branch_dn[0].astype(BF16), w_branch_lru[0].astype(BF16), w_out[0].astype(BF16)],
                          "gather", "ag_mix_start")
    ag_ffn = _split_start([w_up[0].astype(BF16) + ag_mix[4][0, 0].astype(BF16), w_down[0].astype(BF16),
                           ffn_dw[0].reshape(9, -1)], "gather", "ag_ffn_start")
    mine = lax.dynamic_slice(mods_g, (0, me * SUBLANES, 0), (N_DEV, SUBLANES, sw))
    mods = jnp.moveaxis(mine, 0, 1).reshape(SUBLANES, 6, D)[:B + 1]
    mods = jnp.pad(mods, ((0, 0), (0, SUBLANES - 6), (0, 0))) + ag_ffn[4][0, 0]
    dn_conv_f = _from_col_shards(dn_conv_g)
    lru_conv_f = _from_col_shards(lru_conv_g)
    lru_vecs_f = _from_col_shards(lru_vecs_g)
    lru_lam_f = lru_vecs_f[4:6]

    csh = w_in_g.shape[1]
    w_in_t = w_in_g.reshape(N_DEV * csh, D)
    o_z, o_a, o_xl = 3 * HD, 4 * HD, 4 * HD + 2 * nd
    o_yl, o_mg = o_xl + W, o_xl + 2 * W
    w_qkv, w_z = w_in_t[:o_z], w_in_t[o_z:o_a]
    w_ab = jnp.pad(w_in_t[o_a:o_xl], ((0, LANES - 2 * nd), (0, 0)))
    w_xl, w_yl, w_mg = w_in_t[o_xl:o_yl], w_in_t[o_yl:o_mg], w_in_t[o_mg:]

    def blockdiag(wg):
        per = bw // bdim
        g5 = wg.reshape(2, W // bw, per, bdim, bdim)
        eye = jnp.eye(per, dtype=wg.dtype)
        return jnp.einsum("dtpij,pq->dtpiqj", g5, eye).reshape(2, W // bw, bw, bw)

    bd_rg, bd_ig = blockdiag(lru_w_rg[0]), blockdiag(lru_w_ig[0])
    wbd = jnp.stack([bd_rg[0], bd_ig[0], bd_rg[1], bd_ig[1]]).astype(BF16)
    bias4 = jnp.stack([lru_vecs_f[0], lru_vecs_f[2], lru_vecs_f[1], lru_vecs_f[3]])
    alog_v = jnp.pad(dn_a_log.reshape(1, nd), ((0, 0), (0, LANES - nd)))
    dtb_v = jnp.pad(dn_dt_bias.reshape(1, nd), ((0, 0), (0, LANES - nd)))

    h0 = jnp.concatenate([ctx, x], axis=1)
    u1 = _norm_mod_fwd(h0, g_pre_mix, mods, 0, 1, nct, tm, "norm_mod1")
    u1f = u1.reshape(R, D)
    p_qkv = _mm(u1f, w_qkv, "nt", "mm_qkv").reshape(B, T, 3 * HD)
    p_z = _mm(u1f, w_z, "nt", "mm_z", out_dtype=BF16).reshape(B, T, HD)
    p_ab = _mm(u1f, w_ab, "nt", "mm_ab")
    p_xl = _mm(u1f, w_xl, "nt", "mm_xl").reshape(B, T, W)
    p_yl = _mm(u1f, w_yl, "nt", "mm_yl", out_dtype=BF16).reshape(B, T, W)
    p_mg = _mm(u1f, w_mg, "nt", "mm_mg", out_dtype=BF16).reshape(B, T, 2 * D)

    conv_offs = (-2, -1, 0, 1)
    tcc = _tile(HD, 1024)
    gb = _ab_act(p_ab, alog_v, dtb_v, nd, tm)
    gb3 = gb.reshape(B, T, LANES)
    cv, qkvn = _dwconv(p_qkv, 0, 3 * HD, dn_conv_f, None, conv_offs, nct, tm, HD, "dn_conv", qkv_heads=(H, hd))
    o_d0, o_d1, s_in0, s_in1, tm_d0, tm_d1 = _delta_fwd(qkvn, gb3, H, hd, ncc)
    o2 = (o_d0, o_d1)
    y_dn = _dn_out(o2, p_z, 0, dn_onorm, H, hd, nct, tm)

    tcw = _tile(W, 1024)
    xc = _dwconv(p_xl, 0, W, lru_conv_f, lru_conv_b, conv_offs, nct, tm, tcw, "lru_conv")
    tmg = max(t_ for t_ in range(SUBLANES, min(T, 768) + 1, SUBLANES) if T % t_ == 0)
    a2, inp2 = _lru_gates(xc, wbd, bias4, lru_lam_f, tmg, bw)
    hd0 = _lru_scan(a2, inp2, 0, nct, tm, tcw)
    hd1 = _lru_scan(a2, inp2, 1, nct, tm, tcw)
    y_lru = _lru_out(hd0, hd1, p_yl, 0, nct, tm, tcw)

    def gathered(started, after, name):
        xs_, lands_ = _split_wait(started, "gather", after, name)
        return [lax.dynamic_update_slice(land, x_[None], (me,) + (0,) * x_.ndim) for land, x_ in zip(lands_, xs_)]

    w_bdn_g, w_blru_g, w_out_g = gathered(ag_mix, y_lru, "ag_mix_wait")
    w_bdn_f, w_blru_f, w_out_f = w_bdn_g.reshape(HD, D), w_blru_g.reshape(W, D), w_out_g.reshape(D, D)
    bd = _mm(y_dn.reshape(RL, HD), w_bdn_f, "nn", "mm_bdn", out_dtype=BF16).reshape(B, N, D)
    bl = _mm(y_lru.reshape(RL, W), w_blru_f, "nn", "mm_blru", out_dtype=BF16).reshape(B, N, D)
    mixin = _merge(bd, bl, p_mg, b_merge, nct, tm)
    mix = _mm(mixin.reshape(RL, D), w_out_f, "nn", "mm_out").reshape(B, N, D)
    h1 = _resid_norm_fwd(x, mix, g_post_mix, mods, 2, tm, "resid_norm1")
    u2 = _norm_mod_fwd(h1, g_pre_ffn, mods, 3, 4, 0, tm, "norm_mod2")
    w_up_g, w_down_g, ffn_dw_g = gathered(ag_ffn, u2, "ag_ffn_wait")
    w_down_f = w_down_g.reshape(Cf, D)
    ffn_dw_f = _from_col_shards(ffn_dw_g)
    Fp = _mm(u2.reshape(RL, D), w_up_g, "nn", "mm_up").reshape(B, N, 2 * Cf)
    tcf = LANES
    f, f_gel, f_dgel = _ffn_act(Fp, ffn_dw_f, ffn_dw_b, tcf)
    dproj = _mm(f.reshape(RL, Cf), w_down_f, "nn", "mm_down").reshape(B, N, D)

    dd, dh2, acc_f = _final_fwd_bwd(h1, dproj, g_post_ffn, mods, 5, loss_target, tm)
    loss = lax.psum((0.5 / D) * jnp.sum(acc_f[:, 2, :]), MESH_AXES)
    ddf = dd.reshape(RL, D)
    df = _mm(ddf, w_down_f, "nt", "mm_down_dx").reshape(B, N, Cf)
    gw_down = _mm(f.reshape(RL, Cf), ddf, "tn", "mm_down_dw", out_dtype=BF16)
    dF, acc_ffn = _ffn_act_bwd(Fp, ffn_dw_f, f_gel, f_dgel, df, tcf)
    dFf = dF.reshape(RL, 2 * Cf)
    du2 = _mm(dFf, w_up_g, "nt", "mm_up_dx").reshape(B, N, D)
    gw_up = _mm(u2.reshape(RL, D), dFf, "tn", "mm_up_dw", out_dtype=BF16, out_shards=N_DEV)
    jm = 2 * lax.axis_index("x") + lax.axis_index("y")

    def pair_sums(started, after, tag):
        parts8, sib4 = _split_wait(started, "pair", after, f"rs_pair_{tag}_wait")
        return [_pair_sum(p8, s4, f"rs_pair_sum_{tag}{i}") for i, (p8, s4) in enumerate(zip(parts8, sib4))]

    def own_slab(land, x4):
        idx = (jm,) + (0,) * (x4.ndim - 1)
        return lax.dynamic_update_slice(land, lax.dynamic_slice(x4, idx, (1,) + x4.shape[1:]), idx)

    p_early = _split_start([gw_up, gw_down.reshape(N_DEV, Cf // N_DEV, D)], "pair", "rs_pair_ffn_start")
    mods_b = mods + p_early[4][0, 0]
    dh1, acc2x, _ = _norm_mod_bwd(h1, du2, g_pre_ffn, mods_b, 4, 0, tm, dh2, "norm_mod2_bwd")
    dmix, acc_r1 = _resid_norm_bwd(mix, dh1, g_post_mix, mods, 2, tm, "resid_norm1_bwd")
    q_early = _split_start(pair_sums(p_early, dmix, "ffn"), "quad", "rs_quad_ffn_start")
    dmixf = dmix.reshape(RL, D)
    dmixin = _mm(dmixf, w_out_f, "nt", "mm_out_dx").reshape(B, N, D)
    gw_out = _mm(mixin.reshape(RL, D), dmixf, "tn", "mm_out_dw", out_dtype=BF16)
    dbd, dbl, dmg, acc_mg = _merge_bwd(dmixin, bd, bl, p_mg, b_merge + q_early[4][0:1, 0:1], nct, tm)
    dy_dn = _mm(dbd.reshape(RL, D), w_bdn_f, "nt", "mm_bdn_dx").reshape(B, N, HD)
    gw_bdn = _mm(y_dn.reshape(RL, HD), dbd.reshape(RL, D), "tn", "mm_bdn_dw", out_dtype=BF16)
    dy_lru = _mm(dbl.reshape(RL, D), w_blru_f, "nt", "mm_blru_dx").reshape(B, N, W)
    gw_blru = _mm(y_lru.reshape(RL, W), dbl.reshape(RL, D), "tn", "mm_blru_dw", out_dtype=BF16)

    dh, dyl = _lru_out_bwd(hd0, hd1, p_yl, 0, dy_lru, nct, tm, tcw)
    lam0, da0 = _lru_scan_bwd(a2, hd0, dh, 0, nct, tm, tcw)
    lam1, da1 = _lru_scan_bwd(a2, hd1, dh, 1, nct, tm, tcw)
    dxc, dwbd, acc_lru = _lru_gates_bwd(xc, wbd, bias4, lru_lam_f, (lam0, lam1), (da0, da1), tmg, bw)
    dxl = _dwconv(dxc, 0, W, lru_conv_f, None, tuple(-o for o in conv_offs), nct, tm, tcw, "lru_conv_dx", BF16)
    acc_lc = _dwconv_wgrad(dxc, p_xl, 0, W, conv_offs, nct, tm, tcw, "lru_conv_dw")

    do, dz, acc_on = _dn_out_bwd(o2, p_z, 0, dn_onorm, dy_dn, H, hd, nct, tm)
    dqkvn0, dqkvn1, dgb0, dgb1 = _delta_bwd(qkvn, gb3, (s_in0, s_in1), (tm_d0, tm_d1), do, H, hd, ncc)
    dcv = _dn_act_bwd(cv, (dqkvn0, dqkvn1), H, hd, tm)
    dqkv = _dwconv(dcv, 0, 3 * HD, dn_conv_f, None, tuple(-o for o in conv_offs), nct, tm, tcc, "dn_conv_dx", BF16)
    acc_dc = _dwconv_wgrad(dcv, p_qkv, 0, 3 * HD, conv_offs, nct, tm, tcc, "dn_conv_dw")
    dp_ab, acc_ab = _ab_act_bwd(p_ab, gb, (dgb0.reshape(R, LANES), dgb1.reshape(R, LANES)), alog_v, dtb_v, nd, tm)

    groups = [(dqkv.reshape(R, 3 * HD), w_qkv, "qkv"), (dz.reshape(R, HD), w_z, "z"), (dp_ab, w_ab, "ab"),
              (dxl.reshape(R, W), w_xl, "xl"), (dyl.reshape(R, W), w_yl, "yl"), (dmg.reshape(R, 2 * D), w_mg, "mg")]
    du1 = _mm_sum([(dg_, wg_) for dg_, wg_, _ in groups], "mm_in_dx")
    gw_groups = [_mm(dg_, u1f, "tn", "mm_in_dw_" + nm, out_dtype=BF16) for dg_, _, nm in groups]
    gw_groups[2] = gw_groups[2][:2 * nd]
    gw_in = jnp.concatenate(gw_groups, axis=0).reshape(N_DEV, csh, D)
    grad_x, acc1x, acc1c = _norm_mod_bwd(h0, du1.reshape(B, T, D), g_pre_mix, mods, 1, nct, tm, dh1, "norm_mod1_bwd")

    g_lru_conv = jnp.sum(acc_lc[:, :4], 0)
    g_dn_conv = jnp.sum(acc_dc[:, :4], 0)
    g_ffn_dw = jnp.sum(acc_ffn[:, :9], 0).reshape(3, 3, Cf)
    sm_names = ["dn_conv", "lru_conv", "lru_b_rg", "lru_b_ig", "lru_lambda", "ffn_dw"]
    sm_parts = [_col_shards(g_dn_conv), _col_shards(g_lru_conv),
                _col_shards(jnp.stack([acc_lru[0], acc_lru[2]])), _col_shards(jnp.stack([acc_lru[1], acc_lru[3]])),
                _col_shards(acc_lru[4:6]), _col_shards(g_ffn_dw)]
    late8 = [gw_in, gw_bdn.reshape(N_DEV, HD // N_DEV, D), gw_blru.reshape(N_DEV, W // N_DEV, D),
             gw_out.reshape(N_DEV, D // N_DEV, D), _pack(sm_parts, lead=(N_DEV,))]
    p_late = _split_start(late8, "pair", "rs_pair_mix_start")
    results = {}

    zeros_d = jnp.zeros((B, D), F32) + p_late[4][0:1, 0:1]
    dm_rows = jnp.stack([acc1x[:, 0], acc1x[:, 1], acc_r1[:, 0], acc2x[:, 0], acc2x[:, 1], acc_f[:, 0]], axis=1)
    dm_ctx = jnp.stack([jnp.sum(acc1c[:, 0], 0), jnp.sum(acc1c[:, 1], 0)] + [zeros_d[0]] * 4, axis=0)[None]
    dm_pad = jnp.zeros((SUBLANES, 6 * D), F32).at[:B + 1].set(jnp.concatenate([dm_rows, dm_ctx], 0).reshape(B + 1, 6 * D))
    dm_g = _all_gather([dm_pad], "ag_dmods")[0]
    g_mine = lax.dynamic_slice(dm_g, (0, 0, me * sw), (N_DEV, SUBLANES, sw)).reshape(N_DEV * SUBLANES, sw)
    gw_ada, dcc = _ada_bwd(ccp, w_ada[0], g_mine, c_ctx.reshape(1, D), B)

    q_late = _split_start(pair_sums(p_late, dcc, "mix"), "quad", "rs_quad_mix_start")
    ffn_x, ffn_land = _split_wait(q_early, "quad", q_late[4], "rs_quad_ffn_wait")
    for n, x4, land in zip(["w_up", "w_down"], ffn_x, ffn_land):
        results[n] = list(_adamw(params[n], mom1[n], mom2[n], own_slab(land, x4), "adamw_" + n))

    per = bw // bdim

    def diag_blocks(dwf):
        g6 = dwf.reshape(W // bw, per, bdim, per, bdim)
        return jnp.einsum("tpiqj,pq->tpij", g6, jnp.eye(per, dtype=F32)).reshape(nblk, bdim, bdim)

    g_rep = dict(
        c_ctx=dcc[0],
        g_pre_mix=jnp.sum(acc1x[:, 2] + acc1c[:, 2], 0)[None], g_post_mix=jnp.sum(acc_r1[:, 1], 0)[None],
        g_pre_ffn=jnp.sum(acc2x[:, 2], 0)[None], g_post_ffn=jnp.sum(acc_f[:, 1], 0)[None],
        b_merge=jnp.sum(acc_mg[:, 0], 0)[None],
        dn_a_log=acc_ab[0, :nd].reshape(1, 2, H), dn_dt_bias=acc_ab[1, :nd].reshape(1, 2, H),
        dn_onorm=jnp.sum(acc_on[:, 0], 0)[None],
        lru_conv_b=jnp.sum(acc_lc[:, 4], 0)[None],
        lru_w_rg=jnp.stack([diag_blocks(dwbd[0]), diag_blocks(dwbd[2])])[None],
        lru_w_ig=jnp.stack([diag_blocks(dwbd[1]), diag_blocks(dwbd[3])])[None],
        ffn_dw_b=jnp.sum(acc_ffn[:, 9], 0)[None])
    mat_names = ["lru_w_rg", "lru_w_ig"]
    vec_names = [n for n in g_rep if n not in mat_names]
    vec_parts, mat_parts = _all_gather([_pack([g_rep[n] for n in vec_names]),
                                        _pack([g_rep[n] for n in mat_names]).astype(BF16)], "ag_rep_grads")
    for grp, parts, nm in ((vec_names, vec_parts, "adamw_rep_vec"), (mat_names, mat_parts, "adamw_rep_mat")):
        out4 = _adamw(_pack([params[n] for n in grp]), _pack([mom1[n] for n in grp]),
                      _pack([mom2[n] for n in grp]), parts, nm)
        for k, buf in enumerate(out4):
            for n, arr in zip(grp, _unpack(buf, [params[n].shape for n in grp])):
                results.setdefault(n, [None] * 4)[k] = arr

    ba_out = _adamw(_pack([b_ada]), _pack([m_b_ada]), _pack([v_b_ada]),
                    _pack([dm_g.reshape(N_DEV * SUBLANES, 6 * D)], lead=(N_DEV * SUBLANES,)), "adamw_b_ada")
    results["b_ada"] = [_unpack(buf, [b_ada.shape])[0] for buf in ba_out]

    wa_out = _adamw(w_ada, m_w_ada, v_w_ada, gw_ada[None], "adamw_w_ada")
    results["w_ada"] = list(wa_out)

    mix_x, mix_land = _split_wait(q_late, "quad", wa_out[0], "rs_quad_mix_wait")
    recv4 = [own_slab(land, x4) for land, x4 in zip(mix_land, mix_x)]
    for n, g4 in zip(["w_in", "w_branch_dn", "w_branch_lru", "w_out"], recv4[:-1]):
        if n == "w_in":
            g4 = jnp.swapaxes(g4, 1, 2)
        results[n] = list(_adamw(params[n], mom1[n], mom2[n], g4, "adamw_" + n))
    sm_out = _adamw(_pack([params[n] for n in sm_names]), _pack([mom1[n] for n in sm_names]),
                    _pack([mom2[n] for n in sm_names]), recv4[-1], "adamw_small")
    for k, buf in enumerate(sm_out):
        for n, arr in zip(sm_names, _unpack(buf, [params[n].shape for n in sm_names])):
            results.setdefault(n, [None] * 4)[k] = arr

    outs = [loss, grad_x]
    for k in range(4):
        outs += [results[n][k] for n in names]
    return tuple(outs)
```

```python
import math

import jax
import jax.numpy as jnp
from jax import lax
from jax.experimental import pallas as pl
from jax.experimental.pallas import tpu as pltpu

F32 = jnp.float32
BF16 = jnp.bfloat16

EPS = 1e-6
GRID_W = 64
CHUNK = 64
LRU_C = 8.0
N_DEV = 8
ADAM_LR = 0.001
ADAM_B1 = 0.9
ADAM_B2 = 0.999
ADAM_EPS = 1e-08
ADAM_WD = 0.01
ADAM_STEP = 10

LANES = 128
SUBLANES = 8
VMEM_LIMIT = 48 * 1024 * 1024
MESH_AXES = ("x", "y", "c")
GELU_C = math.sqrt(2.0 / math.pi)


def _cparams(sem=None, vmem=None):
    kw = {}
    if sem is not None:
        kw["dimension_semantics"] = sem
    if vmem is not None:
        kw["vmem_limit_bytes"] = vmem
    return pltpu.CompilerParams(**kw)


def _tile(n, pref):
    if n <= pref:
        return n
    t = (pref // LANES) * LANES
    while n % t:
        t -= LANES
    return t


def _sigmoid(x):
    return 1.0 / (1.0 + jnp.exp(-x))


def _silu(x):
    return x * _sigmoid(x)


def _dsilu(x):
    s = _sigmoid(x)
    return s * (1.0 + x * (1.0 - s))


def _softplus(x):
    return jnp.maximum(x, 0.0) + jnp.log(1.0 + jnp.exp(-jnp.abs(x)))


def _gelu(x):
    return 0.5 * x * (1.0 + jnp.tanh(GELU_C * (x + 0.044715 * x * x * x)))


def _dgelu(x):
    t = jnp.tanh(GELU_C * (x + 0.044715 * x * x * x))
    return 0.5 * (1.0 + t) + 0.5 * x * (1.0 - t * t) * GELU_C * (1.0 + 3.0 * 0.044715 * x * x)


def _dot(a, b, dims="nn"):
    dn = {"nn": (((1,), (0,)), ((), ())),
          "nt": (((1,), (1,)), ((), ())),
          "tn": (((0,), (0,)), ((), ()))}[dims]
    return lax.dot_general(a.astype(BF16), b.astype(BF16), dn, preferred_element_type=F32)


def _split2(x):
    hi = x.astype(BF16)
    lo = (x - hi.astype(F32)).astype(BF16)
    return hi, lo


def _split3(x):
    h1 = x.astype(BF16)
    r1 = x - h1.astype(F32)
    h2 = r1.astype(BF16)
    h3 = (r1 - h2.astype(F32)).astype(BF16)
    return h1, h2, h3


def _dot_hi(a, b):
    ah, al = _split2(a)
    bh, bl = _split2(b)
    return _dot(ah, bh) + (_dot(ah, bl) + _dot(al, bh))


def _dot_mask(m, x, dims="nn"):
    h1, h2, h3 = _split3(x)
    if dims == "xtn":
        return _dot(h1, m, "tn") + (_dot(h2, m, "tn") + _dot(h3, m, "tn"))
    return _dot(m, h1, dims) + (_dot(m, h2, dims) + _dot(m, h3, dims))


def _my_index():
    return 4 * lax.axis_index("x") + 2 * lax.axis_index("y") + lax.axis_index("c")


def _hbm_call(body, xs, out_shapes, n_sems, name):
    any_spec = pl.BlockSpec(memory_space=pl.ANY)
    return pl.pallas_call(
        body, name=name, out_shape=tuple(out_shapes),
        in_specs=[any_spec] * len(xs), out_specs=tuple([any_spec] * len(out_shapes)),
        scratch_shapes=[pltpu.SemaphoreType.DMA((n_sems,)), pltpu.SemaphoreType.DMA((n_sems,)),
                        pltpu.SemaphoreType.DMA((len(xs),))],
    )(*xs)


def _all_gather(xs, name):
    n = len(xs)

    def body(*refs):
        x_refs, out_refs = refs[:n], refs[n:2 * n]
        send_sems, recv_sems, local_sems = refs[2 * n:]
        x_, y_, c_ = lax.axis_index("x"), lax.axis_index("y"), lax.axis_index("c")
        me, sibling = (x_, y_, c_), (x_, y_, 1 - c_)
        chips = [(1 - x_, y_), (x_, 1 - y_), (1 - x_, 1 - y_)]

        def slab(a, px, py, pc):
            return out_refs[a].at[4 * px + 2 * py + pc]

        def copy(a, k, block, to, src=None):
            return pltpu.make_async_remote_copy(
                src_ref=slab(a, *block) if src is None else src, dst_ref=slab(a, *block),
                send_sem=send_sems.at[7 * a + k], recv_sem=recv_sems.at[7 * a + k],
                device_id=to, device_id_type=pl.DeviceIdType.MESH)

        mine = [pltpu.make_async_copy(x_refs[a], slab(a, *me), local_sems.at[a]) for a in range(n)]
        for cp in mine:
            cp.start()
        sent = []
        for j, chip in enumerate(chips):
            sent += [copy(a, 1 + j, me, (*chip, c_), src=x_refs[a]) for a in range(n)]
        sent += [copy(a, 0, me, sibling, src=x_refs[a]) for a in range(n)]
        for cp in sent:
            cp.start()
        for j, chip in enumerate(chips):
            for a in range(n):
                copy(a, 1 + j, (*chip, c_), me).wait_recv()
                fwd = copy(a, 4 + j, (*chip, c_), sibling)
                fwd.start()
                sent.append(fwd)
        for a in range(n):
            copy(a, 0, sibling, me).wait_recv()
        for j, chip in enumerate(chips):
            for a in range(n):
                copy(a, 4 + j, (*chip, 1 - c_), me).wait_recv()
        for cp in sent:
            cp.wait_send()
        for cp in mine:
            cp.wait()

    outs = [jax.ShapeDtypeStruct((N_DEV,) + x.shape, x.dtype) for x in xs]
    return _hbm_call(body, xs, outs, 7 * n, name)


SPLIT_COPIES = {"gather": 7, "quad": 3, "pair": 4}


def _split_views(kind, x_ref, land_ref, k):
    x_, y_, c_ = lax.axis_index("x"), lax.axis_index("y"), lax.axis_index("c")
    if kind == "pair":
        return x_ref.at[2 * (k - 1) + (1 - c_)], land_ref.at[k - 1], land_ref.at[k - 1], (x_, y_, 1 - c_)
    if kind == "gather":
        px = 1 - x_ if (k >> 2) & 1 else x_
        py = 1 - y_ if (k >> 1) & 1 else y_
        pc = 1 - c_ if k & 1 else c_
        return x_ref, land_ref.at[4 * x_ + 2 * y_ + c_], land_ref.at[4 * px + 2 * py + pc], (px, py, pc)
    px = 1 - x_ if (k >> 1) & 1 else x_
    py = 1 - y_ if k & 1 else y_
    return x_ref.at[2 * px + py], land_ref.at[2 * x_ + y_], land_ref.at[2 * px + py], (px, py, c_)


def _split_start(xs, kind, name):
    n = len(xs)
    npeer = SPLIT_COPIES[kind]
    land_shape = {"gather": lambda x: (N_DEV,) + x.shape, "quad": lambda x: x.shape, "pair": lambda x: (4,) + x.shape[1:]}
    lands = [lax.empty(land_shape[kind](x), x.dtype) for x in xs]

    def body(*refs):
        x_refs, land_refs = refs[:n], refs[n:2 * n]
        send_sems, recv_sems, token = refs[2 * n], refs[2 * n + 1], refs[4 * n + 2]
        for k in range(1, npeer + 1):
            for a in range(n):
                src, dst, _, peer = _split_views(kind, x_refs[a], land_refs[a], k)
                pltpu.make_async_remote_copy(
                    src_ref=src, dst_ref=dst, send_sem=send_sems.at[npeer * a + k - 1],
                    recv_sem=recv_sems.at[npeer * a + k - 1],
                    device_id=peer, device_id_type=pl.DeviceIdType.MESH).start()
        token[...] = jnp.zeros_like(token)

    hbm = pl.BlockSpec(memory_space=pltpu.HBM)
    sem = pl.BlockSpec(memory_space=pltpu.SEMAPHORE)
    sems = pltpu.SemaphoreType.DMA((npeer * n,))
    out = pl.pallas_call(
        body, name=name,
        out_shape=(sems, sems, *[pltpu.HBM(a.shape, a.dtype) for a in list(xs) + lands],
                   jax.ShapeDtypeStruct((SUBLANES, LANES), F32)),
        in_specs=[hbm] * (2 * n),
        out_specs=(sem, sem, *[hbm] * (2 * n), pl.BlockSpec(memory_space=pltpu.VMEM)),
        input_output_aliases={i: 2 + i for i in range(2 * n)},
        compiler_params=pltpu.CompilerParams(has_side_effects=pltpu.SideEffectType.DATAFLOW_SIDE_EFFECTING),
    )(*[pltpu.with_memory_space_constraint(a, pltpu.HBM) for a in list(xs) + lands])
    return out[0], out[1], list(out[2:2 + n]), list(out[2 + n:2 + 2 * n]), out[-1]


def _split_wait(started, kind, after, name):
    send_sems_, recv_sems_, x_thru, land_thru, _ = started
    n = len(x_thru)
    npeer = SPLIT_COPIES[kind]

    def body(*refs):
        x_refs, land_refs = refs[:n], refs[n:2 * n]
        send_sems, recv_sems = refs[2 * n], refs[2 * n + 1]
        for k in range(1, npeer + 1):
            for a in range(n):
                src, _, landed, peer = _split_views(kind, x_refs[a], land_refs[a], k)
                cp = pltpu.make_async_remote_copy(
                    src_ref=src, dst_ref=landed, send_sem=send_sems.at[npeer * a + k - 1],
                    recv_sem=recv_sems.at[npeer * a + k - 1],
                    device_id=peer, device_id_type=pl.DeviceIdType.MESH)
                cp.wait_send()
                cp.wait_recv()

    hbm = pl.BlockSpec(memory_space=pltpu.HBM)
    sem = pl.BlockSpec(memory_space=pltpu.SEMAPHORE)
    out = pl.pallas_call(
        body, name=name,
        out_shape=tuple(pltpu.HBM(a.shape, a.dtype) for a in x_thru + land_thru),
        in_specs=[hbm] * (2 * n) + [sem, sem, pl.BlockSpec(memory_space=pl.ANY)],
        out_specs=tuple([hbm] * (2 * n)),
        input_output_aliases={i: i for i in range(2 * n)},
        compiler_params=pltpu.CompilerParams(has_side_effects=pltpu.SideEffectType.DATAFLOW_SIDE_EFFECTING),
    )(*x_thru, *land_thru, send_sems_, recv_sems_, after)
    return list(out[:n]), list(out[n:])


def _pair_sum(x8, r4, name):
    _, r, c = x8.shape
    tr = _row_tile(r, c * 12)

    def body(core_ref, x_ref, r_ref, o_ref):
        o_ref[...] = (x_ref[...].astype(F32) + r_ref[...].astype(F32)).astype(o_ref.dtype)

    core = lax.axis_index("c").astype(jnp.int32).reshape(1)
    return pl.pallas_call(
        body, name=name,
        grid_spec=pltpu.PrefetchScalarGridSpec(
            num_scalar_prefetch=1, grid=(4, r // tr),
            in_specs=[pl.BlockSpec((None, tr, c), lambda j, i, core_ref: (2 * j + core_ref[0], i, 0)),
                      pl.BlockSpec((None, tr, c), lambda j, i, core_ref: (j, i, 0))],
            out_specs=pl.BlockSpec((None, tr, c), lambda j, i, core_ref: (j, i, 0))),
        out_shape=jax.ShapeDtypeStruct((4, r, c), x8.dtype),
        compiler_params=_cparams(("parallel", "parallel"), VMEM_LIMIT))(core, x8, r4)


def _row_tile(r, bytes_per_row, budget=4 * 1024 * 1024):
    best = None
    for t in range(16, r + 1, 16):
        if r % t == 0 and t * bytes_per_row <= budget:
            best = t
    return best if best is not None else r


def _mm(a, b, dims, name, out_dtype=F32, out_shards=1, tm=1024, tn=1024, tk=1024):
    S = b.shape[0] if b.ndim == 3 else 1
    bshape = (b.shape[1], S * b.shape[2]) if b.ndim == 3 else b.shape
    if dims == "nn":
        (M, K), (K2, N) = a.shape, bshape
    elif dims == "nt":
        (M, K), (N, K2) = a.shape, bshape
    else:
        (K, M), (K2, N) = a.shape, bshape
    assert K == K2, (a.shape, b.shape, dims)
    cs = bshape[1] // S
    tm, tk = _tile(M, tm), _tile(K, min(tk, cs) if (S > 1 and dims == "nt") else tk)
    tn = _tile(N, min(tn, cs) if (S > 1 and dims != "nt") else min(tn, N // out_shards))
    assert cs % (tk if dims == "nt" else tn) == 0 and (N // out_shards) % tn == 0
    nk = K // tk

    def body(a_ref, b_ref, o_ref, acc):
        k = pl.program_id(2)

        @pl.when(k == 0)
        def _():
            acc[...] = jnp.zeros_like(acc)

        acc[...] += _dot(a_ref[...], b_ref[...], dims)

        @pl.when(k == nk - 1)
        def _():
            o_ref[...] = acc[...].astype(out_dtype)

    a_spec =(pl.BlockSpec((tk, tm), lambda i, j, k: (k, i)) if dims == "tn"
              else pl.BlockSpec((tm, tk), lambda i, j, k: (i, k)))
    if S == 1:
        b_spec = (pl.BlockSpec((tn, tk), lambda i, j, k: (j, k)) if dims == "nt"
                  else pl.BlockSpec((tk, tn), lambda i, j, k: (k, j)))
    elif dims == "nt":
        per = cs // tk
        b_spec = pl.BlockSpec((None, tn, tk), lambda i, j, k: (k // per, j, k % per))
    else:
        per = cs // tn
        b_spec = pl.BlockSpec((None, tk, tn), lambda i, j, k: (j // per, k, j % per))
    if out_shards == 1:
        out_spec, out_shape = pl.BlockSpec((tm, tn), lambda i, j, k: (i, j)), (M, N)
    else:
        oper = N // out_shards // tn
        out_spec = pl.BlockSpec((None, tm, tn), lambda i, j, k: (j // oper, i, j % oper))
        out_shape = (out_shards, M, N // out_shards)
    return pl.pallas_call(
        body, name=name, grid=(M // tm, N // tn, nk),
        in_specs=[a_spec, b_spec], out_specs=out_spec,
        out_shape=jax.ShapeDtypeStruct(out_shape, out_dtype),
        scratch_shapes=[pltpu.VMEM((tm, tn), F32)],
        compiler_params=_cparams(("parallel", "parallel", "arbitrary"), VMEM_LIMIT),
    )(a, b)


VMEM_LIMIT_SUM = 56 * 1024 * 1024


def _mm_sum(pairs, name, out_dtype=F32, tm=768, tn=1024, tk=1024):
    M, N = pairs[0][0].shape[0], pairs[0][1].shape[1]
    tm, tn = _tile(M, tm), _tile(N, tn)
    tks = [_tile(a.shape[1], tk) for a, _ in pairs]
    nks = [a.shape[1] // t for (a, _), t in zip(pairs, tks)]
    starts = [sum(nks[:g]) for g in range(len(pairs))]
    nk = sum(nks)
    G = len(pairs)

    def body(*refs):
        o_ref, acc = refs[2 * G], refs[2 * G + 1]
        k = pl.program_id(2)

        @pl.when(k == 0)
        def _():
            acc[...] = jnp.zeros_like(acc)

        for g in range(G):
            @pl.when(jnp.logical_and(k >= starts[g], k < starts[g] + nks[g]))
            def _(g=g):
                acc[...] += _dot(refs[2 * g][...], refs[2 * g + 1][...])

        @pl.when(k == nk - 1)
        def _():
            o_ref[...] = acc[...].astype(out_dtype)

    in_specs, args = [], []
    for g, (a, b) in enumerate(pairs):
        kk = lambda k, g=g: jnp.clip(k - starts[g], 0, nks[g] - 1)
        in_specs += [pl.BlockSpec((tm, tks[g]), lambda i, j, k, kk=kk: (i, kk(k))),
                     pl.BlockSpec((tks[g], tn), lambda i, j, k, kk=kk: (kk(k), j))]
        args += [a, b]
    return pl.pallas_call(
        body, name=name, grid=(M // tm, N // tn, nk),
        in_specs=in_specs, out_specs=pl.BlockSpec((tm, tn), lambda i, j, k: (i, j)),
        out_shape=jax.ShapeDtypeStruct((M, N), out_dtype),
        scratch_shapes=[pltpu.VMEM((tm, tn), F32)],
        compiler_params=_cparams(("parallel", "parallel", "arbitrary"), VMEM_LIMIT_SUM),
    )(*args)


def _ada_fwd(ccp, w, b):
    def body(c_ref, w_ref, b_ref, o_ref):
        o_ref[...] = _dot(_silu(c_ref[...]), w_ref[...]) + b_ref[...]

    return pl.pallas_call(
        body, name="ada_fwd", out_shape=jax.ShapeDtypeStruct((ccp.shape[0], w.shape[1]), F32),
        compiler_params=_cparams(None, VMEM_LIMIT))(ccp, w, b)


def _ada_bwd(ccp, w, g, cctx, n_loc):
    def body(c_ref, w_ref, g_ref, cc_ref, gw_ref, dc_ref):
        gw_ref[...] = _dot(_silu(c_ref[...]), g_ref[...], "tn")
        dcc = _dot(g_ref[...], w_ref[...], "nt")
        row = lax.broadcasted_iota(jnp.int32, dcc.shape, 0)
        part = jnp.sum(jnp.where(row % SUBLANES == n_loc, dcc, 0.0), axis=0, keepdims=True)
        dc_ref[...] = jnp.broadcast_to(part * _dsilu(cc_ref[...]), dc_ref.shape)

    D = ccp.shape[1]
    return pl.pallas_call(
        body, name="ada_bwd",
        out_shape=(jax.ShapeDtypeStruct(w.shape, F32), jax.ShapeDtypeStruct((SUBLANES, D), F32)),
        compiler_params=_cparams(None, VMEM_LIMIT))(ccp, w, g, cctx)


def _norm_mod_fwd(h, gain, mods, i_shift, i_scale, nct, tm, name):
    B, T, D = h.shape

    def body(h_ref, g_ref, m_ref, u_ref):
        x = h_ref[0]
        r = lax.rsqrt(jnp.mean(x * x, axis=-1, keepdims=True) + EPS)
        n = x * r * g_ref[...]
        u_ref[0] = (n * (1.0 + m_ref[0, i_scale:i_scale + 1, :]) + m_ref[0, i_shift:i_shift + 1, :]).astype(BF16)

    return pl.pallas_call(
        body, name=name, grid=(B, T // tm),
        in_specs=[pl.BlockSpec((1, tm, D), lambda b, t: (b, t, 0)),
                  pl.BlockSpec((1, D), lambda b, t: (0, 0)),
                  pl.BlockSpec((1, SUBLANES, D), lambda b, t: (jnp.where(t < nct, B, b), 0, 0))],
        out_specs=pl.BlockSpec((1, tm, D), lambda b, t: (b, t, 0)),
        out_shape=jax.ShapeDtypeStruct((B, T, D), BF16),
        compiler_params=_cparams(("parallel", "parallel"), VMEM_LIMIT),
    )(h, gain, mods)


def _norm_mod_bwd(h, du, gain, mods, i_scale, nct, tm, res, name):
    B, T, D = h.shape
    nt = T // tm

    def body(h_ref, du_ref, g_ref, m_ref, res_ref, dx_ref, ax_ref, ac_ref):
        t = pl.program_id(1)
        x = h_ref[0]
        r = lax.rsqrt(jnp.mean(x * x, axis=-1, keepdims=True) + EPS)
        nh = x * r
        g = g_ref[...]
        du_ = du_ref[0]
        dn = du_ * (1.0 + m_ref[0, i_scale:i_scale + 1, :])
        dnh = dn * g
        dx = r * (dnh - nh * jnp.mean(dnh * nh, axis=-1, keepdims=True))
        sums = (jnp.sum(du_, axis=0, keepdims=True), jnp.sum(du_ * nh * g, axis=0, keepdims=True),
                jnp.sum(dn * nh, axis=0, keepdims=True))

        @pl.when(t == 0)
        def _():
            ax_ref[...] = jnp.zeros_like(ax_ref)
            ac_ref[...] = jnp.zeros_like(ac_ref)

        def accumulate(ref):
            for i, s in enumerate(sums):
                ref[0, i:i + 1, :] += s

        @pl.when(t < nct)
        def _():
            accumulate(ac_ref)

        @pl.when(t >= nct)
        def _():
            accumulate(ax_ref)
            dx_ref[0] = dx + res_ref[0]

    lat = lambda b, t: (b, jnp.maximum(t - nct, 0), 0)
    acc = pl.BlockSpec((1, SUBLANES, D), lambda b, t: (b, 0, 0))
    return pl.pallas_call(
        body, name=name, grid=(B, nt),
        in_specs=[pl.BlockSpec((1, tm, D), lambda b, t: (b, t, 0)),
                  pl.BlockSpec((1, tm, D), lambda b, t: (b, t, 0)),
                  pl.BlockSpec((1, D), lambda b, t: (0, 0)),
                  pl.BlockSpec((1, SUBLANES, D), lambda b, t: (jnp.where(t < nct, B, b), 0, 0)),
                  pl.BlockSpec((1, tm, D), lat)],
        out_specs=(pl.BlockSpec((1, tm, D), lat), acc, acc),
        out_shape=(jax.ShapeDtypeStruct(res.shape, F32),
                   jax.ShapeDtypeStruct((B, SUBLANES, D), F32), jax.ShapeDtypeStruct((B, SUBLANES, D), F32)),
        compiler_params=_cparams(("parallel", "arbitrary"), VMEM_LIMIT),
    )(h, du, gain, mods, res)


def _resid_norm_fwd(x, y, gain, mods, i_gate, tm, name):
    B, N, D = x.shape

    def body(x_ref, y_ref, g_ref, m_ref, o_ref):
        y_ = y_ref[0]
        r = lax.rsqrt(jnp.mean(y_ * y_, axis=-1, keepdims=True) + EPS)
        o_ref[0] = x_ref[0] + y_ * r * g_ref[...] * m_ref[0, i_gate:i_gate + 1, :]

    row = pl.BlockSpec((1, tm, D), lambda b, t: (b, t, 0))
    return pl.pallas_call(
        body, name=name, grid=(B, N // tm),
        in_specs=[row, row, pl.BlockSpec((1, D), lambda b, t: (0, 0)),
                  pl.BlockSpec((1, SUBLANES, D), lambda b, t: (b, 0, 0))],
        out_specs=row, out_shape=jax.ShapeDtypeStruct((B, N, D), F32),
        compiler_params=_cparams(("parallel", "parallel"), VMEM_LIMIT),
    )(x, y, gain, mods)


def _resid_norm_bwd_math(y_, dh, g, gate):
    r = lax.rsqrt(jnp.mean(y_ * y_, axis=-1, keepdims=True) + EPS)
    nh = y_ * r
    dgate = jnp.sum(dh * nh * g, axis=0, keepdims=True)
    dn = dh * gate
    dgain = jnp.sum(dn * nh, axis=0, keepdims=True)
    dnh = dn * g
    dy = r * (dnh - nh * jnp.mean(dnh * nh, axis=-1, keepdims=True))
    return dy, dgate, dgain


def _resid_norm_bwd(y, dh, gain, mods, i_gate, tm, name):
    B, N, D = y.shape

    def body(y_ref, dh_ref, g_ref, m_ref, dy_ref, acc_ref):
        t = pl.program_id(1)
        dy, dgate, dgain = _resid_norm_bwd_math(y_ref[0], dh_ref[0], g_ref[...], m_ref[0, i_gate:i_gate + 1, :])
        dy_ref[0] = dy.astype(BF16)

        @pl.when(t == 0)
        def _():
            acc_ref[...] = jnp.zeros_like(acc_ref)

        acc_ref[0, 0:1, :] += dgate
        acc_ref[0, 1:2, :] += dgain

    row = pl.BlockSpec((1, tm, D), lambda b, t: (b, t, 0))
    return pl.pallas_call(
        body, name=name, grid=(B, N // tm),
        in_specs=[row, row, pl.BlockSpec((1, D), lambda b, t: (0, 0)),
                  pl.BlockSpec((1, SUBLANES, D), lambda b, t: (b, 0, 0))],
        out_specs=(row, pl.BlockSpec((1, SUBLANES, D), lambda b, t: (b, 0, 0))),
        out_shape=(jax.ShapeDtypeStruct((B, N, D), BF16), jax.ShapeDtypeStruct((B, SUBLANES, D), F32)),
        compiler_params=_cparams(("parallel", "arbitrary"), VMEM_LIMIT),
    )(y, dh, gain, mods)


def _final_fwd_bwd(h1, d, gain, mods, i_gate, tgt, tm):
    B, N, D = h1.shape

    def body(h_ref, d_ref, g_ref, m_ref, t_ref, dd_ref, dh_ref, acc_ref):
        t = pl.program_id(1)
        d_ = d_ref[0]
        g = g_ref[...]
        gate = m_ref[0, i_gate:i_gate + 1, :]
        r = lax.rsqrt(jnp.mean(d_ * d_, axis=-1, keepdims=True) + EPS)
        e = h_ref[0] + d_ * r * g * gate - t_ref[0]
        dh = e * (1.0 / D)
        dh_ref[0] = dh
        dy, dgate, dgain = _resid_norm_bwd_math(d_, dh, g, gate)
        dd_ref[0] = dy.astype(BF16)

        @pl.when(t == 0)
        def _():
            acc_ref[...] = jnp.zeros_like(acc_ref)

        acc_ref[0, 0:1, :] += dgate
        acc_ref[0, 1:2, :] += dgain
        acc_ref[0, 2:3, :] += jnp.sum(e * e, axis=0, keepdims=True)

    row = pl.BlockSpec((1, tm, D), lambda b, t: (b, t, 0))
    return pl.pallas_call(
        body, name="final_fwd_bwd", grid=(B, N // tm),
        in_specs=[row, row, pl.BlockSpec((1, D), lambda b, t: (0, 0)),
                  pl.BlockSpec((1, SUBLANES, D), lambda b, t: (b, 0, 0)), row],
        out_specs=(row, row, pl.BlockSpec((1, SUBLANES, D), lambda b, t: (b, 0, 0))),
        out_shape=(jax.ShapeDtypeStruct((B, N, D), BF16), jax.ShapeDtypeStruct((B, N, D), F32),
                   jax.ShapeDtypeStruct((B, SUBLANES, D), F32)),
        compiler_params=_cparams(("parallel", "arbitrary"), VMEM_LIMIT),
    )(h1, d, gain, mods, tgt)


def _shifted(x, prev, nxt, off, row, tm):
    if off == 0:
        return x
    if off < 0:
        y = pltpu.roll(x, -off, 0)
        for r in range(-off):
            y = jnp.where(row == r, prev[SUBLANES + r + off:SUBLANES + r + off + 1, :], y)
        return y
    y = pltpu.roll(x, tm - off, 0)
    for r in range(off):
        y = jnp.where(row == tm - off + r, nxt[r:r + 1, :], y)
    return y


def _halo_specs(T, tm, tc, cb0, order):
    r8, n8 = tm // SUBLANES, T // SUBLANES
    if order == "btj":
        prev = lambda b, t, j: (b, jnp.maximum(t * r8 - 1, 0), cb0 + j)
        nxt = lambda b, t, j: (b, jnp.minimum((t + 1) * r8, n8 - 1), cb0 + j)
    else:
        prev = lambda b, j, t: (b, jnp.maximum(t * r8 - 1, 0), cb0 + j)
        nxt = lambda b, j, t: (b, jnp.minimum((t + 1) * r8, n8 - 1), cb0 + j)
    return pl.BlockSpec((1, SUBLANES, tc), prev), pl.BlockSpec((1, SUBLANES, tc), nxt)


def _seg_halos(p_ref, n_ref, t, nct, nt):
    seg_start = jnp.logical_or(t == 0, t == nct)
    seg_end = jnp.logical_or(t == nct - 1, t == nt - 1)
    prev = jnp.where(seg_start, 0.0, p_ref[0])
    nxt = jnp.where(seg_end, 0.0, n_ref[0])
    return prev, nxt


def _dwconv(x, col0, C, w, bias, offs, nct, tm, tc, name, out_dtype=F32, qkv_heads=None):
    B, T, _ = x.shape
    nt = T // tm
    cb0 = col0 // tc
    if qkv_heads is not None:
        assert tc == qkv_heads[0] * qkv_heads[1] and C == 3 * tc

    def body(*refs):
        refs = list(refs)
        a_ref = refs.pop() if qkv_heads is not None else None
        if bias is None:
            x_ref, p_ref, n_ref, w_ref, o_ref = refs
        else:
            x_ref, p_ref, n_ref, w_ref, b_ref, o_ref = refs
        t = pl.program_id(1)
        prev, nxt = _seg_halos(p_ref, n_ref, t, nct, nt)
        x_ = x_ref[0]
        row = lax.broadcasted_iota(jnp.int32, x_.shape, 0)
        acc = None
        for j, off in enumerate(offs):
            term = _shifted(x_, prev, nxt, off, row, tm) * w_ref[j:j + 1, :]
            acc = term if acc is None else acc + term
        if bias is not None:
            acc = acc + b_ref[...]
        o_ref[0] = acc.astype(out_dtype)
        if qkv_heads is not None:
            H, hd = qkv_heads
            part = pl.program_id(2)
            for h in range(H):
                sl = slice(h * hd, (h + 1) * hd)
                s = _silu(acc[:, sl])
                rs = lax.rsqrt(jnp.sum(s * s, axis=-1, keepdims=True) + EPS)
                scale = jnp.where(part == 0, rs * (float(hd) ** -0.5), jnp.where(part == 1, rs, 1.0))
                a_ref[0, :, sl] = (s * scale).astype(a_ref.dtype)

    pspec, nspec = _halo_specs(T, tm, tc, cb0, "btj")
    in_specs = [pl.BlockSpec((1, tm, tc), lambda b, t, j: (b, t, cb0 + j)), pspec, nspec,
                pl.BlockSpec((w.shape[0], tc), lambda b, t, j: (0, j))]
    args = [x, x, x, w]
    if bias is not None:
        in_specs.append(pl.BlockSpec((1, tc), lambda b, t, j: (0, j)))
        args.append(bias)
    tile = pl.BlockSpec((1, tm, tc), lambda b, t, j: (b, t, j))
    out_specs, out_shape = tile, jax.ShapeDtypeStruct((B, T, C), out_dtype)
    if qkv_heads is not None:
        out_specs, out_shape = (tile, tile), (out_shape, jax.ShapeDtypeStruct((B, T, C), BF16))
    return pl.pallas_call(
        body, name=name, grid=(B, nt, C // tc),
        in_specs=in_specs, out_specs=out_specs, out_shape=out_shape,
        compiler_params=_cparams(("parallel", "parallel", "parallel"), VMEM_LIMIT),
    )(*args)


def _dwconv_wgrad(dy, x, col0, C, offs, nct, tm, tc, name):
    B, T, _ = x.shape
    nt = T // tm
    cb0 = col0 // tc
    K = len(offs)

    def body(dy_ref, x_ref, p_ref, n_ref, acc_ref):
        t = pl.program_id(2)
        prev, nxt = _seg_halos(p_ref, n_ref, t, nct, nt)
        x_ = x_ref[0]
        dy_ = dy_ref[0]
        row = lax.broadcasted_iota(jnp.int32, x_.shape, 0)

        @pl.when(t == 0)
        def _():
            acc_ref[...] = jnp.zeros_like(acc_ref)

        for j, off in enumerate(offs):
            acc_ref[0, j:j + 1, :] += jnp.sum(dy_ * _shifted(x_, prev, nxt, off, row, tm), axis=0, keepdims=True)
        acc_ref[0, K:K + 1, :] += jnp.sum(dy_, axis=0, keepdims=True)

    pspec, nspec = _halo_specs(T, tm, tc, cb0, "bjt")
    return pl.pallas_call(
        body, name=name, grid=(B, C // tc, nt),
        in_specs=[pl.BlockSpec((1, tm, tc), lambda b, j, t: (b, t, j)),
                  pl.BlockSpec((1, tm, tc), lambda b, j, t: (b, t, cb0 + j)), pspec, nspec],
        out_specs=pl.BlockSpec((1, SUBLANES, tc), lambda b, j, t: (b, 0, j)),
        out_shape=jax.ShapeDtypeStruct((B, SUBLANES, C), F32),
        compiler_params=_cparams(("parallel", "parallel", "arbitrary"), VMEM_LIMIT),
    )(dy, x, x, x)


def _ab_act(pab, alog, dtb, nd, tm):
    R = pab.shape[0]

    def body(p_ref, al_ref, dt_ref, o_ref):
        p = p_ref[...]
        lane = lax.broadcasted_iota(jnp.int32, p.shape, 1)
        g = -jnp.exp(al_ref[...]) * _softplus(p + dt_ref[...])
        o_ref[...] = jnp.where(lane < nd, g, jnp.where(lane < 2 * nd, _sigmoid(p), 0.0))

    row = pl.BlockSpec((tm, LANES), lambda i: (i, 0))
    vec = pl.BlockSpec((1, LANES), lambda i: (0, 0))
    return pl.pallas_call(
        body, name="ab_act", grid=(R // tm,), in_specs=[row, vec, vec], out_specs=row,
        out_shape=jax.ShapeDtypeStruct((R, LANES), F32),
        compiler_params=_cparams(("parallel",), VMEM_LIMIT))(pab, alog, dtb)


def _ab_act_bwd(pab, gb, dgb, alog, dtb, nd, tm):
    R = pab.shape[0]

    def body(p_ref, gb_ref, d0_ref, d1_ref, al_ref, dt_ref, o_ref, acc_ref):
        i = pl.program_id(0)
        p = p_ref[...]
        gb_ = gb_ref[...]
        d_ = d0_ref[...] + d1_ref[...]
        lane = lax.broadcasted_iota(jnp.int32, p.shape, 1)
        is_g = lane < nd
        is_b = jnp.logical_and(lane >= nd, lane < 2 * nd)
        da = jnp.where(is_g, d_ * (-jnp.exp(al_ref[...])) * _sigmoid(p + dt_ref[...]), 0.0)
        db = jnp.where(is_b, d_ * gb_ * (1.0 - gb_), 0.0)
        o_ref[...] = (da + db).astype(BF16)

        @pl.when(i == 0)
        def _():
            acc_ref[...] = jnp.zeros_like(acc_ref)

        acc_ref[0:1, :] += jnp.sum(jnp.where(is_g, d_ * gb_, 0.0), axis=0, keepdims=True)
        acc_ref[1:2, :] += jnp.sum(da, axis=0, keepdims=True)

    row = pl.BlockSpec((tm, LANES), lambda i: (i, 0))
    vec = pl.BlockSpec((1, LANES), lambda i: (0, 0))
    return pl.pallas_call(
        body, name="ab_act_bwd", grid=(R // tm,),
        in_specs=[row, row, row, row, vec, vec],
        out_specs=(row, pl.BlockSpec((SUBLANES, LANES), lambda i: (0, 0))),
        out_shape=(jax.ShapeDtypeStruct((R, LANES), BF16), jax.ShapeDtypeStruct((SUBLANES, LANES), F32)),
        compiler_params=_cparams(("arbitrary",), VMEM_LIMIT))(pab, gb, dgb[0], dgb[1], alog, dtb)


def _dn_act_bwd(cv, dqkv, H, hd, tm):
    B, T, C3 = cv.shape
    qscale = float(hd) ** -0.5

    def body(c_ref, d0_ref, d1_ref, o_ref):
        part = pl.program_id(0)
        for h in range(H):
            sl = slice(h * hd, (h + 1) * hd)
            c = c_ref[0, :, sl]
            s = _silu(c)
            dout = d0_ref[0, :, sl] + d1_ref[0, :, sl]
            rs = lax.rsqrt(jnp.sum(s * s, axis=-1, keepdims=True) + EPS)
            sh = s * rs
            dnorm = rs * (dout - sh * jnp.sum(dout * sh, axis=-1, keepdims=True))
            ds = jnp.where(part == 0, dnorm * qscale, jnp.where(part == 1, dnorm, dout))
            o_ref[0, :, sl] = ds * _dsilu(c)

    spec = pl.BlockSpec((1, tm, H * hd), lambda p, b, t: (b, t, p))
    return pl.pallas_call(
        body, name="dn_act_bwd", grid=(3, B, T // tm), in_specs=[spec, spec, spec], out_specs=spec,
        out_shape=jax.ShapeDtypeStruct((B, T, C3), F32),
        compiler_params=_cparams(("parallel",) * 3, VMEM_LIMIT))(cv, dqkv[0], dqkv[1])


def _chunk_of(d, s, ncc, nch):
    if d == 0:
        return s
    return jnp.where(s < ncc, ncc - 1 - s, nch - 1 - (s - ncc))


def _delta_masks(d):
    row = lax.broadcasted_iota(jnp.int32, (CHUNK, CHUNK), 0)
    col = lax.broadcasted_iota(jnp.int32, (CHUNK, CHUNK), 1)
    sign = 1 - 2 * d
    diff = (row - col) * sign
    return diff >= 0, diff > 0, diff <= 0


def _each(f, *lists):
    return [f(*a) for a in zip(*lists)]


def _unit_tri_inverse(As):
    C = As[0].shape[0]
    row = lax.broadcasted_iota(jnp.int32, (C, C), 0)
    col = lax.broadcasted_iota(jnp.int32, (C, C), 1)
    eye = jnp.where(row == col, 1.0, 0.0).astype(F32)
    same = lambda shift: jnp.right_shift(row, shift) == jnp.right_shift(col, shift)
    in8 = same(3)
    xs = [-jnp.where(in8, a, 0.0) for a in As]
    ts = [eye + x for x in xs]
    ps = _each(lambda x: _dot(x, x), xs)
    tp = _each(lambda t, p: _dot(_rows(t, p), p), ts, ps)
    ts = _each(lambda t, m: t + m[:C], ts, tp)
    ts = _each(lambda t, m: t + _dot(t, m[C:]), ts, tp)
    shift = 3
    while (1 << shift) < C:
        off = jnp.logical_and(same(shift + 1), jnp.right_shift(row, shift) != jnp.right_shift(col, shift))
        los = [jnp.where(off, a, 0.0) for a in As]
        ts = _each(lambda t, lo: t - _dot(t, _dot(lo, t)), ts, los)
        shift += 1
    def residual(a, t):
        (ah, al), (th, tl) = _split2(eye + a), _split2(t)
        m = _dot(_rows(ah, al), th)
        return eye - (m[:C] + (m[C:] + _dot(ah, tl)))

    rs = _each(residual, As, ts)
    return _each(lambda t, r: t + _dot(t, r), ts, rs)


def _rows(*xs):
    return jnp.concatenate(xs, axis=0)


def _cols(*xs):
    return jnp.concatenate(xs, axis=1)


def _delta_chunk_fwd(q, k, v, g_i, g_j, beta_i, gl, S, incl, strict, Tm=None):
    D_ = _each(lambda gi, gj, m: jnp.where(m, jnp.exp(jnp.where(m, gi - gj, 0.0)), 0.0), g_i, g_j, incl)
    C, dk = k[0].shape
    kb = _each(lambda k_, b: k_ * b, k, beta_i)
    kq = _each(lambda kb_, q_, k_: _dot(_rows(kb_, q_), k_, "nt"), kb, q, k)
    A = _each(lambda m_, d, m: jnp.where(m, m_[:C] * d, 0.0), kq, D_, strict)
    P = _each(lambda m_, d, m: jnp.where(m, m_[C:] * d, 0.0), kq, D_, incl)
    if Tm is None:
        Tm = _unit_tri_inverse(A)
    gam = _each(jnp.exp, g_i)
    wu = _each(lambda t, kb_, g, v_, b: _dot(t, _cols(kb_ * g, v_ * b)), Tm, kb, gam, v, beta_i)
    w = [m_[:, :dk] for m_ in wu]
    u = [m_[:, dk:] for m_ in wu]
    qg = _each(lambda q_, g: q_ * g, q, gam)
    ws = _each(lambda w_, qg_, s: _dot(_rows(w_, qg_), s), w, qg, S)
    vn = _each(lambda u_, m_: u_ - m_[:C], u, ws)
    o = _each(lambda m_, p, vn_: m_[C:] + _dot(p, vn_), ws, P, vn)
    e_i = _each(lambda gl_, gi: jnp.exp(gl_ - gi), gl, g_i)
    kd = _each(lambda k_, e: k_ * e, k, e_i)
    Gam = _each(jnp.exp, gl)
    S_new = _each(lambda s, g, kd_, vn_: s * g + _dot(kd_, vn_, "tn"), S, Gam, kd, vn)
    return dict(D=D_, kb=kb, A=A, Tm=Tm, gam=gam, w=w, u=u, vn=vn, P=P, qg=qg, o=o, e=e_i, kd=kd, Gam=Gam, S_new=S_new)


def _delta_gb(gb_ref, d, incl, incl_t, H):
    gb = gb_ref[0]
    g = gb[:, d * H:(d + 1) * H]
    beta = gb[:, (2 + d) * H:(3 + d) * H]
    gc_col = _dot_mask(incl.astype(BF16), g)
    gc_row = _dot_mask(incl_t.astype(BF16), g, "xtn")
    gl = jnp.sum(g, axis=0, keepdims=True)
    return beta, gc_col, gc_row, gl


def _delta_fwd(qkvn, gb, H, hd, ncc):
    B, T, _ = qkvn.shape
    nch = T // CHUNK
    HD = H * hd
    pairs = [(d, h) for d in range(2) for h in range(H)]

    def body(q0, k0, v0, gb0, q1, k1, v1, gb1, o0, o1, sin0, sin1, tm0, tm1, S_ref):
        s = pl.program_id(1)
        q_refs, k_refs, v_refs, gb_refs = (q0, q1), (k0, k1), (v0, v1), (gb0, gb1)
        o_refs, sin_refs, tm_refs = (o0, o1), (sin0, sin1), (tm0, tm1)

        @pl.when(s == 0)
        def _():
            S_ref[...] = jnp.zeros_like(S_ref)

        masks = [_delta_masks(d) for d in range(2)]
        gbs = [_delta_gb(gb_refs[d], d, masks[d][0], masks[d][2], H) for d in range(2)]
        head = lambda ref, h: ref[0, :, h * hd:(h + 1) * hd]
        S = [S_ref[d, h] for d, h in pairs]
        r = _delta_chunk_fwd([head(q_refs[d], h) for d, h in pairs], [head(k_refs[d], h) for d, h in pairs],
                             [head(v_refs[d], h) for d, h in pairs],
                             [gbs[d][1][:, h:h + 1] for d, h in pairs], [gbs[d][2][h:h + 1, :] for d, h in pairs],
                             [gbs[d][0][:, h:h + 1] for d, h in pairs], [gbs[d][3][:, h:h + 1] for d, h in pairs],
                             S, [masks[d][0] for d, h in pairs], [masks[d][1] for d, h in pairs])
        for i, (d, h) in enumerate(pairs):
            sin_refs[d][0, 0, h] = S[i]
            tm_refs[d][0, 0, h] = r["Tm"][i]
            S_ref[d, h] = r["S_new"][i]
            o_refs[d][0, :, h * hd:(h + 1) * hd] = r["o"][i]

    def dir_specs(d):
        cidx = lambda b, s: _chunk_of(d, s, ncc, nch)
        ins = [pl.BlockSpec((1, CHUNK, HD), lambda b, s, p=p: (b, cidx(b, s), p)) for p in range(3)]
        ins.append(pl.BlockSpec((1, CHUNK, LANES), lambda b, s: (b, cidx(b, s), 0)))
        return (ins, pl.BlockSpec((1, CHUNK, HD), lambda b, s: (b, cidx(b, s), 0)),
                pl.BlockSpec((1, 1, H, hd, hd), lambda b, s: (b, cidx(b, s), 0, 0, 0)),
                pl.BlockSpec((1, 1, H, CHUNK, CHUNK), lambda b, s: (b, cidx(b, s), 0, 0, 0)))

    (in0, o_s0, sin_s0, tm_s0), (in1, o_s1, sin_s1, tm_s1) = dir_specs(0), dir_specs(1)
    o_shape = jax.ShapeDtypeStruct((B, T, HD), F32)
    sin_shape = jax.ShapeDtypeStruct((B, nch, H, hd, hd), F32)
    tm_shape = jax.ShapeDtypeStruct((B, nch, H, CHUNK, CHUNK), F32)
    return pl.pallas_call(
        body, name="delta_fwd", grid=(B, nch),
        in_specs=in0 + in1, out_specs=(o_s0, o_s1, sin_s0, sin_s1, tm_s0, tm_s1),
        out_shape=(o_shape, o_shape, sin_shape, sin_shape, tm_shape, tm_shape),
        scratch_shapes=[pltpu.VMEM((2, H, hd, hd), F32)],
        compiler_params=_cparams(("parallel", "arbitrary"), VMEM_LIMIT),
    )(qkvn, qkvn, qkvn, gb, qkvn, qkvn, qkvn, gb)


def _delta_bwd(qkvn, gb, sin, tms, do, H, hd, ncc):
    B, T, _ = qkvn.shape
    nch = T // CHUNK
    HD = H * hd
    pairs = [(d, h) for d in range(2) for h in range(H)]

    def body(q0, k0, v0, gb0, sin0, tm0, do0, q1, k1, v1, gb1, sin1, tm1, do1, dqkv0, dqkv1, dgb0, dgb1, dS_ref):
        s = pl.program_id(1)
        tm_refs = (tm0, tm1)
        q_refs, k_refs, v_refs, gb_refs = (q0, q1), (k0, k1), (v0, v1), (gb0, gb1)
        sin_refs, do_refs, dqkv_refs, dgb_refs = (sin0, sin1), (do0, do1), (dqkv0, dqkv1), (dgb0, dgb1)

        @pl.when(s == 0)
        def _():
            dS_ref[...] = jnp.zeros_like(dS_ref)

        masks = [_delta_masks(d) for d in range(2)]
        gbs = [_delta_gb(gb_refs[d], d, masks[d][0], masks[d][2], H) for d in range(2)]
        incl = [masks[d][0] for d, h in pairs]
        strict = [masks[d][1] for d, h in pairs]
        lane = lax.broadcasted_iota(jnp.int32, (1, LANES), 1)
        ones = jnp.ones((3 * CHUNK, LANES), BF16)
        head = lambda ref, h: ref[0, :, h * hd:(h + 1) * hd]
        q = [head(q_refs[d], h) for d, h in pairs]
        k = [head(k_refs[d], h) for d, h in pairs]
        v = [head(v_refs[d], h) for d, h in pairs]
        do_ = [head(do_refs[d], h) for d, h in pairs]
        beta_i = [gbs[d][0][:, h:h + 1] for d, h in pairs]
        S = [sin_refs[d][0, 0, h] for d, h in pairs]
        dSn = [dS_ref[d, h] for d, h in pairs]
        f = _delta_chunk_fwd(q, k, v, [gbs[d][1][:, h:h + 1] for d, h in pairs], [gbs[d][2][h:h + 1, :] for d, h in pairs],
                             beta_i, [gbs[d][3][:, h:h + 1] for d, h in pairs], S, incl, strict,
                             Tm=[tm_refs[d][0, 0, h] for d, h in pairs])
        rsum = lambda x: jnp.sum(x, axis=1, keepdims=True)
        dvn = _each(lambda p, d_, kd, ds: _dot(p, d_, "tn") + _dot(kd, ds), f["P"], do_, f["kd"], dSn)
        dP = _each(lambda d_, vn, m: jnp.where(m, _dot(d_, vn, "nt"), 0.0), do_, f["vn"], incl)
        dqg = _each(lambda d_, s: _dot(d_, s, "nt"), do_, S)
        dS_in = _each(lambda qg, d_, g, ds, w, dv_: _dot(qg, d_, "tn") + g * ds - _dot(w, dv_, "tn"),
                      f["qg"], do_, f["Gam"], dSn, f["w"], dvn)
        dGam = _each(lambda s, ds: jnp.sum(rsum(s * ds), axis=0, keepdims=True), S, dSn)
        dkd = _each(lambda vn, ds: _dot(vn, ds, "nt"), f["vn"], dSn)
        de = _each(lambda dkd_, k_, e: rsum(dkd_ * k_) * e, dkd, k, f["e"])
        dw = _each(lambda dv_, s: -_dot(dv_, s, "nt"), dvn, S)
        dXw = _each(lambda t, dw_: _dot(t, dw_, "tn"), f["Tm"], dw)
        dXu = _each(lambda t, dv_: _dot(t, dv_, "tn"), f["Tm"], dvn)
        dA = _each(lambda xw, w, xu, u, m: -jnp.where(m, _dot(xw, w, "nt") + _dot(xu, u, "nt"), 0.0),
                   dXw, f["w"], dXu, f["u"], strict)
        dM = _each(lambda a, d_: a * d_, dA, f["D"])
        dN = _each(lambda p, d_: p * d_, dP, f["D"])
        dkb = _each(lambda m, k_, xw, g: _dot(m, k_) + xw * g, dM, k, dXw, f["gam"])
        dq = _each(lambda dqg_, g, n, k_: dqg_ * g + _dot(n, k_), dqg, f["gam"], dN, k)
        dk = _each(lambda dkd_, e, m, kb, n, q_, dkb_, b: dkd_ * e + _dot(m, kb, "tn") + _dot(n, q_, "tn") + dkb_ * b,
                   dkd, f["e"], dM, f["kb"], dN, q, dkb, beta_i)
        dv = _each(lambda xu, b: xu * b, dXu, beta_i)
        dbeta = _each(lambda dkb_, k_, xu, v_: rsum(dkb_ * k_) + rsum(xu * v_), dkb, k, dXu, v)
        E = _each(lambda a, A_, p, P_: a * A_ + p * P_, dA, f["A"], dP, f["P"])

        def colsum(e):
            return _dot(_rows(*_split3(e)), ones, "tn")[:, 0:1]

        dg = _each(lambda e, dqg_, qg, xw, kb, g, de_: rsum(e) - colsum(e) + rsum(dqg_ * qg) + rsum(xw * kb) * g - de_,
                   E, dqg, f["qg"], dXw, f["kb"], f["gam"], de)
        dgl_h = _each(lambda de_, dg_, g: jnp.sum(de_, axis=0, keepdims=True) + dg_ * g, de, dGam, f["Gam"])
        for d in range(2):
            dgc = jnp.zeros((CHUNK, LANES), F32)
            dgl = jnp.zeros((1, LANES), F32)
            for h in range(H):
                i = d * H + h
                g_lane, b_lane = d * H + h, (2 + d) * H + h
                dgc = dgc + dg[i] * (lane == g_lane).astype(F32) + dbeta[i] * (lane == b_lane).astype(F32)
                dgl = dgl + dgl_h[i] * (lane == g_lane).astype(F32)
                dS_ref[d, h] = dS_in[i]
                dqkv_refs[d][0, :, h * hd:(h + 1) * hd] = dq[i]
                dqkv_refs[d][0, :, HD + h * hd:HD + (h + 1) * hd] = dk[i]
                dqkv_refs[d][0, :, 2 * HD + h * hd:2 * HD + (h + 1) * hd] = dv[i]
            draw = _dot_mask(masks[d][0].astype(BF16), dgc, "tn") + dgl
            dgb_refs[d][0] = jnp.where(lane < 2 * H, draw, dgc)

    def dir_specs(d):
        cidx = lambda b, s: _chunk_of(d, nch - 1 - s, ncc, nch)
        ins = [pl.BlockSpec((1, CHUNK, HD), lambda b, s, p=p: (b, cidx(b, s), p)) for p in range(3)]
        ins += [pl.BlockSpec((1, CHUNK, LANES), lambda b, s: (b, cidx(b, s), 0)),
                pl.BlockSpec((1, 1, H, hd, hd), lambda b, s: (b, cidx(b, s), 0, 0, 0)),
                pl.BlockSpec((1, 1, H, CHUNK, CHUNK), lambda b, s: (b, cidx(b, s), 0, 0, 0)),
                pl.BlockSpec((1, CHUNK, HD), lambda b, s: (b, cidx(b, s), 0))]
        return (ins, pl.BlockSpec((1, CHUNK, 3 * HD), lambda b, s: (b, cidx(b, s), 0)),
                pl.BlockSpec((1, CHUNK, LANES), lambda b, s: (b, cidx(b, s), 0)))

    (in0, dq_s0, dg_s0), (in1, dq_s1, dg_s1) = dir_specs(0), dir_specs(1)
    dq_shape = jax.ShapeDtypeStruct((B, T, 3 * HD), F32)
    dg_shape = jax.ShapeDtypeStruct((B, T, LANES), F32)
    return pl.pallas_call(
        body, name="delta_bwd", grid=(B, nch),
        in_specs=in0 + in1, out_specs=(dq_s0, dq_s1, dg_s0, dg_s1),
        out_shape=(dq_shape, dq_shape, dg_shape, dg_shape),
        scratch_shapes=[pltpu.VMEM((2, H, hd, hd), F32)],
        compiler_params=_cparams(("parallel", "arbitrary"), VMEM_LIMIT),
    )(qkvn, qkvn, qkvn, gb, sin[0], tms[0], do, qkvn, qkvn, qkvn, gb, sin[1], tms[1], do)


def _dn_out(o2, pz, zcb0, onorm, H, hd, nct, tm):
    B, T, HD = o2[0].shape
    N = T - nct * tm

    def body(o0_ref, o1_ref, z_ref, g_ref, y_ref):
        for h in range(H):
            sl = slice(h * hd, (h + 1) * hd)
            o = o0_ref[0, :, sl] + o1_ref[0, :, sl]
            r = lax.rsqrt(jnp.mean(o * o, axis=-1, keepdims=True) + EPS)
            y_ref[0, :, sl] = (o * r * g_ref[...] * _silu(z_ref[0, :, sl].astype(F32))).astype(BF16)

    return pl.pallas_call(
        body, name="dn_out", grid=(B, N // tm),
        in_specs=[pl.BlockSpec((1, tm, HD), lambda b, t: (b, t + nct, 0)),
                  pl.BlockSpec((1, tm, HD), lambda b, t: (b, t + nct, 0)),
                  pl.BlockSpec((1, tm, HD), lambda b, t: (b, t + nct, zcb0)),
                  pl.BlockSpec((1, hd), lambda b, t: (0, 0))],
        out_specs=pl.BlockSpec((1, tm, HD), lambda b, t: (b, t, 0)),
        out_shape=jax.ShapeDtypeStruct((B, N, HD), BF16),
        compiler_params=_cparams(("parallel",) * 2, VMEM_LIMIT))(o2[0], o2[1], pz, onorm)


def _dn_out_bwd(o2, pz, zcb0, onorm, dy, H, hd, nct, tm):
    B, T, HD = o2[0].shape
    nt = T // tm

    def body(o0_ref, o1_ref, z_ref, g_ref, dy_ref, do_ref, dz_ref, acc_ref):
        t = pl.program_id(1)

        @pl.when(t == 0)
        def _():
            acc_ref[...] = jnp.zeros_like(acc_ref)

        @pl.when(t < nct)
        def _():
            do_ref[0] = jnp.zeros_like(do_ref[0])
            dz_ref[0] = jnp.zeros_like(dz_ref[0])

        @pl.when(t >= nct)
        def _():
            g = g_ref[...]
            for h in range(H):
                sl = slice(h * hd, (h + 1) * hd)
                o = o0_ref[0, :, sl] + o1_ref[0, :, sl]
                z = z_ref[0, :, sl].astype(F32)
                dy_ = dy_ref[0, :, sl]
                r = lax.rsqrt(jnp.mean(o * o, axis=-1, keepdims=True) + EPS)
                oh = o * r
                dz_ref[0, :, sl] = (dy_ * oh * g * _dsilu(z)).astype(BF16)
                dn = dy_ * _silu(z)
                acc_ref[0, 0:1, :] += jnp.sum(dn * oh, axis=0, keepdims=True)
                doh = dn * g
                do_ref[0, :, sl] = r * (doh - oh * jnp.mean(doh * oh, axis=-1, keepdims=True))

    lat = lambda b, t: (b, jnp.maximum(t - nct, 0), 0)
    row = pl.BlockSpec((1, tm, HD), lambda b, t: (b, t, 0))
    return pl.pallas_call(
        body, name="dn_out_bwd", grid=(B, nt),
        in_specs=[row, row,
                  pl.BlockSpec((1, tm, HD), lambda b, t: (b, t, zcb0)),
                  pl.BlockSpec((1, hd), lambda b, t: (0, 0)),
                  pl.BlockSpec((1, tm, HD), lat)],
        out_specs=(row, row, pl.BlockSpec((1, SUBLANES, hd), lambda b, t: (b, 0, 0))),
        out_shape=(jax.ShapeDtypeStruct((B, T, HD), F32), jax.ShapeDtypeStruct((B, T, HD), BF16),
                   jax.ShapeDtypeStruct((B, SUBLANES, hd), F32)),
        compiler_params=_cparams(("parallel", "arbitrary"), VMEM_LIMIT),
    )(o2[0], o2[1], pz, onorm, dy)


def _lru_gate_math(x, wr, wi, br, bi, lam):
    r = _sigmoid(_dot(x, wr) + br)
    i = _sigmoid(_dot(x, wi) + bi)
    sp = _softplus(-lam)
    la = -LRU_C * r * sp
    a = jnp.exp(la)
    s = jnp.sqrt(-jnp.tanh(la) * (a * a + 1.0))
    return r, i, sp, a, s


def _lru_gates(xc, wbd, bias4, lam, tm, bw):
    B, T, W = xc.shape

    def body(x_ref, w_ref, b_ref, l_ref, a_ref, i_ref):
        x = x_ref[0]
        for d in range(2):
            _, i, _, a, s = _lru_gate_math(x, w_ref[2 * d, 0], w_ref[2 * d + 1, 0],
                                           b_ref[2 * d:2 * d + 1, :], b_ref[2 * d + 1:2 * d + 2, :], l_ref[d:d + 1, :])
            a_ref[d, 0] = a
            i_ref[d, 0] = s * (i * x)

    out = pl.BlockSpec((2, 1, tm, bw), lambda b, t, j: (0, b, t, j))
    return pl.pallas_call(
        body, name="lru_gates", grid=(B, T // tm, W // bw),
        in_specs=[pl.BlockSpec((1, tm, bw), lambda b, t, j: (b, t, j)),
                  pl.BlockSpec((4, 1, bw, bw), lambda b, t, j: (0, j, 0, 0)),
                  pl.BlockSpec((4, bw), lambda b, t, j: (0, j)),
                  pl.BlockSpec((2, bw), lambda b, t, j: (0, j))],
        out_specs=(out, out),
        out_shape=(jax.ShapeDtypeStruct((2, B, T, W), F32), jax.ShapeDtypeStruct((2, B, T, W), F32)),
        compiler_params=_cparams(("parallel",) * 3, VMEM_LIMIT))(xc, wbd, bias4, lam)


def _lru_gates_bwd(xc, wbd, bias4, lam, dinp, da, tm, bw):
    B, T, W = xc.shape
    nt = T // tm

    def body(x_ref, w_ref, b_ref, l_ref, di0_ref, di1_ref, da0_ref, da1_ref, dx_ref, dw_ref, acc_ref):
        di_refs, da_refs = (di0_ref, di1_ref), (da0_ref, da1_ref)
        b_ = pl.program_id(1)
        t = pl.program_id(2)

        @pl.when(jnp.logical_and(b_ == 0, t == 0))
        def _():
            dw_ref[...] = jnp.zeros_like(dw_ref)
            acc_ref[...] = jnp.zeros_like(acc_ref)

        x = x_ref[0]
        dx = jnp.zeros_like(x)
        for d in range(2):
            wr, wi = w_ref[2 * d, 0], w_ref[2 * d + 1, 0]
            lam_ = l_ref[d:d + 1, :]
            r, i, sp, a, s = _lru_gate_math(x, wr, wi, b_ref[2 * d:2 * d + 1, :], b_ref[2 * d + 1:2 * d + 2, :], lam_)
            dinp_ = di_refs[d][0]
            dix = dinp_ * s
            ds = dinp_ * i * x
            dla = da_refs[d][0] * a - ds * (a * a) / s
            dpr = dla * (-LRU_C * sp) * r * (1.0 - r)
            dpi = dix * x * i * (1.0 - i)
            dx = dx + dix * i + _dot(dpr, wr, "nt") + _dot(dpi, wi, "nt")
            dw_ref[2 * d, 0] += _dot(x, dpr, "tn")
            dw_ref[2 * d + 1, 0] += _dot(x, dpi, "tn")
            acc_ref[2 * d:2 * d + 1, :] += jnp.sum(dpr, axis=0, keepdims=True)
            acc_ref[2 * d + 1:2 * d + 2, :] += jnp.sum(dpi, axis=0, keepdims=True)
            acc_ref[4 + d:5 + d, :] += jnp.sum(dla * (-LRU_C * r), axis=0, keepdims=True) * (-_sigmoid(-lam_))
        dx_ref[0] = dx

    tile = pl.BlockSpec((1, tm, bw), lambda j, b, t: (b, t, j))
    return pl.pallas_call(
        body, name="lru_gates_bwd", grid=(W // bw, B, nt),
        in_specs=[pl.BlockSpec((1, tm, bw), lambda j, b, t: (b, t, j)),
                  pl.BlockSpec((4, 1, bw, bw), lambda j, b, t: (0, j, 0, 0)),
                  pl.BlockSpec((4, bw), lambda j, b, t: (0, j)),
                  pl.BlockSpec((2, bw), lambda j, b, t: (0, j)), tile, tile, tile, tile],
        out_specs=(pl.BlockSpec((1, tm, bw), lambda j, b, t: (b, t, j)),
                   pl.BlockSpec((4, 1, bw, bw), lambda j, b, t: (0, j, 0, 0)),
                   pl.BlockSpec((SUBLANES, bw), lambda j, b, t: (0, j))),
        out_shape=(jax.ShapeDtypeStruct((B, T, W), F32), jax.ShapeDtypeStruct(wbd.shape, F32),
                   jax.ShapeDtypeStruct((SUBLANES, W), F32)),
        compiler_params=_cparams(("parallel", "arbitrary", "arbitrary"), VMEM_LIMIT),
    )(xc, wbd, bias4, lam, dinp[0], dinp[1], da[0], da[1])


def _tile_scan(a, x, rev, tm):
    row = lax.broadcasted_iota(jnp.int32, a.shape, 0)
    s = 1
    while s < tm:
        if rev:
            a_sh, x_sh, ok = pltpu.roll(a, tm - s, 0), pltpu.roll(x, tm - s, 0), row < tm - s
        else:
            a_sh, x_sh, ok = pltpu.roll(a, s, 0), pltpu.roll(x, s, 0), row >= s
        x = jnp.where(ok, x + a * x_sh, x)
        a = jnp.where(ok, a * a_sh, a)
        s *= 2
    return a, x


def _scan_tile_of(s, rev, nct, nt):
    if not rev:
        return s
    return jnp.where(s < nct, nct - 1 - s, nt - 1 - (s - nct))


def _lru_scan(a2, x2, d, nct, tm, tc):
    _, B, T, W = a2.shape
    nt = T // tm
    rev = d == 1
    last = 0 if rev else tm - 1

    def body(a_ref, x_ref, h_ref, carry):
        s = pl.program_id(2)

        @pl.when(s == 0)
        def _():
            carry[...] = jnp.zeros_like(carry)

        A, X = _tile_scan(a_ref[0, 0], x_ref[0, 0], rev, tm)
        h = X + A * carry[0:1, :]
        h_ref[0] = h
        carry[...] = jnp.broadcast_to(h[last:last + 1, :], carry.shape)

    tidx = lambda s: _scan_tile_of(s, rev, nct, nt)
    spec = pl.BlockSpec((1, 1, tm, tc), lambda b, j, s: (d, b, tidx(s), j))
    return pl.pallas_call(
        body, name=f"lru_scan{d}", grid=(B, W // tc, nt),
        in_specs=[spec, spec], out_specs=pl.BlockSpec((1, tm, tc), lambda b, j, s: (b, tidx(s), j)),
        out_shape=jax.ShapeDtypeStruct((B, T, W), F32),
        scratch_shapes=[pltpu.VMEM((SUBLANES, tc), F32)],
        compiler_params=_cparams(("parallel", "parallel", "arbitrary"), VMEM_LIMIT),
    )(a2, x2)


def _lru_scan_bwd(a2, h, dh, d, nct, tm, tc):
    _, B, T, W = a2.shape
    nt = T // tm
    rev = d == 1
    first = tm - 1 if rev else 0
    r8, n8 = tm // SUBLANES, T // SUBLANES

    def body(a_ref, h_ref, hp_ref, dh_ref, lam_ref, da_ref, ca, cl):
        s = pl.program_id(2)

        @pl.when(s == 0)
        def _():
            ca[...] = jnp.zeros_like(ca)
            cl[...] = jnp.zeros_like(cl)

        a = a_ref[0, 0]
        row = lax.broadcasted_iota(jnp.int32, a.shape, 0)
        if rev:
            c = jnp.where(row == 0, ca[0:1, :], pltpu.roll(a, 1, 0))
        else:
            c = jnp.where(row == tm - 1, ca[0:1, :], pltpu.roll(a, tm - 1, 0))
        C, L = _tile_scan(c, dh_ref[0], not rev, tm)
        lam = L + C * cl[0:1, :]
        lam_ref[0] = lam
        hp = jnp.where(s == nt - 1, 0.0, hp_ref[0])
        if rev:
            hprev = jnp.where(row == tm - 1, hp[0:1, :], pltpu.roll(h_ref[0], tm - 1, 0))
        else:
            hprev = jnp.where(row == 0, hp[SUBLANES - 1:SUBLANES, :], pltpu.roll(h_ref[0], 1, 0))
        da_ref[0] = lam * hprev
        ca[...] = jnp.broadcast_to(a[first:first + 1, :], ca.shape)
        cl[...] = jnp.broadcast_to(lam[first:first + 1, :], cl.shape)

    tidx = lambda s: _scan_tile_of(nt - 1 - s, rev, nct, nt)

    def hp_idx(b, j, s):
        tt = tidx(s)
        if rev:
            blk = jnp.where(tt == nt - 1, 0, jnp.minimum((tt + 1) * r8, n8 - 1))
        else:
            blk = jnp.maximum(tt * r8 - 1, 0)
        return (b, blk, j)

    tile = pl.BlockSpec((1, tm, tc), lambda b, j, s: (b, tidx(s), j))
    return pl.pallas_call(
        body, name=f"lru_scan_bwd{d}", grid=(B, W // tc, nt),
        in_specs=[pl.BlockSpec((1, 1, tm, tc), lambda b, j, s: (d, b, tidx(s), j)), tile,
                  pl.BlockSpec((1, SUBLANES, tc), hp_idx), tile],
        out_specs=(tile, tile),
        out_shape=(jax.ShapeDtypeStruct((B, T, W), F32), jax.ShapeDtypeStruct((B, T, W), F32)),
        scratch_shapes=[pltpu.VMEM((SUBLANES, tc), F32), pltpu.VMEM((SUBLANES, tc), F32)],
        compiler_params=_cparams(("parallel", "parallel", "arbitrary"), VMEM_LIMIT),
    )(a2, h, h, dh)


def _lru_out(h0, h1, pyl, ycb0, nct, tm, tc):
    B, T, W = h0.shape
    N = T - nct * tm

    def body(h0_ref, h1_ref, y_ref, o_ref):
        o_ref[0] = ((h0_ref[0] + h1_ref[0]) * _gelu(y_ref[0].astype(F32))).astype(BF16)

    hs = pl.BlockSpec((1, tm, tc), lambda b, t, j: (b, t + nct, j))
    return pl.pallas_call(
        body, name="lru_out", grid=(B, N // tm, W // tc),
        in_specs=[hs, hs, pl.BlockSpec((1, tm, tc), lambda b, t, j: (b, t + nct, ycb0 + j))],
        out_specs=pl.BlockSpec((1, tm, tc), lambda b, t, j: (b, t, j)),
        out_shape=jax.ShapeDtypeStruct((B, N, W), BF16),
        compiler_params=_cparams(("parallel",) * 3, VMEM_LIMIT))(h0, h1, pyl)


def _lru_out_bwd(h0, h1, pyl, ycb0, dy, nct, tm, tc):
    B, T, W = h0.shape

    def body(h0_ref, h1_ref, y_ref, dy_ref, dh_ref, dyl_ref):
        t = pl.program_id(1)

        @pl.when(t < nct)
        def _():
            dh_ref[0] = jnp.zeros_like(dh_ref[0])
            dyl_ref[0] = jnp.zeros_like(dyl_ref[0])

        @pl.when(t >= nct)
        def _():
            y = y_ref[0].astype(F32)
            dy_ = dy_ref[0]
            dh_ref[0] = dy_ * _gelu(y)
            dyl_ref[0] = (dy_ * (h0_ref[0] + h1_ref[0]) * _dgelu(y)).astype(BF16)

    hs = pl.BlockSpec((1, tm, tc), lambda b, t, j: (b, t, j))
    return pl.pallas_call(
        body, name="lru_out_bwd", grid=(B, T // tm, W // tc),
        in_specs=[hs, hs, pl.BlockSpec((1, tm, tc), lambda b, t, j: (b, t, ycb0 + j)),
                  pl.BlockSpec((1, tm, tc), lambda b, t, j: (b, jnp.maximum(t - nct, 0), j))],
        out_specs=(hs, hs),
        out_shape=(jax.ShapeDtypeStruct((B, T, W), F32), jax.ShapeDtypeStruct((B, T, W), BF16)),
        compiler_params=_cparams(("parallel",) * 3, VMEM_LIMIT))(h0, h1, pyl, dy)


def _merge(bd, bl, pmg, bm, nct, tm):
    B, N, D = bd.shape

    def body(bd_ref, bl_ref, m0_ref, m1_ref, b_ref, o_ref):
        g0 = _sigmoid(m0_ref[0].astype(F32) + b_ref[:, 0:D])
        g1 = _sigmoid(m1_ref[0].astype(F32) + b_ref[:, D:2 * D])
        o_ref[0] = (g0 * bd_ref[0].astype(F32) + g1 * bl_ref[0].astype(F32)).astype(BF16)

    row = pl.BlockSpec((1, tm, D), lambda b, t: (b, t, 0))
    return pl.pallas_call(
        body, name="merge", grid=(B, N // tm),
        in_specs=[row, row, pl.BlockSpec((1, tm, D), lambda b, t: (b, t + nct, 0)),
                  pl.BlockSpec((1, tm, D), lambda b, t: (b, t + nct, 1)),
                  pl.BlockSpec((1, 2 * D), lambda b, t: (0, 0))],
        out_specs=row, out_shape=jax.ShapeDtypeStruct((B, N, D), BF16),
        compiler_params=_cparams(("parallel", "parallel"), VMEM_LIMIT))(bd, bl, pmg, pmg, bm)


def _merge_bwd(dmi, bd, bl, pmg, bm, nct, tm):
    B, N, D = bd.shape
    T = pmg.shape[1]

    def body(dm_ref, bd_ref, bl_ref, m0_ref, m1_ref, b_ref, dbd_ref, dbl_ref, dmg_ref, acc_ref):
        t = pl.program_id(1)

        @pl.when(t == 0)
        def _():
            acc_ref[...] = jnp.zeros_like(acc_ref)

        @pl.when(t < nct)
        def _():
            dmg_ref[0] = jnp.zeros_like(dmg_ref[0])

        @pl.when(t >= nct)
        def _():
            dm = dm_ref[0]
            g0 = _sigmoid(m0_ref[0].astype(F32) + b_ref[:, 0:D])
            g1 = _sigmoid(m1_ref[0].astype(F32) + b_ref[:, D:2 * D])
            dbd_ref[0] = (dm * g0).astype(BF16)
            dbl_ref[0] = (dm * g1).astype(BF16)
            d0 = dm * bd_ref[0].astype(F32) * g0 * (1.0 - g0)
            d1 = dm * bl_ref[0].astype(F32) * g1 * (1.0 - g1)
            dmg_ref[0, :, 0:D] = d0.astype(BF16)
            dmg_ref[0, :, D:2 * D] = d1.astype(BF16)
            acc_ref[0, 0:1, 0:D] += jnp.sum(d0, axis=0, keepdims=True)
            acc_ref[0, 0:1, D:2 * D] += jnp.sum(d1, axis=0, keepdims=True)

    lat = pl.BlockSpec((1, tm, D), lambda b, t: (b, jnp.maximum(t - nct, 0), 0))
    return pl.pallas_call(
        body, name="merge_bwd", grid=(B, T // tm),
        in_specs=[lat, lat, lat, pl.BlockSpec((1, tm, D), lambda b, t: (b, t, 0)),
                  pl.BlockSpec((1, tm, D), lambda b, t: (b, t, 1)),
                  pl.BlockSpec((1, 2 * D), lambda b, t: (0, 0))],
        out_specs=(lat, lat, pl.BlockSpec((1, tm, 2 * D), lambda b, t: (b, t, 0)),
                   pl.BlockSpec((1, SUBLANES, 2 * D), lambda b, t: (b, 0, 0))),
        out_shape=(jax.ShapeDtypeStruct((B, N, D), BF16), jax.ShapeDtypeStruct((B, N, D), BF16),
                   jax.ShapeDtypeStruct((B, T, 2 * D), BF16), jax.ShapeDtypeStruct((B, SUBLANES, 2 * D), F32)),
        compiler_params=_cparams(("parallel", "arbitrary"), VMEM_LIMIT))(dmi, bd, bl, pmg, pmg, bm)


def _grid_taps():
    return [((di + 1) * 3 + (dj + 1), di, dj) for di in (-1, 0, 1) for dj in (-1, 0, 1)]


def _grid_views(x, n):
    pos = lax.broadcasted_iota(jnp.int32, x.shape, 0)
    gc = jnp.bitwise_and(pos, GRID_W - 1)
    cols = {0: x,
            -1: jnp.where(gc >= 1, pltpu.roll(x, 1, 0), 0.0),
            1: jnp.where(gc <= GRID_W - 2, pltpu.roll(x, n - 1, 0), 0.0)}
    views = {}
    edge = jnp.zeros((GRID_W, x.shape[1]), x.dtype)
    for dj, xc in cols.items():
        views[(0, dj)] = xc
        views[(-1, dj)] = jnp.concatenate([edge, xc[:n - GRID_W]], axis=0)
        views[(1, dj)] = jnp.concatenate([xc[GRID_W:], edge], axis=0)
    return views


def _ffn_act(F, w9, b, tc):
    B, N, C2 = F.shape
    Cf = C2 // 2
    ncb = Cf // tc

    def body(g_ref, v_ref, w_ref, b_ref, o_ref, gel_ref, dgel_ref):
        xv = _grid_views(g_ref[0], N)
        conv = b_ref[...]
        for k, di, dj in _grid_taps():
            conv = conv + xv[(di, dj)] * w_ref[k:k + 1, :]
        th = jnp.tanh(GELU_C * (conv + 0.044715 * conv * conv * conv))
        gel = 0.5 * conv * (1.0 + th)
        o_ref[0] = (gel * v_ref[0]).astype(BF16)
        gel_ref[0] = gel.astype(BF16)
        dgel_ref[0] = (0.5 * (1.0 + th)
                       + 0.5 * conv * (1.0 - th * th) * GELU_C * (1.0 + 3.0 * 0.044715 * conv * conv)).astype(BF16)

    tile = pl.BlockSpec((1, N, tc), lambda b, j: (b, 0, j))
    out = jax.ShapeDtypeStruct((B, N, Cf), BF16)
    return pl.pallas_call(
        body, name="ffn_act", grid=(B, ncb),
        in_specs=[tile, pl.BlockSpec((1, N, tc), lambda b, j: (b, 0, ncb + j)),
                  pl.BlockSpec((9, tc), lambda b, j: (0, j)),
                  pl.BlockSpec((1, tc), lambda b, j: (0, j))],
        out_specs=(tile, tile, tile), out_shape=(out, out, out),
        compiler_params=_cparams(("parallel", "parallel"), VMEM_LIMIT))(F, F, w9, b)


def _ffn_act_bwd(F, w9, gel, dgel, df, tc):
    B, N, C2 = F.shape
    Cf = C2 // 2
    ncb = Cf // tc

    def body(g_ref, v_ref, w_ref, gel_ref, dgel_ref, df_ref, dF_ref, acc_ref):
        p = pl.program_id(2)

        @pl.when(p == 0)
        def _():
            dF_ref[0] = (df_ref[0] * gel_ref[0].astype(F32)).astype(BF16)

        @pl.when(p == 1)
        def _():
            xv = _grid_views(g_ref[0], N)
            df_ = df_ref[0]
            dc = df_ * v_ref[0] * dgel_ref[0].astype(F32)
            dcv = _grid_views(dc, N)
            dx = None
            for k, di, dj in _grid_taps():
                term = dcv[(-di, -dj)] * w_ref[k:k + 1, :]
                dx = term if dx is None else dx + term
                acc_ref[0, k:k + 1, :] = jnp.sum(dc * xv[(di, dj)], axis=0, keepdims=True)
            acc_ref[0, 9:10, :] = jnp.sum(dc, axis=0, keepdims=True)
            acc_ref[0, 10:16, :] = jnp.zeros((6, tc), F32)
            dF_ref[0] = dx.astype(BF16)

    tile = pl.BlockSpec((1, N, tc), lambda b, j, p: (b, 0, j))
    return pl.pallas_call(
        body, name="ffn_act_bwd", grid=(B, ncb, 2),
        in_specs=[tile, pl.BlockSpec((1, N, tc), lambda b, j, p: (b, 0, ncb + j)),
                  pl.BlockSpec((9, tc), lambda b, j, p: (0, j)), tile, tile, tile],
        out_specs=(pl.BlockSpec((1, N, tc), lambda b, j, p: (b, 0, j + (1 - p) * ncb)),
                   pl.BlockSpec((1, 16, tc), lambda b, j, p: (b, 0, j))),
        out_shape=(jax.ShapeDtypeStruct((B, N, C2), BF16), jax.ShapeDtypeStruct((B, 16, Cf), F32)),
        compiler_params=_cparams(("parallel", "parallel", "arbitrary"), VMEM_LIMIT))(F, F, w9, gel, dgel, df)


ADAM_ROWS = 512


def _adamw(w, m, v, gparts, name):
    rows, cols = w.shape[-2:]
    P = gparts.shape[0]
    tr = _row_tile(rows, (P + 14) * 4 * (-(-cols // LANES) * LANES))
    c1 = 1.0 - ADAM_B1 ** ADAM_STEP
    c2 = 1.0 - ADAM_B2 ** ADAM_STEP

    def body(w_ref, m_ref, v_ref, g_ref, go_ref, d_ref, mo_ref, vo_ref):
        g = g_ref[0].astype(F32)
        for p in range(1, P):
            g = g + g_ref[p].astype(F32)
        m_ = ADAM_B1 * m_ref[...] + (1.0 - ADAM_B1) * g
        v_ = ADAM_B2 * v_ref[...] + (1.0 - ADAM_B2) * (g * g)
        go_ref[...] = g
        mo_ref[...] = m_
        vo_ref[...] = v_
        d_ref[...] = -ADAM_LR * ((m_ / c1) / (jnp.sqrt(v_ / c2) + ADAM_EPS) + ADAM_WD * w_ref[...])

    if w.ndim == 3:
        row = pl.BlockSpec((None, tr, cols), lambda i: (0, i, 0))
    else:
        row = pl.BlockSpec((tr, cols), lambda i: (i, 0))
    out = jax.ShapeDtypeStruct(w.shape, F32)
    return pl.pallas_call(
        body, name=name, grid=(rows // tr,),
        in_specs=[row, row, row, pl.BlockSpec((P, tr, cols), lambda i: (0, i, 0))],
        out_specs=(row, row, row, row), out_shape=(out, out, out, out),
        compiler_params=_cparams(("parallel",), VMEM_LIMIT))(w, m, v, gparts)


def _pack_rows(flat_len):
    rows = -(-flat_len // LANES)
    if rows > ADAM_ROWS:
        rows = -(-rows // ADAM_ROWS) * ADAM_ROWS
    else:
        rows = -(-rows // SUBLANES) * SUBLANES
    return rows


PACK_ALIGN = SUBLANES * LANES


def _pack(arrs, lead=()):
    pads = [(0, 0)] * len(lead)
    parts = []
    for a in arrs:
        f = a.reshape(lead + (-1,)).astype(F32)
        parts.append(jnp.pad(f, pads + [(0, -f.shape[-1] % PACK_ALIGN)]))
    flat = jnp.concatenate(parts, axis=-1)
    n = flat.shape[-1]
    rows = _pack_rows(n)
    flat = jnp.pad(flat, pads + [(0, rows * LANES - n)])
    return flat.reshape(lead + (rows, LANES))


def _unpack(buf, shapes):
    out, r = [], 0
    for s in shapes:
        n = math.prod(s)
        nr = -(-n // PACK_ALIGN) * SUBLANES
        out.append(buf[r:r + nr].reshape(-1)[:n].reshape(s))
        r += nr
    return out


def _col_shards(full):
    C = full.shape[-1]
    x = full.reshape(full.shape[:-1] + (N_DEV, C // N_DEV))
    return jnp.moveaxis(x, -2, 0)


def _from_col_shards(g):
    x = jnp.moveaxis(g, 0, -2)
    return x.reshape(x.shape[:-2] + (x.shape[-2] * x.shape[-1],))


def kernel(x, c, ctx, c_ctx, w_ada, b_ada, g_pre_mix, g_post_mix, g_pre_ffn, g_post_ffn, w_in, b_merge, dn_conv, dn_a_log, dn_dt_bias, dn_onorm, lru_conv, lru_conv_b, lru_w_rg, lru_b_rg, lru_w_ig, lru_b_ig, lru_lambda, w_branch_dn, w_branch_lru, w_out, w_up, ffn_dw, ffn_dw_b, w_down, loss_target, m_c_ctx, m_w_ada, m_b_ada, m_g_pre_mix, m_g_post_mix, m_g_pre_ffn, m_g_post_ffn, m_w_in, m_b_merge, m_dn_conv, m_dn_a_log, m_dn_dt_bias, m_dn_onorm, m_lru_conv, m_lru_conv_b, m_lru_w_rg, m_lru_b_rg, m_lru_w_ig, m_lru_b_ig, m_lru_lambda, m_w_branch_dn, m_w_branch_lru, m_w_out, m_w_up, m_ffn_dw, m_ffn_dw_b, m_w_down, v_c_ctx, v_w_ada, v_b_ada, v_g_pre_mix, v_g_post_mix, v_g_pre_ffn, v_g_post_ffn, v_w_in, v_b_merge, v_dn_conv, v_dn_a_log, v_dn_dt_bias, v_dn_onorm, v_lru_conv, v_lru_conv_b, v_lru_w_rg, v_lru_b_rg, v_lru_w_ig, v_lru_b_ig, v_lru_lambda, v_w_branch_dn, v_w_branch_lru, v_w_out, v_w_up, v_ffn_dw, v_ffn_dw_b, v_w_down):
    params = dict(c_ctx=c_ctx, w_ada=w_ada, b_ada=b_ada, g_pre_mix=g_pre_mix, g_post_mix=g_post_mix, g_pre_ffn=g_pre_ffn, g_post_ffn=g_post_ffn, w_in=w_in, b_merge=b_merge, dn_conv=dn_conv, dn_a_log=dn_a_log, dn_dt_bias=dn_dt_bias, dn_onorm=dn_onorm, lru_conv=lru_conv, lru_conv_b=lru_conv_b, lru_w_rg=lru_w_rg, lru_b_rg=lru_b_rg, lru_w_ig=lru_w_ig, lru_b_ig=lru_b_ig, lru_lambda=lru_lambda, w_branch_dn=w_branch_dn, w_branch_lru=w_branch_lru, w_out=w_out, w_up=w_up, ffn_dw=ffn_dw, ffn_dw_b=ffn_dw_b, w_down=w_down)
    mom1 = dict(c_ctx=m_c_ctx, w_ada=m_w_ada, b_ada=m_b_ada, g_pre_mix=m_g_pre_mix, g_post_mix=m_g_post_mix, g_pre_ffn=m_g_pre_ffn, g_post_ffn=m_g_post_ffn, w_in=m_w_in, b_merge=m_b_merge, dn_conv=m_dn_conv, dn_a_log=m_dn_a_log, dn_dt_bias=m_dn_dt_bias, dn_onorm=m_dn_onorm, lru_conv=m_lru_conv, lru_conv_b=m_lru_conv_b, lru_w_rg=m_lru_w_rg, lru_b_rg=m_lru_b_rg, lru_w_ig=m_lru_w_ig, lru_b_ig=m_lru_b_ig, lru_lambda=m_lru_lambda, w_branch_dn=m_w_branch_dn, w_branch_lru=m_w_branch_lru, w_out=m_w_out, w_up=m_w_up, ffn_dw=m_ffn_dw, ffn_dw_b=m_ffn_dw_b, w_down=m_w_down)
    mom2 = dict(c_ctx=v_c_ctx, w_ada=v_w_ada, b_ada=v_b_ada, g_pre_mix=v_g_pre_mix, g_post_mix=v_g_post_mix, g_pre_ffn=v_g_pre_ffn, g_post_ffn=v_g_post_ffn, w_in=v_w_in, b_merge=v_b_merge, dn_conv=v_dn_conv, dn_a_log=v_dn_a_log, dn_dt_bias=v_dn_dt_bias, dn_onorm=v_dn_onorm, lru_conv=v_lru_conv, lru_conv_b=v_lru_conv_b, lru_w_rg=v_lru_w_rg, lru_b_rg=v_lru_b_rg, lru_w_ig=v_lru_w_ig, lru_b_ig=v_lru_b_ig, lru_lambda=v_lru_lambda, w_branch_dn=v_w_branch_dn, w_branch_lru=v_w_branch_lru, w_out=v_w_out, w_up=v_w_up, ffn_dw=v_ffn_dw, ffn_dw_b=v_ffn_dw_b, w_down=v_w_down)
    names = list(params)

    B, N, D = x.shape
    NC = ctx.shape[1]
    T = NC + N
    H, hd = dn_a_log.shape[2], dn_onorm.shape[1]
    HD = H * hd
    nd = 2 * H
    W = lru_conv_b.shape[1]
    nblk, bdim = lru_w_rg.shape[2], lru_w_rg.shape[3]
    Cf = ffn_dw_b.shape[1]
    sw = w_ada.shape[2]
    tm = min(256, NC)
    nct = NC // tm
    ncc = NC // CHUNK
    nch = T // CHUNK
    R, RL = B * T, B * N
    bw = min(256, W)
    me = _my_index()
    assert NC % tm == 0 and N % tm == 0 and B + 1 <= SUBLANES and bw % bdim == 0 and D % LANES == 0

    cpad = jnp.zeros((SUBLANES, D), F32).at[:B].set(c).at[B].set(c_ctx)
    ccp = _all_gather([cpad], "ag_cond")[0].reshape(N_DEV * SUBLANES, D)
    mods_sh = _ada_fwd(ccp, w_ada[0], lax.dynamic_slice(b_ada, (0, me * sw), (1, sw)))
    lru_vecs = jnp.concatenate([lru_b_rg[0], lru_b_ig[0], lru_lambda[0], jnp.zeros_like(lru_lambda[0])], axis=0)
    mods_g, w_in_g, dn_conv_g, lru_conv_g, lru_vecs_g = _all_gather(
        [mods_sh, w_in[0].astype(BF16).T, dn_conv[0], lru_conv[0], lru_vecs], "ag_first")
    ag_mix = _split_start([w_branch_dn[0].astype(BF16), w_branch_lru[0].astype(BF16), w_out[0].astype(BF16)],
                          "gather", "ag_mix_start")
    ag_ffn = _split_start([w_up[0].astype(BF16) + ag_mix[4][0, 0].astype(BF16), w_down[0].astype(BF16),
                           ffn_dw[0].reshape(9, -1)], "gather", "ag_ffn_start")
    mine = lax.dynamic_slice(mods_g, (0, me * SUBLANES, 0), (N_DEV, SUBLANES, sw))
    mods = jnp.moveaxis(mine, 0, 1).reshape(SUBLANES, 6, D)[:B + 1]
    mods = jnp.pad(mods, ((0, 0), (0, SUBLANES - 6), (0, 0))) + ag_ffn[4][0, 0]
    dn_conv_f = _from_col_shards(dn_conv_g)
    lru_conv_f = _from_col_shards(lru_conv_g)
    lru_vecs_f = _from_col_shards(lru_vecs_g)
    lru_lam_f = lru_vecs_f[4:6]

    csh = w_in_g.shape[1]
    w_in_t = w_in_g.reshape(N_DEV * csh, D)
    o_z, o_a, o_xl = 3 * HD, 4 * HD, 4 * HD + 2 * nd
    o_yl, o_mg = o_xl + W, o_xl + 2 * W
    w_qkv, w_z = w_in_t[:o_z], w_in_t[o_z:o_a]
    w_ab = jnp.pad(w_in_t[o_a:o_xl], ((0, LANES - 2 * nd), (0, 0)))
    w_xl, w_yl, w_mg = w_in_t[o_xl:o_yl], w_in_t[o_yl:o_mg], w_in_t[o_mg:]

    def blockdiag(wg):
        per = bw // bdim
        g5 = wg.reshape(2, W // bw, per, bdim, bdim)
        eye = jnp.eye(per, dtype=wg.dtype)
        return jnp.einsum("dtpij,pq->dtpiqj", g5, eye).reshape(2, W // bw, bw, bw)

    bd_rg, bd_ig = blockdiag(lru_w_rg[0]), blockdiag(lru_w_ig[0])
    wbd = jnp.stack([bd_rg[0], bd_ig[0], bd_rg[1], bd_ig[1]]).astype(BF16)
    bias4 = jnp.stack([lru_vecs_f[0], lru_vecs_f[2], lru_vecs_f[1], lru_vecs_f[3]])
    alog_v = jnp.pad(dn_a_log.reshape(1, nd), ((0, 0), (0, LANES - nd)))
    dtb_v = jnp.pad(dn_dt_bias.reshape(1, nd), ((0, 0), (0, LANES - nd)))

    h0 = jnp.concatenate([ctx, x], axis=1)
    u1 = _norm_mod_fwd(h0, g_pre_mix, mods, 0, 1, nct, tm, "norm_mod1")
    u1f = u1.reshape(R, D)
    p_qkv = _mm(u1f, w_qkv, "nt", "mm_qkv").reshape(B, T, 3 * HD)
    p_z = _mm(u1f, w_z, "nt", "mm_z", out_dtype=BF16).reshape(B, T, HD)
    p_ab = _mm(u1f, w_ab, "nt", "mm_ab")
    p_xl = _mm(u1f, w_xl, "nt", "mm_xl").reshape(B, T, W)
    p_yl = _mm(u1f, w_yl, "nt", "mm_yl", out_dtype=BF16).reshape(B, T, W)
    p_mg = _mm(u1f, w_mg, "nt", "mm_mg", out_dtype=BF16).reshape(B, T, 2 * D)

    conv_offs = (-2, -1, 0, 1)
    tcc = _tile(HD, 1024)
    gb = _ab_act(p_ab, alog_v, dtb_v, nd, tm)
    gb3 = gb.reshape(B, T, LANES)
    cv, qkvn = _dwconv(p_qkv, 0, 3 * HD, dn_conv_f, None, conv_offs, nct, tm, HD, "dn_conv", qkv_heads=(H, hd))
    o_d0, o_d1, s_in0, s_in1, tm_d0, tm_d1 = _delta_fwd(qkvn, gb3, H, hd, ncc)
    o2 = (o_d0, o_d1)
    y_dn = _dn_out(o2, p_z, 0, dn_onorm, H, hd, nct, tm)

    tcw = _tile(W, 1024)
    xc = _dwconv(p_xl, 0, W, lru_conv_f, lru_conv_b, conv_offs, nct, tm, tcw, "lru_conv")
    tmg = max(t_ for t_ in range(SUBLANES, min(T, 768) + 1, SUBLANES) if T % t_ == 0)
    a2, inp2 = _lru_gates(xc, wbd, bias4, lru_lam_f, tmg, bw)
    hd0 = _lru_scan(a2, inp2, 0, nct, tm, tcw)
    hd1 = _lru_scan(a2, inp2, 1, nct, tm, tcw)
    y_lru = _lru_out(hd0, hd1, p_yl, 0, nct, tm, tcw)

    def gathered(started, after, name):
        xs_, lands_ = _split_wait(started, "gather", after, name)
        return [lax.dynamic_update_slice(land, x_[None], (me,) + (0,) * x_.ndim) for land, x_ in zip(lands_, xs_)]

    w_bdn_g, w_blru_g, w_out_g = gathered(ag_mix, y_lru, "ag_mix_wait")
    w_bdn_f, w_blru_f, w_out_f = w_bdn_g.reshape(HD, D), w_blru_g.reshape(W, D), w_out_g.reshape(D, D)
    bd = _mm(y_dn.reshape(RL, HD), w_bdn_f, "nn", "mm_bdn", out_dtype=BF16).reshape(B, N, D)
    bl = _mm(y_lru.reshape(RL, W), w_blru_f, "nn", "mm_blru", out_dtype=BF16).reshape(B, N, D)
    mixin = _merge(bd, bl, p_mg, b_merge, nct, tm)
    mix = _mm(mixin.reshape(RL, D), w_out_f, "nn", "mm_out").reshape(B, N, D)
    h1 = _resid_norm_fwd(x, mix, g_post_mix, mods, 2, tm, "resid_norm1")
    u2 = _norm_mod_fwd(h1, g_pre_ffn, mods, 3, 4, 0, tm, "norm_mod2")
    w_up_g, w_down_g, ffn_dw_g = gathered(ag_ffn, u2, "ag_ffn_wait")
    w_down_f = w_down_g.reshape(Cf, D)
    ffn_dw_f = _from_col_shards(ffn_dw_g)
    Fp = _mm(u2.reshape(RL, D), w_up_g, "nn", "mm_up").reshape(B, N, 2 * Cf)
    tcf = LANES
    f, f_gel, f_dgel = _ffn_act(Fp, ffn_dw_f, ffn_dw_b, tcf)
    dproj = _mm(f.reshape(RL, Cf), w_down_f, "nn", "mm_down").reshape(B, N, D)

    dd, dh2, acc_f = _final_fwd_bwd(h1, dproj, g_post_ffn, mods, 5, loss_target, tm)
    loss = lax.psum((0.5 / D) * jnp.sum(acc_f[:, 2, :]), MESH_AXES)
    ddf = dd.reshape(RL, D)
    df = _mm(ddf, w_down_f, "nt", "mm_down_dx").reshape(B, N, Cf)
    gw_down = _mm(f.reshape(RL, Cf), ddf, "tn", "mm_down_dw", out_dtype=BF16)
    dF, acc_ffn = _ffn_act_bwd(Fp, ffn_dw_f, f_gel, f_dgel, df, tcf)
    dFf = dF.reshape(RL, 2 * Cf)
    du2 = _mm(dFf, w_up_g, "nt", "mm_up_dx").reshape(B, N, D)
    gw_up = _mm(u2.reshape(RL, D), dFf, "tn", "mm_up_dw", out_dtype=BF16, out_shards=N_DEV)
    jm = 2 * lax.axis_index("x") + lax.axis_index("y")

    def pair_sums(started, after, tag):
        parts8, sib4 = _split_wait(started, "pair", after, f"rs_pair_{tag}_wait")
        return [_pair_sum(p8, s4, f"rs_pair_sum_{tag}{i}") for i, (p8, s4) in enumerate(zip(parts8, sib4))]

    def own_slab(land, x4):
        idx = (jm,) + (0,) * (x4.ndim - 1)
        return lax.dynamic_update_slice(land, lax.dynamic_slice(x4, idx, (1,) + x4.shape[1:]), idx)

    p_early = _split_start([gw_up, gw_down.reshape(N_DEV, Cf // N_DEV, D)], "pair", "rs_pair_ffn_start")
    mods_b = mods + p_early[4][0, 0]
    dh1, acc2x, _ = _norm_mod_bwd(h1, du2, g_pre_ffn, mods_b, 4, 0, tm, dh2, "norm_mod2_bwd")
    dmix, acc_r1 = _resid_norm_bwd(mix, dh1, g_post_mix, mods, 2, tm, "resid_norm1_bwd")
    q_early = _split_start(pair_sums(p_early, dmix, "ffn"), "quad", "rs_quad_ffn_start")
    dmixf = dmix.reshape(RL, D)
    dmixin = _mm(dmixf, w_out_f, "nt", "mm_out_dx").reshape(B, N, D)
    gw_out = _mm(mixin.reshape(RL, D), dmixf, "tn", "mm_out_dw", out_dtype=BF16)
    dbd, dbl, dmg, acc_mg = _merge_bwd(dmixin, bd, bl, p_mg, b_merge + q_early[4][0:1, 0:1], nct, tm)
    dy_dn = _mm(dbd.reshape(RL, D), w_bdn_f, "nt", "mm_bdn_dx").reshape(B, N, HD)
    gw_bdn = _mm(y_dn.reshape(RL, HD), dbd.reshape(RL, D), "tn", "mm_bdn_dw", out_dtype=BF16)
    dy_lru = _mm(dbl.reshape(RL, D), w_blru_f, "nt", "mm_blru_dx").reshape(B, N, W)
    gw_blru = _mm(y_lru.reshape(RL, W), dbl.reshape(RL, D), "tn", "mm_blru_dw", out_dtype=BF16)

    dh, dyl = _lru_out_bwd(hd0, hd1, p_yl, 0, dy_lru, nct, tm, tcw)
    lam0, da0 = _lru_scan_bwd(a2, hd0, dh, 0, nct, tm, tcw)
    lam1, da1 = _lru_scan_bwd(a2, hd1, dh, 1, nct, tm, tcw)
    dxc, dwbd, acc_lru = _lru_gates_bwd(xc, wbd, bias4, lru_lam_f, (lam0, lam1), (da0, da1), tmg, bw)
    dxl = _dwconv(dxc, 0, W, lru_conv_f, None, tuple(-o for o in conv_offs), nct, tm, tcw, "lru_conv_dx", BF16)
    acc_lc = _dwconv_wgrad(dxc, p_xl, 0, W, conv_offs, nct, tm, tcw, "lru_conv_dw")

    do, dz, acc_on = _dn_out_bwd(o2, p_z, 0, dn_onorm, dy_dn, H, hd, nct, tm)
    dqkvn0, dqkvn1, dgb0, dgb1 = _delta_bwd(qkvn, gb3, (s_in0, s_in1), (tm_d0, tm_d1), do, H, hd, ncc)
    dcv = _dn_act_bwd(cv, (dqkvn0, dqkvn1), H, hd, tm)
    dqkv = _dwconv(dcv, 0, 3 * HD, dn_conv_f, None, tuple(-o for o in conv_offs), nct, tm, tcc, "dn_conv_dx", BF16)
    acc_dc = _dwconv_wgrad(dcv, p_qkv, 0, 3 * HD, conv_offs, nct, tm, tcc, "dn_conv_dw")
    dp_ab, acc_ab = _ab_act_bwd(p_ab, gb, (dgb0.reshape(R, LANES), dgb1.reshape(R, LANES)), alog_v, dtb_v, nd, tm)

    groups = [(dqkv.reshape(R, 3 * HD), w_qkv, "qkv"), (dz.reshape(R, HD), w_z, "z"), (dp_ab, w_ab, "ab"),
              (dxl.reshape(R, W), w_xl, "xl"), (dyl.reshape(R, W), w_yl, "yl"), (dmg.reshape(R, 2 * D), w_mg, "mg")]
    du1 = _mm_sum([(dg_, wg_) for dg_, wg_, _ in groups], "mm_in_dx")
    gw_groups = [_mm(dg_, u1f, "tn", "mm_in_dw_" + nm, out_dtype=BF16) for dg_, _, nm in groups]
    gw_groups[2] = gw_groups[2][:2 * nd]
    gw_in = jnp.concatenate(gw_groups, axis=0).reshape(N_DEV, csh, D)
    grad_x, acc1x, acc1c = _norm_mod_bwd(h0, du1.reshape(B, T, D), g_pre_mix, mods, 1, nct, tm, dh1, "norm_mod1_bwd")

    g_lru_conv = jnp.sum(acc_lc[:, :4], 0)
    g_dn_conv = jnp.sum(acc_dc[:, :4], 0)
    g_ffn_dw = jnp.sum(acc_ffn[:, :9], 0).reshape(3, 3, Cf)
    sm_names = ["dn_conv", "lru_conv", "lru_b_rg", "lru_b_ig", "lru_lambda", "ffn_dw"]
    sm_parts = [_col_shards(g_dn_conv), _col_shards(g_lru_conv),
                _col_shards(jnp.stack([acc_lru[0], acc_lru[2]])), _col_shards(jnp.stack([acc_lru[1], acc_lru[3]])),
                _col_shards(acc_lru[4:6]), _col_shards(g_ffn_dw)]
    late8 = [gw_in, gw_bdn.reshape(N_DEV, HD // N_DEV, D), gw_blru.reshape(N_DEV, W // N_DEV, D),
             gw_out.reshape(N_DEV, D // N_DEV, D), _pack(sm_parts, lead=(N_DEV,))]
    p_late = _split_start(late8, "pair", "rs_pair_mix_start")
    q_late = _split_start(pair_sums(p_late, gw_out, "mix"), "quad", "rs_quad_mix_start")
    ffn_x, ffn_land = _split_wait(q_early, "quad", q_late[4], "rs_quad_ffn_wait")
    results = {}
    for n, x4, land in zip(["w_up", "w_down"], ffn_x, ffn_land):
        results[n] = list(_adamw(params[n], mom1[n], mom2[n], own_slab(land, x4), "adamw_" + n))

    zeros_d = jnp.zeros((B, D), F32)
    dm_rows = jnp.stack([acc1x[:, 0], acc1x[:, 1], acc_r1[:, 0], acc2x[:, 0], acc2x[:, 1], acc_f[:, 0]], axis=1)
    dm_ctx = jnp.stack([jnp.sum(acc1c[:, 0], 0), jnp.sum(acc1c[:, 1], 0)] + [zeros_d[0]] * 4, axis=0)[None]
    dm_pad = jnp.zeros((SUBLANES, 6 * D), F32).at[:B + 1].set(jnp.concatenate([dm_rows, dm_ctx], 0).reshape(B + 1, 6 * D))
    dm_g = _all_gather([dm_pad], "ag_dmods")[0]
    g_mine = lax.dynamic_slice(dm_g, (0, 0, me * sw), (N_DEV, SUBLANES, sw)).reshape(N_DEV * SUBLANES, sw)
    gw_ada, dcc = _ada_bwd(ccp, w_ada[0], g_mine, c_ctx.reshape(1, D), B)

    per = bw // bdim

    def diag_blocks(dwf):
        g6 = dwf.reshape(W // bw, per, bdim, per, bdim)
        return jnp.einsum("tpiqj,pq->tpij", g6, jnp.eye(per, dtype=F32)).reshape(nblk, bdim, bdim)

    g_rep = dict(
        c_ctx=dcc[0],
        g_pre_mix=jnp.sum(acc1x[:, 2] + acc1c[:, 2], 0)[None], g_post_mix=jnp.sum(acc_r1[:, 1], 0)[None],
        g_pre_ffn=jnp.sum(acc2x[:, 2], 0)[None], g_post_ffn=jnp.sum(acc_f[:, 1], 0)[None],
        b_merge=jnp.sum(acc_mg[:, 0], 0)[None],
        dn_a_log=acc_ab[0, :nd].reshape(1, 2, H), dn_dt_bias=acc_ab[1, :nd].reshape(1, 2, H),
        dn_onorm=jnp.sum(acc_on[:, 0], 0)[None],
        lru_conv_b=jnp.sum(acc_lc[:, 4], 0)[None],
        lru_w_rg=jnp.stack([diag_blocks(dwbd[0]), diag_blocks(dwbd[2])])[None],
        lru_w_ig=jnp.stack([diag_blocks(dwbd[1]), diag_blocks(dwbd[3])])[None],
        ffn_dw_b=jnp.sum(acc_ffn[:, 9], 0)[None])
    mat_names = ["lru_w_rg", "lru_w_ig"]
    vec_names = [n for n in g_rep if n not in mat_names]
    vec_parts, mat_parts = _all_gather([_pack([g_rep[n] for n in vec_names]),
                                        _pack([g_rep[n] for n in mat_names]).astype(BF16)], "ag_rep_grads")
    for grp, parts, nm in ((vec_names, vec_parts, "adamw_rep_vec"), (mat_names, mat_parts, "adamw_rep_mat")):
        out4 = _adamw(_pack([params[n] for n in grp]), _pack([mom1[n] for n in grp]),
                      _pack([mom2[n] for n in grp]), parts, nm)
        for k, buf in enumerate(out4):
            for n, arr in zip(grp, _unpack(buf, [params[n].shape for n in grp])):
                results.setdefault(n, [None] * 4)[k] = arr

    ba_out = _adamw(_pack([b_ada]), _pack([m_b_ada]), _pack([v_b_ada]),
                    _pack([dm_g.reshape(N_DEV * SUBLANES, 6 * D)], lead=(N_DEV * SUBLANES,)), "adamw_b_ada")
    results["b_ada"] = [_unpack(buf, [b_ada.shape])[0] for buf in ba_out]

    wa_out = _adamw(w_ada, m_w_ada, v_w_ada, gw_ada[None], "adamw_w_ada")
    results["w_ada"] = list(wa_out)

    mix_x, mix_land = _split_wait(q_late, "quad", wa_out[0], "rs_quad_mix_wait")
    recv4 = [own_slab(land, x4) for land, x4 in zip(mix_land, mix_x)]
    for n, g4 in zip(["w_in", "w_branch_dn", "w_branch_lru", "w_out"], recv4[:-1]):
        if n == "w_in":
            g4 = jnp.swapaxes(g4, 1, 2)
        results[n] = list(_adamw(params[n], mom1[n], mom2[n], g4, "adamw_" + n))
    sm_out = _adamw(_pack([params[n] for n in sm_names]), _pack([mom1[n] for n in sm_names]),
                    _pack([mom2[n] for n in sm_names]), recv4[-1], "adamw_small")
    for k, buf in enumerate(sm_out):
        for n, arr in zip(sm_names, _unpack(buf, [params[n].shape for n in sm_names])):
            results.setdefault(n, [None] * 4)[k] = arr

    outs = [loss, grad_x]
    for k in range(4):
        outs += [results[n][k] for n in names]
    return tuple(outs)
```

```python
import math

import jax
import jax.numpy as jnp
from jax import lax
from jax.experimental import pallas as pl
from jax.experimental.pallas import tpu as pltpu

F32 = jnp.float32
BF16 = jnp.bfloat16

EPS = 1e-6
GRID_W = 64
CHUNK = 64
LRU_C = 8.0
N_DEV = 8
ADAM_LR = 0.001
ADAM_B1 = 0.9
ADAM_B2 = 0.999
ADAM_EPS = 1e-08
ADAM_WD = 0.01
ADAM_STEP = 10

LANES = 128
SUBLANES = 8
VMEM_LIMIT = 48 * 1024 * 1024
MESH_AXES = ("x", "y", "c")
GELU_C = math.sqrt(2.0 / math.pi)


def _cparams(sem=None, vmem=None):
    kw = {}
    if sem is not None:
        kw["dimension_semantics"] = sem
    if vmem is not None:
        kw["vmem_limit_bytes"] = vmem
    return pltpu.CompilerParams(**kw)


def _tile(n, pref):
    if n <= pref:
        return n
    t = (pref // LANES) * LANES
    while n % t:
        t -= LANES
    return t


def _sigmoid(x):
    return 1.0 / (1.0 + jnp.exp(-x))


def _silu(x):
    return x * _sigmoid(x)


def _dsilu(x):
    s = _sigmoid(x)
    return s * (1.0 + x * (1.0 - s))


def _softplus(x):
    return jnp.maximum(x, 0.0) + jnp.log(1.0 + jnp.exp(-jnp.abs(x)))


def _gelu(x):
    return 0.5 * x * (1.0 + jnp.tanh(GELU_C * (x + 0.044715 * x * x * x)))


def _dgelu(x):
    t = jnp.tanh(GELU_C * (x + 0.044715 * x * x * x))
    return 0.5 * (1.0 + t) + 0.5 * x * (1.0 - t * t) * GELU_C * (1.0 + 3.0 * 0.044715 * x * x)


def _dot(a, b, dims="nn"):
    dn = {"nn": (((1,), (0,)), ((), ())),
          "nt": (((1,), (1,)), ((), ())),
          "tn": (((0,), (0,)), ((), ()))}[dims]
    return lax.dot_general(a.astype(BF16), b.astype(BF16), dn, preferred_element_type=F32)


def _split2(x):
    hi = x.astype(BF16)
    lo = (x - hi.astype(F32)).astype(BF16)
    return hi, lo


def _split3(x):
    h1 = x.astype(BF16)
    r1 = x - h1.astype(F32)
    h2 = r1.astype(BF16)
    h3 = (r1 - h2.astype(F32)).astype(BF16)
    return h1, h2, h3


def _dot_hi(a, b):
    ah, al = _split2(a)
    bh, bl = _split2(b)
    return _dot(ah, bh) + (_dot(ah, bl) + _dot(al, bh))


def _dot_mask(m, x, dims="nn"):
    h1, h2, h3 = _split3(x)
    if dims == "xtn":
        return _dot(h1, m, "tn") + (_dot(h2, m, "tn") + _dot(h3, m, "tn"))
    return _dot(m, h1, dims) + (_dot(m, h2, dims) + _dot(m, h3, dims))


def _my_index():
    return 4 * lax.axis_index("x") + 2 * lax.axis_index("y") + lax.axis_index("c")


def _hbm_call(body, xs, out_shapes, n_sems, name):
    any_spec = pl.BlockSpec(memory_space=pl.ANY)
    return pl.pallas_call(
        body, name=name, out_shape=tuple(out_shapes),
        in_specs=[any_spec] * len(xs), out_specs=tuple([any_spec] * len(out_shapes)),
        scratch_shapes=[pltpu.SemaphoreType.DMA((n_sems,)), pltpu.SemaphoreType.DMA((n_sems,)),
                        pltpu.SemaphoreType.DMA((len(xs),))],
    )(*xs)


def _all_gather(xs, name):
    n = len(xs)

    def body(*refs):
        x_refs, out_refs = refs[:n], refs[n:2 * n]
        send_sems, recv_sems, local_sems = refs[2 * n:]
        x_, y_, c_ = lax.axis_index("x"), lax.axis_index("y"), lax.axis_index("c")
        me, sibling = (x_, y_, c_), (x_, y_, 1 - c_)
        chips = [(1 - x_, y_), (x_, 1 - y_), (1 - x_, 1 - y_)]

        def slab(a, px, py, pc):
            return out_refs[a].at[4 * px + 2 * py + pc]

        def copy(a, k, block, to, src=None):
            return pltpu.make_async_remote_copy(
                src_ref=slab(a, *block) if src is None else src, dst_ref=slab(a, *block),
                send_sem=send_sems.at[7 * a + k], recv_sem=recv_sems.at[7 * a + k],
                device_id=to, device_id_type=pl.DeviceIdType.MESH)

        mine = [pltpu.make_async_copy(x_refs[a], slab(a, *me), local_sems.at[a]) for a in range(n)]
        for cp in mine:
            cp.start()
        sent = []
        for j, chip in enumerate(chips):
            sent += [copy(a, 1 + j, me, (*chip, c_), src=x_refs[a]) for a in range(n)]
        sent += [copy(a, 0, me, sibling, src=x_refs[a]) for a in range(n)]
        for cp in sent:
            cp.start()
        for j, chip in enumerate(chips):
            for a in range(n):
                copy(a, 1 + j, (*chip, c_), me).wait_recv()
                fwd = copy(a, 4 + j, (*chip, c_), sibling)
                fwd.start()
                sent.append(fwd)
        for a in range(n):
            copy(a, 0, sibling, me).wait_recv()
        for j, chip in enumerate(chips):
            for a in range(n):
                copy(a, 4 + j, (*chip, 1 - c_), me).wait_recv()
        for cp in sent:
            cp.wait_send()
        for cp in mine:
            cp.wait()

    outs = [jax.ShapeDtypeStruct((N_DEV,) + x.shape, x.dtype) for x in xs]
    return _hbm_call(body, xs, outs, 7 * n, name)


SPLIT_COPIES = {"gather": 7, "quad": 3, "pair": 4}


def _split_views(kind, x_ref, land_ref, k):
    x_, y_, c_ = lax.axis_index("x"), lax.axis_index("y"), lax.axis_index("c")
    if kind == "pair":
        return x_ref.at[2 * (k - 1) + (1 - c_)], land_ref.at[k - 1], land_ref.at[k - 1], (x_, y_, 1 - c_)
    if kind == "gather":
        px = 1 - x_ if (k >> 2) & 1 else x_
        py = 1 - y_ if (k >> 1) & 1 else y_
        pc = 1 - c_ if k & 1 else c_
        return x_ref, land_ref.at[4 * x_ + 2 * y_ + c_], land_ref.at[4 * px + 2 * py + pc], (px, py, pc)
    px = 1 - x_ if (k >> 1) & 1 else x_
    py = 1 - y_ if k & 1 else y_
    return x_ref.at[2 * px + py], land_ref.at[2 * x_ + y_], land_ref.at[2 * px + py], (px, py, c_)


def _split_start(xs, kind, name):
    n = len(xs)
    npeer = SPLIT_COPIES[kind]
    land_shape = {"gather": lambda x: (N_DEV,) + x.shape, "quad": lambda x: x.shape, "pair": lambda x: (4,) + x.shape[1:]}
    lands = [lax.empty(land_shape[kind](x), x.dtype) for x in xs]

    def body(*refs):
        x_refs, land_refs = refs[:n], refs[n:2 * n]
        send_sems, recv_sems, token = refs[2 * n], refs[2 * n + 1], refs[4 * n + 2]
        for k in range(1, npeer + 1):
            for a in range(n):
                src, dst, _, peer = _split_views(kind, x_refs[a], land_refs[a], k)
                pltpu.make_async_remote_copy(
                    src_ref=src, dst_ref=dst, send_sem=send_sems.at[npeer * a + k - 1],
                    recv_sem=recv_sems.at[npeer * a + k - 1],
                    device_id=peer, device_id_type=pl.DeviceIdType.MESH).start()
        token[...] = jnp.zeros_like(token)

    hbm = pl.BlockSpec(memory_space=pltpu.HBM)
    sem = pl.BlockSpec(memory_space=pltpu.SEMAPHORE)
    sems = pltpu.SemaphoreType.DMA((npeer * n,))
    out = pl.pallas_call(
        body, name=name,
        out_shape=(sems, sems, *[pltpu.HBM(a.shape, a.dtype) for a in list(xs) + lands],
                   jax.ShapeDtypeStruct((SUBLANES, LANES), F32)),
        in_specs=[hbm] * (2 * n),
        out_specs=(sem, sem, *[hbm] * (2 * n), pl.BlockSpec(memory_space=pltpu.VMEM)),
        input_output_aliases={i: 2 + i for i in range(2 * n)},
        compiler_params=pltpu.CompilerParams(has_side_effects=pltpu.SideEffectType.DATAFLOW_SIDE_EFFECTING),
    )(*[pltpu.with_memory_space_constraint(a, pltpu.HBM) for a in list(xs) + lands])
    return out[0], out[1], list(out[2:2 + n]), list(out[2 + n:2 + 2 * n]), out[-1]


def _split_wait(started, kind, after, name):
    send_sems_, recv_sems_, x_thru, land_thru, _ = started
    n = len(x_thru)
    npeer = SPLIT_COPIES[kind]

    def body(*refs):
        x_refs, land_refs = refs[:n], refs[n:2 * n]
        send_sems, recv_sems = refs[2 * n], refs[2 * n + 1]
        for k in range(1, npeer + 1):
            for a in range(n):
                src, _, landed, peer = _split_views(kind, x_refs[a], land_refs[a], k)
                cp = pltpu.make_async_remote_copy(
                    src_ref=src, dst_ref=landed, send_sem=send_sems.at[npeer * a + k - 1],
                    recv_sem=recv_sems.at[npeer * a + k - 1],
                    device_id=peer, device_id_type=pl.DeviceIdType.MESH)
                cp.wait_send()
                cp.wait_recv()

    hbm = pl.BlockSpec(memory_space=pltpu.HBM)
    sem = pl.BlockSpec(memory_space=pltpu.SEMAPHORE)
    out = pl.pallas_call(
        body, name=name,
        out_shape=tuple(pltpu.HBM(a.shape, a.dtype) for a in x_thru + land_thru),
        in_specs=[hbm] * (2 * n) + [sem, sem, pl.BlockSpec(memory_space=pl.ANY)],
        out_specs=tuple([hbm] * (2 * n)),
        input_output_aliases={i: i for i in range(2 * n)},
        compiler_params=pltpu.CompilerParams(has_side_effects=pltpu.SideEffectType.DATAFLOW_SIDE_EFFECTING),
    )(*x_thru, *land_thru, send_sems_, recv_sems_, after)
    return list(out[:n]), list(out[n:])


def _pair_sum(x8, r4, name):
    _, r, c = x8.shape
    tr = _row_tile(r, c * 12)

    def body(core_ref, x_ref, r_ref, o_ref):
        o_ref[...] = (x_ref[...].astype(F32) + r_ref[...].astype(F32)).astype(o_ref.dtype)

    core = lax.axis_index("c").astype(jnp.int32).reshape(1)
    return pl.pallas_call(
        body, name=name,
        grid_spec=pltpu.PrefetchScalarGridSpec(
            num_scalar_prefetch=1, grid=(4, r // tr),
            in_specs=[pl.BlockSpec((None, tr, c), lambda j, i, core_ref: (2 * j + core_ref[0], i, 0)),
                      pl.BlockSpec((None, tr, c), lambda j, i, core_ref: (j, i, 0))],
            out_specs=pl.BlockSpec((None, tr, c), lambda j, i, core_ref: (j, i, 0))),
        out_shape=jax.ShapeDtypeStruct((4, r, c), x8.dtype),
        compiler_params=_cparams(("parallel", "parallel"), VMEM_LIMIT))(core, x8, r4)


def _row_tile(r, bytes_per_row, budget=10 * 1024 * 1024):
    best = None
    for t in range(16, r + 1, 16):
        if r % t == 0 and t * bytes_per_row <= budget:
            best = t
    return best if best is not None else r


def _mm(a, b, dims, name, out_dtype=F32, out_shards=1, tm=1024, tn=1024, tk=1024):
    S = b.shape[0] if b.ndim == 3 else 1
    bshape = (b.shape[1], S * b.shape[2]) if b.ndim == 3 else b.shape
    if dims == "nn":
        (M, K), (K2, N) = a.shape, bshape
    elif dims == "nt":
        (M, K), (N, K2) = a.shape, bshape
    else:
        (K, M), (K2, N) = a.shape, bshape
    assert K == K2, (a.shape, b.shape, dims)
    cs = bshape[1] // S
    tm, tk = _tile(M, tm), _tile(K, min(tk, cs) if (S > 1 and dims == "nt") else tk)
    tn = _tile(N, min(tn, cs) if (S > 1 and dims != "nt") else min(tn, N // out_shards))
    assert cs % (tk if dims == "nt" else tn) == 0 and (N // out_shards) % tn == 0
    nk = K // tk

    def body(a_ref, b_ref, o_ref, acc):
        k = pl.program_id(2)

        @pl.when(k == 0)
        def _():
            acc[...] = jnp.zeros_like(acc)

        acc[...] += _dot(a_ref[...], b_ref[...], dims)

        @pl.when(k == nk - 1)
        def _():
            o_ref[...] = acc[...].astype(out_dtype)

    a_spec =(pl.BlockSpec((tk, tm), lambda i, j, k: (k, i)) if dims == "tn"
              else pl.BlockSpec((tm, tk), lambda i, j, k: (i, k)))
    if S == 1:
        b_spec = (pl.BlockSpec((tn, tk), lambda i, j, k: (j, k)) if dims == "nt"
                  else pl.BlockSpec((tk, tn), lambda i, j, k: (k, j)))
    elif dims == "nt":
        per = cs // tk
        b_spec = pl.BlockSpec((None, tn, tk), lambda i, j, k: (k // per, j, k % per))
    else:
        per = cs // tn
        b_spec = pl.BlockSpec((None, tk, tn), lambda i, j, k: (j // per, k, j % per))
    if out_shards == 1:
        out_spec, out_shape = pl.BlockSpec((tm, tn), lambda i, j, k: (i, j)), (M, N)
    else:
        oper = N // out_shards // tn
        out_spec = pl.BlockSpec((None, tm, tn), lambda i, j, k: (j // oper, i, j % oper))
        out_shape = (out_shards, M, N // out_shards)
    return pl.pallas_call(
        body, name=name, grid=(M // tm, N // tn, nk),
        in_specs=[a_spec, b_spec], out_specs=out_spec,
        out_shape=jax.ShapeDtypeStruct(out_shape, out_dtype),
        scratch_shapes=[pltpu.VMEM((tm, tn), F32)],
        compiler_params=_cparams(("parallel", "parallel", "arbitrary"), VMEM_LIMIT),
    )(a, b)


VMEM_LIMIT_SUM = 56 * 1024 * 1024


def _mm_sum(pairs, name, out_dtype=F32, tm=768, tn=1024, tk=1024):
    M, N = pairs[0][0].shape[0], pairs[0][1].shape[1]
    tm, tn = _tile(M, tm), _tile(N, tn)
    tks = [_tile(a.shape[1], tk) for a, _ in pairs]
    nks = [a.shape[1] // t for (a, _), t in zip(pairs, tks)]
    starts = [sum(nks[:g]) for g in range(len(pairs))]
    nk = sum(nks)
    G = len(pairs)

    def body(*refs):
        o_ref, acc = refs[2 * G], refs[2 * G + 1]
        k = pl.program_id(2)

        @pl.when(k == 0)
        def _():
            acc[...] = jnp.zeros_like(acc)

        for g in range(G):
            @pl.when(jnp.logical_and(k >= starts[g], k < starts[g] + nks[g]))
            def _(g=g):
                acc[...] += _dot(refs[2 * g][...], refs[2 * g + 1][...])

        @pl.when(k == nk - 1)
        def _():
            o_ref[...] = acc[...].astype(out_dtype)

    in_specs, args = [], []
    for g, (a, b) in enumerate(pairs):
        kk = lambda k, g=g: jnp.clip(k - starts[g], 0, nks[g] - 1)
        in_specs += [pl.BlockSpec((tm, tks[g]), lambda i, j, k, kk=kk: (i, kk(k))),
                     pl.BlockSpec((tks[g], tn), lambda i, j, k, kk=kk: (kk(k), j))]
        args += [a, b]
    return pl.pallas_call(
        body, name=name, grid=(M // tm, N // tn, nk),
        in_specs=in_specs, out_specs=pl.BlockSpec((tm, tn), lambda i, j, k: (i, j)),
        out_shape=jax.ShapeDtypeStruct((M, N), out_dtype),
        scratch_shapes=[pltpu.VMEM((tm, tn), F32)],
        compiler_params=_cparams(("parallel", "parallel", "arbitrary"), VMEM_LIMIT_SUM),
    )(*args)


def _ada_fwd(ccp, w, b):
    def body(c_ref, w_ref, b_ref, o_ref):
        o_ref[...] = _dot(_silu(c_ref[...]), w_ref[...]) + b_ref[...]

    return pl.pallas_call(
        body, name="ada_fwd", out_shape=jax.ShapeDtypeStruct((ccp.shape[0], w.shape[1]), F32),
        compiler_params=_cparams(None, VMEM_LIMIT))(ccp, w, b)


def _ada_bwd(ccp, w, g, cctx, n_loc):
    def body(c_ref, w_ref, g_ref, cc_ref, gw_ref, dc_ref):
        gw_ref[...] = _dot(_silu(c_ref[...]), g_ref[...], "tn")
        dcc = _dot(g_ref[...], w_ref[...], "nt")
        row = lax.broadcasted_iota(jnp.int32, dcc.shape, 0)
        part = jnp.sum(jnp.where(row % SUBLANES == n_loc, dcc, 0.0), axis=0, keepdims=True)
        dc_ref[...] = jnp.broadcast_to(part * _dsilu(cc_ref[...]), dc_ref.shape)

    D = ccp.shape[1]
    return pl.pallas_call(
        body, name="ada_bwd",
        out_shape=(jax.ShapeDtypeStruct(w.shape, F32), jax.ShapeDtypeStruct((SUBLANES, D), F32)),
        compiler_params=_cparams(None, VMEM_LIMIT))(ccp, w, g, cctx)


def _norm_mod_fwd(h, gain, mods, i_shift, i_scale, nct, tm, name):
    B, T, D = h.shape

    def body(h_ref, g_ref, m_ref, u_ref):
        x = h_ref[0]
        r = lax.rsqrt(jnp.mean(x * x, axis=-1, keepdims=True) + EPS)
        n = x * r * g_ref[...]
        u_ref[0] = (n * (1.0 + m_ref[0, i_scale:i_scale + 1, :]) + m_ref[0, i_shift:i_shift + 1, :]).astype(BF16)

    return pl.pallas_call(
        body, name=name, grid=(B, T // tm),
        in_specs=[pl.BlockSpec((1, tm, D), lambda b, t: (b, t, 0)),
                  pl.BlockSpec((1, D), lambda b, t: (0, 0)),
                  pl.BlockSpec((1, SUBLANES, D), lambda b, t: (jnp.where(t < nct, B, b), 0, 0))],
        out_specs=pl.BlockSpec((1, tm, D), lambda b, t: (b, t, 0)),
        out_shape=jax.ShapeDtypeStruct((B, T, D), BF16),
        compiler_params=_cparams(("parallel", "parallel"), VMEM_LIMIT),
    )(h, gain, mods)


def _norm_mod_bwd(h, du, gain, mods, i_scale, nct, tm, res, name):
    B, T, D = h.shape
    nt = T // tm

    def body(h_ref, du_ref, g_ref, m_ref, res_ref, dx_ref, ax_ref, ac_ref):
        t = pl.program_id(1)
        x = h_ref[0]
        r = lax.rsqrt(jnp.mean(x * x, axis=-1, keepdims=True) + EPS)
        nh = x * r
        g = g_ref[...]
        du_ = du_ref[0]
        dn = du_ * (1.0 + m_ref[0, i_scale:i_scale + 1, :])
        dnh = dn * g
        dx = r * (dnh - nh * jnp.mean(dnh * nh, axis=-1, keepdims=True))
        sums = (jnp.sum(du_, axis=0, keepdims=True), jnp.sum(du_ * nh * g, axis=0, keepdims=True),
                jnp.sum(dn * nh, axis=0, keepdims=True))

        @pl.when(t == 0)
        def _():
            ax_ref[...] = jnp.zeros_like(ax_ref)
            ac_ref[...] = jnp.zeros_like(ac_ref)

        def accumulate(ref):
            for i, s in enumerate(sums):
                ref[0, i:i + 1, :] += s

        @pl.when(t < nct)
        def _():
            accumulate(ac_ref)

        @pl.when(t >= nct)
        def _():
            accumulate(ax_ref)
            dx_ref[0] = dx + res_ref[0]

    lat = lambda b, t: (b, jnp.maximum(t - nct, 0), 0)
    acc = pl.BlockSpec((1, SUBLANES, D), lambda b, t: (b, 0, 0))
    return pl.pallas_call(
        body, name=name, grid=(B, nt),
        in_specs=[pl.BlockSpec((1, tm, D), lambda b, t: (b, t, 0)),
                  pl.BlockSpec((1, tm, D), lambda b, t: (b, t, 0)),
                  pl.BlockSpec((1, D), lambda b, t: (0, 0)),
                  pl.BlockSpec((1, SUBLANES, D), lambda b, t: (jnp.where(t < nct, B, b), 0, 0)),
                  pl.BlockSpec((1, tm, D), lat)],
        out_specs=(pl.BlockSpec((1, tm, D), lat), acc, acc),
        out_shape=(jax.ShapeDtypeStruct(res.shape, F32),
                   jax.ShapeDtypeStruct((B, SUBLANES, D), F32), jax.ShapeDtypeStruct((B, SUBLANES, D), F32)),
        compiler_params=_cparams(("parallel", "arbitrary"), VMEM_LIMIT),
    )(h, du, gain, mods, res)


def _resid_norm_fwd(x, y, gain, mods, i_gate, tm, name):
    B, N, D = x.shape

    def body(x_ref, y_ref, g_ref, m_ref, o_ref):
        y_ = y_ref[0]
        r = lax.rsqrt(jnp.mean(y_ * y_, axis=-1, keepdims=True) + EPS)
        o_ref[0] = x_ref[0] + y_ * r * g_ref[...] * m_ref[0, i_gate:i_gate + 1, :]

    row = pl.BlockSpec((1, tm, D), lambda b, t: (b, t, 0))
    return pl.pallas_call(
        body, name=name, grid=(B, N // tm),
        in_specs=[row, row, pl.BlockSpec((1, D), lambda b, t: (0, 0)),
                  pl.BlockSpec((1, SUBLANES, D), lambda b, t: (b, 0, 0))],
        out_specs=row, out_shape=jax.ShapeDtypeStruct((B, N, D), F32),
        compiler_params=_cparams(("parallel", "parallel"), VMEM_LIMIT),
    )(x, y, gain, mods)


def _resid_norm_bwd_math(y_, dh, g, gate):
    r = lax.rsqrt(jnp.mean(y_ * y_, axis=-1, keepdims=True) + EPS)
    nh = y_ * r
    dgate = jnp.sum(dh * nh * g, axis=0, keepdims=True)
    dn = dh * gate
    dgain = jnp.sum(dn * nh, axis=0, keepdims=True)
    dnh = dn * g
    dy = r * (dnh - nh * jnp.mean(dnh * nh, axis=-1, keepdims=True))
    return dy, dgate, dgain


def _resid_norm_bwd(y, dh, gain, mods, i_gate, tm, name):
    B, N, D = y.shape

    def body(y_ref, dh_ref, g_ref, m_ref, dy_ref, acc_ref):
        t = pl.program_id(1)
        dy, dgate, dgain = _resid_norm_bwd_math(y_ref[0], dh_ref[0], g_ref[...], m_ref[0, i_gate:i_gate + 1, :])
        dy_ref[0] = dy.astype(BF16)

        @pl.when(t == 0)
        def _():
            acc_ref[...] = jnp.zeros_like(acc_ref)

        acc_ref[0, 0:1, :] += dgate
        acc_ref[0, 1:2, :] += dgain

    row = pl.BlockSpec((1, tm, D), lambda b, t: (b, t, 0))
    return pl.pallas_call(
        body, name=name, grid=(B, N // tm),
        in_specs=[row, row, pl.BlockSpec((1, D), lambda b, t: (0, 0)),
                  pl.BlockSpec((1, SUBLANES, D), lambda b, t: (b, 0, 0))],
        out_specs=(row, pl.BlockSpec((1, SUBLANES, D), lambda b, t: (b, 0, 0))),
        out_shape=(jax.ShapeDtypeStruct((B, N, D), BF16), jax.ShapeDtypeStruct((B, SUBLANES, D), F32)),
        compiler_params=_cparams(("parallel", "arbitrary"), VMEM_LIMIT),
    )(y, dh, gain, mods)


def _final_fwd_bwd(h1, d, gain, mods, i_gate, tgt, tm):
    B, N, D = h1.shape

    def body(h_ref, d_ref, g_ref, m_ref, t_ref, dd_ref, dh_ref, acc_ref):
        t = pl.program_id(1)
        d_ = d_ref[0]
        g = g_ref[...]
        gate = m_ref[0, i_gate:i_gate + 1, :]
        r = lax.rsqrt(jnp.mean(d_ * d_, axis=-1, keepdims=True) + EPS)
        e = h_ref[0] + d_ * r * g * gate - t_ref[0]
        dh = e * (1.0 / D)
        dh_ref[0] = dh
        dy, dgate, dgain = _resid_norm_bwd_math(d_, dh, g, gate)
        dd_ref[0] = dy.astype(BF16)

        @pl.when(t == 0)
        def _():
            acc_ref[...] = jnp.zeros_like(acc_ref)

        acc_ref[0, 0:1, :] += dgate
        acc_ref[0, 1:2, :] += dgain
        acc_ref[0, 2:3, :] += jnp.sum(e * e, axis=0, keepdims=True)

    row = pl.BlockSpec((1, tm, D), lambda b, t: (b, t, 0))
    return pl.pallas_call(
        body, name="final_fwd_bwd", grid=(B, N // tm),
        in_specs=[row, row, pl.BlockSpec((1, D), lambda b, t: (0, 0)),
                  pl.BlockSpec((1, SUBLANES, D), lambda b, t: (b, 0, 0)), row],
        out_specs=(row, row, pl.BlockSpec((1, SUBLANES, D), lambda b, t: (b, 0, 0))),
        out_shape=(jax.ShapeDtypeStruct((B, N, D), BF16), jax.ShapeDtypeStruct((B, N, D), F32),
                   jax.ShapeDtypeStruct((B, SUBLANES, D), F32)),
        compiler_params=_cparams(("parallel", "arbitrary"), VMEM_LIMIT),
    )(h1, d, gain, mods, tgt)


def _shifted(x, prev, nxt, off, row, tm):
    if off == 0:
        return x
    if off < 0:
        y = pltpu.roll(x, -off, 0)
        for r in range(-off):
            y = jnp.where(row == r, prev[SUBLANES + r + off:SUBLANES + r + off + 1, :], y)
        return y
    y = pltpu.roll(x, tm - off, 0)
    for r in range(off):
        y = jnp.where(row == tm - off + r, nxt[r:r + 1, :], y)
    return y


def _halo_specs(T, tm, tc, cb0, order):
    r8, n8 = tm // SUBLANES, T // SUBLANES
    if order == "btj":
        prev = lambda b, t, j: (b, jnp.maximum(t * r8 - 1, 0), cb0 + j)
        nxt = lambda b, t, j: (b, jnp.minimum((t + 1) * r8, n8 - 1), cb0 + j)
    else:
        prev = lambda b, j, t: (b, jnp.maximum(t * r8 - 1, 0), cb0 + j)
        nxt = lambda b, j, t: (b, jnp.minimum((t + 1) * r8, n8 - 1), cb0 + j)
    return pl.BlockSpec((1, SUBLANES, tc), prev), pl.BlockSpec((1, SUBLANES, tc), nxt)


def _seg_halos(p_ref, n_ref, t, nct, nt):
    seg_start = jnp.logical_or(t == 0, t == nct)
    seg_end = jnp.logical_or(t == nct - 1, t == nt - 1)
    prev = jnp.where(seg_start, 0.0, p_ref[0])
    nxt = jnp.where(seg_end, 0.0, n_ref[0])
    return prev, nxt


def _dwconv(x, col0, C, w, bias, offs, nct, tm, tc, name, out_dtype=F32, qkv_heads=None):
    B, T, _ = x.shape
    nt = T // tm
    cb0 = col0 // tc
    if qkv_heads is not None:
        assert tc == qkv_heads[0] * qkv_heads[1] and C == 3 * tc

    def body(*refs):
        refs = list(refs)
        a_ref = refs.pop() if qkv_heads is not None else None
        if bias is None:
            x_ref, p_ref, n_ref, w_ref, o_ref = refs
        else:
            x_ref, p_ref, n_ref, w_ref, b_ref, o_ref = refs
        t = pl.program_id(1)
        prev, nxt = _seg_halos(p_ref, n_ref, t, nct, nt)
        x_ = x_ref[0]
        row = lax.broadcasted_iota(jnp.int32, x_.shape, 0)
        acc = None
        for j, off in enumerate(offs):
            term = _shifted(x_, prev, nxt, off, row, tm) * w_ref[j:j + 1, :]
            acc = term if acc is None else acc + term
        if bias is not None:
            acc = acc + b_ref[...]
        o_ref[0] = acc.astype(out_dtype)
        if qkv_heads is not None:
            H, hd = qkv_heads
            part = pl.program_id(2)
            for h in range(H):
                sl = slice(h * hd, (h + 1) * hd)
                s = _silu(acc[:, sl])
                rs = lax.rsqrt(jnp.sum(s * s, axis=-1, keepdims=True) + EPS)
                scale = jnp.where(part == 0, rs * (float(hd) ** -0.5), jnp.where(part == 1, rs, 1.0))
                a_ref[0, :, sl] = (s * scale).astype(a_ref.dtype)

    pspec, nspec = _halo_specs(T, tm, tc, cb0, "btj")
    in_specs = [pl.BlockSpec((1, tm, tc), lambda b, t, j: (b, t, cb0 + j)), pspec, nspec,
                pl.BlockSpec((w.shape[0], tc), lambda b, t, j: (0, j))]
    args = [x, x, x, w]
    if bias is not None:
        in_specs.append(pl.BlockSpec((1, tc), lambda b, t, j: (0, j)))
        args.append(bias)
    tile = pl.BlockSpec((1, tm, tc), lambda b, t, j: (b, t, j))
    out_specs, out_shape = tile, jax.ShapeDtypeStruct((B, T, C), out_dtype)
    if qkv_heads is not None:
        out_specs, out_shape = (tile, tile), (out_shape, jax.ShapeDtypeStruct((B, T, C), BF16))
    return pl.pallas_call(
        body, name=name, grid=(B, nt, C // tc),
        in_specs=in_specs, out_specs=out_specs, out_shape=out_shape,
        compiler_params=_cparams(("parallel", "parallel", "parallel"), VMEM_LIMIT),
    )(*args)


def _dwconv_wgrad(dy, x, col0, C, offs, nct, tm, tc, name):
    B, T, _ = x.shape
    nt = T // tm
    cb0 = col0 // tc
    K = len(offs)

    def body(dy_ref, x_ref, p_ref, n_ref, acc_ref):
        t = pl.program_id(2)
        prev, nxt = _seg_halos(p_ref, n_ref, t, nct, nt)
        x_ = x_ref[0]
        dy_ = dy_ref[0]
        row = lax.broadcasted_iota(jnp.int32, x_.shape, 0)

        @pl.when(t == 0)
        def _():
            acc_ref[...] = jnp.zeros_like(acc_ref)

        for j, off in enumerate(offs):
            acc_ref[0, j:j + 1, :] += jnp.sum(dy_ * _shifted(x_, prev, nxt, off, row, tm), axis=0, keepdims=True)
        acc_ref[0, K:K + 1, :] += jnp.sum(dy_, axis=0, keepdims=True)

    pspec, nspec = _halo_specs(T, tm, tc, cb0, "bjt")
    return pl.pallas_call(
        body, name=name, grid=(B, C // tc, nt),
        in_specs=[pl.BlockSpec((1, tm, tc), lambda b, j, t: (b, t, j)),
                  pl.BlockSpec((1, tm, tc), lambda b, j, t: (b, t, cb0 + j)), pspec, nspec],
        out_specs=pl.BlockSpec((1, SUBLANES, tc), lambda b, j, t: (b, 0, j)),
        out_shape=jax.ShapeDtypeStruct((B, SUBLANES, C), F32),
        compiler_params=_cparams(("parallel", "parallel", "arbitrary"), VMEM_LIMIT),
    )(dy, x, x, x)


def _ab_act(pab, alog, dtb, nd, tm):
    R = pab.shape[0]

    def body(p_ref, al_ref, dt_ref, o_ref):
        p = p_ref[...]
        lane = lax.broadcasted_iota(jnp.int32, p.shape, 1)
        g = -jnp.exp(al_ref[...]) * _softplus(p + dt_ref[...])
        o_ref[...] = jnp.where(lane < nd, g, jnp.where(lane < 2 * nd, _sigmoid(p), 0.0))

    row = pl.BlockSpec((tm, LANES), lambda i: (i, 0))
    vec = pl.BlockSpec((1, LANES), lambda i: (0, 0))
    return pl.pallas_call(
        body, name="ab_act", grid=(R // tm,), in_specs=[row, vec, vec], out_specs=row,
        out_shape=jax.ShapeDtypeStruct((R, LANES), F32),
        compiler_params=_cparams(("parallel",), VMEM_LIMIT))(pab, alog, dtb)


def _ab_act_bwd(pab, gb, dgb, alog, dtb, nd, tm):
    R = pab.shape[0]

    def body(p_ref, gb_ref, d0_ref, d1_ref, al_ref, dt_ref, o_ref, acc_ref):
        i = pl.program_id(0)
        p = p_ref[...]
        gb_ = gb_ref[...]
        d_ = d0_ref[...] + d1_ref[...]
        lane = lax.broadcasted_iota(jnp.int32, p.shape, 1)
        is_g = lane < nd
        is_b = jnp.logical_and(lane >= nd, lane < 2 * nd)
        da = jnp.where(is_g, d_ * (-jnp.exp(al_ref[...])) * _sigmoid(p + dt_ref[...]), 0.0)
        db = jnp.where(is_b, d_ * gb_ * (1.0 - gb_), 0.0)
        o_ref[...] = (da + db).astype(BF16)

        @pl.when(i == 0)
        def _():
            acc_ref[...] = jnp.zeros_like(acc_ref)

        acc_ref[0:1, :] += jnp.sum(jnp.where(is_g, d_ * gb_, 0.0), axis=0, keepdims=True)
        acc_ref[1:2, :] += jnp.sum(da, axis=0, keepdims=True)

    row = pl.BlockSpec((tm, LANES), lambda i: (i, 0))
    vec = pl.BlockSpec((1, LANES), lambda i: (0, 0))
    return pl.pallas_call(
        body, name="ab_act_bwd", grid=(R // tm,),
        in_specs=[row, row, row, row, vec, vec],
        out_specs=(row, pl.BlockSpec((SUBLANES, LANES), lambda i: (0, 0))),
        out_shape=(jax.ShapeDtypeStruct((R, LANES), BF16), jax.ShapeDtypeStruct((SUBLANES, LANES), F32)),
        compiler_params=_cparams(("arbitrary",), VMEM_LIMIT))(pab, gb, dgb[0], dgb[1], alog, dtb)


def _dn_act_bwd(cv, dqkv, H, hd, tm):
    B, T, C3 = cv.shape
    qscale = float(hd) ** -0.5

    def body(c_ref, d0_ref, d1_ref, o_ref):
        part = pl.program_id(0)
        for h in range(H):
            sl = slice(h * hd, (h + 1) * hd)
            c = c_ref[0, :, sl]
            s = _silu(c)
            dout = d0_ref[0, :, sl] + d1_ref[0, :, sl]
            rs = lax.rsqrt(jnp.sum(s * s, axis=-1, keepdims=True) + EPS)
            sh = s * rs
            dnorm = rs * (dout - sh * jnp.sum(dout * sh, axis=-1, keepdims=True))
            ds = jnp.where(part == 0, dnorm * qscale, jnp.where(part == 1, dnorm, dout))
            o_ref[0, :, sl] = ds * _dsilu(c)

    spec = pl.BlockSpec((1, tm, H * hd), lambda p, b, t: (b, t, p))
    return pl.pallas_call(
        body, name="dn_act_bwd", grid=(3, B, T // tm), in_specs=[spec, spec, spec], out_specs=spec,
        out_shape=jax.ShapeDtypeStruct((B, T, C3), F32),
        compiler_params=_cparams(("parallel",) * 3, VMEM_LIMIT))(cv, dqkv[0], dqkv[1])


def _chunk_of(d, s, ncc, nch):
    if d == 0:
        return s
    return jnp.where(s < ncc, ncc - 1 - s, nch - 1 - (s - ncc))


def _delta_masks(d):
    row = lax.broadcasted_iota(jnp.int32, (CHUNK, CHUNK), 0)
    col = lax.broadcasted_iota(jnp.int32, (CHUNK, CHUNK), 1)
    sign = 1 - 2 * d
    diff = (row - col) * sign
    return diff >= 0, diff > 0, diff <= 0


def _each(f, *lists):
    return [f(*a) for a in zip(*lists)]


def _unit_tri_inverse(As):
    C = As[0].shape[0]
    row = lax.broadcasted_iota(jnp.int32, (C, C), 0)
    col = lax.broadcasted_iota(jnp.int32, (C, C), 1)
    eye = jnp.where(row == col, 1.0, 0.0).astype(F32)
    same = lambda shift: jnp.right_shift(row, shift) == jnp.right_shift(col, shift)
    in8 = same(3)
    xs = [-jnp.where(in8, a, 0.0) for a in As]
    ts = [eye + x for x in xs]
    ps = _each(lambda x: _dot(x, x), xs)
    tp = _each(lambda t, p: _dot(_rows(t, p), p), ts, ps)
    ts = _each(lambda t, m: t + m[:C], ts, tp)
    ts = _each(lambda t, m: t + _dot(t, m[C:]), ts, tp)
    shift = 3
    while (1 << shift) < C:
        off = jnp.logical_and(same(shift + 1), jnp.right_shift(row, shift) != jnp.right_shift(col, shift))
        los = [jnp.where(off, a, 0.0) for a in As]
        ts = _each(lambda t, lo: t - _dot(t, _dot(lo, t)), ts, los)
        shift += 1
    def residual(a, t):
        (ah, al), (th, tl) = _split2(eye + a), _split2(t)
        m = _dot(_rows(ah, al), th)
        return eye - (m[:C] + (m[C:] + _dot(ah, tl)))

    rs = _each(residual, As, ts)
    return _each(lambda t, r: t + _dot(t, r), ts, rs)


def _rows(*xs):
    return jnp.concatenate(xs, axis=0)


def _cols(*xs):
    return jnp.concatenate(xs, axis=1)


def _delta_chunk_fwd(q, k, v, g_i, g_j, beta_i, gl, S, incl, strict, Tm=None):
    D_ = _each(lambda gi, gj, m: jnp.where(m, jnp.exp(jnp.where(m, gi - gj, 0.0)), 0.0), g_i, g_j, incl)
    C, dk = k[0].shape
    kb = _each(lambda k_, b: k_ * b, k, beta_i)
    kq = _each(lambda kb_, q_, k_: _dot(_rows(kb_, q_), k_, "nt"), kb, q, k)
    A = _each(lambda m_, d, m: jnp.where(m, m_[:C] * d, 0.0), kq, D_, strict)
    P = _each(lambda m_, d, m: jnp.where(m, m_[C:] * d, 0.0), kq, D_, incl)
    if Tm is None:
        Tm = _unit_tri_inverse(A)
    gam = _each(jnp.exp, g_i)
    wu = _each(lambda t, kb_, g, v_, b: _dot(t, _cols(kb_ * g, v_ * b)), Tm, kb, gam, v, beta_i)
    w = [m_[:, :dk] for m_ in wu]
    u = [m_[:, dk:] for m_ in wu]
    qg = _each(lambda q_, g: q_ * g, q, gam)
    ws = _each(lambda w_, qg_, s: _dot(_rows(w_, qg_), s), w, qg, S)
    vn = _each(lambda u_, m_: u_ - m_[:C], u, ws)
    o = _each(lambda m_, p, vn_: m_[C:] + _dot(p, vn_), ws, P, vn)
    e_i = _each(lambda gl_, gi: jnp.exp(gl_ - gi), gl, g_i)
    kd = _each(lambda k_, e: k_ * e, k, e_i)
    Gam = _each(jnp.exp, gl)
    S_new = _each(lambda s, g, kd_, vn_: s * g + _dot(kd_, vn_, "tn"), S, Gam, kd, vn)
    return dict(D=D_, kb=kb, A=A, Tm=Tm, gam=gam, w=w, u=u, vn=vn, P=P, qg=qg, o=o, e=e_i, kd=kd, Gam=Gam, S_new=S_new)


def _delta_gb(gb_ref, d, incl, incl_t, H):
    gb = gb_ref[0]
    g = gb[:, d * H:(d + 1) * H]
    beta = gb[:, (2 + d) * H:(3 + d) * H]
    gc_col = _dot_mask(incl.astype(BF16), g)
    gc_row = _dot_mask(incl_t.astype(BF16), g, "xtn")
    gl = jnp.sum(g, axis=0, keepdims=True)
    return beta, gc_col, gc_row, gl


def _delta_fwd(qkvn, gb, H, hd, ncc):
    B, T, _ = qkvn.shape
    nch = T // CHUNK
    HD = H * hd
    pairs = [(d, h) for d in range(2) for h in range(H)]

    def body(q0, k0, v0, gb0, q1, k1, v1, gb1, o0, o1, sin0, sin1, tm0, tm1, S_ref):
        s = pl.program_id(1)
        q_refs, k_refs, v_refs, gb_refs = (q0, q1), (k0, k1), (v0, v1), (gb0, gb1)
        o_refs, sin_refs, tm_refs = (o0, o1), (sin0, sin1), (tm0, tm1)

        @pl.when(s == 0)
        def _():
            S_ref[...] = jnp.zeros_like(S_ref)

        masks = [_delta_masks(d) for d in range(2)]
        gbs = [_delta_gb(gb_refs[d], d, masks[d][0], masks[d][2], H) for d in range(2)]
        head = lambda ref, h: ref[0, :, h * hd:(h + 1) * hd]
        S = [S_ref[d, h] for d, h in pairs]
        r = _delta_chunk_fwd([head(q_refs[d], h) for d, h in pairs], [head(k_refs[d], h) for d, h in pairs],
                             [head(v_refs[d], h) for d, h in pairs],
                             [gbs[d][1][:, h:h + 1] for d, h in pairs], [gbs[d][2][h:h + 1, :] for d, h in pairs],
                             [gbs[d][0][:, h:h + 1] for d, h in pairs], [gbs[d][3][:, h:h + 1] for d, h in pairs],
                             S, [masks[d][0] for d, h in pairs], [masks[d][1] for d, h in pairs])
        for i, (d, h) in enumerate(pairs):
            sin_refs[d][0, 0, h] = S[i]
            tm_refs[d][0, 0, h] = r["Tm"][i]
            S_ref[d, h] = r["S_new"][i]
            o_refs[d][0, :, h * hd:(h + 1) * hd] = r["o"][i]

    def dir_specs(d):
        cidx = lambda b, s: _chunk_of(d, s, ncc, nch)
        ins = [pl.BlockSpec((1, CHUNK, HD), lambda b, s, p=p: (b, cidx(b, s), p)) for p in range(3)]
        ins.append(pl.BlockSpec((1, CHUNK, LANES), lambda b, s: (b, cidx(b, s), 0)))
        return (ins, pl.BlockSpec((1, CHUNK, HD), lambda b, s: (b, cidx(b, s), 0)),
                pl.BlockSpec((1, 1, H, hd, hd), lambda b, s: (b, cidx(b, s), 0, 0, 0)),
                pl.BlockSpec((1, 1, H, CHUNK, CHUNK), lambda b, s: (b, cidx(b, s), 0, 0, 0)))

    (in0, o_s0, sin_s0, tm_s0), (in1, o_s1, sin_s1, tm_s1) = dir_specs(0), dir_specs(1)
    o_shape = jax.ShapeDtypeStruct((B, T, HD), F32)
    sin_shape = jax.ShapeDtypeStruct((B, nch, H, hd, hd), F32)
    tm_shape = jax.ShapeDtypeStruct((B, nch, H, CHUNK, CHUNK), F32)
    return pl.pallas_call(
        body, name="delta_fwd", grid=(B, nch),
        in_specs=in0 + in1, out_specs=(o_s0, o_s1, sin_s0, sin_s1, tm_s0, tm_s1),
        out_shape=(o_shape, o_shape, sin_shape, sin_shape, tm_shape, tm_shape),
        scratch_shapes=[pltpu.VMEM((2, H, hd, hd), F32)],
        compiler_params=_cparams(("parallel", "arbitrary"), VMEM_LIMIT),
    )(qkvn, qkvn, qkvn, gb, qkvn, qkvn, qkvn, gb)


def _delta_bwd(qkvn, gb, sin, tms, do, H, hd, ncc):
    B, T, _ = qkvn.shape
    nch = T // CHUNK
    HD = H * hd
    pairs = [(d, h) for d in range(2) for h in range(H)]

    def body(q0, k0, v0, gb0, sin0, tm0, do0, q1, k1, v1, gb1, sin1, tm1, do1, dqkv0, dqkv1, dgb0, dgb1, dS_ref):
        s = pl.program_id(1)
        tm_refs = (tm0, tm1)
        q_refs, k_refs, v_refs, gb_refs = (q0, q1), (k0, k1), (v0, v1), (gb0, gb1)
        sin_refs, do_refs, dqkv_refs, dgb_refs = (sin0, sin1), (do0, do1), (dqkv0, dqkv1), (dgb0, dgb1)

        @pl.when(s == 0)
        def _():
            dS_ref[...] = jnp.zeros_like(dS_ref)

        masks = [_delta_masks(d) for d in range(2)]
        gbs = [_delta_gb(gb_refs[d], d, masks[d][0], masks[d][2], H) for d in range(2)]
        incl = [masks[d][0] for d, h in pairs]
        strict = [masks[d][1] for d, h in pairs]
        lane = lax.broadcasted_iota(jnp.int32, (1, LANES), 1)
        ones = jnp.ones((3 * CHUNK, LANES), BF16)
        head = lambda ref, h: ref[0, :, h * hd:(h + 1) * hd]
        q = [head(q_refs[d], h) for d, h in pairs]
        k = [head(k_refs[d], h) for d, h in pairs]
        v = [head(v_refs[d], h) for d, h in pairs]
        do_ = [head(do_refs[d], h) for d, h in pairs]
        beta_i = [gbs[d][0][:, h:h + 1] for d, h in pairs]
        S = [sin_refs[d][0, 0, h] for d, h in pairs]
        dSn = [dS_ref[d, h] for d, h in pairs]
        f = _delta_chunk_fwd(q, k, v, [gbs[d][1][:, h:h + 1] for d, h in pairs], [gbs[d][2][h:h + 1, :] for d, h in pairs],
                             beta_i, [gbs[d][3][:, h:h + 1] for d, h in pairs], S, incl, strict,
                             Tm=[tm_refs[d][0, 0, h] for d, h in pairs])
        rsum = lambda x: jnp.sum(x, axis=1, keepdims=True)
        dvn = _each(lambda p, d_, kd, ds: _dot(p, d_, "tn") + _dot(kd, ds), f["P"], do_, f["kd"], dSn)
        dP = _each(lambda d_, vn, m: jnp.where(m, _dot(d_, vn, "nt"), 0.0), do_, f["vn"], incl)
        dqg = _each(lambda d_, s: _dot(d_, s, "nt"), do_, S)
        dS_in = _each(lambda qg, d_, g, ds, w, dv_: _dot(qg, d_, "tn") + g * ds - _dot(w, dv_, "tn"),
                      f["qg"], do_, f["Gam"], dSn, f["w"], dvn)
        dGam = _each(lambda s, ds: jnp.sum(rsum(s * ds), axis=0, keepdims=True), S, dSn)
        dkd = _each(lambda vn, ds: _dot(vn, ds, "nt"), f["vn"], dSn)
        de = _each(lambda dkd_, k_, e: rsum(dkd_ * k_) * e, dkd, k, f["e"])
        dw = _each(lambda dv_, s: -_dot(dv_, s, "nt"), dvn, S)
        dXw = _each(lambda t, dw_: _dot(t, dw_, "tn"), f["Tm"], dw)
        dXu = _each(lambda t, dv_: _dot(t, dv_, "tn"), f["Tm"], dvn)
        dA = _each(lambda xw, w, xu, u, m: -jnp.where(m, _dot(xw, w, "nt") + _dot(xu, u, "nt"), 0.0),
                   dXw, f["w"], dXu, f["u"], strict)
        dM = _each(lambda a, d_: a * d_, dA, f["D"])
        dN = _each(lambda p, d_: p * d_, dP, f["D"])
        dkb = _each(lambda m, k_, xw, g: _dot(m, k_) + xw * g, dM, k, dXw, f["gam"])
        dq = _each(lambda dqg_, g, n, k_: dqg_ * g + _dot(n, k_), dqg, f["gam"], dN, k)
        dk = _each(lambda dkd_, e, m, kb, n, q_, dkb_, b: dkd_ * e + _dot(m, kb, "tn") + _dot(n, q_, "tn") + dkb_ * b,
                   dkd, f["e"], dM, f["kb"], dN, q, dkb, beta_i)
        dv = _each(lambda xu, b: xu * b, dXu, beta_i)
        dbeta = _each(lambda dkb_, k_, xu, v_: rsum(dkb_ * k_) + rsum(xu * v_), dkb, k, dXu, v)
        E = _each(lambda a, A_, p, P_: a * A_ + p * P_, dA, f["A"], dP, f["P"])

        def colsum(e):
            return _dot(_rows(*_split3(e)), ones, "tn")[:, 0:1]

        dg = _each(lambda e, dqg_, qg, xw, kb, g, de_: rsum(e) - colsum(e) + rsum(dqg_ * qg) + rsum(xw * kb) * g - de_,
                   E, dqg, f["qg"], dXw, f["kb"], f["gam"], de)
        dgl_h = _each(lambda de_, dg_, g: jnp.sum(de_, axis=0, keepdims=True) + dg_ * g, de, dGam, f["Gam"])
        for d in range(2):
            dgc = jnp.zeros((CHUNK, LANES), F32)
            dgl = jnp.zeros((1, LANES), F32)
            for h in range(H):
                i = d * H + h
                g_lane, b_lane = d * H + h, (2 + d) * H + h
                dgc = dgc + dg[i] * (lane == g_lane).astype(F32) + dbeta[i] * (lane == b_lane).astype(F32)
                dgl = dgl + dgl_h[i] * (lane == g_lane).astype(F32)
                dS_ref[d, h] = dS_in[i]
                dqkv_refs[d][0, :, h * hd:(h + 1) * hd] = dq[i]
                dqkv_refs[d][0, :, HD + h * hd:HD + (h + 1) * hd] = dk[i]
                dqkv_refs[d][0, :, 2 * HD + h * hd:2 * HD + (h + 1) * hd] = dv[i]
            draw = _dot_mask(masks[d][0].astype(BF16), dgc, "tn") + dgl
            dgb_refs[d][0] = jnp.where(lane < 2 * H, draw, dgc)

    def dir_specs(d):
        cidx = lambda b, s: _chunk_of(d, nch - 1 - s, ncc, nch)
        ins = [pl.BlockSpec((1, CHUNK, HD), lambda b, s, p=p: (b, cidx(b, s), p)) for p in range(3)]
        ins += [pl.BlockSpec((1, CHUNK, LANES), lambda b, s: (b, cidx(b, s), 0)),
                pl.BlockSpec((1, 1, H, hd, hd), lambda b, s: (b, cidx(b, s), 0, 0, 0)),
                pl.BlockSpec((1, 1, H, CHUNK, CHUNK), lambda b, s: (b, cidx(b, s), 0, 0, 0)),
                pl.BlockSpec((1, CHUNK, HD), lambda b, s: (b, cidx(b, s), 0))]
        return (ins, pl.BlockSpec((1, CHUNK, 3 * HD), lambda b, s: (b, cidx(b, s), 0)),
                pl.BlockSpec((1, CHUNK, LANES), lambda b, s: (b, cidx(b, s), 0)))

    (in0, dq_s0, dg_s0), (in1, dq_s1, dg_s1) = dir_specs(0), dir_specs(1)
    dq_shape = jax.ShapeDtypeStruct((B, T, 3 * HD), F32)
    dg_shape = jax.ShapeDtypeStruct((B, T, LANES), F32)
    return pl.pallas_call(
        body, name="delta_bwd", grid=(B, nch),
        in_specs=in0 + in1, out_specs=(dq_s0, dq_s1, dg_s0, dg_s1),
        out_shape=(dq_shape, dq_shape, dg_shape, dg_shape),
        scratch_shapes=[pltpu.VMEM((2, H, hd, hd), F32)],
        compiler_params=_cparams(("parallel", "arbitrary"), VMEM_LIMIT),
    )(qkvn, qkvn, qkvn, gb, sin[0], tms[0], do, qkvn, qkvn, qkvn, gb, sin[1], tms[1], do)


def _dn_out(o2, pz, zcb0, onorm, H, hd, nct, tm):
    B, T, HD = o2[0].shape
    N = T - nct * tm

    def body(o0_ref, o1_ref, z_ref, g_ref, y_ref):
        for h in range(H):
            sl = slice(h * hd, (h + 1) * hd)
            o = o0_ref[0, :, sl] + o1_ref[0, :, sl]
            r = lax.rsqrt(jnp.mean(o * o, axis=-1, keepdims=True) + EPS)
            y_ref[0, :, sl] = (o * r * g_ref[...] * _silu(z_ref[0, :, sl].astype(F32))).astype(BF16)

    return pl.pallas_call(
        body, name="dn_out", grid=(B, N // tm),
        in_specs=[pl.BlockSpec((1, tm, HD), lambda b, t: (b, t + nct, 0)),
                  pl.BlockSpec((1, tm, HD), lambda b, t: (b, t + nct, 0)),
                  pl.BlockSpec((1, tm, HD), lambda b, t: (b, t + nct, zcb0)),
                  pl.BlockSpec((1, hd), lambda b, t: (0, 0))],
        out_specs=pl.BlockSpec((1, tm, HD), lambda b, t: (b, t, 0)),
        out_shape=jax.ShapeDtypeStruct((B, N, HD), BF16),
        compiler_params=_cparams(("parallel",) * 2, VMEM_LIMIT))(o2[0], o2[1], pz, onorm)


def _dn_out_bwd(o2, pz, zcb0, onorm, dy, H, hd, nct, tm):
    B, T, HD = o2[0].shape
    nt = T // tm

    def body(o0_ref, o1_ref, z_ref, g_ref, dy_ref, do_ref, dz_ref, acc_ref):
        t = pl.program_id(1)

        @pl.when(t == 0)
        def _():
            acc_ref[...] = jnp.zeros_like(acc_ref)

        @pl.when(t < nct)
        def _():
            do_ref[0] = jnp.zeros_like(do_ref[0])
            dz_ref[0] = jnp.zeros_like(dz_ref[0])

        @pl.when(t >= nct)
        def _():
            g = g_ref[...]
            for h in range(H):
                sl = slice(h * hd, (h + 1) * hd)
                o = o0_ref[0, :, sl] + o1_ref[0, :, sl]
                z = z_ref[0, :, sl].astype(F32)
                dy_ = dy_ref[0, :, sl]
                r = lax.rsqrt(jnp.mean(o * o, axis=-1, keepdims=True) + EPS)
                oh = o * r
                dz_ref[0, :, sl] = (dy_ * oh * g * _dsilu(z)).astype(BF16)
                dn = dy_ * _silu(z)
                acc_ref[0, 0:1, :] += jnp.sum(dn * oh, axis=0, keepdims=True)
                doh = dn * g
                do_ref[0, :, sl] = r * (doh - oh * jnp.mean(doh * oh, axis=-1, keepdims=True))

    lat = lambda b, t: (b, jnp.maximum(t - nct, 0), 0)
    row = pl.BlockSpec((1, tm, HD), lambda b, t: (b, t, 0))
    return pl.pallas_call(
        body, name="dn_out_bwd", grid=(B, nt),
        in_specs=[row, row,
                  pl.BlockSpec((1, tm, HD), lambda b, t: (b, t, zcb0)),
                  pl.BlockSpec((1, hd), lambda b, t: (0, 0)),
                  pl.BlockSpec((1, tm, HD), lat)],
        out_specs=(row, row, pl.BlockSpec((1, SUBLANES, hd), lambda b, t: (b, 0, 0))),
        out_shape=(jax.ShapeDtypeStruct((B, T, HD), F32), jax.ShapeDtypeStruct((B, T, HD), BF16),
                   jax.ShapeDtypeStruct((B, SUBLANES, hd), F32)),
        compiler_params=_cparams(("parallel", "arbitrary"), VMEM_LIMIT),
    )(o2[0], o2[1], pz, onorm, dy)


def _lru_gate_math(x, wr, wi, br, bi, lam):
    r = _sigmoid(_dot(x, wr) + br)
    i = _sigmoid(_dot(x, wi) + bi)
    sp = _softplus(-lam)
    la = -LRU_C * r * sp
    a = jnp.exp(la)
    s = jnp.sqrt(-jnp.tanh(la) * (a * a + 1.0))
    return r, i, sp, a, s


def _lru_gates(xc, wbd, bias4, lam, tm, bw):
    B, T, W = xc.shape

    def body(x_ref, w_ref, b_ref, l_ref, a_ref, i_ref):
        x = x_ref[0]
        for d in range(2):
            _, i, _, a, s = _lru_gate_math(x, w_ref[2 * d, 0], w_ref[2 * d + 1, 0],
                                           b_ref[2 * d:2 * d + 1, :], b_ref[2 * d + 1:2 * d + 2, :], l_ref[d:d + 1, :])
            a_ref[d, 0] = a
            i_ref[d, 0] = s * (i * x)

    out = pl.BlockSpec((2, 1, tm, bw), lambda b, t, j: (0, b, t, j))
    return pl.pallas_call(
        body, name="lru_gates", grid=(B, T // tm, W // bw),
        in_specs=[pl.BlockSpec((1, tm, bw), lambda b, t, j: (b, t, j)),
                  pl.BlockSpec((4, 1, bw, bw), lambda b, t, j: (0, j, 0, 0)),
                  pl.BlockSpec((4, bw), lambda b, t, j: (0, j)),
                  pl.BlockSpec((2, bw), lambda b, t, j: (0, j))],
        out_specs=(out, out),
        out_shape=(jax.ShapeDtypeStruct((2, B, T, W), F32), jax.ShapeDtypeStruct((2, B, T, W), F32)),
        compiler_params=_cparams(("parallel",) * 3, VMEM_LIMIT))(xc, wbd, bias4, lam)


def _lru_gates_bwd(xc, wbd, bias4, lam, dinp, da, tm, bw):
    B, T, W = xc.shape
    nt = T // tm

    def body(x_ref, w_ref, b_ref, l_ref, di0_ref, di1_ref, da0_ref, da1_ref, dx_ref, dw_ref, acc_ref):
        di_refs, da_refs = (di0_ref, di1_ref), (da0_ref, da1_ref)
        b_ = pl.program_id(1)
        t = pl.program_id(2)

        @pl.when(jnp.logical_and(b_ == 0, t == 0))
        def _():
            dw_ref[...] = jnp.zeros_like(dw_ref)
            acc_ref[...] = jnp.zeros_like(acc_ref)

        x = x_ref[0]
        dx = jnp.zeros_like(x)
        for d in range(2):
            wr, wi = w_ref[2 * d, 0], w_ref[2 * d + 1, 0]
            lam_ = l_ref[d:d + 1, :]
            r, i, sp, a, s = _lru_gate_math(x, wr, wi, b_ref[2 * d:2 * d + 1, :], b_ref[2 * d + 1:2 * d + 2, :], lam_)
            dinp_ = di_refs[d][0]
            dix = dinp_ * s
            ds = dinp_ * i * x
            dla = da_refs[d][0] * a - ds * (a * a) / s
            dpr = dla * (-LRU_C * sp) * r * (1.0 - r)
            dpi = dix * x * i * (1.0 - i)
            dx = dx + dix * i + _dot(dpr, wr, "nt") + _dot(dpi, wi, "nt")
            dw_ref[2 * d, 0] += _dot(x, dpr, "tn")
            dw_ref[2 * d + 1, 0] += _dot(x, dpi, "tn")
            acc_ref[2 * d:2 * d + 1, :] += jnp.sum(dpr, axis=0, keepdims=True)
            acc_ref[2 * d + 1:2 * d + 2, :] += jnp.sum(dpi, axis=0, keepdims=True)
            acc_ref[4 + d:5 + d, :] += jnp.sum(dla * (-LRU_C * r), axis=0, keepdims=True) * (-_sigmoid(-lam_))
        dx_ref[0] = dx

    tile = pl.BlockSpec((1, tm, bw), lambda j, b, t: (b, t, j))
    return pl.pallas_call(
        body, name="lru_gates_bwd", grid=(W // bw, B, nt),
        in_specs=[pl.BlockSpec((1, tm, bw), lambda j, b, t: (b, t, j)),
                  pl.BlockSpec((4, 1, bw, bw), lambda j, b, t: (0, j, 0, 0)),
                  pl.BlockSpec((4, bw), lambda j, b, t: (0, j)),
                  pl.BlockSpec((2, bw), lambda j, b, t: (0, j)), tile, tile, tile, tile],
        out_specs=(pl.BlockSpec((1, tm, bw), lambda j, b, t: (b, t, j)),
                   pl.BlockSpec((4, 1, bw, bw), lambda j, b, t: (0, j, 0, 0)),
                   pl.BlockSpec((SUBLANES, bw), lambda j, b, t: (0, j))),
        out_shape=(jax.ShapeDtypeStruct((B, T, W), F32), jax.ShapeDtypeStruct(wbd.shape, F32),
                   jax.ShapeDtypeStruct((SUBLANES, W), F32)),
        compiler_params=_cparams(("parallel", "arbitrary", "arbitrary"), VMEM_LIMIT),
    )(xc, wbd, bias4, lam, dinp[0], dinp[1], da[0], da[1])


def _tile_scan(a, x, rev, tm):
    row = lax.broadcasted_iota(jnp.int32, a.shape, 0)
    s = 1
    while s < tm:
        if rev:
            a_sh, x_sh, ok = pltpu.roll(a, tm - s, 0), pltpu.roll(x, tm - s, 0), row < tm - s
        else:
            a_sh, x_sh, ok = pltpu.roll(a, s, 0), pltpu.roll(x, s, 0), row >= s
        x = jnp.where(ok, x + a * x_sh, x)
        a = jnp.where(ok, a * a_sh, a)
        s *= 2
    return a, x


def _scan_tile_of(s, rev, nct, nt):
    if not rev:
        return s
    return jnp.where(s < nct, nct - 1 - s, nt - 1 - (s - nct))


def _lru_scan(a2, x2, d, nct, tm, tc):
    _, B, T, W = a2.shape
    nt = T // tm
    rev = d == 1
    last = 0 if rev else tm - 1

    def body(a_ref, x_ref, h_ref, carry):
        s = pl.program_id(2)

        @pl.when(s == 0)
        def _():
            carry[...] = jnp.zeros_like(carry)

        A, X = _tile_scan(a_ref[0, 0], x_ref[0, 0], rev, tm)
        h = X + A * carry[0:1, :]
        h_ref[0] = h
        carry[...] = jnp.broadcast_to(h[last:last + 1, :], carry.shape)

    tidx = lambda s: _scan_tile_of(s, rev, nct, nt)
    spec = pl.BlockSpec((1, 1, tm, tc), lambda b, j, s: (d, b, tidx(s), j))
    return pl.pallas_call(
        body, name=f"lru_scan{d}", grid=(B, W // tc, nt),
        in_specs=[spec, spec], out_specs=pl.BlockSpec((1, tm, tc), lambda b, j, s: (b, tidx(s), j)),
        out_shape=jax.ShapeDtypeStruct((B, T, W), F32),
        scratch_shapes=[pltpu.VMEM((SUBLANES, tc), F32)],
        compiler_params=_cparams(("parallel", "parallel", "arbitrary"), VMEM_LIMIT),
    )(a2, x2)


def _lru_scan_bwd(a2, h, dh, d, nct, tm, tc):
    _, B, T, W = a2.shape
    nt = T // tm
    rev = d == 1
    first = tm - 1 if rev else 0
    r8, n8 = tm // SUBLANES, T // SUBLANES

    def body(a_ref, h_ref, hp_ref, dh_ref, lam_ref, da_ref, ca, cl):
        s = pl.program_id(2)

        @pl.when(s == 0)
        def _():
            ca[...] = jnp.zeros_like(ca)
            cl[...] = jnp.zeros_like(cl)

        a = a_ref[0, 0]
        row = lax.broadcasted_iota(jnp.int32, a.shape, 0)
        if rev:
            c = jnp.where(row == 0, ca[0:1, :], pltpu.roll(a, 1, 0))
        else:
            c = jnp.where(row == tm - 1, ca[0:1, :], pltpu.roll(a, tm - 1, 0))
        C, L = _tile_scan(c, dh_ref[0], not rev, tm)
        lam = L + C * cl[0:1, :]
        lam_ref[0] = lam
        hp = jnp.where(s == nt - 1, 0.0, hp_ref[0])
        if rev:
            hprev = jnp.where(row == tm - 1, hp[0:1, :], pltpu.roll(h_ref[0], tm - 1, 0))
        else:
            hprev = jnp.where(row == 0, hp[SUBLANES - 1:SUBLANES, :], pltpu.roll(h_ref[0], 1, 0))
        da_ref[0] = lam * hprev
        ca[...] = jnp.broadcast_to(a[first:first + 1, :], ca.shape)
        cl[...] = jnp.broadcast_to(lam[first:first + 1, :], cl.shape)

    tidx = lambda s: _scan_tile_of(nt - 1 - s, rev, nct, nt)

    def hp_idx(b, j, s):
        tt = tidx(s)
        if rev:
            blk = jnp.where(tt == nt - 1, 0, jnp.minimum((tt + 1) * r8, n8 - 1))
        else:
            blk = jnp.maximum(tt * r8 - 1, 0)
        return (b, blk, j)

    tile = pl.BlockSpec((1, tm, tc), lambda b, j, s: (b, tidx(s), j))
    return pl.pallas_call(
        body, name=f"lru_scan_bwd{d}", grid=(B, W // tc, nt),
        in_specs=[pl.BlockSpec((1, 1, tm, tc), lambda b, j, s: (d, b, tidx(s), j)), tile,
                  pl.BlockSpec((1, SUBLANES, tc), hp_idx), tile],
        out_specs=(tile, tile),
        out_shape=(jax.ShapeDtypeStruct((B, T, W), F32), jax.ShapeDtypeStruct((B, T, W), F32)),
        scratch_shapes=[pltpu.VMEM((SUBLANES, tc), F32), pltpu.VMEM((SUBLANES, tc), F32)],
        compiler_params=_cparams(("parallel", "parallel", "arbitrary"), VMEM_LIMIT),
    )(a2, h, h, dh)


def _lru_out(h0, h1, pyl, ycb0, nct, tm, tc):
    B, T, W = h0.shape
    N = T - nct * tm

    def body(h0_ref, h1_ref, y_ref, o_ref):
        o_ref[0] = ((h0_ref[0] + h1_ref[0]) * _gelu(y_ref[0].astype(F32))).astype(BF16)

    hs = pl.BlockSpec((1, tm, tc), lambda b, t, j: (b, t + nct, j))
    return pl.pallas_call(
        body, name="lru_out", grid=(B, N // tm, W // tc),
        in_specs=[hs, hs, pl.BlockSpec((1, tm, tc), lambda b, t, j: (b, t + nct, ycb0 + j))],
        out_specs=pl.BlockSpec((1, tm, tc), lambda b, t, j: (b, t, j)),
        out_shape=jax.ShapeDtypeStruct((B, N, W), BF16),
        compiler_params=_cparams(("parallel",) * 3, VMEM_LIMIT))(h0, h1, pyl)


def _lru_out_bwd(h0, h1, pyl, ycb0, dy, nct, tm, tc):
    B, T, W = h0.shape

    def body(h0_ref, h1_ref, y_ref, dy_ref, dh_ref, dyl_ref):
        t = pl.program_id(1)

        @pl.when(t < nct)
        def _():
            dh_ref[0] = jnp.zeros_like(dh_ref[0])
            dyl_ref[0] = jnp.zeros_like(dyl_ref[0])

        @pl.when(t >= nct)
        def _():
            y = y_ref[0].astype(F32)
            dy_ = dy_ref[0]
            dh_ref[0] = dy_ * _gelu(y)
            dyl_ref[0] = (dy_ * (h0_ref[0] + h1_ref[0]) * _dgelu(y)).astype(BF16)

    hs = pl.BlockSpec((1, tm, tc), lambda b, t, j: (b, t, j))
    return pl.pallas_call(
        body, name="lru_out_bwd", grid=(B, T // tm, W // tc),
        in_specs=[hs, hs, pl.BlockSpec((1, tm, tc), lambda b, t, j: (b, t, ycb0 + j)),
                  pl.BlockSpec((1, tm, tc), lambda b, t, j: (b, jnp.maximum(t - nct, 0), j))],
        out_specs=(hs, hs),
        out_shape=(jax.ShapeDtypeStruct((B, T, W), F32), jax.ShapeDtypeStruct((B, T, W), BF16)),
        compiler_params=_cparams(("parallel",) * 3, VMEM_LIMIT))(h0, h1, pyl, dy)


def _merge(bd, bl, pmg, bm, nct, tm):
    B, N, D = bd.shape

    def body(bd_ref, bl_ref, m0_ref, m1_ref, b_ref, o_ref):
        g0 = _sigmoid(m0_ref[0].astype(F32) + b_ref[:, 0:D])
        g1 = _sigmoid(m1_ref[0].astype(F32) + b_ref[:, D:2 * D])
        o_ref[0] = (g0 * bd_ref[0].astype(F32) + g1 * bl_ref[0].astype(F32)).astype(BF16)

    row = pl.BlockSpec((1, tm, D), lambda b, t: (b, t, 0))
    return pl.pallas_call(
        body, name="merge", grid=(B, N // tm),
        in_specs=[row, row, pl.BlockSpec((1, tm, D), lambda b, t: (b, t + nct, 0)),
                  pl.BlockSpec((1, tm, D), lambda b, t: (b, t + nct, 1)),
                  pl.BlockSpec((1, 2 * D), lambda b, t: (0, 0))],
        out_specs=row, out_shape=jax.ShapeDtypeStruct((B, N, D), BF16),
        compiler_params=_cparams(("parallel", "parallel"), VMEM_LIMIT))(bd, bl, pmg, pmg, bm)


def _merge_bwd(dmi, bd, bl, pmg, bm, nct, tm):
    B, N, D = bd.shape
    T = pmg.shape[1]

    def body(dm_ref, bd_ref, bl_ref, m0_ref, m1_ref, b_ref, dbd_ref, dbl_ref, dmg_ref, acc_ref):
        t = pl.program_id(1)

        @pl.when(t == 0)
        def _():
            acc_ref[...] = jnp.zeros_like(acc_ref)

        @pl.when(t < nct)
        def _():
            dmg_ref[0] = jnp.zeros_like(dmg_ref[0])

        @pl.when(t >= nct)
        def _():
            dm = dm_ref[0]
            g0 = _sigmoid(m0_ref[0].astype(F32) + b_ref[:, 0:D])
            g1 = _sigmoid(m1_ref[0].astype(F32) + b_ref[:, D:2 * D])
            dbd_ref[0] = (dm * g0).astype(BF16)
            dbl_ref[0] = (dm * g1).astype(BF16)
            d0 = dm * bd_ref[0].astype(F32) * g0 * (1.0 - g0)
            d1 = dm * bl_ref[0].astype(F32) * g1 * (1.0 - g1)
            dmg_ref[0, :, 0:D] = d0.astype(BF16)
            dmg_ref[0, :, D:2 * D] = d1.astype(BF16)
            acc_ref[0, 0:1, 0:D] += jnp.sum(d0, axis=0, keepdims=True)
            acc_ref[0, 0:1, D:2 * D] += jnp.sum(d1, axis=0, keepdims=True)

    lat = pl.BlockSpec((1, tm, D), lambda b, t: (b, jnp.maximum(t - nct, 0), 0))
    return pl.pallas_call(
        body, name="merge_bwd", grid=(B, T // tm),
        in_specs=[lat, lat, lat, pl.BlockSpec((1, tm, D), lambda b, t: (b, t, 0)),
                  pl.BlockSpec((1, tm, D), lambda b, t: (b, t, 1)),
                  pl.BlockSpec((1, 2 * D), lambda b, t: (0, 0))],
        out_specs=(lat, lat, pl.BlockSpec((1, tm, 2 * D), lambda b, t: (b, t, 0)),
                   pl.BlockSpec((1, SUBLANES, 2 * D), lambda b, t: (b, 0, 0))),
        out_shape=(jax.ShapeDtypeStruct((B, N, D), BF16), jax.ShapeDtypeStruct((B, N, D), BF16),
                   jax.ShapeDtypeStruct((B, T, 2 * D), BF16), jax.ShapeDtypeStruct((B, SUBLANES, 2 * D), F32)),
        compiler_params=_cparams(("parallel", "arbitrary"), VMEM_LIMIT))(dmi, bd, bl, pmg, pmg, bm)


def _grid_taps():
    return [((di + 1) * 3 + (dj + 1), di, dj) for di in (-1, 0, 1) for dj in (-1, 0, 1)]


def _grid_views(x, n):
    pos = lax.broadcasted_iota(jnp.int32, x.shape, 0)
    gc = jnp.bitwise_and(pos, GRID_W - 1)
    cols = {0: x,
            -1: jnp.where(gc >= 1, pltpu.roll(x, 1, 0), 0.0),
            1: jnp.where(gc <= GRID_W - 2, pltpu.roll(x, n - 1, 0), 0.0)}
    views = {}
    edge = jnp.zeros((GRID_W, x.shape[1]), x.dtype)
    for dj, xc in cols.items():
        views[(0, dj)] = xc
        views[(-1, dj)] = jnp.concatenate([edge, xc[:n - GRID_W]], axis=0)
        views[(1, dj)] = jnp.concatenate([xc[GRID_W:], edge], axis=0)
    return views


def _ffn_act(Fg, Fv, w9, b, tc):
    B, N, Cf = Fg.shape
    ncb = Cf // tc

    def body(g_ref, v_ref, w_ref, b_ref, o_ref, gel_ref, dgel_ref):
        xv = _grid_views(g_ref[0], N)
        conv = b_ref[...]
        for k, di, dj in _grid_taps():
            conv = conv + xv[(di, dj)] * w_ref[k:k + 1, :]
        th = jnp.tanh(GELU_C * (conv + 0.044715 * conv * conv * conv))
        gel = 0.5 * conv * (1.0 + th)
        o_ref[0] = (gel * v_ref[0]).astype(BF16)
        gel_ref[0] = gel.astype(BF16)
        dgel_ref[0] = (0.5 * (1.0 + th)
                       + 0.5 * conv * (1.0 - th * th) * GELU_C * (1.0 + 3.0 * 0.044715 * conv * conv)).astype(BF16)

    tile = pl.BlockSpec((1, N, tc), lambda b, j: (b, 0, j))
    out = jax.ShapeDtypeStruct((B, N, Cf), BF16)
    return pl.pallas_call(
        body, name="ffn_act", grid=(B, ncb),
        in_specs=[tile, tile,
                  pl.BlockSpec((9, tc), lambda b, j: (0, j)),
                  pl.BlockSpec((1, tc), lambda b, j: (0, j))],
        out_specs=(tile, tile, tile), out_shape=(out, out, out),
        compiler_params=_cparams(("parallel", "parallel"), VMEM_LIMIT))(Fg, Fv, w9, b)


def _ffn_act_bwd(Fg, Fv, w9, gel, dgel, df, tc):
    B, N, Cf = Fg.shape
    C2 = 2 * Cf
    ncb = Cf // tc

    def body(g_ref, v_ref, w_ref, gel_ref, dgel_ref, df_ref, dF_ref, acc_ref):
        p = pl.program_id(2)

        @pl.when(p == 0)
        def _():
            dF_ref[0] = (df_ref[0] * gel_ref[0].astype(F32)).astype(BF16)

        @pl.when(p == 1)
        def _():
            xv = _grid_views(g_ref[0], N)
            df_ = df_ref[0]
            dc = df_ * v_ref[0] * dgel_ref[0].astype(F32)
            dcv = _grid_views(dc, N)
            dx = None
            for k, di, dj in _grid_taps():
                term = dcv[(-di, -dj)] * w_ref[k:k + 1, :]
                dx = term if dx is None else dx + term
                acc_ref[0, k:k + 1, :] = jnp.sum(dc * xv[(di, dj)], axis=0, keepdims=True)
            acc_ref[0, 9:10, :] = jnp.sum(dc, axis=0, keepdims=True)
            acc_ref[0, 10:16, :] = jnp.zeros((6, tc), F32)
            dF_ref[0] = dx.astype(BF16)

    tile = pl.BlockSpec((1, N, tc), lambda b, j, p: (b, 0, j))
    return pl.pallas_call(
        body, name="ffn_act_bwd", grid=(B, ncb, 2),
        in_specs=[tile, tile, pl.BlockSpec((9, tc), lambda b, j, p: (0, j)), tile, tile, tile],
        out_specs=(pl.BlockSpec((1, N, tc), lambda b, j, p: (b, 0, j + (1 - p) * ncb)),
                   pl.BlockSpec((1, 16, tc), lambda b, j, p: (b, 0, j))),
        out_shape=(jax.ShapeDtypeStruct((B, N, C2), BF16), jax.ShapeDtypeStruct((B, 16, Cf), F32)),
        compiler_params=_cparams(("parallel", "parallel", "arbitrary"), VMEM_LIMIT))(Fg, Fv, w9, gel, dgel, df)


ADAM_ROWS = 512


def _adamw(w, m, v, gparts, name):
    rows, cols = w.shape[-2:]
    P = gparts.shape[0]
    tr = _row_tile(rows, (P + 14) * 4 * (-(-cols // LANES) * LANES))
    c1 = 1.0 - ADAM_B1 ** ADAM_STEP
    c2 = 1.0 - ADAM_B2 ** ADAM_STEP

    def body(w_ref, m_ref, v_ref, g_ref, go_ref, d_ref, mo_ref, vo_ref):
        g = g_ref[0].astype(F32)
        for p in range(1, P):
            g = g + g_ref[p].astype(F32)
        m_ = ADAM_B1 * m_ref[...] + (1.0 - ADAM_B1) * g
        v_ = ADAM_B2 * v_ref[...] + (1.0 - ADAM_B2) * (g * g)
        go_ref[...] = g
        mo_ref[...] = m_
        vo_ref[...] = v_
        d_ref[...] = -ADAM_LR * ((m_ / c1) / (jnp.sqrt(v_ / c2) + ADAM_EPS) + ADAM_WD * w_ref[...])

    if w.ndim == 3:
        row = pl.BlockSpec((None, tr, cols), lambda i: (0, i, 0))
    else:
        row = pl.BlockSpec((tr, cols), lambda i: (i, 0))
    out = jax.ShapeDtypeStruct(w.shape, F32)
    return pl.pallas_call(
        body, name=name, grid=(rows // tr,),
        in_specs=[row, row, row, pl.BlockSpec((P, tr, cols), lambda i: (0, i, 0))],
        out_specs=(row, row, row, row), out_shape=(out, out, out, out),
        compiler_params=_cparams(("parallel",), VMEM_LIMIT))(w, m, v, gparts)


def _pack_rows(flat_len):
    rows = -(-flat_len // LANES)
    if rows > ADAM_ROWS:
        rows = -(-rows // ADAM_ROWS) * ADAM_ROWS
    else:
        rows = -(-rows // SUBLANES) * SUBLANES
    return rows


PACK_ALIGN = SUBLANES * LANES


def _pack(arrs, lead=()):
    pads = [(0, 0)] * len(lead)
    parts = []
    for a in arrs:
        f = a.reshape(lead + (-1,)).astype(F32)
        parts.append(jnp.pad(f, pads + [(0, -f.shape[-1] % PACK_ALIGN)]))
    flat = jnp.concatenate(parts, axis=-1)
    n = flat.shape[-1]
    rows = _pack_rows(n)
    flat = jnp.pad(flat, pads + [(0, rows * LANES - n)])
    return flat.reshape(lead + (rows, LANES))


def _unpack(buf, shapes):
    out, r = [], 0
    for s in shapes:
        n = math.prod(s)
        nr = -(-n // PACK_ALIGN) * SUBLANES
        out.append(buf[r:r + nr].reshape(-1)[:n].reshape(s))
        r += nr
    return out


def _col_shards(full):
    C = full.shape[-1]
    x = full.reshape(full.shape[:-1] + (N_DEV, C // N_DEV))
    return jnp.moveaxis(x, -2, 0)


def _from_col_shards(g):
    x = jnp.moveaxis(g, 0, -2)
    return x.reshape(x.shape[:-2] + (x.shape[-2] * x.shape[-1],))


def kernel(x, c, ctx, c_ctx, w_ada, b_ada, g_pre_mix, g_post_mix, g_pre_ffn, g_post_ffn, w_in, b_merge, dn_conv, dn_a_log, dn_dt_bias, dn_onorm, lru_conv, lru_conv_b, lru_w_rg, lru_b_rg, lru_w_ig, lru_b_ig, lru_lambda, w_branch_dn, w_branch_lru, w_out, w_up, ffn_dw, ffn_dw_b, w_down, loss_target, m_c_ctx, m_w_ada, m_b_ada, m_g_pre_mix, m_g_post_mix, m_g_pre_ffn, m_g_post_ffn, m_w_in, m_b_merge, m_dn_conv, m_dn_a_log, m_dn_dt_bias, m_dn_onorm, m_lru_conv, m_lru_conv_b, m_lru_w_rg, m_lru_b_rg, m_lru_w_ig, m_lru_b_ig, m_lru_lambda, m_w_branch_dn, m_w_branch_lru, m_w_out, m_w_up, m_ffn_dw, m_ffn_dw_b, m_w_down, v_c_ctx, v_w_ada, v_b_ada, v_g_pre_mix, v_g_post_mix, v_g_pre_ffn, v_g_post_ffn, v_w_in, v_b_merge, v_dn_conv, v_dn_a_log, v_dn_dt_bias, v_dn_onorm, v_lru_conv, v_lru_conv_b, v_lru_w_rg, v_lru_b_rg, v_lru_w_ig, v_lru_b_ig, v_lru_lambda, v_w_branch_dn, v_w_branch_lru, v_w_out, v_w_up, v_ffn_dw, v_ffn_dw_b, v_w_down):
    params = dict(c_ctx=c_ctx, w_ada=w_ada, b_ada=b_ada, g_pre_mix=g_pre_mix, g_post_mix=g_post_mix, g_pre_ffn=g_pre_ffn, g_post_ffn=g_post_ffn, w_in=w_in, b_merge=b_merge, dn_conv=dn_conv, dn_a_log=dn_a_log, dn_dt_bias=dn_dt_bias, dn_onorm=dn_onorm, lru_conv=lru_conv, lru_conv_b=lru_conv_b, lru_w_rg=lru_w_rg, lru_b_rg=lru_b_rg, lru_w_ig=lru_w_ig, lru_b_ig=lru_b_ig, lru_lambda=lru_lambda, w_branch_dn=w_branch_dn, w_branch_lru=w_branch_lru, w_out=w_out, w_up=w_up, ffn_dw=ffn_dw, ffn_dw_b=ffn_dw_b, w_down=w_down)
    mom1 = dict(c_ctx=m_c_ctx, w_ada=m_w_ada, b_ada=m_b_ada, g_pre_mix=m_g_pre_mix, g_post_mix=m_g_post_mix, g_pre_ffn=m_g_pre_ffn, g_post_ffn=m_g_post_ffn, w_in=m_w_in, b_merge=m_b_merge, dn_conv=m_dn_conv, dn_a_log=m_dn_a_log, dn_dt_bias=m_dn_dt_bias, dn_onorm=m_dn_onorm, lru_conv=m_lru_conv, lru_conv_b=m_lru_conv_b, lru_w_rg=m_lru_w_rg, lru_b_rg=m_lru_b_rg, lru_w_ig=m_lru_w_ig, lru_b_ig=m_lru_b_ig, lru_lambda=m_lru_lambda, w_branch_dn=m_w_branch_dn, w_branch_lru=m_w_branch_lru, w_out=m_w_out, w_up=m_w_up, ffn_dw=m_ffn_dw, ffn_dw_b=m_ffn_dw_b, w_down=m_w_down)
    mom2 = dict(c_ctx=v_c_ctx, w_ada=v_w_ada, b_ada=v_b_ada, g_pre_mix=v_g_pre_mix, g_post_mix=v_g_post_mix, g_pre_ffn=v_g_pre_ffn, g_post_ffn=v_g_post_ffn, w_in=v_w_in, b_merge=v_b_merge, dn_conv=v_dn_conv, dn_a_log=v_dn_a_log, dn_dt_bias=v_dn_dt_bias, dn_onorm=v_dn_onorm, lru_conv=v_lru_conv, lru_conv_b=v_lru_conv_b, lru_w_rg=v_lru_w_rg, lru_b_rg=v_lru_b_rg, lru_w_ig=v_lru_w_ig, lru_b_ig=v_lru_b_ig, lru_lambda=v_lru_lambda, w_branch_dn=v_w_branch_dn, w_branch_lru=v_w_branch_lru, w_out=v_w_out, w_up=v_w_up, ffn_dw=v_ffn_dw, ffn_dw_b=v_ffn_dw_b, w_down=v_w_down)
    names = list(params)

    B, N, D = x.shape
    NC = ctx.shape[1]
    T = NC + N
    H, hd = dn_a_log.shape[2], dn_onorm.shape[1]
    HD = H * hd
    nd = 2 * H
    W = lru_conv_b.shape[1]
    nblk, bdim = lru_w_rg.shape[2], lru_w_rg.shape[3]
    Cf = ffn_dw_b.shape[1]
    sw = w_ada.shape[2]
    tm = min(256, NC)
    nct = NC // tm
    ncc = NC // CHUNK
    nch = T // CHUNK
    R, RL = B * T, B * N
    bw = min(256, W)
    me = _my_index()
    assert NC % tm == 0 and N % tm == 0 and B + 1 <= SUBLANES and bw % bdim == 0 and D % LANES == 0

    cpad = jnp.zeros((SUBLANES, D), F32).at[:B].set(c).at[B].set(c_ctx)
    ccp = _all_gather([cpad], "ag_cond")[0].reshape(N_DEV * SUBLANES, D)
    mods_sh = _ada_fwd(ccp, w_ada[0], lax.dynamic_slice(b_ada, (0, me * sw), (1, sw)))
    lru_vecs = jnp.concatenate([lru_b_rg[0], lru_b_ig[0], lru_lambda[0], jnp.zeros_like(lru_lambda[0])], axis=0)
    mods_g, w_in_g, dn_conv_g, lru_conv_g, lru_vecs_g = _all_gather(
        [mods_sh, w_in[0].astype(BF16).T, dn_conv[0], lru_conv[0], lru_vecs], "ag_first")
    ag_mix = _split_start([w_branch_dn[0].astype(BF16), w_branch_lru[0].astype(BF16), w_out[0].astype(BF16)],
                          "gather", "ag_mix_start")
    ag_ffn = _split_start([w_up[0].astype(BF16) + ag_mix[4][0, 0].astype(BF16), w_down[0].astype(BF16),
                           ffn_dw[0].reshape(9, -1)], "gather", "ag_ffn_start")
    mine = lax.dynamic_slice(mods_g, (0, me * SUBLANES, 0), (N_DEV, SUBLANES, sw))
    mods = jnp.moveaxis(mine, 0, 1).reshape(SUBLANES, 6, D)[:B + 1]
    mods = jnp.pad(mods, ((0, 0), (0, SUBLANES - 6), (0, 0))) + ag_ffn[4][0, 0]
    dn_conv_f = _from_col_shards(dn_conv_g)
    lru_conv_f = _from_col_shards(lru_conv_g)
    lru_vecs_f = _from_col_shards(lru_vecs_g)
    lru_lam_f = lru_vecs_f[4:6]

    csh = w_in_g.shape[1]
    w_in_t = w_in_g.reshape(N_DEV * csh, D)
    o_z, o_a, o_xl = 3 * HD, 4 * HD, 4 * HD + 2 * nd
    o_yl, o_mg = o_xl + W, o_xl + 2 * W
    w_qkv, w_z = w_in_t[:o_z], w_in_t[o_z:o_a]
    w_ab = jnp.pad(w_in_t[o_a:o_xl], ((0, LANES - 2 * nd), (0, 0)))
    w_xl, w_yl, w_mg = w_in_t[o_xl:o_yl], w_in_t[o_yl:o_mg], w_in_t[o_mg:]

    def blockdiag(wg):
        per = bw // bdim
        g5 = wg.reshape(2, W // bw, per, bdim, bdim)
        eye = jnp.eye(per, dtype=wg.dtype)
        return jnp.einsum("dtpij,pq->dtpiqj", g5, eye).reshape(2, W // bw, bw, bw)

    bd_rg, bd_ig = blockdiag(lru_w_rg[0]), blockdiag(lru_w_ig[0])
    wbd = jnp.stack([bd_rg[0], bd_ig[0], bd_rg[1], bd_ig[1]]).astype(BF16)
    bias4 = jnp.stack([lru_vecs_f[0], lru_vecs_f[2], lru_vecs_f[1], lru_vecs_f[3]])
    alog_v = jnp.pad(dn_a_log.reshape(1, nd), ((0, 0), (0, LANES - nd)))
    dtb_v = jnp.pad(dn_dt_bias.reshape(1, nd), ((0, 0), (0, LANES - nd)))

    h0 = jnp.concatenate([ctx, x], axis=1)
    u1 = _norm_mod_fwd(h0, g_pre_mix, mods, 0, 1, nct, tm, "norm_mod1")
    u1f = u1.reshape(R, D)
    p_qkv = _mm(u1f, w_qkv, "nt", "mm_qkv").reshape(B, T, 3 * HD)
    p_z = _mm(u1f, w_z, "nt", "mm_z", out_dtype=BF16).reshape(B, T, HD)
    p_ab = _mm(u1f, w_ab, "nt", "mm_ab")
    p_xl = _mm(u1f, w_xl, "nt", "mm_xl").reshape(B, T, W)
    p_yl = _mm(u1f, w_yl, "nt", "mm_yl", out_dtype=BF16).reshape(B, T, W)
    p_mg = _mm(u1f, w_mg, "nt", "mm_mg", out_dtype=BF16).reshape(B, T, 2 * D)

    conv_offs = (-2, -1, 0, 1)
    tcc = _tile(HD, 1024)
    gb = _ab_act(p_ab, alog_v, dtb_v, nd, tm)
    gb3 = gb.reshape(B, T, LANES)
    cv, qkvn = _dwconv(p_qkv, 0, 3 * HD, dn_conv_f, None, conv_offs, nct, tm, HD, "dn_conv", qkv_heads=(H, hd))
    o_d0, o_d1, s_in0, s_in1, tm_d0, tm_d1 = _delta_fwd(qkvn, gb3, H, hd, ncc)
    o2 = (o_d0, o_d1)
    y_dn = _dn_out(o2, p_z, 0, dn_onorm, H, hd, nct, tm)

    tcw = _tile(W, 1024)
    xc = _dwconv(p_xl, 0, W, lru_conv_f, lru_conv_b, conv_offs, nct, tm, tcw, "lru_conv")
    tmg = max(t_ for t_ in range(SUBLANES, min(T, 768) + 1, SUBLANES) if T % t_ == 0)
    a2, inp2 = _lru_gates(xc, wbd, bias4, lru_lam_f, tmg, bw)
    hd0 = _lru_scan(a2, inp2, 0, nct, tm, tcw)
    hd1 = _lru_scan(a2, inp2, 1, nct, tm, tcw)
    y_lru = _lru_out(hd0, hd1, p_yl, 0, nct, tm, tcw)

    def gathered(started, after, name):
        xs_, lands_ = _split_wait(started, "gather", after, name)
        return [lax.dynamic_update_slice(land, x_[None], (me,) + (0,) * x_.ndim) for land, x_ in zip(lands_, xs_)]

    w_bdn_g, w_blru_g, w_out_g = gathered(ag_mix, y_lru, "ag_mix_wait")
    w_bdn_f, w_blru_f, w_out_f = w_bdn_g.reshape(HD, D), w_blru_g.reshape(W, D), w_out_g.reshape(D, D)
    bd = _mm(y_dn.reshape(RL, HD), w_bdn_f, "nn", "mm_bdn", out_dtype=BF16).reshape(B, N, D)
    bl = _mm(y_lru.reshape(RL, W), w_blru_f, "nn", "mm_blru", out_dtype=BF16).reshape(B, N, D)
    mixin = _merge(bd, bl, p_mg, b_merge, nct, tm)
    mix = _mm(mixin.reshape(RL, D), w_out_f, "nn", "mm_out").reshape(B, N, D)
    h1 = _resid_norm_fwd(x, mix, g_post_mix, mods, 2, tm, "resid_norm1")
    u2 = _norm_mod_fwd(h1, g_pre_ffn, mods, 3, 4, 0, tm, "norm_mod2")
    w_up_g, w_down_g, ffn_dw_g = gathered(ag_ffn, u2, "ag_ffn_wait")
    w_down_f = w_down_g.reshape(Cf, D)
    ffn_dw_f = _from_col_shards(ffn_dw_g)
    half = N_DEV // 2
    Fg = _mm(u2.reshape(RL, D), w_up_g[:half], "nn", "mm_up_gate").reshape(B, N, Cf)
    Fv = _mm(u2.reshape(RL, D), w_up_g[half:], "nn", "mm_up_val", out_dtype=BF16).reshape(B, N, Cf)
    tcf = LANES
    f, f_gel, f_dgel = _ffn_act(Fg, Fv, ffn_dw_f, ffn_dw_b, tcf)
    dproj = _mm(f.reshape(RL, Cf), w_down_f, "nn", "mm_down").reshape(B, N, D)

    dd, dh2, acc_f = _final_fwd_bwd(h1, dproj, g_post_ffn, mods, 5, loss_target, tm)
    loss = lax.psum((0.5 / D) * jnp.sum(acc_f[:, 2, :]), MESH_AXES)
    ddf = dd.reshape(RL, D)
    df = _mm(ddf, w_down_f, "nt", "mm_down_dx").reshape(B, N, Cf)
    gw_down = _mm(f.reshape(RL, Cf), ddf, "tn", "mm_down_dw", out_dtype=BF16)
    dF, acc_ffn = _ffn_act_bwd(Fg, Fv, ffn_dw_f, f_gel, f_dgel, df, tcf)
    dFf = dF.reshape(RL, 2 * Cf)
    du2 = _mm(dFf, w_up_g, "nt", "mm_up_dx").reshape(B, N, D)
    gw_up = _mm(u2.reshape(RL, D), dFf, "tn", "mm_up_dw", out_dtype=BF16, out_shards=N_DEV)
    jm = 2 * lax.axis_index("x") + lax.axis_index("y")

    def pair_sums(started, after, tag):
        parts8, sib4 = _split_wait(started, "pair", after, f"rs_pair_{tag}_wait")
        return [_pair_sum(p8, s4, f"rs_pair_sum_{tag}{i}") for i, (p8, s4) in enumerate(zip(parts8, sib4))]

    def own_slab(land, x4):
        idx = (jm,) + (0,) * (x4.ndim - 1)
        return lax.dynamic_update_slice(land, lax.dynamic_slice(x4, idx, (1,) + x4.shape[1:]), idx)

    p_early = _split_start([gw_up, gw_down.reshape(N_DEV, Cf // N_DEV, D)], "pair", "rs_pair_ffn_start")
    mods_b = mods + p_early[4][0, 0]
    dh1, acc2x, _ = _norm_mod_bwd(h1, du2, g_pre_ffn, mods_b, 4, 0, tm, dh2, "norm_mod2_bwd")
    dmix, acc_r1 = _resid_norm_bwd(mix, dh1, g_post_mix, mods, 2, tm, "resid_norm1_bwd")
    q_early = _split_start(pair_sums(p_early, dmix, "ffn"), "quad", "rs_quad_ffn_start")
    dmixf = dmix.reshape(RL, D)
    dmixin = _mm(dmixf, w_out_f, "nt", "mm_out_dx").reshape(B, N, D)
    gw_out = _mm(mixin.reshape(RL, D), dmixf, "tn", "mm_out_dw", out_dtype=BF16)
    dbd, dbl, dmg, acc_mg = _merge_bwd(dmixin, bd, bl, p_mg, b_merge + q_early[4][0:1, 0:1], nct, tm)
    dy_dn = _mm(dbd.reshape(RL, D), w_bdn_f, "nt", "mm_bdn_dx").reshape(B, N, HD)
    gw_bdn = _mm(y_dn.reshape(RL, HD), dbd.reshape(RL, D), "tn", "mm_bdn_dw", out_dtype=BF16)
    dy_lru = _mm(dbl.reshape(RL, D), w_blru_f, "nt", "mm_blru_dx").reshape(B, N, W)
    gw_blru = _mm(y_lru.reshape(RL, W), dbl.reshape(RL, D), "tn", "mm_blru_dw", out_dtype=BF16)

    dh, dyl = _lru_out_bwd(hd0, hd1, p_yl, 0, dy_lru, nct, tm, tcw)
    lam0, da0 = _lru_scan_bwd(a2, hd0, dh, 0, nct, tm, tcw)
    lam1, da1 = _lru_scan_bwd(a2, hd1, dh, 1, nct, tm, tcw)
    dxc, dwbd, acc_lru = _lru_gates_bwd(xc, wbd, bias4, lru_lam_f, (lam0, lam1), (da0, da1), tmg, bw)
    dxl = _dwconv(dxc, 0, W, lru_conv_f, None, tuple(-o for o in conv_offs), nct, tm, tcw, "lru_conv_dx", BF16)
    acc_lc = _dwconv_wgrad(dxc, p_xl, 0, W, conv_offs, nct, tm, tcw, "lru_conv_dw")

    do, dz, acc_on = _dn_out_bwd(o2, p_z, 0, dn_onorm, dy_dn, H, hd, nct, tm)
    dqkvn0, dqkvn1, dgb0, dgb1 = _delta_bwd(qkvn, gb3, (s_in0, s_in1), (tm_d0, tm_d1), do, H, hd, ncc)
    dcv = _dn_act_bwd(cv, (dqkvn0, dqkvn1), H, hd, tm)
    dqkv = _dwconv(dcv, 0, 3 * HD, dn_conv_f, None, tuple(-o for o in conv_offs), nct, tm, tcc, "dn_conv_dx", BF16)
    acc_dc = _dwconv_wgrad(dcv, p_qkv, 0, 3 * HD, conv_offs, nct, tm, tcc, "dn_conv_dw")
    dp_ab, acc_ab = _ab_act_bwd(p_ab, gb, (dgb0.reshape(R, LANES), dgb1.reshape(R, LANES)), alog_v, dtb_v, nd, tm)

    groups = [(dqkv.reshape(R, 3 * HD), w_qkv, "qkv"), (dz.reshape(R, HD), w_z, "z"), (dp_ab, w_ab, "ab"),
              (dxl.reshape(R, W), w_xl, "xl"), (dyl.reshape(R, W), w_yl, "yl"), (dmg.reshape(R, 2 * D), w_mg, "mg")]
    du1 = _mm_sum([(dg_, wg_) for dg_, wg_, _ in groups], "mm_in_dx")
    gw_groups = [_mm(dg_, u1f, "tn", "mm_in_dw_" + nm, out_dtype=BF16) for dg_, _, nm in groups]
    gw_groups[2] = gw_groups[2][:2 * nd]
    gw_in = jnp.concatenate(gw_groups, axis=0).reshape(N_DEV, csh, D)
    grad_x, acc1x, acc1c = _norm_mod_bwd(h0, du1.reshape(B, T, D), g_pre_mix, mods, 1, nct, tm, dh1, "norm_mod1_bwd")

    g_lru_conv = jnp.sum(acc_lc[:, :4], 0)
    g_dn_conv = jnp.sum(acc_dc[:, :4], 0)
    g_ffn_dw = jnp.sum(acc_ffn[:, :9], 0).reshape(3, 3, Cf)
    sm_names = ["dn_conv", "lru_conv", "lru_b_rg", "lru_b_ig", "lru_lambda", "ffn_dw"]
    sm_parts = [_col_shards(g_dn_conv), _col_shards(g_lru_conv),
                _col_shards(jnp.stack([acc_lru[0], acc_lru[2]])), _col_shards(jnp.stack([acc_lru[1], acc_lru[3]])),
                _col_shards(acc_lru[4:6]), _col_shards(g_ffn_dw)]
    late8 = [gw_in, gw_bdn.reshape(N_DEV, HD // N_DEV, D), gw_blru.reshape(N_DEV, W // N_DEV, D),
             gw_out.reshape(N_DEV, D // N_DEV, D), _pack(sm_parts, lead=(N_DEV,))]
    p_late = _split_start(late8, "pair", "rs_pair_mix_start")
    q_late = _split_start(pair_sums(p_late, gw_out, "mix"), "quad", "rs_quad_mix_start")
    ffn_x, ffn_land = _split_wait(q_early, "quad", q_late[4], "rs_quad_ffn_wait")
    results = {}
    for n, x4, land in zip(["w_up", "w_down"], ffn_x, ffn_land):
        results[n] = list(_adamw(params[n], mom1[n], mom2[n], own_slab(land, x4), "adamw_" + n))

    zeros_d = jnp.zeros((B, D), F32)
    dm_rows = jnp.stack([acc1x[:, 0], acc1x[:, 1], acc_r1[:, 0], acc2x[:, 0], acc2x[:, 1], acc_f[:, 0]], axis=1)
    dm_ctx = jnp.stack([jnp.sum(acc1c[:, 0], 0), jnp.sum(acc1c[:, 1], 0)] + [zeros_d[0]] * 4, axis=0)[None]
    dm_pad = jnp.zeros((SUBLANES, 6 * D), F32).at[:B + 1].set(jnp.concatenate([dm_rows, dm_ctx], 0).reshape(B + 1, 6 * D))
    dm_g = _all_gather([dm_pad], "ag_dmods")[0]
    g_mine = lax.dynamic_slice(dm_g, (0, 0, me * sw), (N_DEV, SUBLANES, sw)).reshape(N_DEV * SUBLANES, sw)
    gw_ada, dcc = _ada_bwd(ccp, w_ada[0], g_mine, c_ctx.reshape(1, D), B)

    per = bw // bdim

    def diag_blocks(dwf):
        g6 = dwf.reshape(W // bw, per, bdim, per, bdim)
        return jnp.einsum("tpiqj,pq->tpij", g6, jnp.eye(per, dtype=F32)).reshape(nblk, bdim, bdim)

    g_rep = dict(
        c_ctx=dcc[0],
        g_pre_mix=jnp.sum(acc1x[:, 2] + acc1c[:, 2], 0)[None], g_post_mix=jnp.sum(acc_r1[:, 1], 0)[None],
        g_pre_ffn=jnp.sum(acc2x[:, 2], 0)[None], g_post_ffn=jnp.sum(acc_f[:, 1], 0)[None],
        b_merge=jnp.sum(acc_mg[:, 0], 0)[None],
        dn_a_log=acc_ab[0, :nd].reshape(1, 2, H), dn_dt_bias=acc_ab[1, :nd].reshape(1, 2, H),
        dn_onorm=jnp.sum(acc_on[:, 0], 0)[None],
        lru_conv_b=jnp.sum(acc_lc[:, 4], 0)[None],
        lru_w_rg=jnp.stack([diag_blocks(dwbd[0]), diag_blocks(dwbd[2])])[None],
        lru_w_ig=jnp.stack([diag_blocks(dwbd[1]), diag_blocks(dwbd[3])])[None],
        ffn_dw_b=jnp.sum(acc_ffn[:, 9], 0)[None])
    mat_names = ["lru_w_rg", "lru_w_ig"]
    vec_names = [n for n in g_rep if n not in mat_names]
    vec_parts, mat_parts = _all_gather([_pack([g_rep[n] for n in vec_names]),
                                        _pack([g_rep[n] for n in mat_names]).astype(BF16)], "ag_rep_grads")
    for grp, parts, nm in ((vec_names, vec_parts, "adamw_rep_vec"), (mat_names, mat_parts, "adamw_rep_mat")):
        out4 = _adamw(_pack([params[n] for n in grp]), _pack([mom1[n] for n in grp]),
                      _pack([mom2[n] for n in grp]), parts, nm)
        for k, buf in enumerate(out4):
            for n, arr in zip(grp, _unpack(buf, [params[n].shape for n in grp])):
                results.setdefault(n, [None] * 4)[k] = arr

    ba_out = _adamw(_pack([b_ada]), _pack([m_b_ada]), _pack([v_b_ada]),
                    _pack([dm_g.reshape(N_DEV * SUBLANES, 6 * D)], lead=(N_DEV * SUBLANES,)), "adamw_b_ada")
    results["b_ada"] = [_unpack(buf, [b_ada.shape])[0] for buf in ba_out]

    wa_out = _adamw(w_ada, m_w_ada, v_w_ada, gw_ada[None], "adamw_w_ada")
    results["w_ada"] = list(wa_out)

    mix_x, mix_land = _split_wait(q_late, "quad", wa_out[0], "rs_quad_mix_wait")
    recv4 = [own_slab(land, x4) for land, x4 in zip(mix_land, mix_x)]
    for n, g4 in zip(["w_in", "w_branch_dn", "w_branch_lru", "w_out"], recv4[:-1]):
        if n == "w_in":
            g4 = jnp.swapaxes(g4, 1, 2)
        results[n] = list(_adamw(params[n], mom1[n], mom2[n], g4, "adamw_" + n))
    sm_out = _adamw(_pack([params[n] for n in sm_names]), _pack([mom1[n] for n in sm_names]),
                    _pack([mom2[n] for n in sm_names]), recv4[-1], "adamw_small")
    for k, buf in enumerate(sm_out):
        for n, arr in zip(sm_names, _unpack(buf, [params[n].shape for n in sm_names])):
            results.setdefault(n, [None] * 4)[k] = arr

    outs = [loss, grad_x]
    for k in range(4):
        outs += [results[n][k] for n in names]
    return tuple(outs)
```

```python
import math

import jax
import jax.numpy as jnp
from jax import lax
from jax.experimental import pallas as pl
from jax.experimental.pallas import tpu as pltpu

F32 = jnp.float32
BF16 = jnp.bfloat16

EPS = 1e-6
GRID_W = 64
CHUNK = 64
LRU_C = 8.0
N_DEV = 8
ADAM_LR = 0.001
ADAM_B1 = 0.9
ADAM_B2 = 0.999
ADAM_EPS = 1e-08
ADAM_WD = 0.01
ADAM_STEP = 10

LANES = 128
SUBLANES = 8
VMEM_LIMIT = 48 * 1024 * 1024
MESH_AXES = ("x", "y", "c")
GELU_C = math.sqrt(2.0 / math.pi)


def _cparams(sem=None, vmem=None):
    kw = {}
    if sem is not None:
        kw["dimension_semantics"] = sem
    if vmem is not None:
        kw["vmem_limit_bytes"] = vmem
    return pltpu.CompilerParams(**kw)


def _tile(n, pref):
    if n <= pref:
        return n
    t = (pref // LANES) * LANES
    while n % t:
        t -= LANES
    return t


def _sigmoid(x):
    return 1.0 / (1.0 + jnp.exp(-x))


def _silu(x):
    return x * _sigmoid(x)


def _dsilu(x):
    s = _sigmoid(x)
    return s * (1.0 + x * (1.0 - s))


def _softplus(x):
    return jnp.maximum(x, 0.0) + jnp.log(1.0 + jnp.exp(-jnp.abs(x)))


def _gelu(x):
    return 0.5 * x * (1.0 + jnp.tanh(GELU_C * (x + 0.044715 * x * x * x)))


def _dgelu(x):
    t = jnp.tanh(GELU_C * (x + 0.044715 * x * x * x))
    return 0.5 * (1.0 + t) + 0.5 * x * (1.0 - t * t) * GELU_C * (1.0 + 3.0 * 0.044715 * x * x)


def _dot(a, b, dims="nn"):
    dn = {"nn": (((1,), (0,)), ((), ())),
          "nt": (((1,), (1,)), ((), ())),
          "tn": (((0,), (0,)), ((), ()))}[dims]
    return lax.dot_general(a.astype(BF16), b.astype(BF16), dn, preferred_element_type=F32)


def _split2(x):
    hi = x.astype(BF16)
    lo = (x - hi.astype(F32)).astype(BF16)
    return hi, lo


def _split3(x):
    h1 = x.astype(BF16)
    r1 = x - h1.astype(F32)
    h2 = r1.astype(BF16)
    h3 = (r1 - h2.astype(F32)).astype(BF16)
    return h1, h2, h3


def _dot_hi(a, b):
    ah, al = _split2(a)
    bh, bl = _split2(b)
    return _dot(ah, bh) + (_dot(ah, bl) + _dot(al, bh))


def _dot_mask(m, x, dims="nn"):
    h1, h2, h3 = _split3(x)
    if dims == "xtn":
        return _dot(h1, m, "tn") + (_dot(h2, m, "tn") + _dot(h3, m, "tn"))
    return _dot(m, h1, dims) + (_dot(m, h2, dims) + _dot(m, h3, dims))


def _my_index():
    return 4 * lax.axis_index("x") + 2 * lax.axis_index("y") + lax.axis_index("c")


def _hbm_call(body, xs, out_shapes, n_sems, name):
    any_spec = pl.BlockSpec(memory_space=pl.ANY)
    return pl.pallas_call(
        body, name=name, out_shape=tuple(out_shapes),
        in_specs=[any_spec] * len(xs), out_specs=tuple([any_spec] * len(out_shapes)),
        scratch_shapes=[pltpu.SemaphoreType.DMA((n_sems,)), pltpu.SemaphoreType.DMA((n_sems,)),
                        pltpu.SemaphoreType.DMA((len(xs),))],
    )(*xs)


def _all_gather(xs, name):
    n = len(xs)

    def body(*refs):
        x_refs, out_refs = refs[:n], refs[n:2 * n]
        send_sems, recv_sems, local_sems = refs[2 * n:]
        x_, y_, c_ = lax.axis_index("x"), lax.axis_index("y"), lax.axis_index("c")
        me, sibling = (x_, y_, c_), (x_, y_, 1 - c_)
        chips = [(1 - x_, y_), (x_, 1 - y_), (1 - x_, 1 - y_)]

        def slab(a, px, py, pc):
            return out_refs[a].at[4 * px + 2 * py + pc]

        def copy(a, k, block, to, src=None):
            return pltpu.make_async_remote_copy(
                src_ref=slab(a, *block) if src is None else src, dst_ref=slab(a, *block),
                send_sem=send_sems.at[7 * a + k], recv_sem=recv_sems.at[7 * a + k],
                device_id=to, device_id_type=pl.DeviceIdType.MESH)

        mine = [pltpu.make_async_copy(x_refs[a], slab(a, *me), local_sems.at[a]) for a in range(n)]
        for cp in mine:
            cp.start()
        sent = []
        for j, chip in enumerate(chips):
            sent += [copy(a, 1 + j, me, (*chip, c_), src=x_refs[a]) for a in range(n)]
        sent += [copy(a, 0, me, sibling, src=x_refs[a]) for a in range(n)]
        for cp in sent:
            cp.start()
        for j, chip in enumerate(chips):
            for a in range(n):
                copy(a, 1 + j, (*chip, c_), me).wait_recv()
                fwd = copy(a, 4 + j, (*chip, c_), sibling)
                fwd.start()
                sent.append(fwd)
        for a in range(n):
            copy(a, 0, sibling, me).wait_recv()
        for j, chip in enumerate(chips):
            for a in range(n):
                copy(a, 4 + j, (*chip, 1 - c_), me).wait_recv()
        for cp in sent:
            cp.wait_send()
        for cp in mine:
            cp.wait()

    outs = [jax.ShapeDtypeStruct((N_DEV,) + x.shape, x.dtype) for x in xs]
    return _hbm_call(body, xs, outs, 7 * n, name)


SPLIT_COPIES = {"gather": 7, "quad": 3, "pair": 4}


def _split_views(kind, x_ref, land_ref, k):
    x_, y_, c_ = lax.axis_index("x"), lax.axis_index("y"), lax.axis_index("c")
    if kind == "pair":
        return x_ref.at[2 * (k - 1) + (1 - c_)], land_ref.at[k - 1], land_ref.at[k - 1], (x_, y_, 1 - c_)
    if kind == "gather":
        px = 1 - x_ if (k >> 2) & 1 else x_
        py = 1 - y_ if (k >> 1) & 1 else y_
        pc = 1 - c_ if k & 1 else c_
        return x_ref, land_ref.at[4 * x_ + 2 * y_ + c_], land_ref.at[4 * px + 2 * py + pc], (px, py, pc)
    px = 1 - x_ if (k >> 1) & 1 else x_
    py = 1 - y_ if k & 1 else y_
    return x_ref.at[2 * px + py], land_ref.at[2 * x_ + y_], land_ref.at[2 * px + py], (px, py, c_)


def _split_start(xs, kind, name):
    n = len(xs)
    npeer = SPLIT_COPIES[kind]
    land_shape = {"gather": lambda x: (N_DEV,) + x.shape, "quad": lambda x: x.shape, "pair": lambda x: (4,) + x.shape[1:]}
    lands = [lax.empty(land_shape[kind](x), x.dtype) for x in xs]

    def body(*refs):
        x_refs, land_refs = refs[:n], refs[n:2 * n]
        send_sems, recv_sems, token = refs[2 * n], refs[2 * n + 1], refs[4 * n + 2]
        for k in range(1, npeer + 1):
            for a in range(n):
                src, dst, _, peer = _split_views(kind, x_refs[a], land_refs[a], k)
                pltpu.make_async_remote_copy(
                    src_ref=src, dst_ref=dst, send_sem=send_sems.at[npeer * a + k - 1],
                    recv_sem=recv_sems.at[npeer * a + k - 1],
                    device_id=peer, device_id_type=pl.DeviceIdType.MESH).start()
        token[...] = jnp.zeros_like(token)

    hbm = pl.BlockSpec(memory_space=pltpu.HBM)
    sem = pl.BlockSpec(memory_space=pltpu.SEMAPHORE)
    sems = pltpu.SemaphoreType.DMA((npeer * n,))
    out = pl.pallas_call(
        body, name=name,
        out_shape=(sems, sems, *[pltpu.HBM(a.shape, a.dtype) for a in list(xs) + lands],
                   jax.ShapeDtypeStruct((SUBLANES, LANES), F32)),
        in_specs=[hbm] * (2 * n),
        out_specs=(sem, sem, *[hbm] * (2 * n), pl.BlockSpec(memory_space=pltpu.VMEM)),
        input_output_aliases={i: 2 + i for i in range(2 * n)},
        compiler_params=pltpu.CompilerParams(has_side_effects=pltpu.SideEffectType.DATAFLOW_SIDE_EFFECTING),
    )(*[pltpu.with_memory_space_constraint(a, pltpu.HBM) for a in list(xs) + lands])
    return out[0], out[1], list(out[2:2 + n]), list(out[2 + n:2 + 2 * n]), out[-1]


def _split_wait(started, kind, after, name):
    send_sems_, recv_sems_, x_thru, land_thru, _ = started
    n = len(x_thru)
    npeer = SPLIT_COPIES[kind]

    def body(*refs):
        x_refs, land_refs = refs[:n], refs[n:2 * n]
        send_sems, recv_sems = refs[2 * n], refs[2 * n + 1]
        for k in range(1, npeer + 1):
            for a in range(n):
                src, _, landed, peer = _split_views(kind, x_refs[a], land_refs[a], k)
                cp = pltpu.make_async_remote_copy(
                    src_ref=src, dst_ref=landed, send_sem=send_sems.at[npeer * a + k - 1],
                    recv_sem=recv_sems.at[npeer * a + k - 1],
                    device_id=peer, device_id_type=pl.DeviceIdType.MESH)
                cp.wait_send()
                cp.wait_recv()

    hbm = pl.BlockSpec(memory_space=pltpu.HBM)
    sem = pl.BlockSpec(memory_space=pltpu.SEMAPHORE)
    out = pl.pallas_call(
        body, name=name,
        out_shape=tuple(pltpu.HBM(a.shape, a.dtype) for a in x_thru + land_thru),
        in_specs=[hbm] * (2 * n) + [sem, sem, pl.BlockSpec(memory_space=pl.ANY)],
        out_specs=tuple([hbm] * (2 * n)),
        input_output_aliases={i: i for i in range(2 * n)},
        compiler_params=pltpu.CompilerParams(has_side_effects=pltpu.SideEffectType.DATAFLOW_SIDE_EFFECTING),
    )(*x_thru, *land_thru, send_sems_, recv_sems_, after)
    return list(out[:n]), list(out[n:])


def _pair_sum(x8, r4, name):
    _, r, c = x8.shape
    tr = _row_tile(r, c * 12)

    def body(core_ref, x_ref, r_ref, o_ref):
        o_ref[...] = (x_ref[...].astype(F32) + r_ref[...].astype(F32)).astype(o_ref.dtype)

    core = lax.axis_index("c").astype(jnp.int32).reshape(1)
    return pl.pallas_call(
        body, name=name,
        grid_spec=pltpu.PrefetchScalarGridSpec(
            num_scalar_prefetch=1, grid=(4, r // tr),
            in_specs=[pl.BlockSpec((None, tr, c), lambda j, i, core_ref: (2 * j + core_ref[0], i, 0)),
                      pl.BlockSpec((None, tr, c), lambda j, i, core_ref: (j, i, 0))],
            out_specs=pl.BlockSpec((None, tr, c), lambda j, i, core_ref: (j, i, 0))),
        out_shape=jax.ShapeDtypeStruct((4, r, c), x8.dtype),
        compiler_params=_cparams(("parallel", "parallel"), VMEM_LIMIT))(core, x8, r4)


def _row_tile(r, bytes_per_row, budget=10 * 1024 * 1024):
    best = None
    for t in range(16, r + 1, 16):
        if r % t == 0 and t * bytes_per_row <= budget:
            best = t
    return best if best is not None else r


def _mm(a, b, dims, name, out_dtype=F32, out_shards=1, tm=1024, tn=1024, tk=1024):
    S = b.shape[0] if b.ndim == 3 else 1
    bshape = (b.shape[1], S * b.shape[2]) if b.ndim == 3 else b.shape
    if dims == "nn":
        (M, K), (K2, N) = a.shape, bshape
    elif dims == "nt":
        (M, K), (N, K2) = a.shape, bshape
    else:
        (K, M), (K2, N) = a.shape, bshape
    assert K == K2, (a.shape, b.shape, dims)
    cs = bshape[1] // S
    tm, tk = _tile(M, tm), _tile(K, min(tk, cs) if (S > 1 and dims == "nt") else tk)
    tn = _tile(N, min(tn, cs) if (S > 1 and dims != "nt") else min(tn, N // out_shards))
    assert cs % (tk if dims == "nt" else tn) == 0 and (N // out_shards) % tn == 0
    nk = K // tk

    def body(a_ref, b_ref, o_ref, acc):
        k = pl.program_id(2)

        @pl.when(k == 0)
        def _():
            acc[...] = jnp.zeros_like(acc)

        acc[...] += _dot(a_ref[...], b_ref[...], dims)

        @pl.when(k == nk - 1)
        def _():
            o_ref[...] = acc[...].astype(out_dtype)

    a_spec =(pl.BlockSpec((tk, tm), lambda i, j, k: (k, i)) if dims == "tn"
              else pl.BlockSpec((tm, tk), lambda i, j, k: (i, k)))
    if S == 1:
        b_spec = (pl.BlockSpec((tn, tk), lambda i, j, k: (j, k)) if dims == "nt"
                  else pl.BlockSpec((tk, tn), lambda i, j, k: (k, j)))
    elif dims == "nt":
        per = cs // tk
        b_spec = pl.BlockSpec((None, tn, tk), lambda i, j, k: (k // per, j, k % per))
    else:
        per = cs // tn
        b_spec = pl.BlockSpec((None, tk, tn), lambda i, j, k: (j // per, k, j % per))
    if out_shards == 1:
        out_spec, out_shape = pl.BlockSpec((tm, tn), lambda i, j, k: (i, j)), (M, N)
    else:
        oper = N // out_shards // tn
        out_spec = pl.BlockSpec((None, tm, tn), lambda i, j, k: (j // oper, i, j % oper))
        out_shape = (out_shards, M, N // out_shards)
    return pl.pallas_call(
        body, name=name, grid=(M // tm, N // tn, nk),
        in_specs=[a_spec, b_spec], out_specs=out_spec,
        out_shape=jax.ShapeDtypeStruct(out_shape, out_dtype),
        scratch_shapes=[pltpu.VMEM((tm, tn), F32)],
        compiler_params=_cparams(("parallel", "parallel", "arbitrary"), VMEM_LIMIT),
    )(a, b)


VMEM_LIMIT_SUM = 56 * 1024 * 1024


def _mm_sum(pairs, name, out_dtype=F32, tm=768, tn=1024, tk=1024):
    M, N = pairs[0][0].shape[0], pairs[0][1].shape[1]
    tm, tn = _tile(M, tm), _tile(N, tn)
    tks = [_tile(a.shape[1], tk) for a, _ in pairs]
    nks = [a.shape[1] // t for (a, _), t in zip(pairs, tks)]
    starts = [sum(nks[:g]) for g in range(len(pairs))]
    nk = sum(nks)
    G = len(pairs)

    def body(*refs):
        o_ref, acc = refs[2 * G], refs[2 * G + 1]
        k = pl.program_id(2)

        @pl.when(k == 0)
        def _():
            acc[...] = jnp.zeros_like(acc)

        for g in range(G):
            @pl.when(jnp.logical_and(k >= starts[g], k < starts[g] + nks[g]))
            def _(g=g):
                acc[...] += _dot(refs[2 * g][...], refs[2 * g + 1][...])

        @pl.when(k == nk - 1)
        def _():
            o_ref[...] = acc[...].astype(out_dtype)

    in_specs, args = [], []
    for g, (a, b) in enumerate(pairs):
        kk = lambda k, g=g: jnp.clip(k - starts[g], 0, nks[g] - 1)
        in_specs += [pl.BlockSpec((tm, tks[g]), lambda i, j, k, kk=kk: (i, kk(k))),
                     pl.BlockSpec((tks[g], tn), lambda i, j, k, kk=kk: (kk(k), j))]
        args += [a, b]
    return pl.pallas_call(
        body, name=name, grid=(M // tm, N // tn, nk),
        in_specs=in_specs, out_specs=pl.BlockSpec((tm, tn), lambda i, j, k: (i, j)),
        out_shape=jax.ShapeDtypeStruct((M, N), out_dtype),
        scratch_shapes=[pltpu.VMEM((tm, tn), F32)],
        compiler_params=_cparams(("parallel", "parallel", "arbitrary"), VMEM_LIMIT_SUM),
    )(*args)


def _ada_fwd(ccp, w, b):
    def body(c_ref, w_ref, b_ref, o_ref):
        o_ref[...] = _dot(_silu(c_ref[...]), w_ref[...]) + b_ref[...]

    return pl.pallas_call(
        body, name="ada_fwd", out_shape=jax.ShapeDtypeStruct((ccp.shape[0], w.shape[1]), F32),
        compiler_params=_cparams(None, VMEM_LIMIT))(ccp, w, b)


def _ada_bwd(ccp, w, g, cctx, n_loc):
    def body(c_ref, w_ref, g_ref, cc_ref, gw_ref, dc_ref):
        gw_ref[...] = _dot(_silu(c_ref[...]), g_ref[...], "tn")
        dcc = _dot(g_ref[...], w_ref[...], "nt")
        row = lax.broadcasted_iota(jnp.int32, dcc.shape, 0)
        part = jnp.sum(jnp.where(row % SUBLANES == n_loc, dcc, 0.0), axis=0, keepdims=True)
        dc_ref[...] = jnp.broadcast_to(part * _dsilu(cc_ref[...]), dc_ref.shape)

    D = ccp.shape[1]
    return pl.pallas_call(
        body, name="ada_bwd",
        out_shape=(jax.ShapeDtypeStruct(w.shape, F32), jax.ShapeDtypeStruct((SUBLANES, D), F32)),
        compiler_params=_cparams(None, VMEM_LIMIT))(ccp, w, g, cctx)


def _norm_mod_fwd(h, gain, mods, i_shift, i_scale, nct, tm, name):
    B, T, D = h.shape

    def body(h_ref, g_ref, m_ref, u_ref):
        x = h_ref[0]
        r = lax.rsqrt(jnp.mean(x * x, axis=-1, keepdims=True) + EPS)
        n = x * r * g_ref[...]
        u_ref[0] = (n * (1.0 + m_ref[0, i_scale:i_scale + 1, :]) + m_ref[0, i_shift:i_shift + 1, :]).astype(BF16)

    return pl.pallas_call(
        body, name=name, grid=(B, T // tm),
        in_specs=[pl.BlockSpec((1, tm, D), lambda b, t: (b, t, 0)),
                  pl.BlockSpec((1, D), lambda b, t: (0, 0)),
                  pl.BlockSpec((1, SUBLANES, D), lambda b, t: (jnp.where(t < nct, B, b), 0, 0))],
        out_specs=pl.BlockSpec((1, tm, D), lambda b, t: (b, t, 0)),
        out_shape=jax.ShapeDtypeStruct((B, T, D), BF16),
        compiler_params=_cparams(("parallel", "parallel"), VMEM_LIMIT),
    )(h, gain, mods)


def _norm_mod_bwd(h, du, gain, mods, i_scale, nct, tm, res, name):
    B, T, D = h.shape
    nt = T // tm

    def body(h_ref, du_ref, g_ref, m_ref, res_ref, dx_ref, ax_ref, ac_ref):
        t = pl.program_id(1)
        x = h_ref[0]
        r = lax.rsqrt(jnp.mean(x * x, axis=-1, keepdims=True) + EPS)
        nh = x * r
        g = g_ref[...]
        du_ = du_ref[0]
        dn = du_ * (1.0 + m_ref[0, i_scale:i_scale + 1, :])
        dnh = dn * g
        dx = r * (dnh - nh * jnp.mean(dnh * nh, axis=-1, keepdims=True))
        sums = (jnp.sum(du_, axis=0, keepdims=True), jnp.sum(du_ * nh * g, axis=0, keepdims=True),
                jnp.sum(dn * nh, axis=0, keepdims=True))

        @pl.when(t == 0)
        def _():
            ax_ref[...] = jnp.zeros_like(ax_ref)
            ac_ref[...] = jnp.zeros_like(ac_ref)

        def accumulate(ref):
            for i, s in enumerate(sums):
                ref[0, i:i + 1, :] += s

        @pl.when(t < nct)
        def _():
            accumulate(ac_ref)

        @pl.when(t >= nct)
        def _():
            accumulate(ax_ref)
            dx_ref[0] = dx + res_ref[0]

    lat = lambda b, t: (b, jnp.maximum(t - nct, 0), 0)
    acc = pl.BlockSpec((1, SUBLANES, D), lambda b, t: (b, 0, 0))
    return pl.pallas_call(
        body, name=name, grid=(B, nt),
        in_specs=[pl.BlockSpec((1, tm, D), lambda b, t: (b, t, 0)),
                  pl.BlockSpec((1, tm, D), lambda b, t: (b, t, 0)),
                  pl.BlockSpec((1, D), lambda b, t: (0, 0)),
                  pl.BlockSpec((1, SUBLANES, D), lambda b, t: (jnp.where(t < nct, B, b), 0, 0)),
                  pl.BlockSpec((1, tm, D), lat)],
        out_specs=(pl.BlockSpec((1, tm, D), lat), acc, acc),
        out_shape=(jax.ShapeDtypeStruct(res.shape, F32),
                   jax.ShapeDtypeStruct((B, SUBLANES, D), F32), jax.ShapeDtypeStruct((B, SUBLANES, D), F32)),
        compiler_params=_cparams(("parallel", "arbitrary"), VMEM_LIMIT),
    )(h, du, gain, mods, res)


def _resid_norm_fwd(x, y, gain, mods, i_gate, tm, name):
    B, N, D = x.shape

    def body(x_ref, y_ref, g_ref, m_ref, o_ref):
        y_ = y_ref[0]
        r = lax.rsqrt(jnp.mean(y_ * y_, axis=-1, keepdims=True) + EPS)
        o_ref[0] = x_ref[0] + y_ * r * g_ref[...] * m_ref[0, i_gate:i_gate + 1, :]

    row = pl.BlockSpec((1, tm, D), lambda b, t: (b, t, 0))
    return pl.pallas_call(
        body, name=name, grid=(B, N // tm),
        in_specs=[row, row, pl.BlockSpec((1, D), lambda b, t: (0, 0)),
                  pl.BlockSpec((1, SUBLANES, D), lambda b, t: (b, 0, 0))],
        out_specs=row, out_shape=jax.ShapeDtypeStruct((B, N, D), F32),
        compiler_params=_cparams(("parallel", "parallel"), VMEM_LIMIT),
    )(x, y, gain, mods)


def _resid_norm_bwd_math(y_, dh, g, gate):
    r = lax.rsqrt(jnp.mean(y_ * y_, axis=-1, keepdims=True) + EPS)
    nh = y_ * r
    dgate = jnp.sum(dh * nh * g, axis=0, keepdims=True)
    dn = dh * gate
    dgain = jnp.sum(dn * nh, axis=0, keepdims=True)
    dnh = dn * g
    dy = r * (dnh - nh * jnp.mean(dnh * nh, axis=-1, keepdims=True))
    return dy, dgate, dgain


def _resid_norm_bwd(y, dh, gain, mods, i_gate, tm, name):
    B, N, D = y.shape

    def body(y_ref, dh_ref, g_ref, m_ref, dy_ref, acc_ref):
        t = pl.program_id(1)
        dy, dgate, dgain = _resid_norm_bwd_math(y_ref[0], dh_ref[0], g_ref[...], m_ref[0, i_gate:i_gate + 1, :])
        dy_ref[0] = dy.astype(BF16)

        @pl.when(t == 0)
        def _():
            acc_ref[...] = jnp.zeros_like(acc_ref)

        acc_ref[0, 0:1, :] += dgate
        acc_ref[0, 1:2, :] += dgain

    row = pl.BlockSpec((1, tm, D), lambda b, t: (b, t, 0))
    return pl.pallas_call(
        body, name=name, grid=(B, N // tm),
        in_specs=[row, row, pl.BlockSpec((1, D), lambda b, t: (0, 0)),
                  pl.BlockSpec((1, SUBLANES, D), lambda b, t: (b, 0, 0))],
        out_specs=(row, pl.BlockSpec((1, SUBLANES, D), lambda b, t: (b, 0, 0))),
        out_shape=(jax.ShapeDtypeStruct((B, N, D), BF16), jax.ShapeDtypeStruct((B, SUBLANES, D), F32)),
        compiler_params=_cparams(("parallel", "arbitrary"), VMEM_LIMIT),
    )(y, dh, gain, mods)


def _final_fwd_bwd(h1, d, gain, mods, i_gate, tgt, tm):
    B, N, D = h1.shape

    def body(h_ref, d_ref, g_ref, m_ref, t_ref, dd_ref, dh_ref, acc_ref):
        t = pl.program_id(1)
        d_ = d_ref[0]
        g = g_ref[...]
        gate = m_ref[0, i_gate:i_gate + 1, :]
        r = lax.rsqrt(jnp.mean(d_ * d_, axis=-1, keepdims=True) + EPS)
        e = h_ref[0] + d_ * r * g * gate - t_ref[0]
        dh = e * (1.0 / D)
        dh_ref[0] = dh
        dy, dgate, dgain = _resid_norm_bwd_math(d_, dh, g, gate)
        dd_ref[0] = dy.astype(BF16)

        @pl.when(t == 0)
        def _():
            acc_ref[...] = jnp.zeros_like(acc_ref)

        acc_ref[0, 0:1, :] += dgate
        acc_ref[0, 1:2, :] += dgain
        acc_ref[0, 2:3, :] += jnp.sum(e * e, axis=0, keepdims=True)

    row = pl.BlockSpec((1, tm, D), lambda b, t: (b, t, 0))
    return pl.pallas_call(
        body, name="final_fwd_bwd", grid=(B, N // tm),
        in_specs=[row, row, pl.BlockSpec((1, D), lambda b, t: (0, 0)),
                  pl.BlockSpec((1, SUBLANES, D), lambda b, t: (b, 0, 0)), row],
        out_specs=(row, row, pl.BlockSpec((1, SUBLANES, D), lambda b, t: (b, 0, 0))),
        out_shape=(jax.ShapeDtypeStruct((B, N, D), BF16), jax.ShapeDtypeStruct((B, N, D), F32),
                   jax.ShapeDtypeStruct((B, SUBLANES, D), F32)),
        compiler_params=_cparams(("parallel", "arbitrary"), VMEM_LIMIT),
    )(h1, d, gain, mods, tgt)


def _shifted(x, prev, nxt, off, row, tm):
    if off == 0:
        return x
    if off < 0:
        y = pltpu.roll(x, -off, 0)
        for r in range(-off):
            y = jnp.where(row == r, prev[SUBLANES + r + off:SUBLANES + r + off + 1, :], y)
        return y
    y = pltpu.roll(x, tm - off, 0)
    for r in range(off):
        y = jnp.where(row == tm - off + r, nxt[r:r + 1, :], y)
    return y


def _halo_specs(T, tm, tc, cb0, order):
    r8, n8 = tm // SUBLANES, T // SUBLANES
    if order == "btj":
        prev = lambda b, t, j: (b, jnp.maximum(t * r8 - 1, 0), cb0 + j)
        nxt = lambda b, t, j: (b, jnp.minimum((t + 1) * r8, n8 - 1), cb0 + j)
    else:
        prev = lambda b, j, t: (b, jnp.maximum(t * r8 - 1, 0), cb0 + j)
        nxt = lambda b, j, t: (b, jnp.minimum((t + 1) * r8, n8 - 1), cb0 + j)
    return pl.BlockSpec((1, SUBLANES, tc), prev), pl.BlockSpec((1, SUBLANES, tc), nxt)


def _seg_halos(p_ref, n_ref, t, nct, nt):
    seg_start = jnp.logical_or(t == 0, t == nct)
    seg_end = jnp.logical_or(t == nct - 1, t == nt - 1)
    prev = jnp.where(seg_start, 0.0, p_ref[0])
    nxt = jnp.where(seg_end, 0.0, n_ref[0])
    return prev, nxt


def _dwconv(x, col0, C, w, bias, offs, nct, tm, tc, name, out_dtype=F32, qkv_heads=None):
    B, T, _ = x.shape
    nt = T // tm
    cb0 = col0 // tc
    if qkv_heads is not None:
        assert tc == qkv_heads[0] * qkv_heads[1] and C == 3 * tc

    def body(*refs):
        refs = list(refs)
        a_ref = refs.pop() if qkv_heads is not None else None
        if bias is None:
            x_ref, p_ref, n_ref, w_ref, o_ref = refs
        else:
            x_ref, p_ref, n_ref, w_ref, b_ref, o_ref = refs
        t = pl.program_id(1)
        prev, nxt = _seg_halos(p_ref, n_ref, t, nct, nt)
        x_ = x_ref[0]
        row = lax.broadcasted_iota(jnp.int32, x_.shape, 0)
        acc = None
        for j, off in enumerate(offs):
            term = _shifted(x_, prev, nxt, off, row, tm) * w_ref[j:j + 1, :]
            acc = term if acc is None else acc + term
        if bias is not None:
            acc = acc + b_ref[...]
        o_ref[0] = acc.astype(out_dtype)
        if qkv_heads is not None:
            H, hd = qkv_heads
            part = pl.program_id(2)
            for h in range(H):
                sl = slice(h * hd, (h + 1) * hd)
                s = _silu(acc[:, sl])
                rs = lax.rsqrt(jnp.sum(s * s, axis=-1, keepdims=True) + EPS)
                scale = jnp.where(part == 0, rs * (float(hd) ** -0.5), jnp.where(part == 1, rs, 1.0))
                a_ref[0, :, sl] = (s * scale).astype(a_ref.dtype)

    pspec, nspec = _halo_specs(T, tm, tc, cb0, "btj")
    in_specs = [pl.BlockSpec((1, tm, tc), lambda b, t, j: (b, t, cb0 + j)), pspec, nspec,
                pl.BlockSpec((w.shape[0], tc), lambda b, t, j: (0, j))]
    args = [x, x, x, w]
    if bias is not None:
        in_specs.append(pl.BlockSpec((1, tc), lambda b, t, j: (0, j)))
        args.append(bias)
    tile = pl.BlockSpec((1, tm, tc), lambda b, t, j: (b, t, j))
    out_specs, out_shape = tile, jax.ShapeDtypeStruct((B, T, C), out_dtype)
    if qkv_heads is not None:
        out_specs, out_shape = (tile, tile), (out_shape, jax.ShapeDtypeStruct((B, T, C), BF16))
    return pl.pallas_call(
        body, name=name, grid=(B, nt, C // tc),
        in_specs=in_specs, out_specs=out_specs, out_shape=out_shape,
        compiler_params=_cparams(("parallel", "parallel", "parallel"), VMEM_LIMIT),
    )(*args)


def _dwconv_wgrad(dy, x, col0, C, offs, nct, tm, tc, name):
    B, T, _ = x.shape
    nt = T // tm
    cb0 = col0 // tc
    K = len(offs)

    def body(dy_ref, x_ref, p_ref, n_ref, acc_ref):
        t = pl.program_id(2)
        prev, nxt = _seg_halos(p_ref, n_ref, t, nct, nt)
        x_ = x_ref[0]
        dy_ = dy_ref[0]
        row = lax.broadcasted_iota(jnp.int32, x_.shape, 0)

        @pl.when(t == 0)
        def _():
            acc_ref[...] = jnp.zeros_like(acc_ref)

        for j, off in enumerate(offs):
            acc_ref[0, j:j + 1, :] += jnp.sum(dy_ * _shifted(x_, prev, nxt, off, row, tm), axis=0, keepdims=True)
        acc_ref[0, K:K + 1, :] += jnp.sum(dy_, axis=0, keepdims=True)

    pspec, nspec = _halo_specs(T, tm, tc, cb0, "bjt")
    return pl.pallas_call(
        body, name=name, grid=(B, C // tc, nt),
        in_specs=[pl.BlockSpec((1, tm, tc), lambda b, j, t: (b, t, j)),
                  pl.BlockSpec((1, tm, tc), lambda b, j, t: (b, t, cb0 + j)), pspec, nspec],
        out_specs=pl.BlockSpec((1, SUBLANES, tc), lambda b, j, t: (b, 0, j)),
        out_shape=jax.ShapeDtypeStruct((B, SUBLANES, C), F32),
        compiler_params=_cparams(("parallel", "parallel", "arbitrary"), VMEM_LIMIT),
    )(dy, x, x, x)


def _ab_act(pab, alog, dtb, nd, tm):
    R = pab.shape[0]

    def body(p_ref, al_ref, dt_ref, o_ref):
        p = p_ref[...]
        lane = lax.broadcasted_iota(jnp.int32, p.shape, 1)
        g = -jnp.exp(al_ref[...]) * _softplus(p + dt_ref[...])
        o_ref[...] = jnp.where(lane < nd, g, jnp.where(lane < 2 * nd, _sigmoid(p), 0.0))

    row = pl.BlockSpec((tm, LANES), lambda i: (i, 0))
    vec = pl.BlockSpec((1, LANES), lambda i: (0, 0))
    return pl.pallas_call(
        body, name="ab_act", grid=(R // tm,), in_specs=[row, vec, vec], out_specs=row,
        out_shape=jax.ShapeDtypeStruct((R, LANES), F32),
        compiler_params=_cparams(("parallel",), VMEM_LIMIT))(pab, alog, dtb)


def _ab_act_bwd(pab, gb, dgb, alog, dtb, nd, tm):
    R = pab.shape[0]

    def body(p_ref, gb_ref, d0_ref, d1_ref, al_ref, dt_ref, o_ref, acc_ref):
        i = pl.program_id(0)
        p = p_ref[...]
        gb_ = gb_ref[...]
        d_ = d0_ref[...] + d1_ref[...]
        lane = lax.broadcasted_iota(jnp.int32, p.shape, 1)
        is_g = lane < nd
        is_b = jnp.logical_and(lane >= nd, lane < 2 * nd)
        da = jnp.where(is_g, d_ * (-jnp.exp(al_ref[...])) * _sigmoid(p + dt_ref[...]), 0.0)
        db = jnp.where(is_b, d_ * gb_ * (1.0 - gb_), 0.0)
        o_ref[...] = (da + db).astype(BF16)

        @pl.when(i == 0)
        def _():
            acc_ref[...] = jnp.zeros_like(acc_ref)

        acc_ref[0:1, :] += jnp.sum(jnp.where(is_g, d_ * gb_, 0.0), axis=0, keepdims=True)
        acc_ref[1:2, :] += jnp.sum(da, axis=0, keepdims=True)

    row = pl.BlockSpec((tm, LANES), lambda i: (i, 0))
    vec = pl.BlockSpec((1, LANES), lambda i: (0, 0))
    return pl.pallas_call(
        body, name="ab_act_bwd", grid=(R // tm,),
        in_specs=[row, row, row, row, vec, vec],
        out_specs=(row, pl.BlockSpec((SUBLANES, LANES), lambda i: (0, 0))),
        out_shape=(jax.ShapeDtypeStruct((R, LANES), BF16), jax.ShapeDtypeStruct((SUBLANES, LANES), F32)),
        compiler_params=_cparams(("arbitrary",), VMEM_LIMIT))(pab, gb, dgb[0], dgb[1], alog, dtb)


def _dn_act_bwd(cv, dqkv, x, offs, H, hd, nct, tm):
    B, T, C3 = cv.shape
    HD = H * hd
    nt = T // tm
    r8, n8 = tm // SUBLANES, T // SUBLANES
    qscale = float(hd) ** -0.5

    def body(c_ref, d0_ref, d1_ref, x_ref, p_ref, n_ref, o_ref, acc_ref):
        part = pl.program_id(0)
        t = pl.program_id(2)
        for h in range(H):
            sl = slice(h * hd, (h + 1) * hd)
            c = c_ref[0, :, sl]
            s = _silu(c)
            dout = d0_ref[0, :, sl] + d1_ref[0, :, sl]
            rs = lax.rsqrt(jnp.sum(s * s, axis=-1, keepdims=True) + EPS)
            sh = s * rs
            dnorm = rs * (dout - sh * jnp.sum(dout * sh, axis=-1, keepdims=True))
            ds = jnp.where(part == 0, dnorm * qscale, jnp.where(part == 1, dnorm, dout))
            o_ref[0, :, sl] = ds * _dsilu(c)

        @pl.when(t == 0)
        def _():
            acc_ref[...] = jnp.zeros_like(acc_ref)

        prev, nxt = _seg_halos(p_ref, n_ref, t, nct, nt)
        x_ = x_ref[0]
        dcv_ = o_ref[0]
        row = lax.broadcasted_iota(jnp.int32, x_.shape, 0)
        for j, off in enumerate(offs):
            acc_ref[0, j:j + 1, :] += jnp.sum(dcv_ * _shifted(x_, prev, nxt, off, row, tm), axis=0, keepdims=True)

    spec = pl.BlockSpec((1, tm, HD), lambda p, b, t: (b, t, p))
    halo_p = pl.BlockSpec((1, SUBLANES, HD), lambda p, b, t: (b, jnp.maximum(t * r8 - 1, 0), p))
    halo_n = pl.BlockSpec((1, SUBLANES, HD), lambda p, b, t: (b, jnp.minimum((t + 1) * r8, n8 - 1), p))
    return pl.pallas_call(
        body, name="dn_act_bwd", grid=(3, B, nt),
        in_specs=[spec, spec, spec, spec, halo_p, halo_n],
        out_specs=(spec, pl.BlockSpec((1, SUBLANES, HD), lambda p, b, t: (b, 0, p))),
        out_shape=(jax.ShapeDtypeStruct((B, T, C3), F32), jax.ShapeDtypeStruct((B, SUBLANES, C3), F32)),
        compiler_params=_cparams(("parallel", "parallel", "arbitrary"), VMEM_LIMIT),
    )(cv, dqkv[0], dqkv[1], x, x, x)


def _chunk_of(d, s, ncc, nch):
    if d == 0:
        return s
    return jnp.where(s < ncc, ncc - 1 - s, nch - 1 - (s - ncc))


def _delta_masks(d):
    row = lax.broadcasted_iota(jnp.int32, (CHUNK, CHUNK), 0)
    col = lax.broadcasted_iota(jnp.int32, (CHUNK, CHUNK), 1)
    sign = 1 - 2 * d
    diff = (row - col) * sign
    return diff >= 0, diff > 0, diff <= 0


def _each(f, *lists):
    return [f(*a) for a in zip(*lists)]


def _unit_tri_inverse(As):
    C = As[0].shape[0]
    row = lax.broadcasted_iota(jnp.int32, (C, C), 0)
    col = lax.broadcasted_iota(jnp.int32, (C, C), 1)
    eye = jnp.where(row == col, 1.0, 0.0).astype(F32)
    same = lambda shift: jnp.right_shift(row, shift) == jnp.right_shift(col, shift)
    in8 = same(3)
    xs = [-jnp.where(in8, a, 0.0) for a in As]
    ts = [eye + x for x in xs]
    ps = _each(lambda x: _dot(x, x), xs)
    tp = _each(lambda t, p: _dot(_rows(t, p), p), ts, ps)
    ts = _each(lambda t, m: t + m[:C], ts, tp)
    ts = _each(lambda t, m: t + _dot(t, m[C:]), ts, tp)
    shift = 3
    while (1 << shift) < C:
        off = jnp.logical_and(same(shift + 1), jnp.right_shift(row, shift) != jnp.right_shift(col, shift))
        los = [jnp.where(off, a, 0.0) for a in As]
        ts = _each(lambda t, lo: t - _dot(t, _dot(lo, t)), ts, los)
        shift += 1
    def residual(a, t):
        (ah, al), (th, tl) = _split2(eye + a), _split2(t)
        m = _dot(_rows(ah, al), th)
        return eye - (m[:C] + (m[C:] + _dot(ah, tl)))

    rs = _each(residual, As, ts)
    return _each(lambda t, r: t + _dot(t, r), ts, rs)


def _rows(*xs):
    return jnp.concatenate(xs, axis=0)


def _cols(*xs):
    return jnp.concatenate(xs, axis=1)


def _delta_chunk_fwd(q, k, v, g_i, g_j, beta_i, gl, S, incl, strict, Tm=None):
    D_ = _each(lambda gi, gj, m: jnp.where(m, jnp.exp(jnp.where(m, gi - gj, 0.0)), 0.0), g_i, g_j, incl)
    C, dk = k[0].shape
    kb = _each(lambda k_, b: k_ * b, k, beta_i)
    kq = _each(lambda kb_, q_, k_: _dot(_rows(kb_, q_), k_, "nt"), kb, q, k)
    A = _each(lambda m_, d, m: jnp.where(m, m_[:C] * d, 0.0), kq, D_, strict)
    P = _each(lambda m_, d, m: jnp.where(m, m_[C:] * d, 0.0), kq, D_, incl)
    if Tm is None:
        Tm = _unit_tri_inverse(A)
    gam = _each(jnp.exp, g_i)
    wu = _each(lambda t, kb_, g, v_, b: _dot(t, _cols(kb_ * g, v_ * b)), Tm, kb, gam, v, beta_i)
    w = [m_[:, :dk] for m_ in wu]
    u = [m_[:, dk:] for m_ in wu]
    qg = _each(lambda q_, g: q_ * g, q, gam)
    ws = _each(lambda w_, qg_, s: _dot(_rows(w_, qg_), s), w, qg, S)
    vn = _each(lambda u_, m_: u_ - m_[:C], u, ws)
    o = _each(lambda m_, p, vn_: m_[C:] + _dot(p, vn_), ws, P, vn)
    e_i = _each(lambda gl_, gi: jnp.exp(gl_ - gi), gl, g_i)
    kd = _each(lambda k_, e: k_ * e, k, e_i)
    Gam = _each(jnp.exp, gl)
    S_new = _each(lambda s, g, kd_, vn_: s * g + _dot(kd_, vn_, "tn"), S, Gam, kd, vn)
    return dict(D=D_, kb=kb, A=A, Tm=Tm, gam=gam, w=w, u=u, vn=vn, P=P, qg=qg, o=o, e=e_i, kd=kd, Gam=Gam, S_new=S_new)


def _delta_gb(gb_ref, d, incl, incl_t, H):
    gb = gb_ref[0]
    g = gb[:, d * H:(d + 1) * H]
    beta = gb[:, (2 + d) * H:(3 + d) * H]
    gc_col = _dot_mask(incl.astype(BF16), g)
    gc_row = _dot_mask(incl_t.astype(BF16), g, "xtn")
    gl = jnp.sum(g, axis=0, keepdims=True)
    return beta, gc_col, gc_row, gl


def _delta_fwd(qkvn, gb, H, hd, ncc):
    B, T, _ = qkvn.shape
    nch = T // CHUNK
    HD = H * hd
    pairs = [(d, h) for d in range(2) for h in range(H)]

    def body(q0, k0, v0, gb0, q1, k1, v1, gb1, o0, o1, sin0, sin1, tm0, tm1, S_ref):
        s = pl.program_id(1)
        q_refs, k_refs, v_refs, gb_refs = (q0, q1), (k0, k1), (v0, v1), (gb0, gb1)
        o_refs, sin_refs, tm_refs = (o0, o1), (sin0, sin1), (tm0, tm1)

        @pl.when(s == 0)
        def _():
            S_ref[...] = jnp.zeros_like(S_ref)

        masks = [_delta_masks(d) for d in range(2)]
        gbs = [_delta_gb(gb_refs[d], d, masks[d][0], masks[d][2], H) for d in range(2)]
        head = lambda ref, h: ref[0, :, h * hd:(h + 1) * hd]
        S = [S_ref[d, h] for d, h in pairs]
        r = _delta_chunk_fwd([head(q_refs[d], h) for d, h in pairs], [head(k_refs[d], h) for d, h in pairs],
                             [head(v_refs[d], h) for d, h in pairs],
                             [gbs[d][1][:, h:h + 1] for d, h in pairs], [gbs[d][2][h:h + 1, :] for d, h in pairs],
                             [gbs[d][0][:, h:h + 1] for d, h in pairs], [gbs[d][3][:, h:h + 1] for d, h in pairs],
                             S, [masks[d][0] for d, h in pairs], [masks[d][1] for d, h in pairs])
        for i, (d, h) in enumerate(pairs):
            sin_refs[d][0, 0, h] = S[i]
            tm_refs[d][0, 0, h] = r["Tm"][i]
            S_ref[d, h] = r["S_new"][i]
            o_refs[d][0, :, h * hd:(h + 1) * hd] = r["o"][i]

    def dir_specs(d):
        cidx = lambda b, s: _chunk_of(d, s, ncc, nch)
        ins = [pl.BlockSpec((1, CHUNK, HD), lambda b, s, p=p: (b, cidx(b, s), p)) for p in range(3)]
        ins.append(pl.BlockSpec((1, CHUNK, LANES), lambda b, s: (b, cidx(b, s), 0)))
        return (ins, pl.BlockSpec((1, CHUNK, HD), lambda b, s: (b, cidx(b, s), 0)),
                pl.BlockSpec((1, 1, H, hd, hd), lambda b, s: (b, cidx(b, s), 0, 0, 0)),
                pl.BlockSpec((1, 1, H, CHUNK, CHUNK), lambda b, s: (b, cidx(b, s), 0, 0, 0)))

    (in0, o_s0, sin_s0, tm_s0), (in1, o_s1, sin_s1, tm_s1) = dir_specs(0), dir_specs(1)
    o_shape = jax.ShapeDtypeStruct((B, T, HD), F32)
    sin_shape = jax.ShapeDtypeStruct((B, nch, H, hd, hd), F32)
    tm_shape = jax.ShapeDtypeStruct((B, nch, H, CHUNK, CHUNK), F32)
    return pl.pallas_call(
        body, name="delta_fwd", grid=(B, nch),
        in_specs=in0 + in1, out_specs=(o_s0, o_s1, sin_s0, sin_s1, tm_s0, tm_s1),
        out_shape=(o_shape, o_shape, sin_shape, sin_shape, tm_shape, tm_shape),
        scratch_shapes=[pltpu.VMEM((2, H, hd, hd), F32)],
        compiler_params=_cparams(("parallel", "arbitrary"), VMEM_LIMIT),
    )(qkvn, qkvn, qkvn, gb, qkvn, qkvn, qkvn, gb)


def _delta_bwd(qkvn, gb, sin, tms, do, H, hd, ncc):
    B, T, _ = qkvn.shape
    nch = T // CHUNK
    HD = H * hd
    pairs = [(d, h) for d in range(2) for h in range(H)]

    def body(q0, k0, v0, gb0, sin0, tm0, do0, q1, k1, v1, gb1, sin1, tm1, do1, dqkv0, dqkv1, dgb0, dgb1, dS_ref):
        s = pl.program_id(1)
        tm_refs = (tm0, tm1)
        q_refs, k_refs, v_refs, gb_refs = (q0, q1), (k0, k1), (v0, v1), (gb0, gb1)
        sin_refs, do_refs, dqkv_refs, dgb_refs = (sin0, sin1), (do0, do1), (dqkv0, dqkv1), (dgb0, dgb1)

        @pl.when(s == 0)
        def _():
            dS_ref[...] = jnp.zeros_like(dS_ref)

        masks = [_delta_masks(d) for d in range(2)]
        gbs = [_delta_gb(gb_refs[d], d, masks[d][0], masks[d][2], H) for d in range(2)]
        incl = [masks[d][0] for d, h in pairs]
        strict = [masks[d][1] for d, h in pairs]
        lane = lax.broadcasted_iota(jnp.int32, (1, LANES), 1)
        ones = jnp.ones((3 * CHUNK, LANES), BF16)
        head = lambda ref, h: ref[0, :, h * hd:(h + 1) * hd]
        q = [head(q_refs[d], h) for d, h in pairs]
        k = [head(k_refs[d], h) for d, h in pairs]
        v = [head(v_refs[d], h) for d, h in pairs]
        do_ = [head(do_refs[d], h) for d, h in pairs]
        beta_i = [gbs[d][0][:, h:h + 1] for d, h in pairs]
        S = [sin_refs[d][0, 0, h] for d, h in pairs]
        dSn = [dS_ref[d, h] for d, h in pairs]
        f = _delta_chunk_fwd(q, k, v, [gbs[d][1][:, h:h + 1] for d, h in pairs], [gbs[d][2][h:h + 1, :] for d, h in pairs],
                             beta_i, [gbs[d][3][:, h:h + 1] for d, h in pairs], S, incl, strict,
                             Tm=[tm_refs[d][0, 0, h] for d, h in pairs])
        rsum = lambda x: jnp.sum(x, axis=1, keepdims=True)
        dvn = _each(lambda p, d_, kd, ds: _dot(p, d_, "tn") + _dot(kd, ds), f["P"], do_, f["kd"], dSn)
        dP = _each(lambda d_, vn, m: jnp.where(m, _dot(d_, vn, "nt"), 0.0), do_, f["vn"], incl)
        dqg = _each(lambda d_, s: _dot(d_, s, "nt"), do_, S)
        dS_in = _each(lambda qg, d_, g, ds, w, dv_: _dot(qg, d_, "tn") + g * ds - _dot(w, dv_, "tn"),
                      f["qg"], do_, f["Gam"], dSn, f["w"], dvn)
        dGam = _each(lambda s, ds: jnp.sum(rsum(s * ds), axis=0, keepdims=True), S, dSn)
        dkd = _each(lambda vn, ds: _dot(vn, ds, "nt"), f["vn"], dSn)
        de = _each(lambda dkd_, k_, e: rsum(dkd_ * k_) * e, dkd, k, f["e"])
        dw = _each(lambda dv_, s: -_dot(dv_, s, "nt"), dvn, S)
        dXw = _each(lambda t, dw_: _dot(t, dw_, "tn"), f["Tm"], dw)
        dXu = _each(lambda t, dv_: _dot(t, dv_, "tn"), f["Tm"], dvn)
        dA = _each(lambda xw, w, xu, u, m: -jnp.where(m, _dot(xw, w, "nt") + _dot(xu, u, "nt"), 0.0),
                   dXw, f["w"], dXu, f["u"], strict)
        dM = _each(lambda a, d_: a * d_, dA, f["D"])
        dN = _each(lambda p, d_: p * d_, dP, f["D"])
        dkb = _each(lambda m, k_, xw, g: _dot(m, k_) + xw * g, dM, k, dXw, f["gam"])
        dq = _each(lambda dqg_, g, n, k_: dqg_ * g + _dot(n, k_), dqg, f["gam"], dN, k)
        dk = _each(lambda dkd_, e, m, kb, n, q_, dkb_, b: dkd_ * e + _dot(m, kb, "tn") + _dot(n, q_, "tn") + dkb_ * b,
                   dkd, f["e"], dM, f["kb"], dN, q, dkb, beta_i)
        dv = _each(lambda xu, b: xu * b, dXu, beta_i)
        dbeta = _each(lambda dkb_, k_, xu, v_: rsum(dkb_ * k_) + rsum(xu * v_), dkb, k, dXu, v)
        E = _each(lambda a, A_, p, P_: a * A_ + p * P_, dA, f["A"], dP, f["P"])

        def colsum(e):
            return _dot(_rows(*_split3(e)), ones, "tn")[:, 0:1]

        dg = _each(lambda e, dqg_, qg, xw, kb, g, de_: rsum(e) - colsum(e) + rsum(dqg_ * qg) + rsum(xw * kb) * g - de_,
                   E, dqg, f["qg"], dXw, f["kb"], f["gam"], de)
        dgl_h = _each(lambda de_, dg_, g: jnp.sum(de_, axis=0, keepdims=True) + dg_ * g, de, dGam, f["Gam"])
        for d in range(2):
            dgc = jnp.zeros((CHUNK, LANES), F32)
            dgl = jnp.zeros((1, LANES), F32)
            for h in range(H):
                i = d * H + h
                g_lane, b_lane = d * H + h, (2 + d) * H + h
                dgc = dgc + dg[i] * (lane == g_lane).astype(F32) + dbeta[i] * (lane == b_lane).astype(F32)
                dgl = dgl + dgl_h[i] * (lane == g_lane).astype(F32)
                dS_ref[d, h] = dS_in[i]
                dqkv_refs[d][0, :, h * hd:(h + 1) * hd] = dq[i]
                dqkv_refs[d][0, :, HD + h * hd:HD + (h + 1) * hd] = dk[i]
                dqkv_refs[d][0, :, 2 * HD + h * hd:2 * HD + (h + 1) * hd] = dv[i]
            draw = _dot_mask(masks[d][0].astype(BF16), dgc, "tn") + dgl
            dgb_refs[d][0] = jnp.where(lane < 2 * H, draw, dgc)

    def dir_specs(d):
        cidx = lambda b, s: _chunk_of(d, nch - 1 - s, ncc, nch)
        ins = [pl.BlockSpec((1, CHUNK, HD), lambda b, s, p=p: (b, cidx(b, s), p)) for p in range(3)]
        ins += [pl.BlockSpec((1, CHUNK, LANES), lambda b, s: (b, cidx(b, s), 0)),
                pl.BlockSpec((1, 1, H, hd, hd), lambda b, s: (b, cidx(b, s), 0, 0, 0)),
                pl.BlockSpec((1, 1, H, CHUNK, CHUNK), lambda b, s: (b, cidx(b, s), 0, 0, 0)),
                pl.BlockSpec((1, CHUNK, HD), lambda b, s: (b, cidx(b, s), 0))]
        return (ins, pl.BlockSpec((1, CHUNK, 3 * HD), lambda b, s: (b, cidx(b, s), 0)),
                pl.BlockSpec((1, CHUNK, LANES), lambda b, s: (b, cidx(b, s), 0)))

    (in0, dq_s0, dg_s0), (in1, dq_s1, dg_s1) = dir_specs(0), dir_specs(1)
    dq_shape = jax.ShapeDtypeStruct((B, T, 3 * HD), F32)
    dg_shape = jax.ShapeDtypeStruct((B, T, LANES), F32)
    return pl.pallas_call(
        body, name="delta_bwd", grid=(B, nch),
        in_specs=in0 + in1, out_specs=(dq_s0, dq_s1, dg_s0, dg_s1),
        out_shape=(dq_shape, dq_shape, dg_shape, dg_shape),
        scratch_shapes=[pltpu.VMEM((2, H, hd, hd), F32)],
        compiler_params=_cparams(("parallel", "arbitrary"), VMEM_LIMIT),
    )(qkvn, qkvn, qkvn, gb, sin[0], tms[0], do, qkvn, qkvn, qkvn, gb, sin[1], tms[1], do)


def _dn_out(o2, pz, zcb0, onorm, H, hd, nct, tm):
    B, T, HD = o2[0].shape
    N = T - nct * tm

    def body(o0_ref, o1_ref, z_ref, g_ref, y_ref):
        for h in range(H):
            sl = slice(h * hd, (h + 1) * hd)
            o = o0_ref[0, :, sl] + o1_ref[0, :, sl]
            r = lax.rsqrt(jnp.mean(o * o, axis=-1, keepdims=True) + EPS)
            y_ref[0, :, sl] = (o * r * g_ref[...] * _silu(z_ref[0, :, sl].astype(F32))).astype(BF16)

    return pl.pallas_call(
        body, name="dn_out", grid=(B, N // tm),
        in_specs=[pl.BlockSpec((1, tm, HD), lambda b, t: (b, t + nct, 0)),
                  pl.BlockSpec((1, tm, HD), lambda b, t: (b, t + nct, 0)),
                  pl.BlockSpec((1, tm, HD), lambda b, t: (b, t + nct, zcb0)),
                  pl.BlockSpec((1, hd), lambda b, t: (0, 0))],
        out_specs=pl.BlockSpec((1, tm, HD), lambda b, t: (b, t, 0)),
        out_shape=jax.ShapeDtypeStruct((B, N, HD), BF16),
        compiler_params=_cparams(("parallel",) * 2, VMEM_LIMIT))(o2[0], o2[1], pz, onorm)


def _dn_out_bwd(o2, pz, zcb0, onorm, dy, H, hd, nct, tm):
    B, T, HD = o2[0].shape
    nt = T // tm

    def body(o0_ref, o1_ref, z_ref, g_ref, dy_ref, do_ref, dz_ref, acc_ref):
        t = pl.program_id(1)

        @pl.when(t == 0)
        def _():
            acc_ref[...] = jnp.zeros_like(acc_ref)

        @pl.when(t < nct)
        def _():
            do_ref[0] = jnp.zeros_like(do_ref[0])
            dz_ref[0] = jnp.zeros_like(dz_ref[0])

        @pl.when(t >= nct)
        def _():
            g = g_ref[...]
            for h in range(H):
                sl = slice(h * hd, (h + 1) * hd)
                o = o0_ref[0, :, sl] + o1_ref[0, :, sl]
                z = z_ref[0, :, sl].astype(F32)
                dy_ = dy_ref[0, :, sl]
                r = lax.rsqrt(jnp.mean(o * o, axis=-1, keepdims=True) + EPS)
                oh = o * r
                dz_ref[0, :, sl] = (dy_ * oh * g * _dsilu(z)).astype(BF16)
                dn = dy_ * _silu(z)
                acc_ref[0, 0:1, :] += jnp.sum(dn * oh, axis=0, keepdims=True)
                doh = dn * g
                do_ref[0, :, sl] = r * (doh - oh * jnp.mean(doh * oh, axis=-1, keepdims=True))

    lat = lambda b, t: (b, jnp.maximum(t - nct, 0), 0)
    row = pl.BlockSpec((1, tm, HD), lambda b, t: (b, t, 0))
    return pl.pallas_call(
        body, name="dn_out_bwd", grid=(B, nt),
        in_specs=[row, row,
                  pl.BlockSpec((1, tm, HD), lambda b, t: (b, t, zcb0)),
                  pl.BlockSpec((1, hd), lambda b, t: (0, 0)),
                  pl.BlockSpec((1, tm, HD), lat)],
        out_specs=(row, row, pl.BlockSpec((1, SUBLANES, hd), lambda b, t: (b, 0, 0))),
        out_shape=(jax.ShapeDtypeStruct((B, T, HD), F32), jax.ShapeDtypeStruct((B, T, HD), BF16),
                   jax.ShapeDtypeStruct((B, SUBLANES, hd), F32)),
        compiler_params=_cparams(("parallel", "arbitrary"), VMEM_LIMIT),
    )(o2[0], o2[1], pz, onorm, dy)


def _lru_gate_math(x, wr, wi, br, bi, lam):
    r = _sigmoid(_dot(x, wr) + br)
    i = _sigmoid(_dot(x, wi) + bi)
    sp = _softplus(-lam)
    la = -LRU_C * r * sp
    a = jnp.exp(la)
    s = jnp.sqrt(-jnp.tanh(la) * (a * a + 1.0))
    return r, i, sp, a, s


def _lru_gates(xc, wbd, bias4, lam, tm, bw):
    B, T, W = xc.shape

    def body(x_ref, w_ref, b_ref, l_ref, a_ref, i_ref):
        x = x_ref[0]
        for d in range(2):
            _, i, _, a, s = _lru_gate_math(x, w_ref[2 * d, 0], w_ref[2 * d + 1, 0],
                                           b_ref[2 * d:2 * d + 1, :], b_ref[2 * d + 1:2 * d + 2, :], l_ref[d:d + 1, :])
            a_ref[d, 0] = a
            i_ref[d, 0] = s * (i * x)

    out = pl.BlockSpec((2, 1, tm, bw), lambda b, t, j: (0, b, t, j))
    return pl.pallas_call(
        body, name="lru_gates", grid=(B, T // tm, W // bw),
        in_specs=[pl.BlockSpec((1, tm, bw), lambda b, t, j: (b, t, j)),
                  pl.BlockSpec((4, 1, bw, bw), lambda b, t, j: (0, j, 0, 0)),
                  pl.BlockSpec((4, bw), lambda b, t, j: (0, j)),
                  pl.BlockSpec((2, bw), lambda b, t, j: (0, j))],
        out_specs=(out, out),
        out_shape=(jax.ShapeDtypeStruct((2, B, T, W), F32), jax.ShapeDtypeStruct((2, B, T, W), F32)),
        compiler_params=_cparams(("parallel",) * 3, VMEM_LIMIT))(xc, wbd, bias4, lam)


def _lru_gates_bwd(xc, wbd, bias4, lam, dinp, da, tm, bw):
    B, T, W = xc.shape
    nt = T // tm

    def body(x_ref, w_ref, b_ref, l_ref, di0_ref, di1_ref, da0_ref, da1_ref, dx_ref, dw_ref, acc_ref):
        di_refs, da_refs = (di0_ref, di1_ref), (da0_ref, da1_ref)
        b_ = pl.program_id(1)
        t = pl.program_id(2)

        @pl.when(jnp.logical_and(b_ == 0, t == 0))
        def _():
            dw_ref[...] = jnp.zeros_like(dw_ref)
            acc_ref[...] = jnp.zeros_like(acc_ref)

        x = x_ref[0]
        dx = jnp.zeros_like(x)
        for d in range(2):
            wr, wi = w_ref[2 * d, 0], w_ref[2 * d + 1, 0]
            lam_ = l_ref[d:d + 1, :]
            r, i, sp, a, s = _lru_gate_math(x, wr, wi, b_ref[2 * d:2 * d + 1, :], b_ref[2 * d + 1:2 * d + 2, :], lam_)
            dinp_ = di_refs[d][0]
            dix = dinp_ * s
            ds = dinp_ * i * x
            dla = da_refs[d][0] * a - ds * (a * a) / s
            dpr = dla * (-LRU_C * sp) * r * (1.0 - r)
            dpi = dix * x * i * (1.0 - i)
            dx = dx + dix * i + _dot(dpr, wr, "nt") + _dot(dpi, wi, "nt")
            dw_ref[2 * d, 0] += _dot(x, dpr, "tn")
            dw_ref[2 * d + 1, 0] += _dot(x, dpi, "tn")
            acc_ref[2 * d:2 * d + 1, :] += jnp.sum(dpr, axis=0, keepdims=True)
            acc_ref[2 * d + 1:2 * d + 2, :] += jnp.sum(dpi, axis=0, keepdims=True)
            acc_ref[4 + d:5 + d, :] += jnp.sum(dla * (-LRU_C * r), axis=0, keepdims=True) * (-_sigmoid(-lam_))
        dx_ref[0] = dx

    tile = pl.BlockSpec((1, tm, bw), lambda j, b, t: (b, t, j))
    return pl.pallas_call(
        body, name="lru_gates_bwd", grid=(W // bw, B, nt),
        in_specs=[pl.BlockSpec((1, tm, bw), lambda j, b, t: (b, t, j)),
                  pl.BlockSpec((4, 1, bw, bw), lambda j, b, t: (0, j, 0, 0)),
                  pl.BlockSpec((4, bw), lambda j, b, t: (0, j)),
                  pl.BlockSpec((2, bw), lambda j, b, t: (0, j)), tile, tile, tile, tile],
        out_specs=(pl.BlockSpec((1, tm, bw), lambda j, b, t: (b, t, j)),
                   pl.BlockSpec((4, 1, bw, bw), lambda j, b, t: (0, j, 0, 0)),
                   pl.BlockSpec((SUBLANES, bw), lambda j, b, t: (0, j))),
        out_shape=(jax.ShapeDtypeStruct((B, T, W), F32), jax.ShapeDtypeStruct(wbd.shape, F32),
                   jax.ShapeDtypeStruct((SUBLANES, W), F32)),
        compiler_params=_cparams(("parallel", "arbitrary", "arbitrary"), VMEM_LIMIT),
    )(xc, wbd, bias4, lam, dinp[0], dinp[1], da[0], da[1])


def _tile_scan(a, x, rev, tm):
    row = lax.broadcasted_iota(jnp.int32, a.shape, 0)
    s = 1
    while s < tm:
        if rev:
            a_sh, x_sh, ok = pltpu.roll(a, tm - s, 0), pltpu.roll(x, tm - s, 0), row < tm - s
        else:
            a_sh, x_sh, ok = pltpu.roll(a, s, 0), pltpu.roll(x, s, 0), row >= s
        x = jnp.where(ok, x + a * x_sh, x)
        a = jnp.where(ok, a * a_sh, a)
        s *= 2
    return a, x


def _scan_tile_of(s, rev, nct, nt):
    if not rev:
        return s
    return jnp.where(s < nct, nct - 1 - s, nt - 1 - (s - nct))


def _lru_scan(a2, x2, d, nct, tm, tc):
    _, B, T, W = a2.shape
    nt = T // tm
    rev = d == 1
    last = 0 if rev else tm - 1

    def body(a_ref, x_ref, h_ref, carry):
        s = pl.program_id(2)

        @pl.when(s == 0)
        def _():
            carry[...] = jnp.zeros_like(carry)

        A, X = _tile_scan(a_ref[0, 0], x_ref[0, 0], rev, tm)
        h = X + A * carry[0:1, :]
        h_ref[0] = h
        carry[...] = jnp.broadcast_to(h[last:last + 1, :], carry.shape)

    tidx = lambda s: _scan_tile_of(s, rev, nct, nt)
    spec = pl.BlockSpec((1, 1, tm, tc), lambda b, j, s: (d, b, tidx(s), j))
    return pl.pallas_call(
        body, name=f"lru_scan{d}", grid=(B, W // tc, nt),
        in_specs=[spec, spec], out_specs=pl.BlockSpec((1, tm, tc), lambda b, j, s: (b, tidx(s), j)),
        out_shape=jax.ShapeDtypeStruct((B, T, W), F32),
        scratch_shapes=[pltpu.VMEM((SUBLANES, tc), F32)],
        compiler_params=_cparams(("parallel", "parallel", "arbitrary"), VMEM_LIMIT),
    )(a2, x2)


def _lru_scan_bwd(a2, h, dh, d, nct, tm, tc):
    _, B, T, W = a2.shape
    nt = T // tm
    rev = d == 1
    first = tm - 1 if rev else 0
    r8, n8 = tm // SUBLANES, T // SUBLANES

    def body(a_ref, h_ref, hp_ref, dh_ref, lam_ref, da_ref, ca, cl):
        s = pl.program_id(2)

        @pl.when(s == 0)
        def _():
            ca[...] = jnp.zeros_like(ca)
            cl[...] = jnp.zeros_like(cl)

        a = a_ref[0, 0]
        row = lax.broadcasted_iota(jnp.int32, a.shape, 0)
        if rev:
            c = jnp.where(row == 0, ca[0:1, :], pltpu.roll(a, 1, 0))
        else:
            c = jnp.where(row == tm - 1, ca[0:1, :], pltpu.roll(a, tm - 1, 0))
        C, L = _tile_scan(c, dh_ref[0], not rev, tm)
        lam = L + C * cl[0:1, :]
        lam_ref[0] = lam
        hp = jnp.where(s == nt - 1, 0.0, hp_ref[0])
        if rev:
            hprev = jnp.where(row == tm - 1, hp[0:1, :], pltpu.roll(h_ref[0], tm - 1, 0))
        else:
            hprev = jnp.where(row == 0, hp[SUBLANES - 1:SUBLANES, :], pltpu.roll(h_ref[0], 1, 0))
        da_ref[0] = lam * hprev
        ca[...] = jnp.broadcast_to(a[first:first + 1, :], ca.shape)
        cl[...] = jnp.broadcast_to(lam[first:first + 1, :], cl.shape)

    tidx = lambda s: _scan_tile_of(nt - 1 - s, rev, nct, nt)

    def hp_idx(b, j, s):
        tt = tidx(s)
        if rev:
            blk = jnp.where(tt == nt - 1, 0, jnp.minimum((tt + 1) * r8, n8 - 1))
        else:
            blk = jnp.maximum(tt * r8 - 1, 0)
        return (b, blk, j)

    tile = pl.BlockSpec((1, tm, tc), lambda b, j, s: (b, tidx(s), j))
    return pl.pallas_call(
        body, name=f"lru_scan_bwd{d}", grid=(B, W // tc, nt),
        in_specs=[pl.BlockSpec((1, 1, tm, tc), lambda b, j, s: (d, b, tidx(s), j)), tile,
                  pl.BlockSpec((1, SUBLANES, tc), hp_idx), tile],
        out_specs=(tile, tile),
        out_shape=(jax.ShapeDtypeStruct((B, T, W), F32), jax.ShapeDtypeStruct((B, T, W), F32)),
        scratch_shapes=[pltpu.VMEM((SUBLANES, tc), F32), pltpu.VMEM((SUBLANES, tc), F32)],
        compiler_params=_cparams(("parallel", "parallel", "arbitrary"), VMEM_LIMIT),
    )(a2, h, h, dh)


def _lru_out(h0, h1, pyl, ycb0, nct, tm, tc):
    B, T, W = h0.shape
    N = T - nct * tm

    def body(h0_ref, h1_ref, y_ref, o_ref):
        o_ref[0] = ((h0_ref[0] + h1_ref[0]) * _gelu(y_ref[0].astype(F32))).astype(BF16)

    hs = pl.BlockSpec((1, tm, tc), lambda b, t, j: (b, t + nct, j))
    return pl.pallas_call(
        body, name="lru_out", grid=(B, N // tm, W // tc),
        in_specs=[hs, hs, pl.BlockSpec((1, tm, tc), lambda b, t, j: (b, t + nct, ycb0 + j))],
        out_specs=pl.BlockSpec((1, tm, tc), lambda b, t, j: (b, t, j)),
        out_shape=jax.ShapeDtypeStruct((B, N, W), BF16),
        compiler_params=_cparams(("parallel",) * 3, VMEM_LIMIT))(h0, h1, pyl)


def _lru_out_bwd(h0, h1, pyl, ycb0, dy, nct, tm, tc):
    B, T, W = h0.shape

    def body(h0_ref, h1_ref, y_ref, dy_ref, dh_ref, dyl_ref):
        t = pl.program_id(1)

        @pl.when(t < nct)
        def _():
            dh_ref[0] = jnp.zeros_like(dh_ref[0])
            dyl_ref[0] = jnp.zeros_like(dyl_ref[0])

        @pl.when(t >= nct)
        def _():
            y = y_ref[0].astype(F32)
            dy_ = dy_ref[0]
            dh_ref[0] = dy_ * _gelu(y)
            dyl_ref[0] = (dy_ * (h0_ref[0] + h1_ref[0]) * _dgelu(y)).astype(BF16)

    hs = pl.BlockSpec((1, tm, tc), lambda b, t, j: (b, t, j))
    return pl.pallas_call(
        body, name="lru_out_bwd", grid=(B, T // tm, W // tc),
        in_specs=[hs, hs, pl.BlockSpec((1, tm, tc), lambda b, t, j: (b, t, ycb0 + j)),
                  pl.BlockSpec((1, tm, tc), lambda b, t, j: (b, jnp.maximum(t - nct, 0), j))],
        out_specs=(hs, hs),
        out_shape=(jax.ShapeDtypeStruct((B, T, W), F32), jax.ShapeDtypeStruct((B, T, W), BF16)),
        compiler_params=_cparams(("parallel",) * 3, VMEM_LIMIT))(h0, h1, pyl, dy)


def _merge(bd, bl, pmg, bm, nct, tm):
    B, N, D = bd.shape

    def body(bd_ref, bl_ref, m0_ref, m1_ref, b_ref, o_ref):
        g0 = _sigmoid(m0_ref[0].astype(F32) + b_ref[:, 0:D])
        g1 = _sigmoid(m1_ref[0].astype(F32) + b_ref[:, D:2 * D])
        o_ref[0] = (g0 * bd_ref[0].astype(F32) + g1 * bl_ref[0].astype(F32)).astype(BF16)

    row = pl.BlockSpec((1, tm, D), lambda b, t: (b, t, 0))
    return pl.pallas_call(
        body, name="merge", grid=(B, N // tm),
        in_specs=[row, row, pl.BlockSpec((1, tm, D), lambda b, t: (b, t + nct, 0)),
                  pl.BlockSpec((1, tm, D), lambda b, t: (b, t + nct, 1)),
                  pl.BlockSpec((1, 2 * D), lambda b, t: (0, 0))],
        out_specs=row, out_shape=jax.ShapeDtypeStruct((B, N, D), BF16),
        compiler_params=_cparams(("parallel", "parallel"), VMEM_LIMIT))(bd, bl, pmg, pmg, bm)


def _merge_bwd(dmi, bd, bl, pmg, bm, nct, tm):
    B, N, D = bd.shape
    T = pmg.shape[1]

    def body(dm_ref, bd_ref, bl_ref, m0_ref, m1_ref, b_ref, dbd_ref, dbl_ref, dmg_ref, acc_ref):
        t = pl.program_id(1)

        @pl.when(t == 0)
        def _():
            acc_ref[...] = jnp.zeros_like(acc_ref)

        @pl.when(t < nct)
        def _():
            dmg_ref[0] = jnp.zeros_like(dmg_ref[0])

        @pl.when(t >= nct)
        def _():
            dm = dm_ref[0]
            g0 = _sigmoid(m0_ref[0].astype(F32) + b_ref[:, 0:D])
            g1 = _sigmoid(m1_ref[0].astype(F32) + b_ref[:, D:2 * D])
            dbd_ref[0] = (dm * g0).astype(BF16)
            dbl_ref[0] = (dm * g1).astype(BF16)
            d0 = dm * bd_ref[0].astype(F32) * g0 * (1.0 - g0)
            d1 = dm * bl_ref[0].astype(F32) * g1 * (1.0 - g1)
            dmg_ref[0, :, 0:D] = d0.astype(BF16)
            dmg_ref[0, :, D:2 * D] = d1.astype(BF16)
            acc_ref[0, 0:1, 0:D] += jnp.sum(d0, axis=0, keepdims=True)
            acc_ref[0, 0:1, D:2 * D] += jnp.sum(d1, axis=0, keepdims=True)

    lat = pl.BlockSpec((1, tm, D), lambda b, t: (b, jnp.maximum(t - nct, 0), 0))
    return pl.pallas_call(
        body, name="merge_bwd", grid=(B, T // tm),
        in_specs=[lat, lat, lat, pl.BlockSpec((1, tm, D), lambda b, t: (b, t, 0)),
                  pl.BlockSpec((1, tm, D), lambda b, t: (b, t, 1)),
                  pl.BlockSpec((1, 2 * D), lambda b, t: (0, 0))],
        out_specs=(lat, lat, pl.BlockSpec((1, tm, 2 * D), lambda b, t: (b, t, 0)),
                   pl.BlockSpec((1, SUBLANES, 2 * D), lambda b, t: (b, 0, 0))),
        out_shape=(jax.ShapeDtypeStruct((B, N, D), BF16), jax.ShapeDtypeStruct((B, N, D), BF16),
                   jax.ShapeDtypeStruct((B, T, 2 * D), BF16), jax.ShapeDtypeStruct((B, SUBLANES, 2 * D), F32)),
        compiler_params=_cparams(("parallel", "arbitrary"), VMEM_LIMIT))(dmi, bd, bl, pmg, pmg, bm)


def _grid_taps():
    return [((di + 1) * 3 + (dj + 1), di, dj) for di in (-1, 0, 1) for dj in (-1, 0, 1)]


def _grid_views(x, n):
    pos = lax.broadcasted_iota(jnp.int32, x.shape, 0)
    gc = jnp.bitwise_and(pos, GRID_W - 1)
    cols = {0: x,
            -1: jnp.where(gc >= 1, pltpu.roll(x, 1, 0), 0.0),
            1: jnp.where(gc <= GRID_W - 2, pltpu.roll(x, n - 1, 0), 0.0)}
    views = {}
    edge = jnp.zeros((GRID_W, x.shape[1]), x.dtype)
    for dj, xc in cols.items():
        views[(0, dj)] = xc
        views[(-1, dj)] = jnp.concatenate([edge, xc[:n - GRID_W]], axis=0)
        views[(1, dj)] = jnp.concatenate([xc[GRID_W:], edge], axis=0)
    return views


def _ffn_act(Fg, Fv, w9, b, tc):
    B, N, Cf = Fg.shape
    ncb = Cf // tc

    def body(g_ref, v_ref, w_ref, b_ref, o_ref, gel_ref, dgel_ref):
        xv = _grid_views(g_ref[0], N)
        conv = b_ref[...]
        for k, di, dj in _grid_taps():
            conv = conv + xv[(di, dj)] * w_ref[k:k + 1, :]
        th = jnp.tanh(GELU_C * (conv + 0.044715 * conv * conv * conv))
        gel = 0.5 * conv * (1.0 + th)
        o_ref[0] = (gel * v_ref[0]).astype(BF16)
        gel_ref[0] = gel.astype(BF16)
        dgel_ref[0] = (0.5 * (1.0 + th)
                       + 0.5 * conv * (1.0 - th * th) * GELU_C * (1.0 + 3.0 * 0.044715 * conv * conv)).astype(BF16)

    tile = pl.BlockSpec((1, N, tc), lambda b, j: (b, 0, j))
    out = jax.ShapeDtypeStruct((B, N, Cf), BF16)
    return pl.pallas_call(
        body, name="ffn_act", grid=(B, ncb),
        in_specs=[tile, tile,
                  pl.BlockSpec((9, tc), lambda b, j: (0, j)),
                  pl.BlockSpec((1, tc), lambda b, j: (0, j))],
        out_specs=(tile, tile, tile), out_shape=(out, out, out),
        compiler_params=_cparams(("parallel", "parallel"), VMEM_LIMIT))(Fg, Fv, w9, b)


def _ffn_act_bwd(Fg, Fv, w9, gel, dgel, df, tc):
    B, N, Cf = Fg.shape
    C2 = 2 * Cf
    ncb = Cf // tc

    def body(g_ref, v_ref, w_ref, gel_ref, dgel_ref, df_ref, dF_ref, acc_ref):
        p = pl.program_id(2)

        @pl.when(p == 0)
        def _():
            dF_ref[0] = (df_ref[0] * gel_ref[0].astype(F32)).astype(BF16)

        @pl.when(p == 1)
        def _():
            xv = _grid_views(g_ref[0], N)
            df_ = df_ref[0]
            dc = df_ * v_ref[0] * dgel_ref[0].astype(F32)
            dcv = _grid_views(dc, N)
            dx = None
            for k, di, dj in _grid_taps():
                term = dcv[(-di, -dj)] * w_ref[k:k + 1, :]
                dx = term if dx is None else dx + term
                acc_ref[0, k:k + 1, :] = jnp.sum(dc * xv[(di, dj)], axis=0, keepdims=True)
            acc_ref[0, 9:10, :] = jnp.sum(dc, axis=0, keepdims=True)
            acc_ref[0, 10:16, :] = jnp.zeros((6, tc), F32)
            dF_ref[0] = dx.astype(BF16)

    tile = pl.BlockSpec((1, N, tc), lambda b, j, p: (b, 0, j))
    return pl.pallas_call(
        body, name="ffn_act_bwd", grid=(B, ncb, 2),
        in_specs=[tile, tile, pl.BlockSpec((9, tc), lambda b, j, p: (0, j)), tile, tile, tile],
        out_specs=(pl.BlockSpec((1, N, tc), lambda b, j, p: (b, 0, j + (1 - p) * ncb)),
                   pl.BlockSpec((1, 16, tc), lambda b, j, p: (b, 0, j))),
        out_shape=(jax.ShapeDtypeStruct((B, N, C2), BF16), jax.ShapeDtypeStruct((B, 16, Cf), F32)),
        compiler_params=_cparams(("parallel", "parallel", "arbitrary"), VMEM_LIMIT))(Fg, Fv, w9, gel, dgel, df)


ADAM_ROWS = 512


def _adamw(w, m, v, gparts, name):
    rows, cols = w.shape[-2:]
    P = gparts.shape[0]
    tr = _row_tile(rows, (P + 14) * 4 * (-(-cols // LANES) * LANES))
    c1 = 1.0 - ADAM_B1 ** ADAM_STEP
    c2 = 1.0 - ADAM_B2 ** ADAM_STEP

    def body(w_ref, m_ref, v_ref, g_ref, go_ref, d_ref, mo_ref, vo_ref):
        g = g_ref[0].astype(F32)
        for p in range(1, P):
            g = g + g_ref[p].astype(F32)
        m_ = ADAM_B1 * m_ref[...] + (1.0 - ADAM_B1) * g
        v_ = ADAM_B2 * v_ref[...] + (1.0 - ADAM_B2) * (g * g)
        go_ref[...] = g
        mo_ref[...] = m_
        vo_ref[...] = v_
        d_ref[...] = -ADAM_LR * ((m_ / c1) / (jnp.sqrt(v_ / c2) + ADAM_EPS) + ADAM_WD * w_ref[...])

    if w.ndim == 3:
        row = pl.BlockSpec((None, tr, cols), lambda i: (0, i, 0))
    else:
        row = pl.BlockSpec((tr, cols), lambda i: (i, 0))
    out = jax.ShapeDtypeStruct(w.shape, F32)
    return pl.pallas_call(
        body, name=name, grid=(rows // tr,),
        in_specs=[row, row, row, pl.BlockSpec((P, tr, cols), lambda i: (0, i, 0))],
        out_specs=(row, row, row, row), out_shape=(out, out, out, out),
        compiler_params=_cparams(("parallel",), VMEM_LIMIT))(w, m, v, gparts)


def _pack_rows(flat_len):
    rows = -(-flat_len // LANES)
    if rows > ADAM_ROWS:
        rows = -(-rows // ADAM_ROWS) * ADAM_ROWS
    else:
        rows = -(-rows // SUBLANES) * SUBLANES
    return rows


PACK_ALIGN = SUBLANES * LANES


def _pack(arrs, lead=()):
    pads = [(0, 0)] * len(lead)
    parts = []
    for a in arrs:
        f = a.reshape(lead + (-1,)).astype(F32)
        parts.append(jnp.pad(f, pads + [(0, -f.shape[-1] % PACK_ALIGN)]))
    flat = jnp.concatenate(parts, axis=-1)
    n = flat.shape[-1]
    rows = _pack_rows(n)
    flat = jnp.pad(flat, pads + [(0, rows * LANES - n)])
    return flat.reshape(lead + (rows, LANES))


def _unpack(buf, shapes):
    out, r = [], 0
    for s in shapes:
        n = math.prod(s)
        nr = -(-n // PACK_ALIGN) * SUBLANES
        out.append(buf[r:r + nr].reshape(-1)[:n].reshape(s))
        r += nr
    return out


def _col_shards(full):
    C = full.shape[-1]
    x = full.reshape(full.shape[:-1] + (N_DEV, C // N_DEV))
    return jnp.moveaxis(x, -2, 0)


def _from_col_shards(g):
    x = jnp.moveaxis(g, 0, -2)
    return x.reshape(x.shape[:-2] + (x.shape[-2] * x.shape[-1],))


def kernel(x, c, ctx, c_ctx, w_ada, b_ada, g_pre_mix, g_post_mix, g_pre_ffn, g_post_ffn, w_in, b_merge, dn_conv, dn_a_log, dn_dt_bias, dn_onorm, lru_conv, lru_conv_b, lru_w_rg, lru_b_rg, lru_w_ig, lru_b_ig, lru_lambda, w_branch_dn, w_branch_lru, w_out, w_up, ffn_dw, ffn_dw_b, w_down, loss_target, m_c_ctx, m_w_ada, m_b_ada, m_g_pre_mix, m_g_post_mix, m_g_pre_ffn, m_g_post_ffn, m_w_in, m_b_merge, m_dn_conv, m_dn_a_log, m_dn_dt_bias, m_dn_onorm, m_lru_conv, m_lru_conv_b, m_lru_w_rg, m_lru_b_rg, m_lru_w_ig, m_lru_b_ig, m_lru_lambda, m_w_branch_dn, m_w_branch_lru, m_w_out, m_w_up, m_ffn_dw, m_ffn_dw_b, m_w_down, v_c_ctx, v_w_ada, v_b_ada, v_g_pre_mix, v_g_post_mix, v_g_pre_ffn, v_g_post_ffn, v_w_in, v_b_merge, v_dn_conv, v_dn_a_log, v_dn_dt_bias, v_dn_onorm, v_lru_conv, v_lru_conv_b, v_lru_w_rg, v_lru_b_rg, v_lru_w_ig, v_lru_b_ig, v_lru_lambda, v_w_branch_dn, v_w_branch_lru, v_w_out, v_w_up, v_ffn_dw, v_ffn_dw_b, v_w_down):
    params = dict(c_ctx=c_ctx, w_ada=w_ada, b_ada=b_ada, g_pre_mix=g_pre_mix, g_post_mix=g_post_mix, g_pre_ffn=g_pre_ffn, g_post_ffn=g_post_ffn, w_in=w_in, b_merge=b_merge, dn_conv=dn_conv, dn_a_log=dn_a_log, dn_dt_bias=dn_dt_bias, dn_onorm=dn_onorm, lru_conv=lru_conv, lru_conv_b=lru_conv_b, lru_w_rg=lru_w_rg, lru_b_rg=lru_b_rg, lru_w_ig=lru_w_ig, lru_b_ig=lru_b_ig, lru_lambda=lru_lambda, w_branch_dn=w_branch_dn, w_branch_lru=w_branch_lru, w_out=w_out, w_up=w_up, ffn_dw=ffn_dw, ffn_dw_b=ffn_dw_b, w_down=w_down)
    mom1 = dict(c_ctx=m_c_ctx, w_ada=m_w_ada, b_ada=m_b_ada, g_pre_mix=m_g_pre_mix, g_post_mix=m_g_post_mix, g_pre_ffn=m_g_pre_ffn, g_post_ffn=m_g_post_ffn, w_in=m_w_in, b_merge=m_b_merge, dn_conv=m_dn_conv, dn_a_log=m_dn_a_log, dn_dt_bias=m_dn_dt_bias, dn_onorm=m_dn_onorm, lru_conv=m_lru_conv, lru_conv_b=m_lru_conv_b, lru_w_rg=m_lru_w_rg, lru_b_rg=m_lru_b_rg, lru_w_ig=m_lru_w_ig, lru_b_ig=m_lru_b_ig, lru_lambda=m_lru_lambda, w_branch_dn=m_w_branch_dn, w_branch_lru=m_w_branch_lru, w_out=m_w_out, w_up=m_w_up, ffn_dw=m_ffn_dw, ffn_dw_b=m_ffn_dw_b, w_down=m_w_down)
    mom2 = dict(c_ctx=v_c_ctx, w_ada=v_w_ada, b_ada=v_b_ada, g_pre_mix=v_g_pre_mix, g_post_mix=v_g_post_mix, g_pre_ffn=v_g_pre_ffn, g_post_ffn=v_g_post_ffn, w_in=v_w_in, b_merge=v_b_merge, dn_conv=v_dn_conv, dn_a_log=v_dn_a_log, dn_dt_bias=v_dn_dt_bias, dn_onorm=v_dn_onorm, lru_conv=v_lru_conv, lru_conv_b=v_lru_conv_b, lru_w_rg=v_lru_w_rg, lru_b_rg=v_lru_b_rg, lru_w_ig=v_lru_w_ig, lru_b_ig=v_lru_b_ig, lru_lambda=v_lru_lambda, w_branch_dn=v_w_branch_dn, w_branch_lru=v_w_branch_lru, w_out=v_w_out, w_up=v_w_up, ffn_dw=v_ffn_dw, ffn_dw_b=v_ffn_dw_b, w_down=v_w_down)
    names = list(params)

    B, N, D = x.shape
    NC = ctx.shape[1]
    T = NC + N
    H, hd = dn_a_log.shape[2], dn_onorm.shape[1]
    HD = H * hd
    nd = 2 * H
    W = lru_conv_b.shape[1]
    nblk, bdim = lru_w_rg.shape[2], lru_w_rg.shape[3]
    Cf = ffn_dw_b.shape[1]
    sw = w_ada.shape[2]
    tm = min(256, NC)
    nct = NC // tm
    ncc = NC // CHUNK
    nch = T // CHUNK
    R, RL = B * T, B * N
    bw = min(256, W)
    me = _my_index()
    assert NC % tm == 0 and N % tm == 0 and B + 1 <= SUBLANES and bw % bdim == 0 and D % LANES == 0

    cpad = jnp.zeros((SUBLANES, D), F32).at[:B].set(c).at[B].set(c_ctx)
    ccp = _all_gather([cpad], "ag_cond")[0].reshape(N_DEV * SUBLANES, D)
    mods_sh = _ada_fwd(ccp, w_ada[0], lax.dynamic_slice(b_ada, (0, me * sw), (1, sw)))
    lru_vecs = jnp.concatenate([lru_b_rg[0], lru_b_ig[0], lru_lambda[0], jnp.zeros_like(lru_lambda[0])], axis=0)
    mods_g, w_in_g, dn_conv_g, lru_conv_g, lru_vecs_g = _all_gather(
        [mods_sh, w_in[0].astype(BF16).T, dn_conv[0], lru_conv[0], lru_vecs], "ag_first")
    ag_mix = _split_start([w_branch_dn[0].astype(BF16), w_branch_lru[0].astype(BF16), w_out[0].astype(BF16)],
                          "gather", "ag_mix_start")
    ag_ffn = _split_start([w_up[0].astype(BF16) + ag_mix[4][0, 0].astype(BF16), w_down[0].astype(BF16),
                           ffn_dw[0].reshape(9, -1)], "gather", "ag_ffn_start")
    mine = lax.dynamic_slice(mods_g, (0, me * SUBLANES, 0), (N_DEV, SUBLANES, sw))
    mods = jnp.moveaxis(mine, 0, 1).reshape(SUBLANES, 6, D)[:B + 1]
    mods = jnp.pad(mods, ((0, 0), (0, SUBLANES - 6), (0, 0))) + ag_ffn[4][0, 0]
    dn_conv_f = _from_col_shards(dn_conv_g)
    lru_conv_f = _from_col_shards(lru_conv_g)
    lru_vecs_f = _from_col_shards(lru_vecs_g)
    lru_lam_f = lru_vecs_f[4:6]

    csh = w_in_g.shape[1]
    w_in_t = w_in_g.reshape(N_DEV * csh, D)
    o_z, o_a, o_xl = 3 * HD, 4 * HD, 4 * HD + 2 * nd
    o_yl, o_mg = o_xl + W, o_xl + 2 * W
    w_qkv, w_z = w_in_t[:o_z], w_in_t[o_z:o_a]
    w_ab = jnp.pad(w_in_t[o_a:o_xl], ((0, LANES - 2 * nd), (0, 0)))
    w_xl, w_yl, w_mg = w_in_t[o_xl:o_yl], w_in_t[o_yl:o_mg], w_in_t[o_mg:]

    def blockdiag(wg):
        per = bw // bdim
        g5 = wg.reshape(2, W // bw, per, bdim, bdim)
        eye = jnp.eye(per, dtype=wg.dtype)
        return jnp.einsum("dtpij,pq->dtpiqj", g5, eye).reshape(2, W // bw, bw, bw)

    bd_rg, bd_ig = blockdiag(lru_w_rg[0]), blockdiag(lru_w_ig[0])
    wbd = jnp.stack([bd_rg[0], bd_ig[0], bd_rg[1], bd_ig[1]]).astype(BF16)
    bias4 = jnp.stack([lru_vecs_f[0], lru_vecs_f[2], lru_vecs_f[1], lru_vecs_f[3]])
    alog_v = jnp.pad(dn_a_log.reshape(1, nd), ((0, 0), (0, LANES - nd)))
    dtb_v = jnp.pad(dn_dt_bias.reshape(1, nd), ((0, 0), (0, LANES - nd)))

    h0 = jnp.concatenate([ctx, x], axis=1)
    u1 = _norm_mod_fwd(h0, g_pre_mix, mods, 0, 1, nct, tm, "norm_mod1")
    u1f = u1.reshape(R, D)
    p_qkv = _mm(u1f, w_qkv, "nt", "mm_qkv").reshape(B, T, 3 * HD)
    p_z = _mm(u1f, w_z, "nt", "mm_z", out_dtype=BF16).reshape(B, T, HD)
    p_ab = _mm(u1f, w_ab, "nt", "mm_ab")
    p_xl = _mm(u1f, w_xl, "nt", "mm_xl").reshape(B, T, W)
    p_yl = _mm(u1f, w_yl, "nt", "mm_yl", out_dtype=BF16).reshape(B, T, W)
    p_mg = _mm(u1f, w_mg, "nt", "mm_mg", out_dtype=BF16).reshape(B, T, 2 * D)

    conv_offs = (-2, -1, 0, 1)
    tcc = _tile(HD, 1024)
    gb = _ab_act(p_ab, alog_v, dtb_v, nd, tm)
    gb3 = gb.reshape(B, T, LANES)
    cv, qkvn = _dwconv(p_qkv, 0, 3 * HD, dn_conv_f, None, conv_offs, nct, tm, HD, "dn_conv", qkv_heads=(H, hd))
    o_d0, o_d1, s_in0, s_in1, tm_d0, tm_d1 = _delta_fwd(qkvn, gb3, H, hd, ncc)
    o2 = (o_d0, o_d1)
    y_dn = _dn_out(o2, p_z, 0, dn_onorm, H, hd, nct, tm)

    tcw = _tile(W, 1024)
    xc = _dwconv(p_xl, 0, W, lru_conv_f, lru_conv_b, conv_offs, nct, tm, tcw, "lru_conv")
    tmg = max(t_ for t_ in range(SUBLANES, min(T, 768) + 1, SUBLANES) if T % t_ == 0)
    a2, inp2 = _lru_gates(xc, wbd, bias4, lru_lam_f, tmg, bw)
    hd0 = _lru_scan(a2, inp2, 0, nct, tm, tcw)
    hd1 = _lru_scan(a2, inp2, 1, nct, tm, tcw)
    y_lru = _lru_out(hd0, hd1, p_yl, 0, nct, tm, tcw)

    def gathered(started, after, name):
        xs_, lands_ = _split_wait(started, "gather", after, name)
        return [lax.dynamic_update_slice(land, x_[None], (me,) + (0,) * x_.ndim) for land, x_ in zip(lands_, xs_)]

    w_bdn_g, w_blru_g, w_out_g = gathered(ag_mix, y_lru, "ag_mix_wait")
    w_bdn_f, w_blru_f, w_out_f = w_bdn_g.reshape(HD, D), w_blru_g.reshape(W, D), w_out_g.reshape(D, D)
    bd = _mm(y_dn.reshape(RL, HD), w_bdn_f, "nn", "mm_bdn", out_dtype=BF16).reshape(B, N, D)
    bl = _mm(y_lru.reshape(RL, W), w_blru_f, "nn", "mm_blru", out_dtype=BF16).reshape(B, N, D)
    mixin = _merge(bd, bl, p_mg, b_merge, nct, tm)
    mix = _mm(mixin.reshape(RL, D), w_out_f, "nn", "mm_out").reshape(B, N, D)
    h1 = _resid_norm_fwd(x, mix, g_post_mix, mods, 2, tm, "resid_norm1")
    u2 = _norm_mod_fwd(h1, g_pre_ffn, mods, 3, 4, 0, tm, "norm_mod2")
    w_up_g, w_down_g, ffn_dw_g = gathered(ag_ffn, u2, "ag_ffn_wait")
    w_down_f = w_down_g.reshape(Cf, D)
    ffn_dw_f = _from_col_shards(ffn_dw_g)
    half = N_DEV // 2
    Fg = _mm(u2.reshape(RL, D), w_up_g[:half], "nn", "mm_up_gate").reshape(B, N, Cf)
    Fv = _mm(u2.reshape(RL, D), w_up_g[half:], "nn", "mm_up_val", out_dtype=BF16).reshape(B, N, Cf)
    tcf = LANES
    f, f_gel, f_dgel = _ffn_act(Fg, Fv, ffn_dw_f, ffn_dw_b, tcf)
    dproj = _mm(f.reshape(RL, Cf), w_down_f, "nn", "mm_down").reshape(B, N, D)

    dd, dh2, acc_f = _final_fwd_bwd(h1, dproj, g_post_ffn, mods, 5, loss_target, tm)
    loss = lax.psum((0.5 / D) * jnp.sum(acc_f[:, 2, :]), MESH_AXES)
    ddf = dd.reshape(RL, D)
    df = _mm(ddf, w_down_f, "nt", "mm_down_dx").reshape(B, N, Cf)
    gw_down = _mm(f.reshape(RL, Cf), ddf, "tn", "mm_down_dw", out_dtype=BF16)
    dF, acc_ffn = _ffn_act_bwd(Fg, Fv, ffn_dw_f, f_gel, f_dgel, df, tcf)
    dFf = dF.reshape(RL, 2 * Cf)
    du2 = _mm(dFf, w_up_g, "nt", "mm_up_dx").reshape(B, N, D)
    gw_up = _mm(u2.reshape(RL, D), dFf, "tn", "mm_up_dw", out_dtype=BF16, out_shards=N_DEV)
    jm = 2 * lax.axis_index("x") + lax.axis_index("y")

    def pair_sums(started, after, tag):
        parts8, sib4 = _split_wait(started, "pair", after, f"rs_pair_{tag}_wait")
        return [_pair_sum(p8, s4, f"rs_pair_sum_{tag}{i}") for i, (p8, s4) in enumerate(zip(parts8, sib4))]

    def own_slab(land, x4):
        idx = (jm,) + (0,) * (x4.ndim - 1)
        return lax.dynamic_update_slice(land, lax.dynamic_slice(x4, idx, (1,) + x4.shape[1:]), idx)

    p_early = _split_start([gw_up, gw_down.reshape(N_DEV, Cf // N_DEV, D)], "pair", "rs_pair_ffn_start")
    mods_b = mods + p_early[4][0, 0]
    dh1, acc2x, _ = _norm_mod_bwd(h1, du2, g_pre_ffn, mods_b, 4, 0, tm, dh2, "norm_mod2_bwd")
    dmix, acc_r1 = _resid_norm_bwd(mix, dh1, g_post_mix, mods, 2, tm, "resid_norm1_bwd")
    q_early = _split_start(pair_sums(p_early, dmix, "ffn"), "quad", "rs_quad_ffn_start")
    dmixf = dmix.reshape(RL, D)
    dmixin = _mm(dmixf, w_out_f, "nt", "mm_out_dx").reshape(B, N, D)
    gw_out = _mm(mixin.reshape(RL, D), dmixf, "tn", "mm_out_dw", out_dtype=BF16)
    dbd, dbl, dmg, acc_mg = _merge_bwd(dmixin, bd, bl, p_mg, b_merge + q_early[4][0:1, 0:1], nct, tm)
    dy_dn = _mm(dbd.reshape(RL, D), w_bdn_f, "nt", "mm_bdn_dx").reshape(B, N, HD)
    gw_bdn = _mm(y_dn.reshape(RL, HD), dbd.reshape(RL, D), "tn", "mm_bdn_dw", out_dtype=BF16)
    dy_lru = _mm(dbl.reshape(RL, D), w_blru_f, "nt", "mm_blru_dx").reshape(B, N, W)
    gw_blru = _mm(y_lru.reshape(RL, W), dbl.reshape(RL, D), "tn", "mm_blru_dw", out_dtype=BF16)

    dh, dyl = _lru_out_bwd(hd0, hd1, p_yl, 0, dy_lru, nct, tm, tcw)
    lam0, da0 = _lru_scan_bwd(a2, hd0, dh, 0, nct, tm, tcw)
    lam1, da1 = _lru_scan_bwd(a2, hd1, dh, 1, nct, tm, tcw)
    dxc, dwbd, acc_lru = _lru_gates_bwd(xc, wbd, bias4, lru_lam_f, (lam0, lam1), (da0, da1), tmg, bw)
    dxl = _dwconv(dxc, 0, W, lru_conv_f, None, tuple(-o for o in conv_offs), nct, tm, tcw, "lru_conv_dx", BF16)
    acc_lc = _dwconv_wgrad(dxc, p_xl, 0, W, conv_offs, nct, tm, tcw, "lru_conv_dw")

    do, dz, acc_on = _dn_out_bwd(o2, p_z, 0, dn_onorm, dy_dn, H, hd, nct, tm)
    dqkvn0, dqkvn1, dgb0, dgb1 = _delta_bwd(qkvn, gb3, (s_in0, s_in1), (tm_d0, tm_d1), do, H, hd, ncc)
    dcv, acc_dc = _dn_act_bwd(cv, (dqkvn0, dqkvn1), p_qkv, conv_offs, H, hd, nct, tm)
    dqkv = _dwconv(dcv, 0, 3 * HD, dn_conv_f, None, tuple(-o for o in conv_offs), nct, tm, tcc, "dn_conv_dx", BF16)
    dp_ab, acc_ab = _ab_act_bwd(p_ab, gb, (dgb0.reshape(R, LANES), dgb1.reshape(R, LANES)), alog_v, dtb_v, nd, tm)

    groups = [(dqkv.reshape(R, 3 * HD), w_qkv, "qkv"), (dz.reshape(R, HD), w_z, "z"), (dp_ab, w_ab, "ab"),
              (dxl.reshape(R, W), w_xl, "xl"), (dyl.reshape(R, W), w_yl, "yl"), (dmg.reshape(R, 2 * D), w_mg, "mg")]
    du1 = _mm_sum([(dg_, wg_) for dg_, wg_, _ in groups], "mm_in_dx")
    gw_groups = [_mm(dg_, u1f, "tn", "mm_in_dw_" + nm, out_dtype=BF16) for dg_, _, nm in groups]
    gw_groups[2] = gw_groups[2][:2 * nd]
    gw_in = jnp.concatenate(gw_groups, axis=0).reshape(N_DEV, csh, D)
    grad_x, acc1x, acc1c = _norm_mod_bwd(h0, du1.reshape(B, T, D), g_pre_mix, mods, 1, nct, tm, dh1, "norm_mod1_bwd")

    g_lru_conv = jnp.sum(acc_lc[:, :4], 0)
    g_dn_conv = jnp.sum(acc_dc[:, :4], 0)
    g_ffn_dw = jnp.sum(acc_ffn[:, :9], 0).reshape(3, 3, Cf)
    sm_names = ["dn_conv", "lru_conv", "lru_b_rg", "lru_b_ig", "lru_lambda", "ffn_dw"]
    sm_parts = [_col_shards(g_dn_conv), _col_shards(g_lru_conv),
                _col_shards(jnp.stack([acc_lru[0], acc_lru[2]])), _col_shards(jnp.stack([acc_lru[1], acc_lru[3]])),
                _col_shards(acc_lru[4:6]), _col_shards(g_ffn_dw)]
    late8 = [gw_in, gw_bdn.reshape(N_DEV, HD // N_DEV, D), gw_blru.reshape(N_DEV, W // N_DEV, D),
             gw_out.reshape(N_DEV, D // N_DEV, D), _pack(sm_parts, lead=(N_DEV,))]
    p_late = _split_start(late8, "pair", "rs_pair_mix_start")
    q_late = _split_start(pair_sums(p_late, gw_out, "mix"), "quad", "rs_quad_mix_start")
    ffn_x, ffn_land = _split_wait(q_early, "quad", q_late[4], "rs_quad_ffn_wait")
    results = {}
    for n, x4, land in zip(["w_up", "w_down"], ffn_x, ffn_land):
        results[n] = list(_adamw(params[n], mom1[n], mom2[n], own_slab(land, x4), "adamw_" + n))

    zeros_d = jnp.zeros((B, D), F32)
    dm_rows = jnp.stack([acc1x[:, 0], acc1x[:, 1], acc_r1[:, 0], acc2x[:, 0], acc2x[:, 1], acc_f[:, 0]], axis=1)
    dm_ctx = jnp.stack([jnp.sum(acc1c[:, 0], 0), jnp.sum(acc1c[:, 1], 0)] + [zeros_d[0]] * 4, axis=0)[None]
    dm_pad = jnp.zeros((SUBLANES, 6 * D), F32).at[:B + 1].set(jnp.concatenate([dm_rows, dm_ctx], 0).reshape(B + 1, 6 * D))
    dm_g = _all_gather([dm_pad], "ag_dmods")[0]
    g_mine = lax.dynamic_slice(dm_g, (0, 0, me * sw), (N_DEV, SUBLANES, sw)).reshape(N_DEV * SUBLANES, sw)
    gw_ada, dcc = _ada_bwd(ccp, w_ada[0], g_mine, c_ctx.reshape(1, D), B)

    per = bw // bdim

    def diag_blocks(dwf):
        g6 = dwf.reshape(W // bw, per, bdim, per, bdim)
        return jnp.einsum("tpiqj,pq->tpij", g6, jnp.eye(per, dtype=F32)).reshape(nblk, bdim, bdim)

    g_rep = dict(
        c_ctx=dcc[0],
        g_pre_mix=jnp.sum(acc1x[:, 2] + acc1c[:, 2], 0)[None], g_post_mix=jnp.sum(acc_r1[:, 1], 0)[None],
        g_pre_ffn=jnp.sum(acc2x[:, 2], 0)[None], g_post_ffn=jnp.sum(acc_f[:, 1], 0)[None],
        b_merge=jnp.sum(acc_mg[:, 0], 0)[None],
        dn_a_log=acc_ab[0, :nd].reshape(1, 2, H), dn_dt_bias=acc_ab[1, :nd].reshape(1, 2, H),
        dn_onorm=jnp.sum(acc_on[:, 0], 0)[None],
        lru_conv_b=jnp.sum(acc_lc[:, 4], 0)[None],
        lru_w_rg=jnp.stack([diag_blocks(dwbd[0]), diag_blocks(dwbd[2])])[None],
        lru_w_ig=jnp.stack([diag_blocks(dwbd[1]), diag_blocks(dwbd[3])])[None],
        ffn_dw_b=jnp.sum(acc_ffn[:, 9], 0)[None])
    mat_names = ["lru_w_rg", "lru_w_ig"]
    vec_names = [n for n in g_rep if n not in mat_names]
    vec_parts, mat_parts = _all_gather([_pack([g_rep[n] for n in vec_names]),
                                        _pack([g_rep[n] for n in mat_names]).astype(BF16)], "ag_rep_grads")
    for grp, parts, nm in ((vec_names, vec_parts, "adamw_rep_vec"), (mat_names, mat_parts, "adamw_rep_mat")):
        out4 = _adamw(_pack([params[n] for n in grp]), _pack([mom1[n] for n in grp]),
                      _pack([mom2[n] for n in grp]), parts, nm)
        for k, buf in enumerate(out4):
            for n, arr in zip(grp, _unpack(buf, [params[n].shape for n in grp])):
                results.setdefault(n, [None] * 4)[k] = arr

    ba_out = _adamw(_pack([b_ada]), _pack([m_b_ada]), _pack([v_b_ada]),
                    _pack([dm_g.reshape(N_DEV * SUBLANES, 6 * D)], lead=(N_DEV * SUBLANES,)), "adamw_b_ada")
    results["b_ada"] = [_unpack(buf, [b_ada.shape])[0] for buf in ba_out]

    wa_out = _adamw(w_ada, m_w_ada, v_w_ada, gw_ada[None], "adamw_w_ada")
    results["w_ada"] = list(wa_out)

    mix_x, mix_land = _split_wait(q_late, "quad", wa_out[0], "rs_quad_mix_wait")
    recv4 = [own_slab(land, x4) for land, x4 in zip(mix_land, mix_x)]
    for n, g4 in zip(["w_in", "w_branch_dn", "w_branch_lru", "w_out"], recv4[:-1]):
        if n == "w_in":
            g4 = jnp.swapaxes(g4, 1, 2)
        results[n] = list(_adamw(params[n], mom1[n], mom2[n], g4, "adamw_" + n))
    sm_out = _adamw(_pack([params[n] for n in sm_names]), _pack([mom1[n] for n in sm_names]),
                    _pack([mom2[n] for n in sm_names]), recv4[-1], "adamw_small")
    for k, buf in enumerate(sm_out):
        for n, arr in zip(sm_names, _unpack(buf, [params[n].shape for n in sm_names])):
            results.setdefault(n, [None] * 4)[k] = arr

    outs = [loss, grad_x]
    for k in range(4):
        outs += [results[n][k] for n in names]
    return tuple(outs)
```

```python
import math

import jax
import jax.numpy as jnp
from jax import lax
from jax.experimental import pallas as pl
from jax.experimental.pallas import tpu as pltpu

F32 = jnp.float32
BF16 = jnp.bfloat16

EPS = 1e-6
GRID_W = 64
CHUNK = 64
LRU_C = 8.0
N_DEV = 8
ADAM_LR = 0.001
ADAM_B1 = 0.9
ADAM_B2 = 0.999
ADAM_EPS = 1e-08
ADAM_WD = 0.01
ADAM_STEP = 10

LANES = 128
SUBLANES = 8
VMEM_LIMIT = 48 * 1024 * 1024
MESH_AXES = ("x", "y", "c")
GELU_C = math.sqrt(2.0 / math.pi)


def _cparams(sem=None, vmem=None):
    kw = {}
    if sem is not None:
        kw["dimension_semantics"] = sem
    if vmem is not None:
        kw["vmem_limit_bytes"] = vmem
    return pltpu.CompilerParams(**kw)


def _tile(n, pref):
    if n <= pref:
        return n
    t = (pref // LANES) * LANES
    while n % t:
        t -= LANES
    return t


def _sigmoid(x):
    return 1.0 / (1.0 + jnp.exp(-x))


def _silu(x):
    return x * _sigmoid(x)


def _dsilu(x):
    s = _sigmoid(x)
    return s * (1.0 + x * (1.0 - s))


def _softplus(x):
    return jnp.maximum(x, 0.0) + jnp.log(1.0 + jnp.exp(-jnp.abs(x)))


def _gelu(x):
    return 0.5 * x * (1.0 + jnp.tanh(GELU_C * (x + 0.044715 * x * x * x)))


def _dgelu(x):
    t = jnp.tanh(GELU_C * (x + 0.044715 * x * x * x))
    return 0.5 * (1.0 + t) + 0.5 * x * (1.0 - t * t) * GELU_C * (1.0 + 3.0 * 0.044715 * x * x)


def _dot(a, b, dims="nn"):
    dn = {"nn": (((1,), (0,)), ((), ())),
          "nt": (((1,), (1,)), ((), ())),
          "tn": (((0,), (0,)), ((), ()))}[dims]
    return lax.dot_general(a.astype(BF16), b.astype(BF16), dn, preferred_element_type=F32)


def _split2(x):
    hi = x.astype(BF16)
    lo = (x - hi.astype(F32)).astype(BF16)
    return hi, lo


def _split3(x):
    h1 = x.astype(BF16)
    r1 = x - h1.astype(F32)
    h2 = r1.astype(BF16)
    h3 = (r1 - h2.astype(F32)).astype(BF16)
    return h1, h2, h3


def _dot_hi(a, b):
    ah, al = _split2(a)
    bh, bl = _split2(b)
    return _dot(ah, bh) + (_dot(ah, bl) + _dot(al, bh))


def _dot_mask(m, x, dims="nn"):
    h1, h2, h3 = _split3(x)
    if dims == "xtn":
        return _dot(h1, m, "tn") + (_dot(h2, m, "tn") + _dot(h3, m, "tn"))
    return _dot(m, h1, dims) + (_dot(m, h2, dims) + _dot(m, h3, dims))


def _my_index():
    return 4 * lax.axis_index("x") + 2 * lax.axis_index("y") + lax.axis_index("c")


def _hbm_call(body, xs, out_shapes, n_sems, name):
    any_spec = pl.BlockSpec(memory_space=pl.ANY)
    return pl.pallas_call(
        body, name=name, out_shape=tuple(out_shapes),
        in_specs=[any_spec] * len(xs), out_specs=tuple([any_spec] * len(out_shapes)),
        scratch_shapes=[pltpu.SemaphoreType.DMA((n_sems,)), pltpu.SemaphoreType.DMA((n_sems,)),
                        pltpu.SemaphoreType.DMA((len(xs),))],
    )(*xs)


def _all_gather(xs, name):
    n = len(xs)

    def body(*refs):
        x_refs, out_refs = refs[:n], refs[n:2 * n]
        send_sems, recv_sems, local_sems = refs[2 * n:]
        x_, y_, c_ = lax.axis_index("x"), lax.axis_index("y"), lax.axis_index("c")
        me, sibling = (x_, y_, c_), (x_, y_, 1 - c_)
        chips = [(1 - x_, y_), (x_, 1 - y_), (1 - x_, 1 - y_)]

        def slab(a, px, py, pc):
            return out_refs[a].at[4 * px + 2 * py + pc]

        def copy(a, k, block, to, src=None):
            return pltpu.make_async_remote_copy(
                src_ref=slab(a, *block) if src is None else src, dst_ref=slab(a, *block),
                send_sem=send_sems.at[7 * a + k], recv_sem=recv_sems.at[7 * a + k],
                device_id=to, device_id_type=pl.DeviceIdType.MESH)

        mine = [pltpu.make_async_copy(x_refs[a], slab(a, *me), local_sems.at[a]) for a in range(n)]
        for cp in mine:
            cp.start()
        sent = []
        for j, chip in enumerate(chips):
            sent += [copy(a, 1 + j, me, (*chip, c_), src=x_refs[a]) for a in range(n)]
        sent += [copy(a, 0, me, sibling, src=x_refs[a]) for a in range(n)]
        for cp in sent:
            cp.start()
        for j, chip in enumerate(chips):
            for a in range(n):
                copy(a, 1 + j, (*chip, c_), me).wait_recv()
                fwd = copy(a, 4 + j, (*chip, c_), sibling)
                fwd.start()
                sent.append(fwd)
        for a in range(n):
            copy(a, 0, sibling, me).wait_recv()
        for j, chip in enumerate(chips):
            for a in range(n):
                copy(a, 4 + j, (*chip, 1 - c_), me).wait_recv()
        for cp in sent:
            cp.wait_send()
        for cp in mine:
            cp.wait()

    outs = [jax.ShapeDtypeStruct((N_DEV,) + x.shape, x.dtype) for x in xs]
    return _hbm_call(body, xs, outs, 7 * n, name)


SPLIT_COPIES = {"gather": 7, "quad": 3, "pair": 4}


def _split_views(kind, x_ref, land_ref, k):
    x_, y_, c_ = lax.axis_index("x"), lax.axis_index("y"), lax.axis_index("c")
    if kind == "pair":
        return x_ref.at[2 * (k - 1) + (1 - c_)], land_ref.at[k - 1], land_ref.at[k - 1], (x_, y_, 1 - c_)
    if kind == "gather":
        px = 1 - x_ if (k >> 2) & 1 else x_
        py = 1 - y_ if (k >> 1) & 1 else y_
        pc = 1 - c_ if k & 1 else c_
        return x_ref, land_ref.at[4 * x_ + 2 * y_ + c_], land_ref.at[4 * px + 2 * py + pc], (px, py, pc)
    px = 1 - x_ if (k >> 1) & 1 else x_
    py = 1 - y_ if k & 1 else y_
    return x_ref.at[2 * px + py], land_ref.at[2 * x_ + y_], land_ref.at[2 * px + py], (px, py, c_)


def _split_start(xs, kind, name):
    n = len(xs)
    npeer = SPLIT_COPIES[kind]
    land_shape = {"gather": lambda x: (N_DEV,) + x.shape, "quad": lambda x: x.shape, "pair": lambda x: (4,) + x.shape[1:]}
    lands = [lax.empty(land_shape[kind](x), x.dtype) for x in xs]

    def body(*refs):
        x_refs, land_refs = refs[:n], refs[n:2 * n]
        send_sems, recv_sems, token = refs[2 * n], refs[2 * n + 1], refs[4 * n + 2]
        for k in range(1, npeer + 1):
            for a in range(n):
                src, dst, _, peer = _split_views(kind, x_refs[a], land_refs[a], k)
                pltpu.make_async_remote_copy(
                    src_ref=src, dst_ref=dst, send_sem=send_sems.at[npeer * a + k - 1],
                    recv_sem=recv_sems.at[npeer * a + k - 1],
                    device_id=peer, device_id_type=pl.DeviceIdType.MESH).start()
        token[...] = jnp.zeros_like(token)

    hbm = pl.BlockSpec(memory_space=pltpu.HBM)
    sem = pl.BlockSpec(memory_space=pltpu.SEMAPHORE)
    sems = pltpu.SemaphoreType.DMA((npeer * n,))
    out = pl.pallas_call(
        body, name=name,
        out_shape=(sems, sems, *[pltpu.HBM(a.shape, a.dtype) for a in list(xs) + lands],
                   jax.ShapeDtypeStruct((SUBLANES, LANES), F32)),
        in_specs=[hbm] * (2 * n),
        out_specs=(sem, sem, *[hbm] * (2 * n), pl.BlockSpec(memory_space=pltpu.VMEM)),
        input_output_aliases={i: 2 + i for i in range(2 * n)},
        compiler_params=pltpu.CompilerParams(has_side_effects=pltpu.SideEffectType.DATAFLOW_SIDE_EFFECTING),
    )(*[pltpu.with_memory_space_constraint(a, pltpu.HBM) for a in list(xs) + lands])
    return out[0], out[1], list(out[2:2 + n]), list(out[2 + n:2 + 2 * n]), out[-1]


def _split_wait(started, kind, after, name):
    send_sems_, recv_sems_, x_thru, land_thru, _ = started
    n = len(x_thru)
    npeer = SPLIT_COPIES[kind]

    def body(*refs):
        x_refs, land_refs = refs[:n], refs[n:2 * n]
        send_sems, recv_sems = refs[2 * n], refs[2 * n + 1]
        for k in range(1, npeer + 1):
            for a in range(n):
                src, _, landed, peer = _split_views(kind, x_refs[a], land_refs[a], k)
                cp = pltpu.make_async_remote_copy(
                    src_ref=src, dst_ref=landed, send_sem=send_sems.at[npeer * a + k - 1],
                    recv_sem=recv_sems.at[npeer * a + k - 1],
                    device_id=peer, device_id_type=pl.DeviceIdType.MESH)
                cp.wait_send()
                cp.wait_recv()

    hbm = pl.BlockSpec(memory_space=pltpu.HBM)
    sem = pl.BlockSpec(memory_space=pltpu.SEMAPHORE)
    out = pl.pallas_call(
        body, name=name,
        out_shape=tuple(pltpu.HBM(a.shape, a.dtype) for a in x_thru + land_thru),
        in_specs=[hbm] * (2 * n) + [sem, sem, pl.BlockSpec(memory_space=pl.ANY)],
        out_specs=tuple([hbm] * (2 * n)),
        input_output_aliases={i: i for i in range(2 * n)},
        compiler_params=pltpu.CompilerParams(has_side_effects=pltpu.SideEffectType.DATAFLOW_SIDE_EFFECTING),
    )(*x_thru, *land_thru, send_sems_, recv_sems_, after)
    return list(out[:n]), list(out[n:])


def _pair_sum(x8, r4, name):
    _, r, c = x8.shape
    tr = _row_tile(r, c * 12)

    def body(core_ref, x_ref, r_ref, o_ref):
        o_ref[...] = (x_ref[...].astype(F32) + r_ref[...].astype(F32)).astype(o_ref.dtype)

    core = lax.axis_index("c").astype(jnp.int32).reshape(1)
    return pl.pallas_call(
        body, name=name,
        grid_spec=pltpu.PrefetchScalarGridSpec(
            num_scalar_prefetch=1, grid=(4, r // tr),
            in_specs=[pl.BlockSpec((None, tr, c), lambda j, i, core_ref: (2 * j + core_ref[0], i, 0)),
                      pl.BlockSpec((None, tr, c), lambda j, i, core_ref: (j, i, 0))],
            out_specs=pl.BlockSpec((None, tr, c), lambda j, i, core_ref: (j, i, 0))),
        out_shape=jax.ShapeDtypeStruct((4, r, c), x8.dtype),
        compiler_params=_cparams(("parallel", "parallel"), VMEM_LIMIT))(core, x8, r4)


def _row_tile(r, bytes_per_row, budget=10 * 1024 * 1024):
    best = None
    for t in range(16, r + 1, 16):
        if r % t == 0 and t * bytes_per_row <= budget:
            best = t
    return best if best is not None else r


def _mm(a, b, dims, name, out_dtype=F32, out_shards=1, tm=1024, tn=1024, tk=1024):
    S = b.shape[0] if b.ndim == 3 else 1
    bshape = (b.shape[1], S * b.shape[2]) if b.ndim == 3 else b.shape
    if dims == "nn":
        (M, K), (K2, N) = a.shape, bshape
    elif dims == "nt":
        (M, K), (N, K2) = a.shape, bshape
    else:
        (K, M), (K2, N) = a.shape, bshape
    assert K == K2, (a.shape, b.shape, dims)
    cs = bshape[1] // S
    tm, tk = _tile(M, tm), _tile(K, min(tk, cs) if (S > 1 and dims == "nt") else tk)
    tn = _tile(N, min(tn, cs) if (S > 1 and dims != "nt") else min(tn, N // out_shards))
    assert cs % (tk if dims == "nt" else tn) == 0 and (N // out_shards) % tn == 0
    nk = K // tk

    def body(a_ref, b_ref, o_ref, acc):
        k = pl.program_id(2)

        @pl.when(k == 0)
        def _():
            acc[...] = jnp.zeros_like(acc)

        acc[...] += _dot(a_ref[...], b_ref[...], dims)

        @pl.when(k == nk - 1)
        def _():
            o_ref[...] = acc[...].astype(out_dtype)

    a_spec =(pl.BlockSpec((tk, tm), lambda i, j, k: (k, i)) if dims == "tn"
              else pl.BlockSpec((tm, tk), lambda i, j, k: (i, k)))
    if S == 1:
        b_spec = (pl.BlockSpec((tn, tk), lambda i, j, k: (j, k)) if dims == "nt"
                  else pl.BlockSpec((tk, tn), lambda i, j, k: (k, j)))
    elif dims == "nt":
        per = cs // tk
        b_spec = pl.BlockSpec((None, tn, tk), lambda i, j, k: (k // per, j, k % per))
    else:
        per = cs // tn
        b_spec = pl.BlockSpec((None, tk, tn), lambda i, j, k: (j // per, k, j % per))
    if out_shards == 1:
        out_spec, out_shape = pl.BlockSpec((tm, tn), lambda i, j, k: (i, j)), (M, N)
    else:
        oper = N // out_shards // tn
        out_spec = pl.BlockSpec((None, tm, tn), lambda i, j, k: (j // oper, i, j % oper))
        out_shape = (out_shards, M, N // out_shards)
    return pl.pallas_call(
        body, name=name, grid=(M // tm, N // tn, nk),
        in_specs=[a_spec, b_spec], out_specs=out_spec,
        out_shape=jax.ShapeDtypeStruct(out_shape, out_dtype),
        scratch_shapes=[pltpu.VMEM((tm, tn), F32)],
        compiler_params=_cparams(("parallel", "parallel", "arbitrary"), VMEM_LIMIT),
    )(a, b)


VMEM_LIMIT_SUM = 56 * 1024 * 1024


def _mm_sum(pairs, name, out_dtype=F32, tm=768, tn=1024, tk=1024):
    M, N = pairs[0][0].shape[0], pairs[0][1].shape[1]
    tm, tn = _tile(M, tm), _tile(N, tn)
    tks = [_tile(a.shape[1], tk) for a, _ in pairs]
    nks = [a.shape[1] // t for (a, _), t in zip(pairs, tks)]
    starts = [sum(nks[:g]) for g in range(len(pairs))]
    nk = sum(nks)
    G = len(pairs)

    def body(*refs):
        o_ref, acc = refs[2 * G], refs[2 * G + 1]
        k = pl.program_id(2)

        @pl.when(k == 0)
        def _():
            acc[...] = jnp.zeros_like(acc)

        for g in range(G):
            @pl.when(jnp.logical_and(k >= starts[g], k < starts[g] + nks[g]))
            def _(g=g):
                acc[...] += _dot(refs[2 * g][...], refs[2 * g + 1][...])

        @pl.when(k == nk - 1)
        def _():
            o_ref[...] = acc[...].astype(out_dtype)

    in_specs, args = [], []
    for g, (a, b) in enumerate(pairs):
        kk = lambda k, g=g: jnp.clip(k - starts[g], 0, nks[g] - 1)
        in_specs += [pl.BlockSpec((tm, tks[g]), lambda i, j, k, kk=kk: (i, kk(k))),
                     pl.BlockSpec((tks[g], tn), lambda i, j, k, kk=kk: (kk(k), j))]
        args += [a, b]
    return pl.pallas_call(
        body, name=name, grid=(M // tm, N // tn, nk),
        in_specs=in_specs, out_specs=pl.BlockSpec((tm, tn), lambda i, j, k: (i, j)),
        out_shape=jax.ShapeDtypeStruct((M, N), out_dtype),
        scratch_shapes=[pltpu.VMEM((tm, tn), F32)],
        compiler_params=_cparams(("parallel", "parallel", "arbitrary"), VMEM_LIMIT_SUM),
    )(*args)


def _ada_fwd(ccp, w, b):
    def body(c_ref, w_ref, b_ref, o_ref):
        o_ref[...] = _dot(_silu(c_ref[...]), w_ref[...]) + b_ref[...]

    return pl.pallas_call(
        body, name="ada_fwd", out_shape=jax.ShapeDtypeStruct((ccp.shape[0], w.shape[1]), F32),
        compiler_params=_cparams(None, VMEM_LIMIT))(ccp, w, b)


def _ada_bwd(ccp, w, g, cctx, n_loc):
    def body(c_ref, w_ref, g_ref, cc_ref, gw_ref, dc_ref):
        gw_ref[...] = _dot(_silu(c_ref[...]), g_ref[...], "tn")
        dcc = _dot(g_ref[...], w_ref[...], "nt")
        row = lax.broadcasted_iota(jnp.int32, dcc.shape, 0)
        part = jnp.sum(jnp.where(row % SUBLANES == n_loc, dcc, 0.0), axis=0, keepdims=True)
        dc_ref[...] = jnp.broadcast_to(part * _dsilu(cc_ref[...]), dc_ref.shape)

    D = ccp.shape[1]
    return pl.pallas_call(
        body, name="ada_bwd",
        out_shape=(jax.ShapeDtypeStruct(w.shape, F32), jax.ShapeDtypeStruct((SUBLANES, D), F32)),
        compiler_params=_cparams(None, VMEM_LIMIT))(ccp, w, g, cctx)


def _norm_mod_fwd(h, gain, mods, i_shift, i_scale, nct, tm, name):
    B, T, D = h.shape

    def body(h_ref, g_ref, m_ref, u_ref):
        x = h_ref[0]
        r = lax.rsqrt(jnp.mean(x * x, axis=-1, keepdims=True) + EPS)
        n = x * r * g_ref[...]
        u_ref[0] = (n * (1.0 + m_ref[0, i_scale:i_scale + 1, :]) + m_ref[0, i_shift:i_shift + 1, :]).astype(BF16)

    return pl.pallas_call(
        body, name=name, grid=(B, T // tm),
        in_specs=[pl.BlockSpec((1, tm, D), lambda b, t: (b, t, 0)),
                  pl.BlockSpec((1, D), lambda b, t: (0, 0)),
                  pl.BlockSpec((1, SUBLANES, D), lambda b, t: (jnp.where(t < nct, B, b), 0, 0))],
        out_specs=pl.BlockSpec((1, tm, D), lambda b, t: (b, t, 0)),
        out_shape=jax.ShapeDtypeStruct((B, T, D), BF16),
        compiler_params=_cparams(("parallel", "parallel"), VMEM_LIMIT),
    )(h, gain, mods)


def _norm_mod_bwd(h, du, gain, mods, i_scale, nct, tm, res, name):
    B, T, D = h.shape
    nt = T // tm

    def body(h_ref, du_ref, g_ref, m_ref, res_ref, dx_ref, ax_ref, ac_ref):
        t = pl.program_id(1)
        x = h_ref[0]
        r = lax.rsqrt(jnp.mean(x * x, axis=-1, keepdims=True) + EPS)
        nh = x * r
        g = g_ref[...]
        du_ = du_ref[0]
        dn = du_ * (1.0 + m_ref[0, i_scale:i_scale + 1, :])
        dnh = dn * g
        dx = r * (dnh - nh * jnp.mean(dnh * nh, axis=-1, keepdims=True))
        sums = (jnp.sum(du_, axis=0, keepdims=True), jnp.sum(du_ * nh * g, axis=0, keepdims=True),
                jnp.sum(dn * nh, axis=0, keepdims=True))

        @pl.when(t == 0)
        def _():
            ax_ref[...] = jnp.zeros_like(ax_ref)
            ac_ref[...] = jnp.zeros_like(ac_ref)

        def accumulate(ref):
            for i, s in enumerate(sums):
                ref[0, i:i + 1, :] += s

        @pl.when(t < nct)
        def _():
            accumulate(ac_ref)

        @pl.when(t >= nct)
        def _():
            accumulate(ax_ref)
            dx_ref[0] = dx + res_ref[0]

    lat = lambda b, t: (b, jnp.maximum(t - nct, 0), 0)
    acc = pl.BlockSpec((1, SUBLANES, D), lambda b, t: (b, 0, 0))
    return pl.pallas_call(
        body, name=name, grid=(B, nt),
        in_specs=[pl.BlockSpec((1, tm, D), lambda b, t: (b, t, 0)),
                  pl.BlockSpec((1, tm, D), lambda b, t: (b, t, 0)),
                  pl.BlockSpec((1, D), lambda b, t: (0, 0)),
                  pl.BlockSpec((1, SUBLANES, D), lambda b, t: (jnp.where(t < nct, B, b), 0, 0)),
                  pl.BlockSpec((1, tm, D), lat)],
        out_specs=(pl.BlockSpec((1, tm, D), lat), acc, acc),
        out_shape=(jax.ShapeDtypeStruct(res.shape, F32),
                   jax.ShapeDtypeStruct((B, SUBLANES, D), F32), jax.ShapeDtypeStruct((B, SUBLANES, D), F32)),
        compiler_params=_cparams(("parallel", "arbitrary"), VMEM_LIMIT),
    )(h, du, gain, mods, res)


def _resid_norm_fwd(x, y, gain, mods, i_gate, tm, name):
    B, N, D = x.shape

    def body(x_ref, y_ref, g_ref, m_ref, o_ref):
        y_ = y_ref[0]
        r = lax.rsqrt(jnp.mean(y_ * y_, axis=-1, keepdims=True) + EPS)
        o_ref[0] = x_ref[0] + y_ * r * g_ref[...] * m_ref[0, i_gate:i_gate + 1, :]

    row = pl.BlockSpec((1, tm, D), lambda b, t: (b, t, 0))
    return pl.pallas_call(
        body, name=name, grid=(B, N // tm),
        in_specs=[row, row, pl.BlockSpec((1, D), lambda b, t: (0, 0)),
                  pl.BlockSpec((1, SUBLANES, D), lambda b, t: (b, 0, 0))],
        out_specs=row, out_shape=jax.ShapeDtypeStruct((B, N, D), F32),
        compiler_params=_cparams(("parallel", "parallel"), VMEM_LIMIT),
    )(x, y, gain, mods)


def _resid_norm_bwd_math(y_, dh, g, gate):
    r = lax.rsqrt(jnp.mean(y_ * y_, axis=-1, keepdims=True) + EPS)
    nh = y_ * r
    dgate = jnp.sum(dh * nh * g, axis=0, keepdims=True)
    dn = dh * gate
    dgain = jnp.sum(dn * nh, axis=0, keepdims=True)
    dnh = dn * g
    dy = r * (dnh - nh * jnp.mean(dnh * nh, axis=-1, keepdims=True))
    return dy, dgate, dgain


def _resid_norm_bwd(y, dh, gain, mods, i_gate, tm, name):
    B, N, D = y.shape

    def body(y_ref, dh_ref, g_ref, m_ref, dy_ref, acc_ref):
        t = pl.program_id(1)
        dy, dgate, dgain = _resid_norm_bwd_math(y_ref[0], dh_ref[0], g_ref[...], m_ref[0, i_gate:i_gate + 1, :])
        dy_ref[0] = dy.astype(BF16)

        @pl.when(t == 0)
        def _():
            acc_ref[...] = jnp.zeros_like(acc_ref)

        acc_ref[0, 0:1, :] += dgate
        acc_ref[0, 1:2, :] += dgain

    row = pl.BlockSpec((1, tm, D), lambda b, t: (b, t, 0))
    return pl.pallas_call(
        body, name=name, grid=(B, N // tm),
        in_specs=[row, row, pl.BlockSpec((1, D), lambda b, t: (0, 0)),
                  pl.BlockSpec((1, SUBLANES, D), lambda b, t: (b, 0, 0))],
        out_specs=(row, pl.BlockSpec((1, SUBLANES, D), lambda b, t: (b, 0, 0))),
        out_shape=(jax.ShapeDtypeStruct((B, N, D), BF16), jax.ShapeDtypeStruct((B, SUBLANES, D), F32)),
        compiler_params=_cparams(("parallel", "arbitrary"), VMEM_LIMIT),
    )(y, dh, gain, mods)


def _final_fwd_bwd(h1, d, gain, mods, i_gate, tgt, tm):
    B, N, D = h1.shape

    def body(h_ref, d_ref, g_ref, m_ref, t_ref, dd_ref, dh_ref, acc_ref):
        t = pl.program_id(1)
        d_ = d_ref[0]
        g = g_ref[...]
        gate = m_ref[0, i_gate:i_gate + 1, :]
        r = lax.rsqrt(jnp.mean(d_ * d_, axis=-1, keepdims=True) + EPS)
        e = h_ref[0] + d_ * r * g * gate - t_ref[0]
        dh = e * (1.0 / D)
        dh_ref[0] = dh
        dy, dgate, dgain = _resid_norm_bwd_math(d_, dh, g, gate)
        dd_ref[0] = dy.astype(BF16)

        @pl.when(t == 0)
        def _():
            acc_ref[...] = jnp.zeros_like(acc_ref)

        acc_ref[0, 0:1, :] += dgate
        acc_ref[0, 1:2, :] += dgain
        acc_ref[0, 2:3, :] += jnp.sum(e * e, axis=0, keepdims=True)

    row = pl.BlockSpec((1, tm, D), lambda b, t: (b, t, 0))
    return pl.pallas_call(
        body, name="final_fwd_bwd", grid=(B, N // tm),
        in_specs=[row, row, pl.BlockSpec((1, D), lambda b, t: (0, 0)),
                  pl.BlockSpec((1, SUBLANES, D), lambda b, t: (b, 0, 0)), row],
        out_specs=(row, row, pl.BlockSpec((1, SUBLANES, D), lambda b, t: (b, 0, 0))),
        out_shape=(jax.ShapeDtypeStruct((B, N, D), BF16), jax.ShapeDtypeStruct((B, N, D), F32),
                   jax.ShapeDtypeStruct((B, SUBLANES, D), F32)),
        compiler_params=_cparams(("parallel", "arbitrary"), VMEM_LIMIT),
    )(h1, d, gain, mods, tgt)


def _shifted(x, prev, nxt, off, row, tm):
    if off == 0:
        return x
    if off < 0:
        y = pltpu.roll(x, -off, 0)
        for r in range(-off):
            y = jnp.where(row == r, prev[SUBLANES + r + off:SUBLANES + r + off + 1, :], y)
        return y
    y = pltpu.roll(x, tm - off, 0)
    for r in range(off):
        y = jnp.where(row == tm - off + r, nxt[r:r + 1, :], y)
    return y


def _halo_specs(T, tm, tc, cb0, order):
    r8, n8 = tm // SUBLANES, T // SUBLANES
    if order == "btj":
        prev = lambda b, t, j: (b, jnp.maximum(t * r8 - 1, 0), cb0 + j)
        nxt = lambda b, t, j: (b, jnp.minimum((t + 1) * r8, n8 - 1), cb0 + j)
    else:
        prev = lambda b, j, t: (b, jnp.maximum(t * r8 - 1, 0), cb0 + j)
        nxt = lambda b, j, t: (b, jnp.minimum((t + 1) * r8, n8 - 1), cb0 + j)
    return pl.BlockSpec((1, SUBLANES, tc), prev), pl.BlockSpec((1, SUBLANES, tc), nxt)


def _seg_halos(p_ref, n_ref, t, nct, nt):
    seg_start = jnp.logical_or(t == 0, t == nct)
    seg_end = jnp.logical_or(t == nct - 1, t == nt - 1)
    prev = jnp.where(seg_start, 0.0, p_ref[0])
    nxt = jnp.where(seg_end, 0.0, n_ref[0])
    return prev, nxt


def _dwconv(x, col0, C, w, bias, offs, nct, tm, tc, name, out_dtype=F32, qkv_heads=None):
    B, T, _ = x.shape
    nt = T // tm
    cb0 = col0 // tc
    if qkv_heads is not None:
        assert tc == qkv_heads[0] * qkv_heads[1] and C == 3 * tc

    def body(*refs):
        refs = list(refs)
        a_ref = refs.pop() if qkv_heads is not None else None
        if bias is None:
            x_ref, p_ref, n_ref, w_ref, o_ref = refs
        else:
            x_ref, p_ref, n_ref, w_ref, b_ref, o_ref = refs
        t = pl.program_id(1)
        prev, nxt = _seg_halos(p_ref, n_ref, t, nct, nt)
        x_ = x_ref[0]
        row = lax.broadcasted_iota(jnp.int32, x_.shape, 0)
        acc = None
        for j, off in enumerate(offs):
            term = _shifted(x_, prev, nxt, off, row, tm) * w_ref[j:j + 1, :]
            acc = term if acc is None else acc + term
        if bias is not None:
            acc = acc + b_ref[...]
        o_ref[0] = acc.astype(out_dtype)
        if qkv_heads is not None:
            H, hd = qkv_heads
            part = pl.program_id(2)
            for h in range(H):
                sl = slice(h * hd, (h + 1) * hd)
                s = _silu(acc[:, sl])
                rs = lax.rsqrt(jnp.sum(s * s, axis=-1, keepdims=True) + EPS)
                scale = jnp.where(part == 0, rs * (float(hd) ** -0.5), jnp.where(part == 1, rs, 1.0))
                a_ref[0, :, sl] = (s * scale).astype(a_ref.dtype)

    pspec, nspec = _halo_specs(T, tm, tc, cb0, "btj")
    in_specs = [pl.BlockSpec((1, tm, tc), lambda b, t, j: (b, t, cb0 + j)), pspec, nspec,
                pl.BlockSpec((w.shape[0], tc), lambda b, t, j: (0, j))]
    args = [x, x, x, w]
    if bias is not None:
        in_specs.append(pl.BlockSpec((1, tc), lambda b, t, j: (0, j)))
        args.append(bias)
    tile = pl.BlockSpec((1, tm, tc), lambda b, t, j: (b, t, j))
    out_specs, out_shape = tile, jax.ShapeDtypeStruct((B, T, C), out_dtype)
    if qkv_heads is not None:
        out_specs, out_shape = (tile, tile), (out_shape, jax.ShapeDtypeStruct((B, T, C), BF16))
    return pl.pallas_call(
        body, name=name, grid=(B, nt, C // tc),
        in_specs=in_specs, out_specs=out_specs, out_shape=out_shape,
        compiler_params=_cparams(("parallel", "parallel", "parallel"), VMEM_LIMIT),
    )(*args)


def _dwconv_wgrad(dy, x, col0, C, offs, nct, tm, tc, name):
    B, T, _ = x.shape
    nt = T // tm
    cb0 = col0 // tc
    K = len(offs)

    def body(dy_ref, x_ref, p_ref, n_ref, acc_ref):
        t = pl.program_id(2)
        prev, nxt = _seg_halos(p_ref, n_ref, t, nct, nt)
        x_ = x_ref[0]
        dy_ = dy_ref[0]
        row = lax.broadcasted_iota(jnp.int32, x_.shape, 0)

        @pl.when(t == 0)
        def _():
            acc_ref[...] = jnp.zeros_like(acc_ref)

        for j, off in enumerate(offs):
            acc_ref[0, j:j + 1, :] += jnp.sum(dy_ * _shifted(x_, prev, nxt, off, row, tm), axis=0, keepdims=True)
        acc_ref[0, K:K + 1, :] += jnp.sum(dy_, axis=0, keepdims=True)

    pspec, nspec = _halo_specs(T, tm, tc, cb0, "bjt")
    return pl.pallas_call(
        body, name=name, grid=(B, C // tc, nt),
        in_specs=[pl.BlockSpec((1, tm, tc), lambda b, j, t: (b, t, j)),
                  pl.BlockSpec((1, tm, tc), lambda b, j, t: (b, t, cb0 + j)), pspec, nspec],
        out_specs=pl.BlockSpec((1, SUBLANES, tc), lambda b, j, t: (b, 0, j)),
        out_shape=jax.ShapeDtypeStruct((B, SUBLANES, C), F32),
        compiler_params=_cparams(("parallel", "parallel", "arbitrary"), VMEM_LIMIT),
    )(dy, x, x, x)


def _ab_act(pab, alog, dtb, nd, tm):
    R = pab.shape[0]

    def body(p_ref, al_ref, dt_ref, o_ref):
        p = p_ref[...]
        lane = lax.broadcasted_iota(jnp.int32, p.shape, 1)
        g = -jnp.exp(al_ref[...]) * _softplus(p + dt_ref[...])
        o_ref[...] = jnp.where(lane < nd, g, jnp.where(lane < 2 * nd, _sigmoid(p), 0.0))

    row = pl.BlockSpec((tm, LANES), lambda i: (i, 0))
    vec = pl.BlockSpec((1, LANES), lambda i: (0, 0))
    return pl.pallas_call(
        body, name="ab_act", grid=(R // tm,), in_specs=[row, vec, vec], out_specs=row,
        out_shape=jax.ShapeDtypeStruct((R, LANES), F32),
        compiler_params=_cparams(("parallel",), VMEM_LIMIT))(pab, alog, dtb)


def _ab_act_bwd(pab, gb, dgb, alog, dtb, nd, tm):
    R = pab.shape[0]

    def body(p_ref, gb_ref, d0_ref, d1_ref, al_ref, dt_ref, o_ref, acc_ref):
        i = pl.program_id(0)
        p = p_ref[...]
        gb_ = gb_ref[...]
        d_ = d0_ref[...] + d1_ref[...]
        lane = lax.broadcasted_iota(jnp.int32, p.shape, 1)
        is_g = lane < nd
        is_b = jnp.logical_and(lane >= nd, lane < 2 * nd)
        da = jnp.where(is_g, d_ * (-jnp.exp(al_ref[...])) * _sigmoid(p + dt_ref[...]), 0.0)
        db = jnp.where(is_b, d_ * gb_ * (1.0 - gb_), 0.0)
        o_ref[...] = (da + db).astype(BF16)

        @pl.when(i == 0)
        def _():
            acc_ref[...] = jnp.zeros_like(acc_ref)

        acc_ref[0:1, :] += jnp.sum(jnp.where(is_g, d_ * gb_, 0.0), axis=0, keepdims=True)
        acc_ref[1:2, :] += jnp.sum(da, axis=0, keepdims=True)

    row = pl.BlockSpec((tm, LANES), lambda i: (i, 0))
    vec = pl.BlockSpec((1, LANES), lambda i: (0, 0))
    return pl.pallas_call(
        body, name="ab_act_bwd", grid=(R // tm,),
        in_specs=[row, row, row, row, vec, vec],
        out_specs=(row, pl.BlockSpec((SUBLANES, LANES), lambda i: (0, 0))),
        out_shape=(jax.ShapeDtypeStruct((R, LANES), BF16), jax.ShapeDtypeStruct((SUBLANES, LANES), F32)),
        compiler_params=_cparams(("arbitrary",), VMEM_LIMIT))(pab, gb, dgb[0], dgb[1], alog, dtb)


def _dn_act_bwd(cv, dqkv, x, offs, H, hd, nct, tm):
    B, T, C3 = cv.shape
    HD = H * hd
    nt = T // tm
    r8, n8 = tm // SUBLANES, T // SUBLANES
    qscale = float(hd) ** -0.5

    def body(c_ref, d0_ref, d1_ref, x_ref, p_ref, n_ref, o_ref, acc_ref):
        part = pl.program_id(0)
        t = pl.program_id(2)
        for h in range(H):
            sl = slice(h * hd, (h + 1) * hd)
            c = c_ref[0, :, sl]
            s = _silu(c)
            dout = d0_ref[0, :, sl] + d1_ref[0, :, sl]
            rs = lax.rsqrt(jnp.sum(s * s, axis=-1, keepdims=True) + EPS)
            sh = s * rs
            dnorm = rs * (dout - sh * jnp.sum(dout * sh, axis=-1, keepdims=True))
            ds = jnp.where(part == 0, dnorm * qscale, jnp.where(part == 1, dnorm, dout))
            o_ref[0, :, sl] = ds * _dsilu(c)

        @pl.when(t == 0)
        def _():
            acc_ref[...] = jnp.zeros_like(acc_ref)

        prev, nxt = _seg_halos(p_ref, n_ref, t, nct, nt)
        x_ = x_ref[0]
        dcv_ = o_ref[0]
        row = lax.broadcasted_iota(jnp.int32, x_.shape, 0)
        for j, off in enumerate(offs):
            acc_ref[0, j:j + 1, :] += jnp.sum(dcv_ * _shifted(x_, prev, nxt, off, row, tm), axis=0, keepdims=True)

    spec = pl.BlockSpec((1, tm, HD), lambda p, b, t: (b, t, p))
    halo_p = pl.BlockSpec((1, SUBLANES, HD), lambda p, b, t: (b, jnp.maximum(t * r8 - 1, 0), p))
    halo_n = pl.BlockSpec((1, SUBLANES, HD), lambda p, b, t: (b, jnp.minimum((t + 1) * r8, n8 - 1), p))
    return pl.pallas_call(
        body, name="dn_act_bwd", grid=(3, B, nt),
        in_specs=[spec, spec, spec, spec, halo_p, halo_n],
        out_specs=(spec, pl.BlockSpec((1, SUBLANES, HD), lambda p, b, t: (b, 0, p))),
        out_shape=(jax.ShapeDtypeStruct((B, T, C3), F32), jax.ShapeDtypeStruct((B, SUBLANES, C3), F32)),
        compiler_params=_cparams(("parallel", "parallel", "arbitrary"), VMEM_LIMIT),
    )(cv, dqkv[0], dqkv[1], x, x, x)


def _chunk_of(d, s, ncc, nch):
    if d == 0:
        return s
    return jnp.where(s < ncc, ncc - 1 - s, nch - 1 - (s - ncc))


def _delta_masks(d):
    row = lax.broadcasted_iota(jnp.int32, (CHUNK, CHUNK), 0)
    col = lax.broadcasted_iota(jnp.int32, (CHUNK, CHUNK), 1)
    sign = 1 - 2 * d
    diff = (row - col) * sign
    return diff >= 0, diff > 0, diff <= 0


def _each(f, *lists):
    return [f(*a) for a in zip(*lists)]


def _unit_tri_inverse(As):
    C = As[0].shape[0]
    row = lax.broadcasted_iota(jnp.int32, (C, C), 0)
    col = lax.broadcasted_iota(jnp.int32, (C, C), 1)
    eye = jnp.where(row == col, 1.0, 0.0).astype(F32)
    same = lambda shift: jnp.right_shift(row, shift) == jnp.right_shift(col, shift)
    in8 = same(3)
    xs = [-jnp.where(in8, a, 0.0) for a in As]
    ts = [eye + x for x in xs]
    ps = _each(lambda x: _dot(x, x), xs)
    tp = _each(lambda t, p: _dot(_rows(t, p), p), ts, ps)
    ts = _each(lambda t, m: t + m[:C], ts, tp)
    ts = _each(lambda t, m: t + _dot(t, m[C:]), ts, tp)
    shift = 3
    while (1 << shift) < C:
        off = jnp.logical_and(same(shift + 1), jnp.right_shift(row, shift) != jnp.right_shift(col, shift))
        los = [jnp.where(off, a, 0.0) for a in As]
        ts = _each(lambda t, lo: t - _dot(t, _dot(lo, t)), ts, los)
        shift += 1
    def residual(a, t):
        (ah, al), (th, tl) = _split2(eye + a), _split2(t)
        m = _dot(_rows(ah, al), th)
        return eye - (m[:C] + (m[C:] + _dot(ah, tl)))

    rs = _each(residual, As, ts)
    return _each(lambda t, r: t + _dot(t, r), ts, rs)


def _rows(*xs):
    return jnp.concatenate(xs, axis=0)


def _cols(*xs):
    return jnp.concatenate(xs, axis=1)


def _delta_chunk_fwd(q, k, v, g_i, g_j, beta_i, gl, S, incl, strict, Tm=None):
    D_ = _each(lambda gi, gj, m: jnp.where(m, jnp.exp(jnp.where(m, gi - gj, 0.0)), 0.0), g_i, g_j, incl)
    C, dk = k[0].shape
    kb = _each(lambda k_, b: k_ * b, k, beta_i)
    kq = _each(lambda kb_, q_, k_: _dot(_rows(kb_, q_), k_, "nt"), kb, q, k)
    A = _each(lambda m_, d, m: jnp.where(m, m_[:C] * d, 0.0), kq, D_, strict)
    P = _each(lambda m_, d, m: jnp.where(m, m_[C:] * d, 0.0), kq, D_, incl)
    if Tm is None:
        Tm = _unit_tri_inverse(A)
    gam = _each(jnp.exp, g_i)
    wu = _each(lambda t, kb_, g, v_, b: _dot(t, _cols(kb_ * g, v_ * b)), Tm, kb, gam, v, beta_i)
    w = [m_[:, :dk] for m_ in wu]
    u = [m_[:, dk:] for m_ in wu]
    qg = _each(lambda q_, g: q_ * g, q, gam)
    ws = _each(lambda w_, qg_, s: _dot(_rows(w_, qg_), s), w, qg, S)
    vn = _each(lambda u_, m_: u_ - m_[:C], u, ws)
    o = _each(lambda m_, p, vn_: m_[C:] + _dot(p, vn_), ws, P, vn)
    e_i = _each(lambda gl_, gi: jnp.exp(gl_ - gi), gl, g_i)
    kd = _each(lambda k_, e: k_ * e, k, e_i)
    Gam = _each(jnp.exp, gl)
    S_new = _each(lambda s, g, kd_, vn_: s * g + _dot(kd_, vn_, "tn"), S, Gam, kd, vn)
    return dict(D=D_, kb=kb, A=A, Tm=Tm, gam=gam, w=w, u=u, vn=vn, P=P, qg=qg, o=o, e=e_i, kd=kd, Gam=Gam, S_new=S_new)


def _delta_gb(gb_ref, d, incl, incl_t, H):
    gb = gb_ref[0]
    g = gb[:, d * H:(d + 1) * H]
    beta = gb[:, (2 + d) * H:(3 + d) * H]
    gc_col = _dot_mask(incl.astype(BF16), g)
    gc_row = _dot_mask(incl_t.astype(BF16), g, "xtn")
    gl = jnp.sum(g, axis=0, keepdims=True)
    return beta, gc_col, gc_row, gl


def _delta_fwd(qkvn, gb, H, hd, ncc):
    B, T, _ = qkvn.shape
    nch = T // CHUNK
    HD = H * hd
    pairs = [(d, h) for d in range(2) for h in range(H)]

    def body(q0, k0, v0, gb0, q1, k1, v1, gb1, o0, o1, sin0, sin1, tm0, tm1, S_ref):
        s = pl.program_id(1)
        q_refs, k_refs, v_refs, gb_refs = (q0, q1), (k0, k1), (v0, v1), (gb0, gb1)
        o_refs, sin_refs, tm_refs = (o0, o1), (sin0, sin1), (tm0, tm1)

        @pl.when(s == 0)
        def _():
            S_ref[...] = jnp.zeros_like(S_ref)

        masks = [_delta_masks(d) for d in range(2)]
        gbs = [_delta_gb(gb_refs[d], d, masks[d][0], masks[d][2], H) for d in range(2)]
        head = lambda ref, h: ref[0, :, h * hd:(h + 1) * hd]
        S = [S_ref[d, h] for d, h in pairs]
        r = _delta_chunk_fwd([head(q_refs[d], h) for d, h in pairs], [head(k_refs[d], h) for d, h in pairs],
                             [head(v_refs[d], h) for d, h in pairs],
                             [gbs[d][1][:, h:h + 1] for d, h in pairs], [gbs[d][2][h:h + 1, :] for d, h in pairs],
                             [gbs[d][0][:, h:h + 1] for d, h in pairs], [gbs[d][3][:, h:h + 1] for d, h in pairs],
                             S, [masks[d][0] for d, h in pairs], [masks[d][1] for d, h in pairs])
        for i, (d, h) in enumerate(pairs):
            sin_refs[d][0, 0, h] = S[i]
            tm_refs[d][0, 0, h] = r["Tm"][i]
            S_ref[d, h] = r["S_new"][i]
            o_refs[d][0, :, h * hd:(h + 1) * hd] = r["o"][i]

    def dir_specs(d):
        cidx = lambda b, s: _chunk_of(d, s, ncc, nch)
        ins = [pl.BlockSpec((1, CHUNK, HD), lambda b, s, p=p: (b, cidx(b, s), p)) for p in range(3)]
        ins.append(pl.BlockSpec((1, CHUNK, LANES), lambda b, s: (b, cidx(b, s), 0)))
        return (ins, pl.BlockSpec((1, CHUNK, HD), lambda b, s: (b, cidx(b, s), 0)),
                pl.BlockSpec((1, 1, H, hd, hd), lambda b, s: (b, cidx(b, s), 0, 0, 0)),
                pl.BlockSpec((1, 1, H, CHUNK, CHUNK), lambda b, s: (b, cidx(b, s), 0, 0, 0)))

    (in0, o_s0, sin_s0, tm_s0), (in1, o_s1, sin_s1, tm_s1) = dir_specs(0), dir_specs(1)
    o_shape = jax.ShapeDtypeStruct((B, T, HD), F32)
    sin_shape = jax.ShapeDtypeStruct((B, nch, H, hd, hd), F32)
    tm_shape = jax.ShapeDtypeStruct((B, nch, H, CHUNK, CHUNK), F32)
    return pl.pallas_call(
        body, name="delta_fwd", grid=(B, nch),
        in_specs=in0 + in1, out_specs=(o_s0, o_s1, sin_s0, sin_s1, tm_s0, tm_s1),
        out_shape=(o_shape, o_shape, sin_shape, sin_shape, tm_shape, tm_shape),
        scratch_shapes=[pltpu.VMEM((2, H, hd, hd), F32)],
        compiler_params=_cparams(("parallel", "arbitrary"), VMEM_LIMIT),
    )(qkvn, qkvn, qkvn, gb, qkvn, qkvn, qkvn, gb)


def _delta_bwd(qkvn, gb, sin, tms, do, H, hd, ncc):
    B, T, _ = qkvn.shape
    nch = T // CHUNK
    HD = H * hd
    pairs = [(d, h) for d in range(2) for h in range(H)]

    def body(q0, k0, v0, gb0, sin0, tm0, do0, q1, k1, v1, gb1, sin1, tm1, do1, dqkv0, dqkv1, dgb0, dgb1, dS_ref):
        s = pl.program_id(1)
        tm_refs = (tm0, tm1)
        q_refs, k_refs, v_refs, gb_refs = (q0, q1), (k0, k1), (v0, v1), (gb0, gb1)
        sin_refs, do_refs, dqkv_refs, dgb_refs = (sin0, sin1), (do0, do1), (dqkv0, dqkv1), (dgb0, dgb1)

        @pl.when(s == 0)
        def _():
            dS_ref[...] = jnp.zeros_like(dS_ref)

        masks = [_delta_masks(d) for d in range(2)]
        gbs = [_delta_gb(gb_refs[d], d, masks[d][0], masks[d][2], H) for d in range(2)]
        incl = [masks[d][0] for d, h in pairs]
        strict = [masks[d][1] for d, h in pairs]
        lane = lax.broadcasted_iota(jnp.int32, (1, LANES), 1)
        ones = jnp.ones((3 * CHUNK, LANES), BF16)
        head = lambda ref, h: ref[0, :, h * hd:(h + 1) * hd]
        q = [head(q_refs[d], h) for d, h in pairs]
        k = [head(k_refs[d], h) for d, h in pairs]
        v = [head(v_refs[d], h) for d, h in pairs]
        do_ = [head(do_refs[d], h) for d, h in pairs]
        beta_i = [gbs[d][0][:, h:h + 1] for d, h in pairs]
        S = [sin_refs[d][0, 0, h] for d, h in pairs]
        dSn = [dS_ref[d, h] for d, h in pairs]
        f = _delta_chunk_fwd(q, k, v, [gbs[d][1][:, h:h + 1] for d, h in pairs], [gbs[d][2][h:h + 1, :] for d, h in pairs],
                             beta_i, [gbs[d][3][:, h:h + 1] for d, h in pairs], S, incl, strict,
                             Tm=[tm_refs[d][0, 0, h] for d, h in pairs])
        rsum = lambda x: jnp.sum(x, axis=1, keepdims=True)
        dvn = _each(lambda p, d_, kd, ds: _dot(p, d_, "tn") + _dot(kd, ds), f["P"], do_, f["kd"], dSn)
        dP = _each(lambda d_, vn, m: jnp.where(m, _dot(d_, vn, "nt"), 0.0), do_, f["vn"], incl)
        dqg = _each(lambda d_, s: _dot(d_, s, "nt"), do_, S)
        dS_in = _each(lambda qg, d_, g, ds, w, dv_: _dot(qg, d_, "tn") + g * ds - _dot(w, dv_, "tn"),
                      f["qg"], do_, f["Gam"], dSn, f["w"], dvn)
        dGam = _each(lambda s, ds: jnp.sum(rsum(s * ds), axis=0, keepdims=True), S, dSn)
        dkd = _each(lambda vn, ds: _dot(vn, ds, "nt"), f["vn"], dSn)
        de = _each(lambda dkd_, k_, e: rsum(dkd_ * k_) * e, dkd, k, f["e"])
        dw = _each(lambda dv_, s: -_dot(dv_, s, "nt"), dvn, S)
        dXw = _each(lambda t, dw_: _dot(t, dw_, "tn"), f["Tm"], dw)
        dXu = _each(lambda t, dv_: _dot(t, dv_, "tn"), f["Tm"], dvn)
        dA = _each(lambda xw, w, xu, u, m: -jnp.where(m, _dot(xw, w, "nt") + _dot(xu, u, "nt"), 0.0),
                   dXw, f["w"], dXu, f["u"], strict)
        dM = _each(lambda a, d_: a * d_, dA, f["D"])
        dN = _each(lambda p, d_: p * d_, dP, f["D"])
        dkb = _each(lambda m, k_, xw, g: _dot(m, k_) + xw * g, dM, k, dXw, f["gam"])
        dq = _each(lambda dqg_, g, n, k_: dqg_ * g + _dot(n, k_), dqg, f["gam"], dN, k)
        dk = _each(lambda dkd_, e, m, kb, n, q_, dkb_, b: dkd_ * e + _dot(m, kb, "tn") + _dot(n, q_, "tn") + dkb_ * b,
                   dkd, f["e"], dM, f["kb"], dN, q, dkb, beta_i)
        dv = _each(lambda xu, b: xu * b, dXu, beta_i)
        dbeta = _each(lambda dkb_, k_, xu, v_: rsum(dkb_ * k_) + rsum(xu * v_), dkb, k, dXu, v)
        E = _each(lambda a, A_, p, P_: a * A_ + p * P_, dA, f["A"], dP, f["P"])

        def colsum(e):
            return _dot(_rows(*_split3(e)), ones, "tn")[:, 0:1]

        dg = _each(lambda e, dqg_, qg, xw, kb, g, de_: rsum(e) - colsum(e) + rsum(dqg_ * qg) + rsum(xw * kb) * g - de_,
                   E, dqg, f["qg"], dXw, f["kb"], f["gam"], de)
        dgl_h = _each(lambda de_, dg_, g: jnp.sum(de_, axis=0, keepdims=True) + dg_ * g, de, dGam, f["Gam"])
        for d in range(2):
            dgc = jnp.zeros((CHUNK, LANES), F32)
            dgl = jnp.zeros((1, LANES), F32)
            for h in range(H):
                i = d * H + h
                g_lane, b_lane = d * H + h, (2 + d) * H + h
                dgc = dgc + dg[i] * (lane == g_lane).astype(F32) + dbeta[i] * (lane == b_lane).astype(F32)
                dgl = dgl + dgl_h[i] * (lane == g_lane).astype(F32)
                dS_ref[d, h] = dS_in[i]
                dqkv_refs[d][0, :, h * hd:(h + 1) * hd] = dq[i]
                dqkv_refs[d][0, :, HD + h * hd:HD + (h + 1) * hd] = dk[i]
                dqkv_refs[d][0, :, 2 * HD + h * hd:2 * HD + (h + 1) * hd] = dv[i]
            draw = _dot_mask(masks[d][0].astype(BF16), dgc, "tn") + dgl
            dgb_refs[d][0] = jnp.where(lane < 2 * H, draw, dgc)

    def dir_specs(d):
        cidx = lambda b, s: _chunk_of(d, nch - 1 - s, ncc, nch)
        ins = [pl.BlockSpec((1, CHUNK, HD), lambda b, s, p=p: (b, cidx(b, s), p)) for p in range(3)]
        ins += [pl.BlockSpec((1, CHUNK, LANES), lambda b, s: (b, cidx(b, s), 0)),
                pl.BlockSpec((1, 1, H, hd, hd), lambda b, s: (b, cidx(b, s), 0, 0, 0)),
                pl.BlockSpec((1, 1, H, CHUNK, CHUNK), lambda b, s: (b, cidx(b, s), 0, 0, 0)),
                pl.BlockSpec((1, CHUNK, HD), lambda b, s: (b, cidx(b, s), 0))]
        return (ins, pl.BlockSpec((1, CHUNK, 3 * HD), lambda b, s: (b, cidx(b, s), 0)),
                pl.BlockSpec((1, CHUNK, LANES), lambda b, s: (b, cidx(b, s), 0)))

    (in0, dq_s0, dg_s0), (in1, dq_s1, dg_s1) = dir_specs(0), dir_specs(1)
    dq_shape = jax.ShapeDtypeStruct((B, T, 3 * HD), F32)
    dg_shape = jax.ShapeDtypeStruct((B, T, LANES), F32)
    return pl.pallas_call(
        body, name="delta_bwd", grid=(B, nch),
        in_specs=in0 + in1, out_specs=(dq_s0, dq_s1, dg_s0, dg_s1),
        out_shape=(dq_shape, dq_shape, dg_shape, dg_shape),
        scratch_shapes=[pltpu.VMEM((2, H, hd, hd), F32)],
        compiler_params=_cparams(("parallel", "arbitrary"), VMEM_LIMIT),
    )(qkvn, qkvn, qkvn, gb, sin[0], tms[0], do, qkvn, qkvn, qkvn, gb, sin[1], tms[1], do)


def _dn_out(o2, pz, zcb0, onorm, H, hd, nct, tm):
    B, T, HD = o2[0].shape
    N = T - nct * tm

    def body(o0_ref, o1_ref, z_ref, g_ref, y_ref):
        for h in range(H):
            sl = slice(h * hd, (h + 1) * hd)
            o = o0_ref[0, :, sl] + o1_ref[0, :, sl]
            r = lax.rsqrt(jnp.mean(o * o, axis=-1, keepdims=True) + EPS)
            y_ref[0, :, sl] = (o * r * g_ref[...] * _silu(z_ref[0, :, sl].astype(F32))).astype(BF16)

    return pl.pallas_call(
        body, name="dn_out", grid=(B, N // tm),
        in_specs=[pl.BlockSpec((1, tm, HD), lambda b, t: (b, t + nct, 0)),
                  pl.BlockSpec((1, tm, HD), lambda b, t: (b, t + nct, 0)),
                  pl.BlockSpec((1, tm, HD), lambda b, t: (b, t + nct, zcb0)),
                  pl.BlockSpec((1, hd), lambda b, t: (0, 0))],
        out_specs=pl.BlockSpec((1, tm, HD), lambda b, t: (b, t, 0)),
        out_shape=jax.ShapeDtypeStruct((B, N, HD), BF16),
        compiler_params=_cparams(("parallel",) * 2, VMEM_LIMIT))(o2[0], o2[1], pz, onorm)


def _dn_out_bwd(o2, pz, zcb0, onorm, dy, H, hd, nct, tm):
    B, T, HD = o2[0].shape
    nt = T // tm

    def body(o0_ref, o1_ref, z_ref, g_ref, dy_ref, do_ref, dz_ref, acc_ref):
        t = pl.program_id(1)

        @pl.when(t == 0)
        def _():
            acc_ref[...] = jnp.zeros_like(acc_ref)

        @pl.when(t < nct)
        def _():
            do_ref[0] = jnp.zeros_like(do_ref[0])
            dz_ref[0] = jnp.zeros_like(dz_ref[0])

        @pl.when(t >= nct)
        def _():
            g = g_ref[...]
            for h in range(H):
                sl = slice(h * hd, (h + 1) * hd)
                o = o0_ref[0, :, sl] + o1_ref[0, :, sl]
                z = z_ref[0, :, sl].astype(F32)
                dy_ = dy_ref[0, :, sl]
                r = lax.rsqrt(jnp.mean(o * o, axis=-1, keepdims=True) + EPS)
                oh = o * r
                dz_ref[0, :, sl] = (dy_ * oh * g * _dsilu(z)).astype(BF16)
                dn = dy_ * _silu(z)
                acc_ref[0, 0:1, :] += jnp.sum(dn * oh, axis=0, keepdims=True)
                doh = dn * g
                do_ref[0, :, sl] = r * (doh - oh * jnp.mean(doh * oh, axis=-1, keepdims=True))

    lat = lambda b, t: (b, jnp.maximum(t - nct, 0), 0)
    row = pl.BlockSpec((1, tm, HD), lambda b, t: (b, t, 0))
    return pl.pallas_call(
        body, name="dn_out_bwd", grid=(B, nt),
        in_specs=[row, row,
                  pl.BlockSpec((1, tm, HD), lambda b, t: (b, t, zcb0)),
                  pl.BlockSpec((1, hd), lambda b, t: (0, 0)),
                  pl.BlockSpec((1, tm, HD), lat)],
        out_specs=(row, row, pl.BlockSpec((1, SUBLANES, hd), lambda b, t: (b, 0, 0))),
        out_shape=(jax.ShapeDtypeStruct((B, T, HD), F32), jax.ShapeDtypeStruct((B, T, HD), BF16),
                   jax.ShapeDtypeStruct((B, SUBLANES, hd), F32)),
        compiler_params=_cparams(("parallel", "arbitrary"), VMEM_LIMIT),
    )(o2[0], o2[1], pz, onorm, dy)


def _lru_gate_math(x, wr, wi, br, bi, lam):
    r = _sigmoid(_dot(x, wr) + br)
    i = _sigmoid(_dot(x, wi) + bi)
    sp = _softplus(-lam)
    la = -LRU_C * r * sp
    a = jnp.exp(la)
    s = jnp.sqrt(-jnp.tanh(la) * (a * a + 1.0))
    return r, i, sp, a, s


def _lru_gates(xc, wbd, bias4, lam, tm, bw):
    B, T, W = xc.shape

    def body(x_ref, w_ref, b_ref, l_ref, a_ref, i_ref):
        x = x_ref[0]
        for d in range(2):
            _, i, _, a, s = _lru_gate_math(x, w_ref[2 * d, 0], w_ref[2 * d + 1, 0],
                                           b_ref[2 * d:2 * d + 1, :], b_ref[2 * d + 1:2 * d + 2, :], l_ref[d:d + 1, :])
            a_ref[d, 0] = a
            i_ref[d, 0] = s * (i * x)

    out = pl.BlockSpec((2, 1, tm, bw), lambda b, t, j: (0, b, t, j))
    return pl.pallas_call(
        body, name="lru_gates", grid=(B, T // tm, W // bw),
        in_specs=[pl.BlockSpec((1, tm, bw), lambda b, t, j: (b, t, j)),
                  pl.BlockSpec((4, 1, bw, bw), lambda b, t, j: (0, j, 0, 0)),
                  pl.BlockSpec((4, bw), lambda b, t, j: (0, j)),
                  pl.BlockSpec((2, bw), lambda b, t, j: (0, j))],
        out_specs=(out, out),
        out_shape=(jax.ShapeDtypeStruct((2, B, T, W), F32), jax.ShapeDtypeStruct((2, B, T, W), F32)),
        compiler_params=_cparams(("parallel",) * 3, VMEM_LIMIT))(xc, wbd, bias4, lam)


def _lru_gates_bwd(xc, wbd, bias4, lam, dinp, da, tm, bw):
    B, T, W = xc.shape
    nt = T // tm

    def body(x_ref, w_ref, b_ref, l_ref, di0_ref, di1_ref, da0_ref, da1_ref, dx_ref, dw_ref, acc_ref):
        di_refs, da_refs = (di0_ref, di1_ref), (da0_ref, da1_ref)
        b_ = pl.program_id(1)
        t = pl.program_id(2)

        @pl.when(jnp.logical_and(b_ == 0, t == 0))
        def _():
            dw_ref[...] = jnp.zeros_like(dw_ref)
            acc_ref[...] = jnp.zeros_like(acc_ref)

        x = x_ref[0]
        dx = jnp.zeros_like(x)
        for d in range(2):
            wr, wi = w_ref[2 * d, 0], w_ref[2 * d + 1, 0]
            lam_ = l_ref[d:d + 1, :]
            r, i, sp, a, s = _lru_gate_math(x, wr, wi, b_ref[2 * d:2 * d + 1, :], b_ref[2 * d + 1:2 * d + 2, :], lam_)
            dinp_ = di_refs[d][0]
            dix = dinp_ * s
            ds = dinp_ * i * x
            dla = da_refs[d][0] * a - ds * (a * a) / s
            dpr = dla * (-LRU_C * sp) * r * (1.0 - r)
            dpi = dix * x * i * (1.0 - i)
            dx = dx + dix * i + _dot(dpr, wr, "nt") + _dot(dpi, wi, "nt")
            dw_ref[2 * d, 0] += _dot(x, dpr, "tn")
            dw_ref[2 * d + 1, 0] += _dot(x, dpi, "tn")
            acc_ref[2 * d:2 * d + 1, :] += jnp.sum(dpr, axis=0, keepdims=True)
            acc_ref[2 * d + 1:2 * d + 2, :] += jnp.sum(dpi, axis=0, keepdims=True)
            acc_ref[4 + d:5 + d, :] += jnp.sum(dla * (-LRU_C * r), axis=0, keepdims=True) * (-_sigmoid(-lam_))
        dx_ref[0] = dx

    tile = pl.BlockSpec((1, tm, bw), lambda j, b, t: (b, t, j))
    return pl.pallas_call(
        body, name="lru_gates_bwd", grid=(W // bw, B, nt),
        in_specs=[pl.BlockSpec((1, tm, bw), lambda j, b, t: (b, t, j)),
                  pl.BlockSpec((4, 1, bw, bw), lambda j, b, t: (0, j, 0, 0)),
                  pl.BlockSpec((4, bw), lambda j, b, t: (0, j)),
                  pl.BlockSpec((2, bw), lambda j, b, t: (0, j)), tile, tile, tile, tile],
        out_specs=(pl.BlockSpec((1, tm, bw), lambda j, b, t: (b, t, j)),
                   pl.BlockSpec((4, 1, bw, bw), lambda j, b, t: (0, j, 0, 0)),
                   pl.BlockSpec((SUBLANES, bw), lambda j, b, t: (0, j))),
        out_shape=(jax.ShapeDtypeStruct((B, T, W), F32), jax.ShapeDtypeStruct(wbd.shape, F32),
                   jax.ShapeDtypeStruct((SUBLANES, W), F32)),
        compiler_params=_cparams(("parallel", "arbitrary", "arbitrary"), VMEM_LIMIT),
    )(xc, wbd, bias4, lam, dinp[0], dinp[1], da[0], da[1])


def _tile_scan(a, x, rev, tm):
    row = lax.broadcasted_iota(jnp.int32, a.shape, 0)
    s = 1
    while s < tm:
        if rev:
            a_sh, x_sh, ok = pltpu.roll(a, tm - s, 0), pltpu.roll(x, tm - s, 0), row < tm - s
        else:
            a_sh, x_sh, ok = pltpu.roll(a, s, 0), pltpu.roll(x, s, 0), row >= s
        x = jnp.where(ok, x + a * x_sh, x)
        a = jnp.where(ok, a * a_sh, a)
        s *= 2
    return a, x


def _scan_tile_of(s, rev, nct, nt):
    if not rev:
        return s
    return jnp.where(s < nct, nct - 1 - s, nt - 1 - (s - nct))


def _lru_scan(a2, x2, d, nct, tm, tc):
    _, B, T, W = a2.shape
    nt = T // tm
    rev = d == 1
    last = 0 if rev else tm - 1

    def body(a_ref, x_ref, h_ref, carry):
        s = pl.program_id(2)

        @pl.when(s == 0)
        def _():
            carry[...] = jnp.zeros_like(carry)

        A, X = _tile_scan(a_ref[0, 0], x_ref[0, 0], rev, tm)
        h = X + A * carry[0:1, :]
        h_ref[0] = h
        carry[...] = jnp.broadcast_to(h[last:last + 1, :], carry.shape)

    tidx = lambda s: _scan_tile_of(s, rev, nct, nt)
    spec = pl.BlockSpec((1, 1, tm, tc), lambda b, j, s: (d, b, tidx(s), j))
    return pl.pallas_call(
        body, name=f"lru_scan{d}", grid=(B, W // tc, nt),
        in_specs=[spec, spec], out_specs=pl.BlockSpec((1, tm, tc), lambda b, j, s: (b, tidx(s), j)),
        out_shape=jax.ShapeDtypeStruct((B, T, W), F32),
        scratch_shapes=[pltpu.VMEM((SUBLANES, tc), F32)],
        compiler_params=_cparams(("parallel", "parallel", "arbitrary"), VMEM_LIMIT),
    )(a2, x2)


def _lru_scan_bwd(a2, h, dh, d, nct, tm, tc):
    _, B, T, W = a2.shape
    nt = T // tm
    rev = d == 1
    first = tm - 1 if rev else 0
    r8, n8 = tm // SUBLANES, T // SUBLANES

    def body(a_ref, h_ref, hp_ref, dh_ref, lam_ref, da_ref, ca, cl):
        s = pl.program_id(2)

        @pl.when(s == 0)
        def _():
            ca[...] = jnp.zeros_like(ca)
            cl[...] = jnp.zeros_like(cl)

        a = a_ref[0, 0]
        row = lax.broadcasted_iota(jnp.int32, a.shape, 0)
        if rev:
            c = jnp.where(row == 0, ca[0:1, :], pltpu.roll(a, 1, 0))
        else:
            c = jnp.where(row == tm - 1, ca[0:1, :], pltpu.roll(a, tm - 1, 0))
        C, L = _tile_scan(c, dh_ref[0], not rev, tm)
        lam = L + C * cl[0:1, :]
        lam_ref[0] = lam
        hp = jnp.where(s == nt - 1, 0.0, hp_ref[0])
        if rev:
            hprev = jnp.where(row == tm - 1, hp[0:1, :], pltpu.roll(h_ref[0], tm - 1, 0))
        else:
            hprev = jnp.where(row == 0, hp[SUBLANES - 1:SUBLANES, :], pltpu.roll(h_ref[0], 1, 0))
        da_ref[0] = lam * hprev
        ca[...] = jnp.broadcast_to(a[first:first + 1, :], ca.shape)
        cl[...] = jnp.broadcast_to(lam[first:first + 1, :], cl.shape)

    tidx = lambda s: _scan_tile_of(nt - 1 - s, rev, nct, nt)

    def hp_idx(b, j, s):
        tt = tidx(s)
        if rev:
            blk = jnp.where(tt == nt - 1, 0, jnp.minimum((tt + 1) * r8, n8 - 1))
        else:
            blk = jnp.maximum(tt * r8 - 1, 0)
        return (b, blk, j)

    tile = pl.BlockSpec((1, tm, tc), lambda b, j, s: (b, tidx(s), j))
    return pl.pallas_call(
        body, name=f"lru_scan_bwd{d}", grid=(B, W // tc, nt),
        in_specs=[pl.BlockSpec((1, 1, tm, tc), lambda b, j, s: (d, b, tidx(s), j)), tile,
                  pl.BlockSpec((1, SUBLANES, tc), hp_idx), tile],
        out_specs=(tile, tile),
        out_shape=(jax.ShapeDtypeStruct((B, T, W), F32), jax.ShapeDtypeStruct((B, T, W), F32)),
        scratch_shapes=[pltpu.VMEM((SUBLANES, tc), F32), pltpu.VMEM((SUBLANES, tc), F32)],
        compiler_params=_cparams(("parallel", "parallel", "arbitrary"), VMEM_LIMIT),
    )(a2, h, h, dh)


def _lru_out(h0, h1, pyl, ycb0, nct, tm, tc):
    B, T, W = h0.shape
    N = T - nct * tm

    def body(h0_ref, h1_ref, y_ref, o_ref):
        o_ref[0] = ((h0_ref[0] + h1_ref[0]) * _gelu(y_ref[0].astype(F32))).astype(BF16)

    hs = pl.BlockSpec((1, tm, tc), lambda b, t, j: (b, t + nct, j))
    return pl.pallas_call(
        body, name="lru_out", grid=(B, N // tm, W // tc),
        in_specs=[hs, hs, pl.BlockSpec((1, tm, tc), lambda b, t, j: (b, t + nct, ycb0 + j))],
        out_specs=pl.BlockSpec((1, tm, tc), lambda b, t, j: (b, t, j)),
        out_shape=jax.ShapeDtypeStruct((B, N, W), BF16),
        compiler_params=_cparams(("parallel",) * 3, VMEM_LIMIT))(h0, h1, pyl)


def _lru_out_bwd(h0, h1, pyl, ycb0, dy, nct, tm, tc):
    B, T, W = h0.shape

    def body(h0_ref, h1_ref, y_ref, dy_ref, dh_ref, dyl_ref):
        t = pl.program_id(1)

        @pl.when(t < nct)
        def _():
            dh_ref[0] = jnp.zeros_like(dh_ref[0])
            dyl_ref[0] = jnp.zeros_like(dyl_ref[0])

        @pl.when(t >= nct)
        def _():
            y = y_ref[0].astype(F32)
            dy_ = dy_ref[0]
            dh_ref[0] = dy_ * _gelu(y)
            dyl_ref[0] = (dy_ * (h0_ref[0] + h1_ref[0]) * _dgelu(y)).astype(BF16)

    hs = pl.BlockSpec((1, tm, tc), lambda b, t, j: (b, t, j))
    return pl.pallas_call(
        body, name="lru_out_bwd", grid=(B, T // tm, W // tc),
        in_specs=[hs, hs, pl.BlockSpec((1, tm, tc), lambda b, t, j: (b, t, ycb0 + j)),
                  pl.BlockSpec((1, tm, tc), lambda b, t, j: (b, jnp.maximum(t - nct, 0), j))],
        out_specs=(hs, hs),
        out_shape=(jax.ShapeDtypeStruct((B, T, W), F32), jax.ShapeDtypeStruct((B, T, W), BF16)),
        compiler_params=_cparams(("parallel",) * 3, VMEM_LIMIT))(h0, h1, pyl, dy)


def _merge(bd, bl, pmg, bm, nct, tm):
    B, N, D = bd.shape

    def body(bd_ref, bl_ref, m0_ref, m1_ref, b_ref, o_ref):
        g0 = _sigmoid(m0_ref[0].astype(F32) + b_ref[:, 0:D])
        g1 = _sigmoid(m1_ref[0].astype(F32) + b_ref[:, D:2 * D])
        o_ref[0] = (g0 * bd_ref[0].astype(F32) + g1 * bl_ref[0].astype(F32)).astype(BF16)

    row = pl.BlockSpec((1, tm, D), lambda b, t: (b, t, 0))
    return pl.pallas_call(
        body, name="merge", grid=(B, N // tm),
        in_specs=[row, row, pl.BlockSpec((1, tm, D), lambda b, t: (b, t + nct, 0)),
                  pl.BlockSpec((1, tm, D), lambda b, t: (b, t + nct, 1)),
                  pl.BlockSpec((1, 2 * D), lambda b, t: (0, 0))],
        out_specs=row, out_shape=jax.ShapeDtypeStruct((B, N, D), BF16),
        compiler_params=_cparams(("parallel", "parallel"), VMEM_LIMIT))(bd, bl, pmg, pmg, bm)


def _merge_bwd(dmi, bd, bl, pmg, bm, nct, tm):
    B, N, D = bd.shape
    T = pmg.shape[1]

    def body(dm_ref, bd_ref, bl_ref, m0_ref, m1_ref, b_ref, dbd_ref, dbl_ref, dmg_ref, acc_ref):
        t = pl.program_id(1)

        @pl.when(t == 0)
        def _():
            acc_ref[...] = jnp.zeros_like(acc_ref)

        @pl.when(t < nct)
        def _():
            dmg_ref[0] = jnp.zeros_like(dmg_ref[0])

        @pl.when(t >= nct)
        def _():
            dm = dm_ref[0]
            g0 = _sigmoid(m0_ref[0].astype(F32) + b_ref[:, 0:D])
            g1 = _sigmoid(m1_ref[0].astype(F32) + b_ref[:, D:2 * D])
            dbd_ref[0] = (dm * g0).astype(BF16)
            dbl_ref[0] = (dm * g1).astype(BF16)
            d0 = dm * bd_ref[0].astype(F32) * g0 * (1.0 - g0)
            d1 = dm * bl_ref[0].astype(F32) * g1 * (1.0 - g1)
            dmg_ref[0, :, 0:D] = d0.astype(BF16)
            dmg_ref[0, :, D:2 * D] = d1.astype(BF16)
            acc_ref[0, 0:1, 0:D] += jnp.sum(d0, axis=0, keepdims=True)
            acc_ref[0, 0:1, D:2 * D] += jnp.sum(d1, axis=0, keepdims=True)

    lat = pl.BlockSpec((1, tm, D), lambda b, t: (b, jnp.maximum(t - nct, 0), 0))
    return pl.pallas_call(
        body, name="merge_bwd", grid=(B, T // tm),
        in_specs=[lat, lat, lat, pl.BlockSpec((1, tm, D), lambda b, t: (b, t, 0)),
                  pl.BlockSpec((1, tm, D), lambda b, t: (b, t, 1)),
                  pl.BlockSpec((1, 2 * D), lambda b, t: (0, 0))],
        out_specs=(lat, lat, pl.BlockSpec((1, tm, 2 * D), lambda b, t: (b, t, 0)),
                   pl.BlockSpec((1, SUBLANES, 2 * D), lambda b, t: (b, 0, 0))),
        out_shape=(jax.ShapeDtypeStruct((B, N, D), BF16), jax.ShapeDtypeStruct((B, N, D), BF16),
                   jax.ShapeDtypeStruct((B, T, 2 * D), BF16), jax.ShapeDtypeStruct((B, SUBLANES, 2 * D), F32)),
        compiler_params=_cparams(("parallel", "arbitrary"), VMEM_LIMIT))(dmi, bd, bl, pmg, pmg, bm)


def _grid_taps():
    return [((di + 1) * 3 + (dj + 1), di, dj) for di in (-1, 0, 1) for dj in (-1, 0, 1)]


def _grid_views(x, n):
    pos = lax.broadcasted_iota(jnp.int32, x.shape, 0)
    gc = jnp.bitwise_and(pos, GRID_W - 1)
    cols = {0: x,
            -1: jnp.where(gc >= 1, pltpu.roll(x, 1, 0), 0.0),
            1: jnp.where(gc <= GRID_W - 2, pltpu.roll(x, n - 1, 0), 0.0)}
    views = {}
    edge = jnp.zeros((GRID_W, x.shape[1]), x.dtype)
    for dj, xc in cols.items():
        views[(0, dj)] = xc
        views[(-1, dj)] = jnp.concatenate([edge, xc[:n - GRID_W]], axis=0)
        views[(1, dj)] = jnp.concatenate([xc[GRID_W:], edge], axis=0)
    return views


def _ffn_act(Fg, Fv, w9, b, tc):
    B, N, Cf = Fg.shape
    ncb = Cf // tc

    def body(g_ref, v_ref, w_ref, b_ref, o_ref, gel_ref, dgel_ref):
        xv = _grid_views(g_ref[0], N)
        conv = b_ref[...]
        for k, di, dj in _grid_taps():
            conv = conv + xv[(di, dj)] * w_ref[k:k + 1, :]
        th = jnp.tanh(GELU_C * (conv + 0.044715 * conv * conv * conv))
        gel = 0.5 * conv * (1.0 + th)
        o_ref[0] = (gel * v_ref[0]).astype(BF16)
        gel_ref[0] = gel.astype(BF16)
        dgel_ref[0] = (0.5 * (1.0 + th)
                       + 0.5 * conv * (1.0 - th * th) * GELU_C * (1.0 + 3.0 * 0.044715 * conv * conv)).astype(BF16)

    tile = pl.BlockSpec((1, N, tc), lambda b, j: (b, 0, j))
    out = jax.ShapeDtypeStruct((B, N, Cf), BF16)
    return pl.pallas_call(
        body, name="ffn_act", grid=(B, ncb),
        in_specs=[tile, tile,
                  pl.BlockSpec((9, tc), lambda b, j: (0, j)),
                  pl.BlockSpec((1, tc), lambda b, j: (0, j))],
        out_specs=(tile, tile, tile), out_shape=(out, out, out),
        compiler_params=_cparams(("parallel", "parallel"), VMEM_LIMIT))(Fg, Fv, w9, b)


def _ffn_act_bwd(Fg, Fv, w9, gel, dgel, df, tc):
    B, N, Cf = Fg.shape
    C2 = 2 * Cf
    ncb = Cf // tc

    def body(g_ref, v_ref, w_ref, gel_ref, dgel_ref, df_ref, dF_ref, acc_ref):
        p = pl.program_id(2)

        @pl.when(p == 0)
        def _():
            dF_ref[0] = (df_ref[0] * gel_ref[0].astype(F32)).astype(BF16)

        @pl.when(p == 1)
        def _():
            xv = _grid_views(g_ref[0], N)
            df_ = df_ref[0]
            dc = df_ * v_ref[0] * dgel_ref[0].astype(F32)
            dcv = _grid_views(dc, N)
            dx = None
            for k, di, dj in _grid_taps():
                term = dcv[(-di, -dj)] * w_ref[k:k + 1, :]
                dx = term if dx is None else dx + term
                acc_ref[0, k:k + 1, :] = jnp.sum(dc * xv[(di, dj)], axis=0, keepdims=True)
            acc_ref[0, 9:10, :] = jnp.sum(dc, axis=0, keepdims=True)
            acc_ref[0, 10:16, :] = jnp.zeros((6, tc), F32)
            dF_ref[0] = dx.astype(BF16)

    tile = pl.BlockSpec((1, N, tc), lambda b, j, p: (b, 0, j))
    return pl.pallas_call(
        body, name="ffn_act_bwd", grid=(B, ncb, 2),
        in_specs=[tile, tile, pl.BlockSpec((9, tc), lambda b, j, p: (0, j)), tile, tile, tile],
        out_specs=(pl.BlockSpec((1, N, tc), lambda b, j, p: (b, 0, j + (1 - p) * ncb)),
                   pl.BlockSpec((1, 16, tc), lambda b, j, p: (b, 0, j))),
        out_shape=(jax.ShapeDtypeStruct((B, N, C2), BF16), jax.ShapeDtypeStruct((B, 16, Cf), F32)),
        compiler_params=_cparams(("parallel", "parallel", "arbitrary"), VMEM_LIMIT))(Fg, Fv, w9, gel, dgel, df)


ADAM_ROWS = 512


def _adamw(w, m, v, gparts, name):
    rows, cols = w.shape[-2:]
    P = gparts.shape[0]
    tr = _row_tile(rows, (P + 14) * 4 * (-(-cols // LANES) * LANES))
    c1 = 1.0 - ADAM_B1 ** ADAM_STEP
    c2 = 1.0 - ADAM_B2 ** ADAM_STEP

    def body(w_ref, m_ref, v_ref, g_ref, go_ref, d_ref, mo_ref, vo_ref):
        g = g_ref[0].astype(F32)
        for p in range(1, P):
            g = g + g_ref[p].astype(F32)
        m_ = ADAM_B1 * m_ref[...] + (1.0 - ADAM_B1) * g
        v_ = ADAM_B2 * v_ref[...] + (1.0 - ADAM_B2) * (g * g)
        go_ref[...] = g
        mo_ref[...] = m_
        vo_ref[...] = v_
        d_ref[...] = -ADAM_LR * ((m_ / c1) / (jnp.sqrt(v_ / c2) + ADAM_EPS) + ADAM_WD * w_ref[...])

    if w.ndim == 3:
        row = pl.BlockSpec((None, tr, cols), lambda i: (0, i, 0))
    else:
        row = pl.BlockSpec((tr, cols), lambda i: (i, 0))
    out = jax.ShapeDtypeStruct(w.shape, F32)
    return pl.pallas_call(
        body, name=name, grid=(rows // tr,),
        in_specs=[row, row, row, pl.BlockSpec((P, tr, cols), lambda i: (0, i, 0))],
        out_specs=(row, row, row, row), out_shape=(out, out, out, out),
        compiler_params=_cparams(("parallel",), VMEM_LIMIT))(w, m, v, gparts)


def _pack_rows(flat_len):
    rows = -(-flat_len // LANES)
    if rows > ADAM_ROWS:
        rows = -(-rows // ADAM_ROWS) * ADAM_ROWS
    else:
        rows = -(-rows // SUBLANES) * SUBLANES
    return rows


PACK_ALIGN = SUBLANES * LANES


def _pack(arrs, lead=()):
    pads = [(0, 0)] * len(lead)
    parts = []
    for a in arrs:
        f = a.reshape(lead + (-1,)).astype(F32)
        parts.append(jnp.pad(f, pads + [(0, -f.shape[-1] % PACK_ALIGN)]))
    flat = jnp.concatenate(parts, axis=-1)
    n = flat.shape[-1]
    rows = _pack_rows(n)
    flat = jnp.pad(flat, pads + [(0, rows * LANES - n)])
    return flat.reshape(lead + (rows, LANES))


def _unpack(buf, shapes):
    out, r = [], 0
    for s in shapes:
        n = math.prod(s)
        nr = -(-n // PACK_ALIGN) * SUBLANES
        out.append(buf[r:r + nr].reshape(-1)[:n].reshape(s))
        r += nr
    return out


def _col_shards(full):
    C = full.shape[-1]
    x = full.reshape(full.shape[:-1] + (N_DEV, C // N_DEV))
    return jnp.moveaxis(x, -2, 0)


def _from_col_shards(g):
    x = jnp.moveaxis(g, 0, -2)
    return x.reshape(x.shape[:-2] + (x.shape[-2] * x.shape[-1],))


def kernel(x, c, ctx, c_ctx, w_ada, b_ada, g_pre_mix, g_post_mix, g_pre_ffn, g_post_ffn, w_in, b_merge, dn_conv, dn_a_log, dn_dt_bias, dn_onorm, lru_conv, lru_conv_b, lru_w_rg, lru_b_rg, lru_w_ig, lru_b_ig, lru_lambda, w_branch_dn, w_branch_lru, w_out, w_up, ffn_dw, ffn_dw_b, w_down, loss_target, m_c_ctx, m_w_ada, m_b_ada, m_g_pre_mix, m_g_post_mix, m_g_pre_ffn, m_g_post_ffn, m_w_in, m_b_merge, m_dn_conv, m_dn_a_log, m_dn_dt_bias, m_dn_onorm, m_lru_conv, m_lru_conv_b, m_lru_w_rg, m_lru_b_rg, m_lru_w_ig, m_lru_b_ig, m_lru_lambda, m_w_branch_dn, m_w_branch_lru, m_w_out, m_w_up, m_ffn_dw, m_ffn_dw_b, m_w_down, v_c_ctx, v_w_ada, v_b_ada, v_g_pre_mix, v_g_post_mix, v_g_pre_ffn, v_g_post_ffn, v_w_in, v_b_merge, v_dn_conv, v_dn_a_log, v_dn_dt_bias, v_dn_onorm, v_lru_conv, v_lru_conv_b, v_lru_w_rg, v_lru_b_rg, v_lru_w_ig, v_lru_b_ig, v_lru_lambda, v_w_branch_dn, v_w_branch_lru, v_w_out, v_w_up, v_ffn_dw, v_ffn_dw_b, v_w_down):
    params = dict(c_ctx=c_ctx, w_ada=w_ada, b_ada=b_ada, g_pre_mix=g_pre_mix, g_post_mix=g_post_mix, g_pre_ffn=g_pre_ffn, g_post_ffn=g_post_ffn, w_in=w_in, b_merge=b_merge, dn_conv=dn_conv, dn_a_log=dn_a_log, dn_dt_bias=dn_dt_bias, dn_onorm=dn_onorm, lru_conv=lru_conv, lru_conv_b=lru_conv_b, lru_w_rg=lru_w_rg, lru_b_rg=lru_b_rg, lru_w_ig=lru_w_ig, lru_b_ig=lru_b_ig, lru_lambda=lru_lambda, w_branch_dn=w_branch_dn, w_branch_lru=w_branch_lru, w_out=w_out, w_up=w_up, ffn_dw=ffn_dw, ffn_dw_b=ffn_dw_b, w_down=w_down)
    mom1 = dict(c_ctx=m_c_ctx, w_ada=m_w_ada, b_ada=m_b_ada, g_pre_mix=m_g_pre_mix, g_post_mix=m_g_post_mix, g_pre_ffn=m_g_pre_ffn, g_post_ffn=m_g_post_ffn, w_in=m_w_in, b_merge=m_b_merge, dn_conv=m_dn_conv, dn_a_log=m_dn_a_log, dn_dt_bias=m_dn_dt_bias, dn_onorm=m_dn_onorm, lru_conv=m_lru_conv, lru_conv_b=m_lru_conv_b, lru_w_rg=m_lru_w_rg, lru_b_rg=m_lru_b_rg, lru_w_ig=m_lru_w_ig, lru_b_ig=m_lru_b_ig, lru_lambda=m_lru_lambda, w_branch_dn=m_w_branch_dn, w_branch_lru=m_w_branch_lru, w_out=m_w_out, w_up=m_w_up, ffn_dw=m_ffn_dw, ffn_dw_b=m_ffn_dw_b, w_down=m_w_down)
    mom2 = dict(c_ctx=v_c_ctx, w_ada=v_w_ada, b_ada=v_b_ada, g_pre_mix=v_g_pre_mix, g_post_mix=v_g_post_mix, g_pre_ffn=v_g_pre_ffn, g_post_ffn=v_g_post_ffn, w_in=v_w_in, b_merge=v_b_merge, dn_conv=v_dn_conv, dn_a_log=v_dn_a_log, dn_dt_bias=v_dn_dt_bias, dn_onorm=v_dn_onorm, lru_conv=v_lru_conv, lru_conv_b=v_lru_conv_b, lru_w_rg=v_lru_w_rg, lru_b_rg=v_lru_b_rg, lru_w_ig=v_lru_w_ig, lru_b_ig=v_lru_b_ig, lru_lambda=v_lru_lambda, w_branch_dn=v_w_branch_dn, w_branch_lru=v_w_branch_lru, w_out=v_w_out, w_up=v_w_up, ffn_dw=v_ffn_dw, ffn_dw_b=v_ffn_dw_b, w_down=v_w_down)
    names = list(params)

    B, N, D = x.shape
    NC = ctx.shape[1]
    T = NC + N
    H, hd = dn_a_log.shape[2], dn_onorm.shape[1]
    HD = H * hd
    nd = 2 * H
    W = lru_conv_b.shape[1]
    nblk, bdim = lru_w_rg.shape[2], lru_w_rg.shape[3]
    Cf = ffn_dw_b.shape[1]
    sw = w_ada.shape[2]
    tm = min(256, NC)
    nct = NC // tm
    ncc = NC // CHUNK
    nch = T // CHUNK
    R, RL = B * T, B * N
    bw = min(256, W)
    me = _my_index()
    assert NC % tm == 0 and N % tm == 0 and B + 1 <= SUBLANES and bw % bdim == 0 and D % LANES == 0

    cpad = jnp.zeros((SUBLANES, D), F32).at[:B].set(c).at[B].set(c_ctx)
    ccp = _all_gather([cpad], "ag_cond")[0].reshape(N_DEV * SUBLANES, D)
    mods_sh = _ada_fwd(ccp, w_ada[0], lax.dynamic_slice(b_ada, (0, me * sw), (1, sw)))
    lru_vecs = jnp.concatenate([lru_b_rg[0], lru_b_ig[0], lru_lambda[0], jnp.zeros_like(lru_lambda[0])], axis=0)
    mods_g, w_in_g, dn_conv_g, lru_conv_g, lru_vecs_g = _all_gather(
        [mods_sh, w_in[0].astype(BF16).T, dn_conv[0], lru_conv[0], lru_vecs], "ag_first")
    ag_mix = _split_start([w_branch_dn[0].astype(BF16), w_branch_lru[0].astype(BF16), w_out[0].astype(BF16)],
                          "gather", "ag_mix_start")
    ag_ffn = _split_start([w_up[0].astype(BF16) + ag_mix[4][0, 0].astype(BF16), w_down[0].astype(BF16),
                           ffn_dw[0].reshape(9, -1)], "gather", "ag_ffn_start")
    mine = lax.dynamic_slice(mods_g, (0, me * SUBLANES, 0), (N_DEV, SUBLANES, sw))
    mods = jnp.moveaxis(mine, 0, 1).reshape(SUBLANES, 6, D)[:B + 1]
    mods = jnp.pad(mods, ((0, 0), (0, SUBLANES - 6), (0, 0))) + ag_ffn[4][0, 0]
    dn_conv_f = _from_col_shards(dn_conv_g)
    lru_conv_f = _from_col_shards(lru_conv_g)
    lru_vecs_f = _from_col_shards(lru_vecs_g)
    lru_lam_f = lru_vecs_f[4:6]

    csh = w_in_g.shape[1]
    w_in_t = w_in_g.reshape(N_DEV * csh, D)
    o_z, o_a, o_xl = 3 * HD, 4 * HD, 4 * HD + 2 * nd
    o_yl, o_mg = o_xl + W, o_xl + 2 * W
    w_qkv, w_z = w_in_t[:o_z], w_in_t[o_z:o_a]
    w_ab = jnp.pad(w_in_t[o_a:o_xl], ((0, LANES - 2 * nd), (0, 0)))
    w_xl, w_yl, w_mg = w_in_t[o_xl:o_yl], w_in_t[o_yl:o_mg], w_in_t[o_mg:]

    def blockdiag(wg):
        per = bw // bdim
        g5 = wg.reshape(2, W // bw, per, bdim, bdim)
        eye = jnp.eye(per, dtype=wg.dtype)
        return jnp.einsum("dtpij,pq->dtpiqj", g5, eye).reshape(2, W // bw, bw, bw)

    bd_rg, bd_ig = blockdiag(lru_w_rg[0]), blockdiag(lru_w_ig[0])
    wbd = jnp.stack([bd_rg[0], bd_ig[0], bd_rg[1], bd_ig[1]]).astype(BF16)
    bias4 = jnp.stack([lru_vecs_f[0], lru_vecs_f[2], lru_vecs_f[1], lru_vecs_f[3]])
    alog_v = jnp.pad(dn_a_log.reshape(1, nd), ((0, 0), (0, LANES - nd)))
    dtb_v = jnp.pad(dn_dt_bias.reshape(1, nd), ((0, 0), (0, LANES - nd)))

    h0 = jnp.concatenate([ctx, x], axis=1)
    u1 = _norm_mod_fwd(h0, g_pre_mix, mods, 0, 1, nct, tm, "norm_mod1")
    u1f = u1.reshape(R, D)
    p_qkv = _mm(u1f, w_qkv, "nt", "mm_qkv").reshape(B, T, 3 * HD)
    p_z = _mm(u1f, w_z, "nt", "mm_z", out_dtype=BF16).reshape(B, T, HD)
    p_ab = _mm(u1f, w_ab, "nt", "mm_ab")
    p_xl = _mm(u1f, w_xl, "nt", "mm_xl").reshape(B, T, W)
    p_yl = _mm(u1f, w_yl, "nt", "mm_yl", out_dtype=BF16).reshape(B, T, W)
    p_mg = _mm(u1f, w_mg, "nt", "mm_mg", out_dtype=BF16).reshape(B, T, 2 * D)

    conv_offs = (-2, -1, 0, 1)
    tcc = _tile(HD, 1024)
    gb = _ab_act(p_ab, alog_v, dtb_v, nd, tm)
    gb3 = gb.reshape(B, T, LANES)
    cv, qkvn = _dwconv(p_qkv, 0, 3 * HD, dn_conv_f, None, conv_offs, nct, tm, HD, "dn_conv", qkv_heads=(H, hd))
    o_d0, o_d1, s_in0, s_in1, tm_d0, tm_d1 = _delta_fwd(qkvn, gb3, H, hd, ncc)
    o2 = (o_d0, o_d1)
    y_dn = _dn_out(o2, p_z, 0, dn_onorm, H, hd, nct, tm)

    tcw = _tile(W, 1024)
    xc = _dwconv(p_xl, 0, W, lru_conv_f, lru_conv_b, conv_offs, nct, tm, tcw, "lru_conv")
    tmg = max(t_ for t_ in range(SUBLANES, min(T, 768) + 1, SUBLANES) if T % t_ == 0)
    a2, inp2 = _lru_gates(xc, wbd, bias4, lru_lam_f, tmg, bw)
    hd0 = _lru_scan(a2, inp2, 0, nct, tm, tcw)
    hd1 = _lru_scan(a2, inp2, 1, nct, tm, tcw)
    y_lru = _lru_out(hd0, hd1, p_yl, 0, nct, tm, tcw)

    def gathered(started, after, name):
        xs_, lands_ = _split_wait(started, "gather", after, name)
        return [lax.dynamic_update_slice(land, x_[None], (me,) + (0,) * x_.ndim) for land, x_ in zip(lands_, xs_)]

    w_bdn_g, w_blru_g, w_out_g = gathered(ag_mix, y_lru, "ag_mix_wait")
    w_bdn_f, w_blru_f, w_out_f = w_bdn_g.reshape(HD, D), w_blru_g.reshape(W, D), w_out_g.reshape(D, D)
    bd = _mm(y_dn.reshape(RL, HD), w_bdn_f, "nn", "mm_bdn", out_dtype=BF16).reshape(B, N, D)
    bl = _mm(y_lru.reshape(RL, W), w_blru_f, "nn", "mm_blru", out_dtype=BF16).reshape(B, N, D)
    mixin = _merge(bd, bl, p_mg, b_merge, nct, tm)
    mix = _mm(mixin.reshape(RL, D), w_out_f, "nn", "mm_out").reshape(B, N, D)
    h1 = _resid_norm_fwd(x, mix, g_post_mix, mods, 2, tm, "resid_norm1")
    u2 = _norm_mod_fwd(h1, g_pre_ffn, mods, 3, 4, 0, tm, "norm_mod2")
    w_up_g, w_down_g, ffn_dw_g = gathered(ag_ffn, u2, "ag_ffn_wait")
    w_down_f = w_down_g.reshape(Cf, D)
    ffn_dw_f = _from_col_shards(ffn_dw_g)
    half = N_DEV // 2
    Fg = _mm(u2.reshape(RL, D), w_up_g[:half], "nn", "mm_up_gate").reshape(B, N, Cf)
    Fv = _mm(u2.reshape(RL, D), w_up_g[half:], "nn", "mm_up_val", out_dtype=BF16).reshape(B, N, Cf)
    tcf = LANES
    f, f_gel, f_dgel = _ffn_act(Fg, Fv, ffn_dw_f, ffn_dw_b, tcf)
    dproj = _mm(f.reshape(RL, Cf), w_down_f, "nn", "mm_down").reshape(B, N, D)

    dd, dh2, acc_f = _final_fwd_bwd(h1, dproj, g_post_ffn, mods, 5, loss_target, tm)
    loss = lax.psum((0.5 / D) * jnp.sum(acc_f[:, 2, :]), MESH_AXES)
    ddf = dd.reshape(RL, D)
    df = _mm(ddf, w_down_f, "nt", "mm_down_dx").reshape(B, N, Cf)
    gw_down = _mm(f.reshape(RL, Cf), ddf, "tn", "mm_down_dw", out_dtype=BF16)
    dF, acc_ffn = _ffn_act_bwd(Fg, Fv, ffn_dw_f, f_gel, f_dgel, df, tcf)
    dFf = dF.reshape(RL, 2 * Cf)
    du2 = _mm(dFf, w_up_g, "nt", "mm_up_dx").reshape(B, N, D)
    gw_up = _mm(u2.reshape(RL, D), dFf, "tn", "mm_up_dw", out_dtype=BF16, out_shards=N_DEV)
    jm = 2 * lax.axis_index("x") + lax.axis_index("y")

    def pair_sums(started, after, tag):
        parts8, sib4 = _split_wait(started, "pair", after, f"rs_pair_{tag}_wait")
        return [_pair_sum(p8, s4, f"rs_pair_sum_{tag}{i}") for i, (p8, s4) in enumerate(zip(parts8, sib4))]

    def own_slab(land, x4):
        idx = (jm,) + (0,) * (x4.ndim - 1)
        return lax.dynamic_update_slice(land, lax.dynamic_slice(x4, idx, (1,) + x4.shape[1:]), idx)

    p_early = _split_start([gw_up, gw_down.reshape(N_DEV, Cf // N_DEV, D)], "pair", "rs_pair_ffn_start")
    mods_b = mods + p_early[4][0, 0]
    dh1, acc2x, _ = _norm_mod_bwd(h1, du2, g_pre_ffn, mods_b, 4, 0, tm, dh2, "norm_mod2_bwd")
    dmix, acc_r1 = _resid_norm_bwd(mix, dh1, g_post_mix, mods, 2, tm, "resid_norm1_bwd")
    q_early = _split_start(pair_sums(p_early, dmix, "ffn"), "quad", "rs_quad_ffn_start")
    dmixf = dmix.reshape(RL, D)
    dmixin = _mm(dmixf, w_out_f, "nt", "mm_out_dx").reshape(B, N, D)
    gw_out = _mm(mixin.reshape(RL, D), dmixf, "tn", "mm_out_dw", out_dtype=BF16)
    dbd, dbl, dmg, acc_mg = _merge_bwd(dmixin, bd, bl, p_mg, b_merge + q_early[4][0:1, 0:1], nct, tm)
    dy_dn = _mm(dbd.reshape(RL, D), w_bdn_f, "nt", "mm_bdn_dx").reshape(B, N, HD)
    gw_bdn = _mm(y_dn.reshape(RL, HD), dbd.reshape(RL, D), "tn", "mm_bdn_dw", out_dtype=BF16)
    dy_lru = _mm(dbl.reshape(RL, D), w_blru_f, "nt", "mm_blru_dx").reshape(B, N, W)
    gw_blru = _mm(y_lru.reshape(RL, W), dbl.reshape(RL, D), "tn", "mm_blru_dw", out_dtype=BF16)

    dh, dyl = _lru_out_bwd(hd0, hd1, p_yl, 0, dy_lru, nct, tm, tcw)
    lam0, da0 = _lru_scan_bwd(a2, hd0, dh, 0, nct, tm, tcw)
    lam1, da1 = _lru_scan_bwd(a2, hd1, dh, 1, nct, tm, tcw)
    dxc, dwbd, acc_lru = _lru_gates_bwd(xc, wbd, bias4, lru_lam_f, (lam0, lam1), (da0, da1), tmg, bw)
    dxl = _dwconv(dxc, 0, W, lru_conv_f, None, tuple(-o for o in conv_offs), nct, tm, tcw, "lru_conv_dx", BF16)
    acc_lc = _dwconv_wgrad(dxc, p_xl, 0, W, conv_offs, nct, tm, tcw, "lru_conv_dw")

    do, dz, acc_on = _dn_out_bwd(o2, p_z, 0, dn_onorm, dy_dn, H, hd, nct, tm)
    dqkvn0, dqkvn1, dgb0, dgb1 = _delta_bwd(qkvn, gb3, (s_in0, s_in1), (tm_d0, tm_d1), do, H, hd, ncc)
    dcv, acc_dc = _dn_act_bwd(cv, (dqkvn0, dqkvn1), p_qkv, conv_offs, H, hd, nct, tm)
    dqkv = _dwconv(dcv, 0, 3 * HD, dn_conv_f, None, tuple(-o for o in conv_offs), nct, tm, tcc, "dn_conv_dx", BF16)
    dp_ab, acc_ab = _ab_act_bwd(p_ab, gb, (dgb0.reshape(R, LANES), dgb1.reshape(R, LANES)), alog_v, dtb_v, nd, tm)

    groups = [(dqkv.reshape(R, 3 * HD), w_qkv, "qkv"), (dz.reshape(R, HD), w_z, "z"), (dp_ab, w_ab, "ab"),
              (dxl.reshape(R, W), w_xl, "xl"), (dyl.reshape(R, W), w_yl, "yl"), (dmg.reshape(R, 2 * D), w_mg, "mg")]
    du1 = _mm_sum([(dg_, wg_) for dg_, wg_, _ in groups], "mm_in_dx")
    gw_groups = [_mm(dg_, u1f, "tn", "mm_in_dw_" + nm, out_dtype=BF16) for dg_, _, nm in groups]
    gw_groups[2] = gw_groups[2][:2 * nd]
    gw_in = jnp.concatenate(gw_groups, axis=0).reshape(N_DEV, csh, D)
    grad_x, acc1x, acc1c = _norm_mod_bwd(h0, du1.reshape(B, T, D), g_pre_mix, mods, 1, nct, tm, dh1, "norm_mod1_bwd")

    g_lru_conv = jnp.sum(acc_lc[:, :4], 0)
    g_dn_conv = jnp.sum(acc_dc[:, :4], 0)
    g_ffn_dw = jnp.sum(acc_ffn[:, :9], 0).reshape(3, 3, Cf)
    sm_names = ["dn_conv", "lru_conv", "lru_b_rg", "lru_b_ig", "lru_lambda", "ffn_dw"]
    sm_parts = [_col_shards(g_dn_conv), _col_shards(g_lru_conv),
                _col_shards(jnp.stack([acc_lru[0], acc_lru[2]])), _col_shards(jnp.stack([acc_lru[1], acc_lru[3]])),
                _col_shards(acc_lru[4:6]), _col_shards(g_ffn_dw)]
    late8 = [gw_in, gw_bdn.reshape(N_DEV, HD // N_DEV, D), gw_blru.reshape(N_DEV, W // N_DEV, D),
             gw_out.reshape(N_DEV, D // N_DEV, D), _pack(sm_parts, lead=(N_DEV,))]
    p_late = _split_start(late8, "pair", "rs_pair_mix_start")

    zeros_d = jnp.zeros((B, D), F32) + p_late[4][0:1, 0:1]
    dm_rows = jnp.stack([acc1x[:, 0], acc1x[:, 1], acc_r1[:, 0], acc2x[:, 0], acc2x[:, 1], acc_f[:, 0]], axis=1)
    dm_ctx = jnp.stack([jnp.sum(acc1c[:, 0], 0), jnp.sum(acc1c[:, 1], 0)] + [zeros_d[0]] * 4, axis=0)[None]
    dm_pad = jnp.zeros((SUBLANES, 6 * D), F32).at[:B + 1].set(jnp.concatenate([dm_rows, dm_ctx], 0).reshape(B + 1, 6 * D))
    dm_g = _all_gather([dm_pad], "ag_dmods")[0]

    q_late = _split_start(pair_sums(p_late, dm_g, "mix"), "quad", "rs_quad_mix_start")
    ffn_x, ffn_land = _split_wait(q_early, "quad", q_late[4], "rs_quad_ffn_wait")
    results = {}
    for n, x4, land in zip(["w_up", "w_down"], ffn_x, ffn_land):
        results[n] = list(_adamw(params[n], mom1[n], mom2[n], own_slab(land, x4), "adamw_" + n))

    g_mine = lax.dynamic_slice(dm_g, (0, 0, me * sw), (N_DEV, SUBLANES, sw)).reshape(N_DEV * SUBLANES, sw)
    gw_ada, dcc = _ada_bwd(ccp, w_ada[0], g_mine, c_ctx.reshape(1, D), B)

    per = bw // bdim

    def diag_blocks(dwf):
        g6 = dwf.reshape(W // bw, per, bdim, per, bdim)
        return jnp.einsum("tpiqj,pq->tpij", g6, jnp.eye(per, dtype=F32)).reshape(nblk, bdim, bdim)

    g_rep = dict(
        c_ctx=dcc[0],
        g_pre_mix=jnp.sum(acc1x[:, 2] + acc1c[:, 2], 0)[None], g_post_mix=jnp.sum(acc_r1[:, 1], 0)[None],
        g_pre_ffn=jnp.sum(acc2x[:, 2], 0)[None], g_post_ffn=jnp.sum(acc_f[:, 1], 0)[None],
        b_merge=jnp.sum(acc_mg[:, 0], 0)[None],
        dn_a_log=acc_ab[0, :nd].reshape(1, 2, H), dn_dt_bias=acc_ab[1, :nd].reshape(1, 2, H),
        dn_onorm=jnp.sum(acc_on[:, 0], 0)[None],
        lru_conv_b=jnp.sum(acc_lc[:, 4], 0)[None],
        lru_w_rg=jnp.stack([diag_blocks(dwbd[0]), diag_blocks(dwbd[2])])[None],
        lru_w_ig=jnp.stack([diag_blocks(dwbd[1]), diag_blocks(dwbd[3])])[None],
        ffn_dw_b=jnp.sum(acc_ffn[:, 9], 0)[None])
    mat_names = ["lru_w_rg", "lru_w_ig"]
    vec_names = [n for n in g_rep if n not in mat_names]
    vec_parts, mat_parts = _all_gather([_pack([g_rep[n] for n in vec_names]),
                                        _pack([g_rep[n] for n in mat_names]).astype(BF16)], "ag_rep_grads")
    for grp, parts, nm in ((vec_names, vec_parts, "adamw_rep_vec"), (mat_names, mat_parts, "adamw_rep_mat")):
        out4 = _adamw(_pack([params[n] for n in grp]), _pack([mom1[n] for n in grp]),
                      _pack([mom2[n] for n in grp]), parts, nm)
        for k, buf in enumerate(out4):
            for n, arr in zip(grp, _unpack(buf, [params[n].shape for n in grp])):
                results.setdefault(n, [None] * 4)[k] = arr

    ba_out = _adamw(_pack([b_ada]), _pack([m_b_ada]), _pack([v_b_ada]),
                    _pack([dm_g.reshape(N_DEV * SUBLANES, 6 * D)], lead=(N_DEV * SUBLANES,)), "adamw_b_ada")
    results["b_ada"] = [_unpack(buf, [b_ada.shape])[0] for buf in ba_out]

    wa_out = _adamw(w_ada, m_w_ada, v_w_ada, gw_ada[None], "adamw_w_ada")
    results["w_ada"] = list(wa_out)

    mix_x, mix_land = _split_wait(q_late, "quad", wa_out[0], "rs_quad_mix_wait")
    recv4 = [own_slab(land, x4) for land, x4 in zip(mix_land, mix_x)]
    for n, g4 in zip(["w_in", "w_branch_dn", "w_branch_lru", "w_out"], recv4[:-1]):
        if n == "w_in":
            g4 = jnp.swapaxes(g4, 1, 2)
        results[n] = list(_adamw(params[n], mom1[n], mom2[n], g4, "adamw_" + n))
    sm_out = _adamw(_pack([params[n] for n in sm_names]), _pack([mom1[n] for n in sm_names]),
                    _pack([mom2[n] for n in sm_names]), recv4[-1], "adamw_small")
    for k, buf in enumerate(sm_out):
        for n, arr in zip(sm_names, _unpack(buf, [params[n].shape for n in sm_names])):
            results.setdefault(n, [None] * 4)[k] = arr

    outs = [loss, grad_x]
    for k in range(4):
        outs += [results[n][k] for n in names]
    return tuple(outs)
```

```python
import math

import jax
import jax.numpy as jnp
from jax import lax
from jax.experimental import pallas as pl
from jax.experimental.pallas import tpu as pltpu

F32 = jnp.float32
BF16 = jnp.bfloat16

EPS = 1e-6
GRID_W = 64
CHUNK = 64
LRU_C = 8.0
N_DEV = 8
ADAM_LR = 0.001
ADAM_B1 = 0.9
ADAM_B2 = 0.999
ADAM_EPS = 1e-08
ADAM_WD = 0.01
ADAM_STEP = 10

LANES = 128
SUBLANES = 8
VMEM_LIMIT = 48 * 1024 * 1024
MESH_AXES = ("x", "y", "c")
GELU_C = math.sqrt(2.0 / math.pi)


def _cparams(sem=None, vmem=None):
    kw = {}
    if sem is not None:
        kw["dimension_semantics"] = sem
    if vmem is not None:
        kw["vmem_limit_bytes"] = vmem
    return pltpu.CompilerParams(**kw)


def _tile(n, pref):
    if n <= pref:
        return n
    t = (pref // LANES) * LANES
    while n % t:
        t -= LANES
    return t


def _sigmoid(x):
    return 1.0 / (1.0 + jnp.exp(-x))


def _silu(x):
    return x * _sigmoid(x)


def _dsilu(x):
    s = _sigmoid(x)
    return s * (1.0 + x * (1.0 - s))


def _softplus(x):
    return jnp.maximum(x, 0.0) + jnp.log(1.0 + jnp.exp(-jnp.abs(x)))


def _gelu(x):
    return 0.5 * x * (1.0 + jnp.tanh(GELU_C * (x + 0.044715 * x * x * x)))


def _dgelu(x):
    t = jnp.tanh(GELU_C * (x + 0.044715 * x * x * x))
    return 0.5 * (1.0 + t) + 0.5 * x * (1.0 - t * t) * GELU_C * (1.0 + 3.0 * 0.044715 * x * x)


def _dot(a, b, dims="nn"):
    dn = {"nn": (((1,), (0,)), ((), ())),
          "nt": (((1,), (1,)), ((), ())),
          "tn": (((0,), (0,)), ((), ()))}[dims]
    return lax.dot_general(a.astype(BF16), b.astype(BF16), dn, preferred_element_type=F32)


def _split2(x):
    hi = x.astype(BF16)
    lo = (x - hi.astype(F32)).astype(BF16)
    return hi, lo


def _split3(x):
    h1 = x.astype(BF16)
    r1 = x - h1.astype(F32)
    h2 = r1.astype(BF16)
    h3 = (r1 - h2.astype(F32)).astype(BF16)
    return h1, h2, h3


def _dot_hi(a, b):
    ah, al = _split2(a)
    bh, bl = _split2(b)
    return _dot(ah, bh) + (_dot(ah, bl) + _dot(al, bh))


def _dot_mask(m, x, dims="nn"):
    h1, h2, h3 = _split3(x)
    if dims == "xtn":
        return _dot(h1, m, "tn") + (_dot(h2, m, "tn") + _dot(h3, m, "tn"))
    return _dot(m, h1, dims) + (_dot(m, h2, dims) + _dot(m, h3, dims))


def _my_index():
    return 4 * lax.axis_index("x") + 2 * lax.axis_index("y") + lax.axis_index("c")


def _hbm_call(body, xs, out_shapes, n_sems, name):
    any_spec = pl.BlockSpec(memory_space=pl.ANY)
    return pl.pallas_call(
        body, name=name, out_shape=tuple(out_shapes),
        in_specs=[any_spec] * len(xs), out_specs=tuple([any_spec] * len(out_shapes)),
        scratch_shapes=[pltpu.SemaphoreType.DMA((n_sems,)), pltpu.SemaphoreType.DMA((n_sems,)),
                        pltpu.SemaphoreType.DMA((len(xs),))],
    )(*xs)


def _all_gather(xs, name):
    n = len(xs)

    def body(*refs):
        x_refs, out_refs = refs[:n], refs[n:2 * n]
        send_sems, recv_sems, local_sems = refs[2 * n:]
        x_, y_, c_ = lax.axis_index("x"), lax.axis_index("y"), lax.axis_index("c")
        me, sibling = (x_, y_, c_), (x_, y_, 1 - c_)
        chips = [(1 - x_, y_), (x_, 1 - y_), (1 - x_, 1 - y_)]

        def slab(a, px, py, pc):
            return out_refs[a].at[4 * px + 2 * py + pc]

        def copy(a, k, block, to, src=None):
            return pltpu.make_async_remote_copy(
                src_ref=slab(a, *block) if src is None else src, dst_ref=slab(a, *block),
                send_sem=send_sems.at[7 * a + k], recv_sem=recv_sems.at[7 * a + k],
                device_id=to, device_id_type=pl.DeviceIdType.MESH)

        mine = [pltpu.make_async_copy(x_refs[a], slab(a, *me), local_sems.at[a]) for a in range(n)]
        for cp in mine:
            cp.start()
        sent = []
        for j, chip in enumerate(chips):
            sent += [copy(a, 1 + j, me, (*chip, c_), src=x_refs[a]) for a in range(n)]
        sent += [copy(a, 0, me, sibling, src=x_refs[a]) for a in range(n)]
        for cp in sent:
            cp.start()
        for j, chip in enumerate(chips):
            for a in range(n):
                copy(a, 1 + j, (*chip, c_), me).wait_recv()
                fwd = copy(a, 4 + j, (*chip, c_), sibling)
                fwd.start()
                sent.append(fwd)
        for a in range(n):
            copy(a, 0, sibling, me).wait_recv()
        for j, chip in enumerate(chips):
            for a in range(n):
                copy(a, 4 + j, (*chip, 1 - c_), me).wait_recv()
        for cp in sent:
            cp.wait_send()
        for cp in mine:
            cp.wait()

    outs = [jax.ShapeDtypeStruct((N_DEV,) + x.shape, x.dtype) for x in xs]
    return _hbm_call(body, xs, outs, 7 * n, name)


SPLIT_COPIES = {"gather": 7, "quad": 3, "pair": 4}


def _split_views(kind, x_ref, land_ref, k):
    x_, y_, c_ = lax.axis_index("x"), lax.axis_index("y"), lax.axis_index("c")
    if kind == "pair":
        return x_ref.at[2 * (k - 1) + (1 - c_)], land_ref.at[k - 1], land_ref.at[k - 1], (x_, y_, 1 - c_)
    if kind == "gather":
        px = 1 - x_ if (k >> 2) & 1 else x_
        py = 1 - y_ if (k >> 1) & 1 else y_
        pc = 1 - c_ if k & 1 else c_
        return x_ref, land_ref.at[4 * x_ + 2 * y_ + c_], land_ref.at[4 * px + 2 * py + pc], (px, py, pc)
    px = 1 - x_ if (k >> 1) & 1 else x_
    py = 1 - y_ if k & 1 else y_
    return x_ref.at[2 * px + py], land_ref.at[2 * x_ + y_], land_ref.at[2 * px + py], (px, py, c_)


def _split_start(xs, kind, name):
    n = len(xs)
    npeer = SPLIT_COPIES[kind]
    land_shape = {"gather": lambda x: (N_DEV,) + x.shape, "quad": lambda x: x.shape, "pair": lambda x: (4,) + x.shape[1:]}
    lands = [lax.empty(land_shape[kind](x), x.dtype) for x in xs]

    def body(*refs):
        x_refs, land_refs = refs[:n], refs[n:2 * n]
        send_sems, recv_sems, token = refs[2 * n], refs[2 * n + 1], refs[4 * n + 2]
        for k in range(1, npeer + 1):
            for a in range(n):
                src, dst, _, peer = _split_views(kind, x_refs[a], land_refs[a], k)
                pltpu.make_async_remote_copy(
                    src_ref=src, dst_ref=dst, send_sem=send_sems.at[npeer * a + k - 1],
                    recv_sem=recv_sems.at[npeer * a + k - 1],
                    device_id=peer, device_id_type=pl.DeviceIdType.MESH).start()
        token[...] = jnp.zeros_like(token)

    hbm = pl.BlockSpec(memory_space=pltpu.HBM)
    sem = pl.BlockSpec(memory_space=pltpu.SEMAPHORE)
    sems = pltpu.SemaphoreType.DMA((npeer * n,))
    out = pl.pallas_call(
        body, name=name,
        out_shape=(sems, sems, *[pltpu.HBM(a.shape, a.dtype) for a in list(xs) + lands],
                   jax.ShapeDtypeStruct((SUBLANES, LANES), F32)),
        in_specs=[hbm] * (2 * n),
        out_specs=(sem, sem, *[hbm] * (2 * n), pl.BlockSpec(memory_space=pltpu.VMEM)),
        input_output_aliases={i: 2 + i for i in range(2 * n)},
        compiler_params=pltpu.CompilerParams(has_side_effects=pltpu.SideEffectType.DATAFLOW_SIDE_EFFECTING),
    )(*[pltpu.with_memory_space_constraint(a, pltpu.HBM) for a in list(xs) + lands])
    return out[0], out[1], list(out[2:2 + n]), list(out[2 + n:2 + 2 * n]), out[-1]


def _split_wait(started, kind, after, name):
    send_sems_, recv_sems_, x_thru, land_thru, _ = started
    n = len(x_thru)
    npeer = SPLIT_COPIES[kind]

    def body(*refs):
        x_refs, land_refs = refs[:n], refs[n:2 * n]
        send_sems, recv_sems = refs[2 * n], refs[2 * n + 1]
        for k in range(1, npeer + 1):
            for a in range(n):
                src, _, landed, peer = _split_views(kind, x_refs[a], land_refs[a], k)
                cp = pltpu.make_async_remote_copy(
                    src_ref=src, dst_ref=landed, send_sem=send_sems.at[npeer * a + k - 1],
                    recv_sem=recv_sems.at[npeer * a + k - 1],
                    device_id=peer, device_id_type=pl.DeviceIdType.MESH)
                cp.wait_send()
                cp.wait_recv()

    hbm = pl.BlockSpec(memory_space=pltpu.HBM)
    sem = pl.BlockSpec(memory_space=pltpu.SEMAPHORE)
    out = pl.pallas_call(
        body, name=name,
        out_shape=tuple(pltpu.HBM(a.shape, a.dtype) for a in x_thru + land_thru),
        in_specs=[hbm] * (2 * n) + [sem, sem, pl.BlockSpec(memory_space=pl.ANY)],
        out_specs=tuple([hbm] * (2 * n)),
        input_output_aliases={i: i for i in range(2 * n)},
        compiler_params=pltpu.CompilerParams(has_side_effects=pltpu.SideEffectType.DATAFLOW_SIDE_EFFECTING),
    )(*x_thru, *land_thru, send_sems_, recv_sems_, after)
    return list(out[:n]), list(out[n:])


def _pair_sum(x8, r4, name):
    _, r, c = x8.shape
    tr = _row_tile(r, c * 12)

    def body(core_ref, x_ref, r_ref, o_ref):
        o_ref[...] = (x_ref[...].astype(F32) + r_ref[...].astype(F32)).astype(o_ref.dtype)

    core = lax.axis_index("c").astype(jnp.int32).reshape(1)
    return pl.pallas_call(
        body, name=name,
        grid_spec=pltpu.PrefetchScalarGridSpec(
            num_scalar_prefetch=1, grid=(4, r // tr),
            in_specs=[pl.BlockSpec((None, tr, c), lambda j, i, core_ref: (2 * j + core_ref[0], i, 0)),
                      pl.BlockSpec((None, tr, c), lambda j, i, core_ref: (j, i, 0))],
            out_specs=pl.BlockSpec((None, tr, c), lambda j, i, core_ref: (j, i, 0))),
        out_shape=jax.ShapeDtypeStruct((4, r, c), x8.dtype),
        compiler_params=_cparams(("parallel", "parallel"), VMEM_LIMIT))(core, x8, r4)


def _row_tile(r, bytes_per_row, budget=10 * 1024 * 1024):
    best = None
    for t in range(16, r + 1, 16):
        if r % t == 0 and t * bytes_per_row <= budget:
            best = t
    return best if best is not None else r


def _mm(a, b, dims, name, out_dtype=F32, out_shards=1, tm=1024, tn=1024, tk=1024):
    S = b.shape[0] if b.ndim == 3 else 1
    bshape = (b.shape[1], S * b.shape[2]) if b.ndim == 3 else b.shape
    if dims == "nn":
        (M, K), (K2, N) = a.shape, bshape
    elif dims == "nt":
        (M, K), (N, K2) = a.shape, bshape
    else:
        (K, M), (K2, N) = a.shape, bshape
    assert K == K2, (a.shape, b.shape, dims)
    cs = bshape[1] // S
    tm, tk = _tile(M, tm), _tile(K, min(tk, cs) if (S > 1 and dims == "nt") else tk)
    tn = _tile(N, min(tn, cs) if (S > 1 and dims != "nt") else min(tn, N // out_shards))
    assert cs % (tk if dims == "nt" else tn) == 0 and (N // out_shards) % tn == 0
    nk = K // tk

    def body(a_ref, b_ref, o_ref, acc):
        k = pl.program_id(2)

        @pl.when(k == 0)
        def _():
            acc[...] = jnp.zeros_like(acc)

        acc[...] += _dot(a_ref[...], b_ref[...], dims)

        @pl.when(k == nk - 1)
        def _():
            o_ref[...] = acc[...].astype(out_dtype)

    a_spec =(pl.BlockSpec((tk, tm), lambda i, j, k: (k, i)) if dims == "tn"
              else pl.BlockSpec((tm, tk), lambda i, j, k: (i, k)))
    if S == 1:
        b_spec = (pl.BlockSpec((tn, tk), lambda i, j, k: (j, k)) if dims == "nt"
                  else pl.BlockSpec((tk, tn), lambda i, j, k: (k, j)))
    elif dims == "nt":
        per = cs // tk
        b_spec = pl.BlockSpec((None, tn, tk), lambda i, j, k: (k // per, j, k % per))
    else:
        per = cs // tn
        b_spec = pl.BlockSpec((None, tk, tn), lambda i, j, k: (j // per, k, j % per))
    if out_shards == 1:
        out_spec, out_shape = pl.BlockSpec((tm, tn), lambda i, j, k: (i, j)), (M, N)
    else:
        oper = N // out_shards // tn
        out_spec = pl.BlockSpec((None, tm, tn), lambda i, j, k: (j // oper, i, j % oper))
        out_shape = (out_shards, M, N // out_shards)
    return pl.pallas_call(
        body, name=name, grid=(M // tm, N // tn, nk),
        in_specs=[a_spec, b_spec], out_specs=out_spec,
        out_shape=jax.ShapeDtypeStruct(out_shape, out_dtype),
        scratch_shapes=[pltpu.VMEM((tm, tn), F32)],
        compiler_params=_cparams(("parallel", "parallel", "arbitrary"), VMEM_LIMIT),
    )(a, b)


VMEM_LIMIT_SUM = 56 * 1024 * 1024


def _mm_sum(pairs, name, out_dtype=F32, tm=768, tn=1024, tk=1024):
    M, N = pairs[0][0].shape[0], pairs[0][1].shape[1]
    tm, tn = _tile(M, tm), _tile(N, tn)
    tks = [_tile(a.shape[1], tk) for a, _ in pairs]
    nks = [a.shape[1] // t for (a, _), t in zip(pairs, tks)]
    starts = [sum(nks[:g]) for g in range(len(pairs))]
    nk = sum(nks)
    G = len(pairs)

    def body(*refs):
        o_ref, acc = refs[2 * G], refs[2 * G + 1]
        k = pl.program_id(2)

        @pl.when(k == 0)
        def _():
            acc[...] = jnp.zeros_like(acc)

        for g in range(G):
            @pl.when(jnp.logical_and(k >= starts[g], k < starts[g] + nks[g]))
            def _(g=g):
                acc[...] += _dot(refs[2 * g][...], refs[2 * g + 1][...])

        @pl.when(k == nk - 1)
        def _():
            o_ref[...] = acc[...].astype(out_dtype)

    in_specs, args = [], []
    for g, (a, b) in enumerate(pairs):
        kk = lambda k, g=g: jnp.clip(k - starts[g], 0, nks[g] - 1)
        in_specs += [pl.BlockSpec((tm, tks[g]), lambda i, j, k, kk=kk: (i, kk(k))),
                     pl.BlockSpec((tks[g], tn), lambda i, j, k, kk=kk: (kk(k), j))]
        args += [a, b]
    return pl.pallas_call(
        body, name=name, grid=(M // tm, N // tn, nk),
        in_specs=in_specs, out_specs=pl.BlockSpec((tm, tn), lambda i, j, k: (i, j)),
        out_shape=jax.ShapeDtypeStruct((M, N), out_dtype),
        scratch_shapes=[pltpu.VMEM((tm, tn), F32)],
        compiler_params=_cparams(("parallel", "parallel", "arbitrary"), VMEM_LIMIT_SUM),
    )(*args)


def _ada_fwd(ccp, w, b):
    def body(c_ref, w_ref, b_ref, o_ref):
        o_ref[...] = _dot(_silu(c_ref[...]), w_ref[...]) + b_ref[...]

    return pl.pallas_call(
        body, name="ada_fwd", out_shape=jax.ShapeDtypeStruct((ccp.shape[0], w.shape[1]), F32),
        compiler_params=_cparams(None, VMEM_LIMIT))(ccp, w, b)


def _ada_bwd(ccp, w, g, cctx, n_loc):
    def body(c_ref, w_ref, g_ref, cc_ref, gw_ref, dc_ref):
        gw_ref[...] = _dot(_silu(c_ref[...]), g_ref[...], "tn")
        dcc = _dot(g_ref[...], w_ref[...], "nt")
        row = lax.broadcasted_iota(jnp.int32, dcc.shape, 0)
        part = jnp.sum(jnp.where(row % SUBLANES == n_loc, dcc, 0.0), axis=0, keepdims=True)
        dc_ref[...] = jnp.broadcast_to(part * _dsilu(cc_ref[...]), dc_ref.shape)

    D = ccp.shape[1]
    return pl.pallas_call(
        body, name="ada_bwd",
        out_shape=(jax.ShapeDtypeStruct(w.shape, F32), jax.ShapeDtypeStruct((SUBLANES, D), F32)),
        compiler_params=_cparams(None, VMEM_LIMIT))(ccp, w, g, cctx)


def _norm_mod_fwd(h, gain, mods, i_shift, i_scale, nct, tm, name):
    B, T, D = h.shape

    def body(h_ref, g_ref, m_ref, u_ref):
        x = h_ref[0]
        r = lax.rsqrt(jnp.mean(x * x, axis=-1, keepdims=True) + EPS)
        n = x * r * g_ref[...]
        u_ref[0] = (n * (1.0 + m_ref[0, i_scale:i_scale + 1, :]) + m_ref[0, i_shift:i_shift + 1, :]).astype(BF16)

    return pl.pallas_call(
        body, name=name, grid=(B, T // tm),
        in_specs=[pl.BlockSpec((1, tm, D), lambda b, t: (b, t, 0)),
                  pl.BlockSpec((1, D), lambda b, t: (0, 0)),
                  pl.BlockSpec((1, SUBLANES, D), lambda b, t: (jnp.where(t < nct, B, b), 0, 0))],
        out_specs=pl.BlockSpec((1, tm, D), lambda b, t: (b, t, 0)),
        out_shape=jax.ShapeDtypeStruct((B, T, D), BF16),
        compiler_params=_cparams(("parallel", "parallel"), VMEM_LIMIT),
    )(h, gain, mods)


def _norm_mod_bwd(h, du, gain, mods, i_scale, nct, tm, res, name):
    B, T, D = h.shape
    nt = T // tm

    def body(h_ref, du_ref, g_ref, m_ref, res_ref, dx_ref, ax_ref, ac_ref):
        t = pl.program_id(1)
        x = h_ref[0]
        r = lax.rsqrt(jnp.mean(x * x, axis=-1, keepdims=True) + EPS)
        nh = x * r
        g = g_ref[...]
        du_ = du_ref[0]
        dn = du_ * (1.0 + m_ref[0, i_scale:i_scale + 1, :])
        dnh = dn * g
        dx = r * (dnh - nh * jnp.mean(dnh * nh, axis=-1, keepdims=True))
        sums = (jnp.sum(du_, axis=0, keepdims=True), jnp.sum(du_ * nh * g, axis=0, keepdims=True),
                jnp.sum(dn * nh, axis=0, keepdims=True))

        @pl.when(t == 0)
        def _():
            ax_ref[...] = jnp.zeros_like(ax_ref)
            ac_ref[...] = jnp.zeros_like(ac_ref)

        def accumulate(ref):
            for i, s in enumerate(sums):
                ref[0, i:i + 1, :] += s

        @pl.when(t < nct)
        def _():
            accumulate(ac_ref)

        @pl.when(t >= nct)
        def _():
            accumulate(ax_ref)
            dx_ref[0] = dx + res_ref[0]

    lat = lambda b, t: (b, jnp.maximum(t - nct, 0), 0)
    acc = pl.BlockSpec((1, SUBLANES, D), lambda b, t: (b, 0, 0))
    return pl.pallas_call(
        body, name=name, grid=(B, nt),
        in_specs=[pl.BlockSpec((1, tm, D), lambda b, t: (b, t, 0)),
                  pl.BlockSpec((1, tm, D), lambda b, t: (b, t, 0)),
                  pl.BlockSpec((1, D), lambda b, t: (0, 0)),
                  pl.BlockSpec((1, SUBLANES, D), lambda b, t: (jnp.where(t < nct, B, b), 0, 0)),
                  pl.BlockSpec((1, tm, D), lat)],
        out_specs=(pl.BlockSpec((1, tm, D), lat), acc, acc),
        out_shape=(jax.ShapeDtypeStruct(res.shape, F32),
                   jax.ShapeDtypeStruct((B, SUBLANES, D), F32), jax.ShapeDtypeStruct((B, SUBLANES, D), F32)),
        compiler_params=_cparams(("parallel", "arbitrary"), VMEM_LIMIT),
    )(h, du, gain, mods, res)


def _resid_norm_fwd(x, y, gain, mods, i_gate, tm, name):
    B, N, D = x.shape

    def body(x_ref, y_ref, g_ref, m_ref, o_ref):
        y_ = y_ref[0]
        r = lax.rsqrt(jnp.mean(y_ * y_, axis=-1, keepdims=True) + EPS)
        o_ref[0] = x_ref[0] + y_ * r * g_ref[...] * m_ref[0, i_gate:i_gate + 1, :]

    row = pl.BlockSpec((1, tm, D), lambda b, t: (b, t, 0))
    return pl.pallas_call(
        body, name=name, grid=(B, N // tm),
        in_specs=[pl.BlockSpec((1, tm, D), lambda b, t: (b, t, 0)), pl.BlockSpec((1, tm, D), lambda b, t: (b, t, 0)),
                  pl.BlockSpec((1, D), lambda b, t: (0, 0)),
                  pl.BlockSpec((1, SUBLANES, D), lambda b, t: (b, 0, 0))],
        out_specs=row, out_shape=jax.ShapeDtypeStruct((B, N, D), F32),
        compiler_params=_cparams(("parallel", "parallel"), VMEM_LIMIT),
    )(x, y, gain, mods)


def _resid_norm_bwd_math(y_, dh, g, gate):
    r = lax.rsqrt(jnp.mean(y_ * y_, axis=-1, keepdims=True) + EPS)
    nh = y_ * r
    dgate = jnp.sum(dh * nh * g, axis=0, keepdims=True)
    dn = dh * gate
    dgain = jnp.sum(dn * nh, axis=0, keepdims=True)
    dnh = dn * g
    dy = r * (dnh - nh * jnp.mean(dnh * nh, axis=-1, keepdims=True))
    return dy, dgate, dgain


def _resid_norm_bwd(y, dh, gain, mods, i_gate, tm, name):
    B, N, D = y.shape

    def body(y_ref, dh_ref, g_ref, m_ref, dy_ref, acc_ref):
        t = pl.program_id(1)
        dy, dgate, dgain = _resid_norm_bwd_math(y_ref[0], dh_ref[0], g_ref[...], m_ref[0, i_gate:i_gate + 1, :])
        dy_ref[0] = dy.astype(BF16)

        @pl.when(t == 0)
        def _():
            acc_ref[...] = jnp.zeros_like(acc_ref)

        acc_ref[0, 0:1, :] += dgate
        acc_ref[0, 1:2, :] += dgain

    row = pl.BlockSpec((1, tm, D), lambda b, t: (b, t, 0))
    return pl.pallas_call(
        body, name=name, grid=(B, N // tm),
        in_specs=[pl.BlockSpec((1, tm, D), lambda b, t: (b, t, 0)), pl.BlockSpec((1, tm, D), lambda b, t: (b, t, 0)),
                  pl.BlockSpec((1, D), lambda b, t: (0, 0)),
                  pl.BlockSpec((1, SUBLANES, D), lambda b, t: (b, 0, 0))],
        out_specs=(row, pl.BlockSpec((1, SUBLANES, D), lambda b, t: (b, 0, 0))),
        out_shape=(jax.ShapeDtypeStruct((B, N, D), BF16), jax.ShapeDtypeStruct((B, SUBLANES, D), F32)),
        compiler_params=_cparams(("parallel", "arbitrary"), VMEM_LIMIT),
    )(y, dh, gain, mods)


def _final_fwd_bwd(h1, d, gain, mods, i_gate, tgt, tm):
    B, N, D = h1.shape

    def body(h_ref, d_ref, g_ref, m_ref, t_ref, dd_ref, dh_ref, acc_ref):
        t = pl.program_id(1)
        d_ = d_ref[0]
        g = g_ref[...]
        gate = m_ref[0, i_gate:i_gate + 1, :]
        r = lax.rsqrt(jnp.mean(d_ * d_, axis=-1, keepdims=True) + EPS)
        e = h_ref[0] + d_ * r * g * gate - t_ref[0]
        dh = e * (1.0 / D)
        dh_ref[0] = dh
        dy, dgate, dgain = _resid_norm_bwd_math(d_, dh, g, gate)
        dd_ref[0] = dy.astype(BF16)

        @pl.when(t == 0)
        def _():
            acc_ref[...] = jnp.zeros_like(acc_ref)

        acc_ref[0, 0:1, :] += dgate
        acc_ref[0, 1:2, :] += dgain
        acc_ref[0, 2:3, :] += jnp.sum(e * e, axis=0, keepdims=True)

    row = pl.BlockSpec((1, tm, D), lambda b, t: (b, t, 0))
    return pl.pallas_call(
        body, name="final_fwd_bwd", grid=(B, N // tm),
        in_specs=[pl.BlockSpec((1, tm, D), lambda b, t: (b, t, 0)), pl.BlockSpec((1, tm, D), lambda b, t: (b, t, 0)),
                  pl.BlockSpec((1, D), lambda b, t: (0, 0)),
                  pl.BlockSpec((1, SUBLANES, D), lambda b, t: (b, 0, 0)), pl.BlockSpec((1, tm, D), lambda b, t: (b, t, 0))],
        out_specs=(row, row, pl.BlockSpec((1, SUBLANES, D), lambda b, t: (b, 0, 0))),
        out_shape=(jax.ShapeDtypeStruct((B, N, D), BF16), jax.ShapeDtypeStruct((B, N, D), F32),
                   jax.ShapeDtypeStruct((B, SUBLANES, D), F32)),
        compiler_params=_cparams(("parallel", "arbitrary"), VMEM_LIMIT),
    )(h1, d, gain, mods, tgt)


def _shifted(x, prev, nxt, off, row, tm):
    if off == 0:
        return x
    if off < 0:
        y = pltpu.roll(x, -off, 0)
        for r in range(-off):
            y = jnp.where(row == r, prev[SUBLANES + r + off:SUBLANES + r + off + 1, :], y)
        return y
    y = pltpu.roll(x, tm - off, 0)
    for r in range(off):
        y = jnp.where(row == tm - off + r, nxt[r:r + 1, :], y)
    return y


def _halo_specs(T, tm, tc, cb0, order):
    r8, n8 = tm // SUBLANES, T // SUBLANES
    if order == "btj":
        prev = lambda b, t, j: (b, jnp.maximum(t * r8 - 1, 0), cb0 + j)
        nxt = lambda b, t, j: (b, jnp.minimum((t + 1) * r8, n8 - 1), cb0 + j)
    else:
        prev = lambda b, j, t: (b, jnp.maximum(t * r8 - 1, 0), cb0 + j)
        nxt = lambda b, j, t: (b, jnp.minimum((t + 1) * r8, n8 - 1), cb0 + j)
    return pl.BlockSpec((1, SUBLANES, tc), prev), pl.BlockSpec((1, SUBLANES, tc), nxt)


def _seg_halos(p_ref, n_ref, t, nct, nt):
    seg_start = jnp.logical_or(t == 0, t == nct)
    seg_end = jnp.logical_or(t == nct - 1, t == nt - 1)
    prev = jnp.where(seg_start, 0.0, p_ref[0])
    nxt = jnp.where(seg_end, 0.0, n_ref[0])
    return prev, nxt


def _dwconv(x, col0, C, w, bias, offs, nct, tm, tc, name, out_dtype=F32, qkv_heads=None):
    B, T, _ = x.shape
    nt = T // tm
    cb0 = col0 // tc
    if qkv_heads is not None:
        assert tc == qkv_heads[0] * qkv_heads[1] and C == 3 * tc

    ncb = C // tc
    steps = B * nt * ncb

    def body(*refs):
        refs = list(refs)
        sems, buf = refs.pop(), refs.pop()
        a_ref = refs.pop() if qkv_heads is not None else None
        if bias is None:
            x_any, p_ref, n_ref, w_ref, o_ref = refs
        else:
            x_any, p_ref, n_ref, w_ref, b_ref, o_ref = refs
        t = pl.program_id(1)
        i = (pl.program_id(0) * nt + t) * ncb + pl.program_id(2)

        def fetch(k, slot):
            bk, tk, jk = k // (nt * ncb), (k // ncb) % nt, k % ncb
            src = x_any.at[bk, pl.ds(pl.multiple_of(tk * tm, tm), tm), pl.ds(pl.multiple_of((cb0 + jk) * tc, tc), tc)]
            return pltpu.make_async_copy(src, buf.at[slot], sems.at[slot])

        @pl.when(i == 0)
        def _():
            for k in range(min(3, steps)):
                fetch(k, k).start()

        slot = i % 3
        fetch(i, slot).wait()
        prev, nxt = _seg_halos(p_ref, n_ref, t, nct, nt)
        x_ = buf[slot]
        row = lax.broadcasted_iota(jnp.int32, x_.shape, 0)
        acc = None
        for j, off in enumerate(offs):
            term = _shifted(x_, prev, nxt, off, row, tm) * w_ref[j:j + 1, :]
            acc = term if acc is None else acc + term
        if bias is not None:
            acc = acc + b_ref[...]
        o_ref[0] = acc.astype(out_dtype)
        if qkv_heads is not None:
            H, hd = qkv_heads
            part = pl.program_id(2)
            for h in range(H):
                sl = slice(h * hd, (h + 1) * hd)
                s = _silu(acc[:, sl])
                rs = lax.rsqrt(jnp.sum(s * s, axis=-1, keepdims=True) + EPS)
                scale = jnp.where(part == 0, rs * (float(hd) ** -0.5), jnp.where(part == 1, rs, 1.0))
                a_ref[0, :, sl] = (s * scale).astype(a_ref.dtype)

        @pl.when(i + 3 < steps)
        def _():
            fetch(i + 3, slot).start()

    pspec, nspec = _halo_specs(T, tm, tc, cb0, "btj")
    in_specs = [pl.BlockSpec(memory_space=pl.ANY), pspec, nspec,
                pl.BlockSpec((w.shape[0], tc), lambda b, t, j: (0, j))]
    args = [x, x, x, w]
    if bias is not None:
        in_specs.append(pl.BlockSpec((1, tc), lambda b, t, j: (0, j)))
        args.append(bias)
    tile = pl.BlockSpec((1, tm, tc), lambda b, t, j: (b, t, j))
    out_specs, out_shape = tile, jax.ShapeDtypeStruct((B, T, C), out_dtype)
    if qkv_heads is not None:
        out_specs, out_shape = (tile, tile), (out_shape, jax.ShapeDtypeStruct((B, T, C), BF16))
    return pl.pallas_call(
        body, name=name, grid=(B, nt, ncb),
        in_specs=in_specs, out_specs=out_specs, out_shape=out_shape,
        scratch_shapes=[pltpu.VMEM((3, tm, tc), x.dtype), pltpu.SemaphoreType.DMA((3,))],
        compiler_params=_cparams(("arbitrary", "arbitrary", "arbitrary"), VMEM_LIMIT),
    )(*args)


def _dwconv_wgrad(dy, x, col0, C, offs, nct, tm, tc, name):
    B, T, _ = x.shape
    nt = T // tm
    cb0 = col0 // tc
    K = len(offs)

    def body(dy_ref, x_ref, p_ref, n_ref, acc_ref):
        t = pl.program_id(2)
        prev, nxt = _seg_halos(p_ref, n_ref, t, nct, nt)
        x_ = x_ref[0]
        dy_ = dy_ref[0]
        row = lax.broadcasted_iota(jnp.int32, x_.shape, 0)

        @pl.when(t == 0)
        def _():
            acc_ref[...] = jnp.zeros_like(acc_ref)

        for j, off in enumerate(offs):
            acc_ref[0, j:j + 1, :] += jnp.sum(dy_ * _shifted(x_, prev, nxt, off, row, tm), axis=0, keepdims=True)
        acc_ref[0, K:K + 1, :] += jnp.sum(dy_, axis=0, keepdims=True)

    pspec, nspec = _halo_specs(T, tm, tc, cb0, "bjt")
    return pl.pallas_call(
        body, name=name, grid=(B, C // tc, nt),
        in_specs=[pl.BlockSpec((1, tm, tc), lambda b, j, t: (b, t, j)),
                  pl.BlockSpec((1, tm, tc), lambda b, j, t: (b, t, cb0 + j)), pspec, nspec],
        out_specs=pl.BlockSpec((1, SUBLANES, tc), lambda b, j, t: (b, 0, j)),
        out_shape=jax.ShapeDtypeStruct((B, SUBLANES, C), F32),
        compiler_params=_cparams(("parallel", "parallel", "arbitrary"), VMEM_LIMIT),
    )(dy, x, x, x)


def _ab_act(pab, alog, dtb, nd, tm):
    R = pab.shape[0]

    def body(p_ref, al_ref, dt_ref, o_ref):
        p = p_ref[...]
        lane = lax.broadcasted_iota(jnp.int32, p.shape, 1)
        g = -jnp.exp(al_ref[...]) * _softplus(p + dt_ref[...])
        o_ref[...] = jnp.where(lane < nd, g, jnp.where(lane < 2 * nd, _sigmoid(p), 0.0))

    row = pl.BlockSpec((tm, LANES), lambda i: (i, 0))
    vec = pl.BlockSpec((1, LANES), lambda i: (0, 0))
    return pl.pallas_call(
        body, name="ab_act", grid=(R // tm,), in_specs=[row, vec, vec], out_specs=row,
        out_shape=jax.ShapeDtypeStruct((R, LANES), F32),
        compiler_params=_cparams(("parallel",), VMEM_LIMIT))(pab, alog, dtb)


def _ab_act_bwd(pab, gb, dgb, alog, dtb, nd, tm):
    R = pab.shape[0]

    def body(p_ref, gb_ref, d0_ref, d1_ref, al_ref, dt_ref, o_ref, acc_ref):
        i = pl.program_id(0)
        p = p_ref[...]
        gb_ = gb_ref[...]
        d_ = d0_ref[...] + d1_ref[...]
        lane = lax.broadcasted_iota(jnp.int32, p.shape, 1)
        is_g = lane < nd
        is_b = jnp.logical_and(lane >= nd, lane < 2 * nd)
        da = jnp.where(is_g, d_ * (-jnp.exp(al_ref[...])) * _sigmoid(p + dt_ref[...]), 0.0)
        db = jnp.where(is_b, d_ * gb_ * (1.0 - gb_), 0.0)
        o_ref[...] = (da + db).astype(BF16)

        @pl.when(i == 0)
        def _():
            acc_ref[...] = jnp.zeros_like(acc_ref)

        acc_ref[0:1, :] += jnp.sum(jnp.where(is_g, d_ * gb_, 0.0), axis=0, keepdims=True)
        acc_ref[1:2, :] += jnp.sum(da, axis=0, keepdims=True)

    row = pl.BlockSpec((tm, LANES), lambda i: (i, 0))
    vec = pl.BlockSpec((1, LANES), lambda i: (0, 0))
    return pl.pallas_call(
        body, name="ab_act_bwd", grid=(R // tm,),
        in_specs=[row, row, row, row, vec, vec],
        out_specs=(row, pl.BlockSpec((SUBLANES, LANES), lambda i: (0, 0))),
        out_shape=(jax.ShapeDtypeStruct((R, LANES), BF16), jax.ShapeDtypeStruct((SUBLANES, LANES), F32)),
        compiler_params=_cparams(("arbitrary",), VMEM_LIMIT))(pab, gb, dgb[0], dgb[1], alog, dtb)


def _dn_act_bwd(cv, dqkv, x, offs, H, hd, nct, tm):
    B, T, C3 = cv.shape
    HD = H * hd
    nt = T // tm
    r8, n8 = tm // SUBLANES, T // SUBLANES
    qscale = float(hd) ** -0.5

    def body(c_ref, d0_ref, d1_ref, x_ref, p_ref, n_ref, o_ref, acc_ref):
        part = pl.program_id(0)
        t = pl.program_id(2)
        for h in range(H):
            sl = slice(h * hd, (h + 1) * hd)
            c = c_ref[0, :, sl]
            s = _silu(c)
            dout = d0_ref[0, :, sl] + d1_ref[0, :, sl]
            rs = lax.rsqrt(jnp.sum(s * s, axis=-1, keepdims=True) + EPS)
            sh = s * rs
            dnorm = rs * (dout - sh * jnp.sum(dout * sh, axis=-1, keepdims=True))
            ds = jnp.where(part == 0, dnorm * qscale, jnp.where(part == 1, dnorm, dout))
            o_ref[0, :, sl] = ds * _dsilu(c)

        @pl.when(t == 0)
        def _():
            acc_ref[...] = jnp.zeros_like(acc_ref)

        prev, nxt = _seg_halos(p_ref, n_ref, t, nct, nt)
        x_ = x_ref[0]
        dcv_ = o_ref[0]
        row = lax.broadcasted_iota(jnp.int32, x_.shape, 0)
        for j, off in enumerate(offs):
            acc_ref[0, j:j + 1, :] += jnp.sum(dcv_ * _shifted(x_, prev, nxt, off, row, tm), axis=0, keepdims=True)

    spec = pl.BlockSpec((1, tm, HD), lambda p, b, t: (b, t, p))
    halo_p = pl.BlockSpec((1, SUBLANES, HD), lambda p, b, t: (b, jnp.maximum(t * r8 - 1, 0), p))
    halo_n = pl.BlockSpec((1, SUBLANES, HD), lambda p, b, t: (b, jnp.minimum((t + 1) * r8, n8 - 1), p))
    return pl.pallas_call(
        body, name="dn_act_bwd", grid=(3, B, nt),
        in_specs=[spec, spec, spec, spec, halo_p, halo_n],
        out_specs=(spec, pl.BlockSpec((1, SUBLANES, HD), lambda p, b, t: (b, 0, p))),
        out_shape=(jax.ShapeDtypeStruct((B, T, C3), F32), jax.ShapeDtypeStruct((B, SUBLANES, C3), F32)),
        compiler_params=_cparams(("parallel", "parallel", "arbitrary"), VMEM_LIMIT),
    )(cv, dqkv[0], dqkv[1], x, x, x)


def _chunk_of(d, s, ncc, nch):
    if d == 0:
        return s
    return jnp.where(s < ncc, ncc - 1 - s, nch - 1 - (s - ncc))


def _delta_masks(d):
    row = lax.broadcasted_iota(jnp.int32, (CHUNK, CHUNK), 0)
    col = lax.broadcasted_iota(jnp.int32, (CHUNK, CHUNK), 1)
    sign = 1 - 2 * d
    diff = (row - col) * sign
    return diff >= 0, diff > 0, diff <= 0


def _each(f, *lists):
    return [f(*a) for a in zip(*lists)]


def _unit_tri_inverse(As):
    C = As[0].shape[0]
    row = lax.broadcasted_iota(jnp.int32, (C, C), 0)
    col = lax.broadcasted_iota(jnp.int32, (C, C), 1)
    eye = jnp.where(row == col, 1.0, 0.0).astype(F32)
    same = lambda shift: jnp.right_shift(row, shift) == jnp.right_shift(col, shift)
    in8 = same(3)
    xs = [-jnp.where(in8, a, 0.0) for a in As]
    ts = [eye + x for x in xs]
    ps = _each(lambda x: _dot(x, x), xs)
    tp = _each(lambda t, p: _dot(_rows(t, p), p), ts, ps)
    ts = _each(lambda t, m: t + m[:C], ts, tp)
    ts = _each(lambda t, m: t + _dot(t, m[C:]), ts, tp)
    shift = 3
    while (1 << shift) < C:
        off = jnp.logical_and(same(shift + 1), jnp.right_shift(row, shift) != jnp.right_shift(col, shift))
        los = [jnp.where(off, a, 0.0) for a in As]
        ts = _each(lambda t, lo: t - _dot(t, _dot(lo, t)), ts, los)
        shift += 1
    def residual(a, t):
        (ah, al), (th, tl) = _split2(eye + a), _split2(t)
        m = _dot(_rows(ah, al), th)
        return eye - (m[:C] + (m[C:] + _dot(ah, tl)))

    rs = _each(residual, As, ts)
    return _each(lambda t, r: t + _dot(t, r), ts, rs)


def _rows(*xs):
    return jnp.concatenate(xs, axis=0)


def _cols(*xs):
    return jnp.concatenate(xs, axis=1)


def _delta_chunk_fwd(q, k, v, g_i, g_j, beta_i, gl, S, incl, strict, Tm=None):
    D_ = _each(lambda gi, gj, m: jnp.where(m, jnp.exp(jnp.where(m, gi - gj, 0.0)), 0.0), g_i, g_j, incl)
    C, dk = k[0].shape
    kb = _each(lambda k_, b: k_ * b, k, beta_i)
    kq = _each(lambda kb_, q_, k_: _dot(_rows(kb_, q_), k_, "nt"), kb, q, k)
    A = _each(lambda m_, d, m: jnp.where(m, m_[:C] * d, 0.0), kq, D_, strict)
    P = _each(lambda m_, d, m: jnp.where(m, m_[C:] * d, 0.0), kq, D_, incl)
    if Tm is None:
        Tm = _unit_tri_inverse(A)
    gam = _each(jnp.exp, g_i)
    wu = _each(lambda t, kb_, g, v_, b: _dot(t, _cols(kb_ * g, v_ * b)), Tm, kb, gam, v, beta_i)
    w = [m_[:, :dk] for m_ in wu]
    u = [m_[:, dk:] for m_ in wu]
    qg = _each(lambda q_, g: q_ * g, q, gam)
    ws = _each(lambda w_, qg_, s: _dot(_rows(w_, qg_), s), w, qg, S)
    vn = _each(lambda u_, m_: u_ - m_[:C], u, ws)
    o = _each(lambda m_, p, vn_: m_[C:] + _dot(p, vn_), ws, P, vn)
    e_i = _each(lambda gl_, gi: jnp.exp(gl_ - gi), gl, g_i)
    kd = _each(lambda k_, e: k_ * e, k, e_i)
    Gam = _each(jnp.exp, gl)
    S_new = _each(lambda s, g, kd_, vn_: s * g + _dot(kd_, vn_, "tn"), S, Gam, kd, vn)
    return dict(D=D_, kb=kb, A=A, Tm=Tm, gam=gam, w=w, u=u, vn=vn, P=P, qg=qg, o=o, e=e_i, kd=kd, Gam=Gam, S_new=S_new)


def _delta_gb(gb_ref, d, incl, incl_t, H):
    gb = gb_ref[0]
    g = gb[:, d * H:(d + 1) * H]
    beta = gb[:, (2 + d) * H:(3 + d) * H]
    gc_col = _dot_mask(incl.astype(BF16), g)
    gc_row = _dot_mask(incl_t.astype(BF16), g, "xtn")
    gl = jnp.sum(g, axis=0, keepdims=True)
    return beta, gc_col, gc_row, gl


def _delta_fwd(qkvn, gb, H, hd, ncc):
    B, T, _ = qkvn.shape
    nch = T // CHUNK
    HD = H * hd
    pairs = [(d, h) for d in range(2) for h in range(H)]

    def body(q0, k0, v0, gb0, q1, k1, v1, gb1, o0, o1, sin0, sin1, tm0, tm1, S_ref):
        s = pl.program_id(1)
        q_refs, k_refs, v_refs, gb_refs = (q0, q1), (k0, k1), (v0, v1), (gb0, gb1)
        o_refs, sin_refs, tm_refs = (o0, o1), (sin0, sin1), (tm0, tm1)

        @pl.when(s == 0)
        def _():
            S_ref[...] = jnp.zeros_like(S_ref)

        masks = [_delta_masks(d) for d in range(2)]
        gbs = [_delta_gb(gb_refs[d], d, masks[d][0], masks[d][2], H) for d in range(2)]
        head = lambda ref, h: ref[0, :, h * hd:(h + 1) * hd]
        S = [S_ref[d, h] for d, h in pairs]
        r = _delta_chunk_fwd([head(q_refs[d], h) for d, h in pairs], [head(k_refs[d], h) for d, h in pairs],
                             [head(v_refs[d], h) for d, h in pairs],
                             [gbs[d][1][:, h:h + 1] for d, h in pairs], [gbs[d][2][h:h + 1, :] for d, h in pairs],
                             [gbs[d][0][:, h:h + 1] for d, h in pairs], [gbs[d][3][:, h:h + 1] for d, h in pairs],
                             S, [masks[d][0] for d, h in pairs], [masks[d][1] for d, h in pairs])
        for i, (d, h) in enumerate(pairs):
            sin_refs[d][0, 0, h] = S[i]
            tm_refs[d][0, 0, h] = r["Tm"][i]
            S_ref[d, h] = r["S_new"][i]
            o_refs[d][0, :, h * hd:(h + 1) * hd] = r["o"][i]

    def dir_specs(d):
        cidx = lambda b, s: _chunk_of(d, s, ncc, nch)
        ins = [pl.BlockSpec((1, CHUNK, HD), lambda b, s, p=p: (b, cidx(b, s), p)) for p in range(3)]
        ins.append(pl.BlockSpec((1, CHUNK, LANES), lambda b, s: (b, cidx(b, s), 0)))
        return (ins, pl.BlockSpec((1, CHUNK, HD), lambda b, s: (b, cidx(b, s), 0)),
                pl.BlockSpec((1, 1, H, hd, hd), lambda b, s: (b, cidx(b, s), 0, 0, 0)),
                pl.BlockSpec((1, 1, H, CHUNK, CHUNK), lambda b, s: (b, cidx(b, s), 0, 0, 0)))

    (in0, o_s0, sin_s0, tm_s0), (in1, o_s1, sin_s1, tm_s1) = dir_specs(0), dir_specs(1)
    o_shape = jax.ShapeDtypeStruct((B, T, HD), F32)
    sin_shape = jax.ShapeDtypeStruct((B, nch, H, hd, hd), F32)
    tm_shape = jax.ShapeDtypeStruct((B, nch, H, CHUNK, CHUNK), F32)
    return pl.pallas_call(
        body, name="delta_fwd", grid=(B, nch),
        in_specs=in0 + in1, out_specs=(o_s0, o_s1, sin_s0, sin_s1, tm_s0, tm_s1),
        out_shape=(o_shape, o_shape, sin_shape, sin_shape, tm_shape, tm_shape),
        scratch_shapes=[pltpu.VMEM((2, H, hd, hd), F32)],
        compiler_params=_cparams(("parallel", "arbitrary"), VMEM_LIMIT),
    )(qkvn, qkvn, qkvn, gb, qkvn, qkvn, qkvn, gb)


def _delta_bwd(qkvn, gb, sin, tms, do, H, hd, ncc):
    B, T, _ = qkvn.shape
    nch = T // CHUNK
    HD = H * hd
    pairs = [(d, h) for d in range(2) for h in range(H)]

    def body(q0, k0, v0, gb0, sin0, tm0, do0, q1, k1, v1, gb1, sin1, tm1, do1, dqkv0, dqkv1, dgb0, dgb1, dS_ref):
        s = pl.program_id(1)
        tm_refs = (tm0, tm1)
        q_refs, k_refs, v_refs, gb_refs = (q0, q1), (k0, k1), (v0, v1), (gb0, gb1)
        sin_refs, do_refs, dqkv_refs, dgb_refs = (sin0, sin1), (do0, do1), (dqkv0, dqkv1), (dgb0, dgb1)

        @pl.when(s == 0)
        def _():
            dS_ref[...] = jnp.zeros_like(dS_ref)

        masks = [_delta_masks(d) for d in range(2)]
        gbs = [_delta_gb(gb_refs[d], d, masks[d][0], masks[d][2], H) for d in range(2)]
        incl = [masks[d][0] for d, h in pairs]
        strict = [masks[d][1] for d, h in pairs]
        lane = lax.broadcasted_iota(jnp.int32, (1, LANES), 1)
        ones = jnp.ones((3 * CHUNK, LANES), BF16)
        head = lambda ref, h: ref[0, :, h * hd:(h + 1) * hd]
        q = [head(q_refs[d], h) for d, h in pairs]
        k = [head(k_refs[d], h) for d, h in pairs]
        v = [head(v_refs[d], h) for d, h in pairs]
        do_ = [head(do_refs[d], h) for d, h in pairs]
        beta_i = [gbs[d][0][:, h:h + 1] for d, h in pairs]
        S = [sin_refs[d][0, 0, h] for d, h in pairs]
        dSn = [dS_ref[d, h] for d, h in pairs]
        f = _delta_chunk_fwd(q, k, v, [gbs[d][1][:, h:h + 1] for d, h in pairs], [gbs[d][2][h:h + 1, :] for d, h in pairs],
                             beta_i, [gbs[d][3][:, h:h + 1] for d, h in pairs], S, incl, strict,
                             Tm=[tm_refs[d][0, 0, h] for d, h in pairs])
        rsum = lambda x: jnp.sum(x, axis=1, keepdims=True)
        dvn = _each(lambda p, d_, kd, ds: _dot(p, d_, "tn") + _dot(kd, ds), f["P"], do_, f["kd"], dSn)
        dP = _each(lambda d_, vn, m: jnp.where(m, _dot(d_, vn, "nt"), 0.0), do_, f["vn"], incl)
        dqg = _each(lambda d_, s: _dot(d_, s, "nt"), do_, S)
        dS_in = _each(lambda qg, d_, g, ds, w, dv_: _dot(qg, d_, "tn") + g * ds - _dot(w, dv_, "tn"),
                      f["qg"], do_, f["Gam"], dSn, f["w"], dvn)
        dGam = _each(lambda s, ds: jnp.sum(rsum(s * ds), axis=0, keepdims=True), S, dSn)
        dkd = _each(lambda vn, ds: _dot(vn, ds, "nt"), f["vn"], dSn)
        de = _each(lambda dkd_, k_, e: rsum(dkd_ * k_) * e, dkd, k, f["e"])
        dw = _each(lambda dv_, s: -_dot(dv_, s, "nt"), dvn, S)
        dXw = _each(lambda t, dw_: _dot(t, dw_, "tn"), f["Tm"], dw)
        dXu = _each(lambda t, dv_: _dot(t, dv_, "tn"), f["Tm"], dvn)
        dA = _each(lambda xw, w, xu, u, m: -jnp.where(m, _dot(xw, w, "nt") + _dot(xu, u, "nt"), 0.0),
                   dXw, f["w"], dXu, f["u"], strict)
        dM = _each(lambda a, d_: a * d_, dA, f["D"])
        dN = _each(lambda p, d_: p * d_, dP, f["D"])
        dkb = _each(lambda m, k_, xw, g: _dot(m, k_) + xw * g, dM, k, dXw, f["gam"])
        dq = _each(lambda dqg_, g, n, k_: dqg_ * g + _dot(n, k_), dqg, f["gam"], dN, k)
        dk = _each(lambda dkd_, e, m, kb, n, q_, dkb_, b: dkd_ * e + _dot(m, kb, "tn") + _dot(n, q_, "tn") + dkb_ * b,
                   dkd, f["e"], dM, f["kb"], dN, q, dkb, beta_i)
        dv = _each(lambda xu, b: xu * b, dXu, beta_i)
        dbeta = _each(lambda dkb_, k_, xu, v_: rsum(dkb_ * k_) + rsum(xu * v_), dkb, k, dXu, v)
        E = _each(lambda a, A_, p, P_: a * A_ + p * P_, dA, f["A"], dP, f["P"])

        def colsum(e):
            return _dot(_rows(*_split3(e)), ones, "tn")[:, 0:1]

        dg = _each(lambda e, dqg_, qg, xw, kb, g, de_: rsum(e) - colsum(e) + rsum(dqg_ * qg) + rsum(xw * kb) * g - de_,
                   E, dqg, f["qg"], dXw, f["kb"], f["gam"], de)
        dgl_h = _each(lambda de_, dg_, g: jnp.sum(de_, axis=0, keepdims=True) + dg_ * g, de, dGam, f["Gam"])
        for d in range(2):
            dgc = jnp.zeros((CHUNK, LANES), F32)
            dgl = jnp.zeros((1, LANES), F32)
            for h in range(H):
                i = d * H + h
                g_lane, b_lane = d * H + h, (2 + d) * H + h
                dgc = dgc + dg[i] * (lane == g_lane).astype(F32) + dbeta[i] * (lane == b_lane).astype(F32)
                dgl = dgl + dgl_h[i] * (lane == g_lane).astype(F32)
                dS_ref[d, h] = dS_in[i]
                dqkv_refs[d][0, :, h * hd:(h + 1) * hd] = dq[i]
                dqkv_refs[d][0, :, HD + h * hd:HD + (h + 1) * hd] = dk[i]
                dqkv_refs[d][0, :, 2 * HD + h * hd:2 * HD + (h + 1) * hd] = dv[i]
            draw = _dot_mask(masks[d][0].astype(BF16), dgc, "tn") + dgl
            dgb_refs[d][0] = jnp.where(lane < 2 * H, draw, dgc)

    def dir_specs(d):
        cidx = lambda b, s: _chunk_of(d, nch - 1 - s, ncc, nch)
        ins = [pl.BlockSpec((1, CHUNK, HD), lambda b, s, p=p: (b, cidx(b, s), p)) for p in range(3)]
        ins += [pl.BlockSpec((1, CHUNK, LANES), lambda b, s: (b, cidx(b, s), 0)),
                pl.BlockSpec((1, 1, H, hd, hd), lambda b, s: (b, cidx(b, s), 0, 0, 0)),
                pl.BlockSpec((1, 1, H, CHUNK, CHUNK), lambda b, s: (b, cidx(b, s), 0, 0, 0)),
                pl.BlockSpec((1, CHUNK, HD), lambda b, s: (b, cidx(b, s), 0))]
        return (ins, pl.BlockSpec((1, CHUNK, 3 * HD), lambda b, s: (b, cidx(b, s), 0)),
                pl.BlockSpec((1, CHUNK, LANES), lambda b, s: (b, cidx(b, s), 0)))

    (in0, dq_s0, dg_s0), (in1, dq_s1, dg_s1) = dir_specs(0), dir_specs(1)
    dq_shape = jax.ShapeDtypeStruct((B, T, 3 * HD), F32)
    dg_shape = jax.ShapeDtypeStruct((B, T, LANES), F32)
    return pl.pallas_call(
        body, name="delta_bwd", grid=(B, nch),
        in_specs=in0 + in1, out_specs=(dq_s0, dq_s1, dg_s0, dg_s1),
        out_shape=(dq_shape, dq_shape, dg_shape, dg_shape),
        scratch_shapes=[pltpu.VMEM((2, H, hd, hd), F32)],
        compiler_params=_cparams(("parallel", "arbitrary"), VMEM_LIMIT),
    )(qkvn, qkvn, qkvn, gb, sin[0], tms[0], do, qkvn, qkvn, qkvn, gb, sin[1], tms[1], do)


def _dn_out(o2, pz, zcb0, onorm, H, hd, nct, tm):
    B, T, HD = o2[0].shape
    N = T - nct * tm

    def body(o0_ref, o1_ref, z_ref, g_ref, y_ref):
        for h in range(H):
            sl = slice(h * hd, (h + 1) * hd)
            o = o0_ref[0, :, sl] + o1_ref[0, :, sl]
            r = lax.rsqrt(jnp.mean(o * o, axis=-1, keepdims=True) + EPS)
            y_ref[0, :, sl] = (o * r * g_ref[...] * _silu(z_ref[0, :, sl].astype(F32))).astype(BF16)

    return pl.pallas_call(
        body, name="dn_out", grid=(B, N // tm),
        in_specs=[pl.BlockSpec((1, tm, HD), lambda b, t: (b, t + nct, 0)),
                  pl.BlockSpec((1, tm, HD), lambda b, t: (b, t + nct, 0)),
                  pl.BlockSpec((1, tm, HD), lambda b, t: (b, t + nct, zcb0)),
                  pl.BlockSpec((1, hd), lambda b, t: (0, 0))],
        out_specs=pl.BlockSpec((1, tm, HD), lambda b, t: (b, t, 0)),
        out_shape=jax.ShapeDtypeStruct((B, N, HD), BF16),
        compiler_params=_cparams(("parallel",) * 2, VMEM_LIMIT))(o2[0], o2[1], pz, onorm)


def _dn_out_bwd(o2, pz, zcb0, onorm, dy, H, hd, nct, tm):
    B, T, HD = o2[0].shape
    nt = T // tm

    def body(o0_ref, o1_ref, z_ref, g_ref, dy_ref, do_ref, dz_ref, acc_ref):
        t = pl.program_id(1)

        @pl.when(t == 0)
        def _():
            acc_ref[...] = jnp.zeros_like(acc_ref)

        @pl.when(t < nct)
        def _():
            do_ref[0] = jnp.zeros_like(do_ref[0])
            dz_ref[0] = jnp.zeros_like(dz_ref[0])

        @pl.when(t >= nct)
        def _():
            g = g_ref[...]
            for h in range(H):
                sl = slice(h * hd, (h + 1) * hd)
                o = o0_ref[0, :, sl] + o1_ref[0, :, sl]
                z = z_ref[0, :, sl].astype(F32)
                dy_ = dy_ref[0, :, sl]
                r = lax.rsqrt(jnp.mean(o * o, axis=-1, keepdims=True) + EPS)
                oh = o * r
                dz_ref[0, :, sl] = (dy_ * oh * g * _dsilu(z)).astype(BF16)
                dn = dy_ * _silu(z)
                acc_ref[0, 0:1, :] += jnp.sum(dn * oh, axis=0, keepdims=True)
                doh = dn * g
                do_ref[0, :, sl] = r * (doh - oh * jnp.mean(doh * oh, axis=-1, keepdims=True))

    lat = lambda b, t: (b, jnp.maximum(t - nct, 0), 0)
    row = pl.BlockSpec((1, tm, HD), lambda b, t: (b, t, 0))
    return pl.pallas_call(
        body, name="dn_out_bwd", grid=(B, nt),
        in_specs=[row, row,
                  pl.BlockSpec((1, tm, HD), lambda b, t: (b, t, zcb0)),
                  pl.BlockSpec((1, hd), lambda b, t: (0, 0)),
                  pl.BlockSpec((1, tm, HD), lat)],
        out_specs=(row, row, pl.BlockSpec((1, SUBLANES, hd), lambda b, t: (b, 0, 0))),
        out_shape=(jax.ShapeDtypeStruct((B, T, HD), F32), jax.ShapeDtypeStruct((B, T, HD), BF16),
                   jax.ShapeDtypeStruct((B, SUBLANES, hd), F32)),
        compiler_params=_cparams(("parallel", "arbitrary"), VMEM_LIMIT),
    )(o2[0], o2[1], pz, onorm, dy)


def _lru_gate_math(x, wr, wi, br, bi, lam):
    r = _sigmoid(_dot(x, wr) + br)
    i = _sigmoid(_dot(x, wi) + bi)
    sp = _softplus(-lam)
    la = -LRU_C * r * sp
    a = jnp.exp(la)
    s = jnp.sqrt(-jnp.tanh(la) * (a * a + 1.0))
    return r, i, sp, a, s


def _lru_gates(xc, wbd, bias4, lam, tm, bw):
    B, T, W = xc.shape

    def body(x_ref, w_ref, b_ref, l_ref, a_ref, i_ref):
        x = x_ref[0]
        for d in range(2):
            _, i, _, a, s = _lru_gate_math(x, w_ref[2 * d, 0], w_ref[2 * d + 1, 0],
                                           b_ref[2 * d:2 * d + 1, :], b_ref[2 * d + 1:2 * d + 2, :], l_ref[d:d + 1, :])
            a_ref[d, 0] = a
            i_ref[d, 0] = s * (i * x)

    out = pl.BlockSpec((2, 1, tm, bw), lambda b, t, j: (0, b, t, j))
    return pl.pallas_call(
        body, name="lru_gates", grid=(B, T // tm, W // bw),
        in_specs=[pl.BlockSpec((1, tm, bw), lambda b, t, j: (b, t, j)),
                  pl.BlockSpec((4, 1, bw, bw), lambda b, t, j: (0, j, 0, 0)),
                  pl.BlockSpec((4, bw), lambda b, t, j: (0, j)),
                  pl.BlockSpec((2, bw), lambda b, t, j: (0, j))],
        out_specs=(out, out),
        out_shape=(jax.ShapeDtypeStruct((2, B, T, W), F32), jax.ShapeDtypeStruct((2, B, T, W), F32)),
        compiler_params=_cparams(("parallel",) * 3, VMEM_LIMIT))(xc, wbd, bias4, lam)


def _lru_gates_bwd(xc, wbd, bias4, lam, dinp, da, tm, bw):
    B, T, W = xc.shape
    nt = T // tm

    def body(x_ref, w_ref, b_ref, l_ref, di0_ref, di1_ref, da0_ref, da1_ref, dx_ref, dw_ref, acc_ref):
        di_refs, da_refs = (di0_ref, di1_ref), (da0_ref, da1_ref)
        b_ = pl.program_id(1)
        t = pl.program_id(2)

        @pl.when(jnp.logical_and(b_ == 0, t == 0))
        def _():
            dw_ref[...] = jnp.zeros_like(dw_ref)
            acc_ref[...] = jnp.zeros_like(acc_ref)

        x = x_ref[0]
        dx = jnp.zeros_like(x)
        for d in range(2):
            wr, wi = w_ref[2 * d, 0], w_ref[2 * d + 1, 0]
            lam_ = l_ref[d:d + 1, :]
            r, i, sp, a, s = _lru_gate_math(x, wr, wi, b_ref[2 * d:2 * d + 1, :], b_ref[2 * d + 1:2 * d + 2, :], lam_)
            dinp_ = di_refs[d][0]
            dix = dinp_ * s
            ds = dinp_ * i * x
            dla = da_refs[d][0] * a - ds * (a * a) / s
            dpr = dla * (-LRU_C * sp) * r * (1.0 - r)
            dpi = dix * x * i * (1.0 - i)
            dx = dx + dix * i + _dot(dpr, wr, "nt") + _dot(dpi, wi, "nt")
            dw_ref[2 * d, 0] += _dot(x, dpr, "tn")
            dw_ref[2 * d + 1, 0] += _dot(x, dpi, "tn")
            acc_ref[2 * d:2 * d + 1, :] += jnp.sum(dpr, axis=0, keepdims=True)
            acc_ref[2 * d + 1:2 * d + 2, :] += jnp.sum(dpi, axis=0, keepdims=True)
            acc_ref[4 + d:5 + d, :] += jnp.sum(dla * (-LRU_C * r), axis=0, keepdims=True) * (-_sigmoid(-lam_))
        dx_ref[0] = dx

    tile = pl.BlockSpec((1, tm, bw), lambda j, b, t: (b, t, j))
    return pl.pallas_call(
        body, name="lru_gates_bwd", grid=(W // bw, B, nt),
        in_specs=[pl.BlockSpec((1, tm, bw), lambda j, b, t: (b, t, j)),
                  pl.BlockSpec((4, 1, bw, bw), lambda j, b, t: (0, j, 0, 0)),
                  pl.BlockSpec((4, bw), lambda j, b, t: (0, j)),
                  pl.BlockSpec((2, bw), lambda j, b, t: (0, j)), tile, tile, tile, tile],
        out_specs=(pl.BlockSpec((1, tm, bw), lambda j, b, t: (b, t, j)),
                   pl.BlockSpec((4, 1, bw, bw), lambda j, b, t: (0, j, 0, 0)),
                   pl.BlockSpec((SUBLANES, bw), lambda j, b, t: (0, j))),
        out_shape=(jax.ShapeDtypeStruct((B, T, W), F32), jax.ShapeDtypeStruct(wbd.shape, F32),
                   jax.ShapeDtypeStruct((SUBLANES, W), F32)),
        compiler_params=_cparams(("parallel", "arbitrary", "arbitrary"), VMEM_LIMIT),
    )(xc, wbd, bias4, lam, dinp[0], dinp[1], da[0], da[1])


def _tile_scan(a, x, rev, tm):
    row = lax.broadcasted_iota(jnp.int32, a.shape, 0)
    s = 1
    while s < tm:
        if rev:
            a_sh, x_sh, ok = pltpu.roll(a, tm - s, 0), pltpu.roll(x, tm - s, 0), row < tm - s
        else:
            a_sh, x_sh, ok = pltpu.roll(a, s, 0), pltpu.roll(x, s, 0), row >= s
        x = jnp.where(ok, x + a * x_sh, x)
        a = jnp.where(ok, a * a_sh, a)
        s *= 2
    return a, x


def _scan_tile_of(s, rev, nct, nt):
    if not rev:
        return s
    return jnp.where(s < nct, nct - 1 - s, nt - 1 - (s - nct))


def _lru_scan(a2, x2, d, nct, tm, tc):
    _, B, T, W = a2.shape
    nt = T // tm
    rev = d == 1
    last = 0 if rev else tm - 1

    def body(a_ref, x_ref, h_ref, carry):
        s = pl.program_id(2)

        @pl.when(s == 0)
        def _():
            carry[...] = jnp.zeros_like(carry)

        A, X = _tile_scan(a_ref[0, 0], x_ref[0, 0], rev, tm)
        h = X + A * carry[0:1, :]
        h_ref[0] = h
        carry[...] = jnp.broadcast_to(h[last:last + 1, :], carry.shape)

    tidx = lambda s: _scan_tile_of(s, rev, nct, nt)
    spec = pl.BlockSpec((1, 1, tm, tc), lambda b, j, s: (d, b, tidx(s), j))
    return pl.pallas_call(
        body, name=f"lru_scan{d}", grid=(B, W // tc, nt),
        in_specs=[spec, spec], out_specs=pl.BlockSpec((1, tm, tc), lambda b, j, s: (b, tidx(s), j)),
        out_shape=jax.ShapeDtypeStruct((B, T, W), F32),
        scratch_shapes=[pltpu.VMEM((SUBLANES, tc), F32)],
        compiler_params=_cparams(("parallel", "parallel", "arbitrary"), VMEM_LIMIT),
    )(a2, x2)


def _lru_scan_bwd(a2, h, dh, d, nct, tm, tc):
    _, B, T, W = a2.shape
    nt = T // tm
    rev = d == 1
    first = tm - 1 if rev else 0
    r8, n8 = tm // SUBLANES, T // SUBLANES

    def body(a_ref, h_ref, hp_ref, dh_ref, lam_ref, da_ref, ca, cl):
        s = pl.program_id(2)

        @pl.when(s == 0)
        def _():
            ca[...] = jnp.zeros_like(ca)
            cl[...] = jnp.zeros_like(cl)

        a = a_ref[0, 0]
        row = lax.broadcasted_iota(jnp.int32, a.shape, 0)
        if rev:
            c = jnp.where(row == 0, ca[0:1, :], pltpu.roll(a, 1, 0))
        else:
            c = jnp.where(row == tm - 1, ca[0:1, :], pltpu.roll(a, tm - 1, 0))
        C, L = _tile_scan(c, dh_ref[0], not rev, tm)
        lam = L + C * cl[0:1, :]
        lam_ref[0] = lam
        hp = jnp.where(s == nt - 1, 0.0, hp_ref[0])
        if rev:
            hprev = jnp.where(row == tm - 1, hp[0:1, :], pltpu.roll(h_ref[0], tm - 1, 0))
        else:
            hprev = jnp.where(row == 0, hp[SUBLANES - 1:SUBLANES, :], pltpu.roll(h_ref[0], 1, 0))
        da_ref[0] = lam * hprev
        ca[...] = jnp.broadcast_to(a[first:first + 1, :], ca.shape)
        cl[...] = jnp.broadcast_to(lam[first:first + 1, :], cl.shape)

    tidx = lambda s: _scan_tile_of(nt - 1 - s, rev, nct, nt)

    def hp_idx(b, j, s):
        tt = tidx(s)
        if rev:
            blk = jnp.where(tt == nt - 1, 0, jnp.minimum((tt + 1) * r8, n8 - 1))
        else:
            blk = jnp.maximum(tt * r8 - 1, 0)
        return (b, blk, j)

    tile = pl.BlockSpec((1, tm, tc), lambda b, j, s: (b, tidx(s), j))
    return pl.pallas_call(
        body, name=f"lru_scan_bwd{d}", grid=(B, W // tc, nt),
        in_specs=[pl.BlockSpec((1, 1, tm, tc), lambda b, j, s: (d, b, tidx(s), j)), tile,
                  pl.BlockSpec((1, SUBLANES, tc), hp_idx), tile],
        out_specs=(tile, tile),
        out_shape=(jax.ShapeDtypeStruct((B, T, W), F32), jax.ShapeDtypeStruct((B, T, W), F32)),
        scratch_shapes=[pltpu.VMEM((SUBLANES, tc), F32), pltpu.VMEM((SUBLANES, tc), F32)],
        compiler_params=_cparams(("parallel", "parallel", "arbitrary"), VMEM_LIMIT),
    )(a2, h, h, dh)


def _lru_out(h0, h1, pyl, ycb0, nct, tm, tc):
    B, T, W = h0.shape
    N = T - nct * tm

    def body(h0_ref, h1_ref, y_ref, o_ref):
        o_ref[0] = ((h0_ref[0] + h1_ref[0]) * _gelu(y_ref[0].astype(F32))).astype(BF16)

    hs = pl.BlockSpec((1, tm, tc), lambda b, t, j: (b, t + nct, j))
    return pl.pallas_call(
        body, name="lru_out", grid=(B, N // tm, W // tc),
        in_specs=[hs, hs, pl.BlockSpec((1, tm, tc), lambda b, t, j: (b, t + nct, ycb0 + j))],
        out_specs=pl.BlockSpec((1, tm, tc), lambda b, t, j: (b, t, j)),
        out_shape=jax.ShapeDtypeStruct((B, N, W), BF16),
        compiler_params=_cparams(("parallel",) * 3, VMEM_LIMIT))(h0, h1, pyl)


def _lru_out_bwd(h0, h1, pyl, ycb0, dy, nct, tm, tc):
    B, T, W = h0.shape

    def body(h0_ref, h1_ref, y_ref, dy_ref, dh_ref, dyl_ref):
        t = pl.program_id(1)

        @pl.when(t < nct)
        def _():
            dh_ref[0] = jnp.zeros_like(dh_ref[0])
            dyl_ref[0] = jnp.zeros_like(dyl_ref[0])

        @pl.when(t >= nct)
        def _():
            y = y_ref[0].astype(F32)
            dy_ = dy_ref[0]
            dh_ref[0] = dy_ * _gelu(y)
            dyl_ref[0] = (dy_ * (h0_ref[0] + h1_ref[0]) * _dgelu(y)).astype(BF16)

    hs = pl.BlockSpec((1, tm, tc), lambda b, t, j: (b, t, j))
    return pl.pallas_call(
        body, name="lru_out_bwd", grid=(B, T // tm, W // tc),
        in_specs=[hs, hs, pl.BlockSpec((1, tm, tc), lambda b, t, j: (b, t, ycb0 + j)),
                  pl.BlockSpec((1, tm, tc), lambda b, t, j: (b, jnp.maximum(t - nct, 0), j))],
        out_specs=(hs, hs),
        out_shape=(jax.ShapeDtypeStruct((B, T, W), F32), jax.ShapeDtypeStruct((B, T, W), BF16)),
        compiler_params=_cparams(("parallel",) * 3, VMEM_LIMIT))(h0, h1, pyl, dy)


def _merge(bd, bl, pmg, bm, nct, tm):
    B, N, D = bd.shape

    def body(bd_ref, bl_ref, m0_ref, m1_ref, b_ref, o_ref):
        g0 = _sigmoid(m0_ref[0].astype(F32) + b_ref[:, 0:D])
        g1 = _sigmoid(m1_ref[0].astype(F32) + b_ref[:, D:2 * D])
        o_ref[0] = (g0 * bd_ref[0].astype(F32) + g1 * bl_ref[0].astype(F32)).astype(BF16)

    row = pl.BlockSpec((1, tm, D), lambda b, t: (b, t, 0))
    return pl.pallas_call(
        body, name="merge", grid=(B, N // tm),
        in_specs=[row, row, pl.BlockSpec((1, tm, D), lambda b, t: (b, t + nct, 0)),
                  pl.BlockSpec((1, tm, D), lambda b, t: (b, t + nct, 1)),
                  pl.BlockSpec((1, 2 * D), lambda b, t: (0, 0))],
        out_specs=row, out_shape=jax.ShapeDtypeStruct((B, N, D), BF16),
        compiler_params=_cparams(("parallel", "parallel"), VMEM_LIMIT))(bd, bl, pmg, pmg, bm)


def _merge_bwd(dmi, bd, bl, pmg, bm, nct, tm):
    B, N, D = bd.shape
    T = pmg.shape[1]

    def body(dm_ref, bd_ref, bl_ref, m0_ref, m1_ref, b_ref, dbd_ref, dbl_ref, dmg_ref, acc_ref):
        t = pl.program_id(1)

        @pl.when(t == 0)
        def _():
            acc_ref[...] = jnp.zeros_like(acc_ref)

        @pl.when(t < nct)
        def _():
            dmg_ref[0] = jnp.zeros_like(dmg_ref[0])

        @pl.when(t >= nct)
        def _():
            dm = dm_ref[0]
            g0 = _sigmoid(m0_ref[0].astype(F32) + b_ref[:, 0:D])
            g1 = _sigmoid(m1_ref[0].astype(F32) + b_ref[:, D:2 * D])
            dbd_ref[0] = (dm * g0).astype(BF16)
            dbl_ref[0] = (dm * g1).astype(BF16)
            d0 = dm * bd_ref[0].astype(F32) * g0 * (1.0 - g0)
            d1 = dm * bl_ref[0].astype(F32) * g1 * (1.0 - g1)
            dmg_ref[0, :, 0:D] = d0.astype(BF16)
            dmg_ref[0, :, D:2 * D] = d1.astype(BF16)
            acc_ref[0, 0:1, 0:D] += jnp.sum(d0, axis=0, keepdims=True)
            acc_ref[0, 0:1, D:2 * D] += jnp.sum(d1, axis=0, keepdims=True)

    lat = pl.BlockSpec((1, tm, D), lambda b, t: (b, jnp.maximum(t - nct, 0), 0))
    return pl.pallas_call(
        body, name="merge_bwd", grid=(B, T // tm),
        in_specs=[lat, lat, lat, pl.BlockSpec((1, tm, D), lambda b, t: (b, t, 0)),
                  pl.BlockSpec((1, tm, D), lambda b, t: (b, t, 1)),
                  pl.BlockSpec((1, 2 * D), lambda b, t: (0, 0))],
        out_specs=(lat, lat, pl.BlockSpec((1, tm, 2 * D), lambda b, t: (b, t, 0)),
                   pl.BlockSpec((1, SUBLANES, 2 * D), lambda b, t: (b, 0, 0))),
        out_shape=(jax.ShapeDtypeStruct((B, N, D), BF16), jax.ShapeDtypeStruct((B, N, D), BF16),
                   jax.ShapeDtypeStruct((B, T, 2 * D), BF16), jax.ShapeDtypeStruct((B, SUBLANES, 2 * D), F32)),
        compiler_params=_cparams(("parallel", "arbitrary"), VMEM_LIMIT))(dmi, bd, bl, pmg, pmg, bm)


def _grid_taps():
    return [((di + 1) * 3 + (dj + 1), di, dj) for di in (-1, 0, 1) for dj in (-1, 0, 1)]


def _grid_views(x, n):
    pos = lax.broadcasted_iota(jnp.int32, x.shape, 0)
    gc = jnp.bitwise_and(pos, GRID_W - 1)
    cols = {0: x,
            -1: jnp.where(gc >= 1, pltpu.roll(x, 1, 0), 0.0),
            1: jnp.where(gc <= GRID_W - 2, pltpu.roll(x, n - 1, 0), 0.0)}
    views = {}
    edge = jnp.zeros((GRID_W, x.shape[1]), x.dtype)
    for dj, xc in cols.items():
        views[(0, dj)] = xc
        views[(-1, dj)] = jnp.concatenate([edge, xc[:n - GRID_W]], axis=0)
        views[(1, dj)] = jnp.concatenate([xc[GRID_W:], edge], axis=0)
    return views


def _ffn_act(Fg, Fv, w9, b, tc):
    B, N, Cf = Fg.shape
    ncb = Cf // tc

    def body(g_ref, v_ref, w_ref, b_ref, o_ref, gel_ref, dgel_ref):
        xv = _grid_views(g_ref[0], N)
        conv = b_ref[...]
        for k, di, dj in _grid_taps():
            conv = conv + xv[(di, dj)] * w_ref[k:k + 1, :]
        th = jnp.tanh(GELU_C * (conv + 0.044715 * conv * conv * conv))
        gel = 0.5 * conv * (1.0 + th)
        o_ref[0] = (gel * v_ref[0]).astype(BF16)
        gel_ref[0] = gel.astype(BF16)
        dgel_ref[0] = (0.5 * (1.0 + th)
                       + 0.5 * conv * (1.0 - th * th) * GELU_C * (1.0 + 3.0 * 0.044715 * conv * conv)).astype(BF16)

    tile = pl.BlockSpec((1, N, tc), lambda b, j: (b, 0, j))
    out = jax.ShapeDtypeStruct((B, N, Cf), BF16)
    return pl.pallas_call(
        body, name="ffn_act", grid=(B, ncb),
        in_specs=[tile, tile,
                  pl.BlockSpec((9, tc), lambda b, j: (0, j)),
                  pl.BlockSpec((1, tc), lambda b, j: (0, j))],
        out_specs=(tile, tile, tile), out_shape=(out, out, out),
        compiler_params=_cparams(("parallel", "parallel"), VMEM_LIMIT))(Fg, Fv, w9, b)


def _ffn_act_bwd(Fg, Fv, w9, gel, dgel, df, tc):
    B, N, Cf = Fg.shape
    C2 = 2 * Cf
    ncb = Cf // tc

    def body(g_ref, v_ref, w_ref, gel_ref, dgel_ref, df_ref, dF_ref, acc_ref):
        p = pl.program_id(2)

        @pl.when(p == 0)
        def _():
            dF_ref[0] = (df_ref[0] * gel_ref[0].astype(F32)).astype(BF16)

        @pl.when(p == 1)
        def _():
            xv = _grid_views(g_ref[0], N)
            df_ = df_ref[0]
            dc = df_ * v_ref[0] * dgel_ref[0].astype(F32)
            dcv = _grid_views(dc, N)
            dx = None
            for k, di, dj in _grid_taps():
                term = dcv[(-di, -dj)] * w_ref[k:k + 1, :]
                dx = term if dx is None else dx + term
                acc_ref[0, k:k + 1, :] = jnp.sum(dc * xv[(di, dj)], axis=0, keepdims=True)
            acc_ref[0, 9:10, :] = jnp.sum(dc, axis=0, keepdims=True)
            acc_ref[0, 10:16, :] = jnp.zeros((6, tc), F32)
            dF_ref[0] = dx.astype(BF16)

    tile = pl.BlockSpec((1, N, tc), lambda b, j, p: (b, 0, j))
    return pl.pallas_call(
        body, name="ffn_act_bwd", grid=(B, ncb, 2),
        in_specs=[tile, tile, pl.BlockSpec((9, tc), lambda b, j, p: (0, j)), tile, tile, tile],
        out_specs=(pl.BlockSpec((1, N, tc), lambda b, j, p: (b, 0, j + (1 - p) * ncb)),
                   pl.BlockSpec((1, 16, tc), lambda b, j, p: (b, 0, j))),
        out_shape=(jax.ShapeDtypeStruct((B, N, C2), BF16), jax.ShapeDtypeStruct((B, 16, Cf), F32)),
        compiler_params=_cparams(("parallel", "parallel", "arbitrary"), VMEM_LIMIT))(Fg, Fv, w9, gel, dgel, df)


ADAM_ROWS = 512


def _adamw(w, m, v, gparts, name):
    rows, cols = w.shape[-2:]
    P = gparts.shape[0]
    tr = _row_tile(rows, (P + 14) * 4 * (-(-cols // LANES) * LANES))
    c1 = 1.0 - ADAM_B1 ** ADAM_STEP
    c2 = 1.0 - ADAM_B2 ** ADAM_STEP

    def body(w_ref, m_ref, v_ref, g_ref, go_ref, d_ref, mo_ref, vo_ref):
        g = g_ref[0].astype(F32)
        for p in range(1, P):
            g = g + g_ref[p].astype(F32)
        m_ = ADAM_B1 * m_ref[...] + (1.0 - ADAM_B1) * g
        v_ = ADAM_B2 * v_ref[...] + (1.0 - ADAM_B2) * (g * g)
        go_ref[...] = g
        mo_ref[...] = m_
        vo_ref[...] = v_
        d_ref[...] = -ADAM_LR * ((m_ / c1) / (jnp.sqrt(v_ / c2) + ADAM_EPS) + ADAM_WD * w_ref[...])

    if w.ndim == 3:
        row = pl.BlockSpec((None, tr, cols), lambda i: (0, i, 0))
    else:
        row = pl.BlockSpec((tr, cols), lambda i: (i, 0))
    out = jax.ShapeDtypeStruct(w.shape, F32)
    return pl.pallas_call(
        body, name=name, grid=(rows // tr,),
        in_specs=[row, row, row, pl.BlockSpec((P, tr, cols), lambda i: (0, i, 0))],
        out_specs=(row, row, row, row), out_shape=(out, out, out, out),
        compiler_params=_cparams(("parallel",), VMEM_LIMIT))(w, m, v, gparts)


def _pack_rows(flat_len):
    rows = -(-flat_len // LANES)
    if rows > ADAM_ROWS:
        rows = -(-rows // ADAM_ROWS) * ADAM_ROWS
    else:
        rows = -(-rows // SUBLANES) * SUBLANES
    return rows


PACK_ALIGN = SUBLANES * LANES


def _pack(arrs, lead=()):
    pads = [(0, 0)] * len(lead)
    parts = []
    for a in arrs:
        f = a.reshape(lead + (-1,)).astype(F32)
        parts.append(jnp.pad(f, pads + [(0, -f.shape[-1] % PACK_ALIGN)]))
    flat = jnp.concatenate(parts, axis=-1)
    n = flat.shape[-1]
    rows = _pack_rows(n)
    flat = jnp.pad(flat, pads + [(0, rows * LANES - n)])
    return flat.reshape(lead + (rows, LANES))


def _unpack(buf, shapes):
    out, r = [], 0
    for s in shapes:
        n = math.prod(s)
        nr = -(-n // PACK_ALIGN) * SUBLANES
        out.append(buf[r:r + nr].reshape(-1)[:n].reshape(s))
        r += nr
    return out


def _col_shards(full):
    C = full.shape[-1]
    x = full.reshape(full.shape[:-1] + (N_DEV, C // N_DEV))
    return jnp.moveaxis(x, -2, 0)


def _from_col_shards(g):
    x = jnp.moveaxis(g, 0, -2)
    return x.reshape(x.shape[:-2] + (x.shape[-2] * x.shape[-1],))


def kernel(x, c, ctx, c_ctx, w_ada, b_ada, g_pre_mix, g_post_mix, g_pre_ffn, g_post_ffn, w_in, b_merge, dn_conv, dn_a_log, dn_dt_bias, dn_onorm, lru_conv, lru_conv_b, lru_w_rg, lru_b_rg, lru_w_ig, lru_b_ig, lru_lambda, w_branch_dn, w_branch_lru, w_out, w_up, ffn_dw, ffn_dw_b, w_down, loss_target, m_c_ctx, m_w_ada, m_b_ada, m_g_pre_mix, m_g_post_mix, m_g_pre_ffn, m_g_post_ffn, m_w_in, m_b_merge, m_dn_conv, m_dn_a_log, m_dn_dt_bias, m_dn_onorm, m_lru_conv, m_lru_conv_b, m_lru_w_rg, m_lru_b_rg, m_lru_w_ig, m_lru_b_ig, m_lru_lambda, m_w_branch_dn, m_w_branch_lru, m_w_out, m_w_up, m_ffn_dw, m_ffn_dw_b, m_w_down, v_c_ctx, v_w_ada, v_b_ada, v_g_pre_mix, v_g_post_mix, v_g_pre_ffn, v_g_post_ffn, v_w_in, v_b_merge, v_dn_conv, v_dn_a_log, v_dn_dt_bias, v_dn_onorm, v_lru_conv, v_lru_conv_b, v_lru_w_rg, v_lru_b_rg, v_lru_w_ig, v_lru_b_ig, v_lru_lambda, v_w_branch_dn, v_w_branch_lru, v_w_out, v_w_up, v_ffn_dw, v_ffn_dw_b, v_w_down):
    params = dict(c_ctx=c_ctx, w_ada=w_ada, b_ada=b_ada, g_pre_mix=g_pre_mix, g_post_mix=g_post_mix, g_pre_ffn=g_pre_ffn, g_post_ffn=g_post_ffn, w_in=w_in, b_merge=b_merge, dn_conv=dn_conv, dn_a_log=dn_a_log, dn_dt_bias=dn_dt_bias, dn_onorm=dn_onorm, lru_conv=lru_conv, lru_conv_b=lru_conv_b, lru_w_rg=lru_w_rg, lru_b_rg=lru_b_rg, lru_w_ig=lru_w_ig, lru_b_ig=lru_b_ig, lru_lambda=lru_lambda, w_branch_dn=w_branch_dn, w_branch_lru=w_branch_lru, w_out=w_out, w_up=w_up, ffn_dw=ffn_dw, ffn_dw_b=ffn_dw_b, w_down=w_down)
    mom1 = dict(c_ctx=m_c_ctx, w_ada=m_w_ada, b_ada=m_b_ada, g_pre_mix=m_g_pre_mix, g_post_mix=m_g_post_mix, g_pre_ffn=m_g_pre_ffn, g_post_ffn=m_g_post_ffn, w_in=m_w_in, b_merge=m_b_merge, dn_conv=m_dn_conv, dn_a_log=m_dn_a_log, dn_dt_bias=m_dn_dt_bias, dn_onorm=m_dn_onorm, lru_conv=m_lru_conv, lru_conv_b=m_lru_conv_b, lru_w_rg=m_lru_w_rg, lru_b_rg=m_lru_b_rg, lru_w_ig=m_lru_w_ig, lru_b_ig=m_lru_b_ig, lru_lambda=m_lru_lambda, w_branch_dn=m_w_branch_dn, w_branch_lru=m_w_branch_lru, w_out=m_w_out, w_up=m_w_up, ffn_dw=m_ffn_dw, ffn_dw_b=m_ffn_dw_b, w_down=m_w_down)
    mom2 = dict(c_ctx=v_c_ctx, w_ada=v_w_ada, b_ada=v_b_ada, g_pre_mix=v_g_pre_mix, g_post_mix=v_g_post_mix, g_pre_ffn=v_g_pre_ffn, g_post_ffn=v_g_post_ffn, w_in=v_w_in, b_merge=v_b_merge, dn_conv=v_dn_conv, dn_a_log=v_dn_a_log, dn_dt_bias=v_dn_dt_bias, dn_onorm=v_dn_onorm, lru_conv=v_lru_conv, lru_conv_b=v_lru_conv_b, lru_w_rg=v_lru_w_rg, lru_b_rg=v_lru_b_rg, lru_w_ig=v_lru_w_ig, lru_b_ig=v_lru_b_ig, lru_lambda=v_lru_lambda, w_branch_dn=v_w_branch_dn, w_branch_lru=v_w_branch_lru, w_out=v_w_out, w_up=v_w_up, ffn_dw=v_ffn_dw, ffn_dw_b=v_ffn_dw_b, w_down=v_w_down)
    names = list(params)

    B, N, D = x.shape
    NC = ctx.shape[1]
    T = NC + N
    H, hd = dn_a_log.shape[2], dn_onorm.shape[1]
    HD = H * hd
    nd = 2 * H
    W = lru_conv_b.shape[1]
    nblk, bdim = lru_w_rg.shape[2], lru_w_rg.shape[3]
    Cf = ffn_dw_b.shape[1]
    sw = w_ada.shape[2]
    tm = min(256, NC)
    nct = NC // tm
    ncc = NC // CHUNK
    nch = T // CHUNK
    R, RL = B * T, B * N
    bw = min(256, W)
    me = _my_index()
    assert NC % tm == 0 and N % tm == 0 and B + 1 <= SUBLANES and bw % bdim == 0 and D % LANES == 0

    cpad = jnp.zeros((SUBLANES, D), F32).at[:B].set(c).at[B].set(c_ctx)
    ccp = _all_gather([cpad], "ag_cond")[0].reshape(N_DEV * SUBLANES, D)
    mods_sh = _ada_fwd(ccp, w_ada[0], lax.dynamic_slice(b_ada, (0, me * sw), (1, sw)))
    lru_vecs = jnp.concatenate([lru_b_rg[0], lru_b_ig[0], lru_lambda[0], jnp.zeros_like(lru_lambda[0])], axis=0)
    mods_g, w_in_g, dn_conv_g, lru_conv_g, lru_vecs_g = _all_gather(
        [mods_sh, w_in[0].astype(BF16).T, dn_conv[0], lru_conv[0], lru_vecs], "ag_first")
    ag_mix = _split_start([w_branch_dn[0].astype(BF16), w_branch_lru[0].astype(BF16), w_out[0].astype(BF16)],
                          "gather", "ag_mix_start")
    ag_ffn = _split_start([w_up[0].astype(BF16) + ag_mix[4][0, 0].astype(BF16), w_down[0].astype(BF16),
                           ffn_dw[0].reshape(9, -1)], "gather", "ag_ffn_start")
    mine = lax.dynamic_slice(mods_g, (0, me * SUBLANES, 0), (N_DEV, SUBLANES, sw))
    mods = jnp.moveaxis(mine, 0, 1).reshape(SUBLANES, 6, D)[:B + 1]
    mods = jnp.pad(mods, ((0, 0), (0, SUBLANES - 6), (0, 0))) + ag_ffn[4][0, 0]
    dn_conv_f = _from_col_shards(dn_conv_g)
    lru_conv_f = _from_col_shards(lru_conv_g)
    lru_vecs_f = _from_col_shards(lru_vecs_g)
    lru_lam_f = lru_vecs_f[4:6]

    csh = w_in_g.shape[1]
    w_in_t = w_in_g.reshape(N_DEV * csh, D)
    o_z, o_a, o_xl = 3 * HD, 4 * HD, 4 * HD + 2 * nd
    o_yl, o_mg = o_xl + W, o_xl + 2 * W
    w_qkv, w_z = w_in_t[:o_z], w_in_t[o_z:o_a]
    w_ab = jnp.pad(w_in_t[o_a:o_xl], ((0, LANES - 2 * nd), (0, 0)))
    w_xl, w_yl, w_mg = w_in_t[o_xl:o_yl], w_in_t[o_yl:o_mg], w_in_t[o_mg:]

    def blockdiag(wg):
        per = bw // bdim
        g5 = wg.reshape(2, W // bw, per, bdim, bdim)
        eye = jnp.eye(per, dtype=wg.dtype)
        return jnp.einsum("dtpij,pq->dtpiqj", g5, eye).reshape(2, W // bw, bw, bw)

    bd_rg, bd_ig = blockdiag(lru_w_rg[0]), blockdiag(lru_w_ig[0])
    wbd = jnp.stack([bd_rg[0], bd_ig[0], bd_rg[1], bd_ig[1]]).astype(BF16)
    bias4 = jnp.stack([lru_vecs_f[0], lru_vecs_f[2], lru_vecs_f[1], lru_vecs_f[3]])
    alog_v = jnp.pad(dn_a_log.reshape(1, nd), ((0, 0), (0, LANES - nd)))
    dtb_v = jnp.pad(dn_dt_bias.reshape(1, nd), ((0, 0), (0, LANES - nd)))

    h0 = jnp.concatenate([ctx, x], axis=1)
    u1 = _norm_mod_fwd(h0, g_pre_mix, mods, 0, 1, nct, tm, "norm_mod1")
    u1f = u1.reshape(R, D)
    p_qkv = _mm(u1f, w_qkv, "nt", "mm_qkv").reshape(B, T, 3 * HD)
    p_z = _mm(u1f, w_z, "nt", "mm_z", out_dtype=BF16).reshape(B, T, HD)
    p_ab = _mm(u1f, w_ab, "nt", "mm_ab")
    p_xl = _mm(u1f, w_xl, "nt", "mm_xl").reshape(B, T, W)
    p_yl = _mm(u1f, w_yl, "nt", "mm_yl", out_dtype=BF16).reshape(B, T, W)
    p_mg = _mm(u1f, w_mg, "nt", "mm_mg", out_dtype=BF16).reshape(B, T, 2 * D)

    conv_offs = (-2, -1, 0, 1)
    tcc = _tile(HD, 1024)
    gb = _ab_act(p_ab, alog_v, dtb_v, nd, tm)
    gb3 = gb.reshape(B, T, LANES)
    cv, qkvn = _dwconv(p_qkv, 0, 3 * HD, dn_conv_f, None, conv_offs, nct, tm, HD, "dn_conv", qkv_heads=(H, hd))
    o_d0, o_d1, s_in0, s_in1, tm_d0, tm_d1 = _delta_fwd(qkvn, gb3, H, hd, ncc)
    o2 = (o_d0, o_d1)
    y_dn = _dn_out(o2, p_z, 0, dn_onorm, H, hd, nct, tm)

    tcw = _tile(W, 1024)
    xc = _dwconv(p_xl, 0, W, lru_conv_f, lru_conv_b, conv_offs, nct, tm, tcw, "lru_conv")
    tmg = max(t_ for t_ in range(SUBLANES, min(T, 768) + 1, SUBLANES) if T % t_ == 0)
    a2, inp2 = _lru_gates(xc, wbd, bias4, lru_lam_f, tmg, bw)
    hd0 = _lru_scan(a2, inp2, 0, nct, tm, tcw)
    hd1 = _lru_scan(a2, inp2, 1, nct, tm, tcw)
    y_lru = _lru_out(hd0, hd1, p_yl, 0, nct, tm, tcw)

    def gathered(started, after, name):
        xs_, lands_ = _split_wait(started, "gather", after, name)
        return [lax.dynamic_update_slice(land, x_[None], (me,) + (0,) * x_.ndim) for land, x_ in zip(lands_, xs_)]

    w_bdn_g, w_blru_g, w_out_g = gathered(ag_mix, y_lru, "ag_mix_wait")
    w_bdn_f, w_blru_f, w_out_f = w_bdn_g.reshape(HD, D), w_blru_g.reshape(W, D), w_out_g.reshape(D, D)
    bd = _mm(y_dn.reshape(RL, HD), w_bdn_f, "nn", "mm_bdn", out_dtype=BF16).reshape(B, N, D)
    bl = _mm(y_lru.reshape(RL, W), w_blru_f, "nn", "mm_blru", out_dtype=BF16).reshape(B, N, D)
    mixin = _merge(bd, bl, p_mg, b_merge, nct, tm)
    mix = _mm(mixin.reshape(RL, D), w_out_f, "nn", "mm_out").reshape(B, N, D)
    h1 = _resid_norm_fwd(x, mix, g_post_mix, mods, 2, tm, "resid_norm1")
    u2 = _norm_mod_fwd(h1, g_pre_ffn, mods, 3, 4, 0, tm, "norm_mod2")
    w_up_g, w_down_g, ffn_dw_g = gathered(ag_ffn, u2, "ag_ffn_wait")
    w_down_f = w_down_g.reshape(Cf, D)
    ffn_dw_f = _from_col_shards(ffn_dw_g)
    half = N_DEV // 2
    Fg = _mm(u2.reshape(RL, D), w_up_g[:half], "nn", "mm_up_gate").reshape(B, N, Cf)
    Fv = _mm(u2.reshape(RL, D), w_up_g[half:], "nn", "mm_up_val", out_dtype=BF16).reshape(B, N, Cf)
    tcf = LANES
    f, f_gel, f_dgel = _ffn_act(Fg, Fv, ffn_dw_f, ffn_dw_b, tcf)
    dproj = _mm(f.reshape(RL, Cf), w_down_f, "nn", "mm_down").reshape(B, N, D)

    dd, dh2, acc_f = _final_fwd_bwd(h1, dproj, g_post_ffn, mods, 5, loss_target, tm)
    loss = lax.psum((0.5 / D) * jnp.sum(acc_f[:, 2, :]), MESH_AXES)
    ddf = dd.reshape(RL, D)
    df = _mm(ddf, w_down_f, "nt", "mm_down_dx").reshape(B, N, Cf)
    gw_down = _mm(f.reshape(RL, Cf), ddf, "tn", "mm_down_dw", out_dtype=BF16)
    dF, acc_ffn = _ffn_act_bwd(Fg, Fv, ffn_dw_f, f_gel, f_dgel, df, tcf)
    dFf = dF.reshape(RL, 2 * Cf)
    du2 = _mm(dFf, w_up_g, "nt", "mm_up_dx").reshape(B, N, D)
    gw_up = _mm(u2.reshape(RL, D), dFf, "tn", "mm_up_dw", out_dtype=BF16, out_shards=N_DEV)
    jm = 2 * lax.axis_index("x") + lax.axis_index("y")

    def pair_sums(started, after, tag):
        parts8, sib4 = _split_wait(started, "pair", after, f"rs_pair_{tag}_wait")
        return [_pair_sum(p8, s4, f"rs_pair_sum_{tag}{i}") for i, (p8, s4) in enumerate(zip(parts8, sib4))]

    def own_slab(land, x4):
        idx = (jm,) + (0,) * (x4.ndim - 1)
        return lax.dynamic_update_slice(land, lax.dynamic_slice(x4, idx, (1,) + x4.shape[1:]), idx)

    p_early = _split_start([gw_up, gw_down.reshape(N_DEV, Cf // N_DEV, D)], "pair", "rs_pair_ffn_start")
    mods_b = mods + p_early[4][0, 0]
    dh1, acc2x, _ = _norm_mod_bwd(h1, du2, g_pre_ffn, mods_b, 4, 0, tm, dh2, "norm_mod2_bwd")
    dmix, acc_r1 = _resid_norm_bwd(mix, dh1, g_post_mix, mods, 2, tm, "resid_norm1_bwd")
    q_early = _split_start(pair_sums(p_early, dmix, "ffn"), "quad", "rs_quad_ffn_start")
    dmixf = dmix.reshape(RL, D)
    dmixin = _mm(dmixf, w_out_f, "nt", "mm_out_dx").reshape(B, N, D)
    gw_out = _mm(mixin.reshape(RL, D), dmixf, "tn", "mm_out_dw", out_dtype=BF16)
    dbd, dbl, dmg, acc_mg = _merge_bwd(dmixin, bd, bl, p_mg, b_merge + q_early[4][0:1, 0:1], nct, tm)
    dy_dn = _mm(dbd.reshape(RL, D), w_bdn_f, "nt", "mm_bdn_dx").reshape(B, N, HD)
    gw_bdn = _mm(y_dn.reshape(RL, HD), dbd.reshape(RL, D), "tn", "mm_bdn_dw", out_dtype=BF16)
    dy_lru = _mm(dbl.reshape(RL, D), w_blru_f, "nt", "mm_blru_dx").reshape(B, N, W)
    gw_blru = _mm(y_lru.reshape(RL, W), dbl.reshape(RL, D), "tn", "mm_blru_dw", out_dtype=BF16)

    dh, dyl = _lru_out_bwd(hd0, hd1, p_yl, 0, dy_lru, nct, tm, tcw)
    lam0, da0 = _lru_scan_bwd(a2, hd0, dh, 0, nct, tm, tcw)
    lam1, da1 = _lru_scan_bwd(a2, hd1, dh, 1, nct, tm, tcw)
    dxc, dwbd, acc_lru = _lru_gates_bwd(xc, wbd, bias4, lru_lam_f, (lam0, lam1), (da0, da1), tmg, bw)
    dxl = _dwconv(dxc, 0, W, lru_conv_f, None, tuple(-o for o in conv_offs), nct, tm, tcw, "lru_conv_dx", BF16)
    acc_lc = _dwconv_wgrad(dxc, p_xl, 0, W, conv_offs, nct, tm, tcw, "lru_conv_dw")

    do, dz, acc_on = _dn_out_bwd(o2, p_z, 0, dn_onorm, dy_dn, H, hd, nct, tm)
    dqkvn0, dqkvn1, dgb0, dgb1 = _delta_bwd(qkvn, gb3, (s_in0, s_in1), (tm_d0, tm_d1), do, H, hd, ncc)
    dcv, acc_dc = _dn_act_bwd(cv, (dqkvn0, dqkvn1), p_qkv, conv_offs, H, hd, nct, tm)
    dqkv = _dwconv(dcv, 0, 3 * HD, dn_conv_f, None, tuple(-o for o in conv_offs), nct, tm, tcc, "dn_conv_dx", BF16)
    dp_ab, acc_ab = _ab_act_bwd(p_ab, gb, (dgb0.reshape(R, LANES), dgb1.reshape(R, LANES)), alog_v, dtb_v, nd, tm)

    groups = [(dqkv.reshape(R, 3 * HD), w_qkv, "qkv"), (dz.reshape(R, HD), w_z, "z"), (dp_ab, w_ab, "ab"),
              (dxl.reshape(R, W), w_xl, "xl"), (dyl.reshape(R, W), w_yl, "yl"), (dmg.reshape(R, 2 * D), w_mg, "mg")]
    du1 = _mm_sum([(dg_, wg_) for dg_, wg_, _ in groups], "mm_in_dx")
    gw_groups = [_mm(dg_, u1f, "tn", "mm_in_dw_" + nm, out_dtype=BF16) for dg_, _, nm in groups]
    gw_groups[2] = gw_groups[2][:2 * nd]
    gw_in = jnp.concatenate(gw_groups, axis=0).reshape(N_DEV, csh, D)
    grad_x, acc1x, acc1c = _norm_mod_bwd(h0, du1.reshape(B, T, D), g_pre_mix, mods, 1, nct, tm, dh1, "norm_mod1_bwd")

    g_lru_conv = jnp.sum(acc_lc[:, :4], 0)
    g_dn_conv = jnp.sum(acc_dc[:, :4], 0)
    g_ffn_dw = jnp.sum(acc_ffn[:, :9], 0).reshape(3, 3, Cf)
    sm_names = ["dn_conv", "lru_conv", "lru_b_rg", "lru_b_ig", "lru_lambda", "ffn_dw"]
    sm_parts = [_col_shards(g_dn_conv), _col_shards(g_lru_conv),
                _col_shards(jnp.stack([acc_lru[0], acc_lru[2]])), _col_shards(jnp.stack([acc_lru[1], acc_lru[3]])),
                _col_shards(acc_lru[4:6]), _col_shards(g_ffn_dw)]
    late8 = [gw_in, gw_bdn.reshape(N_DEV, HD // N_DEV, D), gw_blru.reshape(N_DEV, W // N_DEV, D),
             gw_out.reshape(N_DEV, D // N_DEV, D), _pack(sm_parts, lead=(N_DEV,))]
    p_late = _split_start(late8, "pair", "rs_pair_mix_start")
    q_late = _split_start(pair_sums(p_late, gw_out, "mix"), "quad", "rs_quad_mix_start")
    ffn_x, ffn_land = _split_wait(q_early, "quad", q_late[4], "rs_quad_ffn_wait")
    results = {}
    for n, x4, land in zip(["w_up", "w_down"], ffn_x, ffn_land):
        results[n] = list(_adamw(params[n], mom1[n], mom2[n], own_slab(land, x4), "adamw_" + n))

    zeros_d = jnp.zeros((B, D), F32)
    dm_rows = jnp.stack([acc1x[:, 0], acc1x[:, 1], acc_r1[:, 0], acc2x[:, 0], acc2x[:, 1], acc_f[:, 0]], axis=1)
    dm_ctx = jnp.stack([jnp.sum(acc1c[:, 0], 0), jnp.sum(acc1c[:, 1], 0)] + [zeros_d[0]] * 4, axis=0)[None]
    dm_pad = jnp.zeros((SUBLANES, 6 * D), F32).at[:B + 1].set(jnp.concatenate([dm_rows, dm_ctx], 0).reshape(B + 1, 6 * D))
    dm_g = _all_gather([dm_pad], "ag_dmods")[0]
    g_mine = lax.dynamic_slice(dm_g, (0, 0, me * sw), (N_DEV, SUBLANES, sw)).reshape(N_DEV * SUBLANES, sw)
    gw_ada, dcc = _ada_bwd(ccp, w_ada[0], g_mine, c_ctx.reshape(1, D), B)

    per = bw // bdim

    def diag_blocks(dwf):
        g6 = dwf.reshape(W // bw, per, bdim, per, bdim)
        return jnp.einsum("tpiqj,pq->tpij", g6, jnp.eye(per, dtype=F32)).reshape(nblk, bdim, bdim)

    g_rep = dict(
        c_ctx=dcc[0],
        g_pre_mix=jnp.sum(acc1x[:, 2] + acc1c[:, 2], 0)[None], g_post_mix=jnp.sum(acc_r1[:, 1], 0)[None],
        g_pre_ffn=jnp.sum(acc2x[:, 2], 0)[None], g_post_ffn=jnp.sum(acc_f[:, 1], 0)[None],
        b_merge=jnp.sum(acc_mg[:, 0], 0)[None],
        dn_a_log=acc_ab[0, :nd].reshape(1, 2, H), dn_dt_bias=acc_ab[1, :nd].reshape(1, 2, H),
        dn_onorm=jnp.sum(acc_on[:, 0], 0)[None],
        lru_conv_b=jnp.sum(acc_lc[:, 4], 0)[None],
        lru_w_rg=jnp.stack([diag_blocks(dwbd[0]), diag_blocks(dwbd[2])])[None],
        lru_w_ig=jnp.stack([diag_blocks(dwbd[1]), diag_blocks(dwbd[3])])[None],
        ffn_dw_b=jnp.sum(acc_ffn[:, 9], 0)[None])
    mat_names = ["lru_w_rg", "lru_w_ig"]
    vec_names = [n for n in g_rep if n not in mat_names]
    vec_parts, mat_parts = _all_gather([_pack([g_rep[n] for n in vec_names]),
                                        _pack([g_rep[n] for n in mat_names]).astype(BF16)], "ag_rep_grads")
    for grp, parts, nm in ((vec_names, vec_parts, "adamw_rep_vec"), (mat_names, mat_parts, "adamw_rep_mat")):
        out4 = _adamw(_pack([params[n] for n in grp]), _pack([mom1[n] for n in grp]),
                      _pack([mom2[n] for n in grp]), parts, nm)
        for k, buf in enumerate(out4):
            for n, arr in zip(grp, _unpack(buf, [params[n].shape for n in grp])):
                results.setdefault(n, [None] * 4)[k] = arr

    ba_out = _adamw(_pack([b_ada]), _pack([m_b_ada]), _pack([v_b_ada]),
                    _pack([dm_g.reshape(N_DEV * SUBLANES, 6 * D)], lead=(N_DEV * SUBLANES,)), "adamw_b_ada")
    results["b_ada"] = [_unpack(buf, [b_ada.shape])[0] for buf in ba_out]

    wa_out = _adamw(w_ada, m_w_ada, v_w_ada, gw_ada[None], "adamw_w_ada")
    results["w_ada"] = list(wa_out)

    mix_x, mix_land = _split_wait(q_late, "quad", wa_out[0], "rs_quad_mix_wait")
    recv4 = [own_slab(land, x4) for land, x4 in zip(mix_land, mix_x)]
    for n, g4 in zip(["w_in", "w_branch_dn", "w_branch_lru", "w_out"], recv4[:-1]):
        if n == "w_in":
            g4 = jnp.swapaxes(g4, 1, 2)
        results[n] = list(_adamw(params[n], mom1[n], mom2[n], g4, "adamw_" + n))
    sm_out = _adamw(_pack([params[n] for n in sm_names]), _pack([mom1[n] for n in sm_names]),
                    _pack([mom2[n] for n in sm_names]), recv4[-1], "adamw_small")
    for k, buf in enumerate(sm_out):
        for n, arr in zip(sm_names, _unpack(buf, [params[n].shape for n in sm_names])):
            results.setdefault(n, [None] * 4)[k] = arr

    outs = [loss, grad_x]
    for k in range(4):
        outs += [results[n][k] for n in names]
    return tuple(outs)
```
